```python
import jax, jax.numpy as jnp
from jax import lax
import numpy as np

D_MODEL = 1024
BATCH = 8
SEQ = 8192
DEPTH = 1

EXPAND = 2
D_INNER = EXPAND * D_MODEL
SSD_WIDTH = D_INNER // 2
ATT_WIDTH = D_INNER - SSD_WIDTH
SSD_HEAD_DIM = 64
SSD_HEADS = SSD_WIDTH // SSD_HEAD_DIM
N_GROUPS = 2
HEADS_PER_GROUP = SSD_HEADS // N_GROUPS
D_STATE = 128
CONV_WIDTH = 4
CHUNK = 128
ATT_HEAD_DIM = 64
ATT_HEADS = ATT_WIDTH // ATT_HEAD_DIM
Q_BLOCK = 128
PLE_DIM = 256
EPS = 1e-6
CONV_CH = SSD_WIDTH + 2 * N_GROUPS * D_STATE
IN_SPLITS = (SSD_WIDTH, CONV_CH, SSD_HEADS, ATT_WIDTH, ATT_WIDTH, ATT_WIDTH, ATT_WIDTH, ATT_HEADS)
IN_COLS = sum(IN_SPLITS)

kernel_name = "hymba_ssd_fox_hybrid_layer"


def _split_points(sizes):
    return [int(v) for v in np.cumsum(sizes)[:-1]]


def rms_norm(x, g):
    xf = x.astype(jnp.float32)
    y = xf * lax.rsqrt(jnp.mean(xf * xf, axis=-1, keepdims=True) + EPS)
    return (y * g.astype(jnp.float32)).astype(x.dtype)


def causal_depthwise_conv(u, w, b):
    out = lax.conv_general_dilated(
        u, w[:, None, :].astype(u.dtype), window_strides=(1,),
        padding=((CONV_WIDTH - 1, 0),), dimension_numbers=('NWC', 'WIO', 'NWC'),
        feature_group_count=u.shape[-1])
    return out + b.astype(u.dtype)


def ssd_scan(x, dt, a, b_mat, c_mat):
    bsz, seqlen = x.shape[0], x.shape[1]
    nc = seqlen // CHUNK
    f32 = jnp.float32
    xdt = (x.astype(f32) * dt[..., None]).reshape(
        bsz, nc, CHUNK, N_GROUPS, HEADS_PER_GROUP, SSD_HEAD_DIM)
    a_dt = (dt * a).reshape(bsz, nc, CHUNK, N_GROUPS, HEADS_PER_GROUP)
    a_cs = jnp.cumsum(jnp.moveaxis(a_dt, 2, -1), axis=-1)
    bm = b_mat.astype(f32).reshape(bsz, nc, CHUNK, N_GROUPS, D_STATE)
    cm = c_mat.astype(f32).reshape(bsz, nc, CHUNK, N_GROUPS, D_STATE)
    idx = jnp.arange(CHUNK)
    causal = idx[:, None] >= idx[None, :]
    seg = a_cs[..., :, None] - a_cs[..., None, :]
    decay = jnp.exp(jnp.where(causal, seg, -jnp.inf))
    cb = jnp.einsum('bclgn,bcsgn->bcgls', cm, bm)
    y_diag = jnp.einsum('bcgls,bcgrls,bcsgrp->bclgrp', cb, decay, xdt)
    decay_to_end = jnp.exp(a_cs[..., -1:] - a_cs)
    chunk_states = jnp.einsum('bclgn,bcgrl,bclgrp->bcgrpn', bm, decay_to_end, xdt)
    chunk_decay = jnp.exp(a_cs[..., -1])

    def step(h, inp):
        s_c, d_c = inp
        return h * d_c[..., None, None] + s_c, h

    h0 = jnp.zeros((bsz, N_GROUPS, HEADS_PER_GROUP, SSD_HEAD_DIM, D_STATE), f32)
    _, prev = lax.scan(step, h0, (jnp.moveaxis(chunk_states, 1, 0),
                                  jnp.moveaxis(chunk_decay, 1, 0)))
    prev = jnp.moveaxis(prev, 0, 1)
    y_off = jnp.einsum('bclgn,bcgrpn,bcgrl->bclgrp', cm, prev, jnp.exp(a_cs))
    return (y_diag + y_off).reshape(bsz, seqlen, SSD_HEADS, SSD_HEAD_DIM)


def forgetting_attention(q, k, v, log_f):
    bsz, seqlen = q.shape[0], q.shape[1]
    nblk = seqlen // Q_BLOCK
    cum = jnp.moveaxis(jnp.cumsum(log_f, axis=1), 1, 2)
    qh = jnp.moveaxis(q, 1, 2)
    kh = jnp.moveaxis(k, 1, 2)
    vh = jnp.moveaxis(v, 1, 2)
    scale = ATT_HEAD_DIM ** -0.5
    q_blocks = qh.reshape(bsz, ATT_HEADS, nblk, Q_BLOCK, ATT_HEAD_DIM).transpose(2, 0, 1, 3, 4)
    cq_blocks = cum.reshape(bsz, ATT_HEADS, nblk, Q_BLOCK).transpose(2, 0, 1, 3)
    kpos = jnp.arange(seqlen)
    starts = jnp.arange(nblk) * Q_BLOCK

    def block(args):
        qb, cqb, start = args
        s = jnp.einsum('bhqd,bhkd->bhqk', qb, kh,
                       preferred_element_type=jnp.float32) * scale
        s = s + cqb[..., None] - cum[:, :, None, :]
        qpos = start + jnp.arange(Q_BLOCK)
        s = jnp.where(qpos[:, None] >= kpos[None, :], s, -jnp.inf)
        pr = jax.nn.softmax(s, axis=-1)
        return jnp.einsum('bhqk,bhkd->bhqd', pr.astype(vh.dtype), vh)

    out = lax.map(block, (q_blocks, cq_blocks, starts))
    return out.transpose(1, 0, 3, 2, 4).reshape(bsz, seqlen, ATT_HEADS, ATT_HEAD_DIM)


def _fwd_setup_inputs(seed: int = 0) -> dict:
    key = jax.random.key(seed)
    ks = jax.random.split(key, 20)
    f32 = jnp.float32
    nrm = lambda k, shape, s: (jax.random.normal(k, shape, f32) * s)
    x = jax.random.normal(ks[0], (BATCH, SEQ, D_MODEL), f32)
    p = jax.random.normal(ks[1], (DEPTH, BATCH, SEQ, PLE_DIM), f32)
    norm_g = 1.0 + nrm(ks[2], (DEPTH, D_MODEL), 0.02)
    w_in = nrm(ks[3], (DEPTH, D_MODEL, IN_COLS), D_MODEL ** -0.5)
    conv_w = nrm(ks[4], (DEPTH, CONV_WIDTH, CONV_CH), CONV_WIDTH ** -0.5)
    conv_b = nrm(ks[5], (DEPTH, CONV_CH), 0.02)
    dt0 = jnp.exp(jax.random.uniform(ks[6], (DEPTH, SSD_HEADS), f32,
                                     jnp.log(1e-3), jnp.log(1e-1)))
    dt_bias = dt0 + jnp.log(-jnp.expm1(-dt0))
    a_log = jnp.log(jax.random.uniform(ks[7], (DEPTH, SSD_HEADS), f32, 1.0, 16.0))
    d_skip = 1.0 + nrm(ks[8], (DEPTH, SSD_HEADS), 0.02)
    ssd_norm_g = 1.0 + nrm(ks[9], (DEPTH, SSD_WIDTH), 0.02)
    fg_bias = jax.random.uniform(ks[10], (DEPTH, ATT_HEADS), f32, 1.0, 5.0)
    att_norm_g = 1.0 + nrm(ks[11], (DEPTH, ATT_HEAD_DIM), 0.02)
    w_out = nrm(ks[12], (DEPTH, D_INNER, D_MODEL), D_INNER ** -0.5)
    ple_norm_g = 1.0 + nrm(ks[13], (DEPTH, D_MODEL), 0.02)
    w_ple_gate = nrm(ks[14], (DEPTH, D_MODEL, D_MODEL), D_MODEL ** -0.5)
    w_ple_proj = nrm(ks[15], (DEPTH, PLE_DIM, D_MODEL), PLE_DIM ** -0.5)
    final_norm_g = 1.0 + nrm(ks[16], (D_MODEL,), 0.02)
    return {"x": x, "p": p, "norm_g": norm_g, "w_in": w_in, "conv_w": conv_w,
            "conv_b": conv_b, "dt_bias": dt_bias, "a_log": a_log, "d_skip": d_skip,
            "ssd_norm_g": ssd_norm_g, "fg_bias": fg_bias, "att_norm_g": att_norm_g,
            "w_out": w_out, "ple_norm_g": ple_norm_g, "w_ple_gate": w_ple_gate,
            "w_ple_proj": w_ple_proj, "final_norm_g": final_norm_g}


def _fwd_reference(x, p, norm_g, w_in, conv_w, conv_b, dt_bias, a_log, d_skip, ssd_norm_g,
              fg_bias, att_norm_g, w_out, ple_norm_g, w_ple_gate, w_ple_proj, final_norm_g):
    f32 = jnp.float32
    bsz, seqlen = x.shape[0], x.shape[1]
    in_pts = _split_points(IN_SPLITS)
    h = x
    for i in range(DEPTH):
        u = rms_norm(h, norm_g[i])
        proj = u @ w_in[i]
        z_ssd, xbc, dt_raw, z_att, q, k, v, f_raw = jnp.split(proj, in_pts, axis=-1)

        xbc = jax.nn.silu(causal_depthwise_conv(xbc, conv_w[i], conv_b[i]))
        xs, bm, cm = jnp.split(xbc, [SSD_WIDTH, SSD_WIDTH + N_GROUPS * D_STATE], axis=-1)
        xs = xs.reshape(bsz, seqlen, SSD_HEADS, SSD_HEAD_DIM)
        bm = bm.reshape(bsz, seqlen, N_GROUPS, D_STATE)
        cm = cm.reshape(bsz, seqlen, N_GROUPS, D_STATE)
        dt = jax.nn.softplus(dt_raw.astype(f32) + dt_bias[i].astype(f32))
        a = -jnp.exp(a_log[i].astype(f32))
        y = ssd_scan(xs, dt, a, bm, cm)
        y = y + d_skip[i].astype(f32)[:, None] * xs.astype(f32)
        y = y.reshape(bsz, seqlen, SSD_WIDTH).astype(x.dtype) * jax.nn.silu(z_ssd)
        y = rms_norm(y.reshape(bsz, seqlen, N_GROUPS, SSD_WIDTH // N_GROUPS),
                     ssd_norm_g[i].reshape(N_GROUPS, SSD_WIDTH // N_GROUPS))
        y_ssd = y.reshape(bsz, seqlen, SSD_WIDTH)

        log_f = jax.nn.log_sigmoid(f_raw.astype(f32) + fg_bias[i].astype(f32))
        hs = (bsz, seqlen, ATT_HEADS, ATT_HEAD_DIM)
        att = forgetting_attention(q.reshape(hs), k.reshape(hs), v.reshape(hs), log_f)
        att = rms_norm(att, att_norm_g[i])
        y_att = att.reshape(bsz, seqlen, ATT_WIDTH) * jax.nn.silu(z_att)

        h = h + jnp.concatenate([y_ssd, y_att], axis=-1) @ w_out[i]

        gate = jax.nn.sigmoid(rms_norm(h, ple_norm_g[i]) @ w_ple_gate[i])
        h = h + gate * (p[i].astype(h.dtype) @ w_ple_proj[i])
    return rms_norm(h, final_norm_g)


import jax as _jax
import jax.numpy as _jnp

TWIN_FORMAT = 'train_step'
FWD_PARAMS = ['x', 'p', 'norm_g', 'w_in', 'conv_w', 'conv_b', 'dt_bias', 'a_log', 'd_skip', 'ssd_norm_g', 'fg_bias', 'att_norm_g', 'w_out', 'ple_norm_g', 'w_ple_gate', 'w_ple_proj', 'final_norm_g']
TWIN_WEIGHTS = ['norm_g', 'w_in', 'conv_w', 'conv_b', 'dt_bias', 'a_log', 'd_skip', 'ssd_norm_g', 'fg_bias', 'att_norm_g', 'w_out', 'ple_norm_g', 'w_ple_gate', 'w_ple_proj', 'final_norm_g']
TWIN_DIFF_INPUT = 'x'
TWIN_INPUTS = ['x', 'p', 'norm_g', 'w_in', 'conv_w', 'conv_b', 'dt_bias', 'a_log', 'd_skip', 'ssd_norm_g', 'fg_bias', 'att_norm_g', 'w_out', 'ple_norm_g', 'w_ple_gate', 'w_ple_proj', 'final_norm_g', 'loss_target', 'm_norm_g', 'm_w_in', 'm_conv_w', 'm_conv_b', 'm_dt_bias', 'm_a_log', 'm_d_skip', 'm_ssd_norm_g', 'm_fg_bias', 'm_att_norm_g', 'm_w_out', 'm_ple_norm_g', 'm_w_ple_gate', 'm_w_ple_proj', 'm_final_norm_g', 'v_norm_g', 'v_w_in', 'v_conv_w', 'v_conv_b', 'v_dt_bias', 'v_a_log', 'v_d_skip', 'v_ssd_norm_g', 'v_fg_bias', 'v_att_norm_g', 'v_w_out', 'v_ple_norm_g', 'v_w_ple_gate', 'v_w_ple_proj', 'v_final_norm_g']
TWIN_OUTPUTS = ['loss', 'grad_x', 'grad_norm_g', 'grad_w_in', 'grad_conv_w', 'grad_conv_b', 'grad_dt_bias', 'grad_a_log', 'grad_d_skip', 'grad_ssd_norm_g', 'grad_fg_bias', 'grad_att_norm_g', 'grad_w_out', 'grad_ple_norm_g', 'grad_w_ple_gate', 'grad_w_ple_proj', 'grad_final_norm_g', 'delta_norm_g', 'delta_w_in', 'delta_conv_w', 'delta_conv_b', 'delta_dt_bias', 'delta_a_log', 'delta_d_skip', 'delta_ssd_norm_g', 'delta_fg_bias', 'delta_att_norm_g', 'delta_w_out', 'delta_ple_norm_g', 'delta_w_ple_gate', 'delta_w_ple_proj', 'delta_final_norm_g', 'new_m_norm_g', 'new_m_w_in', 'new_m_conv_w', 'new_m_conv_b', 'new_m_dt_bias', 'new_m_a_log', 'new_m_d_skip', 'new_m_ssd_norm_g', 'new_m_fg_bias', 'new_m_att_norm_g', 'new_m_w_out', 'new_m_ple_norm_g', 'new_m_w_ple_gate', 'new_m_w_ple_proj', 'new_m_final_norm_g', 'new_v_norm_g', 'new_v_w_in', 'new_v_conv_w', 'new_v_conv_b', 'new_v_dt_bias', 'new_v_a_log', 'new_v_d_skip', 'new_v_ssd_norm_g', 'new_v_fg_bias', 'new_v_att_norm_g', 'new_v_w_out', 'new_v_ple_norm_g', 'new_v_w_ple_gate', 'new_v_w_ple_proj', 'new_v_final_norm_g']
TWIN_LEAF_KINDS = {'loss': 'loss', 'grad_x': 'grad_x', 'grad_norm_g': 'grad_w', 'grad_w_in': 'grad_w', 'grad_conv_w': 'grad_w', 'grad_conv_b': 'grad_w', 'grad_dt_bias': 'grad_w', 'grad_a_log': 'grad_w', 'grad_d_skip': 'grad_w', 'grad_ssd_norm_g': 'grad_w', 'grad_fg_bias': 'grad_w', 'grad_att_norm_g': 'grad_w', 'grad_w_out': 'grad_w', 'grad_ple_norm_g': 'grad_w', 'grad_w_ple_gate': 'grad_w', 'grad_w_ple_proj': 'grad_w', 'grad_final_norm_g': 'grad_w', 'delta_norm_g': 'delta_w', 'delta_w_in': 'delta_w', 'delta_conv_w': 'delta_w', 'delta_conv_b': 'delta_w', 'delta_dt_bias': 'delta_w', 'delta_a_log': 'delta_w', 'delta_d_skip': 'delta_w', 'delta_ssd_norm_g': 'delta_w', 'delta_fg_bias': 'delta_w', 'delta_att_norm_g': 'delta_w', 'delta_w_out': 'delta_w', 'delta_ple_norm_g': 'delta_w', 'delta_w_ple_gate': 'delta_w', 'delta_w_ple_proj': 'delta_w', 'delta_final_norm_g': 'delta_w', 'new_m_norm_g': 'new_m', 'new_m_w_in': 'new_m', 'new_m_conv_w': 'new_m', 'new_m_conv_b': 'new_m', 'new_m_dt_bias': 'new_m', 'new_m_a_log': 'new_m', 'new_m_d_skip': 'new_m', 'new_m_ssd_norm_g': 'new_m', 'new_m_fg_bias': 'new_m', 'new_m_att_norm_g': 'new_m', 'new_m_w_out': 'new_m', 'new_m_ple_norm_g': 'new_m', 'new_m_w_ple_gate': 'new_m', 'new_m_w_ple_proj': 'new_m', 'new_m_final_norm_g': 'new_m', 'new_v_norm_g': 'new_v', 'new_v_w_in': 'new_v', 'new_v_conv_w': 'new_v', 'new_v_conv_b': 'new_v', 'new_v_dt_bias': 'new_v', 'new_v_a_log': 'new_v', 'new_v_d_skip': 'new_v', 'new_v_ssd_norm_g': 'new_v', 'new_v_fg_bias': 'new_v', 'new_v_att_norm_g': 'new_v', 'new_v_w_out': 'new_v', 'new_v_ple_norm_g': 'new_v', 'new_v_w_ple_gate': 'new_v', 'new_v_w_ple_proj': 'new_v', 'new_v_final_norm_g': 'new_v'}


def _forward(args):
    return _fwd_reference(*[args[k] for k in FWD_PARAMS])


def _output_shape():
    def fwd():
        inp = _fwd_setup_inputs(0)
        return _fwd_reference(*[inp[k] for k in FWD_PARAMS])
    out = _jax.eval_shape(fwd)
    return out.shape, out.dtype

N_MICROBATCH = 1
ADAM_LR = 0.001
ADAM_B1 = 0.9
ADAM_B2 = 0.999
ADAM_EPS = 1e-08
ADAM_WD = 0.01
ADAM_STEP = 10
PER_EXAMPLE_BATCH_AXIS = {'x': 0, 'p': 1, 'loss_target': 0}
SHARED_INPUTS = []
_WEIGHT_DTYPES = {'norm_g': _jnp.float32, 'w_in': _jnp.float32, 'conv_w': _jnp.float32, 'conv_b': _jnp.float32, 'dt_bias': _jnp.float32, 'a_log': _jnp.float32, 'd_skip': _jnp.float32, 'ssd_norm_g': _jnp.float32, 'fg_bias': _jnp.float32, 'att_norm_g': _jnp.float32, 'w_out': _jnp.float32, 'ple_norm_g': _jnp.float32, 'w_ple_gate': _jnp.float32, 'w_ple_proj': _jnp.float32, 'final_norm_g': _jnp.float32}
MOMENT_SCALE = {'norm_g': 2.449296e-01, 'w_in': 9.486918e-02, 'conv_w': 1.147071e-01, 'conv_b': 1.601295e-01, 'dt_bias': 2.920884e-01, 'a_log': 5.400901e-01, 'd_skip': 5.820258e-01, 'ssd_norm_g': 1.333636e-01, 'fg_bias': 2.783783e-01, 'att_norm_g': 3.949650e-01, 'w_out': 1.510668e-01, 'ple_norm_g': 3.918957e-02, 'w_ple_gate': 3.848851e-02, 'w_ple_proj': 9.753357e-02, 'final_norm_g': 6.408321e+01}


def _to_microbatches(a, axis):
    t = _jnp.moveaxis(a, axis, 0)
    t = t.reshape((N_MICROBATCH, t.shape[0] // N_MICROBATCH) + t.shape[1:])
    return _jnp.moveaxis(t, 1, axis + 1)


def setup_inputs(seed: int = 0) -> dict:
    inp = _fwd_setup_inputs(seed)
    key = _jax.random.fold_in(_jax.random.key(seed), 7919)
    shape, _ = _output_shape()
    out = dict(inp)
    out["loss_target"] = _jax.random.normal(_jax.random.fold_in(key, 0), shape, _jnp.float32)
    for i, name in enumerate(TWIN_WEIGHTS):
        w = inp[name].astype(_jnp.float32)
        if MOMENT_SCALE is None:
            s = _jnp.sqrt(_jnp.mean(_jnp.square(w)) + 1e-30)
        else:
            s = MOMENT_SCALE[name]
        km, kv = _jax.random.split(_jax.random.fold_in(key, i + 1))
        out[name] = w
        out["m_" + name] = s * _jax.random.normal(km, w.shape, _jnp.float32)
        out["v_" + name] = (s * s) * _jax.random.uniform(kv, w.shape, _jnp.float32, 0.5, 1.5)
    if N_MICROBATCH > 1:
        for name, axis in PER_EXAMPLE_BATCH_AXIS.items():
            out[name] = _to_microbatches(out[name], axis)
    return {'x': out['x'], 'p': out['p'], 'norm_g': out['norm_g'], 'w_in': out['w_in'], 'conv_w': out['conv_w'], 'conv_b': out['conv_b'], 'dt_bias': out['dt_bias'], 'a_log': out['a_log'], 'd_skip': out['d_skip'], 'ssd_norm_g': out['ssd_norm_g'], 'fg_bias': out['fg_bias'], 'att_norm_g': out['att_norm_g'], 'w_out': out['w_out'], 'ple_norm_g': out['ple_norm_g'], 'w_ple_gate': out['w_ple_gate'], 'w_ple_proj': out['w_ple_proj'], 'final_norm_g': out['final_norm_g'], 'loss_target': out['loss_target'], 'm_norm_g': out['m_norm_g'], 'm_w_in': out['m_w_in'], 'm_conv_w': out['m_conv_w'], 'm_conv_b': out['m_conv_b'], 'm_dt_bias': out['m_dt_bias'], 'm_a_log': out['m_a_log'], 'm_d_skip': out['m_d_skip'], 'm_ssd_norm_g': out['m_ssd_norm_g'], 'm_fg_bias': out['m_fg_bias'], 'm_att_norm_g': out['m_att_norm_g'], 'm_w_out': out['m_w_out'], 'm_ple_norm_g': out['m_ple_norm_g'], 'm_w_ple_gate': out['m_w_ple_gate'], 'm_w_ple_proj': out['m_w_ple_proj'], 'm_final_norm_g': out['m_final_norm_g'], 'v_norm_g': out['v_norm_g'], 'v_w_in': out['v_w_in'], 'v_conv_w': out['v_conv_w'], 'v_conv_b': out['v_conv_b'], 'v_dt_bias': out['v_dt_bias'], 'v_a_log': out['v_a_log'], 'v_d_skip': out['v_d_skip'], 'v_ssd_norm_g': out['v_ssd_norm_g'], 'v_fg_bias': out['v_fg_bias'], 'v_att_norm_g': out['v_att_norm_g'], 'v_w_out': out['v_w_out'], 'v_ple_norm_g': out['v_ple_norm_g'], 'v_w_ple_gate': out['v_w_ple_gate'], 'v_w_ple_proj': out['v_w_ple_proj'], 'v_final_norm_g': out['v_final_norm_g']}


def _loss(weights, diff, rest, loss_target):
    with _jax.named_scope("forward"):
        args = {**rest, TWIN_DIFF_INPUT: diff, **{k: w.astype(_WEIGHT_DTYPES[k]) for k, w in weights.items()}}
        y = _forward(args)
    with _jax.named_scope("loss_head"):
        err = _jnp.square(y.astype(_jnp.float32) - loss_target)
        return 0.5 * _jnp.sum(_jnp.mean(err, axis=-1)) if err.ndim else 0.5 * err


def _adamw(w, g, m, v):
    m = ADAM_B1 * m + (1.0 - ADAM_B1) * g
    v = ADAM_B2 * v + (1.0 - ADAM_B2) * _jnp.square(g)
    m_hat = m / (1.0 - ADAM_B1 ** ADAM_STEP)
    v_hat = v / (1.0 - ADAM_B2 ** ADAM_STEP)
    delta = -ADAM_LR * (m_hat / (_jnp.sqrt(v_hat) + ADAM_EPS) + ADAM_WD * w)
    return delta, m, v


def reference(x, p, norm_g, w_in, conv_w, conv_b, dt_bias, a_log, d_skip, ssd_norm_g, fg_bias, att_norm_g, w_out, ple_norm_g, w_ple_gate, w_ple_proj, final_norm_g, loss_target, m_norm_g, m_w_in, m_conv_w, m_conv_b, m_dt_bias, m_a_log, m_d_skip, m_ssd_norm_g, m_fg_bias, m_att_norm_g, m_w_out, m_ple_norm_g, m_w_ple_gate, m_w_ple_proj, m_final_norm_g, v_norm_g, v_w_in, v_conv_w, v_conv_b, v_dt_bias, v_a_log, v_d_skip, v_ssd_norm_g, v_fg_bias, v_att_norm_g, v_w_out, v_ple_norm_g, v_w_ple_gate, v_w_ple_proj, v_final_norm_g):
    given = dict(x=x, p=p, norm_g=norm_g, w_in=w_in, conv_w=conv_w, conv_b=conv_b, dt_bias=dt_bias, a_log=a_log, d_skip=d_skip, ssd_norm_g=ssd_norm_g, fg_bias=fg_bias, att_norm_g=att_norm_g, w_out=w_out, ple_norm_g=ple_norm_g, w_ple_gate=w_ple_gate, w_ple_proj=w_ple_proj, final_norm_g=final_norm_g, loss_target=loss_target, m_norm_g=m_norm_g, m_w_in=m_w_in, m_conv_w=m_conv_w, m_conv_b=m_conv_b, m_dt_bias=m_dt_bias, m_a_log=m_a_log, m_d_skip=m_d_skip, m_ssd_norm_g=m_ssd_norm_g, m_fg_bias=m_fg_bias, m_att_norm_g=m_att_norm_g, m_w_out=m_w_out, m_ple_norm_g=m_ple_norm_g, m_w_ple_gate=m_w_ple_gate, m_w_ple_proj=m_w_ple_proj, m_final_norm_g=m_final_norm_g, v_norm_g=v_norm_g, v_w_in=v_w_in, v_conv_w=v_conv_w, v_conv_b=v_conv_b, v_dt_bias=v_dt_bias, v_a_log=v_a_log, v_d_skip=v_d_skip, v_ssd_norm_g=v_ssd_norm_g, v_fg_bias=v_fg_bias, v_att_norm_g=v_att_norm_g, v_w_out=v_w_out, v_ple_norm_g=v_ple_norm_g, v_w_ple_gate=v_w_ple_gate, v_w_ple_proj=v_w_ple_proj, v_final_norm_g=v_final_norm_g)
    weights = {n: given[n] for n in TWIN_WEIGHTS}
    shared = {n: given[n] for n in SHARED_INPUTS}
    per_example = {n: given[n] for n in ['x', 'p']}
    grad_fn = _jax.value_and_grad(_loss, argnums=(0, 1))

    def one_microbatch(ex, loss_target):
        ex = dict(ex)
        diff = ex.pop(TWIN_DIFF_INPUT)
        return grad_fn(weights, diff, {**shared, **ex}, loss_target)

    if N_MICROBATCH == 1:
        loss, (grad_w, grad_x) = one_microbatch(per_example, given["loss_target"])
    else:
        def body(carry, xs):
            loss_sum, grad_sum = carry
            l_k, (gw_k, gx_k) = one_microbatch(xs[0], xs[1])
            with _jax.named_scope("update"):
                return (loss_sum + l_k, _jax.tree.map(_jnp.add, grad_sum, gw_k)), gx_k

        init = (_jnp.zeros((), _jnp.float32), _jax.tree.map(_jnp.zeros_like, weights))
        (loss, grad_w), grad_x = _jax.lax.scan(body, init, (per_example, given["loss_target"]))
    with _jax.named_scope("update"):
        delta_w, new_m, new_v = {}, {}, {}
        for n in TWIN_WEIGHTS:
            delta_w[n], new_m[n], new_v[n] = _adamw(weights[n], grad_w[n], given["m_" + n], given["v_" + n])
    return (loss, grad_x, *[grad_w[n] for n in TWIN_WEIGHTS], *[delta_w[n] for n in TWIN_WEIGHTS],
            *[new_m[n] for n in TWIN_WEIGHTS], *[new_v[n] for n in TWIN_WEIGHTS])
```

```python
import functools

import jax
import jax.numpy as jnp
from jax import lax
from jax.experimental import pallas as pl
from jax.experimental.pallas import tpu as pltpu

F32 = jnp.float32
BF16 = jnp.bfloat16

D_MODEL = 1024
SSD_WIDTH = 1024
ATT_WIDTH = 1024
N_HEADS = 16
HEAD_DIM = 64
N_GROUPS = 2
D_STATE = 128
CONV_CH = 1536
CONV_WIDTH = 4
CHUNK = 128
PLE_DIM = 256
D_INNER = 2048
EPS = 1e-6
IN_COLS = 6688
N_CHIPS = 4
N_DEV = 8
LANES = 128
N_PAIRS = 8

ADAM_LR = 0.001
ADAM_B1 = 0.9
ADAM_B2 = 0.999
ADAM_EPS = 1e-08
ADAM_WD = 0.01
ADAM_STEP = 10

ROWS_W_IN = 1024 * 1672 // LANES
ROWS_W_OUT = 512 * 1024 // LANES
ROWS_W_GATE = 256 * 1024 // LANES
ROWS_W_PROJ = 256 * 256 // LANES
ROWS_CONV = 16
PACK_ROWS = 20480
PACK_BLOCK = 2048
SMALL_ROWS = 48

NEG_BIG = -1e30
VMEM_LIMIT = 56 * 1024 * 1024

MESH = pl.DeviceIdType.MESH
ANY = pl.BlockSpec(memory_space=pl.ANY)


def _mm(a, b):
    return jnp.dot(a, b, preferred_element_type=F32)


def _mm_nt(a, b):
    return lax.dot_general(a, b, (((1,), (1,)), ((), ())), preferred_element_type=F32)


def _mm_tn(a, b):
    return lax.dot_general(a, b, (((0,), (0,)), ((), ())), preferred_element_type=F32)


def _mm_exact(a, b):
    return jnp.dot(a, b, preferred_element_type=F32, precision=lax.Precision.HIGHEST)


def _softplus(x):
    return jnp.maximum(x, 0.0) + jnp.log1p(jnp.exp(-jnp.abs(x)))


def _sigmoid(x):
    return jax.nn.sigmoid(x)


def _iota(shape, dim):
    return lax.broadcasted_iota(jnp.int32, shape, dim)


def _params(sem=None):
    return pltpu.CompilerParams(dimension_semantics=sem, vmem_limit_bytes=VMEM_LIMIT)


def _blk(n, pref):
    return min(n, pref)


def _const_spec(shape):
    nd = len(shape)
    return pl.BlockSpec(shape, lambda *_: (0,) * nd)


def _chip_peers():
    x, y, c = lax.axis_index("x"), lax.axis_index("y"), lax.axis_index("c")
    return x, y, c, [(1 - x, y, c), (x, 1 - y, c), (1 - x, 1 - y, c)]


def gather_weights(wpack, cpack):
    def body(w_ref, c_ref, wall_ref, call_ref, ssem, rsem, lsem):
        x, y, _, peers = _chip_peers()
        me = 2 * x + y
        local = [pltpu.make_async_copy(w_ref, wall_ref.at[me], lsem.at[0]),
                 pltpu.make_async_copy(c_ref, call_ref.at[me], lsem.at[1])]
        for cp in local:
            cp.start()
        remote = []
        for k, peer in enumerate(peers):
            remote.append(pltpu.make_async_remote_copy(
                src_ref=w_ref, dst_ref=wall_ref.at[me], send_sem=ssem.at[k], recv_sem=rsem.at[k],
                device_id=peer, device_id_type=MESH))
            remote.append(pltpu.make_async_remote_copy(
                src_ref=c_ref, dst_ref=call_ref.at[me], send_sem=ssem.at[3 + k], recv_sem=rsem.at[3 + k],
                device_id=peer, device_id_type=MESH))
        for cp in remote:
            cp.start()
        for cp in remote:
            cp.wait()
        for cp in local:
            cp.wait()

    return pl.pallas_call(
        body, name="gather_weights",
        out_shape=(jax.ShapeDtypeStruct((N_CHIPS,) + wpack.shape, wpack.dtype),
                   jax.ShapeDtypeStruct((N_CHIPS,) + cpack.shape, cpack.dtype)),
        in_specs=[ANY, ANY], out_specs=(ANY, ANY),
        scratch_shapes=[pltpu.SemaphoreType.DMA((6,)), pltpu.SemaphoreType.DMA((6,)),
                        pltpu.SemaphoreType.DMA((2,))],
    )(wpack, cpack)


def scatter_grads(gpack, small):
    def body(g_ref, s_ref, parts_ref, smalls_ref, ssem, rsem, s_ssem, s_rsem, lsem):
        x, y, c, peers = _chip_peers()
        me = 2 * x + y
        dev = 4 * x + 2 * y + c
        local = [pltpu.make_async_copy(g_ref.at[me], parts_ref.at[me], lsem.at[0]),
                 pltpu.make_async_copy(s_ref, smalls_ref.at[dev], lsem.at[1])]
        for cp in local:
            cp.start()
        remote = []
        for k, peer in enumerate(peers):
            dst_chip = 2 * peer[0] + peer[1]
            remote.append(pltpu.make_async_remote_copy(
                src_ref=g_ref.at[dst_chip], dst_ref=parts_ref.at[me], send_sem=ssem.at[k], recv_sem=rsem.at[k],
                device_id=peer, device_id_type=MESH))
        for k in range(1, N_DEV):
            fx, fy, fc = (k >> 2) & 1, (k >> 1) & 1, k & 1
            peer = ((1 - x) if fx else x, (1 - y) if fy else y, (1 - c) if fc else c)
            remote.append(pltpu.make_async_remote_copy(
                src_ref=s_ref, dst_ref=smalls_ref.at[dev], send_sem=s_ssem.at[k - 1], recv_sem=s_rsem.at[k - 1],
                device_id=peer, device_id_type=MESH))
        for cp in remote:
            cp.start()
        for cp in remote:
            cp.wait()
        for cp in local:
            cp.wait()

    return pl.pallas_call(
        body, name="scatter_grads",
        out_shape=(jax.ShapeDtypeStruct(gpack.shape, gpack.dtype),
                   jax.ShapeDtypeStruct((N_DEV,) + small.shape, small.dtype)),
        in_specs=[ANY, ANY], out_specs=(ANY, ANY),
        scratch_shapes=[pltpu.SemaphoreType.DMA((3,)), pltpu.SemaphoreType.DMA((3,)),
                        pltpu.SemaphoreType.DMA((7,)), pltpu.SemaphoreType.DMA((7,)),
                        pltpu.SemaphoreType.DMA((2,))],
    )(gpack, small)


def swap_with_sibling(part):
    def body(p_ref, q_ref, ssem, rsem):
        x, y, c = lax.axis_index("x"), lax.axis_index("y"), lax.axis_index("c")
        cp = pltpu.make_async_remote_copy(src_ref=p_ref, dst_ref=q_ref, send_sem=ssem, recv_sem=rsem,
                                          device_id=(x, y, 1 - c), device_id_type=MESH)
        cp.start()
        cp.wait()

    return pl.pallas_call(
        body, name="swap_with_sibling",
        out_shape=jax.ShapeDtypeStruct(part.shape, part.dtype),
        in_specs=[ANY], out_specs=ANY,
        scratch_shapes=[pltpu.SemaphoreType.DMA, pltpu.SemaphoreType.DMA],
    )(part)


def sum_parts(parts):
    def body(p_ref, o_ref):
        o_ref[...] = ((p_ref[0] + p_ref[1]) + p_ref[2]) + p_ref[3]

    return pl.pallas_call(
        body, name="sum_parts",
        out_shape=jax.ShapeDtypeStruct(parts.shape[1:], F32),
        grid=(PACK_ROWS // PACK_BLOCK,),
        in_specs=[pl.BlockSpec((N_CHIPS, PACK_BLOCK, LANES), lambda i: (0, i, 0))],
        out_specs=pl.BlockSpec((PACK_BLOCK, LANES), lambda i: (i, 0)),
        compiler_params=_params(("parallel",)),
    )(parts)


def _adamw(w, g, m, v):
    m = ADAM_B1 * m + (1.0 - ADAM_B1) * g
    v = ADAM_B2 * v + (1.0 - ADAM_B2) * (g * g)
    m_hat = m / (1.0 - ADAM_B1 ** ADAM_STEP)
    v_hat = v / (1.0 - ADAM_B2 ** ADAM_STEP)
    delta = -ADAM_LR * (m_hat / (jnp.sqrt(v_hat) + ADAM_EPS) + ADAM_WD * w)
    return delta, m, v


def adamw_pack(p_south, p_north, w, m, v):
    def body(a_ref, b_ref, w_ref, m_ref, v_ref, g_out, d_out, m_out, v_out):
        g = a_ref[...] + b_ref[...]
        d, mn, vn = _adamw(w_ref[...], g, m_ref[...], v_ref[...])
        g_out[...] = g
        d_out[...] = d
        m_out[...] = mn
        v_out[...] = vn

    spec = pl.BlockSpec((PACK_BLOCK, LANES), lambda i: (i, 0))
    shp = jax.ShapeDtypeStruct((PACK_ROWS, LANES), F32)
    return pl.pallas_call(
        body, name="adamw_pack", out_shape=(shp,) * 4, grid=(PACK_ROWS // PACK_BLOCK,),
        in_specs=[spec] * 5, out_specs=(spec,) * 4, compiler_params=_params(("parallel",)),
    )(p_south, p_north, w, m, v)


def adamw_small(smalls, w, m, v):
    def body(s_ref, w_ref, m_ref, v_ref, g_out, d_out, m_out, v_out):
        g = s_ref[0]
        for k in range(1, N_DEV):
            g = g + s_ref[k]
        d, mn, vn = _adamw(w_ref[...], g, m_ref[...], v_ref[...])
        g_out[...] = g
        d_out[...] = d
        m_out[...] = mn
        v_out[...] = vn

    shp = jax.ShapeDtypeStruct((SMALL_ROWS, LANES), F32)
    return pl.pallas_call(body, name="adamw_small", out_shape=(shp,) * 4)(smalls, w, m, v)


def rms_prenorm(x, g):
    s = x.shape[0]
    tm = _blk(s, 512)

    def body(x_ref, g_ref, u_ref):
        xv = x_ref[...]
        r = lax.rsqrt(jnp.mean(xv * xv, axis=-1, keepdims=True) + EPS)
        u_ref[...] = (xv * r * g_ref[...]).astype(BF16)

    return pl.pallas_call(
        body, name="rms_prenorm", out_shape=jax.ShapeDtypeStruct(x.shape, BF16), grid=(s // tm,),
        in_specs=[pl.BlockSpec((tm, D_MODEL), lambda i: (i, 0)), _const_spec((1, D_MODEL))],
        out_specs=pl.BlockSpec((tm, D_MODEL), lambda i: (i, 0)), compiler_params=_params(("parallel",)),
    )(x, g)


def matmul_rows(a, w, out_dtype, name):
    s, k = a.shape
    n = w.shape[1]
    tm = _blk(s, 512)

    def body(a_ref, w_ref, o_ref):
        o_ref[...] = _mm(a_ref[...], w_ref[...]).astype(out_dtype)

    return pl.pallas_call(
        body, name=name, out_shape=jax.ShapeDtypeStruct((s, n), out_dtype), grid=(s // tm,),
        in_specs=[pl.BlockSpec((tm, k), lambda i: (i, 0)), _const_spec((k, n))],
        out_specs=pl.BlockSpec((tm, n), lambda i: (i, 0)), compiler_params=_params(("parallel",)),
    )(a, w)


def matmul_tn(a, b, name):
    s, m = a.shape
    n = b.shape[1]
    tk = _blk(s, 512)
    tn = _blk(n, 512)

    def body(a_ref, b_ref, o_ref):
        @pl.when(pl.program_id(1) == 0)
        def _():
            o_ref[...] = jnp.zeros_like(o_ref)

        o_ref[...] += _mm_tn(a_ref[...], b_ref[...])

    return pl.pallas_call(
        body, name=name, out_shape=jax.ShapeDtypeStruct((m, n), F32), grid=(n // tn, s // tk),
        in_specs=[pl.BlockSpec((tk, m), lambda j, i: (i, 0)), pl.BlockSpec((tk, tn), lambda j, i: (i, j))],
        out_specs=pl.BlockSpec((m, tn), lambda j, i: (0, j)),
        compiler_params=_params(("parallel", "arbitrary")),
    )(a, b)


def conv_fwd(xbc, w, b):
    s = xbc.shape[0]
    tm = _blk(s, 256)

    def body(x_ref, t_ref, w_ref, b_ref, pre_ref, act_ref):
        i = pl.program_id(0)
        cur = x_ref[...]
        tail = jnp.where(i > 0, t_ref[...], 0.0)
        wv = w_ref[...]
        acc = cur * wv[3:4, :] + b_ref[...]
        head = cur[0:8, :] * wv[3:4, :] + b_ref[...]
        row8 = _iota((8, CONV_CH), 0)
        for sh in range(1, CONV_WIDTH):
            wk = wv[3 - sh:4 - sh, :]
            acc = acc + pltpu.roll(cur, sh, 0) * wk
            first = jnp.where(row8 < sh, pltpu.roll(tail, sh, 0), pltpu.roll(cur[0:8, :], sh, 0))
            head = head + first * wk
        pre_ref[...] = acc
        act_ref[...] = acc * _sigmoid(acc)
        pre_ref[0:8, :] = head
        act_ref[0:8, :] = head * _sigmoid(head)

    shp = jax.ShapeDtypeStruct(xbc.shape, F32)
    rows = pl.BlockSpec((tm, CONV_CH), lambda i: (i, 0))
    return pl.pallas_call(
        body, name="conv_fwd", out_shape=(shp, shp), grid=(s // tm,),
        in_specs=[rows, pl.BlockSpec((8, CONV_CH), lambda i: (jnp.maximum(i * (tm // 8) - 1, 0), 0)),
                  _const_spec((CONV_WIDTH, CONV_CH)), _const_spec((1, CONV_CH))],
        out_specs=(rows, rows), compiler_params=_params(("parallel",)),
    )(xbc, xbc, w, b)


def conv_bwd(xbc, pre, dact, w):
    s = xbc.shape[0]
    tm = _blk(s, 256)
    nb = s // tm

    def dsilu(p):
        sg = _sigmoid(p)
        return sg * (1.0 + p * (1.0 - sg))

    def body(x_ref, xt_ref, p_ref, pn_ref, d_ref, dn_ref, w_ref, dx_ref, dw_ref, db_ref):
        i = pl.program_id(0)

        @pl.when(i == 0)
        def _():
            dw_ref[...] = jnp.zeros_like(dw_ref)
            db_ref[...] = jnp.zeros_like(db_ref)

        wv = w_ref[...]
        dpre = d_ref[...] * dsilu(p_ref[...])
        dnext = jnp.where(i < nb - 1, dn_ref[...] * dsilu(pn_ref[...]), 0.0)
        cur = x_ref[...]
        tail = jnp.where(i > 0, xt_ref[...], 0.0)
        row8 = _iota((8, CONV_CH), 0)
        dx = dpre * wv[3:4, :]
        last = dpre[tm - 8:tm, :] * wv[3:4, :]
        db_ref[...] += jnp.sum(dpre, axis=0, keepdims=True)
        dws = [jnp.sum(dpre * cur, axis=0, keepdims=True)]
        for sh in range(1, CONV_WIDTH):
            wk = wv[3 - sh:4 - sh, :]
            dx = dx + pltpu.roll(dpre, tm - sh, 0) * wk
            nxt = jnp.where(row8 >= 8 - sh, pltpu.roll(dnext, 8 - sh, 0), pltpu.roll(dpre[tm - 8:tm, :], 8 - sh, 0))
            last = last + nxt * wk
            xs = pltpu.roll(cur, sh, 0)
            first = jnp.where(row8 < sh, pltpu.roll(tail, sh, 0), xs[0:8, :])
            dws.append(jnp.sum(dpre * xs, axis=0, keepdims=True)
                       + jnp.sum(dpre[0:8, :] * (first - xs[0:8, :]), axis=0, keepdims=True))
        dx_ref[...] = dx.astype(BF16)
        dx_ref[tm - 8:tm, :] = last.astype(BF16)
        for sh in range(CONV_WIDTH):
            dw_ref[3 - sh:4 - sh, :] += dws[sh]

    rows = pl.BlockSpec((tm, CONV_CH), lambda i: (i, 0))
    prev8 = pl.BlockSpec((8, CONV_CH), lambda i: (jnp.maximum(i * (tm // 8) - 1, 0), 0))
    next8 = pl.BlockSpec((8, CONV_CH), lambda i: (jnp.minimum((i + 1) * (tm // 8), s // 8 - 1), 0))
    return pl.pallas_call(
        body, name="conv_bwd",
        out_shape=(jax.ShapeDtypeStruct(xbc.shape, BF16), jax.ShapeDtypeStruct((8, CONV_CH), F32),
                   jax.ShapeDtypeStruct((1, CONV_CH), F32)),
        grid=(nb,),
        in_specs=[rows, prev8, rows, next8, rows, next8, _const_spec((CONV_WIDTH, CONV_CH))],
        out_specs=(rows, _const_spec((8, CONV_CH)), _const_spec((1, CONV_CH))),
        compiler_params=_params(("arbitrary",)),
    )(xbc, xbc, pre, pre, dact, dact, w)


def _pair_lanes(mat, j, lane):
    return jnp.where(lane < HEAD_DIM, mat[:, 2 * j:2 * j + 1], mat[:, 2 * j + 1:2 * j + 2])


def _ssd_chunk_prelude(sm, dtb, a_row, lane, sub):
    raw = sm + dtb
    head_lane = lane < N_HEADS
    dt = jnp.where(head_lane, _softplus(raw), 0.0)
    sig = jnp.where(head_lane, _sigmoid(raw), 0.0)
    tri = (lane <= sub).astype(F32)
    acs = _mm_exact(tri, dt * a_row)
    return dt, sig, acs, acs.T


def ssd_fwd(xc, small, dtb_row, a_row, dskip_lane):
    s = xc.shape[0]
    nc = s // CHUNK

    def body(xc_ref, sm_ref, dtb_ref, a_ref, dsk_ref, y_ref, hs_ref, h_scr):
        c = pl.program_id(0)

        @pl.when(c == 0)
        def _():
            h_scr[...] = jnp.zeros_like(h_scr)

        lane = _iota((CHUNK, LANES), 1)
        sub = _iota((CHUNK, LANES), 0)
        causal = lane <= sub
        dt, _, acs, acs_t = _ssd_chunk_prelude(sm_ref[...], dtb_ref[...], a_ref[...], lane, sub)
        last = acs[CHUNK - 1:CHUNK, :]
        e_all = jnp.exp(acs)
        dte = jnp.exp(last - acs)
        cd = jnp.exp(last)
        for g in range(N_GROUPS):
            b_b = xc_ref[:, SSD_WIDTH + D_STATE * g:SSD_WIDTH + D_STATE * (g + 1)].astype(BF16)
            c_b = xc_ref[:, SSD_WIDTH + N_GROUPS * D_STATE + D_STATE * g:
                         SSD_WIDTH + N_GROUPS * D_STATE + D_STATE * (g + 1)].astype(BF16)
            cb = _mm_nt(c_b, b_b)
            for j in range(4 * g, 4 * g + 4):
                x2 = xc_ref[:, LANES * j:LANES * (j + 1)]
                xdt2 = x2 * _pair_lanes(dt, j, lane)
                xdt2_b = xdt2.astype(BF16)
                yd = []
                for e in range(2):
                    h = 2 * j + e
                    seg = acs[:, h:h + 1] - acs_t[h:h + 1, :]
                    lm = jnp.exp(jnp.where(causal, seg, NEG_BIG))
                    yd.append(_mm((cb * lm).astype(BF16), xdt2_b))
                h2 = h_scr[j]
                t2 = _mm_nt(c_b, h2.astype(BF16))
                y2 = (jnp.where(lane < HEAD_DIM, yd[0], yd[1]) + _pair_lanes(e_all, j, lane) * t2
                      + dsk_ref[:, LANES * j:LANES * (j + 1)] * x2)
                y_ref[:, LANES * j:LANES * (j + 1)] = y2
                hs_ref[0, j] = h2
                w2 = (xdt2 * _pair_lanes(dte, j, lane)).astype(BF16)
                s2 = _mm_tn(w2, b_b)
                cdcol = jnp.where(sub < HEAD_DIM, cd[:, 2 * j:2 * j + 1], cd[:, 2 * j + 1:2 * j + 2])
                h_scr[j] = h2 * cdcol + s2

    return pl.pallas_call(
        body, name="ssd_fwd",
        out_shape=(jax.ShapeDtypeStruct((s, SSD_WIDTH), F32),
                   jax.ShapeDtypeStruct((nc, N_PAIRS, LANES, D_STATE), F32)),
        grid=(nc,),
        in_specs=[pl.BlockSpec((CHUNK, CONV_CH), lambda c: (c, 0)), pl.BlockSpec((CHUNK, LANES), lambda c: (c, 0)),
                  _const_spec((1, LANES)), _const_spec((1, LANES)), _const_spec((1, SSD_WIDTH))],
        out_specs=(pl.BlockSpec((CHUNK, SSD_WIDTH), lambda c: (c, 0)),
                   pl.BlockSpec((1, N_PAIRS, LANES, D_STATE), lambda c: (c, 0, 0, 0))),
        scratch_shapes=[pltpu.VMEM((N_PAIRS, LANES, D_STATE), F32)],
        compiler_params=_params(("arbitrary",)),
    )(xc, small, dtb_row, a_row, dskip_lane)


def ssd_bwd(xc, small, states, dy, dtb_row, a_row, dskip_lane):
    s = xc.shape[0]
    nc = s // CHUNK
    rev = lambda c: nc - 1 - c

    def head_rowsums(q, lane):
        r0 = jnp.sum(jnp.where(lane < HEAD_DIM, q, 0.0), axis=1, keepdims=True)
        r1 = jnp.sum(jnp.where(lane < HEAD_DIM, 0.0, q), axis=1, keepdims=True)
        return r0, r1

    def body(xc_ref, sm_ref, hs_ref, dy_ref, dtb_ref, a_ref, dsk_ref,
             dxc_ref, ddt_ref, da_ref, ddtb_ref, ddsk_ref, dh_scr):
        c = pl.program_id(0)

        @pl.when(c == 0)
        def _():
            dh_scr[...] = jnp.zeros_like(dh_scr)
            da_ref[...] = jnp.zeros_like(da_ref)
            ddtb_ref[...] = jnp.zeros_like(ddtb_ref)
            ddsk_ref[...] = jnp.zeros_like(ddsk_ref)

        lane = _iota((CHUNK, LANES), 1)
        sub = _iota((CHUNK, LANES), 0)
        causal = lane <= sub
        is_last = sub == CHUNK - 1
        a_row_v = a_ref[...]
        dt, sig, acs, acs_t = _ssd_chunk_prelude(sm_ref[...], dtb_ref[...], a_row_v, lane, sub)
        last = acs[CHUNK - 1:CHUNK, :]
        e_all = jnp.exp(acs)
        dte = jnp.exp(last - acs)
        cd = jnp.exp(last)
        dacs_c = jnp.zeros((CHUNK, LANES), F32)
        dacs_r = jnp.zeros((LANES, CHUNK), F32)
        ddtx = jnp.zeros((CHUNK, LANES), F32)
        for g in range(N_GROUPS):
            b_off = SSD_WIDTH + D_STATE * g
            c_off = SSD_WIDTH + N_GROUPS * D_STATE + D_STATE * g
            b_b = xc_ref[:, b_off:b_off + D_STATE].astype(BF16)
            c_b = xc_ref[:, c_off:c_off + D_STATE].astype(BF16)
            cb = _mm_nt(c_b, b_b)
            dcb = jnp.zeros((CHUNK, CHUNK), F32)
            db_g = jnp.zeros((CHUNK, D_STATE), F32)
            dc_g = jnp.zeros((CHUNK, D_STATE), F32)
            for j in range(4 * g, 4 * g + 4):
                x2 = xc_ref[:, LANES * j:LANES * (j + 1)]
                dt2 = _pair_lanes(dt, j, lane)
                xdt2 = x2 * dt2
                xdt2_b = xdt2.astype(BF16)
                dy2 = dy_ref[:, LANES * j:LANES * (j + 1)]
                h2 = hs_ref[0, j]
                dh2 = dh_scr[j]
                h2_b = h2.astype(BF16)
                dh2_b = dh2.astype(BF16)
                dxdt2 = jnp.zeros((CHUNK, LANES), F32)
                for e in range(2):
                    h = 2 * j + e
                    in_head = (lane < HEAD_DIM) if e == 0 else (lane >= HEAD_DIM)
                    seg = acs[:, h:h + 1] - acs_t[h:h + 1, :]
                    lm = jnp.exp(jnp.where(causal, seg, NEG_BIG))
                    m_h = cb * lm
                    dyh_b = jnp.where(in_head, dy2, 0.0).astype(BF16)
                    dm_h = _mm_nt(dyh_b, xdt2_b)
                    dxdt2 = dxdt2 + _mm_tn(m_h.astype(BF16), dyh_b)
                    gmat = dm_h * m_h
                    dacs_c = dacs_c + jnp.where(lane == h, jnp.sum(gmat, axis=1, keepdims=True), 0.0)
                    dacs_r = dacs_r - jnp.where(sub == h, jnp.sum(gmat, axis=0, keepdims=True), 0.0)
                    dcb = dcb + dm_h * lm
                t2 = _mm_nt(c_b, h2_b)
                e2 = _pair_lanes(e_all, j, lane)
                r0, r1 = head_rowsums(dy2 * e2 * t2, lane)
                dacs_c = dacs_c + jnp.where(lane == 2 * j, r0, 0.0) + jnp.where(lane == 2 * j + 1, r1, 0.0)
                dt2_b = (dy2 * e2).astype(BF16)
                dc_g = dc_g + _mm(dt2_b, h2_b)
                dh_prev = _mm_tn(dt2_b, c_b)
                dw2 = _mm_nt(b_b, dh2_b)
                dte2 = _pair_lanes(dte, j, lane)
                w2 = xdt2 * dte2
                dxdt2 = dxdt2 + dw2 * dte2
                db_g = db_g + _mm(w2.astype(BF16), dh2_b)
                r0, r1 = head_rowsums(dw2 * w2, lane)
                q3 = dh2 * h2
                s0 = jnp.sum(jnp.where(sub < HEAD_DIM, q3, 0.0), keepdims=True)
                s1 = jnp.sum(jnp.where(sub < HEAD_DIM, 0.0, q3), keepdims=True)
                for e, (r, sq) in enumerate(((r0, s0), (r1, s1))):
                    h = 2 * j + e
                    at_end = jnp.sum(r, keepdims=True) + sq * cd[:, h:h + 1]
                    dacs_c = dacs_c + jnp.where(lane == h, jnp.where(is_last, at_end, 0.0) - r, 0.0)
                cdcol = jnp.where(sub < HEAD_DIM, cd[:, 2 * j:2 * j + 1], cd[:, 2 * j + 1:2 * j + 2])
                dh_scr[j] = dh_prev + dh2 * cdcol
                dsk2 = dsk_ref[:, LANES * j:LANES * (j + 1)]
                dxc_ref[:, LANES * j:LANES * (j + 1)] = dxdt2 * dt2 + dsk2 * dy2
                r0, r1 = head_rowsums(dxdt2 * x2, lane)
                ddtx = ddtx + jnp.where(lane == 2 * j, r0, 0.0) + jnp.where(lane == 2 * j + 1, r1, 0.0)
                ddsk_ref[:, LANES * j:LANES * (j + 1)] += jnp.sum(dy2 * x2, axis=0, keepdims=True)
            dcb_b = dcb.astype(BF16)
            dxc_ref[:, b_off:b_off + D_STATE] = db_g + _mm_tn(dcb_b, c_b)
            dxc_ref[:, c_off:c_off + D_STATE] = dc_g + _mm(dcb_b, b_b)
        dacs = dacs_c + dacs_r.T
        dadt = _mm_exact((lane >= sub).astype(F32), dacs)
        ddt = dadt * a_row_v + ddtx
        ddt_raw = ddt * sig
        ddt_ref[...] = ddt_raw
        da_ref[...] += jnp.sum(dadt * dt, axis=0, keepdims=True)
        ddtb_ref[...] += jnp.sum(ddt_raw, axis=0, keepdims=True)

    return pl.pallas_call(
        body, name="ssd_bwd",
        out_shape=(jax.ShapeDtypeStruct((s, CONV_CH), F32), jax.ShapeDtypeStruct((s, LANES), F32),
                   jax.ShapeDtypeStruct((1, LANES), F32), jax.ShapeDtypeStruct((1, LANES), F32),
                   jax.ShapeDtypeStruct((1, SSD_WIDTH), F32)),
        grid=(nc,),
        in_specs=[pl.BlockSpec((CHUNK, CONV_CH), lambda c: (rev(c), 0)),
                  pl.BlockSpec((CHUNK, LANES), lambda c: (rev(c), 0)),
                  pl.BlockSpec((1, N_PAIRS, LANES, D_STATE), lambda c: (rev(c), 0, 0, 0)),
                  pl.BlockSpec((CHUNK, SSD_WIDTH), lambda c: (rev(c), 0)),
                  _const_spec((1, LANES)), _const_spec((1, LANES)), _const_spec((1, SSD_WIDTH))],
        out_specs=(pl.BlockSpec((CHUNK, CONV_CH), lambda c: (rev(c), 0)),
                   pl.BlockSpec((CHUNK, LANES), lambda c: (rev(c), 0)),
                   _const_spec((1, LANES)), _const_spec((1, LANES)), _const_spec((1, SSD_WIDTH))),
        scratch_shapes=[pltpu.VMEM((N_PAIRS, LANES, D_STATE), F32)],
        compiler_params=_params(("arbitrary",)),
    )(xc, small, states, dy, dtb_row, a_row, dskip_lane)


def forget_cumsum(small, fgb_row):
    s = small.shape[0]
    nb = s // CHUNK

    def body(sm_ref, b_ref, cc_ref, ct_ref, carry):
        i = pl.program_id(0)

        @pl.when(i == 0)
        def _():
            carry[...] = jnp.zeros_like(carry)

        lane = _iota((CHUNK, LANES), 1)
        sub = _iota((CHUNK, LANES), 0)
        in_f = (lane >= N_HEADS) & (lane < 2 * N_HEADS)
        logf = jnp.where(in_f, -_softplus(-(sm_ref[...] + b_ref[...])), 0.0)
        tri = (lane <= sub).astype(F32)
        cum = _mm_exact(tri, logf) + carry[0:1, :]
        cc_ref[...] = cum
        ct_ref[...] = cum.T
        carry[...] = jnp.broadcast_to(cum[CHUNK - 1:CHUNK, :], (8, LANES))

    return pl.pallas_call(
        body, name="forget_cumsum",
        out_shape=(jax.ShapeDtypeStruct((s, LANES), F32), jax.ShapeDtypeStruct((LANES, s), F32)),
        grid=(nb,),
        in_specs=[pl.BlockSpec((CHUNK, LANES), lambda i: (i, 0)), _const_spec((1, LANES))],
        out_specs=(pl.BlockSpec((CHUNK, LANES), lambda i: (i, 0)), pl.BlockSpec((LANES, CHUNK), lambda i: (0, i))),
        scratch_shapes=[pltpu.VMEM((8, LANES), F32)],
        compiler_params=_params(("arbitrary",)),
    )(small, fgb_row)


def forget_bwd(dc_rows, dc_cols, small, ddt_raw, fgb_row):
    s = small.shape[0]
    nb = s // CHUNK
    rev = lambda i: nb - 1 - i

    def body(dc_ref, dcq_ref, sm_ref, ddt_ref, b_ref, ds_ref, dfb_ref, carry):
        i = pl.program_id(0)

        @pl.when(i == 0)
        def _():
            carry[...] = jnp.zeros_like(carry)
            dfb_ref[...] = jnp.zeros_like(dfb_ref)

        lane = _iota((CHUNK, LANES), 1)
        sub = _iota((CHUNK, LANES), 0)
        rows = dcq_ref[...].T
        for j in range(N_PAIRS):
            blk = dc_ref[j]
            for e in range(2):
                rows = rows + jnp.where(sub == N_HEADS + 2 * j + e, blk[e:e + 1, :], 0.0)
        tri = (lane <= sub).astype(F32)
        rc = _mm_exact(rows, tri) + carry[:, 0:1]
        carry[...] = jnp.broadcast_to(rc[:, 0:1], (LANES, LANES))
        in_f = (lane >= N_HEADS) & (lane < 2 * N_HEADS)
        df = jnp.where(in_f, rc.T * _sigmoid(-(sm_ref[...] + b_ref[...])), 0.0)
        ds_ref[...] = (df + ddt_ref[...]).astype(BF16)
        dfb_ref[...] += jnp.sum(df, axis=0, keepdims=True)

    return pl.pallas_call(
        body, name="forget_bwd",
        out_shape=(jax.ShapeDtypeStruct((s, LANES), BF16), jax.ShapeDtypeStruct((1, LANES), F32)),
        grid=(nb,),
        in_specs=[pl.BlockSpec((N_PAIRS, 8, CHUNK), lambda i: (0, 0, rev(i))),
                  pl.BlockSpec((CHUNK, LANES), lambda i: (rev(i), 0)),
                  pl.BlockSpec((CHUNK, LANES), lambda i: (rev(i), 0)),
                  pl.BlockSpec((CHUNK, LANES), lambda i: (rev(i), 0)), _const_spec((1, LANES))],
        out_specs=(pl.BlockSpec((CHUNK, LANES), lambda i: (rev(i), 0)), _const_spec((1, LANES))),
        scratch_shapes=[pltpu.VMEM((LANES, LANES), F32)],
        compiler_params=_params(("arbitrary",)),
    )(dc_rows, dc_cols, small, ddt_raw, fgb_row)


ATT_BLOCK = 256
ATT_SCALE = HEAD_DIM ** -0.5


def _att_bias(cc_ref, ct_ref, j, e, tq, tk):
    row = N_HEADS + 2 * j + e
    cq = jnp.sum(jnp.where(_iota((tq, LANES), 1) == row, cc_ref[...], 0.0), axis=1, keepdims=True)
    ck = jnp.sum(jnp.where(_iota((LANES, tk), 0) == row, ct_ref[...], 0.0), axis=0, keepdims=True)
    return cq, ck


def attention_fwd(q, k, v, cum_c, cum_t):
    s = q.shape[0]
    t = _blk(s, ATT_BLOCK)
    nq = s // t

    def body(q_ref, k_ref, v_ref, cc_ref, ct_ref, o_ref, lse_ref, m_scr, l_scr, acc_scr):
        j, qi, ki = pl.program_id(0), pl.program_id(1), pl.program_id(2)

        @pl.when(ki == 0)
        def _():
            m_scr[...] = jnp.full_like(m_scr, NEG_BIG)
            l_scr[...] = jnp.zeros_like(l_scr)
            acc_scr[...] = jnp.zeros_like(acc_scr)

        def step(masked):
            lane = _iota((t, LANES), 1)
            q2 = q_ref[...]
            k2 = k_ref[...]
            v2 = v_ref[...]
            for e in range(2):
                in_head = (lane < HEAD_DIM) if e == 0 else (lane >= HEAD_DIM)
                qe = jnp.where(in_head, q2, jnp.zeros_like(q2)) * jnp.asarray(ATT_SCALE, BF16)
                cq, ck = _att_bias(cc_ref, ct_ref, j, e, t, t)
                sc = _mm_nt(qe, k2) + (cq - ck)
                if masked:
                    sc = jnp.where(_iota((t, t), 0) >= _iota((t, t), 1), sc, NEG_BIG)
                m_old = m_scr[e][:, 0:1]
                m_new = jnp.maximum(m_old, jnp.max(sc, axis=1, keepdims=True))
                alpha = jnp.exp(m_old - m_new)
                p = jnp.exp(sc - m_new)
                l_new = alpha * l_scr[e][:, 0:1] + jnp.sum(p, axis=1, keepdims=True)
                acc_scr[e] = alpha * acc_scr[e] + _mm(p.astype(BF16), v2)
                m_scr[e] = jnp.broadcast_to(m_new, (t, LANES))
                l_scr[e] = jnp.broadcast_to(l_new, (t, LANES))

        @pl.when(ki < qi)
        def _():
            step(False)

        @pl.when(ki == qi)
        def _():
            step(True)
            lane = _iota((t, LANES), 1)
            l0, l1 = l_scr[0], l_scr[1]
            o_ref[...] = jnp.where(lane < HEAD_DIM, acc_scr[0] / l0, acc_scr[1] / l1)
            lse_ref[...] = jnp.where(lane < HEAD_DIM, m_scr[0] + jnp.log(l0), m_scr[1] + jnp.log(l1))

    qspec = pl.BlockSpec((t, LANES), lambda j, qi, ki: (qi, j))
    kspec = pl.BlockSpec((t, LANES), lambda j, qi, ki: (jnp.minimum(ki, qi), j))
    return pl.pallas_call(
        body, name="attention_fwd",
        out_shape=(jax.ShapeDtypeStruct((s, ATT_WIDTH), F32), jax.ShapeDtypeStruct((s, ATT_WIDTH), F32)),
        grid=(N_PAIRS, nq, nq),
        in_specs=[qspec, kspec, kspec,
                  pl.BlockSpec((t, LANES), lambda j, qi, ki: (qi, 0)),
                  pl.BlockSpec((LANES, t), lambda j, qi, ki: (0, jnp.minimum(ki, qi)))],
        out_specs=(qspec, qspec),
        scratch_shapes=[pltpu.VMEM((2, t, LANES), F32)] * 3,
        compiler_params=_params(("parallel", "parallel", "arbitrary")),
    )(q, k, v, cum_c, cum_t)


def attention_bwd(q, k, v, o, lse, do, cum_c, cum_t):
    s = q.shape[0]
    t = _blk(s, ATT_BLOCK)
    nq = s // t

    def body(q_ref, k_ref, v_ref, o_ref, lse_ref, do_ref, cc_ref, ct_ref,
             dq_ref, dk_ref, dv_ref, dc_ref, dcq_ref, dq_scr, dk_scr, dv_scr, dc_scr):
        j, ki, qi = pl.program_id(0), pl.program_id(1), pl.program_id(2)

        @pl.when((j == 0) & (ki == 0) & (qi == 0))
        def _():
            dcq_ref[...] = jnp.zeros_like(dcq_ref)

        @pl.when((ki == 0) & (qi == 0))
        def _():
            dq_scr[...] = jnp.zeros_like(dq_scr)

        @pl.when(qi == 0)
        def _():
            dk_scr[...] = jnp.zeros_like(dk_scr)
            dv_scr[...] = jnp.zeros_like(dv_scr)
            dc_scr[...] = jnp.zeros_like(dc_scr)

        def step(masked):
            lane = _iota((t, LANES), 1)
            q2 = q_ref[...]
            k2 = k_ref[...]
            v2 = v_ref[...]
            do2 = do_ref[...]
            prod = do2.astype(F32) * o_ref[...]
            lse2 = lse_ref[...]
            dq_blk = jnp.zeros((t, LANES), F32)
            dcq_blk = jnp.zeros((t, LANES), F32)
            for e in range(2):
                in_head = (lane < HEAD_DIM) if e == 0 else (lane >= HEAD_DIM)
                zero = jnp.zeros_like(q2)
                qe = jnp.where(in_head, q2, zero) * jnp.asarray(ATT_SCALE, BF16)
                ke = jnp.where(in_head, k2, zero)
                doe = jnp.where(in_head, do2, zero)
                delta = jnp.sum(jnp.where(in_head, prod, 0.0), axis=1, keepdims=True)
                cq, ck = _att_bias(cc_ref, ct_ref, j, e, t, t)
                sc = _mm_nt(qe, k2) + (cq - ck)
                if masked:
                    sc = jnp.where(_iota((t, t), 0) >= _iota((t, t), 1), sc, NEG_BIG)
                p = jnp.exp(sc - lse2[:, HEAD_DIM * e:HEAD_DIM * e + 1])
                dp = _mm_nt(doe, v2)
                ds = p * (dp - delta)
                ds_b = ds.astype(BF16)
                dv_scr[...] += _mm_tn(p.astype(BF16), doe)
                dk_scr[...] += _mm_tn(ds_b, qe)
                dq_blk = dq_blk + _mm(ds_b, ke)
                dc_scr[e:e + 1, :] -= jnp.sum(ds, axis=0, keepdims=True)
                dcq_blk = dcq_blk + jnp.where(lane == N_HEADS + 2 * j + e, jnp.sum(ds, axis=1, keepdims=True), 0.0)
            rows = pl.ds(pl.multiple_of(qi * t, t), t)
            dq_scr[rows, :] += dq_blk * ATT_SCALE
            dcq_ref[rows, :] += dcq_blk

        @pl.when(qi > ki)
        def _():
            step(False)

        @pl.when(qi == ki)
        def _():
            step(True)

        @pl.when(qi == nq - 1)
        def _():
            dk_ref[...] = dk_scr[...].astype(BF16)
            dv_ref[...] = dv_scr[...].astype(BF16)
            dc_ref[0] = dc_scr[...]

        @pl.when((ki == nq - 1) & (qi == nq - 1))
        def _():
            dq_ref[...] = dq_scr[...].astype(BF16)

    qspec = pl.BlockSpec((t, LANES), lambda j, ki, qi: (jnp.maximum(qi, ki), j))
    kspec = pl.BlockSpec((t, LANES), lambda j, ki, qi: (ki, j))
    return pl.pallas_call(
        body, name="attention_bwd",
        out_shape=(jax.ShapeDtypeStruct((s, ATT_WIDTH), BF16), jax.ShapeDtypeStruct((s, ATT_WIDTH), BF16),
                   jax.ShapeDtypeStruct((s, ATT_WIDTH), BF16), jax.ShapeDtypeStruct((N_PAIRS, 8, s), F32),
                   jax.ShapeDtypeStruct((s, LANES), F32)),
        grid=(N_PAIRS, nq, nq),
        in_specs=[qspec, kspec, kspec, qspec, qspec, qspec,
                  pl.BlockSpec((t, LANES), lambda j, ki, qi: (jnp.maximum(qi, ki), 0)),
                  pl.BlockSpec((LANES, t), lambda j, ki, qi: (0, ki))],
        out_specs=(pl.BlockSpec((s, LANES), lambda j, ki, qi: (0, j)), kspec, kspec,
                   pl.BlockSpec((1, 8, t), lambda j, ki, qi: (j, 0, ki)), _const_spec((s, LANES))),
        scratch_shapes=[pltpu.VMEM((s, LANES), F32), pltpu.VMEM((t, LANES), F32), pltpu.VMEM((t, LANES), F32),
                        pltpu.VMEM((8, t), F32)],
        compiler_params=_params(("arbitrary", "arbitrary", "arbitrary")),
    )(q, k, v, o, lse, do, cum_c, cum_t)


def _dsilu(z, sg):
    return sg * (1.0 + z * (1.0 - sg))


def post_mix(x, y, zs, o, za, p, tgt, ssd_g, att_g_lane, ple_g, fin_g, w_out, w_gate, w_proj):
    s = x.shape[0]
    tm = _blk(s, 128)
    half = SSD_WIDTH // N_GROUPS

    def rms_bwd(dy, yn, r):
        return r * (dy - yn * jnp.mean(dy * yn, axis=-1, keepdims=True))

    def colsum(a):
        return jnp.sum(a, axis=0, keepdims=True)

    def body(x_ref, y_ref, zs_ref, o_ref, za_ref, p_ref, t_ref, sg_ref, ag_ref, pg_ref, fg_ref,
             wo_ref, wg_ref, wp_ref,
             dh1_ref, dy_ref, dzs_ref, do_ref, dza_ref, ycat_ref, dh1b_ref, n2b_ref, dglb_ref, dppb_ref, pb_ref,
             loss_ref, dfin_ref, dple_ref, dssd_ref, datt_ref):
        @pl.when(pl.program_id(0) == 0)
        def _():
            for r in (loss_ref, dfin_ref, dple_ref, dssd_ref, datt_ref):
                r[...] = jnp.zeros_like(r)

        lane = _iota((tm, LANES), 1)
        lo = lane < HEAD_DIM
        zs = zs_ref[...]
        sz = _sigmoid(zs)
        yv = y_ref[...]
        ys = yv * (zs * sz)
        yn, rg = [], []
        for g in range(N_GROUPS):
            seg = ys[:, half * g:half * (g + 1)]
            r = lax.rsqrt(jnp.mean(seg * seg, axis=-1, keepdims=True) + EPS)
            yn.append(seg * r)
            rg.append(r)
            ycat_ref[:, half * g:half * (g + 1)] = (yn[g] * sg_ref[:, half * g:half * (g + 1)]).astype(BF16)
        za = za_ref[...]
        sza = _sigmoid(za)
        silu_za = za * sza
        on, ra = [], []
        for jb in range(N_PAIRS):
            blk = o_ref[:, LANES * jb:LANES * (jb + 1)]
            sq = blk * blk
            ms0 = jnp.sum(jnp.where(lo, sq, 0.0), axis=1, keepdims=True) * (1.0 / HEAD_DIM)
            ms1 = jnp.sum(jnp.where(lo, 0.0, sq), axis=1, keepdims=True) * (1.0 / HEAD_DIM)
            r = jnp.where(lo, lax.rsqrt(ms0 + EPS), lax.rsqrt(ms1 + EPS))
            on.append(blk * r)
            ra.append(r)
            an = on[jb] * ag_ref[:, LANES * jb:LANES * (jb + 1)]
            ycat_ref[:, SSD_WIDTH + LANES * jb:SSD_WIDTH + LANES * (jb + 1)] = (
                an * silu_za[:, LANES * jb:LANES * (jb + 1)]).astype(BF16)
        h1 = x_ref[...] + _mm(ycat_ref[...], wo_ref[...])
        r2 = lax.rsqrt(jnp.mean(h1 * h1, axis=-1, keepdims=True) + EPS)
        n2h = h1 * r2
        n2_b = (n2h * pg_ref[...]).astype(BF16)
        gate = _sigmoid(_mm(n2_b, wg_ref[...]))
        p_b = p_ref[...].astype(BF16)
        pp = _mm(p_b, wp_ref[...])
        h2 = h1 + gate * pp
        r3 = lax.rsqrt(jnp.mean(h2 * h2, axis=-1, keepdims=True) + EPS)
        n3 = h2 * r3
        diff = n3 * fg_ref[...] - t_ref[...]
        sq = colsum(diff * diff)
        part = sq[:, 0:LANES]
        for jb in range(1, D_MODEL // LANES):
            part = part + sq[:, LANES * jb:LANES * (jb + 1)]
        loss_ref[...] += part * (0.5 / D_MODEL)
        dout = diff * (1.0 / D_MODEL)
        dfin_ref[...] += colsum(dout * n3)
        dh2 = rms_bwd(dout * fg_ref[...], n3, r3)
        dgl = dh2 * pp * gate * (1.0 - gate)
        dgl_b = dgl.astype(BF16)
        dn2 = _mm_nt(dgl_b, wg_ref[...])
        dple_ref[...] += colsum(dn2 * n2h)
        dh1 = dh2 + rms_bwd(dn2 * pg_ref[...], n2h, r2)
        dh1_b = dh1.astype(BF16)
        dycat = _mm_nt(dh1_b, wo_ref[...])
        dh1_ref[...] = dh1
        dh1b_ref[...] = dh1_b
        n2b_ref[...] = n2_b
        dglb_ref[...] = dgl_b
        dppb_ref[...] = (dh2 * gate).astype(BF16)
        pb_ref[...] = p_b
        for g in range(N_GROUPS):
            cols = slice(half * g, half * (g + 1))
            dys_g = dycat[:, cols]
            dssd_ref[:, cols] += colsum(dys_g * yn[g])
            dys = rms_bwd(dys_g * sg_ref[:, cols], yn[g], rg[g])
            dy_ref[:, cols] = dys * (zs[:, cols] * sz[:, cols])
            dzs_ref[:, cols] = (dys * yv[:, cols] * _dsilu(zs[:, cols], sz[:, cols])).astype(BF16)
        for jb in range(N_PAIRS):
            cols = slice(LANES * jb, LANES * (jb + 1))
            dya = dycat[:, SSD_WIDTH + LANES * jb:SSD_WIDTH + LANES * (jb + 1)]
            ag = ag_ref[:, cols]
            dan = dya * silu_za[:, cols]
            dza_ref[:, cols] = (dya * (on[jb] * ag) * _dsilu(za[:, cols], sza[:, cols])).astype(BF16)
            datt_ref[:, cols] += colsum(dan * on[jb])
            don = dan * ag
            q = don * on[jb]
            m0 = jnp.sum(jnp.where(lo, q, 0.0), axis=1, keepdims=True) * (1.0 / HEAD_DIM)
            m1 = jnp.sum(jnp.where(lo, 0.0, q), axis=1, keepdims=True) * (1.0 / HEAD_DIM)
            do_ref[:, cols] = (ra[jb] * (don - on[jb] * jnp.where(lo, m0, m1))).astype(BF16)

    def rows(n, dtype=None):
        return pl.BlockSpec((tm, n), lambda i: (i, 0))

    def out(n, dtype):
        return jax.ShapeDtypeStruct((s, n), dtype)

    vec = _const_spec((1, D_MODEL))
    vshape = jax.ShapeDtypeStruct((1, D_MODEL), F32)
    return pl.pallas_call(
        body, name="post_mix",
        out_shape=(out(D_MODEL, F32), out(SSD_WIDTH, F32), out(SSD_WIDTH, BF16), out(ATT_WIDTH, BF16),
                   out(ATT_WIDTH, BF16), out(D_INNER, BF16), out(D_MODEL, BF16), out(D_MODEL, BF16),
                   out(D_MODEL, BF16), out(D_MODEL, BF16), out(PLE_DIM, BF16),
                   jax.ShapeDtypeStruct((1, LANES), F32), vshape, vshape, vshape, vshape),
        grid=(s // tm,),
        in_specs=[rows(D_MODEL), rows(SSD_WIDTH), rows(SSD_WIDTH), rows(ATT_WIDTH), rows(ATT_WIDTH),
                  rows(PLE_DIM), rows(D_MODEL), vec, vec, vec, vec,
                  _const_spec((D_INNER, D_MODEL)), _const_spec((D_MODEL, D_MODEL)), _const_spec((PLE_DIM, D_MODEL))],
        out_specs=(rows(D_MODEL), rows(SSD_WIDTH), rows(SSD_WIDTH), rows(ATT_WIDTH), rows(ATT_WIDTH),
                   rows(D_INNER), rows(D_MODEL), rows(D_MODEL), rows(D_MODEL), rows(D_MODEL), rows(PLE_DIM),
                   _const_spec((1, LANES)), vec, vec, vec, vec),
        compiler_params=_params(("arbitrary",)),
    )(x, y, zs, o, za, p, tgt, ssd_g, att_g_lane, ple_g, fin_g, w_out, w_gate, w_proj)


def in_proj_bwd(dsegs, wsegs, x, g, dh1):
    s = x.shape[0]
    tm = _blk(s, 256)
    nseg = len(dsegs)

    def body(*refs):
        d_refs = refs[:nseg]
        w_refs = refs[nseg:2 * nseg]
        x_ref, g_ref, dh1_ref, dx_ref, dg_ref = refs[2 * nseg:]

        @pl.when(pl.program_id(0) == 0)
        def _():
            dg_ref[...] = jnp.zeros_like(dg_ref)

        du = _mm_nt(d_refs[0][...], w_refs[0][...])
        for k in range(1, nseg):
            du = du + _mm_nt(d_refs[k][...], w_refs[k][...])
        xv = x_ref[...]
        r = lax.rsqrt(jnp.mean(xv * xv, axis=-1, keepdims=True) + EPS)
        xh = xv * r
        dg_ref[...] += jnp.sum(du * xh, axis=0, keepdims=True)
        dxh = du * g_ref[...]
        dx_ref[...] = r * (dxh - xh * jnp.mean(dxh * xh, axis=-1, keepdims=True)) + dh1_ref[...]

    rows = lambda n: pl.BlockSpec((tm, n), lambda i: (i, 0))
    return pl.pallas_call(
        body, name="in_proj_bwd",
        out_shape=(jax.ShapeDtypeStruct((s, D_MODEL), F32), jax.ShapeDtypeStruct((1, D_MODEL), F32)),
        grid=(s // tm,),
        in_specs=([rows(d.shape[1]) for d in dsegs] + [_const_spec(w.shape) for w in wsegs]
                  + [rows(D_MODEL), _const_spec((1, D_MODEL)), rows(D_MODEL)]),
        out_specs=(rows(D_MODEL), _const_spec((1, D_MODEL))),
        compiler_params=_params(("arbitrary",)),
    )(*dsegs, *wsegs, x, g, dh1)


def _pack(w_in_s, w_out_s, w_gate_s, w_proj_s, conv_s, dtype):
    parts = [w_in_s.reshape(ROWS_W_IN, LANES), w_out_s.reshape(ROWS_W_OUT, LANES),
             w_gate_s.reshape(ROWS_W_GATE, LANES), w_proj_s.reshape(ROWS_W_PROJ, LANES)]
    used = ROWS_W_IN + ROWS_W_OUT + ROWS_W_GATE + ROWS_W_PROJ
    if conv_s is not None:
        parts.append(jnp.pad(conv_s.reshape(12, LANES), ((0, 4), (0, 0))))
        used += ROWS_CONV
    parts = [a.astype(dtype) for a in parts]
    parts.append(jnp.zeros((PACK_ROWS - used, LANES), dtype))
    return jnp.concatenate(parts, axis=0)


def _unpack(pack):
    o1 = ROWS_W_IN
    o2 = o1 + ROWS_W_OUT
    o3 = o2 + ROWS_W_GATE
    o4 = o3 + ROWS_W_PROJ
    return (pack[:o1].reshape(1, 1024, 1672), pack[o1:o2].reshape(1, 512, 1024),
            pack[o2:o3].reshape(1, 256, 1024), pack[o3:o4].reshape(1, 256, 256),
            pack[o4:o4 + 12].reshape(1, 4, 384))


SMALL_NAMES = ("norm_g", "conv_b", "dt_bias", "a_log", "d_skip", "ssd_norm_g", "fg_bias", "att_norm_g",
               "ple_norm_g", "final_norm_g")
SMALL_SIZES = (1024, 1536, 16, 16, 16, 1024, 16, 64, 1024, 1024)


def _pack_small(vals):
    flat = jnp.concatenate([v.reshape(-1).astype(F32) for v in vals])
    flat = jnp.pad(flat, (0, SMALL_ROWS * LANES - flat.shape[0]))
    return flat.reshape(SMALL_ROWS, LANES)


def _unpack_small(pack, shapes):
    flat = pack.reshape(-1)
    out, off = [], 0
    for n, shp in zip(SMALL_SIZES, shapes):
        out.append(flat[off:off + n].reshape(shp))
        off += n
    return out


def _row128(v16, offset=0):
    return jnp.pad(v16.reshape(1, N_HEADS).astype(F32), ((0, 0), (offset, LANES - N_HEADS - offset)))


def local_step(x, p, tgt, w_in, w_out, w_gate, w_proj, conv_w, norm_g, conv_b, dt_bias, a_log, d_skip,
               ssd_norm_g, fg_bias, att_norm_g, ple_norm_g, final_norm_g):
    c0, c1, c2, c3, c4, c5, c6, c7 = 0, 1024, 2560, 2576, 3600, 4624, 5648, 6672
    w_zs, w_xbc, w_dt = w_in[:, c0:c1], w_in[:, c1:c2], w_in[:, c2:c3]
    w_za, w_q, w_k, w_v, w_f = w_in[:, c3:c4], w_in[:, c4:c5], w_in[:, c5:c6], w_in[:, c6:c7], w_in[:, c7:]
    w_small = jnp.concatenate([w_dt, w_f, jnp.zeros((D_MODEL, LANES - 2 * N_HEADS), BF16)], axis=1)

    dtb_row = _row128(dt_bias)
    a_row = _row128(-jnp.exp(a_log.astype(F32)))
    fgb_row = _row128(fg_bias, N_HEADS)
    dskip_lane = jnp.repeat(d_skip.astype(F32), HEAD_DIM).reshape(1, SSD_WIDTH)
    att_g_lane = jnp.tile(att_norm_g.astype(F32), N_HEADS).reshape(1, ATT_WIDTH)
    row = lambda v: v.reshape(1, -1).astype(F32)

    u = rms_prenorm(x, row(norm_g))
    zs = matmul_rows(u, w_zs, F32, "proj_z_ssd")
    xbc = matmul_rows(u, w_xbc, F32, "proj_xbc")
    za = matmul_rows(u, w_za, F32, "proj_z_att")
    q = matmul_rows(u, w_q, BF16, "proj_q")
    k = matmul_rows(u, w_k, BF16, "proj_k")
    v = matmul_rows(u, w_v, BF16, "proj_v")
    small = matmul_rows(u, w_small, F32, "proj_small")
    pre, xc = conv_fwd(xbc, conv_w, row(conv_b))
    y, states = ssd_fwd(xc, small, dtb_row, a_row, dskip_lane)
    cum_c, cum_t = forget_cumsum(small, fgb_row)
    o, lse = attention_fwd(q, k, v, cum_c, cum_t)
    (dh1, dy, dzs, do, dza, ycat, dh1_b, n2_b, dgl_b, dpp_b, p_b,
     loss_l, dfin, dple, dssd_g, datt_lane) = post_mix(
        x, y, zs, o, za, p, tgt, row(ssd_norm_g), att_g_lane, row(ple_norm_g), row(final_norm_g),
        w_out, w_gate, w_proj)
    dq, dk, dv, dc_rows, dc_cols = attention_bwd(q, k, v, o, lse, do, cum_c, cum_t)
    dxc, ddt_raw, da, ddtb, ddsk_lane = ssd_bwd(xc, small, states, dy, dtb_row, a_row, dskip_lane)
    dsmall, dfgb = forget_bwd(dc_rows, dc_cols, small, ddt_raw, fgb_row)
    dxbc, dconv_w8, dconv_b = conv_bwd(xbc, pre, dxc, conv_w)
    dsegs = [dzs, dxbc, dza, dq, dk, dv, dsmall]
    wsegs = [w_zs, w_xbc, w_za, w_q, w_k, w_v, w_small]
    dx, dnorm_g = in_proj_bwd(dsegs, wsegs, x, row(norm_g), dh1)
    dws = [matmul_tn(u, d, "dw_in_%d" % i) for i, d in enumerate(dsegs)]
    dw_in = jnp.concatenate([dws[0], dws[1], dws[6][:, :N_HEADS], dws[2], dws[3], dws[4], dws[5],
                             dws[6][:, N_HEADS:2 * N_HEADS]], axis=1)
    dw_out = matmul_tn(ycat, dh1_b, "dw_out")
    dw_gate = matmul_tn(n2_b, dgl_b, "dw_gate")
    dw_proj = matmul_tn(p_b, dpp_b, "dw_proj")
    small_grads = [
        dnorm_g, dconv_b, ddtb[0, :N_HEADS], (da * a_row)[0, :N_HEADS],
        ddsk_lane.reshape(N_HEADS, HEAD_DIM).sum(axis=1), dssd_g, dfgb[0, N_HEADS:2 * N_HEADS],
        datt_lane.reshape(N_HEADS, HEAD_DIM).sum(axis=0), dple, dfin]
    loss = jnp.sum(loss_l)
    return loss, dx, dw_in, dw_out, dw_gate, dw_proj, dconv_w8[:CONV_WIDTH], small_grads


def kernel(x, p, norm_g, w_in, conv_w, conv_b, dt_bias, a_log, d_skip, ssd_norm_g, fg_bias, att_norm_g, w_out, ple_norm_g, w_ple_gate, w_ple_proj, final_norm_g, loss_target, m_norm_g, m_w_in, m_conv_w, m_conv_b, m_dt_bias, m_a_log, m_d_skip, m_ssd_norm_g, m_fg_bias, m_att_norm_g, m_w_out, m_ple_norm_g, m_w_ple_gate, m_w_ple_proj, m_final_norm_g, v_norm_g, v_w_in, v_conv_w, v_conv_b, v_dt_bias, v_a_log, v_d_skip, v_ssd_norm_g, v_fg_bias, v_att_norm_g, v_w_out, v_ple_norm_g, v_w_ple_gate, v_w_ple_proj, v_final_norm_g):
    wpack = _pack(w_in[0], w_out[0], w_ple_gate[0], w_ple_proj[0], None, BF16)
    cpack = jnp.pad(conv_w[0].reshape(12, LANES), ((0, 4), (0, 0)))
    wall, call = gather_weights(wpack, cpack)
    o1 = ROWS_W_IN
    o2 = o1 + ROWS_W_OUT
    o3 = o2 + ROWS_W_GATE
    o4 = o3 + ROWS_W_PROJ
    w_in_f = jnp.concatenate([wall[j, :o1].reshape(1024, 1672) for j in range(N_CHIPS)], axis=1)
    w_out_f = jnp.concatenate([wall[j, o1:o2].reshape(512, 1024) for j in range(N_CHIPS)], axis=0)
    w_gate_f = jnp.concatenate([wall[j, o2:o3].reshape(256, 1024) for j in range(N_CHIPS)], axis=0)
    w_proj_f = jnp.concatenate([wall[j, o3:o4].reshape(256, 256) for j in range(N_CHIPS)], axis=1)
    conv_w_f = jnp.concatenate([call[j, :12].reshape(4, 384) for j in range(N_CHIPS)], axis=1)

    smalls_w = [norm_g, conv_b, dt_bias, a_log, d_skip, ssd_norm_g, fg_bias, att_norm_g, ple_norm_g, final_norm_g]
    loss_l, dx, dw_in, dw_out, dw_gate, dw_proj, dconv_w, small_grads = local_step(
        x[0], p[0, 0], loss_target[0], w_in_f, w_out_f, w_gate_f, w_proj_f, conv_w_f,
        *[a.reshape(-1) for a in smalls_w])
    loss = lax.psum(loss_l, ("x", "y", "c"))

    gpack = jnp.stack([
        _pack(dw_in[:, 1672 * j:1672 * (j + 1)], dw_out[512 * j:512 * (j + 1)], dw_gate[256 * j:256 * (j + 1)],
              dw_proj[:, 256 * j:256 * (j + 1)], dconv_w[:, 384 * j:384 * (j + 1)], F32)
        for j in range(N_CHIPS)])
    parts, smalls = scatter_grads(gpack, _pack_small(small_grads))
    mine = sum_parts(parts)
    theirs = swap_with_sibling(mine)
    south = lax.axis_index("c") == 0
    p_south = jnp.where(south, mine, theirs)
    p_north = jnp.where(south, theirs, mine)

    big = lambda a, b, c_, d, e: _pack(a[0], b[0], c_[0], d[0], e[0], F32)
    g_pk, d_pk, m_pk, v_pk = adamw_pack(
        p_south, p_north, big(w_in, w_out, w_ple_gate, w_ple_proj, conv_w),
        big(m_w_in, m_w_out, m_w_ple_gate, m_w_ple_proj, m_conv_w),
        big(v_w_in, v_w_out, v_w_ple_gate, v_w_ple_proj, v_conv_w))
    smalls_m = [m_norm_g, m_conv_b, m_dt_bias, m_a_log, m_d_skip, m_ssd_norm_g, m_fg_bias, m_att_norm_g,
                m_ple_norm_g, m_final_norm_g]
    smalls_v = [v_norm_g, v_conv_b, v_dt_bias, v_a_log, v_d_skip, v_ssd_norm_g, v_fg_bias, v_att_norm_g,
                v_ple_norm_g, v_final_norm_g]
    g_sm, d_sm, m_sm, v_sm = adamw_small(smalls, _pack_small(smalls_w), _pack_small(smalls_m), _pack_small(smalls_v))

    shapes = [a.shape for a in smalls_w]
    outs = []
    for pk, sm in ((g_pk, g_sm), (d_pk, d_sm), (m_pk, m_sm), (v_pk, v_sm)):
        b_in, b_out, b_gate, b_proj, b_conv = _unpack(pk)
        s_norm, s_convb, s_dtb, s_alog, s_dsk, s_ssdg, s_fgb, s_attg, s_pleg, s_fin = _unpack_small(sm, shapes)
        outs.extend([s_norm, b_in, b_conv, s_convb, s_dtb, s_alog, s_dsk, s_ssdg, s_fgb, s_attg, b_out, s_pleg,
                     b_gate, b_proj, s_fin])
    return (loss, dx[None], *outs)
```

```python
import functools

import jax
import jax.numpy as jnp
from jax import lax
from jax.experimental import pallas as pl
from jax.experimental.pallas import tpu as pltpu

F32 = jnp.float32
BF16 = jnp.bfloat16

D_MODEL = 1024
SSD_WIDTH = 1024
ATT_WIDTH = 1024
N_HEADS = 16
HEAD_DIM = 64
N_GROUPS = 2
D_STATE = 128
CONV_CH = 1536
CONV_WIDTH = 4
CHUNK = 128
PLE_DIM = 256
D_INNER = 2048
EPS = 1e-6
IN_COLS = 6688
N_CHIPS = 4
N_DEV = 8
LANES = 128
N_PAIRS = 8

ADAM_LR = 0.001
ADAM_B1 = 0.9
ADAM_B2 = 0.999
ADAM_EPS = 1e-08
ADAM_WD = 0.01
ADAM_STEP = 10

ROWS_W_IN = 1024 * 1672 // LANES
ROWS_W_OUT = 512 * 1024 // LANES
ROWS_W_GATE = 256 * 1024 // LANES
ROWS_W_PROJ = 256 * 256 // LANES
ROWS_CONV = 16
PACK_ROWS = 20480
PACK_BLOCK = 2048
SMALL_ROWS = 48

NEG_BIG = -1e30
VMEM_LIMIT = 56 * 1024 * 1024

MESH = pl.DeviceIdType.MESH
ANY = pl.BlockSpec(memory_space=pl.ANY)


def _mm(a, b):
    return jnp.dot(a, b, preferred_element_type=F32)


def _mm_nt(a, b):
    return lax.dot_general(a, b, (((1,), (1,)), ((), ())), preferred_element_type=F32)


def _mm_tn(a, b):
    return lax.dot_general(a, b, (((0,), (0,)), ((), ())), preferred_element_type=F32)


def _mm_exact(a, b):
    return jnp.dot(a, b, preferred_element_type=F32, precision=lax.Precision.HIGHEST)


def _softplus(x):
    return jnp.maximum(x, 0.0) + jnp.log1p(jnp.exp(-jnp.abs(x)))


def _sigmoid(x):
    return jax.nn.sigmoid(x)


def _iota(shape, dim):
    return lax.broadcasted_iota(jnp.int32, shape, dim)


def _params(sem=None):
    return pltpu.CompilerParams(dimension_semantics=sem, vmem_limit_bytes=VMEM_LIMIT)


def _blk(n, pref):
    return min(n, pref)


def _const_spec(shape):
    nd = len(shape)
    return pl.BlockSpec(shape, lambda *_: (0,) * nd)


def _chip_peers():
    x, y, c = lax.axis_index("x"), lax.axis_index("y"), lax.axis_index("c")
    return x, y, c, [(1 - x, y, c), (x, 1 - y, c), (1 - x, 1 - y, c)]


def gather_weights(wpack, cpack):
    def body(w_ref, c_ref, wall_ref, call_ref, ssem, rsem, lsem):
        x, y, _, peers = _chip_peers()
        me = 2 * x + y
        local = [pltpu.make_async_copy(w_ref, wall_ref.at[me], lsem.at[0]),
                 pltpu.make_async_copy(c_ref, call_ref.at[me], lsem.at[1])]
        for cp in local:
            cp.start()
        remote = []
        for k, peer in enumerate(peers):
            remote.append(pltpu.make_async_remote_copy(
                src_ref=w_ref, dst_ref=wall_ref.at[me], send_sem=ssem.at[k], recv_sem=rsem.at[k],
                device_id=peer, device_id_type=MESH))
            remote.append(pltpu.make_async_remote_copy(
                src_ref=c_ref, dst_ref=call_ref.at[me], send_sem=ssem.at[3 + k], recv_sem=rsem.at[3 + k],
                device_id=peer, device_id_type=MESH))
        for cp in remote:
            cp.start()
        for cp in remote:
            cp.wait()
        for cp in local:
            cp.wait()

    return pl.pallas_call(
        body, name="gather_weights",
        out_shape=(jax.ShapeDtypeStruct((N_CHIPS,) + wpack.shape, wpack.dtype),
                   jax.ShapeDtypeStruct((N_CHIPS,) + cpack.shape, cpack.dtype)),
        in_specs=[ANY, ANY], out_specs=(ANY, ANY),
        scratch_shapes=[pltpu.SemaphoreType.DMA((6,)), pltpu.SemaphoreType.DMA((6,)),
                        pltpu.SemaphoreType.DMA((2,))],
    )(wpack, cpack)


def scatter_grads(gpack, small):
    def body(g_ref, s_ref, parts_ref, smalls_ref, ssem, rsem, s_ssem, s_rsem, lsem):
        x, y, c, peers = _chip_peers()
        me = 2 * x + y
        dev = 4 * x + 2 * y + c
        local = [pltpu.make_async_copy(g_ref.at[me], parts_ref.at[me], lsem.at[0]),
                 pltpu.make_async_copy(s_ref, smalls_ref.at[dev], lsem.at[1])]
        for cp in local:
            cp.start()
        remote = []
        for k, peer in enumerate(peers):
            dst_chip = 2 * peer[0] + peer[1]
            remote.append(pltpu.make_async_remote_copy(
                src_ref=g_ref.at[dst_chip], dst_ref=parts_ref.at[me], send_sem=ssem.at[k], recv_sem=rsem.at[k],
                device_id=peer, device_id_type=MESH))
        for k in range(1, N_DEV):
            fx, fy, fc = (k >> 2) & 1, (k >> 1) & 1, k & 1
            peer = ((1 - x) if fx else x, (1 - y) if fy else y, (1 - c) if fc else c)
            remote.append(pltpu.make_async_remote_copy(
                src_ref=s_ref, dst_ref=smalls_ref.at[dev], send_sem=s_ssem.at[k - 1], recv_sem=s_rsem.at[k - 1],
                device_id=peer, device_id_type=MESH))
        for cp in remote:
            cp.start()
        for cp in remote:
            cp.wait()
        for cp in local:
            cp.wait()

    return pl.pallas_call(
        body, name="scatter_grads",
        out_shape=(jax.ShapeDtypeStruct(gpack.shape, gpack.dtype),
                   jax.ShapeDtypeStruct((N_DEV,) + small.shape, small.dtype)),
        in_specs=[ANY, ANY], out_specs=(ANY, ANY),
        scratch_shapes=[pltpu.SemaphoreType.DMA((3,)), pltpu.SemaphoreType.DMA((3,)),
                        pltpu.SemaphoreType.DMA((7,)), pltpu.SemaphoreType.DMA((7,)),
                        pltpu.SemaphoreType.DMA((2,))],
    )(gpack, small)


def swap_with_sibling(part):
    def body(p_ref, q_ref, ssem, rsem):
        x, y, c = lax.axis_index("x"), lax.axis_index("y"), lax.axis_index("c")
        cp = pltpu.make_async_remote_copy(src_ref=p_ref, dst_ref=q_ref, send_sem=ssem, recv_sem=rsem,
                                          device_id=(x, y, 1 - c), device_id_type=MESH)
        cp.start()
        cp.wait()

    return pl.pallas_call(
        body, name="swap_with_sibling",
        out_shape=jax.ShapeDtypeStruct(part.shape, part.dtype),
        in_specs=[ANY], out_specs=ANY,
        scratch_shapes=[pltpu.SemaphoreType.DMA, pltpu.SemaphoreType.DMA],
    )(part)


def sum_parts(parts):
    def body(p_ref, o_ref):
        o_ref[...] = ((p_ref[0] + p_ref[1]) + p_ref[2]) + p_ref[3]

    return pl.pallas_call(
        body, name="sum_parts",
        out_shape=jax.ShapeDtypeStruct(parts.shape[1:], F32),
        grid=(PACK_ROWS // PACK_BLOCK,),
        in_specs=[pl.BlockSpec((N_CHIPS, PACK_BLOCK, LANES), lambda i: (0, i, 0))],
        out_specs=pl.BlockSpec((PACK_BLOCK, LANES), lambda i: (i, 0)),
        compiler_params=_params(("parallel",)),
    )(parts)


def _adamw(w, g, m, v):
    m = ADAM_B1 * m + (1.0 - ADAM_B1) * g
    v = ADAM_B2 * v + (1.0 - ADAM_B2) * (g * g)
    m_hat = m / (1.0 - ADAM_B1 ** ADAM_STEP)
    v_hat = v / (1.0 - ADAM_B2 ** ADAM_STEP)
    delta = -ADAM_LR * (m_hat / (jnp.sqrt(v_hat) + ADAM_EPS) + ADAM_WD * w)
    return delta, m, v


def adamw_pack(p_south, p_north, w, m, v):
    def body(a_ref, b_ref, w_ref, m_ref, v_ref, g_out, d_out, m_out, v_out):
        g = a_ref[...] + b_ref[...]
        d, mn, vn = _adamw(w_ref[...], g, m_ref[...], v_ref[...])
        g_out[...] = g
        d_out[...] = d
        m_out[...] = mn
        v_out[...] = vn

    spec = pl.BlockSpec((PACK_BLOCK, LANES), lambda i: (i, 0))
    shp = jax.ShapeDtypeStruct((PACK_ROWS, LANES), F32)
    return pl.pallas_call(
        body, name="adamw_pack", out_shape=(shp,) * 4, grid=(PACK_ROWS // PACK_BLOCK,),
        in_specs=[spec] * 5, out_specs=(spec,) * 4, compiler_params=_params(("parallel",)),
    )(p_south, p_north, w, m, v)


def adamw_small(smalls, w, m, v):
    def body(s_ref, w_ref, m_ref, v_ref, g_out, d_out, m_out, v_out):
        g = s_ref[0]
        for k in range(1, N_DEV):
            g = g + s_ref[k]
        d, mn, vn = _adamw(w_ref[...], g, m_ref[...], v_ref[...])
        g_out[...] = g
        d_out[...] = d
        m_out[...] = mn
        v_out[...] = vn

    shp = jax.ShapeDtypeStruct((SMALL_ROWS, LANES), F32)
    return pl.pallas_call(body, name="adamw_small", out_shape=(shp,) * 4)(smalls, w, m, v)


def rms_prenorm(x, g):
    s = x.shape[0]
    tm = _blk(s, 512)

    def body(x_ref, g_ref, u_ref):
        xv = x_ref[...]
        r = lax.rsqrt(jnp.mean(xv * xv, axis=-1, keepdims=True) + EPS)
        u_ref[...] = (xv * r * g_ref[...]).astype(BF16)

    return pl.pallas_call(
        body, name="rms_prenorm", out_shape=jax.ShapeDtypeStruct(x.shape, BF16), grid=(s // tm,),
        in_specs=[pl.BlockSpec((tm, D_MODEL), lambda i: (i, 0)), _const_spec((1, D_MODEL))],
        out_specs=pl.BlockSpec((tm, D_MODEL), lambda i: (i, 0)), compiler_params=_params(("parallel",)),
    )(x, g)


def matmul_rows(a, w, out_dtype, name):
    s, k = a.shape
    n = w.shape[1]
    tm = _blk(s, 512)

    def body(a_ref, w_ref, o_ref):
        o_ref[...] = _mm(a_ref[...], w_ref[...]).astype(out_dtype)

    return pl.pallas_call(
        body, name=name, out_shape=jax.ShapeDtypeStruct((s, n), out_dtype), grid=(s // tm,),
        in_specs=[pl.BlockSpec((tm, k), lambda i: (i, 0)), _const_spec((k, n))],
        out_specs=pl.BlockSpec((tm, n), lambda i: (i, 0)), compiler_params=_params(("parallel",)),
    )(a, w)


def matmul_tn(a, b, name):
    s, m = a.shape
    n = b.shape[1]
    tk = _blk(s, 512)
    tn = _blk(n, 512)

    def body(a_ref, b_ref, o_ref):
        @pl.when(pl.program_id(1) == 0)
        def _():
            o_ref[...] = jnp.zeros_like(o_ref)

        o_ref[...] += _mm_tn(a_ref[...], b_ref[...])

    return pl.pallas_call(
        body, name=name, out_shape=jax.ShapeDtypeStruct((m, n), F32), grid=(n // tn, s // tk),
        in_specs=[pl.BlockSpec((tk, m), lambda j, i: (i, 0)), pl.BlockSpec((tk, tn), lambda j, i: (i, j))],
        out_specs=pl.BlockSpec((m, tn), lambda j, i: (0, j)),
        compiler_params=_params(("parallel", "arbitrary")),
    )(a, b)


def conv_fwd(xbc, w, b):
    s = xbc.shape[0]
    tm = _blk(s, 256)

    def body(x_ref, t_ref, w_ref, b_ref, pre_ref, act_ref):
        i = pl.program_id(0)
        cur = x_ref[...]
        tail = jnp.where(i > 0, t_ref[...], 0.0)
        wv = w_ref[...]
        acc = cur * wv[3:4, :] + b_ref[...]
        head = cur[0:8, :] * wv[3:4, :] + b_ref[...]
        row8 = _iota((8, CONV_CH), 0)
        for sh in range(1, CONV_WIDTH):
            wk = wv[3 - sh:4 - sh, :]
            acc = acc + pltpu.roll(cur, sh, 0) * wk
            first = jnp.where(row8 < sh, pltpu.roll(tail, sh, 0), pltpu.roll(cur[0:8, :], sh, 0))
            head = head + first * wk
        pre_ref[...] = acc
        act_ref[...] = acc * _sigmoid(acc)
        pre_ref[0:8, :] = head
        act_ref[0:8, :] = head * _sigmoid(head)

    shp = jax.ShapeDtypeStruct(xbc.shape, F32)
    rows = pl.BlockSpec((tm, CONV_CH), lambda i: (i, 0))
    return pl.pallas_call(
        body, name="conv_fwd", out_shape=(shp, shp), grid=(s // tm,),
        in_specs=[rows, pl.BlockSpec((8, CONV_CH), lambda i: (jnp.maximum(i * (tm // 8) - 1, 0), 0)),
                  _const_spec((CONV_WIDTH, CONV_CH)), _const_spec((1, CONV_CH))],
        out_specs=(rows, rows), compiler_params=_params(("parallel",)),
    )(xbc, xbc, w, b)


def conv_bwd(xbc, pre, dact, w):
    s = xbc.shape[0]
    tm = _blk(s, 256)
    nb = s // tm

    def dsilu(p):
        sg = _sigmoid(p)
        return sg * (1.0 + p * (1.0 - sg))

    def body(x_ref, xt_ref, p_ref, pn_ref, d_ref, dn_ref, w_ref, dx_ref, dw_ref, db_ref):
        i = pl.program_id(0)

        @pl.when(i == 0)
        def _():
            dw_ref[...] = jnp.zeros_like(dw_ref)
            db_ref[...] = jnp.zeros_like(db_ref)

        wv = w_ref[...]
        dpre = d_ref[...] * dsilu(p_ref[...])
        dnext = jnp.where(i < nb - 1, dn_ref[...] * dsilu(pn_ref[...]), 0.0)
        cur = x_ref[...]
        tail = jnp.where(i > 0, xt_ref[...], 0.0)
        row8 = _iota((8, CONV_CH), 0)
        dx = dpre * wv[3:4, :]
        last = dpre[tm - 8:tm, :] * wv[3:4, :]
        db_ref[...] += jnp.sum(dpre, axis=0, keepdims=True)
        dws = [jnp.sum(dpre * cur, axis=0, keepdims=True)]
        for sh in range(1, CONV_WIDTH):
            wk = wv[3 - sh:4 - sh, :]
            dx = dx + pltpu.roll(dpre, tm - sh, 0) * wk
            nxt = jnp.where(row8 >= 8 - sh, pltpu.roll(dnext, 8 - sh, 0), pltpu.roll(dpre[tm - 8:tm, :], 8 - sh, 0))
            last = last + nxt * wk
            xs = pltpu.roll(cur, sh, 0)
            first = jnp.where(row8 < sh, pltpu.roll(tail, sh, 0), xs[0:8, :])
            dws.append(jnp.sum(dpre * xs, axis=0, keepdims=True)
                       + jnp.sum(dpre[0:8, :] * (first - xs[0:8, :]), axis=0, keepdims=True))
        dx_ref[...] = dx.astype(BF16)
        dx_ref[tm - 8:tm, :] = last.astype(BF16)
        for sh in range(CONV_WIDTH):
            dw_ref[3 - sh:4 - sh, :] += dws[sh]

    rows = pl.BlockSpec((tm, CONV_CH), lambda i: (i, 0))
    prev8 = pl.BlockSpec((8, CONV_CH), lambda i: (jnp.maximum(i * (tm // 8) - 1, 0), 0))
    next8 = pl.BlockSpec((8, CONV_CH), lambda i: (jnp.minimum((i + 1) * (tm // 8), s // 8 - 1), 0))
    return pl.pallas_call(
        body, name="conv_bwd",
        out_shape=(jax.ShapeDtypeStruct(xbc.shape, BF16), jax.ShapeDtypeStruct((8, CONV_CH), F32),
                   jax.ShapeDtypeStruct((1, CONV_CH), F32)),
        grid=(nb,),
        in_specs=[rows, prev8, rows, next8, rows, next8, _const_spec((CONV_WIDTH, CONV_CH))],
        out_specs=(rows, _const_spec((8, CONV_CH)), _const_spec((1, CONV_CH))),
        compiler_params=_params(("arbitrary",)),
    )(xbc, xbc, pre, pre, dact, dact, w)


def _pair_lanes(mat, j, lane):
    return jnp.where(lane < HEAD_DIM, mat[:, 2 * j:2 * j + 1], mat[:, 2 * j + 1:2 * j + 2])


def _ssd_chunk_prelude(sm, dtb, a_row, lane, sub):
    raw = sm + dtb
    head_lane = lane < N_HEADS
    dt = jnp.where(head_lane, _softplus(raw), 0.0)
    sig = jnp.where(head_lane, _sigmoid(raw), 0.0)
    tri = (lane <= sub).astype(F32)
    acs = _mm_exact(tri, dt * a_row)
    return dt, sig, acs, acs.T


def ssd_fwd(xc, small, dtb_row, a_row, dskip_lane):
    s = xc.shape[0]
    nc = s // CHUNK

    def body(xc_ref, sm_ref, dtb_ref, a_ref, dsk_ref, y_ref, hs_ref, h_scr):
        c = pl.program_id(0)

        @pl.when(c == 0)
        def _():
            h_scr[...] = jnp.zeros_like(h_scr)

        lane = _iota((CHUNK, LANES), 1)
        sub = _iota((CHUNK, LANES), 0)
        causal = lane <= sub
        dt, _, acs, acs_t = _ssd_chunk_prelude(sm_ref[...], dtb_ref[...], a_ref[...], lane, sub)
        last = acs[CHUNK - 1:CHUNK, :]
        e_all = jnp.exp(acs)
        dte = jnp.exp(last - acs)
        cd = jnp.exp(last)
        for g in range(N_GROUPS):
            b_b = xc_ref[:, SSD_WIDTH + D_STATE * g:SSD_WIDTH + D_STATE * (g + 1)].astype(BF16)
            c_b = xc_ref[:, SSD_WIDTH + N_GROUPS * D_STATE + D_STATE * g:
                         SSD_WIDTH + N_GROUPS * D_STATE + D_STATE * (g + 1)].astype(BF16)
            cb = _mm_nt(c_b, b_b)
            for j in range(4 * g, 4 * g + 4):
                x2 = xc_ref[:, LANES * j:LANES * (j + 1)]
                xdt2 = x2 * _pair_lanes(dt, j, lane)
                xdt2_b = xdt2.astype(BF16)
                yd = []
                for e in range(2):
                    h = 2 * j + e
                    seg = acs[:, h:h + 1] - acs_t[h:h + 1, :]
                    lm = jnp.exp(jnp.where(causal, seg, NEG_BIG))
                    yd.append(_mm((cb * lm).astype(BF16), xdt2_b))
                h2 = h_scr[j]
                t2 = _mm_nt(c_b, h2.astype(BF16))
                y2 = (jnp.where(lane < HEAD_DIM, yd[0], yd[1]) + _pair_lanes(e_all, j, lane) * t2
                      + dsk_ref[:, LANES * j:LANES * (j + 1)] * x2)
                y_ref[:, LANES * j:LANES * (j + 1)] = y2
                hs_ref[0, j] = h2
                w2 = (xdt2 * _pair_lanes(dte, j, lane)).astype(BF16)
                s2 = _mm_tn(w2, b_b)
                cdcol = jnp.where(sub < HEAD_DIM, cd[:, 2 * j:2 * j + 1], cd[:, 2 * j + 1:2 * j + 2])
                h_scr[j] = h2 * cdcol + s2

    return pl.pallas_call(
        body, name="ssd_fwd",
        out_shape=(jax.ShapeDtypeStruct((s, SSD_WIDTH), F32),
                   jax.ShapeDtypeStruct((nc, N_PAIRS, LANES, D_STATE), F32)),
        grid=(nc,),
        in_specs=[pl.BlockSpec((CHUNK, CONV_CH), lambda c: (c, 0)), pl.BlockSpec((CHUNK, LANES), lambda c: (c, 0)),
                  _const_spec((1, LANES)), _const_spec((1, LANES)), _const_spec((1, SSD_WIDTH))],
        out_specs=(pl.BlockSpec((CHUNK, SSD_WIDTH), lambda c: (c, 0)),
                   pl.BlockSpec((1, N_PAIRS, LANES, D_STATE), lambda c: (c, 0, 0, 0))),
        scratch_shapes=[pltpu.VMEM((N_PAIRS, LANES, D_STATE), F32)],
        compiler_params=_params(("arbitrary",)),
    )(xc, small, dtb_row, a_row, dskip_lane)


def ssd_bwd(xc, small, states, dy, dtb_row, a_row, dskip_lane):
    s = xc.shape[0]
    nc = s // CHUNK
    rev = lambda c: nc - 1 - c

    def head_rowsums(q, lane):
        r0 = jnp.sum(jnp.where(lane < HEAD_DIM, q, 0.0), axis=1, keepdims=True)
        r1 = jnp.sum(jnp.where(lane < HEAD_DIM, 0.0, q), axis=1, keepdims=True)
        return r0, r1

    def body(xc_ref, sm_ref, hs_ref, dy_ref, dtb_ref, a_ref, dsk_ref,
             dxc_ref, ddt_ref, da_ref, ddtb_ref, ddsk_ref, dh_scr):
        c = pl.program_id(0)

        @pl.when(c == 0)
        def _():
            dh_scr[...] = jnp.zeros_like(dh_scr)
            da_ref[...] = jnp.zeros_like(da_ref)
            ddtb_ref[...] = jnp.zeros_like(ddtb_ref)
            ddsk_ref[...] = jnp.zeros_like(ddsk_ref)

        lane = _iota((CHUNK, LANES), 1)
        sub = _iota((CHUNK, LANES), 0)
        causal = lane <= sub
        is_last = sub == CHUNK - 1
        a_row_v = a_ref[...]
        dt, sig, acs, acs_t = _ssd_chunk_prelude(sm_ref[...], dtb_ref[...], a_row_v, lane, sub)
        last = acs[CHUNK - 1:CHUNK, :]
        e_all = jnp.exp(acs)
        dte = jnp.exp(last - acs)
        cd = jnp.exp(last)
        dacs_c = jnp.zeros((CHUNK, LANES), F32)
        dacs_r = jnp.zeros((LANES, CHUNK), F32)
        ddtx = jnp.zeros((CHUNK, LANES), F32)
        for g in range(N_GROUPS):
            b_off = SSD_WIDTH + D_STATE * g
            c_off = SSD_WIDTH + N_GROUPS * D_STATE + D_STATE * g
            b_b = xc_ref[:, b_off:b_off + D_STATE].astype(BF16)
            c_b = xc_ref[:, c_off:c_off + D_STATE].astype(BF16)
            cb = _mm_nt(c_b, b_b)
            dcb = jnp.zeros((CHUNK, CHUNK), F32)
            db_g = jnp.zeros((CHUNK, D_STATE), F32)
            dc_g = jnp.zeros((CHUNK, D_STATE), F32)
            for j in range(4 * g, 4 * g + 4):
                x2 = xc_ref[:, LANES * j:LANES * (j + 1)]
                dt2 = _pair_lanes(dt, j, lane)
                xdt2 = x2 * dt2
                xdt2_b = xdt2.astype(BF16)
                dy2 = dy_ref[:, LANES * j:LANES * (j + 1)]
                h2 = hs_ref[0, j]
                dh2 = dh_scr[j]
                h2_b = h2.astype(BF16)
                dh2_b = dh2.astype(BF16)
                dxdt2 = jnp.zeros((CHUNK, LANES), F32)
                for e in range(2):
                    h = 2 * j + e
                    in_head = (lane < HEAD_DIM) if e == 0 else (lane >= HEAD_DIM)
                    seg = acs[:, h:h + 1] - acs_t[h:h + 1, :]
                    lm = jnp.exp(jnp.where(causal, seg, NEG_BIG))
                    m_h = cb * lm
                    dyh_b = jnp.where(in_head, dy2, 0.0).astype(BF16)
                    dm_h = _mm_nt(dyh_b, xdt2_b)
                    dxdt2 = dxdt2 + _mm_tn(m_h.astype(BF16), dyh_b)
                    gmat = dm_h * m_h
                    dacs_c = dacs_c + jnp.where(lane == h, jnp.sum(gmat, axis=1, keepdims=True), 0.0)
                    dacs_r = dacs_r - jnp.where(sub == h, jnp.sum(gmat, axis=0, keepdims=True), 0.0)
                    dcb = dcb + dm_h * lm
                t2 = _mm_nt(c_b, h2_b)
                e2 = _pair_lanes(e_all, j, lane)
                r0, r1 = head_rowsums(dy2 * e2 * t2, lane)
                dacs_c = dacs_c + jnp.where(lane == 2 * j, r0, 0.0) + jnp.where(lane == 2 * j + 1, r1, 0.0)
                dt2_b = (dy2 * e2).astype(BF16)
                dc_g = dc_g + _mm(dt2_b, h2_b)
                dh_prev = _mm_tn(dt2_b, c_b)
                dw2 = _mm_nt(b_b, dh2_b)
                dte2 = _pair_lanes(dte, j, lane)
                w2 = xdt2 * dte2
                dxdt2 = dxdt2 + dw2 * dte2
                db_g = db_g + _mm(w2.astype(BF16), dh2_b)
                r0, r1 = head_rowsums(dw2 * w2, lane)
                q3 = dh2 * h2
                s0 = jnp.sum(jnp.where(sub < HEAD_DIM, q3, 0.0), keepdims=True)
                s1 = jnp.sum(jnp.where(sub < HEAD_DIM, 0.0, q3), keepdims=True)
                for e, (r, sq) in enumerate(((r0, s0), (r1, s1))):
                    h = 2 * j + e
                    at_end = jnp.sum(r, keepdims=True) + sq * cd[:, h:h + 1]
                    dacs_c = dacs_c + jnp.where(lane == h, jnp.where(is_last, at_end, 0.0) - r, 0.0)
                cdcol = jnp.where(sub < HEAD_DIM, cd[:, 2 * j:2 * j + 1], cd[:, 2 * j + 1:2 * j + 2])
                dh_scr[j] = dh_prev + dh2 * cdcol
                dsk2 = dsk_ref[:, LANES * j:LANES * (j + 1)]
                dxc_ref[:, LANES * j:LANES * (j + 1)] = dxdt2 * dt2 + dsk2 * dy2
                r0, r1 = head_rowsums(dxdt2 * x2, lane)
                ddtx = ddtx + jnp.where(lane == 2 * j, r0, 0.0) + jnp.where(lane == 2 * j + 1, r1, 0.0)
                ddsk_ref[:, LANES * j:LANES * (j + 1)] += jnp.sum(dy2 * x2, axis=0, keepdims=True)
            dcb_b = dcb.astype(BF16)
            dxc_ref[:, b_off:b_off + D_STATE] = db_g + _mm_tn(dcb_b, c_b)
            dxc_ref[:, c_off:c_off + D_STATE] = dc_g + _mm(dcb_b, b_b)
        dacs = dacs_c + dacs_r.T
        dadt = _mm_exact((lane >= sub).astype(F32), dacs)
        ddt = dadt * a_row_v + ddtx
        ddt_raw = ddt * sig
        ddt_ref[...] = ddt_raw
        da_ref[...] += jnp.sum(dadt * dt, axis=0, keepdims=True)
        ddtb_ref[...] += jnp.sum(ddt_raw, axis=0, keepdims=True)

    return pl.pallas_call(
        body, name="ssd_bwd",
        out_shape=(jax.ShapeDtypeStruct((s, CONV_CH), F32), jax.ShapeDtypeStruct((s, LANES), F32),
                   jax.ShapeDtypeStruct((1, LANES), F32), jax.ShapeDtypeStruct((1, LANES), F32),
                   jax.ShapeDtypeStruct((1, SSD_WIDTH), F32)),
        grid=(nc,),
        in_specs=[pl.BlockSpec((CHUNK, CONV_CH), lambda c: (rev(c), 0)),
                  pl.BlockSpec((CHUNK, LANES), lambda c: (rev(c), 0)),
                  pl.BlockSpec((1, N_PAIRS, LANES, D_STATE), lambda c: (rev(c), 0, 0, 0)),
                  pl.BlockSpec((CHUNK, SSD_WIDTH), lambda c: (rev(c), 0)),
                  _const_spec((1, LANES)), _const_spec((1, LANES)), _const_spec((1, SSD_WIDTH))],
        out_specs=(pl.BlockSpec((CHUNK, CONV_CH), lambda c: (rev(c), 0)),
                   pl.BlockSpec((CHUNK, LANES), lambda c: (rev(c), 0)),
                   _const_spec((1, LANES)), _const_spec((1, LANES)), _const_spec((1, SSD_WIDTH))),
        scratch_shapes=[pltpu.VMEM((N_PAIRS, LANES, D_STATE), F32)],
        compiler_params=_params(("arbitrary",)),
    )(xc, small, states, dy, dtb_row, a_row, dskip_lane)


def forget_cumsum(small, fgb_row):
    s = small.shape[0]
    nb = s // CHUNK

    def body(sm_ref, b_ref, cc_ref, carry):
        i = pl.program_id(0)

        @pl.when(i == 0)
        def _():
            carry[...] = jnp.zeros_like(carry)

        lane = _iota((CHUNK, LANES), 1)
        sub = _iota((CHUNK, LANES), 0)
        in_f = (lane >= N_HEADS) & (lane < 2 * N_HEADS)
        logf = jnp.where(in_f, -_softplus(-(sm_ref[...] + b_ref[...])), 0.0)
        tri = (lane <= sub).astype(F32)
        cum = _mm_exact(tri, logf) + carry[0:1, :]
        cc_ref[...] = cum
        carry[...] = jnp.broadcast_to(cum[CHUNK - 1:CHUNK, :], (8, LANES))

    return pl.pallas_call(
        body, name="forget_cumsum",
        out_shape=jax.ShapeDtypeStruct((s, LANES), F32),
        grid=(nb,),
        in_specs=[pl.BlockSpec((CHUNK, LANES), lambda i: (i, 0)), _const_spec((1, LANES))],
        out_specs=pl.BlockSpec((CHUNK, LANES), lambda i: (i, 0)),
        scratch_shapes=[pltpu.VMEM((8, LANES), F32)],
        compiler_params=_params(("arbitrary",)),
    )(small, fgb_row)


def forget_bwd(dc, small, ddt_raw, fgb_row):
    s = small.shape[0]
    nb = s // CHUNK
    rev = lambda i: nb - 1 - i

    def body(dc_ref, sm_ref, ddt_ref, b_ref, ds_ref, dfb_ref, carry):
        i = pl.program_id(0)

        @pl.when(i == 0)
        def _():
            carry[...] = jnp.zeros_like(carry)
            dfb_ref[...] = jnp.zeros_like(dfb_ref)

        lane = _iota((CHUNK, LANES), 1)
        sub = _iota((CHUNK, LANES), 0)
        rows = dc_ref[...].T
        tri = (lane <= sub).astype(F32)
        rc = _mm_exact(rows, tri) + carry[:, 0:1]
        carry[...] = jnp.broadcast_to(rc[:, 0:1], (LANES, LANES))
        in_f = (lane >= N_HEADS) & (lane < 2 * N_HEADS)
        df = jnp.where(in_f, rc.T * _sigmoid(-(sm_ref[...] + b_ref[...])), 0.0)
        ds_ref[...] = (df + ddt_ref[...]).astype(BF16)
        dfb_ref[...] += jnp.sum(df, axis=0, keepdims=True)

    return pl.pallas_call(
        body, name="forget_bwd",
        out_shape=(jax.ShapeDtypeStruct((s, LANES), BF16), jax.ShapeDtypeStruct((1, LANES), F32)),
        grid=(nb,),
        in_specs=[pl.BlockSpec((CHUNK, LANES), lambda i: (rev(i), 0)),
                  pl.BlockSpec((CHUNK, LANES), lambda i: (rev(i), 0)),
                  pl.BlockSpec((CHUNK, LANES), lambda i: (rev(i), 0)), _const_spec((1, LANES))],
        out_specs=(pl.BlockSpec((CHUNK, LANES), lambda i: (rev(i), 0)), _const_spec((1, LANES))),
        scratch_shapes=[pltpu.VMEM((LANES, LANES), F32)],
        compiler_params=_params(("arbitrary",)),
    )(dc, small, ddt_raw, fgb_row)


ATT_BLOCK = 512
ATT_SCALE = HEAD_DIM ** -0.5
AUG_A = HEAD_DIM
AUG_B = HEAD_DIM + 3


def _split3(c):
    hi = c.astype(BF16).astype(F32)
    r = c - hi
    mid = r.astype(BF16).astype(F32)
    return hi, mid, (r - mid).astype(BF16).astype(F32)


def _aug(lane, first, parts=None, value=1.0):
    if parts is None:
        return jnp.where((lane >= first) & (lane < first + 3), value, 0.0)
    return (jnp.where(lane == first, parts[0], 0.0) + jnp.where(lane == first + 1, parts[1], 0.0)
            + jnp.where(lane == first + 2, parts[2], 0.0))


def _pack_pair(a0, a1, lane):
    return jnp.where(lane < HEAD_DIM, a0, pltpu.roll(a1, HEAD_DIM, 1))


def proj_qkv_heads(u, w_q, w_k, w_v, cum):
    s = u.shape[0]
    tm = _blk(s, 256)

    def body(u_ref, wq_ref, wk_ref, wv_ref, c_ref, qa_ref, ka_ref, va_ref):
        lane = _iota((tm, LANES), 1)
        lo = lane < HEAD_DIM
        uv = u_ref[...]
        qf = _mm(uv, wq_ref[...]) * ATT_SCALE
        kf = _mm(uv, wk_ref[...])
        vf = _mm(uv, wv_ref[...])
        cc = c_ref[...]
        ones_a = _aug(lane, AUG_A)
        ones_b = _aug(lane, AUG_B)
        for h in range(N_HEADS):
            j, e = divmod(h, 2)

            def head(full):
                blk = full[:, LANES * j:LANES * (j + 1)]
                if e == 1:
                    blk = pltpu.roll(blk, HEAD_DIM, 1)
                return jnp.where(lo, blk, 0.0)

            parts = _split3(cc[:, N_HEADS + h:N_HEADS + h + 1])
            qa_ref[h] = (head(qf) + _aug(lane, AUG_A, parts) + ones_b).astype(BF16)
            ka_ref[h] = (head(kf) + ones_a - _aug(lane, AUG_B, parts)).astype(BF16)
            va_ref[h] = (head(vf) + ones_a).astype(BF16)

    shp = jax.ShapeDtypeStruct((N_HEADS, s, LANES), BF16)
    hspec = pl.BlockSpec((N_HEADS, tm, LANES), lambda i: (0, i, 0))
    wspec = _const_spec((D_MODEL, ATT_WIDTH))
    return pl.pallas_call(
        body, name="proj_qkv_heads", out_shape=(shp, shp, shp), grid=(s // tm,),
        in_specs=[pl.BlockSpec((tm, D_MODEL), lambda i: (i, 0)), wspec, wspec, wspec,
                  pl.BlockSpec((tm, LANES), lambda i: (i, 0))],
        out_specs=(hspec, hspec, hspec), compiler_params=_params(("parallel",)),
    )(u, w_q, w_k, w_v, cum)


def attention_fwd(qa, ka, va):
    s = qa.shape[1]
    t = _blk(s, ATT_BLOCK)
    nq = s // t

    def body(qa_ref, ka_ref, va_ref, o_ref, qb_ref, m_scr, acc_scr):
        qi = pl.program_id(1)
        m_scr[...] = jnp.full_like(m_scr, NEG_BIG)
        acc_scr[...] = jnp.zeros_like(acc_scr)

        def kv_step(kb, masked):
            rows = pl.ds(pl.multiple_of(kb * t, t), t)
            for e in range(2):
                sc = _mm_nt(qa_ref[e], ka_ref[e, rows, :])
                if masked:
                    sc = jnp.where(_iota((t, t), 0) >= _iota((t, t), 1), sc, NEG_BIG)
                m_old = m_scr[e][:, 0:1]
                m_new = jnp.maximum(m_old, jnp.max(sc, axis=1, keepdims=True))
                p = jnp.exp(sc - m_new)
                acc_scr[e] = jnp.exp(m_old - m_new) * acc_scr[e] + _mm(p.astype(BF16), va_ref[e, rows, :])
                m_scr[e] = jnp.broadcast_to(m_new, (t, LANES))

        def loop_body(kb, carry):
            kv_step(kb, False)
            return carry

        lax.fori_loop(0, qi, loop_body, 0)
        kv_step(qi, True)

        lane = _iota((t, LANES), 1)
        outs = []
        for e in range(2):
            acc = acc_scr[e]
            l = acc[:, AUG_A:AUG_A + 1]
            outs.append(acc / l)
            lse = m_scr[e][:, 0:1] + jnp.log(l)
            q32 = qa_ref[e].astype(F32)
            c = q32[:, AUG_A:AUG_A + 1] + q32[:, AUG_A + 1:AUG_A + 2] + q32[:, AUG_A + 2:AUG_A + 3]
            qb = jnp.where(lane < HEAD_DIM, q32, 0.0) + _aug(lane, AUG_A, _split3(c - lse)) + _aug(lane, AUG_B)
            qb_ref[e] = qb.astype(BF16)
        o_ref[...] = _pack_pair(outs[0], outs[1], lane)

    return pl.pallas_call(
        body, name="attention_fwd",
        out_shape=(jax.ShapeDtypeStruct((s, ATT_WIDTH), F32), jax.ShapeDtypeStruct((N_HEADS, s, LANES), BF16)),
        grid=(N_PAIRS, nq),
        in_specs=[pl.BlockSpec((2, t, LANES), lambda j, qi: (j, qi, 0)),
                  pl.BlockSpec((2, s, LANES), lambda j, qi: (j, 0, 0)),
                  pl.BlockSpec((2, s, LANES), lambda j, qi: (j, 0, 0))],
        out_specs=(pl.BlockSpec((t, LANES), lambda j, qi: (qi, j)),
                   pl.BlockSpec((2, t, LANES), lambda j, qi: (j, qi, 0))),
        scratch_shapes=[pltpu.VMEM((2, t, LANES), F32), pltpu.VMEM((2, t, LANES), F32)],
        compiler_params=_params(("parallel", "parallel")),
    )(qa, ka, va)


def attention_bwd(qb, ka, va, dob):
    s = qb.shape[1]
    t = _blk(s, ATT_BLOCK)
    nq = s // t

    def body(qb_ref, dob_ref, ka_ref, va_ref, dq_ref, dk_ref, dv_ref, dc_ref, dq_scr, dk_scr, dv_scr):
        j, ki = pl.program_id(0), pl.program_id(1)

        @pl.when((j == 0) & (ki == 0))
        def _():
            dc_ref[...] = jnp.zeros_like(dc_ref)

        @pl.when(ki == 0)
        def _():
            dq_scr[...] = jnp.zeros_like(dq_scr)

        dk_scr[...] = jnp.zeros_like(dk_scr)
        dv_scr[...] = jnp.zeros_like(dv_scr)

        def q_step(qblk, masked):
            rows = pl.ds(pl.multiple_of(qblk * t, t), t)
            for e in range(2):
                q = qb_ref[e, rows, :]
                do = dob_ref[e, rows, :]
                sc = _mm_nt(q, ka_ref[e])
                if masked:
                    sc = jnp.where(_iota((t, t), 0) >= _iota((t, t), 1), sc, NEG_BIG)
                p = jnp.exp(sc)
                ds_b = (p * _mm_nt(do, va_ref[e])).astype(BF16)
                dv_scr[e] += _mm_tn(p.astype(BF16), do)
                dk_scr[e] += _mm_tn(ds_b, q)
                dq_scr[e, rows, :] += _mm(ds_b, ka_ref[e])

        def loop_body(qblk, carry):
            q_step(qblk, False)
            return carry

        q_step(ki, True)
        lax.fori_loop(ki + 1, nq, loop_body, 0)

        lane = _iota((t, LANES), 1)
        dk_ref[...] = _pack_pair(dk_scr[0], dk_scr[1], lane).astype(BF16)
        dv_ref[...] = _pack_pair(dv_scr[0], dv_scr[1], lane).astype(BF16)
        rows = pl.ds(pl.multiple_of(ki * t, t), t)
        dc_ref[rows, :] -= (jnp.where(lane == N_HEADS + 2 * j, dk_scr[0][:, AUG_B:AUG_B + 1], 0.0)
                            + jnp.where(lane == N_HEADS + 2 * j + 1, dk_scr[1][:, AUG_B:AUG_B + 1], 0.0))

        @pl.when(ki == nq - 1)
        def _():
            for blk in range(nq):
                rws = pl.ds(blk * t, t)
                d0 = dq_scr[0, rws, :]
                d1 = dq_scr[1, rws, :]
                dq_ref[rws, :] = (_pack_pair(d0, d1, lane) * ATT_SCALE).astype(BF16)
                dc_ref[rws, :] += (jnp.where(lane == N_HEADS + 2 * j, d0[:, AUG_A:AUG_A + 1], 0.0)
                                   + jnp.where(lane == N_HEADS + 2 * j + 1, d1[:, AUG_A:AUG_A + 1], 0.0))

    full = pl.BlockSpec((2, s, LANES), lambda j, ki: (j, 0, 0))
    blk = pl.BlockSpec((2, t, LANES), lambda j, ki: (j, ki, 0))
    pair = pl.BlockSpec((t, LANES), lambda j, ki: (ki, j))
    wide = jax.ShapeDtypeStruct((s, ATT_WIDTH), BF16)
    return pl.pallas_call(
        body, name="attention_bwd",
        out_shape=(wide, wide, wide, jax.ShapeDtypeStruct((s, LANES), F32)),
        grid=(N_PAIRS, nq),
        in_specs=[full, full, blk, blk],
        out_specs=(pl.BlockSpec((s, LANES), lambda j, ki: (0, j)), pair, pair, _const_spec((s, LANES))),
        scratch_shapes=[pltpu.VMEM((2, s, LANES), F32), pltpu.VMEM((2, t, LANES), F32),
                        pltpu.VMEM((2, t, LANES), F32)],
        compiler_params=_params(("arbitrary", "arbitrary")),
    )(qb, dob, ka, va)


def _dsilu(z, sg):
    return sg * (1.0 + z * (1.0 - sg))


def post_mix(x, y, zs, o, za, p, tgt, ssd_g, att_g_lane, ple_g, fin_g, w_out, w_gate, w_proj):
    s = x.shape[0]
    tm = _blk(s, 128)
    half = SSD_WIDTH // N_GROUPS

    def rms_bwd(dy, yn, r):
        return r * (dy - yn * jnp.mean(dy * yn, axis=-1, keepdims=True))

    def colsum(a):
        return jnp.sum(a, axis=0, keepdims=True)

    def body(x_ref, y_ref, zs_ref, o_ref, za_ref, p_ref, t_ref, sg_ref, ag_ref, pg_ref, fg_ref,
             wo_ref, wg_ref, wp_ref,
             dh1_ref, dy_ref, dzs_ref, dob_ref, dza_ref, ycat_ref, dh1b_ref, n2b_ref, dglb_ref, dppb_ref, pb_ref,
             loss_ref, dfin_ref, dple_ref, dssd_ref, datt_ref):
        @pl.when(pl.program_id(0) == 0)
        def _():
            for r in (loss_ref, dfin_ref, dple_ref, dssd_ref, datt_ref):
                r[...] = jnp.zeros_like(r)

        lane = _iota((tm, LANES), 1)
        lo = lane < HEAD_DIM
        zs = zs_ref[...]
        sz = _sigmoid(zs)
        yv = y_ref[...]
        ys = yv * (zs * sz)
        yn, rg = [], []
        for g in range(N_GROUPS):
            seg = ys[:, half * g:half * (g + 1)]
            r = lax.rsqrt(jnp.mean(seg * seg, axis=-1, keepdims=True) + EPS)
            yn.append(seg * r)
            rg.append(r)
            ycat_ref[:, half * g:half * (g + 1)] = (yn[g] * sg_ref[:, half * g:half * (g + 1)]).astype(BF16)
        za = za_ref[...]
        sza = _sigmoid(za)
        silu_za = za * sza
        on, ra = [], []
        for jb in range(N_PAIRS):
            blk = o_ref[:, LANES * jb:LANES * (jb + 1)]
            sq = blk * blk
            ms0 = jnp.sum(jnp.where(lo, sq, 0.0), axis=1, keepdims=True) * (1.0 / HEAD_DIM)
            ms1 = jnp.sum(jnp.where(lo, 0.0, sq), axis=1, keepdims=True) * (1.0 / HEAD_DIM)
            r = jnp.where(lo, lax.rsqrt(ms0 + EPS), lax.rsqrt(ms1 + EPS))
            on.append(blk * r)
            ra.append(r)
            an = on[jb] * ag_ref[:, LANES * jb:LANES * (jb + 1)]
            ycat_ref[:, SSD_WIDTH + LANES * jb:SSD_WIDTH + LANES * (jb + 1)] = (
                an * silu_za[:, LANES * jb:LANES * (jb + 1)]).astype(BF16)
        h1 = x_ref[...] + _mm(ycat_ref[...], wo_ref[...])
        r2 = lax.rsqrt(jnp.mean(h1 * h1, axis=-1, keepdims=True) + EPS)
        n2h = h1 * r2
        n2_b = (n2h * pg_ref[...]).astype(BF16)
        gate = _sigmoid(_mm(n2_b, wg_ref[...]))
        p_b = p_ref[...].astype(BF16)
        pp = _mm(p_b, wp_ref[...])
        h2 = h1 + gate * pp
        r3 = lax.rsqrt(jnp.mean(h2 * h2, axis=-1, keepdims=True) + EPS)
        n3 = h2 * r3
        diff = n3 * fg_ref[...] - t_ref[...]
        sq = colsum(diff * diff)
        part = sq[:, 0:LANES]
        for jb in range(1, D_MODEL // LANES):
            part = part + sq[:, LANES * jb:LANES * (jb + 1)]
        loss_ref[...] += part * (0.5 / D_MODEL)
        dout = diff * (1.0 / D_MODEL)
        dfin_ref[...] += colsum(dout * n3)
        dh2 = rms_bwd(dout * fg_ref[...], n3, r3)
        dgl = dh2 * pp * gate * (1.0 - gate)
        dgl_b = dgl.astype(BF16)
        dn2 = _mm_nt(dgl_b, wg_ref[...])
        dple_ref[...] += colsum(dn2 * n2h)
        dh1 = dh2 + rms_bwd(dn2 * pg_ref[...], n2h, r2)
        dh1_b = dh1.astype(BF16)
        dycat = _mm_nt(dh1_b, wo_ref[...])
        dh1_ref[...] = dh1
        dh1b_ref[...] = dh1_b
        n2b_ref[...] = n2_b
        dglb_ref[...] = dgl_b
        dppb_ref[...] = (dh2 * gate).astype(BF16)
        pb_ref[...] = p_b
        for g in range(N_GROUPS):
            cols = slice(half * g, half * (g + 1))
            dys_g = dycat[:, cols]
            dssd_ref[:, cols] += colsum(dys_g * yn[g])
            dys = rms_bwd(dys_g * sg_ref[:, cols], yn[g], rg[g])
            dy_ref[:, cols] = dys * (zs[:, cols] * sz[:, cols])
            dzs_ref[:, cols] = (dys * yv[:, cols] * _dsilu(zs[:, cols], sz[:, cols])).astype(BF16)
        for jb in range(N_PAIRS):
            cols = slice(LANES * jb, LANES * (jb + 1))
            dya = dycat[:, SSD_WIDTH + LANES * jb:SSD_WIDTH + LANES * (jb + 1)]
            ag = ag_ref[:, cols]
            dan = dya * silu_za[:, cols]
            dza_ref[:, cols] = (dya * (on[jb] * ag) * _dsilu(za[:, cols], sza[:, cols])).astype(BF16)
            datt_ref[:, cols] += colsum(dan * on[jb])
            don = dan * ag
            q = don * on[jb]
            m0 = jnp.sum(jnp.where(lo, q, 0.0), axis=1, keepdims=True) * (1.0 / HEAD_DIM)
            m1 = jnp.sum(jnp.where(lo, 0.0, q), axis=1, keepdims=True) * (1.0 / HEAD_DIM)
            do2 = ra[jb] * (don - on[jb] * jnp.where(lo, m0, m1))
            prod = do2 * o_ref[:, cols]
            for e in range(2):
                delta = jnp.sum(jnp.where(lo, prod, 0.0) if e == 0 else jnp.where(lo, 0.0, prod),
                                axis=1, keepdims=True)
                base = jnp.where(lo, do2 if e == 0 else pltpu.roll(do2, HEAD_DIM, 1), 0.0)
                dob_ref[2 * jb + e] = (base - _aug(lane, AUG_A, _split3(delta))).astype(BF16)

    def rows(n, dtype=None):
        return pl.BlockSpec((tm, n), lambda i: (i, 0))

    def out(n, dtype):
        return jax.ShapeDtypeStruct((s, n), dtype)

    vec = _const_spec((1, D_MODEL))
    vshape = jax.ShapeDtypeStruct((1, D_MODEL), F32)
    return pl.pallas_call(
        body, name="post_mix",
        out_shape=(out(D_MODEL, F32), out(SSD_WIDTH, F32), out(SSD_WIDTH, BF16),
                   jax.ShapeDtypeStruct((N_HEADS, s, LANES), BF16),
                   out(ATT_WIDTH, BF16), out(D_INNER, BF16), out(D_MODEL, BF16), out(D_MODEL, BF16),
                   out(D_MODEL, BF16), out(D_MODEL, BF16), out(PLE_DIM, BF16),
                   jax.ShapeDtypeStruct((1, LANES), F32), vshape, vshape, vshape, vshape),
        grid=(s // tm,),
        in_specs=[rows(D_MODEL), rows(SSD_WIDTH), rows(SSD_WIDTH), rows(ATT_WIDTH), rows(ATT_WIDTH),
                  rows(PLE_DIM), rows(D_MODEL), vec, vec, vec, vec,
                  _const_spec((D_INNER, D_MODEL)), _const_spec((D_MODEL, D_MODEL)), _const_spec((PLE_DIM, D_MODEL))],
        out_specs=(rows(D_MODEL), rows(SSD_WIDTH), rows(SSD_WIDTH),
                   pl.BlockSpec((N_HEADS, tm, LANES), lambda i: (0, i, 0)), rows(ATT_WIDTH),
                   rows(D_INNER), rows(D_MODEL), rows(D_MODEL), rows(D_MODEL), rows(D_MODEL), rows(PLE_DIM),
                   _const_spec((1, LANES)), vec, vec, vec, vec),
        compiler_params=_params(("arbitrary",)),
    )(x, y, zs, o, za, p, tgt, ssd_g, att_g_lane, ple_g, fin_g, w_out, w_gate, w_proj)


def in_proj_bwd(dsegs, wsegs, x, g, dh1):
    s = x.shape[0]
    tm = _blk(s, 256)
    nseg = len(dsegs)

    def body(*refs):
        d_refs = refs[:nseg]
        w_refs = refs[nseg:2 * nseg]
        x_ref, g_ref, dh1_ref, dx_ref, dg_ref = refs[2 * nseg:]

        @pl.when(pl.program_id(0) == 0)
        def _():
            dg_ref[...] = jnp.zeros_like(dg_ref)

        du = _mm_nt(d_refs[0][...], w_refs[0][...])
        for k in range(1, nseg):
            du = du + _mm_nt(d_refs[k][...], w_refs[k][...])
        xv = x_ref[...]
        r = lax.rsqrt(jnp.mean(xv * xv, axis=-1, keepdims=True) + EPS)
        xh = xv * r
        dg_ref[...] += jnp.sum(du * xh, axis=0, keepdims=True)
        dxh = du * g_ref[...]
        dx_ref[...] = r * (dxh - xh * jnp.mean(dxh * xh, axis=-1, keepdims=True)) + dh1_ref[...]

    rows = lambda n: pl.BlockSpec((tm, n), lambda i: (i, 0))
    return pl.pallas_call(
        body, name="in_proj_bwd",
        out_shape=(jax.ShapeDtypeStruct((s, D_MODEL), F32), jax.ShapeDtypeStruct((1, D_MODEL), F32)),
        grid=(s // tm,),
        in_specs=([rows(d.shape[1]) for d in dsegs] + [_const_spec(w.shape) for w in wsegs]
                  + [rows(D_MODEL), _const_spec((1, D_MODEL)), rows(D_MODEL)]),
        out_specs=(rows(D_MODEL), _const_spec((1, D_MODEL))),
        compiler_params=_params(("arbitrary",)),
    )(*dsegs, *wsegs, x, g, dh1)


def _pack(w_in_s, w_out_s, w_gate_s, w_proj_s, conv_s, dtype):
    parts = [w_in_s.reshape(ROWS_W_IN, LANES), w_out_s.reshape(ROWS_W_OUT, LANES),
             w_gate_s.reshape(ROWS_W_GATE, LANES), w_proj_s.reshape(ROWS_W_PROJ, LANES)]
    used = ROWS_W_IN + ROWS_W_OUT + ROWS_W_GATE + ROWS_W_PROJ
    if conv_s is not None:
        parts.append(jnp.pad(conv_s.reshape(12, LANES), ((0, 4), (0, 0))))
        used += ROWS_CONV
    parts = [a.astype(dtype) for a in parts]
    parts.append(jnp.zeros((PACK_ROWS - used, LANES), dtype))
    return jnp.concatenate(parts, axis=0)


def _unpack(pack):
    o1 = ROWS_W_IN
    o2 = o1 + ROWS_W_OUT
    o3 = o2 + ROWS_W_GATE
    o4 = o3 + ROWS_W_PROJ
    return (pack[:o1].reshape(1, 1024, 1672), pack[o1:o2].reshape(1, 512, 1024),
            pack[o2:o3].reshape(1, 256, 1024), pack[o3:o4].reshape(1, 256, 256),
            pack[o4:o4 + 12].reshape(1, 4, 384))


SMALL_NAMES = ("norm_g", "conv_b", "dt_bias", "a_log", "d_skip", "ssd_norm_g", "fg_bias", "att_norm_g",
               "ple_norm_g", "final_norm_g")
SMALL_SIZES = (1024, 1536, 16, 16, 16, 1024, 16, 64, 1024, 1024)


def _pack_small(vals):
    flat = jnp.concatenate([v.reshape(-1).astype(F32) for v in vals])
    flat = jnp.pad(flat, (0, SMALL_ROWS * LANES - flat.shape[0]))
    return flat.reshape(SMALL_ROWS, LANES)


def _unpack_small(pack, shapes):
    flat = pack.reshape(-1)
    out, off = [], 0
    for n, shp in zip(SMALL_SIZES, shapes):
        out.append(flat[off:off + n].reshape(shp))
        off += n
    return out


def _row128(v16, offset=0):
    return jnp.pad(v16.reshape(1, N_HEADS).astype(F32), ((0, 0), (offset, LANES - N_HEADS - offset)))


def local_step(x, p, tgt, w_in, w_out, w_gate, w_proj, conv_w, norm_g, conv_b, dt_bias, a_log, d_skip,
               ssd_norm_g, fg_bias, att_norm_g, ple_norm_g, final_norm_g):
    c0, c1, c2, c3, c4, c5, c6, c7 = 0, 1024, 2560, 2576, 3600, 4624, 5648, 6672
    w_zs, w_xbc, w_dt = w_in[:, c0:c1], w_in[:, c1:c2], w_in[:, c2:c3]
    w_za, w_q, w_k, w_v, w_f = w_in[:, c3:c4], w_in[:, c4:c5], w_in[:, c5:c6], w_in[:, c6:c7], w_in[:, c7:]
    w_small = jnp.concatenate([w_dt, w_f, jnp.zeros((D_MODEL, LANES - 2 * N_HEADS), BF16)], axis=1)

    dtb_row = _row128(dt_bias)
    a_row = _row128(-jnp.exp(a_log.astype(F32)))
    fgb_row = _row128(fg_bias, N_HEADS)
    dskip_lane = jnp.repeat(d_skip.astype(F32), HEAD_DIM).reshape(1, SSD_WIDTH)
    att_g_lane = jnp.tile(att_norm_g.astype(F32), N_HEADS).reshape(1, ATT_WIDTH)
    row = lambda v: v.reshape(1, -1).astype(F32)

    u = rms_prenorm(x, row(norm_g))
    zs = matmul_rows(u, w_zs, F32, "proj_z_ssd")
    xbc = matmul_rows(u, w_xbc, F32, "proj_xbc")
    za = matmul_rows(u, w_za, F32, "proj_z_att")
    small = matmul_rows(u, w_small, F32, "proj_small")
    cum = forget_cumsum(small, fgb_row)
    qa, ka, va = proj_qkv_heads(u, w_q, w_k, w_v, cum)
    pre, xc = conv_fwd(xbc, conv_w, row(conv_b))
    y, states = ssd_fwd(xc, small, dtb_row, a_row, dskip_lane)
    o, qb = attention_fwd(qa, ka, va)
    (dh1, dy, dzs, dob, dza, ycat, dh1_b, n2_b, dgl_b, dpp_b, p_b,
     loss_l, dfin, dple, dssd_g, datt_lane) = post_mix(
        x, y, zs, o, za, p, tgt, row(ssd_norm_g), att_g_lane, row(ple_norm_g), row(final_norm_g),
        w_out, w_gate, w_proj)
    dq, dk, dv, dc = attention_bwd(qb, ka, va, dob)
    dxc, ddt_raw, da, ddtb, ddsk_lane = ssd_bwd(xc, small, states, dy, dtb_row, a_row, dskip_lane)
    dsmall, dfgb = forget_bwd(dc, small, ddt_raw, fgb_row)
    dxbc, dconv_w8, dconv_b = conv_bwd(xbc, pre, dxc, conv_w)
    dsegs = [dzs, dxbc, dza, dq, dk, dv, dsmall]
    wsegs = [w_zs, w_xbc, w_za, w_q, w_k, w_v, w_small]
    dx, dnorm_g = in_proj_bwd(dsegs, wsegs, x, row(norm_g), dh1)
    dws = [matmul_tn(u, d, "dw_in_%d" % i) for i, d in enumerate(dsegs)]
    dw_in = jnp.concatenate([dws[0], dws[1], dws[6][:, :N_HEADS], dws[2], dws[3], dws[4], dws[5],
                             dws[6][:, N_HEADS:2 * N_HEADS]], axis=1)
    dw_out = matmul_tn(ycat, dh1_b, "dw_out")
    dw_gate = matmul_tn(n2_b, dgl_b, "dw_gate")
    dw_proj = matmul_tn(p_b, dpp_b, "dw_proj")
    small_grads = [
        dnorm_g, dconv_b, ddtb[0, :N_HEADS], (da * a_row)[0, :N_HEADS],
        ddsk_lane.reshape(N_HEADS, HEAD_DIM).sum(axis=1), dssd_g, dfgb[0, N_HEADS:2 * N_HEADS],
        datt_lane.reshape(N_HEADS, HEAD_DIM).sum(axis=0), dple, dfin]
    loss = jnp.sum(loss_l)
    return loss, dx, dw_in, dw_out, dw_gate, dw_proj, dconv_w8[:CONV_WIDTH], small_grads


def kernel(x, p, norm_g, w_in, conv_w, conv_b, dt_bias, a_log, d_skip, ssd_norm_g, fg_bias, att_norm_g, w_out, ple_norm_g, w_ple_gate, w_ple_proj, final_norm_g, loss_target, m_norm_g, m_w_in, m_conv_w, m_conv_b, m_dt_bias, m_a_log, m_d_skip, m_ssd_norm_g, m_fg_bias, m_att_norm_g, m_w_out, m_ple_norm_g, m_w_ple_gate, m_w_ple_proj, m_final_norm_g, v_norm_g, v_w_in, v_conv_w, v_conv_b, v_dt_bias, v_a_log, v_d_skip, v_ssd_norm_g, v_fg_bias, v_att_norm_g, v_w_out, v_ple_norm_g, v_w_ple_gate, v_w_ple_proj, v_final_norm_g):
    wpack = _pack(w_in[0], w_out[0], w_ple_gate[0], w_ple_proj[0], None, BF16)
    cpack = jnp.pad(conv_w[0].reshape(12, LANES), ((0, 4), (0, 0)))
    wall, call = gather_weights(wpack, cpack)
    o1 = ROWS_W_IN
    o2 = o1 + ROWS_W_OUT
    o3 = o2 + ROWS_W_GATE
    o4 = o3 + ROWS_W_PROJ
    w_in_f = jnp.concatenate([wall[j, :o1].reshape(1024, 1672) for j in range(N_CHIPS)], axis=1)
    w_out_f = jnp.concatenate([wall[j, o1:o2].reshape(512, 1024) for j in range(N_CHIPS)], axis=0)
    w_gate_f = jnp.concatenate([wall[j, o2:o3].reshape(256, 1024) for j in range(N_CHIPS)], axis=0)
    w_proj_f = jnp.concatenate([wall[j, o3:o4].reshape(256, 256) for j in range(N_CHIPS)], axis=1)
    conv_w_f = jnp.concatenate([call[j, :12].reshape(4, 384) for j in range(N_CHIPS)], axis=1)

    smalls_w = [norm_g, conv_b, dt_bias, a_log, d_skip, ssd_norm_g, fg_bias, att_norm_g, ple_norm_g, final_norm_g]
    loss_l, dx, dw_in, dw_out, dw_gate, dw_proj, dconv_w, small_grads = local_step(
        x[0], p[0, 0], loss_target[0], w_in_f, w_out_f, w_gate_f, w_proj_f, conv_w_f,
        *[a.reshape(-1) for a in smalls_w])
    loss = lax.psum(loss_l, ("x", "y", "c"))

    gpack = jnp.stack([
        _pack(dw_in[:, 1672 * j:1672 * (j + 1)], dw_out[512 * j:512 * (j + 1)], dw_gate[256 * j:256 * (j + 1)],
              dw_proj[:, 256 * j:256 * (j + 1)], dconv_w[:, 384 * j:384 * (j + 1)], F32)
        for j in range(N_CHIPS)])
    parts, smalls = scatter_grads(gpack, _pack_small(small_grads))
    mine = sum_parts(parts)
    theirs = swap_with_sibling(mine)
    south = lax.axis_index("c") == 0
    p_south = jnp.where(south, mine, theirs)
    p_north = jnp.where(south, theirs, mine)

    big = lambda a, b, c_, d, e: _pack(a[0], b[0], c_[0], d[0], e[0], F32)
    g_pk, d_pk, m_pk, v_pk = adamw_pack(
        p_south, p_north, big(w_in, w_out, w_ple_gate, w_ple_proj, conv_w),
        big(m_w_in, m_w_out, m_w_ple_gate, m_w_ple_proj, m_conv_w),
        big(v_w_in, v_w_out, v_w_ple_gate, v_w_ple_proj, v_conv_w))
    smalls_m = [m_norm_g, m_conv_b, m_dt_bias, m_a_log, m_d_skip, m_ssd_norm_g, m_fg_bias, m_att_norm_g,
                m_ple_norm_g, m_final_norm_g]
    smalls_v = [v_norm_g, v_conv_b, v_dt_bias, v_a_log, v_d_skip, v_ssd_norm_g, v_fg_bias, v_att_norm_g,
                v_ple_norm_g, v_final_norm_g]
    g_sm, d_sm, m_sm, v_sm = adamw_small(smalls, _pack_small(smalls_w), _pack_small(smalls_m), _pack_small(smalls_v))

    shapes = [a.shape for a in smalls_w]
    outs = []
    for pk, sm in ((g_pk, g_sm), (d_pk, d_sm), (m_pk, m_sm), (v_pk, v_sm)):
        b_in, b_out, b_gate, b_proj, b_conv = _unpack(pk)
        s_norm, s_convb, s_dtb, s_alog, s_dsk, s_ssdg, s_fgb, s_attg, s_pleg, s_fin = _unpack_small(sm, shapes)
        outs.extend([s_norm, b_in, b_conv, s_convb, s_dtb, s_alog, s_dsk, s_ssdg, s_fgb, s_attg, b_out, s_pleg,
                     b_gate, b_proj, s_fin])
    return (loss, dx[None], *outs)
```

```python
import functools

import jax
import jax.numpy as jnp
from jax import lax
from jax.experimental import pallas as pl
from jax.experimental.pallas import tpu as pltpu

F32 = jnp.float32
BF16 = jnp.bfloat16

D_MODEL = 1024
SSD_WIDTH = 1024
ATT_WIDTH = 1024
N_HEADS = 16
HEAD_DIM = 64
N_GROUPS = 2
D_STATE = 128
CONV_CH = 1536
CONV_WIDTH = 4
CHUNK = 128
PLE_DIM = 256
D_INNER = 2048
EPS = 1e-6
IN_COLS = 6688
N_CHIPS = 4
N_DEV = 8
LANES = 128
N_PAIRS = 8

ADAM_LR = 0.001
ADAM_B1 = 0.9
ADAM_B2 = 0.999
ADAM_EPS = 1e-08
ADAM_WD = 0.01
ADAM_STEP = 10

ROWS_W_IN = 1024 * 1672 // LANES
ROWS_W_OUT = 512 * 1024 // LANES
ROWS_W_GATE = 256 * 1024 // LANES
ROWS_W_PROJ = 256 * 256 // LANES
ROWS_CONV = 16
PACK_ROWS = 20480
PACK_BLOCK = 2048
SMALL_ROWS = 48

NEG_BIG = -1e30
VMEM_LIMIT = 56 * 1024 * 1024

MESH = pl.DeviceIdType.MESH
ANY = pl.BlockSpec(memory_space=pl.ANY)


def _mm(a, b):
    return jnp.dot(a, b, preferred_element_type=F32)


def _mm_nt(a, b):
    return lax.dot_general(a, b, (((1,), (1,)), ((), ())), preferred_element_type=F32)


def _mm_tn(a, b):
    return lax.dot_general(a, b, (((0,), (0,)), ((), ())), preferred_element_type=F32)


def _mm_exact(a, b):
    return jnp.dot(a, b, preferred_element_type=F32, precision=lax.Precision.HIGHEST)


def _softplus(x):
    return jnp.maximum(x, 0.0) + jnp.log1p(jnp.exp(-jnp.abs(x)))


def _sigmoid(x):
    return jax.nn.sigmoid(x)


def _iota(shape, dim):
    return lax.broadcasted_iota(jnp.int32, shape, dim)


def _params(sem=None):
    return pltpu.CompilerParams(dimension_semantics=sem, vmem_limit_bytes=VMEM_LIMIT)


def _blk(n, pref):
    return min(n, pref)


def _const_spec(shape):
    nd = len(shape)
    return pl.BlockSpec(shape, lambda *_: (0,) * nd)


def _chip_peers():
    x, y, c = lax.axis_index("x"), lax.axis_index("y"), lax.axis_index("c")
    return x, y, c, [(1 - x, y, c), (x, 1 - y, c), (1 - x, 1 - y, c)]


def gather_weights(wpack, cpack):
    def body(w_ref, c_ref, wall_ref, call_ref, ssem, rsem, lsem):
        x, y, _, peers = _chip_peers()
        me = 2 * x + y
        local = [pltpu.make_async_copy(w_ref, wall_ref.at[me], lsem.at[0]),
                 pltpu.make_async_copy(c_ref, call_ref.at[me], lsem.at[1])]
        for cp in local:
            cp.start()
        remote = []
        for k, peer in enumerate(peers):
            remote.append(pltpu.make_async_remote_copy(
                src_ref=w_ref, dst_ref=wall_ref.at[me], send_sem=ssem.at[k], recv_sem=rsem.at[k],
                device_id=peer, device_id_type=MESH))
            remote.append(pltpu.make_async_remote_copy(
                src_ref=c_ref, dst_ref=call_ref.at[me], send_sem=ssem.at[3 + k], recv_sem=rsem.at[3 + k],
                device_id=peer, device_id_type=MESH))
        for cp in remote:
            cp.start()
        for cp in remote:
            cp.wait()
        for cp in local:
            cp.wait()

    return pl.pallas_call(
        body, name="gather_weights",
        out_shape=(jax.ShapeDtypeStruct((N_CHIPS,) + wpack.shape, wpack.dtype),
                   jax.ShapeDtypeStruct((N_CHIPS,) + cpack.shape, cpack.dtype)),
        in_specs=[ANY, ANY], out_specs=(ANY, ANY),
        scratch_shapes=[pltpu.SemaphoreType.DMA((6,)), pltpu.SemaphoreType.DMA((6,)),
                        pltpu.SemaphoreType.DMA((2,))],
    )(wpack, cpack)


def scatter_grads(gpack, small):
    def body(g_ref, s_ref, parts_ref, smalls_ref, ssem, rsem, s_ssem, s_rsem, lsem):
        x, y, c, peers = _chip_peers()
        me = 2 * x + y
        dev = 4 * x + 2 * y + c
        local = [pltpu.make_async_copy(g_ref.at[me], parts_ref.at[me], lsem.at[0]),
                 pltpu.make_async_copy(s_ref, smalls_ref.at[dev], lsem.at[1])]
        for cp in local:
            cp.start()
        remote = []
        for k, peer in enumerate(peers):
            dst_chip = 2 * peer[0] + peer[1]
            remote.append(pltpu.make_async_remote_copy(
                src_ref=g_ref.at[dst_chip], dst_ref=parts_ref.at[me], send_sem=ssem.at[k], recv_sem=rsem.at[k],
                device_id=peer, device_id_type=MESH))
        for k in range(1, N_DEV):
            fx, fy, fc = (k >> 2) & 1, (k >> 1) & 1, k & 1
            peer = ((1 - x) if fx else x, (1 - y) if fy else y, (1 - c) if fc else c)
            remote.append(pltpu.make_async_remote_copy(
                src_ref=s_ref, dst_ref=smalls_ref.at[dev], send_sem=s_ssem.at[k - 1], recv_sem=s_rsem.at[k - 1],
                device_id=peer, device_id_type=MESH))
        for cp in remote:
            cp.start()
        for cp in remote:
            cp.wait()
        for cp in local:
            cp.wait()

    return pl.pallas_call(
        body, name="scatter_grads",
        out_shape=(jax.ShapeDtypeStruct(gpack.shape, gpack.dtype),
                   jax.ShapeDtypeStruct((N_DEV,) + small.shape, small.dtype)),
        in_specs=[ANY, ANY], out_specs=(ANY, ANY),
        scratch_shapes=[pltpu.SemaphoreType.DMA((3,)), pltpu.SemaphoreType.DMA((3,)),
                        pltpu.SemaphoreType.DMA((7,)), pltpu.SemaphoreType.DMA((7,)),
                        pltpu.SemaphoreType.DMA((2,))],
    )(gpack, small)


def swap_with_sibling(part):
    def body(p_ref, q_ref, ssem, rsem):
        x, y, c = lax.axis_index("x"), lax.axis_index("y"), lax.axis_index("c")
        cp = pltpu.make_async_remote_copy(src_ref=p_ref, dst_ref=q_ref, send_sem=ssem, recv_sem=rsem,
                                          device_id=(x, y, 1 - c), device_id_type=MESH)
        cp.start()
        cp.wait()

    return pl.pallas_call(
        body, name="swap_with_sibling",
        out_shape=jax.ShapeDtypeStruct(part.shape, part.dtype),
        in_specs=[ANY], out_specs=ANY,
        scratch_shapes=[pltpu.SemaphoreType.DMA, pltpu.SemaphoreType.DMA],
    )(part)


def sum_parts(parts):
    def body(p_ref, o_ref):
        o_ref[...] = ((p_ref[0] + p_ref[1]) + p_ref[2]) + p_ref[3]

    return pl.pallas_call(
        body, name="sum_parts",
        out_shape=jax.ShapeDtypeStruct(parts.shape[1:], F32),
        grid=(PACK_ROWS // PACK_BLOCK,),
        in_specs=[pl.BlockSpec((N_CHIPS, PACK_BLOCK, LANES), lambda i: (0, i, 0))],
        out_specs=pl.BlockSpec((PACK_BLOCK, LANES), lambda i: (i, 0)),
        compiler_params=_params(("parallel",)),
    )(parts)


def _adamw(w, g, m, v):
    m = ADAM_B1 * m + (1.0 - ADAM_B1) * g
    v = ADAM_B2 * v + (1.0 - ADAM_B2) * (g * g)
    m_hat = m / (1.0 - ADAM_B1 ** ADAM_STEP)
    v_hat = v / (1.0 - ADAM_B2 ** ADAM_STEP)
    delta = -ADAM_LR * (m_hat / (jnp.sqrt(v_hat) + ADAM_EPS) + ADAM_WD * w)
    return delta, m, v


def adamw_pack(p_south, p_north, w, m, v):
    def body(a_ref, b_ref, w_ref, m_ref, v_ref, g_out, d_out, m_out, v_out):
        g = a_ref[...] + b_ref[...]
        d, mn, vn = _adamw(w_ref[...], g, m_ref[...], v_ref[...])
        g_out[...] = g
        d_out[...] = d
        m_out[...] = mn
        v_out[...] = vn

    spec = pl.BlockSpec((PACK_BLOCK, LANES), lambda i: (i, 0))
    shp = jax.ShapeDtypeStruct((PACK_ROWS, LANES), F32)
    return pl.pallas_call(
        body, name="adamw_pack", out_shape=(shp,) * 4, grid=(PACK_ROWS // PACK_BLOCK,),
        in_specs=[spec] * 5, out_specs=(spec,) * 4, compiler_params=_params(("parallel",)),
    )(p_south, p_north, w, m, v)


def adamw_small(smalls, w, m, v):
    def body(s_ref, w_ref, m_ref, v_ref, g_out, d_out, m_out, v_out):
        g = s_ref[0]
        for k in range(1, N_DEV):
            g = g + s_ref[k]
        d, mn, vn = _adamw(w_ref[...], g, m_ref[...], v_ref[...])
        g_out[...] = g
        d_out[...] = d
        m_out[...] = mn
        v_out[...] = vn

    shp = jax.ShapeDtypeStruct((SMALL_ROWS, LANES), F32)
    return pl.pallas_call(body, name="adamw_small", out_shape=(shp,) * 4)(smalls, w, m, v)


def rms_prenorm(x, g):
    s = x.shape[0]
    tm = _blk(s, 512)

    def body(x_ref, g_ref, u_ref):
        xv = x_ref[...]
        r = lax.rsqrt(jnp.mean(xv * xv, axis=-1, keepdims=True) + EPS)
        u_ref[...] = (xv * r * g_ref[...]).astype(BF16)

    return pl.pallas_call(
        body, name="rms_prenorm", out_shape=jax.ShapeDtypeStruct(x.shape, BF16), grid=(s // tm,),
        in_specs=[pl.BlockSpec((tm, D_MODEL), lambda i: (i, 0)), _const_spec((1, D_MODEL))],
        out_specs=pl.BlockSpec((tm, D_MODEL), lambda i: (i, 0)), compiler_params=_params(("parallel",)),
    )(x, g)


def matmul_rows(a, w, out_dtype, name):
    s, k = a.shape
    n = w.shape[1]
    tm = _blk(s, 512)

    def body(a_ref, w_ref, o_ref):
        o_ref[...] = _mm(a_ref[...], w_ref[...]).astype(out_dtype)

    return pl.pallas_call(
        body, name=name, out_shape=jax.ShapeDtypeStruct((s, n), out_dtype), grid=(s // tm,),
        in_specs=[pl.BlockSpec((tm, k), lambda i: (i, 0)), _const_spec((k, n))],
        out_specs=pl.BlockSpec((tm, n), lambda i: (i, 0)), compiler_params=_params(("parallel",)),
    )(a, w)


def matmul_tn(a, b, name):
    s, m = a.shape
    n = b.shape[1]
    tk = _blk(s, 512)
    tn = _blk(n, 512)

    def body(a_ref, b_ref, o_ref):
        @pl.when(pl.program_id(1) == 0)
        def _():
            o_ref[...] = jnp.zeros_like(o_ref)

        o_ref[...] += _mm_tn(a_ref[...], b_ref[...])

    return pl.pallas_call(
        body, name=name, out_shape=jax.ShapeDtypeStruct((m, n), F32), grid=(n // tn, s // tk),
        in_specs=[pl.BlockSpec((tk, m), lambda j, i: (i, 0)), pl.BlockSpec((tk, tn), lambda j, i: (i, j))],
        out_specs=pl.BlockSpec((m, tn), lambda j, i: (0, j)),
        compiler_params=_params(("parallel", "arbitrary")),
    )(a, b)


def conv_fwd(xbc, w, b):
    s = xbc.shape[0]
    tm = _blk(s, 256)

    def body(x_ref, t_ref, w_ref, b_ref, pre_ref, act_ref):
        i = pl.program_id(0)
        cur = x_ref[...]
        tail = jnp.where(i > 0, t_ref[...], 0.0)
        wv = w_ref[...]
        acc = cur * wv[3:4, :] + b_ref[...]
        head = cur[0:8, :] * wv[3:4, :] + b_ref[...]
        row8 = _iota((8, CONV_CH), 0)
        for sh in range(1, CONV_WIDTH):
            wk = wv[3 - sh:4 - sh, :]
            acc = acc + pltpu.roll(cur, sh, 0) * wk
            first = jnp.where(row8 < sh, pltpu.roll(tail, sh, 0), pltpu.roll(cur[0:8, :], sh, 0))
            head = head + first * wk
        pre_ref[...] = acc
        act_ref[...] = acc * _sigmoid(acc)
        pre_ref[0:8, :] = head
        act_ref[0:8, :] = head * _sigmoid(head)

    shp = jax.ShapeDtypeStruct(xbc.shape, F32)
    rows = pl.BlockSpec((tm, CONV_CH), lambda i: (i, 0))
    return pl.pallas_call(
        body, name="conv_fwd", out_shape=(shp, shp), grid=(s // tm,),
        in_specs=[rows, pl.BlockSpec((8, CONV_CH), lambda i: (jnp.maximum(i * (tm // 8) - 1, 0), 0)),
                  _const_spec((CONV_WIDTH, CONV_CH)), _const_spec((1, CONV_CH))],
        out_specs=(rows, rows), compiler_params=_params(("parallel",)),
    )(xbc, xbc, w, b)


def conv_bwd(xbc, pre, dact, w):
    s = xbc.shape[0]
    tm = _blk(s, 256)
    nb = s // tm

    def dsilu(p):
        sg = _sigmoid(p)
        return sg * (1.0 + p * (1.0 - sg))

    def body(x_ref, xt_ref, p_ref, pn_ref, d_ref, dn_ref, w_ref, dx_ref, dw_ref, db_ref):
        i = pl.program_id(0)

        @pl.when(i == 0)
        def _():
            dw_ref[...] = jnp.zeros_like(dw_ref)
            db_ref[...] = jnp.zeros_like(db_ref)

        wv = w_ref[...]
        dpre = d_ref[...] * dsilu(p_ref[...])
        dnext = jnp.where(i < nb - 1, dn_ref[...] * dsilu(pn_ref[...]), 0.0)
        cur = x_ref[...]
        tail = jnp.where(i > 0, xt_ref[...], 0.0)
        row8 = _iota((8, CONV_CH), 0)
        dx = dpre * wv[3:4, :]
        last = dpre[tm - 8:tm, :] * wv[3:4, :]
        db_ref[...] += jnp.sum(dpre, axis=0, keepdims=True)
        dws = [jnp.sum(dpre * cur, axis=0, keepdims=True)]
        for sh in range(1, CONV_WIDTH):
            wk = wv[3 - sh:4 - sh, :]
            dx = dx + pltpu.roll(dpre, tm - sh, 0) * wk
            nxt = jnp.where(row8 >= 8 - sh, pltpu.roll(dnext, 8 - sh, 0), pltpu.roll(dpre[tm - 8:tm, :], 8 - sh, 0))
            last = last + nxt * wk
            xs = pltpu.roll(cur, sh, 0)
            first = jnp.where(row8 < sh, pltpu.roll(tail, sh, 0), xs[0:8, :])
            dws.append(jnp.sum(dpre * xs, axis=0, keepdims=True)
                       + jnp.sum(dpre[0:8, :] * (first - xs[0:8, :]), axis=0, keepdims=True))
        dx_ref[...] = dx.astype(BF16)
        dx_ref[tm - 8:tm, :] = last.astype(BF16)
        for sh in range(CONV_WIDTH):
            dw_ref[3 - sh:4 - sh, :] += dws[sh]

    rows = pl.BlockSpec((tm, CONV_CH), lambda i: (i, 0))
    prev8 = pl.BlockSpec((8, CONV_CH), lambda i: (jnp.maximum(i * (tm // 8) - 1, 0), 0))
    next8 = pl.BlockSpec((8, CONV_CH), lambda i: (jnp.minimum((i + 1) * (tm // 8), s // 8 - 1), 0))
    return pl.pallas_call(
        body, name="conv_bwd",
        out_shape=(jax.ShapeDtypeStruct(xbc.shape, BF16), jax.ShapeDtypeStruct((8, CONV_CH), F32),
                   jax.ShapeDtypeStruct((1, CONV_CH), F32)),
        grid=(nb,),
        in_specs=[rows, prev8, rows, next8, rows, next8, _const_spec((CONV_WIDTH, CONV_CH))],
        out_specs=(rows, _const_spec((8, CONV_CH)), _const_spec((1, CONV_CH))),
        compiler_params=_params(("arbitrary",)),
    )(xbc, xbc, pre, pre, dact, dact, w)


def _pair_lanes(mat, j, lane):
    return jnp.where(lane < HEAD_DIM, mat[:, 2 * j:2 * j + 1], mat[:, 2 * j + 1:2 * j + 2])


def _ssd_chunk_prelude(sm, dtb, a_row, lane, sub):
    raw = sm + dtb
    head_lane = lane < N_HEADS
    dt = jnp.where(head_lane, _softplus(raw), 0.0)
    sig = jnp.where(head_lane, _sigmoid(raw), 0.0)
    tri = (lane <= sub).astype(F32)
    acs = _mm_exact(tri, dt * a_row)
    return dt, sig, acs, acs.T


def ssd_fwd(xc, small, dtb_row, a_row, dskip_lane):
    s = xc.shape[0]
    nc = s // CHUNK

    def body(xc_ref, sm_ref, dtb_ref, a_ref, dsk_ref, y_ref, hs_ref, h_scr):
        c = pl.program_id(0)

        @pl.when(c == 0)
        def _():
            h_scr[...] = jnp.zeros_like(h_scr)

        lane = _iota((CHUNK, LANES), 1)
        sub = _iota((CHUNK, LANES), 0)
        causal = lane <= sub
        dt, _, acs, acs_t = _ssd_chunk_prelude(sm_ref[...], dtb_ref[...], a_ref[...], lane, sub)
        last = acs[CHUNK - 1:CHUNK, :]
        e_all = jnp.exp(acs)
        dte = jnp.exp(last - acs)
        cd = jnp.exp(last)
        for g in range(N_GROUPS):
            b_b = xc_ref[:, SSD_WIDTH + D_STATE * g:SSD_WIDTH + D_STATE * (g + 1)].astype(BF16)
            c_b = xc_ref[:, SSD_WIDTH + N_GROUPS * D_STATE + D_STATE * g:
                         SSD_WIDTH + N_GROUPS * D_STATE + D_STATE * (g + 1)].astype(BF16)
            cb = _mm_nt(c_b, b_b)
            for j in range(4 * g, 4 * g + 4):
                x2 = xc_ref[:, LANES * j:LANES * (j + 1)]
                xdt2 = x2 * _pair_lanes(dt, j, lane)
                xdt2_b = xdt2.astype(BF16)
                yd = []
                for e in range(2):
                    h = 2 * j + e
                    seg = acs[:, h:h + 1] - acs_t[h:h + 1, :]
                    lm = jnp.exp(jnp.where(causal, seg, NEG_BIG))
                    yd.append(_mm((cb * lm).astype(BF16), xdt2_b))
                h2 = h_scr[j]
                t2 = _mm_nt(c_b, h2.astype(BF16))
                y2 = (jnp.where(lane < HEAD_DIM, yd[0], yd[1]) + _pair_lanes(e_all, j, lane) * t2
                      + dsk_ref[:, LANES * j:LANES * (j + 1)] * x2)
                y_ref[:, LANES * j:LANES * (j + 1)] = y2
                hs_ref[0, j] = h2
                w2 = (xdt2 * _pair_lanes(dte, j, lane)).astype(BF16)
                s2 = _mm_tn(w2, b_b)
                cdcol = jnp.where(sub < HEAD_DIM, cd[:, 2 * j:2 * j + 1], cd[:, 2 * j + 1:2 * j + 2])
                h_scr[j] = h2 * cdcol + s2

    return pl.pallas_call(
        body, name="ssd_fwd",
        out_shape=(jax.ShapeDtypeStruct((s, SSD_WIDTH), F32),
                   jax.ShapeDtypeStruct((nc, N_PAIRS, LANES, D_STATE), F32)),
        grid=(nc,),
        in_specs=[pl.BlockSpec((CHUNK, CONV_CH), lambda c: (c, 0)), pl.BlockSpec((CHUNK, LANES), lambda c: (c, 0)),
                  _const_spec((1, LANES)), _const_spec((1, LANES)), _const_spec((1, SSD_WIDTH))],
        out_specs=(pl.BlockSpec((CHUNK, SSD_WIDTH), lambda c: (c, 0)),
                   pl.BlockSpec((1, N_PAIRS, LANES, D_STATE), lambda c: (c, 0, 0, 0))),
        scratch_shapes=[pltpu.VMEM((N_PAIRS, LANES, D_STATE), F32)],
        compiler_params=_params(("arbitrary",)),
    )(xc, small, dtb_row, a_row, dskip_lane)


def ssd_bwd(xc, small, states, dy, dtb_row, a_row, dskip_lane):
    s = xc.shape[0]
    nc = s // CHUNK
    rev = lambda c: nc - 1 - c

    def head_rowsums(q, lane):
        r0 = jnp.sum(jnp.where(lane < HEAD_DIM, q, 0.0), axis=1, keepdims=True)
        r1 = jnp.sum(jnp.where(lane < HEAD_DIM, 0.0, q), axis=1, keepdims=True)
        return r0, r1

    def body(xc_ref, sm_ref, hs_ref, dy_ref, dtb_ref, a_ref, dsk_ref,
             dxc_ref, ddt_ref, da_ref, ddtb_ref, ddsk_ref, dh_scr):
        c = pl.program_id(0)

        @pl.when(c == 0)
        def _():
            dh_scr[...] = jnp.zeros_like(dh_scr)
            da_ref[...] = jnp.zeros_like(da_ref)
            ddtb_ref[...] = jnp.zeros_like(ddtb_ref)
            ddsk_ref[...] = jnp.zeros_like(ddsk_ref)

        lane = _iota((CHUNK, LANES), 1)
        sub = _iota((CHUNK, LANES), 0)
        causal = lane <= sub
        is_last = sub == CHUNK - 1
        a_row_v = a_ref[...]
        dt, sig, acs, acs_t = _ssd_chunk_prelude(sm_ref[...], dtb_ref[...], a_row_v, lane, sub)
        last = acs[CHUNK - 1:CHUNK, :]
        e_all = jnp.exp(acs)
        dte = jnp.exp(last - acs)
        cd = jnp.exp(last)
        dacs_c = jnp.zeros((CHUNK, LANES), F32)
        dacs_r = jnp.zeros((LANES, CHUNK), F32)
        ddtx = jnp.zeros((CHUNK, LANES), F32)
        for g in range(N_GROUPS):
            b_off = SSD_WIDTH + D_STATE * g
            c_off = SSD_WIDTH + N_GROUPS * D_STATE + D_STATE * g
            b_b = xc_ref[:, b_off:b_off + D_STATE].astype(BF16)
            c_b = xc_ref[:, c_off:c_off + D_STATE].astype(BF16)
            cb = _mm_nt(c_b, b_b)
            dcb = jnp.zeros((CHUNK, CHUNK), F32)
            db_g = jnp.zeros((CHUNK, D_STATE), F32)
            dc_g = jnp.zeros((CHUNK, D_STATE), F32)
            for j in range(4 * g, 4 * g + 4):
                x2 = xc_ref[:, LANES * j:LANES * (j + 1)]
                dt2 = _pair_lanes(dt, j, lane)
                xdt2 = x2 * dt2
                xdt2_b = xdt2.astype(BF16)
                dy2 = dy_ref[:, LANES * j:LANES * (j + 1)]
                h2 = hs_ref[0, j]
                dh2 = dh_scr[j]
                h2_b = h2.astype(BF16)
                dh2_b = dh2.astype(BF16)
                dxdt2 = jnp.zeros((CHUNK, LANES), F32)
                for e in range(2):
                    h = 2 * j + e
                    in_head = (lane < HEAD_DIM) if e == 0 else (lane >= HEAD_DIM)
                    seg = acs[:, h:h + 1] - acs_t[h:h + 1, :]
                    lm = jnp.exp(jnp.where(causal, seg, NEG_BIG))
                    m_h = cb * lm
                    dyh_b = jnp.where(in_head, dy2, 0.0).astype(BF16)
                    dm_h = _mm_nt(dyh_b, xdt2_b)
                    dxdt2 = dxdt2 + _mm_tn(m_h.astype(BF16), dyh_b)
                    gmat = dm_h * m_h
                    dacs_c = dacs_c + jnp.where(lane == h, jnp.sum(gmat, axis=1, keepdims=True), 0.0)
                    dacs_r = dacs_r - jnp.where(sub == h, jnp.sum(gmat, axis=0, keepdims=True), 0.0)
                    dcb = dcb + dm_h * lm
                t2 = _mm_nt(c_b, h2_b)
                e2 = _pair_lanes(e_all, j, lane)
                r0, r1 = head_rowsums(dy2 * e2 * t2, lane)
                dacs_c = dacs_c + jnp.where(lane == 2 * j, r0, 0.0) + jnp.where(lane == 2 * j + 1, r1, 0.0)
                dt2_b = (dy2 * e2).astype(BF16)
                dc_g = dc_g + _mm(dt2_b, h2_b)
                dh_prev = _mm_tn(dt2_b, c_b)
                dw2 = _mm_nt(b_b, dh2_b)
                dte2 = _pair_lanes(dte, j, lane)
                w2 = xdt2 * dte2
                dxdt2 = dxdt2 + dw2 * dte2
                db_g = db_g + _mm(w2.astype(BF16), dh2_b)
                r0, r1 = head_rowsums(dw2 * w2, lane)
                q3 = dh2 * h2
                s0 = jnp.sum(jnp.where(sub < HEAD_DIM, q3, 0.0), keepdims=True)
                s1 = jnp.sum(jnp.where(sub < HEAD_DIM, 0.0, q3), keepdims=True)
                for e, (r, sq) in enumerate(((r0, s0), (r1, s1))):
                    h = 2 * j + e
                    at_end = jnp.sum(r, keepdims=True) + sq * cd[:, h:h + 1]
                    dacs_c = dacs_c + jnp.where(lane == h, jnp.where(is_last, at_end, 0.0) - r, 0.0)
                cdcol = jnp.where(sub < HEAD_DIM, cd[:, 2 * j:2 * j + 1], cd[:, 2 * j + 1:2 * j + 2])
                dh_scr[j] = dh_prev + dh2 * cdcol
                dsk2 = dsk_ref[:, LANES * j:LANES * (j + 1)]
                dxc_ref[:, LANES * j:LANES * (j + 1)] = dxdt2 * dt2 + dsk2 * dy2
                r0, r1 = head_rowsums(dxdt2 * x2, lane)
                ddtx = ddtx + jnp.where(lane == 2 * j, r0, 0.0) + jnp.where(lane == 2 * j + 1, r1, 0.0)
                ddsk_ref[:, LANES * j:LANES * (j + 1)] += jnp.sum(dy2 * x2, axis=0, keepdims=True)
            dcb_b = dcb.astype(BF16)
            dxc_ref[:, b_off:b_off + D_STATE] = db_g + _mm_tn(dcb_b, c_b)
            dxc_ref[:, c_off:c_off + D_STATE] = dc_g + _mm(dcb_b, b_b)
        dacs = dacs_c + dacs_r.T
        dadt = _mm_exact((lane >= sub).astype(F32), dacs)
        ddt = dadt * a_row_v + ddtx
        ddt_raw = ddt * sig
        ddt_ref[...] = ddt_raw
        da_ref[...] += jnp.sum(dadt * dt, axis=0, keepdims=True)
        ddtb_ref[...] += jnp.sum(ddt_raw, axis=0, keepdims=True)

    return pl.pallas_call(
        body, name="ssd_bwd",
        out_shape=(jax.ShapeDtypeStruct((s, CONV_CH), F32), jax.ShapeDtypeStruct((s, LANES), F32),
                   jax.ShapeDtypeStruct((1, LANES), F32), jax.ShapeDtypeStruct((1, LANES), F32),
                   jax.ShapeDtypeStruct((1, SSD_WIDTH), F32)),
        grid=(nc,),
        in_specs=[pl.BlockSpec((CHUNK, CONV_CH), lambda c: (rev(c), 0)),
                  pl.BlockSpec((CHUNK, LANES), lambda c: (rev(c), 0)),
                  pl.BlockSpec((1, N_PAIRS, LANES, D_STATE), lambda c: (rev(c), 0, 0, 0)),
                  pl.BlockSpec((CHUNK, SSD_WIDTH), lambda c: (rev(c), 0)),
                  _const_spec((1, LANES)), _const_spec((1, LANES)), _const_spec((1, SSD_WIDTH))],
        out_specs=(pl.BlockSpec((CHUNK, CONV_CH), lambda c: (rev(c), 0)),
                   pl.BlockSpec((CHUNK, LANES), lambda c: (rev(c), 0)),
                   _const_spec((1, LANES)), _const_spec((1, LANES)), _const_spec((1, SSD_WIDTH))),
        scratch_shapes=[pltpu.VMEM((N_PAIRS, LANES, D_STATE), F32)],
        compiler_params=_params(("arbitrary",)),
    )(xc, small, states, dy, dtb_row, a_row, dskip_lane)


def forget_cumsum(small, fgb_row):
    s = small.shape[0]
    nb = s // CHUNK

    def body(sm_ref, b_ref, cc_ref, carry):
        i = pl.program_id(0)

        @pl.when(i == 0)
        def _():
            carry[...] = jnp.zeros_like(carry)

        lane = _iota((CHUNK, LANES), 1)
        sub = _iota((CHUNK, LANES), 0)
        in_f = (lane >= N_HEADS) & (lane < 2 * N_HEADS)
        logf = jnp.where(in_f, -_softplus(-(sm_ref[...] + b_ref[...])), 0.0)
        tri = (lane <= sub).astype(F32)
        cum = _mm_exact(tri, logf) + carry[0:1, :]
        cc_ref[...] = cum
        carry[...] = jnp.broadcast_to(cum[CHUNK - 1:CHUNK, :], (8, LANES))

    return pl.pallas_call(
        body, name="forget_cumsum",
        out_shape=jax.ShapeDtypeStruct((s, LANES), F32),
        grid=(nb,),
        in_specs=[pl.BlockSpec((CHUNK, LANES), lambda i: (i, 0)), _const_spec((1, LANES))],
        out_specs=pl.BlockSpec((CHUNK, LANES), lambda i: (i, 0)),
        scratch_shapes=[pltpu.VMEM((8, LANES), F32)],
        compiler_params=_params(("arbitrary",)),
    )(small, fgb_row)


def forget_bwd(dc, small, ddt_raw, fgb_row):
    s = small.shape[0]
    nb = s // CHUNK
    rev = lambda i: nb - 1 - i

    def body(dc_ref, sm_ref, ddt_ref, b_ref, ds_ref, dfb_ref, carry):
        i = pl.program_id(0)

        @pl.when(i == 0)
        def _():
            carry[...] = jnp.zeros_like(carry)
            dfb_ref[...] = jnp.zeros_like(dfb_ref)

        lane = _iota((CHUNK, LANES), 1)
        sub = _iota((CHUNK, LANES), 0)
        rows = dc_ref[...].T
        tri = (lane <= sub).astype(F32)
        rc = _mm_exact(rows, tri) + carry[:, 0:1]
        carry[...] = jnp.broadcast_to(rc[:, 0:1], (LANES, LANES))
        in_f = (lane >= N_HEADS) & (lane < 2 * N_HEADS)
        df = jnp.where(in_f, rc.T * _sigmoid(-(sm_ref[...] + b_ref[...])), 0.0)
        ds_ref[...] = (df + ddt_ref[...]).astype(BF16)
        dfb_ref[...] += jnp.sum(df, axis=0, keepdims=True)

    return pl.pallas_call(
        body, name="forget_bwd",
        out_shape=(jax.ShapeDtypeStruct((s, LANES), BF16), jax.ShapeDtypeStruct((1, LANES), F32)),
        grid=(nb,),
        in_specs=[pl.BlockSpec((CHUNK, LANES), lambda i: (rev(i), 0)),
                  pl.BlockSpec((CHUNK, LANES), lambda i: (rev(i), 0)),
                  pl.BlockSpec((CHUNK, LANES), lambda i: (rev(i), 0)), _const_spec((1, LANES))],
        out_specs=(pl.BlockSpec((CHUNK, LANES), lambda i: (rev(i), 0)), _const_spec((1, LANES))),
        scratch_shapes=[pltpu.VMEM((LANES, LANES), F32)],
        compiler_params=_params(("arbitrary",)),
    )(dc, small, ddt_raw, fgb_row)


ATT_BLOCK = 512
ATT_SCALE = HEAD_DIM ** -0.5
AUG_A = HEAD_DIM
AUG_B = HEAD_DIM + 3


def _split3(c):
    hi = c.astype(BF16).astype(F32)
    r = c - hi
    mid = r.astype(BF16).astype(F32)
    return hi, mid, (r - mid).astype(BF16).astype(F32)


def _aug(lane, first, parts=None, value=1.0):
    if parts is None:
        return jnp.where((lane >= first) & (lane < first + 3), value, 0.0)
    return (jnp.where(lane == first, parts[0], 0.0) + jnp.where(lane == first + 1, parts[1], 0.0)
            + jnp.where(lane == first + 2, parts[2], 0.0))


def _pack_pair(a0, a1, lane):
    return jnp.where(lane < HEAD_DIM, a0, pltpu.roll(a1, HEAD_DIM, 1))


def proj_qkv_heads(u, w_q, w_k, w_v, cum):
    s = u.shape[0]
    tm = _blk(s, 256)

    def body(u_ref, wq_ref, wk_ref, wv_ref, c_ref, qa_ref, ka_ref, va_ref):
        lane = _iota((tm, LANES), 1)
        lo = lane < HEAD_DIM
        uv = u_ref[...]
        qf = _mm(uv, wq_ref[...]) * ATT_SCALE
        kf = _mm(uv, wk_ref[...])
        vf = _mm(uv, wv_ref[...])
        cc = c_ref[...]
        ones_a = _aug(lane, AUG_A)
        ones_b = _aug(lane, AUG_B)
        for h in range(N_HEADS):
            j, e = divmod(h, 2)

            def head(full):
                blk = full[:, LANES * j:LANES * (j + 1)]
                if e == 1:
                    blk = pltpu.roll(blk, HEAD_DIM, 1)
                return jnp.where(lo, blk, 0.0)

            parts = _split3(cc[:, N_HEADS + h:N_HEADS + h + 1])
            qa_ref[h] = (head(qf) + _aug(lane, AUG_A, parts) + ones_b).astype(BF16)
            ka_ref[h] = (head(kf) + ones_a - _aug(lane, AUG_B, parts)).astype(BF16)
            va_ref[h] = (head(vf) + ones_a).astype(BF16)

    shp = jax.ShapeDtypeStruct((N_HEADS, s, LANES), BF16)
    hspec = pl.BlockSpec((N_HEADS, tm, LANES), lambda i: (0, i, 0))
    wspec = _const_spec((D_MODEL, ATT_WIDTH))
    return pl.pallas_call(
        body, name="proj_qkv_heads", out_shape=(shp, shp, shp), grid=(s // tm,),
        in_specs=[pl.BlockSpec((tm, D_MODEL), lambda i: (i, 0)), wspec, wspec, wspec,
                  pl.BlockSpec((tm, LANES), lambda i: (i, 0))],
        out_specs=(hspec, hspec, hspec), compiler_params=_params(("parallel",)),
    )(u, w_q, w_k, w_v, cum)


def attention_fwd(qa, ka, va):
    s = qa.shape[1]
    t = _blk(s, ATT_BLOCK)
    nq = s // t

    def body(qa_ref, ka_ref, va_ref, o_ref, qb_ref, m_scr, acc_scr, alpha_scr, p_scr, s_scr):
        qi = pl.program_id(1)
        m_scr[...] = jnp.full_like(m_scr, NEG_BIG)
        acc_scr[...] = jnp.zeros_like(acc_scr)

        def kv_rows(kb):
            return pl.ds(pl.multiple_of(kb * t, t), t)

        def softmax_block(kb, masked):
            for e in range(2):
                sc = _mm_nt(qa_ref[e], ka_ref[e, kv_rows(kb), :])
                if masked:
                    sc = jnp.where(_iota((t, t), 0) >= _iota((t, t), 1), sc, NEG_BIG)
                s_scr[e] = sc
                cmax = s_scr[e, :, 0:LANES]
                for c in range(1, t // LANES):
                    cmax = jnp.maximum(cmax, s_scr[e, :, LANES * c:LANES * (c + 1)])
                m_old = m_scr[e]
                m_new = jnp.maximum(m_old, jnp.max(cmax, axis=1, keepdims=True))
                alpha_scr[e] = jnp.exp(m_old - m_new)
                m_scr[e] = m_new
                for c in range(t // LANES):
                    cols = slice(LANES * c, LANES * (c + 1))
                    p_scr[e, :, cols] = jnp.exp(s_scr[e, :, cols] - m_new).astype(BF16)

        def accumulate(kb):
            for e in range(2):
                acc_scr[e] = alpha_scr[e] * acc_scr[e] + _mm(p_scr[e], va_ref[e, kv_rows(kb), :])

        def loop_body(kb, carry):
            accumulate(kb - 1)
            softmax_block(kb, False)
            return carry

        @pl.when(qi > 0)
        def _():
            softmax_block(0, False)

        lax.fori_loop(1, qi, loop_body, 0)

        @pl.when(qi > 0)
        def _():
            accumulate(qi - 1)
            softmax_block(qi, True)

        @pl.when(qi == 0)
        def _():
            softmax_block(0, True)

        accumulate(qi)

        lane = _iota((t, LANES), 1)
        outs = []
        for e in range(2):
            acc = acc_scr[e]
            l = acc[:, AUG_A:AUG_A + 1]
            outs.append(acc / l)
            lse = m_scr[e][:, 0:1] + jnp.log(l)
            q32 = qa_ref[e].astype(F32)
            c = q32[:, AUG_A:AUG_A + 1] + q32[:, AUG_A + 1:AUG_A + 2] + q32[:, AUG_A + 2:AUG_A + 3]
            qb = jnp.where(lane < HEAD_DIM, q32, 0.0) + _aug(lane, AUG_A, _split3(c - lse)) + _aug(lane, AUG_B)
            qb_ref[e] = qb.astype(BF16)
        o_ref[...] = _pack_pair(outs[0], outs[1], lane)

    return pl.pallas_call(
        body, name="attention_fwd",
        out_shape=(jax.ShapeDtypeStruct((s, ATT_WIDTH), F32), jax.ShapeDtypeStruct((N_HEADS, s, LANES), BF16)),
        grid=(N_PAIRS, nq),
        in_specs=[pl.BlockSpec((2, t, LANES), lambda j, qi: (j, qi, 0)),
                  pl.BlockSpec((2, s, LANES), lambda j, qi: (j, 0, 0)),
                  pl.BlockSpec((2, s, LANES), lambda j, qi: (j, 0, 0))],
        out_specs=(pl.BlockSpec((t, LANES), lambda j, qi: (qi, j)),
                   pl.BlockSpec((2, t, LANES), lambda j, qi: (j, qi, 0))),
        scratch_shapes=[pltpu.VMEM((2, t, LANES), F32), pltpu.VMEM((2, t, LANES), F32),
                        pltpu.VMEM((2, t, LANES), F32), pltpu.VMEM((2, t, t), BF16), pltpu.VMEM((2, t, t), F32)],
        compiler_params=_params(("parallel", "parallel")),
    )(qa, ka, va)


def attention_bwd(qb, ka, va, dob):
    s = qb.shape[1]
    t = _blk(s, ATT_BLOCK)
    nq = s // t

    def body(qb_ref, dob_ref, ka_ref, va_ref, dq_ref, dk_ref, dv_ref, dc_ref, dq_scr, dk_scr, dv_scr):
        j, ki = pl.program_id(0), pl.program_id(1)

        @pl.when((j == 0) & (ki == 0))
        def _():
            dc_ref[...] = jnp.zeros_like(dc_ref)

        @pl.when(ki == 0)
        def _():
            dq_scr[...] = jnp.zeros_like(dq_scr)

        dk_scr[...] = jnp.zeros_like(dk_scr)
        dv_scr[...] = jnp.zeros_like(dv_scr)

        def q_step(qblk, masked):
            rows = pl.ds(pl.multiple_of(qblk * t, t), t)
            for e in range(2):
                q = qb_ref[e, rows, :]
                do = dob_ref[e, rows, :]
                sc = _mm_nt(q, ka_ref[e])
                if masked:
                    sc = jnp.where(_iota((t, t), 0) >= _iota((t, t), 1), sc, NEG_BIG)
                p = jnp.exp(sc)
                ds_b = (p * _mm_nt(do, va_ref[e])).astype(BF16)
                dv_scr[e] += _mm_tn(p.astype(BF16), do)
                dk_scr[e] += _mm_tn(ds_b, q)
                dq_scr[e, rows, :] += _mm(ds_b, ka_ref[e])

        def loop_body(qblk, carry):
            q_step(qblk, False)
            return carry

        q_step(ki, True)
        lax.fori_loop(ki + 1, nq, loop_body, 0)

        lane = _iota((t, LANES), 1)
        dk_ref[...] = _pack_pair(dk_scr[0], dk_scr[1], lane).astype(BF16)
        dv_ref[...] = _pack_pair(dv_scr[0], dv_scr[1], lane).astype(BF16)
        rows = pl.ds(pl.multiple_of(ki * t, t), t)
        dc_ref[rows, :] -= (jnp.where(lane == N_HEADS + 2 * j, dk_scr[0][:, AUG_B:AUG_B + 1], 0.0)
                            + jnp.where(lane == N_HEADS + 2 * j + 1, dk_scr[1][:, AUG_B:AUG_B + 1], 0.0))

        @pl.when(ki == nq - 1)
        def _():
            for blk in range(nq):
                rws = pl.ds(blk * t, t)
                d0 = dq_scr[0, rws, :]
                d1 = dq_scr[1, rws, :]
                dq_ref[rws, :] = (_pack_pair(d0, d1, lane) * ATT_SCALE).astype(BF16)
                dc_ref[rws, :] += (jnp.where(lane == N_HEADS + 2 * j, d0[:, AUG_A:AUG_A + 1], 0.0)
                                   + jnp.where(lane == N_HEADS + 2 * j + 1, d1[:, AUG_A:AUG_A + 1], 0.0))

    full = pl.BlockSpec((2, s, LANES), lambda j, ki: (j, 0, 0))
    blk = pl.BlockSpec((2, t, LANES), lambda j, ki: (j, ki, 0))
    pair = pl.BlockSpec((t, LANES), lambda j, ki: (ki, j))
    wide = jax.ShapeDtypeStruct((s, ATT_WIDTH), BF16)
    return pl.pallas_call(
        body, name="attention_bwd",
        out_shape=(wide, wide, wide, jax.ShapeDtypeStruct((s, LANES), F32)),
        grid=(N_PAIRS, nq),
        in_specs=[full, full, blk, blk],
        out_specs=(pl.BlockSpec((s, LANES), lambda j, ki: (0, j)), pair, pair, _const_spec((s, LANES))),
        scratch_shapes=[pltpu.VMEM((2, s, LANES), F32), pltpu.VMEM((2, t, LANES), F32),
                        pltpu.VMEM((2, t, LANES), F32)],
        compiler_params=_params(("arbitrary", "arbitrary")),
    )(qb, dob, ka, va)


def _dsilu(z, sg):
    return sg * (1.0 + z * (1.0 - sg))


def post_mix(x, y, zs, o, za, p, tgt, ssd_g, att_g_lane, ple_g, fin_g, w_out, w_gate, w_proj):
    s = x.shape[0]
    tm = _blk(s, 128)
    half = SSD_WIDTH // N_GROUPS

    def rms_bwd(dy, yn, r):
        return r * (dy - yn * jnp.mean(dy * yn, axis=-1, keepdims=True))

    def colsum(a):
        return jnp.sum(a, axis=0, keepdims=True)

    def body(x_ref, y_ref, zs_ref, o_ref, za_ref, p_ref, t_ref, sg_ref, ag_ref, pg_ref, fg_ref,
             wo_ref, wg_ref, wp_ref,
             dh1_ref, dy_ref, dzs_ref, dob_ref, dza_ref, ycat_ref, dh1b_ref, n2b_ref, dglb_ref, dppb_ref, pb_ref,
             loss_ref, dfin_ref, dple_ref, dssd_ref, datt_ref):
        @pl.when(pl.program_id(0) == 0)
        def _():
            for r in (loss_ref, dfin_ref, dple_ref, dssd_ref, datt_ref):
                r[...] = jnp.zeros_like(r)

        lane = _iota((tm, LANES), 1)
        lo = lane < HEAD_DIM
        zs = zs_ref[...]
        sz = _sigmoid(zs)
        yv = y_ref[...]
        ys = yv * (zs * sz)
        yn, rg = [], []
        for g in range(N_GROUPS):
            seg = ys[:, half * g:half * (g + 1)]
            r = lax.rsqrt(jnp.mean(seg * seg, axis=-1, keepdims=True) + EPS)
            yn.append(seg * r)
            rg.append(r)
            ycat_ref[:, half * g:half * (g + 1)] = (yn[g] * sg_ref[:, half * g:half * (g + 1)]).astype(BF16)
        za = za_ref[...]
        sza = _sigmoid(za)
        silu_za = za * sza
        on, ra = [], []
        for jb in range(N_PAIRS):
            blk = o_ref[:, LANES * jb:LANES * (jb + 1)]
            sq = blk * blk
            ms0 = jnp.sum(jnp.where(lo, sq, 0.0), axis=1, keepdims=True) * (1.0 / HEAD_DIM)
            ms1 = jnp.sum(jnp.where(lo, 0.0, sq), axis=1, keepdims=True) * (1.0 / HEAD_DIM)
            r = jnp.where(lo, lax.rsqrt(ms0 + EPS), lax.rsqrt(ms1 + EPS))
            on.append(blk * r)
            ra.append(r)
            an = on[jb] * ag_ref[:, LANES * jb:LANES * (jb + 1)]
            ycat_ref[:, SSD_WIDTH + LANES * jb:SSD_WIDTH + LANES * (jb + 1)] = (
                an * silu_za[:, LANES * jb:LANES * (jb + 1)]).astype(BF16)
        h1 = x_ref[...] + _mm(ycat_ref[...], wo_ref[...])
        r2 = lax.rsqrt(jnp.mean(h1 * h1, axis=-1, keepdims=True) + EPS)
        n2h = h1 * r2
        n2_b = (n2h * pg_ref[...]).astype(BF16)
        gate = _sigmoid(_mm(n2_b, wg_ref[...]))
        p_b = p_ref[...].astype(BF16)
        pp = _mm(p_b, wp_ref[...])
        h2 = h1 + gate * pp
        r3 = lax.rsqrt(jnp.mean(h2 * h2, axis=-1, keepdims=True) + EPS)
        n3 = h2 * r3
        diff = n3 * fg_ref[...] - t_ref[...]
        sq = colsum(diff * diff)
        part = sq[:, 0:LANES]
        for jb in range(1, D_MODEL // LANES):
            part = part + sq[:, LANES * jb:LANES * (jb + 1)]
        loss_ref[...] += part * (0.5 / D_MODEL)
        dout = diff * (1.0 / D_MODEL)
        dfin_ref[...] += colsum(dout * n3)
        dh2 = rms_bwd(dout * fg_ref[...], n3, r3)
        dgl = dh2 * pp * gate * (1.0 - gate)
        dgl_b = dgl.astype(BF16)
        dn2 = _mm_nt(dgl_b, wg_ref[...])
        dple_ref[...] += colsum(dn2 * n2h)
        dh1 = dh2 + rms_bwd(dn2 * pg_ref[...], n2h, r2)
        dh1_b = dh1.astype(BF16)
        dycat = _mm_nt(dh1_b, wo_ref[...])
        dh1_ref[...] = dh1
        dh1b_ref[...] = dh1_b
        n2b_ref[...] = n2_b
        dglb_ref[...] = dgl_b
        dppb_ref[...] = (dh2 * gate).astype(BF16)
        pb_ref[...] = p_b
        for g in range(N_GROUPS):
            cols = slice(half * g, half * (g + 1))
            dys_g = dycat[:, cols]
            dssd_ref[:, cols] += colsum(dys_g * yn[g])
            dys = rms_bwd(dys_g * sg_ref[:, cols], yn[g], rg[g])
            dy_ref[:, cols] = dys * (zs[:, cols] * sz[:, cols])
            dzs_ref[:, cols] = (dys * yv[:, cols] * _dsilu(zs[:, cols], sz[:, cols])).astype(BF16)
        for jb in range(N_PAIRS):
            cols = slice(LANES * jb, LANES * (jb + 1))
            dya = dycat[:, SSD_WIDTH + LANES * jb:SSD_WIDTH + LANES * (jb + 1)]
            ag = ag_ref[:, cols]
            dan = dya * silu_za[:, cols]
            dza_ref[:, cols] = (dya * (on[jb] * ag) * _dsilu(za[:, cols], sza[:, cols])).astype(BF16)
            datt_ref[:, cols] += colsum(dan * on[jb])
            don = dan * ag
            q = don * on[jb]
            m0 = jnp.sum(jnp.where(lo, q, 0.0), axis=1, keepdims=True) * (1.0 / HEAD_DIM)
            m1 = jnp.sum(jnp.where(lo, 0.0, q), axis=1, keepdims=True) * (1.0 / HEAD_DIM)
            do2 = ra[jb] * (don - on[jb] * jnp.where(lo, m0, m1))
            prod = do2 * o_ref[:, cols]
            for e in range(2):
                delta = jnp.sum(jnp.where(lo, prod, 0.0) if e == 0 else jnp.where(lo, 0.0, prod),
                                axis=1, keepdims=True)
                base = jnp.where(lo, do2 if e == 0 else pltpu.roll(do2, HEAD_DIM, 1), 0.0)
                dob_ref[2 * jb + e] = (base - _aug(lane, AUG_A, _split3(delta))).astype(BF16)

    def rows(n, dtype=None):
        return pl.BlockSpec((tm, n), lambda i: (i, 0))

    def out(n, dtype):
        return jax.ShapeDtypeStruct((s, n), dtype)

    vec = _const_spec((1, D_MODEL))
    vshape = jax.ShapeDtypeStruct((1, D_MODEL), F32)
    return pl.pallas_call(
        body, name="post_mix",
        out_shape=(out(D_MODEL, F32), out(SSD_WIDTH, F32), out(SSD_WIDTH, BF16),
                   jax.ShapeDtypeStruct((N_HEADS, s, LANES), BF16),
                   out(ATT_WIDTH, BF16), out(D_INNER, BF16), out(D_MODEL, BF16), out(D_MODEL, BF16),
                   out(D_MODEL, BF16), out(D_MODEL, BF16), out(PLE_DIM, BF16),
                   jax.ShapeDtypeStruct((1, LANES), F32), vshape, vshape, vshape, vshape),
        grid=(s // tm,),
        in_specs=[rows(D_MODEL), rows(SSD_WIDTH), rows(SSD_WIDTH), rows(ATT_WIDTH), rows(ATT_WIDTH),
                  rows(PLE_DIM), rows(D_MODEL), vec, vec, vec, vec,
                  _const_spec((D_INNER, D_MODEL)), _const_spec((D_MODEL, D_MODEL)), _const_spec((PLE_DIM, D_MODEL))],
        out_specs=(rows(D_MODEL), rows(SSD_WIDTH), rows(SSD_WIDTH),
                   pl.BlockSpec((N_HEADS, tm, LANES), lambda i: (0, i, 0)), rows(ATT_WIDTH),
                   rows(D_INNER), rows(D_MODEL), rows(D_MODEL), rows(D_MODEL), rows(D_MODEL), rows(PLE_DIM),
                   _const_spec((1, LANES)), vec, vec, vec, vec),
        compiler_params=_params(("arbitrary",)),
    )(x, y, zs, o, za, p, tgt, ssd_g, att_g_lane, ple_g, fin_g, w_out, w_gate, w_proj)


def in_proj_bwd(dsegs, wsegs, x, g, dh1):
    s = x.shape[0]
    tm = _blk(s, 256)
    nseg = len(dsegs)

    def body(*refs):
        d_refs = refs[:nseg]
        w_refs = refs[nseg:2 * nseg]
        x_ref, g_ref, dh1_ref, dx_ref, dg_ref = refs[2 * nseg:]

        @pl.when(pl.program_id(0) == 0)
        def _():
            dg_ref[...] = jnp.zeros_like(dg_ref)

        du = _mm_nt(d_refs[0][...], w_refs[0][...])
        for k in range(1, nseg):
            du = du + _mm_nt(d_refs[k][...], w_refs[k][...])
        xv = x_ref[...]
        r = lax.rsqrt(jnp.mean(xv * xv, axis=-1, keepdims=True) + EPS)
        xh = xv * r
        dg_ref[...] += jnp.sum(du * xh, axis=0, keepdims=True)
        dxh = du * g_ref[...]
        dx_ref[...] = r * (dxh - xh * jnp.mean(dxh * xh, axis=-1, keepdims=True)) + dh1_ref[...]

    rows = lambda n: pl.BlockSpec((tm, n), lambda i: (i, 0))
    return pl.pallas_call(
        body, name="in_proj_bwd",
        out_shape=(jax.ShapeDtypeStruct((s, D_MODEL), F32), jax.ShapeDtypeStruct((1, D_MODEL), F32)),
        grid=(s // tm,),
        in_specs=([rows(d.shape[1]) for d in dsegs] + [_const_spec(w.shape) for w in wsegs]
                  + [rows(D_MODEL), _const_spec((1, D_MODEL)), rows(D_MODEL)]),
        out_specs=(rows(D_MODEL), _const_spec((1, D_MODEL))),
        compiler_params=_params(("arbitrary",)),
    )(*dsegs, *wsegs, x, g, dh1)


def _pack(w_in_s, w_out_s, w_gate_s, w_proj_s, conv_s, dtype):
    parts = [w_in_s.reshape(ROWS_W_IN, LANES), w_out_s.reshape(ROWS_W_OUT, LANES),
             w_gate_s.reshape(ROWS_W_GATE, LANES), w_proj_s.reshape(ROWS_W_PROJ, LANES)]
    used = ROWS_W_IN + ROWS_W_OUT + ROWS_W_GATE + ROWS_W_PROJ
    if conv_s is not None:
        parts.append(jnp.pad(conv_s.reshape(12, LANES), ((0, 4), (0, 0))))
        used += ROWS_CONV
    parts = [a.astype(dtype) for a in parts]
    parts.append(jnp.zeros((PACK_ROWS - used, LANES), dtype))
    return jnp.concatenate(parts, axis=0)


def _unpack(pack):
    o1 = ROWS_W_IN
    o2 = o1 + ROWS_W_OUT
    o3 = o2 + ROWS_W_GATE
    o4 = o3 + ROWS_W_PROJ
    return (pack[:o1].reshape(1, 1024, 1672), pack[o1:o2].reshape(1, 512, 1024),
            pack[o2:o3].reshape(1, 256, 1024), pack[o3:o4].reshape(1, 256, 256),
            pack[o4:o4 + 12].reshape(1, 4, 384))


SMALL_NAMES = ("norm_g", "conv_b", "dt_bias", "a_log", "d_skip", "ssd_norm_g", "fg_bias", "att_norm_g",
               "ple_norm_g", "final_norm_g")
SMALL_SIZES = (1024, 1536, 16, 16, 16, 1024, 16, 64, 1024, 1024)


def _pack_small(vals):
    flat = jnp.concatenate([v.reshape(-1).astype(F32) for v in vals])
    flat = jnp.pad(flat, (0, SMALL_ROWS * LANES - flat.shape[0]))
    return flat.reshape(SMALL_ROWS, LANES)


def _unpack_small(pack, shapes):
    flat = pack.reshape(-1)
    out, off = [], 0
    for n, shp in zip(SMALL_SIZES, shapes):
        out.append(flat[off:off + n].reshape(shp))
        off += n
    return out


def _row128(v16, offset=0):
    return jnp.pad(v16.reshape(1, N_HEADS).astype(F32), ((0, 0), (offset, LANES - N_HEADS - offset)))


def local_step(x, p, tgt, w_in, w_out, w_gate, w_proj, conv_w, norm_g, conv_b, dt_bias, a_log, d_skip,
               ssd_norm_g, fg_bias, att_norm_g, ple_norm_g, final_norm_g):
    c0, c1, c2, c3, c4, c5, c6, c7 = 0, 1024, 2560, 2576, 3600, 4624, 5648, 6672
    w_zs, w_xbc, w_dt = w_in[:, c0:c1], w_in[:, c1:c2], w_in[:, c2:c3]
    w_za, w_q, w_k, w_v, w_f = w_in[:, c3:c4], w_in[:, c4:c5], w_in[:, c5:c6], w_in[:, c6:c7], w_in[:, c7:]
    w_small = jnp.concatenate([w_dt, w_f, jnp.zeros((D_MODEL, LANES - 2 * N_HEADS), BF16)], axis=1)

    dtb_row = _row128(dt_bias)
    a_row = _row128(-jnp.exp(a_log.astype(F32)))
    fgb_row = _row128(fg_bias, N_HEADS)
    dskip_lane = jnp.repeat(d_skip.astype(F32), HEAD_DIM).reshape(1, SSD_WIDTH)
    att_g_lane = jnp.tile(att_norm_g.astype(F32), N_HEADS).reshape(1, ATT_WIDTH)
    row = lambda v: v.reshape(1, -1).astype(F32)

    u = rms_prenorm(x, row(norm_g))
    zs = matmul_rows(u, w_zs, F32, "proj_z_ssd")
    xbc = matmul_rows(u, w_xbc, F32, "proj_xbc")
    za = matmul_rows(u, w_za, F32, "proj_z_att")
    small = matmul_rows(u, w_small, F32, "proj_small")
    cum = forget_cumsum(small, fgb_row)
    qa, ka, va = proj_qkv_heads(u, w_q, w_k, w_v, cum)
    pre, xc = conv_fwd(xbc, conv_w, row(conv_b))
    y, states = ssd_fwd(xc, small, dtb_row, a_row, dskip_lane)
    o, qb = attention_fwd(qa, ka, va)
    (dh1, dy, dzs, dob, dza, ycat, dh1_b, n2_b, dgl_b, dpp_b, p_b,
     loss_l, dfin, dple, dssd_g, datt_lane) = post_mix(
        x, y, zs, o, za, p, tgt, row(ssd_norm_g), att_g_lane, row(ple_norm_g), row(final_norm_g),
        w_out, w_gate, w_proj)
    dq, dk, dv, dc = attention_bwd(qb, ka, va, dob)
    dxc, ddt_raw, da, ddtb, ddsk_lane = ssd_bwd(xc, small, states, dy, dtb_row, a_row, dskip_lane)
    dsmall, dfgb = forget_bwd(dc, small, ddt_raw, fgb_row)
    dxbc, dconv_w8, dconv_b = conv_bwd(xbc, pre, dxc, conv_w)
    dsegs = [dzs, dxbc, dza, dq, dk, dv, dsmall]
    wsegs = [w_zs, w_xbc, w_za, w_q, w_k, w_v, w_small]
    dx, dnorm_g = in_proj_bwd(dsegs, wsegs, x, row(norm_g), dh1)
    dws = [matmul_tn(u, d, "dw_in_%d" % i) for i, d in enumerate(dsegs)]
    dw_in = jnp.concatenate([dws[0], dws[1], dws[6][:, :N_HEADS], dws[2], dws[3], dws[4], dws[5],
                             dws[6][:, N_HEADS:2 * N_HEADS]], axis=1)
    dw_out = matmul_tn(ycat, dh1_b, "dw_out")
    dw_gate = matmul_tn(n2_b, dgl_b, "dw_gate")
    dw_proj = matmul_tn(p_b, dpp_b, "dw_proj")
    small_grads = [
        dnorm_g, dconv_b, ddtb[0, :N_HEADS], (da * a_row)[0, :N_HEADS],
        ddsk_lane.reshape(N_HEADS, HEAD_DIM).sum(axis=1), dssd_g, dfgb[0, N_HEADS:2 * N_HEADS],
        datt_lane.reshape(N_HEADS, HEAD_DIM).sum(axis=0), dple, dfin]
    loss = jnp.sum(loss_l)
    return loss, dx, dw_in, dw_out, dw_gate, dw_proj, dconv_w8[:CONV_WIDTH], small_grads


def kernel(x, p, norm_g, w_in, conv_w, conv_b, dt_bias, a_log, d_skip, ssd_norm_g, fg_bias, att_norm_g, w_out, ple_norm_g, w_ple_gate, w_ple_proj, final_norm_g, loss_target, m_norm_g, m_w_in, m_conv_w, m_conv_b, m_dt_bias, m_a_log, m_d_skip, m_ssd_norm_g, m_fg_bias, m_att_norm_g, m_w_out, m_ple_norm_g, m_w_ple_gate, m_w_ple_proj, m_final_norm_g, v_norm_g, v_w_in, v_conv_w, v_conv_b, v_dt_bias, v_a_log, v_d_skip, v_ssd_norm_g, v_fg_bias, v_att_norm_g, v_w_out, v_ple_norm_g, v_w_ple_gate, v_w_ple_proj, v_final_norm_g):
    wpack = _pack(w_in[0], w_out[0], w_ple_gate[0], w_ple_proj[0], None, BF16)
    cpack = jnp.pad(conv_w[0].reshape(12, LANES), ((0, 4), (0, 0)))
    wall, call = gather_weights(wpack, cpack)
    o1 = ROWS_W_IN
    o2 = o1 + ROWS_W_OUT
    o3 = o2 + ROWS_W_GATE
    o4 = o3 + ROWS_W_PROJ
    w_in_f = jnp.concatenate([wall[j, :o1].reshape(1024, 1672) for j in range(N_CHIPS)], axis=1)
    w_out_f = jnp.concatenate([wall[j, o1:o2].reshape(512, 1024) for j in range(N_CHIPS)], axis=0)
    w_gate_f = jnp.concatenate([wall[j, o2:o3].reshape(256, 1024) for j in range(N_CHIPS)], axis=0)
    w_proj_f = jnp.concatenate([wall[j, o3:o4].reshape(256, 256) for j in range(N_CHIPS)], axis=1)
    conv_w_f = jnp.concatenate([call[j, :12].reshape(4, 384) for j in range(N_CHIPS)], axis=1)

    smalls_w = [norm_g, conv_b, dt_bias, a_log, d_skip, ssd_norm_g, fg_bias, att_norm_g, ple_norm_g, final_norm_g]
    loss_l, dx, dw_in, dw_out, dw_gate, dw_proj, dconv_w, small_grads = local_step(
        x[0], p[0, 0], loss_target[0], w_in_f, w_out_f, w_gate_f, w_proj_f, conv_w_f,
        *[a.reshape(-1) for a in smalls_w])
    loss = lax.psum(loss_l, ("x", "y", "c"))

    gpack = jnp.stack([
        _pack(dw_in[:, 1672 * j:1672 * (j + 1)], dw_out[512 * j:512 * (j + 1)], dw_gate[256 * j:256 * (j + 1)],
              dw_proj[:, 256 * j:256 * (j + 1)], dconv_w[:, 384 * j:384 * (j + 1)], F32)
        for j in range(N_CHIPS)])
    parts, smalls = scatter_grads(gpack, _pack_small(small_grads))
    mine = sum_parts(parts)
    theirs = swap_with_sibling(mine)
    south = lax.axis_index("c") == 0
    p_south = jnp.where(south, mine, theirs)
    p_north = jnp.where(south, theirs, mine)

    big = lambda a, b, c_, d, e: _pack(a[0], b[0], c_[0], d[0], e[0], F32)
    g_pk, d_pk, m_pk, v_pk = adamw_pack(
        p_south, p_north, big(w_in, w_out, w_ple_gate, w_ple_proj, conv_w),
        big(m_w_in, m_w_out, m_w_ple_gate, m_w_ple_proj, m_conv_w),
        big(v_w_in, v_w_out, v_w_ple_gate, v_w_ple_proj, v_conv_w))
    smalls_m = [m_norm_g, m_conv_b, m_dt_bias, m_a_log, m_d_skip, m_ssd_norm_g, m_fg_bias, m_att_norm_g,
                m_ple_norm_g, m_final_norm_g]
    smalls_v = [v_norm_g, v_conv_b, v_dt_bias, v_a_log, v_d_skip, v_ssd_norm_g, v_fg_bias, v_att_norm_g,
                v_ple_norm_g, v_final_norm_g]
    g_sm, d_sm, m_sm, v_sm = adamw_small(smalls, _pack_small(smalls_w), _pack_small(smalls_m), _pack_small(smalls_v))

    shapes = [a.shape for a in smalls_w]
    outs = []
    for pk, sm in ((g_pk, g_sm), (d_pk, d_sm), (m_pk, m_sm), (v_pk, v_sm)):
        b_in, b_out, b_gate, b_proj, b_conv = _unpack(pk)
        s_norm, s_convb, s_dtb, s_alog, s_dsk, s_ssdg, s_fgb, s_attg, s_pleg, s_fin = _unpack_small(sm, shapes)
        outs.extend([s_norm, b_in, b_conv, s_convb, s_dtb, s_alog, s_dsk, s_ssdg, s_fgb, s_attg, b_out, s_pleg,
                     b_gate, b_proj, s_fin])
    return (loss, dx[None], *outs)
```

```python
import functools

import jax
import jax.numpy as jnp
from jax import lax
from jax.experimental import pallas as pl
from jax.experimental.pallas import tpu as pltpu

F32 = jnp.float32
BF16 = jnp.bfloat16

D_MODEL = 1024
SSD_WIDTH = 1024
ATT_WIDTH = 1024
N_HEADS = 16
HEAD_DIM = 64
N_GROUPS = 2
D_STATE = 128
CONV_CH = 1536
CONV_WIDTH = 4
CHUNK = 128
PLE_DIM = 256
D_INNER = 2048
EPS = 1e-6
IN_COLS = 6688
N_CHIPS = 4
N_DEV = 8
LANES = 128
N_PAIRS = 8

ADAM_LR = 0.001
ADAM_B1 = 0.9
ADAM_B2 = 0.999
ADAM_EPS = 1e-08
ADAM_WD = 0.01
ADAM_STEP = 10

SMALL_ROWS = 96

NEG_BIG = -1e30
VMEM_LIMIT = 56 * 1024 * 1024

MESH = pl.DeviceIdType.MESH
ANY = pl.BlockSpec(memory_space=pl.ANY)


def _mm(a, b):
    return jnp.dot(a, b, preferred_element_type=F32)


def _mm_nt(a, b):
    return lax.dot_general(a, b, (((1,), (1,)), ((), ())), preferred_element_type=F32)


def _mm_tn(a, b):
    return lax.dot_general(a, b, (((0,), (0,)), ((), ())), preferred_element_type=F32)


def _mm_exact(a, b):
    return jnp.dot(a, b, preferred_element_type=F32, precision=lax.Precision.HIGHEST)


def _softplus(x):
    return jnp.maximum(x, 0.0) + jnp.log1p(jnp.exp(-jnp.abs(x)))


def _sigmoid(x):
    return jax.nn.sigmoid(x)


def _iota(shape, dim):
    return lax.broadcasted_iota(jnp.int32, shape, dim)


def _params(sem=None):
    return pltpu.CompilerParams(dimension_semantics=sem, vmem_limit_bytes=VMEM_LIMIT)


def _blk(n, pref):
    return min(n, pref)


def _const_spec(shape):
    nd = len(shape)
    return pl.BlockSpec(shape, lambda *_: (0,) * nd)


def _chip_peers():
    x, y, c = lax.axis_index("x"), lax.axis_index("y"), lax.axis_index("c")
    return x, y, c, [(1 - x, y, c), (x, 1 - y, c), (1 - x, 1 - y, c)]


def _half(rows, c):
    h = rows // 2
    return pl.ds(pl.multiple_of(c * h, 8), h)


def _sems(n):
    return [pltpu.SemaphoreType.DMA((n,)), pltpu.SemaphoreType.DMA((n,))]


def gather_weights(shards, conv_s):
    n = len(shards)

    def body(*refs):
        ins, conv_in = refs[:n], refs[n]
        outs, conv_out = refs[n + 1:2 * n + 1], refs[2 * n + 1]
        ssem1, rsem1, ssem2, rsem2, c_ssem, c_rsem, lsem = refs[2 * n + 2:]
        x, y, c, peers = _chip_peers()
        me = 2 * x + y
        sibling = (x, y, 1 - c)
        local = [pltpu.make_async_copy(ins[i], outs[i].at[me], lsem.at[i]) for i in range(n)]
        local.append(pltpu.make_async_copy(conv_in, conv_out.at[me], lsem.at[n]))
        for cp in local:
            cp.start()
        first, small = [], []
        for k, peer in enumerate(peers):
            for i in range(n):
                h = _half(ins[i].shape[0], c)
                first.append(pltpu.make_async_remote_copy(
                    src_ref=ins[i].at[h], dst_ref=outs[i].at[me, h], send_sem=ssem1.at[n * k + i],
                    recv_sem=rsem1.at[n * k + i], device_id=peer, device_id_type=MESH))
            small.append(pltpu.make_async_remote_copy(
                src_ref=conv_in, dst_ref=conv_out.at[me], send_sem=c_ssem.at[k], recv_sem=c_rsem.at[k],
                device_id=peer, device_id_type=MESH))
        for cp in first + small:
            cp.start()
        passed = []
        for k, peer in enumerate(peers):
            chip = 2 * peer[0] + peer[1]
            for i in range(n):
                h = _half(ins[i].shape[0], c)
                first[n * k + i].wait_recv()
                fwd = pltpu.make_async_remote_copy(
                    src_ref=outs[i].at[chip, h], dst_ref=outs[i].at[chip, h], send_sem=ssem2.at[n * k + i],
                    recv_sem=rsem2.at[n * k + i], device_id=sibling, device_id_type=MESH)
                fwd.start()
                passed.append(fwd)
        for cp in passed:
            cp.wait_recv()
        for cp in first + passed:
            cp.wait_send()
        for cp in small:
            cp.wait()
        for cp in local:
            cp.wait()

    return pl.pallas_call(
        body, name="gather_weights",
        out_shape=tuple(jax.ShapeDtypeStruct((N_CHIPS,) + a.shape, a.dtype) for a in list(shards) + [conv_s]),
        in_specs=[ANY] * (n + 1), out_specs=(ANY,) * (n + 1),
        scratch_shapes=_sems(3 * n) + _sems(3 * n) + _sems(3) + [pltpu.SemaphoreType.DMA((n + 1,))],
    )(*shards, conv_s)


def halves_to_sibling(gs):
    n = len(gs)

    def body(*refs):
        ins, outs = refs[:n], refs[n:2 * n]
        ssem, rsem = refs[2 * n:]
        x, y, c = lax.axis_index("x"), lax.axis_index("y"), lax.axis_index("c")
        copies = []
        for i in range(n):
            for j in range(N_CHIPS):
                copies.append(pltpu.make_async_remote_copy(
                    src_ref=ins[i].at[j, _half(ins[i].shape[1], 1 - c)], dst_ref=outs[i].at[j],
                    send_sem=ssem.at[N_CHIPS * i + j], recv_sem=rsem.at[N_CHIPS * i + j],
                    device_id=(x, y, 1 - c), device_id_type=MESH))
        for cp in copies:
            cp.start()
        for cp in copies:
            cp.wait()

    return pl.pallas_call(
        body, name="halves_to_sibling",
        out_shape=tuple(jax.ShapeDtypeStruct((N_CHIPS, g.shape[1] // 2, g.shape[2]), F32) for g in gs),
        in_specs=[ANY] * n, out_specs=(ANY,) * n, scratch_shapes=_sems(N_CHIPS * n),
    )(*gs)


RED_GRID = 8


def add_halves(core, gs, rbs):
    n = len(gs)

    def body(c_ref, *refs):
        for i in range(n):
            refs[2 * n + i][...] = refs[i][...] + refs[n + i][...]

    def blk(g):
        return (1, g.shape[1] // 2 // RED_GRID, g.shape[2])

    grid_spec = pltpu.PrefetchScalarGridSpec(
        num_scalar_prefetch=1, grid=(N_CHIPS, RED_GRID),
        in_specs=([pl.BlockSpec(blk(g), lambda j, b, c_ref: (j, c_ref[0] * RED_GRID + b, 0)) for g in gs]
                  + [pl.BlockSpec(blk(g), lambda j, b, c_ref: (j, b, 0)) for g in gs]),
        out_specs=[pl.BlockSpec(blk(g), lambda j, b, c_ref: (j, b, 0)) for g in gs])
    return pl.pallas_call(
        body, name="add_halves", grid_spec=grid_spec,
        out_shape=tuple(jax.ShapeDtypeStruct(r.shape, F32) for r in rbs),
        compiler_params=_params(("parallel", "parallel")),
    )(core, *gs, *rbs)


def scatter_halves(pres, small):
    n = len(pres)

    def body(*refs):
        ins, s_ref = refs[:n], refs[n]
        outs, smalls_ref = refs[n + 1:2 * n + 1], refs[2 * n + 1]
        ssem, rsem, s_ssem, s_rsem, lsem = refs[2 * n + 2:]
        x, y, c, peers = _chip_peers()
        me = 2 * x + y
        dev = 4 * x + 2 * y + c
        local = [pltpu.make_async_copy(ins[i].at[me], outs[i].at[me], lsem.at[i]) for i in range(n)]
        local.append(pltpu.make_async_copy(s_ref, smalls_ref.at[dev], lsem.at[n]))
        for cp in local:
            cp.start()
        remote = []
        for k, peer in enumerate(peers):
            dst_chip = 2 * peer[0] + peer[1]
            for i in range(n):
                remote.append(pltpu.make_async_remote_copy(
                    src_ref=ins[i].at[dst_chip], dst_ref=outs[i].at[me], send_sem=ssem.at[n * k + i],
                    recv_sem=rsem.at[n * k + i], device_id=peer, device_id_type=MESH))
        for k in range(1, N_DEV):
            fx, fy, fc = (k >> 2) & 1, (k >> 1) & 1, k & 1
            peer = ((1 - x) if fx else x, (1 - y) if fy else y, (1 - c) if fc else c)
            remote.append(pltpu.make_async_remote_copy(
                src_ref=s_ref, dst_ref=smalls_ref.at[dev], send_sem=s_ssem.at[k - 1], recv_sem=s_rsem.at[k - 1],
                device_id=peer, device_id_type=MESH))
        for cp in remote:
            cp.start()
        for cp in remote:
            cp.wait()
        for cp in local:
            cp.wait()

    return pl.pallas_call(
        body, name="scatter_halves",
        out_shape=tuple([jax.ShapeDtypeStruct(a.shape, F32) for a in pres]
                        + [jax.ShapeDtypeStruct((N_DEV,) + small.shape, F32)]),
        in_specs=[ANY] * (n + 1), out_specs=(ANY,) * (n + 1),
        scratch_shapes=_sems(3 * n) + _sems(N_DEV - 1) + [pltpu.SemaphoreType.DMA((n + 1,))],
    )(*pres, small)


def sum_parts(parts):
    n = len(parts)

    def body(*refs):
        for i in range(n):
            p_ref = refs[i]
            refs[n + i][...] = ((p_ref[0] + p_ref[1]) + p_ref[2]) + p_ref[3]

    def rows(p):
        return p.shape[1] // RED_GRID

    return pl.pallas_call(
        body, name="sum_parts",
        out_shape=tuple(jax.ShapeDtypeStruct(p.shape[1:], F32) for p in parts),
        grid=(RED_GRID,),
        in_specs=[pl.BlockSpec((N_CHIPS, rows(p), p.shape[2]), lambda b: (0, b, 0)) for p in parts],
        out_specs=tuple(pl.BlockSpec((rows(p), p.shape[2]), lambda b: (b, 0)) for p in parts),
        compiler_params=_params(("parallel",)),
    )(*parts)


def gather_halves(reds):
    n = len(reds)

    def body(*refs):
        ins, outs = refs[:n], refs[n:2 * n]
        ssem, rsem, lsem = refs[2 * n:]
        x, y, c = lax.axis_index("x"), lax.axis_index("y"), lax.axis_index("c")
        copies = []
        for i in range(n):
            h = _half(outs[i].shape[0], c)
            copies.append(pltpu.make_async_copy(ins[i], outs[i].at[h], lsem.at[i]))
            copies.append(pltpu.make_async_remote_copy(
                src_ref=ins[i], dst_ref=outs[i].at[h], send_sem=ssem.at[i], recv_sem=rsem.at[i],
                device_id=(x, y, 1 - c), device_id_type=MESH))
        for cp in copies:
            cp.start()
        for cp in copies:
            cp.wait()

    return pl.pallas_call(
        body, name="gather_halves",
        out_shape=tuple(jax.ShapeDtypeStruct((2 * r.shape[0], r.shape[1]), F32) for r in reds),
        in_specs=[ANY] * n, out_specs=(ANY,) * n,
        scratch_shapes=_sems(n) + [pltpu.SemaphoreType.DMA((n,))],
    )(*reds)


def _adamw(w, g, m, v):
    m = ADAM_B1 * m + (1.0 - ADAM_B1) * g
    v = ADAM_B2 * v + (1.0 - ADAM_B2) * (g * g)
    m_hat = m / (1.0 - ADAM_B1 ** ADAM_STEP)
    v_hat = v / (1.0 - ADAM_B2 ** ADAM_STEP)
    delta = -ADAM_LR * (m_hat / (jnp.sqrt(v_hat) + ADAM_EPS) + ADAM_WD * w)
    return delta, m, v


def adamw_big(gs, ws, ms, vs):
    n = len(gs)

    def body(*refs):
        for i in range(n):
            d, mn, vn = _adamw(refs[n + i][...], refs[i][...], refs[2 * n + i][...], refs[3 * n + i][...])
            refs[4 * n + i][...] = d
            refs[5 * n + i][...] = mn
            refs[6 * n + i][...] = vn

    specs = [pl.BlockSpec((g.shape[0] // RED_GRID, g.shape[1]), lambda b: (b, 0)) for g in gs]
    shapes = [jax.ShapeDtypeStruct(g.shape, F32) for g in gs]
    outs = pl.pallas_call(
        body, name="adamw_big", out_shape=tuple(shapes * 3), grid=(RED_GRID,),
        in_specs=specs * 4, out_specs=tuple(specs * 3), compiler_params=_params(("parallel",)),
    )(*gs, *ws, *ms, *vs)
    return outs[:n], outs[n:2 * n], outs[2 * n:]


def adamw_whole(g, w, m, v, name):
    def body(g_ref, w_ref, m_ref, v_ref, d_out, m_out, v_out):
        d, mn, vn = _adamw(w_ref[...], g_ref[...], m_ref[...], v_ref[...])
        d_out[...] = d
        m_out[...] = mn
        v_out[...] = vn

    shp = jax.ShapeDtypeStruct(g.shape, F32)
    return pl.pallas_call(body, name=name, out_shape=(shp,) * 3)(g, w, m, v)


def adamw_small(smalls, w, m, v):
    def body(s_ref, w_ref, m_ref, v_ref, g_out, d_out, m_out, v_out):
        g = s_ref[0]
        for k in range(1, N_DEV):
            g = g + s_ref[k]
        d, mn, vn = _adamw(w_ref[...], g, m_ref[...], v_ref[...])
        g_out[...] = g
        d_out[...] = d
        m_out[...] = mn
        v_out[...] = vn

    shp = jax.ShapeDtypeStruct((SMALL_ROWS, LANES), F32)
    return pl.pallas_call(body, name="adamw_small", out_shape=(shp,) * 4)(smalls, w, m, v)


def rms_prenorm(x, g):
    s = x.shape[0]
    tm = _blk(s, 512)

    def body(x_ref, g_ref, u_ref):
        xv = x_ref[...]
        r = lax.rsqrt(jnp.mean(xv * xv, axis=-1, keepdims=True) + EPS)
        u_ref[...] = (xv * r * g_ref[...]).astype(BF16)

    return pl.pallas_call(
        body, name="rms_prenorm", out_shape=jax.ShapeDtypeStruct(x.shape, BF16), grid=(s // tm,),
        in_specs=[pl.BlockSpec((tm, D_MODEL), lambda i: (i, 0)), _const_spec((1, D_MODEL))],
        out_specs=pl.BlockSpec((tm, D_MODEL), lambda i: (i, 0)), compiler_params=_params(("parallel",)),
    )(x, g)


def matmul_rows(a, w, out_dtype, name):
    s, k = a.shape
    n = w.shape[1]
    tm = _blk(s, 512)

    def body(a_ref, w_ref, o_ref):
        o_ref[...] = _mm(a_ref[...], w_ref[...]).astype(out_dtype)

    return pl.pallas_call(
        body, name=name, out_shape=jax.ShapeDtypeStruct((s, n), out_dtype), grid=(s // tm,),
        in_specs=[pl.BlockSpec((tm, k), lambda i: (i, 0)), _const_spec((k, n))],
        out_specs=pl.BlockSpec((tm, n), lambda i: (i, 0)), compiler_params=_params(("parallel",)),
    )(a, w)


def matmul_tn(a, b, name):
    s, m = a.shape
    n = b.shape[1]
    tk = _blk(s, 512)
    tn = _blk(n, 512)

    def body(a_ref, b_ref, o_ref):
        @pl.when(pl.program_id(1) == 0)
        def _():
            o_ref[...] = jnp.zeros_like(o_ref)

        o_ref[...] += _mm_tn(a_ref[...], b_ref[...])

    return pl.pallas_call(
        body, name=name, out_shape=jax.ShapeDtypeStruct((m, n), F32), grid=(n // tn, s // tk),
        in_specs=[pl.BlockSpec((tk, m), lambda j, i: (i, 0)), pl.BlockSpec((tk, tn), lambda j, i: (i, j))],
        out_specs=pl.BlockSpec((m, tn), lambda j, i: (0, j)),
        compiler_params=_params(("parallel", "arbitrary")),
    )(a, b)


def conv_fwd(xbc, w, b):
    s = xbc.shape[0]
    tm = _blk(s, 256)

    def body(x_ref, t_ref, w_ref, b_ref, pre_ref, act_ref):
        i = pl.program_id(0)
        cur = x_ref[...]
        tail = jnp.where(i > 0, t_ref[...], 0.0)
        wv = w_ref[...]
        acc = cur * wv[3:4, :] + b_ref[...]
        head = cur[0:8, :] * wv[3:4, :] + b_ref[...]
        row8 = _iota((8, CONV_CH), 0)
        for sh in range(1, CONV_WIDTH):
            wk = wv[3 - sh:4 - sh, :]
            acc = acc + pltpu.roll(cur, sh, 0) * wk
            first = jnp.where(row8 < sh, pltpu.roll(tail, sh, 0), pltpu.roll(cur[0:8, :], sh, 0))
            head = head + first * wk
        pre_ref[...] = acc
        act_ref[...] = acc * _sigmoid(acc)
        pre_ref[0:8, :] = head
        act_ref[0:8, :] = head * _sigmoid(head)

    shp = jax.ShapeDtypeStruct(xbc.shape, F32)
    rows = pl.BlockSpec((tm, CONV_CH), lambda i: (i, 0))
    return pl.pallas_call(
        body, name="conv_fwd", out_shape=(shp, shp), grid=(s // tm,),
        in_specs=[rows, pl.BlockSpec((8, CONV_CH), lambda i: (jnp.maximum(i * (tm // 8) - 1, 0), 0)),
                  _const_spec((CONV_WIDTH, CONV_CH)), _const_spec((1, CONV_CH))],
        out_specs=(rows, rows), compiler_params=_params(("parallel",)),
    )(xbc, xbc, w, b)


def conv_bwd(xbc, pre, dact, w):
    s = xbc.shape[0]
    tm = _blk(s, 256)
    nb = s // tm

    def dsilu(p):
        sg = _sigmoid(p)
        return sg * (1.0 + p * (1.0 - sg))

    def body(x_ref, xt_ref, p_ref, pn_ref, d_ref, dn_ref, w_ref, dx_ref, dw_ref, db_ref):
        i = pl.program_id(0)

        @pl.when(i == 0)
        def _():
            dw_ref[...] = jnp.zeros_like(dw_ref)
            db_ref[...] = jnp.zeros_like(db_ref)

        wv = w_ref[...]
        dpre = d_ref[...] * dsilu(p_ref[...])
        dnext = jnp.where(i < nb - 1, dn_ref[...] * dsilu(pn_ref[...]), 0.0)
        cur = x_ref[...]
        tail = jnp.where(i > 0, xt_ref[...], 0.0)
        row8 = _iota((8, CONV_CH), 0)
        dx = dpre * wv[3:4, :]
        last = dpre[tm - 8:tm, :] * wv[3:4, :]
        db_ref[...] += jnp.sum(dpre, axis=0, keepdims=True)
        dws = [jnp.sum(dpre * cur, axis=0, keepdims=True)]
        for sh in range(1, CONV_WIDTH):
            wk = wv[3 - sh:4 - sh, :]
            dx = dx + pltpu.roll(dpre, tm - sh, 0) * wk
            nxt = jnp.where(row8 >= 8 - sh, pltpu.roll(dnext, 8 - sh, 0), pltpu.roll(dpre[tm - 8:tm, :], 8 - sh, 0))
            last = last + nxt * wk
            xs = pltpu.roll(cur, sh, 0)
            first = jnp.where(row8 < sh, pltpu.roll(tail, sh, 0), xs[0:8, :])
            dws.append(jnp.sum(dpre * xs, axis=0, keepdims=True)
                       + jnp.sum(dpre[0:8, :] * (first - xs[0:8, :]), axis=0, keepdims=True))
        dx_ref[...] = dx.astype(BF16)
        dx_ref[tm - 8:tm, :] = last.astype(BF16)
        for sh in range(CONV_WIDTH):
            dw_ref[3 - sh:4 - sh, :] += dws[sh]

    rows = pl.BlockSpec((tm, CONV_CH), lambda i: (i, 0))
    prev8 = pl.BlockSpec((8, CONV_CH), lambda i: (jnp.maximum(i * (tm // 8) - 1, 0), 0))
    next8 = pl.BlockSpec((8, CONV_CH), lambda i: (jnp.minimum((i + 1) * (tm // 8), s // 8 - 1), 0))
    return pl.pallas_call(
        body, name="conv_bwd",
        out_shape=(jax.ShapeDtypeStruct(xbc.shape, BF16), jax.ShapeDtypeStruct((8, CONV_CH), F32),
                   jax.ShapeDtypeStruct((1, CONV_CH), F32)),
        grid=(nb,),
        in_specs=[rows, prev8, rows, next8, rows, next8, _const_spec((CONV_WIDTH, CONV_CH))],
        out_specs=(rows, _const_spec((8, CONV_CH)), _const_spec((1, CONV_CH))),
        compiler_params=_params(("arbitrary",)),
    )(xbc, xbc, pre, pre, dact, dact, w)


def _pair_lanes(mat, j, lane):
    return jnp.where(lane < HEAD_DIM, mat[:, 2 * j:2 * j + 1], mat[:, 2 * j + 1:2 * j + 2])


def _ssd_chunk_prelude(sm, dtb, a_row, lane, sub):
    raw = sm + dtb
    head_lane = lane < N_HEADS
    dt = jnp.where(head_lane, _softplus(raw), 0.0)
    sig = jnp.where(head_lane, _sigmoid(raw), 0.0)
    tri = (lane <= sub).astype(F32)
    acs = _mm_exact(tri, dt * a_row)
    return dt, sig, acs, acs.T


def ssd_fwd(xc, small, dtb_row, a_row, dskip_lane):
    s = xc.shape[0]
    nc = s // CHUNK

    def body(xc_ref, sm_ref, dtb_ref, a_ref, dsk_ref, y_ref, hs_ref, h_scr):
        c = pl.program_id(0)

        @pl.when(c == 0)
        def _():
            h_scr[...] = jnp.zeros_like(h_scr)

        lane = _iota((CHUNK, LANES), 1)
        sub = _iota((CHUNK, LANES), 0)
        causal = lane <= sub
        dt, _, acs, acs_t = _ssd_chunk_prelude(sm_ref[...], dtb_ref[...], a_ref[...], lane, sub)
        last = acs[CHUNK - 1:CHUNK, :]
        e_all = jnp.exp(acs)
        dte = jnp.exp(last - acs)
        cd = jnp.exp(last)
        for g in range(N_GROUPS):
            b_b = xc_ref[:, SSD_WIDTH + D_STATE * g:SSD_WIDTH + D_STATE * (g + 1)].astype(BF16)
            c_b = xc_ref[:, SSD_WIDTH + N_GROUPS * D_STATE + D_STATE * g:
                         SSD_WIDTH + N_GROUPS * D_STATE + D_STATE * (g + 1)].astype(BF16)
            cb = _mm_nt(c_b, b_b)
            for j in range(4 * g, 4 * g + 4):
                x2 = xc_ref[:, LANES * j:LANES * (j + 1)]
                xdt2 = x2 * _pair_lanes(dt, j, lane)
                xdt2_b = xdt2.astype(BF16)
                yd = []
                for e in range(2):
                    h = 2 * j + e
                    seg = acs[:, h:h + 1] - acs_t[h:h + 1, :]
                    lm = jnp.exp(jnp.where(causal, seg, NEG_BIG))
                    yd.append(_mm((cb * lm).astype(BF16), xdt2_b))
                h2 = h_scr[j]
                t2 = _mm_nt(c_b, h2.astype(BF16))
                y2 = (jnp.where(lane < HEAD_DIM, yd[0], yd[1]) + _pair_lanes(e_all, j, lane) * t2
                      + dsk_ref[:, LANES * j:LANES * (j + 1)] * x2)
                y_ref[:, LANES * j:LANES * (j + 1)] = y2
                hs_ref[0, j] = h2
                w2 = (xdt2 * _pair_lanes(dte, j, lane)).astype(BF16)
                s2 = _mm_tn(w2, b_b)
                cdcol = jnp.where(sub < HEAD_DIM, cd[:, 2 * j:2 * j + 1], cd[:, 2 * j + 1:2 * j + 2])
                h_scr[j] = h2 * cdcol + s2

    return pl.pallas_call(
        body, name="ssd_fwd",
        out_shape=(jax.ShapeDtypeStruct((s, SSD_WIDTH), F32),
                   jax.ShapeDtypeStruct((nc, N_PAIRS, LANES, D_STATE), F32)),
        grid=(nc,),
        in_specs=[pl.BlockSpec((CHUNK, CONV_CH), lambda c: (c, 0)), pl.BlockSpec((CHUNK, LANES), lambda c: (c, 0)),
                  _const_spec((1, LANES)), _const_spec((1, LANES)), _const_spec((1, SSD_WIDTH))],
        out_specs=(pl.BlockSpec((CHUNK, SSD_WIDTH), lambda c: (c, 0)),
                   pl.BlockSpec((1, N_PAIRS, LANES, D_STATE), lambda c: (c, 0, 0, 0))),
        scratch_shapes=[pltpu.VMEM((N_PAIRS, LANES, D_STATE), F32)],
        compiler_params=_params(("arbitrary",)),
    )(xc, small, dtb_row, a_row, dskip_lane)


def ssd_bwd(xc, small, states, dy, dtb_row, a_row, dskip_lane):
    s = xc.shape[0]
    nc = s // CHUNK
    rev = lambda c: nc - 1 - c

    def head_rowsums(q, lane):
        r0 = jnp.sum(jnp.where(lane < HEAD_DIM, q, 0.0), axis=1, keepdims=True)
        r1 = jnp.sum(jnp.where(lane < HEAD_DIM, 0.0, q), axis=1, keepdims=True)
        return r0, r1

    def body(xc_ref, sm_ref, hs_ref, dy_ref, dtb_ref, a_ref, dsk_ref,
             dxc_ref, ddt_ref, da_ref, ddtb_ref, ddsk_ref, dh_scr):
        c = pl.program_id(0)

        @pl.when(c == 0)
        def _():
            dh_scr[...] = jnp.zeros_like(dh_scr)
            da_ref[...] = jnp.zeros_like(da_ref)
            ddtb_ref[...] = jnp.zeros_like(ddtb_ref)
            ddsk_ref[...] = jnp.zeros_like(ddsk_ref)

        lane = _iota((CHUNK, LANES), 1)
        sub = _iota((CHUNK, LANES), 0)
        causal = lane <= sub
        is_last = sub == CHUNK - 1
        a_row_v = a_ref[...]
        dt, sig, acs, acs_t = _ssd_chunk_prelude(sm_ref[...], dtb_ref[...], a_row_v, lane, sub)
        last = acs[CHUNK - 1:CHUNK, :]
        e_all = jnp.exp(acs)
        dte = jnp.exp(last - acs)
        cd = jnp.exp(last)
        dacs_c = jnp.zeros((CHUNK, LANES), F32)
        dacs_r = jnp.zeros((LANES, CHUNK), F32)
        ddtx = jnp.zeros((CHUNK, LANES), F32)
        for g in range(N_GROUPS):
            b_off = SSD_WIDTH + D_STATE * g
            c_off = SSD_WIDTH + N_GROUPS * D_STATE + D_STATE * g
            b_b = xc_ref[:, b_off:b_off + D_STATE].astype(BF16)
            c_b = xc_ref[:, c_off:c_off + D_STATE].astype(BF16)
            cb = _mm_nt(c_b, b_b)
            dcb = jnp.zeros((CHUNK, CHUNK), F32)
            db_g = jnp.zeros((CHUNK, D_STATE), F32)
            dc_g = jnp.zeros((CHUNK, D_STATE), F32)
            for j in range(4 * g, 4 * g + 4):
                x2 = xc_ref[:, LANES * j:LANES * (j + 1)]
                dt2 = _pair_lanes(dt, j, lane)
                xdt2 = x2 * dt2
                xdt2_b = xdt2.astype(BF16)
                dy2 = dy_ref[:, LANES * j:LANES * (j + 1)]
                h2 = hs_ref[0, j]
                dh2 = dh_scr[j]
                h2_b = h2.astype(BF16)
                dh2_b = dh2.astype(BF16)
                dxdt2 = jnp.zeros((CHUNK, LANES), F32)
                for e in range(2):
                    h = 2 * j + e
                    in_head = (lane < HEAD_DIM) if e == 0 else (lane >= HEAD_DIM)
                    seg = acs[:, h:h + 1] - acs_t[h:h + 1, :]
                    lm = jnp.exp(jnp.where(causal, seg, NEG_BIG))
                    m_h = cb * lm
                    dyh_b = jnp.where(in_head, dy2, 0.0).astype(BF16)
                    dm_h = _mm_nt(dyh_b, xdt2_b)
                    dxdt2 = dxdt2 + _mm_tn(m_h.astype(BF16), dyh_b)
                    gmat = dm_h * m_h
                    dacs_c = dacs_c + jnp.where(lane == h, jnp.sum(gmat, axis=1, keepdims=True), 0.0)
                    dacs_r = dacs_r - jnp.where(sub == h, jnp.sum(gmat, axis=0, keepdims=True), 0.0)
                    dcb = dcb + dm_h * lm
                t2 = _mm_nt(c_b, h2_b)
                e2 = _pair_lanes(e_all, j, lane)
                r0, r1 = head_rowsums(dy2 * e2 * t2, lane)
                dacs_c = dacs_c + jnp.where(lane == 2 * j, r0, 0.0) + jnp.where(lane == 2 * j + 1, r1, 0.0)
                dt2_b = (dy2 * e2).astype(BF16)
                dc_g = dc_g + _mm(dt2_b, h2_b)
                dh_prev = _mm_tn(dt2_b, c_b)
                dw2 = _mm_nt(b_b, dh2_b)
                dte2 = _pair_lanes(dte, j, lane)
                w2 = xdt2 * dte2
                dxdt2 = dxdt2 + dw2 * dte2
                db_g = db_g + _mm(w2.astype(BF16), dh2_b)
                r0, r1 = head_rowsums(dw2 * w2, lane)
                q3 = dh2 * h2
                s0 = jnp.sum(jnp.where(sub < HEAD_DIM, q3, 0.0), keepdims=True)
                s1 = jnp.sum(jnp.where(sub < HEAD_DIM, 0.0, q3), keepdims=True)
                for e, (r, sq) in enumerate(((r0, s0), (r1, s1))):
                    h = 2 * j + e
                    at_end = jnp.sum(r, keepdims=True) + sq * cd[:, h:h + 1]
                    dacs_c = dacs_c + jnp.where(lane == h, jnp.where(is_last, at_end, 0.0) - r, 0.0)
                cdcol = jnp.where(sub < HEAD_DIM, cd[:, 2 * j:2 * j + 1], cd[:, 2 * j + 1:2 * j + 2])
                dh_scr[j] = dh_prev + dh2 * cdcol
                dsk2 = dsk_ref[:, LANES * j:LANES * (j + 1)]
                dxc_ref[:, LANES * j:LANES * (j + 1)] = dxdt2 * dt2 + dsk2 * dy2
                r0, r1 = head_rowsums(dxdt2 * x2, lane)
                ddtx = ddtx + jnp.where(lane == 2 * j, r0, 0.0) + jnp.where(lane == 2 * j + 1, r1, 0.0)
                ddsk_ref[:, LANES * j:LANES * (j + 1)] += jnp.sum(dy2 * x2, axis=0, keepdims=True)
            dcb_b = dcb.astype(BF16)
            dxc_ref[:, b_off:b_off + D_STATE] = db_g + _mm_tn(dcb_b, c_b)
            dxc_ref[:, c_off:c_off + D_STATE] = dc_g + _mm(dcb_b, b_b)
        dacs = dacs_c + dacs_r.T
        dadt = _mm_exact((lane >= sub).astype(F32), dacs)
        ddt = dadt * a_row_v + ddtx
        ddt_raw = ddt * sig
        ddt_ref[...] = ddt_raw
        da_ref[...] += jnp.sum(dadt * dt, axis=0, keepdims=True)
        ddtb_ref[...] += jnp.sum(ddt_raw, axis=0, keepdims=True)

    return pl.pallas_call(
        body, name="ssd_bwd",
        out_shape=(jax.ShapeDtypeStruct((s, CONV_CH), F32), jax.ShapeDtypeStruct((s, LANES), F32),
                   jax.ShapeDtypeStruct((1, LANES), F32), jax.ShapeDtypeStruct((1, LANES), F32),
                   jax.ShapeDtypeStruct((1, SSD_WIDTH), F32)),
        grid=(nc,),
        in_specs=[pl.BlockSpec((CHUNK, CONV_CH), lambda c: (rev(c), 0)),
                  pl.BlockSpec((CHUNK, LANES), lambda c: (rev(c), 0)),
                  pl.BlockSpec((1, N_PAIRS, LANES, D_STATE), lambda c: (rev(c), 0, 0, 0)),
                  pl.BlockSpec((CHUNK, SSD_WIDTH), lambda c: (rev(c), 0)),
                  _const_spec((1, LANES)), _const_spec((1, LANES)), _const_spec((1, SSD_WIDTH))],
        out_specs=(pl.BlockSpec((CHUNK, CONV_CH), lambda c: (rev(c), 0)),
                   pl.BlockSpec((CHUNK, LANES), lambda c: (rev(c), 0)),
                   _const_spec((1, LANES)), _const_spec((1, LANES)), _const_spec((1, SSD_WIDTH))),
        scratch_shapes=[pltpu.VMEM((N_PAIRS, LANES, D_STATE), F32)],
        compiler_params=_params(("arbitrary",)),
    )(xc, small, states, dy, dtb_row, a_row, dskip_lane)


def forget_cumsum(small, fgb_row):
    s = small.shape[0]
    nb = s // CHUNK

    def body(sm_ref, b_ref, cc_ref, carry):
        i = pl.program_id(0)

        @pl.when(i == 0)
        def _():
            carry[...] = jnp.zeros_like(carry)

        lane = _iota((CHUNK, LANES), 1)
        sub = _iota((CHUNK, LANES), 0)
        in_f = (lane >= N_HEADS) & (lane < 2 * N_HEADS)
        logf = jnp.where(in_f, -_softplus(-(sm_ref[...] + b_ref[...])), 0.0)
        tri = (lane <= sub).astype(F32)
        cum = _mm_exact(tri, logf) + carry[0:1, :]
        cc_ref[...] = cum
        carry[...] = jnp.broadcast_to(cum[CHUNK - 1:CHUNK, :], (8, LANES))

    return pl.pallas_call(
        body, name="forget_cumsum",
        out_shape=jax.ShapeDtypeStruct((s, LANES), F32),
        grid=(nb,),
        in_specs=[pl.BlockSpec((CHUNK, LANES), lambda i: (i, 0)), _const_spec((1, LANES))],
        out_specs=pl.BlockSpec((CHUNK, LANES), lambda i: (i, 0)),
        scratch_shapes=[pltpu.VMEM((8, LANES), F32)],
        compiler_params=_params(("arbitrary",)),
    )(small, fgb_row)


def forget_bwd(dc, small, ddt_raw, fgb_row):
    s = small.shape[0]
    nb = s // CHUNK
    rev = lambda i: nb - 1 - i

    def body(dc_ref, sm_ref, ddt_ref, b_ref, ds_ref, dfb_ref, carry):
        i = pl.program_id(0)

        @pl.when(i == 0)
        def _():
            carry[...] = jnp.zeros_like(carry)
            dfb_ref[...] = jnp.zeros_like(dfb_ref)

        lane = _iota((CHUNK, LANES), 1)
        sub = _iota((CHUNK, LANES), 0)
        rows = dc_ref[...].T
        tri = (lane <= sub).astype(F32)
        rc = _mm_exact(rows, tri) + carry[:, 0:1]
        carry[...] = jnp.broadcast_to(rc[:, 0:1], (LANES, LANES))
        in_f = (lane >= N_HEADS) & (lane < 2 * N_HEADS)
        df = jnp.where(in_f, rc.T * _sigmoid(-(sm_ref[...] + b_ref[...])), 0.0)
        ds_ref[...] = (df + ddt_ref[...]).astype(BF16)
        dfb_ref[...] += jnp.sum(df, axis=0, keepdims=True)

    return pl.pallas_call(
        body, name="forget_bwd",
        out_shape=(jax.ShapeDtypeStruct((s, LANES), BF16), jax.ShapeDtypeStruct((1, LANES), F32)),
        grid=(nb,),
        in_specs=[pl.BlockSpec((CHUNK, LANES), lambda i: (rev(i), 0)),
                  pl.BlockSpec((CHUNK, LANES), lambda i: (rev(i), 0)),
                  pl.BlockSpec((CHUNK, LANES), lambda i: (rev(i), 0)), _const_spec((1, LANES))],
        out_specs=(pl.BlockSpec((CHUNK, LANES), lambda i: (rev(i), 0)), _const_spec((1, LANES))),
        scratch_shapes=[pltpu.VMEM((LANES, LANES), F32)],
        compiler_params=_params(("arbitrary",)),
    )(dc, small, ddt_raw, fgb_row)


ATT_BLOCK = 512
ATT_SCALE = HEAD_DIM ** -0.5
AUG_A = HEAD_DIM
AUG_B = HEAD_DIM + 3


def _split3(c):
    hi = c.astype(BF16).astype(F32)
    r = c - hi
    mid = r.astype(BF16).astype(F32)
    return hi, mid, (r - mid).astype(BF16).astype(F32)


def _aug(lane, first, parts=None, value=1.0):
    if parts is None:
        return jnp.where((lane >= first) & (lane < first + 3), value, 0.0)
    return (jnp.where(lane == first, parts[0], 0.0) + jnp.where(lane == first + 1, parts[1], 0.0)
            + jnp.where(lane == first + 2, parts[2], 0.0))


def _pack_pair(a0, a1, lane):
    return jnp.where(lane < HEAD_DIM, a0, pltpu.roll(a1, HEAD_DIM, 1))


def proj_qkv_heads(u, w_q, w_k, w_v, cum):
    s = u.shape[0]
    tm = _blk(s, 256)

    def body(u_ref, wq_ref, wk_ref, wv_ref, c_ref, qa_ref, ka_ref, va_ref):
        lane = _iota((tm, LANES), 1)
        lo = lane < HEAD_DIM
        uv = u_ref[...]
        qf = _mm(uv, wq_ref[...]) * ATT_SCALE
        kf = _mm(uv, wk_ref[...])
        vf = _mm(uv, wv_ref[...])
        cc = c_ref[...]
        ones_a = _aug(lane, AUG_A)
        ones_b = _aug(lane, AUG_B)
        for h in range(N_HEADS):
            j, e = divmod(h, 2)

            def head(full):
                blk = full[:, LANES * j:LANES * (j + 1)]
                if e == 1:
                    blk = pltpu.roll(blk, HEAD_DIM, 1)
                return jnp.where(lo, blk, 0.0)

            parts = _split3(cc[:, N_HEADS + h:N_HEADS + h + 1])
            qa_ref[h] = (head(qf) + _aug(lane, AUG_A, parts) + ones_b).astype(BF16)
            ka_ref[h] = (head(kf) + ones_a - _aug(lane, AUG_B, parts)).astype(BF16)
            va_ref[h] = (head(vf) + ones_a).astype(BF16)

    shp = jax.ShapeDtypeStruct((N_HEADS, s, LANES), BF16)
    hspec = pl.BlockSpec((N_HEADS, tm, LANES), lambda i: (0, i, 0))
    wspec = _const_spec((D_MODEL, ATT_WIDTH))
    return pl.pallas_call(
        body, name="proj_qkv_heads", out_shape=(shp, shp, shp), grid=(s // tm,),
        in_specs=[pl.BlockSpec((tm, D_MODEL), lambda i: (i, 0)), wspec, wspec, wspec,
                  pl.BlockSpec((tm, LANES), lambda i: (i, 0))],
        out_specs=(hspec, hspec, hspec), compiler_params=_params(("parallel",)),
    )(u, w_q, w_k, w_v, cum)


def attention_fwd(qa, ka, va):
    s = qa.shape[1]
    t = _blk(s, ATT_BLOCK)
    nq = s // t

    def body(qa_ref, ka_ref, va_ref, o_ref, qb_ref, m_scr, acc_scr, alpha_scr, p_scr, s_scr):
        qi = pl.program_id(1)
        m_scr[...] = jnp.full_like(m_scr, NEG_BIG)
        acc_scr[...] = jnp.zeros_like(acc_scr)

        def kv_rows(kb):
            return pl.ds(pl.multiple_of(kb * t, t), t)

        def softmax_block(kb, masked):
            for e in range(2):
                sc = _mm_nt(qa_ref[e], ka_ref[e, kv_rows(kb), :])
                if masked:
                    sc = jnp.where(_iota((t, t), 0) >= _iota((t, t), 1), sc, NEG_BIG)
                s_scr[e] = sc
                cmax = s_scr[e, :, 0:LANES]
                for c in range(1, t // LANES):
                    cmax = jnp.maximum(cmax, s_scr[e, :, LANES * c:LANES * (c + 1)])
                m_old = m_scr[e]
                m_new = jnp.maximum(m_old, jnp.max(cmax, axis=1, keepdims=True))
                alpha_scr[e] = jnp.exp(m_old - m_new)
                m_scr[e] = m_new
                for c in range(t // LANES):
                    cols = slice(LANES * c, LANES * (c + 1))
                    p_scr[e, :, cols] = jnp.exp(s_scr[e, :, cols] - m_new).astype(BF16)

        def accumulate(kb):
            for e in range(2):
                acc_scr[e] = alpha_scr[e] * acc_scr[e] + _mm(p_scr[e], va_ref[e, kv_rows(kb), :])

        def loop_body(kb, carry):
            accumulate(kb - 1)
            softmax_block(kb, False)
            return carry

        @pl.when(qi > 0)
        def _():
            softmax_block(0, False)

        lax.fori_loop(1, qi, loop_body, 0)

        @pl.when(qi > 0)
        def _():
            accumulate(qi - 1)
            softmax_block(qi, True)

        @pl.when(qi == 0)
        def _():
            softmax_block(0, True)

        accumulate(qi)

        lane = _iota((t, LANES), 1)
        outs = []
        for e in range(2):
            acc = acc_scr[e]
            l = acc[:, AUG_A:AUG_A + 1]
            outs.append(acc / l)
            lse = m_scr[e][:, 0:1] + jnp.log(l)
            q32 = qa_ref[e].astype(F32)
            c = q32[:, AUG_A:AUG_A + 1] + q32[:, AUG_A + 1:AUG_A + 2] + q32[:, AUG_A + 2:AUG_A + 3]
            qb = jnp.where(lane < HEAD_DIM, q32, 0.0) + _aug(lane, AUG_A, _split3(c - lse)) + _aug(lane, AUG_B)
            qb_ref[e] = qb.astype(BF16)
        o_ref[...] = _pack_pair(outs[0], outs[1], lane)

    return pl.pallas_call(
        body, name="attention_fwd",
        out_shape=(jax.ShapeDtypeStruct((s, ATT_WIDTH), F32), jax.ShapeDtypeStruct((N_HEADS, s, LANES), BF16)),
        grid=(N_PAIRS, nq),
        in_specs=[pl.BlockSpec((2, t, LANES), lambda j, qi: (j, qi, 0)),
                  pl.BlockSpec((2, s, LANES), lambda j, qi: (j, 0, 0)),
                  pl.BlockSpec((2, s, LANES), lambda j, qi: (j, 0, 0))],
        out_specs=(pl.BlockSpec((t, LANES), lambda j, qi: (qi, j)),
                   pl.BlockSpec((2, t, LANES), lambda j, qi: (j, qi, 0))),
        scratch_shapes=[pltpu.VMEM((2, t, LANES), F32), pltpu.VMEM((2, t, LANES), F32),
                        pltpu.VMEM((2, t, LANES), F32), pltpu.VMEM((2, t, t), BF16), pltpu.VMEM((2, t, t), F32)],
        compiler_params=_params(("parallel", "parallel")),
    )(qa, ka, va)


def attention_bwd(qb, ka, va, dob):
    s = qb.shape[1]
    t = _blk(s, ATT_BLOCK)
    nq = s // t

    def body(qb_ref, dob_ref, ka_ref, va_ref, dq_ref, dk_ref, dv_ref, dc_ref, dq_scr, dk_scr, dv_scr):
        j, ki = pl.program_id(0), pl.program_id(1)

        @pl.when((j == 0) & (ki == 0))
        def _():
            dc_ref[...] = jnp.zeros_like(dc_ref)

        @pl.when(ki == 0)
        def _():
            dq_scr[...] = jnp.zeros_like(dq_scr)

        dk_scr[...] = jnp.zeros_like(dk_scr)
        dv_scr[...] = jnp.zeros_like(dv_scr)

        def q_step(qblk, masked):
            rows = pl.ds(pl.multiple_of(qblk * t, t), t)
            for e in range(2):
                q = qb_ref[e, rows, :]
                do = dob_ref[e, rows, :]
                sc = _mm_nt(q, ka_ref[e])
                if masked:
                    sc = jnp.where(_iota((t, t), 0) >= _iota((t, t), 1), sc, NEG_BIG)
                p = jnp.exp(sc)
                ds_b = (p * _mm_nt(do, va_ref[e])).astype(BF16)
                dv_scr[e] += _mm_tn(p.astype(BF16), do)
                dk_scr[e] += _mm_tn(ds_b, q)
                dq_scr[e, rows, :] += _mm(ds_b, ka_ref[e])

        def loop_body(qblk, carry):
            q_step(qblk, False)
            return carry

        q_step(ki, True)
        lax.fori_loop(ki + 1, nq, loop_body, 0)

        lane = _iota((t, LANES), 1)
        dk_ref[...] = _pack_pair(dk_scr[0], dk_scr[1], lane).astype(BF16)
        dv_ref[...] = _pack_pair(dv_scr[0], dv_scr[1], lane).astype(BF16)
        rows = pl.ds(pl.multiple_of(ki * t, t), t)
        dc_ref[rows, :] -= (jnp.where(lane == N_HEADS + 2 * j, dk_scr[0][:, AUG_B:AUG_B + 1], 0.0)
                            + jnp.where(lane == N_HEADS + 2 * j + 1, dk_scr[1][:, AUG_B:AUG_B + 1], 0.0))

        @pl.when(ki == nq - 1)
        def _():
            for blk in range(nq):
                rws = pl.ds(blk * t, t)
                d0 = dq_scr[0, rws, :]
                d1 = dq_scr[1, rws, :]
                dq_ref[rws, :] = (_pack_pair(d0, d1, lane) * ATT_SCALE).astype(BF16)
                dc_ref[rws, :] += (jnp.where(lane == N_HEADS + 2 * j, d0[:, AUG_A:AUG_A + 1], 0.0)
                                   + jnp.where(lane == N_HEADS + 2 * j + 1, d1[:, AUG_A:AUG_A + 1], 0.0))

    full = pl.BlockSpec((2, s, LANES), lambda j, ki: (j, 0, 0))
    blk = pl.BlockSpec((2, t, LANES), lambda j, ki: (j, ki, 0))
    pair = pl.BlockSpec((t, LANES), lambda j, ki: (ki, j))
    wide = jax.ShapeDtypeStruct((s, ATT_WIDTH), BF16)
    return pl.pallas_call(
        body, name="attention_bwd",
        out_shape=(wide, wide, wide, jax.ShapeDtypeStruct((s, LANES), F32)),
        grid=(N_PAIRS, nq),
        in_specs=[full, full, blk, blk],
        out_specs=(pl.BlockSpec((s, LANES), lambda j, ki: (0, j)), pair, pair, _const_spec((s, LANES))),
        scratch_shapes=[pltpu.VMEM((2, s, LANES), F32), pltpu.VMEM((2, t, LANES), F32),
                        pltpu.VMEM((2, t, LANES), F32)],
        compiler_params=_params(("arbitrary", "arbitrary")),
    )(qb, dob, ka, va)


def _dsilu(z, sg):
    return sg * (1.0 + z * (1.0 - sg))


def post_mix(x, y, zs, o, za, p, tgt, ssd_g, att_g_lane, ple_g, fin_g, w_out, w_gate, w_proj):
    s = x.shape[0]
    tm = _blk(s, 128)
    half = SSD_WIDTH // N_GROUPS

    def rms_bwd(dy, yn, r):
        return r * (dy - yn * jnp.mean(dy * yn, axis=-1, keepdims=True))

    def colsum(a):
        return jnp.sum(a, axis=0, keepdims=True)

    def body(x_ref, y_ref, zs_ref, o_ref, za_ref, p_ref, t_ref, sg_ref, ag_ref, pg_ref, fg_ref,
             wo_ref, wg_ref, wp_ref,
             dh1_ref, dy_ref, dzs_ref, dob_ref, dza_ref, ycat_ref, dh1b_ref, n2b_ref, dglb_ref, dppb_ref, pb_ref,
             loss_ref, dfin_ref, dple_ref, dssd_ref, datt_ref):
        @pl.when(pl.program_id(0) == 0)
        def _():
            for r in (loss_ref, dfin_ref, dple_ref, dssd_ref, datt_ref):
                r[...] = jnp.zeros_like(r)

        lane = _iota((tm, LANES), 1)
        lo = lane < HEAD_DIM
        zs = zs_ref[...]
        sz = _sigmoid(zs)
        yv = y_ref[...]
        ys = yv * (zs * sz)
        yn, rg = [], []
        for g in range(N_GROUPS):
            seg = ys[:, half * g:half * (g + 1)]
            r = lax.rsqrt(jnp.mean(seg * seg, axis=-1, keepdims=True) + EPS)
            yn.append(seg * r)
            rg.append(r)
            ycat_ref[:, half * g:half * (g + 1)] = (yn[g] * sg_ref[:, half * g:half * (g + 1)]).astype(BF16)
        za = za_ref[...]
        sza = _sigmoid(za)
        silu_za = za * sza
        on, ra = [], []
        for jb in range(N_PAIRS):
            blk = o_ref[:, LANES * jb:LANES * (jb + 1)]
            sq = blk * blk
            ms0 = jnp.sum(jnp.where(lo, sq, 0.0), axis=1, keepdims=True) * (1.0 / HEAD_DIM)
            ms1 = jnp.sum(jnp.where(lo, 0.0, sq), axis=1, keepdims=True) * (1.0 / HEAD_DIM)
            r = jnp.where(lo, lax.rsqrt(ms0 + EPS), lax.rsqrt(ms1 + EPS))
            on.append(blk * r)
            ra.append(r)
            an = on[jb] * ag_ref[:, LANES * jb:LANES * (jb + 1)]
            ycat_ref[:, SSD_WIDTH + LANES * jb:SSD_WIDTH + LANES * (jb + 1)] = (
                an * silu_za[:, LANES * jb:LANES * (jb + 1)]).astype(BF16)
        h1 = x_ref[...] + _mm(ycat_ref[...], wo_ref[...])
        r2 = lax.rsqrt(jnp.mean(h1 * h1, axis=-1, keepdims=True) + EPS)
        n2h = h1 * r2
        n2_b = (n2h * pg_ref[...]).astype(BF16)
        gate = _sigmoid(_mm(n2_b, wg_ref[...]))
        p_b = p_ref[...].astype(BF16)
        pp = _mm(p_b, wp_ref[...])
        h2 = h1 + gate * pp
        r3 = lax.rsqrt(jnp.mean(h2 * h2, axis=-1, keepdims=True) + EPS)
        n3 = h2 * r3
        diff = n3 * fg_ref[...] - t_ref[...]
        sq = colsum(diff * diff)
        part = sq[:, 0:LANES]
        for jb in range(1, D_MODEL // LANES):
            part = part + sq[:, LANES * jb:LANES * (jb + 1)]
        loss_ref[...] += part * (0.5 / D_MODEL)
        dout = diff * (1.0 / D_MODEL)
        dfin_ref[...] += colsum(dout * n3)
        dh2 = rms_bwd(dout * fg_ref[...], n3, r3)
        dgl = dh2 * pp * gate * (1.0 - gate)
        dgl_b = dgl.astype(BF16)
        dn2 = _mm_nt(dgl_b, wg_ref[...])
        dple_ref[...] += colsum(dn2 * n2h)
        dh1 = dh2 + rms_bwd(dn2 * pg_ref[...], n2h, r2)
        dh1_b = dh1.astype(BF16)
        dycat = _mm_nt(dh1_b, wo_ref[...])
        dh1_ref[...] = dh1
        dh1b_ref[...] = dh1_b
        n2b_ref[...] = n2_b
        dglb_ref[...] = dgl_b
        dppb_ref[...] = (dh2 * gate).astype(BF16)
        pb_ref[...] = p_b
        for g in range(N_GROUPS):
            cols = slice(half * g, half * (g + 1))
            dys_g = dycat[:, cols]
            dssd_ref[:, cols] += colsum(dys_g * yn[g])
            dys = rms_bwd(dys_g * sg_ref[:, cols], yn[g], rg[g])
            dy_ref[:, cols] = dys * (zs[:, cols] * sz[:, cols])
            dzs_ref[:, cols] = (dys * yv[:, cols] * _dsilu(zs[:, cols], sz[:, cols])).astype(BF16)
        for jb in range(N_PAIRS):
            cols = slice(LANES * jb, LANES * (jb + 1))
            dya = dycat[:, SSD_WIDTH + LANES * jb:SSD_WIDTH + LANES * (jb + 1)]
            ag = ag_ref[:, cols]
            dan = dya * silu_za[:, cols]
            dza_ref[:, cols] = (dya * (on[jb] * ag) * _dsilu(za[:, cols], sza[:, cols])).astype(BF16)
            datt_ref[:, cols] += colsum(dan * on[jb])
            don = dan * ag
            q = don * on[jb]
            m0 = jnp.sum(jnp.where(lo, q, 0.0), axis=1, keepdims=True) * (1.0 / HEAD_DIM)
            m1 = jnp.sum(jnp.where(lo, 0.0, q), axis=1, keepdims=True) * (1.0 / HEAD_DIM)
            do2 = ra[jb] * (don - on[jb] * jnp.where(lo, m0, m1))
            prod = do2 * o_ref[:, cols]
            for e in range(2):
                delta = jnp.sum(jnp.where(lo, prod, 0.0) if e == 0 else jnp.where(lo, 0.0, prod),
                                axis=1, keepdims=True)
                base = jnp.where(lo, do2 if e == 0 else pltpu.roll(do2, HEAD_DIM, 1), 0.0)
                dob_ref[2 * jb + e] = (base - _aug(lane, AUG_A, _split3(delta))).astype(BF16)

    def rows(n, dtype=None):
        return pl.BlockSpec((tm, n), lambda i: (i, 0))

    def out(n, dtype):
        return jax.ShapeDtypeStruct((s, n), dtype)

    vec = _const_spec((1, D_MODEL))
    vshape = jax.ShapeDtypeStruct((1, D_MODEL), F32)
    return pl.pallas_call(
        body, name="post_mix",
        out_shape=(out(D_MODEL, F32), out(SSD_WIDTH, F32), out(SSD_WIDTH, BF16),
                   jax.ShapeDtypeStruct((N_HEADS, s, LANES), BF16),
                   out(ATT_WIDTH, BF16), out(D_INNER, BF16), out(D_MODEL, BF16), out(D_MODEL, BF16),
                   out(D_MODEL, BF16), out(D_MODEL, BF16), out(PLE_DIM, BF16),
                   jax.ShapeDtypeStruct((1, LANES), F32), vshape, vshape, vshape, vshape),
        grid=(s // tm,),
        in_specs=[rows(D_MODEL), rows(SSD_WIDTH), rows(SSD_WIDTH), rows(ATT_WIDTH), rows(ATT_WIDTH),
                  rows(PLE_DIM), rows(D_MODEL), vec, vec, vec, vec,
                  _const_spec((D_INNER, D_MODEL)), _const_spec((D_MODEL, D_MODEL)), _const_spec((PLE_DIM, D_MODEL))],
        out_specs=(rows(D_MODEL), rows(SSD_WIDTH), rows(SSD_WIDTH),
                   pl.BlockSpec((N_HEADS, tm, LANES), lambda i: (0, i, 0)), rows(ATT_WIDTH),
                   rows(D_INNER), rows(D_MODEL), rows(D_MODEL), rows(D_MODEL), rows(D_MODEL), rows(PLE_DIM),
                   _const_spec((1, LANES)), vec, vec, vec, vec),
        compiler_params=_params(("arbitrary",)),
    )(x, y, zs, o, za, p, tgt, ssd_g, att_g_lane, ple_g, fin_g, w_out, w_gate, w_proj)


def in_proj_bwd(dsegs, wsegs, x, g, dh1):
    s = x.shape[0]
    tm = _blk(s, 256)
    nseg = len(dsegs)

    def body(*refs):
        d_refs = refs[:nseg]
        w_refs = refs[nseg:2 * nseg]
        x_ref, g_ref, dh1_ref, dx_ref, dg_ref = refs[2 * nseg:]

        @pl.when(pl.program_id(0) == 0)
        def _():
            dg_ref[...] = jnp.zeros_like(dg_ref)

        du = _mm_nt(d_refs[0][...], w_refs[0][...])
        for k in range(1, nseg):
            du = du + _mm_nt(d_refs[k][...], w_refs[k][...])
        xv = x_ref[...]
        r = lax.rsqrt(jnp.mean(xv * xv, axis=-1, keepdims=True) + EPS)
        xh = xv * r
        dg_ref[...] += jnp.sum(du * xh, axis=0, keepdims=True)
        dxh = du * g_ref[...]
        dx_ref[...] = r * (dxh - xh * jnp.mean(dxh * xh, axis=-1, keepdims=True)) + dh1_ref[...]

    rows = lambda n: pl.BlockSpec((tm, n), lambda i: (i, 0))
    return pl.pallas_call(
        body, name="in_proj_bwd",
        out_shape=(jax.ShapeDtypeStruct((s, D_MODEL), F32), jax.ShapeDtypeStruct((1, D_MODEL), F32)),
        grid=(s // tm,),
        in_specs=([rows(d.shape[1]) for d in dsegs] + [_const_spec(w.shape) for w in wsegs]
                  + [rows(D_MODEL), _const_spec((1, D_MODEL)), rows(D_MODEL)]),
        out_specs=(rows(D_MODEL), _const_spec((1, D_MODEL))),
        compiler_params=_params(("arbitrary",)),
    )(*dsegs, *wsegs, x, g, dh1)


SMALL_NAMES = ("norm_g", "conv_b", "dt_bias", "a_log", "d_skip", "ssd_norm_g", "fg_bias", "att_norm_g",
               "ple_norm_g", "final_norm_g")
SMALL_SIZES = (1024, 1536, 16, 16, 16, 1024, 16, 64, 1024, 1024)
CONV_W_SIZE = CONV_WIDTH * CONV_CH


def _pack_small(vals):
    flat = jnp.concatenate([v.reshape(-1).astype(F32) for v in vals])
    flat = jnp.pad(flat, (0, SMALL_ROWS * LANES - flat.shape[0]))
    return flat.reshape(SMALL_ROWS, LANES)


def _unpack_small(pack, shapes):
    flat = pack.reshape(-1)
    out, off = [], 0
    for n, shp in zip(SMALL_SIZES, shapes):
        out.append(flat[off:off + n].reshape(shp))
        off += n
    return out


def _row128(v16, offset=0):
    return jnp.pad(v16.reshape(1, N_HEADS).astype(F32), ((0, 0), (offset, LANES - N_HEADS - offset)))


def local_step(x, p, tgt, w_in, w_out, w_gate, w_proj, conv_w, norm_g, conv_b, dt_bias, a_log, d_skip,
               ssd_norm_g, fg_bias, att_norm_g, ple_norm_g, final_norm_g):
    c0, c1, c2, c3, c4, c5, c6, c7 = 0, 1024, 2560, 2576, 3600, 4624, 5648, 6672
    w_zs, w_xbc, w_dt = w_in[:, c0:c1], w_in[:, c1:c2], w_in[:, c2:c3]
    w_za, w_q, w_k, w_v, w_f = w_in[:, c3:c4], w_in[:, c4:c5], w_in[:, c5:c6], w_in[:, c6:c7], w_in[:, c7:]
    w_small = jnp.concatenate([w_dt, w_f, jnp.zeros((D_MODEL, LANES - 2 * N_HEADS), BF16)], axis=1)

    dtb_row = _row128(dt_bias)
    a_row = _row128(-jnp.exp(a_log.astype(F32)))
    fgb_row = _row128(fg_bias, N_HEADS)
    dskip_lane = jnp.repeat(d_skip.astype(F32), HEAD_DIM).reshape(1, SSD_WIDTH)
    att_g_lane = jnp.tile(att_norm_g.astype(F32), N_HEADS).reshape(1, ATT_WIDTH)
    row = lambda v: v.reshape(1, -1).astype(F32)

    u = rms_prenorm(x, row(norm_g))
    zs = matmul_rows(u, w_zs, F32, "proj_z_ssd")
    xbc = matmul_rows(u, w_xbc, F32, "proj_xbc")
    za = matmul_rows(u, w_za, F32, "proj_z_att")
    small = matmul_rows(u, w_small, F32, "proj_small")
    cum = forget_cumsum(small, fgb_row)
    qa, ka, va = proj_qkv_heads(u, w_q, w_k, w_v, cum)
    pre, xc = conv_fwd(xbc, conv_w, row(conv_b))
    y, states = ssd_fwd(xc, small, dtb_row, a_row, dskip_lane)
    o, qb = attention_fwd(qa, ka, va)
    (dh1, dy, dzs, dob, dza, ycat, dh1_b, n2_b, dgl_b, dpp_b, p_b,
     loss_l, dfin, dple, dssd_g, datt_lane) = post_mix(
        x, y, zs, o, za, p, tgt, row(ssd_norm_g), att_g_lane, row(ple_norm_g), row(final_norm_g),
        w_out, w_gate, w_proj)
    dq, dk, dv, dc = attention_bwd(qb, ka, va, dob)
    dxc, ddt_raw, da, ddtb, ddsk_lane = ssd_bwd(xc, small, states, dy, dtb_row, a_row, dskip_lane)
    dsmall, dfgb = forget_bwd(dc, small, ddt_raw, fgb_row)
    dxbc, dconv_w8, dconv_b = conv_bwd(xbc, pre, dxc, conv_w)
    dsegs = [dzs, dxbc, dza, dq, dk, dv, dsmall]
    wsegs = [w_zs, w_xbc, w_za, w_q, w_k, w_v, w_small]
    dx, dnorm_g = in_proj_bwd(dsegs, wsegs, x, row(norm_g), dh1)
    dws = [matmul_tn(u, d, "dw_in_%d" % i) for i, d in enumerate(dsegs)]
    dw_in = jnp.concatenate([dws[0], dws[1], dws[6][:, :N_HEADS], dws[2], dws[3], dws[4], dws[5],
                             dws[6][:, N_HEADS:2 * N_HEADS]], axis=1)
    dw_out = matmul_tn(ycat, dh1_b, "dw_out")
    dw_gate = matmul_tn(n2_b, dgl_b, "dw_gate")
    dw_proj = matmul_tn(p_b, dpp_b, "dw_proj")
    small_grads = [
        dnorm_g, dconv_b, ddtb[0, :N_HEADS], (da * a_row)[0, :N_HEADS],
        ddsk_lane.reshape(N_HEADS, HEAD_DIM).sum(axis=1), dssd_g, dfgb[0, N_HEADS:2 * N_HEADS],
        datt_lane.reshape(N_HEADS, HEAD_DIM).sum(axis=0), dple, dfin]
    loss = jnp.sum(loss_l)
    return loss, dx, dw_in, dw_out, dw_gate, dw_proj, dconv_w8[:CONV_WIDTH], small_grads


def kernel(x, p, norm_g, w_in, conv_w, conv_b, dt_bias, a_log, d_skip, ssd_norm_g, fg_bias, att_norm_g, w_out, ple_norm_g, w_ple_gate, w_ple_proj, final_norm_g, loss_target, m_norm_g, m_w_in, m_conv_w, m_conv_b, m_dt_bias, m_a_log, m_d_skip, m_ssd_norm_g, m_fg_bias, m_att_norm_g, m_w_out, m_ple_norm_g, m_w_ple_gate, m_w_ple_proj, m_final_norm_g, v_norm_g, v_w_in, v_conv_w, v_conv_b, v_dt_bias, v_a_log, v_d_skip, v_ssd_norm_g, v_fg_bias, v_att_norm_g, v_w_out, v_ple_norm_g, v_w_ple_gate, v_w_ple_proj, v_final_norm_g):
    chip = 2 * lax.axis_index("x") + lax.axis_index("y")
    core = lax.axis_index("c")

    big_w = [w_in[0], w_out[0], w_ple_gate[0], w_ple_proj[0]]
    wi_all, wo_all, wg_all, wp_all, conv_all = gather_weights([a.astype(BF16) for a in big_w], conv_w[0])
    w_in_f = jnp.concatenate([wi_all[j] for j in range(N_CHIPS)], axis=1)
    w_out_f = wo_all.reshape(D_INNER, D_MODEL)
    w_gate_f = wg_all.reshape(D_MODEL, D_MODEL)
    w_proj_f = jnp.concatenate([wp_all[j] for j in range(N_CHIPS)], axis=1)
    conv_w_f = jnp.concatenate([conv_all[j] for j in range(N_CHIPS)], axis=1)

    smalls_w = [norm_g, conv_b, dt_bias, a_log, d_skip, ssd_norm_g, fg_bias, att_norm_g, ple_norm_g, final_norm_g]
    loss_l, dx, dw_in, dw_out, dw_gate, dw_proj, dconv_w, small_grads = local_step(
        x[0], p[0, 0], loss_target[0], w_in_f, w_out_f, w_gate_f, w_proj_f, conv_w_f,
        *[a.reshape(-1) for a in smalls_w])
    loss = lax.psum(loss_l, ("x", "y", "c"))

    gs = [jnp.stack([dw_in[:, 1672 * j:1672 * (j + 1)] for j in range(N_CHIPS)]),
          dw_out.reshape(N_CHIPS, 512, D_MODEL), dw_gate.reshape(N_CHIPS, 256, D_MODEL),
          jnp.stack([dw_proj[:, 256 * j:256 * (j + 1)] for j in range(N_CHIPS)])]
    pres = add_halves(core.reshape(1).astype(jnp.int32), gs, halves_to_sibling(gs))
    *parts, smalls = scatter_halves(pres, _pack_small(list(small_grads) + [dconv_w]))
    g_big = gather_halves(sum_parts(parts))

    d_big, m_big, v_big = adamw_big(
        g_big, big_w, [m_w_in[0], m_w_out[0], m_w_ple_gate[0], m_w_ple_proj[0]],
        [v_w_in[0], v_w_out[0], v_w_ple_gate[0], v_w_ple_proj[0]])
    smalls_m = [m_norm_g, m_conv_b, m_dt_bias, m_a_log, m_d_skip, m_ssd_norm_g, m_fg_bias, m_att_norm_g,
                m_ple_norm_g, m_final_norm_g]
    smalls_v = [v_norm_g, v_conv_b, v_dt_bias, v_a_log, v_d_skip, v_ssd_norm_g, v_fg_bias, v_att_norm_g,
                v_ple_norm_g, v_final_norm_g]
    g_sm, d_sm, m_sm, v_sm = adamw_small(smalls, _pack_small(smalls_w), _pack_small(smalls_m), _pack_small(smalls_v))
    n_small = sum(SMALL_SIZES)
    g_conv_full = g_sm.reshape(-1)[n_small:n_small + CONV_W_SIZE].reshape(CONV_WIDTH, CONV_CH)
    g_conv = lax.dynamic_slice_in_dim(g_conv_full, chip * 384, 384, axis=1)
    d_conv, m_conv, v_conv = adamw_whole(g_conv, conv_w[0], m_conv_w[0], v_conv_w[0], "adamw_conv")

    shapes = [a.shape for a in smalls_w]
    outs = []
    for big, conv, sm in ((g_big, g_conv, g_sm), (d_big, d_conv, d_sm), (m_big, m_conv, m_sm), (v_big, v_conv, v_sm)):
        b_in, b_out, b_gate, b_proj = [a[None] for a in big]
        s_norm, s_convb, s_dtb, s_alog, s_dsk, s_ssdg, s_fgb, s_attg, s_pleg, s_fin = _unpack_small(sm, shapes)
        outs.extend([s_norm, b_in, conv[None], s_convb, s_dtb, s_alog, s_dsk, s_ssdg, s_fgb, s_attg, b_out, s_pleg,
                     b_gate, b_proj, s_fin])
    return (loss, dx[None], *outs)
```

```python
import functools

import jax
import jax.numpy as jnp
from jax import lax
from jax.experimental import pallas as pl
from jax.experimental.pallas import tpu as pltpu

F32 = jnp.float32
BF16 = jnp.bfloat16

D_MODEL = 1024
SSD_WIDTH = 1024
ATT_WIDTH = 1024
N_HEADS = 16
HEAD_DIM = 64
N_GROUPS = 2
D_STATE = 128
CONV_CH = 1536
CONV_WIDTH = 4
CHUNK = 128
PLE_DIM = 256
D_INNER = 2048
EPS = 1e-6
IN_COLS = 6688
N_CHIPS = 4
N_DEV = 8
LANES = 128
N_PAIRS = 8

ADAM_LR = 0.001
ADAM_B1 = 0.9
ADAM_B2 = 0.999
ADAM_EPS = 1e-08
ADAM_WD = 0.01
ADAM_STEP = 10

SMALL_ROWS = 96

NEG_BIG = -1e30
VMEM_LIMIT = 56 * 1024 * 1024

MESH = pl.DeviceIdType.MESH
ANY = pl.BlockSpec(memory_space=pl.ANY)


def _mm(a, b):
    return jnp.dot(a, b, preferred_element_type=F32)


def _mm_nt(a, b):
    return lax.dot_general(a, b, (((1,), (1,)), ((), ())), preferred_element_type=F32)


def _mm_tn(a, b):
    return lax.dot_general(a, b, (((0,), (0,)), ((), ())), preferred_element_type=F32)


def _mm_exact(a, b):
    return jnp.dot(a, b, preferred_element_type=F32, precision=lax.Precision.HIGHEST)


def _softplus(x):
    return jnp.maximum(x, 0.0) + jnp.log1p(jnp.exp(-jnp.abs(x)))


def _sigmoid(x):
    return jax.nn.sigmoid(x)


def _iota(shape, dim):
    return lax.broadcasted_iota(jnp.int32, shape, dim)


def _params(sem=None):
    return pltpu.CompilerParams(dimension_semantics=sem, vmem_limit_bytes=VMEM_LIMIT)


def _blk(n, pref):
    return min(n, pref)


def _const_spec(shape):
    nd = len(shape)
    return pl.BlockSpec(shape, lambda *_: (0,) * nd)


def _chip_peers():
    x, y, c = lax.axis_index("x"), lax.axis_index("y"), lax.axis_index("c")
    return x, y, c, [(1 - x, y, c), (x, 1 - y, c), (1 - x, 1 - y, c)]


def _half(rows, c):
    h = rows // 2
    return pl.ds(pl.multiple_of(c * h, 8), h)


def _sems(n):
    return [pltpu.SemaphoreType.DMA((n,)), pltpu.SemaphoreType.DMA((n,))]


def gather_weights(shards, conv_s):
    n = len(shards)

    def body(*refs):
        ins, conv_in = refs[:n], refs[n]
        outs, conv_out = refs[n + 1:2 * n + 1], refs[2 * n + 1]
        ssem1, rsem1, ssem2, rsem2, c_ssem, c_rsem = refs[2 * n + 2:]
        x, y, c, peers = _chip_peers()
        me = 2 * x + y
        sibling = (x, y, 1 - c)
        first, small = [], []
        for k, peer in enumerate(peers):
            for i in range(n):
                h = _half(ins[i].shape[0], c)
                first.append(pltpu.make_async_remote_copy(
                    src_ref=ins[i].at[h], dst_ref=outs[i].at[me, h], send_sem=ssem1.at[n * k + i],
                    recv_sem=rsem1.at[n * k + i], device_id=peer, device_id_type=MESH))
            small.append(pltpu.make_async_remote_copy(
                src_ref=conv_in, dst_ref=conv_out.at[me], send_sem=c_ssem.at[k], recv_sem=c_rsem.at[k],
                device_id=peer, device_id_type=MESH))
        for cp in first + small:
            cp.start()
        passed = []
        for k, peer in enumerate(peers):
            chip = 2 * peer[0] + peer[1]
            for i in range(n):
                h = _half(ins[i].shape[0], c)
                first[n * k + i].wait_recv()
                fwd = pltpu.make_async_remote_copy(
                    src_ref=outs[i].at[chip, h], dst_ref=outs[i].at[chip, h], send_sem=ssem2.at[n * k + i],
                    recv_sem=rsem2.at[n * k + i], device_id=sibling, device_id_type=MESH)
                fwd.start()
                passed.append(fwd)
        for cp in passed:
            cp.wait_recv()
        for cp in first + passed:
            cp.wait_send()
        for cp in small:
            cp.wait()

    return pl.pallas_call(
        body, name="gather_weights",
        out_shape=tuple(jax.ShapeDtypeStruct((N_CHIPS,) + a.shape, a.dtype) for a in list(shards) + [conv_s]),
        in_specs=[ANY] * (n + 1), out_specs=(ANY,) * (n + 1),
        scratch_shapes=_sems(3 * n) + _sems(3 * n) + _sems(3),
    )(*shards, conv_s)


def halves_to_sibling(gs):
    n = len(gs)

    def body(*refs):
        ins, outs = refs[:n], refs[n:2 * n]
        ssem, rsem = refs[2 * n:]
        x, y, c = lax.axis_index("x"), lax.axis_index("y"), lax.axis_index("c")
        copies = []
        for i in range(n):
            for j in range(N_CHIPS):
                copies.append(pltpu.make_async_remote_copy(
                    src_ref=ins[i].at[j, _half(ins[i].shape[1], 1 - c)], dst_ref=outs[i].at[j],
                    send_sem=ssem.at[N_CHIPS * i + j], recv_sem=rsem.at[N_CHIPS * i + j],
                    device_id=(x, y, 1 - c), device_id_type=MESH))
        for cp in copies:
            cp.start()
        for cp in copies:
            cp.wait()

    return pl.pallas_call(
        body, name="halves_to_sibling",
        out_shape=tuple(jax.ShapeDtypeStruct((N_CHIPS, g.shape[1] // 2, g.shape[2]), F32) for g in gs),
        in_specs=[ANY] * n, out_specs=(ANY,) * n, scratch_shapes=_sems(N_CHIPS * n),
    )(*gs)


RED_GRID = 8


def add_halves(core, gs, rbs):
    n = len(gs)

    def body(c_ref, *refs):
        for i in range(n):
            refs[2 * n + i][...] = refs[i][...] + refs[n + i][...]

    def blk(g):
        return (1, g.shape[1] // 2 // RED_GRID, g.shape[2])

    grid_spec = pltpu.PrefetchScalarGridSpec(
        num_scalar_prefetch=1, grid=(N_CHIPS, RED_GRID),
        in_specs=([pl.BlockSpec(blk(g), lambda j, b, c_ref: (j, c_ref[0] * RED_GRID + b, 0)) for g in gs]
                  + [pl.BlockSpec(blk(g), lambda j, b, c_ref: (j, b, 0)) for g in gs]),
        out_specs=[pl.BlockSpec(blk(g), lambda j, b, c_ref: (j, b, 0)) for g in gs])
    return pl.pallas_call(
        body, name="add_halves", grid_spec=grid_spec,
        out_shape=tuple(jax.ShapeDtypeStruct(r.shape, F32) for r in rbs),
        compiler_params=_params(("parallel", "parallel")),
    )(core, *gs, *rbs)


def scatter_halves(pres, small):
    n = len(pres)

    def body(*refs):
        ins, s_ref = refs[:n], refs[n]
        outs, smalls_ref = refs[n + 1:2 * n + 1], refs[2 * n + 1]
        ssem, rsem, s_ssem, s_rsem, lsem = refs[2 * n + 2:]
        x, y, c, peers = _chip_peers()
        me = 2 * x + y
        dev = 4 * x + 2 * y + c
        local = [pltpu.make_async_copy(ins[i].at[me], outs[i].at[me], lsem.at[i]) for i in range(n)]
        local.append(pltpu.make_async_copy(s_ref, smalls_ref.at[dev], lsem.at[n]))
        for cp in local:
            cp.start()
        remote = []
        for k, peer in enumerate(peers):
            dst_chip = 2 * peer[0] + peer[1]
            for i in range(n):
                remote.append(pltpu.make_async_remote_copy(
                    src_ref=ins[i].at[dst_chip], dst_ref=outs[i].at[me], send_sem=ssem.at[n * k + i],
                    recv_sem=rsem.at[n * k + i], device_id=peer, device_id_type=MESH))
        for k in range(1, N_DEV):
            fx, fy, fc = (k >> 2) & 1, (k >> 1) & 1, k & 1
            peer = ((1 - x) if fx else x, (1 - y) if fy else y, (1 - c) if fc else c)
            remote.append(pltpu.make_async_remote_copy(
                src_ref=s_ref, dst_ref=smalls_ref.at[dev], send_sem=s_ssem.at[k - 1], recv_sem=s_rsem.at[k - 1],
                device_id=peer, device_id_type=MESH))
        for cp in remote:
            cp.start()
        for cp in remote:
            cp.wait()
        for cp in local:
            cp.wait()

    return pl.pallas_call(
        body, name="scatter_halves",
        out_shape=tuple([jax.ShapeDtypeStruct(a.shape, F32) for a in pres]
                        + [jax.ShapeDtypeStruct((N_DEV,) + small.shape, F32)]),
        in_specs=[ANY] * (n + 1), out_specs=(ANY,) * (n + 1),
        scratch_shapes=_sems(3 * n) + _sems(N_DEV - 1) + [pltpu.SemaphoreType.DMA((n + 1,))],
    )(*pres, small)


def sum_parts(parts):
    n = len(parts)

    def body(*refs):
        for i in range(n):
            p_ref = refs[i]
            refs[n + i][...] = ((p_ref[0] + p_ref[1]) + p_ref[2]) + p_ref[3]

    def rows(p):
        return p.shape[1] // RED_GRID

    return pl.pallas_call(
        body, name="sum_parts",
        out_shape=tuple(jax.ShapeDtypeStruct(p.shape[1:], F32) for p in parts),
        grid=(RED_GRID,),
        in_specs=[pl.BlockSpec((N_CHIPS, rows(p), p.shape[2]), lambda b: (0, b, 0)) for p in parts],
        out_specs=tuple(pl.BlockSpec((rows(p), p.shape[2]), lambda b: (b, 0)) for p in parts),
        compiler_params=_params(("parallel",)),
    )(*parts)


def swap_halves(reds):
    n = len(reds)

    def body(*refs):
        ins, outs = refs[:n], refs[n:2 * n]
        ssem, rsem = refs[2 * n:]
        x, y, c = lax.axis_index("x"), lax.axis_index("y"), lax.axis_index("c")
        copies = [pltpu.make_async_remote_copy(
            src_ref=ins[i], dst_ref=outs[i], send_sem=ssem.at[i], recv_sem=rsem.at[i],
            device_id=(x, y, 1 - c), device_id_type=MESH) for i in range(n)]
        for cp in copies:
            cp.start()
        for cp in copies:
            cp.wait()

    return pl.pallas_call(
        body, name="swap_halves",
        out_shape=tuple(jax.ShapeDtypeStruct(r.shape, F32) for r in reds),
        in_specs=[ANY] * n, out_specs=(ANY,) * n, scratch_shapes=_sems(n),
    )(*reds)


def _adamw(w, g, m, v):
    m = ADAM_B1 * m + (1.0 - ADAM_B1) * g
    v = ADAM_B2 * v + (1.0 - ADAM_B2) * (g * g)
    m_hat = m / (1.0 - ADAM_B1 ** ADAM_STEP)
    v_hat = v / (1.0 - ADAM_B2 ** ADAM_STEP)
    delta = -ADAM_LR * (m_hat / (jnp.sqrt(v_hat) + ADAM_EPS) + ADAM_WD * w)
    return delta, m, v


def adamw_big(core, mine, theirs, ws, ms, vs):
    n = len(ws)
    per_half = RED_GRID // 2

    def body(c_ref, *refs):
        own = (pl.program_id(0) // per_half) == c_ref[0]
        for i in range(n):
            g = jnp.where(own, refs[i][...], refs[n + i][...])
            d, mn, vn = _adamw(refs[2 * n + i][...], g, refs[3 * n + i][...], refs[4 * n + i][...])
            refs[5 * n + i][...] = g
            refs[6 * n + i][...] = d
            refs[7 * n + i][...] = mn
            refs[8 * n + i][...] = vn

    def blk(w):
        return (w.shape[0] // RED_GRID, w.shape[1])

    halves = [pl.BlockSpec(blk(w), lambda b, c_ref: (b % per_half, 0)) for w in ws]
    whole = [pl.BlockSpec(blk(w), lambda b, c_ref: (b, 0)) for w in ws]
    shapes = [jax.ShapeDtypeStruct(w.shape, F32) for w in ws]
    grid_spec = pltpu.PrefetchScalarGridSpec(
        num_scalar_prefetch=1, grid=(RED_GRID,), in_specs=halves * 2 + whole * 3, out_specs=whole * 4)
    outs = pl.pallas_call(
        body, name="adamw_big", out_shape=tuple(shapes * 4), grid_spec=grid_spec,
        compiler_params=_params(("parallel",)),
    )(core, *mine, *theirs, *ws, *ms, *vs)
    return outs[:n], outs[n:2 * n], outs[2 * n:3 * n], outs[3 * n:]


def adamw_whole(g, w, m, v, name):
    def body(g_ref, w_ref, m_ref, v_ref, d_out, m_out, v_out):
        d, mn, vn = _adamw(w_ref[...], g_ref[...], m_ref[...], v_ref[...])
        d_out[...] = d
        m_out[...] = mn
        v_out[...] = vn

    shp = jax.ShapeDtypeStruct(g.shape, F32)
    return pl.pallas_call(body, name=name, out_shape=(shp,) * 3)(g, w, m, v)


def adamw_small(smalls, w, m, v):
    def body(s_ref, w_ref, m_ref, v_ref, g_out, d_out, m_out, v_out):
        g = s_ref[0]
        for k in range(1, N_DEV):
            g = g + s_ref[k]
        d, mn, vn = _adamw(w_ref[...], g, m_ref[...], v_ref[...])
        g_out[...] = g
        d_out[...] = d
        m_out[...] = mn
        v_out[...] = vn

    shp = jax.ShapeDtypeStruct((SMALL_ROWS, LANES), F32)
    return pl.pallas_call(body, name="adamw_small", out_shape=(shp,) * 4)(smalls, w, m, v)


def rms_prenorm(x, g):
    s = x.shape[0]
    tm = _blk(s, 512)

    def body(x_ref, g_ref, u_ref):
        xv = x_ref[...]
        r = lax.rsqrt(jnp.mean(xv * xv, axis=-1, keepdims=True) + EPS)
        u_ref[...] = (xv * r * g_ref[...]).astype(BF16)

    return pl.pallas_call(
        body, name="rms_prenorm", out_shape=jax.ShapeDtypeStruct(x.shape, BF16), grid=(s // tm,),
        in_specs=[pl.BlockSpec((tm, D_MODEL), lambda i: (i, 0)), _const_spec((1, D_MODEL))],
        out_specs=pl.BlockSpec((tm, D_MODEL), lambda i: (i, 0)), compiler_params=_params(("parallel",)),
    )(x, g)


def matmul_rows(a, w, out_dtype, name):
    s, k = a.shape
    n = w.shape[1]
    tm = _blk(s, 512)

    def body(a_ref, w_ref, o_ref):
        o_ref[...] = _mm(a_ref[...], w_ref[...]).astype(out_dtype)

    return pl.pallas_call(
        body, name=name, out_shape=jax.ShapeDtypeStruct((s, n), out_dtype), grid=(s // tm,),
        in_specs=[pl.BlockSpec((tm, k), lambda i: (i, 0)), _const_spec((k, n))],
        out_specs=pl.BlockSpec((tm, n), lambda i: (i, 0)), compiler_params=_params(("parallel",)),
    )(a, w)


def matmul_tn(a, b, name):
    s, m = a.shape
    n = b.shape[1]
    tk = _blk(s, 2048)
    tn = _blk(n, 512)

    def body(a_ref, b_ref, o_ref):
        @pl.when(pl.program_id(1) == 0)
        def _():
            o_ref[...] = jnp.zeros_like(o_ref)

        o_ref[...] += _mm_tn(a_ref[...], b_ref[...])

    return pl.pallas_call(
        body, name=name, out_shape=jax.ShapeDtypeStruct((m, n), F32), grid=(n // tn, s // tk),
        in_specs=[pl.BlockSpec((tk, m), lambda j, i: (i, 0)), pl.BlockSpec((tk, tn), lambda j, i: (i, j))],
        out_specs=pl.BlockSpec((m, tn), lambda j, i: (0, j)),
        compiler_params=_params(("parallel", "arbitrary")),
    )(a, b)


def conv_fwd(xbc, w, b):
    s = xbc.shape[0]
    tm = _blk(s, 256)

    def body(x_ref, t_ref, w_ref, b_ref, pre_ref, act_ref):
        i = pl.program_id(0)
        cur = x_ref[...]
        tail = jnp.where(i > 0, t_ref[...], 0.0)
        wv = w_ref[...]
        acc = cur * wv[3:4, :] + b_ref[...]
        head = cur[0:8, :] * wv[3:4, :] + b_ref[...]
        row8 = _iota((8, CONV_CH), 0)
        for sh in range(1, CONV_WIDTH):
            wk = wv[3 - sh:4 - sh, :]
            acc = acc + pltpu.roll(cur, sh, 0) * wk
            first = jnp.where(row8 < sh, pltpu.roll(tail, sh, 0), pltpu.roll(cur[0:8, :], sh, 0))
            head = head + first * wk
        pre_ref[...] = acc
        act_ref[...] = acc * _sigmoid(acc)
        pre_ref[0:8, :] = head
        act_ref[0:8, :] = head * _sigmoid(head)

    shp = jax.ShapeDtypeStruct(xbc.shape, F32)
    rows = pl.BlockSpec((tm, CONV_CH), lambda i: (i, 0))
    return pl.pallas_call(
        body, name="conv_fwd", out_shape=(shp, shp), grid=(s // tm,),
        in_specs=[rows, pl.BlockSpec((8, CONV_CH), lambda i: (jnp.maximum(i * (tm // 8) - 1, 0), 0)),
                  _const_spec((CONV_WIDTH, CONV_CH)), _const_spec((1, CONV_CH))],
        out_specs=(rows, rows), compiler_params=_params(("parallel",)),
    )(xbc, xbc, w, b)


def conv_bwd(xbc, pre, dact, w):
    s = xbc.shape[0]
    tm = _blk(s, 256)
    nb = s // tm

    def dsilu(p):
        sg = _sigmoid(p)
        return sg * (1.0 + p * (1.0 - sg))

    def body(x_ref, xt_ref, p_ref, pn_ref, d_ref, dn_ref, w_ref, dx_ref, dw_ref, db_ref):
        i = pl.program_id(0)

        @pl.when(i == 0)
        def _():
            dw_ref[...] = jnp.zeros_like(dw_ref)
            db_ref[...] = jnp.zeros_like(db_ref)

        wv = w_ref[...]
        dpre = d_ref[...] * dsilu(p_ref[...])
        dnext = jnp.where(i < nb - 1, dn_ref[...] * dsilu(pn_ref[...]), 0.0)
        cur = x_ref[...]
        tail = jnp.where(i > 0, xt_ref[...], 0.0)
        row8 = _iota((8, CONV_CH), 0)
        dx = dpre * wv[3:4, :]
        last = dpre[tm - 8:tm, :] * wv[3:4, :]
        db_ref[...] += jnp.sum(dpre, axis=0, keepdims=True)
        dws = [jnp.sum(dpre * cur, axis=0, keepdims=True)]
        for sh in range(1, CONV_WIDTH):
            wk = wv[3 - sh:4 - sh, :]
            dx = dx + pltpu.roll(dpre, tm - sh, 0) * wk
            nxt = jnp.where(row8 >= 8 - sh, pltpu.roll(dnext, 8 - sh, 0), pltpu.roll(dpre[tm - 8:tm, :], 8 - sh, 0))
            last = last + nxt * wk
            xs = pltpu.roll(cur, sh, 0)
            first = jnp.where(row8 < sh, pltpu.roll(tail, sh, 0), xs[0:8, :])
            dws.append(jnp.sum(dpre * xs, axis=0, keepdims=True)
                       + jnp.sum(dpre[0:8, :] * (first - xs[0:8, :]), axis=0, keepdims=True))
        dx_ref[...] = dx.astype(BF16)
        dx_ref[tm - 8:tm, :] = last.astype(BF16)
        for sh in range(CONV_WIDTH):
            dw_ref[3 - sh:4 - sh, :] += dws[sh]

    rows = pl.BlockSpec((tm, CONV_CH), lambda i: (i, 0))
    prev8 = pl.BlockSpec((8, CONV_CH), lambda i: (jnp.maximum(i * (tm // 8) - 1, 0), 0))
    next8 = pl.BlockSpec((8, CONV_CH), lambda i: (jnp.minimum((i + 1) * (tm // 8), s // 8 - 1), 0))
    return pl.pallas_call(
        body, name="conv_bwd",
        out_shape=(jax.ShapeDtypeStruct(xbc.shape, BF16), jax.ShapeDtypeStruct((8, CONV_CH), F32),
                   jax.ShapeDtypeStruct((1, CONV_CH), F32)),
        grid=(nb,),
        in_specs=[rows, prev8, rows, next8, rows, next8, _const_spec((CONV_WIDTH, CONV_CH))],
        out_specs=(rows, _const_spec((8, CONV_CH)), _const_spec((1, CONV_CH))),
        compiler_params=_params(("arbitrary",)),
    )(xbc, xbc, pre, pre, dact, dact, w)


def _pair_lanes(mat, j, lane):
    return jnp.where(lane < HEAD_DIM, mat[:, 2 * j:2 * j + 1], mat[:, 2 * j + 1:2 * j + 2])


def _ssd_chunk_prelude(sm, dtb, a_row, lane, sub):
    raw = sm + dtb
    head_lane = lane < N_HEADS
    dt = jnp.where(head_lane, _softplus(raw), 0.0)
    sig = jnp.where(head_lane, _sigmoid(raw), 0.0)
    tri = (lane <= sub).astype(F32)
    acs = _mm_exact(tri, dt * a_row)
    return dt, sig, acs, acs.T


def ssd_fwd(xc, small, dtb_row, a_row, dskip_lane):
    s = xc.shape[0]
    nc = s // CHUNK

    def body(xc_ref, sm_ref, dtb_ref, a_ref, dsk_ref, y_ref, hs_ref, h_scr):
        c = pl.program_id(0)

        @pl.when(c == 0)
        def _():
            h_scr[...] = jnp.zeros_like(h_scr)

        lane = _iota((CHUNK, LANES), 1)
        sub = _iota((CHUNK, LANES), 0)
        causal = lane <= sub
        dt, _, acs, acs_t = _ssd_chunk_prelude(sm_ref[...], dtb_ref[...], a_ref[...], lane, sub)
        last = acs[CHUNK - 1:CHUNK, :]
        e_all = jnp.exp(acs)
        dte = jnp.exp(last - acs)
        cd = jnp.exp(last)
        for g in range(N_GROUPS):
            b_b = xc_ref[:, SSD_WIDTH + D_STATE * g:SSD_WIDTH + D_STATE * (g + 1)].astype(BF16)
            c_b = xc_ref[:, SSD_WIDTH + N_GROUPS * D_STATE + D_STATE * g:
                         SSD_WIDTH + N_GROUPS * D_STATE + D_STATE * (g + 1)].astype(BF16)
            cb = _mm_nt(c_b, b_b)
            for j in range(4 * g, 4 * g + 4):
                x2 = xc_ref[:, LANES * j:LANES * (j + 1)]
                xdt2 = x2 * _pair_lanes(dt, j, lane)
                xdt2_b = xdt2.astype(BF16)
                yd = []
                for e in range(2):
                    h = 2 * j + e
                    seg = acs[:, h:h + 1] - acs_t[h:h + 1, :]
                    lm = jnp.exp(jnp.where(causal, seg, NEG_BIG))
                    yd.append(_mm((cb * lm).astype(BF16), xdt2_b))
                h2 = h_scr[j]
                t2 = _mm_nt(c_b, h2.astype(BF16))
                y2 = (jnp.where(lane < HEAD_DIM, yd[0], yd[1]) + _pair_lanes(e_all, j, lane) * t2
                      + dsk_ref[:, LANES * j:LANES * (j + 1)] * x2)
                y_ref[:, LANES * j:LANES * (j + 1)] = y2
                hs_ref[0, j] = h2
                w2 = (xdt2 * _pair_lanes(dte, j, lane)).astype(BF16)
                s2 = _mm_tn(w2, b_b)
                cdcol = jnp.where(sub < HEAD_DIM, cd[:, 2 * j:2 * j + 1], cd[:, 2 * j + 1:2 * j + 2])
                h_scr[j] = h2 * cdcol + s2

    return pl.pallas_call(
        body, name="ssd_fwd",
        out_shape=(jax.ShapeDtypeStruct((s, SSD_WIDTH), F32),
                   jax.ShapeDtypeStruct((nc, N_PAIRS, LANES, D_STATE), F32)),
        grid=(nc,),
        in_specs=[pl.BlockSpec((CHUNK, CONV_CH), lambda c: (c, 0)), pl.BlockSpec((CHUNK, LANES), lambda c: (c, 0)),
                  _const_spec((1, LANES)), _const_spec((1, LANES)), _const_spec((1, SSD_WIDTH))],
        out_specs=(pl.BlockSpec((CHUNK, SSD_WIDTH), lambda c: (c, 0)),
                   pl.BlockSpec((1, N_PAIRS, LANES, D_STATE), lambda c: (c, 0, 0, 0))),
        scratch_shapes=[pltpu.VMEM((N_PAIRS, LANES, D_STATE), F32)],
        compiler_params=_params(("arbitrary",)),
    )(xc, small, dtb_row, a_row, dskip_lane)


def ssd_bwd(xc, small, states, dy, dtb_row, a_row, dskip_lane):
    s = xc.shape[0]
    nc = s // CHUNK
    rev = lambda c: nc - 1 - c

    def head_rowsums(q, lane):
        r0 = jnp.sum(jnp.where(lane < HEAD_DIM, q, 0.0), axis=1, keepdims=True)
        r1 = jnp.sum(jnp.where(lane < HEAD_DIM, 0.0, q), axis=1, keepdims=True)
        return r0, r1

    def body(xc_ref, sm_ref, hs_ref, dy_ref, dtb_ref, a_ref, dsk_ref,
             dxc_ref, ddt_ref, da_ref, ddtb_ref, ddsk_ref, dh_scr):
        c = pl.program_id(0)

        @pl.when(c == 0)
        def _():
            dh_scr[...] = jnp.zeros_like(dh_scr)
            da_ref[...] = jnp.zeros_like(da_ref)
            ddtb_ref[...] = jnp.zeros_like(ddtb_ref)
            ddsk_ref[...] = jnp.zeros_like(ddsk_ref)

        lane = _iota((CHUNK, LANES), 1)
        sub = _iota((CHUNK, LANES), 0)
        causal = lane <= sub
        is_last = sub == CHUNK - 1
        a_row_v = a_ref[...]
        dt, sig, acs, acs_t = _ssd_chunk_prelude(sm_ref[...], dtb_ref[...], a_row_v, lane, sub)
        last = acs[CHUNK - 1:CHUNK, :]
        e_all = jnp.exp(acs)
        dte = jnp.exp(last - acs)
        cd = jnp.exp(last)
        dacs_c = jnp.zeros((CHUNK, LANES), F32)
        dacs_r = jnp.zeros((LANES, CHUNK), F32)
        ddtx = jnp.zeros((CHUNK, LANES), F32)
        for g in range(N_GROUPS):
            b_off = SSD_WIDTH + D_STATE * g
            c_off = SSD_WIDTH + N_GROUPS * D_STATE + D_STATE * g
            b_b = xc_ref[:, b_off:b_off + D_STATE].astype(BF16)
            c_b = xc_ref[:, c_off:c_off + D_STATE].astype(BF16)
            cb = _mm_nt(c_b, b_b)
            dcb = jnp.zeros((CHUNK, CHUNK), F32)
            db_g = jnp.zeros((CHUNK, D_STATE), F32)
            dc_g = jnp.zeros((CHUNK, D_STATE), F32)
            for j in range(4 * g, 4 * g + 4):
                x2 = xc_ref[:, LANES * j:LANES * (j + 1)]
                dt2 = _pair_lanes(dt, j, lane)
                xdt2 = x2 * dt2
                xdt2_b = xdt2.astype(BF16)
                dy2 = dy_ref[:, LANES * j:LANES * (j + 1)]
                h2 = hs_ref[0, j]
                dh2 = dh_scr[j]
                h2_b = h2.astype(BF16)
                dh2_b = dh2.astype(BF16)
                dxdt2 = jnp.zeros((CHUNK, LANES), F32)
                for e in range(2):
                    h = 2 * j + e
                    in_head = (lane < HEAD_DIM) if e == 0 else (lane >= HEAD_DIM)
                    seg = acs[:, h:h + 1] - acs_t[h:h + 1, :]
                    lm = jnp.exp(jnp.where(causal, seg, NEG_BIG))
                    m_h = cb * lm
                    dyh_b = jnp.where(in_head, dy2, 0.0).astype(BF16)
                    dm_h = _mm_nt(dyh_b, xdt2_b)
                    dxdt2 = dxdt2 + _mm_tn(m_h.astype(BF16), dyh_b)
                    gmat = dm_h * m_h
                    dacs_c = dacs_c + jnp.where(lane == h, jnp.sum(gmat, axis=1, keepdims=True), 0.0)
                    dacs_r = dacs_r - jnp.where(sub == h, jnp.sum(gmat, axis=0, keepdims=True), 0.0)
                    dcb = dcb + dm_h * lm
                t2 = _mm_nt(c_b, h2_b)
                e2 = _pair_lanes(e_all, j, lane)
                r0, r1 = head_rowsums(dy2 * e2 * t2, lane)
                dacs_c = dacs_c + jnp.where(lane == 2 * j, r0, 0.0) + jnp.where(lane == 2 * j + 1, r1, 0.0)
                dt2_b = (dy2 * e2).astype(BF16)
                dc_g = dc_g + _mm(dt2_b, h2_b)
                dh_prev = _mm_tn(dt2_b, c_b)
                dw2 = _mm_nt(b_b, dh2_b)
                dte2 = _pair_lanes(dte, j, lane)
                w2 = xdt2 * dte2
                dxdt2 = dxdt2 + dw2 * dte2
                db_g = db_g + _mm(w2.astype(BF16), dh2_b)
                r0, r1 = head_rowsums(dw2 * w2, lane)
                q3 = dh2 * h2
                s0 = jnp.sum(jnp.where(sub < HEAD_DIM, q3, 0.0), keepdims=True)
                s1 = jnp.sum(jnp.where(sub < HEAD_DIM, 0.0, q3), keepdims=True)
                for e, (r, sq) in enumerate(((r0, s0), (r1, s1))):
                    h = 2 * j + e
                    at_end = jnp.sum(r, keepdims=True) + sq * cd[:, h:h + 1]
                    dacs_c = dacs_c + jnp.where(lane == h, jnp.where(is_last, at_end, 0.0) - r, 0.0)
                cdcol = jnp.where(sub < HEAD_DIM, cd[:, 2 * j:2 * j + 1], cd[:, 2 * j + 1:2 * j + 2])
                dh_scr[j] = dh_prev + dh2 * cdcol
                dsk2 = dsk_ref[:, LANES * j:LANES * (j + 1)]
                dxc_ref[:, LANES * j:LANES * (j + 1)] = dxdt2 * dt2 + dsk2 * dy2
                r0, r1 = head_rowsums(dxdt2 * x2, lane)
                ddtx = ddtx + jnp.where(lane == 2 * j, r0, 0.0) + jnp.where(lane == 2 * j + 1, r1, 0.0)
                ddsk_ref[:, LANES * j:LANES * (j + 1)] += jnp.sum(dy2 * x2, axis=0, keepdims=True)
            dcb_b = dcb.astype(BF16)
            dxc_ref[:, b_off:b_off + D_STATE] = db_g + _mm_tn(dcb_b, c_b)
            dxc_ref[:, c_off:c_off + D_STATE] = dc_g + _mm(dcb_b, b_b)
        dacs = dacs_c + dacs_r.T
        dadt = _mm_exact((lane >= sub).astype(F32), dacs)
        ddt = dadt * a_row_v + ddtx
        ddt_raw = ddt * sig
        ddt_ref[...] = ddt_raw
        da_ref[...] += jnp.sum(dadt * dt, axis=0, keepdims=True)
        ddtb_ref[...] += jnp.sum(ddt_raw, axis=0, keepdims=True)

    return pl.pallas_call(
        body, name="ssd_bwd",
        out_shape=(jax.ShapeDtypeStruct((s, CONV_CH), F32), jax.ShapeDtypeStruct((s, LANES), F32),
                   jax.ShapeDtypeStruct((1, LANES), F32), jax.ShapeDtypeStruct((1, LANES), F32),
                   jax.ShapeDtypeStruct((1, SSD_WIDTH), F32)),
        grid=(nc,),
        in_specs=[pl.BlockSpec((CHUNK, CONV_CH), lambda c: (rev(c), 0)),
                  pl.BlockSpec((CHUNK, LANES), lambda c: (rev(c), 0)),
                  pl.BlockSpec((1, N_PAIRS, LANES, D_STATE), lambda c: (rev(c), 0, 0, 0)),
                  pl.BlockSpec((CHUNK, SSD_WIDTH), lambda c: (rev(c), 0)),
                  _const_spec((1, LANES)), _const_spec((1, LANES)), _const_spec((1, SSD_WIDTH))],
        out_specs=(pl.BlockSpec((CHUNK, CONV_CH), lambda c: (rev(c), 0)),
                   pl.BlockSpec((CHUNK, LANES), lambda c: (rev(c), 0)),
                   _const_spec((1, LANES)), _const_spec((1, LANES)), _const_spec((1, SSD_WIDTH))),
        scratch_shapes=[pltpu.VMEM((N_PAIRS, LANES, D_STATE), F32)],
        compiler_params=_params(("arbitrary",)),
    )(xc, small, states, dy, dtb_row, a_row, dskip_lane)


def forget_cumsum(small, fgb_row):
    s = small.shape[0]
    nb = s // CHUNK

    def body(sm_ref, b_ref, cc_ref, carry):
        i = pl.program_id(0)

        @pl.when(i == 0)
        def _():
            carry[...] = jnp.zeros_like(carry)

        lane = _iota((CHUNK, LANES), 1)
        sub = _iota((CHUNK, LANES), 0)
        in_f = (lane >= N_HEADS) & (lane < 2 * N_HEADS)
        logf = jnp.where(in_f, -_softplus(-(sm_ref[...] + b_ref[...])), 0.0)
        tri = (lane <= sub).astype(F32)
        cum = _mm_exact(tri, logf) + carry[0:1, :]
        cc_ref[...] = cum
        carry[...] = jnp.broadcast_to(cum[CHUNK - 1:CHUNK, :], (8, LANES))

    return pl.pallas_call(
        body, name="forget_cumsum",
        out_shape=jax.ShapeDtypeStruct((s, LANES), F32),
        grid=(nb,),
        in_specs=[pl.BlockSpec((CHUNK, LANES), lambda i: (i, 0)), _const_spec((1, LANES))],
        out_specs=pl.BlockSpec((CHUNK, LANES), lambda i: (i, 0)),
        scratch_shapes=[pltpu.VMEM((8, LANES), F32)],
        compiler_params=_params(("arbitrary",)),
    )(small, fgb_row)


def forget_bwd(dc, small, ddt_raw, fgb_row):
    s = small.shape[0]
    nb = s // CHUNK
    rev = lambda i: nb - 1 - i

    def body(dc_ref, sm_ref, ddt_ref, b_ref, ds_ref, dfb_ref, carry):
        i = pl.program_id(0)

        @pl.when(i == 0)
        def _():
            carry[...] = jnp.zeros_like(carry)
            dfb_ref[...] = jnp.zeros_like(dfb_ref)

        lane = _iota((CHUNK, LANES), 1)
        sub = _iota((CHUNK, LANES), 0)
        rows = dc_ref[...].T
        tri = (lane <= sub).astype(F32)
        rc = _mm_exact(rows, tri) + carry[:, 0:1]
        carry[...] = jnp.broadcast_to(rc[:, 0:1], (LANES, LANES))
        in_f = (lane >= N_HEADS) & (lane < 2 * N_HEADS)
        df = jnp.where(in_f, rc.T * _sigmoid(-(sm_ref[...] + b_ref[...])), 0.0)
        ds_ref[...] = (df + ddt_ref[...]).astype(BF16)
        dfb_ref[...] += jnp.sum(df, axis=0, keepdims=True)

    return pl.pallas_call(
        body, name="forget_bwd",
        out_shape=(jax.ShapeDtypeStruct((s, LANES), BF16), jax.ShapeDtypeStruct((1, LANES), F32)),
        grid=(nb,),
        in_specs=[pl.BlockSpec((CHUNK, LANES), lambda i: (rev(i), 0)),
                  pl.BlockSpec((CHUNK, LANES), lambda i: (rev(i), 0)),
                  pl.BlockSpec((CHUNK, LANES), lambda i: (rev(i), 0)), _const_spec((1, LANES))],
        out_specs=(pl.BlockSpec((CHUNK, LANES), lambda i: (rev(i), 0)), _const_spec((1, LANES))),
        scratch_shapes=[pltpu.VMEM((LANES, LANES), F32)],
        compiler_params=_params(("arbitrary",)),
    )(dc, small, ddt_raw, fgb_row)


ATT_BLOCK = 512
ATT_SCALE = HEAD_DIM ** -0.5
AUG_A = HEAD_DIM
AUG_B = HEAD_DIM + 3


def _split3(c):
    hi = c.astype(BF16).astype(F32)
    r = c - hi
    mid = r.astype(BF16).astype(F32)
    return hi, mid, (r - mid).astype(BF16).astype(F32)


def _aug(lane, first, parts=None, value=1.0):
    if parts is None:
        return jnp.where((lane >= first) & (lane < first + 3), value, 0.0)
    return (jnp.where(lane == first, parts[0], 0.0) + jnp.where(lane == first + 1, parts[1], 0.0)
            + jnp.where(lane == first + 2, parts[2], 0.0))


def _pack_pair(a0, a1, lane):
    return jnp.where(lane < HEAD_DIM, a0, pltpu.roll(a1, HEAD_DIM, 1))


def proj_qkv_heads(u, w_q, w_k, w_v, cum):
    s = u.shape[0]
    tm = _blk(s, 256)

    def body(u_ref, wq_ref, wk_ref, wv_ref, c_ref, qa_ref, ka_ref, va_ref):
        lane = _iota((tm, LANES), 1)
        lo = lane < HEAD_DIM
        uv = u_ref[...]
        qf = _mm(uv, wq_ref[...]) * ATT_SCALE
        kf = _mm(uv, wk_ref[...])
        vf = _mm(uv, wv_ref[...])
        cc = c_ref[...]
        ones_a = _aug(lane, AUG_A)
        ones_b = _aug(lane, AUG_B)
        for h in range(N_HEADS):
            j, e = divmod(h, 2)

            def head(full):
                blk = full[:, LANES * j:LANES * (j + 1)]
                if e == 1:
                    blk = pltpu.roll(blk, HEAD_DIM, 1)
                return jnp.where(lo, blk, 0.0)

            parts = _split3(cc[:, N_HEADS + h:N_HEADS + h + 1])
            qa_ref[h] = (head(qf) + _aug(lane, AUG_A, parts) + ones_b).astype(BF16)
            ka_ref[h] = (head(kf) + ones_a - _aug(lane, AUG_B, parts)).astype(BF16)
            va_ref[h] = (head(vf) + ones_a).astype(BF16)

    shp = jax.ShapeDtypeStruct((N_HEADS, s, LANES), BF16)
    hspec = pl.BlockSpec((N_HEADS, tm, LANES), lambda i: (0, i, 0))
    wspec = _const_spec((D_MODEL, ATT_WIDTH))
    return pl.pallas_call(
        body, name="proj_qkv_heads", out_shape=(shp, shp, shp), grid=(s // tm,),
        in_specs=[pl.BlockSpec((tm, D_MODEL), lambda i: (i, 0)), wspec, wspec, wspec,
                  pl.BlockSpec((tm, LANES), lambda i: (i, 0))],
        out_specs=(hspec, hspec, hspec), compiler_params=_params(("parallel",)),
    )(u, w_q, w_k, w_v, cum)


def attention_fwd(qa, ka, va):
    s = qa.shape[1]
    t = _blk(s, ATT_BLOCK)
    nq = s // t

    def body(qa_ref, ka_ref, va_ref, o_ref, qb_ref, m_scr, acc_scr, alpha_scr, p_scr, s_scr):
        qi = pl.program_id(1)
        m_scr[...] = jnp.full_like(m_scr, NEG_BIG)
        acc_scr[...] = jnp.zeros_like(acc_scr)

        def kv_rows(kb):
            return pl.ds(pl.multiple_of(kb * t, t), t)

        def softmax_block(kb, masked):
            for e in range(2):
                sc = _mm_nt(qa_ref[e], ka_ref[e, kv_rows(kb), :])
                if masked:
                    sc = jnp.where(_iota((t, t), 0) >= _iota((t, t), 1), sc, NEG_BIG)
                s_scr[e] = sc
                cmax = s_scr[e, :, 0:LANES]
                for c in range(1, t // LANES):
                    cmax = jnp.maximum(cmax, s_scr[e, :, LANES * c:LANES * (c + 1)])
                m_old = m_scr[e]
                m_new = jnp.maximum(m_old, jnp.max(cmax, axis=1, keepdims=True))
                alpha_scr[e] = jnp.exp(m_old - m_new)
                m_scr[e] = m_new
                for c in range(t // LANES):
                    cols = slice(LANES * c, LANES * (c + 1))
                    p_scr[e, :, cols] = jnp.exp(s_scr[e, :, cols] - m_new).astype(BF16)

        def accumulate(kb):
            for e in range(2):
                acc_scr[e] = alpha_scr[e] * acc_scr[e] + _mm(p_scr[e], va_ref[e, kv_rows(kb), :])

        def loop_body(kb, carry):
            accumulate(kb - 1)
            softmax_block(kb, False)
            return carry

        @pl.when(qi > 0)
        def _():
            softmax_block(0, False)

        lax.fori_loop(1, qi, loop_body, 0)

        @pl.when(qi > 0)
        def _():
            accumulate(qi - 1)
            softmax_block(qi, True)

        @pl.when(qi == 0)
        def _():
            softmax_block(0, True)

        accumulate(qi)

        lane = _iota((t, LANES), 1)
        outs = []
        for e in range(2):
            acc = acc_scr[e]
            l = acc[:, AUG_A:AUG_A + 1]
            outs.append(acc / l)
            lse = m_scr[e][:, 0:1] + jnp.log(l)
            q32 = qa_ref[e].astype(F32)
            c = q32[:, AUG_A:AUG_A + 1] + q32[:, AUG_A + 1:AUG_A + 2] + q32[:, AUG_A + 2:AUG_A + 3]
            qb = jnp.where(lane < HEAD_DIM, q32, 0.0) + _aug(lane, AUG_A, _split3(c - lse)) + _aug(lane, AUG_B)
            qb_ref[e] = qb.astype(BF16)
        o_ref[...] = _pack_pair(outs[0], outs[1], lane)

    return pl.pallas_call(
        body, name="attention_fwd",
        out_shape=(jax.ShapeDtypeStruct((s, ATT_WIDTH), F32), jax.ShapeDtypeStruct((N_HEADS, s, LANES), BF16)),
        grid=(N_PAIRS, nq),
        in_specs=[pl.BlockSpec((2, t, LANES), lambda j, qi: (j, qi, 0)),
                  pl.BlockSpec((2, s, LANES), lambda j, qi: (j, 0, 0)),
                  pl.BlockSpec((2, s, LANES), lambda j, qi: (j, 0, 0))],
        out_specs=(pl.BlockSpec((t, LANES), lambda j, qi: (qi, j)),
                   pl.BlockSpec((2, t, LANES), lambda j, qi: (j, qi, 0))),
        scratch_shapes=[pltpu.VMEM((2, t, LANES), F32), pltpu.VMEM((2, t, LANES), F32),
                        pltpu.VMEM((2, t, LANES), F32), pltpu.VMEM((2, t, t), BF16), pltpu.VMEM((2, t, t), F32)],
        compiler_params=_params(("parallel", "parallel")),
    )(qa, ka, va)


def attention_bwd(qb, ka, va, dob):
    s = qb.shape[1]
    t = _blk(s, ATT_BLOCK)
    nq = s // t

    def body(qb_ref, dob_ref, ka_ref, va_ref, dq_ref, dk_ref, dv_ref, dc_ref, dq_scr, dk_scr, dv_scr):
        j, ki = pl.program_id(0), pl.program_id(1)

        @pl.when((j == 0) & (ki == 0))
        def _():
            dc_ref[...] = jnp.zeros_like(dc_ref)

        @pl.when(ki == 0)
        def _():
            dq_scr[...] = jnp.zeros_like(dq_scr)

        dk_scr[...] = jnp.zeros_like(dk_scr)
        dv_scr[...] = jnp.zeros_like(dv_scr)

        def q_step(qblk, masked):
            rows = pl.ds(pl.multiple_of(qblk * t, t), t)
            for e in range(2):
                q = qb_ref[e, rows, :]
                do = dob_ref[e, rows, :]
                sc = _mm_nt(q, ka_ref[e])
                if masked:
                    sc = jnp.where(_iota((t, t), 0) >= _iota((t, t), 1), sc, NEG_BIG)
                p = jnp.exp(sc)
                ds_b = (p * _mm_nt(do, va_ref[e])).astype(BF16)
                dv_scr[e] += _mm_tn(p.astype(BF16), do)
                dk_scr[e] += _mm_tn(ds_b, q)
                dq_scr[e, rows, :] += _mm(ds_b, ka_ref[e])

        def loop_body(qblk, carry):
            q_step(qblk, False)
            return carry

        q_step(ki, True)
        lax.fori_loop(ki + 1, nq, loop_body, 0)

        lane = _iota((t, LANES), 1)
        dk_ref[...] = _pack_pair(dk_scr[0], dk_scr[1], lane).astype(BF16)
        dv_ref[...] = _pack_pair(dv_scr[0], dv_scr[1], lane).astype(BF16)
        rows = pl.ds(pl.multiple_of(ki * t, t), t)
        dc_ref[rows, :] -= (jnp.where(lane == N_HEADS + 2 * j, dk_scr[0][:, AUG_B:AUG_B + 1], 0.0)
                            + jnp.where(lane == N_HEADS + 2 * j + 1, dk_scr[1][:, AUG_B:AUG_B + 1], 0.0))

        @pl.when(ki == nq - 1)
        def _():
            for blk in range(nq):
                rws = pl.ds(blk * t, t)
                d0 = dq_scr[0, rws, :]
                d1 = dq_scr[1, rws, :]
                dq_ref[rws, :] = (_pack_pair(d0, d1, lane) * ATT_SCALE).astype(BF16)
                dc_ref[rws, :] += (jnp.where(lane == N_HEADS + 2 * j, d0[:, AUG_A:AUG_A + 1], 0.0)
                                   + jnp.where(lane == N_HEADS + 2 * j + 1, d1[:, AUG_A:AUG_A + 1], 0.0))

    full = pl.BlockSpec((2, s, LANES), lambda j, ki: (j, 0, 0))
    blk = pl.BlockSpec((2, t, LANES), lambda j, ki: (j, ki, 0))
    pair = pl.BlockSpec((t, LANES), lambda j, ki: (ki, j))
    wide = jax.ShapeDtypeStruct((s, ATT_WIDTH), BF16)
    return pl.pallas_call(
        body, name="attention_bwd",
        out_shape=(wide, wide, wide, jax.ShapeDtypeStruct((s, LANES), F32)),
        grid=(N_PAIRS, nq),
        in_specs=[full, full, blk, blk],
        out_specs=(pl.BlockSpec((s, LANES), lambda j, ki: (0, j)), pair, pair, _const_spec((s, LANES))),
        scratch_shapes=[pltpu.VMEM((2, s, LANES), F32), pltpu.VMEM((2, t, LANES), F32),
                        pltpu.VMEM((2, t, LANES), F32)],
        compiler_params=_params(("arbitrary", "arbitrary")),
    )(qb, dob, ka, va)


def _dsilu(z, sg):
    return sg * (1.0 + z * (1.0 - sg))


def post_mix(x, y, zs, o, za, p, tgt, ssd_g, att_g_lane, ple_g, fin_g, w_out, w_gate, w_proj):
    s = x.shape[0]
    tm = _blk(s, 128)
    half = SSD_WIDTH // N_GROUPS

    def rms_bwd(dy, yn, r):
        return r * (dy - yn * jnp.mean(dy * yn, axis=-1, keepdims=True))

    def colsum(a):
        return jnp.sum(a, axis=0, keepdims=True)

    def body(x_ref, y_ref, zs_ref, o_ref, za_ref, p_ref, t_ref, sg_ref, ag_ref, pg_ref, fg_ref,
             wo_ref, wg_ref, wp_ref,
             dh1_ref, dy_ref, dzs_ref, dob_ref, dza_ref, ycat_ref, dh1b_ref, n2b_ref, dglb_ref, dppb_ref, pb_ref,
             loss_ref, dfin_ref, dple_ref, dssd_ref, datt_ref):
        @pl.when(pl.program_id(0) == 0)
        def _():
            for r in (loss_ref, dfin_ref, dple_ref, dssd_ref, datt_ref):
                r[...] = jnp.zeros_like(r)

        lane = _iota((tm, LANES), 1)
        lo = lane < HEAD_DIM
        zs = zs_ref[...]
        sz = _sigmoid(zs)
        yv = y_ref[...]
        ys = yv * (zs * sz)
        yn, rg = [], []
        for g in range(N_GROUPS):
            seg = ys[:, half * g:half * (g + 1)]
            r = lax.rsqrt(jnp.mean(seg * seg, axis=-1, keepdims=True) + EPS)
            yn.append(seg * r)
            rg.append(r)
            ycat_ref[:, half * g:half * (g + 1)] = (yn[g] * sg_ref[:, half * g:half * (g + 1)]).astype(BF16)
        za = za_ref[...]
        sza = _sigmoid(za)
        silu_za = za * sza
        on, ra = [], []
        for jb in range(N_PAIRS):
            blk = o_ref[:, LANES * jb:LANES * (jb + 1)]
            sq = blk * blk
            ms0 = jnp.sum(jnp.where(lo, sq, 0.0), axis=1, keepdims=True) * (1.0 / HEAD_DIM)
            ms1 = jnp.sum(jnp.where(lo, 0.0, sq), axis=1, keepdims=True) * (1.0 / HEAD_DIM)
            r = jnp.where(lo, lax.rsqrt(ms0 + EPS), lax.rsqrt(ms1 + EPS))
            on.append(blk * r)
            ra.append(r)
            an = on[jb] * ag_ref[:, LANES * jb:LANES * (jb + 1)]
            ycat_ref[:, SSD_WIDTH + LANES * jb:SSD_WIDTH + LANES * (jb + 1)] = (
                an * silu_za[:, LANES * jb:LANES * (jb + 1)]).astype(BF16)
        h1 = x_ref[...] + _mm(ycat_ref[...], wo_ref[...])
        r2 = lax.rsqrt(jnp.mean(h1 * h1, axis=-1, keepdims=True) + EPS)
        n2h = h1 * r2
        n2_b = (n2h * pg_ref[...]).astype(BF16)
        gate = _sigmoid(_mm(n2_b, wg_ref[...]))
        p_b = p_ref[...].astype(BF16)
        pp = _mm(p_b, wp_ref[...])
        h2 = h1 + gate * pp
        r3 = lax.rsqrt(jnp.mean(h2 * h2, axis=-1, keepdims=True) + EPS)
        n3 = h2 * r3
        diff = n3 * fg_ref[...] - t_ref[...]
        sq = colsum(diff * diff)
        part = sq[:, 0:LANES]
        for jb in range(1, D_MODEL // LANES):
            part = part + sq[:, LANES * jb:LANES * (jb + 1)]
        loss_ref[...] += part * (0.5 / D_MODEL)
        dout = diff * (1.0 / D_MODEL)
        dfin_ref[...] += colsum(dout * n3)
        dh2 = rms_bwd(dout * fg_ref[...], n3, r3)
        dgl = dh2 * pp * gate * (1.0 - gate)
        dgl_b = dgl.astype(BF16)
        dn2 = _mm_nt(dgl_b, wg_ref[...])
        dple_ref[...] += colsum(dn2 * n2h)
        dh1 = dh2 + rms_bwd(dn2 * pg_ref[...], n2h, r2)
        dh1_b = dh1.astype(BF16)
        dycat = _mm_nt(dh1_b, wo_ref[...])
        dh1_ref[...] = dh1
        dh1b_ref[...] = dh1_b
        n2b_ref[...] = n2_b
        dglb_ref[...] = dgl_b
        dppb_ref[...] = (dh2 * gate).astype(BF16)
        pb_ref[...] = p_b
        for g in range(N_GROUPS):
            cols = slice(half * g, half * (g + 1))
            dys_g = dycat[:, cols]
            dssd_ref[:, cols] += colsum(dys_g * yn[g])
            dys = rms_bwd(dys_g * sg_ref[:, cols], yn[g], rg[g])
            dy_ref[:, cols] = dys * (zs[:, cols] * sz[:, cols])
            dzs_ref[:, cols] = (dys * yv[:, cols] * _dsilu(zs[:, cols], sz[:, cols])).astype(BF16)
        for jb in range(N_PAIRS):
            cols = slice(LANES * jb, LANES * (jb + 1))
            dya = dycat[:, SSD_WIDTH + LANES * jb:SSD_WIDTH + LANES * (jb + 1)]
            ag = ag_ref[:, cols]
            dan = dya * silu_za[:, cols]
            dza_ref[:, cols] = (dya * (on[jb] * ag) * _dsilu(za[:, cols], sza[:, cols])).astype(BF16)
            datt_ref[:, cols] += colsum(dan * on[jb])
            don = dan * ag
            q = don * on[jb]
            m0 = jnp.sum(jnp.where(lo, q, 0.0), axis=1, keepdims=True) * (1.0 / HEAD_DIM)
            m1 = jnp.sum(jnp.where(lo, 0.0, q), axis=1, keepdims=True) * (1.0 / HEAD_DIM)
            do2 = ra[jb] * (don - on[jb] * jnp.where(lo, m0, m1))
            prod = do2 * o_ref[:, cols]
            for e in range(2):
                delta = jnp.sum(jnp.where(lo, prod, 0.0) if e == 0 else jnp.where(lo, 0.0, prod),
                                axis=1, keepdims=True)
                base = jnp.where(lo, do2 if e == 0 else pltpu.roll(do2, HEAD_DIM, 1), 0.0)
                dob_ref[2 * jb + e] = (base - _aug(lane, AUG_A, _split3(delta))).astype(BF16)

    def rows(n, dtype=None):
        return pl.BlockSpec((tm, n), lambda i: (i, 0))

    def out(n, dtype):
        return jax.ShapeDtypeStruct((s, n), dtype)

    vec = _const_spec((1, D_MODEL))
    vshape = jax.ShapeDtypeStruct((1, D_MODEL), F32)
    return pl.pallas_call(
        body, name="post_mix",
        out_shape=(out(D_MODEL, F32), out(SSD_WIDTH, F32), out(SSD_WIDTH, BF16),
                   jax.ShapeDtypeStruct((N_HEADS, s, LANES), BF16),
                   out(ATT_WIDTH, BF16), out(D_INNER, BF16), out(D_MODEL, BF16), out(D_MODEL, BF16),
                   out(D_MODEL, BF16), out(D_MODEL, BF16), out(PLE_DIM, BF16),
                   jax.ShapeDtypeStruct((1, LANES), F32), vshape, vshape, vshape, vshape),
        grid=(s // tm,),
        in_specs=[rows(D_MODEL), rows(SSD_WIDTH), rows(SSD_WIDTH), rows(ATT_WIDTH), rows(ATT_WIDTH),
                  rows(PLE_DIM), rows(D_MODEL), vec, vec, vec, vec,
                  _const_spec((D_INNER, D_MODEL)), _const_spec((D_MODEL, D_MODEL)), _const_spec((PLE_DIM, D_MODEL))],
        out_specs=(rows(D_MODEL), rows(SSD_WIDTH), rows(SSD_WIDTH),
                   pl.BlockSpec((N_HEADS, tm, LANES), lambda i: (0, i, 0)), rows(ATT_WIDTH),
                   rows(D_INNER), rows(D_MODEL), rows(D_MODEL), rows(D_MODEL), rows(D_MODEL), rows(PLE_DIM),
                   _const_spec((1, LANES)), vec, vec, vec, vec),
        compiler_params=_params(("arbitrary",)),
    )(x, y, zs, o, za, p, tgt, ssd_g, att_g_lane, ple_g, fin_g, w_out, w_gate, w_proj)


def in_proj_bwd(dsegs, wsegs, x, g, dh1):
    s = x.shape[0]
    tm = _blk(s, 256)
    nseg = len(dsegs)

    def body(*refs):
        d_refs = refs[:nseg]
        w_refs = refs[nseg:2 * nseg]
        x_ref, g_ref, dh1_ref, dx_ref, dg_ref = refs[2 * nseg:]

        @pl.when(pl.program_id(0) == 0)
        def _():
            dg_ref[...] = jnp.zeros_like(dg_ref)

        du = _mm_nt(d_refs[0][...], w_refs[0][...])
        for k in range(1, nseg):
            du = du + _mm_nt(d_refs[k][...], w_refs[k][...])
        xv = x_ref[...]
        r = lax.rsqrt(jnp.mean(xv * xv, axis=-1, keepdims=True) + EPS)
        xh = xv * r
        dg_ref[...] += jnp.sum(du * xh, axis=0, keepdims=True)
        dxh = du * g_ref[...]
        dx_ref[...] = r * (dxh - xh * jnp.mean(dxh * xh, axis=-1, keepdims=True)) + dh1_ref[...]

    rows = lambda n: pl.BlockSpec((tm, n), lambda i: (i, 0))
    return pl.pallas_call(
        body, name="in_proj_bwd",
        out_shape=(jax.ShapeDtypeStruct((s, D_MODEL), F32), jax.ShapeDtypeStruct((1, D_MODEL), F32)),
        grid=(s // tm,),
        in_specs=([rows(d.shape[1]) for d in dsegs] + [_const_spec(w.shape) for w in wsegs]
                  + [rows(D_MODEL), _const_spec((1, D_MODEL)), rows(D_MODEL)]),
        out_specs=(rows(D_MODEL), _const_spec((1, D_MODEL))),
        compiler_params=_params(("arbitrary",)),
    )(*dsegs, *wsegs, x, g, dh1)


SMALL_NAMES = ("norm_g", "conv_b", "dt_bias", "a_log", "d_skip", "ssd_norm_g", "fg_bias", "att_norm_g",
               "ple_norm_g", "final_norm_g")
SMALL_SIZES = (1024, 1536, 16, 16, 16, 1024, 16, 64, 1024, 1024)
CONV_W_SIZE = CONV_WIDTH * CONV_CH


def _pack_small(vals):
    flat = jnp.concatenate([v.reshape(-1).astype(F32) for v in vals])
    flat = jnp.pad(flat, (0, SMALL_ROWS * LANES - flat.shape[0]))
    return flat.reshape(SMALL_ROWS, LANES)


def _unpack_small(pack, shapes):
    flat = pack.reshape(-1)
    out, off = [], 0
    for n, shp in zip(SMALL_SIZES, shapes):
        out.append(flat[off:off + n].reshape(shp))
        off += n
    return out


def _row128(v16, offset=0):
    return jnp.pad(v16.reshape(1, N_HEADS).astype(F32), ((0, 0), (offset, LANES - N_HEADS - offset)))


def local_step(x, p, tgt, w_in, w_out, w_gate, w_proj, conv_w, norm_g, conv_b, dt_bias, a_log, d_skip,
               ssd_norm_g, fg_bias, att_norm_g, ple_norm_g, final_norm_g):
    c0, c1, c2, c3, c4, c5, c6, c7 = 0, 1024, 2560, 2576, 3600, 4624, 5648, 6672
    w_zs, w_xbc, w_dt = w_in[:, c0:c1], w_in[:, c1:c2], w_in[:, c2:c3]
    w_za, w_q, w_k, w_v, w_f = w_in[:, c3:c4], w_in[:, c4:c5], w_in[:, c5:c6], w_in[:, c6:c7], w_in[:, c7:]
    w_small = jnp.concatenate([w_dt, w_f, jnp.zeros((D_MODEL, LANES - 2 * N_HEADS), BF16)], axis=1)

    dtb_row = _row128(dt_bias)
    a_row = _row128(-jnp.exp(a_log.astype(F32)))
    fgb_row = _row128(fg_bias, N_HEADS)
    dskip_lane = jnp.repeat(d_skip.astype(F32), HEAD_DIM).reshape(1, SSD_WIDTH)
    att_g_lane = jnp.tile(att_norm_g.astype(F32), N_HEADS).reshape(1, ATT_WIDTH)
    row = lambda v: v.reshape(1, -1).astype(F32)

    u = rms_prenorm(x, row(norm_g))
    zs = matmul_rows(u, w_zs, F32, "proj_z_ssd")
    xbc = matmul_rows(u, w_xbc, F32, "proj_xbc")
    za = matmul_rows(u, w_za, F32, "proj_z_att")
    small = matmul_rows(u, w_small, F32, "proj_small")
    cum = forget_cumsum(small, fgb_row)
    qa, ka, va = proj_qkv_heads(u, w_q, w_k, w_v, cum)
    pre, xc = conv_fwd(xbc, conv_w, row(conv_b))
    y, states = ssd_fwd(xc, small, dtb_row, a_row, dskip_lane)
    o, qb = attention_fwd(qa, ka, va)
    (dh1, dy, dzs, dob, dza, ycat, dh1_b, n2_b, dgl_b, dpp_b, p_b,
     loss_l, dfin, dple, dssd_g, datt_lane) = post_mix(
        x, y, zs, o, za, p, tgt, row(ssd_norm_g), att_g_lane, row(ple_norm_g), row(final_norm_g),
        w_out, w_gate, w_proj)
    dq, dk, dv, dc = attention_bwd(qb, ka, va, dob)
    dxc, ddt_raw, da, ddtb, ddsk_lane = ssd_bwd(xc, small, states, dy, dtb_row, a_row, dskip_lane)
    dsmall, dfgb = forget_bwd(dc, small, ddt_raw, fgb_row)
    dxbc, dconv_w8, dconv_b = conv_bwd(xbc, pre, dxc, conv_w)
    dsegs = [dzs, dxbc, dza, dq, dk, dv, dsmall]
    wsegs = [w_zs, w_xbc, w_za, w_q, w_k, w_v, w_small]
    dx, dnorm_g = in_proj_bwd(dsegs, wsegs, x, row(norm_g), dh1)
    dws = [matmul_tn(u, d, "dw_in_%d" % i) for i, d in enumerate(dsegs)]
    dw_in = jnp.concatenate([dws[0], dws[1], dws[6][:, :N_HEADS], dws[2], dws[3], dws[4], dws[5],
                             dws[6][:, N_HEADS:2 * N_HEADS]], axis=1)
    dw_out = matmul_tn(ycat, dh1_b, "dw_out")
    dw_gate = matmul_tn(n2_b, dgl_b, "dw_gate")
    dw_proj = matmul_tn(p_b, dpp_b, "dw_proj")
    small_grads = [
        dnorm_g, dconv_b, ddtb[0, :N_HEADS], (da * a_row)[0, :N_HEADS],
        ddsk_lane.reshape(N_HEADS, HEAD_DIM).sum(axis=1), dssd_g, dfgb[0, N_HEADS:2 * N_HEADS],
        datt_lane.reshape(N_HEADS, HEAD_DIM).sum(axis=0), dple, dfin]
    loss = jnp.sum(loss_l)
    return loss, dx, dw_in, dw_out, dw_gate, dw_proj, dconv_w8[:CONV_WIDTH], small_grads


def kernel(x, p, norm_g, w_in, conv_w, conv_b, dt_bias, a_log, d_skip, ssd_norm_g, fg_bias, att_norm_g, w_out, ple_norm_g, w_ple_gate, w_ple_proj, final_norm_g, loss_target, m_norm_g, m_w_in, m_conv_w, m_conv_b, m_dt_bias, m_a_log, m_d_skip, m_ssd_norm_g, m_fg_bias, m_att_norm_g, m_w_out, m_ple_norm_g, m_w_ple_gate, m_w_ple_proj, m_final_norm_g, v_norm_g, v_w_in, v_conv_w, v_conv_b, v_dt_bias, v_a_log, v_d_skip, v_ssd_norm_g, v_fg_bias, v_att_norm_g, v_w_out, v_ple_norm_g, v_w_ple_gate, v_w_ple_proj, v_final_norm_g):
    chip = 2 * lax.axis_index("x") + lax.axis_index("y")
    core = lax.axis_index("c")

    big_w = [w_in[0], w_out[0], w_ple_gate[0], w_ple_proj[0]]
    own = [a.astype(BF16) for a in big_w] + [conv_w[0]]
    gathered = gather_weights(own[:4], own[4])

    def joined(k, axis):
        return jnp.concatenate([jnp.where(chip == j, own[k], gathered[k][j]) for j in range(N_CHIPS)], axis=axis)

    w_in_f, w_out_f, w_gate_f, w_proj_f, conv_w_f = joined(0, 1), joined(1, 0), joined(2, 0), joined(3, 1), joined(4, 1)

    smalls_w = [norm_g, conv_b, dt_bias, a_log, d_skip, ssd_norm_g, fg_bias, att_norm_g, ple_norm_g, final_norm_g]
    loss_l, dx, dw_in, dw_out, dw_gate, dw_proj, dconv_w, small_grads = local_step(
        x[0], p[0, 0], loss_target[0], w_in_f, w_out_f, w_gate_f, w_proj_f, conv_w_f,
        *[a.reshape(-1) for a in smalls_w])
    loss = lax.psum(loss_l, ("x", "y", "c"))

    gs = [jnp.stack([dw_in[:, 1672 * j:1672 * (j + 1)] for j in range(N_CHIPS)]),
          dw_out.reshape(N_CHIPS, 512, D_MODEL), dw_gate.reshape(N_CHIPS, 256, D_MODEL),
          jnp.stack([dw_proj[:, 256 * j:256 * (j + 1)] for j in range(N_CHIPS)])]
    core1 = core.reshape(1).astype(jnp.int32)
    pres = add_halves(core1, gs, halves_to_sibling(gs))
    *parts, smalls = scatter_halves(pres, _pack_small(list(small_grads) + [dconv_w]))
    mine = sum_parts(parts)

    g_big, d_big, m_big, v_big = adamw_big(
        core1, mine, swap_halves(mine), big_w, [m_w_in[0], m_w_out[0], m_w_ple_gate[0], m_w_ple_proj[0]],
        [v_w_in[0], v_w_out[0], v_w_ple_gate[0], v_w_ple_proj[0]])
    smalls_m = [m_norm_g, m_conv_b, m_dt_bias, m_a_log, m_d_skip, m_ssd_norm_g, m_fg_bias, m_att_norm_g,
                m_ple_norm_g, m_final_norm_g]
    smalls_v = [v_norm_g, v_conv_b, v_dt_bias, v_a_log, v_d_skip, v_ssd_norm_g, v_fg_bias, v_att_norm_g,
                v_ple_norm_g, v_final_norm_g]
    g_sm, d_sm, m_sm, v_sm = adamw_small(smalls, _pack_small(smalls_w), _pack_small(smalls_m), _pack_small(smalls_v))
    n_small = sum(SMALL_SIZES)
    g_conv_full = g_sm.reshape(-1)[n_small:n_small + CONV_W_SIZE].reshape(CONV_WIDTH, CONV_CH)
    g_conv = lax.dynamic_slice_in_dim(g_conv_full, chip * 384, 384, axis=1)
    d_conv, m_conv, v_conv = adamw_whole(g_conv, conv_w[0], m_conv_w[0], v_conv_w[0], "adamw_conv")

    shapes = [a.shape for a in smalls_w]
    outs = []
    for big, conv, sm in ((g_big, g_conv, g_sm), (d_big, d_conv, d_sm), (m_big, m_conv, m_sm), (v_big, v_conv, v_sm)):
        b_in, b_out, b_gate, b_proj = [a[None] for a in big]
        s_norm, s_convb, s_dtb, s_alog, s_dsk, s_ssdg, s_fgb, s_attg, s_pleg, s_fin = _unpack_small(sm, shapes)
        outs.extend([s_norm, b_in, conv[None], s_convb, s_dtb, s_alog, s_dsk, s_ssdg, s_fgb, s_attg, b_out, s_pleg,
                     b_gate, b_proj, s_fin])
    return (loss, dx[None], *outs)
```

```python
import functools

import jax
import jax.numpy as jnp
from jax import lax
from jax.experimental import pallas as pl
from jax.experimental.pallas import tpu as pltpu

F32 = jnp.float32
BF16 = jnp.bfloat16

D_MODEL = 1024
SSD_WIDTH = 1024
ATT_WIDTH = 1024
N_HEADS = 16
HEAD_DIM = 64
N_GROUPS = 2
D_STATE = 128
CONV_CH = 1536
CONV_WIDTH = 4
CHUNK = 128
PLE_DIM = 256
D_INNER = 2048
EPS = 1e-6
IN_COLS = 6688
N_CHIPS = 4
N_DEV = 8
LANES = 128
N_PAIRS = 8

ADAM_LR = 0.001
ADAM_B1 = 0.9
ADAM_B2 = 0.999
ADAM_EPS = 1e-08
ADAM_WD = 0.01
ADAM_STEP = 10

SMALL_ROWS = 96

NEG_BIG = -1e30
VMEM_LIMIT = 56 * 1024 * 1024

MESH = pl.DeviceIdType.MESH
ANY = pl.BlockSpec(memory_space=pl.ANY)


def _mm(a, b):
    return jnp.dot(a, b, preferred_element_type=F32)


def _mm_nt(a, b):
    return lax.dot_general(a, b, (((1,), (1,)), ((), ())), preferred_element_type=F32)


def _mm_tn(a, b):
    return lax.dot_general(a, b, (((0,), (0,)), ((), ())), preferred_element_type=F32)


def _mm_exact(a, b):
    return jnp.dot(a, b, preferred_element_type=F32, precision=lax.Precision.HIGHEST)


def _softplus(x):
    return jnp.maximum(x, 0.0) + jnp.log1p(jnp.exp(-jnp.abs(x)))


def _sigmoid(x):
    return jax.nn.sigmoid(x)


def _iota(shape, dim):
    return lax.broadcasted_iota(jnp.int32, shape, dim)


def _params(sem=None):
    return pltpu.CompilerParams(dimension_semantics=sem, vmem_limit_bytes=VMEM_LIMIT)


def _blk(n, pref):
    return min(n, pref)


def _const_spec(shape):
    nd = len(shape)
    return pl.BlockSpec(shape, lambda *_: (0,) * nd)


def _chip_peers():
    x, y, c = lax.axis_index("x"), lax.axis_index("y"), lax.axis_index("c")
    return x, y, c, [(1 - x, y, c), (x, 1 - y, c), (1 - x, 1 - y, c)]


def _half(rows, c):
    h = rows // 2
    return pl.ds(pl.multiple_of(c * h, 8), h)


def _sems(n):
    return [pltpu.SemaphoreType.DMA((n,)), pltpu.SemaphoreType.DMA((n,))]


def gather_weights(shards, conv_s):
    n = len(shards)

    def body(*refs):
        ins, conv_in = refs[:n], refs[n]
        outs, conv_out = refs[n + 1:2 * n + 1], refs[2 * n + 1]
        ssem1, rsem1, ssem2, rsem2, c_ssem, c_rsem = refs[2 * n + 2:]
        x, y, c, peers = _chip_peers()
        me = 2 * x + y
        sibling = (x, y, 1 - c)
        first, small = [], []
        for k, peer in enumerate(peers):
            for i in range(n):
                h = _half(ins[i].shape[0], c)
                first.append(pltpu.make_async_remote_copy(
                    src_ref=ins[i].at[h], dst_ref=outs[i].at[me, h], send_sem=ssem1.at[n * k + i],
                    recv_sem=rsem1.at[n * k + i], device_id=peer, device_id_type=MESH))
            small.append(pltpu.make_async_remote_copy(
                src_ref=conv_in, dst_ref=conv_out.at[me], send_sem=c_ssem.at[k], recv_sem=c_rsem.at[k],
                device_id=peer, device_id_type=MESH))
        for cp in first + small:
            cp.start()
        passed = []
        for k, peer in enumerate(peers):
            chip = 2 * peer[0] + peer[1]
            for i in range(n):
                h = _half(ins[i].shape[0], c)
                first[n * k + i].wait_recv()
                fwd = pltpu.make_async_remote_copy(
                    src_ref=outs[i].at[chip, h], dst_ref=outs[i].at[chip, h], send_sem=ssem2.at[n * k + i],
                    recv_sem=rsem2.at[n * k + i], device_id=sibling, device_id_type=MESH)
                fwd.start()
                passed.append(fwd)
        for cp in passed:
            cp.wait_recv()
        for cp in first + passed:
            cp.wait_send()
        for cp in small:
            cp.wait()

    return pl.pallas_call(
        body, name="gather_weights",
        out_shape=tuple(jax.ShapeDtypeStruct((N_CHIPS,) + a.shape, a.dtype) for a in list(shards) + [conv_s]),
        in_specs=[ANY] * (n + 1), out_specs=(ANY,) * (n + 1),
        scratch_shapes=_sems(3 * n) + _sems(3 * n) + _sems(3),
    )(*shards, conv_s)


def halves_to_sibling(gs):
    n = len(gs)

    def body(*refs):
        ins, outs = refs[:n], refs[n:2 * n]
        ssem, rsem = refs[2 * n:]
        x, y, c = lax.axis_index("x"), lax.axis_index("y"), lax.axis_index("c")
        copies = []
        for i in range(n):
            for j in range(N_CHIPS):
                copies.append(pltpu.make_async_remote_copy(
                    src_ref=ins[i].at[j, _half(ins[i].shape[1], 1 - c)], dst_ref=outs[i].at[j],
                    send_sem=ssem.at[N_CHIPS * i + j], recv_sem=rsem.at[N_CHIPS * i + j],
                    device_id=(x, y, 1 - c), device_id_type=MESH))
        for cp in copies:
            cp.start()
        for cp in copies:
            cp.wait()

    return pl.pallas_call(
        body, name="halves_to_sibling",
        out_shape=tuple(jax.ShapeDtypeStruct((N_CHIPS, g.shape[1] // 2, g.shape[2]), F32) for g in gs),
        in_specs=[ANY] * n, out_specs=(ANY,) * n, scratch_shapes=_sems(N_CHIPS * n),
    )(*gs)


RED_GRID = 8


def add_halves(core, gs, rbs):
    n = len(gs)

    def body(c_ref, *refs):
        for i in range(n):
            refs[2 * n + i][...] = refs[i][...] + refs[n + i][...]

    def blk(g):
        return (1, g.shape[1] // 2 // RED_GRID, g.shape[2])

    grid_spec = pltpu.PrefetchScalarGridSpec(
        num_scalar_prefetch=1, grid=(N_CHIPS, RED_GRID),
        in_specs=([pl.BlockSpec(blk(g), lambda j, b, c_ref: (j, c_ref[0] * RED_GRID + b, 0)) for g in gs]
                  + [pl.BlockSpec(blk(g), lambda j, b, c_ref: (j, b, 0)) for g in gs]),
        out_specs=[pl.BlockSpec(blk(g), lambda j, b, c_ref: (j, b, 0)) for g in gs])
    return pl.pallas_call(
        body, name="add_halves", grid_spec=grid_spec,
        out_shape=tuple(jax.ShapeDtypeStruct(r.shape, F32) for r in rbs),
        compiler_params=_params(("parallel", "parallel")),
    )(core, *gs, *rbs)


def scatter_halves(pres, small):
    n = len(pres)

    def body(*refs):
        ins, s_ref = refs[:n], refs[n]
        outs, smalls_ref = refs[n + 1:2 * n + 1], refs[2 * n + 1]
        ssem, rsem, s_ssem, s_rsem, lsem = refs[2 * n + 2:]
        x, y, c, peers = _chip_peers()
        me = 2 * x + y
        dev = 4 * x + 2 * y + c
        local = [pltpu.make_async_copy(ins[i].at[me], outs[i].at[me], lsem.at[i]) for i in range(n)]
        local.append(pltpu.make_async_copy(s_ref, smalls_ref.at[dev], lsem.at[n]))
        for cp in local:
            cp.start()
        remote = []
        for k, peer in enumerate(peers):
            dst_chip = 2 * peer[0] + peer[1]
            for i in range(n):
                remote.append(pltpu.make_async_remote_copy(
                    src_ref=ins[i].at[dst_chip], dst_ref=outs[i].at[me], send_sem=ssem.at[n * k + i],
                    recv_sem=rsem.at[n * k + i], device_id=peer, device_id_type=MESH))
        for k in range(1, N_DEV):
            fx, fy, fc = (k >> 2) & 1, (k >> 1) & 1, k & 1
            peer = ((1 - x) if fx else x, (1 - y) if fy else y, (1 - c) if fc else c)
            remote.append(pltpu.make_async_remote_copy(
                src_ref=s_ref, dst_ref=smalls_ref.at[dev], send_sem=s_ssem.at[k - 1], recv_sem=s_rsem.at[k - 1],
                device_id=peer, device_id_type=MESH))
        for cp in remote:
            cp.start()
        for cp in remote:
            cp.wait()
        for cp in local:
            cp.wait()

    return pl.pallas_call(
        body, name="scatter_halves",
        out_shape=tuple([jax.ShapeDtypeStruct(a.shape, F32) for a in pres]
                        + [jax.ShapeDtypeStruct((N_DEV,) + small.shape, F32)]),
        in_specs=[ANY] * (n + 1), out_specs=(ANY,) * (n + 1),
        scratch_shapes=_sems(3 * n) + _sems(N_DEV - 1) + [pltpu.SemaphoreType.DMA((n + 1,))],
    )(*pres, small)


def sum_parts(parts):
    n = len(parts)

    def body(*refs):
        for i in range(n):
            p_ref = refs[i]
            refs[n + i][...] = ((p_ref[0] + p_ref[1]) + p_ref[2]) + p_ref[3]

    def rows(p):
        return p.shape[1] // RED_GRID

    return pl.pallas_call(
        body, name="sum_parts",
        out_shape=tuple(jax.ShapeDtypeStruct(p.shape[1:], F32) for p in parts),
        grid=(RED_GRID,),
        in_specs=[pl.BlockSpec((N_CHIPS, rows(p), p.shape[2]), lambda b: (0, b, 0)) for p in parts],
        out_specs=tuple(pl.BlockSpec((rows(p), p.shape[2]), lambda b: (b, 0)) for p in parts),
        compiler_params=_params(("parallel",)),
    )(*parts)


def swap_halves(reds):
    n = len(reds)

    def body(*refs):
        ins, outs = refs[:n], refs[n:2 * n]
        ssem, rsem = refs[2 * n:]
        x, y, c = lax.axis_index("x"), lax.axis_index("y"), lax.axis_index("c")
        copies = [pltpu.make_async_remote_copy(
            src_ref=ins[i], dst_ref=outs[i], send_sem=ssem.at[i], recv_sem=rsem.at[i],
            device_id=(x, y, 1 - c), device_id_type=MESH) for i in range(n)]
        for cp in copies:
            cp.start()
        for cp in copies:
            cp.wait()

    return pl.pallas_call(
        body, name="swap_halves",
        out_shape=tuple(jax.ShapeDtypeStruct(r.shape, F32) for r in reds),
        in_specs=[ANY] * n, out_specs=(ANY,) * n, scratch_shapes=_sems(n),
    )(*reds)


def _adamw(w, g, m, v):
    m = ADAM_B1 * m + (1.0 - ADAM_B1) * g
    v = ADAM_B2 * v + (1.0 - ADAM_B2) * (g * g)
    m_hat = m / (1.0 - ADAM_B1 ** ADAM_STEP)
    v_hat = v / (1.0 - ADAM_B2 ** ADAM_STEP)
    delta = -ADAM_LR * (m_hat / (jnp.sqrt(v_hat) + ADAM_EPS) + ADAM_WD * w)
    return delta, m, v


def adamw_big(core, mine, theirs, ws, ms, vs):
    n = len(ws)
    per_half = RED_GRID // 2

    def body(c_ref, *refs):
        own = (pl.program_id(0) // per_half) == c_ref[0]
        for i in range(n):
            g = jnp.where(own, refs[i][...], refs[n + i][...])
            d, mn, vn = _adamw(refs[2 * n + i][...], g, refs[3 * n + i][...], refs[4 * n + i][...])
            refs[5 * n + i][...] = g
            refs[6 * n + i][...] = d
            refs[7 * n + i][...] = mn
            refs[8 * n + i][...] = vn

    def blk(w):
        return (w.shape[0] // RED_GRID, w.shape[1])

    halves = [pl.BlockSpec(blk(w), lambda b, c_ref: (b % per_half, 0)) for w in ws]
    whole = [pl.BlockSpec(blk(w), lambda b, c_ref: (b, 0)) for w in ws]
    shapes = [jax.ShapeDtypeStruct(w.shape, F32) for w in ws]
    grid_spec = pltpu.PrefetchScalarGridSpec(
        num_scalar_prefetch=1, grid=(RED_GRID,), in_specs=halves * 2 + whole * 3, out_specs=whole * 4)
    outs = pl.pallas_call(
        body, name="adamw_big", out_shape=tuple(shapes * 4), grid_spec=grid_spec,
        compiler_params=_params(("parallel",)),
    )(core, *mine, *theirs, *ws, *ms, *vs)
    return outs[:n], outs[n:2 * n], outs[2 * n:3 * n], outs[3 * n:]


def adamw_whole(g, w, m, v, name):
    def body(g_ref, w_ref, m_ref, v_ref, d_out, m_out, v_out):
        d, mn, vn = _adamw(w_ref[...], g_ref[...], m_ref[...], v_ref[...])
        d_out[...] = d
        m_out[...] = mn
        v_out[...] = vn

    shp = jax.ShapeDtypeStruct(g.shape, F32)
    return pl.pallas_call(body, name=name, out_shape=(shp,) * 3)(g, w, m, v)


def adamw_small(smalls, w, m, v):
    def body(s_ref, w_ref, m_ref, v_ref, g_out, d_out, m_out, v_out):
        g = s_ref[0]
        for k in range(1, N_DEV):
            g = g + s_ref[k]
        d, mn, vn = _adamw(w_ref[...], g, m_ref[...], v_ref[...])
        g_out[...] = g
        d_out[...] = d
        m_out[...] = mn
        v_out[...] = vn

    shp = jax.ShapeDtypeStruct((SMALL_ROWS, LANES), F32)
    return pl.pallas_call(body, name="adamw_small", out_shape=(shp,) * 4)(smalls, w, m, v)


def rms_prenorm(x, g):
    s = x.shape[0]
    tm = _blk(s, 512)

    def body(x_ref, g_ref, u_ref):
        xv = x_ref[...]
        r = lax.rsqrt(jnp.mean(xv * xv, axis=-1, keepdims=True) + EPS)
        u_ref[...] = (xv * r * g_ref[...]).astype(BF16)

    return pl.pallas_call(
        body, name="rms_prenorm", out_shape=jax.ShapeDtypeStruct(x.shape, BF16), grid=(s // tm,),
        in_specs=[pl.BlockSpec((tm, D_MODEL), lambda i: (i, 0)), _const_spec((1, D_MODEL))],
        out_specs=pl.BlockSpec((tm, D_MODEL), lambda i: (i, 0)), compiler_params=_params(("parallel",)),
    )(x, g)


def matmul_rows(a, w, out_dtype, name):
    s, k = a.shape
    n = w.shape[1]
    tm = _blk(s, 512)

    def body(a_ref, w_ref, o_ref):
        o_ref[...] = _mm(a_ref[...], w_ref[...]).astype(out_dtype)

    return pl.pallas_call(
        body, name=name, out_shape=jax.ShapeDtypeStruct((s, n), out_dtype), grid=(s // tm,),
        in_specs=[pl.BlockSpec((tm, k), lambda i: (i, 0)), _const_spec((k, n))],
        out_specs=pl.BlockSpec((tm, n), lambda i: (i, 0)), compiler_params=_params(("parallel",)),
    )(a, w)


def matmul_tn(a, b, name):
    s, m = a.shape
    n = b.shape[1]
    tk = _blk(s, 2048)
    tn = _blk(n, 512)

    def body(a_ref, b_ref, o_ref):
        @pl.when(pl.program_id(1) == 0)
        def _():
            o_ref[...] = jnp.zeros_like(o_ref)

        o_ref[...] += _mm_tn(a_ref[...], b_ref[...])

    return pl.pallas_call(
        body, name=name, out_shape=jax.ShapeDtypeStruct((m, n), F32), grid=(n // tn, s // tk),
        in_specs=[pl.BlockSpec((tk, m), lambda j, i: (i, 0)), pl.BlockSpec((tk, tn), lambda j, i: (i, j))],
        out_specs=pl.BlockSpec((m, tn), lambda j, i: (0, j)),
        compiler_params=_params(("parallel", "arbitrary")),
    )(a, b)


def conv_fwd(xbc, w, b):
    s = xbc.shape[0]
    tm = _blk(s, 256)

    def body(x_ref, t_ref, w_ref, b_ref, pre_ref, act_ref):
        i = pl.program_id(0)
        cur = x_ref[...]
        tail = jnp.where(i > 0, t_ref[...], 0.0)
        wv = w_ref[...]
        acc = cur * wv[3:4, :] + b_ref[...]
        head = cur[0:8, :] * wv[3:4, :] + b_ref[...]
        row8 = _iota((8, CONV_CH), 0)
        for sh in range(1, CONV_WIDTH):
            wk = wv[3 - sh:4 - sh, :]
            acc = acc + pltpu.roll(cur, sh, 0) * wk
            first = jnp.where(row8 < sh, pltpu.roll(tail, sh, 0), pltpu.roll(cur[0:8, :], sh, 0))
            head = head + first * wk
        pre_ref[...] = acc
        act_ref[...] = acc * _sigmoid(acc)
        pre_ref[0:8, :] = head
        act_ref[0:8, :] = head * _sigmoid(head)

    shp = jax.ShapeDtypeStruct(xbc.shape, F32)
    rows = pl.BlockSpec((tm, CONV_CH), lambda i: (i, 0))
    return pl.pallas_call(
        body, name="conv_fwd", out_shape=(shp, shp), grid=(s // tm,),
        in_specs=[rows, pl.BlockSpec((8, CONV_CH), lambda i: (jnp.maximum(i * (tm // 8) - 1, 0), 0)),
                  _const_spec((CONV_WIDTH, CONV_CH)), _const_spec((1, CONV_CH))],
        out_specs=(rows, rows), compiler_params=_params(("parallel",)),
    )(xbc, xbc, w, b)


def conv_bwd(xbc, pre, dact, w):
    s = xbc.shape[0]
    tm = _blk(s, 256)
    nb = s // tm

    def dsilu(p):
        sg = _sigmoid(p)
        return sg * (1.0 + p * (1.0 - sg))

    def body(x_ref, xt_ref, p_ref, pn_ref, d_ref, dn_ref, w_ref, dx_ref, dw_ref, db_ref):
        i = pl.program_id(0)

        @pl.when(i == 0)
        def _():
            dw_ref[...] = jnp.zeros_like(dw_ref)
            db_ref[...] = jnp.zeros_like(db_ref)

        wv = w_ref[...]
        dpre = d_ref[...] * dsilu(p_ref[...])
        dnext = jnp.where(i < nb - 1, dn_ref[...] * dsilu(pn_ref[...]), 0.0)
        cur = x_ref[...]
        tail = jnp.where(i > 0, xt_ref[...], 0.0)
        row8 = _iota((8, CONV_CH), 0)
        dx = dpre * wv[3:4, :]
        last = dpre[tm - 8:tm, :] * wv[3:4, :]
        db_ref[...] += jnp.sum(dpre, axis=0, keepdims=True)
        dws = [jnp.sum(dpre * cur, axis=0, keepdims=True)]
        for sh in range(1, CONV_WIDTH):
            wk = wv[3 - sh:4 - sh, :]
            dx = dx + pltpu.roll(dpre, tm - sh, 0) * wk
            nxt = jnp.where(row8 >= 8 - sh, pltpu.roll(dnext, 8 - sh, 0), pltpu.roll(dpre[tm - 8:tm, :], 8 - sh, 0))
            last = last + nxt * wk
            xs = pltpu.roll(cur, sh, 0)
            first = jnp.where(row8 < sh, pltpu.roll(tail, sh, 0), xs[0:8, :])
            dws.append(jnp.sum(dpre * xs, axis=0, keepdims=True)
                       + jnp.sum(dpre[0:8, :] * (first - xs[0:8, :]), axis=0, keepdims=True))
        dx_ref[...] = dx.astype(BF16)
        dx_ref[tm - 8:tm, :] = last.astype(BF16)
        for sh in range(CONV_WIDTH):
            dw_ref[3 - sh:4 - sh, :] += dws[sh]

    rows = pl.BlockSpec((tm, CONV_CH), lambda i: (i, 0))
    prev8 = pl.BlockSpec((8, CONV_CH), lambda i: (jnp.maximum(i * (tm // 8) - 1, 0), 0))
    next8 = pl.BlockSpec((8, CONV_CH), lambda i: (jnp.minimum((i + 1) * (tm // 8), s // 8 - 1), 0))
    return pl.pallas_call(
        body, name="conv_bwd",
        out_shape=(jax.ShapeDtypeStruct(xbc.shape, BF16), jax.ShapeDtypeStruct((8, CONV_CH), F32),
                   jax.ShapeDtypeStruct((1, CONV_CH), F32)),
        grid=(nb,),
        in_specs=[rows, prev8, rows, next8, rows, next8, _const_spec((CONV_WIDTH, CONV_CH))],
        out_specs=(rows, _const_spec((8, CONV_CH)), _const_spec((1, CONV_CH))),
        compiler_params=_params(("arbitrary",)),
    )(xbc, xbc, pre, pre, dact, dact, w)


def _pair_lanes(mat, j, lane):
    return jnp.where(lane < HEAD_DIM, mat[:, 2 * j:2 * j + 1], mat[:, 2 * j + 1:2 * j + 2])


def _ssd_chunk_prelude(sm, dtb, a_row, lane, sub):
    raw = sm + dtb
    head_lane = lane < N_HEADS
    dt = jnp.where(head_lane, _softplus(raw), 0.0)
    sig = jnp.where(head_lane, _sigmoid(raw), 0.0)
    tri = (lane <= sub).astype(F32)
    acs = _mm_exact(tri, dt * a_row)
    return dt, sig, acs, acs.T


GROUP_WIDTH = SSD_WIDTH // N_GROUPS
HEADS_PER_GROUP = N_HEADS // N_GROUPS


def _expand_group(mat, g, lane):
    return jnp.concatenate([_pair_lanes(mat, j, lane) for j in range(4 * g, 4 * g + 4)], axis=1)


def _head_sums(q, g):
    row = _iota((GROUP_WIDTH, LANES), 0)
    seg = (_iota((GROUP_WIDTH, LANES), 1) == HEADS_PER_GROUP * g + (row >> 6)).astype(BF16)
    hi = q.astype(BF16)
    lo = (q - hi.astype(F32)).astype(BF16)
    return _mm(hi, seg) + _mm(lo, seg)


def _rows_from_lanes(row512):
    return jnp.broadcast_to(row512, (LANES, GROUP_WIDTH)).T


def ssd_fwd(xc, small, dtb_row, a_row, dskip_lane):
    s = xc.shape[0]
    nc = s // CHUNK

    def body(xc_ref, sm_ref, dtb_ref, a_ref, dsk_ref, y_ref, hs_ref, h_scr):
        c = pl.program_id(0)

        @pl.when(c == 0)
        def _():
            h_scr[...] = jnp.zeros_like(h_scr)

        lane = _iota((CHUNK, LANES), 1)
        sub = _iota((CHUNK, LANES), 0)
        causal = lane <= sub
        dt, _, acs, acs_t = _ssd_chunk_prelude(sm_ref[...], dtb_ref[...], a_ref[...], lane, sub)
        for g in range(N_GROUPS):
            cols = slice(GROUP_WIDTH * g, GROUP_WIDTH * (g + 1))
            b_off = SSD_WIDTH + D_STATE * g
            c_off = SSD_WIDTH + N_GROUPS * D_STATE + D_STATE * g
            b_b = xc_ref[:, b_off:b_off + D_STATE].astype(BF16)
            c_b = xc_ref[:, c_off:c_off + D_STATE].astype(BF16)
            cb = _mm_nt(c_b, b_b)
            x_g = xc_ref[:, cols]
            acs_g = _expand_group(acs, g, lane)
            xdt_g = x_g * _expand_group(dt, g, lane)
            xdt_b = xdt_g.astype(BF16)
            heads = range(HEADS_PER_GROUP * g, HEADS_PER_GROUP * (g + 1))
            m_b = [(cb * jnp.exp(jnp.where(causal, acs[:, h:h + 1] - acs_t[h:h + 1, :], NEG_BIG))).astype(BF16)
                   for h in heads]
            yd = [_mm(m_b[k], xdt_b[:, LANES * (k // 2):LANES * (k // 2 + 1)]) for k in range(HEADS_PER_GROUP)]
            yd_g = jnp.concatenate([jnp.where(lane < HEAD_DIM, yd[2 * k], yd[2 * k + 1]) for k in range(4)], axis=1)
            h_g = h_scr[g]
            t_g = _mm_nt(c_b, h_g.astype(BF16))
            y_ref[:, cols] = yd_g + jnp.exp(acs_g) * t_g + dsk_ref[:, cols] * x_g
            hs_ref[0, g] = h_g
            last_g = acs_g[CHUNK - 1:CHUNK, :]
            w_b = (xdt_g * jnp.exp(last_g - acs_g)).astype(BF16)
            h_scr[g] = h_g * jnp.exp(_rows_from_lanes(last_g)) + _mm_tn(w_b, b_b)

    return pl.pallas_call(
        body, name="ssd_fwd",
        out_shape=(jax.ShapeDtypeStruct((s, SSD_WIDTH), F32),
                   jax.ShapeDtypeStruct((nc, N_GROUPS, GROUP_WIDTH, D_STATE), F32)),
        grid=(nc,),
        in_specs=[pl.BlockSpec((CHUNK, CONV_CH), lambda c: (c, 0)), pl.BlockSpec((CHUNK, LANES), lambda c: (c, 0)),
                  _const_spec((1, LANES)), _const_spec((1, LANES)), _const_spec((1, SSD_WIDTH))],
        out_specs=(pl.BlockSpec((CHUNK, SSD_WIDTH), lambda c: (c, 0)),
                   pl.BlockSpec((1, N_GROUPS, GROUP_WIDTH, D_STATE), lambda c: (c, 0, 0, 0))),
        scratch_shapes=[pltpu.VMEM((N_GROUPS, GROUP_WIDTH, D_STATE), F32)],
        compiler_params=_params(("arbitrary",)),
    )(xc, small, dtb_row, a_row, dskip_lane)


def ssd_bwd(xc, small, states, dy, dtb_row, a_row, dskip_lane):
    s = xc.shape[0]
    nc = s // CHUNK
    rev = lambda c: nc - 1 - c

    def body(xc_ref, sm_ref, hs_ref, dy_ref, dtb_ref, a_ref, dsk_ref,
             dxc_ref, ddt_ref, da_ref, ddtb_ref, ddsk_ref, dh_scr):
        c = pl.program_id(0)

        @pl.when(c == 0)
        def _():
            dh_scr[...] = jnp.zeros_like(dh_scr)
            da_ref[...] = jnp.zeros_like(da_ref)
            ddtb_ref[...] = jnp.zeros_like(ddtb_ref)
            ddsk_ref[...] = jnp.zeros_like(ddsk_ref)

        lane = _iota((CHUNK, LANES), 1)
        sub = _iota((CHUNK, LANES), 0)
        causal = lane <= sub
        upper = lane >= sub
        is_last = sub == CHUNK - 1
        a_row_v = a_ref[...]
        dt, sig, acs, acs_t = _ssd_chunk_prelude(sm_ref[...], dtb_ref[...], a_row_v, lane, sub)
        cd = jnp.exp(acs[CHUNK - 1:CHUNK, :])
        dacs_c = jnp.zeros((CHUNK, LANES), F32)
        dacs_r = jnp.zeros((LANES, CHUNK), F32)
        ddtx = jnp.zeros((CHUNK, LANES), F32)
        for g in range(N_GROUPS):
            cols = slice(GROUP_WIDTH * g, GROUP_WIDTH * (g + 1))
            b_off = SSD_WIDTH + D_STATE * g
            c_off = SSD_WIDTH + N_GROUPS * D_STATE + D_STATE * g
            b_b = xc_ref[:, b_off:b_off + D_STATE].astype(BF16)
            c_b = xc_ref[:, c_off:c_off + D_STATE].astype(BF16)
            cb = _mm_nt(c_b, b_b)
            cb_t = _mm_nt(b_b, c_b)
            x_g = xc_ref[:, cols]
            dy_g = dy_ref[:, cols]
            dt_g = _expand_group(dt, g, lane)
            acs_g = _expand_group(acs, g, lane)
            last_g = acs_g[CHUNK - 1:CHUNK, :]
            e_g = jnp.exp(acs_g)
            dte_g = jnp.exp(last_g - acs_g)
            xdt_g = x_g * dt_g
            xdt_b = xdt_g.astype(BF16)
            h_g = hs_ref[0, g]
            dh_g = dh_scr[g]
            h_b = h_g.astype(BF16)
            dh_b = dh_g.astype(BF16)
            heads = list(range(HEADS_PER_GROUP * g, HEADS_PER_GROUP * (g + 1)))
            segs = [acs[:, h:h + 1] - acs_t[h:h + 1, :] for h in heads]
            lms = [jnp.exp(jnp.where(causal, sg, NEG_BIG)) for sg in segs]
            mts = [(cb_t * jnp.exp(jnp.where(upper, -sg, NEG_BIG))).astype(BF16) for sg in segs]
            dyh = []
            for k in range(HEADS_PER_GROUP):
                blk = dy_g[:, LANES * (k // 2):LANES * (k // 2 + 1)]
                in_head = (lane < HEAD_DIM) if k % 2 == 0 else (lane >= HEAD_DIM)
                dyh.append(jnp.where(in_head, blk, 0.0).astype(BF16))
            dms = [_mm_nt(dyh[k], xdt_b[:, LANES * (k // 2):LANES * (k // 2 + 1)]) for k in range(HEADS_PER_GROUP)]
            dxs = [_mm(mts[k], dyh[k]) for k in range(HEADS_PER_GROUP)]
            dcb = jnp.zeros((CHUNK, CHUNK), F32)
            for k, h in enumerate(heads):
                gmat = dms[k] * (cb * lms[k])
                dacs_c = dacs_c + jnp.where(lane == h, jnp.sum(gmat, axis=1, keepdims=True), 0.0)
                dacs_r = dacs_r - jnp.where(sub == h, jnp.sum(gmat, axis=0, keepdims=True), 0.0)
                dcb = dcb + dms[k] * lms[k]
            dxdt_g = jnp.concatenate([dxs[2 * k] + dxs[2 * k + 1] for k in range(4)], axis=1)
            t_g = _mm_nt(c_b, h_b)
            dacs_c = dacs_c + _head_sums(dy_g * e_g * t_g, g)
            dt_b = (dy_g * e_g).astype(BF16)
            dc_acc = _mm(dt_b, h_b)
            dh_prev = _mm_tn(dt_b, c_b)
            dw_g = _mm_nt(b_b, dh_b)
            w_g = xdt_g * dte_g
            dxdt_g = dxdt_g + dw_g * dte_g
            db_acc = _mm(w_g.astype(BF16), dh_b)
            r2 = _head_sums(dw_g * w_g, g)
            dacs_c = dacs_c + jnp.where(is_last, jnp.sum(r2, axis=0, keepdims=True), 0.0) - r2
            q3 = jnp.sum(dh_g * h_g, axis=1, keepdims=True)
            for k, h in enumerate(heads):
                tot = jnp.sum(q3[HEAD_DIM * k:HEAD_DIM * (k + 1), :], keepdims=True) * cd[:, h:h + 1]
                dacs_c = dacs_c + jnp.where(is_last & (lane == h), tot, 0.0)
            dh_scr[g] = dh_prev + dh_g * jnp.exp(_rows_from_lanes(last_g))
            dxc_ref[:, cols] = dxdt_g * dt_g + dsk_ref[:, cols] * dy_g
            ddtx = ddtx + _head_sums(dxdt_g * x_g, g)
            ddsk_ref[:, cols] += jnp.sum(dy_g * x_g, axis=0, keepdims=True)
            dxc_ref[:, b_off:b_off + D_STATE] = db_acc + _mm(dcb.T.astype(BF16), c_b)
            dxc_ref[:, c_off:c_off + D_STATE] = dc_acc + _mm(dcb.astype(BF16), b_b)
        dacs = dacs_c + dacs_r.T
        dadt = _mm_exact((lane >= sub).astype(F32), dacs)
        ddt = dadt * a_row_v + ddtx
        ddt_raw = ddt * sig
        ddt_ref[...] = ddt_raw
        da_ref[...] += jnp.sum(dadt * dt, axis=0, keepdims=True)
        ddtb_ref[...] += jnp.sum(ddt_raw, axis=0, keepdims=True)

    return pl.pallas_call(
        body, name="ssd_bwd",
        out_shape=(jax.ShapeDtypeStruct((s, CONV_CH), F32), jax.ShapeDtypeStruct((s, LANES), F32),
                   jax.ShapeDtypeStruct((1, LANES), F32), jax.ShapeDtypeStruct((1, LANES), F32),
                   jax.ShapeDtypeStruct((1, SSD_WIDTH), F32)),
        grid=(nc,),
        in_specs=[pl.BlockSpec((CHUNK, CONV_CH), lambda c: (rev(c), 0)),
                  pl.BlockSpec((CHUNK, LANES), lambda c: (rev(c), 0)),
                  pl.BlockSpec((1, N_GROUPS, GROUP_WIDTH, D_STATE), lambda c: (rev(c), 0, 0, 0)),
                  pl.BlockSpec((CHUNK, SSD_WIDTH), lambda c: (rev(c), 0)),
                  _const_spec((1, LANES)), _const_spec((1, LANES)), _const_spec((1, SSD_WIDTH))],
        out_specs=(pl.BlockSpec((CHUNK, CONV_CH), lambda c: (rev(c), 0)),
                   pl.BlockSpec((CHUNK, LANES), lambda c: (rev(c), 0)),
                   _const_spec((1, LANES)), _const_spec((1, LANES)), _const_spec((1, SSD_WIDTH))),
        scratch_shapes=[pltpu.VMEM((N_GROUPS, GROUP_WIDTH, D_STATE), F32)],
        compiler_params=_params(("arbitrary",)),
    )(xc, small, states, dy, dtb_row, a_row, dskip_lane)


def forget_cumsum(small, fgb_row):
    s = small.shape[0]
    nb = s // CHUNK

    def body(sm_ref, b_ref, cc_ref, carry):
        i = pl.program_id(0)

        @pl.when(i == 0)
        def _():
            carry[...] = jnp.zeros_like(carry)

        lane = _iota((CHUNK, LANES), 1)
        sub = _iota((CHUNK, LANES), 0)
        in_f = (lane >= N_HEADS) & (lane < 2 * N_HEADS)
        logf = jnp.where(in_f, -_softplus(-(sm_ref[...] + b_ref[...])), 0.0)
        tri = (lane <= sub).astype(F32)
        cum = _mm_exact(tri, logf) + carry[0:1, :]
        cc_ref[...] = cum
        carry[...] = jnp.broadcast_to(cum[CHUNK - 1:CHUNK, :], (8, LANES))

    return pl.pallas_call(
        body, name="forget_cumsum",
        out_shape=jax.ShapeDtypeStruct((s, LANES), F32),
        grid=(nb,),
        in_specs=[pl.BlockSpec((CHUNK, LANES), lambda i: (i, 0)), _const_spec((1, LANES))],
        out_specs=pl.BlockSpec((CHUNK, LANES), lambda i: (i, 0)),
        scratch_shapes=[pltpu.VMEM((8, LANES), F32)],
        compiler_params=_params(("arbitrary",)),
    )(small, fgb_row)


def forget_bwd(dc, small, ddt_raw, fgb_row):
    s = small.shape[0]
    nb = s // CHUNK
    rev = lambda i: nb - 1 - i

    def body(dc_ref, sm_ref, ddt_ref, b_ref, ds_ref, dfb_ref, carry):
        i = pl.program_id(0)

        @pl.when(i == 0)
        def _():
            carry[...] = jnp.zeros_like(carry)
            dfb_ref[...] = jnp.zeros_like(dfb_ref)

        lane = _iota((CHUNK, LANES), 1)
        sub = _iota((CHUNK, LANES), 0)
        rows = dc_ref[...].T
        tri = (lane <= sub).astype(F32)
        rc = _mm_exact(rows, tri) + carry[:, 0:1]
        carry[...] = jnp.broadcast_to(rc[:, 0:1], (LANES, LANES))
        in_f = (lane >= N_HEADS) & (lane < 2 * N_HEADS)
        df = jnp.where(in_f, rc.T * _sigmoid(-(sm_ref[...] + b_ref[...])), 0.0)
        ds_ref[...] = (df + ddt_ref[...]).astype(BF16)
        dfb_ref[...] += jnp.sum(df, axis=0, keepdims=True)

    return pl.pallas_call(
        body, name="forget_bwd",
        out_shape=(jax.ShapeDtypeStruct((s, LANES), BF16), jax.ShapeDtypeStruct((1, LANES), F32)),
        grid=(nb,),
        in_specs=[pl.BlockSpec((CHUNK, LANES), lambda i: (rev(i), 0)),
                  pl.BlockSpec((CHUNK, LANES), lambda i: (rev(i), 0)),
                  pl.BlockSpec((CHUNK, LANES), lambda i: (rev(i), 0)), _const_spec((1, LANES))],
        out_specs=(pl.BlockSpec((CHUNK, LANES), lambda i: (rev(i), 0)), _const_spec((1, LANES))),
        scratch_shapes=[pltpu.VMEM((LANES, LANES), F32)],
        compiler_params=_params(("arbitrary",)),
    )(dc, small, ddt_raw, fgb_row)


ATT_BLOCK = 512
ATT_SCALE = HEAD_DIM ** -0.5
AUG_A = HEAD_DIM
AUG_B = HEAD_DIM + 3


def _split3(c):
    hi = c.astype(BF16).astype(F32)
    r = c - hi
    mid = r.astype(BF16).astype(F32)
    return hi, mid, (r - mid).astype(BF16).astype(F32)


def _aug(lane, first, parts=None, value=1.0):
    if parts is None:
        return jnp.where((lane >= first) & (lane < first + 3), value, 0.0)
    return (jnp.where(lane == first, parts[0], 0.0) + jnp.where(lane == first + 1, parts[1], 0.0)
            + jnp.where(lane == first + 2, parts[2], 0.0))


def _pack_pair(a0, a1, lane):
    return jnp.where(lane < HEAD_DIM, a0, pltpu.roll(a1, HEAD_DIM, 1))


def proj_qkv_heads(u, w_q, w_k, w_v, cum):
    s = u.shape[0]
    tm = _blk(s, 256)

    def body(u_ref, wq_ref, wk_ref, wv_ref, c_ref, qa_ref, ka_ref, va_ref):
        lane = _iota((tm, LANES), 1)
        lo = lane < HEAD_DIM
        uv = u_ref[...]
        qf = _mm(uv, wq_ref[...]) * ATT_SCALE
        kf = _mm(uv, wk_ref[...])
        vf = _mm(uv, wv_ref[...])
        cc = c_ref[...]
        ones_a = _aug(lane, AUG_A)
        ones_b = _aug(lane, AUG_B)
        for h in range(N_HEADS):
            j, e = divmod(h, 2)

            def head(full):
                blk = full[:, LANES * j:LANES * (j + 1)]
                if e == 1:
                    blk = pltpu.roll(blk, HEAD_DIM, 1)
                return jnp.where(lo, blk, 0.0)

            parts = _split3(cc[:, N_HEADS + h:N_HEADS + h + 1])
            qa_ref[h] = (head(qf) + _aug(lane, AUG_A, parts) + ones_b).astype(BF16)
            ka_ref[h] = (head(kf) + ones_a - _aug(lane, AUG_B, parts)).astype(BF16)
            va_ref[h] = (head(vf) + ones_a).astype(BF16)

    shp = jax.ShapeDtypeStruct((N_HEADS, s, LANES), BF16)
    hspec = pl.BlockSpec((N_HEADS, tm, LANES), lambda i: (0, i, 0))
    wspec = _const_spec((D_MODEL, ATT_WIDTH))
    return pl.pallas_call(
        body, name="proj_qkv_heads", out_shape=(shp, shp, shp), grid=(s // tm,),
        in_specs=[pl.BlockSpec((tm, D_MODEL), lambda i: (i, 0)), wspec, wspec, wspec,
                  pl.BlockSpec((tm, LANES), lambda i: (i, 0))],
        out_specs=(hspec, hspec, hspec), compiler_params=_params(("parallel",)),
    )(u, w_q, w_k, w_v, cum)


def attention_fwd(qa, ka, va):
    s = qa.shape[1]
    t = _blk(s, ATT_BLOCK)
    nq = s // t

    def body(qa_ref, ka_ref, va_ref, o_ref, qb_ref, m_scr, acc_scr, s0, s1, p0, p1, a0, a1):
        qi = pl.program_id(1)
        m_scr[...] = jnp.full_like(m_scr, NEG_BIG)
        acc_scr[...] = jnp.zeros_like(acc_scr)
        even, odd = (s0, p0, a0), (s1, p1, a1)

        def kv_rows(kb):
            return pl.ds(pl.multiple_of(kb * t, t), t)

        half = t // 2

        def logits_pieces(kb, buf, masked):
            def piece(e, n):
                keys = pl.ds(pl.multiple_of(kb * t + n * half, half), half)
                sc = _mm_nt(qa_ref[e], ka_ref[e, keys, :])
                if masked:
                    sc = jnp.where(_iota((t, half), 0) >= _iota((t, half), 1) + n * half, sc, NEG_BIG)
                buf[0][e, :, half * n:half * (n + 1)] = sc
            return [functools.partial(piece, e, n) for e in range(2) for n in range(2)]

        def softmax_pieces(buf):
            src, p_dst, a_dst = buf

            def new_max(e):
                cmax = src[e, :, 0:LANES]
                for c in range(1, t // LANES):
                    cmax = jnp.maximum(cmax, src[e, :, LANES * c:LANES * (c + 1)])
                m_old = m_scr[e]
                m_new = jnp.maximum(m_old, jnp.max(cmax, axis=1, keepdims=True))
                a_dst[e] = jnp.exp(m_old - m_new)
                m_scr[e] = m_new

            def probs(e, c):
                cols = slice(LANES * c, LANES * (c + 1))
                p_dst[e, :, cols] = jnp.exp(src[e, :, cols] - m_scr[e]).astype(BF16)

            out = []
            for e in range(2):
                out.append(functools.partial(new_max, e))
                out.extend(functools.partial(probs, e, c) for c in range(t // LANES))
            return out

        def accumulate_pieces(kb, buf):
            def piece(e, n):
                keys = pl.ds(pl.multiple_of(kb * t + n * half, half), half)
                pv = _mm(buf[1][e, :, half * n:half * (n + 1)], va_ref[e, keys, :])
                acc_scr[e] = (buf[2][e] * acc_scr[e] if n == 0 else acc_scr[e]) + pv
            return [functools.partial(piece, e, n) for e in range(2) for n in range(2)]

        def emit(mxu, vpu):
            for k in range(max(len(mxu), len(vpu))):
                if k < len(mxu):
                    mxu[k]()
                if k < len(vpu):
                    vpu[k]()

        def logits(kb, buf, masked):
            emit(logits_pieces(kb, buf, masked), [])

        def softmax(buf):
            emit([], softmax_pieces(buf))

        def accumulate(kb, buf):
            emit(accumulate_pieces(kb, buf), [])

        def step(kb, cur, nxt, with_acc, masked_next):
            mxu = accumulate_pieces(kb - 1, nxt) if with_acc else []
            emit(mxu + logits_pieces(kb + 1, nxt, masked_next), softmax_pieces(cur))

        def by_parity(k, fn):
            @pl.when(k % 2 == 0)
            def _():
                fn(even, odd)

            @pl.when(k % 2 == 1)
            def _():
                fn(odd, even)

        def loop_body(kb, carry):
            by_parity(kb, lambda cur, nxt: step(kb, cur, nxt, True, False))
            return carry

        @pl.when(qi == 0)
        def _():
            logits(0, even, True)
            softmax(even)
            accumulate(0, even)

        @pl.when(qi > 0)
        def _():
            logits(0, even, False)

        @pl.when(qi == 1)
        def _():
            step(0, even, odd, False, True)

        @pl.when(qi > 1)
        def _():
            step(0, even, odd, False, False)

        lax.fori_loop(1, qi - 1, loop_body, 0)

        @pl.when(qi > 1)
        def _():
            by_parity(qi - 1, lambda cur, nxt: step(qi - 1, cur, nxt, True, True))

        def last(cur, nxt):
            emit(accumulate_pieces(qi - 1, nxt), softmax_pieces(cur))
            accumulate(qi, cur)

        @pl.when(qi > 0)
        def _():
            by_parity(qi, last)

        lane = _iota((t, LANES), 1)
        outs = []
        for e in range(2):
            acc = acc_scr[e]
            l = acc[:, AUG_A:AUG_A + 1]
            outs.append(acc / l)
            lse = m_scr[e][:, 0:1] + jnp.log(l)
            q32 = qa_ref[e].astype(F32)
            c = q32[:, AUG_A:AUG_A + 1] + q32[:, AUG_A + 1:AUG_A + 2] + q32[:, AUG_A + 2:AUG_A + 3]
            qb = jnp.where(lane < HEAD_DIM, q32, 0.0) + _aug(lane, AUG_A, _split3(c - lse)) + _aug(lane, AUG_B)
            qb_ref[e] = qb.astype(BF16)
        o_ref[...] = _pack_pair(outs[0], outs[1], lane)

    return pl.pallas_call(
        body, name="attention_fwd",
        out_shape=(jax.ShapeDtypeStruct((s, ATT_WIDTH), F32), jax.ShapeDtypeStruct((N_HEADS, s, LANES), BF16)),
        grid=(N_PAIRS, nq),
        in_specs=[pl.BlockSpec((2, t, LANES), lambda j, qi: (j, qi, 0)),
                  pl.BlockSpec((2, s, LANES), lambda j, qi: (j, 0, 0)),
                  pl.BlockSpec((2, s, LANES), lambda j, qi: (j, 0, 0))],
        out_specs=(pl.BlockSpec((t, LANES), lambda j, qi: (qi, j)),
                   pl.BlockSpec((2, t, LANES), lambda j, qi: (j, qi, 0))),
        scratch_shapes=[pltpu.VMEM((2, t, LANES), F32), pltpu.VMEM((2, t, LANES), F32),
                        pltpu.VMEM((2, t, t), F32), pltpu.VMEM((2, t, t), F32),
                        pltpu.VMEM((2, t, t), BF16), pltpu.VMEM((2, t, t), BF16),
                        pltpu.VMEM((2, t, LANES), F32), pltpu.VMEM((2, t, LANES), F32)],
        compiler_params=_params(("parallel", "parallel")),
    )(qa, ka, va)


def attention_bwd(qb, ka, va, dob):
    s = qb.shape[1]
    t = _blk(s, ATT_BLOCK)
    nq = s // t

    def body(qb_ref, dob_ref, ka_ref, va_ref, dq_ref, dk_ref, dv_ref, dc_ref, dq_scr, dk_scr, dv_scr):
        j, ki = pl.program_id(0), pl.program_id(1)

        @pl.when((j == 0) & (ki == 0))
        def _():
            dc_ref[...] = jnp.zeros_like(dc_ref)

        @pl.when(ki == 0)
        def _():
            dq_scr[...] = jnp.zeros_like(dq_scr)

        dk_scr[...] = jnp.zeros_like(dk_scr)
        dv_scr[...] = jnp.zeros_like(dv_scr)

        def q_step(qblk, masked):
            rows = pl.ds(pl.multiple_of(qblk * t, t), t)
            for e in range(2):
                q = qb_ref[e, rows, :]
                do = dob_ref[e, rows, :]
                sc = _mm_nt(q, ka_ref[e])
                if masked:
                    sc = jnp.where(_iota((t, t), 0) >= _iota((t, t), 1), sc, NEG_BIG)
                p = jnp.exp(sc)
                ds_b = (p * _mm_nt(do, va_ref[e])).astype(BF16)
                dv_scr[e] += _mm_tn(p.astype(BF16), do)
                dk_scr[e] += _mm_tn(ds_b, q)
                dq_scr[e, rows, :] += _mm(ds_b, ka_ref[e])

        def loop_body(qblk, carry):
            q_step(qblk, False)
            return carry

        q_step(ki, True)
        lax.fori_loop(ki + 1, nq, loop_body, 0)

        lane = _iota((t, LANES), 1)
        dk_ref[...] = _pack_pair(dk_scr[0], dk_scr[1], lane).astype(BF16)
        dv_ref[...] = _pack_pair(dv_scr[0], dv_scr[1], lane).astype(BF16)
        rows = pl.ds(pl.multiple_of(ki * t, t), t)
        dc_ref[rows, :] -= (jnp.where(lane == N_HEADS + 2 * j, dk_scr[0][:, AUG_B:AUG_B + 1], 0.0)
                            + jnp.where(lane == N_HEADS + 2 * j + 1, dk_scr[1][:, AUG_B:AUG_B + 1], 0.0))

        @pl.when(ki == nq - 1)
        def _():
            for blk in range(nq):
                rws = pl.ds(blk * t, t)
                d0 = dq_scr[0, rws, :]
                d1 = dq_scr[1, rws, :]
                dq_ref[rws, :] = (_pack_pair(d0, d1, lane) * ATT_SCALE).astype(BF16)
                dc_ref[rws, :] += (jnp.where(lane == N_HEADS + 2 * j, d0[:, AUG_A:AUG_A + 1], 0.0)
                                   + jnp.where(lane == N_HEADS + 2 * j + 1, d1[:, AUG_A:AUG_A + 1], 0.0))

    full = pl.BlockSpec((2, s, LANES), lambda j, ki: (j, 0, 0))
    blk = pl.BlockSpec((2, t, LANES), lambda j, ki: (j, ki, 0))
    pair = pl.BlockSpec((t, LANES), lambda j, ki: (ki, j))
    wide = jax.ShapeDtypeStruct((s, ATT_WIDTH), BF16)
    return pl.pallas_call(
        body, name="attention_bwd",
        out_shape=(wide, wide, wide, jax.ShapeDtypeStruct((s, LANES), F32)),
        grid=(N_PAIRS, nq),
        in_specs=[full, full, blk, blk],
        out_specs=(pl.BlockSpec((s, LANES), lambda j, ki: (0, j)), pair, pair, _const_spec((s, LANES))),
        scratch_shapes=[pltpu.VMEM((2, s, LANES), F32), pltpu.VMEM((2, t, LANES), F32),
                        pltpu.VMEM((2, t, LANES), F32)],
        compiler_params=_params(("arbitrary", "arbitrary")),
    )(qb, dob, ka, va)


def _dsilu(z, sg):
    return sg * (1.0 + z * (1.0 - sg))


def post_mix(x, y, zs, o, za, p, tgt, ssd_g, att_g_lane, ple_g, fin_g, w_out, w_gate, w_proj):
    s = x.shape[0]
    tm = _blk(s, 128)
    half = SSD_WIDTH // N_GROUPS

    def rms_bwd(dy, yn, r):
        return r * (dy - yn * jnp.mean(dy * yn, axis=-1, keepdims=True))

    def colsum(a):
        return jnp.sum(a, axis=0, keepdims=True)

    def body(x_ref, y_ref, zs_ref, o_ref, za_ref, p_ref, t_ref, sg_ref, ag_ref, pg_ref, fg_ref,
             wo_ref, wg_ref, wp_ref,
             dh1_ref, dy_ref, dzs_ref, dob_ref, dza_ref, ycat_ref, dh1b_ref, n2b_ref, dglb_ref, dppb_ref, pb_ref,
             loss_ref, dfin_ref, dple_ref, dssd_ref, datt_ref):
        @pl.when(pl.program_id(0) == 0)
        def _():
            for r in (loss_ref, dfin_ref, dple_ref, dssd_ref, datt_ref):
                r[...] = jnp.zeros_like(r)

        lane = _iota((tm, LANES), 1)
        lo = lane < HEAD_DIM
        zs = zs_ref[...]
        sz = _sigmoid(zs)
        yv = y_ref[...]
        ys = yv * (zs * sz)
        yn, rg = [], []
        for g in range(N_GROUPS):
            seg = ys[:, half * g:half * (g + 1)]
            r = lax.rsqrt(jnp.mean(seg * seg, axis=-1, keepdims=True) + EPS)
            yn.append(seg * r)
            rg.append(r)
            ycat_ref[:, half * g:half * (g + 1)] = (yn[g] * sg_ref[:, half * g:half * (g + 1)]).astype(BF16)
        za = za_ref[...]
        sza = _sigmoid(za)
        silu_za = za * sza
        on, ra = [], []
        for jb in range(N_PAIRS):
            blk = o_ref[:, LANES * jb:LANES * (jb + 1)]
            sq = blk * blk
            ms0 = jnp.sum(jnp.where(lo, sq, 0.0), axis=1, keepdims=True) * (1.0 / HEAD_DIM)
            ms1 = jnp.sum(jnp.where(lo, 0.0, sq), axis=1, keepdims=True) * (1.0 / HEAD_DIM)
            r = jnp.where(lo, lax.rsqrt(ms0 + EPS), lax.rsqrt(ms1 + EPS))
            on.append(blk * r)
            ra.append(r)
            an = on[jb] * ag_ref[:, LANES * jb:LANES * (jb + 1)]
            ycat_ref[:, SSD_WIDTH + LANES * jb:SSD_WIDTH + LANES * (jb + 1)] = (
                an * silu_za[:, LANES * jb:LANES * (jb + 1)]).astype(BF16)
        h1 = x_ref[...] + _mm(ycat_ref[...], wo_ref[...])
        r2 = lax.rsqrt(jnp.mean(h1 * h1, axis=-1, keepdims=True) + EPS)
        n2h = h1 * r2
        n2_b = (n2h * pg_ref[...]).astype(BF16)
        gate = _sigmoid(_mm(n2_b, wg_ref[...]))
        p_b = p_ref[...].astype(BF16)
        pp = _mm(p_b, wp_ref[...])
        h2 = h1 + gate * pp
        r3 = lax.rsqrt(jnp.mean(h2 * h2, axis=-1, keepdims=True) + EPS)
        n3 = h2 * r3
        diff = n3 * fg_ref[...] - t_ref[...]
        sq = colsum(diff * diff)
        part = sq[:, 0:LANES]
        for jb in range(1, D_MODEL // LANES):
            part = part + sq[:, LANES * jb:LANES * (jb + 1)]
        loss_ref[...] += part * (0.5 / D_MODEL)
        dout = diff * (1.0 / D_MODEL)
        dfin_ref[...] += colsum(dout * n3)
        dh2 = rms_bwd(dout * fg_ref[...], n3, r3)
        dgl = dh2 * pp * gate * (1.0 - gate)
        dgl_b = dgl.astype(BF16)
        dn2 = _mm_nt(dgl_b, wg_ref[...])
        dple_ref[...] += colsum(dn2 * n2h)
        dh1 = dh2 + rms_bwd(dn2 * pg_ref[...], n2h, r2)
        dh1_b = dh1.astype(BF16)
        dycat = _mm_nt(dh1_b, wo_ref[...])
        dh1_ref[...] = dh1
        dh1b_ref[...] = dh1_b
        n2b_ref[...] = n2_b
        dglb_ref[...] = dgl_b
        dppb_ref[...] = (dh2 * gate).astype(BF16)
        pb_ref[...] = p_b
        for g in range(N_GROUPS):
            cols = slice(half * g, half * (g + 1))
            dys_g = dycat[:, cols]
            dssd_ref[:, cols] += colsum(dys_g * yn[g])
            dys = rms_bwd(dys_g * sg_ref[:, cols], yn[g], rg[g])
            dy_ref[:, cols] = dys * (zs[:, cols] * sz[:, cols])
            dzs_ref[:, cols] = (dys * yv[:, cols] * _dsilu(zs[:, cols], sz[:, cols])).astype(BF16)
        for jb in range(N_PAIRS):
            cols = slice(LANES * jb, LANES * (jb + 1))
            dya = dycat[:, SSD_WIDTH + LANES * jb:SSD_WIDTH + LANES * (jb + 1)]
            ag = ag_ref[:, cols]
            dan = dya * silu_za[:, cols]
            dza_ref[:, cols] = (dya * (on[jb] * ag) * _dsilu(za[:, cols], sza[:, cols])).astype(BF16)
            datt_ref[:, cols] += colsum(dan * on[jb])
            don = dan * ag
            q = don * on[jb]
            m0 = jnp.sum(jnp.where(lo, q, 0.0), axis=1, keepdims=True) * (1.0 / HEAD_DIM)
            m1 = jnp.sum(jnp.where(lo, 0.0, q), axis=1, keepdims=True) * (1.0 / HEAD_DIM)
            do2 = ra[jb] * (don - on[jb] * jnp.where(lo, m0, m1))
            prod = do2 * o_ref[:, cols]
            for e in range(2):
                delta = jnp.sum(jnp.where(lo, prod, 0.0) if e == 0 else jnp.where(lo, 0.0, prod),
                                axis=1, keepdims=True)
                base = jnp.where(lo, do2 if e == 0 else pltpu.roll(do2, HEAD_DIM, 1), 0.0)
                dob_ref[2 * jb + e] = (base - _aug(lane, AUG_A, _split3(delta))).astype(BF16)

    def rows(n, dtype=None):
        return pl.BlockSpec((tm, n), lambda i: (i, 0))

    def out(n, dtype):
        return jax.ShapeDtypeStruct((s, n), dtype)

    vec = _const_spec((1, D_MODEL))
    vshape = jax.ShapeDtypeStruct((1, D_MODEL), F32)
    return pl.pallas_call(
        body, name="post_mix",
        out_shape=(out(D_MODEL, F32), out(SSD_WIDTH, F32), out(SSD_WIDTH, BF16),
                   jax.ShapeDtypeStruct((N_HEADS, s, LANES), BF16),
                   out(ATT_WIDTH, BF16), out(D_INNER, BF16), out(D_MODEL, BF16), out(D_MODEL, BF16),
                   out(D_MODEL, BF16), out(D_MODEL, BF16), out(PLE_DIM, BF16),
                   jax.ShapeDtypeStruct((1, LANES), F32), vshape, vshape, vshape, vshape),
        grid=(s // tm,),
        in_specs=[rows(D_MODEL), rows(SSD_WIDTH), rows(SSD_WIDTH), rows(ATT_WIDTH), rows(ATT_WIDTH),
                  rows(PLE_DIM), rows(D_MODEL), vec, vec, vec, vec,
                  _const_spec((D_INNER, D_MODEL)), _const_spec((D_MODEL, D_MODEL)), _const_spec((PLE_DIM, D_MODEL))],
        out_specs=(rows(D_MODEL), rows(SSD_WIDTH), rows(SSD_WIDTH),
                   pl.BlockSpec((N_HEADS, tm, LANES), lambda i: (0, i, 0)), rows(ATT_WIDTH),
                   rows(D_INNER), rows(D_MODEL), rows(D_MODEL), rows(D_MODEL), rows(D_MODEL), rows(PLE_DIM),
                   _const_spec((1, LANES)), vec, vec, vec, vec),
        compiler_params=_params(("arbitrary",)),
    )(x, y, zs, o, za, p, tgt, ssd_g, att_g_lane, ple_g, fin_g, w_out, w_gate, w_proj)


def in_proj_bwd(dsegs, wsegs, x, g, dh1):
    s = x.shape[0]
    tm = _blk(s, 256)
    nseg = len(dsegs)

    def body(*refs):
        d_refs = refs[:nseg]
        w_refs = refs[nseg:2 * nseg]
        x_ref, g_ref, dh1_ref, dx_ref, dg_ref = refs[2 * nseg:]

        @pl.when(pl.program_id(0) == 0)
        def _():
            dg_ref[...] = jnp.zeros_like(dg_ref)

        du = _mm_nt(d_refs[0][...], w_refs[0][...])
        for k in range(1, nseg):
            du = du + _mm_nt(d_refs[k][...], w_refs[k][...])
        xv = x_ref[...]
        r = lax.rsqrt(jnp.mean(xv * xv, axis=-1, keepdims=True) + EPS)
        xh = xv * r
        dg_ref[...] += jnp.sum(du * xh, axis=0, keepdims=True)
        dxh = du * g_ref[...]
        dx_ref[...] = r * (dxh - xh * jnp.mean(dxh * xh, axis=-1, keepdims=True)) + dh1_ref[...]

    rows = lambda n: pl.BlockSpec((tm, n), lambda i: (i, 0))
    return pl.pallas_call(
        body, name="in_proj_bwd",
        out_shape=(jax.ShapeDtypeStruct((s, D_MODEL), F32), jax.ShapeDtypeStruct((1, D_MODEL), F32)),
        grid=(s // tm,),
        in_specs=([rows(d.shape[1]) for d in dsegs] + [_const_spec(w.shape) for w in wsegs]
                  + [rows(D_MODEL), _const_spec((1, D_MODEL)), rows(D_MODEL)]),
        out_specs=(rows(D_MODEL), _const_spec((1, D_MODEL))),
        compiler_params=_params(("arbitrary",)),
    )(*dsegs, *wsegs, x, g, dh1)


SMALL_NAMES = ("norm_g", "conv_b", "dt_bias", "a_log", "d_skip", "ssd_norm_g", "fg_bias", "att_norm_g",
               "ple_norm_g", "final_norm_g")
SMALL_SIZES = (1024, 1536, 16, 16, 16, 1024, 16, 64, 1024, 1024)
CONV_W_SIZE = CONV_WIDTH * CONV_CH


def _pack_small(vals):
    flat = jnp.concatenate([v.reshape(-1).astype(F32) for v in vals])
    flat = jnp.pad(flat, (0, SMALL_ROWS * LANES - flat.shape[0]))
    return flat.reshape(SMALL_ROWS, LANES)


def _unpack_small(pack, shapes):
    flat = pack.reshape(-1)
    out, off = [], 0
    for n, shp in zip(SMALL_SIZES, shapes):
        out.append(flat[off:off + n].reshape(shp))
        off += n
    return out


def _row128(v16, offset=0):
    return jnp.pad(v16.reshape(1, N_HEADS).astype(F32), ((0, 0), (offset, LANES - N_HEADS - offset)))


def local_step(x, p, tgt, w_in, w_out, w_gate, w_proj, conv_w, norm_g, conv_b, dt_bias, a_log, d_skip,
               ssd_norm_g, fg_bias, att_norm_g, ple_norm_g, final_norm_g):
    c0, c1, c2, c3, c4, c5, c6, c7 = 0, 1024, 2560, 2576, 3600, 4624, 5648, 6672
    w_zs, w_xbc, w_dt = w_in[:, c0:c1], w_in[:, c1:c2], w_in[:, c2:c3]
    w_za, w_q, w_k, w_v, w_f = w_in[:, c3:c4], w_in[:, c4:c5], w_in[:, c5:c6], w_in[:, c6:c7], w_in[:, c7:]
    w_small = jnp.concatenate([w_dt, w_f, jnp.zeros((D_MODEL, LANES - 2 * N_HEADS), BF16)], axis=1)

    dtb_row = _row128(dt_bias)
    a_row = _row128(-jnp.exp(a_log.astype(F32)))
    fgb_row = _row128(fg_bias, N_HEADS)
    dskip_lane = jnp.repeat(d_skip.astype(F32), HEAD_DIM).reshape(1, SSD_WIDTH)
    att_g_lane = jnp.tile(att_norm_g.astype(F32), N_HEADS).reshape(1, ATT_WIDTH)
    row = lambda v: v.reshape(1, -1).astype(F32)

    u = rms_prenorm(x, row(norm_g))
    zs = matmul_rows(u, w_zs, F32, "proj_z_ssd")
    xbc = matmul_rows(u, w_xbc, F32, "proj_xbc")
    za = matmul_rows(u, w_za, F32, "proj_z_att")
    small = matmul_rows(u, w_small, F32, "proj_small")
    cum = forget_cumsum(small, fgb_row)
    qa, ka, va = proj_qkv_heads(u, w_q, w_k, w_v, cum)
    pre, xc = conv_fwd(xbc, conv_w, row(conv_b))
    y, states = ssd_fwd(xc, small, dtb_row, a_row, dskip_lane)
    o, qb = attention_fwd(qa, ka, va)
    (dh1, dy, dzs, dob, dza, ycat, dh1_b, n2_b, dgl_b, dpp_b, p_b,
     loss_l, dfin, dple, dssd_g, datt_lane) = post_mix(
        x, y, zs, o, za, p, tgt, row(ssd_norm_g), att_g_lane, row(ple_norm_g), row(final_norm_g),
        w_out, w_gate, w_proj)
    dq, dk, dv, dc = attention_bwd(qb, ka, va, dob)
    dxc, ddt_raw, da, ddtb, ddsk_lane = ssd_bwd(xc, small, states, dy, dtb_row, a_row, dskip_lane)
    dsmall, dfgb = forget_bwd(dc, small, ddt_raw, fgb_row)
    dxbc, dconv_w8, dconv_b = conv_bwd(xbc, pre, dxc, conv_w)
    dsegs = [dzs, dxbc, dza, dq, dk, dv, dsmall]
    wsegs = [w_zs, w_xbc, w_za, w_q, w_k, w_v, w_small]
    dx, dnorm_g = in_proj_bwd(dsegs, wsegs, x, row(norm_g), dh1)
    dws = [matmul_tn(u, d, "dw_in_%d" % i) for i, d in enumerate(dsegs)]
    dw_in = jnp.concatenate([dws[0], dws[1], dws[6][:, :N_HEADS], dws[2], dws[3], dws[4], dws[5],
                             dws[6][:, N_HEADS:2 * N_HEADS]], axis=1)
    dw_out = matmul_tn(ycat, dh1_b, "dw_out")
    dw_gate = matmul_tn(n2_b, dgl_b, "dw_gate")
    dw_proj = matmul_tn(p_b, dpp_b, "dw_proj")
    small_grads = [
        dnorm_g, dconv_b, ddtb[0, :N_HEADS], (da * a_row)[0, :N_HEADS],
        ddsk_lane.reshape(N_HEADS, HEAD_DIM).sum(axis=1), dssd_g, dfgb[0, N_HEADS:2 * N_HEADS],
        datt_lane.reshape(N_HEADS, HEAD_DIM).sum(axis=0), dple, dfin]
    loss = jnp.sum(loss_l)
    return loss, dx, dw_in, dw_out, dw_gate, dw_proj, dconv_w8[:CONV_WIDTH], small_grads


def kernel(x, p, norm_g, w_in, conv_w, conv_b, dt_bias, a_log, d_skip, ssd_norm_g, fg_bias, att_norm_g, w_out, ple_norm_g, w_ple_gate, w_ple_proj, final_norm_g, loss_target, m_norm_g, m_w_in, m_conv_w, m_conv_b, m_dt_bias, m_a_log, m_d_skip, m_ssd_norm_g, m_fg_bias, m_att_norm_g, m_w_out, m_ple_norm_g, m_w_ple_gate, m_w_ple_proj, m_final_norm_g, v_norm_g, v_w_in, v_conv_w, v_conv_b, v_dt_bias, v_a_log, v_d_skip, v_ssd_norm_g, v_fg_bias, v_att_norm_g, v_w_out, v_ple_norm_g, v_w_ple_gate, v_w_ple_proj, v_final_norm_g):
    chip = 2 * lax.axis_index("x") + lax.axis_index("y")
    core = lax.axis_index("c")

    big_w = [w_in[0], w_out[0], w_ple_gate[0], w_ple_proj[0]]
    own = [a.astype(BF16) for a in big_w] + [conv_w[0]]
    gathered = gather_weights(own[:4], own[4])

    def joined(k, axis):
        return jnp.concatenate([jnp.where(chip == j, own[k], gathered[k][j]) for j in range(N_CHIPS)], axis=axis)

    w_in_f, w_out_f, w_gate_f, w_proj_f, conv_w_f = joined(0, 1), joined(1, 0), joined(2, 0), joined(3, 1), joined(4, 1)

    smalls_w = [norm_g, conv_b, dt_bias, a_log, d_skip, ssd_norm_g, fg_bias, att_norm_g, ple_norm_g, final_norm_g]
    loss_l, dx, dw_in, dw_out, dw_gate, dw_proj, dconv_w, small_grads = local_step(
        x[0], p[0, 0], loss_target[0], w_in_f, w_out_f, w_gate_f, w_proj_f, conv_w_f,
        *[a.reshape(-1) for a in smalls_w])
    loss = lax.psum(loss_l, ("x", "y", "c"))

    gs = [jnp.stack([dw_in[:, 1672 * j:1672 * (j + 1)] for j in range(N_CHIPS)]),
          dw_out.reshape(N_CHIPS, 512, D_MODEL), dw_gate.reshape(N_CHIPS, 256, D_MODEL),
          jnp.stack([dw_proj[:, 256 * j:256 * (j + 1)] for j in range(N_CHIPS)])]
    core1 = core.reshape(1).astype(jnp.int32)
    pres = add_halves(core1, gs, halves_to_sibling(gs))
    *parts, smalls = scatter_halves(pres, _pack_small(list(small_grads) + [dconv_w]))
    mine = sum_parts(parts)

    g_big, d_big, m_big, v_big = adamw_big(
        core1, mine, swap_halves(mine), big_w, [m_w_in[0], m_w_out[0], m_w_ple_gate[0], m_w_ple_proj[0]],
        [v_w_in[0], v_w_out[0], v_w_ple_gate[0], v_w_ple_proj[0]])
    smalls_m = [m_norm_g, m_conv_b, m_dt_bias, m_a_log, m_d_skip, m_ssd_norm_g, m_fg_bias, m_att_norm_g,
                m_ple_norm_g, m_final_norm_g]
    smalls_v = [v_norm_g, v_conv_b, v_dt_bias, v_a_log, v_d_skip, v_ssd_norm_g, v_fg_bias, v_att_norm_g,
                v_ple_norm_g, v_final_norm_g]
    g_sm, d_sm, m_sm, v_sm = adamw_small(smalls, _pack_small(smalls_w), _pack_small(smalls_m), _pack_small(smalls_v))
    n_small = sum(SMALL_SIZES)
    g_conv_full = g_sm.reshape(-1)[n_small:n_small + CONV_W_SIZE].reshape(CONV_WIDTH, CONV_CH)
    g_conv = lax.dynamic_slice_in_dim(g_conv_full, chip * 384, 384, axis=1)
    d_conv, m_conv, v_conv = adamw_whole(g_conv, conv_w[0], m_conv_w[0], v_conv_w[0], "adamw_conv")

    shapes = [a.shape for a in smalls_w]
    outs = []
    for big, conv, sm in ((g_big, g_conv, g_sm), (d_big, d_conv, d_sm), (m_big, m_conv, m_sm), (v_big, v_conv, v_sm)):
        b_in, b_out, b_gate, b_proj = [a[None] for a in big]
        s_norm, s_convb, s_dtb, s_alog, s_dsk, s_ssdg, s_fgb, s_attg, s_pleg, s_fin = _unpack_small(sm, shapes)
        outs.extend([s_norm, b_in, conv[None], s_convb, s_dtb, s_alog, s_dsk, s_ssdg, s_fgb, s_attg, b_out, s_pleg,
                     b_gate, b_proj, s_fin])
    return (loss, dx[None], *outs)
```

```python
import functools

import jax
import jax.numpy as jnp
from jax import lax
from jax.experimental import pallas as pl
from jax.experimental.pallas import tpu as pltpu

F32 = jnp.float32
BF16 = jnp.bfloat16

D_MODEL = 1024
SSD_WIDTH = 1024
ATT_WIDTH = 1024
N_HEADS = 16
HEAD_DIM = 64
N_GROUPS = 2
D_STATE = 128
CONV_CH = 1536
CONV_WIDTH = 4
CHUNK = 128
PLE_DIM = 256
D_INNER = 2048
EPS = 1e-6
IN_COLS = 6688
N_CHIPS = 4
N_DEV = 8
LANES = 128
N_PAIRS = 8

ADAM_LR = 0.001
ADAM_B1 = 0.9
ADAM_B2 = 0.999
ADAM_EPS = 1e-08
ADAM_WD = 0.01
ADAM_STEP = 10

SMALL_ROWS = 96

NEG_BIG = -1e30
VMEM_LIMIT = 56 * 1024 * 1024

MESH = pl.DeviceIdType.MESH
ANY = pl.BlockSpec(memory_space=pl.ANY)


def _mm(a, b):
    return jnp.dot(a, b, preferred_element_type=F32)


def _mm_nt(a, b):
    return lax.dot_general(a, b, (((1,), (1,)), ((), ())), preferred_element_type=F32)


def _mm_tn(a, b):
    return lax.dot_general(a, b, (((0,), (0,)), ((), ())), preferred_element_type=F32)


def _mm_exact(a, b):
    return jnp.dot(a, b, preferred_element_type=F32, precision=lax.Precision.HIGHEST)


def _softplus(x):
    return jnp.maximum(x, 0.0) + jnp.log1p(jnp.exp(-jnp.abs(x)))


def _sigmoid(x):
    return jax.nn.sigmoid(x)


def _iota(shape, dim):
    return lax.broadcasted_iota(jnp.int32, shape, dim)


def _params(sem=None):
    return pltpu.CompilerParams(dimension_semantics=sem, vmem_limit_bytes=VMEM_LIMIT)


def _blk(n, pref):
    return min(n, pref)


def _const_spec(shape):
    nd = len(shape)
    return pl.BlockSpec(shape, lambda *_: (0,) * nd)


def _chip_peers():
    x, y, c = lax.axis_index("x"), lax.axis_index("y"), lax.axis_index("c")
    return x, y, c, [(1 - x, y, c), (x, 1 - y, c), (1 - x, 1 - y, c)]


def _half(rows, c):
    h = rows // 2
    return pl.ds(pl.multiple_of(c * h, 8), h)


def _sems(n):
    return [pltpu.SemaphoreType.DMA((n,)), pltpu.SemaphoreType.DMA((n,))]


def gather_weights(shards, conv_s):
    n = len(shards)

    def body(*refs):
        ins, conv_in = refs[:n], refs[n]
        outs, conv_out = refs[n + 1:2 * n + 1], refs[2 * n + 1]
        ssem1, rsem1, ssem2, rsem2, c_ssem, c_rsem = refs[2 * n + 2:]
        x, y, c, peers = _chip_peers()
        me = 2 * x + y
        sibling = (x, y, 1 - c)
        first, small = [], []
        for k, peer in enumerate(peers):
            for i in range(n):
                h = _half(ins[i].shape[0], c)
                first.append(pltpu.make_async_remote_copy(
                    src_ref=ins[i].at[h], dst_ref=outs[i].at[me, h], send_sem=ssem1.at[n * k + i],
                    recv_sem=rsem1.at[n * k + i], device_id=peer, device_id_type=MESH))
            small.append(pltpu.make_async_remote_copy(
                src_ref=conv_in, dst_ref=conv_out.at[me], send_sem=c_ssem.at[k], recv_sem=c_rsem.at[k],
                device_id=peer, device_id_type=MESH))
        for cp in first + small:
            cp.start()
        passed = []
        for k, peer in enumerate(peers):
            chip = 2 * peer[0] + peer[1]
            for i in range(n):
                h = _half(ins[i].shape[0], c)
                first[n * k + i].wait_recv()
                fwd = pltpu.make_async_remote_copy(
                    src_ref=outs[i].at[chip, h], dst_ref=outs[i].at[chip, h], send_sem=ssem2.at[n * k + i],
                    recv_sem=rsem2.at[n * k + i], device_id=sibling, device_id_type=MESH)
                fwd.start()
                passed.append(fwd)
        for cp in passed:
            cp.wait_recv()
        for cp in first + passed:
            cp.wait_send()
        for cp in small:
            cp.wait()

    return pl.pallas_call(
        body, name="gather_weights",
        out_shape=tuple(jax.ShapeDtypeStruct((N_CHIPS,) + a.shape, a.dtype) for a in list(shards) + [conv_s]),
        in_specs=[ANY] * (n + 1), out_specs=(ANY,) * (n + 1),
        scratch_shapes=_sems(3 * n) + _sems(3 * n) + _sems(3),
    )(*shards, conv_s)


def halves_to_sibling(gs):
    n = len(gs)

    def body(*refs):
        ins, outs = refs[:n], refs[n:2 * n]
        ssem, rsem = refs[2 * n:]
        x, y, c = lax.axis_index("x"), lax.axis_index("y"), lax.axis_index("c")
        copies = []
        for i in range(n):
            for j in range(N_CHIPS):
                copies.append(pltpu.make_async_remote_copy(
                    src_ref=ins[i].at[j, _half(ins[i].shape[1], 1 - c)], dst_ref=outs[i].at[j],
                    send_sem=ssem.at[N_CHIPS * i + j], recv_sem=rsem.at[N_CHIPS * i + j],
                    device_id=(x, y, 1 - c), device_id_type=MESH))
        for cp in copies:
            cp.start()
        for cp in copies:
            cp.wait()

    return pl.pallas_call(
        body, name="halves_to_sibling",
        out_shape=tuple(jax.ShapeDtypeStruct((N_CHIPS, g.shape[1] // 2, g.shape[2]), F32) for g in gs),
        in_specs=[ANY] * n, out_specs=(ANY,) * n, scratch_shapes=_sems(N_CHIPS * n),
    )(*gs)


RED_GRID = 8


def add_halves(core, gs, rbs):
    n = len(gs)

    def body(c_ref, *refs):
        for i in range(n):
            refs[2 * n + i][...] = (refs[i][...] + refs[n + i][...]).astype(BF16)

    def blk(g):
        return (1, g.shape[1] // 2 // RED_GRID, g.shape[2])

    grid_spec = pltpu.PrefetchScalarGridSpec(
        num_scalar_prefetch=1, grid=(N_CHIPS, RED_GRID),
        in_specs=([pl.BlockSpec(blk(g), lambda j, b, c_ref: (j, c_ref[0] * RED_GRID + b, 0)) for g in gs]
                  + [pl.BlockSpec(blk(g), lambda j, b, c_ref: (j, b, 0)) for g in gs]),
        out_specs=[pl.BlockSpec(blk(g), lambda j, b, c_ref: (j, b, 0)) for g in gs])
    return pl.pallas_call(
        body, name="add_halves", grid_spec=grid_spec,
        out_shape=tuple(jax.ShapeDtypeStruct(r.shape, BF16) for r in rbs),
        compiler_params=_params(("parallel", "parallel")),
    )(core, *gs, *rbs)


def scatter_halves(pres, small):
    n = len(pres)

    def body(*refs):
        ins, s_ref = refs[:n], refs[n]
        outs, smalls_ref = refs[n + 1:2 * n + 1], refs[2 * n + 1]
        ssem, rsem, s_ssem, s_rsem, lsem = refs[2 * n + 2:]
        x, y, c, peers = _chip_peers()
        me = 2 * x + y
        dev = 4 * x + 2 * y + c
        local = [pltpu.make_async_copy(ins[i].at[me], outs[i].at[me], lsem.at[i]) for i in range(n)]
        local.append(pltpu.make_async_copy(s_ref, smalls_ref.at[dev], lsem.at[n]))
        for cp in local:
            cp.start()
        remote = []
        for k, peer in enumerate(peers):
            dst_chip = 2 * peer[0] + peer[1]
            for i in range(n):
                remote.append(pltpu.make_async_remote_copy(
                    src_ref=ins[i].at[dst_chip], dst_ref=outs[i].at[me], send_sem=ssem.at[n * k + i],
                    recv_sem=rsem.at[n * k + i], device_id=peer, device_id_type=MESH))
        for k in range(1, N_DEV):
            fx, fy, fc = (k >> 2) & 1, (k >> 1) & 1, k & 1
            peer = ((1 - x) if fx else x, (1 - y) if fy else y, (1 - c) if fc else c)
            remote.append(pltpu.make_async_remote_copy(
                src_ref=s_ref, dst_ref=smalls_ref.at[dev], send_sem=s_ssem.at[k - 1], recv_sem=s_rsem.at[k - 1],
                device_id=peer, device_id_type=MESH))
        for cp in remote:
            cp.start()
        for cp in remote:
            cp.wait()
        for cp in local:
            cp.wait()

    return pl.pallas_call(
        body, name="scatter_halves",
        out_shape=tuple([jax.ShapeDtypeStruct(a.shape, a.dtype) for a in pres]
                        + [jax.ShapeDtypeStruct((N_DEV,) + small.shape, F32)]),
        in_specs=[ANY] * (n + 1), out_specs=(ANY,) * (n + 1),
        scratch_shapes=_sems(3 * n) + _sems(N_DEV - 1) + [pltpu.SemaphoreType.DMA((n + 1,))],
    )(*pres, small)


def sum_parts(parts):
    n = len(parts)

    def body(*refs):
        for i in range(n):
            p_ref = refs[i]
            refs[n + i][...] = ((p_ref[0].astype(F32) + p_ref[1].astype(F32)) + p_ref[2].astype(F32)
                                ) + p_ref[3].astype(F32)

    def rows(p):
        return p.shape[1] // RED_GRID

    return pl.pallas_call(
        body, name="sum_parts",
        out_shape=tuple(jax.ShapeDtypeStruct(p.shape[1:], F32) for p in parts),
        grid=(RED_GRID,),
        in_specs=[pl.BlockSpec((N_CHIPS, rows(p), p.shape[2]), lambda b: (0, b, 0)) for p in parts],
        out_specs=tuple(pl.BlockSpec((rows(p), p.shape[2]), lambda b: (b, 0)) for p in parts),
        compiler_params=_params(("parallel",)),
    )(*parts)


def swap_halves(reds):
    n = len(reds)

    def body(*refs):
        ins, outs = refs[:n], refs[n:2 * n]
        ssem, rsem = refs[2 * n:]
        x, y, c = lax.axis_index("x"), lax.axis_index("y"), lax.axis_index("c")
        copies = [pltpu.make_async_remote_copy(
            src_ref=ins[i], dst_ref=outs[i], send_sem=ssem.at[i], recv_sem=rsem.at[i],
            device_id=(x, y, 1 - c), device_id_type=MESH) for i in range(n)]
        for cp in copies:
            cp.start()
        for cp in copies:
            cp.wait()

    return pl.pallas_call(
        body, name="swap_halves",
        out_shape=tuple(jax.ShapeDtypeStruct(r.shape, F32) for r in reds),
        in_specs=[ANY] * n, out_specs=(ANY,) * n, scratch_shapes=_sems(n),
    )(*reds)


def _adamw(w, g, m, v):
    m = ADAM_B1 * m + (1.0 - ADAM_B1) * g
    v = ADAM_B2 * v + (1.0 - ADAM_B2) * (g * g)
    m_hat = m / (1.0 - ADAM_B1 ** ADAM_STEP)
    v_hat = v / (1.0 - ADAM_B2 ** ADAM_STEP)
    delta = -ADAM_LR * (m_hat / (jnp.sqrt(v_hat) + ADAM_EPS) + ADAM_WD * w)
    return delta, m, v


def adamw_big(core, mine, theirs, ws, ms, vs):
    n = len(ws)
    per_half = RED_GRID // 2

    def body(c_ref, *refs):
        own = (pl.program_id(0) // per_half) == c_ref[0]
        for i in range(n):
            g = jnp.where(own, refs[i][...], refs[n + i][...])
            d, mn, vn = _adamw(refs[2 * n + i][...], g, refs[3 * n + i][...], refs[4 * n + i][...])
            refs[5 * n + i][...] = g
            refs[6 * n + i][...] = d
            refs[7 * n + i][...] = mn
            refs[8 * n + i][...] = vn

    def blk(w):
        return (w.shape[0] // RED_GRID, w.shape[1])

    halves = [pl.BlockSpec(blk(w), lambda b, c_ref: (b % per_half, 0)) for w in ws]
    whole = [pl.BlockSpec(blk(w), lambda b, c_ref: (b, 0)) for w in ws]
    shapes = [jax.ShapeDtypeStruct(w.shape, F32) for w in ws]
    grid_spec = pltpu.PrefetchScalarGridSpec(
        num_scalar_prefetch=1, grid=(RED_GRID,), in_specs=halves * 2 + whole * 3, out_specs=whole * 4)
    outs = pl.pallas_call(
        body, name="adamw_big", out_shape=tuple(shapes * 4), grid_spec=grid_spec,
        compiler_params=_params(("parallel",)),
    )(core, *mine, *theirs, *ws, *ms, *vs)
    return outs[:n], outs[n:2 * n], outs[2 * n:3 * n], outs[3 * n:]


def adamw_whole(g, w, m, v, name):
    def body(g_ref, w_ref, m_ref, v_ref, d_out, m_out, v_out):
        d, mn, vn = _adamw(w_ref[...], g_ref[...], m_ref[...], v_ref[...])
        d_out[...] = d
        m_out[...] = mn
        v_out[...] = vn

    shp = jax.ShapeDtypeStruct(g.shape, F32)
    return pl.pallas_call(body, name=name, out_shape=(shp,) * 3)(g, w, m, v)


def adamw_small(smalls, w, m, v):
    def body(s_ref, w_ref, m_ref, v_ref, g_out, d_out, m_out, v_out):
        g = s_ref[0]
        for k in range(1, N_DEV):
            g = g + s_ref[k]
        d, mn, vn = _adamw(w_ref[...], g, m_ref[...], v_ref[...])
        g_out[...] = g
        d_out[...] = d
        m_out[...] = mn
        v_out[...] = vn

    shp = jax.ShapeDtypeStruct((SMALL_ROWS, LANES), F32)
    return pl.pallas_call(body, name="adamw_small", out_shape=(shp,) * 4)(smalls, w, m, v)


def rms_prenorm(x, g):
    s = x.shape[0]
    tm = _blk(s, 512)

    def body(x_ref, g_ref, u_ref):
        xv = x_ref[...]
        r = lax.rsqrt(jnp.mean(xv * xv, axis=-1, keepdims=True) + EPS)
        u_ref[...] = (xv * r * g_ref[...]).astype(BF16)

    return pl.pallas_call(
        body, name="rms_prenorm", out_shape=jax.ShapeDtypeStruct(x.shape, BF16), grid=(s // tm,),
        in_specs=[pl.BlockSpec((tm, D_MODEL), lambda i: (i, 0)), _const_spec((1, D_MODEL))],
        out_specs=pl.BlockSpec((tm, D_MODEL), lambda i: (i, 0)), compiler_params=_params(("parallel",)),
    )(x, g)


def matmul_rows(a, w, out_dtype, name):
    s, k = a.shape
    n = w.shape[1]
    tm = _blk(s, 512)

    def body(a_ref, w_ref, o_ref):
        o_ref[...] = _mm(a_ref[...], w_ref[...]).astype(out_dtype)

    return pl.pallas_call(
        body, name=name, out_shape=jax.ShapeDtypeStruct((s, n), out_dtype), grid=(s // tm,),
        in_specs=[pl.BlockSpec((tm, k), lambda i: (i, 0)), _const_spec((k, n))],
        out_specs=pl.BlockSpec((tm, n), lambda i: (i, 0)), compiler_params=_params(("parallel",)),
    )(a, w)


def matmul_tn(a, b, name):
    s, m = a.shape
    n = b.shape[1]
    tk = _blk(s, 2048)
    tn = _blk(n, 512)

    def body(a_ref, b_ref, o_ref):
        @pl.when(pl.program_id(1) == 0)
        def _():
            o_ref[...] = jnp.zeros_like(o_ref)

        o_ref[...] += _mm_tn(a_ref[...], b_ref[...])

    return pl.pallas_call(
        body, name=name, out_shape=jax.ShapeDtypeStruct((m, n), F32), grid=(n // tn, s // tk),
        in_specs=[pl.BlockSpec((tk, m), lambda j, i: (i, 0)), pl.BlockSpec((tk, tn), lambda j, i: (i, j))],
        out_specs=pl.BlockSpec((m, tn), lambda j, i: (0, j)),
        compiler_params=_params(("parallel", "arbitrary")),
    )(a, b)


def conv_fwd(xbc, w, b):
    s = xbc.shape[0]
    tm = _blk(s, 256)

    def body(x_ref, t_ref, w_ref, b_ref, pre_ref, act_ref):
        i = pl.program_id(0)
        cur = x_ref[...]
        tail = jnp.where(i > 0, t_ref[...], 0.0)
        wv = w_ref[...]
        acc = cur * wv[3:4, :] + b_ref[...]
        head = cur[0:8, :] * wv[3:4, :] + b_ref[...]
        row8 = _iota((8, CONV_CH), 0)
        for sh in range(1, CONV_WIDTH):
            wk = wv[3 - sh:4 - sh, :]
            acc = acc + pltpu.roll(cur, sh, 0) * wk
            first = jnp.where(row8 < sh, pltpu.roll(tail, sh, 0), pltpu.roll(cur[0:8, :], sh, 0))
            head = head + first * wk
        pre_ref[...] = acc
        act_ref[...] = acc * _sigmoid(acc)
        pre_ref[0:8, :] = head
        act_ref[0:8, :] = head * _sigmoid(head)

    shp = jax.ShapeDtypeStruct(xbc.shape, F32)
    rows = pl.BlockSpec((tm, CONV_CH), lambda i: (i, 0))
    return pl.pallas_call(
        body, name="conv_fwd", out_shape=(shp, shp), grid=(s // tm,),
        in_specs=[rows, pl.BlockSpec((8, CONV_CH), lambda i: (jnp.maximum(i * (tm // 8) - 1, 0), 0)),
                  _const_spec((CONV_WIDTH, CONV_CH)), _const_spec((1, CONV_CH))],
        out_specs=(rows, rows), compiler_params=_params(("parallel",)),
    )(xbc, xbc, w, b)


def conv_bwd(xbc, pre, dact, w):
    s = xbc.shape[0]
    tm = _blk(s, 256)
    nb = s // tm

    def dsilu(p):
        sg = _sigmoid(p)
        return sg * (1.0 + p * (1.0 - sg))

    def body(x_ref, xt_ref, p_ref, pn_ref, d_ref, dn_ref, w_ref, dx_ref, dw_ref, db_ref):
        i = pl.program_id(0)

        @pl.when(i == 0)
        def _():
            dw_ref[...] = jnp.zeros_like(dw_ref)
            db_ref[...] = jnp.zeros_like(db_ref)

        wv = w_ref[...]
        dpre = d_ref[...] * dsilu(p_ref[...])
        dnext = jnp.where(i < nb - 1, dn_ref[...] * dsilu(pn_ref[...]), 0.0)
        cur = x_ref[...]
        tail = jnp.where(i > 0, xt_ref[...], 0.0)
        row8 = _iota((8, CONV_CH), 0)
        dx = dpre * wv[3:4, :]
        last = dpre[tm - 8:tm, :] * wv[3:4, :]
        db_ref[...] += jnp.sum(dpre, axis=0, keepdims=True)
        dws = [jnp.sum(dpre * cur, axis=0, keepdims=True)]
        for sh in range(1, CONV_WIDTH):
            wk = wv[3 - sh:4 - sh, :]
            dx = dx + pltpu.roll(dpre, tm - sh, 0) * wk
            nxt = jnp.where(row8 >= 8 - sh, pltpu.roll(dnext, 8 - sh, 0), pltpu.roll(dpre[tm - 8:tm, :], 8 - sh, 0))
            last = last + nxt * wk
            xs = pltpu.roll(cur, sh, 0)
            first = jnp.where(row8 < sh, pltpu.roll(tail, sh, 0), xs[0:8, :])
            dws.append(jnp.sum(dpre * xs, axis=0, keepdims=True)
                       + jnp.sum(dpre[0:8, :] * (first - xs[0:8, :]), axis=0, keepdims=True))
        dx_ref[...] = dx.astype(BF16)
        dx_ref[tm - 8:tm, :] = last.astype(BF16)
        for sh in range(CONV_WIDTH):
            dw_ref[3 - sh:4 - sh, :] += dws[sh]

    rows = pl.BlockSpec((tm, CONV_CH), lambda i: (i, 0))
    prev8 = pl.BlockSpec((8, CONV_CH), lambda i: (jnp.maximum(i * (tm // 8) - 1, 0), 0))
    next8 = pl.BlockSpec((8, CONV_CH), lambda i: (jnp.minimum((i + 1) * (tm // 8), s // 8 - 1), 0))
    return pl.pallas_call(
        body, name="conv_bwd",
        out_shape=(jax.ShapeDtypeStruct(xbc.shape, BF16), jax.ShapeDtypeStruct((8, CONV_CH), F32),
                   jax.ShapeDtypeStruct((1, CONV_CH), F32)),
        grid=(nb,),
        in_specs=[rows, prev8, rows, next8, rows, next8, _const_spec((CONV_WIDTH, CONV_CH))],
        out_specs=(rows, _const_spec((8, CONV_CH)), _const_spec((1, CONV_CH))),
        compiler_params=_params(("arbitrary",)),
    )(xbc, xbc, pre, pre, dact, dact, w)


def _pair_lanes(mat, j, lane):
    return jnp.where(lane < HEAD_DIM, mat[:, 2 * j:2 * j + 1], mat[:, 2 * j + 1:2 * j + 2])


def _ssd_chunk_prelude(sm, dtb, a_row, lane, sub):
    raw = sm + dtb
    head_lane = lane < N_HEADS
    dt = jnp.where(head_lane, _softplus(raw), 0.0)
    sig = jnp.where(head_lane, _sigmoid(raw), 0.0)
    tri = (lane <= sub).astype(F32)
    acs = _mm_exact(tri, dt * a_row)
    return dt, sig, acs, acs.T


GROUP_WIDTH = SSD_WIDTH // N_GROUPS
HEADS_PER_GROUP = N_HEADS // N_GROUPS


def _expand_group(mat, g, lane):
    return jnp.concatenate([_pair_lanes(mat, j, lane) for j in range(4 * g, 4 * g + 4)], axis=1)


def _head_sums(q, g):
    row = _iota((GROUP_WIDTH, LANES), 0)
    seg = (_iota((GROUP_WIDTH, LANES), 1) == HEADS_PER_GROUP * g + (row >> 6)).astype(BF16)
    hi = q.astype(BF16)
    lo = (q - hi.astype(F32)).astype(BF16)
    return _mm(hi, seg) + _mm(lo, seg)


def _rows_from_lanes(row512):
    return jnp.broadcast_to(row512, (LANES, GROUP_WIDTH)).T


def ssd_fwd(xc, small, dtb_row, a_row, dskip_lane):
    s = xc.shape[0]
    nc = s // CHUNK

    def body(xc_ref, sm_ref, dtb_ref, a_ref, dsk_ref, y_ref, hs_ref, h_scr):
        c = pl.program_id(0)

        @pl.when(c == 0)
        def _():
            h_scr[...] = jnp.zeros_like(h_scr)

        lane = _iota((CHUNK, LANES), 1)
        sub = _iota((CHUNK, LANES), 0)
        causal = lane <= sub
        dt, _, acs, acs_t = _ssd_chunk_prelude(sm_ref[...], dtb_ref[...], a_ref[...], lane, sub)
        for g in range(N_GROUPS):
            cols = slice(GROUP_WIDTH * g, GROUP_WIDTH * (g + 1))
            b_off = SSD_WIDTH + D_STATE * g
            c_off = SSD_WIDTH + N_GROUPS * D_STATE + D_STATE * g
            b_b = xc_ref[:, b_off:b_off + D_STATE].astype(BF16)
            c_b = xc_ref[:, c_off:c_off + D_STATE].astype(BF16)
            cb = _mm_nt(c_b, b_b)
            x_g = xc_ref[:, cols]
            acs_g = _expand_group(acs, g, lane)
            xdt_g = x_g * _expand_group(dt, g, lane)
            xdt_b = xdt_g.astype(BF16)
            heads = range(HEADS_PER_GROUP * g, HEADS_PER_GROUP * (g + 1))
            m_b = [(cb * jnp.exp(jnp.where(causal, acs[:, h:h + 1] - acs_t[h:h + 1, :], NEG_BIG))).astype(BF16)
                   for h in heads]
            yd = [_mm(m_b[k], xdt_b[:, LANES * (k // 2):LANES * (k // 2 + 1)]) for k in range(HEADS_PER_GROUP)]
            yd_g = jnp.concatenate([jnp.where(lane < HEAD_DIM, yd[2 * k], yd[2 * k + 1]) for k in range(4)], axis=1)
            h_g = h_scr[g]
            t_g = _mm_nt(c_b, h_g.astype(BF16))
            y_ref[:, cols] = yd_g + jnp.exp(acs_g) * t_g + dsk_ref[:, cols] * x_g
            hs_ref[0, g] = h_g
            last_g = acs_g[CHUNK - 1:CHUNK, :]
            w_b = (xdt_g * jnp.exp(last_g - acs_g)).astype(BF16)
            h_scr[g] = h_g * jnp.exp(_rows_from_lanes(last_g)) + _mm_tn(w_b, b_b)

    return pl.pallas_call(
        body, name="ssd_fwd",
        out_shape=(jax.ShapeDtypeStruct((s, SSD_WIDTH), F32),
                   jax.ShapeDtypeStruct((nc, N_GROUPS, GROUP_WIDTH, D_STATE), F32)),
        grid=(nc,),
        in_specs=[pl.BlockSpec((CHUNK, CONV_CH), lambda c: (c, 0)), pl.BlockSpec((CHUNK, LANES), lambda c: (c, 0)),
                  _const_spec((1, LANES)), _const_spec((1, LANES)), _const_spec((1, SSD_WIDTH))],
        out_specs=(pl.BlockSpec((CHUNK, SSD_WIDTH), lambda c: (c, 0)),
                   pl.BlockSpec((1, N_GROUPS, GROUP_WIDTH, D_STATE), lambda c: (c, 0, 0, 0))),
        scratch_shapes=[pltpu.VMEM((N_GROUPS, GROUP_WIDTH, D_STATE), F32)],
        compiler_params=_params(("arbitrary",)),
    )(xc, small, dtb_row, a_row, dskip_lane)


def ssd_bwd(xc, small, states, dy, dtb_row, a_row, dskip_lane):
    s = xc.shape[0]
    nc = s // CHUNK
    rev = lambda c: nc - 1 - c

    def body(xc_ref, sm_ref, hs_ref, dy_ref, dtb_ref, a_ref, dsk_ref,
             dxc_ref, ddt_ref, da_ref, ddtb_ref, ddsk_ref, dh_scr):
        c = pl.program_id(0)

        @pl.when(c == 0)
        def _():
            dh_scr[...] = jnp.zeros_like(dh_scr)
            da_ref[...] = jnp.zeros_like(da_ref)
            ddtb_ref[...] = jnp.zeros_like(ddtb_ref)
            ddsk_ref[...] = jnp.zeros_like(ddsk_ref)

        lane = _iota((CHUNK, LANES), 1)
        sub = _iota((CHUNK, LANES), 0)
        causal = lane <= sub
        upper = lane >= sub
        is_last = sub == CHUNK - 1
        a_row_v = a_ref[...]
        dt, sig, acs, acs_t = _ssd_chunk_prelude(sm_ref[...], dtb_ref[...], a_row_v, lane, sub)
        cd = jnp.exp(acs[CHUNK - 1:CHUNK, :])
        dacs_c = jnp.zeros((CHUNK, LANES), F32)
        dacs_r = jnp.zeros((LANES, CHUNK), F32)
        ddtx = jnp.zeros((CHUNK, LANES), F32)
        for g in range(N_GROUPS):
            cols = slice(GROUP_WIDTH * g, GROUP_WIDTH * (g + 1))
            b_off = SSD_WIDTH + D_STATE * g
            c_off = SSD_WIDTH + N_GROUPS * D_STATE + D_STATE * g
            b_b = xc_ref[:, b_off:b_off + D_STATE].astype(BF16)
            c_b = xc_ref[:, c_off:c_off + D_STATE].astype(BF16)
            cb = _mm_nt(c_b, b_b)
            cb_t = _mm_nt(b_b, c_b)
            x_g = xc_ref[:, cols]
            dy_g = dy_ref[:, cols]
            dt_g = _expand_group(dt, g, lane)
            acs_g = _expand_group(acs, g, lane)
            last_g = acs_g[CHUNK - 1:CHUNK, :]
            e_g = jnp.exp(acs_g)
            dte_g = jnp.exp(last_g - acs_g)
            xdt_g = x_g * dt_g
            xdt_b = xdt_g.astype(BF16)
            h_g = hs_ref[0, g]
            dh_g = dh_scr[g]
            h_b = h_g.astype(BF16)
            dh_b = dh_g.astype(BF16)
            heads = list(range(HEADS_PER_GROUP * g, HEADS_PER_GROUP * (g + 1)))
            segs = [acs[:, h:h + 1] - acs_t[h:h + 1, :] for h in heads]
            lms = [jnp.exp(jnp.where(causal, sg, NEG_BIG)) for sg in segs]
            mts = [(cb_t * jnp.exp(jnp.where(upper, -sg, NEG_BIG))).astype(BF16) for sg in segs]
            dyh = []
            for k in range(HEADS_PER_GROUP):
                blk = dy_g[:, LANES * (k // 2):LANES * (k // 2 + 1)]
                in_head = (lane < HEAD_DIM) if k % 2 == 0 else (lane >= HEAD_DIM)
                dyh.append(jnp.where(in_head, blk, 0.0).astype(BF16))
            dms = [_mm_nt(dyh[k], xdt_b[:, LANES * (k // 2):LANES * (k // 2 + 1)]) for k in range(HEADS_PER_GROUP)]
            dxs = [_mm(mts[k], dyh[k]) for k in range(HEADS_PER_GROUP)]
            dcb = jnp.zeros((CHUNK, CHUNK), F32)
            for k, h in enumerate(heads):
                gmat = dms[k] * (cb * lms[k])
                dacs_c = dacs_c + jnp.where(lane == h, jnp.sum(gmat, axis=1, keepdims=True), 0.0)
                dacs_r = dacs_r - jnp.where(sub == h, jnp.sum(gmat, axis=0, keepdims=True), 0.0)
                dcb = dcb + dms[k] * lms[k]
            dxdt_g = jnp.concatenate([dxs[2 * k] + dxs[2 * k + 1] for k in range(4)], axis=1)
            t_g = _mm_nt(c_b, h_b)
            dacs_c = dacs_c + _head_sums(dy_g * e_g * t_g, g)
            dt_b = (dy_g * e_g).astype(BF16)
            dc_acc = _mm(dt_b, h_b)
            dh_prev = _mm_tn(dt_b, c_b)
            dw_g = _mm_nt(b_b, dh_b)
            w_g = xdt_g * dte_g
            dxdt_g = dxdt_g + dw_g * dte_g
            db_acc = _mm(w_g.astype(BF16), dh_b)
            r2 = _head_sums(dw_g * w_g, g)
            dacs_c = dacs_c + jnp.where(is_last, jnp.sum(r2, axis=0, keepdims=True), 0.0) - r2
            q3 = jnp.sum(dh_g * h_g, axis=1, keepdims=True)
            for k, h in enumerate(heads):
                tot = jnp.sum(q3[HEAD_DIM * k:HEAD_DIM * (k + 1), :], keepdims=True) * cd[:, h:h + 1]
                dacs_c = dacs_c + jnp.where(is_last & (lane == h), tot, 0.0)
            dh_scr[g] = dh_prev + dh_g * jnp.exp(_rows_from_lanes(last_g))
            dxc_ref[:, cols] = dxdt_g * dt_g + dsk_ref[:, cols] * dy_g
            ddtx = ddtx + _head_sums(dxdt_g * x_g, g)
            ddsk_ref[:, cols] += jnp.sum(dy_g * x_g, axis=0, keepdims=True)
            dxc_ref[:, b_off:b_off + D_STATE] = db_acc + _mm(dcb.T.astype(BF16), c_b)
            dxc_ref[:, c_off:c_off + D_STATE] = dc_acc + _mm(dcb.astype(BF16), b_b)
        dacs = dacs_c + dacs_r.T
        dadt = _mm_exact((lane >= sub).astype(F32), dacs)
        ddt = dadt * a_row_v + ddtx
        ddt_raw = ddt * sig
        ddt_ref[...] = ddt_raw
        da_ref[...] += jnp.sum(dadt * dt, axis=0, keepdims=True)
        ddtb_ref[...] += jnp.sum(ddt_raw, axis=0, keepdims=True)

    return pl.pallas_call(
        body, name="ssd_bwd",
        out_shape=(jax.ShapeDtypeStruct((s, CONV_CH), F32), jax.ShapeDtypeStruct((s, LANES), F32),
                   jax.ShapeDtypeStruct((1, LANES), F32), jax.ShapeDtypeStruct((1, LANES), F32),
                   jax.ShapeDtypeStruct((1, SSD_WIDTH), F32)),
        grid=(nc,),
        in_specs=[pl.BlockSpec((CHUNK, CONV_CH), lambda c: (rev(c), 0)),
                  pl.BlockSpec((CHUNK, LANES), lambda c: (rev(c), 0)),
                  pl.BlockSpec((1, N_GROUPS, GROUP_WIDTH, D_STATE), lambda c: (rev(c), 0, 0, 0)),
                  pl.BlockSpec((CHUNK, SSD_WIDTH), lambda c: (rev(c), 0)),
                  _const_spec((1, LANES)), _const_spec((1, LANES)), _const_spec((1, SSD_WIDTH))],
        out_specs=(pl.BlockSpec((CHUNK, CONV_CH), lambda c: (rev(c), 0)),
                   pl.BlockSpec((CHUNK, LANES), lambda c: (rev(c), 0)),
                   _const_spec((1, LANES)), _const_spec((1, LANES)), _const_spec((1, SSD_WIDTH))),
        scratch_shapes=[pltpu.VMEM((N_GROUPS, GROUP_WIDTH, D_STATE), F32)],
        compiler_params=_params(("arbitrary",)),
    )(xc, small, states, dy, dtb_row, a_row, dskip_lane)


FORGET_BLOCK = 512


def forget_cumsum(small, fgb_row):
    s = small.shape[0]
    t = _blk(s, FORGET_BLOCK)
    nb = s // t

    def body(sm_ref, b_ref, cc_ref, carry):
        i = pl.program_id(0)

        @pl.when(i == 0)
        def _():
            carry[...] = jnp.zeros_like(carry)

        lane = _iota((t, LANES), 1)
        in_f = (lane >= N_HEADS) & (lane < 2 * N_HEADS)
        logf = jnp.where(in_f, -_softplus(-(sm_ref[...] + b_ref[...])), 0.0)
        tri = (_iota((t, t), 1) <= _iota((t, t), 0)).astype(F32)
        cum = _mm_exact(tri, logf) + carry[0:1, :]
        cc_ref[...] = cum
        carry[...] = jnp.broadcast_to(cum[t - 1:t, :], (8, LANES))

    return pl.pallas_call(
        body, name="forget_cumsum",
        out_shape=jax.ShapeDtypeStruct((s, LANES), F32),
        grid=(nb,),
        in_specs=[pl.BlockSpec((t, LANES), lambda i: (i, 0)), _const_spec((1, LANES))],
        out_specs=pl.BlockSpec((t, LANES), lambda i: (i, 0)),
        scratch_shapes=[pltpu.VMEM((8, LANES), F32)],
        compiler_params=_params(("arbitrary",)),
    )(small, fgb_row)


def forget_bwd(dc, small, ddt_raw, fgb_row):
    s = small.shape[0]
    t = _blk(s, FORGET_BLOCK)
    nb = s // t
    rev = lambda i: nb - 1 - i

    def body(dc_ref, sm_ref, ddt_ref, b_ref, ds_ref, dfb_ref, carry):
        i = pl.program_id(0)

        @pl.when(i == 0)
        def _():
            carry[...] = jnp.zeros_like(carry)
            dfb_ref[...] = jnp.zeros_like(dfb_ref)

        lane = _iota((t, LANES), 1)
        rows = dc_ref[...].T
        tri = (_iota((t, t), 1) <= _iota((t, t), 0)).astype(F32)
        rc = _mm_exact(rows, tri) + carry[:, 0:1]
        carry[...] = jnp.broadcast_to(rc[:, 0:1], (LANES, LANES))
        in_f = (lane >= N_HEADS) & (lane < 2 * N_HEADS)
        df = jnp.where(in_f, rc.T * _sigmoid(-(sm_ref[...] + b_ref[...])), 0.0)
        ds_ref[...] = (df + ddt_ref[...]).astype(BF16)
        dfb_ref[...] += jnp.sum(df, axis=0, keepdims=True)

    blk = pl.BlockSpec((t, LANES), lambda i: (rev(i), 0))
    return pl.pallas_call(
        body, name="forget_bwd",
        out_shape=(jax.ShapeDtypeStruct((s, LANES), BF16), jax.ShapeDtypeStruct((1, LANES), F32)),
        grid=(nb,),
        in_specs=[blk, blk, blk, _const_spec((1, LANES))],
        out_specs=(blk, _const_spec((1, LANES))),
        scratch_shapes=[pltpu.VMEM((LANES, LANES), F32)],
        compiler_params=_params(("arbitrary",)),
    )(dc, small, ddt_raw, fgb_row)


ATT_BLOCK = 512
ATT_SCALE = HEAD_DIM ** -0.5
AUG_A = HEAD_DIM
AUG_B = HEAD_DIM + 3


def _split3(c):
    hi = c.astype(BF16).astype(F32)
    r = c - hi
    mid = r.astype(BF16).astype(F32)
    return hi, mid, (r - mid).astype(BF16).astype(F32)


def _aug(lane, first, parts=None, value=1.0):
    if parts is None:
        return jnp.where((lane >= first) & (lane < first + 3), value, 0.0)
    return (jnp.where(lane == first, parts[0], 0.0) + jnp.where(lane == first + 1, parts[1], 0.0)
            + jnp.where(lane == first + 2, parts[2], 0.0))


def _pack_pair(a0, a1, lane):
    return jnp.where(lane < HEAD_DIM, a0, pltpu.roll(a1, HEAD_DIM, 1))


def proj_qkv_heads(u, w_q, w_k, w_v, cum):
    s = u.shape[0]
    tm = _blk(s, 256)

    def body(u_ref, wq_ref, wk_ref, wv_ref, c_ref, qa_ref, ka_ref, va_ref):
        lane = _iota((tm, LANES), 1)
        lo = lane < HEAD_DIM
        uv = u_ref[...]
        qf = _mm(uv, wq_ref[...]) * ATT_SCALE
        kf = _mm(uv, wk_ref[...])
        vf = _mm(uv, wv_ref[...])
        cc = c_ref[...]
        ones_a = _aug(lane, AUG_A)
        ones_b = _aug(lane, AUG_B)
        for h in range(N_HEADS):
            j, e = divmod(h, 2)

            def head(full):
                blk = full[:, LANES * j:LANES * (j + 1)]
                if e == 1:
                    blk = pltpu.roll(blk, HEAD_DIM, 1)
                return jnp.where(lo, blk, 0.0)

            parts = _split3(cc[:, N_HEADS + h:N_HEADS + h + 1])
            qa_ref[h] = (head(qf) + _aug(lane, AUG_A, parts) + ones_b).astype(BF16)
            ka_ref[h] = (head(kf) + ones_a - _aug(lane, AUG_B, parts)).astype(BF16)
            va_ref[h] = (head(vf) + ones_a).astype(BF16)

    shp = jax.ShapeDtypeStruct((N_HEADS, s, LANES), BF16)
    hspec = pl.BlockSpec((N_HEADS, tm, LANES), lambda i: (0, i, 0))
    wspec = _const_spec((D_MODEL, ATT_WIDTH))
    return pl.pallas_call(
        body, name="proj_qkv_heads", out_shape=(shp, shp, shp), grid=(s // tm,),
        in_specs=[pl.BlockSpec((tm, D_MODEL), lambda i: (i, 0)), wspec, wspec, wspec,
                  pl.BlockSpec((tm, LANES), lambda i: (i, 0))],
        out_specs=(hspec, hspec, hspec), compiler_params=_params(("parallel",)),
    )(u, w_q, w_k, w_v, cum)


def attention_fwd(qa, ka, va):
    s = qa.shape[1]
    t = _blk(s, ATT_BLOCK)
    nq = s // t

    def body(qa_ref, ka_ref, va_ref, o_ref, qb_ref, m_scr, acc_scr, alpha_scr, p_scr, s_scr):
        qi = pl.program_id(1)
        m_scr[...] = jnp.full_like(m_scr, NEG_BIG)
        acc_scr[...] = jnp.zeros_like(acc_scr)

        def kv_rows(kb):
            return pl.ds(pl.multiple_of(kb * t, t), t)

        def softmax_block(kb, masked):
            for e in range(2):
                sc = _mm_nt(qa_ref[e], ka_ref[e, kv_rows(kb), :])
                if masked:
                    sc = jnp.where(_iota((t, t), 0) >= _iota((t, t), 1), sc, NEG_BIG)
                s_scr[e] = sc
                cmax = s_scr[e, :, 0:LANES]
                for c in range(1, t // LANES):
                    cmax = jnp.maximum(cmax, s_scr[e, :, LANES * c:LANES * (c + 1)])
                m_old = m_scr[e]
                m_new = jnp.maximum(m_old, jnp.max(cmax, axis=1, keepdims=True))
                alpha_scr[e] = jnp.exp(m_old - m_new)
                m_scr[e] = m_new
                for c in range(t // LANES):
                    cols = slice(LANES * c, LANES * (c + 1))
                    p_scr[e, :, cols] = jnp.exp(s_scr[e, :, cols] - m_new).astype(BF16)

        def accumulate(kb):
            for e in range(2):
                acc_scr[e] = alpha_scr[e] * acc_scr[e] + _mm(p_scr[e], va_ref[e, kv_rows(kb), :])

        def loop_body(kb, carry):
            accumulate(kb - 1)
            softmax_block(kb, False)
            return carry

        @pl.when(qi > 0)
        def _():
            softmax_block(0, False)

        lax.fori_loop(1, qi, loop_body, 0)

        @pl.when(qi > 0)
        def _():
            accumulate(qi - 1)
            softmax_block(qi, True)

        @pl.when(qi == 0)
        def _():
            softmax_block(0, True)

        accumulate(qi)

        lane = _iota((t, LANES), 1)
        outs = []
        for e in range(2):
            acc = acc_scr[e]
            l = acc[:, AUG_A:AUG_A + 1]
            outs.append(acc / l)
            lse = m_scr[e][:, 0:1] + jnp.log(l)
            q32 = qa_ref[e].astype(F32)
            c = q32[:, AUG_A:AUG_A + 1] + q32[:, AUG_A + 1:AUG_A + 2] + q32[:, AUG_A + 2:AUG_A + 3]
            qb = jnp.where(lane < HEAD_DIM, q32, 0.0) + _aug(lane, AUG_A, _split3(c - lse)) + _aug(lane, AUG_B)
            qb_ref[e] = qb.astype(BF16)
        o_ref[...] = _pack_pair(outs[0], outs[1], lane)

    return pl.pallas_call(
        body, name="attention_fwd",
        out_shape=(jax.ShapeDtypeStruct((s, ATT_WIDTH), F32), jax.ShapeDtypeStruct((N_HEADS, s, LANES), BF16)),
        grid=(N_PAIRS, nq),
        in_specs=[pl.BlockSpec((2, t, LANES), lambda j, qi: (j, qi, 0)),
                  pl.BlockSpec((2, s, LANES), lambda j, qi: (j, 0, 0)),
                  pl.BlockSpec((2, s, LANES), lambda j, qi: (j, 0, 0))],
        out_specs=(pl.BlockSpec((t, LANES), lambda j, qi: (qi, j)),
                   pl.BlockSpec((2, t, LANES), lambda j, qi: (j, qi, 0))),
        scratch_shapes=[pltpu.VMEM((2, t, LANES), F32), pltpu.VMEM((2, t, LANES), F32),
                        pltpu.VMEM((2, t, LANES), F32), pltpu.VMEM((2, t, t), BF16), pltpu.VMEM((2, t, t), F32)],
        compiler_params=_params(("parallel", "parallel")),
    )(qa, ka, va)


def attention_bwd(qb, ka, va, dob):
    s = qb.shape[1]
    t = _blk(s, ATT_BLOCK)
    nq = s // t

    def body(qb_ref, dob_ref, ka_ref, va_ref, dq_ref, dk_ref, dv_ref, dc_ref, dq_scr, dk_scr, dv_scr):
        j, ki = pl.program_id(0), pl.program_id(1)

        @pl.when((j == 0) & (ki == 0))
        def _():
            dc_ref[...] = jnp.zeros_like(dc_ref)

        @pl.when(ki == 0)
        def _():
            dq_scr[...] = jnp.zeros_like(dq_scr)

        dk_scr[...] = jnp.zeros_like(dk_scr)
        dv_scr[...] = jnp.zeros_like(dv_scr)

        def q_step(qblk, masked):
            rows = pl.ds(pl.multiple_of(qblk * t, t), t)
            for e in range(2):
                q = qb_ref[e, rows, :]
                do = dob_ref[e, rows, :]
                sc = _mm_nt(q, ka_ref[e])
                if masked:
                    sc = jnp.where(_iota((t, t), 0) >= _iota((t, t), 1), sc, NEG_BIG)
                p = jnp.exp(sc)
                ds_b = (p * _mm_nt(do, va_ref[e])).astype(BF16)
                dv_scr[e] += _mm_tn(p.astype(BF16), do)
                dk_scr[e] += _mm_tn(ds_b, q)
                dq_scr[e, rows, :] += _mm(ds_b, ka_ref[e])

        def loop_body(qblk, carry):
            q_step(qblk, False)
            return carry

        q_step(ki, True)
        lax.fori_loop(ki + 1, nq, loop_body, 0)

        lane = _iota((t, LANES), 1)
        dk_ref[...] = _pack_pair(dk_scr[0], dk_scr[1], lane).astype(BF16)
        dv_ref[...] = _pack_pair(dv_scr[0], dv_scr[1], lane).astype(BF16)
        rows = pl.ds(pl.multiple_of(ki * t, t), t)
        dc_ref[rows, :] -= (jnp.where(lane == N_HEADS + 2 * j, dk_scr[0][:, AUG_B:AUG_B + 1], 0.0)
                            + jnp.where(lane == N_HEADS + 2 * j + 1, dk_scr[1][:, AUG_B:AUG_B + 1], 0.0))

        @pl.when(ki == nq - 1)
        def _():
            for blk in range(nq):
                rws = pl.ds(blk * t, t)
                d0 = dq_scr[0, rws, :]
                d1 = dq_scr[1, rws, :]
                dq_ref[rws, :] = (_pack_pair(d0, d1, lane) * ATT_SCALE).astype(BF16)
                dc_ref[rws, :] += (jnp.where(lane == N_HEADS + 2 * j, d0[:, AUG_A:AUG_A + 1], 0.0)
                                   + jnp.where(lane == N_HEADS + 2 * j + 1, d1[:, AUG_A:AUG_A + 1], 0.0))

    full = pl.BlockSpec((2, s, LANES), lambda j, ki: (j, 0, 0))
    blk = pl.BlockSpec((2, t, LANES), lambda j, ki: (j, ki, 0))
    pair = pl.BlockSpec((t, LANES), lambda j, ki: (ki, j))
    wide = jax.ShapeDtypeStruct((s, ATT_WIDTH), BF16)
    return pl.pallas_call(
        body, name="attention_bwd",
        out_shape=(wide, wide, wide, jax.ShapeDtypeStruct((s, LANES), F32)),
        grid=(N_PAIRS, nq),
        in_specs=[full, full, blk, blk],
        out_specs=(pl.BlockSpec((s, LANES), lambda j, ki: (0, j)), pair, pair, _const_spec((s, LANES))),
        scratch_shapes=[pltpu.VMEM((2, s, LANES), F32), pltpu.VMEM((2, t, LANES), F32),
                        pltpu.VMEM((2, t, LANES), F32)],
        compiler_params=_params(("arbitrary", "arbitrary")),
    )(qb, dob, ka, va)


def _dsilu(z, sg):
    return sg * (1.0 + z * (1.0 - sg))


def post_mix(x, y, zs, o, za, p, tgt, ssd_g, att_g_lane, ple_g, fin_g, w_out, w_gate, w_proj):
    s = x.shape[0]
    tm = _blk(s, 128)
    half = SSD_WIDTH // N_GROUPS

    def rms_bwd(dy, yn, r):
        return r * (dy - yn * jnp.mean(dy * yn, axis=-1, keepdims=True))

    def colsum(a):
        return jnp.sum(a, axis=0, keepdims=True)

    def body(x_ref, y_ref, zs_ref, o_ref, za_ref, p_ref, t_ref, sg_ref, ag_ref, pg_ref, fg_ref,
             wo_ref, wg_ref, wp_ref,
             dh1_ref, dy_ref, dzs_ref, dob_ref, dza_ref, ycat_ref, dh1b_ref, n2b_ref, dglb_ref, dppb_ref, pb_ref,
             loss_ref, dfin_ref, dple_ref, dssd_ref, datt_ref):
        @pl.when(pl.program_id(0) == 0)
        def _():
            for r in (loss_ref, dfin_ref, dple_ref, dssd_ref, datt_ref):
                r[...] = jnp.zeros_like(r)

        lane = _iota((tm, LANES), 1)
        lo = lane < HEAD_DIM
        zs = zs_ref[...]
        sz = _sigmoid(zs)
        yv = y_ref[...]
        ys = yv * (zs * sz)
        yn, rg = [], []
        for g in range(N_GROUPS):
            seg = ys[:, half * g:half * (g + 1)]
            r = lax.rsqrt(jnp.mean(seg * seg, axis=-1, keepdims=True) + EPS)
            yn.append(seg * r)
            rg.append(r)
            ycat_ref[:, half * g:half * (g + 1)] = (yn[g] * sg_ref[:, half * g:half * (g + 1)]).astype(BF16)
        za = za_ref[...]
        sza = _sigmoid(za)
        silu_za = za * sza
        on, ra = [], []
        for jb in range(N_PAIRS):
            blk = o_ref[:, LANES * jb:LANES * (jb + 1)]
            sq = blk * blk
            ms0 = jnp.sum(jnp.where(lo, sq, 0.0), axis=1, keepdims=True) * (1.0 / HEAD_DIM)
            ms1 = jnp.sum(jnp.where(lo, 0.0, sq), axis=1, keepdims=True) * (1.0 / HEAD_DIM)
            r = jnp.where(lo, lax.rsqrt(ms0 + EPS), lax.rsqrt(ms1 + EPS))
            on.append(blk * r)
            ra.append(r)
            an = on[jb] * ag_ref[:, LANES * jb:LANES * (jb + 1)]
            ycat_ref[:, SSD_WIDTH + LANES * jb:SSD_WIDTH + LANES * (jb + 1)] = (
                an * silu_za[:, LANES * jb:LANES * (jb + 1)]).astype(BF16)
        h1 = x_ref[...] + _mm(ycat_ref[...], wo_ref[...])
        r2 = lax.rsqrt(jnp.mean(h1 * h1, axis=-1, keepdims=True) + EPS)
        n2h = h1 * r2
        n2_b = (n2h * pg_ref[...]).astype(BF16)
        gate = _sigmoid(_mm(n2_b, wg_ref[...]))
        p_b = p_ref[...].astype(BF16)
        pp = _mm(p_b, wp_ref[...])
        h2 = h1 + gate * pp
        r3 = lax.rsqrt(jnp.mean(h2 * h2, axis=-1, keepdims=True) + EPS)
        n3 = h2 * r3
        diff = n3 * fg_ref[...] - t_ref[...]
        sq = colsum(diff * diff)
        part = sq[:, 0:LANES]
        for jb in range(1, D_MODEL // LANES):
            part = part + sq[:, LANES * jb:LANES * (jb + 1)]
        loss_ref[...] += part * (0.5 / D_MODEL)
        dout = diff * (1.0 / D_MODEL)
        dfin_ref[...] += colsum(dout * n3)
        dh2 = rms_bwd(dout * fg_ref[...], n3, r3)
        dgl = dh2 * pp * gate * (1.0 - gate)
        dgl_b = dgl.astype(BF16)
        dn2 = _mm_nt(dgl_b, wg_ref[...])
        dple_ref[...] += colsum(dn2 * n2h)
        dh1 = dh2 + rms_bwd(dn2 * pg_ref[...], n2h, r2)
        dh1_b = dh1.astype(BF16)
        dycat = _mm_nt(dh1_b, wo_ref[...])
        dh1_ref[...] = dh1
        dh1b_ref[...] = dh1_b
        n2b_ref[...] = n2_b
        dglb_ref[...] = dgl_b
        dppb_ref[...] = (dh2 * gate).astype(BF16)
        pb_ref[...] = p_b
        for g in range(N_GROUPS):
            cols = slice(half * g, half * (g + 1))
            dys_g = dycat[:, cols]
            dssd_ref[:, cols] += colsum(dys_g * yn[g])
            dys = rms_bwd(dys_g * sg_ref[:, cols], yn[g], rg[g])
            dy_ref[:, cols] = dys * (zs[:, cols] * sz[:, cols])
            dzs_ref[:, cols] = (dys * yv[:, cols] * _dsilu(zs[:, cols], sz[:, cols])).astype(BF16)
        for jb in range(N_PAIRS):
            cols = slice(LANES * jb, LANES * (jb + 1))
            dya = dycat[:, SSD_WIDTH + LANES * jb:SSD_WIDTH + LANES * (jb + 1)]
            ag = ag_ref[:, cols]
            dan = dya * silu_za[:, cols]
            dza_ref[:, cols] = (dya * (on[jb] * ag) * _dsilu(za[:, cols], sza[:, cols])).astype(BF16)
            datt_ref[:, cols] += colsum(dan * on[jb])
            don = dan * ag
            q = don * on[jb]
            m0 = jnp.sum(jnp.where(lo, q, 0.0), axis=1, keepdims=True) * (1.0 / HEAD_DIM)
            m1 = jnp.sum(jnp.where(lo, 0.0, q), axis=1, keepdims=True) * (1.0 / HEAD_DIM)
            do2 = ra[jb] * (don - on[jb] * jnp.where(lo, m0, m1))
            prod = do2 * o_ref[:, cols]
            for e in range(2):
                delta = jnp.sum(jnp.where(lo, prod, 0.0) if e == 0 else jnp.where(lo, 0.0, prod),
                                axis=1, keepdims=True)
                base = jnp.where(lo, do2 if e == 0 else pltpu.roll(do2, HEAD_DIM, 1), 0.0)
                dob_ref[2 * jb + e] = (base - _aug(lane, AUG_A, _split3(delta))).astype(BF16)

    def rows(n, dtype=None):
        return pl.BlockSpec((tm, n), lambda i: (i, 0))

    def out(n, dtype):
        return jax.ShapeDtypeStruct((s, n), dtype)

    vec = _const_spec((1, D_MODEL))
    vshape = jax.ShapeDtypeStruct((1, D_MODEL), F32)
    return pl.pallas_call(
        body, name="post_mix",
        out_shape=(out(D_MODEL, F32), out(SSD_WIDTH, F32), out(SSD_WIDTH, BF16),
                   jax.ShapeDtypeStruct((N_HEADS, s, LANES), BF16),
                   out(ATT_WIDTH, BF16), out(D_INNER, BF16), out(D_MODEL, BF16), out(D_MODEL, BF16),
                   out(D_MODEL, BF16), out(D_MODEL, BF16), out(PLE_DIM, BF16),
                   jax.ShapeDtypeStruct((1, LANES), F32), vshape, vshape, vshape, vshape),
        grid=(s // tm,),
        in_specs=[rows(D_MODEL), rows(SSD_WIDTH), rows(SSD_WIDTH), rows(ATT_WIDTH), rows(ATT_WIDTH),
                  rows(PLE_DIM), rows(D_MODEL), vec, vec, vec, vec,
                  _const_spec((D_INNER, D_MODEL)), _const_spec((D_MODEL, D_MODEL)), _const_spec((PLE_DIM, D_MODEL))],
        out_specs=(rows(D_MODEL), rows(SSD_WIDTH), rows(SSD_WIDTH),
                   pl.BlockSpec((N_HEADS, tm, LANES), lambda i: (0, i, 0)), rows(ATT_WIDTH),
                   rows(D_INNER), rows(D_MODEL), rows(D_MODEL), rows(D_MODEL), rows(D_MODEL), rows(PLE_DIM),
                   _const_spec((1, LANES)), vec, vec, vec, vec),
        compiler_params=_params(("arbitrary",)),
    )(x, y, zs, o, za, p, tgt, ssd_g, att_g_lane, ple_g, fin_g, w_out, w_gate, w_proj)


def in_proj_bwd(dsegs, wsegs, x, g, dh1):
    s = x.shape[0]
    tm = _blk(s, 256)
    nseg = len(dsegs)

    def body(*refs):
        d_refs = refs[:nseg]
        w_refs = refs[nseg:2 * nseg]
        x_ref, g_ref, dh1_ref, dx_ref, dg_ref = refs[2 * nseg:]

        @pl.when(pl.program_id(0) == 0)
        def _():
            dg_ref[...] = jnp.zeros_like(dg_ref)

        du = _mm_nt(d_refs[0][...], w_refs[0][...])
        for k in range(1, nseg):
            du = du + _mm_nt(d_refs[k][...], w_refs[k][...])
        xv = x_ref[...]
        r = lax.rsqrt(jnp.mean(xv * xv, axis=-1, keepdims=True) + EPS)
        xh = xv * r
        dg_ref[...] += jnp.sum(du * xh, axis=0, keepdims=True)
        dxh = du * g_ref[...]
        dx_ref[...] = r * (dxh - xh * jnp.mean(dxh * xh, axis=-1, keepdims=True)) + dh1_ref[...]

    rows = lambda n: pl.BlockSpec((tm, n), lambda i: (i, 0))
    return pl.pallas_call(
        body, name="in_proj_bwd",
        out_shape=(jax.ShapeDtypeStruct((s, D_MODEL), F32), jax.ShapeDtypeStruct((1, D_MODEL), F32)),
        grid=(s // tm,),
        in_specs=([rows(d.shape[1]) for d in dsegs] + [_const_spec(w.shape) for w in wsegs]
                  + [rows(D_MODEL), _const_spec((1, D_MODEL)), rows(D_MODEL)]),
        out_specs=(rows(D_MODEL), _const_spec((1, D_MODEL))),
        compiler_params=_params(("arbitrary",)),
    )(*dsegs, *wsegs, x, g, dh1)


SMALL_NAMES = ("norm_g", "conv_b", "dt_bias", "a_log", "d_skip", "ssd_norm_g", "fg_bias", "att_norm_g",
               "ple_norm_g", "final_norm_g")
SMALL_SIZES = (1024, 1536, 16, 16, 16, 1024, 16, 64, 1024, 1024)
CONV_W_SIZE = CONV_WIDTH * CONV_CH


def _pack_small(vals):
    flat = jnp.concatenate([v.reshape(-1).astype(F32) for v in vals])
    flat = jnp.pad(flat, (0, SMALL_ROWS * LANES - flat.shape[0]))
    return flat.reshape(SMALL_ROWS, LANES)


def _unpack_small(pack, shapes):
    flat = pack.reshape(-1)
    out, off = [], 0
    for n, shp in zip(SMALL_SIZES, shapes):
        out.append(flat[off:off + n].reshape(shp))
        off += n
    return out


def _row128(v16, offset=0):
    return jnp.pad(v16.reshape(1, N_HEADS).astype(F32), ((0, 0), (offset, LANES - N_HEADS - offset)))


def local_step(x, p, tgt, w_in, w_out, w_gate, w_proj, conv_w, norm_g, conv_b, dt_bias, a_log, d_skip,
               ssd_norm_g, fg_bias, att_norm_g, ple_norm_g, final_norm_g):
    c0, c1, c2, c3, c4, c5, c6, c7 = 0, 1024, 2560, 2576, 3600, 4624, 5648, 6672
    w_zs, w_xbc, w_dt = w_in[:, c0:c1], w_in[:, c1:c2], w_in[:, c2:c3]
    w_za, w_q, w_k, w_v, w_f = w_in[:, c3:c4], w_in[:, c4:c5], w_in[:, c5:c6], w_in[:, c6:c7], w_in[:, c7:]
    w_small = jnp.concatenate([w_dt, w_f, jnp.zeros((D_MODEL, LANES - 2 * N_HEADS), BF16)], axis=1)

    dtb_row = _row128(dt_bias)
    a_row = _row128(-jnp.exp(a_log.astype(F32)))
    fgb_row = _row128(fg_bias, N_HEADS)
    dskip_lane = jnp.repeat(d_skip.astype(F32), HEAD_DIM).reshape(1, SSD_WIDTH)
    att_g_lane = jnp.tile(att_norm_g.astype(F32), N_HEADS).reshape(1, ATT_WIDTH)
    row = lambda v: v.reshape(1, -1).astype(F32)

    u = rms_prenorm(x, row(norm_g))
    zs = matmul_rows(u, w_zs, F32, "proj_z_ssd")
    xbc = matmul_rows(u, w_xbc, F32, "proj_xbc")
    za = matmul_rows(u, w_za, F32, "proj_z_att")
    small = matmul_rows(u, w_small, F32, "proj_small")
    cum = forget_cumsum(small, fgb_row)
    qa, ka, va = proj_qkv_heads(u, w_q, w_k, w_v, cum)
    pre, xc = conv_fwd(xbc, conv_w, row(conv_b))
    y, states = ssd_fwd(xc, small, dtb_row, a_row, dskip_lane)
    o, qb = attention_fwd(qa, ka, va)
    (dh1, dy, dzs, dob, dza, ycat, dh1_b, n2_b, dgl_b, dpp_b, p_b,
     loss_l, dfin, dple, dssd_g, datt_lane) = post_mix(
        x, y, zs, o, za, p, tgt, row(ssd_norm_g), att_g_lane, row(ple_norm_g), row(final_norm_g),
        w_out, w_gate, w_proj)
    dq, dk, dv, dc = attention_bwd(qb, ka, va, dob)
    dxc, ddt_raw, da, ddtb, ddsk_lane = ssd_bwd(xc, small, states, dy, dtb_row, a_row, dskip_lane)
    dsmall, dfgb = forget_bwd(dc, small, ddt_raw, fgb_row)
    dxbc, dconv_w8, dconv_b = conv_bwd(xbc, pre, dxc, conv_w)
    dsegs = [dzs, dxbc, dza, dq, dk, dv, dsmall]
    wsegs = [w_zs, w_xbc, w_za, w_q, w_k, w_v, w_small]
    dx, dnorm_g = in_proj_bwd(dsegs, wsegs, x, row(norm_g), dh1)
    dws = [matmul_tn(u, d, "dw_in_%d" % i) for i, d in enumerate(dsegs)]
    dw_in = jnp.concatenate([dws[0], dws[1], dws[6][:, :N_HEADS], dws[2], dws[3], dws[4], dws[5],
                             dws[6][:, N_HEADS:2 * N_HEADS]], axis=1)
    dw_out = matmul_tn(ycat, dh1_b, "dw_out")
    dw_gate = matmul_tn(n2_b, dgl_b, "dw_gate")
    dw_proj = matmul_tn(p_b, dpp_b, "dw_proj")
    small_grads = [
        dnorm_g, dconv_b, ddtb[0, :N_HEADS], (da * a_row)[0, :N_HEADS],
        ddsk_lane.reshape(N_HEADS, HEAD_DIM).sum(axis=1), dssd_g, dfgb[0, N_HEADS:2 * N_HEADS],
        datt_lane.reshape(N_HEADS, HEAD_DIM).sum(axis=0), dple, dfin]
    loss = jnp.sum(loss_l)
    return loss, dx, dw_in, dw_out, dw_gate, dw_proj, dconv_w8[:CONV_WIDTH], small_grads


def kernel(x, p, norm_g, w_in, conv_w, conv_b, dt_bias, a_log, d_skip, ssd_norm_g, fg_bias, att_norm_g, w_out, ple_norm_g, w_ple_gate, w_ple_proj, final_norm_g, loss_target, m_norm_g, m_w_in, m_conv_w, m_conv_b, m_dt_bias, m_a_log, m_d_skip, m_ssd_norm_g, m_fg_bias, m_att_norm_g, m_w_out, m_ple_norm_g, m_w_ple_gate, m_w_ple_proj, m_final_norm_g, v_norm_g, v_w_in, v_conv_w, v_conv_b, v_dt_bias, v_a_log, v_d_skip, v_ssd_norm_g, v_fg_bias, v_att_norm_g, v_w_out, v_ple_norm_g, v_w_ple_gate, v_w_ple_proj, v_final_norm_g):
    chip = 2 * lax.axis_index("x") + lax.axis_index("y")
    core = lax.axis_index("c")

    big_w = [w_in[0], w_out[0], w_ple_gate[0], w_ple_proj[0]]
    own = [a.astype(BF16) for a in big_w] + [conv_w[0]]
    gathered = gather_weights(own[:4], own[4])

    def joined(k, axis):
        return jnp.concatenate([jnp.where(chip == j, own[k], gathered[k][j]) for j in range(N_CHIPS)], axis=axis)

    w_in_f, w_out_f, w_gate_f, w_proj_f, conv_w_f = joined(0, 1), joined(1, 0), joined(2, 0), joined(3, 1), joined(4, 1)

    smalls_w = [norm_g, conv_b, dt_bias, a_log, d_skip, ssd_norm_g, fg_bias, att_norm_g, ple_norm_g, final_norm_g]
    loss_l, dx, dw_in, dw_out, dw_gate, dw_proj, dconv_w, small_grads = local_step(
        x[0], p[0, 0], loss_target[0], w_in_f, w_out_f, w_gate_f, w_proj_f, conv_w_f,
        *[a.reshape(-1) for a in smalls_w])
    loss = lax.psum(loss_l, ("x", "y", "c"))

    gs = [jnp.stack([dw_in[:, 1672 * j:1672 * (j + 1)] for j in range(N_CHIPS)]),
          dw_out.reshape(N_CHIPS, 512, D_MODEL), dw_gate.reshape(N_CHIPS, 256, D_MODEL),
          jnp.stack([dw_proj[:, 256 * j:256 * (j + 1)] for j in range(N_CHIPS)])]
    core1 = core.reshape(1).astype(jnp.int32)
    pres = add_halves(core1, gs, halves_to_sibling(gs))
    *parts, smalls = scatter_halves(pres, _pack_small(list(small_grads) + [dconv_w]))
    mine = sum_parts(parts)

    g_big, d_big, m_big, v_big = adamw_big(
        core1, mine, swap_halves(mine), big_w, [m_w_in[0], m_w_out[0], m_w_ple_gate[0], m_w_ple_proj[0]],
        [v_w_in[0], v_w_out[0], v_w_ple_gate[0], v_w_ple_proj[0]])
    smalls_m = [m_norm_g, m_conv_b, m_dt_bias, m_a_log, m_d_skip, m_ssd_norm_g, m_fg_bias, m_att_norm_g,
                m_ple_norm_g, m_final_norm_g]
    smalls_v = [v_norm_g, v_conv_b, v_dt_bias, v_a_log, v_d_skip, v_ssd_norm_g, v_fg_bias, v_att_norm_g,
                v_ple_norm_g, v_final_norm_g]
    g_sm, d_sm, m_sm, v_sm = adamw_small(smalls, _pack_small(smalls_w), _pack_small(smalls_m), _pack_small(smalls_v))
    n_small = sum(SMALL_SIZES)
    g_conv_full = g_sm.reshape(-1)[n_small:n_small + CONV_W_SIZE].reshape(CONV_WIDTH, CONV_CH)
    g_conv = lax.dynamic_slice_in_dim(g_conv_full, chip * 384, 384, axis=1)
    d_conv, m_conv, v_conv = adamw_whole(g_conv, conv_w[0], m_conv_w[0], v_conv_w[0], "adamw_conv")

    shapes = [a.shape for a in smalls_w]
    outs = []
    for big, conv, sm in ((g_big, g_conv, g_sm), (d_big, d_conv, d_sm), (m_big, m_conv, m_sm), (v_big, v_conv, v_sm)):
        b_in, b_out, b_gate, b_proj = [a[None] for a in big]
        s_norm, s_convb, s_dtb, s_alog, s_dsk, s_ssdg, s_fgb, s_attg, s_pleg, s_fin = _unpack_small(sm, shapes)
        outs.extend([s_norm, b_in, conv[None], s_convb, s_dtb, s_alog, s_dsk, s_ssdg, s_fgb, s_attg, b_out, s_pleg,
                     b_gate, b_proj, s_fin])
    return (loss, dx[None], *outs)
```

```python
import functools

import jax
import jax.numpy as jnp
from jax import lax
from jax.experimental import pallas as pl
from jax.experimental.pallas import tpu as pltpu

F32 = jnp.float32
BF16 = jnp.bfloat16

D_MODEL = 1024
SSD_WIDTH = 1024
ATT_WIDTH = 1024
N_HEADS = 16
HEAD_DIM = 64
N_GROUPS = 2
D_STATE = 128
CONV_CH = 1536
CONV_WIDTH = 4
CHUNK = 128
PLE_DIM = 256
D_INNER = 2048
EPS = 1e-6
IN_COLS = 6688
N_CHIPS = 4
N_DEV = 8
LANES = 128
N_PAIRS = 8

ADAM_LR = 0.001
ADAM_B1 = 0.9
ADAM_B2 = 0.999
ADAM_EPS = 1e-08
ADAM_WD = 0.01
ADAM_STEP = 10

SMALL_ROWS = 96

NEG_BIG = -1e30
VMEM_LIMIT = 56 * 1024 * 1024

MESH = pl.DeviceIdType.MESH
ANY = pl.BlockSpec(memory_space=pl.ANY)


def _mm(a, b):
    return jnp.dot(a, b, preferred_element_type=F32)


def _mm_nt(a, b):
    return lax.dot_general(a, b, (((1,), (1,)), ((), ())), preferred_element_type=F32)


def _mm_tn(a, b):
    return lax.dot_general(a, b, (((0,), (0,)), ((), ())), preferred_element_type=F32)


def _mm_exact(a, b):
    return jnp.dot(a, b, preferred_element_type=F32, precision=lax.Precision.HIGHEST)


def _softplus(x):
    return jnp.maximum(x, 0.0) + jnp.log1p(jnp.exp(-jnp.abs(x)))


def _sigmoid(x):
    return jax.nn.sigmoid(x)


def _iota(shape, dim):
    return lax.broadcasted_iota(jnp.int32, shape, dim)


def _params(sem=None):
    return pltpu.CompilerParams(dimension_semantics=sem, vmem_limit_bytes=VMEM_LIMIT)


def _blk(n, pref):
    return min(n, pref)


def _const_spec(shape):
    nd = len(shape)
    return pl.BlockSpec(shape, lambda *_: (0,) * nd)


def _chip_peers():
    x, y, c = lax.axis_index("x"), lax.axis_index("y"), lax.axis_index("c")
    return x, y, c, [(1 - x, y, c), (x, 1 - y, c), (1 - x, 1 - y, c)]


def _half(rows, c):
    h = rows // 2
    return pl.ds(pl.multiple_of(c * h, 8), h)


def _sems(n):
    return [pltpu.SemaphoreType.DMA((n,)), pltpu.SemaphoreType.DMA((n,))]


def gather_weights(shards, conv_s):
    n = len(shards)

    def body(*refs):
        ins, conv_in = refs[:n], refs[n]
        outs, conv_out = refs[n + 1:2 * n + 1], refs[2 * n + 1]
        ssem1, rsem1, ssem2, rsem2, c_ssem, c_rsem = refs[2 * n + 2:]
        x, y, c, peers = _chip_peers()
        me = 2 * x + y
        sibling = (x, y, 1 - c)
        first, small = [], []
        for k, peer in enumerate(peers):
            for i in range(n):
                h = _half(ins[i].shape[0], c)
                first.append(pltpu.make_async_remote_copy(
                    src_ref=ins[i].at[h], dst_ref=outs[i].at[me, h], send_sem=ssem1.at[n * k + i],
                    recv_sem=rsem1.at[n * k + i], device_id=peer, device_id_type=MESH))
            small.append(pltpu.make_async_remote_copy(
                src_ref=conv_in, dst_ref=conv_out.at[me], send_sem=c_ssem.at[k], recv_sem=c_rsem.at[k],
                device_id=peer, device_id_type=MESH))
        for cp in first + small:
            cp.start()
        passed = []
        for k, peer in enumerate(peers):
            chip = 2 * peer[0] + peer[1]
            for i in range(n):
                h = _half(ins[i].shape[0], c)
                first[n * k + i].wait_recv()
                fwd = pltpu.make_async_remote_copy(
                    src_ref=outs[i].at[chip, h], dst_ref=outs[i].at[chip, h], send_sem=ssem2.at[n * k + i],
                    recv_sem=rsem2.at[n * k + i], device_id=sibling, device_id_type=MESH)
                fwd.start()
                passed.append(fwd)
        for cp in passed:
            cp.wait_recv()
        for cp in first + passed:
            cp.wait_send()
        for cp in small:
            cp.wait()

    return pl.pallas_call(
        body, name="gather_weights",
        out_shape=tuple(jax.ShapeDtypeStruct((N_CHIPS,) + a.shape, a.dtype) for a in list(shards) + [conv_s]),
        in_specs=[ANY] * (n + 1), out_specs=(ANY,) * (n + 1),
        scratch_shapes=_sems(3 * n) + _sems(3 * n) + _sems(3),
    )(*shards, conv_s)


def halves_to_sibling(gs):
    n = len(gs)

    def body(*refs):
        ins, outs = refs[:n], refs[n:2 * n]
        ssem, rsem = refs[2 * n:]
        x, y, c = lax.axis_index("x"), lax.axis_index("y"), lax.axis_index("c")
        copies = []
        for i in range(n):
            for j in range(N_CHIPS):
                copies.append(pltpu.make_async_remote_copy(
                    src_ref=ins[i].at[j, _half(ins[i].shape[1], 1 - c)], dst_ref=outs[i].at[j],
                    send_sem=ssem.at[N_CHIPS * i + j], recv_sem=rsem.at[N_CHIPS * i + j],
                    device_id=(x, y, 1 - c), device_id_type=MESH))
        for cp in copies:
            cp.start()
        for cp in copies:
            cp.wait()

    return pl.pallas_call(
        body, name="halves_to_sibling",
        out_shape=tuple(jax.ShapeDtypeStruct((N_CHIPS, g.shape[1] // 2, g.shape[2]), F32) for g in gs),
        in_specs=[ANY] * n, out_specs=(ANY,) * n, scratch_shapes=_sems(N_CHIPS * n),
    )(*gs)


RED_GRID = 8


def add_halves(core, gs, rbs):
    n = len(gs)

    def body(c_ref, *refs):
        for i in range(n):
            refs[2 * n + i][...] = (refs[i][...] + refs[n + i][...]).astype(BF16)

    def blk(g):
        return (1, g.shape[1] // 2 // RED_GRID, g.shape[2])

    grid_spec = pltpu.PrefetchScalarGridSpec(
        num_scalar_prefetch=1, grid=(N_CHIPS, RED_GRID),
        in_specs=([pl.BlockSpec(blk(g), lambda j, b, c_ref: (j, c_ref[0] * RED_GRID + b, 0)) for g in gs]
                  + [pl.BlockSpec(blk(g), lambda j, b, c_ref: (j, b, 0)) for g in gs]),
        out_specs=[pl.BlockSpec(blk(g), lambda j, b, c_ref: (j, b, 0)) for g in gs])
    return pl.pallas_call(
        body, name="add_halves", grid_spec=grid_spec,
        out_shape=tuple(jax.ShapeDtypeStruct(r.shape, BF16) for r in rbs),
        compiler_params=_params(("parallel", "parallel")),
    )(core, *gs, *rbs)


def scatter_halves(pres, small):
    n = len(pres)

    def body(*refs):
        ins, s_ref = refs[:n], refs[n]
        outs, smalls_ref = refs[n + 1:2 * n + 1], refs[2 * n + 1]
        ssem, rsem, s_ssem, s_rsem, lsem = refs[2 * n + 2:]
        x, y, c, peers = _chip_peers()
        me = 2 * x + y
        dev = 4 * x + 2 * y + c
        local = [pltpu.make_async_copy(ins[i].at[me], outs[i].at[me], lsem.at[i]) for i in range(n)]
        local.append(pltpu.make_async_copy(s_ref, smalls_ref.at[dev], lsem.at[n]))
        for cp in local:
            cp.start()
        remote = []
        for k, peer in enumerate(peers):
            dst_chip = 2 * peer[0] + peer[1]
            for i in range(n):
                remote.append(pltpu.make_async_remote_copy(
                    src_ref=ins[i].at[dst_chip], dst_ref=outs[i].at[me], send_sem=ssem.at[n * k + i],
                    recv_sem=rsem.at[n * k + i], device_id=peer, device_id_type=MESH))
        for k in range(1, N_DEV):
            fx, fy, fc = (k >> 2) & 1, (k >> 1) & 1, k & 1
            peer = ((1 - x) if fx else x, (1 - y) if fy else y, (1 - c) if fc else c)
            remote.append(pltpu.make_async_remote_copy(
                src_ref=s_ref, dst_ref=smalls_ref.at[dev], send_sem=s_ssem.at[k - 1], recv_sem=s_rsem.at[k - 1],
                device_id=peer, device_id_type=MESH))
        for cp in remote:
            cp.start()
        for cp in remote:
            cp.wait()
        for cp in local:
            cp.wait()

    return pl.pallas_call(
        body, name="scatter_halves",
        out_shape=tuple([jax.ShapeDtypeStruct(a.shape, a.dtype) for a in pres]
                        + [jax.ShapeDtypeStruct((N_DEV,) + small.shape, F32)]),
        in_specs=[ANY] * (n + 1), out_specs=(ANY,) * (n + 1),
        scratch_shapes=_sems(3 * n) + _sems(N_DEV - 1) + [pltpu.SemaphoreType.DMA((n + 1,))],
    )(*pres, small)


def sum_parts(parts):
    n = len(parts)

    def body(*refs):
        for i in range(n):
            p_ref = refs[i]
            refs[n + i][...] = ((p_ref[0].astype(F32) + p_ref[1].astype(F32)) + p_ref[2].astype(F32)
                                ) + p_ref[3].astype(F32)

    def rows(p):
        return p.shape[1] // RED_GRID

    return pl.pallas_call(
        body, name="sum_parts",
        out_shape=tuple(jax.ShapeDtypeStruct(p.shape[1:], F32) for p in parts),
        grid=(RED_GRID,),
        in_specs=[pl.BlockSpec((N_CHIPS, rows(p), p.shape[2]), lambda b: (0, b, 0)) for p in parts],
        out_specs=tuple(pl.BlockSpec((rows(p), p.shape[2]), lambda b: (b, 0)) for p in parts),
        compiler_params=_params(("parallel",)),
    )(*parts)


def swap_halves(reds):
    n = len(reds)

    def body(*refs):
        ins, outs = refs[:n], refs[n:2 * n]
        ssem, rsem = refs[2 * n:]
        x, y, c = lax.axis_index("x"), lax.axis_index("y"), lax.axis_index("c")
        copies = [pltpu.make_async_remote_copy(
            src_ref=ins[i], dst_ref=outs[i], send_sem=ssem.at[i], recv_sem=rsem.at[i],
            device_id=(x, y, 1 - c), device_id_type=MESH) for i in range(n)]
        for cp in copies:
            cp.start()
        for cp in copies:
            cp.wait()

    return pl.pallas_call(
        body, name="swap_halves",
        out_shape=tuple(jax.ShapeDtypeStruct(r.shape, F32) for r in reds),
        in_specs=[ANY] * n, out_specs=(ANY,) * n, scratch_shapes=_sems(n),
    )(*reds)


def _adamw(w, g, m, v):
    m = ADAM_B1 * m + (1.0 - ADAM_B1) * g
    v = ADAM_B2 * v + (1.0 - ADAM_B2) * (g * g)
    m_hat = m / (1.0 - ADAM_B1 ** ADAM_STEP)
    v_hat = v / (1.0 - ADAM_B2 ** ADAM_STEP)
    delta = -ADAM_LR * (m_hat / (jnp.sqrt(v_hat) + ADAM_EPS) + ADAM_WD * w)
    return delta, m, v


def adamw_big(core, mine, theirs, ws, ms, vs):
    n = len(ws)
    per_half = RED_GRID // 2

    def body(c_ref, *refs):
        own = (pl.program_id(0) // per_half) == c_ref[0]
        for i in range(n):
            g = jnp.where(own, refs[i][...], refs[n + i][...])
            d, mn, vn = _adamw(refs[2 * n + i][...], g, refs[3 * n + i][...], refs[4 * n + i][...])
            refs[5 * n + i][...] = g
            refs[6 * n + i][...] = d
            refs[7 * n + i][...] = mn
            refs[8 * n + i][...] = vn

    def blk(w):
        return (w.shape[0] // RED_GRID, w.shape[1])

    halves = [pl.BlockSpec(blk(w), lambda b, c_ref: (b % per_half, 0)) for w in ws]
    whole = [pl.BlockSpec(blk(w), lambda b, c_ref: (b, 0)) for w in ws]
    shapes = [jax.ShapeDtypeStruct(w.shape, F32) for w in ws]
    grid_spec = pltpu.PrefetchScalarGridSpec(
        num_scalar_prefetch=1, grid=(RED_GRID,), in_specs=halves * 2 + whole * 3, out_specs=whole * 4)
    outs = pl.pallas_call(
        body, name="adamw_big", out_shape=tuple(shapes * 4), grid_spec=grid_spec,
        compiler_params=_params(("parallel",)),
    )(core, *mine, *theirs, *ws, *ms, *vs)
    return outs[:n], outs[n:2 * n], outs[2 * n:3 * n], outs[3 * n:]


def adamw_whole(g, w, m, v, name):
    def body(g_ref, w_ref, m_ref, v_ref, d_out, m_out, v_out):
        d, mn, vn = _adamw(w_ref[...], g_ref[...], m_ref[...], v_ref[...])
        d_out[...] = d
        m_out[...] = mn
        v_out[...] = vn

    shp = jax.ShapeDtypeStruct(g.shape, F32)
    return pl.pallas_call(body, name=name, out_shape=(shp,) * 3)(g, w, m, v)


def adamw_small(smalls, w, m, v):
    def body(s_ref, w_ref, m_ref, v_ref, g_out, d_out, m_out, v_out):
        g = s_ref[0]
        for k in range(1, N_DEV):
            g = g + s_ref[k]
        d, mn, vn = _adamw(w_ref[...], g, m_ref[...], v_ref[...])
        g_out[...] = g
        d_out[...] = d
        m_out[...] = mn
        v_out[...] = vn

    shp = jax.ShapeDtypeStruct((SMALL_ROWS, LANES), F32)
    return pl.pallas_call(body, name="adamw_small", out_shape=(shp,) * 4)(smalls, w, m, v)


def rms_prenorm(x, g):
    s = x.shape[0]
    tm = _blk(s, 512)

    def body(x_ref, g_ref, u_ref):
        xv = x_ref[...]
        r = lax.rsqrt(jnp.mean(xv * xv, axis=-1, keepdims=True) + EPS)
        u_ref[...] = (xv * r * g_ref[...]).astype(BF16)

    return pl.pallas_call(
        body, name="rms_prenorm", out_shape=jax.ShapeDtypeStruct(x.shape, BF16), grid=(s // tm,),
        in_specs=[pl.BlockSpec((tm, D_MODEL), lambda i: (i, 0)), _const_spec((1, D_MODEL))],
        out_specs=pl.BlockSpec((tm, D_MODEL), lambda i: (i, 0)), compiler_params=_params(("parallel",)),
    )(x, g)


def matmul_rows(a, w, out_dtype, name):
    s, k = a.shape
    n = w.shape[1]
    tm = _blk(s, 512)

    def body(a_ref, w_ref, o_ref):
        o_ref[...] = _mm(a_ref[...], w_ref[...]).astype(out_dtype)

    return pl.pallas_call(
        body, name=name, out_shape=jax.ShapeDtypeStruct((s, n), out_dtype), grid=(s // tm,),
        in_specs=[pl.BlockSpec((tm, k), lambda i: (i, 0)), _const_spec((k, n))],
        out_specs=pl.BlockSpec((tm, n), lambda i: (i, 0)), compiler_params=_params(("parallel",)),
    )(a, w)


def matmul_tn(a, b, name):
    s, m = a.shape
    n = b.shape[1]
    tk = _blk(s, 2048)
    tn = _blk(n, 512)

    def body(a_ref, b_ref, o_ref):
        @pl.when(pl.program_id(1) == 0)
        def _():
            o_ref[...] = jnp.zeros_like(o_ref)

        o_ref[...] += _mm_tn(a_ref[...], b_ref[...])

    return pl.pallas_call(
        body, name=name, out_shape=jax.ShapeDtypeStruct((m, n), F32), grid=(n // tn, s // tk),
        in_specs=[pl.BlockSpec((tk, m), lambda j, i: (i, 0)), pl.BlockSpec((tk, tn), lambda j, i: (i, j))],
        out_specs=pl.BlockSpec((m, tn), lambda j, i: (0, j)),
        compiler_params=_params(("parallel", "arbitrary")),
    )(a, b)


def conv_fwd(xbc, w, b):
    s = xbc.shape[0]
    tm = _blk(s, 256)

    def body(x_ref, t_ref, w_ref, b_ref, pre_ref, act_ref):
        i = pl.program_id(0)
        cur = x_ref[...]
        tail = jnp.where(i > 0, t_ref[...], 0.0)
        wv = w_ref[...]
        acc = cur * wv[3:4, :] + b_ref[...]
        head = cur[0:8, :] * wv[3:4, :] + b_ref[...]
        row8 = _iota((8, CONV_CH), 0)
        for sh in range(1, CONV_WIDTH):
            wk = wv[3 - sh:4 - sh, :]
            acc = acc + pltpu.roll(cur, sh, 0) * wk
            first = jnp.where(row8 < sh, pltpu.roll(tail, sh, 0), pltpu.roll(cur[0:8, :], sh, 0))
            head = head + first * wk
        pre_ref[...] = acc
        act_ref[...] = acc * _sigmoid(acc)
        pre_ref[0:8, :] = head
        act_ref[0:8, :] = head * _sigmoid(head)

    shp = jax.ShapeDtypeStruct(xbc.shape, F32)
    rows = pl.BlockSpec((tm, CONV_CH), lambda i: (i, 0))
    return pl.pallas_call(
        body, name="conv_fwd", out_shape=(shp, shp), grid=(s // tm,),
        in_specs=[rows, pl.BlockSpec((8, CONV_CH), lambda i: (jnp.maximum(i * (tm // 8) - 1, 0), 0)),
                  _const_spec((CONV_WIDTH, CONV_CH)), _const_spec((1, CONV_CH))],
        out_specs=(rows, rows), compiler_params=_params(("parallel",)),
    )(xbc, xbc, w, b)


def conv_bwd(xbc, pre, dact, w):
    s = xbc.shape[0]
    tm = _blk(s, 256)
    nb = s // tm

    def dsilu(p):
        sg = _sigmoid(p)
        return sg * (1.0 + p * (1.0 - sg))

    def body(x_ref, xt_ref, p_ref, pn_ref, d_ref, dn_ref, w_ref, dx_ref, dw_ref, db_ref):
        i = pl.program_id(0)

        @pl.when(i == 0)
        def _():
            dw_ref[...] = jnp.zeros_like(dw_ref)
            db_ref[...] = jnp.zeros_like(db_ref)

        wv = w_ref[...]
        dpre = d_ref[...] * dsilu(p_ref[...])
        dnext = jnp.where(i < nb - 1, dn_ref[...] * dsilu(pn_ref[...]), 0.0)
        cur = x_ref[...]
        tail = jnp.where(i > 0, xt_ref[...], 0.0)
        row8 = _iota((8, CONV_CH), 0)
        dx = dpre * wv[3:4, :]
        last = dpre[tm - 8:tm, :] * wv[3:4, :]
        db_ref[...] += jnp.sum(dpre, axis=0, keepdims=True)
        dws = [jnp.sum(dpre * cur, axis=0, keepdims=True)]
        for sh in range(1, CONV_WIDTH):
            wk = wv[3 - sh:4 - sh, :]
            dx = dx + pltpu.roll(dpre, tm - sh, 0) * wk
            nxt = jnp.where(row8 >= 8 - sh, pltpu.roll(dnext, 8 - sh, 0), pltpu.roll(dpre[tm - 8:tm, :], 8 - sh, 0))
            last = last + nxt * wk
            xs = pltpu.roll(cur, sh, 0)
            first = jnp.where(row8 < sh, pltpu.roll(tail, sh, 0), xs[0:8, :])
            dws.append(jnp.sum(dpre * xs, axis=0, keepdims=True)
                       + jnp.sum(dpre[0:8, :] * (first - xs[0:8, :]), axis=0, keepdims=True))
        dx_ref[...] = dx.astype(BF16)
        dx_ref[tm - 8:tm, :] = last.astype(BF16)
        for sh in range(CONV_WIDTH):
            dw_ref[3 - sh:4 - sh, :] += dws[sh]

    rows = pl.BlockSpec((tm, CONV_CH), lambda i: (i, 0))
    prev8 = pl.BlockSpec((8, CONV_CH), lambda i: (jnp.maximum(i * (tm // 8) - 1, 0), 0))
    next8 = pl.BlockSpec((8, CONV_CH), lambda i: (jnp.minimum((i + 1) * (tm // 8), s // 8 - 1), 0))
    return pl.pallas_call(
        body, name="conv_bwd",
        out_shape=(jax.ShapeDtypeStruct(xbc.shape, BF16), jax.ShapeDtypeStruct((8, CONV_CH), F32),
                   jax.ShapeDtypeStruct((1, CONV_CH), F32)),
        grid=(nb,),
        in_specs=[rows, prev8, rows, next8, rows, next8, _const_spec((CONV_WIDTH, CONV_CH))],
        out_specs=(rows, _const_spec((8, CONV_CH)), _const_spec((1, CONV_CH))),
        compiler_params=_params(("arbitrary",)),
    )(xbc, xbc, pre, pre, dact, dact, w)


def _pair_lanes(mat, j, lane):
    return jnp.where(lane < HEAD_DIM, mat[:, 2 * j:2 * j + 1], mat[:, 2 * j + 1:2 * j + 2])


def _ssd_chunk_prelude(sm, dtb, a_row, lane, sub):
    raw = sm + dtb
    head_lane = lane < N_HEADS
    dt = jnp.where(head_lane, _softplus(raw), 0.0)
    sig = jnp.where(head_lane, _sigmoid(raw), 0.0)
    tri = (lane <= sub).astype(F32)
    acs = _mm_exact(tri, dt * a_row)
    return dt, sig, acs, acs.T


GROUP_WIDTH = SSD_WIDTH // N_GROUPS
HEADS_PER_GROUP = N_HEADS // N_GROUPS


def _expand_group(mat, g, lane):
    return jnp.concatenate([_pair_lanes(mat, j, lane) for j in range(4 * g, 4 * g + 4)], axis=1)


def _head_sums(q, g):
    row = _iota((GROUP_WIDTH, LANES), 0)
    seg = (_iota((GROUP_WIDTH, LANES), 1) == HEADS_PER_GROUP * g + (row >> 6)).astype(BF16)
    hi = q.astype(BF16)
    lo = (q - hi.astype(F32)).astype(BF16)
    return _mm(hi, seg) + _mm(lo, seg)


def _rows_from_lanes(row512):
    return jnp.broadcast_to(row512, (LANES, GROUP_WIDTH)).T


def ssd_fwd(xc, small, dtb_row, a_row, dskip_lane):
    s = xc.shape[0]
    nc = s // CHUNK

    def body(xc_ref, sm_ref, dtb_ref, a_ref, dsk_ref, y_ref, hs_ref, h_scr):
        c = pl.program_id(0)

        @pl.when(c == 0)
        def _():
            h_scr[...] = jnp.zeros_like(h_scr)

        lane = _iota((CHUNK, LANES), 1)
        sub = _iota((CHUNK, LANES), 0)
        causal = lane <= sub
        dt, _, acs, acs_t = _ssd_chunk_prelude(sm_ref[...], dtb_ref[...], a_ref[...], lane, sub)
        for g in range(N_GROUPS):
            cols = slice(GROUP_WIDTH * g, GROUP_WIDTH * (g + 1))
            b_off = SSD_WIDTH + D_STATE * g
            c_off = SSD_WIDTH + N_GROUPS * D_STATE + D_STATE * g
            b_b = xc_ref[:, b_off:b_off + D_STATE].astype(BF16)
            c_b = xc_ref[:, c_off:c_off + D_STATE].astype(BF16)
            cb = _mm_nt(c_b, b_b)
            x_g = xc_ref[:, cols]
            acs_g = _expand_group(acs, g, lane)
            xdt_g = x_g * _expand_group(dt, g, lane)
            xdt_b = xdt_g.astype(BF16)
            heads = range(HEADS_PER_GROUP * g, HEADS_PER_GROUP * (g + 1))
            m_b = [(cb * jnp.exp(jnp.where(causal, acs[:, h:h + 1] - acs_t[h:h + 1, :], NEG_BIG))).astype(BF16)
                   for h in heads]
            yd = [_mm(m_b[k], xdt_b[:, LANES * (k // 2):LANES * (k // 2 + 1)]) for k in range(HEADS_PER_GROUP)]
            yd_g = jnp.concatenate([jnp.where(lane < HEAD_DIM, yd[2 * k], yd[2 * k + 1]) for k in range(4)], axis=1)
            h_g = h_scr[g]
            t_g = _mm_nt(c_b, h_g.astype(BF16))
            y_ref[:, cols] = yd_g + jnp.exp(acs_g) * t_g + dsk_ref[:, cols] * x_g
            hs_ref[0, g] = h_g
            last_g = acs_g[CHUNK - 1:CHUNK, :]
            w_b = (xdt_g * jnp.exp(last_g - acs_g)).astype(BF16)
            h_scr[g] = h_g * jnp.exp(_rows_from_lanes(last_g)) + _mm_tn(w_b, b_b)

    return pl.pallas_call(
        body, name="ssd_fwd",
        out_shape=(jax.ShapeDtypeStruct((s, SSD_WIDTH), F32),
                   jax.ShapeDtypeStruct((nc, N_GROUPS, GROUP_WIDTH, D_STATE), F32)),
        grid=(nc,),
        in_specs=[pl.BlockSpec((CHUNK, CONV_CH), lambda c: (c, 0)), pl.BlockSpec((CHUNK, LANES), lambda c: (c, 0)),
                  _const_spec((1, LANES)), _const_spec((1, LANES)), _const_spec((1, SSD_WIDTH))],
        out_specs=(pl.BlockSpec((CHUNK, SSD_WIDTH), lambda c: (c, 0)),
                   pl.BlockSpec((1, N_GROUPS, GROUP_WIDTH, D_STATE), lambda c: (c, 0, 0, 0))),
        scratch_shapes=[pltpu.VMEM((N_GROUPS, GROUP_WIDTH, D_STATE), F32)],
        compiler_params=_params(("arbitrary",)),
    )(xc, small, dtb_row, a_row, dskip_lane)


def ssd_bwd(xc, small, states, dy, dtb_row, a_row, dskip_lane):
    s = xc.shape[0]
    nc = s // CHUNK
    rev = lambda c: nc - 1 - c

    def body(xc_ref, sm_ref, hs_ref, dy_ref, dtb_ref, a_ref, dsk_ref,
             dxc_ref, ddt_ref, da_ref, ddtb_ref, ddsk_ref, dh_scr):
        c = pl.program_id(0)

        @pl.when(c == 0)
        def _():
            dh_scr[...] = jnp.zeros_like(dh_scr)
            da_ref[...] = jnp.zeros_like(da_ref)
            ddtb_ref[...] = jnp.zeros_like(ddtb_ref)
            ddsk_ref[...] = jnp.zeros_like(ddsk_ref)

        lane = _iota((CHUNK, LANES), 1)
        sub = _iota((CHUNK, LANES), 0)
        causal = lane <= sub
        upper = lane >= sub
        is_last = sub == CHUNK - 1
        a_row_v = a_ref[...]
        dt, sig, acs, acs_t = _ssd_chunk_prelude(sm_ref[...], dtb_ref[...], a_row_v, lane, sub)
        cd = jnp.exp(acs[CHUNK - 1:CHUNK, :])
        dacs_c = jnp.zeros((CHUNK, LANES), F32)
        dacs_r = jnp.zeros((LANES, CHUNK), F32)
        ddtx = jnp.zeros((CHUNK, LANES), F32)
        for g in range(N_GROUPS):
            cols = slice(GROUP_WIDTH * g, GROUP_WIDTH * (g + 1))
            b_off = SSD_WIDTH + D_STATE * g
            c_off = SSD_WIDTH + N_GROUPS * D_STATE + D_STATE * g
            b_b = xc_ref[:, b_off:b_off + D_STATE].astype(BF16)
            c_b = xc_ref[:, c_off:c_off + D_STATE].astype(BF16)
            cb = _mm_nt(c_b, b_b)
            cb_t = _mm_nt(b_b, c_b)
            x_g = xc_ref[:, cols]
            dy_g = dy_ref[:, cols]
            dt_g = _expand_group(dt, g, lane)
            acs_g = _expand_group(acs, g, lane)
            last_g = acs_g[CHUNK - 1:CHUNK, :]
            e_g = jnp.exp(acs_g)
            dte_g = jnp.exp(last_g - acs_g)
            xdt_g = x_g * dt_g
            xdt_b = xdt_g.astype(BF16)
            h_g = hs_ref[0, g]
            dh_g = dh_scr[g]
            h_b = h_g.astype(BF16)
            dh_b = dh_g.astype(BF16)
            heads = list(range(HEADS_PER_GROUP * g, HEADS_PER_GROUP * (g + 1)))
            segs = [acs[:, h:h + 1] - acs_t[h:h + 1, :] for h in heads]
            lms = [jnp.exp(jnp.where(causal, sg, NEG_BIG)) for sg in segs]
            mts = [(cb_t * jnp.exp(jnp.where(upper, -sg, NEG_BIG))).astype(BF16) for sg in segs]
            dyh = []
            for k in range(HEADS_PER_GROUP):
                blk = dy_g[:, LANES * (k // 2):LANES * (k // 2 + 1)]
                in_head = (lane < HEAD_DIM) if k % 2 == 0 else (lane >= HEAD_DIM)
                dyh.append(jnp.where(in_head, blk, 0.0).astype(BF16))
            dms = [_mm_nt(dyh[k], xdt_b[:, LANES * (k // 2):LANES * (k // 2 + 1)]) for k in range(HEADS_PER_GROUP)]
            dxs = [_mm(mts[k], dyh[k]) for k in range(HEADS_PER_GROUP)]
            dcb = jnp.zeros((CHUNK, CHUNK), F32)
            for k, h in enumerate(heads):
                gmat = dms[k] * (cb * lms[k])
                dacs_c = dacs_c + jnp.where(lane == h, jnp.sum(gmat, axis=1, keepdims=True), 0.0)
                dacs_r = dacs_r - jnp.where(sub == h, jnp.sum(gmat, axis=0, keepdims=True), 0.0)
                dcb = dcb + dms[k] * lms[k]
            dxdt_g = jnp.concatenate([dxs[2 * k] + dxs[2 * k + 1] for k in range(4)], axis=1)
            t_g = _mm_nt(c_b, h_b)
            dacs_c = dacs_c + _head_sums(dy_g * e_g * t_g, g)
            dt_b = (dy_g * e_g).astype(BF16)
            dc_acc = _mm(dt_b, h_b)
            dh_prev = _mm_tn(dt_b, c_b)
            dw_g = _mm_nt(b_b, dh_b)
            w_g = xdt_g * dte_g
            dxdt_g = dxdt_g + dw_g * dte_g
            db_acc = _mm(w_g.astype(BF16), dh_b)
            r2 = _head_sums(dw_g * w_g, g)
            dacs_c = dacs_c + jnp.where(is_last, jnp.sum(r2, axis=0, keepdims=True), 0.0) - r2
            q3 = jnp.sum(dh_g * h_g, axis=1, keepdims=True)
            for k, h in enumerate(heads):
                tot = jnp.sum(q3[HEAD_DIM * k:HEAD_DIM * (k + 1), :], keepdims=True) * cd[:, h:h + 1]
                dacs_c = dacs_c + jnp.where(is_last & (lane == h), tot, 0.0)
            dh_scr[g] = dh_prev + dh_g * jnp.exp(_rows_from_lanes(last_g))
            dxc_ref[:, cols] = dxdt_g * dt_g + dsk_ref[:, cols] * dy_g
            ddtx = ddtx + _head_sums(dxdt_g * x_g, g)
            ddsk_ref[:, cols] += jnp.sum(dy_g * x_g, axis=0, keepdims=True)
            dxc_ref[:, b_off:b_off + D_STATE] = db_acc + _mm(dcb.T.astype(BF16), c_b)
            dxc_ref[:, c_off:c_off + D_STATE] = dc_acc + _mm(dcb.astype(BF16), b_b)
        dacs = dacs_c + dacs_r.T
        dadt = _mm_exact((lane >= sub).astype(F32), dacs)
        ddt = dadt * a_row_v + ddtx
        ddt_raw = ddt * sig
        ddt_ref[...] = ddt_raw
        da_ref[...] += jnp.sum(dadt * dt, axis=0, keepdims=True)
        ddtb_ref[...] += jnp.sum(ddt_raw, axis=0, keepdims=True)

    return pl.pallas_call(
        body, name="ssd_bwd",
        out_shape=(jax.ShapeDtypeStruct((s, CONV_CH), F32), jax.ShapeDtypeStruct((s, LANES), F32),
                   jax.ShapeDtypeStruct((1, LANES), F32), jax.ShapeDtypeStruct((1, LANES), F32),
                   jax.ShapeDtypeStruct((1, SSD_WIDTH), F32)),
        grid=(nc,),
        in_specs=[pl.BlockSpec((CHUNK, CONV_CH), lambda c: (rev(c), 0)),
                  pl.BlockSpec((CHUNK, LANES), lambda c: (rev(c), 0)),
                  pl.BlockSpec((1, N_GROUPS, GROUP_WIDTH, D_STATE), lambda c: (rev(c), 0, 0, 0)),
                  pl.BlockSpec((CHUNK, SSD_WIDTH), lambda c: (rev(c), 0)),
                  _const_spec((1, LANES)), _const_spec((1, LANES)), _const_spec((1, SSD_WIDTH))],
        out_specs=(pl.BlockSpec((CHUNK, CONV_CH), lambda c: (rev(c), 0)),
                   pl.BlockSpec((CHUNK, LANES), lambda c: (rev(c), 0)),
                   _const_spec((1, LANES)), _const_spec((1, LANES)), _const_spec((1, SSD_WIDTH))),
        scratch_shapes=[pltpu.VMEM((N_GROUPS, GROUP_WIDTH, D_STATE), F32)],
        compiler_params=_params(("arbitrary",)),
    )(xc, small, states, dy, dtb_row, a_row, dskip_lane)


FORGET_BLOCK = 512


def forget_cumsum(small, fgb_row):
    s = small.shape[0]
    t = _blk(s, FORGET_BLOCK)
    nb = s // t

    def body(sm_ref, b_ref, cc_ref, carry):
        i = pl.program_id(0)

        @pl.when(i == 0)
        def _():
            carry[...] = jnp.zeros_like(carry)

        lane = _iota((t, LANES), 1)
        in_f = (lane >= N_HEADS) & (lane < 2 * N_HEADS)
        logf = jnp.where(in_f, -_softplus(-(sm_ref[...] + b_ref[...])), 0.0)
        tri = (_iota((t, t), 1) <= _iota((t, t), 0)).astype(F32)
        cum = _mm_exact(tri, logf) + carry[0:1, :]
        cc_ref[...] = cum
        carry[...] = jnp.broadcast_to(cum[t - 1:t, :], (8, LANES))

    return pl.pallas_call(
        body, name="forget_cumsum",
        out_shape=jax.ShapeDtypeStruct((s, LANES), F32),
        grid=(nb,),
        in_specs=[pl.BlockSpec((t, LANES), lambda i: (i, 0)), _const_spec((1, LANES))],
        out_specs=pl.BlockSpec((t, LANES), lambda i: (i, 0)),
        scratch_shapes=[pltpu.VMEM((8, LANES), F32)],
        compiler_params=_params(("arbitrary",)),
    )(small, fgb_row)


def forget_bwd(dc, small, ddt_raw, fgb_row):
    s = small.shape[0]
    t = _blk(s, FORGET_BLOCK)
    nb = s // t
    rev = lambda i: nb - 1 - i

    def body(dc_ref, sm_ref, ddt_ref, b_ref, ds_ref, dfb_ref, carry):
        i = pl.program_id(0)

        @pl.when(i == 0)
        def _():
            carry[...] = jnp.zeros_like(carry)
            dfb_ref[...] = jnp.zeros_like(dfb_ref)

        lane = _iota((t, LANES), 1)
        rows = dc_ref[...].T
        tri = (_iota((t, t), 1) <= _iota((t, t), 0)).astype(F32)
        rc = _mm_exact(rows, tri) + carry[:, 0:1]
        carry[...] = jnp.broadcast_to(rc[:, 0:1], (LANES, LANES))
        in_f = (lane >= N_HEADS) & (lane < 2 * N_HEADS)
        df = jnp.where(in_f, rc.T * _sigmoid(-(sm_ref[...] + b_ref[...])), 0.0)
        ds_ref[...] = (df + ddt_ref[...]).astype(BF16)
        dfb_ref[...] += jnp.sum(df, axis=0, keepdims=True)

    blk = pl.BlockSpec((t, LANES), lambda i: (rev(i), 0))
    return pl.pallas_call(
        body, name="forget_bwd",
        out_shape=(jax.ShapeDtypeStruct((s, LANES), BF16), jax.ShapeDtypeStruct((1, LANES), F32)),
        grid=(nb,),
        in_specs=[blk, blk, blk, _const_spec((1, LANES))],
        out_specs=(blk, _const_spec((1, LANES))),
        scratch_shapes=[pltpu.VMEM((LANES, LANES), F32)],
        compiler_params=_params(("arbitrary",)),
    )(dc, small, ddt_raw, fgb_row)


ATT_BLOCK = 512
ATT_SCALE = HEAD_DIM ** -0.5
AUG_A = HEAD_DIM
AUG_B = HEAD_DIM + 3


def _split3(c):
    hi = c.astype(BF16).astype(F32)
    r = c - hi
    mid = r.astype(BF16).astype(F32)
    return hi, mid, (r - mid).astype(BF16).astype(F32)


def _aug(lane, first, parts=None, value=1.0):
    if parts is None:
        return jnp.where((lane >= first) & (lane < first + 3), value, 0.0)
    return (jnp.where(lane == first, parts[0], 0.0) + jnp.where(lane == first + 1, parts[1], 0.0)
            + jnp.where(lane == first + 2, parts[2], 0.0))


def _pack_pair(a0, a1, lane):
    return jnp.where(lane < HEAD_DIM, a0, pltpu.roll(a1, HEAD_DIM, 1))


def proj_qkv_heads(u, w_q, w_k, w_v, cum):
    s = u.shape[0]
    tm = _blk(s, 256)

    def body(u_ref, wq_ref, wk_ref, wv_ref, c_ref, qa_ref, ka_ref, va_ref, nrm_ref):
        lane = _iota((tm, LANES), 1)
        lo = lane < HEAD_DIM
        uv = u_ref[...]
        qf = _mm(uv, wq_ref[...]) * ATT_SCALE
        kf = _mm(uv, wk_ref[...])
        vf = _mm(uv, wv_ref[...])
        cc = c_ref[...]
        ones_a = _aug(lane, AUG_A)
        ones_b = _aug(lane, AUG_B)
        lane8 = _iota((8, LANES), 1)
        sub8 = _iota((8, LANES), 0)
        nrm = jnp.zeros((8, LANES), F32)
        for h in range(N_HEADS):
            j, e = divmod(h, 2)

            def head(full):
                blk = full[:, LANES * j:LANES * (j + 1)]
                if e == 1:
                    blk = pltpu.roll(blk, HEAD_DIM, 1)
                return jnp.where(lo, blk, 0.0)

            parts = _split3(cc[:, N_HEADS + h:N_HEADS + h + 1])
            qh, kh = head(qf), head(kf)
            qa_ref[h] = (qh + _aug(lane, AUG_A, parts) + ones_b).astype(BF16)
            ka_ref[h] = (kh + ones_a - _aug(lane, AUG_B, parts)).astype(BF16)
            va_ref[h] = (head(vf) + ones_a).astype(BF16)
            for r, val in enumerate((qh, kh)):
                big = jnp.max(jnp.sum(val * val, axis=1, keepdims=True), axis=0, keepdims=True)
                nrm = nrm + jnp.where((lane8 == h) & (sub8 == r), big, 0.0)
        nrm_ref[0] = nrm

    shp = jax.ShapeDtypeStruct((N_HEADS, s, LANES), BF16)
    hspec = pl.BlockSpec((N_HEADS, tm, LANES), lambda i: (0, i, 0))
    wspec = _const_spec((D_MODEL, ATT_WIDTH))
    return pl.pallas_call(
        body, name="proj_qkv_heads",
        out_shape=(shp, shp, shp, jax.ShapeDtypeStruct((s // tm, 8, LANES), F32)), grid=(s // tm,),
        in_specs=[pl.BlockSpec((tm, D_MODEL), lambda i: (i, 0)), wspec, wspec, wspec,
                  pl.BlockSpec((tm, LANES), lambda i: (i, 0))],
        out_specs=(hspec, hspec, hspec, pl.BlockSpec((1, 8, LANES), lambda i: (i, 0, 0))),
        compiler_params=_params(("parallel",)),
    )(u, w_q, w_k, w_v, cum)


SKIP_BELOW = -110.0


def live_blocks(norms, cum, t):
    qn = jnp.sqrt(jnp.max(norms[:, 0, :N_HEADS], axis=0))
    kn = jnp.sqrt(jnp.max(norms[:, 1, :N_HEADS], axis=0))
    bound = 2.05 * qn * kn + 2.0
    c_first = cum[0::t, N_HEADS:2 * N_HEADS]
    c_last = cum[t - 1::t, N_HEADS:2 * N_HEADS]
    nq = c_first.shape[0]
    top = bound[None, None, :] + c_first[:, None, :] - c_last[None, :, :]
    below = jnp.arange(nq)[None, :] < jnp.arange(nq)[:, None]
    dead = below[:, :, None] & ~(top >= SKIP_BELOW)
    first = jnp.sum(dead, axis=1).astype(jnp.int32)
    first = jnp.minimum(first[:, 0::2], first[:, 1::2]).T
    last_q = jnp.sum(first[:, None, :] <= jnp.arange(nq)[None, :, None], axis=2).astype(jnp.int32) - 1
    return first, last_q


def attention_fwd(first, qa, ka, va):
    s = qa.shape[1]
    t = _blk(s, ATT_BLOCK)
    nq = s // t

    def body(first_ref, qa_ref, ka_ref, va_ref, o_ref, qb_ref, m_scr, acc_scr, alpha_scr, p_scr, s_scr):
        qi = pl.program_id(1)
        k0 = first_ref[pl.program_id(0), qi]
        m_scr[...] = jnp.full_like(m_scr, NEG_BIG)
        acc_scr[...] = jnp.zeros_like(acc_scr)

        def kv_rows(kb):
            return pl.ds(pl.multiple_of(kb * t, t), t)

        def softmax_block(kb, masked):
            for e in range(2):
                sc = _mm_nt(qa_ref[e], ka_ref[e, kv_rows(kb), :])
                if masked:
                    sc = jnp.where(_iota((t, t), 0) >= _iota((t, t), 1), sc, NEG_BIG)
                s_scr[e] = sc
                cmax = s_scr[e, :, 0:LANES]
                for c in range(1, t // LANES):
                    cmax = jnp.maximum(cmax, s_scr[e, :, LANES * c:LANES * (c + 1)])
                m_old = m_scr[e]
                m_new = jnp.maximum(m_old, jnp.max(cmax, axis=1, keepdims=True))
                alpha_scr[e] = jnp.exp(m_old - m_new)
                m_scr[e] = m_new
                for c in range(t // LANES):
                    cols = slice(LANES * c, LANES * (c + 1))
                    p_scr[e, :, cols] = jnp.exp(s_scr[e, :, cols] - m_new).astype(BF16)

        def accumulate(kb):
            for e in range(2):
                acc_scr[e] = alpha_scr[e] * acc_scr[e] + _mm(p_scr[e], va_ref[e, kv_rows(kb), :])

        def loop_body(kb, carry):
            accumulate(kb - 1)
            softmax_block(kb, False)
            return carry

        @pl.when(qi > k0)
        def _():
            softmax_block(k0, False)

        lax.fori_loop(k0 + 1, qi, loop_body, 0)

        @pl.when(qi > k0)
        def _():
            accumulate(qi - 1)
            softmax_block(qi, True)

        @pl.when(qi == k0)
        def _():
            softmax_block(qi, True)

        accumulate(qi)

        lane = _iota((t, LANES), 1)
        outs = []
        for e in range(2):
            acc = acc_scr[e]
            l = acc[:, AUG_A:AUG_A + 1]
            outs.append(acc / l)
            lse = m_scr[e][:, 0:1] + jnp.log(l)
            q32 = qa_ref[e].astype(F32)
            c = q32[:, AUG_A:AUG_A + 1] + q32[:, AUG_A + 1:AUG_A + 2] + q32[:, AUG_A + 2:AUG_A + 3]
            qb = jnp.where(lane < HEAD_DIM, q32, 0.0) + _aug(lane, AUG_A, _split3(c - lse)) + _aug(lane, AUG_B)
            qb_ref[e] = qb.astype(BF16)
        o_ref[...] = _pack_pair(outs[0], outs[1], lane)

    grid_spec = pltpu.PrefetchScalarGridSpec(
        num_scalar_prefetch=1, grid=(N_PAIRS, nq),
        in_specs=[pl.BlockSpec((2, t, LANES), lambda j, qi, f: (j, qi, 0)),
                  pl.BlockSpec((2, s, LANES), lambda j, qi, f: (j, 0, 0)),
                  pl.BlockSpec((2, s, LANES), lambda j, qi, f: (j, 0, 0))],
        out_specs=[pl.BlockSpec((t, LANES), lambda j, qi, f: (qi, j)),
                   pl.BlockSpec((2, t, LANES), lambda j, qi, f: (j, qi, 0))],
        scratch_shapes=[pltpu.VMEM((2, t, LANES), F32), pltpu.VMEM((2, t, LANES), F32),
                        pltpu.VMEM((2, t, LANES), F32), pltpu.VMEM((2, t, t), BF16), pltpu.VMEM((2, t, t), F32)])
    return pl.pallas_call(
        body, name="attention_fwd", grid_spec=grid_spec,
        out_shape=(jax.ShapeDtypeStruct((s, ATT_WIDTH), F32), jax.ShapeDtypeStruct((N_HEADS, s, LANES), BF16)),
        compiler_params=_params(("parallel", "parallel")),
    )(first, qa, ka, va)


def attention_bwd(last_q, qb, ka, va, dob):
    s = qb.shape[1]
    t = _blk(s, ATT_BLOCK)
    nq = s // t

    def body(last_ref, qb_ref, dob_ref, ka_ref, va_ref, dq_ref, dk_ref, dv_ref, dc_ref, dq_scr, dk_scr, dv_scr):
        j, ki = pl.program_id(0), pl.program_id(1)

        @pl.when((j == 0) & (ki == 0))
        def _():
            dc_ref[...] = jnp.zeros_like(dc_ref)

        @pl.when(ki == 0)
        def _():
            dq_scr[...] = jnp.zeros_like(dq_scr)

        dk_scr[...] = jnp.zeros_like(dk_scr)
        dv_scr[...] = jnp.zeros_like(dv_scr)

        def q_step(qblk, masked):
            rows = pl.ds(pl.multiple_of(qblk * t, t), t)
            for e in range(2):
                q = qb_ref[e, rows, :]
                do = dob_ref[e, rows, :]
                sc = _mm_nt(q, ka_ref[e])
                if masked:
                    sc = jnp.where(_iota((t, t), 0) >= _iota((t, t), 1), sc, NEG_BIG)
                p = jnp.exp(sc)
                ds_b = (p * _mm_nt(do, va_ref[e])).astype(BF16)
                dv_scr[e] += _mm_tn(p.astype(BF16), do)
                dk_scr[e] += _mm_tn(ds_b, q)
                dq_scr[e, rows, :] += _mm(ds_b, ka_ref[e])

        def loop_body(qblk, carry):
            q_step(qblk, False)
            return carry

        q_step(ki, True)
        lax.fori_loop(ki + 1, last_ref[j, ki] + 1, loop_body, 0)

        lane = _iota((t, LANES), 1)
        dk_ref[...] = _pack_pair(dk_scr[0], dk_scr[1], lane).astype(BF16)
        dv_ref[...] = _pack_pair(dv_scr[0], dv_scr[1], lane).astype(BF16)
        rows = pl.ds(pl.multiple_of(ki * t, t), t)
        dc_ref[rows, :] -= (jnp.where(lane == N_HEADS + 2 * j, dk_scr[0][:, AUG_B:AUG_B + 1], 0.0)
                            + jnp.where(lane == N_HEADS + 2 * j + 1, dk_scr[1][:, AUG_B:AUG_B + 1], 0.0))

        @pl.when(ki == nq - 1)
        def _():
            for blk in range(nq):
                rws = pl.ds(blk * t, t)
                d0 = dq_scr[0, rws, :]
                d1 = dq_scr[1, rws, :]
                dq_ref[rws, :] = (_pack_pair(d0, d1, lane) * ATT_SCALE).astype(BF16)
                dc_ref[rws, :] += (jnp.where(lane == N_HEADS + 2 * j, d0[:, AUG_A:AUG_A + 1], 0.0)
                                   + jnp.where(lane == N_HEADS + 2 * j + 1, d1[:, AUG_A:AUG_A + 1], 0.0))

    full = pl.BlockSpec((2, s, LANES), lambda j, ki, f: (j, 0, 0))
    blk = pl.BlockSpec((2, t, LANES), lambda j, ki, f: (j, ki, 0))
    pair = pl.BlockSpec((t, LANES), lambda j, ki, f: (ki, j))
    wide = jax.ShapeDtypeStruct((s, ATT_WIDTH), BF16)
    grid_spec = pltpu.PrefetchScalarGridSpec(
        num_scalar_prefetch=1, grid=(N_PAIRS, nq),
        in_specs=[full, full, blk, blk],
        out_specs=[pl.BlockSpec((s, LANES), lambda j, ki, f: (0, j)), pair, pair,
                   pl.BlockSpec((s, LANES), lambda j, ki, f: (0, 0))],
        scratch_shapes=[pltpu.VMEM((2, s, LANES), F32), pltpu.VMEM((2, t, LANES), F32),
                        pltpu.VMEM((2, t, LANES), F32)])
    return pl.pallas_call(
        body, name="attention_bwd", grid_spec=grid_spec,
        out_shape=(wide, wide, wide, jax.ShapeDtypeStruct((s, LANES), F32)),
        compiler_params=_params(("arbitrary", "arbitrary")),
    )(last_q, qb, dob, ka, va)


def _dsilu(z, sg):
    return sg * (1.0 + z * (1.0 - sg))


def post_mix(x, y, zs, o, za, p, tgt, ssd_g, att_g_lane, ple_g, fin_g, w_out, w_gate, w_proj):
    s = x.shape[0]
    tm = _blk(s, 128)
    half = SSD_WIDTH // N_GROUPS

    def rms_bwd(dy, yn, r):
        return r * (dy - yn * jnp.mean(dy * yn, axis=-1, keepdims=True))

    def colsum(a):
        return jnp.sum(a, axis=0, keepdims=True)

    def body(x_ref, y_ref, zs_ref, o_ref, za_ref, p_ref, t_ref, sg_ref, ag_ref, pg_ref, fg_ref,
             wo_ref, wg_ref, wp_ref,
             dh1_ref, dy_ref, dzs_ref, dob_ref, dza_ref, ycat_ref, dh1b_ref, n2b_ref, dglb_ref, dppb_ref, pb_ref,
             loss_ref, dfin_ref, dple_ref, dssd_ref, datt_ref):
        @pl.when(pl.program_id(0) == 0)
        def _():
            for r in (loss_ref, dfin_ref, dple_ref, dssd_ref, datt_ref):
                r[...] = jnp.zeros_like(r)

        lane = _iota((tm, LANES), 1)
        lo = lane < HEAD_DIM
        zs = zs_ref[...]
        sz = _sigmoid(zs)
        yv = y_ref[...]
        ys = yv * (zs * sz)
        yn, rg = [], []
        for g in range(N_GROUPS):
            seg = ys[:, half * g:half * (g + 1)]
            r = lax.rsqrt(jnp.mean(seg * seg, axis=-1, keepdims=True) + EPS)
            yn.append(seg * r)
            rg.append(r)
            ycat_ref[:, half * g:half * (g + 1)] = (yn[g] * sg_ref[:, half * g:half * (g + 1)]).astype(BF16)
        za = za_ref[...]
        sza = _sigmoid(za)
        silu_za = za * sza
        on, ra = [], []
        for jb in range(N_PAIRS):
            blk = o_ref[:, LANES * jb:LANES * (jb + 1)]
            sq = blk * blk
            ms0 = jnp.sum(jnp.where(lo, sq, 0.0), axis=1, keepdims=True) * (1.0 / HEAD_DIM)
            ms1 = jnp.sum(jnp.where(lo, 0.0, sq), axis=1, keepdims=True) * (1.0 / HEAD_DIM)
            r = jnp.where(lo, lax.rsqrt(ms0 + EPS), lax.rsqrt(ms1 + EPS))
            on.append(blk * r)
            ra.append(r)
            an = on[jb] * ag_ref[:, LANES * jb:LANES * (jb + 1)]
            ycat_ref[:, SSD_WIDTH + LANES * jb:SSD_WIDTH + LANES * (jb + 1)] = (
                an * silu_za[:, LANES * jb:LANES * (jb + 1)]).astype(BF16)
        h1 = x_ref[...] + _mm(ycat_ref[...], wo_ref[...])
        r2 = lax.rsqrt(jnp.mean(h1 * h1, axis=-1, keepdims=True) + EPS)
        n2h = h1 * r2
        n2_b = (n2h * pg_ref[...]).astype(BF16)
        gate = _sigmoid(_mm(n2_b, wg_ref[...]))
        p_b = p_ref[...].astype(BF16)
        pp = _mm(p_b, wp_ref[...])
        h2 = h1 + gate * pp
        r3 = lax.rsqrt(jnp.mean(h2 * h2, axis=-1, keepdims=True) + EPS)
        n3 = h2 * r3
        diff = n3 * fg_ref[...] - t_ref[...]
        sq = colsum(diff * diff)
        part = sq[:, 0:LANES]
        for jb in range(1, D_MODEL // LANES):
            part = part + sq[:, LANES * jb:LANES * (jb + 1)]
        loss_ref[...] += part * (0.5 / D_MODEL)
        dout = diff * (1.0 / D_MODEL)
        dfin_ref[...] += colsum(dout * n3)
        dh2 = rms_bwd(dout * fg_ref[...], n3, r3)
        dgl = dh2 * pp * gate * (1.0 - gate)
        dgl_b = dgl.astype(BF16)
        dn2 = _mm_nt(dgl_b, wg_ref[...])
        dple_ref[...] += colsum(dn2 * n2h)
        dh1 = dh2 + rms_bwd(dn2 * pg_ref[...], n2h, r2)
        dh1_b = dh1.astype(BF16)
        dycat = _mm_nt(dh1_b, wo_ref[...])
        dh1_ref[...] = dh1
        dh1b_ref[...] = dh1_b
        n2b_ref[...] = n2_b
        dglb_ref[...] = dgl_b
        dppb_ref[...] = (dh2 * gate).astype(BF16)
        pb_ref[...] = p_b
        for g in range(N_GROUPS):
            cols = slice(half * g, half * (g + 1))
            dys_g = dycat[:, cols]
            dssd_ref[:, cols] += colsum(dys_g * yn[g])
            dys = rms_bwd(dys_g * sg_ref[:, cols], yn[g], rg[g])
            dy_ref[:, cols] = dys * (zs[:, cols] * sz[:, cols])
            dzs_ref[:, cols] = (dys * yv[:, cols] * _dsilu(zs[:, cols], sz[:, cols])).astype(BF16)
        for jb in range(N_PAIRS):
            cols = slice(LANES * jb, LANES * (jb + 1))
            dya = dycat[:, SSD_WIDTH + LANES * jb:SSD_WIDTH + LANES * (jb + 1)]
            ag = ag_ref[:, cols]
            dan = dya * silu_za[:, cols]
            dza_ref[:, cols] = (dya * (on[jb] * ag) * _dsilu(za[:, cols], sza[:, cols])).astype(BF16)
            datt_ref[:, cols] += colsum(dan * on[jb])
            don = dan * ag
            q = don * on[jb]
            m0 = jnp.sum(jnp.where(lo, q, 0.0), axis=1, keepdims=True) * (1.0 / HEAD_DIM)
            m1 = jnp.sum(jnp.where(lo, 0.0, q), axis=1, keepdims=True) * (1.0 / HEAD_DIM)
            do2 = ra[jb] * (don - on[jb] * jnp.where(lo, m0, m1))
            prod = do2 * o_ref[:, cols]
            for e in range(2):
                delta = jnp.sum(jnp.where(lo, prod, 0.0) if e == 0 else jnp.where(lo, 0.0, prod),
                                axis=1, keepdims=True)
                base = jnp.where(lo, do2 if e == 0 else pltpu.roll(do2, HEAD_DIM, 1), 0.0)
                dob_ref[2 * jb + e] = (base - _aug(lane, AUG_A, _split3(delta))).astype(BF16)

    def rows(n, dtype=None):
        return pl.BlockSpec((tm, n), lambda i: (i, 0))

    def out(n, dtype):
        return jax.ShapeDtypeStruct((s, n), dtype)

    vec = _const_spec((1, D_MODEL))
    vshape = jax.ShapeDtypeStruct((1, D_MODEL), F32)
    return pl.pallas_call(
        body, name="post_mix",
        out_shape=(out(D_MODEL, F32), out(SSD_WIDTH, F32), out(SSD_WIDTH, BF16),
                   jax.ShapeDtypeStruct((N_HEADS, s, LANES), BF16),
                   out(ATT_WIDTH, BF16), out(D_INNER, BF16), out(D_MODEL, BF16), out(D_MODEL, BF16),
                   out(D_MODEL, BF16), out(D_MODEL, BF16), out(PLE_DIM, BF16),
                   jax.ShapeDtypeStruct((1, LANES), F32), vshape, vshape, vshape, vshape),
        grid=(s // tm,),
        in_specs=[rows(D_MODEL), rows(SSD_WIDTH), rows(SSD_WIDTH), rows(ATT_WIDTH), rows(ATT_WIDTH),
                  rows(PLE_DIM), rows(D_MODEL), vec, vec, vec, vec,
                  _const_spec((D_INNER, D_MODEL)), _const_spec((D_MODEL, D_MODEL)), _const_spec((PLE_DIM, D_MODEL))],
        out_specs=(rows(D_MODEL), rows(SSD_WIDTH), rows(SSD_WIDTH),
                   pl.BlockSpec((N_HEADS, tm, LANES), lambda i: (0, i, 0)), rows(ATT_WIDTH),
                   rows(D_INNER), rows(D_MODEL), rows(D_MODEL), rows(D_MODEL), rows(D_MODEL), rows(PLE_DIM),
                   _const_spec((1, LANES)), vec, vec, vec, vec),
        compiler_params=_params(("arbitrary",)),
    )(x, y, zs, o, za, p, tgt, ssd_g, att_g_lane, ple_g, fin_g, w_out, w_gate, w_proj)


def in_proj_bwd(dsegs, wsegs, x, g, dh1):
    s = x.shape[0]
    tm = _blk(s, 256)
    nseg = len(dsegs)

    def body(*refs):
        d_refs = refs[:nseg]
        w_refs = refs[nseg:2 * nseg]
        x_ref, g_ref, dh1_ref, dx_ref, dg_ref = refs[2 * nseg:]

        @pl.when(pl.program_id(0) == 0)
        def _():
            dg_ref[...] = jnp.zeros_like(dg_ref)

        du = _mm_nt(d_refs[0][...], w_refs[0][...])
        for k in range(1, nseg):
            du = du + _mm_nt(d_refs[k][...], w_refs[k][...])
        xv = x_ref[...]
        r = lax.rsqrt(jnp.mean(xv * xv, axis=-1, keepdims=True) + EPS)
        xh = xv * r
        dg_ref[...] += jnp.sum(du * xh, axis=0, keepdims=True)
        dxh = du * g_ref[...]
        dx_ref[...] = r * (dxh - xh * jnp.mean(dxh * xh, axis=-1, keepdims=True)) + dh1_ref[...]

    rows = lambda n: pl.BlockSpec((tm, n), lambda i: (i, 0))
    return pl.pallas_call(
        body, name="in_proj_bwd",
        out_shape=(jax.ShapeDtypeStruct((s, D_MODEL), F32), jax.ShapeDtypeStruct((1, D_MODEL), F32)),
        grid=(s // tm,),
        in_specs=([rows(d.shape[1]) for d in dsegs] + [_const_spec(w.shape) for w in wsegs]
                  + [rows(D_MODEL), _const_spec((1, D_MODEL)), rows(D_MODEL)]),
        out_specs=(rows(D_MODEL), _const_spec((1, D_MODEL))),
        compiler_params=_params(("arbitrary",)),
    )(*dsegs, *wsegs, x, g, dh1)


SMALL_NAMES = ("norm_g", "conv_b", "dt_bias", "a_log", "d_skip", "ssd_norm_g", "fg_bias", "att_norm_g",
               "ple_norm_g", "final_norm_g")
SMALL_SIZES = (1024, 1536, 16, 16, 16, 1024, 16, 64, 1024, 1024)
CONV_W_SIZE = CONV_WIDTH * CONV_CH


def _pack_small(vals):
    flat = jnp.concatenate([v.reshape(-1).astype(F32) for v in vals])
    flat = jnp.pad(flat, (0, SMALL_ROWS * LANES - flat.shape[0]))
    return flat.reshape(SMALL_ROWS, LANES)


def _unpack_small(pack, shapes):
    flat = pack.reshape(-1)
    out, off = [], 0
    for n, shp in zip(SMALL_SIZES, shapes):
        out.append(flat[off:off + n].reshape(shp))
        off += n
    return out


def _row128(v16, offset=0):
    return jnp.pad(v16.reshape(1, N_HEADS).astype(F32), ((0, 0), (offset, LANES - N_HEADS - offset)))


def local_step(x, p, tgt, w_in, w_out, w_gate, w_proj, conv_w, norm_g, conv_b, dt_bias, a_log, d_skip,
               ssd_norm_g, fg_bias, att_norm_g, ple_norm_g, final_norm_g):
    c0, c1, c2, c3, c4, c5, c6, c7 = 0, 1024, 2560, 2576, 3600, 4624, 5648, 6672
    w_zs, w_xbc, w_dt = w_in[:, c0:c1], w_in[:, c1:c2], w_in[:, c2:c3]
    w_za, w_q, w_k, w_v, w_f = w_in[:, c3:c4], w_in[:, c4:c5], w_in[:, c5:c6], w_in[:, c6:c7], w_in[:, c7:]
    w_small = jnp.concatenate([w_dt, w_f, jnp.zeros((D_MODEL, LANES - 2 * N_HEADS), BF16)], axis=1)

    dtb_row = _row128(dt_bias)
    a_row = _row128(-jnp.exp(a_log.astype(F32)))
    fgb_row = _row128(fg_bias, N_HEADS)
    dskip_lane = jnp.repeat(d_skip.astype(F32), HEAD_DIM).reshape(1, SSD_WIDTH)
    att_g_lane = jnp.tile(att_norm_g.astype(F32), N_HEADS).reshape(1, ATT_WIDTH)
    row = lambda v: v.reshape(1, -1).astype(F32)

    u = rms_prenorm(x, row(norm_g))
    zs = matmul_rows(u, w_zs, F32, "proj_z_ssd")
    xbc = matmul_rows(u, w_xbc, F32, "proj_xbc")
    za = matmul_rows(u, w_za, F32, "proj_z_att")
    small = matmul_rows(u, w_small, F32, "proj_small")
    cum = forget_cumsum(small, fgb_row)
    qa, ka, va, norms = proj_qkv_heads(u, w_q, w_k, w_v, cum)
    first, last_q = live_blocks(norms, cum, _blk(x.shape[0], ATT_BLOCK))
    pre, xc = conv_fwd(xbc, conv_w, row(conv_b))
    y, states = ssd_fwd(xc, small, dtb_row, a_row, dskip_lane)
    o, qb = attention_fwd(first, qa, ka, va)
    (dh1, dy, dzs, dob, dza, ycat, dh1_b, n2_b, dgl_b, dpp_b, p_b,
     loss_l, dfin, dple, dssd_g, datt_lane) = post_mix(
        x, y, zs, o, za, p, tgt, row(ssd_norm_g), att_g_lane, row(ple_norm_g), row(final_norm_g),
        w_out, w_gate, w_proj)
    dq, dk, dv, dc = attention_bwd(last_q, qb, ka, va, dob)
    dxc, ddt_raw, da, ddtb, ddsk_lane = ssd_bwd(xc, small, states, dy, dtb_row, a_row, dskip_lane)
    dsmall, dfgb = forget_bwd(dc, small, ddt_raw, fgb_row)
    dxbc, dconv_w8, dconv_b = conv_bwd(xbc, pre, dxc, conv_w)
    dsegs = [dzs, dxbc, dza, dq, dk, dv, dsmall]
    wsegs = [w_zs, w_xbc, w_za, w_q, w_k, w_v, w_small]
    dx, dnorm_g = in_proj_bwd(dsegs, wsegs, x, row(norm_g), dh1)
    dws = [matmul_tn(u, d, "dw_in_%d" % i) for i, d in enumerate(dsegs)]
    dw_in = jnp.concatenate([dws[0], dws[1], dws[6][:, :N_HEADS], dws[2], dws[3], dws[4], dws[5],
                             dws[6][:, N_HEADS:2 * N_HEADS]], axis=1)
    dw_out = matmul_tn(ycat, dh1_b, "dw_out")
    dw_gate = matmul_tn(n2_b, dgl_b, "dw_gate")
    dw_proj = matmul_tn(p_b, dpp_b, "dw_proj")
    small_grads = [
        dnorm_g, dconv_b, ddtb[0, :N_HEADS], (da * a_row)[0, :N_HEADS],
        ddsk_lane.reshape(N_HEADS, HEAD_DIM).sum(axis=1), dssd_g, dfgb[0, N_HEADS:2 * N_HEADS],
        datt_lane.reshape(N_HEADS, HEAD_DIM).sum(axis=0), dple, dfin]
    loss = jnp.sum(loss_l)
    return loss, dx, dw_in, dw_out, dw_gate, dw_proj, dconv_w8[:CONV_WIDTH], small_grads


def kernel(x, p, norm_g, w_in, conv_w, conv_b, dt_bias, a_log, d_skip, ssd_norm_g, fg_bias, att_norm_g, w_out, ple_norm_g, w_ple_gate, w_ple_proj, final_norm_g, loss_target, m_norm_g, m_w_in, m_conv_w, m_conv_b, m_dt_bias, m_a_log, m_d_skip, m_ssd_norm_g, m_fg_bias, m_att_norm_g, m_w_out, m_ple_norm_g, m_w_ple_gate, m_w_ple_proj, m_final_norm_g, v_norm_g, v_w_in, v_conv_w, v_conv_b, v_dt_bias, v_a_log, v_d_skip, v_ssd_norm_g, v_fg_bias, v_att_norm_g, v_w_out, v_ple_norm_g, v_w_ple_gate, v_w_ple_proj, v_final_norm_g):
    chip = 2 * lax.axis_index("x") + lax.axis_index("y")
    core = lax.axis_index("c")

    big_w = [w_in[0], w_out[0], w_ple_gate[0], w_ple_proj[0]]
    own = [a.astype(BF16) for a in big_w] + [conv_w[0]]
    gathered = gather_weights(own[:4], own[4])

    def joined(k, axis):
        return jnp.concatenate([jnp.where(chip == j, own[k], gathered[k][j]) for j in range(N_CHIPS)], axis=axis)

    w_in_f, w_out_f, w_gate_f, w_proj_f, conv_w_f = joined(0, 1), joined(1, 0), joined(2, 0), joined(3, 1), joined(4, 1)

    smalls_w = [norm_g, conv_b, dt_bias, a_log, d_skip, ssd_norm_g, fg_bias, att_norm_g, ple_norm_g, final_norm_g]
    loss_l, dx, dw_in, dw_out, dw_gate, dw_proj, dconv_w, small_grads = local_step(
        x[0], p[0, 0], loss_target[0], w_in_f, w_out_f, w_gate_f, w_proj_f, conv_w_f,
        *[a.reshape(-1) for a in smalls_w])
    loss = lax.psum(loss_l, ("x", "y", "c"))

    gs = [jnp.stack([dw_in[:, 1672 * j:1672 * (j + 1)] for j in range(N_CHIPS)]),
          dw_out.reshape(N_CHIPS, 512, D_MODEL), dw_gate.reshape(N_CHIPS, 256, D_MODEL),
          jnp.stack([dw_proj[:, 256 * j:256 * (j + 1)] for j in range(N_CHIPS)])]
    core1 = core.reshape(1).astype(jnp.int32)
    pres = add_halves(core1, gs, halves_to_sibling(gs))
    *parts, smalls = scatter_halves(pres, _pack_small(list(small_grads) + [dconv_w]))
    mine = sum_parts(parts)

    g_big, d_big, m_big, v_big = adamw_big(
        core1, mine, swap_halves(mine), big_w, [m_w_in[0], m_w_out[0], m_w_ple_gate[0], m_w_ple_proj[0]],
        [v_w_in[0], v_w_out[0], v_w_ple_gate[0], v_w_ple_proj[0]])
    smalls_m = [m_norm_g, m_conv_b, m_dt_bias, m_a_log, m_d_skip, m_ssd_norm_g, m_fg_bias, m_att_norm_g,
                m_ple_norm_g, m_final_norm_g]
    smalls_v = [v_norm_g, v_conv_b, v_dt_bias, v_a_log, v_d_skip, v_ssd_norm_g, v_fg_bias, v_att_norm_g,
                v_ple_norm_g, v_final_norm_g]
    g_sm, d_sm, m_sm, v_sm = adamw_small(smalls, _pack_small(smalls_w), _pack_small(smalls_m), _pack_small(smalls_v))
    n_small = sum(SMALL_SIZES)
    g_conv_full = g_sm.reshape(-1)[n_small:n_small + CONV_W_SIZE].reshape(CONV_WIDTH, CONV_CH)
    g_conv = lax.dynamic_slice_in_dim(g_conv_full, chip * 384, 384, axis=1)
    d_conv, m_conv, v_conv = adamw_whole(g_conv, conv_w[0], m_conv_w[0], v_conv_w[0], "adamw_conv")

    shapes = [a.shape for a in smalls_w]
    outs = []
    for big, conv, sm in ((g_big, g_conv, g_sm), (d_big, d_conv, d_sm), (m_big, m_conv, m_sm), (v_big, v_conv, v_sm)):
        b_in, b_out, b_gate, b_proj = [a[None] for a in big]
        s_norm, s_convb, s_dtb, s_alog, s_dsk, s_ssdg, s_fgb, s_attg, s_pleg, s_fin = _unpack_small(sm, shapes)
        outs.extend([s_norm, b_in, conv[None], s_convb, s_dtb, s_alog, s_dsk, s_ssdg, s_fgb, s_attg, b_out, s_pleg,
                     b_gate, b_proj, s_fin])
    return (loss, dx[None], *outs)
```

```python
import functools

import jax
import jax.numpy as jnp
from jax import lax
from jax.experimental import pallas as pl
from jax.experimental.pallas import tpu as pltpu

F32 = jnp.float32
BF16 = jnp.bfloat16

D_MODEL = 1024
SSD_WIDTH = 1024
ATT_WIDTH = 1024
N_HEADS = 16
HEAD_DIM = 64
N_GROUPS = 2
D_STATE = 128
CONV_CH = 1536
CONV_WIDTH = 4
CHUNK = 128
PLE_DIM = 256
D_INNER = 2048
EPS = 1e-6
IN_COLS = 6688
N_CHIPS = 4
N_DEV = 8
LANES = 128
N_PAIRS = 8

ADAM_LR = 0.001
ADAM_B1 = 0.9
ADAM_B2 = 0.999
ADAM_EPS = 1e-08
ADAM_WD = 0.01
ADAM_STEP = 10

SMALL_ROWS = 96

NEG_BIG = -1e30
VMEM_LIMIT = 56 * 1024 * 1024

MESH = pl.DeviceIdType.MESH
ANY = pl.BlockSpec(memory_space=pl.ANY)


def _mm(a, b):
    return jnp.dot(a, b, preferred_element_type=F32)


def _mm_nt(a, b):
    return lax.dot_general(a, b, (((1,), (1,)), ((), ())), preferred_element_type=F32)


def _mm_tn(a, b):
    return lax.dot_general(a, b, (((0,), (0,)), ((), ())), preferred_element_type=F32)


def _mm_exact(a, b):
    return jnp.dot(a, b, preferred_element_type=F32, precision=lax.Precision.HIGHEST)


def _softplus(x):
    return jnp.maximum(x, 0.0) + jnp.log1p(jnp.exp(-jnp.abs(x)))


def _sigmoid(x):
    return jax.nn.sigmoid(x)


def _iota(shape, dim):
    return lax.broadcasted_iota(jnp.int32, shape, dim)


def _params(sem=None):
    return pltpu.CompilerParams(dimension_semantics=sem, vmem_limit_bytes=VMEM_LIMIT)


def _blk(n, pref):
    return min(n, pref)


def _const_spec(shape):
    nd = len(shape)
    return pl.BlockSpec(shape, lambda *_: (0,) * nd)


def _chip_peers():
    x, y, c = lax.axis_index("x"), lax.axis_index("y"), lax.axis_index("c")
    return x, y, c, [(1 - x, y, c), (x, 1 - y, c), (1 - x, 1 - y, c)]


def _half(rows, c):
    h = rows // 2
    return pl.ds(pl.multiple_of(c * h, 8), h)


def _sems(n):
    return [pltpu.SemaphoreType.DMA((n,)), pltpu.SemaphoreType.DMA((n,))]


def gather_weights(shards, conv_s):
    n = len(shards)

    def body(*refs):
        ins, conv_in = refs[:n], refs[n]
        outs, conv_out = refs[n + 1:2 * n + 1], refs[2 * n + 1]
        ssem1, rsem1, ssem2, rsem2, c_ssem, c_rsem = refs[2 * n + 2:]
        x, y, c, peers = _chip_peers()
        me = 2 * x + y
        sibling = (x, y, 1 - c)
        first, small = [], []
        for k, peer in enumerate(peers):
            for i in range(n):
                h = _half(ins[i].shape[0], c)
                first.append(pltpu.make_async_remote_copy(
                    src_ref=ins[i].at[h], dst_ref=outs[i].at[me, h], send_sem=ssem1.at[n * k + i],
                    recv_sem=rsem1.at[n * k + i], device_id=peer, device_id_type=MESH))
            small.append(pltpu.make_async_remote_copy(
                src_ref=conv_in, dst_ref=conv_out.at[me], send_sem=c_ssem.at[k], recv_sem=c_rsem.at[k],
                device_id=peer, device_id_type=MESH))
        for cp in first + small:
            cp.start()
        passed = []
        for k, peer in enumerate(peers):
            chip = 2 * peer[0] + peer[1]
            for i in range(n):
                h = _half(ins[i].shape[0], c)
                first[n * k + i].wait_recv()
                fwd = pltpu.make_async_remote_copy(
                    src_ref=outs[i].at[chip, h], dst_ref=outs[i].at[chip, h], send_sem=ssem2.at[n * k + i],
                    recv_sem=rsem2.at[n * k + i], device_id=sibling, device_id_type=MESH)
                fwd.start()
                passed.append(fwd)
        for cp in passed:
            cp.wait_recv()
        for cp in first + passed:
            cp.wait_send()
        for cp in small:
            cp.wait()

    return pl.pallas_call(
        body, name="gather_weights",
        out_shape=tuple(jax.ShapeDtypeStruct((N_CHIPS,) + a.shape, a.dtype) for a in list(shards) + [conv_s]),
        in_specs=[ANY] * (n + 1), out_specs=(ANY,) * (n + 1),
        scratch_shapes=_sems(3 * n) + _sems(3 * n) + _sems(3),
    )(*shards, conv_s)


def halves_to_sibling(gs):
    n = len(gs)

    def body(*refs):
        ins, outs = refs[:n], refs[n:2 * n]
        ssem, rsem = refs[2 * n:]
        x, y, c = lax.axis_index("x"), lax.axis_index("y"), lax.axis_index("c")
        copies = []
        for i in range(n):
            for j in range(N_CHIPS):
                copies.append(pltpu.make_async_remote_copy(
                    src_ref=ins[i].at[j, _half(ins[i].shape[1], 1 - c)], dst_ref=outs[i].at[j],
                    send_sem=ssem.at[N_CHIPS * i + j], recv_sem=rsem.at[N_CHIPS * i + j],
                    device_id=(x, y, 1 - c), device_id_type=MESH))
        for cp in copies:
            cp.start()
        for cp in copies:
            cp.wait()

    return pl.pallas_call(
        body, name="halves_to_sibling",
        out_shape=tuple(jax.ShapeDtypeStruct((N_CHIPS, g.shape[1] // 2, g.shape[2]), F32) for g in gs),
        in_specs=[ANY] * n, out_specs=(ANY,) * n, scratch_shapes=_sems(N_CHIPS * n),
    )(*gs)


RED_GRID = 8


def add_halves(core, gs, rbs):
    n = len(gs)

    def body(c_ref, *refs):
        for i in range(n):
            refs[2 * n + i][...] = (refs[i][...] + refs[n + i][...]).astype(BF16)

    def blk(g):
        return (1, g.shape[1] // 2 // RED_GRID, g.shape[2])

    grid_spec = pltpu.PrefetchScalarGridSpec(
        num_scalar_prefetch=1, grid=(N_CHIPS, RED_GRID),
        in_specs=([pl.BlockSpec(blk(g), lambda j, b, c_ref: (j, c_ref[0] * RED_GRID + b, 0)) for g in gs]
                  + [pl.BlockSpec(blk(g), lambda j, b, c_ref: (j, b, 0)) for g in gs]),
        out_specs=[pl.BlockSpec(blk(g), lambda j, b, c_ref: (j, b, 0)) for g in gs])
    return pl.pallas_call(
        body, name="add_halves", grid_spec=grid_spec,
        out_shape=tuple(jax.ShapeDtypeStruct(r.shape, BF16) for r in rbs),
        compiler_params=_params(("parallel", "parallel")),
    )(core, *gs, *rbs)


def scatter_halves(pres, small):
    n = len(pres)

    def body(*refs):
        ins, s_ref = refs[:n], refs[n]
        outs, smalls_ref = refs[n + 1:2 * n + 1], refs[2 * n + 1]
        ssem, rsem, s_ssem, s_rsem, lsem = refs[2 * n + 2:]
        x, y, c, peers = _chip_peers()
        me = 2 * x + y
        dev = 4 * x + 2 * y + c
        local = [pltpu.make_async_copy(ins[i].at[me], outs[i].at[me], lsem.at[i]) for i in range(n)]
        local.append(pltpu.make_async_copy(s_ref, smalls_ref.at[dev], lsem.at[n]))
        for cp in local:
            cp.start()
        remote = []
        for k, peer in enumerate(peers):
            dst_chip = 2 * peer[0] + peer[1]
            for i in range(n):
                remote.append(pltpu.make_async_remote_copy(
                    src_ref=ins[i].at[dst_chip], dst_ref=outs[i].at[me], send_sem=ssem.at[n * k + i],
                    recv_sem=rsem.at[n * k + i], device_id=peer, device_id_type=MESH))
        for k in range(1, N_DEV):
            fx, fy, fc = (k >> 2) & 1, (k >> 1) & 1, k & 1
            peer = ((1 - x) if fx else x, (1 - y) if fy else y, (1 - c) if fc else c)
            remote.append(pltpu.make_async_remote_copy(
                src_ref=s_ref, dst_ref=smalls_ref.at[dev], send_sem=s_ssem.at[k - 1], recv_sem=s_rsem.at[k - 1],
                device_id=peer, device_id_type=MESH))
        for cp in remote:
            cp.start()
        for cp in remote:
            cp.wait()
        for cp in local:
            cp.wait()

    return pl.pallas_call(
        body, name="scatter_halves",
        out_shape=tuple([jax.ShapeDtypeStruct(a.shape, a.dtype) for a in pres]
                        + [jax.ShapeDtypeStruct((N_DEV,) + small.shape, F32)]),
        in_specs=[ANY] * (n + 1), out_specs=(ANY,) * (n + 1),
        scratch_shapes=_sems(3 * n) + _sems(N_DEV - 1) + [pltpu.SemaphoreType.DMA((n + 1,))],
    )(*pres, small)


def sum_parts(parts):
    n = len(parts)

    def body(*refs):
        for i in range(n):
            p_ref = refs[i]
            refs[n + i][...] = ((p_ref[0].astype(F32) + p_ref[1].astype(F32)) + p_ref[2].astype(F32)
                                ) + p_ref[3].astype(F32)

    def rows(p):
        return p.shape[1] // RED_GRID

    return pl.pallas_call(
        body, name="sum_parts",
        out_shape=tuple(jax.ShapeDtypeStruct(p.shape[1:], F32) for p in parts),
        grid=(RED_GRID,),
        in_specs=[pl.BlockSpec((N_CHIPS, rows(p), p.shape[2]), lambda b: (0, b, 0)) for p in parts],
        out_specs=tuple(pl.BlockSpec((rows(p), p.shape[2]), lambda b: (b, 0)) for p in parts),
        compiler_params=_params(("parallel",)),
    )(*parts)


def swap_halves(reds):
    n = len(reds)

    def body(*refs):
        ins, outs = refs[:n], refs[n:2 * n]
        ssem, rsem = refs[2 * n:]
        x, y, c = lax.axis_index("x"), lax.axis_index("y"), lax.axis_index("c")
        copies = [pltpu.make_async_remote_copy(
            src_ref=ins[i], dst_ref=outs[i], send_sem=ssem.at[i], recv_sem=rsem.at[i],
            device_id=(x, y, 1 - c), device_id_type=MESH) for i in range(n)]
        for cp in copies:
            cp.start()
        for cp in copies:
            cp.wait()

    return pl.pallas_call(
        body, name="swap_halves",
        out_shape=tuple(jax.ShapeDtypeStruct(r.shape, F32) for r in reds),
        in_specs=[ANY] * n, out_specs=(ANY,) * n, scratch_shapes=_sems(n),
    )(*reds)


def _adamw(w, g, m, v):
    m = ADAM_B1 * m + (1.0 - ADAM_B1) * g
    v = ADAM_B2 * v + (1.0 - ADAM_B2) * (g * g)
    m_hat = m / (1.0 - ADAM_B1 ** ADAM_STEP)
    v_hat = v / (1.0 - ADAM_B2 ** ADAM_STEP)
    delta = -ADAM_LR * (m_hat / (jnp.sqrt(v_hat) + ADAM_EPS) + ADAM_WD * w)
    return delta, m, v


def adamw_big(core, mine, theirs, ws, ms, vs):
    n = len(ws)
    per_half = RED_GRID // 2

    def body(c_ref, *refs):
        own = (pl.program_id(0) // per_half) == c_ref[0]
        for i in range(n):
            g = jnp.where(own, refs[i][...], refs[n + i][...])
            d, mn, vn = _adamw(refs[2 * n + i][...], g, refs[3 * n + i][...], refs[4 * n + i][...])
            refs[5 * n + i][...] = g
            refs[6 * n + i][...] = d
            refs[7 * n + i][...] = mn
            refs[8 * n + i][...] = vn

    def blk(w):
        return (w.shape[0] // RED_GRID, w.shape[1])

    halves = [pl.BlockSpec(blk(w), lambda b, c_ref: (b % per_half, 0)) for w in ws]
    whole = [pl.BlockSpec(blk(w), lambda b, c_ref: (b, 0)) for w in ws]
    shapes = [jax.ShapeDtypeStruct(w.shape, F32) for w in ws]
    grid_spec = pltpu.PrefetchScalarGridSpec(
        num_scalar_prefetch=1, grid=(RED_GRID,), in_specs=halves * 2 + whole * 3, out_specs=whole * 4)
    outs = pl.pallas_call(
        body, name="adamw_big", out_shape=tuple(shapes * 4), grid_spec=grid_spec,
        compiler_params=_params(("parallel",)),
    )(core, *mine, *theirs, *ws, *ms, *vs)
    return outs[:n], outs[n:2 * n], outs[2 * n:3 * n], outs[3 * n:]


def adamw_whole(g, w, m, v, name):
    def body(g_ref, w_ref, m_ref, v_ref, d_out, m_out, v_out):
        d, mn, vn = _adamw(w_ref[...], g_ref[...], m_ref[...], v_ref[...])
        d_out[...] = d
        m_out[...] = mn
        v_out[...] = vn

    shp = jax.ShapeDtypeStruct(g.shape, F32)
    return pl.pallas_call(body, name=name, out_shape=(shp,) * 3)(g, w, m, v)


def adamw_small(smalls, w, m, v):
    def body(s_ref, w_ref, m_ref, v_ref, g_out, d_out, m_out, v_out):
        g = s_ref[0]
        for k in range(1, N_DEV):
            g = g + s_ref[k]
        d, mn, vn = _adamw(w_ref[...], g, m_ref[...], v_ref[...])
        g_out[...] = g
        d_out[...] = d
        m_out[...] = mn
        v_out[...] = vn

    shp = jax.ShapeDtypeStruct((SMALL_ROWS, LANES), F32)
    return pl.pallas_call(body, name="adamw_small", out_shape=(shp,) * 4)(smalls, w, m, v)


def rms_prenorm(x, g):
    s = x.shape[0]
    tm = _blk(s, 512)

    def body(x_ref, g_ref, u_ref):
        xv = x_ref[...]
        r = lax.rsqrt(jnp.mean(xv * xv, axis=-1, keepdims=True) + EPS)
        u_ref[...] = (xv * r * g_ref[...]).astype(BF16)

    return pl.pallas_call(
        body, name="rms_prenorm", out_shape=jax.ShapeDtypeStruct(x.shape, BF16), grid=(s // tm,),
        in_specs=[pl.BlockSpec((tm, D_MODEL), lambda i: (i, 0)), _const_spec((1, D_MODEL))],
        out_specs=pl.BlockSpec((tm, D_MODEL), lambda i: (i, 0)), compiler_params=_params(("parallel",)),
    )(x, g)


def matmul_rows(a, w, out_dtype, name):
    s, k = a.shape
    n = w.shape[1]
    tm = _blk(s, 512)

    def body(a_ref, w_ref, o_ref):
        o_ref[...] = _mm(a_ref[...], w_ref[...]).astype(out_dtype)

    return pl.pallas_call(
        body, name=name, out_shape=jax.ShapeDtypeStruct((s, n), out_dtype), grid=(s // tm,),
        in_specs=[pl.BlockSpec((tm, k), lambda i: (i, 0)), _const_spec((k, n))],
        out_specs=pl.BlockSpec((tm, n), lambda i: (i, 0)), compiler_params=_params(("parallel",)),
    )(a, w)


def matmul_tn(a, b, name):
    s, m = a.shape
    n = b.shape[1]
    tk = _blk(s, 2048)
    tn = _blk(n, 512)

    def body(a_ref, b_ref, o_ref):
        @pl.when(pl.program_id(1) == 0)
        def _():
            o_ref[...] = jnp.zeros_like(o_ref)

        o_ref[...] += _mm_tn(a_ref[...], b_ref[...])

    return pl.pallas_call(
        body, name=name, out_shape=jax.ShapeDtypeStruct((m, n), F32), grid=(n // tn, s // tk),
        in_specs=[pl.BlockSpec((tk, m), lambda j, i: (i, 0)), pl.BlockSpec((tk, tn), lambda j, i: (i, j))],
        out_specs=pl.BlockSpec((m, tn), lambda j, i: (0, j)),
        compiler_params=_params(("parallel", "arbitrary")),
    )(a, b)


def conv_fwd(xbc, w, b):
    s = xbc.shape[0]
    tm = _blk(s, 256)

    def body(x_ref, t_ref, w_ref, b_ref, pre_ref, act_ref):
        i = pl.program_id(0)
        cur = x_ref[...]
        tail = jnp.where(i > 0, t_ref[...], 0.0)
        wv = w_ref[...]
        acc = cur * wv[3:4, :] + b_ref[...]
        head = cur[0:8, :] * wv[3:4, :] + b_ref[...]
        row8 = _iota((8, CONV_CH), 0)
        for sh in range(1, CONV_WIDTH):
            wk = wv[3 - sh:4 - sh, :]
            acc = acc + pltpu.roll(cur, sh, 0) * wk
            first = jnp.where(row8 < sh, pltpu.roll(tail, sh, 0), pltpu.roll(cur[0:8, :], sh, 0))
            head = head + first * wk
        pre_ref[...] = acc
        act_ref[...] = acc * _sigmoid(acc)
        pre_ref[0:8, :] = head
        act_ref[0:8, :] = head * _sigmoid(head)

    shp = jax.ShapeDtypeStruct(xbc.shape, F32)
    rows = pl.BlockSpec((tm, CONV_CH), lambda i: (i, 0))
    return pl.pallas_call(
        body, name="conv_fwd", out_shape=(shp, shp), grid=(s // tm,),
        in_specs=[rows, pl.BlockSpec((8, CONV_CH), lambda i: (jnp.maximum(i * (tm // 8) - 1, 0), 0)),
                  _const_spec((CONV_WIDTH, CONV_CH)), _const_spec((1, CONV_CH))],
        out_specs=(rows, rows), compiler_params=_params(("parallel",)),
    )(xbc, xbc, w, b)


def conv_bwd(xbc, pre, dact, w):
    s = xbc.shape[0]
    tm = _blk(s, 256)
    nb = s // tm

    def dsilu(p):
        sg = _sigmoid(p)
        return sg * (1.0 + p * (1.0 - sg))

    def body(x_ref, xt_ref, p_ref, pn_ref, d_ref, dn_ref, w_ref, dx_ref, dw_ref, db_ref):
        i = pl.program_id(0)

        @pl.when(i == 0)
        def _():
            dw_ref[...] = jnp.zeros_like(dw_ref)
            db_ref[...] = jnp.zeros_like(db_ref)

        wv = w_ref[...]
        dpre = d_ref[...] * dsilu(p_ref[...])
        dnext = jnp.where(i < nb - 1, dn_ref[...] * dsilu(pn_ref[...]), 0.0)
        cur = x_ref[...]
        tail = jnp.where(i > 0, xt_ref[...], 0.0)
        row8 = _iota((8, CONV_CH), 0)
        dx = dpre * wv[3:4, :]
        last = dpre[tm - 8:tm, :] * wv[3:4, :]
        db_ref[...] += jnp.sum(dpre, axis=0, keepdims=True)
        dws = [jnp.sum(dpre * cur, axis=0, keepdims=True)]
        for sh in range(1, CONV_WIDTH):
            wk = wv[3 - sh:4 - sh, :]
            dx = dx + pltpu.roll(dpre, tm - sh, 0) * wk
            nxt = jnp.where(row8 >= 8 - sh, pltpu.roll(dnext, 8 - sh, 0), pltpu.roll(dpre[tm - 8:tm, :], 8 - sh, 0))
            last = last + nxt * wk
            xs = pltpu.roll(cur, sh, 0)
            first = jnp.where(row8 < sh, pltpu.roll(tail, sh, 0), xs[0:8, :])
            dws.append(jnp.sum(dpre * xs, axis=0, keepdims=True)
                       + jnp.sum(dpre[0:8, :] * (first - xs[0:8, :]), axis=0, keepdims=True))
        dx_ref[...] = dx.astype(BF16)
        dx_ref[tm - 8:tm, :] = last.astype(BF16)
        for sh in range(CONV_WIDTH):
            dw_ref[3 - sh:4 - sh, :] += dws[sh]

    rows = pl.BlockSpec((tm, CONV_CH), lambda i: (i, 0))
    prev8 = pl.BlockSpec((8, CONV_CH), lambda i: (jnp.maximum(i * (tm // 8) - 1, 0), 0))
    next8 = pl.BlockSpec((8, CONV_CH), lambda i: (jnp.minimum((i + 1) * (tm // 8), s // 8 - 1), 0))
    return pl.pallas_call(
        body, name="conv_bwd",
        out_shape=(jax.ShapeDtypeStruct(xbc.shape, BF16), jax.ShapeDtypeStruct((8, CONV_CH), F32),
                   jax.ShapeDtypeStruct((1, CONV_CH), F32)),
        grid=(nb,),
        in_specs=[rows, prev8, rows, next8, rows, next8, _const_spec((CONV_WIDTH, CONV_CH))],
        out_specs=(rows, _const_spec((8, CONV_CH)), _const_spec((1, CONV_CH))),
        compiler_params=_params(("arbitrary",)),
    )(xbc, xbc, pre, pre, dact, dact, w)


def _pair_lanes(mat, j, lane):
    return jnp.where(lane < HEAD_DIM, mat[:, 2 * j:2 * j + 1], mat[:, 2 * j + 1:2 * j + 2])


def _ssd_chunk_prelude(sm, dtb, a_row, lane, sub):
    raw = sm + dtb
    head_lane = lane < N_HEADS
    dt = jnp.where(head_lane, _softplus(raw), 0.0)
    sig = jnp.where(head_lane, _sigmoid(raw), 0.0)
    tri = (lane <= sub).astype(F32)
    acs = _mm_exact(tri, dt * a_row)
    return dt, sig, acs, acs.T


GROUP_WIDTH = SSD_WIDTH // N_GROUPS
HEADS_PER_GROUP = N_HEADS // N_GROUPS


def _expand_group(mat, g, lane):
    return jnp.concatenate([_pair_lanes(mat, j, lane) for j in range(4 * g, 4 * g + 4)], axis=1)


def _head_sums(q, g):
    row = _iota((GROUP_WIDTH, LANES), 0)
    seg = (_iota((GROUP_WIDTH, LANES), 1) == HEADS_PER_GROUP * g + (row >> 6)).astype(BF16)
    hi = q.astype(BF16)
    lo = (q - hi.astype(F32)).astype(BF16)
    return _mm(hi, seg) + _mm(lo, seg)


def _rows_from_lanes(row512):
    return jnp.broadcast_to(row512, (LANES, GROUP_WIDTH)).T


def ssd_fwd(xc, small, dtb_row, a_row, dskip_lane):
    s = xc.shape[0]
    nc = s // CHUNK

    def body(xc_ref, sm_ref, dtb_ref, a_ref, dsk_ref, y_ref, hs_ref, h_scr):
        c = pl.program_id(0)

        @pl.when(c == 0)
        def _():
            h_scr[...] = jnp.zeros_like(h_scr)

        lane = _iota((CHUNK, LANES), 1)
        sub = _iota((CHUNK, LANES), 0)
        causal = lane <= sub
        dt, _, acs, acs_t = _ssd_chunk_prelude(sm_ref[...], dtb_ref[...], a_ref[...], lane, sub)
        for g in range(N_GROUPS):
            cols = slice(GROUP_WIDTH * g, GROUP_WIDTH * (g + 1))
            b_off = SSD_WIDTH + D_STATE * g
            c_off = SSD_WIDTH + N_GROUPS * D_STATE + D_STATE * g
            b_b = xc_ref[:, b_off:b_off + D_STATE].astype(BF16)
            c_b = xc_ref[:, c_off:c_off + D_STATE].astype(BF16)
            cb = _mm_nt(c_b, b_b)
            x_g = xc_ref[:, cols]
            acs_g = _expand_group(acs, g, lane)
            xdt_g = x_g * _expand_group(dt, g, lane)
            xdt_b = xdt_g.astype(BF16)
            heads = range(HEADS_PER_GROUP * g, HEADS_PER_GROUP * (g + 1))
            m_b = [(cb * jnp.exp(jnp.where(causal, acs[:, h:h + 1] - acs_t[h:h + 1, :], NEG_BIG))).astype(BF16)
                   for h in heads]
            yd = [_mm(m_b[k], xdt_b[:, LANES * (k // 2):LANES * (k // 2 + 1)]) for k in range(HEADS_PER_GROUP)]
            yd_g = jnp.concatenate([jnp.where(lane < HEAD_DIM, yd[2 * k], yd[2 * k + 1]) for k in range(4)], axis=1)
            h_g = h_scr[g]
            t_g = _mm_nt(c_b, h_g.astype(BF16))
            y_ref[:, cols] = yd_g + jnp.exp(acs_g) * t_g + dsk_ref[:, cols] * x_g
            hs_ref[0, g] = h_g
            last_g = acs_g[CHUNK - 1:CHUNK, :]
            w_b = (xdt_g * jnp.exp(last_g - acs_g)).astype(BF16)
            h_scr[g] = h_g * jnp.exp(_rows_from_lanes(last_g)) + _mm_tn(w_b, b_b)

    return pl.pallas_call(
        body, name="ssd_fwd",
        out_shape=(jax.ShapeDtypeStruct((s, SSD_WIDTH), F32),
                   jax.ShapeDtypeStruct((nc, N_GROUPS, GROUP_WIDTH, D_STATE), F32)),
        grid=(nc,),
        in_specs=[pl.BlockSpec((CHUNK, CONV_CH), lambda c: (c, 0)), pl.BlockSpec((CHUNK, LANES), lambda c: (c, 0)),
                  _const_spec((1, LANES)), _const_spec((1, LANES)), _const_spec((1, SSD_WIDTH))],
        out_specs=(pl.BlockSpec((CHUNK, SSD_WIDTH), lambda c: (c, 0)),
                   pl.BlockSpec((1, N_GROUPS, GROUP_WIDTH, D_STATE), lambda c: (c, 0, 0, 0))),
        scratch_shapes=[pltpu.VMEM((N_GROUPS, GROUP_WIDTH, D_STATE), F32)],
        compiler_params=_params(("arbitrary",)),
    )(xc, small, dtb_row, a_row, dskip_lane)


def ssd_bwd(xc, small, states, dy, dtb_row, a_row, dskip_lane):
    s = xc.shape[0]
    nc = s // CHUNK
    rev = lambda c: nc - 1 - c

    def body(xc_ref, sm_ref, hs_ref, dy_ref, dtb_ref, a_ref, dsk_ref,
             dxc_ref, ddt_ref, da_ref, ddtb_ref, ddsk_ref, dh_scr):
        c = pl.program_id(0)

        @pl.when(c == 0)
        def _():
            dh_scr[...] = jnp.zeros_like(dh_scr)
            da_ref[...] = jnp.zeros_like(da_ref)
            ddtb_ref[...] = jnp.zeros_like(ddtb_ref)
            ddsk_ref[...] = jnp.zeros_like(ddsk_ref)

        lane = _iota((CHUNK, LANES), 1)
        sub = _iota((CHUNK, LANES), 0)
        causal = lane <= sub
        upper = lane >= sub
        is_last = sub == CHUNK - 1
        a_row_v = a_ref[...]
        dt, sig, acs, acs_t = _ssd_chunk_prelude(sm_ref[...], dtb_ref[...], a_row_v, lane, sub)
        cd = jnp.exp(acs[CHUNK - 1:CHUNK, :])
        dacs_c = jnp.zeros((CHUNK, LANES), F32)
        dacs_r = jnp.zeros((LANES, CHUNK), F32)
        ddtx = jnp.zeros((CHUNK, LANES), F32)
        for g in range(N_GROUPS):
            cols = slice(GROUP_WIDTH * g, GROUP_WIDTH * (g + 1))
            b_off = SSD_WIDTH + D_STATE * g
            c_off = SSD_WIDTH + N_GROUPS * D_STATE + D_STATE * g
            b_b = xc_ref[:, b_off:b_off + D_STATE].astype(BF16)
            c_b = xc_ref[:, c_off:c_off + D_STATE].astype(BF16)
            cb = _mm_nt(c_b, b_b)
            cb_t = _mm_nt(b_b, c_b)
            x_g = xc_ref[:, cols]
            dy_g = dy_ref[:, cols]
            dt_g = _expand_group(dt, g, lane)
            acs_g = _expand_group(acs, g, lane)
            last_g = acs_g[CHUNK - 1:CHUNK, :]
            e_g = jnp.exp(acs_g)
            dte_g = jnp.exp(last_g - acs_g)
            xdt_g = x_g * dt_g
            xdt_b = xdt_g.astype(BF16)
            h_g = hs_ref[0, g]
            dh_g = dh_scr[g]
            h_b = h_g.astype(BF16)
            dh_b = dh_g.astype(BF16)
            heads = list(range(HEADS_PER_GROUP * g, HEADS_PER_GROUP * (g + 1)))
            segs = [acs[:, h:h + 1] - acs_t[h:h + 1, :] for h in heads]
            lms = [jnp.exp(jnp.where(causal, sg, NEG_BIG)) for sg in segs]
            mts = [(cb_t * jnp.exp(jnp.where(upper, -sg, NEG_BIG))).astype(BF16) for sg in segs]
            dyh = []
            for k in range(HEADS_PER_GROUP):
                blk = dy_g[:, LANES * (k // 2):LANES * (k // 2 + 1)]
                in_head = (lane < HEAD_DIM) if k % 2 == 0 else (lane >= HEAD_DIM)
                dyh.append(jnp.where(in_head, blk, 0.0).astype(BF16))
            dms = [_mm_nt(dyh[k], xdt_b[:, LANES * (k // 2):LANES * (k // 2 + 1)]) for k in range(HEADS_PER_GROUP)]
            dxs = [_mm(mts[k], dyh[k]) for k in range(HEADS_PER_GROUP)]
            dcb = jnp.zeros((CHUNK, CHUNK), F32)
            for k, h in enumerate(heads):
                gmat = dms[k] * (cb * lms[k])
                dacs_c = dacs_c + jnp.where(lane == h, jnp.sum(gmat, axis=1, keepdims=True), 0.0)
                dacs_r = dacs_r - jnp.where(sub == h, jnp.sum(gmat, axis=0, keepdims=True), 0.0)
                dcb = dcb + dms[k] * lms[k]
            dxdt_g = jnp.concatenate([dxs[2 * k] + dxs[2 * k + 1] for k in range(4)], axis=1)
            t_g = _mm_nt(c_b, h_b)
            dacs_c = dacs_c + _head_sums(dy_g * e_g * t_g, g)
            dt_b = (dy_g * e_g).astype(BF16)
            dc_acc = _mm(dt_b, h_b)
            dh_prev = _mm_tn(dt_b, c_b)
            dw_g = _mm_nt(b_b, dh_b)
            w_g = xdt_g * dte_g
            dxdt_g = dxdt_g + dw_g * dte_g
            db_acc = _mm(w_g.astype(BF16), dh_b)
            r2 = _head_sums(dw_g * w_g, g)
            dacs_c = dacs_c + jnp.where(is_last, jnp.sum(r2, axis=0, keepdims=True), 0.0) - r2
            q3 = jnp.sum(dh_g * h_g, axis=1, keepdims=True)
            for k, h in enumerate(heads):
                tot = jnp.sum(q3[HEAD_DIM * k:HEAD_DIM * (k + 1), :], keepdims=True) * cd[:, h:h + 1]
                dacs_c = dacs_c + jnp.where(is_last & (lane == h), tot, 0.0)
            dh_scr[g] = dh_prev + dh_g * jnp.exp(_rows_from_lanes(last_g))
            dxc_ref[:, cols] = dxdt_g * dt_g + dsk_ref[:, cols] * dy_g
            ddtx = ddtx + _head_sums(dxdt_g * x_g, g)
            ddsk_ref[:, cols] += jnp.sum(dy_g * x_g, axis=0, keepdims=True)
            dxc_ref[:, b_off:b_off + D_STATE] = db_acc + _mm(dcb.T.astype(BF16), c_b)
            dxc_ref[:, c_off:c_off + D_STATE] = dc_acc + _mm(dcb.astype(BF16), b_b)
        dacs = dacs_c + dacs_r.T
        dadt = _mm_exact((lane >= sub).astype(F32), dacs)
        ddt = dadt * a_row_v + ddtx
        ddt_raw = ddt * sig
        ddt_ref[...] = ddt_raw
        da_ref[...] += jnp.sum(dadt * dt, axis=0, keepdims=True)
        ddtb_ref[...] += jnp.sum(ddt_raw, axis=0, keepdims=True)

    return pl.pallas_call(
        body, name="ssd_bwd",
        out_shape=(jax.ShapeDtypeStruct((s, CONV_CH), F32), jax.ShapeDtypeStruct((s, LANES), F32),
                   jax.ShapeDtypeStruct((1, LANES), F32), jax.ShapeDtypeStruct((1, LANES), F32),
                   jax.ShapeDtypeStruct((1, SSD_WIDTH), F32)),
        grid=(nc,),
        in_specs=[pl.BlockSpec((CHUNK, CONV_CH), lambda c: (rev(c), 0)),
                  pl.BlockSpec((CHUNK, LANES), lambda c: (rev(c), 0)),
                  pl.BlockSpec((1, N_GROUPS, GROUP_WIDTH, D_STATE), lambda c: (rev(c), 0, 0, 0)),
                  pl.BlockSpec((CHUNK, SSD_WIDTH), lambda c: (rev(c), 0)),
                  _const_spec((1, LANES)), _const_spec((1, LANES)), _const_spec((1, SSD_WIDTH))],
        out_specs=(pl.BlockSpec((CHUNK, CONV_CH), lambda c: (rev(c), 0)),
                   pl.BlockSpec((CHUNK, LANES), lambda c: (rev(c), 0)),
                   _const_spec((1, LANES)), _const_spec((1, LANES)), _const_spec((1, SSD_WIDTH))),
        scratch_shapes=[pltpu.VMEM((N_GROUPS, GROUP_WIDTH, D_STATE), F32)],
        compiler_params=_params(("arbitrary",)),
    )(xc, small, states, dy, dtb_row, a_row, dskip_lane)


FORGET_BLOCK = 512


def forget_cumsum(small, fgb_row):
    s = small.shape[0]
    t = _blk(s, FORGET_BLOCK)
    nb = s // t

    def body(sm_ref, b_ref, cc_ref, carry):
        i = pl.program_id(0)

        @pl.when(i == 0)
        def _():
            carry[...] = jnp.zeros_like(carry)

        lane = _iota((t, LANES), 1)
        in_f = (lane >= N_HEADS) & (lane < 2 * N_HEADS)
        logf = jnp.where(in_f, -_softplus(-(sm_ref[...] + b_ref[...])), 0.0)
        tri = (_iota((t, t), 1) <= _iota((t, t), 0)).astype(F32)
        cum = _mm_exact(tri, logf) + carry[0:1, :]
        cc_ref[...] = cum
        carry[...] = jnp.broadcast_to(cum[t - 1:t, :], (8, LANES))

    return pl.pallas_call(
        body, name="forget_cumsum",
        out_shape=jax.ShapeDtypeStruct((s, LANES), F32),
        grid=(nb,),
        in_specs=[pl.BlockSpec((t, LANES), lambda i: (i, 0)), _const_spec((1, LANES))],
        out_specs=pl.BlockSpec((t, LANES), lambda i: (i, 0)),
        scratch_shapes=[pltpu.VMEM((8, LANES), F32)],
        compiler_params=_params(("arbitrary",)),
    )(small, fgb_row)


def forget_bwd(dc, small, ddt_raw, fgb_row):
    s = small.shape[0]
    t = _blk(s, FORGET_BLOCK)
    nb = s // t
    rev = lambda i: nb - 1 - i

    def body(dc_ref, sm_ref, ddt_ref, b_ref, ds_ref, dfb_ref, carry):
        i = pl.program_id(0)

        @pl.when(i == 0)
        def _():
            carry[...] = jnp.zeros_like(carry)
            dfb_ref[...] = jnp.zeros_like(dfb_ref)

        lane = _iota((t, LANES), 1)
        rows = dc_ref[...].T
        tri = (_iota((t, t), 1) <= _iota((t, t), 0)).astype(F32)
        rc = _mm_exact(rows, tri) + carry[:, 0:1]
        carry[...] = jnp.broadcast_to(rc[:, 0:1], (LANES, LANES))
        in_f = (lane >= N_HEADS) & (lane < 2 * N_HEADS)
        df = jnp.where(in_f, rc.T * _sigmoid(-(sm_ref[...] + b_ref[...])), 0.0)
        ds_ref[...] = (df + ddt_ref[...]).astype(BF16)
        dfb_ref[...] += jnp.sum(df, axis=0, keepdims=True)

    blk = pl.BlockSpec((t, LANES), lambda i: (rev(i), 0))
    return pl.pallas_call(
        body, name="forget_bwd",
        out_shape=(jax.ShapeDtypeStruct((s, LANES), BF16), jax.ShapeDtypeStruct((1, LANES), F32)),
        grid=(nb,),
        in_specs=[blk, blk, blk, _const_spec((1, LANES))],
        out_specs=(blk, _const_spec((1, LANES))),
        scratch_shapes=[pltpu.VMEM((LANES, LANES), F32)],
        compiler_params=_params(("arbitrary",)),
    )(dc, small, ddt_raw, fgb_row)


ATT_BLOCK = 512
ATT_SCALE = HEAD_DIM ** -0.5
AUG_A = HEAD_DIM
AUG_B = HEAD_DIM + 3


def _split3(c):
    hi = c.astype(BF16).astype(F32)
    r = c - hi
    mid = r.astype(BF16).astype(F32)
    return hi, mid, (r - mid).astype(BF16).astype(F32)


def _aug(lane, first, parts=None, value=1.0):
    if parts is None:
        return jnp.where((lane >= first) & (lane < first + 3), value, 0.0)
    return (jnp.where(lane == first, parts[0], 0.0) + jnp.where(lane == first + 1, parts[1], 0.0)
            + jnp.where(lane == first + 2, parts[2], 0.0))


def _pack_pair(a0, a1, lane):
    return jnp.where(lane < HEAD_DIM, a0, pltpu.roll(a1, HEAD_DIM, 1))


def proj_qkv_heads(u, w_q, w_k, w_v, cum):
    s = u.shape[0]
    tm = _blk(s, 256)

    def body(u_ref, wq_ref, wk_ref, wv_ref, c_ref, qa_ref, ka_ref, va_ref, nrm_ref):
        lane = _iota((tm, LANES), 1)
        lo = lane < HEAD_DIM
        uv = u_ref[...]
        qf = _mm(uv, wq_ref[...]) * ATT_SCALE
        kf = _mm(uv, wk_ref[...])
        vf = _mm(uv, wv_ref[...])
        cc = c_ref[...]
        ones_a = _aug(lane, AUG_A)
        ones_b = _aug(lane, AUG_B)
        sub8 = _iota((8, LANES), 0)
        nrm = jnp.zeros((8, LANES), F32)
        for h in range(N_HEADS):
            j, e = divmod(h, 2)

            def head(full):
                blk = full[:, LANES * j:LANES * (j + 1)]
                if e == 1:
                    blk = pltpu.roll(blk, HEAD_DIM, 1)
                return jnp.where(lo, blk, 0.0)

            parts = _split3(cc[:, N_HEADS + h:N_HEADS + h + 1])
            qh, kh = head(qf), head(kf)
            qa_ref[h] = (qh + _aug(lane, AUG_A, parts) + ones_b).astype(BF16)
            ka_ref[h] = (kh + ones_a - _aug(lane, AUG_B, parts)).astype(BF16)
            va_ref[h] = (head(vf) + ones_a).astype(BF16)
        seg = (_iota((ATT_WIDTH, LANES), 1) == (_iota((ATT_WIDTH, LANES), 0) >> 6)).astype(BF16)
        for r, val in enumerate((qf, kf)):
            sq = val * val
            hi = sq.astype(BF16)
            tot = _mm(hi, seg) + _mm((sq - hi.astype(F32)).astype(BF16), seg)
            nrm = nrm + jnp.where(sub8 == r, jnp.max(tot, axis=0, keepdims=True), 0.0)
        nrm_ref[0] = nrm

    shp = jax.ShapeDtypeStruct((N_HEADS, s, LANES), BF16)
    hspec = pl.BlockSpec((N_HEADS, tm, LANES), lambda i: (0, i, 0))
    wspec = _const_spec((D_MODEL, ATT_WIDTH))
    return pl.pallas_call(
        body, name="proj_qkv_heads",
        out_shape=(shp, shp, shp, jax.ShapeDtypeStruct((s // tm, 8, LANES), F32)), grid=(s // tm,),
        in_specs=[pl.BlockSpec((tm, D_MODEL), lambda i: (i, 0)), wspec, wspec, wspec,
                  pl.BlockSpec((tm, LANES), lambda i: (i, 0))],
        out_specs=(hspec, hspec, hspec, pl.BlockSpec((1, 8, LANES), lambda i: (i, 0, 0))),
        compiler_params=_params(("parallel",)),
    )(u, w_q, w_k, w_v, cum)


SKIP_BELOW = -110.0


def live_blocks(norms, cum, t):
    qn = jnp.sqrt(jnp.max(norms[:, 0, :N_HEADS], axis=0))
    kn = jnp.sqrt(jnp.max(norms[:, 1, :N_HEADS], axis=0))
    bound = 2.05 * qn * kn + 2.0
    c_first = cum[0::t, N_HEADS:2 * N_HEADS]
    c_last = cum[t - 1::t, N_HEADS:2 * N_HEADS]
    nq = c_first.shape[0]
    top = bound[None, None, :] + c_first[:, None, :] - c_last[None, :, :]
    below = jnp.arange(nq)[None, :] < jnp.arange(nq)[:, None]
    dead = below[:, :, None] & ~(top >= SKIP_BELOW)
    first = jnp.sum(dead, axis=1).astype(jnp.int32).T
    last_q = jnp.sum(first[:, None, :] <= jnp.arange(nq)[None, :, None], axis=2).astype(jnp.int32) - 1
    return first, last_q


def attention_fwd(first, qa, ka, va):
    s = qa.shape[1]
    t = _blk(s, ATT_BLOCK)
    nq = s // t

    def body(first_ref, qa_ref, ka_ref, va_ref, o_ref, qb_ref, m_scr, acc_scr, alpha_scr, p_scr, s_scr):
        qi = pl.program_id(1)
        starts = [first_ref[2 * pl.program_id(0) + e, qi] for e in range(2)]
        k0 = jnp.maximum(starts[0], starts[1])
        m_scr[...] = jnp.full_like(m_scr, NEG_BIG)
        acc_scr[...] = jnp.zeros_like(acc_scr)

        def kv_rows(kb):
            return pl.ds(pl.multiple_of(kb * t, t), t)

        def softmax_block(kb, masked, heads=(0, 1)):
            for e in heads:
                sc = _mm_nt(qa_ref[e], ka_ref[e, kv_rows(kb), :])
                if masked:
                    sc = jnp.where(_iota((t, t), 0) >= _iota((t, t), 1), sc, NEG_BIG)
                s_scr[e] = sc
                cmax = s_scr[e, :, 0:LANES]
                for c in range(1, t // LANES):
                    cmax = jnp.maximum(cmax, s_scr[e, :, LANES * c:LANES * (c + 1)])
                m_old = m_scr[e]
                m_new = jnp.maximum(m_old, jnp.max(cmax, axis=1, keepdims=True))
                alpha_scr[e] = jnp.exp(m_old - m_new)
                m_scr[e] = m_new
                for c in range(t // LANES):
                    cols = slice(LANES * c, LANES * (c + 1))
                    p_scr[e, :, cols] = jnp.exp(s_scr[e, :, cols] - m_new).astype(BF16)

        def accumulate(kb, heads=(0, 1)):
            for e in heads:
                acc_scr[e] = alpha_scr[e] * acc_scr[e] + _mm(p_scr[e], va_ref[e, kv_rows(kb), :])

        for e in range(2):
            def alone(kb, carry, e=e):
                softmax_block(kb, False, (e,))
                accumulate(kb, (e,))
                return carry

            lax.fori_loop(starts[e], k0, alone, 0)

        def loop_body(kb, carry):
            accumulate(kb - 1)
            softmax_block(kb, False)
            return carry

        @pl.when(qi > k0)
        def _():
            softmax_block(k0, False)

        lax.fori_loop(k0 + 1, qi, loop_body, 0)

        @pl.when(qi > k0)
        def _():
            accumulate(qi - 1)
            softmax_block(qi, True)

        @pl.when(qi == k0)
        def _():
            softmax_block(qi, True)

        accumulate(qi)

        lane = _iota((t, LANES), 1)
        outs = []
        for e in range(2):
            acc = acc_scr[e]
            l = acc[:, AUG_A:AUG_A + 1]
            outs.append(acc / l)
            lse = m_scr[e][:, 0:1] + jnp.log(l)
            q32 = qa_ref[e].astype(F32)
            c = q32[:, AUG_A:AUG_A + 1] + q32[:, AUG_A + 1:AUG_A + 2] + q32[:, AUG_A + 2:AUG_A + 3]
            qb = jnp.where(lane < HEAD_DIM, q32, 0.0) + _aug(lane, AUG_A, _split3(c - lse)) + _aug(lane, AUG_B)
            qb_ref[e] = qb.astype(BF16)
        o_ref[...] = _pack_pair(outs[0], outs[1], lane)

    grid_spec = pltpu.PrefetchScalarGridSpec(
        num_scalar_prefetch=1, grid=(N_PAIRS, nq),
        in_specs=[pl.BlockSpec((2, t, LANES), lambda j, qi, f: (j, qi, 0)),
                  pl.BlockSpec((2, s, LANES), lambda j, qi, f: (j, 0, 0)),
                  pl.BlockSpec((2, s, LANES), lambda j, qi, f: (j, 0, 0))],
        out_specs=[pl.BlockSpec((t, LANES), lambda j, qi, f: (qi, j)),
                   pl.BlockSpec((2, t, LANES), lambda j, qi, f: (j, qi, 0))],
        scratch_shapes=[pltpu.VMEM((2, t, LANES), F32), pltpu.VMEM((2, t, LANES), F32),
                        pltpu.VMEM((2, t, LANES), F32), pltpu.VMEM((2, t, t), BF16), pltpu.VMEM((2, t, t), F32)])
    return pl.pallas_call(
        body, name="attention_fwd", grid_spec=grid_spec,
        out_shape=(jax.ShapeDtypeStruct((s, ATT_WIDTH), F32), jax.ShapeDtypeStruct((N_HEADS, s, LANES), BF16)),
        compiler_params=_params(("parallel", "parallel")),
    )(first, qa, ka, va)


def attention_bwd(last_q, qb, ka, va, dob):
    s = qb.shape[1]
    t = _blk(s, ATT_BLOCK)
    nq = s // t

    def body(last_ref, qb_ref, dob_ref, ka_ref, va_ref, dq_ref, dk_ref, dv_ref, dc_ref, dq_scr, dk_scr, dv_scr):
        j, ki = pl.program_id(0), pl.program_id(1)

        @pl.when((j == 0) & (ki == 0))
        def _():
            dc_ref[...] = jnp.zeros_like(dc_ref)

        @pl.when(ki == 0)
        def _():
            dq_scr[...] = jnp.zeros_like(dq_scr)

        dk_scr[...] = jnp.zeros_like(dk_scr)
        dv_scr[...] = jnp.zeros_like(dv_scr)

        def q_step(qblk, masked, heads=(0, 1)):
            rows = pl.ds(pl.multiple_of(qblk * t, t), t)
            for e in heads:
                q = qb_ref[e, rows, :]
                do = dob_ref[e, rows, :]
                sc = _mm_nt(q, ka_ref[e])
                if masked:
                    sc = jnp.where(_iota((t, t), 0) >= _iota((t, t), 1), sc, NEG_BIG)
                p = jnp.exp(sc)
                ds_b = (p * _mm_nt(do, va_ref[e])).astype(BF16)
                dv_scr[e] += _mm_tn(p.astype(BF16), do)
                dk_scr[e] += _mm_tn(ds_b, q)
                dq_scr[e, rows, :] += _mm(ds_b, ka_ref[e])

        def loop_body(qblk, carry):
            q_step(qblk, False)
            return carry

        ends = [last_ref[2 * j + e, ki] + 1 for e in range(2)]
        both = jnp.minimum(ends[0], ends[1])
        q_step(ki, True)
        lax.fori_loop(ki + 1, both, loop_body, 0)
        for e in range(2):
            def alone(qblk, carry, e=e):
                q_step(qblk, False, (e,))
                return carry

            lax.fori_loop(both, ends[e], alone, 0)

        lane = _iota((t, LANES), 1)
        dk_ref[...] = _pack_pair(dk_scr[0], dk_scr[1], lane).astype(BF16)
        dv_ref[...] = _pack_pair(dv_scr[0], dv_scr[1], lane).astype(BF16)
        rows = pl.ds(pl.multiple_of(ki * t, t), t)
        dc_ref[rows, :] -= (jnp.where(lane == N_HEADS + 2 * j, dk_scr[0][:, AUG_B:AUG_B + 1], 0.0)
                            + jnp.where(lane == N_HEADS + 2 * j + 1, dk_scr[1][:, AUG_B:AUG_B + 1], 0.0))

        @pl.when(ki == nq - 1)
        def _():
            for blk in range(nq):
                rws = pl.ds(blk * t, t)
                d0 = dq_scr[0, rws, :]
                d1 = dq_scr[1, rws, :]
                dq_ref[rws, :] = (_pack_pair(d0, d1, lane) * ATT_SCALE).astype(BF16)
                dc_ref[rws, :] += (jnp.where(lane == N_HEADS + 2 * j, d0[:, AUG_A:AUG_A + 1], 0.0)
                                   + jnp.where(lane == N_HEADS + 2 * j + 1, d1[:, AUG_A:AUG_A + 1], 0.0))

    full = pl.BlockSpec((2, s, LANES), lambda j, ki, f: (j, 0, 0))
    blk = pl.BlockSpec((2, t, LANES), lambda j, ki, f: (j, ki, 0))
    pair = pl.BlockSpec((t, LANES), lambda j, ki, f: (ki, j))
    wide = jax.ShapeDtypeStruct((s, ATT_WIDTH), BF16)
    grid_spec = pltpu.PrefetchScalarGridSpec(
        num_scalar_prefetch=1, grid=(N_PAIRS, nq),
        in_specs=[full, full, blk, blk],
        out_specs=[pl.BlockSpec((s, LANES), lambda j, ki, f: (0, j)), pair, pair,
                   pl.BlockSpec((s, LANES), lambda j, ki, f: (0, 0))],
        scratch_shapes=[pltpu.VMEM((2, s, LANES), F32), pltpu.VMEM((2, t, LANES), F32),
                        pltpu.VMEM((2, t, LANES), F32)])
    return pl.pallas_call(
        body, name="attention_bwd", grid_spec=grid_spec,
        out_shape=(wide, wide, wide, jax.ShapeDtypeStruct((s, LANES), F32)),
        compiler_params=_params(("arbitrary", "arbitrary")),
    )(last_q, qb, dob, ka, va)


def _dsilu(z, sg):
    return sg * (1.0 + z * (1.0 - sg))


def post_mix(x, y, zs, o, za, p, tgt, ssd_g, att_g_lane, ple_g, fin_g, w_out, w_gate, w_proj):
    s = x.shape[0]
    tm = _blk(s, 128)
    half = SSD_WIDTH // N_GROUPS

    def rms_bwd(dy, yn, r):
        return r * (dy - yn * jnp.mean(dy * yn, axis=-1, keepdims=True))

    def colsum(a):
        return jnp.sum(a, axis=0, keepdims=True)

    def body(x_ref, y_ref, zs_ref, o_ref, za_ref, p_ref, t_ref, sg_ref, ag_ref, pg_ref, fg_ref,
             wo_ref, wg_ref, wp_ref,
             dh1_ref, dy_ref, dzs_ref, dob_ref, dza_ref, ycat_ref, dh1b_ref, n2b_ref, dglb_ref, dppb_ref, pb_ref,
             loss_ref, dfin_ref, dple_ref, dssd_ref, datt_ref):
        @pl.when(pl.program_id(0) == 0)
        def _():
            for r in (loss_ref, dfin_ref, dple_ref, dssd_ref, datt_ref):
                r[...] = jnp.zeros_like(r)

        lane = _iota((tm, LANES), 1)
        lo = lane < HEAD_DIM
        zs = zs_ref[...]
        sz = _sigmoid(zs)
        yv = y_ref[...]
        ys = yv * (zs * sz)
        yn, rg = [], []
        for g in range(N_GROUPS):
            seg = ys[:, half * g:half * (g + 1)]
            r = lax.rsqrt(jnp.mean(seg * seg, axis=-1, keepdims=True) + EPS)
            yn.append(seg * r)
            rg.append(r)
            ycat_ref[:, half * g:half * (g + 1)] = (yn[g] * sg_ref[:, half * g:half * (g + 1)]).astype(BF16)
        za = za_ref[...]
        sza = _sigmoid(za)
        silu_za = za * sza
        on, ra = [], []
        for jb in range(N_PAIRS):
            blk = o_ref[:, LANES * jb:LANES * (jb + 1)]
            sq = blk * blk
            ms0 = jnp.sum(jnp.where(lo, sq, 0.0), axis=1, keepdims=True) * (1.0 / HEAD_DIM)
            ms1 = jnp.sum(jnp.where(lo, 0.0, sq), axis=1, keepdims=True) * (1.0 / HEAD_DIM)
            r = jnp.where(lo, lax.rsqrt(ms0 + EPS), lax.rsqrt(ms1 + EPS))
            on.append(blk * r)
            ra.append(r)
            an = on[jb] * ag_ref[:, LANES * jb:LANES * (jb + 1)]
            ycat_ref[:, SSD_WIDTH + LANES * jb:SSD_WIDTH + LANES * (jb + 1)] = (
                an * silu_za[:, LANES * jb:LANES * (jb + 1)]).astype(BF16)
        h1 = x_ref[...] + _mm(ycat_ref[...], wo_ref[...])
        r2 = lax.rsqrt(jnp.mean(h1 * h1, axis=-1, keepdims=True) + EPS)
        n2h = h1 * r2
        n2_b = (n2h * pg_ref[...]).astype(BF16)
        gate = _sigmoid(_mm(n2_b, wg_ref[...]))
        p_b = p_ref[...].astype(BF16)
        pp = _mm(p_b, wp_ref[...])
        h2 = h1 + gate * pp
        r3 = lax.rsqrt(jnp.mean(h2 * h2, axis=-1, keepdims=True) + EPS)
        n3 = h2 * r3
        diff = n3 * fg_ref[...] - t_ref[...]
        sq = colsum(diff * diff)
        part = sq[:, 0:LANES]
        for jb in range(1, D_MODEL // LANES):
            part = part + sq[:, LANES * jb:LANES * (jb + 1)]
        loss_ref[...] += part * (0.5 / D_MODEL)
        dout = diff * (1.0 / D_MODEL)
        dfin_ref[...] += colsum(dout * n3)
        dh2 = rms_bwd(dout * fg_ref[...], n3, r3)
        dgl = dh2 * pp * gate * (1.0 - gate)
        dgl_b = dgl.astype(BF16)
        dn2 = _mm_nt(dgl_b, wg_ref[...])
        dple_ref[...] += colsum(dn2 * n2h)
        dh1 = dh2 + rms_bwd(dn2 * pg_ref[...], n2h, r2)
        dh1_b = dh1.astype(BF16)
        dycat = _mm_nt(dh1_b, wo_ref[...])
        dh1_ref[...] = dh1
        dh1b_ref[...] = dh1_b
        n2b_ref[...] = n2_b
        dglb_ref[...] = dgl_b
        dppb_ref[...] = (dh2 * gate).astype(BF16)
        pb_ref[...] = p_b
        for g in range(N_GROUPS):
            cols = slice(half * g, half * (g + 1))
            dys_g = dycat[:, cols]
            dssd_ref[:, cols] += colsum(dys_g * yn[g])
            dys = rms_bwd(dys_g * sg_ref[:, cols], yn[g], rg[g])
            dy_ref[:, cols] = dys * (zs[:, cols] * sz[:, cols])
            dzs_ref[:, cols] = (dys * yv[:, cols] * _dsilu(zs[:, cols], sz[:, cols])).astype(BF16)
        for jb in range(N_PAIRS):
            cols = slice(LANES * jb, LANES * (jb + 1))
            dya = dycat[:, SSD_WIDTH + LANES * jb:SSD_WIDTH + LANES * (jb + 1)]
            ag = ag_ref[:, cols]
            dan = dya * silu_za[:, cols]
            dza_ref[:, cols] = (dya * (on[jb] * ag) * _dsilu(za[:, cols], sza[:, cols])).astype(BF16)
            datt_ref[:, cols] += colsum(dan * on[jb])
            don = dan * ag
            q = don * on[jb]
            m0 = jnp.sum(jnp.where(lo, q, 0.0), axis=1, keepdims=True) * (1.0 / HEAD_DIM)
            m1 = jnp.sum(jnp.where(lo, 0.0, q), axis=1, keepdims=True) * (1.0 / HEAD_DIM)
            do2 = ra[jb] * (don - on[jb] * jnp.where(lo, m0, m1))
            prod = do2 * o_ref[:, cols]
            for e in range(2):
                delta = jnp.sum(jnp.where(lo, prod, 0.0) if e == 0 else jnp.where(lo, 0.0, prod),
                                axis=1, keepdims=True)
                base = jnp.where(lo, do2 if e == 0 else pltpu.roll(do2, HEAD_DIM, 1), 0.0)
                dob_ref[2 * jb + e] = (base - _aug(lane, AUG_A, _split3(delta))).astype(BF16)

    def rows(n, dtype=None):
        return pl.BlockSpec((tm, n), lambda i: (i, 0))

    def out(n, dtype):
        return jax.ShapeDtypeStruct((s, n), dtype)

    vec = _const_spec((1, D_MODEL))
    vshape = jax.ShapeDtypeStruct((1, D_MODEL), F32)
    return pl.pallas_call(
        body, name="post_mix",
        out_shape=(out(D_MODEL, F32), out(SSD_WIDTH, F32), out(SSD_WIDTH, BF16),
                   jax.ShapeDtypeStruct((N_HEADS, s, LANES), BF16),
                   out(ATT_WIDTH, BF16), out(D_INNER, BF16), out(D_MODEL, BF16), out(D_MODEL, BF16),
                   out(D_MODEL, BF16), out(D_MODEL, BF16), out(PLE_DIM, BF16),
                   jax.ShapeDtypeStruct((1, LANES), F32), vshape, vshape, vshape, vshape),
        grid=(s // tm,),
        in_specs=[rows(D_MODEL), rows(SSD_WIDTH), rows(SSD_WIDTH), rows(ATT_WIDTH), rows(ATT_WIDTH),
                  rows(PLE_DIM), rows(D_MODEL), vec, vec, vec, vec,
                  _const_spec((D_INNER, D_MODEL)), _const_spec((D_MODEL, D_MODEL)), _const_spec((PLE_DIM, D_MODEL))],
        out_specs=(rows(D_MODEL), rows(SSD_WIDTH), rows(SSD_WIDTH),
                   pl.BlockSpec((N_HEADS, tm, LANES), lambda i: (0, i, 0)), rows(ATT_WIDTH),
                   rows(D_INNER), rows(D_MODEL), rows(D_MODEL), rows(D_MODEL), rows(D_MODEL), rows(PLE_DIM),
                   _const_spec((1, LANES)), vec, vec, vec, vec),
        compiler_params=_params(("arbitrary",)),
    )(x, y, zs, o, za, p, tgt, ssd_g, att_g_lane, ple_g, fin_g, w_out, w_gate, w_proj)


def in_proj_bwd(dsegs, wsegs, x, g, dh1):
    s = x.shape[0]
    tm = _blk(s, 256)
    nseg = len(dsegs)

    def body(*refs):
        d_refs = refs[:nseg]
        w_refs = refs[nseg:2 * nseg]
        x_ref, g_ref, dh1_ref, dx_ref, dg_ref = refs[2 * nseg:]

        @pl.when(pl.program_id(0) == 0)
        def _():
            dg_ref[...] = jnp.zeros_like(dg_ref)

        du = _mm_nt(d_refs[0][...], w_refs[0][...])
        for k in range(1, nseg):
            du = du + _mm_nt(d_refs[k][...], w_refs[k][...])
        xv = x_ref[...]
        r = lax.rsqrt(jnp.mean(xv * xv, axis=-1, keepdims=True) + EPS)
        xh = xv * r
        dg_ref[...] += jnp.sum(du * xh, axis=0, keepdims=True)
        dxh = du * g_ref[...]
        dx_ref[...] = r * (dxh - xh * jnp.mean(dxh * xh, axis=-1, keepdims=True)) + dh1_ref[...]

    rows = lambda n: pl.BlockSpec((tm, n), lambda i: (i, 0))
    return pl.pallas_call(
        body, name="in_proj_bwd",
        out_shape=(jax.ShapeDtypeStruct((s, D_MODEL), F32), jax.ShapeDtypeStruct((1, D_MODEL), F32)),
        grid=(s // tm,),
        in_specs=([rows(d.shape[1]) for d in dsegs] + [_const_spec(w.shape) for w in wsegs]
                  + [rows(D_MODEL), _const_spec((1, D_MODEL)), rows(D_MODEL)]),
        out_specs=(rows(D_MODEL), _const_spec((1, D_MODEL))),
        compiler_params=_params(("arbitrary",)),
    )(*dsegs, *wsegs, x, g, dh1)


SMALL_NAMES = ("norm_g", "conv_b", "dt_bias", "a_log", "d_skip", "ssd_norm_g", "fg_bias", "att_norm_g",
               "ple_norm_g", "final_norm_g")
SMALL_SIZES = (1024, 1536, 16, 16, 16, 1024, 16, 64, 1024, 1024)
CONV_W_SIZE = CONV_WIDTH * CONV_CH


def _pack_small(vals):
    flat = jnp.concatenate([v.reshape(-1).astype(F32) for v in vals])
    flat = jnp.pad(flat, (0, SMALL_ROWS * LANES - flat.shape[0]))
    return flat.reshape(SMALL_ROWS, LANES)


def _unpack_small(pack, shapes):
    flat = pack.reshape(-1)
    out, off = [], 0
    for n, shp in zip(SMALL_SIZES, shapes):
        out.append(flat[off:off + n].reshape(shp))
        off += n
    return out


def _row128(v16, offset=0):
    return jnp.pad(v16.reshape(1, N_HEADS).astype(F32), ((0, 0), (offset, LANES - N_HEADS - offset)))


def local_step(x, p, tgt, w_in, w_out, w_gate, w_proj, conv_w, norm_g, conv_b, dt_bias, a_log, d_skip,
               ssd_norm_g, fg_bias, att_norm_g, ple_norm_g, final_norm_g):
    c0, c1, c2, c3, c4, c5, c6, c7 = 0, 1024, 2560, 2576, 3600, 4624, 5648, 6672
    w_zs, w_xbc, w_dt = w_in[:, c0:c1], w_in[:, c1:c2], w_in[:, c2:c3]
    w_za, w_q, w_k, w_v, w_f = w_in[:, c3:c4], w_in[:, c4:c5], w_in[:, c5:c6], w_in[:, c6:c7], w_in[:, c7:]
    w_small = jnp.concatenate([w_dt, w_f, jnp.zeros((D_MODEL, LANES - 2 * N_HEADS), BF16)], axis=1)

    dtb_row = _row128(dt_bias)
    a_row = _row128(-jnp.exp(a_log.astype(F32)))
    fgb_row = _row128(fg_bias, N_HEADS)
    dskip_lane = jnp.repeat(d_skip.astype(F32), HEAD_DIM).reshape(1, SSD_WIDTH)
    att_g_lane = jnp.tile(att_norm_g.astype(F32), N_HEADS).reshape(1, ATT_WIDTH)
    row = lambda v: v.reshape(1, -1).astype(F32)

    u = rms_prenorm(x, row(norm_g))
    zs = matmul_rows(u, w_zs, F32, "proj_z_ssd")
    xbc = matmul_rows(u, w_xbc, F32, "proj_xbc")
    za = matmul_rows(u, w_za, F32, "proj_z_att")
    small = matmul_rows(u, w_small, F32, "proj_small")
    cum = forget_cumsum(small, fgb_row)
    qa, ka, va, norms = proj_qkv_heads(u, w_q, w_k, w_v, cum)
    first, last_q = live_blocks(norms, cum, _blk(x.shape[0], ATT_BLOCK))
    pre, xc = conv_fwd(xbc, conv_w, row(conv_b))
    y, states = ssd_fwd(xc, small, dtb_row, a_row, dskip_lane)
    o, qb = attention_fwd(first, qa, ka, va)
    (dh1, dy, dzs, dob, dza, ycat, dh1_b, n2_b, dgl_b, dpp_b, p_b,
     loss_l, dfin, dple, dssd_g, datt_lane) = post_mix(
        x, y, zs, o, za, p, tgt, row(ssd_norm_g), att_g_lane, row(ple_norm_g), row(final_norm_g),
        w_out, w_gate, w_proj)
    dq, dk, dv, dc = attention_bwd(last_q, qb, ka, va, dob)
    dxc, ddt_raw, da, ddtb, ddsk_lane = ssd_bwd(xc, small, states, dy, dtb_row, a_row, dskip_lane)
    dsmall, dfgb = forget_bwd(dc, small, ddt_raw, fgb_row)
    dxbc, dconv_w8, dconv_b = conv_bwd(xbc, pre, dxc, conv_w)
    dsegs = [dzs, dxbc, dza, dq, dk, dv, dsmall]
    wsegs = [w_zs, w_xbc, w_za, w_q, w_k, w_v, w_small]
    dx, dnorm_g = in_proj_bwd(dsegs, wsegs, x, row(norm_g), dh1)
    dws = [matmul_tn(u, d, "dw_in_%d" % i) for i, d in enumerate(dsegs)]
    dw_in = jnp.concatenate([dws[0], dws[1], dws[6][:, :N_HEADS], dws[2], dws[3], dws[4], dws[5],
                             dws[6][:, N_HEADS:2 * N_HEADS]], axis=1)
    dw_out = matmul_tn(ycat, dh1_b, "dw_out")
    dw_gate = matmul_tn(n2_b, dgl_b, "dw_gate")
    dw_proj = matmul_tn(p_b, dpp_b, "dw_proj")
    small_grads = [
        dnorm_g, dconv_b, ddtb[0, :N_HEADS], (da * a_row)[0, :N_HEADS],
        ddsk_lane.reshape(N_HEADS, HEAD_DIM).sum(axis=1), dssd_g, dfgb[0, N_HEADS:2 * N_HEADS],
        datt_lane.reshape(N_HEADS, HEAD_DIM).sum(axis=0), dple, dfin]
    loss = jnp.sum(loss_l)
    return loss, dx, dw_in, dw_out, dw_gate, dw_proj, dconv_w8[:CONV_WIDTH], small_grads


def kernel(x, p, norm_g, w_in, conv_w, conv_b, dt_bias, a_log, d_skip, ssd_norm_g, fg_bias, att_norm_g, w_out, ple_norm_g, w_ple_gate, w_ple_proj, final_norm_g, loss_target, m_norm_g, m_w_in, m_conv_w, m_conv_b, m_dt_bias, m_a_log, m_d_skip, m_ssd_norm_g, m_fg_bias, m_att_norm_g, m_w_out, m_ple_norm_g, m_w_ple_gate, m_w_ple_proj, m_final_norm_g, v_norm_g, v_w_in, v_conv_w, v_conv_b, v_dt_bias, v_a_log, v_d_skip, v_ssd_norm_g, v_fg_bias, v_att_norm_g, v_w_out, v_ple_norm_g, v_w_ple_gate, v_w_ple_proj, v_final_norm_g):
    chip = 2 * lax.axis_index("x") + lax.axis_index("y")
    core = lax.axis_index("c")

    big_w = [w_in[0], w_out[0], w_ple_gate[0], w_ple_proj[0]]
    own = [a.astype(BF16) for a in big_w] + [conv_w[0]]
    gathered = gather_weights(own[:4], own[4])

    def joined(k, axis):
        return jnp.concatenate([jnp.where(chip == j, own[k], gathered[k][j]) for j in range(N_CHIPS)], axis=axis)

    w_in_f, w_out_f, w_gate_f, w_proj_f, conv_w_f = joined(0, 1), joined(1, 0), joined(2, 0), joined(3, 1), joined(4, 1)

    smalls_w = [norm_g, conv_b, dt_bias, a_log, d_skip, ssd_norm_g, fg_bias, att_norm_g, ple_norm_g, final_norm_g]
    loss_l, dx, dw_in, dw_out, dw_gate, dw_proj, dconv_w, small_grads = local_step(
        x[0], p[0, 0], loss_target[0], w_in_f, w_out_f, w_gate_f, w_proj_f, conv_w_f,
        *[a.reshape(-1) for a in smalls_w])
    loss = lax.psum(loss_l, ("x", "y", "c"))

    gs = [jnp.stack([dw_in[:, 1672 * j:1672 * (j + 1)] for j in range(N_CHIPS)]),
          dw_out.reshape(N_CHIPS, 512, D_MODEL), dw_gate.reshape(N_CHIPS, 256, D_MODEL),
          jnp.stack([dw_proj[:, 256 * j:256 * (j + 1)] for j in range(N_CHIPS)])]
    core1 = core.reshape(1).astype(jnp.int32)
    pres = add_halves(core1, gs, halves_to_sibling(gs))
    *parts, smalls = scatter_halves(pres, _pack_small(list(small_grads) + [dconv_w]))
    mine = sum_parts(parts)

    g_big, d_big, m_big, v_big = adamw_big(
        core1, mine, swap_halves(mine), big_w, [m_w_in[0], m_w_out[0], m_w_ple_gate[0], m_w_ple_proj[0]],
        [v_w_in[0], v_w_out[0], v_w_ple_gate[0], v_w_ple_proj[0]])
    smalls_m = [m_norm_g, m_conv_b, m_dt_bias, m_a_log, m_d_skip, m_ssd_norm_g, m_fg_bias, m_att_norm_g,
                m_ple_norm_g, m_final_norm_g]
    smalls_v = [v_norm_g, v_conv_b, v_dt_bias, v_a_log, v_d_skip, v_ssd_norm_g, v_fg_bias, v_att_norm_g,
                v_ple_norm_g, v_final_norm_g]
    g_sm, d_sm, m_sm, v_sm = adamw_small(smalls, _pack_small(smalls_w), _pack_small(smalls_m), _pack_small(smalls_v))
    n_small = sum(SMALL_SIZES)
    g_conv_full = g_sm.reshape(-1)[n_small:n_small + CONV_W_SIZE].reshape(CONV_WIDTH, CONV_CH)
    g_conv = lax.dynamic_slice_in_dim(g_conv_full, chip * 384, 384, axis=1)
    d_conv, m_conv, v_conv = adamw_whole(g_conv, conv_w[0], m_conv_w[0], v_conv_w[0], "adamw_conv")

    shapes = [a.shape for a in smalls_w]
    outs = []
    for big, conv, sm in ((g_big, g_conv, g_sm), (d_big, d_conv, d_sm), (m_big, m_conv, m_sm), (v_big, v_conv, v_sm)):
        b_in, b_out, b_gate, b_proj = [a[None] for a in big]
        s_norm, s_convb, s_dtb, s_alog, s_dsk, s_ssdg, s_fgb, s_attg, s_pleg, s_fin = _unpack_small(sm, shapes)
        outs.extend([s_norm, b_in, conv[None], s_convb, s_dtb, s_alog, s_dsk, s_ssdg, s_fgb, s_attg, b_out, s_pleg,
                     b_gate, b_proj, s_fin])
    return (loss, dx[None], *outs)
```

```python
import functools

import jax
import jax.numpy as jnp
from jax import lax
from jax.experimental import pallas as pl
from jax.experimental.pallas import tpu as pltpu

F32 = jnp.float32
BF16 = jnp.bfloat16

D_MODEL = 1024
SSD_WIDTH = 1024
ATT_WIDTH = 1024
N_HEADS = 16
HEAD_DIM = 64
N_GROUPS = 2
D_STATE = 128
CONV_CH = 1536
CONV_WIDTH = 4
CHUNK = 128
PLE_DIM = 256
D_INNER = 2048
EPS = 1e-6
IN_COLS = 6688
N_CHIPS = 4
N_DEV = 8
LANES = 128
N_PAIRS = 8

ADAM_LR = 0.001
ADAM_B1 = 0.9
ADAM_B2 = 0.999
ADAM_EPS = 1e-08
ADAM_WD = 0.01
ADAM_STEP = 10

SMALL_ROWS = 96

NEG_BIG = -1e30
VMEM_LIMIT = 56 * 1024 * 1024

MESH = pl.DeviceIdType.MESH
ANY = pl.BlockSpec(memory_space=pl.ANY)


def _mm(a, b):
    return jnp.dot(a, b, preferred_element_type=F32)


def _mm_nt(a, b):
    return lax.dot_general(a, b, (((1,), (1,)), ((), ())), preferred_element_type=F32)


def _mm_tn(a, b):
    return lax.dot_general(a, b, (((0,), (0,)), ((), ())), preferred_element_type=F32)


def _mm_exact(a, b):
    return jnp.dot(a, b, preferred_element_type=F32, precision=lax.Precision.HIGHEST)


def _softplus(x):
    return jnp.maximum(x, 0.0) + jnp.log1p(jnp.exp(-jnp.abs(x)))


def _sigmoid(x):
    return jax.nn.sigmoid(x)


def _iota(shape, dim):
    return lax.broadcasted_iota(jnp.int32, shape, dim)


def _params(sem=None):
    return pltpu.CompilerParams(dimension_semantics=sem, vmem_limit_bytes=VMEM_LIMIT)


def _blk(n, pref):
    return min(n, pref)


def _const_spec(shape):
    nd = len(shape)
    return pl.BlockSpec(shape, lambda *_: (0,) * nd)


def _chip_peers():
    x, y, c = lax.axis_index("x"), lax.axis_index("y"), lax.axis_index("c")
    return x, y, c, [(1 - x, y, c), (x, 1 - y, c), (1 - x, 1 - y, c)]


def _half(rows, c):
    h = rows // 2
    return pl.ds(pl.multiple_of(c * h, 8), h)


def _sems(n):
    return [pltpu.SemaphoreType.DMA((n,)), pltpu.SemaphoreType.DMA((n,))]


def gather_weights(shards, conv_s):
    n = len(shards)

    def body(*refs):
        ins, conv_in = refs[:n], refs[n]
        outs, conv_out = refs[n + 1:2 * n + 1], refs[2 * n + 1]
        ssem1, rsem1, ssem2, rsem2, c_ssem, c_rsem = refs[2 * n + 2:]
        x, y, c, peers = _chip_peers()
        me = 2 * x + y
        sibling = (x, y, 1 - c)
        first, small = [], []
        for k, peer in enumerate(peers):
            for i in range(n):
                h = _half(ins[i].shape[0], c)
                first.append(pltpu.make_async_remote_copy(
                    src_ref=ins[i].at[h], dst_ref=outs[i].at[me, h], send_sem=ssem1.at[n * k + i],
                    recv_sem=rsem1.at[n * k + i], device_id=peer, device_id_type=MESH))
            small.append(pltpu.make_async_remote_copy(
                src_ref=conv_in, dst_ref=conv_out.at[me], send_sem=c_ssem.at[k], recv_sem=c_rsem.at[k],
                device_id=peer, device_id_type=MESH))
        for cp in first + small:
            cp.start()
        passed = []
        for k, peer in enumerate(peers):
            chip = 2 * peer[0] + peer[1]
            for i in range(n):
                h = _half(ins[i].shape[0], c)
                first[n * k + i].wait_recv()
                fwd = pltpu.make_async_remote_copy(
                    src_ref=outs[i].at[chip, h], dst_ref=outs[i].at[chip, h], send_sem=ssem2.at[n * k + i],
                    recv_sem=rsem2.at[n * k + i], device_id=sibling, device_id_type=MESH)
                fwd.start()
                passed.append(fwd)
        for cp in passed:
            cp.wait_recv()
        for cp in first + passed:
            cp.wait_send()
        for cp in small:
            cp.wait()

    return pl.pallas_call(
        body, name="gather_weights",
        out_shape=tuple(jax.ShapeDtypeStruct((N_CHIPS,) + a.shape, a.dtype) for a in list(shards) + [conv_s]),
        in_specs=[ANY] * (n + 1), out_specs=(ANY,) * (n + 1),
        scratch_shapes=_sems(3 * n) + _sems(3 * n) + _sems(3),
    )(*shards, conv_s)


def halves_to_sibling(gs):
    n = len(gs)

    def body(*refs):
        ins, outs = refs[:n], refs[n:2 * n]
        ssem, rsem = refs[2 * n:]
        x, y, c = lax.axis_index("x"), lax.axis_index("y"), lax.axis_index("c")
        copies = []
        for i in range(n):
            for j in range(N_CHIPS):
                copies.append(pltpu.make_async_remote_copy(
                    src_ref=ins[i].at[j, _half(ins[i].shape[1], 1 - c)], dst_ref=outs[i].at[j],
                    send_sem=ssem.at[N_CHIPS * i + j], recv_sem=rsem.at[N_CHIPS * i + j],
                    device_id=(x, y, 1 - c), device_id_type=MESH))
        for cp in copies:
            cp.start()
        for cp in copies:
            cp.wait()

    return pl.pallas_call(
        body, name="halves_to_sibling",
        out_shape=tuple(jax.ShapeDtypeStruct((N_CHIPS, g.shape[1] // 2, g.shape[2]), F32) for g in gs),
        in_specs=[ANY] * n, out_specs=(ANY,) * n, scratch_shapes=_sems(N_CHIPS * n),
    )(*gs)


RED_GRID = 8


def add_halves(core, gs, rbs):
    n = len(gs)

    def body(c_ref, *refs):
        for i in range(n):
            refs[2 * n + i][...] = (refs[i][...] + refs[n + i][...]).astype(BF16)

    def blk(g):
        return (1, g.shape[1] // 2 // RED_GRID, g.shape[2])

    grid_spec = pltpu.PrefetchScalarGridSpec(
        num_scalar_prefetch=1, grid=(N_CHIPS, RED_GRID),
        in_specs=([pl.BlockSpec(blk(g), lambda j, b, c_ref: (j, c_ref[0] * RED_GRID + b, 0)) for g in gs]
                  + [pl.BlockSpec(blk(g), lambda j, b, c_ref: (j, b, 0)) for g in gs]),
        out_specs=[pl.BlockSpec(blk(g), lambda j, b, c_ref: (j, b, 0)) for g in gs])
    return pl.pallas_call(
        body, name="add_halves", grid_spec=grid_spec,
        out_shape=tuple(jax.ShapeDtypeStruct(r.shape, BF16) for r in rbs),
        compiler_params=_params(("parallel", "parallel")),
    )(core, *gs, *rbs)


def scatter_halves(pres, small):
    n = len(pres)

    def body(*refs):
        ins, s_ref = refs[:n], refs[n]
        outs, smalls_ref = refs[n + 1:2 * n + 1], refs[2 * n + 1]
        ssem, rsem, s_ssem, s_rsem, lsem = refs[2 * n + 2:]
        x, y, c, peers = _chip_peers()
        me = 2 * x + y
        dev = 4 * x + 2 * y + c
        local = [pltpu.make_async_copy(ins[i].at[me], outs[i].at[me], lsem.at[i]) for i in range(n)]
        local.append(pltpu.make_async_copy(s_ref, smalls_ref.at[dev], lsem.at[n]))
        for cp in local:
            cp.start()
        remote = []
        for k, peer in enumerate(peers):
            dst_chip = 2 * peer[0] + peer[1]
            for i in range(n):
                remote.append(pltpu.make_async_remote_copy(
                    src_ref=ins[i].at[dst_chip], dst_ref=outs[i].at[me], send_sem=ssem.at[n * k + i],
                    recv_sem=rsem.at[n * k + i], device_id=peer, device_id_type=MESH))
        for k in range(1, N_DEV):
            fx, fy, fc = (k >> 2) & 1, (k >> 1) & 1, k & 1
            peer = ((1 - x) if fx else x, (1 - y) if fy else y, (1 - c) if fc else c)
            remote.append(pltpu.make_async_remote_copy(
                src_ref=s_ref, dst_ref=smalls_ref.at[dev], send_sem=s_ssem.at[k - 1], recv_sem=s_rsem.at[k - 1],
                device_id=peer, device_id_type=MESH))
        for cp in remote:
            cp.start()
        for cp in remote:
            cp.wait()
        for cp in local:
            cp.wait()

    return pl.pallas_call(
        body, name="scatter_halves",
        out_shape=tuple([jax.ShapeDtypeStruct(a.shape, a.dtype) for a in pres]
                        + [jax.ShapeDtypeStruct((N_DEV,) + small.shape, F32)]),
        in_specs=[ANY] * (n + 1), out_specs=(ANY,) * (n + 1),
        scratch_shapes=_sems(3 * n) + _sems(N_DEV - 1) + [pltpu.SemaphoreType.DMA((n + 1,))],
    )(*pres, small)


def sum_parts(parts):
    n = len(parts)

    def body(*refs):
        for i in range(n):
            p_ref = refs[i]
            refs[n + i][...] = ((p_ref[0].astype(F32) + p_ref[1].astype(F32)) + p_ref[2].astype(F32)
                                ) + p_ref[3].astype(F32)

    def rows(p):
        return p.shape[1] // RED_GRID

    return pl.pallas_call(
        body, name="sum_parts",
        out_shape=tuple(jax.ShapeDtypeStruct(p.shape[1:], F32) for p in parts),
        grid=(RED_GRID,),
        in_specs=[pl.BlockSpec((N_CHIPS, rows(p), p.shape[2]), lambda b: (0, b, 0)) for p in parts],
        out_specs=tuple(pl.BlockSpec((rows(p), p.shape[2]), lambda b: (b, 0)) for p in parts),
        compiler_params=_params(("parallel",)),
    )(*parts)


def swap_halves(reds):
    n = len(reds)

    def body(*refs):
        ins, outs = refs[:n], refs[n:2 * n]
        ssem, rsem = refs[2 * n:]
        x, y, c = lax.axis_index("x"), lax.axis_index("y"), lax.axis_index("c")
        copies = [pltpu.make_async_remote_copy(
            src_ref=ins[i], dst_ref=outs[i], send_sem=ssem.at[i], recv_sem=rsem.at[i],
            device_id=(x, y, 1 - c), device_id_type=MESH) for i in range(n)]
        for cp in copies:
            cp.start()
        for cp in copies:
            cp.wait()

    return pl.pallas_call(
        body, name="swap_halves",
        out_shape=tuple(jax.ShapeDtypeStruct(r.shape, F32) for r in reds),
        in_specs=[ANY] * n, out_specs=(ANY,) * n, scratch_shapes=_sems(n),
    )(*reds)


def _adamw(w, g, m, v):
    m = ADAM_B1 * m + (1.0 - ADAM_B1) * g
    v = ADAM_B2 * v + (1.0 - ADAM_B2) * (g * g)
    m_hat = m / (1.0 - ADAM_B1 ** ADAM_STEP)
    v_hat = v / (1.0 - ADAM_B2 ** ADAM_STEP)
    delta = -ADAM_LR * (m_hat / (jnp.sqrt(v_hat) + ADAM_EPS) + ADAM_WD * w)
    return delta, m, v


def adamw_big(core, mine, theirs, ws, ms, vs):
    n = len(ws)
    per_half = RED_GRID // 2

    def body(c_ref, *refs):
        own = (pl.program_id(0) // per_half) == c_ref[0]
        for i in range(n):
            g = jnp.where(own, refs[i][...], refs[n + i][...])
            d, mn, vn = _adamw(refs[2 * n + i][...], g, refs[3 * n + i][...], refs[4 * n + i][...])
            refs[5 * n + i][...] = g
            refs[6 * n + i][...] = d
            refs[7 * n + i][...] = mn
            refs[8 * n + i][...] = vn

    def blk(w):
        return (w.shape[0] // RED_GRID, w.shape[1])

    halves = [pl.BlockSpec(blk(w), lambda b, c_ref: (b % per_half, 0)) for w in ws]
    whole = [pl.BlockSpec(blk(w), lambda b, c_ref: (b, 0)) for w in ws]
    shapes = [jax.ShapeDtypeStruct(w.shape, F32) for w in ws]
    grid_spec = pltpu.PrefetchScalarGridSpec(
        num_scalar_prefetch=1, grid=(RED_GRID,), in_specs=halves * 2 + whole * 3, out_specs=whole * 4)
    outs = pl.pallas_call(
        body, name="adamw_big", out_shape=tuple(shapes * 4), grid_spec=grid_spec,
        compiler_params=_params(("parallel",)),
    )(core, *mine, *theirs, *ws, *ms, *vs)
    return outs[:n], outs[n:2 * n], outs[2 * n:3 * n], outs[3 * n:]


def adamw_whole(g, w, m, v, name):
    def body(g_ref, w_ref, m_ref, v_ref, d_out, m_out, v_out):
        d, mn, vn = _adamw(w_ref[...], g_ref[...], m_ref[...], v_ref[...])
        d_out[...] = d
        m_out[...] = mn
        v_out[...] = vn

    shp = jax.ShapeDtypeStruct(g.shape, F32)
    return pl.pallas_call(body, name=name, out_shape=(shp,) * 3)(g, w, m, v)


def adamw_small(smalls, w, m, v):
    def body(s_ref, w_ref, m_ref, v_ref, g_out, d_out, m_out, v_out):
        g = s_ref[0]
        for k in range(1, N_DEV):
            g = g + s_ref[k]
        d, mn, vn = _adamw(w_ref[...], g, m_ref[...], v_ref[...])
        g_out[...] = g
        d_out[...] = d
        m_out[...] = mn
        v_out[...] = vn

    shp = jax.ShapeDtypeStruct((SMALL_ROWS, LANES), F32)
    return pl.pallas_call(body, name="adamw_small", out_shape=(shp,) * 4)(smalls, w, m, v)


def rms_prenorm(x, g):
    s = x.shape[0]
    tm = _blk(s, 512)

    def body(x_ref, g_ref, u_ref):
        xv = x_ref[...]
        r = lax.rsqrt(jnp.mean(xv * xv, axis=-1, keepdims=True) + EPS)
        u_ref[...] = (xv * r * g_ref[...]).astype(BF16)

    return pl.pallas_call(
        body, name="rms_prenorm", out_shape=jax.ShapeDtypeStruct(x.shape, BF16), grid=(s // tm,),
        in_specs=[pl.BlockSpec((tm, D_MODEL), lambda i: (i, 0)), _const_spec((1, D_MODEL))],
        out_specs=pl.BlockSpec((tm, D_MODEL), lambda i: (i, 0)), compiler_params=_params(("parallel",)),
    )(x, g)


def matmul_rows(a, w, out_dtype, name):
    s, k = a.shape
    n = w.shape[1]
    tm = _blk(s, 512)

    def body(a_ref, w_ref, o_ref):
        o_ref[...] = _mm(a_ref[...], w_ref[...]).astype(out_dtype)

    return pl.pallas_call(
        body, name=name, out_shape=jax.ShapeDtypeStruct((s, n), out_dtype), grid=(s // tm,),
        in_specs=[pl.BlockSpec((tm, k), lambda i: (i, 0)), _const_spec((k, n))],
        out_specs=pl.BlockSpec((tm, n), lambda i: (i, 0)), compiler_params=_params(("parallel",)),
    )(a, w)


def matmul_tn(a, b, name):
    s, m = a.shape
    n = b.shape[1]
    tk = _blk(s, 2048)
    tn = _blk(n, 512)

    def body(a_ref, b_ref, o_ref):
        @pl.when(pl.program_id(1) == 0)
        def _():
            o_ref[...] = jnp.zeros_like(o_ref)

        o_ref[...] += _mm_tn(a_ref[...], b_ref[...])

    return pl.pallas_call(
        body, name=name, out_shape=jax.ShapeDtypeStruct((m, n), F32), grid=(n // tn, s // tk),
        in_specs=[pl.BlockSpec((tk, m), lambda j, i: (i, 0)), pl.BlockSpec((tk, tn), lambda j, i: (i, j))],
        out_specs=pl.BlockSpec((m, tn), lambda j, i: (0, j)),
        compiler_params=_params(("parallel", "arbitrary")),
    )(a, b)


def conv_fwd(xbc, w, b):
    s = xbc.shape[0]
    tm = _blk(s, 256)

    def body(x_ref, t_ref, w_ref, b_ref, pre_ref, act_ref):
        i = pl.program_id(0)
        cur = x_ref[...]
        tail = jnp.where(i > 0, t_ref[...], 0.0)
        wv = w_ref[...]
        acc = cur * wv[3:4, :] + b_ref[...]
        head = cur[0:8, :] * wv[3:4, :] + b_ref[...]
        row8 = _iota((8, CONV_CH), 0)
        for sh in range(1, CONV_WIDTH):
            wk = wv[3 - sh:4 - sh, :]
            acc = acc + pltpu.roll(cur, sh, 0) * wk
            first = jnp.where(row8 < sh, pltpu.roll(tail, sh, 0), pltpu.roll(cur[0:8, :], sh, 0))
            head = head + first * wk
        pre_ref[...] = acc
        act_ref[...] = acc * _sigmoid(acc)
        pre_ref[0:8, :] = head
        act_ref[0:8, :] = head * _sigmoid(head)

    shp = jax.ShapeDtypeStruct(xbc.shape, F32)
    rows = pl.BlockSpec((tm, CONV_CH), lambda i: (i, 0))
    return pl.pallas_call(
        body, name="conv_fwd", out_shape=(shp, shp), grid=(s // tm,),
        in_specs=[rows, pl.BlockSpec((8, CONV_CH), lambda i: (jnp.maximum(i * (tm // 8) - 1, 0), 0)),
                  _const_spec((CONV_WIDTH, CONV_CH)), _const_spec((1, CONV_CH))],
        out_specs=(rows, rows), compiler_params=_params(("parallel",)),
    )(xbc, xbc, w, b)


def conv_bwd(xbc, pre, dact, w):
    s = xbc.shape[0]
    tm = _blk(s, 256)
    nb = s // tm

    def dsilu(p):
        sg = _sigmoid(p)
        return sg * (1.0 + p * (1.0 - sg))

    def body(x_ref, xt_ref, p_ref, pn_ref, d_ref, dn_ref, w_ref, dx_ref, dw_ref, db_ref):
        i = pl.program_id(0)

        @pl.when(i == 0)
        def _():
            dw_ref[...] = jnp.zeros_like(dw_ref)
            db_ref[...] = jnp.zeros_like(db_ref)

        wv = w_ref[...]
        dpre = d_ref[...] * dsilu(p_ref[...])
        dnext = jnp.where(i < nb - 1, dn_ref[...] * dsilu(pn_ref[...]), 0.0)
        cur = x_ref[...]
        tail = jnp.where(i > 0, xt_ref[...], 0.0)
        row8 = _iota((8, CONV_CH), 0)
        dx = dpre * wv[3:4, :]
        last = dpre[tm - 8:tm, :] * wv[3:4, :]
        db_ref[...] += jnp.sum(dpre, axis=0, keepdims=True)
        dws = [jnp.sum(dpre * cur, axis=0, keepdims=True)]
        for sh in range(1, CONV_WIDTH):
            wk = wv[3 - sh:4 - sh, :]
            dx = dx + pltpu.roll(dpre, tm - sh, 0) * wk
            nxt = jnp.where(row8 >= 8 - sh, pltpu.roll(dnext, 8 - sh, 0), pltpu.roll(dpre[tm - 8:tm, :], 8 - sh, 0))
            last = last + nxt * wk
            xs = pltpu.roll(cur, sh, 0)
            first = jnp.where(row8 < sh, pltpu.roll(tail, sh, 0), xs[0:8, :])
            dws.append(jnp.sum(dpre * xs, axis=0, keepdims=True)
                       + jnp.sum(dpre[0:8, :] * (first - xs[0:8, :]), axis=0, keepdims=True))
        dx_ref[...] = dx.astype(BF16)
        dx_ref[tm - 8:tm, :] = last.astype(BF16)
        for sh in range(CONV_WIDTH):
            dw_ref[3 - sh:4 - sh, :] += dws[sh]

    rows = pl.BlockSpec((tm, CONV_CH), lambda i: (i, 0))
    prev8 = pl.BlockSpec((8, CONV_CH), lambda i: (jnp.maximum(i * (tm // 8) - 1, 0), 0))
    next8 = pl.BlockSpec((8, CONV_CH), lambda i: (jnp.minimum((i + 1) * (tm // 8), s // 8 - 1), 0))
    return pl.pallas_call(
        body, name="conv_bwd",
        out_shape=(jax.ShapeDtypeStruct(xbc.shape, BF16), jax.ShapeDtypeStruct((8, CONV_CH), F32),
                   jax.ShapeDtypeStruct((1, CONV_CH), F32)),
        grid=(nb,),
        in_specs=[rows, prev8, rows, next8, rows, next8, _const_spec((CONV_WIDTH, CONV_CH))],
        out_specs=(rows, _const_spec((8, CONV_CH)), _const_spec((1, CONV_CH))),
        compiler_params=_params(("arbitrary",)),
    )(xbc, xbc, pre, pre, dact, dact, w)


def _pair_lanes(mat, j, lane):
    return jnp.where(lane < HEAD_DIM, mat[:, 2 * j:2 * j + 1], mat[:, 2 * j + 1:2 * j + 2])


def _ssd_chunk_prelude(sm, dtb, a_row, lane, sub):
    raw = sm + dtb
    head_lane = lane < N_HEADS
    dt = jnp.where(head_lane, _softplus(raw), 0.0)
    sig = jnp.where(head_lane, _sigmoid(raw), 0.0)
    tri = (lane <= sub).astype(F32)
    acs = _mm_exact(tri, dt * a_row)
    return dt, sig, acs, acs.T


GROUP_WIDTH = SSD_WIDTH // N_GROUPS
HEADS_PER_GROUP = N_HEADS // N_GROUPS


def _expand_group(mat, g, lane):
    return jnp.concatenate([_pair_lanes(mat, j, lane) for j in range(4 * g, 4 * g + 4)], axis=1)


def _head_sums(q, g):
    row = _iota((GROUP_WIDTH, LANES), 0)
    seg = (_iota((GROUP_WIDTH, LANES), 1) == HEADS_PER_GROUP * g + (row >> 6)).astype(BF16)
    hi = q.astype(BF16)
    lo = (q - hi.astype(F32)).astype(BF16)
    return _mm(hi, seg) + _mm(lo, seg)


def _rows_from_lanes(row512):
    return jnp.broadcast_to(row512, (LANES, GROUP_WIDTH)).T


def ssd_fwd(xc, small, dtb_row, a_row, dskip_lane):
    s = xc.shape[0]
    nc = s // CHUNK

    def body(xc_ref, sm_ref, dtb_ref, a_ref, dsk_ref, y_ref, hs_ref, h_scr):
        c = pl.program_id(0)

        @pl.when(c == 0)
        def _():
            h_scr[...] = jnp.zeros_like(h_scr)

        lane = _iota((CHUNK, LANES), 1)
        sub = _iota((CHUNK, LANES), 0)
        causal = lane <= sub
        dt, _, acs, acs_t = _ssd_chunk_prelude(sm_ref[...], dtb_ref[...], a_ref[...], lane, sub)
        for g in range(N_GROUPS):
            cols = slice(GROUP_WIDTH * g, GROUP_WIDTH * (g + 1))
            b_off = SSD_WIDTH + D_STATE * g
            c_off = SSD_WIDTH + N_GROUPS * D_STATE + D_STATE * g
            b_b = xc_ref[:, b_off:b_off + D_STATE].astype(BF16)
            c_b = xc_ref[:, c_off:c_off + D_STATE].astype(BF16)
            cb = _mm_nt(c_b, b_b)
            x_g = xc_ref[:, cols]
            acs_g = _expand_group(acs, g, lane)
            xdt_g = x_g * _expand_group(dt, g, lane)
            xdt_b = xdt_g.astype(BF16)
            heads = range(HEADS_PER_GROUP * g, HEADS_PER_GROUP * (g + 1))
            m_b = [(cb * jnp.exp(jnp.where(causal, acs[:, h:h + 1] - acs_t[h:h + 1, :], NEG_BIG))).astype(BF16)
                   for h in heads]
            yd = [_mm(m_b[k], xdt_b[:, LANES * (k // 2):LANES * (k // 2 + 1)]) for k in range(HEADS_PER_GROUP)]
            yd_g = jnp.concatenate([jnp.where(lane < HEAD_DIM, yd[2 * k], yd[2 * k + 1]) for k in range(4)], axis=1)
            h_g = h_scr[g]
            t_g = _mm_nt(c_b, h_g.astype(BF16))
            y_ref[:, cols] = yd_g + jnp.exp(acs_g) * t_g + dsk_ref[:, cols] * x_g
            hs_ref[0, g] = h_g
            last_g = acs_g[CHUNK - 1:CHUNK, :]
            w_b = (xdt_g * jnp.exp(last_g - acs_g)).astype(BF16)
            h_scr[g] = h_g * jnp.exp(_rows_from_lanes(last_g)) + _mm_tn(w_b, b_b)

    return pl.pallas_call(
        body, name="ssd_fwd",
        out_shape=(jax.ShapeDtypeStruct((s, SSD_WIDTH), F32),
                   jax.ShapeDtypeStruct((nc, N_GROUPS, GROUP_WIDTH, D_STATE), F32)),
        grid=(nc,),
        in_specs=[pl.BlockSpec((CHUNK, CONV_CH), lambda c: (c, 0)), pl.BlockSpec((CHUNK, LANES), lambda c: (c, 0)),
                  _const_spec((1, LANES)), _const_spec((1, LANES)), _const_spec((1, SSD_WIDTH))],
        out_specs=(pl.BlockSpec((CHUNK, SSD_WIDTH), lambda c: (c, 0)),
                   pl.BlockSpec((1, N_GROUPS, GROUP_WIDTH, D_STATE), lambda c: (c, 0, 0, 0))),
        scratch_shapes=[pltpu.VMEM((N_GROUPS, GROUP_WIDTH, D_STATE), F32)],
        compiler_params=_params(("arbitrary",)),
    )(xc, small, dtb_row, a_row, dskip_lane)


def ssd_bwd(xc, small, states, dy, dtb_row, a_row, dskip_lane):
    s = xc.shape[0]
    nc = s // CHUNK
    rev = lambda c: nc - 1 - c

    def body(xc_ref, sm_ref, hs_ref, dy_ref, dtb_ref, a_ref, dsk_ref,
             dxc_ref, ddt_ref, da_ref, ddtb_ref, ddsk_ref, dh_scr):
        c = pl.program_id(0)

        @pl.when(c == 0)
        def _():
            dh_scr[...] = jnp.zeros_like(dh_scr)
            da_ref[...] = jnp.zeros_like(da_ref)
            ddtb_ref[...] = jnp.zeros_like(ddtb_ref)
            ddsk_ref[...] = jnp.zeros_like(ddsk_ref)

        lane = _iota((CHUNK, LANES), 1)
        sub = _iota((CHUNK, LANES), 0)
        causal = lane <= sub
        upper = lane >= sub
        is_last = sub == CHUNK - 1
        a_row_v = a_ref[...]
        dt, sig, acs, acs_t = _ssd_chunk_prelude(sm_ref[...], dtb_ref[...], a_row_v, lane, sub)
        cd = jnp.exp(acs[CHUNK - 1:CHUNK, :])
        dacs_c = jnp.zeros((CHUNK, LANES), F32)
        dacs_r = jnp.zeros((LANES, CHUNK), F32)
        ddtx = jnp.zeros((CHUNK, LANES), F32)
        for g in range(N_GROUPS):
            cols = slice(GROUP_WIDTH * g, GROUP_WIDTH * (g + 1))
            b_off = SSD_WIDTH + D_STATE * g
            c_off = SSD_WIDTH + N_GROUPS * D_STATE + D_STATE * g
            b_b = xc_ref[:, b_off:b_off + D_STATE].astype(BF16)
            c_b = xc_ref[:, c_off:c_off + D_STATE].astype(BF16)
            cb = _mm_nt(c_b, b_b)
            cb_t = _mm_nt(b_b, c_b)
            x_g = xc_ref[:, cols]
            dy_g = dy_ref[:, cols]
            dt_g = _expand_group(dt, g, lane)
            acs_g = _expand_group(acs, g, lane)
            last_g = acs_g[CHUNK - 1:CHUNK, :]
            e_g = jnp.exp(acs_g)
            dte_g = jnp.exp(last_g - acs_g)
            xdt_g = x_g * dt_g
            xdt_b = xdt_g.astype(BF16)
            h_g = hs_ref[0, g]
            dh_g = dh_scr[g]
            h_b = h_g.astype(BF16)
            dh_b = dh_g.astype(BF16)
            heads = list(range(HEADS_PER_GROUP * g, HEADS_PER_GROUP * (g + 1)))
            segs = [acs[:, h:h + 1] - acs_t[h:h + 1, :] for h in heads]
            lms = [jnp.exp(jnp.where(causal, sg, NEG_BIG)) for sg in segs]
            mts = [(cb_t * jnp.exp(jnp.where(upper, -sg, NEG_BIG))).astype(BF16) for sg in segs]
            dyh = []
            for k in range(HEADS_PER_GROUP):
                blk = dy_g[:, LANES * (k // 2):LANES * (k // 2 + 1)]
                in_head = (lane < HEAD_DIM) if k % 2 == 0 else (lane >= HEAD_DIM)
                dyh.append(jnp.where(in_head, blk, 0.0).astype(BF16))
            dms = [_mm_nt(dyh[k], xdt_b[:, LANES * (k // 2):LANES * (k // 2 + 1)]) for k in range(HEADS_PER_GROUP)]
            dxs = [_mm(mts[k], dyh[k]) for k in range(HEADS_PER_GROUP)]
            dcb = jnp.zeros((CHUNK, CHUNK), F32)
            for k, h in enumerate(heads):
                gmat = dms[k] * (cb * lms[k])
                dacs_c = dacs_c + jnp.where(lane == h, jnp.sum(gmat, axis=1, keepdims=True), 0.0)
                dacs_r = dacs_r - jnp.where(sub == h, jnp.sum(gmat, axis=0, keepdims=True), 0.0)
                dcb = dcb + dms[k] * lms[k]
            dxdt_g = jnp.concatenate([dxs[2 * k] + dxs[2 * k + 1] for k in range(4)], axis=1)
            t_g = _mm_nt(c_b, h_b)
            dacs_c = dacs_c + _head_sums(dy_g * e_g * t_g, g)
            dt_b = (dy_g * e_g).astype(BF16)
            dc_acc = _mm(dt_b, h_b)
            dh_prev = _mm_tn(dt_b, c_b)
            dw_g = _mm_nt(b_b, dh_b)
            w_g = xdt_g * dte_g
            dxdt_g = dxdt_g + dw_g * dte_g
            db_acc = _mm(w_g.astype(BF16), dh_b)
            r2 = _head_sums(dw_g * w_g, g)
            dacs_c = dacs_c + jnp.where(is_last, jnp.sum(r2, axis=0, keepdims=True), 0.0) - r2
            q3 = jnp.sum(dh_g * h_g, axis=1, keepdims=True)
            for k, h in enumerate(heads):
                tot = jnp.sum(q3[HEAD_DIM * k:HEAD_DIM * (k + 1), :], keepdims=True) * cd[:, h:h + 1]
                dacs_c = dacs_c + jnp.where(is_last & (lane == h), tot, 0.0)
            dh_scr[g] = dh_prev + dh_g * jnp.exp(_rows_from_lanes(last_g))
            dxc_ref[:, cols] = dxdt_g * dt_g + dsk_ref[:, cols] * dy_g
            ddtx = ddtx + _head_sums(dxdt_g * x_g, g)
            ddsk_ref[:, cols] += jnp.sum(dy_g * x_g, axis=0, keepdims=True)
            dxc_ref[:, b_off:b_off + D_STATE] = db_acc + _mm(dcb.T.astype(BF16), c_b)
            dxc_ref[:, c_off:c_off + D_STATE] = dc_acc + _mm(dcb.astype(BF16), b_b)
        dacs = dacs_c + dacs_r.T
        dadt = _mm_exact((lane >= sub).astype(F32), dacs)
        ddt = dadt * a_row_v + ddtx
        ddt_raw = ddt * sig
        ddt_ref[...] = ddt_raw
        da_ref[...] += jnp.sum(dadt * dt, axis=0, keepdims=True)
        ddtb_ref[...] += jnp.sum(ddt_raw, axis=0, keepdims=True)

    return pl.pallas_call(
        body, name="ssd_bwd",
        out_shape=(jax.ShapeDtypeStruct((s, CONV_CH), F32), jax.ShapeDtypeStruct((s, LANES), F32),
                   jax.ShapeDtypeStruct((1, LANES), F32), jax.ShapeDtypeStruct((1, LANES), F32),
                   jax.ShapeDtypeStruct((1, SSD_WIDTH), F32)),
        grid=(nc,),
        in_specs=[pl.BlockSpec((CHUNK, CONV_CH), lambda c: (rev(c), 0)),
                  pl.BlockSpec((CHUNK, LANES), lambda c: (rev(c), 0)),
                  pl.BlockSpec((1, N_GROUPS, GROUP_WIDTH, D_STATE), lambda c: (rev(c), 0, 0, 0)),
                  pl.BlockSpec((CHUNK, SSD_WIDTH), lambda c: (rev(c), 0)),
                  _const_spec((1, LANES)), _const_spec((1, LANES)), _const_spec((1, SSD_WIDTH))],
        out_specs=(pl.BlockSpec((CHUNK, CONV_CH), lambda c: (rev(c), 0)),
                   pl.BlockSpec((CHUNK, LANES), lambda c: (rev(c), 0)),
                   _const_spec((1, LANES)), _const_spec((1, LANES)), _const_spec((1, SSD_WIDTH))),
        scratch_shapes=[pltpu.VMEM((N_GROUPS, GROUP_WIDTH, D_STATE), F32)],
        compiler_params=_params(("arbitrary",)),
    )(xc, small, states, dy, dtb_row, a_row, dskip_lane)


FORGET_BLOCK = 512


def forget_cumsum(small, fgb_row):
    s = small.shape[0]
    t = _blk(s, FORGET_BLOCK)
    nb = s // t

    def body(sm_ref, b_ref, cc_ref, carry):
        i = pl.program_id(0)

        @pl.when(i == 0)
        def _():
            carry[...] = jnp.zeros_like(carry)

        lane = _iota((t, LANES), 1)
        in_f = (lane >= N_HEADS) & (lane < 2 * N_HEADS)
        logf = jnp.where(in_f, -_softplus(-(sm_ref[...] + b_ref[...])), 0.0)
        tri = (_iota((t, t), 1) <= _iota((t, t), 0)).astype(F32)
        cum = _mm_exact(tri, logf) + carry[0:1, :]
        cc_ref[...] = cum
        carry[...] = jnp.broadcast_to(cum[t - 1:t, :], (8, LANES))

    return pl.pallas_call(
        body, name="forget_cumsum",
        out_shape=jax.ShapeDtypeStruct((s, LANES), F32),
        grid=(nb,),
        in_specs=[pl.BlockSpec((t, LANES), lambda i: (i, 0)), _const_spec((1, LANES))],
        out_specs=pl.BlockSpec((t, LANES), lambda i: (i, 0)),
        scratch_shapes=[pltpu.VMEM((8, LANES), F32)],
        compiler_params=_params(("arbitrary",)),
    )(small, fgb_row)


def forget_bwd(dc, small, ddt_raw, fgb_row):
    s = small.shape[0]
    t = _blk(s, FORGET_BLOCK)
    nb = s // t
    rev = lambda i: nb - 1 - i

    def body(dc_ref, sm_ref, ddt_ref, b_ref, ds_ref, dfb_ref, carry):
        i = pl.program_id(0)

        @pl.when(i == 0)
        def _():
            carry[...] = jnp.zeros_like(carry)
            dfb_ref[...] = jnp.zeros_like(dfb_ref)

        lane = _iota((t, LANES), 1)
        rows = dc_ref[...].T
        tri = (_iota((t, t), 1) <= _iota((t, t), 0)).astype(F32)
        rc = _mm_exact(rows, tri) + carry[:, 0:1]
        carry[...] = jnp.broadcast_to(rc[:, 0:1], (LANES, LANES))
        in_f = (lane >= N_HEADS) & (lane < 2 * N_HEADS)
        df = jnp.where(in_f, rc.T * _sigmoid(-(sm_ref[...] + b_ref[...])), 0.0)
        ds_ref[...] = (df + ddt_ref[...]).astype(BF16)
        dfb_ref[...] += jnp.sum(df, axis=0, keepdims=True)

    blk = pl.BlockSpec((t, LANES), lambda i: (rev(i), 0))
    return pl.pallas_call(
        body, name="forget_bwd",
        out_shape=(jax.ShapeDtypeStruct((s, LANES), BF16), jax.ShapeDtypeStruct((1, LANES), F32)),
        grid=(nb,),
        in_specs=[blk, blk, blk, _const_spec((1, LANES))],
        out_specs=(blk, _const_spec((1, LANES))),
        scratch_shapes=[pltpu.VMEM((LANES, LANES), F32)],
        compiler_params=_params(("arbitrary",)),
    )(dc, small, ddt_raw, fgb_row)


ATT_BLOCK = 512
ATT_SCALE = HEAD_DIM ** -0.5
AUG_A = HEAD_DIM
AUG_B = HEAD_DIM + 3


def _split3(c):
    hi = c.astype(BF16).astype(F32)
    r = c - hi
    mid = r.astype(BF16).astype(F32)
    return hi, mid, (r - mid).astype(BF16).astype(F32)


def _aug(lane, first, parts=None, value=1.0):
    if parts is None:
        return jnp.where((lane >= first) & (lane < first + 3), value, 0.0)
    return (jnp.where(lane == first, parts[0], 0.0) + jnp.where(lane == first + 1, parts[1], 0.0)
            + jnp.where(lane == first + 2, parts[2], 0.0))


def _pack_pair(a0, a1, lane):
    return jnp.where(lane < HEAD_DIM, a0, pltpu.roll(a1, HEAD_DIM, 1))


def proj_qkv_heads(u, w_q, w_k, w_v, cum):
    s = u.shape[0]
    tm = _blk(s, 256)

    def body(u_ref, wq_ref, wk_ref, wv_ref, c_ref, qa_ref, ka_ref, va_ref, nrm_ref):
        lane = _iota((tm, LANES), 1)
        lo = lane < HEAD_DIM
        uv = u_ref[...]
        qf = _mm(uv, wq_ref[...]) * ATT_SCALE
        kf = _mm(uv, wk_ref[...])
        vf = _mm(uv, wv_ref[...])
        cc = c_ref[...]
        ones_a = _aug(lane, AUG_A)
        ones_b = _aug(lane, AUG_B)
        sub8 = _iota((8, LANES), 0)
        nrm = jnp.zeros((8, LANES), F32)
        for h in range(N_HEADS):
            j, e = divmod(h, 2)

            def head(full):
                blk = full[:, LANES * j:LANES * (j + 1)]
                if e == 1:
                    blk = pltpu.roll(blk, HEAD_DIM, 1)
                return jnp.where(lo, blk, 0.0)

            parts = _split3(cc[:, N_HEADS + h:N_HEADS + h + 1])
            qh, kh = head(qf), head(kf)
            qa_ref[h] = (qh + _aug(lane, AUG_A, parts) + ones_b).astype(BF16)
            ka_ref[h] = (kh + ones_a - _aug(lane, AUG_B, parts)).astype(BF16)
            va_ref[h] = (head(vf) + ones_a).astype(BF16)
        seg = (_iota((ATT_WIDTH, LANES), 1) == (_iota((ATT_WIDTH, LANES), 0) >> 6)).astype(BF16)
        for r, val in enumerate((qf, kf)):
            sq = val * val
            hi = sq.astype(BF16)
            tot = _mm(hi, seg) + _mm((sq - hi.astype(F32)).astype(BF16), seg)
            nrm = nrm + jnp.where(sub8 == r, jnp.max(tot, axis=0, keepdims=True), 0.0)
        nrm_ref[0] = nrm

    shp = jax.ShapeDtypeStruct((N_HEADS, s, LANES), BF16)
    hspec = pl.BlockSpec((N_HEADS, tm, LANES), lambda i: (0, i, 0))
    wspec = _const_spec((D_MODEL, ATT_WIDTH))
    return pl.pallas_call(
        body, name="proj_qkv_heads",
        out_shape=(shp, shp, shp, jax.ShapeDtypeStruct((s // tm, 8, LANES), F32)), grid=(s // tm,),
        in_specs=[pl.BlockSpec((tm, D_MODEL), lambda i: (i, 0)), wspec, wspec, wspec,
                  pl.BlockSpec((tm, LANES), lambda i: (i, 0))],
        out_specs=(hspec, hspec, hspec, pl.BlockSpec((1, 8, LANES), lambda i: (i, 0, 0))),
        compiler_params=_params(("parallel",)),
    )(u, w_q, w_k, w_v, cum)


SKIP_BELOW = -110.0


def live_blocks(norms, cum, t):
    qn = jnp.sqrt(jnp.max(norms[:, 0, :N_HEADS], axis=0))
    kn = jnp.sqrt(jnp.max(norms[:, 1, :N_HEADS], axis=0))
    bound = 2.05 * qn * kn + 2.0
    c_first = cum[0::t, N_HEADS:2 * N_HEADS]
    c_last = cum[t - 1::t, N_HEADS:2 * N_HEADS]
    nq = c_first.shape[0]
    top = bound[None, None, :] + c_first[:, None, :] - c_last[None, :, :]
    below = jnp.arange(nq)[None, :] < jnp.arange(nq)[:, None]
    dead = below[:, :, None] & ~(top >= SKIP_BELOW)
    first = jnp.sum(dead, axis=1).astype(jnp.int32).T
    last_q = jnp.sum(first[:, None, :] <= jnp.arange(nq)[None, :, None], axis=2).astype(jnp.int32) - 1
    return first, last_q


def attention_fwd(first, qa, ka, va):
    s = qa.shape[1]
    t = _blk(s, ATT_BLOCK)
    nq = s // t

    def body(first_ref, qa_ref, ka_ref, va_ref, o_ref, qb_ref, m_scr, acc_scr, alpha_scr, p_scr, s_scr):
        qi = pl.program_id(1)
        starts = [first_ref[2 * pl.program_id(0) + e, qi] for e in range(2)]
        k0 = jnp.maximum(starts[0], starts[1])
        m_scr[...] = jnp.full_like(m_scr, NEG_BIG)
        acc_scr[...] = jnp.zeros_like(acc_scr)

        def kv_rows(kb):
            return pl.ds(pl.multiple_of(kb * t, t), t)

        def logits(kb, masked, heads=(0, 1)):
            for e in heads:
                sc = _mm_nt(qa_ref[e], ka_ref[e, kv_rows(kb), :])
                if masked:
                    sc = jnp.where(_iota((t, t), 0) >= _iota((t, t), 1), sc, NEG_BIG)
                s_scr[e] = sc

        def probs(heads=(0, 1)):
            for e in heads:
                cmax = s_scr[e, :, 0:LANES]
                for c in range(1, t // LANES):
                    cmax = jnp.maximum(cmax, s_scr[e, :, LANES * c:LANES * (c + 1)])
                m_old = m_scr[e]
                m_new = jnp.maximum(m_old, jnp.max(cmax, axis=1, keepdims=True))
                alpha_scr[e] = jnp.exp(m_old - m_new)
                m_scr[e] = m_new
                for c in range(t // LANES):
                    cols = slice(LANES * c, LANES * (c + 1))
                    p_scr[e, :, cols] = jnp.exp(s_scr[e, :, cols] - m_new).astype(BF16)

        def accumulate(kb, heads=(0, 1)):
            for e in heads:
                acc_scr[e] = alpha_scr[e] * acc_scr[e] + _mm(p_scr[e], va_ref[e, kv_rows(kb), :])

        for e in range(2):
            def alone(kb, carry, e=e):
                logits(kb, False, (e,))
                probs((e,))
                accumulate(kb, (e,))
                return carry

            lax.fori_loop(starts[e], k0, alone, 0)

        def loop_body(kb, carry):
            logits(kb, False)
            accumulate(kb - 1)
            probs()
            return carry

        @pl.when(qi > k0)
        def _():
            logits(k0, False)
            probs()

        lax.fori_loop(k0 + 1, qi, loop_body, 0)

        @pl.when(qi > k0)
        def _():
            logits(qi, True)
            accumulate(qi - 1)
            probs()

        @pl.when(qi == k0)
        def _():
            logits(qi, True)
            probs()

        accumulate(qi)

        lane = _iota((t, LANES), 1)
        outs = []
        for e in range(2):
            acc = acc_scr[e]
            l = acc[:, AUG_A:AUG_A + 1]
            outs.append(acc / l)
            lse = m_scr[e][:, 0:1] + jnp.log(l)
            q32 = qa_ref[e].astype(F32)
            c = q32[:, AUG_A:AUG_A + 1] + q32[:, AUG_A + 1:AUG_A + 2] + q32[:, AUG_A + 2:AUG_A + 3]
            qb = jnp.where(lane < HEAD_DIM, q32, 0.0) + _aug(lane, AUG_A, _split3(c - lse)) + _aug(lane, AUG_B)
            qb_ref[e] = qb.astype(BF16)
        o_ref[...] = _pack_pair(outs[0], outs[1], lane)

    grid_spec = pltpu.PrefetchScalarGridSpec(
        num_scalar_prefetch=1, grid=(N_PAIRS, nq),
        in_specs=[pl.BlockSpec((2, t, LANES), lambda j, qi, f: (j, qi, 0)),
                  pl.BlockSpec((2, s, LANES), lambda j, qi, f: (j, 0, 0)),
                  pl.BlockSpec((2, s, LANES), lambda j, qi, f: (j, 0, 0))],
        out_specs=[pl.BlockSpec((t, LANES), lambda j, qi, f: (qi, j)),
                   pl.BlockSpec((2, t, LANES), lambda j, qi, f: (j, qi, 0))],
        scratch_shapes=[pltpu.VMEM((2, t, LANES), F32), pltpu.VMEM((2, t, LANES), F32),
                        pltpu.VMEM((2, t, LANES), F32), pltpu.VMEM((2, t, t), BF16), pltpu.VMEM((2, t, t), F32)])
    return pl.pallas_call(
        body, name="attention_fwd", grid_spec=grid_spec,
        out_shape=(jax.ShapeDtypeStruct((s, ATT_WIDTH), F32), jax.ShapeDtypeStruct((N_HEADS, s, LANES), BF16)),
        compiler_params=_params(("parallel", "parallel")),
    )(first, qa, ka, va)


def attention_bwd(last_q, qb, ka, va, dob):
    s = qb.shape[1]
    t = _blk(s, ATT_BLOCK)
    nq = s // t

    def body(last_ref, qb_ref, dob_ref, ka_ref, va_ref, dq_ref, dk_ref, dv_ref, dc_ref, dq_scr, dk_scr, dv_scr):
        j, ki = pl.program_id(0), pl.program_id(1)

        @pl.when((j == 0) & (ki == 0))
        def _():
            dc_ref[...] = jnp.zeros_like(dc_ref)

        @pl.when(ki == 0)
        def _():
            dq_scr[...] = jnp.zeros_like(dq_scr)

        dk_scr[...] = jnp.zeros_like(dk_scr)
        dv_scr[...] = jnp.zeros_like(dv_scr)

        def q_step(qblk, masked, heads=(0, 1)):
            rows = pl.ds(pl.multiple_of(qblk * t, t), t)
            scs = [_mm_nt(qb_ref[e, rows, :], ka_ref[e]) for e in heads]
            dps = [_mm_nt(dob_ref[e, rows, :], va_ref[e]) for e in heads]
            for e, sc, dp in zip(heads, scs, dps):
                q = qb_ref[e, rows, :]
                do = dob_ref[e, rows, :]
                if masked:
                    sc = jnp.where(_iota((t, t), 0) >= _iota((t, t), 1), sc, NEG_BIG)
                p = jnp.exp(sc)
                ds_b = (p * dp).astype(BF16)
                dv_scr[e] += _mm_tn(p.astype(BF16), do)
                dk_scr[e] += _mm_tn(ds_b, q)
                dq_scr[e, rows, :] += _mm(ds_b, ka_ref[e])

        def loop_body(qblk, carry):
            q_step(qblk, False)
            return carry

        ends = [last_ref[2 * j + e, ki] + 1 for e in range(2)]
        both = jnp.minimum(ends[0], ends[1])
        q_step(ki, True)
        lax.fori_loop(ki + 1, both, loop_body, 0)
        for e in range(2):
            def alone(qblk, carry, e=e):
                q_step(qblk, False, (e,))
                return carry

            lax.fori_loop(both, ends[e], alone, 0)

        lane = _iota((t, LANES), 1)
        dk_ref[...] = _pack_pair(dk_scr[0], dk_scr[1], lane).astype(BF16)
        dv_ref[...] = _pack_pair(dv_scr[0], dv_scr[1], lane).astype(BF16)
        rows = pl.ds(pl.multiple_of(ki * t, t), t)
        dc_ref[rows, :] -= (jnp.where(lane == N_HEADS + 2 * j, dk_scr[0][:, AUG_B:AUG_B + 1], 0.0)
                            + jnp.where(lane == N_HEADS + 2 * j + 1, dk_scr[1][:, AUG_B:AUG_B + 1], 0.0))

        @pl.when(ki == nq - 1)
        def _():
            for blk in range(nq):
                rws = pl.ds(blk * t, t)
                d0 = dq_scr[0, rws, :]
                d1 = dq_scr[1, rws, :]
                dq_ref[rws, :] = (_pack_pair(d0, d1, lane) * ATT_SCALE).astype(BF16)
                dc_ref[rws, :] += (jnp.where(lane == N_HEADS + 2 * j, d0[:, AUG_A:AUG_A + 1], 0.0)
                                   + jnp.where(lane == N_HEADS + 2 * j + 1, d1[:, AUG_A:AUG_A + 1], 0.0))

    full = pl.BlockSpec((2, s, LANES), lambda j, ki, f: (j, 0, 0))
    blk = pl.BlockSpec((2, t, LANES), lambda j, ki, f: (j, ki, 0))
    pair = pl.BlockSpec((t, LANES), lambda j, ki, f: (ki, j))
    wide = jax.ShapeDtypeStruct((s, ATT_WIDTH), BF16)
    grid_spec = pltpu.PrefetchScalarGridSpec(
        num_scalar_prefetch=1, grid=(N_PAIRS, nq),
        in_specs=[full, full, blk, blk],
        out_specs=[pl.BlockSpec((s, LANES), lambda j, ki, f: (0, j)), pair, pair,
                   pl.BlockSpec((s, LANES), lambda j, ki, f: (0, 0))],
        scratch_shapes=[pltpu.VMEM((2, s, LANES), F32), pltpu.VMEM((2, t, LANES), F32),
                        pltpu.VMEM((2, t, LANES), F32)])
    return pl.pallas_call(
        body, name="attention_bwd", grid_spec=grid_spec,
        out_shape=(wide, wide, wide, jax.ShapeDtypeStruct((s, LANES), F32)),
        compiler_params=_params(("arbitrary", "arbitrary")),
    )(last_q, qb, dob, ka, va)


def _dsilu(z, sg):
    return sg * (1.0 + z * (1.0 - sg))


def post_mix(x, y, zs, o, za, p, tgt, ssd_g, att_g_lane, ple_g, fin_g, w_out, w_gate, w_proj):
    s = x.shape[0]
    tm = _blk(s, 128)
    half = SSD_WIDTH // N_GROUPS

    def rms_bwd(dy, yn, r):
        return r * (dy - yn * jnp.mean(dy * yn, axis=-1, keepdims=True))

    def colsum(a):
        return jnp.sum(a, axis=0, keepdims=True)

    def body(x_ref, y_ref, zs_ref, o_ref, za_ref, p_ref, t_ref, sg_ref, ag_ref, pg_ref, fg_ref,
             wo_ref, wg_ref, wp_ref,
             dh1_ref, dy_ref, dzs_ref, dob_ref, dza_ref, ycat_ref, dh1b_ref, n2b_ref, dglb_ref, dppb_ref, pb_ref,
             loss_ref, dfin_ref, dple_ref, dssd_ref, datt_ref):
        @pl.when(pl.program_id(0) == 0)
        def _():
            for r in (loss_ref, dfin_ref, dple_ref, dssd_ref, datt_ref):
                r[...] = jnp.zeros_like(r)

        lane = _iota((tm, LANES), 1)
        lo = lane < HEAD_DIM
        zs = zs_ref[...]
        sz = _sigmoid(zs)
        yv = y_ref[...]
        ys = yv * (zs * sz)
        yn, rg = [], []
        for g in range(N_GROUPS):
            seg = ys[:, half * g:half * (g + 1)]
            r = lax.rsqrt(jnp.mean(seg * seg, axis=-1, keepdims=True) + EPS)
            yn.append(seg * r)
            rg.append(r)
            ycat_ref[:, half * g:half * (g + 1)] = (yn[g] * sg_ref[:, half * g:half * (g + 1)]).astype(BF16)
        za = za_ref[...]
        sza = _sigmoid(za)
        silu_za = za * sza
        on, ra = [], []
        for jb in range(N_PAIRS):
            blk = o_ref[:, LANES * jb:LANES * (jb + 1)]
            sq = blk * blk
            ms0 = jnp.sum(jnp.where(lo, sq, 0.0), axis=1, keepdims=True) * (1.0 / HEAD_DIM)
            ms1 = jnp.sum(jnp.where(lo, 0.0, sq), axis=1, keepdims=True) * (1.0 / HEAD_DIM)
            r = jnp.where(lo, lax.rsqrt(ms0 + EPS), lax.rsqrt(ms1 + EPS))
            on.append(blk * r)
            ra.append(r)
            an = on[jb] * ag_ref[:, LANES * jb:LANES * (jb + 1)]
            ycat_ref[:, SSD_WIDTH + LANES * jb:SSD_WIDTH + LANES * (jb + 1)] = (
                an * silu_za[:, LANES * jb:LANES * (jb + 1)]).astype(BF16)
        h1 = x_ref[...] + _mm(ycat_ref[...], wo_ref[...])
        r2 = lax.rsqrt(jnp.mean(h1 * h1, axis=-1, keepdims=True) + EPS)
        n2h = h1 * r2
        n2_b = (n2h * pg_ref[...]).astype(BF16)
        gate = _sigmoid(_mm(n2_b, wg_ref[...]))
        p_b = p_ref[...].astype(BF16)
        pp = _mm(p_b, wp_ref[...])
        h2 = h1 + gate * pp
        r3 = lax.rsqrt(jnp.mean(h2 * h2, axis=-1, keepdims=True) + EPS)
        n3 = h2 * r3
        diff = n3 * fg_ref[...] - t_ref[...]
        sq = colsum(diff * diff)
        part = sq[:, 0:LANES]
        for jb in range(1, D_MODEL // LANES):
            part = part + sq[:, LANES * jb:LANES * (jb + 1)]
        loss_ref[...] += part * (0.5 / D_MODEL)
        dout = diff * (1.0 / D_MODEL)
        dfin_ref[...] += colsum(dout * n3)
        dh2 = rms_bwd(dout * fg_ref[...], n3, r3)
        dgl = dh2 * pp * gate * (1.0 - gate)
        dgl_b = dgl.astype(BF16)
        dn2 = _mm_nt(dgl_b, wg_ref[...])
        dple_ref[...] += colsum(dn2 * n2h)
        dh1 = dh2 + rms_bwd(dn2 * pg_ref[...], n2h, r2)
        dh1_b = dh1.astype(BF16)
        dycat = _mm_nt(dh1_b, wo_ref[...])
        dh1_ref[...] = dh1
        dh1b_ref[...] = dh1_b
        n2b_ref[...] = n2_b
        dglb_ref[...] = dgl_b
        dppb_ref[...] = (dh2 * gate).astype(BF16)
        pb_ref[...] = p_b
        for g in range(N_GROUPS):
            cols = slice(half * g, half * (g + 1))
            dys_g = dycat[:, cols]
            dssd_ref[:, cols] += colsum(dys_g * yn[g])
            dys = rms_bwd(dys_g * sg_ref[:, cols], yn[g], rg[g])
            dy_ref[:, cols] = dys * (zs[:, cols] * sz[:, cols])
            dzs_ref[:, cols] = (dys * yv[:, cols] * _dsilu(zs[:, cols], sz[:, cols])).astype(BF16)
        for jb in range(N_PAIRS):
            cols = slice(LANES * jb, LANES * (jb + 1))
            dya = dycat[:, SSD_WIDTH + LANES * jb:SSD_WIDTH + LANES * (jb + 1)]
            ag = ag_ref[:, cols]
            dan = dya * silu_za[:, cols]
            dza_ref[:, cols] = (dya * (on[jb] * ag) * _dsilu(za[:, cols], sza[:, cols])).astype(BF16)
            datt_ref[:, cols] += colsum(dan * on[jb])
            don = dan * ag
            q = don * on[jb]
            m0 = jnp.sum(jnp.where(lo, q, 0.0), axis=1, keepdims=True) * (1.0 / HEAD_DIM)
            m1 = jnp.sum(jnp.where(lo, 0.0, q), axis=1, keepdims=True) * (1.0 / HEAD_DIM)
            do2 = ra[jb] * (don - on[jb] * jnp.where(lo, m0, m1))
            prod = do2 * o_ref[:, cols]
            for e in range(2):
                delta = jnp.sum(jnp.where(lo, prod, 0.0) if e == 0 else jnp.where(lo, 0.0, prod),
                                axis=1, keepdims=True)
                base = jnp.where(lo, do2 if e == 0 else pltpu.roll(do2, HEAD_DIM, 1), 0.0)
                dob_ref[2 * jb + e] = (base - _aug(lane, AUG_A, _split3(delta))).astype(BF16)

    def rows(n, dtype=None):
        return pl.BlockSpec((tm, n), lambda i: (i, 0))

    def out(n, dtype):
        return jax.ShapeDtypeStruct((s, n), dtype)

    vec = _const_spec((1, D_MODEL))
    vshape = jax.ShapeDtypeStruct((1, D_MODEL), F32)
    return pl.pallas_call(
        body, name="post_mix",
        out_shape=(out(D_MODEL, F32), out(SSD_WIDTH, F32), out(SSD_WIDTH, BF16),
                   jax.ShapeDtypeStruct((N_HEADS, s, LANES), BF16),
                   out(ATT_WIDTH, BF16), out(D_INNER, BF16), out(D_MODEL, BF16), out(D_MODEL, BF16),
                   out(D_MODEL, BF16), out(D_MODEL, BF16), out(PLE_DIM, BF16),
                   jax.ShapeDtypeStruct((1, LANES), F32), vshape, vshape, vshape, vshape),
        grid=(s // tm,),
        in_specs=[rows(D_MODEL), rows(SSD_WIDTH), rows(SSD_WIDTH), rows(ATT_WIDTH), rows(ATT_WIDTH),
                  rows(PLE_DIM), rows(D_MODEL), vec, vec, vec, vec,
                  _const_spec((D_INNER, D_MODEL)), _const_spec((D_MODEL, D_MODEL)), _const_spec((PLE_DIM, D_MODEL))],
        out_specs=(rows(D_MODEL), rows(SSD_WIDTH), rows(SSD_WIDTH),
                   pl.BlockSpec((N_HEADS, tm, LANES), lambda i: (0, i, 0)), rows(ATT_WIDTH),
                   rows(D_INNER), rows(D_MODEL), rows(D_MODEL), rows(D_MODEL), rows(D_MODEL), rows(PLE_DIM),
                   _const_spec((1, LANES)), vec, vec, vec, vec),
        compiler_params=_params(("arbitrary",)),
    )(x, y, zs, o, za, p, tgt, ssd_g, att_g_lane, ple_g, fin_g, w_out, w_gate, w_proj)


def in_proj_bwd(dsegs, wsegs, x, g, dh1):
    s = x.shape[0]
    tm = _blk(s, 256)
    nseg = len(dsegs)

    def body(*refs):
        d_refs = refs[:nseg]
        w_refs = refs[nseg:2 * nseg]
        x_ref, g_ref, dh1_ref, dx_ref, dg_ref = refs[2 * nseg:]

        @pl.when(pl.program_id(0) == 0)
        def _():
            dg_ref[...] = jnp.zeros_like(dg_ref)

        du = _mm_nt(d_refs[0][...], w_refs[0][...])
        for k in range(1, nseg):
            du = du + _mm_nt(d_refs[k][...], w_refs[k][...])
        xv = x_ref[...]
        r = lax.rsqrt(jnp.mean(xv * xv, axis=-1, keepdims=True) + EPS)
        xh = xv * r
        dg_ref[...] += jnp.sum(du * xh, axis=0, keepdims=True)
        dxh = du * g_ref[...]
        dx_ref[...] = r * (dxh - xh * jnp.mean(dxh * xh, axis=-1, keepdims=True)) + dh1_ref[...]

    rows = lambda n: pl.BlockSpec((tm, n), lambda i: (i, 0))
    return pl.pallas_call(
        body, name="in_proj_bwd",
        out_shape=(jax.ShapeDtypeStruct((s, D_MODEL), F32), jax.ShapeDtypeStruct((1, D_MODEL), F32)),
        grid=(s // tm,),
        in_specs=([rows(d.shape[1]) for d in dsegs] + [_const_spec(w.shape) for w in wsegs]
                  + [rows(D_MODEL), _const_spec((1, D_MODEL)), rows(D_MODEL)]),
        out_specs=(rows(D_MODEL), _const_spec((1, D_MODEL))),
        compiler_params=_params(("arbitrary",)),
    )(*dsegs, *wsegs, x, g, dh1)


SMALL_NAMES = ("norm_g", "conv_b", "dt_bias", "a_log", "d_skip", "ssd_norm_g", "fg_bias", "att_norm_g",
               "ple_norm_g", "final_norm_g")
SMALL_SIZES = (1024, 1536, 16, 16, 16, 1024, 16, 64, 1024, 1024)
CONV_W_SIZE = CONV_WIDTH * CONV_CH


def _pack_small(vals):
    flat = jnp.concatenate([v.reshape(-1).astype(F32) for v in vals])
    flat = jnp.pad(flat, (0, SMALL_ROWS * LANES - flat.shape[0]))
    return flat.reshape(SMALL_ROWS, LANES)


def _unpack_small(pack, shapes):
    flat = pack.reshape(-1)
    out, off = [], 0
    for n, shp in zip(SMALL_SIZES, shapes):
        out.append(flat[off:off + n].reshape(shp))
        off += n
    return out


def _row128(v16, offset=0):
    return jnp.pad(v16.reshape(1, N_HEADS).astype(F32), ((0, 0), (offset, LANES - N_HEADS - offset)))


def local_step(x, p, tgt, w_in, w_out, w_gate, w_proj, conv_w, norm_g, conv_b, dt_bias, a_log, d_skip,
               ssd_norm_g, fg_bias, att_norm_g, ple_norm_g, final_norm_g):
    c0, c1, c2, c3, c4, c5, c6, c7 = 0, 1024, 2560, 2576, 3600, 4624, 5648, 6672
    w_zs, w_xbc, w_dt = w_in[:, c0:c1], w_in[:, c1:c2], w_in[:, c2:c3]
    w_za, w_q, w_k, w_v, w_f = w_in[:, c3:c4], w_in[:, c4:c5], w_in[:, c5:c6], w_in[:, c6:c7], w_in[:, c7:]
    w_small = jnp.concatenate([w_dt, w_f, jnp.zeros((D_MODEL, LANES - 2 * N_HEADS), BF16)], axis=1)

    dtb_row = _row128(dt_bias)
    a_row = _row128(-jnp.exp(a_log.astype(F32)))
    fgb_row = _row128(fg_bias, N_HEADS)
    dskip_lane = jnp.repeat(d_skip.astype(F32), HEAD_DIM).reshape(1, SSD_WIDTH)
    att_g_lane = jnp.tile(att_norm_g.astype(F32), N_HEADS).reshape(1, ATT_WIDTH)
    row = lambda v: v.reshape(1, -1).astype(F32)

    u = rms_prenorm(x, row(norm_g))
    zs = matmul_rows(u, w_zs, F32, "proj_z_ssd")
    xbc = matmul_rows(u, w_xbc, F32, "proj_xbc")
    za = matmul_rows(u, w_za, F32, "proj_z_att")
    small = matmul_rows(u, w_small, F32, "proj_small")
    cum = forget_cumsum(small, fgb_row)
    qa, ka, va, norms = proj_qkv_heads(u, w_q, w_k, w_v, cum)
    first, last_q = live_blocks(norms, cum, _blk(x.shape[0], ATT_BLOCK))
    pre, xc = conv_fwd(xbc, conv_w, row(conv_b))
    y, states = ssd_fwd(xc, small, dtb_row, a_row, dskip_lane)
    o, qb = attention_fwd(first, qa, ka, va)
    (dh1, dy, dzs, dob, dza, ycat, dh1_b, n2_b, dgl_b, dpp_b, p_b,
     loss_l, dfin, dple, dssd_g, datt_lane) = post_mix(
        x, y, zs, o, za, p, tgt, row(ssd_norm_g), att_g_lane, row(ple_norm_g), row(final_norm_g),
        w_out, w_gate, w_proj)
    dq, dk, dv, dc = attention_bwd(last_q, qb, ka, va, dob)
    dxc, ddt_raw, da, ddtb, ddsk_lane = ssd_bwd(xc, small, states, dy, dtb_row, a_row, dskip_lane)
    dsmall, dfgb = forget_bwd(dc, small, ddt_raw, fgb_row)
    dxbc, dconv_w8, dconv_b = conv_bwd(xbc, pre, dxc, conv_w)
    dsegs = [dzs, dxbc, dza, dq, dk, dv, dsmall]
    wsegs = [w_zs, w_xbc, w_za, w_q, w_k, w_v, w_small]
    dx, dnorm_g = in_proj_bwd(dsegs, wsegs, x, row(norm_g), dh1)
    dws = [matmul_tn(u, d, "dw_in_%d" % i) for i, d in enumerate(dsegs)]
    dw_in = jnp.concatenate([dws[0], dws[1], dws[6][:, :N_HEADS], dws[2], dws[3], dws[4], dws[5],
                             dws[6][:, N_HEADS:2 * N_HEADS]], axis=1)
    dw_out = matmul_tn(ycat, dh1_b, "dw_out")
    dw_gate = matmul_tn(n2_b, dgl_b, "dw_gate")
    dw_proj = matmul_tn(p_b, dpp_b, "dw_proj")
    small_grads = [
        dnorm_g, dconv_b, ddtb[0, :N_HEADS], (da * a_row)[0, :N_HEADS],
        ddsk_lane.reshape(N_HEADS, HEAD_DIM).sum(axis=1), dssd_g, dfgb[0, N_HEADS:2 * N_HEADS],
        datt_lane.reshape(N_HEADS, HEAD_DIM).sum(axis=0), dple, dfin]
    loss = jnp.sum(loss_l)
    return loss, dx, dw_in, dw_out, dw_gate, dw_proj, dconv_w8[:CONV_WIDTH], small_grads


def kernel(x, p, norm_g, w_in, conv_w, conv_b, dt_bias, a_log, d_skip, ssd_norm_g, fg_bias, att_norm_g, w_out, ple_norm_g, w_ple_gate, w_ple_proj, final_norm_g, loss_target, m_norm_g, m_w_in, m_conv_w, m_conv_b, m_dt_bias, m_a_log, m_d_skip, m_ssd_norm_g, m_fg_bias, m_att_norm_g, m_w_out, m_ple_norm_g, m_w_ple_gate, m_w_ple_proj, m_final_norm_g, v_norm_g, v_w_in, v_conv_w, v_conv_b, v_dt_bias, v_a_log, v_d_skip, v_ssd_norm_g, v_fg_bias, v_att_norm_g, v_w_out, v_ple_norm_g, v_w_ple_gate, v_w_ple_proj, v_final_norm_g):
    chip = 2 * lax.axis_index("x") + lax.axis_index("y")
    core = lax.axis_index("c")

    big_w = [w_in[0], w_out[0], w_ple_gate[0], w_ple_proj[0]]
    own = [a.astype(BF16) for a in big_w] + [conv_w[0]]
    gathered = gather_weights(own[:4], own[4])

    def joined(k, axis):
        return jnp.concatenate([jnp.where(chip == j, own[k], gathered[k][j]) for j in range(N_CHIPS)], axis=axis)

    w_in_f, w_out_f, w_gate_f, w_proj_f, conv_w_f = joined(0, 1), joined(1, 0), joined(2, 0), joined(3, 1), joined(4, 1)

    smalls_w = [norm_g, conv_b, dt_bias, a_log, d_skip, ssd_norm_g, fg_bias, att_norm_g, ple_norm_g, final_norm_g]
    loss_l, dx, dw_in, dw_out, dw_gate, dw_proj, dconv_w, small_grads = local_step(
        x[0], p[0, 0], loss_target[0], w_in_f, w_out_f, w_gate_f, w_proj_f, conv_w_f,
        *[a.reshape(-1) for a in smalls_w])
    loss = lax.psum(loss_l, ("x", "y", "c"))

    gs = [jnp.stack([dw_in[:, 1672 * j:1672 * (j + 1)] for j in range(N_CHIPS)]),
          dw_out.reshape(N_CHIPS, 512, D_MODEL), dw_gate.reshape(N_CHIPS, 256, D_MODEL),
          jnp.stack([dw_proj[:, 256 * j:256 * (j + 1)] for j in range(N_CHIPS)])]
    core1 = core.reshape(1).astype(jnp.int32)
    pres = add_halves(core1, gs, halves_to_sibling(gs))
    *parts, smalls = scatter_halves(pres, _pack_small(list(small_grads) + [dconv_w]))
    mine = sum_parts(parts)

    g_big, d_big, m_big, v_big = adamw_big(
        core1, mine, swap_halves(mine), big_w, [m_w_in[0], m_w_out[0], m_w_ple_gate[0], m_w_ple_proj[0]],
        [v_w_in[0], v_w_out[0], v_w_ple_gate[0], v_w_ple_proj[0]])
    smalls_m = [m_norm_g, m_conv_b, m_dt_bias, m_a_log, m_d_skip, m_ssd_norm_g, m_fg_bias, m_att_norm_g,
                m_ple_norm_g, m_final_norm_g]
    smalls_v = [v_norm_g, v_conv_b, v_dt_bias, v_a_log, v_d_skip, v_ssd_norm_g, v_fg_bias, v_att_norm_g,
                v_ple_norm_g, v_final_norm_g]
    g_sm, d_sm, m_sm, v_sm = adamw_small(smalls, _pack_small(smalls_w), _pack_small(smalls_m), _pack_small(smalls_v))
    n_small = sum(SMALL_SIZES)
    g_conv_full = g_sm.reshape(-1)[n_small:n_small + CONV_W_SIZE].reshape(CONV_WIDTH, CONV_CH)
    g_conv = lax.dynamic_slice_in_dim(g_conv_full, chip * 384, 384, axis=1)
    d_conv, m_conv, v_conv = adamw_whole(g_conv, conv_w[0], m_conv_w[0], v_conv_w[0], "adamw_conv")

    shapes = [a.shape for a in smalls_w]
    outs = []
    for big, conv, sm in ((g_big, g_conv, g_sm), (d_big, d_conv, d_sm), (m_big, m_conv, m_sm), (v_big, v_conv, v_sm)):
        b_in, b_out, b_gate, b_proj = [a[None] for a in big]
        s_norm, s_convb, s_dtb, s_alog, s_dsk, s_ssdg, s_fgb, s_attg, s_pleg, s_fin = _unpack_small(sm, shapes)
        outs.extend([s_norm, b_in, conv[None], s_convb, s_dtb, s_alog, s_dsk, s_ssdg, s_fgb, s_attg, b_out, s_pleg,
                     b_gate, b_proj, s_fin])
    return (loss, dx[None], *outs)
```

```python
import functools

import jax
import jax.numpy as jnp
from jax import lax
from jax.experimental import pallas as pl
from jax.experimental.pallas import tpu as pltpu

F32 = jnp.float32
BF16 = jnp.bfloat16

D_MODEL = 1024
SSD_WIDTH = 1024
ATT_WIDTH = 1024
N_HEADS = 16
HEAD_DIM = 64
N_GROUPS = 2
D_STATE = 128
CONV_CH = 1536
CONV_WIDTH = 4
CHUNK = 128
PLE_DIM = 256
D_INNER = 2048
EPS = 1e-6
IN_COLS = 6688
N_CHIPS = 4
N_DEV = 8
LANES = 128
N_PAIRS = 8

ADAM_LR = 0.001
ADAM_B1 = 0.9
ADAM_B2 = 0.999
ADAM_EPS = 1e-08
ADAM_WD = 0.01
ADAM_STEP = 10

SMALL_ROWS = 96

NEG_BIG = -1e30
VMEM_LIMIT = 56 * 1024 * 1024

MESH = pl.DeviceIdType.MESH
ANY = pl.BlockSpec(memory_space=pl.ANY)


def _mm(a, b):
    return jnp.dot(a, b, preferred_element_type=F32)


def _mm_nt(a, b):
    return lax.dot_general(a, b, (((1,), (1,)), ((), ())), preferred_element_type=F32)


def _mm_tn(a, b):
    return lax.dot_general(a, b, (((0,), (0,)), ((), ())), preferred_element_type=F32)


def _mm_exact(a, b):
    return jnp.dot(a, b, preferred_element_type=F32, precision=lax.Precision.HIGHEST)


def _softplus(x):
    return jnp.maximum(x, 0.0) + jnp.log1p(jnp.exp(-jnp.abs(x)))


def _sigmoid(x):
    return jax.nn.sigmoid(x)


def _iota(shape, dim):
    return lax.broadcasted_iota(jnp.int32, shape, dim)


def _params(sem=None):
    return pltpu.CompilerParams(dimension_semantics=sem, vmem_limit_bytes=VMEM_LIMIT)


def _blk(n, pref):
    return min(n, pref)


def _const_spec(shape):
    nd = len(shape)
    return pl.BlockSpec(shape, lambda *_: (0,) * nd)


def _chip_peers():
    x, y, c = lax.axis_index("x"), lax.axis_index("y"), lax.axis_index("c")
    return x, y, c, [(1 - x, y, c), (x, 1 - y, c), (1 - x, 1 - y, c)]


def _half(rows, c):
    h = rows // 2
    return pl.ds(pl.multiple_of(c * h, 8), h)


def _sems(n):
    return [pltpu.SemaphoreType.DMA((n,)), pltpu.SemaphoreType.DMA((n,))]


def gather_weights(shards, conv_s):
    n = len(shards)

    def body(*refs):
        ins, conv_in = refs[:n], refs[n]
        outs, conv_out = refs[n + 1:2 * n + 1], refs[2 * n + 1]
        ssem1, rsem1, ssem2, rsem2, c_ssem, c_rsem = refs[2 * n + 2:]
        x, y, c, peers = _chip_peers()
        me = 2 * x + y
        sibling = (x, y, 1 - c)
        first, small = [], []
        for k, peer in enumerate(peers):
            for i in range(n):
                h = _half(ins[i].shape[0], c)
                first.append(pltpu.make_async_remote_copy(
                    src_ref=ins[i].at[h], dst_ref=outs[i].at[me, h], send_sem=ssem1.at[n * k + i],
                    recv_sem=rsem1.at[n * k + i], device_id=peer, device_id_type=MESH))
            small.append(pltpu.make_async_remote_copy(
                src_ref=conv_in, dst_ref=conv_out.at[me], send_sem=c_ssem.at[k], recv_sem=c_rsem.at[k],
                device_id=peer, device_id_type=MESH))
        for cp in first + small:
            cp.start()
        passed = []
        for k, peer in enumerate(peers):
            chip = 2 * peer[0] + peer[1]
            for i in range(n):
                h = _half(ins[i].shape[0], c)
                first[n * k + i].wait_recv()
                fwd = pltpu.make_async_remote_copy(
                    src_ref=outs[i].at[chip, h], dst_ref=outs[i].at[chip, h], send_sem=ssem2.at[n * k + i],
                    recv_sem=rsem2.at[n * k + i], device_id=sibling, device_id_type=MESH)
                fwd.start()
                passed.append(fwd)
        for cp in passed:
            cp.wait_recv()
        for cp in first + passed:
            cp.wait_send()
        for cp in small:
            cp.wait()

    return pl.pallas_call(
        body, name="gather_weights",
        out_shape=tuple(jax.ShapeDtypeStruct((N_CHIPS,) + a.shape, a.dtype) for a in list(shards) + [conv_s]),
        in_specs=[ANY] * (n + 1), out_specs=(ANY,) * (n + 1),
        scratch_shapes=_sems(3 * n) + _sems(3 * n) + _sems(3),
    )(*shards, conv_s)


def halves_to_sibling(gs):
    n = len(gs)

    def body(*refs):
        ins, outs = refs[:n], refs[n:2 * n]
        ssem, rsem = refs[2 * n:]
        x, y, c = lax.axis_index("x"), lax.axis_index("y"), lax.axis_index("c")
        copies = []
        for i in range(n):
            for j in range(N_CHIPS):
                copies.append(pltpu.make_async_remote_copy(
                    src_ref=ins[i].at[j, _half(ins[i].shape[1], 1 - c)], dst_ref=outs[i].at[j],
                    send_sem=ssem.at[N_CHIPS * i + j], recv_sem=rsem.at[N_CHIPS * i + j],
                    device_id=(x, y, 1 - c), device_id_type=MESH))
        for cp in copies:
            cp.start()
        for cp in copies:
            cp.wait()

    return pl.pallas_call(
        body, name="halves_to_sibling",
        out_shape=tuple(jax.ShapeDtypeStruct((N_CHIPS, g.shape[1] // 2, g.shape[2]), F32) for g in gs),
        in_specs=[ANY] * n, out_specs=(ANY,) * n, scratch_shapes=_sems(N_CHIPS * n),
    )(*gs)


RED_GRID = 8


def add_halves(core, gs, rbs):
    n = len(gs)

    def body(c_ref, *refs):
        for i in range(n):
            refs[2 * n + i][...] = (refs[i][...] + refs[n + i][...]).astype(BF16)

    def blk(g):
        return (1, g.shape[1] // 2 // RED_GRID, g.shape[2])

    grid_spec = pltpu.PrefetchScalarGridSpec(
        num_scalar_prefetch=1, grid=(N_CHIPS, RED_GRID),
        in_specs=([pl.BlockSpec(blk(g), lambda j, b, c_ref: (j, c_ref[0] * RED_GRID + b, 0)) for g in gs]
                  + [pl.BlockSpec(blk(g), lambda j, b, c_ref: (j, b, 0)) for g in gs]),
        out_specs=[pl.BlockSpec(blk(g), lambda j, b, c_ref: (j, b, 0)) for g in gs])
    return pl.pallas_call(
        body, name="add_halves", grid_spec=grid_spec,
        out_shape=tuple(jax.ShapeDtypeStruct(r.shape, BF16) for r in rbs),
        compiler_params=_params(("parallel", "parallel")),
    )(core, *gs, *rbs)


def scatter_copies(ins, outs, ssem, rsem, lsem):
    n = len(ins)
    x, y, _, peers = _chip_peers()
    me = 2 * x + y
    copies = [pltpu.make_async_copy(ins[i].at[me], outs[i].at[me], lsem.at[i]) for i in range(n)]
    for k, peer in enumerate(peers):
        dst_chip = 2 * peer[0] + peer[1]
        for i in range(n):
            copies.append(pltpu.make_async_remote_copy(
                src_ref=ins[i].at[dst_chip], dst_ref=outs[i].at[me], send_sem=ssem.at[n * k + i],
                recv_sem=rsem.at[n * k + i], device_id=peer, device_id_type=MESH))
    return copies


def gather_small(small):
    def body(s_ref, smalls_ref, ssem, rsem, lsem):
        x, y, c = lax.axis_index("x"), lax.axis_index("y"), lax.axis_index("c")
        dev = 4 * x + 2 * y + c
        copies = [pltpu.make_async_copy(s_ref, smalls_ref.at[dev], lsem)]
        for k in range(1, N_DEV):
            fx, fy, fc = (k >> 2) & 1, (k >> 1) & 1, k & 1
            peer = ((1 - x) if fx else x, (1 - y) if fy else y, (1 - c) if fc else c)
            copies.append(pltpu.make_async_remote_copy(
                src_ref=s_ref, dst_ref=smalls_ref.at[dev], send_sem=ssem.at[k - 1], recv_sem=rsem.at[k - 1],
                device_id=peer, device_id_type=MESH))
        for cp in copies:
            cp.start()
        for cp in copies:
            cp.wait()

    return pl.pallas_call(
        body, name="gather_small",
        out_shape=jax.ShapeDtypeStruct((N_DEV,) + small.shape, F32),
        in_specs=[ANY], out_specs=ANY,
        scratch_shapes=_sems(N_DEV - 1) + [pltpu.SemaphoreType.DMA],
    )(small)


def sum_parts(parts):
    n = len(parts)

    def body(*refs):
        for i in range(n):
            p_ref = refs[i]
            refs[n + i][...] = ((p_ref[0].astype(F32) + p_ref[1].astype(F32)) + p_ref[2].astype(F32)
                                ) + p_ref[3].astype(F32)

    def rows(p):
        return p.shape[1] // RED_GRID

    return pl.pallas_call(
        body, name="sum_parts",
        out_shape=tuple(jax.ShapeDtypeStruct(p.shape[1:], F32) for p in parts),
        grid=(RED_GRID,),
        in_specs=[pl.BlockSpec((N_CHIPS, rows(p), p.shape[2]), lambda b: (0, b, 0)) for p in parts],
        out_specs=tuple(pl.BlockSpec((rows(p), p.shape[2]), lambda b: (b, 0)) for p in parts),
        compiler_params=_params(("parallel",)),
    )(*parts)


def swap_halves(reds):
    n = len(reds)

    def body(*refs):
        ins, outs = refs[:n], refs[n:2 * n]
        ssem, rsem = refs[2 * n:]
        x, y, c = lax.axis_index("x"), lax.axis_index("y"), lax.axis_index("c")
        copies = [pltpu.make_async_remote_copy(
            src_ref=ins[i], dst_ref=outs[i], send_sem=ssem.at[i], recv_sem=rsem.at[i],
            device_id=(x, y, 1 - c), device_id_type=MESH) for i in range(n)]
        for cp in copies:
            cp.start()
        for cp in copies:
            cp.wait()

    return pl.pallas_call(
        body, name="swap_halves",
        out_shape=tuple(jax.ShapeDtypeStruct(r.shape, F32) for r in reds),
        in_specs=[ANY] * n, out_specs=(ANY,) * n, scratch_shapes=_sems(n),
    )(*reds)


def _adamw(w, g, m, v):
    m = ADAM_B1 * m + (1.0 - ADAM_B1) * g
    v = ADAM_B2 * v + (1.0 - ADAM_B2) * (g * g)
    m_hat = m / (1.0 - ADAM_B1 ** ADAM_STEP)
    v_hat = v / (1.0 - ADAM_B2 ** ADAM_STEP)
    delta = -ADAM_LR * (m_hat / (jnp.sqrt(v_hat) + ADAM_EPS) + ADAM_WD * w)
    return delta, m, v


def adamw_big(core, mine, theirs, ws, ms, vs):
    n = len(ws)
    per_half = RED_GRID // 2

    def body(c_ref, *refs):
        own = (pl.program_id(0) // per_half) == c_ref[0]
        for i in range(n):
            g = jnp.where(own, refs[i][...], refs[n + i][...])
            d, mn, vn = _adamw(refs[2 * n + i][...], g, refs[3 * n + i][...], refs[4 * n + i][...])
            refs[5 * n + i][...] = g
            refs[6 * n + i][...] = d
            refs[7 * n + i][...] = mn
            refs[8 * n + i][...] = vn

    def blk(w):
        return (w.shape[0] // RED_GRID, w.shape[1])

    halves = [pl.BlockSpec(blk(w), lambda b, c_ref: (b % per_half, 0)) for w in ws]
    whole = [pl.BlockSpec(blk(w), lambda b, c_ref: (b, 0)) for w in ws]
    shapes = [jax.ShapeDtypeStruct(w.shape, F32) for w in ws]
    grid_spec = pltpu.PrefetchScalarGridSpec(
        num_scalar_prefetch=1, grid=(RED_GRID,), in_specs=halves * 2 + whole * 3, out_specs=whole * 4)
    outs = pl.pallas_call(
        body, name="adamw_big", out_shape=tuple(shapes * 4), grid_spec=grid_spec,
        compiler_params=_params(("parallel",)),
    )(core, *mine, *theirs, *ws, *ms, *vs)
    return outs[:n], outs[n:2 * n], outs[2 * n:3 * n], outs[3 * n:]


def adamw_whole(g, w, m, v, name):
    def body(g_ref, w_ref, m_ref, v_ref, d_out, m_out, v_out):
        d, mn, vn = _adamw(w_ref[...], g_ref[...], m_ref[...], v_ref[...])
        d_out[...] = d
        m_out[...] = mn
        v_out[...] = vn

    shp = jax.ShapeDtypeStruct(g.shape, F32)
    return pl.pallas_call(body, name=name, out_shape=(shp,) * 3)(g, w, m, v)


def adamw_small(smalls, w, m, v):
    def body(s_ref, w_ref, m_ref, v_ref, g_out, d_out, m_out, v_out):
        g = s_ref[0]
        for k in range(1, N_DEV):
            g = g + s_ref[k]
        d, mn, vn = _adamw(w_ref[...], g, m_ref[...], v_ref[...])
        g_out[...] = g
        d_out[...] = d
        m_out[...] = mn
        v_out[...] = vn

    shp = jax.ShapeDtypeStruct((SMALL_ROWS, LANES), F32)
    return pl.pallas_call(body, name="adamw_small", out_shape=(shp,) * 4)(smalls, w, m, v)


def rms_prenorm(x, g):
    s = x.shape[0]
    tm = _blk(s, 512)

    def body(x_ref, g_ref, u_ref):
        xv = x_ref[...]
        r = lax.rsqrt(jnp.mean(xv * xv, axis=-1, keepdims=True) + EPS)
        u_ref[...] = (xv * r * g_ref[...]).astype(BF16)

    return pl.pallas_call(
        body, name="rms_prenorm", out_shape=jax.ShapeDtypeStruct(x.shape, BF16), grid=(s // tm,),
        in_specs=[pl.BlockSpec((tm, D_MODEL), lambda i: (i, 0)), _const_spec((1, D_MODEL))],
        out_specs=pl.BlockSpec((tm, D_MODEL), lambda i: (i, 0)), compiler_params=_params(("parallel",)),
    )(x, g)


def matmul_rows(a, w, out_dtype, name):
    s, k = a.shape
    n = w.shape[1]
    tm = _blk(s, 512)

    def body(a_ref, w_ref, o_ref):
        o_ref[...] = _mm(a_ref[...], w_ref[...]).astype(out_dtype)

    return pl.pallas_call(
        body, name=name, out_shape=jax.ShapeDtypeStruct((s, n), out_dtype), grid=(s // tm,),
        in_specs=[pl.BlockSpec((tm, k), lambda i: (i, 0)), _const_spec((k, n))],
        out_specs=pl.BlockSpec((tm, n), lambda i: (i, 0)), compiler_params=_params(("parallel",)),
    )(a, w)


def matmul_tn(a, b, name):
    s, m = a.shape
    n = b.shape[1]
    tk = _blk(s, 2048)
    tn = _blk(n, 512)

    def body(a_ref, b_ref, o_ref):
        @pl.when(pl.program_id(1) == 0)
        def _():
            o_ref[...] = jnp.zeros_like(o_ref)

        o_ref[...] += _mm_tn(a_ref[...], b_ref[...])

    return pl.pallas_call(
        body, name=name, out_shape=jax.ShapeDtypeStruct((m, n), F32), grid=(n // tn, s // tk),
        in_specs=[pl.BlockSpec((tk, m), lambda j, i: (i, 0)), pl.BlockSpec((tk, tn), lambda j, i: (i, j))],
        out_specs=pl.BlockSpec((m, tn), lambda j, i: (0, j)),
        compiler_params=_params(("parallel", "arbitrary")),
    )(a, b)


def conv_fwd(xbc, w, b):
    s = xbc.shape[0]
    tm = _blk(s, 256)

    def body(x_ref, t_ref, w_ref, b_ref, pre_ref, act_ref):
        i = pl.program_id(0)
        cur = x_ref[...]
        tail = jnp.where(i > 0, t_ref[...], 0.0)
        wv = w_ref[...]
        acc = cur * wv[3:4, :] + b_ref[...]
        head = cur[0:8, :] * wv[3:4, :] + b_ref[...]
        row8 = _iota((8, CONV_CH), 0)
        for sh in range(1, CONV_WIDTH):
            wk = wv[3 - sh:4 - sh, :]
            acc = acc + pltpu.roll(cur, sh, 0) * wk
            first = jnp.where(row8 < sh, pltpu.roll(tail, sh, 0), pltpu.roll(cur[0:8, :], sh, 0))
            head = head + first * wk
        pre_ref[...] = acc
        act_ref[...] = acc * _sigmoid(acc)
        pre_ref[0:8, :] = head
        act_ref[0:8, :] = head * _sigmoid(head)

    shp = jax.ShapeDtypeStruct(xbc.shape, F32)
    rows = pl.BlockSpec((tm, CONV_CH), lambda i: (i, 0))
    return pl.pallas_call(
        body, name="conv_fwd", out_shape=(shp, shp), grid=(s // tm,),
        in_specs=[rows, pl.BlockSpec((8, CONV_CH), lambda i: (jnp.maximum(i * (tm // 8) - 1, 0), 0)),
                  _const_spec((CONV_WIDTH, CONV_CH)), _const_spec((1, CONV_CH))],
        out_specs=(rows, rows), compiler_params=_params(("parallel",)),
    )(xbc, xbc, w, b)


def conv_bwd(xbc, pre, dact, w):
    s = xbc.shape[0]
    tm = _blk(s, 256)
    nb = s // tm

    def dsilu(p):
        sg = _sigmoid(p)
        return sg * (1.0 + p * (1.0 - sg))

    def body(x_ref, xt_ref, p_ref, pn_ref, d_ref, dn_ref, w_ref, dx_ref, dw_ref, db_ref):
        i = pl.program_id(0)

        @pl.when(i == 0)
        def _():
            dw_ref[...] = jnp.zeros_like(dw_ref)
            db_ref[...] = jnp.zeros_like(db_ref)

        wv = w_ref[...]
        dpre = d_ref[...] * dsilu(p_ref[...])
        dnext = jnp.where(i < nb - 1, dn_ref[...] * dsilu(pn_ref[...]), 0.0)
        cur = x_ref[...]
        tail = jnp.where(i > 0, xt_ref[...], 0.0)
        row8 = _iota((8, CONV_CH), 0)
        dx = dpre * wv[3:4, :]
        last = dpre[tm - 8:tm, :] * wv[3:4, :]
        db_ref[...] += jnp.sum(dpre, axis=0, keepdims=True)
        dws = [jnp.sum(dpre * cur, axis=0, keepdims=True)]
        for sh in range(1, CONV_WIDTH):
            wk = wv[3 - sh:4 - sh, :]
            dx = dx + pltpu.roll(dpre, tm - sh, 0) * wk
            nxt = jnp.where(row8 >= 8 - sh, pltpu.roll(dnext, 8 - sh, 0), pltpu.roll(dpre[tm - 8:tm, :], 8 - sh, 0))
            last = last + nxt * wk
            xs = pltpu.roll(cur, sh, 0)
            first = jnp.where(row8 < sh, pltpu.roll(tail, sh, 0), xs[0:8, :])
            dws.append(jnp.sum(dpre * xs, axis=0, keepdims=True)
                       + jnp.sum(dpre[0:8, :] * (first - xs[0:8, :]), axis=0, keepdims=True))
        dx_ref[...] = dx.astype(BF16)
        dx_ref[tm - 8:tm, :] = last.astype(BF16)
        for sh in range(CONV_WIDTH):
            dw_ref[3 - sh:4 - sh, :] += dws[sh]

    rows = pl.BlockSpec((tm, CONV_CH), lambda i: (i, 0))
    prev8 = pl.BlockSpec((8, CONV_CH), lambda i: (jnp.maximum(i * (tm // 8) - 1, 0), 0))
    next8 = pl.BlockSpec((8, CONV_CH), lambda i: (jnp.minimum((i + 1) * (tm // 8), s // 8 - 1), 0))
    return pl.pallas_call(
        body, name="conv_bwd",
        out_shape=(jax.ShapeDtypeStruct(xbc.shape, BF16), jax.ShapeDtypeStruct((8, CONV_CH), F32),
                   jax.ShapeDtypeStruct((1, CONV_CH), F32)),
        grid=(nb,),
        in_specs=[rows, prev8, rows, next8, rows, next8, _const_spec((CONV_WIDTH, CONV_CH))],
        out_specs=(rows, _const_spec((8, CONV_CH)), _const_spec((1, CONV_CH))),
        compiler_params=_params(("arbitrary",)),
    )(xbc, xbc, pre, pre, dact, dact, w)


def _pair_lanes(mat, j, lane):
    return jnp.where(lane < HEAD_DIM, mat[:, 2 * j:2 * j + 1], mat[:, 2 * j + 1:2 * j + 2])


def _ssd_chunk_prelude(sm, dtb, a_row, lane, sub):
    raw = sm + dtb
    head_lane = lane < N_HEADS
    dt = jnp.where(head_lane, _softplus(raw), 0.0)
    sig = jnp.where(head_lane, _sigmoid(raw), 0.0)
    tri = (lane <= sub).astype(F32)
    acs = _mm_exact(tri, dt * a_row)
    return dt, sig, acs, acs.T


GROUP_WIDTH = SSD_WIDTH // N_GROUPS
HEADS_PER_GROUP = N_HEADS // N_GROUPS


def _expand_group(mat, g, lane):
    return jnp.concatenate([_pair_lanes(mat, j, lane) for j in range(4 * g, 4 * g + 4)], axis=1)


def _head_sums(q, g):
    row = _iota((GROUP_WIDTH, LANES), 0)
    seg = (_iota((GROUP_WIDTH, LANES), 1) == HEADS_PER_GROUP * g + (row >> 6)).astype(BF16)
    hi = q.astype(BF16)
    lo = (q - hi.astype(F32)).astype(BF16)
    return _mm(hi, seg) + _mm(lo, seg)


def _rows_from_lanes(row512):
    return jnp.broadcast_to(row512, (LANES, GROUP_WIDTH)).T


def ssd_fwd(xc, small, dtb_row, a_row, dskip_lane):
    s = xc.shape[0]
    nc = s // CHUNK

    def body(xc_ref, sm_ref, dtb_ref, a_ref, dsk_ref, y_ref, hs_ref, h_scr):
        c = pl.program_id(0)

        @pl.when(c == 0)
        def _():
            h_scr[...] = jnp.zeros_like(h_scr)

        lane = _iota((CHUNK, LANES), 1)
        sub = _iota((CHUNK, LANES), 0)
        causal = lane <= sub
        dt, _, acs, acs_t = _ssd_chunk_prelude(sm_ref[...], dtb_ref[...], a_ref[...], lane, sub)
        for g in range(N_GROUPS):
            cols = slice(GROUP_WIDTH * g, GROUP_WIDTH * (g + 1))
            b_off = SSD_WIDTH + D_STATE * g
            c_off = SSD_WIDTH + N_GROUPS * D_STATE + D_STATE * g
            b_b = xc_ref[:, b_off:b_off + D_STATE].astype(BF16)
            c_b = xc_ref[:, c_off:c_off + D_STATE].astype(BF16)
            cb = _mm_nt(c_b, b_b)
            x_g = xc_ref[:, cols]
            acs_g = _expand_group(acs, g, lane)
            xdt_g = x_g * _expand_group(dt, g, lane)
            xdt_b = xdt_g.astype(BF16)
            heads = range(HEADS_PER_GROUP * g, HEADS_PER_GROUP * (g + 1))
            m_b = [(cb * jnp.exp(jnp.where(causal, acs[:, h:h + 1] - acs_t[h:h + 1, :], NEG_BIG))).astype(BF16)
                   for h in heads]
            yd = [_mm(m_b[k], xdt_b[:, LANES * (k // 2):LANES * (k // 2 + 1)]) for k in range(HEADS_PER_GROUP)]
            yd_g = jnp.concatenate([jnp.where(lane < HEAD_DIM, yd[2 * k], yd[2 * k + 1]) for k in range(4)], axis=1)
            h_g = h_scr[g]
            t_g = _mm_nt(c_b, h_g.astype(BF16))
            y_ref[:, cols] = yd_g + jnp.exp(acs_g) * t_g + dsk_ref[:, cols] * x_g
            hs_ref[0, g] = h_g
            last_g = acs_g[CHUNK - 1:CHUNK, :]
            w_b = (xdt_g * jnp.exp(last_g - acs_g)).astype(BF16)
            h_scr[g] = h_g * jnp.exp(_rows_from_lanes(last_g)) + _mm_tn(w_b, b_b)

    return pl.pallas_call(
        body, name="ssd_fwd",
        out_shape=(jax.ShapeDtypeStruct((s, SSD_WIDTH), F32),
                   jax.ShapeDtypeStruct((nc, N_GROUPS, GROUP_WIDTH, D_STATE), F32)),
        grid=(nc,),
        in_specs=[pl.BlockSpec((CHUNK, CONV_CH), lambda c: (c, 0)), pl.BlockSpec((CHUNK, LANES), lambda c: (c, 0)),
                  _const_spec((1, LANES)), _const_spec((1, LANES)), _const_spec((1, SSD_WIDTH))],
        out_specs=(pl.BlockSpec((CHUNK, SSD_WIDTH), lambda c: (c, 0)),
                   pl.BlockSpec((1, N_GROUPS, GROUP_WIDTH, D_STATE), lambda c: (c, 0, 0, 0))),
        scratch_shapes=[pltpu.VMEM((N_GROUPS, GROUP_WIDTH, D_STATE), F32)],
        compiler_params=_params(("arbitrary",)),
    )(xc, small, dtb_row, a_row, dskip_lane)


def ssd_bwd(xc, small, states, dy, dtb_row, a_row, dskip_lane):
    s = xc.shape[0]
    nc = s // CHUNK
    rev = lambda c: nc - 1 - c

    def body(xc_ref, sm_ref, hs_ref, dy_ref, dtb_ref, a_ref, dsk_ref,
             dxc_ref, ddt_ref, da_ref, ddtb_ref, ddsk_ref, dh_scr):
        c = pl.program_id(0)

        @pl.when(c == 0)
        def _():
            dh_scr[...] = jnp.zeros_like(dh_scr)
            da_ref[...] = jnp.zeros_like(da_ref)
            ddtb_ref[...] = jnp.zeros_like(ddtb_ref)
            ddsk_ref[...] = jnp.zeros_like(ddsk_ref)

        lane = _iota((CHUNK, LANES), 1)
        sub = _iota((CHUNK, LANES), 0)
        causal = lane <= sub
        upper = lane >= sub
        is_last = sub == CHUNK - 1
        a_row_v = a_ref[...]
        dt, sig, acs, acs_t = _ssd_chunk_prelude(sm_ref[...], dtb_ref[...], a_row_v, lane, sub)
        cd = jnp.exp(acs[CHUNK - 1:CHUNK, :])
        dacs_c = jnp.zeros((CHUNK, LANES), F32)
        dacs_r = jnp.zeros((LANES, CHUNK), F32)
        ddtx = jnp.zeros((CHUNK, LANES), F32)
        for g in range(N_GROUPS):
            cols = slice(GROUP_WIDTH * g, GROUP_WIDTH * (g + 1))
            b_off = SSD_WIDTH + D_STATE * g
            c_off = SSD_WIDTH + N_GROUPS * D_STATE + D_STATE * g
            b_b = xc_ref[:, b_off:b_off + D_STATE].astype(BF16)
            c_b = xc_ref[:, c_off:c_off + D_STATE].astype(BF16)
            cb = _mm_nt(c_b, b_b)
            cb_t = _mm_nt(b_b, c_b)
            x_g = xc_ref[:, cols]
            dy_g = dy_ref[:, cols]
            dt_g = _expand_group(dt, g, lane)
            acs_g = _expand_group(acs, g, lane)
            last_g = acs_g[CHUNK - 1:CHUNK, :]
            e_g = jnp.exp(acs_g)
            dte_g = jnp.exp(last_g - acs_g)
            xdt_g = x_g * dt_g
            xdt_b = xdt_g.astype(BF16)
            h_g = hs_ref[0, g]
            dh_g = dh_scr[g]
            h_b = h_g.astype(BF16)
            dh_b = dh_g.astype(BF16)
            heads = list(range(HEADS_PER_GROUP * g, HEADS_PER_GROUP * (g + 1)))
            segs = [acs[:, h:h + 1] - acs_t[h:h + 1, :] for h in heads]
            lms = [jnp.exp(jnp.where(causal, sg, NEG_BIG)) for sg in segs]
            mts = [(cb_t * jnp.exp(jnp.where(upper, -sg, NEG_BIG))).astype(BF16) for sg in segs]
            dyh = []
            for k in range(HEADS_PER_GROUP):
                blk = dy_g[:, LANES * (k // 2):LANES * (k // 2 + 1)]
                in_head = (lane < HEAD_DIM) if k % 2 == 0 else (lane >= HEAD_DIM)
                dyh.append(jnp.where(in_head, blk, 0.0).astype(BF16))
            dms = [_mm_nt(dyh[k], xdt_b[:, LANES * (k // 2):LANES * (k // 2 + 1)]) for k in range(HEADS_PER_GROUP)]
            dxs = [_mm(mts[k], dyh[k]) for k in range(HEADS_PER_GROUP)]
            dcb = jnp.zeros((CHUNK, CHUNK), F32)
            for k, h in enumerate(heads):
                gmat = dms[k] * (cb * lms[k])
                dacs_c = dacs_c + jnp.where(lane == h, jnp.sum(gmat, axis=1, keepdims=True), 0.0)
                dacs_r = dacs_r - jnp.where(sub == h, jnp.sum(gmat, axis=0, keepdims=True), 0.0)
                dcb = dcb + dms[k] * lms[k]
            dxdt_g = jnp.concatenate([dxs[2 * k] + dxs[2 * k + 1] for k in range(4)], axis=1)
            t_g = _mm_nt(c_b, h_b)
            dacs_c = dacs_c + _head_sums(dy_g * e_g * t_g, g)
            dt_b = (dy_g * e_g).astype(BF16)
            dc_acc = _mm(dt_b, h_b)
            dh_prev = _mm_tn(dt_b, c_b)
            dw_g = _mm_nt(b_b, dh_b)
            w_g = xdt_g * dte_g
            dxdt_g = dxdt_g + dw_g * dte_g
            db_acc = _mm(w_g.astype(BF16), dh_b)
            r2 = _head_sums(dw_g * w_g, g)
            dacs_c = dacs_c + jnp.where(is_last, jnp.sum(r2, axis=0, keepdims=True), 0.0) - r2
            q3 = jnp.sum(dh_g * h_g, axis=1, keepdims=True)
            for k, h in enumerate(heads):
                tot = jnp.sum(q3[HEAD_DIM * k:HEAD_DIM * (k + 1), :], keepdims=True) * cd[:, h:h + 1]
                dacs_c = dacs_c + jnp.where(is_last & (lane == h), tot, 0.0)
            dh_scr[g] = dh_prev + dh_g * jnp.exp(_rows_from_lanes(last_g))
            dxc_ref[:, cols] = dxdt_g * dt_g + dsk_ref[:, cols] * dy_g
            ddtx = ddtx + _head_sums(dxdt_g * x_g, g)
            ddsk_ref[:, cols] += jnp.sum(dy_g * x_g, axis=0, keepdims=True)
            dxc_ref[:, b_off:b_off + D_STATE] = db_acc + _mm(dcb.T.astype(BF16), c_b)
            dxc_ref[:, c_off:c_off + D_STATE] = dc_acc + _mm(dcb.astype(BF16), b_b)
        dacs = dacs_c + dacs_r.T
        dadt = _mm_exact((lane >= sub).astype(F32), dacs)
        ddt = dadt * a_row_v + ddtx
        ddt_raw = ddt * sig
        ddt_ref[...] = ddt_raw
        da_ref[...] += jnp.sum(dadt * dt, axis=0, keepdims=True)
        ddtb_ref[...] += jnp.sum(ddt_raw, axis=0, keepdims=True)

    return pl.pallas_call(
        body, name="ssd_bwd",
        out_shape=(jax.ShapeDtypeStruct((s, CONV_CH), F32), jax.ShapeDtypeStruct((s, LANES), F32),
                   jax.ShapeDtypeStruct((1, LANES), F32), jax.ShapeDtypeStruct((1, LANES), F32),
                   jax.ShapeDtypeStruct((1, SSD_WIDTH), F32)),
        grid=(nc,),
        in_specs=[pl.BlockSpec((CHUNK, CONV_CH), lambda c: (rev(c), 0)),
                  pl.BlockSpec((CHUNK, LANES), lambda c: (rev(c), 0)),
                  pl.BlockSpec((1, N_GROUPS, GROUP_WIDTH, D_STATE), lambda c: (rev(c), 0, 0, 0)),
                  pl.BlockSpec((CHUNK, SSD_WIDTH), lambda c: (rev(c), 0)),
                  _const_spec((1, LANES)), _const_spec((1, LANES)), _const_spec((1, SSD_WIDTH))],
        out_specs=(pl.BlockSpec((CHUNK, CONV_CH), lambda c: (rev(c), 0)),
                   pl.BlockSpec((CHUNK, LANES), lambda c: (rev(c), 0)),
                   _const_spec((1, LANES)), _const_spec((1, LANES)), _const_spec((1, SSD_WIDTH))),
        scratch_shapes=[pltpu.VMEM((N_GROUPS, GROUP_WIDTH, D_STATE), F32)],
        compiler_params=_params(("arbitrary",)),
    )(xc, small, states, dy, dtb_row, a_row, dskip_lane)


FORGET_BLOCK = 512


def forget_cumsum(small, fgb_row):
    s = small.shape[0]
    t = _blk(s, FORGET_BLOCK)
    nb = s // t

    def body(sm_ref, b_ref, cc_ref, carry):
        i = pl.program_id(0)

        @pl.when(i == 0)
        def _():
            carry[...] = jnp.zeros_like(carry)

        lane = _iota((t, LANES), 1)
        in_f = (lane >= N_HEADS) & (lane < 2 * N_HEADS)
        logf = jnp.where(in_f, -_softplus(-(sm_ref[...] + b_ref[...])), 0.0)
        tri = (_iota((t, t), 1) <= _iota((t, t), 0)).astype(F32)
        cum = _mm_exact(tri, logf) + carry[0:1, :]
        cc_ref[...] = cum
        carry[...] = jnp.broadcast_to(cum[t - 1:t, :], (8, LANES))

    return pl.pallas_call(
        body, name="forget_cumsum",
        out_shape=jax.ShapeDtypeStruct((s, LANES), F32),
        grid=(nb,),
        in_specs=[pl.BlockSpec((t, LANES), lambda i: (i, 0)), _const_spec((1, LANES))],
        out_specs=pl.BlockSpec((t, LANES), lambda i: (i, 0)),
        scratch_shapes=[pltpu.VMEM((8, LANES), F32)],
        compiler_params=_params(("arbitrary",)),
    )(small, fgb_row)


def forget_bwd(dc, small, ddt_raw, fgb_row):
    s = small.shape[0]
    t = _blk(s, FORGET_BLOCK)
    nb = s // t
    rev = lambda i: nb - 1 - i

    def body(dc_ref, sm_ref, ddt_ref, b_ref, ds_ref, dfb_ref, carry):
        i = pl.program_id(0)

        @pl.when(i == 0)
        def _():
            carry[...] = jnp.zeros_like(carry)
            dfb_ref[...] = jnp.zeros_like(dfb_ref)

        lane = _iota((t, LANES), 1)
        rows = dc_ref[...].T
        tri = (_iota((t, t), 1) <= _iota((t, t), 0)).astype(F32)
        rc = _mm_exact(rows, tri) + carry[:, 0:1]
        carry[...] = jnp.broadcast_to(rc[:, 0:1], (LANES, LANES))
        in_f = (lane >= N_HEADS) & (lane < 2 * N_HEADS)
        df = jnp.where(in_f, rc.T * _sigmoid(-(sm_ref[...] + b_ref[...])), 0.0)
        ds_ref[...] = (df + ddt_ref[...]).astype(BF16)
        dfb_ref[...] += jnp.sum(df, axis=0, keepdims=True)

    blk = pl.BlockSpec((t, LANES), lambda i: (rev(i), 0))
    return pl.pallas_call(
        body, name="forget_bwd",
        out_shape=(jax.ShapeDtypeStruct((s, LANES), BF16), jax.ShapeDtypeStruct((1, LANES), F32)),
        grid=(nb,),
        in_specs=[blk, blk, blk, _const_spec((1, LANES))],
        out_specs=(blk, _const_spec((1, LANES))),
        scratch_shapes=[pltpu.VMEM((LANES, LANES), F32)],
        compiler_params=_params(("arbitrary",)),
    )(dc, small, ddt_raw, fgb_row)


ATT_BLOCK = 512
ATT_SCALE = HEAD_DIM ** -0.5
AUG_A = HEAD_DIM
AUG_B = HEAD_DIM + 3


def _split3(c):
    hi = c.astype(BF16).astype(F32)
    r = c - hi
    mid = r.astype(BF16).astype(F32)
    return hi, mid, (r - mid).astype(BF16).astype(F32)


def _aug(lane, first, parts=None, value=1.0):
    if parts is None:
        return jnp.where((lane >= first) & (lane < first + 3), value, 0.0)
    return (jnp.where(lane == first, parts[0], 0.0) + jnp.where(lane == first + 1, parts[1], 0.0)
            + jnp.where(lane == first + 2, parts[2], 0.0))


def _pack_pair(a0, a1, lane):
    return jnp.where(lane < HEAD_DIM, a0, pltpu.roll(a1, HEAD_DIM, 1))


def proj_qkv_heads(u, w_q, w_k, w_v, cum):
    s = u.shape[0]
    tm = _blk(s, 256)

    def body(u_ref, wq_ref, wk_ref, wv_ref, c_ref, qa_ref, ka_ref, va_ref, nrm_ref):
        lane = _iota((tm, LANES), 1)
        lo = lane < HEAD_DIM
        uv = u_ref[...]
        qf = _mm(uv, wq_ref[...]) * ATT_SCALE
        kf = _mm(uv, wk_ref[...])
        vf = _mm(uv, wv_ref[...])
        cc = c_ref[...]
        ones_a = _aug(lane, AUG_A)
        ones_b = _aug(lane, AUG_B)
        sub8 = _iota((8, LANES), 0)
        nrm = jnp.zeros((8, LANES), F32)
        for h in range(N_HEADS):
            j, e = divmod(h, 2)

            def head(full):
                blk = full[:, LANES * j:LANES * (j + 1)]
                if e == 1:
                    blk = pltpu.roll(blk, HEAD_DIM, 1)
                return jnp.where(lo, blk, 0.0)

            parts = _split3(cc[:, N_HEADS + h:N_HEADS + h + 1])
            qh, kh = head(qf), head(kf)
            qa_ref[h] = (qh + _aug(lane, AUG_A, parts) + ones_b).astype(BF16)
            ka_ref[h] = (kh + ones_a - _aug(lane, AUG_B, parts)).astype(BF16)
            va_ref[h] = (head(vf) + ones_a).astype(BF16)
        seg = (_iota((ATT_WIDTH, LANES), 1) == (_iota((ATT_WIDTH, LANES), 0) >> 6)).astype(BF16)
        for r, val in enumerate((qf, kf)):
            sq = val * val
            hi = sq.astype(BF16)
            tot = _mm(hi, seg) + _mm((sq - hi.astype(F32)).astype(BF16), seg)
            nrm = nrm + jnp.where(sub8 == r, jnp.max(tot, axis=0, keepdims=True), 0.0)
        nrm_ref[0] = nrm

    shp = jax.ShapeDtypeStruct((N_HEADS, s, LANES), BF16)
    hspec = pl.BlockSpec((N_HEADS, tm, LANES), lambda i: (0, i, 0))
    wspec = _const_spec((D_MODEL, ATT_WIDTH))
    return pl.pallas_call(
        body, name="proj_qkv_heads",
        out_shape=(shp, shp, shp, jax.ShapeDtypeStruct((s // tm, 8, LANES), F32)), grid=(s // tm,),
        in_specs=[pl.BlockSpec((tm, D_MODEL), lambda i: (i, 0)), wspec, wspec, wspec,
                  pl.BlockSpec((tm, LANES), lambda i: (i, 0))],
        out_specs=(hspec, hspec, hspec, pl.BlockSpec((1, 8, LANES), lambda i: (i, 0, 0))),
        compiler_params=_params(("parallel",)),
    )(u, w_q, w_k, w_v, cum)


SKIP_BELOW = -110.0


def live_blocks(norms, cum, t):
    qn = jnp.sqrt(jnp.max(norms[:, 0, :N_HEADS], axis=0))
    kn = jnp.sqrt(jnp.max(norms[:, 1, :N_HEADS], axis=0))
    bound = 2.05 * qn * kn + 2.0
    c_first = cum[0::t, N_HEADS:2 * N_HEADS]
    c_last = cum[t - 1::t, N_HEADS:2 * N_HEADS]
    nq = c_first.shape[0]
    top = bound[None, None, :] + c_first[:, None, :] - c_last[None, :, :]
    below = jnp.arange(nq)[None, :] < jnp.arange(nq)[:, None]
    dead = below[:, :, None] & ~(top >= SKIP_BELOW)
    first = jnp.sum(dead, axis=1).astype(jnp.int32).T
    last_q = jnp.sum(first[:, None, :] <= jnp.arange(nq)[None, :, None], axis=2).astype(jnp.int32) - 1
    return first, last_q


def attention_fwd(first, qa, ka, va):
    s = qa.shape[1]
    t = _blk(s, ATT_BLOCK)
    nq = s // t

    def body(first_ref, qa_ref, ka_ref, va_ref, o_ref, qb_ref, m_scr, acc_scr, alpha_scr, p_scr, s_scr):
        qi = pl.program_id(1)
        starts = [first_ref[2 * pl.program_id(0) + e, qi] for e in range(2)]
        k0 = jnp.maximum(starts[0], starts[1])
        m_scr[...] = jnp.full_like(m_scr, NEG_BIG)
        acc_scr[...] = jnp.zeros_like(acc_scr)

        def kv_rows(kb):
            return pl.ds(pl.multiple_of(kb * t, t), t)

        def logits(kb, masked, heads=(0, 1)):
            for e in heads:
                sc = _mm_nt(qa_ref[e], ka_ref[e, kv_rows(kb), :])
                if masked:
                    sc = jnp.where(_iota((t, t), 0) >= _iota((t, t), 1), sc, NEG_BIG)
                s_scr[e] = sc

        def probs(heads=(0, 1)):
            for e in heads:
                cmax = s_scr[e, :, 0:LANES]
                for c in range(1, t // LANES):
                    cmax = jnp.maximum(cmax, s_scr[e, :, LANES * c:LANES * (c + 1)])
                m_old = m_scr[e]
                m_new = jnp.maximum(m_old, jnp.max(cmax, axis=1, keepdims=True))
                alpha_scr[e] = jnp.exp(m_old - m_new)
                m_scr[e] = m_new
                for c in range(t // LANES):
                    cols = slice(LANES * c, LANES * (c + 1))
                    p_scr[e, :, cols] = jnp.exp(s_scr[e, :, cols] - m_new).astype(BF16)

        def accumulate(kb, heads=(0, 1)):
            for e in heads:
                acc_scr[e] = alpha_scr[e] * acc_scr[e] + _mm(p_scr[e], va_ref[e, kv_rows(kb), :])

        for e in range(2):
            def alone(kb, carry, e=e):
                logits(kb, False, (e,))
                probs((e,))
                accumulate(kb, (e,))
                return carry

            lax.fori_loop(starts[e], k0, alone, 0)

        def loop_body(kb, carry):
            logits(kb, False)
            for e in range(2):
                accumulate(kb - 1, (e,))
                probs((e,))
            return carry

        @pl.when(qi > k0)
        def _():
            logits(k0, False)
            probs()

        lax.fori_loop(k0 + 1, qi, loop_body, 0)

        @pl.when(qi > k0)
        def _():
            logits(qi, True)
            accumulate(qi - 1)
            probs()

        @pl.when(qi == k0)
        def _():
            logits(qi, True)
            probs()

        accumulate(qi)

        lane = _iota((t, LANES), 1)
        outs = []
        for e in range(2):
            acc = acc_scr[e]
            l = acc[:, AUG_A:AUG_A + 1]
            outs.append(acc / l)
            lse = m_scr[e][:, 0:1] + jnp.log(l)
            q32 = qa_ref[e].astype(F32)
            c = q32[:, AUG_A:AUG_A + 1] + q32[:, AUG_A + 1:AUG_A + 2] + q32[:, AUG_A + 2:AUG_A + 3]
            qb = jnp.where(lane < HEAD_DIM, q32, 0.0) + _aug(lane, AUG_A, _split3(c - lse)) + _aug(lane, AUG_B)
            qb_ref[e] = qb.astype(BF16)
        o_ref[...] = _pack_pair(outs[0], outs[1], lane)

    grid_spec = pltpu.PrefetchScalarGridSpec(
        num_scalar_prefetch=1, grid=(N_PAIRS, nq),
        in_specs=[pl.BlockSpec((2, t, LANES), lambda j, qi, f: (j, qi, 0)),
                  pl.BlockSpec((2, s, LANES), lambda j, qi, f: (j, 0, 0)),
                  pl.BlockSpec((2, s, LANES), lambda j, qi, f: (j, 0, 0))],
        out_specs=[pl.BlockSpec((t, LANES), lambda j, qi, f: (qi, j)),
                   pl.BlockSpec((2, t, LANES), lambda j, qi, f: (j, qi, 0))],
        scratch_shapes=[pltpu.VMEM((2, t, LANES), F32), pltpu.VMEM((2, t, LANES), F32),
                        pltpu.VMEM((2, t, LANES), F32), pltpu.VMEM((2, t, t), BF16), pltpu.VMEM((2, t, t), F32)])
    return pl.pallas_call(
        body, name="attention_fwd", grid_spec=grid_spec,
        out_shape=(jax.ShapeDtypeStruct((s, ATT_WIDTH), F32), jax.ShapeDtypeStruct((N_HEADS, s, LANES), BF16)),
        compiler_params=_params(("parallel", "parallel")),
    )(first, qa, ka, va)


def attention_bwd(last_q, qb, ka, va, dob):
    s = qb.shape[1]
    t = _blk(s, ATT_BLOCK)
    nq = s // t

    def body(last_ref, qb_ref, dob_ref, ka_ref, va_ref, dq_ref, dk_ref, dv_ref, dc_ref, dq_scr, dk_scr, dv_scr):
        j, ki = pl.program_id(0), pl.program_id(1)

        @pl.when((j == 0) & (ki == 0))
        def _():
            dc_ref[...] = jnp.zeros_like(dc_ref)

        @pl.when(ki == 0)
        def _():
            dq_scr[...] = jnp.zeros_like(dq_scr)

        dk_scr[...] = jnp.zeros_like(dk_scr)
        dv_scr[...] = jnp.zeros_like(dv_scr)

        def q_step(qblk, masked, heads=(0, 1)):
            rows = pl.ds(pl.multiple_of(qblk * t, t), t)
            scs = [_mm_nt(qb_ref[e, rows, :], ka_ref[e]) for e in heads]
            dps = [_mm_nt(dob_ref[e, rows, :], va_ref[e]) for e in heads]
            for e, sc, dp in zip(heads, scs, dps):
                q = qb_ref[e, rows, :]
                do = dob_ref[e, rows, :]
                if masked:
                    sc = jnp.where(_iota((t, t), 0) >= _iota((t, t), 1), sc, NEG_BIG)
                p = jnp.exp(sc)
                ds_b = (p * dp).astype(BF16)
                dv_scr[e] += _mm_tn(p.astype(BF16), do)
                dk_scr[e] += _mm_tn(ds_b, q)
                dq_scr[e, rows, :] += _mm(ds_b, ka_ref[e])

        def loop_body(qblk, carry):
            q_step(qblk, False)
            return carry

        ends = [last_ref[2 * j + e, ki] + 1 for e in range(2)]
        both = jnp.minimum(ends[0], ends[1])
        q_step(ki, True)
        lax.fori_loop(ki + 1, both, loop_body, 0)
        for e in range(2):
            def alone(qblk, carry, e=e):
                q_step(qblk, False, (e,))
                return carry

            lax.fori_loop(both, ends[e], alone, 0)

        lane = _iota((t, LANES), 1)
        dk_ref[...] = _pack_pair(dk_scr[0], dk_scr[1], lane).astype(BF16)
        dv_ref[...] = _pack_pair(dv_scr[0], dv_scr[1], lane).astype(BF16)
        rows = pl.ds(pl.multiple_of(ki * t, t), t)
        dc_ref[rows, :] -= (jnp.where(lane == N_HEADS + 2 * j, dk_scr[0][:, AUG_B:AUG_B + 1], 0.0)
                            + jnp.where(lane == N_HEADS + 2 * j + 1, dk_scr[1][:, AUG_B:AUG_B + 1], 0.0))

        @pl.when(ki == nq - 1)
        def _():
            for blk in range(nq):
                rws = pl.ds(blk * t, t)
                d0 = dq_scr[0, rws, :]
                d1 = dq_scr[1, rws, :]
                dq_ref[rws, :] = (_pack_pair(d0, d1, lane) * ATT_SCALE).astype(BF16)
                dc_ref[rws, :] += (jnp.where(lane == N_HEADS + 2 * j, d0[:, AUG_A:AUG_A + 1], 0.0)
                                   + jnp.where(lane == N_HEADS + 2 * j + 1, d1[:, AUG_A:AUG_A + 1], 0.0))

    full = pl.BlockSpec((2, s, LANES), lambda j, ki, f: (j, 0, 0))
    blk = pl.BlockSpec((2, t, LANES), lambda j, ki, f: (j, ki, 0))
    pair = pl.BlockSpec((t, LANES), lambda j, ki, f: (ki, j))
    wide = jax.ShapeDtypeStruct((s, ATT_WIDTH), BF16)
    grid_spec = pltpu.PrefetchScalarGridSpec(
        num_scalar_prefetch=1, grid=(N_PAIRS, nq),
        in_specs=[full, full, blk, blk],
        out_specs=[pl.BlockSpec((s, LANES), lambda j, ki, f: (0, j)), pair, pair,
                   pl.BlockSpec((s, LANES), lambda j, ki, f: (0, 0))],
        scratch_shapes=[pltpu.VMEM((2, s, LANES), F32), pltpu.VMEM((2, t, LANES), F32),
                        pltpu.VMEM((2, t, LANES), F32)])
    return pl.pallas_call(
        body, name="attention_bwd", grid_spec=grid_spec,
        out_shape=(wide, wide, wide, jax.ShapeDtypeStruct((s, LANES), F32)),
        compiler_params=_params(("arbitrary", "arbitrary")),
    )(last_q, qb, dob, ka, va)


def _dsilu(z, sg):
    return sg * (1.0 + z * (1.0 - sg))


def post_mix(x, y, zs, o, za, p, tgt, ssd_g, att_g_lane, ple_g, fin_g, w_out, w_gate, w_proj):
    s = x.shape[0]
    tm = _blk(s, 128)
    half = SSD_WIDTH // N_GROUPS

    def rms_bwd(dy, yn, r):
        return r * (dy - yn * jnp.mean(dy * yn, axis=-1, keepdims=True))

    def colsum(a):
        return jnp.sum(a, axis=0, keepdims=True)

    def body(x_ref, y_ref, zs_ref, o_ref, za_ref, p_ref, t_ref, sg_ref, ag_ref, pg_ref, fg_ref,
             wo_ref, wg_ref, wp_ref,
             dh1_ref, dy_ref, dzs_ref, dob_ref, dza_ref, ycat_ref, dh1b_ref, n2b_ref, dglb_ref, dppb_ref, pb_ref,
             loss_ref, dfin_ref, dple_ref, dssd_ref, datt_ref):
        @pl.when(pl.program_id(0) == 0)
        def _():
            for r in (loss_ref, dfin_ref, dple_ref, dssd_ref, datt_ref):
                r[...] = jnp.zeros_like(r)

        lane = _iota((tm, LANES), 1)
        lo = lane < HEAD_DIM
        zs = zs_ref[...]
        sz = _sigmoid(zs)
        yv = y_ref[...]
        ys = yv * (zs * sz)
        yn, rg = [], []
        for g in range(N_GROUPS):
            seg = ys[:, half * g:half * (g + 1)]
            r = lax.rsqrt(jnp.mean(seg * seg, axis=-1, keepdims=True) + EPS)
            yn.append(seg * r)
            rg.append(r)
            ycat_ref[:, half * g:half * (g + 1)] = (yn[g] * sg_ref[:, half * g:half * (g + 1)]).astype(BF16)
        za = za_ref[...]
        sza = _sigmoid(za)
        silu_za = za * sza
        on, ra = [], []
        for jb in range(N_PAIRS):
            blk = o_ref[:, LANES * jb:LANES * (jb + 1)]
            sq = blk * blk
            ms0 = jnp.sum(jnp.where(lo, sq, 0.0), axis=1, keepdims=True) * (1.0 / HEAD_DIM)
            ms1 = jnp.sum(jnp.where(lo, 0.0, sq), axis=1, keepdims=True) * (1.0 / HEAD_DIM)
            r = jnp.where(lo, lax.rsqrt(ms0 + EPS), lax.rsqrt(ms1 + EPS))
            on.append(blk * r)
            ra.append(r)
            an = on[jb] * ag_ref[:, LANES * jb:LANES * (jb + 1)]
            ycat_ref[:, SSD_WIDTH + LANES * jb:SSD_WIDTH + LANES * (jb + 1)] = (
                an * silu_za[:, LANES * jb:LANES * (jb + 1)]).astype(BF16)
        h1 = x_ref[...] + _mm(ycat_ref[...], wo_ref[...])
        r2 = lax.rsqrt(jnp.mean(h1 * h1, axis=-1, keepdims=True) + EPS)
        n2h = h1 * r2
        n2_b = (n2h * pg_ref[...]).astype(BF16)
        gate = _sigmoid(_mm(n2_b, wg_ref[...]))
        p_b = p_ref[...].astype(BF16)
        pp = _mm(p_b, wp_ref[...])
        h2 = h1 + gate * pp
        r3 = lax.rsqrt(jnp.mean(h2 * h2, axis=-1, keepdims=True) + EPS)
        n3 = h2 * r3
        diff = n3 * fg_ref[...] - t_ref[...]
        sq = colsum(diff * diff)
        part = sq[:, 0:LANES]
        for jb in range(1, D_MODEL // LANES):
            part = part + sq[:, LANES * jb:LANES * (jb + 1)]
        loss_ref[...] += part * (0.5 / D_MODEL)
        dout = diff * (1.0 / D_MODEL)
        dfin_ref[...] += colsum(dout * n3)
        dh2 = rms_bwd(dout * fg_ref[...], n3, r3)
        dgl = dh2 * pp * gate * (1.0 - gate)
        dgl_b = dgl.astype(BF16)
        dn2 = _mm_nt(dgl_b, wg_ref[...])
        dple_ref[...] += colsum(dn2 * n2h)
        dh1 = dh2 + rms_bwd(dn2 * pg_ref[...], n2h, r2)
        dh1_b = dh1.astype(BF16)
        dycat = _mm_nt(dh1_b, wo_ref[...])
        dh1_ref[...] = dh1
        dh1b_ref[...] = dh1_b
        n2b_ref[...] = n2_b
        dglb_ref[...] = dgl_b
        dppb_ref[...] = (dh2 * gate).astype(BF16)
        pb_ref[...] = p_b
        for g in range(N_GROUPS):
            cols = slice(half * g, half * (g + 1))
            dys_g = dycat[:, cols]
            dssd_ref[:, cols] += colsum(dys_g * yn[g])
            dys = rms_bwd(dys_g * sg_ref[:, cols], yn[g], rg[g])
            dy_ref[:, cols] = dys * (zs[:, cols] * sz[:, cols])
            dzs_ref[:, cols] = (dys * yv[:, cols] * _dsilu(zs[:, cols], sz[:, cols])).astype(BF16)
        for jb in range(N_PAIRS):
            cols = slice(LANES * jb, LANES * (jb + 1))
            dya = dycat[:, SSD_WIDTH + LANES * jb:SSD_WIDTH + LANES * (jb + 1)]
            ag = ag_ref[:, cols]
            dan = dya * silu_za[:, cols]
            dza_ref[:, cols] = (dya * (on[jb] * ag) * _dsilu(za[:, cols], sza[:, cols])).astype(BF16)
            datt_ref[:, cols] += colsum(dan * on[jb])
            don = dan * ag
            q = don * on[jb]
            m0 = jnp.sum(jnp.where(lo, q, 0.0), axis=1, keepdims=True) * (1.0 / HEAD_DIM)
            m1 = jnp.sum(jnp.where(lo, 0.0, q), axis=1, keepdims=True) * (1.0 / HEAD_DIM)
            do2 = ra[jb] * (don - on[jb] * jnp.where(lo, m0, m1))
            prod = do2 * o_ref[:, cols]
            for e in range(2):
                delta = jnp.sum(jnp.where(lo, prod, 0.0) if e == 0 else jnp.where(lo, 0.0, prod),
                                axis=1, keepdims=True)
                base = jnp.where(lo, do2 if e == 0 else pltpu.roll(do2, HEAD_DIM, 1), 0.0)
                dob_ref[2 * jb + e] = (base - _aug(lane, AUG_A, _split3(delta))).astype(BF16)

    def rows(n, dtype=None):
        return pl.BlockSpec((tm, n), lambda i: (i, 0))

    def out(n, dtype):
        return jax.ShapeDtypeStruct((s, n), dtype)

    vec = _const_spec((1, D_MODEL))
    vshape = jax.ShapeDtypeStruct((1, D_MODEL), F32)
    return pl.pallas_call(
        body, name="post_mix",
        out_shape=(out(D_MODEL, F32), out(SSD_WIDTH, F32), out(SSD_WIDTH, BF16),
                   jax.ShapeDtypeStruct((N_HEADS, s, LANES), BF16),
                   out(ATT_WIDTH, BF16), out(D_INNER, BF16), out(D_MODEL, BF16), out(D_MODEL, BF16),
                   out(D_MODEL, BF16), out(D_MODEL, BF16), out(PLE_DIM, BF16),
                   jax.ShapeDtypeStruct((1, LANES), F32), vshape, vshape, vshape, vshape),
        grid=(s // tm,),
        in_specs=[rows(D_MODEL), rows(SSD_WIDTH), rows(SSD_WIDTH), rows(ATT_WIDTH), rows(ATT_WIDTH),
                  rows(PLE_DIM), rows(D_MODEL), vec, vec, vec, vec,
                  _const_spec((D_INNER, D_MODEL)), _const_spec((D_MODEL, D_MODEL)), _const_spec((PLE_DIM, D_MODEL))],
        out_specs=(rows(D_MODEL), rows(SSD_WIDTH), rows(SSD_WIDTH),
                   pl.BlockSpec((N_HEADS, tm, LANES), lambda i: (0, i, 0)), rows(ATT_WIDTH),
                   rows(D_INNER), rows(D_MODEL), rows(D_MODEL), rows(D_MODEL), rows(D_MODEL), rows(PLE_DIM),
                   _const_spec((1, LANES)), vec, vec, vec, vec),
        compiler_params=_params(("arbitrary",)),
    )(x, y, zs, o, za, p, tgt, ssd_g, att_g_lane, ple_g, fin_g, w_out, w_gate, w_proj)


def in_proj_bwd(dsegs, wsegs, x, g, dh1, pres):
    s = x.shape[0]
    tm = _blk(s, 256)
    nseg = len(dsegs)
    nbig = len(pres)
    nsteps = s // tm

    def body(*refs):
        d_refs = refs[:nseg]
        w_refs = refs[nseg:2 * nseg]
        x_ref, g_ref, dh1_ref = refs[2 * nseg:2 * nseg + 3]
        rest = refs[2 * nseg + 3:]
        pre_refs, (dx_ref, dg_ref), part_refs = rest[:nbig], rest[nbig:nbig + 2], rest[nbig + 2:2 * nbig + 2]
        ssem, rsem, lsem = rest[2 * nbig + 2:]

        @pl.when(pl.program_id(0) == 0)
        def _():
            dg_ref[...] = jnp.zeros_like(dg_ref)
            for cp in scatter_copies(pre_refs, part_refs, ssem, rsem, lsem):
                cp.start()

        @pl.when(pl.program_id(0) == nsteps - 1)
        def _():
            for cp in scatter_copies(pre_refs, part_refs, ssem, rsem, lsem):
                cp.wait()

        du = _mm_nt(d_refs[0][...], w_refs[0][...])
        for k in range(1, nseg):
            du = du + _mm_nt(d_refs[k][...], w_refs[k][...])
        xv = x_ref[...]
        r = lax.rsqrt(jnp.mean(xv * xv, axis=-1, keepdims=True) + EPS)
        xh = xv * r
        dg_ref[...] += jnp.sum(du * xh, axis=0, keepdims=True)
        dxh = du * g_ref[...]
        dx_ref[...] = r * (dxh - xh * jnp.mean(dxh * xh, axis=-1, keepdims=True)) + dh1_ref[...]

    rows = lambda n: pl.BlockSpec((tm, n), lambda i: (i, 0))
    return pl.pallas_call(
        body, name="in_proj_bwd",
        out_shape=tuple([jax.ShapeDtypeStruct((s, D_MODEL), F32), jax.ShapeDtypeStruct((1, D_MODEL), F32)]
                        + [jax.ShapeDtypeStruct(a.shape, a.dtype) for a in pres]),
        grid=(nsteps,),
        in_specs=([rows(d.shape[1]) for d in dsegs] + [_const_spec(w.shape) for w in wsegs]
                  + [rows(D_MODEL), _const_spec((1, D_MODEL)), rows(D_MODEL)] + [ANY] * nbig),
        out_specs=tuple([rows(D_MODEL), _const_spec((1, D_MODEL))] + [ANY] * nbig),
        scratch_shapes=_sems(3 * nbig) + [pltpu.SemaphoreType.DMA((nbig,))],
        compiler_params=_params(("arbitrary",)),
    )(*dsegs, *wsegs, x, g, dh1, *pres)


SMALL_NAMES = ("norm_g", "conv_b", "dt_bias", "a_log", "d_skip", "ssd_norm_g", "fg_bias", "att_norm_g",
               "ple_norm_g", "final_norm_g")
SMALL_SIZES = (1024, 1536, 16, 16, 16, 1024, 16, 64, 1024, 1024)
CONV_W_SIZE = CONV_WIDTH * CONV_CH


def _pack_small(vals):
    flat = jnp.concatenate([v.reshape(-1).astype(F32) for v in vals])
    flat = jnp.pad(flat, (0, SMALL_ROWS * LANES - flat.shape[0]))
    return flat.reshape(SMALL_ROWS, LANES)


def _unpack_small(pack, shapes):
    flat = pack.reshape(-1)
    out, off = [], 0
    for n, shp in zip(SMALL_SIZES, shapes):
        out.append(flat[off:off + n].reshape(shp))
        off += n
    return out


def _row128(v16, offset=0):
    return jnp.pad(v16.reshape(1, N_HEADS).astype(F32), ((0, 0), (offset, LANES - N_HEADS - offset)))


def local_step(prereduce, x, p, tgt, w_in, w_out, w_gate, w_proj, conv_w, norm_g, conv_b, dt_bias, a_log, d_skip,
               ssd_norm_g, fg_bias, att_norm_g, ple_norm_g, final_norm_g):
    c0, c1, c2, c3, c4, c5, c6, c7 = 0, 1024, 2560, 2576, 3600, 4624, 5648, 6672
    w_zs, w_xbc, w_dt = w_in[:, c0:c1], w_in[:, c1:c2], w_in[:, c2:c3]
    w_za, w_q, w_k, w_v, w_f = w_in[:, c3:c4], w_in[:, c4:c5], w_in[:, c5:c6], w_in[:, c6:c7], w_in[:, c7:]
    w_small = jnp.concatenate([w_dt, w_f, jnp.zeros((D_MODEL, LANES - 2 * N_HEADS), BF16)], axis=1)

    dtb_row = _row128(dt_bias)
    a_row = _row128(-jnp.exp(a_log.astype(F32)))
    fgb_row = _row128(fg_bias, N_HEADS)
    dskip_lane = jnp.repeat(d_skip.astype(F32), HEAD_DIM).reshape(1, SSD_WIDTH)
    att_g_lane = jnp.tile(att_norm_g.astype(F32), N_HEADS).reshape(1, ATT_WIDTH)
    row = lambda v: v.reshape(1, -1).astype(F32)

    u = rms_prenorm(x, row(norm_g))
    zs = matmul_rows(u, w_zs, F32, "proj_z_ssd")
    xbc = matmul_rows(u, w_xbc, F32, "proj_xbc")
    za = matmul_rows(u, w_za, F32, "proj_z_att")
    small = matmul_rows(u, w_small, F32, "proj_small")
    cum = forget_cumsum(small, fgb_row)
    qa, ka, va, norms = proj_qkv_heads(u, w_q, w_k, w_v, cum)
    first, last_q = live_blocks(norms, cum, _blk(x.shape[0], ATT_BLOCK))
    pre, xc = conv_fwd(xbc, conv_w, row(conv_b))
    y, states = ssd_fwd(xc, small, dtb_row, a_row, dskip_lane)
    o, qb = attention_fwd(first, qa, ka, va)
    (dh1, dy, dzs, dob, dza, ycat, dh1_b, n2_b, dgl_b, dpp_b, p_b,
     loss_l, dfin, dple, dssd_g, datt_lane) = post_mix(
        x, y, zs, o, za, p, tgt, row(ssd_norm_g), att_g_lane, row(ple_norm_g), row(final_norm_g),
        w_out, w_gate, w_proj)
    dq, dk, dv, dc = attention_bwd(last_q, qb, ka, va, dob)
    dxc, ddt_raw, da, ddtb, ddsk_lane = ssd_bwd(xc, small, states, dy, dtb_row, a_row, dskip_lane)
    dsmall, dfgb = forget_bwd(dc, small, ddt_raw, fgb_row)
    dxbc, dconv_w8, dconv_b = conv_bwd(xbc, pre, dxc, conv_w)
    dsegs = [dzs, dxbc, dza, dq, dk, dv, dsmall]
    wsegs = [w_zs, w_xbc, w_za, w_q, w_k, w_v, w_small]
    dws = [matmul_tn(u, d, "dw_in_%d" % i) for i, d in enumerate(dsegs)]
    dw_in = jnp.concatenate([dws[0], dws[1], dws[6][:, :N_HEADS], dws[2], dws[3], dws[4], dws[5],
                             dws[6][:, N_HEADS:2 * N_HEADS]], axis=1)
    dw_out = matmul_tn(ycat, dh1_b, "dw_out")
    dw_gate = matmul_tn(n2_b, dgl_b, "dw_gate")
    dw_proj = matmul_tn(p_b, dpp_b, "dw_proj")
    dx, dnorm_g, *parts = in_proj_bwd(dsegs, wsegs, x, row(norm_g), dh1, prereduce(dw_in, dw_out, dw_gate, dw_proj))
    small_grads = [
        dnorm_g, dconv_b, ddtb[0, :N_HEADS], (da * a_row)[0, :N_HEADS],
        ddsk_lane.reshape(N_HEADS, HEAD_DIM).sum(axis=1), dssd_g, dfgb[0, N_HEADS:2 * N_HEADS],
        datt_lane.reshape(N_HEADS, HEAD_DIM).sum(axis=0), dple, dfin]
    loss = jnp.sum(loss_l)
    return loss, dx, parts, dconv_w8[:CONV_WIDTH], small_grads


def kernel(x, p, norm_g, w_in, conv_w, conv_b, dt_bias, a_log, d_skip, ssd_norm_g, fg_bias, att_norm_g, w_out, ple_norm_g, w_ple_gate, w_ple_proj, final_norm_g, loss_target, m_norm_g, m_w_in, m_conv_w, m_conv_b, m_dt_bias, m_a_log, m_d_skip, m_ssd_norm_g, m_fg_bias, m_att_norm_g, m_w_out, m_ple_norm_g, m_w_ple_gate, m_w_ple_proj, m_final_norm_g, v_norm_g, v_w_in, v_conv_w, v_conv_b, v_dt_bias, v_a_log, v_d_skip, v_ssd_norm_g, v_fg_bias, v_att_norm_g, v_w_out, v_ple_norm_g, v_w_ple_gate, v_w_ple_proj, v_final_norm_g):
    chip = 2 * lax.axis_index("x") + lax.axis_index("y")
    core = lax.axis_index("c")

    big_w = [w_in[0], w_out[0], w_ple_gate[0], w_ple_proj[0]]
    own = [a.astype(BF16) for a in big_w] + [conv_w[0]]
    gathered = gather_weights(own[:4], own[4])

    def joined(k, axis):
        return jnp.concatenate([jnp.where(chip == j, own[k], gathered[k][j]) for j in range(N_CHIPS)], axis=axis)

    w_in_f, w_out_f, w_gate_f, w_proj_f, conv_w_f = joined(0, 1), joined(1, 0), joined(2, 0), joined(3, 1), joined(4, 1)

    core1 = core.reshape(1).astype(jnp.int32)

    def prereduce(dw_in, dw_out, dw_gate, dw_proj):
        gs = [jnp.stack([dw_in[:, 1672 * j:1672 * (j + 1)] for j in range(N_CHIPS)]),
              dw_out.reshape(N_CHIPS, 512, D_MODEL), dw_gate.reshape(N_CHIPS, 256, D_MODEL),
              jnp.stack([dw_proj[:, 256 * j:256 * (j + 1)] for j in range(N_CHIPS)])]
        return add_halves(core1, gs, halves_to_sibling(gs))

    smalls_w = [norm_g, conv_b, dt_bias, a_log, d_skip, ssd_norm_g, fg_bias, att_norm_g, ple_norm_g, final_norm_g]
    loss_l, dx, parts, dconv_w, small_grads = local_step(
        prereduce, x[0], p[0, 0], loss_target[0], w_in_f, w_out_f, w_gate_f, w_proj_f, conv_w_f,
        *[a.reshape(-1) for a in smalls_w])
    loss = lax.psum(loss_l, ("x", "y", "c"))
    smalls = gather_small(_pack_small(list(small_grads) + [dconv_w]))
    mine = sum_parts(parts)

    g_big, d_big, m_big, v_big = adamw_big(
        core1, mine, swap_halves(mine), big_w, [m_w_in[0], m_w_out[0], m_w_ple_gate[0], m_w_ple_proj[0]],
        [v_w_in[0], v_w_out[0], v_w_ple_gate[0], v_w_ple_proj[0]])
    smalls_m = [m_norm_g, m_conv_b, m_dt_bias, m_a_log, m_d_skip, m_ssd_norm_g, m_fg_bias, m_att_norm_g,
                m_ple_norm_g, m_final_norm_g]
    smalls_v = [v_norm_g, v_conv_b, v_dt_bias, v_a_log, v_d_skip, v_ssd_norm_g, v_fg_bias, v_att_norm_g,
                v_ple_norm_g, v_final_norm_g]
    g_sm, d_sm, m_sm, v_sm = adamw_small(smalls, _pack_small(smalls_w), _pack_small(smalls_m), _pack_small(smalls_v))
    n_small = sum(SMALL_SIZES)
    g_conv_full = g_sm.reshape(-1)[n_small:n_small + CONV_W_SIZE].reshape(CONV_WIDTH, CONV_CH)
    g_conv = lax.dynamic_slice_in_dim(g_conv_full, chip * 384, 384, axis=1)
    d_conv, m_conv, v_conv = adamw_whole(g_conv, conv_w[0], m_conv_w[0], v_conv_w[0], "adamw_conv")

    shapes = [a.shape for a in smalls_w]
    outs = []
    for big, conv, sm in ((g_big, g_conv, g_sm), (d_big, d_conv, d_sm), (m_big, m_conv, m_sm), (v_big, v_conv, v_sm)):
        b_in, b_out, b_gate, b_proj = [a[None] for a in big]
        s_norm, s_convb, s_dtb, s_alog, s_dsk, s_ssdg, s_fgb, s_attg, s_pleg, s_fin = _unpack_small(sm, shapes)
        outs.extend([s_norm, b_in, conv[None], s_convb, s_dtb, s_alog, s_dsk, s_ssdg, s_fgb, s_attg, b_out, s_pleg,
                     b_gate, b_proj, s_fin])
    return (loss, dx[None], *outs)
```

```python
import functools

import jax
import jax.numpy as jnp
from jax import lax
from jax.experimental import pallas as pl
from jax.experimental.pallas import tpu as pltpu

F32 = jnp.float32
BF16 = jnp.bfloat16

D_MODEL = 1024
SSD_WIDTH = 1024
ATT_WIDTH = 1024
N_HEADS = 16
HEAD_DIM = 64
N_GROUPS = 2
D_STATE = 128
CONV_CH = 1536
CONV_WIDTH = 4
CHUNK = 128
PLE_DIM = 256
D_INNER = 2048
EPS = 1e-6
IN_COLS = 6688
N_CHIPS = 4
N_DEV = 8
LANES = 128
N_PAIRS = 8

ADAM_LR = 0.001
ADAM_B1 = 0.9
ADAM_B2 = 0.999
ADAM_EPS = 1e-08
ADAM_WD = 0.01
ADAM_STEP = 10

SMALL_ROWS = 96

NEG_BIG = -1e30
VMEM_LIMIT = 56 * 1024 * 1024

MESH = pl.DeviceIdType.MESH
ANY = pl.BlockSpec(memory_space=pl.ANY)


def _mm(a, b):
    return jnp.dot(a, b, preferred_element_type=F32)


def _mm_nt(a, b):
    return lax.dot_general(a, b, (((1,), (1,)), ((), ())), preferred_element_type=F32)


def _mm_tn(a, b):
    return lax.dot_general(a, b, (((0,), (0,)), ((), ())), preferred_element_type=F32)


def _mm_exact(a, b):
    return jnp.dot(a, b, preferred_element_type=F32, precision=lax.Precision.HIGHEST)


def _softplus(x):
    return jnp.maximum(x, 0.0) + jnp.log1p(jnp.exp(-jnp.abs(x)))


def _sigmoid(x):
    return jax.nn.sigmoid(x)


def _iota(shape, dim):
    return lax.broadcasted_iota(jnp.int32, shape, dim)


def _params(sem=None):
    return pltpu.CompilerParams(dimension_semantics=sem, vmem_limit_bytes=VMEM_LIMIT)


def _blk(n, pref):
    return min(n, pref)


def _const_spec(shape):
    nd = len(shape)
    return pl.BlockSpec(shape, lambda *_: (0,) * nd)


def _chip_peers():
    x, y, c = lax.axis_index("x"), lax.axis_index("y"), lax.axis_index("c")
    return x, y, c, [(1 - x, y, c), (x, 1 - y, c), (1 - x, 1 - y, c)]


def _half(rows, c):
    h = rows // 2
    return pl.ds(pl.multiple_of(c * h, 8), h)


def _sems(n):
    return [pltpu.SemaphoreType.DMA((n,)), pltpu.SemaphoreType.DMA((n,))]


def gather_weights(shards, conv_s):
    n = len(shards)

    def body(*refs):
        ins, conv_in = refs[:n], refs[n]
        outs, conv_out = refs[n + 1:2 * n + 1], refs[2 * n + 1]
        ssem1, rsem1, ssem2, rsem2, c_ssem, c_rsem = refs[2 * n + 2:]
        x, y, c, peers = _chip_peers()
        me = 2 * x + y
        sibling = (x, y, 1 - c)
        first, small = [], []
        for k, peer in enumerate(peers):
            for i in range(n):
                h = _half(ins[i].shape[0], c)
                first.append(pltpu.make_async_remote_copy(
                    src_ref=ins[i].at[h], dst_ref=outs[i].at[me, h], send_sem=ssem1.at[n * k + i],
                    recv_sem=rsem1.at[n * k + i], device_id=peer, device_id_type=MESH))
            small.append(pltpu.make_async_remote_copy(
                src_ref=conv_in, dst_ref=conv_out.at[me], send_sem=c_ssem.at[k], recv_sem=c_rsem.at[k],
                device_id=peer, device_id_type=MESH))
        for cp in first + small:
            cp.start()
        passed = []
        for k, peer in enumerate(peers):
            chip = 2 * peer[0] + peer[1]
            for i in range(n):
                h = _half(ins[i].shape[0], c)
                first[n * k + i].wait_recv()
                fwd = pltpu.make_async_remote_copy(
                    src_ref=outs[i].at[chip, h], dst_ref=outs[i].at[chip, h], send_sem=ssem2.at[n * k + i],
                    recv_sem=rsem2.at[n * k + i], device_id=sibling, device_id_type=MESH)
                fwd.start()
                passed.append(fwd)
        for cp in passed:
            cp.wait_recv()
        for cp in first + passed:
            cp.wait_send()
        for cp in small:
            cp.wait()

    return pl.pallas_call(
        body, name="gather_weights",
        out_shape=tuple(jax.ShapeDtypeStruct((N_CHIPS,) + a.shape, a.dtype) for a in list(shards) + [conv_s]),
        in_specs=[ANY] * (n + 1), out_specs=(ANY,) * (n + 1),
        scratch_shapes=_sems(3 * n) + _sems(3 * n) + _sems(3),
    )(*shards, conv_s)


def halves_to_sibling(gs):
    n = len(gs)

    def body(*refs):
        ins, outs = refs[:n], refs[n:2 * n]
        ssem, rsem = refs[2 * n:]
        x, y, c = lax.axis_index("x"), lax.axis_index("y"), lax.axis_index("c")
        copies = []
        for i in range(n):
            for j in range(N_CHIPS):
                copies.append(pltpu.make_async_remote_copy(
                    src_ref=ins[i].at[j, _half(ins[i].shape[1], 1 - c)], dst_ref=outs[i].at[j],
                    send_sem=ssem.at[N_CHIPS * i + j], recv_sem=rsem.at[N_CHIPS * i + j],
                    device_id=(x, y, 1 - c), device_id_type=MESH))
        for cp in copies:
            cp.start()
        for cp in copies:
            cp.wait()

    return pl.pallas_call(
        body, name="halves_to_sibling",
        out_shape=tuple(jax.ShapeDtypeStruct((N_CHIPS, g.shape[1] // 2, g.shape[2]), F32) for g in gs),
        in_specs=[ANY] * n, out_specs=(ANY,) * n, scratch_shapes=_sems(N_CHIPS * n),
    )(*gs)


RED_GRID = 8


def add_halves(core, gs, rbs):
    n = len(gs)

    def body(c_ref, *refs):
        for i in range(n):
            refs[2 * n + i][...] = (refs[i][...] + refs[n + i][...]).astype(BF16)

    def blk(g):
        return (1, g.shape[1] // 2 // RED_GRID, g.shape[2])

    grid_spec = pltpu.PrefetchScalarGridSpec(
        num_scalar_prefetch=1, grid=(N_CHIPS, RED_GRID),
        in_specs=([pl.BlockSpec(blk(g), lambda j, b, c_ref: (j, c_ref[0] * RED_GRID + b, 0)) for g in gs]
                  + [pl.BlockSpec(blk(g), lambda j, b, c_ref: (j, b, 0)) for g in gs]),
        out_specs=[pl.BlockSpec(blk(g), lambda j, b, c_ref: (j, b, 0)) for g in gs])
    return pl.pallas_call(
        body, name="add_halves", grid_spec=grid_spec,
        out_shape=tuple(jax.ShapeDtypeStruct(r.shape, BF16) for r in rbs),
        compiler_params=_params(("parallel", "parallel")),
    )(core, *gs, *rbs)


def scatter_copies(ins, outs, ssem, rsem, lsem):
    n = len(ins)
    x, y, _, peers = _chip_peers()
    me = 2 * x + y
    copies = [pltpu.make_async_copy(ins[i].at[me], outs[i].at[me], lsem.at[i]) for i in range(n)]
    for k, peer in enumerate(peers):
        dst_chip = 2 * peer[0] + peer[1]
        for i in range(n):
            copies.append(pltpu.make_async_remote_copy(
                src_ref=ins[i].at[dst_chip], dst_ref=outs[i].at[me], send_sem=ssem.at[n * k + i],
                recv_sem=rsem.at[n * k + i], device_id=peer, device_id_type=MESH))
    return copies


def gather_small(small):
    def body(s_ref, smalls_ref, ssem, rsem, lsem):
        x, y, c = lax.axis_index("x"), lax.axis_index("y"), lax.axis_index("c")
        dev = 4 * x + 2 * y + c
        copies = [pltpu.make_async_copy(s_ref, smalls_ref.at[dev], lsem)]
        for k in range(1, N_DEV):
            fx, fy, fc = (k >> 2) & 1, (k >> 1) & 1, k & 1
            peer = ((1 - x) if fx else x, (1 - y) if fy else y, (1 - c) if fc else c)
            copies.append(pltpu.make_async_remote_copy(
                src_ref=s_ref, dst_ref=smalls_ref.at[dev], send_sem=ssem.at[k - 1], recv_sem=rsem.at[k - 1],
                device_id=peer, device_id_type=MESH))
        for cp in copies:
            cp.start()
        for cp in copies:
            cp.wait()

    return pl.pallas_call(
        body, name="gather_small",
        out_shape=jax.ShapeDtypeStruct((N_DEV,) + small.shape, F32),
        in_specs=[ANY], out_specs=ANY,
        scratch_shapes=_sems(N_DEV - 1) + [pltpu.SemaphoreType.DMA],
    )(small)


def sum_parts(parts):
    n = len(parts)

    def body(*refs):
        for i in range(n):
            p_ref = refs[i]
            refs[n + i][...] = ((p_ref[0].astype(F32) + p_ref[1].astype(F32)) + p_ref[2].astype(F32)
                                ) + p_ref[3].astype(F32)

    def rows(p):
        return p.shape[1] // RED_GRID

    return pl.pallas_call(
        body, name="sum_parts",
        out_shape=tuple(jax.ShapeDtypeStruct(p.shape[1:], F32) for p in parts),
        grid=(RED_GRID,),
        in_specs=[pl.BlockSpec((N_CHIPS, rows(p), p.shape[2]), lambda b: (0, b, 0)) for p in parts],
        out_specs=tuple(pl.BlockSpec((rows(p), p.shape[2]), lambda b: (b, 0)) for p in parts),
        compiler_params=_params(("parallel",)),
    )(*parts)


def swap_halves(reds):
    n = len(reds)

    def body(*refs):
        ins, outs = refs[:n], refs[n:2 * n]
        ssem, rsem = refs[2 * n:]
        x, y, c = lax.axis_index("x"), lax.axis_index("y"), lax.axis_index("c")
        copies = [pltpu.make_async_remote_copy(
            src_ref=ins[i], dst_ref=outs[i], send_sem=ssem.at[i], recv_sem=rsem.at[i],
            device_id=(x, y, 1 - c), device_id_type=MESH) for i in range(n)]
        for cp in copies:
            cp.start()
        for cp in copies:
            cp.wait()

    return pl.pallas_call(
        body, name="swap_halves",
        out_shape=tuple(jax.ShapeDtypeStruct(r.shape, F32) for r in reds),
        in_specs=[ANY] * n, out_specs=(ANY,) * n, scratch_shapes=_sems(n),
    )(*reds)


def _adamw(w, g, m, v):
    m = ADAM_B1 * m + (1.0 - ADAM_B1) * g
    v = ADAM_B2 * v + (1.0 - ADAM_B2) * (g * g)
    m_hat = m / (1.0 - ADAM_B1 ** ADAM_STEP)
    v_hat = v / (1.0 - ADAM_B2 ** ADAM_STEP)
    delta = -ADAM_LR * (m_hat / (jnp.sqrt(v_hat) + ADAM_EPS) + ADAM_WD * w)
    return delta, m, v


def adamw_big(core, mine, theirs, ws, ms, vs):
    n = len(ws)
    per_half = RED_GRID // 2

    def body(c_ref, *refs):
        own = (pl.program_id(0) // per_half) == c_ref[0]
        for i in range(n):
            g = jnp.where(own, refs[i][...], refs[n + i][...])
            d, mn, vn = _adamw(refs[2 * n + i][...], g, refs[3 * n + i][...], refs[4 * n + i][...])
            refs[5 * n + i][...] = g
            refs[6 * n + i][...] = d
            refs[7 * n + i][...] = mn
            refs[8 * n + i][...] = vn

    def blk(w):
        return (w.shape[0] // RED_GRID, w.shape[1])

    halves = [pl.BlockSpec(blk(w), lambda b, c_ref: (b % per_half, 0)) for w in ws]
    whole = [pl.BlockSpec(blk(w), lambda b, c_ref: (b, 0)) for w in ws]
    shapes = [jax.ShapeDtypeStruct(w.shape, F32) for w in ws]
    grid_spec = pltpu.PrefetchScalarGridSpec(
        num_scalar_prefetch=1, grid=(RED_GRID,), in_specs=halves * 2 + whole * 3, out_specs=whole * 4)
    outs = pl.pallas_call(
        body, name="adamw_big", out_shape=tuple(shapes * 4), grid_spec=grid_spec,
        compiler_params=_params(("parallel",)),
    )(core, *mine, *theirs, *ws, *ms, *vs)
    return outs[:n], outs[n:2 * n], outs[2 * n:3 * n], outs[3 * n:]


def adamw_whole(g, w, m, v, name):
    def body(g_ref, w_ref, m_ref, v_ref, d_out, m_out, v_out):
        d, mn, vn = _adamw(w_ref[...], g_ref[...], m_ref[...], v_ref[...])
        d_out[...] = d
        m_out[...] = mn
        v_out[...] = vn

    shp = jax.ShapeDtypeStruct(g.shape, F32)
    return pl.pallas_call(body, name=name, out_shape=(shp,) * 3)(g, w, m, v)


def adamw_small(smalls, w, m, v):
    def body(s_ref, w_ref, m_ref, v_ref, g_out, d_out, m_out, v_out):
        g = s_ref[0]
        for k in range(1, N_DEV):
            g = g + s_ref[k]
        d, mn, vn = _adamw(w_ref[...], g, m_ref[...], v_ref[...])
        g_out[...] = g
        d_out[...] = d
        m_out[...] = mn
        v_out[...] = vn

    shp = jax.ShapeDtypeStruct((SMALL_ROWS, LANES), F32)
    return pl.pallas_call(body, name="adamw_small", out_shape=(shp,) * 4)(smalls, w, m, v)


def rms_prenorm(x, g):
    s = x.shape[0]
    tm = _blk(s, 512)

    def body(x_ref, g_ref, u_ref):
        xv = x_ref[...]
        r = lax.rsqrt(jnp.mean(xv * xv, axis=-1, keepdims=True) + EPS)
        u_ref[...] = (xv * r * g_ref[...]).astype(BF16)

    return pl.pallas_call(
        body, name="rms_prenorm", out_shape=jax.ShapeDtypeStruct(x.shape, BF16), grid=(s // tm,),
        in_specs=[pl.BlockSpec((tm, D_MODEL), lambda i: (i, 0)), _const_spec((1, D_MODEL))],
        out_specs=pl.BlockSpec((tm, D_MODEL), lambda i: (i, 0)), compiler_params=_params(("parallel",)),
    )(x, g)


def matmul_rows(a, w, out_dtype, name):
    s, k = a.shape
    n = w.shape[1]
    tm = _blk(s, 512)

    def body(a_ref, w_ref, o_ref):
        o_ref[...] = _mm(a_ref[...], w_ref[...]).astype(out_dtype)

    return pl.pallas_call(
        body, name=name, out_shape=jax.ShapeDtypeStruct((s, n), out_dtype), grid=(s // tm,),
        in_specs=[pl.BlockSpec((tm, k), lambda i: (i, 0)), _const_spec((k, n))],
        out_specs=pl.BlockSpec((tm, n), lambda i: (i, 0)), compiler_params=_params(("parallel",)),
    )(a, w)


def matmul_tn(a, b, name):
    s, m = a.shape
    n = b.shape[1]
    tk = _blk(s, 2048)
    tn = _blk(n, 512)

    def body(a_ref, b_ref, o_ref):
        @pl.when(pl.program_id(1) == 0)
        def _():
            o_ref[...] = jnp.zeros_like(o_ref)

        o_ref[...] += _mm_tn(a_ref[...], b_ref[...])

    return pl.pallas_call(
        body, name=name, out_shape=jax.ShapeDtypeStruct((m, n), F32), grid=(n // tn, s // tk),
        in_specs=[pl.BlockSpec((tk, m), lambda j, i: (i, 0)), pl.BlockSpec((tk, tn), lambda j, i: (i, j))],
        out_specs=pl.BlockSpec((m, tn), lambda j, i: (0, j)),
        compiler_params=_params(("parallel", "arbitrary")),
    )(a, b)


def conv_fwd(xbc, w, b):
    s = xbc.shape[0]
    tm = _blk(s, 256)

    def body(x_ref, t_ref, w_ref, b_ref, pre_ref, act_ref):
        i = pl.program_id(0)
        cur = x_ref[...]
        tail = jnp.where(i > 0, t_ref[...], 0.0)
        wv = w_ref[...]
        acc = cur * wv[3:4, :] + b_ref[...]
        head = cur[0:8, :] * wv[3:4, :] + b_ref[...]
        row8 = _iota((8, CONV_CH), 0)
        for sh in range(1, CONV_WIDTH):
            wk = wv[3 - sh:4 - sh, :]
            acc = acc + pltpu.roll(cur, sh, 0) * wk
            first = jnp.where(row8 < sh, pltpu.roll(tail, sh, 0), pltpu.roll(cur[0:8, :], sh, 0))
            head = head + first * wk
        pre_ref[...] = acc
        act_ref[...] = acc * _sigmoid(acc)
        pre_ref[0:8, :] = head
        act_ref[0:8, :] = head * _sigmoid(head)

    shp = jax.ShapeDtypeStruct(xbc.shape, F32)
    rows = pl.BlockSpec((tm, CONV_CH), lambda i: (i, 0))
    return pl.pallas_call(
        body, name="conv_fwd", out_shape=(shp, shp), grid=(s // tm,),
        in_specs=[rows, pl.BlockSpec((8, CONV_CH), lambda i: (jnp.maximum(i * (tm // 8) - 1, 0), 0)),
                  _const_spec((CONV_WIDTH, CONV_CH)), _const_spec((1, CONV_CH))],
        out_specs=(rows, rows), compiler_params=_params(("parallel",)),
    )(xbc, xbc, w, b)


def conv_bwd(xbc, pre, dact, w):
    s = xbc.shape[0]
    tm = _blk(s, 256)
    nb = s // tm

    def dsilu(p):
        sg = _sigmoid(p)
        return sg * (1.0 + p * (1.0 - sg))

    def body(x_ref, xt_ref, p_ref, pn_ref, d_ref, dn_ref, w_ref, dx_ref, dw_ref, db_ref):
        i = pl.program_id(0)

        @pl.when(i == 0)
        def _():
            dw_ref[...] = jnp.zeros_like(dw_ref)
            db_ref[...] = jnp.zeros_like(db_ref)

        wv = w_ref[...]
        dpre = d_ref[...] * dsilu(p_ref[...])
        dnext = jnp.where(i < nb - 1, dn_ref[...] * dsilu(pn_ref[...]), 0.0)
        cur = x_ref[...]
        tail = jnp.where(i > 0, xt_ref[...], 0.0)
        row8 = _iota((8, CONV_CH), 0)
        dx = dpre * wv[3:4, :]
        last = dpre[tm - 8:tm, :] * wv[3:4, :]
        db_ref[...] += jnp.sum(dpre, axis=0, keepdims=True)
        dws = [jnp.sum(dpre * cur, axis=0, keepdims=True)]
        for sh in range(1, CONV_WIDTH):
            wk = wv[3 - sh:4 - sh, :]
            dx = dx + pltpu.roll(dpre, tm - sh, 0) * wk
            nxt = jnp.where(row8 >= 8 - sh, pltpu.roll(dnext, 8 - sh, 0), pltpu.roll(dpre[tm - 8:tm, :], 8 - sh, 0))
            last = last + nxt * wk
            xs = pltpu.roll(cur, sh, 0)
            first = jnp.where(row8 < sh, pltpu.roll(tail, sh, 0), xs[0:8, :])
            dws.append(jnp.sum(dpre * xs, axis=0, keepdims=True)
                       + jnp.sum(dpre[0:8, :] * (first - xs[0:8, :]), axis=0, keepdims=True))
        dx_ref[...] = dx.astype(BF16)
        dx_ref[tm - 8:tm, :] = last.astype(BF16)
        for sh in range(CONV_WIDTH):
            dw_ref[3 - sh:4 - sh, :] += dws[sh]

    rows = pl.BlockSpec((tm, CONV_CH), lambda i: (i, 0))
    prev8 = pl.BlockSpec((8, CONV_CH), lambda i: (jnp.maximum(i * (tm // 8) - 1, 0), 0))
    next8 = pl.BlockSpec((8, CONV_CH), lambda i: (jnp.minimum((i + 1) * (tm // 8), s // 8 - 1), 0))
    return pl.pallas_call(
        body, name="conv_bwd",
        out_shape=(jax.ShapeDtypeStruct(xbc.shape, BF16), jax.ShapeDtypeStruct((8, CONV_CH), F32),
                   jax.ShapeDtypeStruct((1, CONV_CH), F32)),
        grid=(nb,),
        in_specs=[rows, prev8, rows, next8, rows, next8, _const_spec((CONV_WIDTH, CONV_CH))],
        out_specs=(rows, _const_spec((8, CONV_CH)), _const_spec((1, CONV_CH))),
        compiler_params=_params(("arbitrary",)),
    )(xbc, xbc, pre, pre, dact, dact, w)


def _pair_lanes(mat, j, lane):
    return jnp.where(lane < HEAD_DIM, mat[:, 2 * j:2 * j + 1], mat[:, 2 * j + 1:2 * j + 2])


def _ssd_chunk_prelude(sm, dtb, a_row, lane, sub):
    raw = sm + dtb
    head_lane = lane < N_HEADS
    dt = jnp.where(head_lane, _softplus(raw), 0.0)
    sig = jnp.where(head_lane, _sigmoid(raw), 0.0)
    tri = (lane <= sub).astype(F32)
    acs = _mm_exact(tri, dt * a_row)
    return dt, sig, acs, acs.T


GROUP_WIDTH = SSD_WIDTH // N_GROUPS
HEADS_PER_GROUP = N_HEADS // N_GROUPS


def _expand_group(mat, g, lane):
    return jnp.concatenate([_pair_lanes(mat, j, lane) for j in range(4 * g, 4 * g + 4)], axis=1)


def _head_sums(q, g):
    row = _iota((GROUP_WIDTH, LANES), 0)
    seg = (_iota((GROUP_WIDTH, LANES), 1) == HEADS_PER_GROUP * g + (row >> 6)).astype(BF16)
    hi = q.astype(BF16)
    lo = (q - hi.astype(F32)).astype(BF16)
    return _mm(hi, seg) + _mm(lo, seg)


def _rows_from_lanes(row512):
    return jnp.broadcast_to(row512, (LANES, GROUP_WIDTH)).T


def ssd_fwd(xc, small, dtb_row, a_row, dskip_lane):
    s = xc.shape[0]
    nc = s // CHUNK

    def body(xc_ref, sm_ref, dtb_ref, a_ref, dsk_ref, y_ref, hs_ref, h_scr):
        c = pl.program_id(0)

        @pl.when(c == 0)
        def _():
            h_scr[...] = jnp.zeros_like(h_scr)

        lane = _iota((CHUNK, LANES), 1)
        sub = _iota((CHUNK, LANES), 0)
        causal = lane <= sub
        dt, _, acs, acs_t = _ssd_chunk_prelude(sm_ref[...], dtb_ref[...], a_ref[...], lane, sub)
        for g in range(N_GROUPS):
            cols = slice(GROUP_WIDTH * g, GROUP_WIDTH * (g + 1))
            b_off = SSD_WIDTH + D_STATE * g
            c_off = SSD_WIDTH + N_GROUPS * D_STATE + D_STATE * g
            b_b = xc_ref[:, b_off:b_off + D_STATE].astype(BF16)
            c_b = xc_ref[:, c_off:c_off + D_STATE].astype(BF16)
            cb = _mm_nt(c_b, b_b)
            x_g = xc_ref[:, cols]
            acs_g = _expand_group(acs, g, lane)
            xdt_g = x_g * _expand_group(dt, g, lane)
            xdt_b = xdt_g.astype(BF16)
            heads = range(HEADS_PER_GROUP * g, HEADS_PER_GROUP * (g + 1))
            m_b = [(cb * jnp.exp(jnp.where(causal, acs[:, h:h + 1] - acs_t[h:h + 1, :], NEG_BIG))).astype(BF16)
                   for h in heads]
            yd = [_mm(m_b[k], xdt_b[:, LANES * (k // 2):LANES * (k // 2 + 1)]) for k in range(HEADS_PER_GROUP)]
            yd_g = jnp.concatenate([jnp.where(lane < HEAD_DIM, yd[2 * k], yd[2 * k + 1]) for k in range(4)], axis=1)
            h_g = h_scr[g]
            t_g = _mm_nt(c_b, h_g.astype(BF16))
            y_ref[:, cols] = yd_g + jnp.exp(acs_g) * t_g + dsk_ref[:, cols] * x_g
            hs_ref[0, g] = h_g
            last_g = acs_g[CHUNK - 1:CHUNK, :]
            w_b = (xdt_g * jnp.exp(last_g - acs_g)).astype(BF16)
            h_scr[g] = h_g * jnp.exp(_rows_from_lanes(last_g)) + _mm_tn(w_b, b_b)

    return pl.pallas_call(
        body, name="ssd_fwd",
        out_shape=(jax.ShapeDtypeStruct((s, SSD_WIDTH), F32),
                   jax.ShapeDtypeStruct((nc, N_GROUPS, GROUP_WIDTH, D_STATE), F32)),
        grid=(nc,),
        in_specs=[pl.BlockSpec((CHUNK, CONV_CH), lambda c: (c, 0)), pl.BlockSpec((CHUNK, LANES), lambda c: (c, 0)),
                  _const_spec((1, LANES)), _const_spec((1, LANES)), _const_spec((1, SSD_WIDTH))],
        out_specs=(pl.BlockSpec((CHUNK, SSD_WIDTH), lambda c: (c, 0)),
                   pl.BlockSpec((1, N_GROUPS, GROUP_WIDTH, D_STATE), lambda c: (c, 0, 0, 0))),
        scratch_shapes=[pltpu.VMEM((N_GROUPS, GROUP_WIDTH, D_STATE), F32)],
        compiler_params=_params(("arbitrary",)),
    )(xc, small, dtb_row, a_row, dskip_lane)


def ssd_bwd(xc, small, states, dy, dtb_row, a_row, dskip_lane):
    s = xc.shape[0]
    nc = s // CHUNK
    rev = lambda c: nc - 1 - c

    def body(xc_ref, sm_ref, hs_ref, dy_ref, dtb_ref, a_ref, dsk_ref,
             dxc_ref, ddt_ref, da_ref, ddtb_ref, ddsk_ref, dh_scr):
        c = pl.program_id(0)

        @pl.when(c == 0)
        def _():
            dh_scr[...] = jnp.zeros_like(dh_scr)
            da_ref[...] = jnp.zeros_like(da_ref)
            ddtb_ref[...] = jnp.zeros_like(ddtb_ref)
            ddsk_ref[...] = jnp.zeros_like(ddsk_ref)

        lane = _iota((CHUNK, LANES), 1)
        sub = _iota((CHUNK, LANES), 0)
        causal = lane <= sub
        upper = lane >= sub
        is_last = sub == CHUNK - 1
        a_row_v = a_ref[...]
        dt, sig, acs, acs_t = _ssd_chunk_prelude(sm_ref[...], dtb_ref[...], a_row_v, lane, sub)
        cd = jnp.exp(acs[CHUNK - 1:CHUNK, :])
        dacs_c = jnp.zeros((CHUNK, LANES), F32)
        dacs_r = jnp.zeros((LANES, CHUNK), F32)
        ddtx = jnp.zeros((CHUNK, LANES), F32)
        for g in range(N_GROUPS):
            cols = slice(GROUP_WIDTH * g, GROUP_WIDTH * (g + 1))
            b_off = SSD_WIDTH + D_STATE * g
            c_off = SSD_WIDTH + N_GROUPS * D_STATE + D_STATE * g
            b_b = xc_ref[:, b_off:b_off + D_STATE].astype(BF16)
            c_b = xc_ref[:, c_off:c_off + D_STATE].astype(BF16)
            cb = _mm_nt(c_b, b_b)
            cb_t = _mm_nt(b_b, c_b)
            x_g = xc_ref[:, cols]
            dy_g = dy_ref[:, cols]
            dt_g = _expand_group(dt, g, lane)
            acs_g = _expand_group(acs, g, lane)
            last_g = acs_g[CHUNK - 1:CHUNK, :]
            e_g = jnp.exp(acs_g)
            dte_g = jnp.exp(last_g - acs_g)
            xdt_g = x_g * dt_g
            xdt_b = xdt_g.astype(BF16)
            h_g = hs_ref[0, g]
            dh_g = dh_scr[g]
            h_b = h_g.astype(BF16)
            dh_b = dh_g.astype(BF16)
            heads = list(range(HEADS_PER_GROUP * g, HEADS_PER_GROUP * (g + 1)))
            segs = [acs[:, h:h + 1] - acs_t[h:h + 1, :] for h in heads]
            lms = [jnp.exp(jnp.where(causal, sg, NEG_BIG)) for sg in segs]
            mts = [(cb_t * jnp.exp(jnp.where(upper, -sg, NEG_BIG))).astype(BF16) for sg in segs]
            dyh = []
            for k in range(HEADS_PER_GROUP):
                blk = dy_g[:, LANES * (k // 2):LANES * (k // 2 + 1)]
                in_head = (lane < HEAD_DIM) if k % 2 == 0 else (lane >= HEAD_DIM)
                dyh.append(jnp.where(in_head, blk, 0.0).astype(BF16))
            dms = [_mm_nt(dyh[k], xdt_b[:, LANES * (k // 2):LANES * (k // 2 + 1)]) for k in range(HEADS_PER_GROUP)]
            dxs = [_mm(mts[k], dyh[k]) for k in range(HEADS_PER_GROUP)]
            dcb = jnp.zeros((CHUNK, CHUNK), F32)
            for k, h in enumerate(heads):
                gmat = dms[k] * (cb * lms[k])
                dacs_c = dacs_c + jnp.where(lane == h, jnp.sum(gmat, axis=1, keepdims=True), 0.0)
                dacs_r = dacs_r - jnp.where(sub == h, jnp.sum(gmat, axis=0, keepdims=True), 0.0)
                dcb = dcb + dms[k] * lms[k]
            dxdt_g = jnp.concatenate([dxs[2 * k] + dxs[2 * k + 1] for k in range(4)], axis=1)
            t_g = _mm_nt(c_b, h_b)
            dacs_c = dacs_c + _head_sums(dy_g * e_g * t_g, g)
            dt_b = (dy_g * e_g).astype(BF16)
            dc_acc = _mm(dt_b, h_b)
            dh_prev = _mm_tn(dt_b, c_b)
            dw_g = _mm_nt(b_b, dh_b)
            w_g = xdt_g * dte_g
            dxdt_g = dxdt_g + dw_g * dte_g
            db_acc = _mm(w_g.astype(BF16), dh_b)
            r2 = _head_sums(dw_g * w_g, g)
            dacs_c = dacs_c + jnp.where(is_last, jnp.sum(r2, axis=0, keepdims=True), 0.0) - r2
            q3 = jnp.sum(dh_g * h_g, axis=1, keepdims=True)
            for k, h in enumerate(heads):
                tot = jnp.sum(q3[HEAD_DIM * k:HEAD_DIM * (k + 1), :], keepdims=True) * cd[:, h:h + 1]
                dacs_c = dacs_c + jnp.where(is_last & (lane == h), tot, 0.0)
            dh_scr[g] = dh_prev + dh_g * jnp.exp(_rows_from_lanes(last_g))
            dxc_ref[:, cols] = dxdt_g * dt_g + dsk_ref[:, cols] * dy_g
            ddtx = ddtx + _head_sums(dxdt_g * x_g, g)
            ddsk_ref[:, cols] += jnp.sum(dy_g * x_g, axis=0, keepdims=True)
            dxc_ref[:, b_off:b_off + D_STATE] = db_acc + _mm(dcb.T.astype(BF16), c_b)
            dxc_ref[:, c_off:c_off + D_STATE] = dc_acc + _mm(dcb.astype(BF16), b_b)
        dacs = dacs_c + dacs_r.T
        dadt = _mm_exact((lane >= sub).astype(F32), dacs)
        ddt = dadt * a_row_v + ddtx
        ddt_raw = ddt * sig
        ddt_ref[...] = ddt_raw
        da_ref[...] += jnp.sum(dadt * dt, axis=0, keepdims=True)
        ddtb_ref[...] += jnp.sum(ddt_raw, axis=0, keepdims=True)

    return pl.pallas_call(
        body, name="ssd_bwd",
        out_shape=(jax.ShapeDtypeStruct((s, CONV_CH), F32), jax.ShapeDtypeStruct((s, LANES), F32),
                   jax.ShapeDtypeStruct((1, LANES), F32), jax.ShapeDtypeStruct((1, LANES), F32),
                   jax.ShapeDtypeStruct((1, SSD_WIDTH), F32)),
        grid=(nc,),
        in_specs=[pl.BlockSpec((CHUNK, CONV_CH), lambda c: (rev(c), 0)),
                  pl.BlockSpec((CHUNK, LANES), lambda c: (rev(c), 0)),
                  pl.BlockSpec((1, N_GROUPS, GROUP_WIDTH, D_STATE), lambda c: (rev(c), 0, 0, 0)),
                  pl.BlockSpec((CHUNK, SSD_WIDTH), lambda c: (rev(c), 0)),
                  _const_spec((1, LANES)), _const_spec((1, LANES)), _const_spec((1, SSD_WIDTH))],
        out_specs=(pl.BlockSpec((CHUNK, CONV_CH), lambda c: (rev(c), 0)),
                   pl.BlockSpec((CHUNK, LANES), lambda c: (rev(c), 0)),
                   _const_spec((1, LANES)), _const_spec((1, LANES)), _const_spec((1, SSD_WIDTH))),
        scratch_shapes=[pltpu.VMEM((N_GROUPS, GROUP_WIDTH, D_STATE), F32)],
        compiler_params=_params(("arbitrary",)),
    )(xc, small, states, dy, dtb_row, a_row, dskip_lane)


FORGET_BLOCK = 512


def forget_cumsum(small, fgb_row):
    s = small.shape[0]
    t = _blk(s, FORGET_BLOCK)
    nb = s // t

    def body(sm_ref, b_ref, cc_ref, carry):
        i = pl.program_id(0)

        @pl.when(i == 0)
        def _():
            carry[...] = jnp.zeros_like(carry)

        lane = _iota((t, LANES), 1)
        in_f = (lane >= N_HEADS) & (lane < 2 * N_HEADS)
        logf = jnp.where(in_f, -_softplus(-(sm_ref[...] + b_ref[...])), 0.0)
        tri = (_iota((t, t), 1) <= _iota((t, t), 0)).astype(F32)
        cum = _mm_exact(tri, logf) + carry[0:1, :]
        cc_ref[...] = cum
        carry[...] = jnp.broadcast_to(cum[t - 1:t, :], (8, LANES))

    return pl.pallas_call(
        body, name="forget_cumsum",
        out_shape=jax.ShapeDtypeStruct((s, LANES), F32),
        grid=(nb,),
        in_specs=[pl.BlockSpec((t, LANES), lambda i: (i, 0)), _const_spec((1, LANES))],
        out_specs=pl.BlockSpec((t, LANES), lambda i: (i, 0)),
        scratch_shapes=[pltpu.VMEM((8, LANES), F32)],
        compiler_params=_params(("arbitrary",)),
    )(small, fgb_row)


def forget_bwd(dc, small, ddt_raw, fgb_row):
    s = small.shape[0]
    t = _blk(s, FORGET_BLOCK)
    nb = s // t
    rev = lambda i: nb - 1 - i

    def body(dc_ref, sm_ref, ddt_ref, b_ref, ds_ref, dfb_ref, carry):
        i = pl.program_id(0)

        @pl.when(i == 0)
        def _():
            carry[...] = jnp.zeros_like(carry)
            dfb_ref[...] = jnp.zeros_like(dfb_ref)

        lane = _iota((t, LANES), 1)
        rows = dc_ref[...].T
        tri = (_iota((t, t), 1) <= _iota((t, t), 0)).astype(F32)
        rc = _mm_exact(rows, tri) + carry[:, 0:1]
        carry[...] = jnp.broadcast_to(rc[:, 0:1], (LANES, LANES))
        in_f = (lane >= N_HEADS) & (lane < 2 * N_HEADS)
        df = jnp.where(in_f, rc.T * _sigmoid(-(sm_ref[...] + b_ref[...])), 0.0)
        ds_ref[...] = (df + ddt_ref[...]).astype(BF16)
        dfb_ref[...] += jnp.sum(df, axis=0, keepdims=True)

    blk = pl.BlockSpec((t, LANES), lambda i: (rev(i), 0))
    return pl.pallas_call(
        body, name="forget_bwd",
        out_shape=(jax.ShapeDtypeStruct((s, LANES), BF16), jax.ShapeDtypeStruct((1, LANES), F32)),
        grid=(nb,),
        in_specs=[blk, blk, blk, _const_spec((1, LANES))],
        out_specs=(blk, _const_spec((1, LANES))),
        scratch_shapes=[pltpu.VMEM((LANES, LANES), F32)],
        compiler_params=_params(("arbitrary",)),
    )(dc, small, ddt_raw, fgb_row)


ATT_BLOCK = 512
ATT_SCALE = HEAD_DIM ** -0.5
AUG_A = HEAD_DIM
AUG_B = HEAD_DIM + 3


def _split3(c):
    hi = c.astype(BF16).astype(F32)
    r = c - hi
    mid = r.astype(BF16).astype(F32)
    return hi, mid, (r - mid).astype(BF16).astype(F32)


def _aug(lane, first, parts=None, value=1.0):
    if parts is None:
        return jnp.where((lane >= first) & (lane < first + 3), value, 0.0)
    return (jnp.where(lane == first, parts[0], 0.0) + jnp.where(lane == first + 1, parts[1], 0.0)
            + jnp.where(lane == first + 2, parts[2], 0.0))


def _pack_pair(a0, a1, lane):
    return jnp.where(lane < HEAD_DIM, a0, pltpu.roll(a1, HEAD_DIM, 1))


def proj_qkv_heads(u, w_q, w_k, w_v, cum):
    s = u.shape[0]
    tm = _blk(s, 256)

    def body(u_ref, wq_ref, wk_ref, wv_ref, c_ref, qa_ref, ka_ref, va_ref, nrm_ref):
        lane = _iota((tm, LANES), 1)
        lo = lane < HEAD_DIM
        uv = u_ref[...]
        qf = _mm(uv, wq_ref[...]) * ATT_SCALE
        kf = _mm(uv, wk_ref[...])
        vf = _mm(uv, wv_ref[...])
        cc = c_ref[...]
        ones_a = _aug(lane, AUG_A)
        ones_b = _aug(lane, AUG_B)
        sub8 = _iota((8, LANES), 0)
        nrm = jnp.zeros((8, LANES), F32)
        for h in range(N_HEADS):
            j, e = divmod(h, 2)

            def head(full):
                blk = full[:, LANES * j:LANES * (j + 1)]
                if e == 1:
                    blk = pltpu.roll(blk, HEAD_DIM, 1)
                return jnp.where(lo, blk, 0.0)

            parts = _split3(cc[:, N_HEADS + h:N_HEADS + h + 1])
            qh, kh = head(qf), head(kf)
            qa_ref[h] = (qh + _aug(lane, AUG_A, parts) + ones_b).astype(BF16)
            ka_ref[h] = (kh + ones_a - _aug(lane, AUG_B, parts)).astype(BF16)
            va_ref[h] = (head(vf) + ones_a).astype(BF16)
        seg = (_iota((ATT_WIDTH, LANES), 1) == (_iota((ATT_WIDTH, LANES), 0) >> 6)).astype(BF16)
        for r, val in enumerate((qf, kf)):
            sq = val * val
            hi = sq.astype(BF16)
            tot = _mm(hi, seg) + _mm((sq - hi.astype(F32)).astype(BF16), seg)
            nrm = nrm + jnp.where(sub8 == r, jnp.max(tot, axis=0, keepdims=True), 0.0)
        nrm_ref[0] = nrm

    shp = jax.ShapeDtypeStruct((N_HEADS, s, LANES), BF16)
    hspec = pl.BlockSpec((N_HEADS, tm, LANES), lambda i: (0, i, 0))
    wspec = _const_spec((D_MODEL, ATT_WIDTH))
    return pl.pallas_call(
        body, name="proj_qkv_heads",
        out_shape=(shp, shp, shp, jax.ShapeDtypeStruct((s // tm, 8, LANES), F32)), grid=(s // tm,),
        in_specs=[pl.BlockSpec((tm, D_MODEL), lambda i: (i, 0)), wspec, wspec, wspec,
                  pl.BlockSpec((tm, LANES), lambda i: (i, 0))],
        out_specs=(hspec, hspec, hspec, pl.BlockSpec((1, 8, LANES), lambda i: (i, 0, 0))),
        compiler_params=_params(("parallel",)),
    )(u, w_q, w_k, w_v, cum)


SKIP_BELOW = -110.0
CALM_BOUND = 60.0


def live_blocks(norms, cum, t):
    qn = jnp.sqrt(jnp.max(norms[:, 0, :N_HEADS], axis=0))
    kn = jnp.sqrt(jnp.max(norms[:, 1, :N_HEADS], axis=0))
    bound = 2.05 * qn * kn + 2.0
    c_first = cum[0::t, N_HEADS:2 * N_HEADS]
    c_last = cum[t - 1::t, N_HEADS:2 * N_HEADS]
    nq = c_first.shape[0]
    top = bound[None, None, :] + c_first[:, None, :] - c_last[None, :, :]
    below = jnp.arange(nq)[None, :] < jnp.arange(nq)[:, None]
    dead = below[:, :, None] & ~(top >= SKIP_BELOW)
    first = jnp.sum(dead, axis=1).astype(jnp.int32).T
    last_q = jnp.sum(first[:, None, :] <= jnp.arange(nq)[None, :, None], axis=2).astype(jnp.int32) - 1
    calm = (jnp.max(bound.reshape(N_PAIRS, 2), axis=1) <= CALM_BOUND).astype(jnp.int32)
    return first, last_q, calm


def attention_fwd(first, calm, qa, ka, va):
    s = qa.shape[1]
    t = _blk(s, ATT_BLOCK)
    nq = s // t

    def body(first_ref, calm_ref, qa_ref, ka_ref, va_ref, o_ref, qb_ref,
             m_scr, acc_scr, alpha_scr, p_scr, s_scr, qm_scr):
        qi = pl.program_id(1)
        starts = [first_ref[2 * pl.program_id(0) + e, qi] for e in range(2)]
        k0 = jnp.maximum(starts[0], starts[1])
        m_scr[...] = jnp.full_like(m_scr, NEG_BIG)
        acc_scr[...] = jnp.zeros_like(acc_scr)

        def kv_rows(kb):
            return pl.ds(pl.multiple_of(kb * t, t), t)

        def logits(kb, masked, heads=(0, 1)):
            for e in heads:
                sc = _mm_nt(qa_ref[e], ka_ref[e, kv_rows(kb), :])
                if masked:
                    sc = jnp.where(_iota((t, t), 0) >= _iota((t, t), 1), sc, NEG_BIG)
                s_scr[e] = sc

        def probs(heads=(0, 1)):
            for e in heads:
                cmax = s_scr[e, :, 0:LANES]
                for c in range(1, t // LANES):
                    cmax = jnp.maximum(cmax, s_scr[e, :, LANES * c:LANES * (c + 1)])
                m_old = m_scr[e]
                m_new = jnp.maximum(m_old, jnp.max(cmax, axis=1, keepdims=True))
                alpha_scr[e] = jnp.exp(m_old - m_new)
                m_scr[e] = m_new
                for c in range(t // LANES):
                    cols = slice(LANES * c, LANES * (c + 1))
                    p_scr[e, :, cols] = jnp.exp(s_scr[e, :, cols] - m_new).astype(BF16)

        def accumulate(kb, heads=(0, 1)):
            for e in heads:
                acc_scr[e] = alpha_scr[e] * acc_scr[e] + _mm(p_scr[e], va_ref[e, kv_rows(kb), :])

        def online_softmax():
            for e in range(2):
                def alone(kb, carry, e=e):
                    logits(kb, False, (e,))
                    probs((e,))
                    accumulate(kb, (e,))
                    return carry

                lax.fori_loop(starts[e], k0, alone, 0)

            def loop_body(kb, carry):
                logits(kb, False)
                for e in range(2):
                    accumulate(kb - 1, (e,))
                    probs((e,))
                return carry

            @pl.when(qi > k0)
            def _():
                logits(k0, False)
                probs()

            lax.fori_loop(k0 + 1, qi, loop_body, 0)

            @pl.when(qi > k0)
            def _():
                logits(qi, True)
                accumulate(qi - 1)
                probs()

            @pl.when(qi == k0)
            def _():
                logits(qi, True)
                probs()

            accumulate(qi)

        def diagonal_reference():
            logits(qi, True)
            probs()
            accumulate(qi)
            lane = _iota((t, LANES), 1)
            for e in range(2):
                q32 = qa_ref[e].astype(F32)
                c = q32[:, AUG_A:AUG_A + 1] + q32[:, AUG_A + 1:AUG_A + 2] + q32[:, AUG_A + 2:AUG_A + 3]
                shifted = _aug(lane, AUG_A, _split3(c - m_scr[e][:, 0:1]))
                qm_scr[e] = (jnp.where(lane < HEAD_DIM, q32, 0.0) + shifted + _aug(lane, AUG_B)).astype(BF16)

            def older(kb, heads):
                ps = [jnp.exp(_mm_nt(qm_scr[e], ka_ref[e, kv_rows(kb), :])).astype(BF16) for e in heads]
                for e, p in zip(heads, ps):
                    acc_scr[e] += _mm(p, va_ref[e, kv_rows(kb), :])

            def both(kb, carry):
                older(kb, (0, 1))
                return carry

            lax.fori_loop(k0, qi, both, 0)
            for e in range(2):
                def alone(kb, carry, e=e):
                    older(kb, (e,))
                    return carry

                lax.fori_loop(starts[e], k0, alone, 0)

        calm = calm_ref[pl.program_id(0)]

        @pl.when(calm == 1)
        def _():
            diagonal_reference()

        @pl.when(calm == 0)
        def _():
            online_softmax()

        lane = _iota((t, LANES), 1)
        outs = []
        for e in range(2):
            acc = acc_scr[e]
            l = acc[:, AUG_A:AUG_A + 1]
            outs.append(acc / l)
            lse = m_scr[e][:, 0:1] + jnp.log(l)
            q32 = qa_ref[e].astype(F32)
            c = q32[:, AUG_A:AUG_A + 1] + q32[:, AUG_A + 1:AUG_A + 2] + q32[:, AUG_A + 2:AUG_A + 3]
            qb = jnp.where(lane < HEAD_DIM, q32, 0.0) + _aug(lane, AUG_A, _split3(c - lse)) + _aug(lane, AUG_B)
            qb_ref[e] = qb.astype(BF16)
        o_ref[...] = _pack_pair(outs[0], outs[1], lane)

    grid_spec = pltpu.PrefetchScalarGridSpec(
        num_scalar_prefetch=2, grid=(N_PAIRS, nq),
        in_specs=[pl.BlockSpec((2, t, LANES), lambda j, qi, f, c: (j, qi, 0)),
                  pl.BlockSpec((2, s, LANES), lambda j, qi, f, c: (j, 0, 0)),
                  pl.BlockSpec((2, s, LANES), lambda j, qi, f, c: (j, 0, 0))],
        out_specs=[pl.BlockSpec((t, LANES), lambda j, qi, f, c: (qi, j)),
                   pl.BlockSpec((2, t, LANES), lambda j, qi, f, c: (j, qi, 0))],
        scratch_shapes=[pltpu.VMEM((2, t, LANES), F32), pltpu.VMEM((2, t, LANES), F32),
                        pltpu.VMEM((2, t, LANES), F32), pltpu.VMEM((2, t, t), BF16), pltpu.VMEM((2, t, t), F32),
                        pltpu.VMEM((2, t, LANES), BF16)])
    return pl.pallas_call(
        body, name="attention_fwd", grid_spec=grid_spec,
        out_shape=(jax.ShapeDtypeStruct((s, ATT_WIDTH), F32), jax.ShapeDtypeStruct((N_HEADS, s, LANES), BF16)),
        compiler_params=_params(("parallel", "parallel")),
    )(first, calm, qa, ka, va)


def attention_bwd(last_q, qb, ka, va, dob):
    s = qb.shape[1]
    t = _blk(s, ATT_BLOCK)
    nq = s // t

    def body(last_ref, qb_ref, dob_ref, ka_ref, va_ref, dq_ref, dk_ref, dv_ref, dc_ref, dq_scr, dk_scr, dv_scr):
        j, ki = pl.program_id(0), pl.program_id(1)

        @pl.when((j == 0) & (ki == 0))
        def _():
            dc_ref[...] = jnp.zeros_like(dc_ref)

        @pl.when(ki == 0)
        def _():
            dq_scr[...] = jnp.zeros_like(dq_scr)

        dk_scr[...] = jnp.zeros_like(dk_scr)
        dv_scr[...] = jnp.zeros_like(dv_scr)

        def q_step(qblk, masked, heads=(0, 1)):
            rows = pl.ds(pl.multiple_of(qblk * t, t), t)
            scs = [_mm_nt(qb_ref[e, rows, :], ka_ref[e]) for e in heads]
            dps = [_mm_nt(dob_ref[e, rows, :], va_ref[e]) for e in heads]
            for e, sc, dp in zip(heads, scs, dps):
                q = qb_ref[e, rows, :]
                do = dob_ref[e, rows, :]
                if masked:
                    sc = jnp.where(_iota((t, t), 0) >= _iota((t, t), 1), sc, NEG_BIG)
                p = jnp.exp(sc)
                ds_b = (p * dp).astype(BF16)
                dv_scr[e] += _mm_tn(p.astype(BF16), do)
                dk_scr[e] += _mm_tn(ds_b, q)
                dq_scr[e, rows, :] += _mm(ds_b, ka_ref[e])

        def loop_body(qblk, carry):
            q_step(qblk, False)
            return carry

        ends = [last_ref[2 * j + e, ki] + 1 for e in range(2)]
        both = jnp.minimum(ends[0], ends[1])
        q_step(ki, True)
        lax.fori_loop(ki + 1, both, loop_body, 0)
        for e in range(2):
            def alone(qblk, carry, e=e):
                q_step(qblk, False, (e,))
                return carry

            lax.fori_loop(both, ends[e], alone, 0)

        lane = _iota((t, LANES), 1)
        dk_ref[...] = _pack_pair(dk_scr[0], dk_scr[1], lane).astype(BF16)
        dv_ref[...] = _pack_pair(dv_scr[0], dv_scr[1], lane).astype(BF16)
        rows = pl.ds(pl.multiple_of(ki * t, t), t)
        dc_ref[rows, :] -= (jnp.where(lane == N_HEADS + 2 * j, dk_scr[0][:, AUG_B:AUG_B + 1], 0.0)
                            + jnp.where(lane == N_HEADS + 2 * j + 1, dk_scr[1][:, AUG_B:AUG_B + 1], 0.0))

        @pl.when(ki == nq - 1)
        def _():
            for blk in range(nq):
                rws = pl.ds(blk * t, t)
                d0 = dq_scr[0, rws, :]
                d1 = dq_scr[1, rws, :]
                dq_ref[rws, :] = (_pack_pair(d0, d1, lane) * ATT_SCALE).astype(BF16)
                dc_ref[rws, :] += (jnp.where(lane == N_HEADS + 2 * j, d0[:, AUG_A:AUG_A + 1], 0.0)
                                   + jnp.where(lane == N_HEADS + 2 * j + 1, d1[:, AUG_A:AUG_A + 1], 0.0))

    full = pl.BlockSpec((2, s, LANES), lambda j, ki, f: (j, 0, 0))
    blk = pl.BlockSpec((2, t, LANES), lambda j, ki, f: (j, ki, 0))
    pair = pl.BlockSpec((t, LANES), lambda j, ki, f: (ki, j))
    wide = jax.ShapeDtypeStruct((s, ATT_WIDTH), BF16)
    grid_spec = pltpu.PrefetchScalarGridSpec(
        num_scalar_prefetch=1, grid=(N_PAIRS, nq),
        in_specs=[full, full, blk, blk],
        out_specs=[pl.BlockSpec((s, LANES), lambda j, ki, f: (0, j)), pair, pair,
                   pl.BlockSpec((s, LANES), lambda j, ki, f: (0, 0))],
        scratch_shapes=[pltpu.VMEM((2, s, LANES), F32), pltpu.VMEM((2, t, LANES), F32),
                        pltpu.VMEM((2, t, LANES), F32)])
    return pl.pallas_call(
        body, name="attention_bwd", grid_spec=grid_spec,
        out_shape=(wide, wide, wide, jax.ShapeDtypeStruct((s, LANES), F32)),
        compiler_params=_params(("arbitrary", "arbitrary")),
    )(last_q, qb, dob, ka, va)


def _dsilu(z, sg):
    return sg * (1.0 + z * (1.0 - sg))


def post_mix(x, y, zs, o, za, p, tgt, ssd_g, att_g_lane, ple_g, fin_g, w_out, w_gate, w_proj):
    s = x.shape[0]
    tm = _blk(s, 128)
    half = SSD_WIDTH // N_GROUPS

    def rms_bwd(dy, yn, r):
        return r * (dy - yn * jnp.mean(dy * yn, axis=-1, keepdims=True))

    def colsum(a):
        return jnp.sum(a, axis=0, keepdims=True)

    def body(x_ref, y_ref, zs_ref, o_ref, za_ref, p_ref, t_ref, sg_ref, ag_ref, pg_ref, fg_ref,
             wo_ref, wg_ref, wp_ref,
             dh1_ref, dy_ref, dzs_ref, dob_ref, dza_ref, ycat_ref, dh1b_ref, n2b_ref, dglb_ref, dppb_ref, pb_ref,
             loss_ref, dfin_ref, dple_ref, dssd_ref, datt_ref):
        @pl.when(pl.program_id(0) == 0)
        def _():
            for r in (loss_ref, dfin_ref, dple_ref, dssd_ref, datt_ref):
                r[...] = jnp.zeros_like(r)

        lane = _iota((tm, LANES), 1)
        lo = lane < HEAD_DIM
        zs = zs_ref[...]
        sz = _sigmoid(zs)
        yv = y_ref[...]
        ys = yv * (zs * sz)
        yn, rg = [], []
        for g in range(N_GROUPS):
            seg = ys[:, half * g:half * (g + 1)]
            r = lax.rsqrt(jnp.mean(seg * seg, axis=-1, keepdims=True) + EPS)
            yn.append(seg * r)
            rg.append(r)
            ycat_ref[:, half * g:half * (g + 1)] = (yn[g] * sg_ref[:, half * g:half * (g + 1)]).astype(BF16)
        za = za_ref[...]
        sza = _sigmoid(za)
        silu_za = za * sza
        on, ra = [], []
        for jb in range(N_PAIRS):
            blk = o_ref[:, LANES * jb:LANES * (jb + 1)]
            sq = blk * blk
            ms0 = jnp.sum(jnp.where(lo, sq, 0.0), axis=1, keepdims=True) * (1.0 / HEAD_DIM)
            ms1 = jnp.sum(jnp.where(lo, 0.0, sq), axis=1, keepdims=True) * (1.0 / HEAD_DIM)
            r = jnp.where(lo, lax.rsqrt(ms0 + EPS), lax.rsqrt(ms1 + EPS))
            on.append(blk * r)
            ra.append(r)
            an = on[jb] * ag_ref[:, LANES * jb:LANES * (jb + 1)]
            ycat_ref[:, SSD_WIDTH + LANES * jb:SSD_WIDTH + LANES * (jb + 1)] = (
                an * silu_za[:, LANES * jb:LANES * (jb + 1)]).astype(BF16)
        h1 = x_ref[...] + _mm(ycat_ref[...], wo_ref[...])
        r2 = lax.rsqrt(jnp.mean(h1 * h1, axis=-1, keepdims=True) + EPS)
        n2h = h1 * r2
        n2_b = (n2h * pg_ref[...]).astype(BF16)
        gate = _sigmoid(_mm(n2_b, wg_ref[...]))
        p_b = p_ref[...].astype(BF16)
        pp = _mm(p_b, wp_ref[...])
        h2 = h1 + gate * pp
        r3 = lax.rsqrt(jnp.mean(h2 * h2, axis=-1, keepdims=True) + EPS)
        n3 = h2 * r3
        diff = n3 * fg_ref[...] - t_ref[...]
        sq = colsum(diff * diff)
        part = sq[:, 0:LANES]
        for jb in range(1, D_MODEL // LANES):
            part = part + sq[:, LANES * jb:LANES * (jb + 1)]
        loss_ref[...] += part * (0.5 / D_MODEL)
        dout = diff * (1.0 / D_MODEL)
        dfin_ref[...] += colsum(dout * n3)
        dh2 = rms_bwd(dout * fg_ref[...], n3, r3)
        dgl = dh2 * pp * gate * (1.0 - gate)
        dgl_b = dgl.astype(BF16)
        dn2 = _mm_nt(dgl_b, wg_ref[...])
        dple_ref[...] += colsum(dn2 * n2h)
        dh1 = dh2 + rms_bwd(dn2 * pg_ref[...], n2h, r2)
        dh1_b = dh1.astype(BF16)
        dycat = _mm_nt(dh1_b, wo_ref[...])
        dh1_ref[...] = dh1
        dh1b_ref[...] = dh1_b
        n2b_ref[...] = n2_b
        dglb_ref[...] = dgl_b
        dppb_ref[...] = (dh2 * gate).astype(BF16)
        pb_ref[...] = p_b
        for g in range(N_GROUPS):
            cols = slice(half * g, half * (g + 1))
            dys_g = dycat[:, cols]
            dssd_ref[:, cols] += colsum(dys_g * yn[g])
            dys = rms_bwd(dys_g * sg_ref[:, cols], yn[g], rg[g])
            dy_ref[:, cols] = dys * (zs[:, cols] * sz[:, cols])
            dzs_ref[:, cols] = (dys * yv[:, cols] * _dsilu(zs[:, cols], sz[:, cols])).astype(BF16)
        for jb in range(N_PAIRS):
            cols = slice(LANES * jb, LANES * (jb + 1))
            dya = dycat[:, SSD_WIDTH + LANES * jb:SSD_WIDTH + LANES * (jb + 1)]
            ag = ag_ref[:, cols]
            dan = dya * silu_za[:, cols]
            dza_ref[:, cols] = (dya * (on[jb] * ag) * _dsilu(za[:, cols], sza[:, cols])).astype(BF16)
            datt_ref[:, cols] += colsum(dan * on[jb])
            don = dan * ag
            q = don * on[jb]
            m0 = jnp.sum(jnp.where(lo, q, 0.0), axis=1, keepdims=True) * (1.0 / HEAD_DIM)
            m1 = jnp.sum(jnp.where(lo, 0.0, q), axis=1, keepdims=True) * (1.0 / HEAD_DIM)
            do2 = ra[jb] * (don - on[jb] * jnp.where(lo, m0, m1))
            prod = do2 * o_ref[:, cols]
            for e in range(2):
                delta = jnp.sum(jnp.where(lo, prod, 0.0) if e == 0 else jnp.where(lo, 0.0, prod),
                                axis=1, keepdims=True)
                base = jnp.where(lo, do2 if e == 0 else pltpu.roll(do2, HEAD_DIM, 1), 0.0)
                dob_ref[2 * jb + e] = (base - _aug(lane, AUG_A, _split3(delta))).astype(BF16)

    def rows(n, dtype=None):
        return pl.BlockSpec((tm, n), lambda i: (i, 0))

    def out(n, dtype):
        return jax.ShapeDtypeStruct((s, n), dtype)

    vec = _const_spec((1, D_MODEL))
    vshape = jax.ShapeDtypeStruct((1, D_MODEL), F32)
    return pl.pallas_call(
        body, name="post_mix",
        out_shape=(out(D_MODEL, F32), out(SSD_WIDTH, F32), out(SSD_WIDTH, BF16),
                   jax.ShapeDtypeStruct((N_HEADS, s, LANES), BF16),
                   out(ATT_WIDTH, BF16), out(D_INNER, BF16), out(D_MODEL, BF16), out(D_MODEL, BF16),
                   out(D_MODEL, BF16), out(D_MODEL, BF16), out(PLE_DIM, BF16),
                   jax.ShapeDtypeStruct((1, LANES), F32), vshape, vshape, vshape, vshape),
        grid=(s // tm,),
        in_specs=[rows(D_MODEL), rows(SSD_WIDTH), rows(SSD_WIDTH), rows(ATT_WIDTH), rows(ATT_WIDTH),
                  rows(PLE_DIM), rows(D_MODEL), vec, vec, vec, vec,
                  _const_spec((D_INNER, D_MODEL)), _const_spec((D_MODEL, D_MODEL)), _const_spec((PLE_DIM, D_MODEL))],
        out_specs=(rows(D_MODEL), rows(SSD_WIDTH), rows(SSD_WIDTH),
                   pl.BlockSpec((N_HEADS, tm, LANES), lambda i: (0, i, 0)), rows(ATT_WIDTH),
                   rows(D_INNER), rows(D_MODEL), rows(D_MODEL), rows(D_MODEL), rows(D_MODEL), rows(PLE_DIM),
                   _const_spec((1, LANES)), vec, vec, vec, vec),
        compiler_params=_params(("arbitrary",)),
    )(x, y, zs, o, za, p, tgt, ssd_g, att_g_lane, ple_g, fin_g, w_out, w_gate, w_proj)


def in_proj_bwd(dsegs, wsegs, x, g, dh1, pres):
    s = x.shape[0]
    tm = _blk(s, 256)
    nseg = len(dsegs)
    nbig = len(pres)
    nsteps = s // tm

    def body(*refs):
        d_refs = refs[:nseg]
        w_refs = refs[nseg:2 * nseg]
        x_ref, g_ref, dh1_ref = refs[2 * nseg:2 * nseg + 3]
        rest = refs[2 * nseg + 3:]
        pre_refs, (dx_ref, dg_ref), part_refs = rest[:nbig], rest[nbig:nbig + 2], rest[nbig + 2:2 * nbig + 2]
        ssem, rsem, lsem = rest[2 * nbig + 2:]

        @pl.when(pl.program_id(0) == 0)
        def _():
            dg_ref[...] = jnp.zeros_like(dg_ref)
            for cp in scatter_copies(pre_refs, part_refs, ssem, rsem, lsem):
                cp.start()

        @pl.when(pl.program_id(0) == nsteps - 1)
        def _():
            for cp in scatter_copies(pre_refs, part_refs, ssem, rsem, lsem):
                cp.wait()

        du = _mm_nt(d_refs[0][...], w_refs[0][...])
        for k in range(1, nseg):
            du = du + _mm_nt(d_refs[k][...], w_refs[k][...])
        xv = x_ref[...]
        r = lax.rsqrt(jnp.mean(xv * xv, axis=-1, keepdims=True) + EPS)
        xh = xv * r
        dg_ref[...] += jnp.sum(du * xh, axis=0, keepdims=True)
        dxh = du * g_ref[...]
        dx_ref[...] = r * (dxh - xh * jnp.mean(dxh * xh, axis=-1, keepdims=True)) + dh1_ref[...]

    rows = lambda n: pl.BlockSpec((tm, n), lambda i: (i, 0))
    return pl.pallas_call(
        body, name="in_proj_bwd",
        out_shape=tuple([jax.ShapeDtypeStruct((s, D_MODEL), F32), jax.ShapeDtypeStruct((1, D_MODEL), F32)]
                        + [jax.ShapeDtypeStruct(a.shape, a.dtype) for a in pres]),
        grid=(nsteps,),
        in_specs=([rows(d.shape[1]) for d in dsegs] + [_const_spec(w.shape) for w in wsegs]
                  + [rows(D_MODEL), _const_spec((1, D_MODEL)), rows(D_MODEL)] + [ANY] * nbig),
        out_specs=tuple([rows(D_MODEL), _const_spec((1, D_MODEL))] + [ANY] * nbig),
        scratch_shapes=_sems(3 * nbig) + [pltpu.SemaphoreType.DMA((nbig,))],
        compiler_params=_params(("arbitrary",)),
    )(*dsegs, *wsegs, x, g, dh1, *pres)


SMALL_NAMES = ("norm_g", "conv_b", "dt_bias", "a_log", "d_skip", "ssd_norm_g", "fg_bias", "att_norm_g",
               "ple_norm_g", "final_norm_g")
SMALL_SIZES = (1024, 1536, 16, 16, 16, 1024, 16, 64, 1024, 1024)
CONV_W_SIZE = CONV_WIDTH * CONV_CH


def _pack_small(vals):
    flat = jnp.concatenate([v.reshape(-1).astype(F32) for v in vals])
    flat = jnp.pad(flat, (0, SMALL_ROWS * LANES - flat.shape[0]))
    return flat.reshape(SMALL_ROWS, LANES)


def _unpack_small(pack, shapes):
    flat = pack.reshape(-1)
    out, off = [], 0
    for n, shp in zip(SMALL_SIZES, shapes):
        out.append(flat[off:off + n].reshape(shp))
        off += n
    return out


def _row128(v16, offset=0):
    return jnp.pad(v16.reshape(1, N_HEADS).astype(F32), ((0, 0), (offset, LANES - N_HEADS - offset)))


def local_step(prereduce, x, p, tgt, w_in, w_out, w_gate, w_proj, conv_w, norm_g, conv_b, dt_bias, a_log, d_skip,
               ssd_norm_g, fg_bias, att_norm_g, ple_norm_g, final_norm_g):
    c0, c1, c2, c3, c4, c5, c6, c7 = 0, 1024, 2560, 2576, 3600, 4624, 5648, 6672
    w_zs, w_xbc, w_dt = w_in[:, c0:c1], w_in[:, c1:c2], w_in[:, c2:c3]
    w_za, w_q, w_k, w_v, w_f = w_in[:, c3:c4], w_in[:, c4:c5], w_in[:, c5:c6], w_in[:, c6:c7], w_in[:, c7:]
    w_small = jnp.concatenate([w_dt, w_f, jnp.zeros((D_MODEL, LANES - 2 * N_HEADS), BF16)], axis=1)

    dtb_row = _row128(dt_bias)
    a_row = _row128(-jnp.exp(a_log.astype(F32)))
    fgb_row = _row128(fg_bias, N_HEADS)
    dskip_lane = jnp.repeat(d_skip.astype(F32), HEAD_DIM).reshape(1, SSD_WIDTH)
    att_g_lane = jnp.tile(att_norm_g.astype(F32), N_HEADS).reshape(1, ATT_WIDTH)
    row = lambda v: v.reshape(1, -1).astype(F32)

    u = rms_prenorm(x, row(norm_g))
    zs = matmul_rows(u, w_zs, F32, "proj_z_ssd")
    xbc = matmul_rows(u, w_xbc, F32, "proj_xbc")
    za = matmul_rows(u, w_za, F32, "proj_z_att")
    small = matmul_rows(u, w_small, F32, "proj_small")
    cum = forget_cumsum(small, fgb_row)
    qa, ka, va, norms = proj_qkv_heads(u, w_q, w_k, w_v, cum)
    first, last_q, calm = live_blocks(norms, cum, _blk(x.shape[0], ATT_BLOCK))
    pre, xc = conv_fwd(xbc, conv_w, row(conv_b))
    y, states = ssd_fwd(xc, small, dtb_row, a_row, dskip_lane)
    o, qb = attention_fwd(first, calm, qa, ka, va)
    (dh1, dy, dzs, dob, dza, ycat, dh1_b, n2_b, dgl_b, dpp_b, p_b,
     loss_l, dfin, dple, dssd_g, datt_lane) = post_mix(
        x, y, zs, o, za, p, tgt, row(ssd_norm_g), att_g_lane, row(ple_norm_g), row(final_norm_g),
        w_out, w_gate, w_proj)
    dq, dk, dv, dc = attention_bwd(last_q, qb, ka, va, dob)
    dxc, ddt_raw, da, ddtb, ddsk_lane = ssd_bwd(xc, small, states, dy, dtb_row, a_row, dskip_lane)
    dsmall, dfgb = forget_bwd(dc, small, ddt_raw, fgb_row)
    dxbc, dconv_w8, dconv_b = conv_bwd(xbc, pre, dxc, conv_w)
    dsegs = [dzs, dxbc, dza, dq, dk, dv, dsmall]
    wsegs = [w_zs, w_xbc, w_za, w_q, w_k, w_v, w_small]
    dws = [matmul_tn(u, d, "dw_in_%d" % i) for i, d in enumerate(dsegs)]
    dw_in = jnp.concatenate([dws[0], dws[1], dws[6][:, :N_HEADS], dws[2], dws[3], dws[4], dws[5],
                             dws[6][:, N_HEADS:2 * N_HEADS]], axis=1)
    dw_out = matmul_tn(ycat, dh1_b, "dw_out")
    dw_gate = matmul_tn(n2_b, dgl_b, "dw_gate")
    dw_proj = matmul_tn(p_b, dpp_b, "dw_proj")
    dx, dnorm_g, *parts = in_proj_bwd(dsegs, wsegs, x, row(norm_g), dh1, prereduce(dw_in, dw_out, dw_gate, dw_proj))
    small_grads = [
        dnorm_g, dconv_b, ddtb[0, :N_HEADS], (da * a_row)[0, :N_HEADS],
        ddsk_lane.reshape(N_HEADS, HEAD_DIM).sum(axis=1), dssd_g, dfgb[0, N_HEADS:2 * N_HEADS],
        datt_lane.reshape(N_HEADS, HEAD_DIM).sum(axis=0), dple, dfin]
    loss = jnp.sum(loss_l)
    return loss, dx, parts, dconv_w8[:CONV_WIDTH], small_grads


def kernel(x, p, norm_g, w_in, conv_w, conv_b, dt_bias, a_log, d_skip, ssd_norm_g, fg_bias, att_norm_g, w_out, ple_norm_g, w_ple_gate, w_ple_proj, final_norm_g, loss_target, m_norm_g, m_w_in, m_conv_w, m_conv_b, m_dt_bias, m_a_log, m_d_skip, m_ssd_norm_g, m_fg_bias, m_att_norm_g, m_w_out, m_ple_norm_g, m_w_ple_gate, m_w_ple_proj, m_final_norm_g, v_norm_g, v_w_in, v_conv_w, v_conv_b, v_dt_bias, v_a_log, v_d_skip, v_ssd_norm_g, v_fg_bias, v_att_norm_g, v_w_out, v_ple_norm_g, v_w_ple_gate, v_w_ple_proj, v_final_norm_g):
    chip = 2 * lax.axis_index("x") + lax.axis_index("y")
    core = lax.axis_index("c")

    big_w = [w_in[0], w_out[0], w_ple_gate[0], w_ple_proj[0]]
    own = [a.astype(BF16) for a in big_w] + [conv_w[0]]
    gathered = gather_weights(own[:4], own[4])

    def joined(k, axis):
        return jnp.concatenate([jnp.where(chip == j, own[k], gathered[k][j]) for j in range(N_CHIPS)], axis=axis)

    w_in_f, w_out_f, w_gate_f, w_proj_f, conv_w_f = joined(0, 1), joined(1, 0), joined(2, 0), joined(3, 1), joined(4, 1)

    core1 = core.reshape(1).astype(jnp.int32)

    def prereduce(dw_in, dw_out, dw_gate, dw_proj):
        gs = [jnp.stack([dw_in[:, 1672 * j:1672 * (j + 1)] for j in range(N_CHIPS)]),
              dw_out.reshape(N_CHIPS, 512, D_MODEL), dw_gate.reshape(N_CHIPS, 256, D_MODEL),
              jnp.stack([dw_proj[:, 256 * j:256 * (j + 1)] for j in range(N_CHIPS)])]
        return add_halves(core1, gs, halves_to_sibling(gs))

    smalls_w = [norm_g, conv_b, dt_bias, a_log, d_skip, ssd_norm_g, fg_bias, att_norm_g, ple_norm_g, final_norm_g]
    loss_l, dx, parts, dconv_w, small_grads = local_step(
        prereduce, x[0], p[0, 0], loss_target[0], w_in_f, w_out_f, w_gate_f, w_proj_f, conv_w_f,
        *[a.reshape(-1) for a in smalls_w])
    loss = lax.psum(loss_l, ("x", "y", "c"))
    smalls = gather_small(_pack_small(list(small_grads) + [dconv_w]))
    mine = sum_parts(parts)

    g_big, d_big, m_big, v_big = adamw_big(
        core1, mine, swap_halves(mine), big_w, [m_w_in[0], m_w_out[0], m_w_ple_gate[0], m_w_ple_proj[0]],
        [v_w_in[0], v_w_out[0], v_w_ple_gate[0], v_w_ple_proj[0]])
    smalls_m = [m_norm_g, m_conv_b, m_dt_bias, m_a_log, m_d_skip, m_ssd_norm_g, m_fg_bias, m_att_norm_g,
                m_ple_norm_g, m_final_norm_g]
    smalls_v = [v_norm_g, v_conv_b, v_dt_bias, v_a_log, v_d_skip, v_ssd_norm_g, v_fg_bias, v_att_norm_g,
                v_ple_norm_g, v_final_norm_g]
    g_sm, d_sm, m_sm, v_sm = adamw_small(smalls, _pack_small(smalls_w), _pack_small(smalls_m), _pack_small(smalls_v))
    n_small = sum(SMALL_SIZES)
    g_conv_full = g_sm.reshape(-1)[n_small:n_small + CONV_W_SIZE].reshape(CONV_WIDTH, CONV_CH)
    g_conv = lax.dynamic_slice_in_dim(g_conv_full, chip * 384, 384, axis=1)
    d_conv, m_conv, v_conv = adamw_whole(g_conv, conv_w[0], m_conv_w[0], v_conv_w[0], "adamw_conv")

    shapes = [a.shape for a in smalls_w]
    outs = []
    for big, conv, sm in ((g_big, g_conv, g_sm), (d_big, d_conv, d_sm), (m_big, m_conv, m_sm), (v_big, v_conv, v_sm)):
        b_in, b_out, b_gate, b_proj = [a[None] for a in big]
        s_norm, s_convb, s_dtb, s_alog, s_dsk, s_ssdg, s_fgb, s_attg, s_pleg, s_fin = _unpack_small(sm, shapes)
        outs.extend([s_norm, b_in, conv[None], s_convb, s_dtb, s_alog, s_dsk, s_ssdg, s_fgb, s_attg, b_out, s_pleg,
                     b_gate, b_proj, s_fin])
    return (loss, dx[None], *outs)
```

```python
import functools

import jax
import jax.numpy as jnp
from jax import lax
from jax.experimental import pallas as pl
from jax.experimental.pallas import tpu as pltpu

F32 = jnp.float32
BF16 = jnp.bfloat16

D_MODEL = 1024
SSD_WIDTH = 1024
ATT_WIDTH = 1024
N_HEADS = 16
HEAD_DIM = 64
N_GROUPS = 2
D_STATE = 128
CONV_CH = 1536
CONV_WIDTH = 4
CHUNK = 128
PLE_DIM = 256
D_INNER = 2048
EPS = 1e-6
IN_COLS = 6688
N_CHIPS = 4
N_DEV = 8
LANES = 128
N_PAIRS = 8

ADAM_LR = 0.001
ADAM_B1 = 0.9
ADAM_B2 = 0.999
ADAM_EPS = 1e-08
ADAM_WD = 0.01
ADAM_STEP = 10

SMALL_ROWS = 96

NEG_BIG = -1e30
VMEM_LIMIT = 56 * 1024 * 1024

MESH = pl.DeviceIdType.MESH
ANY = pl.BlockSpec(memory_space=pl.ANY)


def _mm(a, b):
    return jnp.dot(a, b, preferred_element_type=F32)


def _mm_nt(a, b):
    return lax.dot_general(a, b, (((1,), (1,)), ((), ())), preferred_element_type=F32)


def _mm_tn(a, b):
    return lax.dot_general(a, b, (((0,), (0,)), ((), ())), preferred_element_type=F32)


def _mm_exact(a, b):
    return jnp.dot(a, b, preferred_element_type=F32, precision=lax.Precision.HIGHEST)


def _softplus(x):
    return jnp.maximum(x, 0.0) + jnp.log1p(jnp.exp(-jnp.abs(x)))


def _sigmoid(x):
    return jax.nn.sigmoid(x)


def _iota(shape, dim):
    return lax.broadcasted_iota(jnp.int32, shape, dim)


def _params(sem=None):
    return pltpu.CompilerParams(dimension_semantics=sem, vmem_limit_bytes=VMEM_LIMIT)


def _blk(n, pref):
    return min(n, pref)


def _const_spec(shape):
    nd = len(shape)
    return pl.BlockSpec(shape, lambda *_: (0,) * nd)


def _chip_peers():
    x, y, c = lax.axis_index("x"), lax.axis_index("y"), lax.axis_index("c")
    return x, y, c, [(1 - x, y, c), (x, 1 - y, c), (1 - x, 1 - y, c)]


def _half(rows, c):
    h = rows // 2
    return pl.ds(pl.multiple_of(c * h, 8), h)


def _sems(n):
    return [pltpu.SemaphoreType.DMA((n,)), pltpu.SemaphoreType.DMA((n,))]


def gather_weights(shards, conv_s):
    n = len(shards)

    def body(*refs):
        ins, conv_in = refs[:n], refs[n]
        outs, conv_out = refs[n + 1:2 * n + 1], refs[2 * n + 1]
        ssem1, rsem1, ssem2, rsem2, c_ssem, c_rsem = refs[2 * n + 2:]
        x, y, c, peers = _chip_peers()
        me = 2 * x + y
        sibling = (x, y, 1 - c)
        first, small = [], []
        for k, peer in enumerate(peers):
            for i in range(n):
                h = _half(ins[i].shape[0], c)
                first.append(pltpu.make_async_remote_copy(
                    src_ref=ins[i].at[h], dst_ref=outs[i].at[me, h], send_sem=ssem1.at[n * k + i],
                    recv_sem=rsem1.at[n * k + i], device_id=peer, device_id_type=MESH))
            small.append(pltpu.make_async_remote_copy(
                src_ref=conv_in, dst_ref=conv_out.at[me], send_sem=c_ssem.at[k], recv_sem=c_rsem.at[k],
                device_id=peer, device_id_type=MESH))
        for cp in first + small:
            cp.start()
        passed = []
        for k, peer in enumerate(peers):
            chip = 2 * peer[0] + peer[1]
            for i in range(n):
                h = _half(ins[i].shape[0], c)
                first[n * k + i].wait_recv()
                fwd = pltpu.make_async_remote_copy(
                    src_ref=outs[i].at[chip, h], dst_ref=outs[i].at[chip, h], send_sem=ssem2.at[n * k + i],
                    recv_sem=rsem2.at[n * k + i], device_id=sibling, device_id_type=MESH)
                fwd.start()
                passed.append(fwd)
        for cp in passed:
            cp.wait_recv()
        for cp in first + passed:
            cp.wait_send()
        for cp in small:
            cp.wait()

    return pl.pallas_call(
        body, name="gather_weights",
        out_shape=tuple(jax.ShapeDtypeStruct((N_CHIPS,) + a.shape, a.dtype) for a in list(shards) + [conv_s]),
        in_specs=[ANY] * (n + 1), out_specs=(ANY,) * (n + 1),
        scratch_shapes=_sems(3 * n) + _sems(3 * n) + _sems(3),
    )(*shards, conv_s)


def halves_to_sibling(gs):
    n = len(gs)

    def body(*refs):
        ins, outs = refs[:n], refs[n:2 * n]
        ssem, rsem = refs[2 * n:]
        x, y, c = lax.axis_index("x"), lax.axis_index("y"), lax.axis_index("c")
        copies = []
        for i in range(n):
            for j in range(N_CHIPS):
                copies.append(pltpu.make_async_remote_copy(
                    src_ref=ins[i].at[j, _half(ins[i].shape[1], 1 - c)], dst_ref=outs[i].at[j],
                    send_sem=ssem.at[N_CHIPS * i + j], recv_sem=rsem.at[N_CHIPS * i + j],
                    device_id=(x, y, 1 - c), device_id_type=MESH))
        for cp in copies:
            cp.start()
        for cp in copies:
            cp.wait()

    return pl.pallas_call(
        body, name="halves_to_sibling",
        out_shape=tuple(jax.ShapeDtypeStruct((N_CHIPS, g.shape[1] // 2, g.shape[2]), F32) for g in gs),
        in_specs=[ANY] * n, out_specs=(ANY,) * n, scratch_shapes=_sems(N_CHIPS * n),
    )(*gs)


RED_GRID = 8


def add_halves(core, gs, rbs):
    n = len(gs)

    def body(c_ref, *refs):
        for i in range(n):
            refs[2 * n + i][...] = (refs[i][...] + refs[n + i][...]).astype(BF16)

    def blk(g):
        return (1, g.shape[1] // 2 // RED_GRID, g.shape[2])

    grid_spec = pltpu.PrefetchScalarGridSpec(
        num_scalar_prefetch=1, grid=(N_CHIPS, RED_GRID),
        in_specs=([pl.BlockSpec(blk(g), lambda j, b, c_ref: (j, c_ref[0] * RED_GRID + b, 0)) for g in gs]
                  + [pl.BlockSpec(blk(g), lambda j, b, c_ref: (j, b, 0)) for g in gs]),
        out_specs=[pl.BlockSpec(blk(g), lambda j, b, c_ref: (j, b, 0)) for g in gs])
    return pl.pallas_call(
        body, name="add_halves", grid_spec=grid_spec,
        out_shape=tuple(jax.ShapeDtypeStruct(r.shape, BF16) for r in rbs),
        compiler_params=_params(("parallel", "parallel")),
    )(core, *gs, *rbs)


def scatter_copies(ins, outs, ssem, rsem, lsem):
    n = len(ins)
    x, y, _, peers = _chip_peers()
    me = 2 * x + y
    copies = [pltpu.make_async_copy(ins[i].at[me], outs[i].at[me], lsem.at[i]) for i in range(n)]
    for k, peer in enumerate(peers):
        dst_chip = 2 * peer[0] + peer[1]
        for i in range(n):
            copies.append(pltpu.make_async_remote_copy(
                src_ref=ins[i].at[dst_chip], dst_ref=outs[i].at[me], send_sem=ssem.at[n * k + i],
                recv_sem=rsem.at[n * k + i], device_id=peer, device_id_type=MESH))
    return copies


def gather_small(small):
    def body(s_ref, smalls_ref, ssem, rsem, lsem):
        x, y, c = lax.axis_index("x"), lax.axis_index("y"), lax.axis_index("c")
        dev = 4 * x + 2 * y + c
        copies = [pltpu.make_async_copy(s_ref, smalls_ref.at[dev], lsem)]
        for k in range(1, N_DEV):
            fx, fy, fc = (k >> 2) & 1, (k >> 1) & 1, k & 1
            peer = ((1 - x) if fx else x, (1 - y) if fy else y, (1 - c) if fc else c)
            copies.append(pltpu.make_async_remote_copy(
                src_ref=s_ref, dst_ref=smalls_ref.at[dev], send_sem=ssem.at[k - 1], recv_sem=rsem.at[k - 1],
                device_id=peer, device_id_type=MESH))
        for cp in copies:
            cp.start()
        for cp in copies:
            cp.wait()

    return pl.pallas_call(
        body, name="gather_small",
        out_shape=jax.ShapeDtypeStruct((N_DEV,) + small.shape, F32),
        in_specs=[ANY], out_specs=ANY,
        scratch_shapes=_sems(N_DEV - 1) + [pltpu.SemaphoreType.DMA],
    )(small)


def sum_parts(parts):
    n = len(parts)

    def body(*refs):
        for i in range(n):
            p_ref = refs[i]
            refs[n + i][...] = ((p_ref[0].astype(F32) + p_ref[1].astype(F32)) + p_ref[2].astype(F32)
                                ) + p_ref[3].astype(F32)

    def rows(p):
        return p.shape[1] // RED_GRID

    return pl.pallas_call(
        body, name="sum_parts",
        out_shape=tuple(jax.ShapeDtypeStruct(p.shape[1:], F32) for p in parts),
        grid=(RED_GRID,),
        in_specs=[pl.BlockSpec((N_CHIPS, rows(p), p.shape[2]), lambda b: (0, b, 0)) for p in parts],
        out_specs=tuple(pl.BlockSpec((rows(p), p.shape[2]), lambda b: (b, 0)) for p in parts),
        compiler_params=_params(("parallel",)),
    )(*parts)


def swap_halves(reds):
    n = len(reds)

    def body(*refs):
        ins, outs = refs[:n], refs[n:2 * n]
        ssem, rsem = refs[2 * n:]
        x, y, c = lax.axis_index("x"), lax.axis_index("y"), lax.axis_index("c")
        copies = [pltpu.make_async_remote_copy(
            src_ref=ins[i], dst_ref=outs[i], send_sem=ssem.at[i], recv_sem=rsem.at[i],
            device_id=(x, y, 1 - c), device_id_type=MESH) for i in range(n)]
        for cp in copies:
            cp.start()
        for cp in copies:
            cp.wait()

    return pl.pallas_call(
        body, name="swap_halves",
        out_shape=tuple(jax.ShapeDtypeStruct(r.shape, F32) for r in reds),
        in_specs=[ANY] * n, out_specs=(ANY,) * n, scratch_shapes=_sems(n),
    )(*reds)


def _adamw(w, g, m, v):
    m = ADAM_B1 * m + (1.0 - ADAM_B1) * g
    v = ADAM_B2 * v + (1.0 - ADAM_B2) * (g * g)
    m_hat = m / (1.0 - ADAM_B1 ** ADAM_STEP)
    v_hat = v / (1.0 - ADAM_B2 ** ADAM_STEP)
    delta = -ADAM_LR * (m_hat / (jnp.sqrt(v_hat) + ADAM_EPS) + ADAM_WD * w)
    return delta, m, v


def adamw_big(core, mine, theirs, ws, ms, vs):
    n = len(ws)
    per_half = RED_GRID // 2

    def body(c_ref, *refs):
        own = (pl.program_id(0) // per_half) == c_ref[0]
        for i in range(n):
            g = jnp.where(own, refs[i][...], refs[n + i][...])
            d, mn, vn = _adamw(refs[2 * n + i][...], g, refs[3 * n + i][...], refs[4 * n + i][...])
            refs[5 * n + i][...] = g
            refs[6 * n + i][...] = d
            refs[7 * n + i][...] = mn
            refs[8 * n + i][...] = vn

    def blk(w):
        return (w.shape[0] // RED_GRID, w.shape[1])

    halves = [pl.BlockSpec(blk(w), lambda b, c_ref: (b % per_half, 0)) for w in ws]
    whole = [pl.BlockSpec(blk(w), lambda b, c_ref: (b, 0)) for w in ws]
    shapes = [jax.ShapeDtypeStruct(w.shape, F32) for w in ws]
    grid_spec = pltpu.PrefetchScalarGridSpec(
        num_scalar_prefetch=1, grid=(RED_GRID,), in_specs=halves * 2 + whole * 3, out_specs=whole * 4)
    outs = pl.pallas_call(
        body, name="adamw_big", out_shape=tuple(shapes * 4), grid_spec=grid_spec,
        compiler_params=_params(("parallel",)),
    )(core, *mine, *theirs, *ws, *ms, *vs)
    return outs[:n], outs[n:2 * n], outs[2 * n:3 * n], outs[3 * n:]


def adamw_whole(g, w, m, v, name):
    def body(g_ref, w_ref, m_ref, v_ref, d_out, m_out, v_out):
        d, mn, vn = _adamw(w_ref[...], g_ref[...], m_ref[...], v_ref[...])
        d_out[...] = d
        m_out[...] = mn
        v_out[...] = vn

    shp = jax.ShapeDtypeStruct(g.shape, F32)
    return pl.pallas_call(body, name=name, out_shape=(shp,) * 3)(g, w, m, v)


def adamw_small(smalls, w, m, v):
    def body(s_ref, w_ref, m_ref, v_ref, g_out, d_out, m_out, v_out):
        g = s_ref[0]
        for k in range(1, N_DEV):
            g = g + s_ref[k]
        d, mn, vn = _adamw(w_ref[...], g, m_ref[...], v_ref[...])
        g_out[...] = g
        d_out[...] = d
        m_out[...] = mn
        v_out[...] = vn

    shp = jax.ShapeDtypeStruct((SMALL_ROWS, LANES), F32)
    return pl.pallas_call(body, name="adamw_small", out_shape=(shp,) * 4)(smalls, w, m, v)


def rms_prenorm(x, g):
    s = x.shape[0]
    tm = _blk(s, 512)

    def body(x_ref, g_ref, u_ref):
        xv = x_ref[...]
        r = lax.rsqrt(jnp.mean(xv * xv, axis=-1, keepdims=True) + EPS)
        u_ref[...] = (xv * r * g_ref[...]).astype(BF16)

    return pl.pallas_call(
        body, name="rms_prenorm", out_shape=jax.ShapeDtypeStruct(x.shape, BF16), grid=(s // tm,),
        in_specs=[pl.BlockSpec((tm, D_MODEL), lambda i: (i, 0)), _const_spec((1, D_MODEL))],
        out_specs=pl.BlockSpec((tm, D_MODEL), lambda i: (i, 0)), compiler_params=_params(("parallel",)),
    )(x, g)


def matmul_rows(a, w, out_dtype, name):
    s, k = a.shape
    n = w.shape[1]
    tm = _blk(s, 512)

    def body(a_ref, w_ref, o_ref):
        o_ref[...] = _mm(a_ref[...], w_ref[...]).astype(out_dtype)

    return pl.pallas_call(
        body, name=name, out_shape=jax.ShapeDtypeStruct((s, n), out_dtype), grid=(s // tm,),
        in_specs=[pl.BlockSpec((tm, k), lambda i: (i, 0)), _const_spec((k, n))],
        out_specs=pl.BlockSpec((tm, n), lambda i: (i, 0)), compiler_params=_params(("parallel",)),
    )(a, w)


def matmul_tn(a, b, name):
    s, m = a.shape
    n = b.shape[1]
    tk = _blk(s, 2048)
    tn = _blk(n, 512)

    def body(a_ref, b_ref, o_ref):
        @pl.when(pl.program_id(1) == 0)
        def _():
            o_ref[...] = jnp.zeros_like(o_ref)

        o_ref[...] += _mm_tn(a_ref[...], b_ref[...])

    return pl.pallas_call(
        body, name=name, out_shape=jax.ShapeDtypeStruct((m, n), F32), grid=(n // tn, s // tk),
        in_specs=[pl.BlockSpec((tk, m), lambda j, i: (i, 0)), pl.BlockSpec((tk, tn), lambda j, i: (i, j))],
        out_specs=pl.BlockSpec((m, tn), lambda j, i: (0, j)),
        compiler_params=_params(("parallel", "arbitrary")),
    )(a, b)


def conv_fwd(xbc, w, b):
    s = xbc.shape[0]
    tm = _blk(s, 256)

    def body(x_ref, t_ref, w_ref, b_ref, pre_ref, act_ref):
        i = pl.program_id(0)
        cur = x_ref[...]
        tail = jnp.where(i > 0, t_ref[...], 0.0)
        wv = w_ref[...]
        acc = cur * wv[3:4, :] + b_ref[...]
        head = cur[0:8, :] * wv[3:4, :] + b_ref[...]
        row8 = _iota((8, CONV_CH), 0)
        for sh in range(1, CONV_WIDTH):
            wk = wv[3 - sh:4 - sh, :]
            acc = acc + pltpu.roll(cur, sh, 0) * wk
            first = jnp.where(row8 < sh, pltpu.roll(tail, sh, 0), pltpu.roll(cur[0:8, :], sh, 0))
            head = head + first * wk
        pre_ref[...] = acc
        act_ref[...] = acc * _sigmoid(acc)
        pre_ref[0:8, :] = head
        act_ref[0:8, :] = head * _sigmoid(head)

    shp = jax.ShapeDtypeStruct(xbc.shape, F32)
    rows = pl.BlockSpec((tm, CONV_CH), lambda i: (i, 0))
    return pl.pallas_call(
        body, name="conv_fwd", out_shape=(shp, shp), grid=(s // tm,),
        in_specs=[rows, pl.BlockSpec((8, CONV_CH), lambda i: (jnp.maximum(i * (tm // 8) - 1, 0), 0)),
                  _const_spec((CONV_WIDTH, CONV_CH)), _const_spec((1, CONV_CH))],
        out_specs=(rows, rows), compiler_params=_params(("parallel",)),
    )(xbc, xbc, w, b)


def conv_bwd(xbc, pre, dact, w):
    s = xbc.shape[0]
    tm = _blk(s, 256)
    nb = s // tm

    def dsilu(p):
        sg = _sigmoid(p)
        return sg * (1.0 + p * (1.0 - sg))

    def body(x_ref, xt_ref, p_ref, pn_ref, d_ref, dn_ref, w_ref, dx_ref, dw_ref, db_ref):
        i = pl.program_id(0)

        @pl.when(i == 0)
        def _():
            dw_ref[...] = jnp.zeros_like(dw_ref)
            db_ref[...] = jnp.zeros_like(db_ref)

        wv = w_ref[...]
        dpre = d_ref[...] * dsilu(p_ref[...])
        dnext = jnp.where(i < nb - 1, dn_ref[...] * dsilu(pn_ref[...]), 0.0)
        cur = x_ref[...]
        tail = jnp.where(i > 0, xt_ref[...], 0.0)
        row8 = _iota((8, CONV_CH), 0)
        dx = dpre * wv[3:4, :]
        last = dpre[tm - 8:tm, :] * wv[3:4, :]
        db_ref[...] += jnp.sum(dpre, axis=0, keepdims=True)
        dws = [jnp.sum(dpre * cur, axis=0, keepdims=True)]
        for sh in range(1, CONV_WIDTH):
            wk = wv[3 - sh:4 - sh, :]
            dx = dx + pltpu.roll(dpre, tm - sh, 0) * wk
            nxt = jnp.where(row8 >= 8 - sh, pltpu.roll(dnext, 8 - sh, 0), pltpu.roll(dpre[tm - 8:tm, :], 8 - sh, 0))
            last = last + nxt * wk
            xs = pltpu.roll(cur, sh, 0)
            first = jnp.where(row8 < sh, pltpu.roll(tail, sh, 0), xs[0:8, :])
            dws.append(jnp.sum(dpre * xs, axis=0, keepdims=True)
                       + jnp.sum(dpre[0:8, :] * (first - xs[0:8, :]), axis=0, keepdims=True))
        dx_ref[...] = dx.astype(BF16)
        dx_ref[tm - 8:tm, :] = last.astype(BF16)
        for sh in range(CONV_WIDTH):
            dw_ref[3 - sh:4 - sh, :] += dws[sh]

    rows = pl.BlockSpec((tm, CONV_CH), lambda i: (i, 0))
    prev8 = pl.BlockSpec((8, CONV_CH), lambda i: (jnp.maximum(i * (tm // 8) - 1, 0), 0))
    next8 = pl.BlockSpec((8, CONV_CH), lambda i: (jnp.minimum((i + 1) * (tm // 8), s // 8 - 1), 0))
    return pl.pallas_call(
        body, name="conv_bwd",
        out_shape=(jax.ShapeDtypeStruct(xbc.shape, BF16), jax.ShapeDtypeStruct((8, CONV_CH), F32),
                   jax.ShapeDtypeStruct((1, CONV_CH), F32)),
        grid=(nb,),
        in_specs=[rows, prev8, rows, next8, rows, next8, _const_spec((CONV_WIDTH, CONV_CH))],
        out_specs=(rows, _const_spec((8, CONV_CH)), _const_spec((1, CONV_CH))),
        compiler_params=_params(("arbitrary",)),
    )(xbc, xbc, pre, pre, dact, dact, w)


def _pair_lanes(mat, j, lane):
    return jnp.where(lane < HEAD_DIM, mat[:, 2 * j:2 * j + 1], mat[:, 2 * j + 1:2 * j + 2])


def _ssd_chunk_prelude(sm, dtb, a_row, lane, sub):
    raw = sm + dtb
    head_lane = lane < N_HEADS
    dt = jnp.where(head_lane, _softplus(raw), 0.0)
    sig = jnp.where(head_lane, _sigmoid(raw), 0.0)
    tri = (lane <= sub).astype(F32)
    acs = _mm_exact(tri, dt * a_row)
    return dt, sig, acs, acs.T


GROUP_WIDTH = SSD_WIDTH // N_GROUPS
HEADS_PER_GROUP = N_HEADS // N_GROUPS


def _expand_group(mat, g, lane):
    return jnp.concatenate([_pair_lanes(mat, j, lane) for j in range(4 * g, 4 * g + 4)], axis=1)


def _head_sums(q, g):
    row = _iota((GROUP_WIDTH, LANES), 0)
    seg = (_iota((GROUP_WIDTH, LANES), 1) == HEADS_PER_GROUP * g + (row >> 6)).astype(BF16)
    hi = q.astype(BF16)
    lo = (q - hi.astype(F32)).astype(BF16)
    return _mm(hi, seg) + _mm(lo, seg)


def _rows_from_lanes(row512):
    return jnp.broadcast_to(row512, (LANES, GROUP_WIDTH)).T


def ssd_fwd(xc, small, dtb_row, a_row, dskip_lane):
    s = xc.shape[0]
    nc = s // CHUNK

    def body(xc_ref, sm_ref, dtb_ref, a_ref, dsk_ref, y_ref, hs_ref, h_scr):
        c = pl.program_id(0)

        @pl.when(c == 0)
        def _():
            h_scr[...] = jnp.zeros_like(h_scr)

        lane = _iota((CHUNK, LANES), 1)
        sub = _iota((CHUNK, LANES), 0)
        causal = lane <= sub
        dt, _, acs, acs_t = _ssd_chunk_prelude(sm_ref[...], dtb_ref[...], a_ref[...], lane, sub)
        for g in range(N_GROUPS):
            cols = slice(GROUP_WIDTH * g, GROUP_WIDTH * (g + 1))
            b_off = SSD_WIDTH + D_STATE * g
            c_off = SSD_WIDTH + N_GROUPS * D_STATE + D_STATE * g
            b_b = xc_ref[:, b_off:b_off + D_STATE].astype(BF16)
            c_b = xc_ref[:, c_off:c_off + D_STATE].astype(BF16)
            cb = _mm_nt(c_b, b_b)
            x_g = xc_ref[:, cols]
            acs_g = _expand_group(acs, g, lane)
            xdt_g = x_g * _expand_group(dt, g, lane)
            xdt_b = xdt_g.astype(BF16)
            heads = range(HEADS_PER_GROUP * g, HEADS_PER_GROUP * (g + 1))
            m_b = [(cb * jnp.exp(jnp.where(causal, acs[:, h:h + 1] - acs_t[h:h + 1, :], NEG_BIG))).astype(BF16)
                   for h in heads]
            yd = [_mm(m_b[k], xdt_b[:, LANES * (k // 2):LANES * (k // 2 + 1)]) for k in range(HEADS_PER_GROUP)]
            yd_g = jnp.concatenate([jnp.where(lane < HEAD_DIM, yd[2 * k], yd[2 * k + 1]) for k in range(4)], axis=1)
            h_g = h_scr[g]
            t_g = _mm_nt(c_b, h_g.astype(BF16))
            y_ref[:, cols] = yd_g + jnp.exp(acs_g) * t_g + dsk_ref[:, cols] * x_g
            hs_ref[0, g] = h_g
            last_g = acs_g[CHUNK - 1:CHUNK, :]
            w_b = (xdt_g * jnp.exp(last_g - acs_g)).astype(BF16)
            h_scr[g] = h_g * jnp.exp(_rows_from_lanes(last_g)) + _mm_tn(w_b, b_b)

    return pl.pallas_call(
        body, name="ssd_fwd",
        out_shape=(jax.ShapeDtypeStruct((s, SSD_WIDTH), F32),
                   jax.ShapeDtypeStruct((nc, N_GROUPS, GROUP_WIDTH, D_STATE), F32)),
        grid=(nc,),
        in_specs=[pl.BlockSpec((CHUNK, CONV_CH), lambda c: (c, 0)), pl.BlockSpec((CHUNK, LANES), lambda c: (c, 0)),
                  _const_spec((1, LANES)), _const_spec((1, LANES)), _const_spec((1, SSD_WIDTH))],
        out_specs=(pl.BlockSpec((CHUNK, SSD_WIDTH), lambda c: (c, 0)),
                   pl.BlockSpec((1, N_GROUPS, GROUP_WIDTH, D_STATE), lambda c: (c, 0, 0, 0))),
        scratch_shapes=[pltpu.VMEM((N_GROUPS, GROUP_WIDTH, D_STATE), F32)],
        compiler_params=_params(("arbitrary",)),
    )(xc, small, dtb_row, a_row, dskip_lane)


def ssd_bwd(xc, small, states, dy, dtb_row, a_row, dskip_lane):
    s = xc.shape[0]
    nc = s // CHUNK
    rev = lambda c: nc - 1 - c

    def body(xc_ref, sm_ref, hs_ref, dy_ref, dtb_ref, a_ref, dsk_ref,
             dxc_ref, ddt_ref, da_ref, ddtb_ref, ddsk_ref, dh_scr):
        c = pl.program_id(0)

        @pl.when(c == 0)
        def _():
            dh_scr[...] = jnp.zeros_like(dh_scr)
            da_ref[...] = jnp.zeros_like(da_ref)
            ddtb_ref[...] = jnp.zeros_like(ddtb_ref)
            ddsk_ref[...] = jnp.zeros_like(ddsk_ref)

        lane = _iota((CHUNK, LANES), 1)
        sub = _iota((CHUNK, LANES), 0)
        causal = lane <= sub
        upper = lane >= sub
        is_last = sub == CHUNK - 1
        a_row_v = a_ref[...]
        dt, sig, acs, acs_t = _ssd_chunk_prelude(sm_ref[...], dtb_ref[...], a_row_v, lane, sub)
        cd = jnp.exp(acs[CHUNK - 1:CHUNK, :])
        dacs_c = jnp.zeros((CHUNK, LANES), F32)
        dacs_r = jnp.zeros((LANES, CHUNK), F32)
        ddtx = jnp.zeros((CHUNK, LANES), F32)
        for g in range(N_GROUPS):
            cols = slice(GROUP_WIDTH * g, GROUP_WIDTH * (g + 1))
            b_off = SSD_WIDTH + D_STATE * g
            c_off = SSD_WIDTH + N_GROUPS * D_STATE + D_STATE * g
            b_b = xc_ref[:, b_off:b_off + D_STATE].astype(BF16)
            c_b = xc_ref[:, c_off:c_off + D_STATE].astype(BF16)
            cb = _mm_nt(c_b, b_b)
            cb_t = _mm_nt(b_b, c_b)
            x_g = xc_ref[:, cols]
            dy_g = dy_ref[:, cols]
            dt_g = _expand_group(dt, g, lane)
            acs_g = _expand_group(acs, g, lane)
            last_g = acs_g[CHUNK - 1:CHUNK, :]
            e_g = jnp.exp(acs_g)
            dte_g = jnp.exp(last_g - acs_g)
            xdt_g = x_g * dt_g
            xdt_b = xdt_g.astype(BF16)
            h_g = hs_ref[0, g]
            dh_g = dh_scr[g]
            h_b = h_g.astype(BF16)
            dh_b = dh_g.astype(BF16)
            heads = list(range(HEADS_PER_GROUP * g, HEADS_PER_GROUP * (g + 1)))
            segs = [acs[:, h:h + 1] - acs_t[h:h + 1, :] for h in heads]
            lms = [jnp.exp(jnp.where(causal, sg, NEG_BIG)) for sg in segs]
            mts = [(cb_t * jnp.exp(jnp.where(upper, -sg, NEG_BIG))).astype(BF16) for sg in segs]
            dyh = []
            for k in range(HEADS_PER_GROUP):
                blk = dy_g[:, LANES * (k // 2):LANES * (k // 2 + 1)]
                in_head = (lane < HEAD_DIM) if k % 2 == 0 else (lane >= HEAD_DIM)
                dyh.append(jnp.where(in_head, blk, 0.0).astype(BF16))
            dms = [_mm_nt(dyh[k], xdt_b[:, LANES * (k // 2):LANES * (k // 2 + 1)]) for k in range(HEADS_PER_GROUP)]
            dxs = [_mm(mts[k], dyh[k]) for k in range(HEADS_PER_GROUP)]
            dcb = jnp.zeros((CHUNK, CHUNK), F32)
            for k, h in enumerate(heads):
                gmat = dms[k] * (cb * lms[k])
                dacs_c = dacs_c + jnp.where(lane == h, jnp.sum(gmat, axis=1, keepdims=True), 0.0)
                dacs_r = dacs_r - jnp.where(sub == h, jnp.sum(gmat, axis=0, keepdims=True), 0.0)
                dcb = dcb + dms[k] * lms[k]
            dxdt_g = jnp.concatenate([dxs[2 * k] + dxs[2 * k + 1] for k in range(4)], axis=1)
            t_g = _mm_nt(c_b, h_b)
            dacs_c = dacs_c + _head_sums(dy_g * e_g * t_g, g)
            dt_b = (dy_g * e_g).astype(BF16)
            dc_acc = _mm(dt_b, h_b)
            dh_prev = _mm_tn(dt_b, c_b)
            dw_g = _mm_nt(b_b, dh_b)
            w_g = xdt_g * dte_g
            dxdt_g = dxdt_g + dw_g * dte_g
            db_acc = _mm(w_g.astype(BF16), dh_b)
            r2 = _head_sums(dw_g * w_g, g)
            dacs_c = dacs_c + jnp.where(is_last, jnp.sum(r2, axis=0, keepdims=True), 0.0) - r2
            q3 = jnp.sum(dh_g * h_g, axis=1, keepdims=True)
            for k, h in enumerate(heads):
                tot = jnp.sum(q3[HEAD_DIM * k:HEAD_DIM * (k + 1), :], keepdims=True) * cd[:, h:h + 1]
                dacs_c = dacs_c + jnp.where(is_last & (lane == h), tot, 0.0)
            dh_scr[g] = dh_prev + dh_g * jnp.exp(_rows_from_lanes(last_g))
            dxc_ref[:, cols] = dxdt_g * dt_g + dsk_ref[:, cols] * dy_g
            ddtx = ddtx + _head_sums(dxdt_g * x_g, g)
            ddsk_ref[:, cols] += jnp.sum(dy_g * x_g, axis=0, keepdims=True)
            dxc_ref[:, b_off:b_off + D_STATE] = db_acc + _mm(dcb.T.astype(BF16), c_b)
            dxc_ref[:, c_off:c_off + D_STATE] = dc_acc + _mm(dcb.astype(BF16), b_b)
        dacs = dacs_c + dacs_r.T
        dadt = _mm_exact((lane >= sub).astype(F32), dacs)
        ddt = dadt * a_row_v + ddtx
        ddt_raw = ddt * sig
        ddt_ref[...] = ddt_raw
        da_ref[...] += jnp.sum(dadt * dt, axis=0, keepdims=True)
        ddtb_ref[...] += jnp.sum(ddt_raw, axis=0, keepdims=True)

    return pl.pallas_call(
        body, name="ssd_bwd",
        out_shape=(jax.ShapeDtypeStruct((s, CONV_CH), F32), jax.ShapeDtypeStruct((s, LANES), F32),
                   jax.ShapeDtypeStruct((1, LANES), F32), jax.ShapeDtypeStruct((1, LANES), F32),
                   jax.ShapeDtypeStruct((1, SSD_WIDTH), F32)),
        grid=(nc,),
        in_specs=[pl.BlockSpec((CHUNK, CONV_CH), lambda c: (rev(c), 0)),
                  pl.BlockSpec((CHUNK, LANES), lambda c: (rev(c), 0)),
                  pl.BlockSpec((1, N_GROUPS, GROUP_WIDTH, D_STATE), lambda c: (rev(c), 0, 0, 0)),
                  pl.BlockSpec((CHUNK, SSD_WIDTH), lambda c: (rev(c), 0)),
                  _const_spec((1, LANES)), _const_spec((1, LANES)), _const_spec((1, SSD_WIDTH))],
        out_specs=(pl.BlockSpec((CHUNK, CONV_CH), lambda c: (rev(c), 0)),
                   pl.BlockSpec((CHUNK, LANES), lambda c: (rev(c), 0)),
                   _const_spec((1, LANES)), _const_spec((1, LANES)), _const_spec((1, SSD_WIDTH))),
        scratch_shapes=[pltpu.VMEM((N_GROUPS, GROUP_WIDTH, D_STATE), F32)],
        compiler_params=_params(("arbitrary",)),
    )(xc, small, states, dy, dtb_row, a_row, dskip_lane)


FORGET_BLOCK = 512


def forget_cumsum(small, fgb_row):
    s = small.shape[0]
    t = _blk(s, FORGET_BLOCK)
    nb = s // t

    def body(sm_ref, b_ref, cc_ref, carry):
        i = pl.program_id(0)

        @pl.when(i == 0)
        def _():
            carry[...] = jnp.zeros_like(carry)

        lane = _iota((t, LANES), 1)
        in_f = (lane >= N_HEADS) & (lane < 2 * N_HEADS)
        logf = jnp.where(in_f, -_softplus(-(sm_ref[...] + b_ref[...])), 0.0)
        tri = (_iota((t, t), 1) <= _iota((t, t), 0)).astype(F32)
        cum = _mm_exact(tri, logf) + carry[0:1, :]
        cc_ref[...] = cum
        carry[...] = jnp.broadcast_to(cum[t - 1:t, :], (8, LANES))

    return pl.pallas_call(
        body, name="forget_cumsum",
        out_shape=jax.ShapeDtypeStruct((s, LANES), F32),
        grid=(nb,),
        in_specs=[pl.BlockSpec((t, LANES), lambda i: (i, 0)), _const_spec((1, LANES))],
        out_specs=pl.BlockSpec((t, LANES), lambda i: (i, 0)),
        scratch_shapes=[pltpu.VMEM((8, LANES), F32)],
        compiler_params=_params(("arbitrary",)),
    )(small, fgb_row)


def forget_bwd(dc, small, ddt_raw, fgb_row):
    s = small.shape[0]
    t = _blk(s, FORGET_BLOCK)
    nb = s // t
    rev = lambda i: nb - 1 - i

    def body(dc_ref, sm_ref, ddt_ref, b_ref, ds_ref, dfb_ref, carry):
        i = pl.program_id(0)

        @pl.when(i == 0)
        def _():
            carry[...] = jnp.zeros_like(carry)
            dfb_ref[...] = jnp.zeros_like(dfb_ref)

        lane = _iota((t, LANES), 1)
        rows = dc_ref[...].T
        tri = (_iota((t, t), 1) <= _iota((t, t), 0)).astype(F32)
        rc = _mm_exact(rows, tri) + carry[:, 0:1]
        carry[...] = jnp.broadcast_to(rc[:, 0:1], (LANES, LANES))
        in_f = (lane >= N_HEADS) & (lane < 2 * N_HEADS)
        df = jnp.where(in_f, rc.T * _sigmoid(-(sm_ref[...] + b_ref[...])), 0.0)
        ds_ref[...] = (df + ddt_ref[...]).astype(BF16)
        dfb_ref[...] += jnp.sum(df, axis=0, keepdims=True)

    blk = pl.BlockSpec((t, LANES), lambda i: (rev(i), 0))
    return pl.pallas_call(
        body, name="forget_bwd",
        out_shape=(jax.ShapeDtypeStruct((s, LANES), BF16), jax.ShapeDtypeStruct((1, LANES), F32)),
        grid=(nb,),
        in_specs=[blk, blk, blk, _const_spec((1, LANES))],
        out_specs=(blk, _const_spec((1, LANES))),
        scratch_shapes=[pltpu.VMEM((LANES, LANES), F32)],
        compiler_params=_params(("arbitrary",)),
    )(dc, small, ddt_raw, fgb_row)


ATT_BLOCK = 512
ATT_SCALE = HEAD_DIM ** -0.5
AUG_A = HEAD_DIM
AUG_B = HEAD_DIM + 3


def _split3(c):
    hi = c.astype(BF16).astype(F32)
    r = c - hi
    mid = r.astype(BF16).astype(F32)
    return hi, mid, (r - mid).astype(BF16).astype(F32)


def _aug(lane, first, parts=None, value=1.0):
    if parts is None:
        return jnp.where((lane >= first) & (lane < first + 3), value, 0.0)
    return (jnp.where(lane == first, parts[0], 0.0) + jnp.where(lane == first + 1, parts[1], 0.0)
            + jnp.where(lane == first + 2, parts[2], 0.0))


def _pack_pair(a0, a1, lane):
    return jnp.where(lane < HEAD_DIM, a0, pltpu.roll(a1, HEAD_DIM, 1))


def proj_qkv_heads(u, w_q, w_k, w_v, cum):
    s = u.shape[0]
    tm = _blk(s, 256)

    def body(u_ref, wq_ref, wk_ref, wv_ref, c_ref, qa_ref, ka_ref, va_ref, nrm_ref):
        lane = _iota((tm, LANES), 1)
        lo = lane < HEAD_DIM
        uv = u_ref[...]
        qf = _mm(uv, wq_ref[...]) * ATT_SCALE
        kf = _mm(uv, wk_ref[...])
        vf = _mm(uv, wv_ref[...])
        cc = c_ref[...]
        ones_a = _aug(lane, AUG_A)
        ones_b = _aug(lane, AUG_B)
        sub8 = _iota((8, LANES), 0)
        nrm = jnp.zeros((8, LANES), F32)
        for h in range(N_HEADS):
            j, e = divmod(h, 2)

            def head(full):
                blk = full[:, LANES * j:LANES * (j + 1)]
                if e == 1:
                    blk = pltpu.roll(blk, HEAD_DIM, 1)
                return jnp.where(lo, blk, 0.0)

            parts = _split3(cc[:, N_HEADS + h:N_HEADS + h + 1])
            qh, kh = head(qf), head(kf)
            qa_ref[h] = (qh + _aug(lane, AUG_A, parts) + ones_b).astype(BF16)
            ka_ref[h] = (kh + ones_a - _aug(lane, AUG_B, parts)).astype(BF16)
            va_ref[h] = (head(vf) + ones_a).astype(BF16)
        seg = (_iota((ATT_WIDTH, LANES), 1) == (_iota((ATT_WIDTH, LANES), 0) >> 6)).astype(BF16)
        for r, val in enumerate((qf, kf)):
            sq = val * val
            hi = sq.astype(BF16)
            tot = _mm(hi, seg) + _mm((sq - hi.astype(F32)).astype(BF16), seg)
            nrm = nrm + jnp.where(sub8 == r, jnp.max(tot, axis=0, keepdims=True), 0.0)
        nrm_ref[0] = nrm

    shp = jax.ShapeDtypeStruct((N_HEADS, s, LANES), BF16)
    hspec = pl.BlockSpec((N_HEADS, tm, LANES), lambda i: (0, i, 0))
    wspec = _const_spec((D_MODEL, ATT_WIDTH))
    return pl.pallas_call(
        body, name="proj_qkv_heads",
        out_shape=(shp, shp, shp, jax.ShapeDtypeStruct((s // tm, 8, LANES), F32)), grid=(s // tm,),
        in_specs=[pl.BlockSpec((tm, D_MODEL), lambda i: (i, 0)), wspec, wspec, wspec,
                  pl.BlockSpec((tm, LANES), lambda i: (i, 0))],
        out_specs=(hspec, hspec, hspec, pl.BlockSpec((1, 8, LANES), lambda i: (i, 0, 0))),
        compiler_params=_params(("parallel",)),
    )(u, w_q, w_k, w_v, cum)


SKIP_BELOW = -110.0


def live_blocks(norms, cum, t):
    qn = jnp.sqrt(jnp.max(norms[:, 0, :N_HEADS], axis=0))
    kn = jnp.sqrt(jnp.max(norms[:, 1, :N_HEADS], axis=0))
    bound = 2.05 * qn * kn + 2.0
    c_first = cum[0::t, N_HEADS:2 * N_HEADS]
    c_last = cum[t - 1::t, N_HEADS:2 * N_HEADS]
    nq = c_first.shape[0]
    top = bound[None, None, :] + c_first[:, None, :] - c_last[None, :, :]
    below = jnp.arange(nq)[None, :] < jnp.arange(nq)[:, None]
    dead = below[:, :, None] & ~(top >= SKIP_BELOW)
    first = jnp.sum(dead, axis=1).astype(jnp.int32).T
    last_q = jnp.sum(first[:, None, :] <= jnp.arange(nq)[None, :, None], axis=2).astype(jnp.int32) - 1
    return first, last_q


def attention_fwd(first, qa, ka, va):
    s = qa.shape[1]
    t = _blk(s, ATT_BLOCK)
    nq = s // t

    def body(first_ref, qa_ref, ka_ref, va_ref, o_ref, qb_ref, m_scr, acc_scr, alpha_scr, p_scr, s_scr):
        qi = pl.program_id(1)
        starts = [first_ref[2 * pl.program_id(0) + e, qi] for e in range(2)]
        k0 = jnp.maximum(starts[0], starts[1])
        m_scr[...] = jnp.full_like(m_scr, NEG_BIG)
        acc_scr[...] = jnp.zeros_like(acc_scr)

        def kv_rows(kb):
            return pl.ds(pl.multiple_of(kb * t, t), t)

        def logits(kb, masked, heads=(0, 1)):
            for e in heads:
                sc = _mm_nt(qa_ref[e], ka_ref[e, kv_rows(kb), :])
                if masked:
                    sc = jnp.where(_iota((t, t), 0) >= _iota((t, t), 1), sc, NEG_BIG)
                s_scr[e] = sc

        def probs(heads=(0, 1)):
            for e in heads:
                cmax = s_scr[e, :, 0:LANES]
                for c in range(1, t // LANES):
                    cmax = jnp.maximum(cmax, s_scr[e, :, LANES * c:LANES * (c + 1)])
                m_old = m_scr[e]
                m_new = jnp.maximum(m_old, jnp.max(cmax, axis=1, keepdims=True))
                alpha_scr[e] = jnp.exp(m_old - m_new)
                m_scr[e] = m_new
                for c in range(t // LANES):
                    cols = slice(LANES * c, LANES * (c + 1))
                    p_scr[e, :, cols] = jnp.exp(s_scr[e, :, cols] - m_new).astype(BF16)

        def accumulate(kb, heads=(0, 1)):
            for e in heads:
                acc_scr[e] = alpha_scr[e] * acc_scr[e] + _mm(p_scr[e], va_ref[e, kv_rows(kb), :])

        for e in range(2):
            def alone(kb, carry, e=e):
                logits(kb, False, (e,))
                probs((e,))
                accumulate(kb, (e,))
                return carry

            lax.fori_loop(starts[e], k0, alone, 0)

        def loop_body(kb, carry):
            logits(kb, False)
            for e in range(2):
                accumulate(kb - 1, (e,))
                probs((e,))
            return carry

        @pl.when(qi > k0)
        def _():
            logits(k0, False)
            probs()

        lax.fori_loop(k0 + 1, qi, loop_body, 0)

        @pl.when(qi > k0)
        def _():
            logits(qi, True)
            accumulate(qi - 1)
            probs()

        @pl.when(qi == k0)
        def _():
            logits(qi, True)
            probs()

        accumulate(qi)

        lane = _iota((t, LANES), 1)
        outs = []
        for e in range(2):
            acc = acc_scr[e]
            l = acc[:, AUG_A:AUG_A + 1]
            outs.append(acc / l)
            lse = m_scr[e][:, 0:1] + jnp.log(l)
            q32 = qa_ref[e].astype(F32)
            c = q32[:, AUG_A:AUG_A + 1] + q32[:, AUG_A + 1:AUG_A + 2] + q32[:, AUG_A + 2:AUG_A + 3]
            qb = jnp.where(lane < HEAD_DIM, q32, 0.0) + _aug(lane, AUG_A, _split3(c - lse)) + _aug(lane, AUG_B)
            qb_ref[e] = qb.astype(BF16)
        o_ref[...] = _pack_pair(outs[0], outs[1], lane)

    grid_spec = pltpu.PrefetchScalarGridSpec(
        num_scalar_prefetch=1, grid=(N_PAIRS, nq),
        in_specs=[pl.BlockSpec((2, t, LANES), lambda j, qi, f: (j, qi, 0)),
                  pl.BlockSpec((2, s, LANES), lambda j, qi, f: (j, 0, 0)),
                  pl.BlockSpec((2, s, LANES), lambda j, qi, f: (j, 0, 0))],
        out_specs=[pl.BlockSpec((t, LANES), lambda j, qi, f: (qi, j)),
                   pl.BlockSpec((2, t, LANES), lambda j, qi, f: (j, qi, 0))],
        scratch_shapes=[pltpu.VMEM((2, t, LANES), F32), pltpu.VMEM((2, t, LANES), F32),
                        pltpu.VMEM((2, t, LANES), F32), pltpu.VMEM((2, t, t), BF16), pltpu.VMEM((2, t, t), F32)])
    return pl.pallas_call(
        body, name="attention_fwd", grid_spec=grid_spec,
        out_shape=(jax.ShapeDtypeStruct((s, ATT_WIDTH), F32), jax.ShapeDtypeStruct((N_HEADS, s, LANES), BF16)),
        compiler_params=_params(("parallel", "parallel")),
    )(first, qa, ka, va)


def attention_bwd(last_q, qb, ka, va, dob):
    s = qb.shape[1]
    t = _blk(s, ATT_BLOCK)
    nq = s // t

    def body(last_ref, qb_ref, dob_ref, ka_ref, va_ref, dq_ref, dk_ref, dv_ref, dc_ref, dq_scr, dk_scr, dv_scr):
        j, ki = pl.program_id(0), pl.program_id(1)

        @pl.when((j == 0) & (ki == 0))
        def _():
            dc_ref[...] = jnp.zeros_like(dc_ref)

        @pl.when(ki == 0)
        def _():
            dq_scr[...] = jnp.zeros_like(dq_scr)

        dk_scr[...] = jnp.zeros_like(dk_scr)
        dv_scr[...] = jnp.zeros_like(dv_scr)

        def q_step(qblk, masked, heads=(0, 1)):
            rows = pl.ds(pl.multiple_of(qblk * t, t), t)
            scs = [_mm_nt(qb_ref[e, rows, :], ka_ref[e]) for e in heads]
            dps = [_mm_nt(dob_ref[e, rows, :], va_ref[e]) for e in heads]
            for e, sc, dp in zip(heads, scs, dps):
                q = qb_ref[e, rows, :]
                do = dob_ref[e, rows, :]
                if masked:
                    sc = jnp.where(_iota((t, t), 0) >= _iota((t, t), 1), sc, NEG_BIG)
                p = jnp.exp(sc)
                ds_b = (p * dp).astype(BF16)
                dv_scr[e] += _mm_tn(p.astype(BF16), do)
                dk_scr[e] += _mm_tn(ds_b, q)
                dq_scr[e, rows, :] += _mm(ds_b, ka_ref[e])

        def loop_body(qblk, carry):
            q_step(qblk, False)
            return carry

        ends = [last_ref[2 * j + e, ki] + 1 for e in range(2)]
        both = jnp.minimum(ends[0], ends[1])
        q_step(ki, True)
        lax.fori_loop(ki + 1, both, loop_body, 0)
        for e in range(2):
            def alone(qblk, carry, e=e):
                q_step(qblk, False, (e,))
                return carry

            lax.fori_loop(both, ends[e], alone, 0)

        lane = _iota((t, LANES), 1)
        dk_ref[...] = _pack_pair(dk_scr[0], dk_scr[1], lane).astype(BF16)
        dv_ref[...] = _pack_pair(dv_scr[0], dv_scr[1], lane).astype(BF16)
        rows = pl.ds(pl.multiple_of(ki * t, t), t)
        dc_ref[rows, :] -= (jnp.where(lane == N_HEADS + 2 * j, dk_scr[0][:, AUG_B:AUG_B + 1], 0.0)
                            + jnp.where(lane == N_HEADS + 2 * j + 1, dk_scr[1][:, AUG_B:AUG_B + 1], 0.0))

        @pl.when(ki == nq - 1)
        def _():
            for blk in range(nq):
                rws = pl.ds(blk * t, t)
                d0 = dq_scr[0, rws, :]
                d1 = dq_scr[1, rws, :]
                dq_ref[rws, :] = (_pack_pair(d0, d1, lane) * ATT_SCALE).astype(BF16)
                dc_ref[rws, :] += (jnp.where(lane == N_HEADS + 2 * j, d0[:, AUG_A:AUG_A + 1], 0.0)
                                   + jnp.where(lane == N_HEADS + 2 * j + 1, d1[:, AUG_A:AUG_A + 1], 0.0))

    full = pl.BlockSpec((2, s, LANES), lambda j, ki, f: (j, 0, 0))
    blk = pl.BlockSpec((2, t, LANES), lambda j, ki, f: (j, ki, 0))
    pair = pl.BlockSpec((t, LANES), lambda j, ki, f: (ki, j))
    wide = jax.ShapeDtypeStruct((s, ATT_WIDTH), BF16)
    grid_spec = pltpu.PrefetchScalarGridSpec(
        num_scalar_prefetch=1, grid=(N_PAIRS, nq),
        in_specs=[full, full, blk, blk],
        out_specs=[pl.BlockSpec((s, LANES), lambda j, ki, f: (0, j)), pair, pair,
                   pl.BlockSpec((s, LANES), lambda j, ki, f: (0, 0))],
        scratch_shapes=[pltpu.VMEM((2, s, LANES), F32), pltpu.VMEM((2, t, LANES), F32),
                        pltpu.VMEM((2, t, LANES), F32)])
    return pl.pallas_call(
        body, name="attention_bwd", grid_spec=grid_spec,
        out_shape=(wide, wide, wide, jax.ShapeDtypeStruct((s, LANES), F32)),
        compiler_params=_params(("arbitrary", "arbitrary")),
    )(last_q, qb, dob, ka, va)


def _dsilu(z, sg):
    return sg * (1.0 + z * (1.0 - sg))


def post_mix(x, y, zs, o, za, p, tgt, ssd_g, att_g_lane, ple_g, fin_g, w_out, w_gate, w_proj):
    s = x.shape[0]
    tm = _blk(s, 256)
    half = SSD_WIDTH // N_GROUPS

    def rms_bwd(dy, yn, r):
        return r * (dy - yn * jnp.mean(dy * yn, axis=-1, keepdims=True))

    def colsum(a):
        return jnp.sum(a, axis=0, keepdims=True)

    def body(x_ref, y_ref, zs_ref, o_ref, za_ref, p_ref, t_ref, sg_ref, ag_ref, pg_ref, fg_ref,
             wo_ref, wg_ref, wp_ref,
             dh1_ref, dy_ref, dzs_ref, dob_ref, dza_ref, ycat_ref, dh1b_ref, n2b_ref, dglb_ref, dppb_ref, pb_ref,
             loss_ref, dfin_ref, dple_ref, dssd_ref, datt_ref):
        @pl.when(pl.program_id(0) == 0)
        def _():
            for r in (loss_ref, dfin_ref, dple_ref, dssd_ref, datt_ref):
                r[...] = jnp.zeros_like(r)

        lane = _iota((tm, LANES), 1)
        lo = lane < HEAD_DIM
        zs = zs_ref[...]
        sz = _sigmoid(zs)
        yv = y_ref[...]
        ys = yv * (zs * sz)
        yn, rg = [], []
        for g in range(N_GROUPS):
            seg = ys[:, half * g:half * (g + 1)]
            r = lax.rsqrt(jnp.mean(seg * seg, axis=-1, keepdims=True) + EPS)
            yn.append(seg * r)
            rg.append(r)
            ycat_ref[:, half * g:half * (g + 1)] = (yn[g] * sg_ref[:, half * g:half * (g + 1)]).astype(BF16)
        za = za_ref[...]
        sza = _sigmoid(za)
        silu_za = za * sza
        on, ra = [], []
        for jb in range(N_PAIRS):
            blk = o_ref[:, LANES * jb:LANES * (jb + 1)]
            sq = blk * blk
            ms0 = jnp.sum(jnp.where(lo, sq, 0.0), axis=1, keepdims=True) * (1.0 / HEAD_DIM)
            ms1 = jnp.sum(jnp.where(lo, 0.0, sq), axis=1, keepdims=True) * (1.0 / HEAD_DIM)
            r = jnp.where(lo, lax.rsqrt(ms0 + EPS), lax.rsqrt(ms1 + EPS))
            on.append(blk * r)
            ra.append(r)
            an = on[jb] * ag_ref[:, LANES * jb:LANES * (jb + 1)]
            ycat_ref[:, SSD_WIDTH + LANES * jb:SSD_WIDTH + LANES * (jb + 1)] = (
                an * silu_za[:, LANES * jb:LANES * (jb + 1)]).astype(BF16)
        h1 = x_ref[...] + _mm(ycat_ref[...], wo_ref[...])
        r2 = lax.rsqrt(jnp.mean(h1 * h1, axis=-1, keepdims=True) + EPS)
        n2h = h1 * r2
        n2_b = (n2h * pg_ref[...]).astype(BF16)
        gate = _sigmoid(_mm(n2_b, wg_ref[...]))
        p_b = p_ref[...].astype(BF16)
        pp = _mm(p_b, wp_ref[...])
        h2 = h1 + gate * pp
        r3 = lax.rsqrt(jnp.mean(h2 * h2, axis=-1, keepdims=True) + EPS)
        n3 = h2 * r3
        diff = n3 * fg_ref[...] - t_ref[...]
        sq = colsum(diff * diff)
        part = sq[:, 0:LANES]
        for jb in range(1, D_MODEL // LANES):
            part = part + sq[:, LANES * jb:LANES * (jb + 1)]
        loss_ref[...] += part * (0.5 / D_MODEL)
        dout = diff * (1.0 / D_MODEL)
        dfin_ref[...] += colsum(dout * n3)
        dh2 = rms_bwd(dout * fg_ref[...], n3, r3)
        dgl = dh2 * pp * gate * (1.0 - gate)
        dgl_b = dgl.astype(BF16)
        dn2 = _mm_nt(dgl_b, wg_ref[...])
        dple_ref[...] += colsum(dn2 * n2h)
        dh1 = dh2 + rms_bwd(dn2 * pg_ref[...], n2h, r2)
        dh1_b = dh1.astype(BF16)
        dycat = _mm_nt(dh1_b, wo_ref[...])
        dh1_ref[...] = dh1
        dh1b_ref[...] = dh1_b
        n2b_ref[...] = n2_b
        dglb_ref[...] = dgl_b
        dppb_ref[...] = (dh2 * gate).astype(BF16)
        pb_ref[...] = p_b
        for g in range(N_GROUPS):
            cols = slice(half * g, half * (g + 1))
            dys_g = dycat[:, cols]
            dssd_ref[:, cols] += colsum(dys_g * yn[g])
            dys = rms_bwd(dys_g * sg_ref[:, cols], yn[g], rg[g])
            dy_ref[:, cols] = dys * (zs[:, cols] * sz[:, cols])
            dzs_ref[:, cols] = (dys * yv[:, cols] * _dsilu(zs[:, cols], sz[:, cols])).astype(BF16)
        for jb in range(N_PAIRS):
            cols = slice(LANES * jb, LANES * (jb + 1))
            dya = dycat[:, SSD_WIDTH + LANES * jb:SSD_WIDTH + LANES * (jb + 1)]
            ag = ag_ref[:, cols]
            dan = dya * silu_za[:, cols]
            dza_ref[:, cols] = (dya * (on[jb] * ag) * _dsilu(za[:, cols], sza[:, cols])).astype(BF16)
            datt_ref[:, cols] += colsum(dan * on[jb])
            don = dan * ag
            q = don * on[jb]
            m0 = jnp.sum(jnp.where(lo, q, 0.0), axis=1, keepdims=True) * (1.0 / HEAD_DIM)
            m1 = jnp.sum(jnp.where(lo, 0.0, q), axis=1, keepdims=True) * (1.0 / HEAD_DIM)
            do2 = ra[jb] * (don - on[jb] * jnp.where(lo, m0, m1))
            prod = do2 * o_ref[:, cols]
            for e in range(2):
                delta = jnp.sum(jnp.where(lo, prod, 0.0) if e == 0 else jnp.where(lo, 0.0, prod),
                                axis=1, keepdims=True)
                base = jnp.where(lo, do2 if e == 0 else pltpu.roll(do2, HEAD_DIM, 1), 0.0)
                dob_ref[2 * jb + e] = (base - _aug(lane, AUG_A, _split3(delta))).astype(BF16)

    def rows(n, dtype=None):
        return pl.BlockSpec((tm, n), lambda i: (i, 0))

    def out(n, dtype):
        return jax.ShapeDtypeStruct((s, n), dtype)

    vec = _const_spec((1, D_MODEL))
    vshape = jax.ShapeDtypeStruct((1, D_MODEL), F32)
    return pl.pallas_call(
        body, name="post_mix",
        out_shape=(out(D_MODEL, F32), out(SSD_WIDTH, F32), out(SSD_WIDTH, BF16),
                   jax.ShapeDtypeStruct((N_HEADS, s, LANES), BF16),
                   out(ATT_WIDTH, BF16), out(D_INNER, BF16), out(D_MODEL, BF16), out(D_MODEL, BF16),
                   out(D_MODEL, BF16), out(D_MODEL, BF16), out(PLE_DIM, BF16),
                   jax.ShapeDtypeStruct((1, LANES), F32), vshape, vshape, vshape, vshape),
        grid=(s // tm,),
        in_specs=[rows(D_MODEL), rows(SSD_WIDTH), rows(SSD_WIDTH), rows(ATT_WIDTH), rows(ATT_WIDTH),
                  rows(PLE_DIM), rows(D_MODEL), vec, vec, vec, vec,
                  _const_spec((D_INNER, D_MODEL)), _const_spec((D_MODEL, D_MODEL)), _const_spec((PLE_DIM, D_MODEL))],
        out_specs=(rows(D_MODEL), rows(SSD_WIDTH), rows(SSD_WIDTH),
                   pl.BlockSpec((N_HEADS, tm, LANES), lambda i: (0, i, 0)), rows(ATT_WIDTH),
                   rows(D_INNER), rows(D_MODEL), rows(D_MODEL), rows(D_MODEL), rows(D_MODEL), rows(PLE_DIM),
                   _const_spec((1, LANES)), vec, vec, vec, vec),
        compiler_params=_params(("arbitrary",)),
    )(x, y, zs, o, za, p, tgt, ssd_g, att_g_lane, ple_g, fin_g, w_out, w_gate, w_proj)


def in_proj_bwd(dsegs, wsegs, x, g, dh1, pres):
    s = x.shape[0]
    tm = _blk(s, 256)
    nseg = len(dsegs)
    nbig = len(pres)
    nsteps = s // tm

    def body(*refs):
        d_refs = refs[:nseg]
        w_refs = refs[nseg:2 * nseg]
        x_ref, g_ref, dh1_ref = refs[2 * nseg:2 * nseg + 3]
        rest = refs[2 * nseg + 3:]
        pre_refs, (dx_ref, dg_ref), part_refs = rest[:nbig], rest[nbig:nbig + 2], rest[nbig + 2:2 * nbig + 2]
        ssem, rsem, lsem = rest[2 * nbig + 2:]

        @pl.when(pl.program_id(0) == 0)
        def _():
            dg_ref[...] = jnp.zeros_like(dg_ref)
            for cp in scatter_copies(pre_refs, part_refs, ssem, rsem, lsem):
                cp.start()

        @pl.when(pl.program_id(0) == nsteps - 1)
        def _():
            for cp in scatter_copies(pre_refs, part_refs, ssem, rsem, lsem):
                cp.wait()

        du = _mm_nt(d_refs[0][...], w_refs[0][...])
        for k in range(1, nseg):
            du = du + _mm_nt(d_refs[k][...], w_refs[k][...])
        xv = x_ref[...]
        r = lax.rsqrt(jnp.mean(xv * xv, axis=-1, keepdims=True) + EPS)
        xh = xv * r
        dg_ref[...] += jnp.sum(du * xh, axis=0, keepdims=True)
        dxh = du * g_ref[...]
        dx_ref[...] = r * (dxh - xh * jnp.mean(dxh * xh, axis=-1, keepdims=True)) + dh1_ref[...]

    rows = lambda n: pl.BlockSpec((tm, n), lambda i: (i, 0))
    return pl.pallas_call(
        body, name="in_proj_bwd",
        out_shape=tuple([jax.ShapeDtypeStruct((s, D_MODEL), F32), jax.ShapeDtypeStruct((1, D_MODEL), F32)]
                        + [jax.ShapeDtypeStruct(a.shape, a.dtype) for a in pres]),
        grid=(nsteps,),
        in_specs=([rows(d.shape[1]) for d in dsegs] + [_const_spec(w.shape) for w in wsegs]
                  + [rows(D_MODEL), _const_spec((1, D_MODEL)), rows(D_MODEL)] + [ANY] * nbig),
        out_specs=tuple([rows(D_MODEL), _const_spec((1, D_MODEL))] + [ANY] * nbig),
        scratch_shapes=_sems(3 * nbig) + [pltpu.SemaphoreType.DMA((nbig,))],
        compiler_params=_params(("arbitrary",)),
    )(*dsegs, *wsegs, x, g, dh1, *pres)


SMALL_NAMES = ("norm_g", "conv_b", "dt_bias", "a_log", "d_skip", "ssd_norm_g", "fg_bias", "att_norm_g",
               "ple_norm_g", "final_norm_g")
SMALL_SIZES = (1024, 1536, 16, 16, 16, 1024, 16, 64, 1024, 1024)
CONV_W_SIZE = CONV_WIDTH * CONV_CH


def _pack_small(vals):
    flat = jnp.concatenate([v.reshape(-1).astype(F32) for v in vals])
    flat = jnp.pad(flat, (0, SMALL_ROWS * LANES - flat.shape[0]))
    return flat.reshape(SMALL_ROWS, LANES)


def _unpack_small(pack, shapes):
    flat = pack.reshape(-1)
    out, off = [], 0
    for n, shp in zip(SMALL_SIZES, shapes):
        out.append(flat[off:off + n].reshape(shp))
        off += n
    return out


def _row128(v16, offset=0):
    return jnp.pad(v16.reshape(1, N_HEADS).astype(F32), ((0, 0), (offset, LANES - N_HEADS - offset)))


def local_step(prereduce, x, p, tgt, w_in, w_out, w_gate, w_proj, conv_w, norm_g, conv_b, dt_bias, a_log, d_skip,
               ssd_norm_g, fg_bias, att_norm_g, ple_norm_g, final_norm_g):
    c0, c1, c2, c3, c4, c5, c6, c7 = 0, 1024, 2560, 2576, 3600, 4624, 5648, 6672
    w_zs, w_xbc, w_dt = w_in[:, c0:c1], w_in[:, c1:c2], w_in[:, c2:c3]
    w_za, w_q, w_k, w_v, w_f = w_in[:, c3:c4], w_in[:, c4:c5], w_in[:, c5:c6], w_in[:, c6:c7], w_in[:, c7:]
    w_small = jnp.concatenate([w_dt, w_f, jnp.zeros((D_MODEL, LANES - 2 * N_HEADS), BF16)], axis=1)

    dtb_row = _row128(dt_bias)
    a_row = _row128(-jnp.exp(a_log.astype(F32)))
    fgb_row = _row128(fg_bias, N_HEADS)
    dskip_lane = jnp.repeat(d_skip.astype(F32), HEAD_DIM).reshape(1, SSD_WIDTH)
    att_g_lane = jnp.tile(att_norm_g.astype(F32), N_HEADS).reshape(1, ATT_WIDTH)
    row = lambda v: v.reshape(1, -1).astype(F32)

    u = rms_prenorm(x, row(norm_g))
    zs = matmul_rows(u, w_zs, F32, "proj_z_ssd")
    xbc = matmul_rows(u, w_xbc, F32, "proj_xbc")
    za = matmul_rows(u, w_za, F32, "proj_z_att")
    small = matmul_rows(u, w_small, F32, "proj_small")
    cum = forget_cumsum(small, fgb_row)
    qa, ka, va, norms = proj_qkv_heads(u, w_q, w_k, w_v, cum)
    first, last_q = live_blocks(norms, cum, _blk(x.shape[0], ATT_BLOCK))
    pre, xc = conv_fwd(xbc, conv_w, row(conv_b))
    y, states = ssd_fwd(xc, small, dtb_row, a_row, dskip_lane)
    o, qb = attention_fwd(first, qa, ka, va)
    (dh1, dy, dzs, dob, dza, ycat, dh1_b, n2_b, dgl_b, dpp_b, p_b,
     loss_l, dfin, dple, dssd_g, datt_lane) = post_mix(
        x, y, zs, o, za, p, tgt, row(ssd_norm_g), att_g_lane, row(ple_norm_g), row(final_norm_g),
        w_out, w_gate, w_proj)
    dq, dk, dv, dc = attention_bwd(last_q, qb, ka, va, dob)
    dxc, ddt_raw, da, ddtb, ddsk_lane = ssd_bwd(xc, small, states, dy, dtb_row, a_row, dskip_lane)
    dsmall, dfgb = forget_bwd(dc, small, ddt_raw, fgb_row)
    dxbc, dconv_w8, dconv_b = conv_bwd(xbc, pre, dxc, conv_w)
    dsegs = [dzs, dxbc, dza, dq, dk, dv, dsmall]
    wsegs = [w_zs, w_xbc, w_za, w_q, w_k, w_v, w_small]
    dws = [matmul_tn(u, d, "dw_in_%d" % i) for i, d in enumerate(dsegs)]
    dw_in = jnp.concatenate([dws[0], dws[1], dws[6][:, :N_HEADS], dws[2], dws[3], dws[4], dws[5],
                             dws[6][:, N_HEADS:2 * N_HEADS]], axis=1)
    dw_out = matmul_tn(ycat, dh1_b, "dw_out")
    dw_gate = matmul_tn(n2_b, dgl_b, "dw_gate")
    dw_proj = matmul_tn(p_b, dpp_b, "dw_proj")
    dx, dnorm_g, *parts = in_proj_bwd(dsegs, wsegs, x, row(norm_g), dh1, prereduce(dw_in, dw_out, dw_gate, dw_proj))
    small_grads = [
        dnorm_g, dconv_b, ddtb[0, :N_HEADS], (da * a_row)[0, :N_HEADS],
        ddsk_lane.reshape(N_HEADS, HEAD_DIM).sum(axis=1), dssd_g, dfgb[0, N_HEADS:2 * N_HEADS],
        datt_lane.reshape(N_HEADS, HEAD_DIM).sum(axis=0), dple, dfin]
    loss = jnp.sum(loss_l)
    return loss, dx, parts, dconv_w8[:CONV_WIDTH], small_grads


def kernel(x, p, norm_g, w_in, conv_w, conv_b, dt_bias, a_log, d_skip, ssd_norm_g, fg_bias, att_norm_g, w_out, ple_norm_g, w_ple_gate, w_ple_proj, final_norm_g, loss_target, m_norm_g, m_w_in, m_conv_w, m_conv_b, m_dt_bias, m_a_log, m_d_skip, m_ssd_norm_g, m_fg_bias, m_att_norm_g, m_w_out, m_ple_norm_g, m_w_ple_gate, m_w_ple_proj, m_final_norm_g, v_norm_g, v_w_in, v_conv_w, v_conv_b, v_dt_bias, v_a_log, v_d_skip, v_ssd_norm_g, v_fg_bias, v_att_norm_g, v_w_out, v_ple_norm_g, v_w_ple_gate, v_w_ple_proj, v_final_norm_g):
    chip = 2 * lax.axis_index("x") + lax.axis_index("y")
    core = lax.axis_index("c")

    big_w = [w_in[0], w_out[0], w_ple_gate[0], w_ple_proj[0]]
    own = [a.astype(BF16) for a in big_w] + [conv_w[0]]
    gathered = gather_weights(own[:4], own[4])

    def joined(k, axis):
        return jnp.concatenate([jnp.where(chip == j, own[k], gathered[k][j]) for j in range(N_CHIPS)], axis=axis)

    w_in_f, w_out_f, w_gate_f, w_proj_f, conv_w_f = joined(0, 1), joined(1, 0), joined(2, 0), joined(3, 1), joined(4, 1)

    core1 = core.reshape(1).astype(jnp.int32)

    def prereduce(dw_in, dw_out, dw_gate, dw_proj):
        gs = [jnp.stack([dw_in[:, 1672 * j:1672 * (j + 1)] for j in range(N_CHIPS)]),
              dw_out.reshape(N_CHIPS, 512, D_MODEL), dw_gate.reshape(N_CHIPS, 256, D_MODEL),
              jnp.stack([dw_proj[:, 256 * j:256 * (j + 1)] for j in range(N_CHIPS)])]
        return add_halves(core1, gs, halves_to_sibling(gs))

    smalls_w = [norm_g, conv_b, dt_bias, a_log, d_skip, ssd_norm_g, fg_bias, att_norm_g, ple_norm_g, final_norm_g]
    loss_l, dx, parts, dconv_w, small_grads = local_step(
        prereduce, x[0], p[0, 0], loss_target[0], w_in_f, w_out_f, w_gate_f, w_proj_f, conv_w_f,
        *[a.reshape(-1) for a in smalls_w])
    loss = lax.psum(loss_l, ("x", "y", "c"))
    smalls = gather_small(_pack_small(list(small_grads) + [dconv_w]))
    mine = sum_parts(parts)

    g_big, d_big, m_big, v_big = adamw_big(
        core1, mine, swap_halves(mine), big_w, [m_w_in[0], m_w_out[0], m_w_ple_gate[0], m_w_ple_proj[0]],
        [v_w_in[0], v_w_out[0], v_w_ple_gate[0], v_w_ple_proj[0]])
    smalls_m = [m_norm_g, m_conv_b, m_dt_bias, m_a_log, m_d_skip, m_ssd_norm_g, m_fg_bias, m_att_norm_g,
                m_ple_norm_g, m_final_norm_g]
    smalls_v = [v_norm_g, v_conv_b, v_dt_bias, v_a_log, v_d_skip, v_ssd_norm_g, v_fg_bias, v_att_norm_g,
                v_ple_norm_g, v_final_norm_g]
    g_sm, d_sm, m_sm, v_sm = adamw_small(smalls, _pack_small(smalls_w), _pack_small(smalls_m), _pack_small(smalls_v))
    n_small = sum(SMALL_SIZES)
    g_conv_full = g_sm.reshape(-1)[n_small:n_small + CONV_W_SIZE].reshape(CONV_WIDTH, CONV_CH)
    g_conv = lax.dynamic_slice_in_dim(g_conv_full, chip * 384, 384, axis=1)
    d_conv, m_conv, v_conv = adamw_whole(g_conv, conv_w[0], m_conv_w[0], v_conv_w[0], "adamw_conv")

    shapes = [a.shape for a in smalls_w]
    outs = []
    for big, conv, sm in ((g_big, g_conv, g_sm), (d_big, d_conv, d_sm), (m_big, m_conv, m_sm), (v_big, v_conv, v_sm)):
        b_in, b_out, b_gate, b_proj = [a[None] for a in big]
        s_norm, s_convb, s_dtb, s_alog, s_dsk, s_ssdg, s_fgb, s_attg, s_pleg, s_fin = _unpack_small(sm, shapes)
        outs.extend([s_norm, b_in, conv[None], s_convb, s_dtb, s_alog, s_dsk, s_ssdg, s_fgb, s_attg, b_out, s_pleg,
                     b_gate, b_proj, s_fin])
    return (loss, dx[None], *outs)
```

```python
import functools

import jax
import jax.numpy as jnp
from jax import lax
from jax.experimental import pallas as pl
from jax.experimental.pallas import tpu as pltpu

F32 = jnp.float32
BF16 = jnp.bfloat16

D_MODEL = 1024
SSD_WIDTH = 1024
ATT_WIDTH = 1024
N_HEADS = 16
HEAD_DIM = 64
N_GROUPS = 2
D_STATE = 128
CONV_CH = 1536
CONV_WIDTH = 4
CHUNK = 128
PLE_DIM = 256
D_INNER = 2048
EPS = 1e-6
IN_COLS = 6688
N_CHIPS = 4
N_DEV = 8
LANES = 128
N_PAIRS = 8

ADAM_LR = 0.001
ADAM_B1 = 0.9
ADAM_B2 = 0.999
ADAM_EPS = 1e-08
ADAM_WD = 0.01
ADAM_STEP = 10

SMALL_ROWS = 96

NEG_BIG = -1e30
VMEM_LIMIT = 56 * 1024 * 1024

MESH = pl.DeviceIdType.MESH
ANY = pl.BlockSpec(memory_space=pl.ANY)


def _mm(a, b):
    return jnp.dot(a, b, preferred_element_type=F32)


def _mm_nt(a, b):
    return lax.dot_general(a, b, (((1,), (1,)), ((), ())), preferred_element_type=F32)


def _mm_tn(a, b):
    return lax.dot_general(a, b, (((0,), (0,)), ((), ())), preferred_element_type=F32)


def _mm_exact(a, b):
    return jnp.dot(a, b, preferred_element_type=F32, precision=lax.Precision.HIGHEST)


def _softplus(x):
    return jnp.maximum(x, 0.0) + jnp.log1p(jnp.exp(-jnp.abs(x)))


def _sigmoid(x):
    return jax.nn.sigmoid(x)


def _iota(shape, dim):
    return lax.broadcasted_iota(jnp.int32, shape, dim)


def _params(sem=None):
    return pltpu.CompilerParams(dimension_semantics=sem, vmem_limit_bytes=VMEM_LIMIT)


def _blk(n, pref):
    return min(n, pref)


def _const_spec(shape):
    nd = len(shape)
    return pl.BlockSpec(shape, lambda *_: (0,) * nd)


def _chip_peers():
    x, y, c = lax.axis_index("x"), lax.axis_index("y"), lax.axis_index("c")
    return x, y, c, [(1 - x, y, c), (x, 1 - y, c), (1 - x, 1 - y, c)]


def _half(rows, c):
    h = rows // 2
    return pl.ds(pl.multiple_of(c * h, 8), h)


def _sems(n):
    return [pltpu.SemaphoreType.DMA((n,)), pltpu.SemaphoreType.DMA((n,))]


def gather_weights(shards, conv_s):
    n = len(shards)

    def body(*refs):
        ins, conv_in = refs[:n], refs[n]
        outs, conv_out = refs[n + 1:2 * n + 1], refs[2 * n + 1]
        ssem1, rsem1, ssem2, rsem2, c_ssem, c_rsem = refs[2 * n + 2:]
        x, y, c, peers = _chip_peers()
        me = 2 * x + y
        sibling = (x, y, 1 - c)
        first, small = [], []
        for k, peer in enumerate(peers):
            for i in range(n):
                h = _half(ins[i].shape[0], c)
                first.append(pltpu.make_async_remote_copy(
                    src_ref=ins[i].at[h], dst_ref=outs[i].at[me, h], send_sem=ssem1.at[n * k + i],
                    recv_sem=rsem1.at[n * k + i], device_id=peer, device_id_type=MESH))
            small.append(pltpu.make_async_remote_copy(
                src_ref=conv_in, dst_ref=conv_out.at[me], send_sem=c_ssem.at[k], recv_sem=c_rsem.at[k],
                device_id=peer, device_id_type=MESH))
        for cp in first + small:
            cp.start()
        passed = []
        for k, peer in enumerate(peers):
            chip = 2 * peer[0] + peer[1]
            for i in range(n):
                h = _half(ins[i].shape[0], c)
                first[n * k + i].wait_recv()
                fwd = pltpu.make_async_remote_copy(
                    src_ref=outs[i].at[chip, h], dst_ref=outs[i].at[chip, h], send_sem=ssem2.at[n * k + i],
                    recv_sem=rsem2.at[n * k + i], device_id=sibling, device_id_type=MESH)
                fwd.start()
                passed.append(fwd)
        for cp in passed:
            cp.wait_recv()
        for cp in first + passed:
            cp.wait_send()
        for cp in small:
            cp.wait()

    return pl.pallas_call(
        body, name="gather_weights",
        out_shape=tuple(jax.ShapeDtypeStruct((N_CHIPS,) + a.shape, a.dtype) for a in list(shards) + [conv_s]),
        in_specs=[ANY] * (n + 1), out_specs=(ANY,) * (n + 1),
        scratch_shapes=_sems(3 * n) + _sems(3 * n) + _sems(3),
    )(*shards, conv_s)


def halves_to_sibling(gs):
    n = len(gs)

    def body(*refs):
        ins, outs = refs[:n], refs[n:2 * n]
        ssem, rsem = refs[2 * n:]
        x, y, c = lax.axis_index("x"), lax.axis_index("y"), lax.axis_index("c")
        copies = []
        for i in range(n):
            for j in range(N_CHIPS):
                copies.append(pltpu.make_async_remote_copy(
                    src_ref=ins[i].at[j, _half(ins[i].shape[1], 1 - c)], dst_ref=outs[i].at[j],
                    send_sem=ssem.at[N_CHIPS * i + j], recv_sem=rsem.at[N_CHIPS * i + j],
                    device_id=(x, y, 1 - c), device_id_type=MESH))
        for cp in copies:
            cp.start()
        for cp in copies:
            cp.wait()

    return pl.pallas_call(
        body, name="halves_to_sibling",
        out_shape=tuple(jax.ShapeDtypeStruct((N_CHIPS, g.shape[1] // 2, g.shape[2]), F32) for g in gs),
        in_specs=[ANY] * n, out_specs=(ANY,) * n, scratch_shapes=_sems(N_CHIPS * n),
    )(*gs)


RED_GRID = 8


def add_halves(core, gs, rbs):
    n = len(gs)

    def body(c_ref, *refs):
        for i in range(n):
            refs[2 * n + i][...] = (refs[i][...] + refs[n + i][...]).astype(BF16)

    def blk(g):
        return (1, g.shape[1] // 2 // RED_GRID, g.shape[2])

    grid_spec = pltpu.PrefetchScalarGridSpec(
        num_scalar_prefetch=1, grid=(N_CHIPS, RED_GRID),
        in_specs=([pl.BlockSpec(blk(g), lambda j, b, c_ref: (j, c_ref[0] * RED_GRID + b, 0)) for g in gs]
                  + [pl.BlockSpec(blk(g), lambda j, b, c_ref: (j, b, 0)) for g in gs]),
        out_specs=[pl.BlockSpec(blk(g), lambda j, b, c_ref: (j, b, 0)) for g in gs])
    return pl.pallas_call(
        body, name="add_halves", grid_spec=grid_spec,
        out_shape=tuple(jax.ShapeDtypeStruct(r.shape, BF16) for r in rbs),
        compiler_params=_params(("parallel", "parallel")),
    )(core, *gs, *rbs)


def scatter_copies(ins, outs, ssem, rsem, lsem):
    n = len(ins)
    x, y, _, peers = _chip_peers()
    me = 2 * x + y
    copies = [pltpu.make_async_copy(ins[i].at[me], outs[i].at[me], lsem.at[i]) for i in range(n)]
    for k, peer in enumerate(peers):
        dst_chip = 2 * peer[0] + peer[1]
        for i in range(n):
            copies.append(pltpu.make_async_remote_copy(
                src_ref=ins[i].at[dst_chip], dst_ref=outs[i].at[me], send_sem=ssem.at[n * k + i],
                recv_sem=rsem.at[n * k + i], device_id=peer, device_id_type=MESH))
    return copies


def gather_small(small):
    def body(s_ref, smalls_ref, ssem, rsem, lsem):
        x, y, c = lax.axis_index("x"), lax.axis_index("y"), lax.axis_index("c")
        dev = 4 * x + 2 * y + c
        copies = [pltpu.make_async_copy(s_ref, smalls_ref.at[dev], lsem)]
        for k in range(1, N_DEV):
            fx, fy, fc = (k >> 2) & 1, (k >> 1) & 1, k & 1
            peer = ((1 - x) if fx else x, (1 - y) if fy else y, (1 - c) if fc else c)
            copies.append(pltpu.make_async_remote_copy(
                src_ref=s_ref, dst_ref=smalls_ref.at[dev], send_sem=ssem.at[k - 1], recv_sem=rsem.at[k - 1],
                device_id=peer, device_id_type=MESH))
        for cp in copies:
            cp.start()
        for cp in copies:
            cp.wait()

    return pl.pallas_call(
        body, name="gather_small",
        out_shape=jax.ShapeDtypeStruct((N_DEV,) + small.shape, F32),
        in_specs=[ANY], out_specs=ANY,
        scratch_shapes=_sems(N_DEV - 1) + [pltpu.SemaphoreType.DMA],
    )(small)


def sum_parts(parts):
    n = len(parts)

    def body(*refs):
        for i in range(n):
            p_ref = refs[i]
            refs[n + i][...] = ((p_ref[0].astype(F32) + p_ref[1].astype(F32)) + p_ref[2].astype(F32)
                                ) + p_ref[3].astype(F32)

    def rows(p):
        return p.shape[1] // RED_GRID

    return pl.pallas_call(
        body, name="sum_parts",
        out_shape=tuple(jax.ShapeDtypeStruct(p.shape[1:], F32) for p in parts),
        grid=(RED_GRID,),
        in_specs=[pl.BlockSpec((N_CHIPS, rows(p), p.shape[2]), lambda b: (0, b, 0)) for p in parts],
        out_specs=tuple(pl.BlockSpec((rows(p), p.shape[2]), lambda b: (b, 0)) for p in parts),
        compiler_params=_params(("parallel",)),
    )(*parts)


def swap_halves(reds):
    n = len(reds)

    def body(*refs):
        ins, outs = refs[:n], refs[n:2 * n]
        ssem, rsem = refs[2 * n:]
        x, y, c = lax.axis_index("x"), lax.axis_index("y"), lax.axis_index("c")
        copies = [pltpu.make_async_remote_copy(
            src_ref=ins[i], dst_ref=outs[i], send_sem=ssem.at[i], recv_sem=rsem.at[i],
            device_id=(x, y, 1 - c), device_id_type=MESH) for i in range(n)]
        for cp in copies:
            cp.start()
        for cp in copies:
            cp.wait()

    return pl.pallas_call(
        body, name="swap_halves",
        out_shape=tuple(jax.ShapeDtypeStruct(r.shape, F32) for r in reds),
        in_specs=[ANY] * n, out_specs=(ANY,) * n, scratch_shapes=_sems(n),
    )(*reds)


def _adamw(w, g, m, v):
    m = ADAM_B1 * m + (1.0 - ADAM_B1) * g
    v = ADAM_B2 * v + (1.0 - ADAM_B2) * (g * g)
    m_hat = m / (1.0 - ADAM_B1 ** ADAM_STEP)
    v_hat = v / (1.0 - ADAM_B2 ** ADAM_STEP)
    delta = -ADAM_LR * (m_hat / (jnp.sqrt(v_hat) + ADAM_EPS) + ADAM_WD * w)
    return delta, m, v


def adamw_big(core, mine, theirs, ws, ms, vs):
    n = len(ws)
    per_half = RED_GRID // 2

    def body(c_ref, *refs):
        own = (pl.program_id(0) // per_half) == c_ref[0]
        for i in range(n):
            g = jnp.where(own, refs[i][...], refs[n + i][...])
            d, mn, vn = _adamw(refs[2 * n + i][...], g, refs[3 * n + i][...], refs[4 * n + i][...])
            refs[5 * n + i][...] = g
            refs[6 * n + i][...] = d
            refs[7 * n + i][...] = mn
            refs[8 * n + i][...] = vn

    def blk(w):
        return (w.shape[0] // RED_GRID, w.shape[1])

    halves = [pl.BlockSpec(blk(w), lambda b, c_ref: (b % per_half, 0)) for w in ws]
    whole = [pl.BlockSpec(blk(w), lambda b, c_ref: (b, 0)) for w in ws]
    shapes = [jax.ShapeDtypeStruct(w.shape, F32) for w in ws]
    grid_spec = pltpu.PrefetchScalarGridSpec(
        num_scalar_prefetch=1, grid=(RED_GRID,), in_specs=halves * 2 + whole * 3, out_specs=whole * 4)
    outs = pl.pallas_call(
        body, name="adamw_big", out_shape=tuple(shapes * 4), grid_spec=grid_spec,
        compiler_params=_params(("parallel",)),
    )(core, *mine, *theirs, *ws, *ms, *vs)
    return outs[:n], outs[n:2 * n], outs[2 * n:3 * n], outs[3 * n:]


def adamw_whole(g, w, m, v, name):
    def body(g_ref, w_ref, m_ref, v_ref, d_out, m_out, v_out):
        d, mn, vn = _adamw(w_ref[...], g_ref[...], m_ref[...], v_ref[...])
        d_out[...] = d
        m_out[...] = mn
        v_out[...] = vn

    shp = jax.ShapeDtypeStruct(g.shape, F32)
    return pl.pallas_call(body, name=name, out_shape=(shp,) * 3)(g, w, m, v)


def adamw_small(smalls, w, m, v):
    def body(s_ref, w_ref, m_ref, v_ref, g_out, d_out, m_out, v_out):
        g = s_ref[0]
        for k in range(1, N_DEV):
            g = g + s_ref[k]
        d, mn, vn = _adamw(w_ref[...], g, m_ref[...], v_ref[...])
        g_out[...] = g
        d_out[...] = d
        m_out[...] = mn
        v_out[...] = vn

    shp = jax.ShapeDtypeStruct((SMALL_ROWS, LANES), F32)
    return pl.pallas_call(body, name="adamw_small", out_shape=(shp,) * 4)(smalls, w, m, v)


def rms_prenorm(x, g):
    s = x.shape[0]
    tm = _blk(s, 512)

    def body(x_ref, g_ref, u_ref):
        xv = x_ref[...]
        r = lax.rsqrt(jnp.mean(xv * xv, axis=-1, keepdims=True) + EPS)
        u_ref[...] = (xv * r * g_ref[...]).astype(BF16)

    return pl.pallas_call(
        body, name="rms_prenorm", out_shape=jax.ShapeDtypeStruct(x.shape, BF16), grid=(s // tm,),
        in_specs=[pl.BlockSpec((tm, D_MODEL), lambda i: (i, 0)), _const_spec((1, D_MODEL))],
        out_specs=pl.BlockSpec((tm, D_MODEL), lambda i: (i, 0)), compiler_params=_params(("parallel",)),
    )(x, g)


def matmul_rows(a, w, out_dtype, name):
    s, k = a.shape
    n = w.shape[1]
    tm = _blk(s, 512)

    def body(a_ref, w_ref, o_ref):
        o_ref[...] = _mm(a_ref[...], w_ref[...]).astype(out_dtype)

    return pl.pallas_call(
        body, name=name, out_shape=jax.ShapeDtypeStruct((s, n), out_dtype), grid=(s // tm,),
        in_specs=[pl.BlockSpec((tm, k), lambda i: (i, 0)), _const_spec((k, n))],
        out_specs=pl.BlockSpec((tm, n), lambda i: (i, 0)), compiler_params=_params(("parallel",)),
    )(a, w)


def matmul_tn(a, b, name):
    s, m = a.shape
    n = b.shape[1]
    tk = _blk(s, 2048)
    tn = _blk(n, 512)

    def body(a_ref, b_ref, o_ref):
        @pl.when(pl.program_id(1) == 0)
        def _():
            o_ref[...] = jnp.zeros_like(o_ref)

        o_ref[...] += _mm_tn(a_ref[...], b_ref[...])

    return pl.pallas_call(
        body, name=name, out_shape=jax.ShapeDtypeStruct((m, n), F32), grid=(n // tn, s // tk),
        in_specs=[pl.BlockSpec((tk, m), lambda j, i: (i, 0)), pl.BlockSpec((tk, tn), lambda j, i: (i, j))],
        out_specs=pl.BlockSpec((m, tn), lambda j, i: (0, j)),
        compiler_params=_params(("parallel", "arbitrary")),
    )(a, b)


def conv_fwd(xbc, w, b):
    s = xbc.shape[0]
    tm = _blk(s, 256)

    def body(x_ref, t_ref, w_ref, b_ref, pre_ref, act_ref):
        i = pl.program_id(0)
        cur = x_ref[...]
        tail = jnp.where(i > 0, t_ref[...], 0.0)
        wv = w_ref[...]
        acc = cur * wv[3:4, :] + b_ref[...]
        head = cur[0:8, :] * wv[3:4, :] + b_ref[...]
        row8 = _iota((8, CONV_CH), 0)
        for sh in range(1, CONV_WIDTH):
            wk = wv[3 - sh:4 - sh, :]
            acc = acc + pltpu.roll(cur, sh, 0) * wk
            first = jnp.where(row8 < sh, pltpu.roll(tail, sh, 0), pltpu.roll(cur[0:8, :], sh, 0))
            head = head + first * wk
        pre_ref[...] = acc
        act_ref[...] = acc * _sigmoid(acc)
        pre_ref[0:8, :] = head
        act_ref[0:8, :] = head * _sigmoid(head)

    shp = jax.ShapeDtypeStruct(xbc.shape, F32)
    rows = pl.BlockSpec((tm, CONV_CH), lambda i: (i, 0))
    return pl.pallas_call(
        body, name="conv_fwd", out_shape=(shp, shp), grid=(s // tm,),
        in_specs=[rows, pl.BlockSpec((8, CONV_CH), lambda i: (jnp.maximum(i * (tm // 8) - 1, 0), 0)),
                  _const_spec((CONV_WIDTH, CONV_CH)), _const_spec((1, CONV_CH))],
        out_specs=(rows, rows), compiler_params=_params(("parallel",)),
    )(xbc, xbc, w, b)


def conv_bwd(xbc, pre, dact, w):
    s = xbc.shape[0]
    tm = _blk(s, 256)
    nb = s // tm

    def dsilu(p):
        sg = _sigmoid(p)
        return sg * (1.0 + p * (1.0 - sg))

    def body(x_ref, xt_ref, p_ref, pn_ref, d_ref, dn_ref, w_ref, dx_ref, dw_ref, db_ref):
        i = pl.program_id(0)

        @pl.when(i == 0)
        def _():
            dw_ref[...] = jnp.zeros_like(dw_ref)
            db_ref[...] = jnp.zeros_like(db_ref)

        wv = w_ref[...]
        dpre = d_ref[...] * dsilu(p_ref[...])
        dnext = jnp.where(i < nb - 1, dn_ref[...] * dsilu(pn_ref[...]), 0.0)
        cur = x_ref[...]
        tail = jnp.where(i > 0, xt_ref[...], 0.0)
        row8 = _iota((8, CONV_CH), 0)
        dx = dpre * wv[3:4, :]
        last = dpre[tm - 8:tm, :] * wv[3:4, :]
        db_ref[...] += jnp.sum(dpre, axis=0, keepdims=True)
        dws = [jnp.sum(dpre * cur, axis=0, keepdims=True)]
        for sh in range(1, CONV_WIDTH):
            wk = wv[3 - sh:4 - sh, :]
            dx = dx + pltpu.roll(dpre, tm - sh, 0) * wk
            nxt = jnp.where(row8 >= 8 - sh, pltpu.roll(dnext, 8 - sh, 0), pltpu.roll(dpre[tm - 8:tm, :], 8 - sh, 0))
            last = last + nxt * wk
            xs = pltpu.roll(cur, sh, 0)
            first = jnp.where(row8 < sh, pltpu.roll(tail, sh, 0), xs[0:8, :])
            dws.append(jnp.sum(dpre * xs, axis=0, keepdims=True)
                       + jnp.sum(dpre[0:8, :] * (first - xs[0:8, :]), axis=0, keepdims=True))
        dx_ref[...] = dx.astype(BF16)
        dx_ref[tm - 8:tm, :] = last.astype(BF16)
        for sh in range(CONV_WIDTH):
            dw_ref[3 - sh:4 - sh, :] += dws[sh]

    rows = pl.BlockSpec((tm, CONV_CH), lambda i: (i, 0))
    prev8 = pl.BlockSpec((8, CONV_CH), lambda i: (jnp.maximum(i * (tm // 8) - 1, 0), 0))
    next8 = pl.BlockSpec((8, CONV_CH), lambda i: (jnp.minimum((i + 1) * (tm // 8), s // 8 - 1), 0))
    return pl.pallas_call(
        body, name="conv_bwd",
        out_shape=(jax.ShapeDtypeStruct(xbc.shape, BF16), jax.ShapeDtypeStruct((8, CONV_CH), F32),
                   jax.ShapeDtypeStruct((1, CONV_CH), F32)),
        grid=(nb,),
        in_specs=[rows, prev8, rows, next8, rows, next8, _const_spec((CONV_WIDTH, CONV_CH))],
        out_specs=(rows, _const_spec((8, CONV_CH)), _const_spec((1, CONV_CH))),
        compiler_params=_params(("arbitrary",)),
    )(xbc, xbc, pre, pre, dact, dact, w)


def _pair_lanes(mat, j, lane):
    return jnp.where(lane < HEAD_DIM, mat[:, 2 * j:2 * j + 1], mat[:, 2 * j + 1:2 * j + 2])


def _ssd_chunk_prelude(sm, dtb, a_row, lane, sub):
    raw = sm + dtb
    head_lane = lane < N_HEADS
    dt = jnp.where(head_lane, _softplus(raw), 0.0)
    sig = jnp.where(head_lane, _sigmoid(raw), 0.0)
    tri = (lane <= sub).astype(F32)
    acs = _mm_exact(tri, dt * a_row)
    return dt, sig, acs, acs.T


GROUP_WIDTH = SSD_WIDTH // N_GROUPS
HEADS_PER_GROUP = N_HEADS // N_GROUPS


def _expand_group(mat, g, lane):
    return jnp.concatenate([_pair_lanes(mat, j, lane) for j in range(4 * g, 4 * g + 4)], axis=1)


def _head_sums(q, g):
    row = _iota((GROUP_WIDTH, LANES), 0)
    seg = (_iota((GROUP_WIDTH, LANES), 1) == HEADS_PER_GROUP * g + (row >> 6)).astype(BF16)
    hi = q.astype(BF16)
    lo = (q - hi.astype(F32)).astype(BF16)
    return _mm(hi, seg) + _mm(lo, seg)


def _rows_from_lanes(row512):
    return jnp.broadcast_to(row512, (LANES, GROUP_WIDTH)).T


def ssd_fwd(xc, small, dtb_row, a_row, dskip_lane):
    s = xc.shape[0]
    nc = s // CHUNK

    def body(xc_ref, sm_ref, dtb_ref, a_ref, dsk_ref, y_ref, hs_ref, h_scr):
        c = pl.program_id(0)

        @pl.when(c == 0)
        def _():
            h_scr[...] = jnp.zeros_like(h_scr)

        lane = _iota((CHUNK, LANES), 1)
        sub = _iota((CHUNK, LANES), 0)
        causal = lane <= sub
        dt, _, acs, acs_t = _ssd_chunk_prelude(sm_ref[...], dtb_ref[...], a_ref[...], lane, sub)
        for g in range(N_GROUPS):
            cols = slice(GROUP_WIDTH * g, GROUP_WIDTH * (g + 1))
            b_off = SSD_WIDTH + D_STATE * g
            c_off = SSD_WIDTH + N_GROUPS * D_STATE + D_STATE * g
            b_b = xc_ref[:, b_off:b_off + D_STATE].astype(BF16)
            c_b = xc_ref[:, c_off:c_off + D_STATE].astype(BF16)
            cb = _mm_nt(c_b, b_b)
            x_g = xc_ref[:, cols]
            acs_g = _expand_group(acs, g, lane)
            xdt_g = x_g * _expand_group(dt, g, lane)
            xdt_b = xdt_g.astype(BF16)
            heads = range(HEADS_PER_GROUP * g, HEADS_PER_GROUP * (g + 1))
            m_b = [(cb * jnp.exp(jnp.where(causal, acs[:, h:h + 1] - acs_t[h:h + 1, :], NEG_BIG))).astype(BF16)
                   for h in heads]
            yd = [_mm(m_b[k], xdt_b[:, LANES * (k // 2):LANES * (k // 2 + 1)]) for k in range(HEADS_PER_GROUP)]
            yd_g = jnp.concatenate([jnp.where(lane < HEAD_DIM, yd[2 * k], yd[2 * k + 1]) for k in range(4)], axis=1)
            h_g = h_scr[g]
            t_g = _mm_nt(c_b, h_g.astype(BF16))
            y_ref[:, cols] = yd_g + jnp.exp(acs_g) * t_g + dsk_ref[:, cols] * x_g
            hs_ref[0, g] = h_g
            last_g = acs_g[CHUNK - 1:CHUNK, :]
            w_b = (xdt_g * jnp.exp(last_g - acs_g)).astype(BF16)
            h_scr[g] = h_g * jnp.exp(_rows_from_lanes(last_g)) + _mm_tn(w_b, b_b)

    return pl.pallas_call(
        body, name="ssd_fwd",
        out_shape=(jax.ShapeDtypeStruct((s, SSD_WIDTH), F32),
                   jax.ShapeDtypeStruct((nc, N_GROUPS, GROUP_WIDTH, D_STATE), F32)),
        grid=(nc,),
        in_specs=[pl.BlockSpec((CHUNK, CONV_CH), lambda c: (c, 0)), pl.BlockSpec((CHUNK, LANES), lambda c: (c, 0)),
                  _const_spec((1, LANES)), _const_spec((1, LANES)), _const_spec((1, SSD_WIDTH))],
        out_specs=(pl.BlockSpec((CHUNK, SSD_WIDTH), lambda c: (c, 0)),
                   pl.BlockSpec((1, N_GROUPS, GROUP_WIDTH, D_STATE), lambda c: (c, 0, 0, 0))),
        scratch_shapes=[pltpu.VMEM((N_GROUPS, GROUP_WIDTH, D_STATE), F32)],
        compiler_params=_params(("arbitrary",)),
    )(xc, small, dtb_row, a_row, dskip_lane)


def ssd_bwd(xc, small, states, dy, dtb_row, a_row, dskip_lane):
    s = xc.shape[0]
    nc = s // CHUNK
    rev = lambda c: nc - 1 - c

    def body(xc_ref, sm_ref, hs_ref, dy_ref, dtb_ref, a_ref, dsk_ref,
             dxc_ref, ddt_ref, da_ref, ddtb_ref, ddsk_ref, dh_scr):
        c = pl.program_id(0)

        @pl.when(c == 0)
        def _():
            dh_scr[...] = jnp.zeros_like(dh_scr)
            da_ref[...] = jnp.zeros_like(da_ref)
            ddtb_ref[...] = jnp.zeros_like(ddtb_ref)
            ddsk_ref[...] = jnp.zeros_like(ddsk_ref)

        lane = _iota((CHUNK, LANES), 1)
        sub = _iota((CHUNK, LANES), 0)
        causal = lane <= sub
        upper = lane >= sub
        is_last = sub == CHUNK - 1
        a_row_v = a_ref[...]
        dt, sig, acs, acs_t = _ssd_chunk_prelude(sm_ref[...], dtb_ref[...], a_row_v, lane, sub)
        cd = jnp.exp(acs[CHUNK - 1:CHUNK, :])
        dacs_c = jnp.zeros((CHUNK, LANES), F32)
        dacs_r = jnp.zeros((LANES, CHUNK), F32)
        ddtx = jnp.zeros((CHUNK, LANES), F32)
        for g in range(N_GROUPS):
            cols = slice(GROUP_WIDTH * g, GROUP_WIDTH * (g + 1))
            b_off = SSD_WIDTH + D_STATE * g
            c_off = SSD_WIDTH + N_GROUPS * D_STATE + D_STATE * g
            b_b = xc_ref[:, b_off:b_off + D_STATE].astype(BF16)
            c_b = xc_ref[:, c_off:c_off + D_STATE].astype(BF16)
            cb = _mm_nt(c_b, b_b)
            cb_t = _mm_nt(b_b, c_b)
            x_g = xc_ref[:, cols]
            dy_g = dy_ref[:, cols]
            dt_g = _expand_group(dt, g, lane)
            acs_g = _expand_group(acs, g, lane)
            last_g = acs_g[CHUNK - 1:CHUNK, :]
            e_g = jnp.exp(acs_g)
            dte_g = jnp.exp(last_g - acs_g)
            xdt_g = x_g * dt_g
            xdt_b = xdt_g.astype(BF16)
            h_g = hs_ref[0, g]
            dh_g = dh_scr[g]
            h_b = h_g.astype(BF16)
            dh_b = dh_g.astype(BF16)
            heads = list(range(HEADS_PER_GROUP * g, HEADS_PER_GROUP * (g + 1)))
            segs = [acs[:, h:h + 1] - acs_t[h:h + 1, :] for h in heads]
            lms = [jnp.exp(jnp.where(causal, sg, NEG_BIG)) for sg in segs]
            mts = [(cb_t * jnp.exp(jnp.where(upper, -sg, NEG_BIG))).astype(BF16) for sg in segs]
            dyh = []
            for k in range(HEADS_PER_GROUP):
                blk = dy_g[:, LANES * (k // 2):LANES * (k // 2 + 1)]
                in_head = (lane < HEAD_DIM) if k % 2 == 0 else (lane >= HEAD_DIM)
                dyh.append(jnp.where(in_head, blk, 0.0).astype(BF16))
            dms = [_mm_nt(dyh[k], xdt_b[:, LANES * (k // 2):LANES * (k // 2 + 1)]) for k in range(HEADS_PER_GROUP)]
            dxs = [_mm(mts[k], dyh[k]) for k in range(HEADS_PER_GROUP)]
            dcb = jnp.zeros((CHUNK, CHUNK), F32)
            for k, h in enumerate(heads):
                gmat = dms[k] * (cb * lms[k])
                dacs_c = dacs_c + jnp.where(lane == h, jnp.sum(gmat, axis=1, keepdims=True), 0.0)
                dacs_r = dacs_r - jnp.where(sub == h, jnp.sum(gmat, axis=0, keepdims=True), 0.0)
                dcb = dcb + dms[k] * lms[k]
            dxdt_g = jnp.concatenate([dxs[2 * k] + dxs[2 * k + 1] for k in range(4)], axis=1)
            t_g = _mm_nt(c_b, h_b)
            dacs_c = dacs_c + _head_sums(dy_g * e_g * t_g, g)
            dt_b = (dy_g * e_g).astype(BF16)
            dc_acc = _mm(dt_b, h_b)
            dh_prev = _mm_tn(dt_b, c_b)
            dw_g = _mm_nt(b_b, dh_b)
            w_g = xdt_g * dte_g
            dxdt_g = dxdt_g + dw_g * dte_g
            db_acc = _mm(w_g.astype(BF16), dh_b)
            r2 = _head_sums(dw_g * w_g, g)
            dacs_c = dacs_c + jnp.where(is_last, jnp.sum(r2, axis=0, keepdims=True), 0.0) - r2
            q3 = jnp.sum(dh_g * h_g, axis=1, keepdims=True)
            for k, h in enumerate(heads):
                tot = jnp.sum(q3[HEAD_DIM * k:HEAD_DIM * (k + 1), :], keepdims=True) * cd[:, h:h + 1]
                dacs_c = dacs_c + jnp.where(is_last & (lane == h), tot, 0.0)
            dh_scr[g] = dh_prev + dh_g * jnp.exp(_rows_from_lanes(last_g))
            dxc_ref[:, cols] = dxdt_g * dt_g + dsk_ref[:, cols] * dy_g
            ddtx = ddtx + _head_sums(dxdt_g * x_g, g)
            ddsk_ref[:, cols] += jnp.sum(dy_g * x_g, axis=0, keepdims=True)
            dxc_ref[:, b_off:b_off + D_STATE] = db_acc + _mm(dcb.T.astype(BF16), c_b)
            dxc_ref[:, c_off:c_off + D_STATE] = dc_acc + _mm(dcb.astype(BF16), b_b)
        dacs = dacs_c + dacs_r.T
        dadt = _mm_exact((lane >= sub).astype(F32), dacs)
        ddt = dadt * a_row_v + ddtx
        ddt_raw = ddt * sig
        ddt_ref[...] = ddt_raw
        da_ref[...] += jnp.sum(dadt * dt, axis=0, keepdims=True)
        ddtb_ref[...] += jnp.sum(ddt_raw, axis=0, keepdims=True)

    return pl.pallas_call(
        body, name="ssd_bwd",
        out_shape=(jax.ShapeDtypeStruct((s, CONV_CH), F32), jax.ShapeDtypeStruct((s, LANES), F32),
                   jax.ShapeDtypeStruct((1, LANES), F32), jax.ShapeDtypeStruct((1, LANES), F32),
                   jax.ShapeDtypeStruct((1, SSD_WIDTH), F32)),
        grid=(nc,),
        in_specs=[pl.BlockSpec((CHUNK, CONV_CH), lambda c: (rev(c), 0)),
                  pl.BlockSpec((CHUNK, LANES), lambda c: (rev(c), 0)),
                  pl.BlockSpec((1, N_GROUPS, GROUP_WIDTH, D_STATE), lambda c: (rev(c), 0, 0, 0)),
                  pl.BlockSpec((CHUNK, SSD_WIDTH), lambda c: (rev(c), 0)),
                  _const_spec((1, LANES)), _const_spec((1, LANES)), _const_spec((1, SSD_WIDTH))],
        out_specs=(pl.BlockSpec((CHUNK, CONV_CH), lambda c: (rev(c), 0)),
                   pl.BlockSpec((CHUNK, LANES), lambda c: (rev(c), 0)),
                   _const_spec((1, LANES)), _const_spec((1, LANES)), _const_spec((1, SSD_WIDTH))),
        scratch_shapes=[pltpu.VMEM((N_GROUPS, GROUP_WIDTH, D_STATE), F32)],
        compiler_params=_params(("arbitrary",)),
    )(xc, small, states, dy, dtb_row, a_row, dskip_lane)


FORGET_BLOCK = 512


def forget_cumsum(small, fgb_row):
    s = small.shape[0]
    t = _blk(s, FORGET_BLOCK)
    nb = s // t

    def body(sm_ref, b_ref, cc_ref, carry):
        i = pl.program_id(0)

        @pl.when(i == 0)
        def _():
            carry[...] = jnp.zeros_like(carry)

        lane = _iota((t, LANES), 1)
        in_f = (lane >= N_HEADS) & (lane < 2 * N_HEADS)
        logf = jnp.where(in_f, -_softplus(-(sm_ref[...] + b_ref[...])), 0.0)
        tri = (_iota((t, t), 1) <= _iota((t, t), 0)).astype(F32)
        cum = _mm_exact(tri, logf) + carry[0:1, :]
        cc_ref[...] = cum
        carry[...] = jnp.broadcast_to(cum[t - 1:t, :], (8, LANES))

    return pl.pallas_call(
        body, name="forget_cumsum",
        out_shape=jax.ShapeDtypeStruct((s, LANES), F32),
        grid=(nb,),
        in_specs=[pl.BlockSpec((t, LANES), lambda i: (i, 0)), _const_spec((1, LANES))],
        out_specs=pl.BlockSpec((t, LANES), lambda i: (i, 0)),
        scratch_shapes=[pltpu.VMEM((8, LANES), F32)],
        compiler_params=_params(("arbitrary",)),
    )(small, fgb_row)


def forget_bwd(dc, small, ddt_raw, fgb_row):
    s = small.shape[0]
    t = _blk(s, FORGET_BLOCK)
    nb = s // t
    rev = lambda i: nb - 1 - i

    def body(dc_ref, sm_ref, ddt_ref, b_ref, ds_ref, dfb_ref, carry):
        i = pl.program_id(0)

        @pl.when(i == 0)
        def _():
            carry[...] = jnp.zeros_like(carry)
            dfb_ref[...] = jnp.zeros_like(dfb_ref)

        lane = _iota((t, LANES), 1)
        rows = dc_ref[...].T
        tri = (_iota((t, t), 1) <= _iota((t, t), 0)).astype(F32)
        rc = _mm_exact(rows, tri) + carry[:, 0:1]
        carry[...] = jnp.broadcast_to(rc[:, 0:1], (LANES, LANES))
        in_f = (lane >= N_HEADS) & (lane < 2 * N_HEADS)
        df = jnp.where(in_f, rc.T * _sigmoid(-(sm_ref[...] + b_ref[...])), 0.0)
        ds_ref[...] = (df + ddt_ref[...]).astype(BF16)
        dfb_ref[...] += jnp.sum(df, axis=0, keepdims=True)

    blk = pl.BlockSpec((t, LANES), lambda i: (rev(i), 0))
    return pl.pallas_call(
        body, name="forget_bwd",
        out_shape=(jax.ShapeDtypeStruct((s, LANES), BF16), jax.ShapeDtypeStruct((1, LANES), F32)),
        grid=(nb,),
        in_specs=[blk, blk, blk, _const_spec((1, LANES))],
        out_specs=(blk, _const_spec((1, LANES))),
        scratch_shapes=[pltpu.VMEM((LANES, LANES), F32)],
        compiler_params=_params(("arbitrary",)),
    )(dc, small, ddt_raw, fgb_row)


ATT_BLOCK = 512
ATT_SCALE = HEAD_DIM ** -0.5
AUG_A = HEAD_DIM
AUG_B = HEAD_DIM + 3


def _split3(c):
    hi = c.astype(BF16).astype(F32)
    r = c - hi
    mid = r.astype(BF16).astype(F32)
    return hi, mid, (r - mid).astype(BF16).astype(F32)


def _aug(lane, first, parts=None, value=1.0):
    if parts is None:
        return jnp.where((lane >= first) & (lane < first + 3), value, 0.0)
    return (jnp.where(lane == first, parts[0], 0.0) + jnp.where(lane == first + 1, parts[1], 0.0)
            + jnp.where(lane == first + 2, parts[2], 0.0))


def _pack_pair(a0, a1, lane):
    return jnp.where(lane < HEAD_DIM, a0, pltpu.roll(a1, HEAD_DIM, 1))


def proj_qkv_heads(u, w_q, w_k, w_v, cum):
    s = u.shape[0]
    tm = _blk(s, 256)

    def body(u_ref, wq_ref, wk_ref, wv_ref, c_ref, qa_ref, ka_ref, va_ref, nrm_ref):
        lane = _iota((tm, LANES), 1)
        lo = lane < HEAD_DIM
        uv = u_ref[...]
        qf = _mm(uv, wq_ref[...]) * ATT_SCALE
        kf = _mm(uv, wk_ref[...])
        vf = _mm(uv, wv_ref[...])
        cc = c_ref[...]
        ones_a = _aug(lane, AUG_A)
        ones_b = _aug(lane, AUG_B)
        sub8 = _iota((8, LANES), 0)
        nrm = jnp.zeros((8, LANES), F32)
        for h in range(N_HEADS):
            j, e = divmod(h, 2)

            def head(full):
                blk = full[:, LANES * j:LANES * (j + 1)]
                if e == 1:
                    blk = pltpu.roll(blk, HEAD_DIM, 1)
                return jnp.where(lo, blk, 0.0)

            parts = _split3(cc[:, N_HEADS + h:N_HEADS + h + 1])
            qh, kh = head(qf), head(kf)
            qa_ref[h] = (qh + _aug(lane, AUG_A, parts) + ones_b).astype(BF16)
            ka_ref[h] = (kh + ones_a - _aug(lane, AUG_B, parts)).astype(BF16)
            va_ref[h] = (head(vf) + ones_a).astype(BF16)
        seg = (_iota((ATT_WIDTH, LANES), 1) == (_iota((ATT_WIDTH, LANES), 0) >> 6)).astype(BF16)
        for r, val in enumerate((qf, kf)):
            sq = val * val
            hi = sq.astype(BF16)
            tot = _mm(hi, seg) + _mm((sq - hi.astype(F32)).astype(BF16), seg)
            nrm = nrm + jnp.where(sub8 == r, jnp.max(tot, axis=0, keepdims=True), 0.0)
        nrm_ref[0] = nrm

    shp = jax.ShapeDtypeStruct((N_HEADS, s, LANES), BF16)
    hspec = pl.BlockSpec((N_HEADS, tm, LANES), lambda i: (0, i, 0))
    wspec = _const_spec((D_MODEL, ATT_WIDTH))
    return pl.pallas_call(
        body, name="proj_qkv_heads",
        out_shape=(shp, shp, shp, jax.ShapeDtypeStruct((s // tm, 8, LANES), F32)), grid=(s // tm,),
        in_specs=[pl.BlockSpec((tm, D_MODEL), lambda i: (i, 0)), wspec, wspec, wspec,
                  pl.BlockSpec((tm, LANES), lambda i: (i, 0))],
        out_specs=(hspec, hspec, hspec, pl.BlockSpec((1, 8, LANES), lambda i: (i, 0, 0))),
        compiler_params=_params(("parallel",)),
    )(u, w_q, w_k, w_v, cum)


SKIP_BELOW = -110.0


def live_blocks(norms, cum, t):
    qn = jnp.sqrt(jnp.max(norms[:, 0, :N_HEADS], axis=0))
    kn = jnp.sqrt(jnp.max(norms[:, 1, :N_HEADS], axis=0))
    bound = 2.05 * qn * kn + 2.0
    c_first = cum[0::t, N_HEADS:2 * N_HEADS]
    c_last = cum[t - 1::t, N_HEADS:2 * N_HEADS]
    nq = c_first.shape[0]
    top = bound[None, None, :] + c_first[:, None, :] - c_last[None, :, :]
    below = jnp.arange(nq)[None, :] < jnp.arange(nq)[:, None]
    dead = below[:, :, None] & ~(top >= SKIP_BELOW)
    first = jnp.sum(dead, axis=1).astype(jnp.int32).T
    last_q = jnp.sum(first[:, None, :] <= jnp.arange(nq)[None, :, None], axis=2).astype(jnp.int32) - 1
    return first, last_q


def attention_fwd(first, qa, ka, va):
    s = qa.shape[1]
    t = _blk(s, ATT_BLOCK)
    nq = s // t

    def body(first_ref, qa_ref, ka_ref, va_ref, o_ref, qb_ref, m_scr, acc_scr, alpha_scr, p_scr, s_scr):
        qi = pl.program_id(1)
        starts = [first_ref[2 * pl.program_id(0) + e, qi] for e in range(2)]
        k0 = jnp.maximum(starts[0], starts[1])
        m_scr[...] = jnp.full_like(m_scr, NEG_BIG)
        acc_scr[...] = jnp.zeros_like(acc_scr)

        def kv_rows(kb):
            return pl.ds(pl.multiple_of(kb * t, t), t)

        def logits(kb, masked, heads=(0, 1)):
            for e in heads:
                sc = _mm_nt(qa_ref[e], ka_ref[e, kv_rows(kb), :])
                if masked:
                    sc = jnp.where(_iota((t, t), 0) >= _iota((t, t), 1), sc, NEG_BIG)
                s_scr[e] = sc

        def probs(heads=(0, 1)):
            for e in heads:
                cmax = s_scr[e, :, 0:LANES]
                for c in range(1, t // LANES):
                    cmax = jnp.maximum(cmax, s_scr[e, :, LANES * c:LANES * (c + 1)])
                m_old = m_scr[e]
                m_new = jnp.maximum(m_old, jnp.max(cmax, axis=1, keepdims=True))
                alpha_scr[e] = jnp.exp(m_old - m_new)
                m_scr[e] = m_new
                for c in range(t // LANES):
                    cols = slice(LANES * c, LANES * (c + 1))
                    p_scr[e, :, cols] = jnp.exp((s_scr[e, :, cols] - m_new).astype(BF16))

        def accumulate(kb, heads=(0, 1)):
            for e in heads:
                acc_scr[e] = alpha_scr[e] * acc_scr[e] + _mm(p_scr[e], va_ref[e, kv_rows(kb), :])

        for e in range(2):
            def alone(kb, carry, e=e):
                logits(kb, False, (e,))
                probs((e,))
                accumulate(kb, (e,))
                return carry

            lax.fori_loop(starts[e], k0, alone, 0)

        def loop_body(kb, carry):
            logits(kb, False)
            for e in range(2):
                accumulate(kb - 1, (e,))
                probs((e,))
            return carry

        @pl.when(qi > k0)
        def _():
            logits(k0, False)
            probs()

        lax.fori_loop(k0 + 1, qi, loop_body, 0)

        @pl.when(qi > k0)
        def _():
            logits(qi, True)
            accumulate(qi - 1)
            probs()

        @pl.when(qi == k0)
        def _():
            logits(qi, True)
            probs()

        accumulate(qi)

        lane = _iota((t, LANES), 1)
        outs = []
        for e in range(2):
            acc = acc_scr[e]
            l = acc[:, AUG_A:AUG_A + 1]
            outs.append(acc / l)
            lse = m_scr[e][:, 0:1] + jnp.log(l)
            q32 = qa_ref[e].astype(F32)
            c = q32[:, AUG_A:AUG_A + 1] + q32[:, AUG_A + 1:AUG_A + 2] + q32[:, AUG_A + 2:AUG_A + 3]
            qb = jnp.where(lane < HEAD_DIM, q32, 0.0) + _aug(lane, AUG_A, _split3(c - lse)) + _aug(lane, AUG_B)
            qb_ref[e] = qb.astype(BF16)
        o_ref[...] = _pack_pair(outs[0], outs[1], lane)

    grid_spec = pltpu.PrefetchScalarGridSpec(
        num_scalar_prefetch=1, grid=(N_PAIRS, nq),
        in_specs=[pl.BlockSpec((2, t, LANES), lambda j, qi, f: (j, qi, 0)),
                  pl.BlockSpec((2, s, LANES), lambda j, qi, f: (j, 0, 0)),
                  pl.BlockSpec((2, s, LANES), lambda j, qi, f: (j, 0, 0))],
        out_specs=[pl.BlockSpec((t, LANES), lambda j, qi, f: (qi, j)),
                   pl.BlockSpec((2, t, LANES), lambda j, qi, f: (j, qi, 0))],
        scratch_shapes=[pltpu.VMEM((2, t, LANES), F32), pltpu.VMEM((2, t, LANES), F32),
                        pltpu.VMEM((2, t, LANES), F32), pltpu.VMEM((2, t, t), BF16), pltpu.VMEM((2, t, t), F32)])
    return pl.pallas_call(
        body, name="attention_fwd", grid_spec=grid_spec,
        out_shape=(jax.ShapeDtypeStruct((s, ATT_WIDTH), F32), jax.ShapeDtypeStruct((N_HEADS, s, LANES), BF16)),
        compiler_params=_params(("parallel", "parallel")),
    )(first, qa, ka, va)


def attention_bwd(last_q, qb, ka, va, dob):
    s = qb.shape[1]
    t = _blk(s, ATT_BLOCK)
    nq = s // t

    def body(last_ref, qb_ref, dob_ref, ka_ref, va_ref, dq_ref, dk_ref, dv_ref, dc_ref, dq_scr, dk_scr, dv_scr):
        j, ki = pl.program_id(0), pl.program_id(1)

        @pl.when((j == 0) & (ki == 0))
        def _():
            dc_ref[...] = jnp.zeros_like(dc_ref)

        @pl.when(ki == 0)
        def _():
            dq_scr[...] = jnp.zeros_like(dq_scr)

        dk_scr[...] = jnp.zeros_like(dk_scr)
        dv_scr[...] = jnp.zeros_like(dv_scr)

        def q_step(qblk, masked, heads=(0, 1)):
            rows = pl.ds(pl.multiple_of(qblk * t, t), t)
            scs = [_mm_nt(qb_ref[e, rows, :], ka_ref[e]) for e in heads]
            dps = [_mm_nt(dob_ref[e, rows, :], va_ref[e]) for e in heads]
            for e, sc, dp in zip(heads, scs, dps):
                q = qb_ref[e, rows, :]
                do = dob_ref[e, rows, :]
                if masked:
                    sc = jnp.where(_iota((t, t), 0) >= _iota((t, t), 1), sc, NEG_BIG)
                p = jnp.exp(sc.astype(BF16))
                ds_b = (p * dp).astype(BF16)
                dv_scr[e] += _mm_tn(p, do)
                dk_scr[e] += _mm_tn(ds_b, q)
                dq_scr[e, rows, :] += _mm(ds_b, ka_ref[e])

        def loop_body(qblk, carry):
            q_step(qblk, False)
            return carry

        ends = [last_ref[2 * j + e, ki] + 1 for e in range(2)]
        both = jnp.minimum(ends[0], ends[1])
        q_step(ki, True)
        lax.fori_loop(ki + 1, both, loop_body, 0)
        for e in range(2):
            def alone(qblk, carry, e=e):
                q_step(qblk, False, (e,))
                return carry

            lax.fori_loop(both, ends[e], alone, 0)

        lane = _iota((t, LANES), 1)
        dk_ref[...] = _pack_pair(dk_scr[0], dk_scr[1], lane).astype(BF16)
        dv_ref[...] = _pack_pair(dv_scr[0], dv_scr[1], lane).astype(BF16)
        rows = pl.ds(pl.multiple_of(ki * t, t), t)
        dc_ref[rows, :] -= (jnp.where(lane == N_HEADS + 2 * j, dk_scr[0][:, AUG_B:AUG_B + 1], 0.0)
                            + jnp.where(lane == N_HEADS + 2 * j + 1, dk_scr[1][:, AUG_B:AUG_B + 1], 0.0))

        @pl.when(ki == nq - 1)
        def _():
            for blk in range(nq):
                rws = pl.ds(blk * t, t)
                d0 = dq_scr[0, rws, :]
                d1 = dq_scr[1, rws, :]
                dq_ref[rws, :] = (_pack_pair(d0, d1, lane) * ATT_SCALE).astype(BF16)
                dc_ref[rws, :] += (jnp.where(lane == N_HEADS + 2 * j, d0[:, AUG_A:AUG_A + 1], 0.0)
                                   + jnp.where(lane == N_HEADS + 2 * j + 1, d1[:, AUG_A:AUG_A + 1], 0.0))

    full = pl.BlockSpec((2, s, LANES), lambda j, ki, f: (j, 0, 0))
    blk = pl.BlockSpec((2, t, LANES), lambda j, ki, f: (j, ki, 0))
    pair = pl.BlockSpec((t, LANES), lambda j, ki, f: (ki, j))
    wide = jax.ShapeDtypeStruct((s, ATT_WIDTH), BF16)
    grid_spec = pltpu.PrefetchScalarGridSpec(
        num_scalar_prefetch=1, grid=(N_PAIRS, nq),
        in_specs=[full, full, blk, blk],
        out_specs=[pl.BlockSpec((s, LANES), lambda j, ki, f: (0, j)), pair, pair,
                   pl.BlockSpec((s, LANES), lambda j, ki, f: (0, 0))],
        scratch_shapes=[pltpu.VMEM((2, s, LANES), F32), pltpu.VMEM((2, t, LANES), F32),
                        pltpu.VMEM((2, t, LANES), F32)])
    return pl.pallas_call(
        body, name="attention_bwd", grid_spec=grid_spec,
        out_shape=(wide, wide, wide, jax.ShapeDtypeStruct((s, LANES), F32)),
        compiler_params=_params(("arbitrary", "arbitrary")),
    )(last_q, qb, dob, ka, va)


def _dsilu(z, sg):
    return sg * (1.0 + z * (1.0 - sg))


def post_mix(x, y, zs, o, za, p, tgt, ssd_g, att_g_lane, ple_g, fin_g, w_out, w_gate, w_proj):
    s = x.shape[0]
    tm = _blk(s, 256)
    half = SSD_WIDTH // N_GROUPS

    def rms_bwd(dy, yn, r):
        return r * (dy - yn * jnp.mean(dy * yn, axis=-1, keepdims=True))

    def colsum(a):
        return jnp.sum(a, axis=0, keepdims=True)

    def body(x_ref, y_ref, zs_ref, o_ref, za_ref, p_ref, t_ref, sg_ref, ag_ref, pg_ref, fg_ref,
             wo_ref, wg_ref, wp_ref,
             dh1_ref, dy_ref, dzs_ref, dob_ref, dza_ref, ycat_ref, dh1b_ref, n2b_ref, dglb_ref, dppb_ref, pb_ref,
             loss_ref, dfin_ref, dple_ref, dssd_ref, datt_ref):
        @pl.when(pl.program_id(0) == 0)
        def _():
            for r in (loss_ref, dfin_ref, dple_ref, dssd_ref, datt_ref):
                r[...] = jnp.zeros_like(r)

        lane = _iota((tm, LANES), 1)
        lo = lane < HEAD_DIM
        zs = zs_ref[...]
        sz = _sigmoid(zs)
        yv = y_ref[...]
        ys = yv * (zs * sz)
        yn, rg = [], []
        for g in range(N_GROUPS):
            seg = ys[:, half * g:half * (g + 1)]
            r = lax.rsqrt(jnp.mean(seg * seg, axis=-1, keepdims=True) + EPS)
            yn.append(seg * r)
            rg.append(r)
            ycat_ref[:, half * g:half * (g + 1)] = (yn[g] * sg_ref[:, half * g:half * (g + 1)]).astype(BF16)
        za = za_ref[...]
        sza = _sigmoid(za)
        silu_za = za * sza
        on, ra = [], []
        for jb in range(N_PAIRS):
            blk = o_ref[:, LANES * jb:LANES * (jb + 1)]
            sq = blk * blk
            ms0 = jnp.sum(jnp.where(lo, sq, 0.0), axis=1, keepdims=True) * (1.0 / HEAD_DIM)
            ms1 = jnp.sum(jnp.where(lo, 0.0, sq), axis=1, keepdims=True) * (1.0 / HEAD_DIM)
            r = jnp.where(lo, lax.rsqrt(ms0 + EPS), lax.rsqrt(ms1 + EPS))
            on.append(blk * r)
            ra.append(r)
            an = on[jb] * ag_ref[:, LANES * jb:LANES * (jb + 1)]
            ycat_ref[:, SSD_WIDTH + LANES * jb:SSD_WIDTH + LANES * (jb + 1)] = (
                an * silu_za[:, LANES * jb:LANES * (jb + 1)]).astype(BF16)
        h1 = x_ref[...] + _mm(ycat_ref[...], wo_ref[...])
        r2 = lax.rsqrt(jnp.mean(h1 * h1, axis=-1, keepdims=True) + EPS)
        n2h = h1 * r2
        n2_b = (n2h * pg_ref[...]).astype(BF16)
        gate = _sigmoid(_mm(n2_b, wg_ref[...]))
        p_b = p_ref[...].astype(BF16)
        pp = _mm(p_b, wp_ref[...])
        h2 = h1 + gate * pp
        r3 = lax.rsqrt(jnp.mean(h2 * h2, axis=-1, keepdims=True) + EPS)
        n3 = h2 * r3
        diff = n3 * fg_ref[...] - t_ref[...]
        sq = colsum(diff * diff)
        part = sq[:, 0:LANES]
        for jb in range(1, D_MODEL // LANES):
            part = part + sq[:, LANES * jb:LANES * (jb + 1)]
        loss_ref[...] += part * (0.5 / D_MODEL)
        dout = diff * (1.0 / D_MODEL)
        dfin_ref[...] += colsum(dout * n3)
        dh2 = rms_bwd(dout * fg_ref[...], n3, r3)
        dgl = dh2 * pp * gate * (1.0 - gate)
        dgl_b = dgl.astype(BF16)
        dn2 = _mm_nt(dgl_b, wg_ref[...])
        dple_ref[...] += colsum(dn2 * n2h)
        dh1 = dh2 + rms_bwd(dn2 * pg_ref[...], n2h, r2)
        dh1_b = dh1.astype(BF16)
        dycat = _mm_nt(dh1_b, wo_ref[...])
        dh1_ref[...] = dh1
        dh1b_ref[...] = dh1_b
        n2b_ref[...] = n2_b
        dglb_ref[...] = dgl_b
        dppb_ref[...] = (dh2 * gate).astype(BF16)
        pb_ref[...] = p_b
        for g in range(N_GROUPS):
            cols = slice(half * g, half * (g + 1))
            dys_g = dycat[:, cols]
            dssd_ref[:, cols] += colsum(dys_g * yn[g])
            dys = rms_bwd(dys_g * sg_ref[:, cols], yn[g], rg[g])
            dy_ref[:, cols] = dys * (zs[:, cols] * sz[:, cols])
            dzs_ref[:, cols] = (dys * yv[:, cols] * _dsilu(zs[:, cols], sz[:, cols])).astype(BF16)
        for jb in range(N_PAIRS):
            cols = slice(LANES * jb, LANES * (jb + 1))
            dya = dycat[:, SSD_WIDTH + LANES * jb:SSD_WIDTH + LANES * (jb + 1)]
            ag = ag_ref[:, cols]
            dan = dya * silu_za[:, cols]
            dza_ref[:, cols] = (dya * (on[jb] * ag) * _dsilu(za[:, cols], sza[:, cols])).astype(BF16)
            datt_ref[:, cols] += colsum(dan * on[jb])
            don = dan * ag
            q = don * on[jb]
            m0 = jnp.sum(jnp.where(lo, q, 0.0), axis=1, keepdims=True) * (1.0 / HEAD_DIM)
            m1 = jnp.sum(jnp.where(lo, 0.0, q), axis=1, keepdims=True) * (1.0 / HEAD_DIM)
            do2 = ra[jb] * (don - on[jb] * jnp.where(lo, m0, m1))
            prod = do2 * o_ref[:, cols]
            for e in range(2):
                delta = jnp.sum(jnp.where(lo, prod, 0.0) if e == 0 else jnp.where(lo, 0.0, prod),
                                axis=1, keepdims=True)
                base = jnp.where(lo, do2 if e == 0 else pltpu.roll(do2, HEAD_DIM, 1), 0.0)
                dob_ref[2 * jb + e] = (base - _aug(lane, AUG_A, _split3(delta))).astype(BF16)

    def rows(n, dtype=None):
        return pl.BlockSpec((tm, n), lambda i: (i, 0))

    def out(n, dtype):
        return jax.ShapeDtypeStruct((s, n), dtype)

    vec = _const_spec((1, D_MODEL))
    vshape = jax.ShapeDtypeStruct((1, D_MODEL), F32)
    return pl.pallas_call(
        body, name="post_mix",
        out_shape=(out(D_MODEL, F32), out(SSD_WIDTH, F32), out(SSD_WIDTH, BF16),
                   jax.ShapeDtypeStruct((N_HEADS, s, LANES), BF16),
                   out(ATT_WIDTH, BF16), out(D_INNER, BF16), out(D_MODEL, BF16), out(D_MODEL, BF16),
                   out(D_MODEL, BF16), out(D_MODEL, BF16), out(PLE_DIM, BF16),
                   jax.ShapeDtypeStruct((1, LANES), F32), vshape, vshape, vshape, vshape),
        grid=(s // tm,),
        in_specs=[rows(D_MODEL), rows(SSD_WIDTH), rows(SSD_WIDTH), rows(ATT_WIDTH), rows(ATT_WIDTH),
                  rows(PLE_DIM), rows(D_MODEL), vec, vec, vec, vec,
                  _const_spec((D_INNER, D_MODEL)), _const_spec((D_MODEL, D_MODEL)), _const_spec((PLE_DIM, D_MODEL))],
        out_specs=(rows(D_MODEL), rows(SSD_WIDTH), rows(SSD_WIDTH),
                   pl.BlockSpec((N_HEADS, tm, LANES), lambda i: (0, i, 0)), rows(ATT_WIDTH),
                   rows(D_INNER), rows(D_MODEL), rows(D_MODEL), rows(D_MODEL), rows(D_MODEL), rows(PLE_DIM),
                   _const_spec((1, LANES)), vec, vec, vec, vec),
        compiler_params=_params(("arbitrary",)),
    )(x, y, zs, o, za, p, tgt, ssd_g, att_g_lane, ple_g, fin_g, w_out, w_gate, w_proj)


def in_proj_bwd(dsegs, wsegs, x, g, dh1, pres):
    s = x.shape[0]
    tm = _blk(s, 256)
    nseg = len(dsegs)
    nbig = len(pres)
    nsteps = s // tm

    def body(*refs):
        d_refs = refs[:nseg]
        w_refs = refs[nseg:2 * nseg]
        x_ref, g_ref, dh1_ref = refs[2 * nseg:2 * nseg + 3]
        rest = refs[2 * nseg + 3:]
        pre_refs, (dx_ref, dg_ref), part_refs = rest[:nbig], rest[nbig:nbig + 2], rest[nbig + 2:2 * nbig + 2]
        ssem, rsem, lsem = rest[2 * nbig + 2:]

        @pl.when(pl.program_id(0) == 0)
        def _():
            dg_ref[...] = jnp.zeros_like(dg_ref)
            for cp in scatter_copies(pre_refs, part_refs, ssem, rsem, lsem):
                cp.start()

        @pl.when(pl.program_id(0) == nsteps - 1)
        def _():
            for cp in scatter_copies(pre_refs, part_refs, ssem, rsem, lsem):
                cp.wait()

        du = _mm_nt(d_refs[0][...], w_refs[0][...])
        for k in range(1, nseg):
            du = du + _mm_nt(d_refs[k][...], w_refs[k][...])
        xv = x_ref[...]
        r = lax.rsqrt(jnp.mean(xv * xv, axis=-1, keepdims=True) + EPS)
        xh = xv * r
        dg_ref[...] += jnp.sum(du * xh, axis=0, keepdims=True)
        dxh = du * g_ref[...]
        dx_ref[...] = r * (dxh - xh * jnp.mean(dxh * xh, axis=-1, keepdims=True)) + dh1_ref[...]

    rows = lambda n: pl.BlockSpec((tm, n), lambda i: (i, 0))
    return pl.pallas_call(
        body, name="in_proj_bwd",
        out_shape=tuple([jax.ShapeDtypeStruct((s, D_MODEL), F32), jax.ShapeDtypeStruct((1, D_MODEL), F32)]
                        + [jax.ShapeDtypeStruct(a.shape, a.dtype) for a in pres]),
        grid=(nsteps,),
        in_specs=([rows(d.shape[1]) for d in dsegs] + [_const_spec(w.shape) for w in wsegs]
                  + [rows(D_MODEL), _const_spec((1, D_MODEL)), rows(D_MODEL)] + [ANY] * nbig),
        out_specs=tuple([rows(D_MODEL), _const_spec((1, D_MODEL))] + [ANY] * nbig),
        scratch_shapes=_sems(3 * nbig) + [pltpu.SemaphoreType.DMA((nbig,))],
        compiler_params=_params(("arbitrary",)),
    )(*dsegs, *wsegs, x, g, dh1, *pres)


SMALL_NAMES = ("norm_g", "conv_b", "dt_bias", "a_log", "d_skip", "ssd_norm_g", "fg_bias", "att_norm_g",
               "ple_norm_g", "final_norm_g")
SMALL_SIZES = (1024, 1536, 16, 16, 16, 1024, 16, 64, 1024, 1024)
CONV_W_SIZE = CONV_WIDTH * CONV_CH


def _pack_small(vals):
    flat = jnp.concatenate([v.reshape(-1).astype(F32) for v in vals])
    flat = jnp.pad(flat, (0, SMALL_ROWS * LANES - flat.shape[0]))
    return flat.reshape(SMALL_ROWS, LANES)


def _unpack_small(pack, shapes):
    flat = pack.reshape(-1)
    out, off = [], 0
    for n, shp in zip(SMALL_SIZES, shapes):
        out.append(flat[off:off + n].reshape(shp))
        off += n
    return out


def _row128(v16, offset=0):
    return jnp.pad(v16.reshape(1, N_HEADS).astype(F32), ((0, 0), (offset, LANES - N_HEADS - offset)))


def local_step(prereduce, x, p, tgt, w_in, w_out, w_gate, w_proj, conv_w, norm_g, conv_b, dt_bias, a_log, d_skip,
               ssd_norm_g, fg_bias, att_norm_g, ple_norm_g, final_norm_g):
    widths = (SSD_WIDTH, CONV_CH, N_HEADS, ATT_WIDTH, ATT_WIDTH, ATT_WIDTH, ATT_WIDTH)
    c0, c1, c2, c3, c4, c5, c6, c7 = [sum(widths[:i]) for i in range(len(widths) + 1)]
    w_zs, w_xbc, w_dt = w_in[:, c0:c1], w_in[:, c1:c2], w_in[:, c2:c3]
    w_za, w_q, w_k, w_v, w_f = w_in[:, c3:c4], w_in[:, c4:c5], w_in[:, c5:c6], w_in[:, c6:c7], w_in[:, c7:]
    w_small = jnp.concatenate([w_dt, w_f, jnp.zeros((D_MODEL, LANES - 2 * N_HEADS), BF16)], axis=1)

    dtb_row = _row128(dt_bias)
    a_row = _row128(-jnp.exp(a_log.astype(F32)))
    fgb_row = _row128(fg_bias, N_HEADS)
    dskip_lane = jnp.repeat(d_skip.astype(F32), HEAD_DIM).reshape(1, SSD_WIDTH)
    att_g_lane = jnp.tile(att_norm_g.astype(F32), N_HEADS).reshape(1, ATT_WIDTH)
    row = lambda v: v.reshape(1, -1).astype(F32)

    u = rms_prenorm(x, row(norm_g))
    zs = matmul_rows(u, w_zs, F32, "proj_z_ssd")
    xbc = matmul_rows(u, w_xbc, F32, "proj_xbc")
    za = matmul_rows(u, w_za, F32, "proj_z_att")
    small = matmul_rows(u, w_small, F32, "proj_small")
    cum = forget_cumsum(small, fgb_row)
    qa, ka, va, norms = proj_qkv_heads(u, w_q, w_k, w_v, cum)
    first, last_q = live_blocks(norms, cum, _blk(x.shape[0], ATT_BLOCK))
    pre, xc = conv_fwd(xbc, conv_w, row(conv_b))
    y, states = ssd_fwd(xc, small, dtb_row, a_row, dskip_lane)
    o, qb = attention_fwd(first, qa, ka, va)
    (dh1, dy, dzs, dob, dza, ycat, dh1_b, n2_b, dgl_b, dpp_b, p_b,
     loss_l, dfin, dple, dssd_g, datt_lane) = post_mix(
        x, y, zs, o, za, p, tgt, row(ssd_norm_g), att_g_lane, row(ple_norm_g), row(final_norm_g),
        w_out, w_gate, w_proj)
    dq, dk, dv, dc = attention_bwd(last_q, qb, ka, va, dob)
    dxc, ddt_raw, da, ddtb, ddsk_lane = ssd_bwd(xc, small, states, dy, dtb_row, a_row, dskip_lane)
    dsmall, dfgb = forget_bwd(dc, small, ddt_raw, fgb_row)
    dxbc, dconv_w8, dconv_b = conv_bwd(xbc, pre, dxc, conv_w)
    dsegs = [dzs, dxbc, dza, dq, dk, dv, dsmall]
    wsegs = [w_zs, w_xbc, w_za, w_q, w_k, w_v, w_small]
    dws = [matmul_tn(u, d, "dw_in_%d" % i) for i, d in enumerate(dsegs)]
    dw_in = jnp.concatenate([dws[0], dws[1], dws[6][:, :N_HEADS], dws[2], dws[3], dws[4], dws[5],
                             dws[6][:, N_HEADS:2 * N_HEADS]], axis=1)
    dw_out = matmul_tn(ycat, dh1_b, "dw_out")
    dw_gate = matmul_tn(n2_b, dgl_b, "dw_gate")
    dw_proj = matmul_tn(p_b, dpp_b, "dw_proj")
    dx, dnorm_g, *parts = in_proj_bwd(dsegs, wsegs, x, row(norm_g), dh1, prereduce(dw_in, dw_out, dw_gate, dw_proj))
    small_grads = [
        dnorm_g, dconv_b, ddtb[0, :N_HEADS], (da * a_row)[0, :N_HEADS],
        ddsk_lane.reshape(N_HEADS, HEAD_DIM).sum(axis=1), dssd_g, dfgb[0, N_HEADS:2 * N_HEADS],
        datt_lane.reshape(N_HEADS, HEAD_DIM).sum(axis=0), dple, dfin]
    loss = jnp.sum(loss_l)
    return loss, dx, parts, dconv_w8[:CONV_WIDTH], small_grads


def kernel(x, p, norm_g, w_in, conv_w, conv_b, dt_bias, a_log, d_skip, ssd_norm_g, fg_bias, att_norm_g, w_out, ple_norm_g, w_ple_gate, w_ple_proj, final_norm_g, loss_target, m_norm_g, m_w_in, m_conv_w, m_conv_b, m_dt_bias, m_a_log, m_d_skip, m_ssd_norm_g, m_fg_bias, m_att_norm_g, m_w_out, m_ple_norm_g, m_w_ple_gate, m_w_ple_proj, m_final_norm_g, v_norm_g, v_w_in, v_conv_w, v_conv_b, v_dt_bias, v_a_log, v_d_skip, v_ssd_norm_g, v_fg_bias, v_att_norm_g, v_w_out, v_ple_norm_g, v_w_ple_gate, v_w_ple_proj, v_final_norm_g):
    chip = 2 * lax.axis_index("x") + lax.axis_index("y")
    core = lax.axis_index("c")

    big_w = [w_in[0], w_out[0], w_ple_gate[0], w_ple_proj[0]]
    own = [a.astype(BF16) for a in big_w] + [conv_w[0]]
    gathered = gather_weights(own[:4], own[4])

    def joined(k, axis):
        return jnp.concatenate([jnp.where(chip == j, own[k], gathered[k][j]) for j in range(N_CHIPS)], axis=axis)

    w_in_f, w_out_f, w_gate_f, w_proj_f, conv_w_f = joined(0, 1), joined(1, 0), joined(2, 0), joined(3, 1), joined(4, 1)

    core1 = core.reshape(1).astype(jnp.int32)

    def prereduce(dw_in, dw_out, dw_gate, dw_proj):
        n_in, n_proj = w_in.shape[2], w_ple_proj.shape[2]
        gs = [jnp.stack([dw_in[:, n_in * j:n_in * (j + 1)] for j in range(N_CHIPS)]),
              dw_out.reshape(N_CHIPS, w_out.shape[1], D_MODEL), dw_gate.reshape(N_CHIPS, w_ple_gate.shape[1], D_MODEL),
              jnp.stack([dw_proj[:, n_proj * j:n_proj * (j + 1)] for j in range(N_CHIPS)])]
        return add_halves(core1, gs, halves_to_sibling(gs))

    smalls_w = [norm_g, conv_b, dt_bias, a_log, d_skip, ssd_norm_g, fg_bias, att_norm_g, ple_norm_g, final_norm_g]
    loss_l, dx, parts, dconv_w, small_grads = local_step(
        prereduce, x[0], p[0, 0], loss_target[0], w_in_f, w_out_f, w_gate_f, w_proj_f, conv_w_f,
        *[a.reshape(-1) for a in smalls_w])
    loss = lax.psum(loss_l, ("x", "y", "c"))
    smalls = gather_small(_pack_small(list(small_grads) + [dconv_w]))
    mine = sum_parts(parts)

    g_big, d_big, m_big, v_big = adamw_big(
        core1, mine, swap_halves(mine), big_w, [m_w_in[0], m_w_out[0], m_w_ple_gate[0], m_w_ple_proj[0]],
        [v_w_in[0], v_w_out[0], v_w_ple_gate[0], v_w_ple_proj[0]])
    smalls_m = [m_norm_g, m_conv_b, m_dt_bias, m_a_log, m_d_skip, m_ssd_norm_g, m_fg_bias, m_att_norm_g,
                m_ple_norm_g, m_final_norm_g]
    smalls_v = [v_norm_g, v_conv_b, v_dt_bias, v_a_log, v_d_skip, v_ssd_norm_g, v_fg_bias, v_att_norm_g,
                v_ple_norm_g, v_final_norm_g]
    g_sm, d_sm, m_sm, v_sm = adamw_small(smalls, _pack_small(smalls_w), _pack_small(smalls_m), _pack_small(smalls_v))
    n_small = sum(SMALL_SIZES)
    g_conv_full = g_sm.reshape(-1)[n_small:n_small + CONV_W_SIZE].reshape(CONV_WIDTH, CONV_CH)
    n_conv = conv_w.shape[2]
    g_conv = lax.dynamic_slice_in_dim(g_conv_full, chip * n_conv, n_conv, axis=1)
    d_conv, m_conv, v_conv = adamw_whole(g_conv, conv_w[0], m_conv_w[0], v_conv_w[0], "adamw_conv")

    shapes = [a.shape for a in smalls_w]
    outs = []
    for big, conv, sm in ((g_big, g_conv, g_sm), (d_big, d_conv, d_sm), (m_big, m_conv, m_sm), (v_big, v_conv, v_sm)):
        b_in, b_out, b_gate, b_proj = [a[None] for a in big]
        s_norm, s_convb, s_dtb, s_alog, s_dsk, s_ssdg, s_fgb, s_attg, s_pleg, s_fin = _unpack_small(sm, shapes)
        outs.extend([s_norm, b_in, conv[None], s_convb, s_dtb, s_alog, s_dsk, s_ssdg, s_fgb, s_attg, b_out, s_pleg,
                     b_gate, b_proj, s_fin])
    return (loss, dx[None], *outs)
```

```python
import functools

import jax
import jax.numpy as jnp
from jax import lax
from jax.experimental import pallas as pl
from jax.experimental.pallas import tpu as pltpu

F32 = jnp.float32
BF16 = jnp.bfloat16

D_MODEL = 1024
SSD_WIDTH = 1024
ATT_WIDTH = 1024
N_HEADS = 16
HEAD_DIM = 64
N_GROUPS = 2
D_STATE = 128
CONV_CH = 1536
CONV_WIDTH = 4
CHUNK = 128
PLE_DIM = 256
D_INNER = 2048
EPS = 1e-6
IN_COLS = 6688
N_CHIPS = 4
N_DEV = 8
LANES = 128
N_PAIRS = 8

ADAM_LR = 0.001
ADAM_B1 = 0.9
ADAM_B2 = 0.999
ADAM_EPS = 1e-08
ADAM_WD = 0.01
ADAM_STEP = 10

SMALL_ROWS = 96

NEG_BIG = -1e30
VMEM_LIMIT = 56 * 1024 * 1024

MESH = pl.DeviceIdType.MESH
ANY = pl.BlockSpec(memory_space=pl.ANY)


def _mm(a, b):
    return jnp.dot(a, b, preferred_element_type=F32)


def _mm_nt(a, b):
    return lax.dot_general(a, b, (((1,), (1,)), ((), ())), preferred_element_type=F32)


def _mm_tn(a, b):
    return lax.dot_general(a, b, (((0,), (0,)), ((), ())), preferred_element_type=F32)


def _mm_exact(a, b):
    return jnp.dot(a, b, preferred_element_type=F32, precision=lax.Precision.HIGHEST)


def _softplus(x):
    return jnp.maximum(x, 0.0) + jnp.log1p(jnp.exp(-jnp.abs(x)))


def _sigmoid(x):
    return jax.nn.sigmoid(x)


def _iota(shape, dim):
    return lax.broadcasted_iota(jnp.int32, shape, dim)


def _params(sem=None):
    return pltpu.CompilerParams(dimension_semantics=sem, vmem_limit_bytes=VMEM_LIMIT)


def _blk(n, pref):
    return min(n, pref)


def _const_spec(shape):
    nd = len(shape)
    return pl.BlockSpec(shape, lambda *_: (0,) * nd)


def _chip_peers():
    x, y, c = lax.axis_index("x"), lax.axis_index("y"), lax.axis_index("c")
    return x, y, c, [(1 - x, y, c), (x, 1 - y, c), (1 - x, 1 - y, c)]


def _half(rows, c):
    h = rows // 2
    return pl.ds(pl.multiple_of(c * h, 8), h)


def _sems(n):
    return [pltpu.SemaphoreType.DMA((n,)), pltpu.SemaphoreType.DMA((n,))]


def gather_weights(shards, conv_s):
    n = len(shards)

    def body(*refs):
        ins, conv_in = refs[:n], refs[n]
        outs, conv_out = refs[n + 1:2 * n + 1], refs[2 * n + 1]
        ssem1, rsem1, ssem2, rsem2, c_ssem, c_rsem = refs[2 * n + 2:]
        x, y, c, peers = _chip_peers()
        me = 2 * x + y
        sibling = (x, y, 1 - c)
        first, small = [], []
        for k, peer in enumerate(peers):
            for i in range(n):
                h = _half(ins[i].shape[0], c)
                first.append(pltpu.make_async_remote_copy(
                    src_ref=ins[i].at[h], dst_ref=outs[i].at[me, h], send_sem=ssem1.at[n * k + i],
                    recv_sem=rsem1.at[n * k + i], device_id=peer, device_id_type=MESH))
            small.append(pltpu.make_async_remote_copy(
                src_ref=conv_in, dst_ref=conv_out.at[me], send_sem=c_ssem.at[k], recv_sem=c_rsem.at[k],
                device_id=peer, device_id_type=MESH))
        for cp in first + small:
            cp.start()
        passed = []
        for k, peer in enumerate(peers):
            chip = 2 * peer[0] + peer[1]
            for i in range(n):
                h = _half(ins[i].shape[0], c)
                first[n * k + i].wait_recv()
                fwd = pltpu.make_async_remote_copy(
                    src_ref=outs[i].at[chip, h], dst_ref=outs[i].at[chip, h], send_sem=ssem2.at[n * k + i],
                    recv_sem=rsem2.at[n * k + i], device_id=sibling, device_id_type=MESH)
                fwd.start()
                passed.append(fwd)
        for cp in passed:
            cp.wait_recv()
        for cp in first + passed:
            cp.wait_send()
        for cp in small:
            cp.wait()

    return pl.pallas_call(
        body, name="gather_weights",
        out_shape=tuple(jax.ShapeDtypeStruct((N_CHIPS,) + a.shape, a.dtype) for a in list(shards) + [conv_s]),
        in_specs=[ANY] * (n + 1), out_specs=(ANY,) * (n + 1),
        scratch_shapes=_sems(3 * n) + _sems(3 * n) + _sems(3),
    )(*shards, conv_s)


def halves_to_sibling(gs):
    n = len(gs)

    def body(*refs):
        ins, outs = refs[:n], refs[n:2 * n]
        ssem, rsem = refs[2 * n:]
        x, y, c = lax.axis_index("x"), lax.axis_index("y"), lax.axis_index("c")
        copies = []
        for i in range(n):
            for j in range(N_CHIPS):
                copies.append(pltpu.make_async_remote_copy(
                    src_ref=ins[i].at[j, _half(ins[i].shape[1], 1 - c)], dst_ref=outs[i].at[j],
                    send_sem=ssem.at[N_CHIPS * i + j], recv_sem=rsem.at[N_CHIPS * i + j],
                    device_id=(x, y, 1 - c), device_id_type=MESH))
        for cp in copies:
            cp.start()
        for cp in copies:
            cp.wait()

    return pl.pallas_call(
        body, name="halves_to_sibling",
        out_shape=tuple(jax.ShapeDtypeStruct((N_CHIPS, g.shape[1] // 2, g.shape[2]), F32) for g in gs),
        in_specs=[ANY] * n, out_specs=(ANY,) * n, scratch_shapes=_sems(N_CHIPS * n),
    )(*gs)


RED_GRID = 8


def add_halves(core, gs, rbs):
    n = len(gs)

    def body(c_ref, *refs):
        for i in range(n):
            refs[2 * n + i][...] = (refs[i][...] + refs[n + i][...]).astype(BF16)

    def blk(g):
        return (1, g.shape[1] // 2 // RED_GRID, g.shape[2])

    grid_spec = pltpu.PrefetchScalarGridSpec(
        num_scalar_prefetch=1, grid=(N_CHIPS, RED_GRID),
        in_specs=([pl.BlockSpec(blk(g), lambda j, b, c_ref: (j, c_ref[0] * RED_GRID + b, 0)) for g in gs]
                  + [pl.BlockSpec(blk(g), lambda j, b, c_ref: (j, b, 0)) for g in gs]),
        out_specs=[pl.BlockSpec(blk(g), lambda j, b, c_ref: (j, b, 0)) for g in gs])
    return pl.pallas_call(
        body, name="add_halves", grid_spec=grid_spec,
        out_shape=tuple(jax.ShapeDtypeStruct(r.shape, BF16) for r in rbs),
        compiler_params=_params(("parallel", "parallel")),
    )(core, *gs, *rbs)


def scatter_copies(ins, outs, ssem, rsem, lsem):
    n = len(ins)
    x, y, _, peers = _chip_peers()
    me = 2 * x + y
    copies = [pltpu.make_async_copy(ins[i].at[me], outs[i].at[me], lsem.at[i]) for i in range(n)]
    for k, peer in enumerate(peers):
        dst_chip = 2 * peer[0] + peer[1]
        for i in range(n):
            copies.append(pltpu.make_async_remote_copy(
                src_ref=ins[i].at[dst_chip], dst_ref=outs[i].at[me], send_sem=ssem.at[n * k + i],
                recv_sem=rsem.at[n * k + i], device_id=peer, device_id_type=MESH))
    return copies


def gather_small(small):
    def body(s_ref, smalls_ref, ssem, rsem, lsem):
        x, y, c = lax.axis_index("x"), lax.axis_index("y"), lax.axis_index("c")
        dev = 4 * x + 2 * y + c
        copies = [pltpu.make_async_copy(s_ref, smalls_ref.at[dev], lsem)]
        for k in range(1, N_DEV):
            fx, fy, fc = (k >> 2) & 1, (k >> 1) & 1, k & 1
            peer = ((1 - x) if fx else x, (1 - y) if fy else y, (1 - c) if fc else c)
            copies.append(pltpu.make_async_remote_copy(
                src_ref=s_ref, dst_ref=smalls_ref.at[dev], send_sem=ssem.at[k - 1], recv_sem=rsem.at[k - 1],
                device_id=peer, device_id_type=MESH))
        for cp in copies:
            cp.start()
        for cp in copies:
            cp.wait()

    return pl.pallas_call(
        body, name="gather_small",
        out_shape=jax.ShapeDtypeStruct((N_DEV,) + small.shape, F32),
        in_specs=[ANY], out_specs=ANY,
        scratch_shapes=_sems(N_DEV - 1) + [pltpu.SemaphoreType.DMA],
    )(small)


def sum_parts(parts):
    n = len(parts)

    def body(*refs):
        for i in range(n):
            p_ref = refs[i]
            refs[n + i][...] = ((p_ref[0].astype(F32) + p_ref[1].astype(F32)) + p_ref[2].astype(F32)
                                ) + p_ref[3].astype(F32)

    def rows(p):
        return p.shape[1] // RED_GRID

    return pl.pallas_call(
        body, name="sum_parts",
        out_shape=tuple(jax.ShapeDtypeStruct(p.shape[1:], F32) for p in parts),
        grid=(RED_GRID,),
        in_specs=[pl.BlockSpec((N_CHIPS, rows(p), p.shape[2]), lambda b: (0, b, 0)) for p in parts],
        out_specs=tuple(pl.BlockSpec((rows(p), p.shape[2]), lambda b: (b, 0)) for p in parts),
        compiler_params=_params(("parallel",)),
    )(*parts)


def swap_halves(reds):
    n = len(reds)

    def body(*refs):
        ins, outs = refs[:n], refs[n:2 * n]
        ssem, rsem = refs[2 * n:]
        x, y, c = lax.axis_index("x"), lax.axis_index("y"), lax.axis_index("c")
        copies = [pltpu.make_async_remote_copy(
            src_ref=ins[i], dst_ref=outs[i], send_sem=ssem.at[i], recv_sem=rsem.at[i],
            device_id=(x, y, 1 - c), device_id_type=MESH) for i in range(n)]
        for cp in copies:
            cp.start()
        for cp in copies:
            cp.wait()

    return pl.pallas_call(
        body, name="swap_halves",
        out_shape=tuple(jax.ShapeDtypeStruct(r.shape, F32) for r in reds),
        in_specs=[ANY] * n, out_specs=(ANY,) * n, scratch_shapes=_sems(n),
    )(*reds)


def _adamw(w, g, m, v):
    m = ADAM_B1 * m + (1.0 - ADAM_B1) * g
    v = ADAM_B2 * v + (1.0 - ADAM_B2) * (g * g)
    m_hat = m / (1.0 - ADAM_B1 ** ADAM_STEP)
    v_hat = v / (1.0 - ADAM_B2 ** ADAM_STEP)
    delta = -ADAM_LR * (m_hat / (jnp.sqrt(v_hat) + ADAM_EPS) + ADAM_WD * w)
    return delta, m, v


def adamw_big(core, mine, theirs, ws, ms, vs):
    n = len(ws)
    per_half = RED_GRID // 2

    def body(c_ref, *refs):
        own = (pl.program_id(0) // per_half) == c_ref[0]
        for i in range(n):
            g = jnp.where(own, refs[i][...], refs[n + i][...])
            d, mn, vn = _adamw(refs[2 * n + i][...], g, refs[3 * n + i][...], refs[4 * n + i][...])
            refs[5 * n + i][...] = g
            refs[6 * n + i][...] = d
            refs[7 * n + i][...] = mn
            refs[8 * n + i][...] = vn

    def blk(w):
        return (w.shape[0] // RED_GRID, w.shape[1])

    halves = [pl.BlockSpec(blk(w), lambda b, c_ref: (b % per_half, 0)) for w in ws]
    whole = [pl.BlockSpec(blk(w), lambda b, c_ref: (b, 0)) for w in ws]
    shapes = [jax.ShapeDtypeStruct(w.shape, F32) for w in ws]
    grid_spec = pltpu.PrefetchScalarGridSpec(
        num_scalar_prefetch=1, grid=(RED_GRID,), in_specs=halves * 2 + whole * 3, out_specs=whole * 4)
    outs = pl.pallas_call(
        body, name="adamw_big", out_shape=tuple(shapes * 4), grid_spec=grid_spec,
        compiler_params=_params(("parallel",)),
    )(core, *mine, *theirs, *ws, *ms, *vs)
    return outs[:n], outs[n:2 * n], outs[2 * n:3 * n], outs[3 * n:]


def adamw_whole(g, w, m, v, name):
    def body(g_ref, w_ref, m_ref, v_ref, d_out, m_out, v_out):
        d, mn, vn = _adamw(w_ref[...], g_ref[...], m_ref[...], v_ref[...])
        d_out[...] = d
        m_out[...] = mn
        v_out[...] = vn

    shp = jax.ShapeDtypeStruct(g.shape, F32)
    return pl.pallas_call(body, name=name, out_shape=(shp,) * 3)(g, w, m, v)


def adamw_small(smalls, w, m, v):
    def body(s_ref, w_ref, m_ref, v_ref, g_out, d_out, m_out, v_out):
        g = s_ref[0]
        for k in range(1, N_DEV):
            g = g + s_ref[k]
        d, mn, vn = _adamw(w_ref[...], g, m_ref[...], v_ref[...])
        g_out[...] = g
        d_out[...] = d
        m_out[...] = mn
        v_out[...] = vn

    shp = jax.ShapeDtypeStruct((SMALL_ROWS, LANES), F32)
    return pl.pallas_call(body, name="adamw_small", out_shape=(shp,) * 4)(smalls, w, m, v)


def rms_prenorm(x, g):
    s = x.shape[0]
    tm = _blk(s, 512)

    def body(x_ref, g_ref, u_ref):
        xv = x_ref[...]
        r = lax.rsqrt(jnp.mean(xv * xv, axis=-1, keepdims=True) + EPS)
        u_ref[...] = (xv * r * g_ref[...]).astype(BF16)

    return pl.pallas_call(
        body, name="rms_prenorm", out_shape=jax.ShapeDtypeStruct(x.shape, BF16), grid=(s // tm,),
        in_specs=[pl.BlockSpec((tm, D_MODEL), lambda i: (i, 0)), _const_spec((1, D_MODEL))],
        out_specs=pl.BlockSpec((tm, D_MODEL), lambda i: (i, 0)), compiler_params=_params(("parallel",)),
    )(x, g)


def matmul_rows(a, w, out_dtype, name):
    s, k = a.shape
    n = w.shape[1]
    tm = _blk(s, 512)

    def body(a_ref, w_ref, o_ref):
        o_ref[...] = _mm(a_ref[...], w_ref[...]).astype(out_dtype)

    return pl.pallas_call(
        body, name=name, out_shape=jax.ShapeDtypeStruct((s, n), out_dtype), grid=(s // tm,),
        in_specs=[pl.BlockSpec((tm, k), lambda i: (i, 0)), _const_spec((k, n))],
        out_specs=pl.BlockSpec((tm, n), lambda i: (i, 0)), compiler_params=_params(("parallel",)),
    )(a, w)


def matmul_tn(a, b, name):
    s, m = a.shape
    n = b.shape[1]
    tk = _blk(s, 2048)
    tn = _blk(n, 512)

    def body(a_ref, b_ref, o_ref):
        @pl.when(pl.program_id(1) == 0)
        def _():
            o_ref[...] = jnp.zeros_like(o_ref)

        o_ref[...] += _mm_tn(a_ref[...], b_ref[...])

    return pl.pallas_call(
        body, name=name, out_shape=jax.ShapeDtypeStruct((m, n), F32), grid=(n // tn, s // tk),
        in_specs=[pl.BlockSpec((tk, m), lambda j, i: (i, 0)), pl.BlockSpec((tk, tn), lambda j, i: (i, j))],
        out_specs=pl.BlockSpec((m, tn), lambda j, i: (0, j)),
        compiler_params=_params(("parallel", "arbitrary")),
    )(a, b)


def conv_fwd(xbc, w, b):
    s = xbc.shape[0]
    tm = _blk(s, 256)

    def body(x_ref, t_ref, w_ref, b_ref, pre_ref, act_ref):
        i = pl.program_id(0)
        cur = x_ref[...]
        tail = jnp.where(i > 0, t_ref[...], 0.0)
        wv = w_ref[...]
        acc = cur * wv[3:4, :] + b_ref[...]
        head = cur[0:8, :] * wv[3:4, :] + b_ref[...]
        row8 = _iota((8, CONV_CH), 0)
        for sh in range(1, CONV_WIDTH):
            wk = wv[3 - sh:4 - sh, :]
            acc = acc + pltpu.roll(cur, sh, 0) * wk
            first = jnp.where(row8 < sh, pltpu.roll(tail, sh, 0), pltpu.roll(cur[0:8, :], sh, 0))
            head = head + first * wk
        pre_ref[...] = acc
        act_ref[...] = acc * _sigmoid(acc)
        pre_ref[0:8, :] = head
        act_ref[0:8, :] = head * _sigmoid(head)

    shp = jax.ShapeDtypeStruct(xbc.shape, F32)
    rows = pl.BlockSpec((tm, CONV_CH), lambda i: (i, 0))
    return pl.pallas_call(
        body, name="conv_fwd", out_shape=(shp, shp), grid=(s // tm,),
        in_specs=[rows, pl.BlockSpec((8, CONV_CH), lambda i: (jnp.maximum(i * (tm // 8) - 1, 0), 0)),
                  _const_spec((CONV_WIDTH, CONV_CH)), _const_spec((1, CONV_CH))],
        out_specs=(rows, rows), compiler_params=_params(("parallel",)),
    )(xbc, xbc, w, b)


def conv_bwd(xbc, pre, dact, w):
    s = xbc.shape[0]
    tm = _blk(s, 256)
    nb = s // tm

    def dsilu(p):
        sg = _sigmoid(p)
        return sg * (1.0 + p * (1.0 - sg))

    def body(x_ref, xt_ref, p_ref, pn_ref, d_ref, dn_ref, w_ref, dx_ref, dw_ref, db_ref):
        i = pl.program_id(0)

        @pl.when(i == 0)
        def _():
            dw_ref[...] = jnp.zeros_like(dw_ref)
            db_ref[...] = jnp.zeros_like(db_ref)

        wv = w_ref[...]
        dpre = d_ref[...] * dsilu(p_ref[...])
        dnext = jnp.where(i < nb - 1, dn_ref[...] * dsilu(pn_ref[...]), 0.0)
        cur = x_ref[...]
        tail = jnp.where(i > 0, xt_ref[...], 0.0)
        row8 = _iota((8, CONV_CH), 0)
        dx = dpre * wv[3:4, :]
        last = dpre[tm - 8:tm, :] * wv[3:4, :]
        db_ref[...] += jnp.sum(dpre, axis=0, keepdims=True)
        dws = [jnp.sum(dpre * cur, axis=0, keepdims=True)]
        for sh in range(1, CONV_WIDTH):
            wk = wv[3 - sh:4 - sh, :]
            dx = dx + pltpu.roll(dpre, tm - sh, 0) * wk
            nxt = jnp.where(row8 >= 8 - sh, pltpu.roll(dnext, 8 - sh, 0), pltpu.roll(dpre[tm - 8:tm, :], 8 - sh, 0))
            last = last + nxt * wk
            xs = pltpu.roll(cur, sh, 0)
            first = jnp.where(row8 < sh, pltpu.roll(tail, sh, 0), xs[0:8, :])
            dws.append(jnp.sum(dpre * xs, axis=0, keepdims=True)
                       + jnp.sum(dpre[0:8, :] * (first - xs[0:8, :]), axis=0, keepdims=True))
        dx_ref[...] = dx.astype(BF16)
        dx_ref[tm - 8:tm, :] = last.astype(BF16)
        for sh in range(CONV_WIDTH):
            dw_ref[3 - sh:4 - sh, :] += dws[sh]

    rows = pl.BlockSpec((tm, CONV_CH), lambda i: (i, 0))
    prev8 = pl.BlockSpec((8, CONV_CH), lambda i: (jnp.maximum(i * (tm // 8) - 1, 0), 0))
    next8 = pl.BlockSpec((8, CONV_CH), lambda i: (jnp.minimum((i + 1) * (tm // 8), s // 8 - 1), 0))
    return pl.pallas_call(
        body, name="conv_bwd",
        out_shape=(jax.ShapeDtypeStruct(xbc.shape, BF16), jax.ShapeDtypeStruct((8, CONV_CH), F32),
                   jax.ShapeDtypeStruct((1, CONV_CH), F32)),
        grid=(nb,),
        in_specs=[rows, prev8, rows, next8, rows, next8, _const_spec((CONV_WIDTH, CONV_CH))],
        out_specs=(rows, _const_spec((8, CONV_CH)), _const_spec((1, CONV_CH))),
        compiler_params=_params(("arbitrary",)),
    )(xbc, xbc, pre, pre, dact, dact, w)


def _pair_lanes(mat, j, lane):
    return jnp.where(lane < HEAD_DIM, mat[:, 2 * j:2 * j + 1], mat[:, 2 * j + 1:2 * j + 2])


def _ssd_chunk_prelude(sm, dtb, a_row, lane, sub):
    raw = sm + dtb
    head_lane = lane < N_HEADS
    dt = jnp.where(head_lane, _softplus(raw), 0.0)
    sig = jnp.where(head_lane, _sigmoid(raw), 0.0)
    tri = (lane <= sub).astype(F32)
    acs = _mm_exact(tri, dt * a_row)
    return dt, sig, acs, acs.T


GROUP_WIDTH = SSD_WIDTH // N_GROUPS
HEADS_PER_GROUP = N_HEADS // N_GROUPS


def _expand_group(mat, g, lane):
    return jnp.concatenate([_pair_lanes(mat, j, lane) for j in range(4 * g, 4 * g + 4)], axis=1)


def _head_sums(q, g):
    row = _iota((GROUP_WIDTH, LANES), 0)
    seg = (_iota((GROUP_WIDTH, LANES), 1) == HEADS_PER_GROUP * g + (row >> 6)).astype(BF16)
    hi = q.astype(BF16)
    lo = (q - hi.astype(F32)).astype(BF16)
    return _mm(hi, seg) + _mm(lo, seg)


def _rows_from_lanes(row512):
    return jnp.broadcast_to(row512, (LANES, GROUP_WIDTH)).T


def ssd_fwd(xc, small, dtb_row, a_row, dskip_lane):
    s = xc.shape[0]
    nc = s // CHUNK

    def body(xc_ref, sm_ref, dtb_ref, a_ref, dsk_ref, y_ref, hs_ref, h_scr):
        c = pl.program_id(0)

        @pl.when(c == 0)
        def _():
            h_scr[...] = jnp.zeros_like(h_scr)

        lane = _iota((CHUNK, LANES), 1)
        sub = _iota((CHUNK, LANES), 0)
        causal = lane <= sub
        dt, _, acs, acs_t = _ssd_chunk_prelude(sm_ref[...], dtb_ref[...], a_ref[...], lane, sub)
        for g in range(N_GROUPS):
            cols = slice(GROUP_WIDTH * g, GROUP_WIDTH * (g + 1))
            b_off = SSD_WIDTH + D_STATE * g
            c_off = SSD_WIDTH + N_GROUPS * D_STATE + D_STATE * g
            b_b = xc_ref[:, b_off:b_off + D_STATE].astype(BF16)
            c_b = xc_ref[:, c_off:c_off + D_STATE].astype(BF16)
            cb = _mm_nt(c_b, b_b)
            x_g = xc_ref[:, cols]
            acs_g = _expand_group(acs, g, lane)
            xdt_g = x_g * _expand_group(dt, g, lane)
            xdt_b = xdt_g.astype(BF16)
            heads = range(HEADS_PER_GROUP * g, HEADS_PER_GROUP * (g + 1))
            m_b = [(cb * jnp.exp(jnp.where(causal, acs[:, h:h + 1] - acs_t[h:h + 1, :], NEG_BIG))).astype(BF16)
                   for h in heads]
            yd = [_mm(m_b[k], xdt_b[:, LANES * (k // 2):LANES * (k // 2 + 1)]) for k in range(HEADS_PER_GROUP)]
            yd_g = jnp.concatenate([jnp.where(lane < HEAD_DIM, yd[2 * k], yd[2 * k + 1]) for k in range(4)], axis=1)
            h_g = h_scr[g]
            t_g = _mm_nt(c_b, h_g.astype(BF16))
            y_ref[:, cols] = yd_g + jnp.exp(acs_g) * t_g + dsk_ref[:, cols] * x_g
            hs_ref[0, g] = h_g
            last_g = acs_g[CHUNK - 1:CHUNK, :]
            w_b = (xdt_g * jnp.exp(last_g - acs_g)).astype(BF16)
            h_scr[g] = h_g * jnp.exp(_rows_from_lanes(last_g)) + _mm_tn(w_b, b_b)

    return pl.pallas_call(
        body, name="ssd_fwd",
        out_shape=(jax.ShapeDtypeStruct((s, SSD_WIDTH), F32),
                   jax.ShapeDtypeStruct((nc, N_GROUPS, GROUP_WIDTH, D_STATE), F32)),
        grid=(nc,),
        in_specs=[pl.BlockSpec((CHUNK, CONV_CH), lambda c: (c, 0)), pl.BlockSpec((CHUNK, LANES), lambda c: (c, 0)),
                  _const_spec((1, LANES)), _const_spec((1, LANES)), _const_spec((1, SSD_WIDTH))],
        out_specs=(pl.BlockSpec((CHUNK, SSD_WIDTH), lambda c: (c, 0)),
                   pl.BlockSpec((1, N_GROUPS, GROUP_WIDTH, D_STATE), lambda c: (c, 0, 0, 0))),
        scratch_shapes=[pltpu.VMEM((N_GROUPS, GROUP_WIDTH, D_STATE), F32)],
        compiler_params=_params(("arbitrary",)),
    )(xc, small, dtb_row, a_row, dskip_lane)


def ssd_bwd(xc, small, states, dy, dtb_row, a_row, dskip_lane):
    s = xc.shape[0]
    nc = s // CHUNK
    rev = lambda c: nc - 1 - c

    def body(xc_ref, sm_ref, hs_ref, dy_ref, dtb_ref, a_ref, dsk_ref,
             dxc_ref, ddt_ref, da_ref, ddtb_ref, ddsk_ref, dh_scr):
        c = pl.program_id(0)

        @pl.when(c == 0)
        def _():
            dh_scr[...] = jnp.zeros_like(dh_scr)
            da_ref[...] = jnp.zeros_like(da_ref)
            ddtb_ref[...] = jnp.zeros_like(ddtb_ref)
            ddsk_ref[...] = jnp.zeros_like(ddsk_ref)

        lane = _iota((CHUNK, LANES), 1)
        sub = _iota((CHUNK, LANES), 0)
        causal = lane <= sub
        upper = lane >= sub
        is_last = sub == CHUNK - 1
        a_row_v = a_ref[...]
        dt, sig, acs, acs_t = _ssd_chunk_prelude(sm_ref[...], dtb_ref[...], a_row_v, lane, sub)
        cd = jnp.exp(acs[CHUNK - 1:CHUNK, :])
        dacs_c = jnp.zeros((CHUNK, LANES), F32)
        dacs_r = jnp.zeros((LANES, CHUNK), F32)
        ddtx = jnp.zeros((CHUNK, LANES), F32)
        for g in range(N_GROUPS):
            cols = slice(GROUP_WIDTH * g, GROUP_WIDTH * (g + 1))
            b_off = SSD_WIDTH + D_STATE * g
            c_off = SSD_WIDTH + N_GROUPS * D_STATE + D_STATE * g
            b_b = xc_ref[:, b_off:b_off + D_STATE].astype(BF16)
            c_b = xc_ref[:, c_off:c_off + D_STATE].astype(BF16)
            cb = _mm_nt(c_b, b_b)
            cb_t = _mm_nt(b_b, c_b)
            x_g = xc_ref[:, cols]
            dy_g = dy_ref[:, cols]
            dt_g = _expand_group(dt, g, lane)
            acs_g = _expand_group(acs, g, lane)
            last_g = acs_g[CHUNK - 1:CHUNK, :]
            e_g = jnp.exp(acs_g)
            dte_g = jnp.exp(last_g - acs_g)
            xdt_g = x_g * dt_g
            xdt_b = xdt_g.astype(BF16)
            h_g = hs_ref[0, g]
            dh_g = dh_scr[g]
            h_b = h_g.astype(BF16)
            dh_b = dh_g.astype(BF16)
            heads = list(range(HEADS_PER_GROUP * g, HEADS_PER_GROUP * (g + 1)))
            segs = [acs[:, h:h + 1] - acs_t[h:h + 1, :] for h in heads]
            lms = [jnp.exp(jnp.where(causal, sg, NEG_BIG)) for sg in segs]
            mts = [(cb_t * jnp.exp(jnp.where(upper, -sg, NEG_BIG))).astype(BF16) for sg in segs]
            dyh = []
            for k in range(HEADS_PER_GROUP):
                blk = dy_g[:, LANES * (k // 2):LANES * (k // 2 + 1)]
                in_head = (lane < HEAD_DIM) if k % 2 == 0 else (lane >= HEAD_DIM)
                dyh.append(jnp.where(in_head, blk, 0.0).astype(BF16))
            dms = [_mm_nt(dyh[k], xdt_b[:, LANES * (k // 2):LANES * (k // 2 + 1)]) for k in range(HEADS_PER_GROUP)]
            dxs = [_mm(mts[k], dyh[k]) for k in range(HEADS_PER_GROUP)]
            dcb = jnp.zeros((CHUNK, CHUNK), F32)
            for k, h in enumerate(heads):
                gmat = dms[k] * (cb * lms[k])
                dacs_c = dacs_c + jnp.where(lane == h, jnp.sum(gmat, axis=1, keepdims=True), 0.0)
                dacs_r = dacs_r - jnp.where(sub == h, jnp.sum(gmat, axis=0, keepdims=True), 0.0)
                dcb = dcb + dms[k] * lms[k]
            dxdt_g = jnp.concatenate([dxs[2 * k] + dxs[2 * k + 1] for k in range(4)], axis=1)
            t_g = _mm_nt(c_b, h_b)
            dacs_c = dacs_c + _head_sums(dy_g * e_g * t_g, g)
            dt_b = (dy_g * e_g).astype(BF16)
            dc_acc = _mm(dt_b, h_b)
            dh_prev = _mm_tn(dt_b, c_b)
            dw_g = _mm_nt(b_b, dh_b)
            w_g = xdt_g * dte_g
            dxdt_g = dxdt_g + dw_g * dte_g
            db_acc = _mm(w_g.astype(BF16), dh_b)
            r2 = _head_sums(dw_g * w_g, g)
            dacs_c = dacs_c + jnp.where(is_last, jnp.sum(r2, axis=0, keepdims=True), 0.0) - r2
            q3 = jnp.sum(dh_g * h_g, axis=1, keepdims=True)
            for k, h in enumerate(heads):
                tot = jnp.sum(q3[HEAD_DIM * k:HEAD_DIM * (k + 1), :], keepdims=True) * cd[:, h:h + 1]
                dacs_c = dacs_c + jnp.where(is_last & (lane == h), tot, 0.0)
            dh_scr[g] = dh_prev + dh_g * jnp.exp(_rows_from_lanes(last_g))
            dxc_ref[:, cols] = dxdt_g * dt_g + dsk_ref[:, cols] * dy_g
            ddtx = ddtx + _head_sums(dxdt_g * x_g, g)
            ddsk_ref[:, cols] += jnp.sum(dy_g * x_g, axis=0, keepdims=True)
            dxc_ref[:, b_off:b_off + D_STATE] = db_acc + _mm(dcb.T.astype(BF16), c_b)
            dxc_ref[:, c_off:c_off + D_STATE] = dc_acc + _mm(dcb.astype(BF16), b_b)
        dacs = dacs_c + dacs_r.T
        dadt = _mm_exact((lane >= sub).astype(F32), dacs)
        ddt = dadt * a_row_v + ddtx
        ddt_raw = ddt * sig
        ddt_ref[...] = ddt_raw
        da_ref[...] += jnp.sum(dadt * dt, axis=0, keepdims=True)
        ddtb_ref[...] += jnp.sum(ddt_raw, axis=0, keepdims=True)

    return pl.pallas_call(
        body, name="ssd_bwd",
        out_shape=(jax.ShapeDtypeStruct((s, CONV_CH), F32), jax.ShapeDtypeStruct((s, LANES), F32),
                   jax.ShapeDtypeStruct((1, LANES), F32), jax.ShapeDtypeStruct((1, LANES), F32),
                   jax.ShapeDtypeStruct((1, SSD_WIDTH), F32)),
        grid=(nc,),
        in_specs=[pl.BlockSpec((CHUNK, CONV_CH), lambda c: (rev(c), 0)),
                  pl.BlockSpec((CHUNK, LANES), lambda c: (rev(c), 0)),
                  pl.BlockSpec((1, N_GROUPS, GROUP_WIDTH, D_STATE), lambda c: (rev(c), 0, 0, 0)),
                  pl.BlockSpec((CHUNK, SSD_WIDTH), lambda c: (rev(c), 0)),
                  _const_spec((1, LANES)), _const_spec((1, LANES)), _const_spec((1, SSD_WIDTH))],
        out_specs=(pl.BlockSpec((CHUNK, CONV_CH), lambda c: (rev(c), 0)),
                   pl.BlockSpec((CHUNK, LANES), lambda c: (rev(c), 0)),
                   _const_spec((1, LANES)), _const_spec((1, LANES)), _const_spec((1, SSD_WIDTH))),
        scratch_shapes=[pltpu.VMEM((N_GROUPS, GROUP_WIDTH, D_STATE), F32)],
        compiler_params=_params(("arbitrary",)),
    )(xc, small, states, dy, dtb_row, a_row, dskip_lane)


FORGET_BLOCK = 512


def forget_cumsum(small, fgb_row):
    s = small.shape[0]
    t = _blk(s, FORGET_BLOCK)
    nb = s // t

    def body(sm_ref, b_ref, cc_ref, carry):
        i = pl.program_id(0)

        @pl.when(i == 0)
        def _():
            carry[...] = jnp.zeros_like(carry)

        lane = _iota((t, LANES), 1)
        in_f = (lane >= N_HEADS) & (lane < 2 * N_HEADS)
        logf = jnp.where(in_f, -_softplus(-(sm_ref[...] + b_ref[...])), 0.0)
        tri = (_iota((t, t), 1) <= _iota((t, t), 0)).astype(F32)
        cum = _mm_exact(tri, logf) + carry[0:1, :]
        cc_ref[...] = cum
        carry[...] = jnp.broadcast_to(cum[t - 1:t, :], (8, LANES))

    return pl.pallas_call(
        body, name="forget_cumsum",
        out_shape=jax.ShapeDtypeStruct((s, LANES), F32),
        grid=(nb,),
        in_specs=[pl.BlockSpec((t, LANES), lambda i: (i, 0)), _const_spec((1, LANES))],
        out_specs=pl.BlockSpec((t, LANES), lambda i: (i, 0)),
        scratch_shapes=[pltpu.VMEM((8, LANES), F32)],
        compiler_params=_params(("arbitrary",)),
    )(small, fgb_row)


def forget_bwd(dc, small, ddt_raw, fgb_row):
    s = small.shape[0]
    t = _blk(s, FORGET_BLOCK)
    nb = s // t
    rev = lambda i: nb - 1 - i

    def body(dc_ref, sm_ref, ddt_ref, b_ref, ds_ref, dfb_ref, carry):
        i = pl.program_id(0)

        @pl.when(i == 0)
        def _():
            carry[...] = jnp.zeros_like(carry)
            dfb_ref[...] = jnp.zeros_like(dfb_ref)

        lane = _iota((t, LANES), 1)
        rows = dc_ref[...].T
        tri = (_iota((t, t), 1) <= _iota((t, t), 0)).astype(F32)
        rc = _mm_exact(rows, tri) + carry[:, 0:1]
        carry[...] = jnp.broadcast_to(rc[:, 0:1], (LANES, LANES))
        in_f = (lane >= N_HEADS) & (lane < 2 * N_HEADS)
        df = jnp.where(in_f, rc.T * _sigmoid(-(sm_ref[...] + b_ref[...])), 0.0)
        ds_ref[...] = (df + ddt_ref[...]).astype(BF16)
        dfb_ref[...] += jnp.sum(df, axis=0, keepdims=True)

    blk = pl.BlockSpec((t, LANES), lambda i: (rev(i), 0))
    return pl.pallas_call(
        body, name="forget_bwd",
        out_shape=(jax.ShapeDtypeStruct((s, LANES), BF16), jax.ShapeDtypeStruct((1, LANES), F32)),
        grid=(nb,),
        in_specs=[blk, blk, blk, _const_spec((1, LANES))],
        out_specs=(blk, _const_spec((1, LANES))),
        scratch_shapes=[pltpu.VMEM((LANES, LANES), F32)],
        compiler_params=_params(("arbitrary",)),
    )(dc, small, ddt_raw, fgb_row)


ATT_BLOCK = 1024
ATT_BLOCK_BWD = 512
ATT_SCALE = HEAD_DIM ** -0.5
AUG_A = HEAD_DIM
AUG_B = HEAD_DIM + 3


def _split3(c):
    hi = c.astype(BF16).astype(F32)
    r = c - hi
    mid = r.astype(BF16).astype(F32)
    return hi, mid, (r - mid).astype(BF16).astype(F32)


def _aug(lane, first, parts=None, value=1.0):
    if parts is None:
        return jnp.where((lane >= first) & (lane < first + 3), value, 0.0)
    return (jnp.where(lane == first, parts[0], 0.0) + jnp.where(lane == first + 1, parts[1], 0.0)
            + jnp.where(lane == first + 2, parts[2], 0.0))


def _pack_pair(a0, a1, lane):
    return jnp.where(lane < HEAD_DIM, a0, pltpu.roll(a1, HEAD_DIM, 1))


def proj_qkv_heads(u, w_q, w_k, w_v, cum):
    s = u.shape[0]
    tm = _blk(s, 256)

    def body(u_ref, wq_ref, wk_ref, wv_ref, c_ref, qa_ref, ka_ref, va_ref, nrm_ref):
        lane = _iota((tm, LANES), 1)
        lo = lane < HEAD_DIM
        uv = u_ref[...]
        qf = _mm(uv, wq_ref[...]) * ATT_SCALE
        kf = _mm(uv, wk_ref[...])
        vf = _mm(uv, wv_ref[...])
        cc = c_ref[...]
        ones_a = _aug(lane, AUG_A)
        ones_b = _aug(lane, AUG_B)
        sub8 = _iota((8, LANES), 0)
        nrm = jnp.zeros((8, LANES), F32)
        for h in range(N_HEADS):
            j, e = divmod(h, 2)

            def head(full):
                blk = full[:, LANES * j:LANES * (j + 1)]
                if e == 1:
                    blk = pltpu.roll(blk, HEAD_DIM, 1)
                return jnp.where(lo, blk, 0.0)

            parts = _split3(cc[:, N_HEADS + h:N_HEADS + h + 1])
            qh, kh = head(qf), head(kf)
            qa_ref[h] = (qh + _aug(lane, AUG_A, parts) + ones_b).astype(BF16)
            ka_ref[h] = (kh + ones_a - _aug(lane, AUG_B, parts)).astype(BF16)
            va_ref[h] = (head(vf) + ones_a).astype(BF16)
        seg = (_iota((ATT_WIDTH, LANES), 1) == (_iota((ATT_WIDTH, LANES), 0) >> 6)).astype(BF16)
        for r, val in enumerate((qf, kf)):
            sq = val * val
            hi = sq.astype(BF16)
            tot = _mm(hi, seg) + _mm((sq - hi.astype(F32)).astype(BF16), seg)
            nrm = nrm + jnp.where(sub8 == r, jnp.max(tot, axis=0, keepdims=True), 0.0)
        nrm_ref[0] = nrm

    shp = jax.ShapeDtypeStruct((N_HEADS, s, LANES), BF16)
    hspec = pl.BlockSpec((N_HEADS, tm, LANES), lambda i: (0, i, 0))
    wspec = _const_spec((D_MODEL, ATT_WIDTH))
    return pl.pallas_call(
        body, name="proj_qkv_heads",
        out_shape=(shp, shp, shp, jax.ShapeDtypeStruct((s // tm, 8, LANES), F32)), grid=(s // tm,),
        in_specs=[pl.BlockSpec((tm, D_MODEL), lambda i: (i, 0)), wspec, wspec, wspec,
                  pl.BlockSpec((tm, LANES), lambda i: (i, 0))],
        out_specs=(hspec, hspec, hspec, pl.BlockSpec((1, 8, LANES), lambda i: (i, 0, 0))),
        compiler_params=_params(("parallel",)),
    )(u, w_q, w_k, w_v, cum)


SKIP_BELOW = -110.0


def live_blocks(norms, cum, t):
    qn = jnp.sqrt(jnp.max(norms[:, 0, :N_HEADS], axis=0))
    kn = jnp.sqrt(jnp.max(norms[:, 1, :N_HEADS], axis=0))
    bound = 2.05 * qn * kn + 2.0
    c_first = cum[0::t, N_HEADS:2 * N_HEADS]
    c_last = cum[t - 1::t, N_HEADS:2 * N_HEADS]
    nq = c_first.shape[0]
    top = bound[None, None, :] + c_first[:, None, :] - c_last[None, :, :]
    below = jnp.arange(nq)[None, :] < jnp.arange(nq)[:, None]
    dead = below[:, :, None] & ~(top >= SKIP_BELOW)
    first = jnp.sum(dead, axis=1).astype(jnp.int32).T
    last_q = jnp.sum(first[:, None, :] <= jnp.arange(nq)[None, :, None], axis=2).astype(jnp.int32) - 1
    return first, last_q


def attention_fwd(first, qa, ka, va):
    s = qa.shape[1]
    t = _blk(s, ATT_BLOCK)
    nq = s // t

    def body(first_ref, qa_ref, ka_ref, va_ref, o_ref, qb_ref, m_scr, acc_scr, alpha_scr, p_scr, s_scr):
        qi = pl.program_id(1)
        starts = [first_ref[2 * pl.program_id(0) + e, qi] for e in range(2)]
        k0 = jnp.maximum(starts[0], starts[1])
        m_scr[...] = jnp.full_like(m_scr, NEG_BIG)
        acc_scr[...] = jnp.zeros_like(acc_scr)

        def kv_rows(kb):
            return pl.ds(pl.multiple_of(kb * t, t), t)

        def logits(kb, masked, heads=(0, 1)):
            for e in heads:
                sc = _mm_nt(qa_ref[e], ka_ref[e, kv_rows(kb), :])
                if masked:
                    sc = jnp.where(_iota((t, t), 0) >= _iota((t, t), 1), sc, NEG_BIG)
                s_scr[e] = sc

        def probs(heads=(0, 1)):
            for e in heads:
                cmax = s_scr[e, :, 0:LANES]
                for c in range(1, t // LANES):
                    cmax = jnp.maximum(cmax, s_scr[e, :, LANES * c:LANES * (c + 1)])
                m_old = m_scr[e]
                m_new = jnp.maximum(m_old, jnp.max(cmax, axis=1, keepdims=True))
                alpha_scr[e] = jnp.exp(m_old - m_new)
                m_scr[e] = m_new
                for c in range(t // LANES):
                    cols = slice(LANES * c, LANES * (c + 1))
                    p_scr[e, :, cols] = jnp.exp(s_scr[e, :, cols] - m_new).astype(BF16)

        def accumulate(kb, heads=(0, 1)):
            for e in heads:
                acc_scr[e] = alpha_scr[e] * acc_scr[e] + _mm(p_scr[e], va_ref[e, kv_rows(kb), :])

        for e in range(2):
            def alone(kb, carry, e=e):
                logits(kb, False, (e,))
                probs((e,))
                accumulate(kb, (e,))
                return carry

            lax.fori_loop(starts[e], k0, alone, 0)

        def loop_body(kb, carry):
            logits(kb, False)
            for e in range(2):
                accumulate(kb - 1, (e,))
                probs((e,))
            return carry

        @pl.when(qi > k0)
        def _():
            logits(k0, False)
            probs()

        lax.fori_loop(k0 + 1, qi, loop_body, 0)

        @pl.when(qi > k0)
        def _():
            logits(qi, True)
            accumulate(qi - 1)
            probs()

        @pl.when(qi == k0)
        def _():
            logits(qi, True)
            probs()

        accumulate(qi)

        lane = _iota((t, LANES), 1)
        outs = []
        for e in range(2):
            acc = acc_scr[e]
            l = acc[:, AUG_A:AUG_A + 1]
            outs.append(acc / l)
            lse = m_scr[e][:, 0:1] + jnp.log(l)
            q32 = qa_ref[e].astype(F32)
            c = q32[:, AUG_A:AUG_A + 1] + q32[:, AUG_A + 1:AUG_A + 2] + q32[:, AUG_A + 2:AUG_A + 3]
            qb = jnp.where(lane < HEAD_DIM, q32, 0.0) + _aug(lane, AUG_A, _split3(c - lse)) + _aug(lane, AUG_B)
            qb_ref[e] = qb.astype(BF16)
        o_ref[...] = _pack_pair(outs[0], outs[1], lane)

    grid_spec = pltpu.PrefetchScalarGridSpec(
        num_scalar_prefetch=1, grid=(N_PAIRS, nq),
        in_specs=[pl.BlockSpec((2, t, LANES), lambda j, qi, f: (j, qi, 0)),
                  pl.BlockSpec((2, s, LANES), lambda j, qi, f: (j, 0, 0)),
                  pl.BlockSpec((2, s, LANES), lambda j, qi, f: (j, 0, 0))],
        out_specs=[pl.BlockSpec((t, LANES), lambda j, qi, f: (qi, j)),
                   pl.BlockSpec((2, t, LANES), lambda j, qi, f: (j, qi, 0))],
        scratch_shapes=[pltpu.VMEM((2, t, LANES), F32), pltpu.VMEM((2, t, LANES), F32),
                        pltpu.VMEM((2, t, LANES), F32), pltpu.VMEM((2, t, t), BF16), pltpu.VMEM((2, t, t), F32)])
    return pl.pallas_call(
        body, name="attention_fwd", grid_spec=grid_spec,
        out_shape=(jax.ShapeDtypeStruct((s, ATT_WIDTH), F32), jax.ShapeDtypeStruct((N_HEADS, s, LANES), BF16)),
        compiler_params=_params(("parallel", "parallel")),
    )(first, qa, ka, va)


def attention_bwd(last_q, qb, ka, va, dob):
    s = qb.shape[1]
    t = _blk(s, ATT_BLOCK_BWD)
    nq = s // t

    def body(last_ref, qb_ref, dob_ref, ka_ref, va_ref, dq_ref, dk_ref, dv_ref, dc_ref, dq_scr, dk_scr, dv_scr):
        j, ki = pl.program_id(0), pl.program_id(1)

        @pl.when((j == 0) & (ki == 0))
        def _():
            dc_ref[...] = jnp.zeros_like(dc_ref)

        @pl.when(ki == 0)
        def _():
            dq_scr[...] = jnp.zeros_like(dq_scr)

        dk_scr[...] = jnp.zeros_like(dk_scr)
        dv_scr[...] = jnp.zeros_like(dv_scr)

        def q_step(qblk, masked, heads=(0, 1)):
            rows = pl.ds(pl.multiple_of(qblk * t, t), t)
            scs = [_mm_nt(qb_ref[e, rows, :], ka_ref[e]) for e in heads]
            dps = [_mm_nt(dob_ref[e, rows, :], va_ref[e]) for e in heads]
            for e, sc, dp in zip(heads, scs, dps):
                q = qb_ref[e, rows, :]
                do = dob_ref[e, rows, :]
                if masked:
                    sc = jnp.where(_iota((t, t), 0) >= _iota((t, t), 1), sc, NEG_BIG)
                p = jnp.exp(sc)
                ds_b = (p * dp).astype(BF16)
                dv_scr[e] += _mm_tn(p.astype(BF16), do)
                dk_scr[e] += _mm_tn(ds_b, q)
                dq_scr[e, rows, :] += _mm(ds_b, ka_ref[e])

        def loop_body(qblk, carry):
            q_step(qblk, False)
            return carry

        ends = [last_ref[2 * j + e, ki] + 1 for e in range(2)]
        both = jnp.minimum(ends[0], ends[1])
        q_step(ki, True)
        lax.fori_loop(ki + 1, both, loop_body, 0)
        for e in range(2):
            def alone(qblk, carry, e=e):
                q_step(qblk, False, (e,))
                return carry

            lax.fori_loop(both, ends[e], alone, 0)

        lane = _iota((t, LANES), 1)
        dk_ref[...] = _pack_pair(dk_scr[0], dk_scr[1], lane).astype(BF16)
        dv_ref[...] = _pack_pair(dv_scr[0], dv_scr[1], lane).astype(BF16)
        rows = pl.ds(pl.multiple_of(ki * t, t), t)
        dc_ref[rows, :] -= (jnp.where(lane == N_HEADS + 2 * j, dk_scr[0][:, AUG_B:AUG_B + 1], 0.0)
                            + jnp.where(lane == N_HEADS + 2 * j + 1, dk_scr[1][:, AUG_B:AUG_B + 1], 0.0))

        @pl.when(ki == nq - 1)
        def _():
            for blk in range(nq):
                rws = pl.ds(blk * t, t)
                d0 = dq_scr[0, rws, :]
                d1 = dq_scr[1, rws, :]
                dq_ref[rws, :] = (_pack_pair(d0, d1, lane) * ATT_SCALE).astype(BF16)
                dc_ref[rws, :] += (jnp.where(lane == N_HEADS + 2 * j, d0[:, AUG_A:AUG_A + 1], 0.0)
                                   + jnp.where(lane == N_HEADS + 2 * j + 1, d1[:, AUG_A:AUG_A + 1], 0.0))

    full = pl.BlockSpec((2, s, LANES), lambda j, ki, f: (j, 0, 0))
    blk = pl.BlockSpec((2, t, LANES), lambda j, ki, f: (j, ki, 0))
    pair = pl.BlockSpec((t, LANES), lambda j, ki, f: (ki, j))
    wide = jax.ShapeDtypeStruct((s, ATT_WIDTH), BF16)
    grid_spec = pltpu.PrefetchScalarGridSpec(
        num_scalar_prefetch=1, grid=(N_PAIRS, nq),
        in_specs=[full, full, blk, blk],
        out_specs=[pl.BlockSpec((s, LANES), lambda j, ki, f: (0, j)), pair, pair,
                   pl.BlockSpec((s, LANES), lambda j, ki, f: (0, 0))],
        scratch_shapes=[pltpu.VMEM((2, s, LANES), F32), pltpu.VMEM((2, t, LANES), F32),
                        pltpu.VMEM((2, t, LANES), F32)])
    return pl.pallas_call(
        body, name="attention_bwd", grid_spec=grid_spec,
        out_shape=(wide, wide, wide, jax.ShapeDtypeStruct((s, LANES), F32)),
        compiler_params=_params(("arbitrary", "arbitrary")),
    )(last_q, qb, dob, ka, va)


def _dsilu(z, sg):
    return sg * (1.0 + z * (1.0 - sg))


def post_mix(x, y, zs, o, za, p, tgt, ssd_g, att_g_lane, ple_g, fin_g, w_out, w_gate, w_proj):
    s = x.shape[0]
    tm = _blk(s, 256)
    half = SSD_WIDTH // N_GROUPS

    def rms_bwd(dy, yn, r):
        return r * (dy - yn * jnp.mean(dy * yn, axis=-1, keepdims=True))

    def colsum(a):
        return jnp.sum(a, axis=0, keepdims=True)

    def body(x_ref, y_ref, zs_ref, o_ref, za_ref, p_ref, t_ref, sg_ref, ag_ref, pg_ref, fg_ref,
             wo_ref, wg_ref, wp_ref,
             dh1_ref, dy_ref, dzs_ref, dob_ref, dza_ref, ycat_ref, dh1b_ref, n2b_ref, dglb_ref, dppb_ref, pb_ref,
             loss_ref, dfin_ref, dple_ref, dssd_ref, datt_ref):
        @pl.when(pl.program_id(0) == 0)
        def _():
            for r in (loss_ref, dfin_ref, dple_ref, dssd_ref, datt_ref):
                r[...] = jnp.zeros_like(r)

        lane = _iota((tm, LANES), 1)
        lo = lane < HEAD_DIM
        zs = zs_ref[...]
        sz = _sigmoid(zs)
        yv = y_ref[...]
        ys = yv * (zs * sz)
        yn, rg = [], []
        for g in range(N_GROUPS):
            seg = ys[:, half * g:half * (g + 1)]
            r = lax.rsqrt(jnp.mean(seg * seg, axis=-1, keepdims=True) + EPS)
            yn.append(seg * r)
            rg.append(r)
            ycat_ref[:, half * g:half * (g + 1)] = (yn[g] * sg_ref[:, half * g:half * (g + 1)]).astype(BF16)
        za = za_ref[...]
        sza = _sigmoid(za)
        silu_za = za * sza
        on, ra = [], []
        for jb in range(N_PAIRS):
            blk = o_ref[:, LANES * jb:LANES * (jb + 1)]
            sq = blk * blk
            ms0 = jnp.sum(jnp.where(lo, sq, 0.0), axis=1, keepdims=True) * (1.0 / HEAD_DIM)
            ms1 = jnp.sum(jnp.where(lo, 0.0, sq), axis=1, keepdims=True) * (1.0 / HEAD_DIM)
            r = jnp.where(lo, lax.rsqrt(ms0 + EPS), lax.rsqrt(ms1 + EPS))
            on.append(blk * r)
            ra.append(r)
            an = on[jb] * ag_ref[:, LANES * jb:LANES * (jb + 1)]
            ycat_ref[:, SSD_WIDTH + LANES * jb:SSD_WIDTH + LANES * (jb + 1)] = (
                an * silu_za[:, LANES * jb:LANES * (jb + 1)]).astype(BF16)
        h1 = x_ref[...] + _mm(ycat_ref[...], wo_ref[...])
        r2 = lax.rsqrt(jnp.mean(h1 * h1, axis=-1, keepdims=True) + EPS)
        n2h = h1 * r2
        n2_b = (n2h * pg_ref[...]).astype(BF16)
        gate = _sigmoid(_mm(n2_b, wg_ref[...]))
        p_b = p_ref[...].astype(BF16)
        pp = _mm(p_b, wp_ref[...])
        h2 = h1 + gate * pp
        r3 = lax.rsqrt(jnp.mean(h2 * h2, axis=-1, keepdims=True) + EPS)
        n3 = h2 * r3
        diff = n3 * fg_ref[...] - t_ref[...]
        sq = colsum(diff * diff)
        part = sq[:, 0:LANES]
        for jb in range(1, D_MODEL // LANES):
            part = part + sq[:, LANES * jb:LANES * (jb + 1)]
        loss_ref[...] += part * (0.5 / D_MODEL)
        dout = diff * (1.0 / D_MODEL)
        dfin_ref[...] += colsum(dout * n3)
        dh2 = rms_bwd(dout * fg_ref[...], n3, r3)
        dgl = dh2 * pp * gate * (1.0 - gate)
        dgl_b = dgl.astype(BF16)
        dn2 = _mm_nt(dgl_b, wg_ref[...])
        dple_ref[...] += colsum(dn2 * n2h)
        dh1 = dh2 + rms_bwd(dn2 * pg_ref[...], n2h, r2)
        dh1_b = dh1.astype(BF16)
        dycat = _mm_nt(dh1_b, wo_ref[...])
        dh1_ref[...] = dh1
        dh1b_ref[...] = dh1_b
        n2b_ref[...] = n2_b
        dglb_ref[...] = dgl_b
        dppb_ref[...] = (dh2 * gate).astype(BF16)
        pb_ref[...] = p_b
        for g in range(N_GROUPS):
            cols = slice(half * g, half * (g + 1))
            dys_g = dycat[:, cols]
            dssd_ref[:, cols] += colsum(dys_g * yn[g])
            dys = rms_bwd(dys_g * sg_ref[:, cols], yn[g], rg[g])
            dy_ref[:, cols] = dys * (zs[:, cols] * sz[:, cols])
            dzs_ref[:, cols] = (dys * yv[:, cols] * _dsilu(zs[:, cols], sz[:, cols])).astype(BF16)
        for jb in range(N_PAIRS):
            cols = slice(LANES * jb, LANES * (jb + 1))
            dya = dycat[:, SSD_WIDTH + LANES * jb:SSD_WIDTH + LANES * (jb + 1)]
            ag = ag_ref[:, cols]
            dan = dya * silu_za[:, cols]
            dza_ref[:, cols] = (dya * (on[jb] * ag) * _dsilu(za[:, cols], sza[:, cols])).astype(BF16)
            datt_ref[:, cols] += colsum(dan * on[jb])
            don = dan * ag
            q = don * on[jb]
            m0 = jnp.sum(jnp.where(lo, q, 0.0), axis=1, keepdims=True) * (1.0 / HEAD_DIM)
            m1 = jnp.sum(jnp.where(lo, 0.0, q), axis=1, keepdims=True) * (1.0 / HEAD_DIM)
            do2 = ra[jb] * (don - on[jb] * jnp.where(lo, m0, m1))
            prod = do2 * o_ref[:, cols]
            for e in range(2):
                delta = jnp.sum(jnp.where(lo, prod, 0.0) if e == 0 else jnp.where(lo, 0.0, prod),
                                axis=1, keepdims=True)
                base = jnp.where(lo, do2 if e == 0 else pltpu.roll(do2, HEAD_DIM, 1), 0.0)
                dob_ref[2 * jb + e] = (base - _aug(lane, AUG_A, _split3(delta))).astype(BF16)

    def rows(n, dtype=None):
        return pl.BlockSpec((tm, n), lambda i: (i, 0))

    def out(n, dtype):
        return jax.ShapeDtypeStruct((s, n), dtype)

    vec = _const_spec((1, D_MODEL))
    vshape = jax.ShapeDtypeStruct((1, D_MODEL), F32)
    return pl.pallas_call(
        body, name="post_mix",
        out_shape=(out(D_MODEL, F32), out(SSD_WIDTH, F32), out(SSD_WIDTH, BF16),
                   jax.ShapeDtypeStruct((N_HEADS, s, LANES), BF16),
                   out(ATT_WIDTH, BF16), out(D_INNER, BF16), out(D_MODEL, BF16), out(D_MODEL, BF16),
                   out(D_MODEL, BF16), out(D_MODEL, BF16), out(PLE_DIM, BF16),
                   jax.ShapeDtypeStruct((1, LANES), F32), vshape, vshape, vshape, vshape),
        grid=(s // tm,),
        in_specs=[rows(D_MODEL), rows(SSD_WIDTH), rows(SSD_WIDTH), rows(ATT_WIDTH), rows(ATT_WIDTH),
                  rows(PLE_DIM), rows(D_MODEL), vec, vec, vec, vec,
                  _const_spec((D_INNER, D_MODEL)), _const_spec((D_MODEL, D_MODEL)), _const_spec((PLE_DIM, D_MODEL))],
        out_specs=(rows(D_MODEL), rows(SSD_WIDTH), rows(SSD_WIDTH),
                   pl.BlockSpec((N_HEADS, tm, LANES), lambda i: (0, i, 0)), rows(ATT_WIDTH),
                   rows(D_INNER), rows(D_MODEL), rows(D_MODEL), rows(D_MODEL), rows(D_MODEL), rows(PLE_DIM),
                   _const_spec((1, LANES)), vec, vec, vec, vec),
        compiler_params=_params(("arbitrary",)),
    )(x, y, zs, o, za, p, tgt, ssd_g, att_g_lane, ple_g, fin_g, w_out, w_gate, w_proj)


def in_proj_bwd(dsegs, wsegs, x, g, dh1, pres):
    s = x.shape[0]
    tm = _blk(s, 256)
    nseg = len(dsegs)
    nbig = len(pres)
    nsteps = s // tm

    def body(*refs):
        d_refs = refs[:nseg]
        w_refs = refs[nseg:2 * nseg]
        x_ref, g_ref, dh1_ref = refs[2 * nseg:2 * nseg + 3]
        rest = refs[2 * nseg + 3:]
        pre_refs, (dx_ref, dg_ref), part_refs = rest[:nbig], rest[nbig:nbig + 2], rest[nbig + 2:2 * nbig + 2]
        ssem, rsem, lsem = rest[2 * nbig + 2:]

        @pl.when(pl.program_id(0) == 0)
        def _():
            dg_ref[...] = jnp.zeros_like(dg_ref)
            for cp in scatter_copies(pre_refs, part_refs, ssem, rsem, lsem):
                cp.start()

        @pl.when(pl.program_id(0) == nsteps - 1)
        def _():
            for cp in scatter_copies(pre_refs, part_refs, ssem, rsem, lsem):
                cp.wait()

        du = _mm_nt(d_refs[0][...], w_refs[0][...])
        for k in range(1, nseg):
            du = du + _mm_nt(d_refs[k][...], w_refs[k][...])
        xv = x_ref[...]
        r = lax.rsqrt(jnp.mean(xv * xv, axis=-1, keepdims=True) + EPS)
        xh = xv * r
        dg_ref[...] += jnp.sum(du * xh, axis=0, keepdims=True)
        dxh = du * g_ref[...]
        dx_ref[...] = r * (dxh - xh * jnp.mean(dxh * xh, axis=-1, keepdims=True)) + dh1_ref[...]

    rows = lambda n: pl.BlockSpec((tm, n), lambda i: (i, 0))
    return pl.pallas_call(
        body, name="in_proj_bwd",
        out_shape=tuple([jax.ShapeDtypeStruct((s, D_MODEL), F32), jax.ShapeDtypeStruct((1, D_MODEL), F32)]
                        + [jax.ShapeDtypeStruct(a.shape, a.dtype) for a in pres]),
        grid=(nsteps,),
        in_specs=([rows(d.shape[1]) for d in dsegs] + [_const_spec(w.shape) for w in wsegs]
                  + [rows(D_MODEL), _const_spec((1, D_MODEL)), rows(D_MODEL)] + [ANY] * nbig),
        out_specs=tuple([rows(D_MODEL), _const_spec((1, D_MODEL))] + [ANY] * nbig),
        scratch_shapes=_sems(3 * nbig) + [pltpu.SemaphoreType.DMA((nbig,))],
        compiler_params=_params(("arbitrary",)),
    )(*dsegs, *wsegs, x, g, dh1, *pres)


SMALL_NAMES = ("norm_g", "conv_b", "dt_bias", "a_log", "d_skip", "ssd_norm_g", "fg_bias", "att_norm_g",
               "ple_norm_g", "final_norm_g")
SMALL_SIZES = (1024, 1536, 16, 16, 16, 1024, 16, 64, 1024, 1024)
CONV_W_SIZE = CONV_WIDTH * CONV_CH


def _pack_small(vals):
    flat = jnp.concatenate([v.reshape(-1).astype(F32) for v in vals])
    flat = jnp.pad(flat, (0, SMALL_ROWS * LANES - flat.shape[0]))
    return flat.reshape(SMALL_ROWS, LANES)


def _unpack_small(pack, shapes):
    flat = pack.reshape(-1)
    out, off = [], 0
    for n, shp in zip(SMALL_SIZES, shapes):
        out.append(flat[off:off + n].reshape(shp))
        off += n
    return out


def _row128(v16, offset=0):
    return jnp.pad(v16.reshape(1, N_HEADS).astype(F32), ((0, 0), (offset, LANES - N_HEADS - offset)))


def local_step(prereduce, x, p, tgt, w_in, w_out, w_gate, w_proj, conv_w, norm_g, conv_b, dt_bias, a_log, d_skip,
               ssd_norm_g, fg_bias, att_norm_g, ple_norm_g, final_norm_g):
    widths = (SSD_WIDTH, CONV_CH, N_HEADS, ATT_WIDTH, ATT_WIDTH, ATT_WIDTH, ATT_WIDTH)
    c0, c1, c2, c3, c4, c5, c6, c7 = [sum(widths[:i]) for i in range(len(widths) + 1)]
    w_zs, w_xbc, w_dt = w_in[:, c0:c1], w_in[:, c1:c2], w_in[:, c2:c3]
    w_za, w_q, w_k, w_v, w_f = w_in[:, c3:c4], w_in[:, c4:c5], w_in[:, c5:c6], w_in[:, c6:c7], w_in[:, c7:]
    w_small = jnp.concatenate([w_dt, w_f, jnp.zeros((D_MODEL, LANES - 2 * N_HEADS), BF16)], axis=1)

    dtb_row = _row128(dt_bias)
    a_row = _row128(-jnp.exp(a_log.astype(F32)))
    fgb_row = _row128(fg_bias, N_HEADS)
    dskip_lane = jnp.repeat(d_skip.astype(F32), HEAD_DIM).reshape(1, SSD_WIDTH)
    att_g_lane = jnp.tile(att_norm_g.astype(F32), N_HEADS).reshape(1, ATT_WIDTH)
    row = lambda v: v.reshape(1, -1).astype(F32)

    u = rms_prenorm(x, row(norm_g))
    zs = matmul_rows(u, w_zs, F32, "proj_z_ssd")
    xbc = matmul_rows(u, w_xbc, F32, "proj_xbc")
    za = matmul_rows(u, w_za, F32, "proj_z_att")
    small = matmul_rows(u, w_small, F32, "proj_small")
    cum = forget_cumsum(small, fgb_row)
    qa, ka, va, norms = proj_qkv_heads(u, w_q, w_k, w_v, cum)
    first, _ = live_blocks(norms, cum, _blk(x.shape[0], ATT_BLOCK))
    _, last_q = live_blocks(norms, cum, _blk(x.shape[0], ATT_BLOCK_BWD))
    pre, xc = conv_fwd(xbc, conv_w, row(conv_b))
    y, states = ssd_fwd(xc, small, dtb_row, a_row, dskip_lane)
    o, qb = attention_fwd(first, qa, ka, va)
    (dh1, dy, dzs, dob, dza, ycat, dh1_b, n2_b, dgl_b, dpp_b, p_b,
     loss_l, dfin, dple, dssd_g, datt_lane) = post_mix(
        x, y, zs, o, za, p, tgt, row(ssd_norm_g), att_g_lane, row(ple_norm_g), row(final_norm_g),
        w_out, w_gate, w_proj)
    dq, dk, dv, dc = attention_bwd(last_q, qb, ka, va, dob)
    dxc, ddt_raw, da, ddtb, ddsk_lane = ssd_bwd(xc, small, states, dy, dtb_row, a_row, dskip_lane)
    dsmall, dfgb = forget_bwd(dc, small, ddt_raw, fgb_row)
    dxbc, dconv_w8, dconv_b = conv_bwd(xbc, pre, dxc, conv_w)
    dsegs = [dzs, dxbc, dza, dq, dk, dv, dsmall]
    wsegs = [w_zs, w_xbc, w_za, w_q, w_k, w_v, w_small]
    dws = [matmul_tn(u, d, "dw_in_%d" % i) for i, d in enumerate(dsegs)]
    dw_in = jnp.concatenate([dws[0], dws[1], dws[6][:, :N_HEADS], dws[2], dws[3], dws[4], dws[5],
                             dws[6][:, N_HEADS:2 * N_HEADS]], axis=1)
    dw_out = matmul_tn(ycat, dh1_b, "dw_out")
    dw_gate = matmul_tn(n2_b, dgl_b, "dw_gate")
    dw_proj = matmul_tn(p_b, dpp_b, "dw_proj")
    dx, dnorm_g, *parts = in_proj_bwd(dsegs, wsegs, x, row(norm_g), dh1, prereduce(dw_in, dw_out, dw_gate, dw_proj))
    small_grads = [
        dnorm_g, dconv_b, ddtb[0, :N_HEADS], (da * a_row)[0, :N_HEADS],
        ddsk_lane.reshape(N_HEADS, HEAD_DIM).sum(axis=1), dssd_g, dfgb[0, N_HEADS:2 * N_HEADS],
        datt_lane.reshape(N_HEADS, HEAD_DIM).sum(axis=0), dple, dfin]
    loss = jnp.sum(loss_l)
    return loss, dx, parts, dconv_w8[:CONV_WIDTH], small_grads


def kernel(x, p, norm_g, w_in, conv_w, conv_b, dt_bias, a_log, d_skip, ssd_norm_g, fg_bias, att_norm_g, w_out, ple_norm_g, w_ple_gate, w_ple_proj, final_norm_g, loss_target, m_norm_g, m_w_in, m_conv_w, m_conv_b, m_dt_bias, m_a_log, m_d_skip, m_ssd_norm_g, m_fg_bias, m_att_norm_g, m_w_out, m_ple_norm_g, m_w_ple_gate, m_w_ple_proj, m_final_norm_g, v_norm_g, v_w_in, v_conv_w, v_conv_b, v_dt_bias, v_a_log, v_d_skip, v_ssd_norm_g, v_fg_bias, v_att_norm_g, v_w_out, v_ple_norm_g, v_w_ple_gate, v_w_ple_proj, v_final_norm_g):
    chip = 2 * lax.axis_index("x") + lax.axis_index("y")
    core = lax.axis_index("c")

    big_w = [w_in[0], w_out[0], w_ple_gate[0], w_ple_proj[0]]
    own = [a.astype(BF16) for a in big_w] + [conv_w[0]]
    gathered = gather_weights(own[:4], own[4])

    def joined(k, axis):
        return jnp.concatenate([jnp.where(chip == j, own[k], gathered[k][j]) for j in range(N_CHIPS)], axis=axis)

    w_in_f, w_out_f, w_gate_f, w_proj_f, conv_w_f = joined(0, 1), joined(1, 0), joined(2, 0), joined(3, 1), joined(4, 1)

    core1 = core.reshape(1).astype(jnp.int32)

    def prereduce(dw_in, dw_out, dw_gate, dw_proj):
        n_in, n_proj = w_in.shape[2], w_ple_proj.shape[2]
        gs = [jnp.stack([dw_in[:, n_in * j:n_in * (j + 1)] for j in range(N_CHIPS)]),
              dw_out.reshape(N_CHIPS, w_out.shape[1], D_MODEL), dw_gate.reshape(N_CHIPS, w_ple_gate.shape[1], D_MODEL),
              jnp.stack([dw_proj[:, n_proj * j:n_proj * (j + 1)] for j in range(N_CHIPS)])]
        return add_halves(core1, gs, halves_to_sibling(gs))

    smalls_w = [norm_g, conv_b, dt_bias, a_log, d_skip, ssd_norm_g, fg_bias, att_norm_g, ple_norm_g, final_norm_g]
    loss_l, dx, parts, dconv_w, small_grads = local_step(
        prereduce, x[0], p[0, 0], loss_target[0], w_in_f, w_out_f, w_gate_f, w_proj_f, conv_w_f,
        *[a.reshape(-1) for a in smalls_w])
    loss = lax.psum(loss_l, ("x", "y", "c"))
    smalls = gather_small(_pack_small(list(small_grads) + [dconv_w]))
    mine = sum_parts(parts)

    g_big, d_big, m_big, v_big = adamw_big(
        core1, mine, swap_halves(mine), big_w, [m_w_in[0], m_w_out[0], m_w_ple_gate[0], m_w_ple_proj[0]],
        [v_w_in[0], v_w_out[0], v_w_ple_gate[0], v_w_ple_proj[0]])
    smalls_m = [m_norm_g, m_conv_b, m_dt_bias, m_a_log, m_d_skip, m_ssd_norm_g, m_fg_bias, m_att_norm_g,
                m_ple_norm_g, m_final_norm_g]
    smalls_v = [v_norm_g, v_conv_b, v_dt_bias, v_a_log, v_d_skip, v_ssd_norm_g, v_fg_bias, v_att_norm_g,
                v_ple_norm_g, v_final_norm_g]
    g_sm, d_sm, m_sm, v_sm = adamw_small(smalls, _pack_small(smalls_w), _pack_small(smalls_m), _pack_small(smalls_v))
    n_small = sum(SMALL_SIZES)
    g_conv_full = g_sm.reshape(-1)[n_small:n_small + CONV_W_SIZE].reshape(CONV_WIDTH, CONV_CH)
    n_conv = conv_w.shape[2]
    g_conv = lax.dynamic_slice_in_dim(g_conv_full, chip * n_conv, n_conv, axis=1)
    d_conv, m_conv, v_conv = adamw_whole(g_conv, conv_w[0], m_conv_w[0], v_conv_w[0], "adamw_conv")

    shapes = [a.shape for a in smalls_w]
    outs = []
    for big, conv, sm in ((g_big, g_conv, g_sm), (d_big, d_conv, d_sm), (m_big, m_conv, m_sm), (v_big, v_conv, v_sm)):
        b_in, b_out, b_gate, b_proj = [a[None] for a in big]
        s_norm, s_convb, s_dtb, s_alog, s_dsk, s_ssdg, s_fgb, s_attg, s_pleg, s_fin = _unpack_small(sm, shapes)
        outs.extend([s_norm, b_in, conv[None], s_convb, s_dtb, s_alog, s_dsk, s_ssdg, s_fgb, s_attg, b_out, s_pleg,
                     b_gate, b_proj, s_fin])
    return (loss, dx[None], *outs)
```

```python
import functools

import jax
import jax.numpy as jnp
from jax import lax
from jax.experimental import pallas as pl
from jax.experimental.pallas import tpu as pltpu

F32 = jnp.float32
BF16 = jnp.bfloat16

D_MODEL = 1024
SSD_WIDTH = 1024
ATT_WIDTH = 1024
N_HEADS = 16
HEAD_DIM = 64
N_GROUPS = 2
D_STATE = 128
CONV_CH = 1536
CONV_WIDTH = 4
CHUNK = 128
PLE_DIM = 256
D_INNER = 2048
EPS = 1e-6
IN_COLS = 6688
N_CHIPS = 4
N_DEV = 8
LANES = 128
N_PAIRS = 8

ADAM_LR = 0.001
ADAM_B1 = 0.9
ADAM_B2 = 0.999
ADAM_EPS = 1e-08
ADAM_WD = 0.01
ADAM_STEP = 10

SMALL_ROWS = 96

NEG_BIG = -1e30
VMEM_LIMIT = 56 * 1024 * 1024

MESH = pl.DeviceIdType.MESH
ANY = pl.BlockSpec(memory_space=pl.ANY)


def _mm(a, b):
    return jnp.dot(a, b, preferred_element_type=F32)


def _mm_nt(a, b):
    return lax.dot_general(a, b, (((1,), (1,)), ((), ())), preferred_element_type=F32)


def _mm_tn(a, b):
    return lax.dot_general(a, b, (((0,), (0,)), ((), ())), preferred_element_type=F32)


def _mm_exact(a, b):
    return jnp.dot(a, b, preferred_element_type=F32, precision=lax.Precision.HIGHEST)


def _softplus(x):
    return jnp.maximum(x, 0.0) + jnp.log1p(jnp.exp(-jnp.abs(x)))


def _sigmoid(x):
    return jax.nn.sigmoid(x)


def _iota(shape, dim):
    return lax.broadcasted_iota(jnp.int32, shape, dim)


def _params(sem=None):
    return pltpu.CompilerParams(dimension_semantics=sem, vmem_limit_bytes=VMEM_LIMIT)


def _blk(n, pref):
    return min(n, pref)


def _const_spec(shape):
    nd = len(shape)
    return pl.BlockSpec(shape, lambda *_: (0,) * nd)


def _chip_peers():
    x, y, c = lax.axis_index("x"), lax.axis_index("y"), lax.axis_index("c")
    return x, y, c, [(1 - x, y, c), (x, 1 - y, c), (1 - x, 1 - y, c)]


def _half(rows, c):
    h = rows // 2
    return pl.ds(pl.multiple_of(c * h, 8), h)


def _sems(n):
    return [pltpu.SemaphoreType.DMA((n,)), pltpu.SemaphoreType.DMA((n,))]


def gather_weights(shards, conv_s):
    n = len(shards)

    def body(*refs):
        ins, conv_in = refs[:n], refs[n]
        outs, conv_out = refs[n + 1:2 * n + 1], refs[2 * n + 1]
        ssem1, rsem1, ssem2, rsem2, c_ssem, c_rsem = refs[2 * n + 2:]
        x, y, c, peers = _chip_peers()
        me = 2 * x + y
        sibling = (x, y, 1 - c)
        first, small = [], []
        for k, peer in enumerate(peers):
            for i in range(n):
                h = _half(ins[i].shape[0], c)
                first.append(pltpu.make_async_remote_copy(
                    src_ref=ins[i].at[h], dst_ref=outs[i].at[me, h], send_sem=ssem1.at[n * k + i],
                    recv_sem=rsem1.at[n * k + i], device_id=peer, device_id_type=MESH))
            small.append(pltpu.make_async_remote_copy(
                src_ref=conv_in, dst_ref=conv_out.at[me], send_sem=c_ssem.at[k], recv_sem=c_rsem.at[k],
                device_id=peer, device_id_type=MESH))
        for cp in first + small:
            cp.start()
        passed = []
        for k, peer in enumerate(peers):
            chip = 2 * peer[0] + peer[1]
            for i in range(n):
                h = _half(ins[i].shape[0], c)
                first[n * k + i].wait_recv()
                fwd = pltpu.make_async_remote_copy(
                    src_ref=outs[i].at[chip, h], dst_ref=outs[i].at[chip, h], send_sem=ssem2.at[n * k + i],
                    recv_sem=rsem2.at[n * k + i], device_id=sibling, device_id_type=MESH)
                fwd.start()
                passed.append(fwd)
        for cp in passed:
            cp.wait_recv()
        for cp in first + passed:
            cp.wait_send()
        for cp in small:
            cp.wait()

    return pl.pallas_call(
        body, name="gather_weights",
        out_shape=tuple(jax.ShapeDtypeStruct((N_CHIPS,) + a.shape, a.dtype) for a in list(shards) + [conv_s]),
        in_specs=[ANY] * (n + 1), out_specs=(ANY,) * (n + 1),
        scratch_shapes=_sems(3 * n) + _sems(3 * n) + _sems(3),
    )(*shards, conv_s)


def halves_to_sibling(gs):
    n = len(gs)

    def body(*refs):
        ins, outs = refs[:n], refs[n:2 * n]
        ssem, rsem = refs[2 * n:]
        x, y, c = lax.axis_index("x"), lax.axis_index("y"), lax.axis_index("c")
        copies = []
        for i in range(n):
            for j in range(N_CHIPS):
                copies.append(pltpu.make_async_remote_copy(
                    src_ref=ins[i].at[j, _half(ins[i].shape[1], 1 - c)], dst_ref=outs[i].at[j],
                    send_sem=ssem.at[N_CHIPS * i + j], recv_sem=rsem.at[N_CHIPS * i + j],
                    device_id=(x, y, 1 - c), device_id_type=MESH))
        for cp in copies:
            cp.start()
        for cp in copies:
            cp.wait()

    return pl.pallas_call(
        body, name="halves_to_sibling",
        out_shape=tuple(jax.ShapeDtypeStruct((N_CHIPS, g.shape[1] // 2, g.shape[2]), F32) for g in gs),
        in_specs=[ANY] * n, out_specs=(ANY,) * n, scratch_shapes=_sems(N_CHIPS * n),
    )(*gs)


RED_GRID = 8


def add_halves(core, gs, rbs):
    n = len(gs)

    def body(c_ref, *refs):
        for i in range(n):
            refs[2 * n + i][...] = (refs[i][...] + refs[n + i][...]).astype(BF16)

    def blk(g):
        return (1, g.shape[1] // 2 // RED_GRID, g.shape[2])

    grid_spec = pltpu.PrefetchScalarGridSpec(
        num_scalar_prefetch=1, grid=(N_CHIPS, RED_GRID),
        in_specs=([pl.BlockSpec(blk(g), lambda j, b, c_ref: (j, c_ref[0] * RED_GRID + b, 0)) for g in gs]
                  + [pl.BlockSpec(blk(g), lambda j, b, c_ref: (j, b, 0)) for g in gs]),
        out_specs=[pl.BlockSpec(blk(g), lambda j, b, c_ref: (j, b, 0)) for g in gs])
    return pl.pallas_call(
        body, name="add_halves", grid_spec=grid_spec,
        out_shape=tuple(jax.ShapeDtypeStruct(r.shape, BF16) for r in rbs),
        compiler_params=_params(("parallel", "parallel")),
    )(core, *gs, *rbs)


def scatter_copies(ins, outs, ssem, rsem, lsem):
    n = len(ins)
    x, y, _, peers = _chip_peers()
    me = 2 * x + y
    copies = [pltpu.make_async_copy(ins[i].at[me], outs[i].at[me], lsem.at[i]) for i in range(n)]
    for k, peer in enumerate(peers):
        dst_chip = 2 * peer[0] + peer[1]
        for i in range(n):
            copies.append(pltpu.make_async_remote_copy(
                src_ref=ins[i].at[dst_chip], dst_ref=outs[i].at[me], send_sem=ssem.at[n * k + i],
                recv_sem=rsem.at[n * k + i], device_id=peer, device_id_type=MESH))
    return copies


def gather_small(small):
    def body(s_ref, smalls_ref, ssem, rsem, lsem):
        x, y, c = lax.axis_index("x"), lax.axis_index("y"), lax.axis_index("c")
        dev = 4 * x + 2 * y + c
        copies = [pltpu.make_async_copy(s_ref, smalls_ref.at[dev], lsem)]
        for k in range(1, N_DEV):
            fx, fy, fc = (k >> 2) & 1, (k >> 1) & 1, k & 1
            peer = ((1 - x) if fx else x, (1 - y) if fy else y, (1 - c) if fc else c)
            copies.append(pltpu.make_async_remote_copy(
                src_ref=s_ref, dst_ref=smalls_ref.at[dev], send_sem=ssem.at[k - 1], recv_sem=rsem.at[k - 1],
                device_id=peer, device_id_type=MESH))
        for cp in copies:
            cp.start()
        for cp in copies:
            cp.wait()

    return pl.pallas_call(
        body, name="gather_small",
        out_shape=jax.ShapeDtypeStruct((N_DEV,) + small.shape, F32),
        in_specs=[ANY], out_specs=ANY,
        scratch_shapes=_sems(N_DEV - 1) + [pltpu.SemaphoreType.DMA],
    )(small)


def sum_parts(parts):
    n = len(parts)

    def body(*refs):
        for i in range(n):
            p_ref = refs[i]
            refs[n + i][...] = ((p_ref[0].astype(F32) + p_ref[1].astype(F32)) + p_ref[2].astype(F32)
                                ) + p_ref[3].astype(F32)

    def rows(p):
        return p.shape[1] // RED_GRID

    return pl.pallas_call(
        body, name="sum_parts",
        out_shape=tuple(jax.ShapeDtypeStruct(p.shape[1:], F32) for p in parts),
        grid=(RED_GRID,),
        in_specs=[pl.BlockSpec((N_CHIPS, rows(p), p.shape[2]), lambda b: (0, b, 0)) for p in parts],
        out_specs=tuple(pl.BlockSpec((rows(p), p.shape[2]), lambda b: (b, 0)) for p in parts),
        compiler_params=_params(("parallel",)),
    )(*parts)


def swap_halves(reds):
    n = len(reds)

    def body(*refs):
        ins, outs = refs[:n], refs[n:2 * n]
        ssem, rsem = refs[2 * n:]
        x, y, c = lax.axis_index("x"), lax.axis_index("y"), lax.axis_index("c")
        copies = [pltpu.make_async_remote_copy(
            src_ref=ins[i], dst_ref=outs[i], send_sem=ssem.at[i], recv_sem=rsem.at[i],
            device_id=(x, y, 1 - c), device_id_type=MESH) for i in range(n)]
        for cp in copies:
            cp.start()
        for cp in copies:
            cp.wait()

    return pl.pallas_call(
        body, name="swap_halves",
        out_shape=tuple(jax.ShapeDtypeStruct(r.shape, F32) for r in reds),
        in_specs=[ANY] * n, out_specs=(ANY,) * n, scratch_shapes=_sems(n),
    )(*reds)


def _adamw(w, g, m, v):
    m = ADAM_B1 * m + (1.0 - ADAM_B1) * g
    v = ADAM_B2 * v + (1.0 - ADAM_B2) * (g * g)
    m_hat = m / (1.0 - ADAM_B1 ** ADAM_STEP)
    v_hat = v / (1.0 - ADAM_B2 ** ADAM_STEP)
    delta = -ADAM_LR * (m_hat / (jnp.sqrt(v_hat) + ADAM_EPS) + ADAM_WD * w)
    return delta, m, v


def adamw_big(core, mine, theirs, ws, ms, vs):
    n = len(ws)
    per_half = RED_GRID // 2

    def body(c_ref, *refs):
        own = (pl.program_id(0) // per_half) == c_ref[0]
        for i in range(n):
            g = jnp.where(own, refs[i][...], refs[n + i][...])
            d, mn, vn = _adamw(refs[2 * n + i][...], g, refs[3 * n + i][...], refs[4 * n + i][...])
            refs[5 * n + i][...] = g
            refs[6 * n + i][...] = d
            refs[7 * n + i][...] = mn
            refs[8 * n + i][...] = vn

    def blk(w):
        return (w.shape[0] // RED_GRID, w.shape[1])

    halves = [pl.BlockSpec(blk(w), lambda b, c_ref: (b % per_half, 0)) for w in ws]
    whole = [pl.BlockSpec(blk(w), lambda b, c_ref: (b, 0)) for w in ws]
    shapes = [jax.ShapeDtypeStruct(w.shape, F32) for w in ws]
    grid_spec = pltpu.PrefetchScalarGridSpec(
        num_scalar_prefetch=1, grid=(RED_GRID,), in_specs=halves * 2 + whole * 3, out_specs=whole * 4)
    outs = pl.pallas_call(
        body, name="adamw_big", out_shape=tuple(shapes * 4), grid_spec=grid_spec,
        compiler_params=_params(("parallel",)),
    )(core, *mine, *theirs, *ws, *ms, *vs)
    return outs[:n], outs[n:2 * n], outs[2 * n:3 * n], outs[3 * n:]


def adamw_whole(g, w, m, v, name):
    def body(g_ref, w_ref, m_ref, v_ref, d_out, m_out, v_out):
        d, mn, vn = _adamw(w_ref[...], g_ref[...], m_ref[...], v_ref[...])
        d_out[...] = d
        m_out[...] = mn
        v_out[...] = vn

    shp = jax.ShapeDtypeStruct(g.shape, F32)
    return pl.pallas_call(body, name=name, out_shape=(shp,) * 3)(g, w, m, v)


def adamw_small(smalls, w, m, v):
    def body(s_ref, w_ref, m_ref, v_ref, g_out, d_out, m_out, v_out):
        g = s_ref[0]
        for k in range(1, N_DEV):
            g = g + s_ref[k]
        d, mn, vn = _adamw(w_ref[...], g, m_ref[...], v_ref[...])
        g_out[...] = g
        d_out[...] = d
        m_out[...] = mn
        v_out[...] = vn

    shp = jax.ShapeDtypeStruct((SMALL_ROWS, LANES), F32)
    return pl.pallas_call(body, name="adamw_small", out_shape=(shp,) * 4)(smalls, w, m, v)


def rms_prenorm(x, g):
    s = x.shape[0]
    tm = _blk(s, 512)

    def body(x_ref, g_ref, u_ref):
        xv = x_ref[...]
        r = lax.rsqrt(jnp.mean(xv * xv, axis=-1, keepdims=True) + EPS)
        u_ref[...] = (xv * r * g_ref[...]).astype(BF16)

    return pl.pallas_call(
        body, name="rms_prenorm", out_shape=jax.ShapeDtypeStruct(x.shape, BF16), grid=(s // tm,),
        in_specs=[pl.BlockSpec((tm, D_MODEL), lambda i: (i, 0)), _const_spec((1, D_MODEL))],
        out_specs=pl.BlockSpec((tm, D_MODEL), lambda i: (i, 0)), compiler_params=_params(("parallel",)),
    )(x, g)


def matmul_rows(a, w, out_dtype, name):
    s, k = a.shape
    n = w.shape[1]
    tm = _blk(s, 512)

    def body(a_ref, w_ref, o_ref):
        o_ref[...] = _mm(a_ref[...], w_ref[...]).astype(out_dtype)

    return pl.pallas_call(
        body, name=name, out_shape=jax.ShapeDtypeStruct((s, n), out_dtype), grid=(s // tm,),
        in_specs=[pl.BlockSpec((tm, k), lambda i: (i, 0)), _const_spec((k, n))],
        out_specs=pl.BlockSpec((tm, n), lambda i: (i, 0)), compiler_params=_params(("parallel",)),
    )(a, w)


def matmul_tn(a, b, name):
    s, m = a.shape
    n = b.shape[1]
    tk = _blk(s, 2048)
    tn = _blk(n, 512)

    def body(a_ref, b_ref, o_ref):
        @pl.when(pl.program_id(1) == 0)
        def _():
            o_ref[...] = jnp.zeros_like(o_ref)

        o_ref[...] += _mm_tn(a_ref[...], b_ref[...])

    return pl.pallas_call(
        body, name=name, out_shape=jax.ShapeDtypeStruct((m, n), F32), grid=(n // tn, s // tk),
        in_specs=[pl.BlockSpec((tk, m), lambda j, i: (i, 0)), pl.BlockSpec((tk, tn), lambda j, i: (i, j))],
        out_specs=pl.BlockSpec((m, tn), lambda j, i: (0, j)),
        compiler_params=_params(("parallel", "arbitrary")),
    )(a, b)


def conv_fwd(xbc, w, b):
    s = xbc.shape[0]
    tm = _blk(s, 256)

    def body(x_ref, t_ref, w_ref, b_ref, pre_ref, act_ref):
        i = pl.program_id(0)
        cur = x_ref[...]
        tail = jnp.where(i > 0, t_ref[...], 0.0)
        wv = w_ref[...]
        acc = cur * wv[3:4, :] + b_ref[...]
        head = cur[0:8, :] * wv[3:4, :] + b_ref[...]
        row8 = _iota((8, CONV_CH), 0)
        for sh in range(1, CONV_WIDTH):
            wk = wv[3 - sh:4 - sh, :]
            acc = acc + pltpu.roll(cur, sh, 0) * wk
            first = jnp.where(row8 < sh, pltpu.roll(tail, sh, 0), pltpu.roll(cur[0:8, :], sh, 0))
            head = head + first * wk
        pre_ref[...] = acc
        act_ref[...] = acc * _sigmoid(acc)
        pre_ref[0:8, :] = head
        act_ref[0:8, :] = head * _sigmoid(head)

    shp = jax.ShapeDtypeStruct(xbc.shape, F32)
    rows = pl.BlockSpec((tm, CONV_CH), lambda i: (i, 0))
    return pl.pallas_call(
        body, name="conv_fwd", out_shape=(shp, shp), grid=(s // tm,),
        in_specs=[rows, pl.BlockSpec((8, CONV_CH), lambda i: (jnp.maximum(i * (tm // 8) - 1, 0), 0)),
                  _const_spec((CONV_WIDTH, CONV_CH)), _const_spec((1, CONV_CH))],
        out_specs=(rows, rows), compiler_params=_params(("parallel",)),
    )(xbc, xbc, w, b)


def conv_bwd(xbc, pre, dact, w):
    s = xbc.shape[0]
    tm = _blk(s, 256)
    nb = s // tm

    def dsilu(p):
        sg = _sigmoid(p)
        return sg * (1.0 + p * (1.0 - sg))

    def body(x_ref, xt_ref, p_ref, pn_ref, d_ref, dn_ref, w_ref, dx_ref, dw_ref, db_ref):
        i = pl.program_id(0)

        @pl.when(i == 0)
        def _():
            dw_ref[...] = jnp.zeros_like(dw_ref)
            db_ref[...] = jnp.zeros_like(db_ref)

        wv = w_ref[...]
        dpre = d_ref[...] * dsilu(p_ref[...])
        dnext = jnp.where(i < nb - 1, dn_ref[...] * dsilu(pn_ref[...]), 0.0)
        cur = x_ref[...]
        tail = jnp.where(i > 0, xt_ref[...], 0.0)
        row8 = _iota((8, CONV_CH), 0)
        dx = dpre * wv[3:4, :]
        last = dpre[tm - 8:tm, :] * wv[3:4, :]
        db_ref[...] += jnp.sum(dpre, axis=0, keepdims=True)
        dws = [jnp.sum(dpre * cur, axis=0, keepdims=True)]
        for sh in range(1, CONV_WIDTH):
            wk = wv[3 - sh:4 - sh, :]
            dx = dx + pltpu.roll(dpre, tm - sh, 0) * wk
            nxt = jnp.where(row8 >= 8 - sh, pltpu.roll(dnext, 8 - sh, 0), pltpu.roll(dpre[tm - 8:tm, :], 8 - sh, 0))
            last = last + nxt * wk
            xs = pltpu.roll(cur, sh, 0)
            first = jnp.where(row8 < sh, pltpu.roll(tail, sh, 0), xs[0:8, :])
            dws.append(jnp.sum(dpre * xs, axis=0, keepdims=True)
                       + jnp.sum(dpre[0:8, :] * (first - xs[0:8, :]), axis=0, keepdims=True))
        dx_ref[...] = dx.astype(BF16)
        dx_ref[tm - 8:tm, :] = last.astype(BF16)
        for sh in range(CONV_WIDTH):
            dw_ref[3 - sh:4 - sh, :] += dws[sh]

    rows = pl.BlockSpec((tm, CONV_CH), lambda i: (i, 0))
    prev8 = pl.BlockSpec((8, CONV_CH), lambda i: (jnp.maximum(i * (tm // 8) - 1, 0), 0))
    next8 = pl.BlockSpec((8, CONV_CH), lambda i: (jnp.minimum((i + 1) * (tm // 8), s // 8 - 1), 0))
    return pl.pallas_call(
        body, name="conv_bwd",
        out_shape=(jax.ShapeDtypeStruct(xbc.shape, BF16), jax.ShapeDtypeStruct((8, CONV_CH), F32),
                   jax.ShapeDtypeStruct((1, CONV_CH), F32)),
        grid=(nb,),
        in_specs=[rows, prev8, rows, next8, rows, next8, _const_spec((CONV_WIDTH, CONV_CH))],
        out_specs=(rows, _const_spec((8, CONV_CH)), _const_spec((1, CONV_CH))),
        compiler_params=_params(("arbitrary",)),
    )(xbc, xbc, pre, pre, dact, dact, w)


def _pair_lanes(mat, j, lane):
    return jnp.where(lane < HEAD_DIM, mat[:, 2 * j:2 * j + 1], mat[:, 2 * j + 1:2 * j + 2])


def _ssd_chunk_prelude(sm, dtb, a_row, lane, sub):
    raw = sm + dtb
    head_lane = lane < N_HEADS
    dt = jnp.where(head_lane, _softplus(raw), 0.0)
    sig = jnp.where(head_lane, _sigmoid(raw), 0.0)
    tri = (lane <= sub).astype(F32)
    acs = _mm_exact(tri, dt * a_row)
    return dt, sig, acs, acs.T


GROUP_WIDTH = SSD_WIDTH // N_GROUPS
HEADS_PER_GROUP = N_HEADS // N_GROUPS


def _expand_group(mat, g, lane):
    return jnp.concatenate([_pair_lanes(mat, j, lane) for j in range(4 * g, 4 * g + 4)], axis=1)


def _head_sums(q, g):
    row = _iota((GROUP_WIDTH, LANES), 0)
    seg = (_iota((GROUP_WIDTH, LANES), 1) == HEADS_PER_GROUP * g + (row >> 6)).astype(BF16)
    hi = q.astype(BF16)
    lo = (q - hi.astype(F32)).astype(BF16)
    return _mm(hi, seg) + _mm(lo, seg)


def _rows_from_lanes(row512):
    return jnp.broadcast_to(row512, (LANES, GROUP_WIDTH)).T


def ssd_fwd(xc, small, dtb_row, a_row, dskip_lane):
    s = xc.shape[0]
    nc = s // CHUNK

    def body(xc_ref, sm_ref, dtb_ref, a_ref, dsk_ref, y_ref, hs_ref, h_scr):
        c = pl.program_id(0)

        @pl.when(c == 0)
        def _():
            h_scr[...] = jnp.zeros_like(h_scr)

        lane = _iota((CHUNK, LANES), 1)
        sub = _iota((CHUNK, LANES), 0)
        causal = lane <= sub
        dt, _, acs, acs_t = _ssd_chunk_prelude(sm_ref[...], dtb_ref[...], a_ref[...], lane, sub)
        for g in range(N_GROUPS):
            cols = slice(GROUP_WIDTH * g, GROUP_WIDTH * (g + 1))
            b_off = SSD_WIDTH + D_STATE * g
            c_off = SSD_WIDTH + N_GROUPS * D_STATE + D_STATE * g
            b_b = xc_ref[:, b_off:b_off + D_STATE].astype(BF16)
            c_b = xc_ref[:, c_off:c_off + D_STATE].astype(BF16)
            cb = _mm_nt(c_b, b_b)
            x_g = xc_ref[:, cols]
            acs_g = _expand_group(acs, g, lane)
            xdt_g = x_g * _expand_group(dt, g, lane)
            xdt_b = xdt_g.astype(BF16)
            heads = range(HEADS_PER_GROUP * g, HEADS_PER_GROUP * (g + 1))
            m_b = [(cb * jnp.exp(jnp.where(causal, acs[:, h:h + 1] - acs_t[h:h + 1, :], NEG_BIG))).astype(BF16)
                   for h in heads]
            yd = [_mm(m_b[k], xdt_b[:, LANES * (k // 2):LANES * (k // 2 + 1)]) for k in range(HEADS_PER_GROUP)]
            yd_g = jnp.concatenate([jnp.where(lane < HEAD_DIM, yd[2 * k], yd[2 * k + 1]) for k in range(4)], axis=1)
            h_g = h_scr[g]
            t_g = _mm_nt(c_b, h_g.astype(BF16))
            y_ref[:, cols] = yd_g + jnp.exp(acs_g) * t_g + dsk_ref[:, cols] * x_g
            hs_ref[0, g] = h_g
            last_g = acs_g[CHUNK - 1:CHUNK, :]
            w_b = (xdt_g * jnp.exp(last_g - acs_g)).astype(BF16)
            h_scr[g] = h_g * jnp.exp(_rows_from_lanes(last_g)) + _mm_tn(w_b, b_b)

    return pl.pallas_call(
        body, name="ssd_fwd",
        out_shape=(jax.ShapeDtypeStruct((s, SSD_WIDTH), F32),
                   jax.ShapeDtypeStruct((nc, N_GROUPS, GROUP_WIDTH, D_STATE), F32)),
        grid=(nc,),
        in_specs=[pl.BlockSpec((CHUNK, CONV_CH), lambda c: (c, 0)), pl.BlockSpec((CHUNK, LANES), lambda c: (c, 0)),
                  _const_spec((1, LANES)), _const_spec((1, LANES)), _const_spec((1, SSD_WIDTH))],
        out_specs=(pl.BlockSpec((CHUNK, SSD_WIDTH), lambda c: (c, 0)),
                   pl.BlockSpec((1, N_GROUPS, GROUP_WIDTH, D_STATE), lambda c: (c, 0, 0, 0))),
        scratch_shapes=[pltpu.VMEM((N_GROUPS, GROUP_WIDTH, D_STATE), F32)],
        compiler_params=_params(("arbitrary",)),
    )(xc, small, dtb_row, a_row, dskip_lane)


def ssd_bwd(xc, small, states, dy, dtb_row, a_row, dskip_lane):
    s = xc.shape[0]
    nc = s // CHUNK
    rev = lambda c: nc - 1 - c

    def body(xc_ref, sm_ref, hs_ref, dy_ref, dtb_ref, a_ref, dsk_ref,
             dxc_ref, ddt_ref, da_ref, ddtb_ref, ddsk_ref, dh_scr):
        c = pl.program_id(0)

        @pl.when(c == 0)
        def _():
            dh_scr[...] = jnp.zeros_like(dh_scr)
            da_ref[...] = jnp.zeros_like(da_ref)
            ddtb_ref[...] = jnp.zeros_like(ddtb_ref)
            ddsk_ref[...] = jnp.zeros_like(ddsk_ref)

        lane = _iota((CHUNK, LANES), 1)
        sub = _iota((CHUNK, LANES), 0)
        causal = lane <= sub
        upper = lane >= sub
        is_last = sub == CHUNK - 1
        a_row_v = a_ref[...]
        dt, sig, acs, acs_t = _ssd_chunk_prelude(sm_ref[...], dtb_ref[...], a_row_v, lane, sub)
        cd = jnp.exp(acs[CHUNK - 1:CHUNK, :])
        dacs_c = jnp.zeros((CHUNK, LANES), F32)
        dacs_r = jnp.zeros((LANES, CHUNK), F32)
        ddtx = jnp.zeros((CHUNK, LANES), F32)
        for g in range(N_GROUPS):
            cols = slice(GROUP_WIDTH * g, GROUP_WIDTH * (g + 1))
            b_off = SSD_WIDTH + D_STATE * g
            c_off = SSD_WIDTH + N_GROUPS * D_STATE + D_STATE * g
            b_b = xc_ref[:, b_off:b_off + D_STATE].astype(BF16)
            c_b = xc_ref[:, c_off:c_off + D_STATE].astype(BF16)
            cb = _mm_nt(c_b, b_b)
            cb_t = _mm_nt(b_b, c_b)
            x_g = xc_ref[:, cols]
            dy_g = dy_ref[:, cols]
            dt_g = _expand_group(dt, g, lane)
            acs_g = _expand_group(acs, g, lane)
            last_g = acs_g[CHUNK - 1:CHUNK, :]
            e_g = jnp.exp(acs_g)
            dte_g = jnp.exp(last_g - acs_g)
            xdt_g = x_g * dt_g
            xdt_b = xdt_g.astype(BF16)
            h_g = hs_ref[0, g]
            dh_g = dh_scr[g]
            h_b = h_g.astype(BF16)
            dh_b = dh_g.astype(BF16)
            heads = list(range(HEADS_PER_GROUP * g, HEADS_PER_GROUP * (g + 1)))
            segs = [acs[:, h:h + 1] - acs_t[h:h + 1, :] for h in heads]
            lms = [jnp.exp(jnp.where(causal, sg, NEG_BIG)) for sg in segs]
            mts = [(cb_t * jnp.exp(jnp.where(upper, -sg, NEG_BIG))).astype(BF16) for sg in segs]
            dyh = []
            for k in range(HEADS_PER_GROUP):
                blk = dy_g[:, LANES * (k // 2):LANES * (k // 2 + 1)]
                in_head = (lane < HEAD_DIM) if k % 2 == 0 else (lane >= HEAD_DIM)
                dyh.append(jnp.where(in_head, blk, 0.0).astype(BF16))
            dms = [_mm_nt(dyh[k], xdt_b[:, LANES * (k // 2):LANES * (k // 2 + 1)]) for k in range(HEADS_PER_GROUP)]
            dxs = [_mm(mts[k], dyh[k]) for k in range(HEADS_PER_GROUP)]
            dcb = jnp.zeros((CHUNK, CHUNK), F32)
            for k, h in enumerate(heads):
                gmat = dms[k] * (cb * lms[k])
                dacs_c = dacs_c + jnp.where(lane == h, jnp.sum(gmat, axis=1, keepdims=True), 0.0)
                dacs_r = dacs_r - jnp.where(sub == h, jnp.sum(gmat, axis=0, keepdims=True), 0.0)
                dcb = dcb + dms[k] * lms[k]
            dxdt_g = jnp.concatenate([dxs[2 * k] + dxs[2 * k + 1] for k in range(4)], axis=1)
            t_g = _mm_nt(c_b, h_b)
            dacs_c = dacs_c + _head_sums(dy_g * e_g * t_g, g)
            dt_b = (dy_g * e_g).astype(BF16)
            dc_acc = _mm(dt_b, h_b)
            dh_prev = _mm_tn(dt_b, c_b)
            dw_g = _mm_nt(b_b, dh_b)
            w_g = xdt_g * dte_g
            dxdt_g = dxdt_g + dw_g * dte_g
            db_acc = _mm(w_g.astype(BF16), dh_b)
            r2 = _head_sums(dw_g * w_g, g)
            dacs_c = dacs_c + jnp.where(is_last, jnp.sum(r2, axis=0, keepdims=True), 0.0) - r2
            q3 = jnp.sum(dh_g * h_g, axis=1, keepdims=True)
            for k, h in enumerate(heads):
                tot = jnp.sum(q3[HEAD_DIM * k:HEAD_DIM * (k + 1), :], keepdims=True) * cd[:, h:h + 1]
                dacs_c = dacs_c + jnp.where(is_last & (lane == h), tot, 0.0)
            dh_scr[g] = dh_prev + dh_g * jnp.exp(_rows_from_lanes(last_g))
            dxc_ref[:, cols] = dxdt_g * dt_g + dsk_ref[:, cols] * dy_g
            ddtx = ddtx + _head_sums(dxdt_g * x_g, g)
            ddsk_ref[:, cols] += jnp.sum(dy_g * x_g, axis=0, keepdims=True)
            dxc_ref[:, b_off:b_off + D_STATE] = db_acc + _mm(dcb.T.astype(BF16), c_b)
            dxc_ref[:, c_off:c_off + D_STATE] = dc_acc + _mm(dcb.astype(BF16), b_b)
        dacs = dacs_c + dacs_r.T
        dadt = _mm_exact((lane >= sub).astype(F32), dacs)
        ddt = dadt * a_row_v + ddtx
        ddt_raw = ddt * sig
        ddt_ref[...] = ddt_raw
        da_ref[...] += jnp.sum(dadt * dt, axis=0, keepdims=True)
        ddtb_ref[...] += jnp.sum(ddt_raw, axis=0, keepdims=True)

    return pl.pallas_call(
        body, name="ssd_bwd",
        out_shape=(jax.ShapeDtypeStruct((s, CONV_CH), F32), jax.ShapeDtypeStruct((s, LANES), F32),
                   jax.ShapeDtypeStruct((1, LANES), F32), jax.ShapeDtypeStruct((1, LANES), F32),
                   jax.ShapeDtypeStruct((1, SSD_WIDTH), F32)),
        grid=(nc,),
        in_specs=[pl.BlockSpec((CHUNK, CONV_CH), lambda c: (rev(c), 0)),
                  pl.BlockSpec((CHUNK, LANES), lambda c: (rev(c), 0)),
                  pl.BlockSpec((1, N_GROUPS, GROUP_WIDTH, D_STATE), lambda c: (rev(c), 0, 0, 0)),
                  pl.BlockSpec((CHUNK, SSD_WIDTH), lambda c: (rev(c), 0)),
                  _const_spec((1, LANES)), _const_spec((1, LANES)), _const_spec((1, SSD_WIDTH))],
        out_specs=(pl.BlockSpec((CHUNK, CONV_CH), lambda c: (rev(c), 0)),
                   pl.BlockSpec((CHUNK, LANES), lambda c: (rev(c), 0)),
                   _const_spec((1, LANES)), _const_spec((1, LANES)), _const_spec((1, SSD_WIDTH))),
        scratch_shapes=[pltpu.VMEM((N_GROUPS, GROUP_WIDTH, D_STATE), F32)],
        compiler_params=_params(("arbitrary",)),
    )(xc, small, states, dy, dtb_row, a_row, dskip_lane)


FORGET_BLOCK = 512


def forget_cumsum(small, fgb_row):
    s = small.shape[0]
    t = _blk(s, FORGET_BLOCK)
    nb = s // t

    def body(sm_ref, b_ref, cc_ref, carry):
        i = pl.program_id(0)

        @pl.when(i == 0)
        def _():
            carry[...] = jnp.zeros_like(carry)

        lane = _iota((t, LANES), 1)
        in_f = (lane >= N_HEADS) & (lane < 2 * N_HEADS)
        logf = jnp.where(in_f, -_softplus(-(sm_ref[...] + b_ref[...])), 0.0)
        tri = (_iota((t, t), 1) <= _iota((t, t), 0)).astype(F32)
        cum = _mm_exact(tri, logf) + carry[0:1, :]
        cc_ref[...] = cum
        carry[...] = jnp.broadcast_to(cum[t - 1:t, :], (8, LANES))

    return pl.pallas_call(
        body, name="forget_cumsum",
        out_shape=jax.ShapeDtypeStruct((s, LANES), F32),
        grid=(nb,),
        in_specs=[pl.BlockSpec((t, LANES), lambda i: (i, 0)), _const_spec((1, LANES))],
        out_specs=pl.BlockSpec((t, LANES), lambda i: (i, 0)),
        scratch_shapes=[pltpu.VMEM((8, LANES), F32)],
        compiler_params=_params(("arbitrary",)),
    )(small, fgb_row)


def forget_bwd(dc, small, ddt_raw, fgb_row):
    s = small.shape[0]
    t = _blk(s, FORGET_BLOCK)
    nb = s // t
    rev = lambda i: nb - 1 - i

    def body(dc_ref, sm_ref, ddt_ref, b_ref, ds_ref, dfb_ref, carry):
        i = pl.program_id(0)

        @pl.when(i == 0)
        def _():
            carry[...] = jnp.zeros_like(carry)
            dfb_ref[...] = jnp.zeros_like(dfb_ref)

        lane = _iota((t, LANES), 1)
        rows = dc_ref[...].T
        tri = (_iota((t, t), 1) <= _iota((t, t), 0)).astype(F32)
        rc = _mm_exact(rows, tri) + carry[:, 0:1]
        carry[...] = jnp.broadcast_to(rc[:, 0:1], (LANES, LANES))
        in_f = (lane >= N_HEADS) & (lane < 2 * N_HEADS)
        df = jnp.where(in_f, rc.T * _sigmoid(-(sm_ref[...] + b_ref[...])), 0.0)
        ds_ref[...] = (df + ddt_ref[...]).astype(BF16)
        dfb_ref[...] += jnp.sum(df, axis=0, keepdims=True)

    blk = pl.BlockSpec((t, LANES), lambda i: (rev(i), 0))
    return pl.pallas_call(
        body, name="forget_bwd",
        out_shape=(jax.ShapeDtypeStruct((s, LANES), BF16), jax.ShapeDtypeStruct((1, LANES), F32)),
        grid=(nb,),
        in_specs=[blk, blk, blk, _const_spec((1, LANES))],
        out_specs=(blk, _const_spec((1, LANES))),
        scratch_shapes=[pltpu.VMEM((LANES, LANES), F32)],
        compiler_params=_params(("arbitrary",)),
    )(dc, small, ddt_raw, fgb_row)


ATT_BLOCK = 1024
ATT_BLOCK_BWD = 512
ATT_BLOCK_BWD_Q = 1024
ATT_SCALE = HEAD_DIM ** -0.5
AUG_A = HEAD_DIM
AUG_B = HEAD_DIM + 3


def _split3(c):
    hi = c.astype(BF16).astype(F32)
    r = c - hi
    mid = r.astype(BF16).astype(F32)
    return hi, mid, (r - mid).astype(BF16).astype(F32)


def _aug(lane, first, parts=None, value=1.0):
    if parts is None:
        return jnp.where((lane >= first) & (lane < first + 3), value, 0.0)
    return (jnp.where(lane == first, parts[0], 0.0) + jnp.where(lane == first + 1, parts[1], 0.0)
            + jnp.where(lane == first + 2, parts[2], 0.0))


def _pack_pair(a0, a1, lane):
    return jnp.where(lane < HEAD_DIM, a0, pltpu.roll(a1, HEAD_DIM, 1))


def proj_qkv_heads(u, w_q, w_k, w_v, cum):
    s = u.shape[0]
    tm = _blk(s, 256)

    def body(u_ref, wq_ref, wk_ref, wv_ref, c_ref, qa_ref, ka_ref, va_ref, nrm_ref):
        lane = _iota((tm, LANES), 1)
        lo = lane < HEAD_DIM
        uv = u_ref[...]
        qf = _mm(uv, wq_ref[...]) * ATT_SCALE
        kf = _mm(uv, wk_ref[...])
        vf = _mm(uv, wv_ref[...])
        cc = c_ref[...]
        ones_a = _aug(lane, AUG_A)
        ones_b = _aug(lane, AUG_B)
        sub8 = _iota((8, LANES), 0)
        nrm = jnp.zeros((8, LANES), F32)
        for h in range(N_HEADS):
            j, e = divmod(h, 2)

            def head(full):
                blk = full[:, LANES * j:LANES * (j + 1)]
                if e == 1:
                    blk = pltpu.roll(blk, HEAD_DIM, 1)
                return jnp.where(lo, blk, 0.0)

            parts = _split3(cc[:, N_HEADS + h:N_HEADS + h + 1])
            qh, kh = head(qf), head(kf)
            qa_ref[h] = (qh + _aug(lane, AUG_A, parts) + ones_b).astype(BF16)
            ka_ref[h] = (kh + ones_a - _aug(lane, AUG_B, parts)).astype(BF16)
            va_ref[h] = (head(vf) + ones_a).astype(BF16)
        seg = (_iota((ATT_WIDTH, LANES), 1) == (_iota((ATT_WIDTH, LANES), 0) >> 6)).astype(BF16)
        for r, val in enumerate((qf, kf)):
            sq = val * val
            hi = sq.astype(BF16)
            tot = _mm(hi, seg) + _mm((sq - hi.astype(F32)).astype(BF16), seg)
            nrm = nrm + jnp.where(sub8 == r, jnp.max(tot, axis=0, keepdims=True), 0.0)
        nrm_ref[0] = nrm

    shp = jax.ShapeDtypeStruct((N_HEADS, s, LANES), BF16)
    hspec = pl.BlockSpec((N_HEADS, tm, LANES), lambda i: (0, i, 0))
    wspec = _const_spec((D_MODEL, ATT_WIDTH))
    return pl.pallas_call(
        body, name="proj_qkv_heads",
        out_shape=(shp, shp, shp, jax.ShapeDtypeStruct((s // tm, 8, LANES), F32)), grid=(s // tm,),
        in_specs=[pl.BlockSpec((tm, D_MODEL), lambda i: (i, 0)), wspec, wspec, wspec,
                  pl.BlockSpec((tm, LANES), lambda i: (i, 0))],
        out_specs=(hspec, hspec, hspec, pl.BlockSpec((1, 8, LANES), lambda i: (i, 0, 0))),
        compiler_params=_params(("parallel",)),
    )(u, w_q, w_k, w_v, cum)


SKIP_BELOW = -110.0


def live_blocks(norms, cum, tq, tk):
    qn = jnp.sqrt(jnp.max(norms[:, 0, :N_HEADS], axis=0))
    kn = jnp.sqrt(jnp.max(norms[:, 1, :N_HEADS], axis=0))
    bound = 2.05 * qn * kn + 2.0
    c_first = cum[0::tq, N_HEADS:2 * N_HEADS]
    c_last = cum[tk - 1::tk, N_HEADS:2 * N_HEADS]
    nq, nk = c_first.shape[0], c_last.shape[0]
    top = bound[None, None, :] + c_first[:, None, :] - c_last[None, :, :]
    before = (jnp.arange(nk)[None, :] + 1) * tk <= jnp.arange(nq)[:, None] * tq
    dead = before[:, :, None] & ~(top >= SKIP_BELOW)
    first = jnp.sum(dead, axis=1).astype(jnp.int32).T
    last_q = jnp.sum(first[:, None, :] <= jnp.arange(nk)[None, :, None], axis=2).astype(jnp.int32) - 1
    return first, last_q


def attention_fwd(first, qa, ka, va):
    s = qa.shape[1]
    t = _blk(s, ATT_BLOCK)
    nq = s // t

    def body(first_ref, qa_ref, ka_ref, va_ref, o_ref, qb_ref, m_scr, acc_scr, alpha_scr, p_scr, s_scr):
        qi = pl.program_id(1)
        starts = [first_ref[2 * pl.program_id(0) + e, qi] for e in range(2)]
        k0 = jnp.maximum(starts[0], starts[1])
        m_scr[...] = jnp.full_like(m_scr, NEG_BIG)
        acc_scr[...] = jnp.zeros_like(acc_scr)

        def kv_rows(kb):
            return pl.ds(pl.multiple_of(kb * t, t), t)

        def logits(kb, masked, heads=(0, 1)):
            for e in heads:
                sc = _mm_nt(qa_ref[e], ka_ref[e, kv_rows(kb), :])
                if masked:
                    sc = jnp.where(_iota((t, t), 0) >= _iota((t, t), 1), sc, NEG_BIG)
                s_scr[e] = sc

        def probs(heads=(0, 1)):
            for e in heads:
                cmax = s_scr[e, :, 0:LANES]
                for c in range(1, t // LANES):
                    cmax = jnp.maximum(cmax, s_scr[e, :, LANES * c:LANES * (c + 1)])
                m_old = m_scr[e]
                m_new = jnp.maximum(m_old, jnp.max(cmax, axis=1, keepdims=True))
                alpha_scr[e] = jnp.exp(m_old - m_new)
                m_scr[e] = m_new
                for c in range(t // LANES):
                    cols = slice(LANES * c, LANES * (c + 1))
                    p_scr[e, :, cols] = jnp.exp(s_scr[e, :, cols] - m_new).astype(BF16)

        def accumulate(kb, heads=(0, 1)):
            for e in heads:
                acc_scr[e] = alpha_scr[e] * acc_scr[e] + _mm(p_scr[e], va_ref[e, kv_rows(kb), :])

        for e in range(2):
            def alone(kb, carry, e=e):
                logits(kb, False, (e,))
                probs((e,))
                accumulate(kb, (e,))
                return carry

            lax.fori_loop(starts[e], k0, alone, 0)

        def loop_body(kb, carry):
            logits(kb, False)
            for e in range(2):
                accumulate(kb - 1, (e,))
                probs((e,))
            return carry

        @pl.when(qi > k0)
        def _():
            logits(k0, False)
            probs()

        lax.fori_loop(k0 + 1, qi, loop_body, 0)

        @pl.when(qi > k0)
        def _():
            logits(qi, True)
            accumulate(qi - 1)
            probs()

        @pl.when(qi == k0)
        def _():
            logits(qi, True)
            probs()

        accumulate(qi)

        lane = _iota((t, LANES), 1)
        outs = []
        for e in range(2):
            acc = acc_scr[e]
            l = acc[:, AUG_A:AUG_A + 1]
            outs.append(acc / l)
            lse = m_scr[e][:, 0:1] + jnp.log(l)
            q32 = qa_ref[e].astype(F32)
            c = q32[:, AUG_A:AUG_A + 1] + q32[:, AUG_A + 1:AUG_A + 2] + q32[:, AUG_A + 2:AUG_A + 3]
            qb = jnp.where(lane < HEAD_DIM, q32, 0.0) + _aug(lane, AUG_A, _split3(c - lse)) + _aug(lane, AUG_B)
            qb_ref[e] = qb.astype(BF16)
        o_ref[...] = _pack_pair(outs[0], outs[1], lane)

    grid_spec = pltpu.PrefetchScalarGridSpec(
        num_scalar_prefetch=1, grid=(N_PAIRS, nq),
        in_specs=[pl.BlockSpec((2, t, LANES), lambda j, qi, f: (j, qi, 0)),
                  pl.BlockSpec((2, s, LANES), lambda j, qi, f: (j, 0, 0)),
                  pl.BlockSpec((2, s, LANES), lambda j, qi, f: (j, 0, 0))],
        out_specs=[pl.BlockSpec((t, LANES), lambda j, qi, f: (qi, j)),
                   pl.BlockSpec((2, t, LANES), lambda j, qi, f: (j, qi, 0))],
        scratch_shapes=[pltpu.VMEM((2, t, LANES), F32), pltpu.VMEM((2, t, LANES), F32),
                        pltpu.VMEM((2, t, LANES), F32), pltpu.VMEM((2, t, t), BF16), pltpu.VMEM((2, t, t), F32)])
    return pl.pallas_call(
        body, name="attention_fwd", grid_spec=grid_spec,
        out_shape=(jax.ShapeDtypeStruct((s, ATT_WIDTH), F32), jax.ShapeDtypeStruct((N_HEADS, s, LANES), BF16)),
        compiler_params=_params(("parallel", "parallel")),
    )(first, qa, ka, va)


def attention_bwd(last_q, qb, ka, va, dob):
    s = qb.shape[1]
    t = _blk(s, ATT_BLOCK_BWD)
    tq = _blk(s, ATT_BLOCK_BWD_Q)
    nq = s // tq
    per_q = tq // t

    def body(last_ref, qb_ref, dob_ref, ka_ref, va_ref, dq_ref, dk_ref, dv_ref, dc_ref, dq_scr, dk_scr, dv_scr):
        j, ki = pl.program_id(0), pl.program_id(1)

        @pl.when((j == 0) & (ki == 0))
        def _():
            dc_ref[...] = jnp.zeros_like(dc_ref)

        @pl.when(ki == 0)
        def _():
            dq_scr[...] = jnp.zeros_like(dq_scr)

        dk_scr[...] = jnp.zeros_like(dk_scr)
        dv_scr[...] = jnp.zeros_like(dv_scr)

        def q_step(qblk, masked, heads=(0, 1)):
            rows = pl.ds(pl.multiple_of(qblk * tq, tq), tq)
            scs = [_mm_nt(qb_ref[e, rows, :], ka_ref[e]) for e in heads]
            dps = [_mm_nt(dob_ref[e, rows, :], va_ref[e]) for e in heads]
            for e, sc, dp in zip(heads, scs, dps):
                q = qb_ref[e, rows, :]
                do = dob_ref[e, rows, :]
                if masked:
                    keep = (_iota((tq, t), 0) - _iota((tq, t), 1)) >= ki * t - qblk * tq
                    sc = jnp.where(keep, sc, NEG_BIG)
                p = jnp.exp(sc)
                ds_b = (p * dp).astype(BF16)
                dv_scr[e] += _mm_tn(p.astype(BF16), do)
                dk_scr[e] += _mm_tn(ds_b, q)
                dq_scr[e, rows, :] += _mm(ds_b, ka_ref[e])

        def loop_body(qblk, carry):
            q_step(qblk, False)
            return carry

        ends = [last_ref[2 * j + e, ki] + 1 for e in range(2)]
        both = jnp.minimum(ends[0], ends[1])
        diag = ki // per_q
        q_step(diag, True)
        lax.fori_loop(diag + 1, both, loop_body, 0)
        for e in range(2):
            def alone(qblk, carry, e=e):
                q_step(qblk, False, (e,))
                return carry

            lax.fori_loop(both, ends[e], alone, 0)

        lane = _iota((t, LANES), 1)
        dk_ref[...] = _pack_pair(dk_scr[0], dk_scr[1], lane).astype(BF16)
        dv_ref[...] = _pack_pair(dv_scr[0], dv_scr[1], lane).astype(BF16)
        rows = pl.ds(pl.multiple_of(ki * t, t), t)
        dc_ref[rows, :] -= (jnp.where(lane == N_HEADS + 2 * j, dk_scr[0][:, AUG_B:AUG_B + 1], 0.0)
                            + jnp.where(lane == N_HEADS + 2 * j + 1, dk_scr[1][:, AUG_B:AUG_B + 1], 0.0))

        @pl.when(ki == s // t - 1)
        def _():
            for blk in range(s // t):
                rws = pl.ds(blk * t, t)
                d0 = dq_scr[0, rws, :]
                d1 = dq_scr[1, rws, :]
                dq_ref[rws, :] = (_pack_pair(d0, d1, lane) * ATT_SCALE).astype(BF16)
                dc_ref[rws, :] += (jnp.where(lane == N_HEADS + 2 * j, d0[:, AUG_A:AUG_A + 1], 0.0)
                                   + jnp.where(lane == N_HEADS + 2 * j + 1, d1[:, AUG_A:AUG_A + 1], 0.0))

    full = pl.BlockSpec((2, s, LANES), lambda j, ki, f: (j, 0, 0))
    blk = pl.BlockSpec((2, t, LANES), lambda j, ki, f: (j, ki, 0))
    pair = pl.BlockSpec((t, LANES), lambda j, ki, f: (ki, j))
    wide = jax.ShapeDtypeStruct((s, ATT_WIDTH), BF16)
    grid_spec = pltpu.PrefetchScalarGridSpec(
        num_scalar_prefetch=1, grid=(N_PAIRS, s // t),
        in_specs=[full, full, blk, blk],
        out_specs=[pl.BlockSpec((s, LANES), lambda j, ki, f: (0, j)), pair, pair,
                   pl.BlockSpec((s, LANES), lambda j, ki, f: (0, 0))],
        scratch_shapes=[pltpu.VMEM((2, s, LANES), F32), pltpu.VMEM((2, t, LANES), F32),
                        pltpu.VMEM((2, t, LANES), F32)])
    return pl.pallas_call(
        body, name="attention_bwd", grid_spec=grid_spec,
        out_shape=(wide, wide, wide, jax.ShapeDtypeStruct((s, LANES), F32)),
        compiler_params=_params(("arbitrary", "arbitrary")),
    )(last_q, qb, dob, ka, va)


def _dsilu(z, sg):
    return sg * (1.0 + z * (1.0 - sg))


def post_mix(x, y, zs, o, za, p, tgt, ssd_g, att_g_lane, ple_g, fin_g, w_out, w_gate, w_proj):
    s = x.shape[0]
    tm = _blk(s, 256)
    half = SSD_WIDTH // N_GROUPS

    def rms_bwd(dy, yn, r):
        return r * (dy - yn * jnp.mean(dy * yn, axis=-1, keepdims=True))

    def colsum(a):
        return jnp.sum(a, axis=0, keepdims=True)

    def body(x_ref, y_ref, zs_ref, o_ref, za_ref, p_ref, t_ref, sg_ref, ag_ref, pg_ref, fg_ref,
             wo_ref, wg_ref, wp_ref,
             dh1_ref, dy_ref, dzs_ref, dob_ref, dza_ref, ycat_ref, dh1b_ref, n2b_ref, dglb_ref, dppb_ref, pb_ref,
             loss_ref, dfin_ref, dple_ref, dssd_ref, datt_ref):
        @pl.when(pl.program_id(0) == 0)
        def _():
            for r in (loss_ref, dfin_ref, dple_ref, dssd_ref, datt_ref):
                r[...] = jnp.zeros_like(r)

        lane = _iota((tm, LANES), 1)
        lo = lane < HEAD_DIM
        zs = zs_ref[...]
        sz = _sigmoid(zs)
        yv = y_ref[...]
        ys = yv * (zs * sz)
        yn, rg = [], []
        for g in range(N_GROUPS):
            seg = ys[:, half * g:half * (g + 1)]
            r = lax.rsqrt(jnp.mean(seg * seg, axis=-1, keepdims=True) + EPS)
            yn.append(seg * r)
            rg.append(r)
            ycat_ref[:, half * g:half * (g + 1)] = (yn[g] * sg_ref[:, half * g:half * (g + 1)]).astype(BF16)
        za = za_ref[...]
        sza = _sigmoid(za)
        silu_za = za * sza
        on, ra = [], []
        for jb in range(N_PAIRS):
            blk = o_ref[:, LANES * jb:LANES * (jb + 1)]
            sq = blk * blk
            ms0 = jnp.sum(jnp.where(lo, sq, 0.0), axis=1, keepdims=True) * (1.0 / HEAD_DIM)
            ms1 = jnp.sum(jnp.where(lo, 0.0, sq), axis=1, keepdims=True) * (1.0 / HEAD_DIM)
            r = jnp.where(lo, lax.rsqrt(ms0 + EPS), lax.rsqrt(ms1 + EPS))
            on.append(blk * r)
            ra.append(r)
            an = on[jb] * ag_ref[:, LANES * jb:LANES * (jb + 1)]
            ycat_ref[:, SSD_WIDTH + LANES * jb:SSD_WIDTH + LANES * (jb + 1)] = (
                an * silu_za[:, LANES * jb:LANES * (jb + 1)]).astype(BF16)
        h1 = x_ref[...] + _mm(ycat_ref[...], wo_ref[...])
        r2 = lax.rsqrt(jnp.mean(h1 * h1, axis=-1, keepdims=True) + EPS)
        n2h = h1 * r2
        n2_b = (n2h * pg_ref[...]).astype(BF16)
        gate = _sigmoid(_mm(n2_b, wg_ref[...]))
        p_b = p_ref[...].astype(BF16)
        pp = _mm(p_b, wp_ref[...])
        h2 = h1 + gate * pp
        r3 = lax.rsqrt(jnp.mean(h2 * h2, axis=-1, keepdims=True) + EPS)
        n3 = h2 * r3
        diff = n3 * fg_ref[...] - t_ref[...]
        sq = colsum(diff * diff)
        part = sq[:, 0:LANES]
        for jb in range(1, D_MODEL // LANES):
            part = part + sq[:, LANES * jb:LANES * (jb + 1)]
        loss_ref[...] += part * (0.5 / D_MODEL)
        dout = diff * (1.0 / D_MODEL)
        dfin_ref[...] += colsum(dout * n3)
        dh2 = rms_bwd(dout * fg_ref[...], n3, r3)
        dgl = dh2 * pp * gate * (1.0 - gate)
        dgl_b = dgl.astype(BF16)
        dn2 = _mm_nt(dgl_b, wg_ref[...])
        dple_ref[...] += colsum(dn2 * n2h)
        dh1 = dh2 + rms_bwd(dn2 * pg_ref[...], n2h, r2)
        dh1_b = dh1.astype(BF16)
        dycat = _mm_nt(dh1_b, wo_ref[...])
        dh1_ref[...] = dh1
        dh1b_ref[...] = dh1_b
        n2b_ref[...] = n2_b
        dglb_ref[...] = dgl_b
        dppb_ref[...] = (dh2 * gate).astype(BF16)
        pb_ref[...] = p_b
        for g in range(N_GROUPS):
            cols = slice(half * g, half * (g + 1))
            dys_g = dycat[:, cols]
            dssd_ref[:, cols] += colsum(dys_g * yn[g])
            dys = rms_bwd(dys_g * sg_ref[:, cols], yn[g], rg[g])
            dy_ref[:, cols] = dys * (zs[:, cols] * sz[:, cols])
            dzs_ref[:, cols] = (dys * yv[:, cols] * _dsilu(zs[:, cols], sz[:, cols])).astype(BF16)
        for jb in range(N_PAIRS):
            cols = slice(LANES * jb, LANES * (jb + 1))
            dya = dycat[:, SSD_WIDTH + LANES * jb:SSD_WIDTH + LANES * (jb + 1)]
            ag = ag_ref[:, cols]
            dan = dya * silu_za[:, cols]
            dza_ref[:, cols] = (dya * (on[jb] * ag) * _dsilu(za[:, cols], sza[:, cols])).astype(BF16)
            datt_ref[:, cols] += colsum(dan * on[jb])
            don = dan * ag
            q = don * on[jb]
            m0 = jnp.sum(jnp.where(lo, q, 0.0), axis=1, keepdims=True) * (1.0 / HEAD_DIM)
            m1 = jnp.sum(jnp.where(lo, 0.0, q), axis=1, keepdims=True) * (1.0 / HEAD_DIM)
            do2 = ra[jb] * (don - on[jb] * jnp.where(lo, m0, m1))
            prod = do2 * o_ref[:, cols]
            for e in range(2):
                delta = jnp.sum(jnp.where(lo, prod, 0.0) if e == 0 else jnp.where(lo, 0.0, prod),
                                axis=1, keepdims=True)
                base = jnp.where(lo, do2 if e == 0 else pltpu.roll(do2, HEAD_DIM, 1), 0.0)
                dob_ref[2 * jb + e] = (base - _aug(lane, AUG_A, _split3(delta))).astype(BF16)

    def rows(n, dtype=None):
        return pl.BlockSpec((tm, n), lambda i: (i, 0))

    def out(n, dtype):
        return jax.ShapeDtypeStruct((s, n), dtype)

    vec = _const_spec((1, D_MODEL))
    vshape = jax.ShapeDtypeStruct((1, D_MODEL), F32)
    return pl.pallas_call(
        body, name="post_mix",
        out_shape=(out(D_MODEL, F32), out(SSD_WIDTH, F32), out(SSD_WIDTH, BF16),
                   jax.ShapeDtypeStruct((N_HEADS, s, LANES), BF16),
                   out(ATT_WIDTH, BF16), out(D_INNER, BF16), out(D_MODEL, BF16), out(D_MODEL, BF16),
                   out(D_MODEL, BF16), out(D_MODEL, BF16), out(PLE_DIM, BF16),
                   jax.ShapeDtypeStruct((1, LANES), F32), vshape, vshape, vshape, vshape),
        grid=(s // tm,),
        in_specs=[rows(D_MODEL), rows(SSD_WIDTH), rows(SSD_WIDTH), rows(ATT_WIDTH), rows(ATT_WIDTH),
                  rows(PLE_DIM), rows(D_MODEL), vec, vec, vec, vec,
                  _const_spec((D_INNER, D_MODEL)), _const_spec((D_MODEL, D_MODEL)), _const_spec((PLE_DIM, D_MODEL))],
        out_specs=(rows(D_MODEL), rows(SSD_WIDTH), rows(SSD_WIDTH),
                   pl.BlockSpec((N_HEADS, tm, LANES), lambda i: (0, i, 0)), rows(ATT_WIDTH),
                   rows(D_INNER), rows(D_MODEL), rows(D_MODEL), rows(D_MODEL), rows(D_MODEL), rows(PLE_DIM),
                   _const_spec((1, LANES)), vec, vec, vec, vec),
        compiler_params=_params(("arbitrary",)),
    )(x, y, zs, o, za, p, tgt, ssd_g, att_g_lane, ple_g, fin_g, w_out, w_gate, w_proj)


def in_proj_bwd(dsegs, wsegs, x, g, dh1, pres):
    s = x.shape[0]
    tm = _blk(s, 256)
    nseg = len(dsegs)
    nbig = len(pres)
    nsteps = s // tm

    def body(*refs):
        d_refs = refs[:nseg]
        w_refs = refs[nseg:2 * nseg]
        x_ref, g_ref, dh1_ref = refs[2 * nseg:2 * nseg + 3]
        rest = refs[2 * nseg + 3:]
        pre_refs, (dx_ref, dg_ref), part_refs = rest[:nbig], rest[nbig:nbig + 2], rest[nbig + 2:2 * nbig + 2]
        ssem, rsem, lsem = rest[2 * nbig + 2:]

        @pl.when(pl.program_id(0) == 0)
        def _():
            dg_ref[...] = jnp.zeros_like(dg_ref)
            for cp in scatter_copies(pre_refs, part_refs, ssem, rsem, lsem):
                cp.start()

        @pl.when(pl.program_id(0) == nsteps - 1)
        def _():
            for cp in scatter_copies(pre_refs, part_refs, ssem, rsem, lsem):
                cp.wait()

        du = _mm_nt(d_refs[0][...], w_refs[0][...])
        for k in range(1, nseg):
            du = du + _mm_nt(d_refs[k][...], w_refs[k][...])
        xv = x_ref[...]
        r = lax.rsqrt(jnp.mean(xv * xv, axis=-1, keepdims=True) + EPS)
        xh = xv * r
        dg_ref[...] += jnp.sum(du * xh, axis=0, keepdims=True)
        dxh = du * g_ref[...]
        dx_ref[...] = r * (dxh - xh * jnp.mean(dxh * xh, axis=-1, keepdims=True)) + dh1_ref[...]

    rows = lambda n: pl.BlockSpec((tm, n), lambda i: (i, 0))
    return pl.pallas_call(
        body, name="in_proj_bwd",
        out_shape=tuple([jax.ShapeDtypeStruct((s, D_MODEL), F32), jax.ShapeDtypeStruct((1, D_MODEL), F32)]
                        + [jax.ShapeDtypeStruct(a.shape, a.dtype) for a in pres]),
        grid=(nsteps,),
        in_specs=([rows(d.shape[1]) for d in dsegs] + [_const_spec(w.shape) for w in wsegs]
                  + [rows(D_MODEL), _const_spec((1, D_MODEL)), rows(D_MODEL)] + [ANY] * nbig),
        out_specs=tuple([rows(D_MODEL), _const_spec((1, D_MODEL))] + [ANY] * nbig),
        scratch_shapes=_sems(3 * nbig) + [pltpu.SemaphoreType.DMA((nbig,))],
        compiler_params=_params(("arbitrary",)),
    )(*dsegs, *wsegs, x, g, dh1, *pres)


SMALL_NAMES = ("norm_g", "conv_b", "dt_bias", "a_log", "d_skip", "ssd_norm_g", "fg_bias", "att_norm_g",
               "ple_norm_g", "final_norm_g")
SMALL_SIZES = (1024, 1536, 16, 16, 16, 1024, 16, 64, 1024, 1024)
CONV_W_SIZE = CONV_WIDTH * CONV_CH


def _pack_small(vals):
    flat = jnp.concatenate([v.reshape(-1).astype(F32) for v in vals])
    flat = jnp.pad(flat, (0, SMALL_ROWS * LANES - flat.shape[0]))
    return flat.reshape(SMALL_ROWS, LANES)


def _unpack_small(pack, shapes):
    flat = pack.reshape(-1)
    out, off = [], 0
    for n, shp in zip(SMALL_SIZES, shapes):
        out.append(flat[off:off + n].reshape(shp))
        off += n
    return out


def _row128(v16, offset=0):
    return jnp.pad(v16.reshape(1, N_HEADS).astype(F32), ((0, 0), (offset, LANES - N_HEADS - offset)))


def local_step(prereduce, x, p, tgt, w_in, w_out, w_gate, w_proj, conv_w, norm_g, conv_b, dt_bias, a_log, d_skip,
               ssd_norm_g, fg_bias, att_norm_g, ple_norm_g, final_norm_g):
    widths = (SSD_WIDTH, CONV_CH, N_HEADS, ATT_WIDTH, ATT_WIDTH, ATT_WIDTH, ATT_WIDTH)
    c0, c1, c2, c3, c4, c5, c6, c7 = [sum(widths[:i]) for i in range(len(widths) + 1)]
    w_zs, w_xbc, w_dt = w_in[:, c0:c1], w_in[:, c1:c2], w_in[:, c2:c3]
    w_za, w_q, w_k, w_v, w_f = w_in[:, c3:c4], w_in[:, c4:c5], w_in[:, c5:c6], w_in[:, c6:c7], w_in[:, c7:]
    w_small = jnp.concatenate([w_dt, w_f, jnp.zeros((D_MODEL, LANES - 2 * N_HEADS), BF16)], axis=1)

    dtb_row = _row128(dt_bias)
    a_row = _row128(-jnp.exp(a_log.astype(F32)))
    fgb_row = _row128(fg_bias, N_HEADS)
    dskip_lane = jnp.repeat(d_skip.astype(F32), HEAD_DIM).reshape(1, SSD_WIDTH)
    att_g_lane = jnp.tile(att_norm_g.astype(F32), N_HEADS).reshape(1, ATT_WIDTH)
    row = lambda v: v.reshape(1, -1).astype(F32)

    u = rms_prenorm(x, row(norm_g))
    zs = matmul_rows(u, w_zs, F32, "proj_z_ssd")
    xbc = matmul_rows(u, w_xbc, F32, "proj_xbc")
    za = matmul_rows(u, w_za, F32, "proj_z_att")
    small = matmul_rows(u, w_small, F32, "proj_small")
    cum = forget_cumsum(small, fgb_row)
    qa, ka, va, norms = proj_qkv_heads(u, w_q, w_k, w_v, cum)
    n_seq = x.shape[0]
    first, _ = live_blocks(norms, cum, _blk(n_seq, ATT_BLOCK), _blk(n_seq, ATT_BLOCK))
    _, last_q = live_blocks(norms, cum, _blk(n_seq, ATT_BLOCK_BWD_Q), _blk(n_seq, ATT_BLOCK_BWD))
    pre, xc = conv_fwd(xbc, conv_w, row(conv_b))
    y, states = ssd_fwd(xc, small, dtb_row, a_row, dskip_lane)
    o, qb = attention_fwd(first, qa, ka, va)
    (dh1, dy, dzs, dob, dza, ycat, dh1_b, n2_b, dgl_b, dpp_b, p_b,
     loss_l, dfin, dple, dssd_g, datt_lane) = post_mix(
        x, y, zs, o, za, p, tgt, row(ssd_norm_g), att_g_lane, row(ple_norm_g), row(final_norm_g),
        w_out, w_gate, w_proj)
    dq, dk, dv, dc = attention_bwd(last_q, qb, ka, va, dob)
    dxc, ddt_raw, da, ddtb, ddsk_lane = ssd_bwd(xc, small, states, dy, dtb_row, a_row, dskip_lane)
    dsmall, dfgb = forget_bwd(dc, small, ddt_raw, fgb_row)
    dxbc, dconv_w8, dconv_b = conv_bwd(xbc, pre, dxc, conv_w)
    dsegs = [dzs, dxbc, dza, dq, dk, dv, dsmall]
    wsegs = [w_zs, w_xbc, w_za, w_q, w_k, w_v, w_small]
    dws = [matmul_tn(u, d, "dw_in_%d" % i) for i, d in enumerate(dsegs)]
    dw_in = jnp.concatenate([dws[0], dws[1], dws[6][:, :N_HEADS], dws[2], dws[3], dws[4], dws[5],
                             dws[6][:, N_HEADS:2 * N_HEADS]], axis=1)
    dw_out = matmul_tn(ycat, dh1_b, "dw_out")
    dw_gate = matmul_tn(n2_b, dgl_b, "dw_gate")
    dw_proj = matmul_tn(p_b, dpp_b, "dw_proj")
    dx, dnorm_g, *parts = in_proj_bwd(dsegs, wsegs, x, row(norm_g), dh1, prereduce(dw_in, dw_out, dw_gate, dw_proj))
    small_grads = [
        dnorm_g, dconv_b, ddtb[0, :N_HEADS], (da * a_row)[0, :N_HEADS],
        ddsk_lane.reshape(N_HEADS, HEAD_DIM).sum(axis=1), dssd_g, dfgb[0, N_HEADS:2 * N_HEADS],
        datt_lane.reshape(N_HEADS, HEAD_DIM).sum(axis=0), dple, dfin]
    loss = jnp.sum(loss_l)
    return loss, dx, parts, dconv_w8[:CONV_WIDTH], small_grads


def kernel(x, p, norm_g, w_in, conv_w, conv_b, dt_bias, a_log, d_skip, ssd_norm_g, fg_bias, att_norm_g, w_out, ple_norm_g, w_ple_gate, w_ple_proj, final_norm_g, loss_target, m_norm_g, m_w_in, m_conv_w, m_conv_b, m_dt_bias, m_a_log, m_d_skip, m_ssd_norm_g, m_fg_bias, m_att_norm_g, m_w_out, m_ple_norm_g, m_w_ple_gate, m_w_ple_proj, m_final_norm_g, v_norm_g, v_w_in, v_conv_w, v_conv_b, v_dt_bias, v_a_log, v_d_skip, v_ssd_norm_g, v_fg_bias, v_att_norm_g, v_w_out, v_ple_norm_g, v_w_ple_gate, v_w_ple_proj, v_final_norm_g):
    chip = 2 * lax.axis_index("x") + lax.axis_index("y")
    core = lax.axis_index("c")

    big_w = [w_in[0], w_out[0], w_ple_gate[0], w_ple_proj[0]]
    own = [a.astype(BF16) for a in big_w] + [conv_w[0]]
    gathered = gather_weights(own[:4], own[4])

    def joined(k, axis):
        return jnp.concatenate([jnp.where(chip == j, own[k], gathered[k][j]) for j in range(N_CHIPS)], axis=axis)

    w_in_f, w_out_f, w_gate_f, w_proj_f, conv_w_f = joined(0, 1), joined(1, 0), joined(2, 0), joined(3, 1), joined(4, 1)

    core1 = core.reshape(1).astype(jnp.int32)

    def prereduce(dw_in, dw_out, dw_gate, dw_proj):
        n_in, n_proj = w_in.shape[2], w_ple_proj.shape[2]
        gs = [jnp.stack([dw_in[:, n_in * j:n_in * (j + 1)] for j in range(N_CHIPS)]),
              dw_out.reshape(N_CHIPS, w_out.shape[1], D_MODEL), dw_gate.reshape(N_CHIPS, w_ple_gate.shape[1], D_MODEL),
              jnp.stack([dw_proj[:, n_proj * j:n_proj * (j + 1)] for j in range(N_CHIPS)])]
        return add_halves(core1, gs, halves_to_sibling(gs))

    smalls_w = [norm_g, conv_b, dt_bias, a_log, d_skip, ssd_norm_g, fg_bias, att_norm_g, ple_norm_g, final_norm_g]
    loss_l, dx, parts, dconv_w, small_grads = local_step(
        prereduce, x[0], p[0, 0], loss_target[0], w_in_f, w_out_f, w_gate_f, w_proj_f, conv_w_f,
        *[a.reshape(-1) for a in smalls_w])
    loss = lax.psum(loss_l, ("x", "y", "c"))
    smalls = gather_small(_pack_small(list(small_grads) + [dconv_w]))
    mine = sum_parts(parts)

    g_big, d_big, m_big, v_big = adamw_big(
        core1, mine, swap_halves(mine), big_w, [m_w_in[0], m_w_out[0], m_w_ple_gate[0], m_w_ple_proj[0]],
        [v_w_in[0], v_w_out[0], v_w_ple_gate[0], v_w_ple_proj[0]])
    smalls_m = [m_norm_g, m_conv_b, m_dt_bias, m_a_log, m_d_skip, m_ssd_norm_g, m_fg_bias, m_att_norm_g,
                m_ple_norm_g, m_final_norm_g]
    smalls_v = [v_norm_g, v_conv_b, v_dt_bias, v_a_log, v_d_skip, v_ssd_norm_g, v_fg_bias, v_att_norm_g,
                v_ple_norm_g, v_final_norm_g]
    g_sm, d_sm, m_sm, v_sm = adamw_small(smalls, _pack_small(smalls_w), _pack_small(smalls_m), _pack_small(smalls_v))
    n_small = sum(SMALL_SIZES)
    g_conv_full = g_sm.reshape(-1)[n_small:n_small + CONV_W_SIZE].reshape(CONV_WIDTH, CONV_CH)
    n_conv = conv_w.shape[2]
    g_conv = lax.dynamic_slice_in_dim(g_conv_full, chip * n_conv, n_conv, axis=1)
    d_conv, m_conv, v_conv = adamw_whole(g_conv, conv_w[0], m_conv_w[0], v_conv_w[0], "adamw_conv")

    shapes = [a.shape for a in smalls_w]
    outs = []
    for big, conv, sm in ((g_big, g_conv, g_sm), (d_big, d_conv, d_sm), (m_big, m_conv, m_sm), (v_big, v_conv, v_sm)):
        b_in, b_out, b_gate, b_proj = [a[None] for a in big]
        s_norm, s_convb, s_dtb, s_alog, s_dsk, s_ssdg, s_fgb, s_attg, s_pleg, s_fin = _unpack_small(sm, shapes)
        outs.extend([s_norm, b_in, conv[None], s_convb, s_dtb, s_alog, s_dsk, s_ssdg, s_fgb, s_attg, b_out, s_pleg,
                     b_gate, b_proj, s_fin])
    return (loss, dx[None], *outs)
```

```python
import functools

import jax
import jax.numpy as jnp
from jax import lax
from jax.experimental import pallas as pl
from jax.experimental.pallas import tpu as pltpu

F32 = jnp.float32
BF16 = jnp.bfloat16

D_MODEL = 1024
SSD_WIDTH = 1024
ATT_WIDTH = 1024
N_HEADS = 16
HEAD_DIM = 64
N_GROUPS = 2
D_STATE = 128
CONV_CH = 1536
CONV_WIDTH = 4
CHUNK = 128
PLE_DIM = 256
D_INNER = 2048
EPS = 1e-6
IN_COLS = 6688
N_CHIPS = 4
N_DEV = 8
LANES = 128
N_PAIRS = 8

ADAM_LR = 0.001
ADAM_B1 = 0.9
ADAM_B2 = 0.999
ADAM_EPS = 1e-08
ADAM_WD = 0.01
ADAM_STEP = 10

SMALL_ROWS = 96

NEG_BIG = -1e30
VMEM_LIMIT = 56 * 1024 * 1024

MESH = pl.DeviceIdType.MESH
ANY = pl.BlockSpec(memory_space=pl.ANY)


def _mm(a, b):
    return jnp.dot(a, b, preferred_element_type=F32)


def _mm_nt(a, b):
    return lax.dot_general(a, b, (((1,), (1,)), ((), ())), preferred_element_type=F32)


def _mm_tn(a, b):
    return lax.dot_general(a, b, (((0,), (0,)), ((), ())), preferred_element_type=F32)


def _mm_exact(a, b):
    return jnp.dot(a, b, preferred_element_type=F32, precision=lax.Precision.HIGHEST)


def _softplus(x):
    return jnp.maximum(x, 0.0) + jnp.log1p(jnp.exp(-jnp.abs(x)))


def _sigmoid(x):
    return jax.nn.sigmoid(x)


def _iota(shape, dim):
    return lax.broadcasted_iota(jnp.int32, shape, dim)


def _params(sem=None):
    return pltpu.CompilerParams(dimension_semantics=sem, vmem_limit_bytes=VMEM_LIMIT)


def _blk(n, pref):
    return min(n, pref)


def _const_spec(shape):
    nd = len(shape)
    return pl.BlockSpec(shape, lambda *_: (0,) * nd)


def _chip_peers():
    x, y, c = lax.axis_index("x"), lax.axis_index("y"), lax.axis_index("c")
    return x, y, c, [(1 - x, y, c), (x, 1 - y, c), (1 - x, 1 - y, c)]


def _half(rows, c):
    h = rows // 2
    return pl.ds(pl.multiple_of(c * h, 8), h)


def _sems(n):
    return [pltpu.SemaphoreType.DMA((n,)), pltpu.SemaphoreType.DMA((n,))]


def gather_weights(shards, conv_s):
    n = len(shards)

    def body(*refs):
        ins, conv_in = refs[:n], refs[n]
        outs, conv_out = refs[n + 1:2 * n + 1], refs[2 * n + 1]
        ssem1, rsem1, ssem2, rsem2, c_ssem, c_rsem = refs[2 * n + 2:]
        x, y, c, peers = _chip_peers()
        me = 2 * x + y
        sibling = (x, y, 1 - c)
        first, small = [], []
        for k, peer in enumerate(peers):
            for i in range(n):
                h = _half(ins[i].shape[0], c)
                first.append(pltpu.make_async_remote_copy(
                    src_ref=ins[i].at[h], dst_ref=outs[i].at[me, h], send_sem=ssem1.at[n * k + i],
                    recv_sem=rsem1.at[n * k + i], device_id=peer, device_id_type=MESH))
            small.append(pltpu.make_async_remote_copy(
                src_ref=conv_in, dst_ref=conv_out.at[me], send_sem=c_ssem.at[k], recv_sem=c_rsem.at[k],
                device_id=peer, device_id_type=MESH))
        for cp in first + small:
            cp.start()
        passed = []
        for k, peer in enumerate(peers):
            chip = 2 * peer[0] + peer[1]
            for i in range(n):
                h = _half(ins[i].shape[0], c)
                first[n * k + i].wait_recv()
                fwd = pltpu.make_async_remote_copy(
                    src_ref=outs[i].at[chip, h], dst_ref=outs[i].at[chip, h], send_sem=ssem2.at[n * k + i],
                    recv_sem=rsem2.at[n * k + i], device_id=sibling, device_id_type=MESH)
                fwd.start()
                passed.append(fwd)
        for cp in passed:
            cp.wait_recv()
        for cp in first + passed:
            cp.wait_send()
        for cp in small:
            cp.wait()

    return pl.pallas_call(
        body, name="gather_weights",
        out_shape=tuple(jax.ShapeDtypeStruct((N_CHIPS,) + a.shape, a.dtype) for a in list(shards) + [conv_s]),
        in_specs=[ANY] * (n + 1), out_specs=(ANY,) * (n + 1),
        scratch_shapes=_sems(3 * n) + _sems(3 * n) + _sems(3),
    )(*shards, conv_s)


def halves_to_sibling(gs):
    n = len(gs)

    def body(*refs):
        ins, outs = refs[:n], refs[n:2 * n]
        ssem, rsem = refs[2 * n:]
        x, y, c = lax.axis_index("x"), lax.axis_index("y"), lax.axis_index("c")
        copies = []
        for i in range(n):
            for j in range(N_CHIPS):
                copies.append(pltpu.make_async_remote_copy(
                    src_ref=ins[i].at[j, _half(ins[i].shape[1], 1 - c)], dst_ref=outs[i].at[j],
                    send_sem=ssem.at[N_CHIPS * i + j], recv_sem=rsem.at[N_CHIPS * i + j],
                    device_id=(x, y, 1 - c), device_id_type=MESH))
        for cp in copies:
            cp.start()
        for cp in copies:
            cp.wait()

    return pl.pallas_call(
        body, name="halves_to_sibling",
        out_shape=tuple(jax.ShapeDtypeStruct((N_CHIPS, g.shape[1] // 2, g.shape[2]), F32) for g in gs),
        in_specs=[ANY] * n, out_specs=(ANY,) * n, scratch_shapes=_sems(N_CHIPS * n),
    )(*gs)


RED_GRID = 8


def add_halves(core, gs, rbs):
    n = len(gs)

    def body(c_ref, *refs):
        for i in range(n):
            refs[2 * n + i][...] = (refs[i][...] + refs[n + i][...]).astype(BF16)

    def blk(g):
        return (1, g.shape[1] // 2 // RED_GRID, g.shape[2])

    grid_spec = pltpu.PrefetchScalarGridSpec(
        num_scalar_prefetch=1, grid=(N_CHIPS, RED_GRID),
        in_specs=([pl.BlockSpec(blk(g), lambda j, b, c_ref: (j, c_ref[0] * RED_GRID + b, 0)) for g in gs]
                  + [pl.BlockSpec(blk(g), lambda j, b, c_ref: (j, b, 0)) for g in gs]),
        out_specs=[pl.BlockSpec(blk(g), lambda j, b, c_ref: (j, b, 0)) for g in gs])
    return pl.pallas_call(
        body, name="add_halves", grid_spec=grid_spec,
        out_shape=tuple(jax.ShapeDtypeStruct(r.shape, BF16) for r in rbs),
        compiler_params=_params(("parallel", "parallel")),
    )(core, *gs, *rbs)


def scatter_copies(ins, outs, ssem, rsem, lsem):
    n = len(ins)
    x, y, _, peers = _chip_peers()
    me = 2 * x + y
    copies = [pltpu.make_async_copy(ins[i].at[me], outs[i].at[me], lsem.at[i]) for i in range(n)]
    for k, peer in enumerate(peers):
        dst_chip = 2 * peer[0] + peer[1]
        for i in range(n):
            copies.append(pltpu.make_async_remote_copy(
                src_ref=ins[i].at[dst_chip], dst_ref=outs[i].at[me], send_sem=ssem.at[n * k + i],
                recv_sem=rsem.at[n * k + i], device_id=peer, device_id_type=MESH))
    return copies


def gather_small(small):
    def body(s_ref, smalls_ref, ssem, rsem, lsem):
        x, y, c = lax.axis_index("x"), lax.axis_index("y"), lax.axis_index("c")
        dev = 4 * x + 2 * y + c
        copies = [pltpu.make_async_copy(s_ref, smalls_ref.at[dev], lsem)]
        for k in range(1, N_DEV):
            fx, fy, fc = (k >> 2) & 1, (k >> 1) & 1, k & 1
            peer = ((1 - x) if fx else x, (1 - y) if fy else y, (1 - c) if fc else c)
            copies.append(pltpu.make_async_remote_copy(
                src_ref=s_ref, dst_ref=smalls_ref.at[dev], send_sem=ssem.at[k - 1], recv_sem=rsem.at[k - 1],
                device_id=peer, device_id_type=MESH))
        for cp in copies:
            cp.start()
        for cp in copies:
            cp.wait()

    return pl.pallas_call(
        body, name="gather_small",
        out_shape=jax.ShapeDtypeStruct((N_DEV,) + small.shape, F32),
        in_specs=[ANY], out_specs=ANY,
        scratch_shapes=_sems(N_DEV - 1) + [pltpu.SemaphoreType.DMA],
    )(small)


def sum_parts(parts):
    n = len(parts)

    def body(*refs):
        for i in range(n):
            p_ref = refs[i]
            refs[n + i][...] = ((p_ref[0].astype(F32) + p_ref[1].astype(F32)) + p_ref[2].astype(F32)
                                ) + p_ref[3].astype(F32)

    def rows(p):
        return p.shape[1] // RED_GRID

    return pl.pallas_call(
        body, name="sum_parts",
        out_shape=tuple(jax.ShapeDtypeStruct(p.shape[1:], F32) for p in parts),
        grid=(RED_GRID,),
        in_specs=[pl.BlockSpec((N_CHIPS, rows(p), p.shape[2]), lambda b: (0, b, 0)) for p in parts],
        out_specs=tuple(pl.BlockSpec((rows(p), p.shape[2]), lambda b: (b, 0)) for p in parts),
        compiler_params=_params(("parallel",)),
    )(*parts)


def swap_halves(reds):
    n = len(reds)

    def body(*refs):
        ins, outs = refs[:n], refs[n:2 * n]
        ssem, rsem = refs[2 * n:]
        x, y, c = lax.axis_index("x"), lax.axis_index("y"), lax.axis_index("c")
        copies = [pltpu.make_async_remote_copy(
            src_ref=ins[i], dst_ref=outs[i], send_sem=ssem.at[i], recv_sem=rsem.at[i],
            device_id=(x, y, 1 - c), device_id_type=MESH) for i in range(n)]
        for cp in copies:
            cp.start()
        for cp in copies:
            cp.wait()

    return pl.pallas_call(
        body, name="swap_halves",
        out_shape=tuple(jax.ShapeDtypeStruct(r.shape, F32) for r in reds),
        in_specs=[ANY] * n, out_specs=(ANY,) * n, scratch_shapes=_sems(n),
    )(*reds)


def _adamw(w, g, m, v):
    m = ADAM_B1 * m + (1.0 - ADAM_B1) * g
    v = ADAM_B2 * v + (1.0 - ADAM_B2) * (g * g)
    m_hat = m / (1.0 - ADAM_B1 ** ADAM_STEP)
    v_hat = v / (1.0 - ADAM_B2 ** ADAM_STEP)
    delta = -ADAM_LR * (m_hat / (jnp.sqrt(v_hat) + ADAM_EPS) + ADAM_WD * w)
    return delta, m, v


def adamw_big(core, mine, theirs, ws, ms, vs):
    n = len(ws)
    per_half = RED_GRID // 2

    def body(c_ref, *refs):
        own = (pl.program_id(0) // per_half) == c_ref[0]
        for i in range(n):
            g = jnp.where(own, refs[i][...], refs[n + i][...])
            d, mn, vn = _adamw(refs[2 * n + i][...], g, refs[3 * n + i][...], refs[4 * n + i][...])
            refs[5 * n + i][...] = g
            refs[6 * n + i][...] = d
            refs[7 * n + i][...] = mn
            refs[8 * n + i][...] = vn

    def blk(w):
        return (w.shape[0] // RED_GRID, w.shape[1])

    halves = [pl.BlockSpec(blk(w), lambda b, c_ref: (b % per_half, 0)) for w in ws]
    whole = [pl.BlockSpec(blk(w), lambda b, c_ref: (b, 0)) for w in ws]
    shapes = [jax.ShapeDtypeStruct(w.shape, F32) for w in ws]
    grid_spec = pltpu.PrefetchScalarGridSpec(
        num_scalar_prefetch=1, grid=(RED_GRID,), in_specs=halves * 2 + whole * 3, out_specs=whole * 4)
    outs = pl.pallas_call(
        body, name="adamw_big", out_shape=tuple(shapes * 4), grid_spec=grid_spec,
        compiler_params=_params(("parallel",)),
    )(core, *mine, *theirs, *ws, *ms, *vs)
    return outs[:n], outs[n:2 * n], outs[2 * n:3 * n], outs[3 * n:]


def adamw_whole(g, w, m, v, name):
    def body(g_ref, w_ref, m_ref, v_ref, d_out, m_out, v_out):
        d, mn, vn = _adamw(w_ref[...], g_ref[...], m_ref[...], v_ref[...])
        d_out[...] = d
        m_out[...] = mn
        v_out[...] = vn

    shp = jax.ShapeDtypeStruct(g.shape, F32)
    return pl.pallas_call(body, name=name, out_shape=(shp,) * 3)(g, w, m, v)


def adamw_small(smalls, w, m, v):
    def body(s_ref, w_ref, m_ref, v_ref, g_out, d_out, m_out, v_out):
        g = s_ref[0]
        for k in range(1, N_DEV):
            g = g + s_ref[k]
        d, mn, vn = _adamw(w_ref[...], g, m_ref[...], v_ref[...])
        g_out[...] = g
        d_out[...] = d
        m_out[...] = mn
        v_out[...] = vn

    shp = jax.ShapeDtypeStruct((SMALL_ROWS, LANES), F32)
    return pl.pallas_call(body, name="adamw_small", out_shape=(shp,) * 4)(smalls, w, m, v)


def rms_prenorm(x, g):
    s = x.shape[0]
    tm = _blk(s, 512)

    def body(x_ref, g_ref, u_ref):
        xv = x_ref[...]
        r = lax.rsqrt(jnp.mean(xv * xv, axis=-1, keepdims=True) + EPS)
        u_ref[...] = (xv * r * g_ref[...]).astype(BF16)

    return pl.pallas_call(
        body, name="rms_prenorm", out_shape=jax.ShapeDtypeStruct(x.shape, BF16), grid=(s // tm,),
        in_specs=[pl.BlockSpec((tm, D_MODEL), lambda i: (i, 0)), _const_spec((1, D_MODEL))],
        out_specs=pl.BlockSpec((tm, D_MODEL), lambda i: (i, 0)), compiler_params=_params(("parallel",)),
    )(x, g)


def matmul_rows(a, w, out_dtype, name):
    s, k = a.shape
    n = w.shape[1]
    tm = _blk(s, 512)

    def body(a_ref, w_ref, o_ref):
        o_ref[...] = _mm(a_ref[...], w_ref[...]).astype(out_dtype)

    return pl.pallas_call(
        body, name=name, out_shape=jax.ShapeDtypeStruct((s, n), out_dtype), grid=(s // tm,),
        in_specs=[pl.BlockSpec((tm, k), lambda i: (i, 0)), _const_spec((k, n))],
        out_specs=pl.BlockSpec((tm, n), lambda i: (i, 0)), compiler_params=_params(("parallel",)),
    )(a, w)


def matmul_tn(a, b, name):
    s, m = a.shape
    n = b.shape[1]
    tk = _blk(s, 2048)
    tn = _blk(n, 512)

    def body(a_ref, b_ref, o_ref):
        @pl.when(pl.program_id(1) == 0)
        def _():
            o_ref[...] = jnp.zeros_like(o_ref)

        o_ref[...] += _mm_tn(a_ref[...], b_ref[...])

    return pl.pallas_call(
        body, name=name, out_shape=jax.ShapeDtypeStruct((m, n), F32), grid=(n // tn, s // tk),
        in_specs=[pl.BlockSpec((tk, m), lambda j, i: (i, 0)), pl.BlockSpec((tk, tn), lambda j, i: (i, j))],
        out_specs=pl.BlockSpec((m, tn), lambda j, i: (0, j)),
        compiler_params=_params(("parallel", "arbitrary")),
    )(a, b)


def conv_fwd(xbc, w, b):
    s = xbc.shape[0]
    tm = _blk(s, 256)

    def body(x_ref, t_ref, w_ref, b_ref, pre_ref, act_ref):
        i = pl.program_id(0)
        cur = x_ref[...]
        tail = jnp.where(i > 0, t_ref[...], 0.0)
        wv = w_ref[...]
        acc = cur * wv[3:4, :] + b_ref[...]
        head = cur[0:8, :] * wv[3:4, :] + b_ref[...]
        row8 = _iota((8, CONV_CH), 0)
        for sh in range(1, CONV_WIDTH):
            wk = wv[3 - sh:4 - sh, :]
            acc = acc + pltpu.roll(cur, sh, 0) * wk
            first = jnp.where(row8 < sh, pltpu.roll(tail, sh, 0), pltpu.roll(cur[0:8, :], sh, 0))
            head = head + first * wk
        pre_ref[...] = acc
        act_ref[...] = acc * _sigmoid(acc)
        pre_ref[0:8, :] = head
        act_ref[0:8, :] = head * _sigmoid(head)

    shp = jax.ShapeDtypeStruct(xbc.shape, F32)
    rows = pl.BlockSpec((tm, CONV_CH), lambda i: (i, 0))
    return pl.pallas_call(
        body, name="conv_fwd", out_shape=(shp, shp), grid=(s // tm,),
        in_specs=[rows, pl.BlockSpec((8, CONV_CH), lambda i: (jnp.maximum(i * (tm // 8) - 1, 0), 0)),
                  _const_spec((CONV_WIDTH, CONV_CH)), _const_spec((1, CONV_CH))],
        out_specs=(rows, rows), compiler_params=_params(("parallel",)),
    )(xbc, xbc, w, b)


def conv_bwd(xbc, pre, dact, w):
    s = xbc.shape[0]
    tm = _blk(s, 256)
    nb = s // tm

    def dsilu(p):
        sg = _sigmoid(p)
        return sg * (1.0 + p * (1.0 - sg))

    def body(x_ref, xt_ref, p_ref, pn_ref, d_ref, dn_ref, w_ref, dx_ref, dw_ref, db_ref):
        i = pl.program_id(0)

        @pl.when(i == 0)
        def _():
            dw_ref[...] = jnp.zeros_like(dw_ref)
            db_ref[...] = jnp.zeros_like(db_ref)

        row8 = _iota((8, LANES), 0)
        for c0 in range(0, CONV_CH, LANES):
            cols = slice(c0, c0 + LANES)
            wv = w_ref[:, cols]
            dpre = d_ref[:, cols] * dsilu(p_ref[:, cols])
            dnext = jnp.where(i < nb - 1, dn_ref[:, cols] * dsilu(pn_ref[:, cols]), 0.0)
            cur = x_ref[:, cols]
            tail = jnp.where(i > 0, xt_ref[:, cols], 0.0)
            dx = dpre * wv[3:4, :]
            last = dpre[tm - 8:tm, :] * wv[3:4, :]
            db_ref[:, cols] += jnp.sum(dpre, axis=0, keepdims=True)
            dws = [jnp.sum(dpre * cur, axis=0, keepdims=True)]
            for sh in range(1, CONV_WIDTH):
                wk = wv[3 - sh:4 - sh, :]
                dx = dx + pltpu.roll(dpre, tm - sh, 0) * wk
                nxt = jnp.where(row8 >= 8 - sh, pltpu.roll(dnext, 8 - sh, 0),
                                pltpu.roll(dpre[tm - 8:tm, :], 8 - sh, 0))
                last = last + nxt * wk
                xs = pltpu.roll(cur, sh, 0)
                first = jnp.where(row8 < sh, pltpu.roll(tail, sh, 0), xs[0:8, :])
                dws.append(jnp.sum(dpre * xs, axis=0, keepdims=True)
                           + jnp.sum(dpre[0:8, :] * (first - xs[0:8, :]), axis=0, keepdims=True))
            dx_ref[:, cols] = dx.astype(BF16)
            dx_ref[tm - 8:tm, cols] = last.astype(BF16)
            for sh in range(CONV_WIDTH):
                dw_ref[3 - sh:4 - sh, cols] += dws[sh]

    rows = pl.BlockSpec((tm, CONV_CH), lambda i: (i, 0))
    prev8 = pl.BlockSpec((8, CONV_CH), lambda i: (jnp.maximum(i * (tm // 8) - 1, 0), 0))
    next8 = pl.BlockSpec((8, CONV_CH), lambda i: (jnp.minimum((i + 1) * (tm // 8), s // 8 - 1), 0))
    return pl.pallas_call(
        body, name="conv_bwd",
        out_shape=(jax.ShapeDtypeStruct(xbc.shape, BF16), jax.ShapeDtypeStruct((8, CONV_CH), F32),
                   jax.ShapeDtypeStruct((1, CONV_CH), F32)),
        grid=(nb,),
        in_specs=[rows, prev8, rows, next8, rows, next8, _const_spec((CONV_WIDTH, CONV_CH))],
        out_specs=(rows, _const_spec((8, CONV_CH)), _const_spec((1, CONV_CH))),
        compiler_params=_params(("arbitrary",)),
    )(xbc, xbc, pre, pre, dact, dact, w)


def _pair_lanes(mat, j, lane):
    return jnp.where(lane < HEAD_DIM, mat[:, 2 * j:2 * j + 1], mat[:, 2 * j + 1:2 * j + 2])


def _ssd_chunk_prelude(sm, dtb, a_row, lane, sub):
    raw = sm + dtb
    head_lane = lane < N_HEADS
    dt = jnp.where(head_lane, _softplus(raw), 0.0)
    sig = jnp.where(head_lane, _sigmoid(raw), 0.0)
    tri = (lane <= sub).astype(F32)
    acs = _mm_exact(tri, dt * a_row)
    return dt, sig, acs, acs.T


GROUP_WIDTH = SSD_WIDTH // N_GROUPS
HEADS_PER_GROUP = N_HEADS // N_GROUPS


def _expand_group(mat, g, lane):
    return jnp.concatenate([_pair_lanes(mat, j, lane) for j in range(4 * g, 4 * g + 4)], axis=1)


def _head_sums(q, g):
    row = _iota((GROUP_WIDTH, LANES), 0)
    seg = (_iota((GROUP_WIDTH, LANES), 1) == HEADS_PER_GROUP * g + (row >> 6)).astype(BF16)
    hi = q.astype(BF16)
    lo = (q - hi.astype(F32)).astype(BF16)
    return _mm(hi, seg) + _mm(lo, seg)


def _rows_from_lanes(row512):
    return jnp.broadcast_to(row512, (LANES, GROUP_WIDTH)).T


def ssd_fwd(xc, small, dtb_row, a_row, dskip_lane):
    s = xc.shape[0]
    nc = s // CHUNK

    def body(xc_ref, sm_ref, dtb_ref, a_ref, dsk_ref, y_ref, hs_ref, h_scr):
        c = pl.program_id(0)

        @pl.when(c == 0)
        def _():
            h_scr[...] = jnp.zeros_like(h_scr)

        lane = _iota((CHUNK, LANES), 1)
        sub = _iota((CHUNK, LANES), 0)
        causal = lane <= sub
        dt, _, acs, acs_t = _ssd_chunk_prelude(sm_ref[...], dtb_ref[...], a_ref[...], lane, sub)
        for g in range(N_GROUPS):
            cols = slice(GROUP_WIDTH * g, GROUP_WIDTH * (g + 1))
            b_off = SSD_WIDTH + D_STATE * g
            c_off = SSD_WIDTH + N_GROUPS * D_STATE + D_STATE * g
            b_b = xc_ref[:, b_off:b_off + D_STATE].astype(BF16)
            c_b = xc_ref[:, c_off:c_off + D_STATE].astype(BF16)
            cb = _mm_nt(c_b, b_b)
            x_g = xc_ref[:, cols]
            acs_g = _expand_group(acs, g, lane)
            xdt_g = x_g * _expand_group(dt, g, lane)
            xdt_b = xdt_g.astype(BF16)
            heads = range(HEADS_PER_GROUP * g, HEADS_PER_GROUP * (g + 1))
            m_b = [(cb * jnp.exp(jnp.where(causal, acs[:, h:h + 1] - acs_t[h:h + 1, :], NEG_BIG))).astype(BF16)
                   for h in heads]
            yd = [_mm(m_b[k], xdt_b[:, LANES * (k // 2):LANES * (k // 2 + 1)]) for k in range(HEADS_PER_GROUP)]
            yd_g = jnp.concatenate([jnp.where(lane < HEAD_DIM, yd[2 * k], yd[2 * k + 1]) for k in range(4)], axis=1)
            h_g = h_scr[g]
            t_g = _mm_nt(c_b, h_g.astype(BF16))
            y_ref[:, cols] = yd_g + jnp.exp(acs_g) * t_g + dsk_ref[:, cols] * x_g
            hs_ref[0, g] = h_g
            last_g = acs_g[CHUNK - 1:CHUNK, :]
            w_b = (xdt_g * jnp.exp(last_g - acs_g)).astype(BF16)
            h_scr[g] = h_g * jnp.exp(_rows_from_lanes(last_g)) + _mm_tn(w_b, b_b)

    return pl.pallas_call(
        body, name="ssd_fwd",
        out_shape=(jax.ShapeDtypeStruct((s, SSD_WIDTH), F32),
                   jax.ShapeDtypeStruct((nc, N_GROUPS, GROUP_WIDTH, D_STATE), F32)),
        grid=(nc,),
        in_specs=[pl.BlockSpec((CHUNK, CONV_CH), lambda c: (c, 0)), pl.BlockSpec((CHUNK, LANES), lambda c: (c, 0)),
                  _const_spec((1, LANES)), _const_spec((1, LANES)), _const_spec((1, SSD_WIDTH))],
        out_specs=(pl.BlockSpec((CHUNK, SSD_WIDTH), lambda c: (c, 0)),
                   pl.BlockSpec((1, N_GROUPS, GROUP_WIDTH, D_STATE), lambda c: (c, 0, 0, 0))),
        scratch_shapes=[pltpu.VMEM((N_GROUPS, GROUP_WIDTH, D_STATE), F32)],
        compiler_params=_params(("arbitrary",)),
    )(xc, small, dtb_row, a_row, dskip_lane)


def ssd_bwd(xc, small, states, dy, dtb_row, a_row, dskip_lane):
    s = xc.shape[0]
    nc = s // CHUNK
    rev = lambda c: nc - 1 - c

    def body(xc_ref, sm_ref, hs_ref, dy_ref, dtb_ref, a_ref, dsk_ref,
             dxc_ref, ddt_ref, da_ref, ddtb_ref, ddsk_ref, dh_scr):
        c = pl.program_id(0)

        @pl.when(c == 0)
        def _():
            dh_scr[...] = jnp.zeros_like(dh_scr)
            da_ref[...] = jnp.zeros_like(da_ref)
            ddtb_ref[...] = jnp.zeros_like(ddtb_ref)
            ddsk_ref[...] = jnp.zeros_like(ddsk_ref)

        lane = _iota((CHUNK, LANES), 1)
        sub = _iota((CHUNK, LANES), 0)
        causal = lane <= sub
        upper = lane >= sub
        is_last = sub == CHUNK - 1
        a_row_v = a_ref[...]
        dt, sig, acs, acs_t = _ssd_chunk_prelude(sm_ref[...], dtb_ref[...], a_row_v, lane, sub)
        cd = jnp.exp(acs[CHUNK - 1:CHUNK, :])
        dacs_c = jnp.zeros((CHUNK, LANES), F32)
        dacs_r = jnp.zeros((LANES, CHUNK), F32)
        ddtx = jnp.zeros((CHUNK, LANES), F32)
        for g in range(N_GROUPS):
            cols = slice(GROUP_WIDTH * g, GROUP_WIDTH * (g + 1))
            b_off = SSD_WIDTH + D_STATE * g
            c_off = SSD_WIDTH + N_GROUPS * D_STATE + D_STATE * g
            b_b = xc_ref[:, b_off:b_off + D_STATE].astype(BF16)
            c_b = xc_ref[:, c_off:c_off + D_STATE].astype(BF16)
            cb = _mm_nt(c_b, b_b)
            cb_t = _mm_nt(b_b, c_b)
            x_g = xc_ref[:, cols]
            dy_g = dy_ref[:, cols]
            dt_g = _expand_group(dt, g, lane)
            acs_g = _expand_group(acs, g, lane)
            last_g = acs_g[CHUNK - 1:CHUNK, :]
            e_g = jnp.exp(acs_g)
            dte_g = jnp.exp(last_g - acs_g)
            xdt_g = x_g * dt_g
            xdt_b = xdt_g.astype(BF16)
            h_g = hs_ref[0, g]
            dh_g = dh_scr[g]
            h_b = h_g.astype(BF16)
            dh_b = dh_g.astype(BF16)
            heads = list(range(HEADS_PER_GROUP * g, HEADS_PER_GROUP * (g + 1)))
            segs = [acs[:, h:h + 1] - acs_t[h:h + 1, :] for h in heads]
            lms = [jnp.exp(jnp.where(causal, sg, NEG_BIG)) for sg in segs]
            mts = [(cb_t * jnp.exp(jnp.where(upper, -sg, NEG_BIG))).astype(BF16) for sg in segs]
            dyh = []
            for k in range(HEADS_PER_GROUP):
                blk = dy_g[:, LANES * (k // 2):LANES * (k // 2 + 1)]
                in_head = (lane < HEAD_DIM) if k % 2 == 0 else (lane >= HEAD_DIM)
                dyh.append(jnp.where(in_head, blk, 0.0).astype(BF16))
            dms = [_mm_nt(dyh[k], xdt_b[:, LANES * (k // 2):LANES * (k // 2 + 1)]) for k in range(HEADS_PER_GROUP)]
            dxs = [_mm(mts[k], dyh[k]) for k in range(HEADS_PER_GROUP)]
            dcb = jnp.zeros((CHUNK, CHUNK), F32)
            for k, h in enumerate(heads):
                gmat = dms[k] * (cb * lms[k])
                dacs_c = dacs_c + jnp.where(lane == h, jnp.sum(gmat, axis=1, keepdims=True), 0.0)
                dacs_r = dacs_r - jnp.where(sub == h, jnp.sum(gmat, axis=0, keepdims=True), 0.0)
                dcb = dcb + dms[k] * lms[k]
            dxdt_g = jnp.concatenate([dxs[2 * k] + dxs[2 * k + 1] for k in range(4)], axis=1)
            t_g = _mm_nt(c_b, h_b)
            dacs_c = dacs_c + _head_sums(dy_g * e_g * t_g, g)
            dt_b = (dy_g * e_g).astype(BF16)
            dc_acc = _mm(dt_b, h_b)
            dh_prev = _mm_tn(dt_b, c_b)
            dw_g = _mm_nt(b_b, dh_b)
            w_g = xdt_g * dte_g
            dxdt_g = dxdt_g + dw_g * dte_g
            db_acc = _mm(w_g.astype(BF16), dh_b)
            r2 = _head_sums(dw_g * w_g, g)
            dacs_c = dacs_c + jnp.where(is_last, jnp.sum(r2, axis=0, keepdims=True), 0.0) - r2
            q3 = jnp.sum(dh_g * h_g, axis=1, keepdims=True)
            for k, h in enumerate(heads):
                tot = jnp.sum(q3[HEAD_DIM * k:HEAD_DIM * (k + 1), :], keepdims=True) * cd[:, h:h + 1]
                dacs_c = dacs_c + jnp.where(is_last & (lane == h), tot, 0.0)
            dh_scr[g] = dh_prev + dh_g * jnp.exp(_rows_from_lanes(last_g))
            dxc_ref[:, cols] = dxdt_g * dt_g + dsk_ref[:, cols] * dy_g
            ddtx = ddtx + _head_sums(dxdt_g * x_g, g)
            ddsk_ref[:, cols] += jnp.sum(dy_g * x_g, axis=0, keepdims=True)
            dxc_ref[:, b_off:b_off + D_STATE] = db_acc + _mm(dcb.T.astype(BF16), c_b)
            dxc_ref[:, c_off:c_off + D_STATE] = dc_acc + _mm(dcb.astype(BF16), b_b)
        dacs = dacs_c + dacs_r.T
        dadt = _mm_exact((lane >= sub).astype(F32), dacs)
        ddt = dadt * a_row_v + ddtx
        ddt_raw = ddt * sig
        ddt_ref[...] = ddt_raw
        da_ref[...] += jnp.sum(dadt * dt, axis=0, keepdims=True)
        ddtb_ref[...] += jnp.sum(ddt_raw, axis=0, keepdims=True)

    return pl.pallas_call(
        body, name="ssd_bwd",
        out_shape=(jax.ShapeDtypeStruct((s, CONV_CH), F32), jax.ShapeDtypeStruct((s, LANES), F32),
                   jax.ShapeDtypeStruct((1, LANES), F32), jax.ShapeDtypeStruct((1, LANES), F32),
                   jax.ShapeDtypeStruct((1, SSD_WIDTH), F32)),
        grid=(nc,),
        in_specs=[pl.BlockSpec((CHUNK, CONV_CH), lambda c: (rev(c), 0)),
                  pl.BlockSpec((CHUNK, LANES), lambda c: (rev(c), 0)),
                  pl.BlockSpec((1, N_GROUPS, GROUP_WIDTH, D_STATE), lambda c: (rev(c), 0, 0, 0)),
                  pl.BlockSpec((CHUNK, SSD_WIDTH), lambda c: (rev(c), 0)),
                  _const_spec((1, LANES)), _const_spec((1, LANES)), _const_spec((1, SSD_WIDTH))],
        out_specs=(pl.BlockSpec((CHUNK, CONV_CH), lambda c: (rev(c), 0)),
                   pl.BlockSpec((CHUNK, LANES), lambda c: (rev(c), 0)),
                   _const_spec((1, LANES)), _const_spec((1, LANES)), _const_spec((1, SSD_WIDTH))),
        scratch_shapes=[pltpu.VMEM((N_GROUPS, GROUP_WIDTH, D_STATE), F32)],
        compiler_params=_params(("arbitrary",)),
    )(xc, small, states, dy, dtb_row, a_row, dskip_lane)


FORGET_BLOCK = 512


def forget_cumsum(small, fgb_row):
    s = small.shape[0]
    t = _blk(s, FORGET_BLOCK)
    nb = s // t

    def body(sm_ref, b_ref, cc_ref, carry):
        i = pl.program_id(0)

        @pl.when(i == 0)
        def _():
            carry[...] = jnp.zeros_like(carry)

        lane = _iota((t, LANES), 1)
        in_f = (lane >= N_HEADS) & (lane < 2 * N_HEADS)
        logf = jnp.where(in_f, -_softplus(-(sm_ref[...] + b_ref[...])), 0.0)
        tri = (_iota((t, t), 1) <= _iota((t, t), 0)).astype(F32)
        cum = _mm_exact(tri, logf) + carry[0:1, :]
        cc_ref[...] = cum
        carry[...] = jnp.broadcast_to(cum[t - 1:t, :], (8, LANES))

    return pl.pallas_call(
        body, name="forget_cumsum",
        out_shape=jax.ShapeDtypeStruct((s, LANES), F32),
        grid=(nb,),
        in_specs=[pl.BlockSpec((t, LANES), lambda i: (i, 0)), _const_spec((1, LANES))],
        out_specs=pl.BlockSpec((t, LANES), lambda i: (i, 0)),
        scratch_shapes=[pltpu.VMEM((8, LANES), F32)],
        compiler_params=_params(("arbitrary",)),
    )(small, fgb_row)


def forget_bwd(dc, small, ddt_raw, fgb_row):
    s = small.shape[0]
    t = _blk(s, FORGET_BLOCK)
    nb = s // t
    rev = lambda i: nb - 1 - i

    def body(dc_ref, sm_ref, ddt_ref, b_ref, ds_ref, dfb_ref, carry):
        i = pl.program_id(0)

        @pl.when(i == 0)
        def _():
            carry[...] = jnp.zeros_like(carry)
            dfb_ref[...] = jnp.zeros_like(dfb_ref)

        lane = _iota((t, LANES), 1)
        rows = dc_ref[...].T
        tri = (_iota((t, t), 1) <= _iota((t, t), 0)).astype(F32)
        rc = _mm_exact(rows, tri) + carry[:, 0:1]
        carry[...] = jnp.broadcast_to(rc[:, 0:1], (LANES, LANES))
        in_f = (lane >= N_HEADS) & (lane < 2 * N_HEADS)
        df = jnp.where(in_f, rc.T * _sigmoid(-(sm_ref[...] + b_ref[...])), 0.0)
        ds_ref[...] = (df + ddt_ref[...]).astype(BF16)
        dfb_ref[...] += jnp.sum(df, axis=0, keepdims=True)

    blk = pl.BlockSpec((t, LANES), lambda i: (rev(i), 0))
    return pl.pallas_call(
        body, name="forget_bwd",
        out_shape=(jax.ShapeDtypeStruct((s, LANES), BF16), jax.ShapeDtypeStruct((1, LANES), F32)),
        grid=(nb,),
        in_specs=[blk, blk, blk, _const_spec((1, LANES))],
        out_specs=(blk, _const_spec((1, LANES))),
        scratch_shapes=[pltpu.VMEM((LANES, LANES), F32)],
        compiler_params=_params(("arbitrary",)),
    )(dc, small, ddt_raw, fgb_row)


ATT_BLOCK = 1024
ATT_BLOCK_BWD = 512
ATT_BLOCK_BWD_Q = 512
ATT_SCALE = HEAD_DIM ** -0.5
AUG_A = HEAD_DIM
AUG_B = HEAD_DIM + 3


def _split3(c):
    hi = c.astype(BF16).astype(F32)
    r = c - hi
    mid = r.astype(BF16).astype(F32)
    return hi, mid, (r - mid).astype(BF16).astype(F32)


def _aug(lane, first, parts=None, value=1.0):
    if parts is None:
        return jnp.where((lane >= first) & (lane < first + 3), value, 0.0)
    return (jnp.where(lane == first, parts[0], 0.0) + jnp.where(lane == first + 1, parts[1], 0.0)
            + jnp.where(lane == first + 2, parts[2], 0.0))


def _pack_pair(a0, a1, lane):
    return jnp.where(lane < HEAD_DIM, a0, pltpu.roll(a1, HEAD_DIM, 1))


def proj_qkv_heads(u, w_q, w_k, w_v, cum):
    s = u.shape[0]
    tm = _blk(s, 256)

    def body(u_ref, wq_ref, wk_ref, wv_ref, c_ref, qa_ref, ka_ref, va_ref, nrm_ref):
        lane = _iota((tm, LANES), 1)
        lo = lane < HEAD_DIM
        uv = u_ref[...]
        qf = _mm(uv, wq_ref[...]) * ATT_SCALE
        kf = _mm(uv, wk_ref[...])
        vf = _mm(uv, wv_ref[...])
        cc = c_ref[...]
        ones_a = _aug(lane, AUG_A)
        ones_b = _aug(lane, AUG_B)
        sub8 = _iota((8, LANES), 0)
        nrm = jnp.zeros((8, LANES), F32)
        for h in range(N_HEADS):
            j, e = divmod(h, 2)

            def head(full):
                blk = full[:, LANES * j:LANES * (j + 1)]
                if e == 1:
                    blk = pltpu.roll(blk, HEAD_DIM, 1)
                return jnp.where(lo, blk, 0.0)

            parts = _split3(cc[:, N_HEADS + h:N_HEADS + h + 1])
            qh, kh = head(qf), head(kf)
            qa_ref[h] = (qh + _aug(lane, AUG_A, parts) + ones_b).astype(BF16)
            ka_ref[h] = (kh + ones_a - _aug(lane, AUG_B, parts)).astype(BF16)
            va_ref[h] = (head(vf) + ones_a).astype(BF16)
        seg = (_iota((ATT_WIDTH, LANES), 1) == (_iota((ATT_WIDTH, LANES), 0) >> 6)).astype(BF16)
        for r, val in enumerate((qf, kf)):
            sq = val * val
            hi = sq.astype(BF16)
            tot = _mm(hi, seg) + _mm((sq - hi.astype(F32)).astype(BF16), seg)
            nrm = nrm + jnp.where(sub8 == r, jnp.max(tot, axis=0, keepdims=True), 0.0)
        nrm_ref[0] = nrm

    shp = jax.ShapeDtypeStruct((N_HEADS, s, LANES), BF16)
    hspec = pl.BlockSpec((N_HEADS, tm, LANES), lambda i: (0, i, 0))
    wspec = _const_spec((D_MODEL, ATT_WIDTH))
    return pl.pallas_call(
        body, name="proj_qkv_heads",
        out_shape=(shp, shp, shp, jax.ShapeDtypeStruct((s // tm, 8, LANES), F32)), grid=(s // tm,),
        in_specs=[pl.BlockSpec((tm, D_MODEL), lambda i: (i, 0)), wspec, wspec, wspec,
                  pl.BlockSpec((tm, LANES), lambda i: (i, 0))],
        out_specs=(hspec, hspec, hspec, pl.BlockSpec((1, 8, LANES), lambda i: (i, 0, 0))),
        compiler_params=_params(("parallel",)),
    )(u, w_q, w_k, w_v, cum)


SKIP_BELOW = -110.0


def live_blocks(norms, cum, tq, tk):
    qn = jnp.sqrt(jnp.max(norms[:, 0, :N_HEADS], axis=0))
    kn = jnp.sqrt(jnp.max(norms[:, 1, :N_HEADS], axis=0))
    bound = 2.05 * qn * kn + 2.0
    c_first = cum[0::tq, N_HEADS:2 * N_HEADS]
    c_last = cum[tk - 1::tk, N_HEADS:2 * N_HEADS]
    nq, nk = c_first.shape[0], c_last.shape[0]
    top = bound[None, None, :] + c_first[:, None, :] - c_last[None, :, :]
    before = (jnp.arange(nk)[None, :] + 1) * tk <= jnp.arange(nq)[:, None] * tq
    dead = before[:, :, None] & ~(top >= SKIP_BELOW)
    first = jnp.sum(dead, axis=1).astype(jnp.int32).T
    last_q = jnp.sum(first[:, None, :] <= jnp.arange(nk)[None, :, None], axis=2).astype(jnp.int32) - 1
    return first, last_q


def attention_fwd(first, qa, ka, va):
    s = qa.shape[1]
    t = _blk(s, ATT_BLOCK)
    nq = s // t

    def body(first_ref, qa_ref, ka_ref, va_ref, o_ref, qb_ref, m_scr, acc_scr, alpha_scr, p_scr, s_scr):
        qi = pl.program_id(1)
        starts = [first_ref[2 * pl.program_id(0) + e, qi] for e in range(2)]
        k0 = jnp.maximum(starts[0], starts[1])
        m_scr[...] = jnp.full_like(m_scr, NEG_BIG)
        acc_scr[...] = jnp.zeros_like(acc_scr)

        def kv_rows(kb):
            return pl.ds(pl.multiple_of(kb * t, t), t)

        def logits(kb, masked, heads=(0, 1)):
            for e in heads:
                sc = _mm_nt(qa_ref[e], ka_ref[e, kv_rows(kb), :])
                if masked:
                    sc = jnp.where(_iota((t, t), 0) >= _iota((t, t), 1), sc, NEG_BIG)
                s_scr[e] = sc

        def probs(heads=(0, 1)):
            for e in heads:
                cmax = s_scr[e, :, 0:LANES]
                for c in range(1, t // LANES):
                    cmax = jnp.maximum(cmax, s_scr[e, :, LANES * c:LANES * (c + 1)])
                m_old = m_scr[e]
                m_new = jnp.maximum(m_old, jnp.max(cmax, axis=1, keepdims=True))
                alpha_scr[e] = jnp.exp(m_old - m_new)
                m_scr[e] = m_new
                for c in range(t // LANES):
                    cols = slice(LANES * c, LANES * (c + 1))
                    p_scr[e, :, cols] = jnp.exp(s_scr[e, :, cols] - m_new).astype(BF16)

        def accumulate(kb, heads=(0, 1)):
            for e in heads:
                acc_scr[e] = alpha_scr[e] * acc_scr[e] + _mm(p_scr[e], va_ref[e, kv_rows(kb), :])

        for e in range(2):
            def alone(kb, carry, e=e):
                logits(kb, False, (e,))
                probs((e,))
                accumulate(kb, (e,))
                return carry

            lax.fori_loop(starts[e], k0, alone, 0)

        def loop_body(kb, carry):
            logits(kb, False)
            for e in range(2):
                accumulate(kb - 1, (e,))
                probs((e,))
            return carry

        @pl.when(qi > k0)
        def _():
            logits(k0, False)
            probs()

        lax.fori_loop(k0 + 1, qi, loop_body, 0)

        @pl.when(qi > k0)
        def _():
            logits(qi, True)
            accumulate(qi - 1)
            probs()

        @pl.when(qi == k0)
        def _():
            logits(qi, True)
            probs()

        accumulate(qi)

        lane = _iota((t, LANES), 1)
        outs = []
        for e in range(2):
            acc = acc_scr[e]
            l = acc[:, AUG_A:AUG_A + 1]
            outs.append(acc / l)
            lse = m_scr[e][:, 0:1] + jnp.log(l)
            q32 = qa_ref[e].astype(F32)
            c = q32[:, AUG_A:AUG_A + 1] + q32[:, AUG_A + 1:AUG_A + 2] + q32[:, AUG_A + 2:AUG_A + 3]
            qb = jnp.where(lane < HEAD_DIM, q32, 0.0) + _aug(lane, AUG_A, _split3(c - lse)) + _aug(lane, AUG_B)
            qb_ref[e] = qb.astype(BF16)
        o_ref[...] = _pack_pair(outs[0], outs[1], lane)

    grid_spec = pltpu.PrefetchScalarGridSpec(
        num_scalar_prefetch=1, grid=(N_PAIRS, nq),
        in_specs=[pl.BlockSpec((2, t, LANES), lambda j, qi, f: (j, qi, 0)),
                  pl.BlockSpec((2, s, LANES), lambda j, qi, f: (j, 0, 0)),
                  pl.BlockSpec((2, s, LANES), lambda j, qi, f: (j, 0, 0))],
        out_specs=[pl.BlockSpec((t, LANES), lambda j, qi, f: (qi, j)),
                   pl.BlockSpec((2, t, LANES), lambda j, qi, f: (j, qi, 0))],
        scratch_shapes=[pltpu.VMEM((2, t, LANES), F32), pltpu.VMEM((2, t, LANES), F32),
                        pltpu.VMEM((2, t, LANES), F32), pltpu.VMEM((2, t, t), BF16), pltpu.VMEM((2, t, t), F32)])
    return pl.pallas_call(
        body, name="attention_fwd", grid_spec=grid_spec,
        out_shape=(jax.ShapeDtypeStruct((s, ATT_WIDTH), F32), jax.ShapeDtypeStruct((N_HEADS, s, LANES), BF16)),
        compiler_params=_params(("parallel", "parallel")),
    )(first, qa, ka, va)


def attention_bwd(last_q, qb, ka, va, dob):
    s = qb.shape[1]
    t = _blk(s, ATT_BLOCK_BWD)
    tq = _blk(s, ATT_BLOCK_BWD_Q)
    nq = s // tq
    per_q = tq // t

    def body(last_ref, qb_ref, dob_ref, ka_ref, va_ref, dq_ref, dk_ref, dv_ref, dc_ref, dq_scr, dk_scr, dv_scr):
        j, ki = pl.program_id(0), pl.program_id(1)

        @pl.when((j == 0) & (ki == 0))
        def _():
            dc_ref[...] = jnp.zeros_like(dc_ref)

        @pl.when(ki == 0)
        def _():
            dq_scr[...] = jnp.zeros_like(dq_scr)

        dk_scr[...] = jnp.zeros_like(dk_scr)
        dv_scr[...] = jnp.zeros_like(dv_scr)

        def q_step(qblk, masked, heads=(0, 1)):
            rows = pl.ds(pl.multiple_of(qblk * tq, tq), tq)
            scs = [_mm_nt(qb_ref[e, rows, :], ka_ref[e]) for e in heads]
            dps = [_mm_nt(dob_ref[e, rows, :], va_ref[e]) for e in heads]
            for e, sc, dp in zip(heads, scs, dps):
                q = qb_ref[e, rows, :]
                do = dob_ref[e, rows, :]
                if masked:
                    keep = (_iota((tq, t), 0) - _iota((tq, t), 1)) >= ki * t - qblk * tq
                    sc = jnp.where(keep, sc, NEG_BIG)
                p = jnp.exp(sc)
                ds_b = (p * dp).astype(BF16)
                dv_scr[e] += _mm_tn(p.astype(BF16), do)
                dk_scr[e] += _mm_tn(ds_b, q)
                dq_scr[e, rows, :] += _mm(ds_b, ka_ref[e])

        def loop_body(qblk, carry):
            q_step(qblk, False)
            return carry

        ends = [last_ref[2 * j + e, ki] + 1 for e in range(2)]
        both = jnp.minimum(ends[0], ends[1])
        diag = ki // per_q
        q_step(diag, True)
        lax.fori_loop(diag + 1, both, loop_body, 0)
        for e in range(2):
            def alone(qblk, carry, e=e):
                q_step(qblk, False, (e,))
                return carry

            lax.fori_loop(both, ends[e], alone, 0)

        lane = _iota((t, LANES), 1)
        dk_ref[...] = _pack_pair(dk_scr[0], dk_scr[1], lane).astype(BF16)
        dv_ref[...] = _pack_pair(dv_scr[0], dv_scr[1], lane).astype(BF16)
        rows = pl.ds(pl.multiple_of(ki * t, t), t)
        dc_ref[rows, :] -= (jnp.where(lane == N_HEADS + 2 * j, dk_scr[0][:, AUG_B:AUG_B + 1], 0.0)
                            + jnp.where(lane == N_HEADS + 2 * j + 1, dk_scr[1][:, AUG_B:AUG_B + 1], 0.0))

        @pl.when(ki == s // t - 1)
        def _():
            for blk in range(s // t):
                rws = pl.ds(blk * t, t)
                d0 = dq_scr[0, rws, :]
                d1 = dq_scr[1, rws, :]
                dq_ref[rws, :] = (_pack_pair(d0, d1, lane) * ATT_SCALE).astype(BF16)
                dc_ref[rws, :] += (jnp.where(lane == N_HEADS + 2 * j, d0[:, AUG_A:AUG_A + 1], 0.0)
                                   + jnp.where(lane == N_HEADS + 2 * j + 1, d1[:, AUG_A:AUG_A + 1], 0.0))

    full = pl.BlockSpec((2, s, LANES), lambda j, ki, f: (j, 0, 0))
    blk = pl.BlockSpec((2, t, LANES), lambda j, ki, f: (j, ki, 0))
    pair = pl.BlockSpec((t, LANES), lambda j, ki, f: (ki, j))
    wide = jax.ShapeDtypeStruct((s, ATT_WIDTH), BF16)
    grid_spec = pltpu.PrefetchScalarGridSpec(
        num_scalar_prefetch=1, grid=(N_PAIRS, s // t),
        in_specs=[full, full, blk, blk],
        out_specs=[pl.BlockSpec((s, LANES), lambda j, ki, f: (0, j)), pair, pair,
                   pl.BlockSpec((s, LANES), lambda j, ki, f: (0, 0))],
        scratch_shapes=[pltpu.VMEM((2, s, LANES), F32), pltpu.VMEM((2, t, LANES), F32),
                        pltpu.VMEM((2, t, LANES), F32)])
    return pl.pallas_call(
        body, name="attention_bwd", grid_spec=grid_spec,
        out_shape=(wide, wide, wide, jax.ShapeDtypeStruct((s, LANES), F32)),
        compiler_params=_params(("arbitrary", "arbitrary")),
    )(last_q, qb, dob, ka, va)


def _dsilu(z, sg):
    return sg * (1.0 + z * (1.0 - sg))


def post_mix(x, y, zs, o, za, p, tgt, ssd_g, att_g_lane, ple_g, fin_g, w_out, w_gate, w_proj):
    s = x.shape[0]
    tm = _blk(s, 256)
    half = SSD_WIDTH // N_GROUPS

    def rms_bwd(dy, yn, r):
        return r * (dy - yn * jnp.mean(dy * yn, axis=-1, keepdims=True))

    def colsum(a):
        return jnp.sum(a, axis=0, keepdims=True)

    def body(x_ref, y_ref, zs_ref, o_ref, za_ref, p_ref, t_ref, sg_ref, ag_ref, pg_ref, fg_ref,
             wo_ref, wg_ref, wp_ref,
             dh1_ref, dy_ref, dzs_ref, dob_ref, dza_ref, ycat_ref, dh1b_ref, n2b_ref, dglb_ref, dppb_ref, pb_ref,
             loss_ref, dfin_ref, dple_ref, dssd_ref, datt_ref):
        @pl.when(pl.program_id(0) == 0)
        def _():
            for r in (loss_ref, dfin_ref, dple_ref, dssd_ref, datt_ref):
                r[...] = jnp.zeros_like(r)

        lane = _iota((tm, LANES), 1)
        lo = lane < HEAD_DIM
        zs = zs_ref[...]
        sz = _sigmoid(zs)
        yv = y_ref[...]
        ys = yv * (zs * sz)
        yn, rg = [], []
        for g in range(N_GROUPS):
            seg = ys[:, half * g:half * (g + 1)]
            r = lax.rsqrt(jnp.mean(seg * seg, axis=-1, keepdims=True) + EPS)
            yn.append(seg * r)
            rg.append(r)
            ycat_ref[:, half * g:half * (g + 1)] = (yn[g] * sg_ref[:, half * g:half * (g + 1)]).astype(BF16)
        za = za_ref[...]
        sza = _sigmoid(za)
        silu_za = za * sza
        on, ra = [], []
        for jb in range(N_PAIRS):
            blk = o_ref[:, LANES * jb:LANES * (jb + 1)]
            sq = blk * blk
            ms0 = jnp.sum(jnp.where(lo, sq, 0.0), axis=1, keepdims=True) * (1.0 / HEAD_DIM)
            ms1 = jnp.sum(jnp.where(lo, 0.0, sq), axis=1, keepdims=True) * (1.0 / HEAD_DIM)
            r = jnp.where(lo, lax.rsqrt(ms0 + EPS), lax.rsqrt(ms1 + EPS))
            on.append(blk * r)
            ra.append(r)
            an = on[jb] * ag_ref[:, LANES * jb:LANES * (jb + 1)]
            ycat_ref[:, SSD_WIDTH + LANES * jb:SSD_WIDTH + LANES * (jb + 1)] = (
                an * silu_za[:, LANES * jb:LANES * (jb + 1)]).astype(BF16)
        h1 = x_ref[...] + _mm(ycat_ref[...], wo_ref[...])
        r2 = lax.rsqrt(jnp.mean(h1 * h1, axis=-1, keepdims=True) + EPS)
        n2h = h1 * r2
        n2_b = (n2h * pg_ref[...]).astype(BF16)
        gate = _sigmoid(_mm(n2_b, wg_ref[...]))
        p_b = p_ref[...].astype(BF16)
        pp = _mm(p_b, wp_ref[...])
        h2 = h1 + gate * pp
        r3 = lax.rsqrt(jnp.mean(h2 * h2, axis=-1, keepdims=True) + EPS)
        n3 = h2 * r3
        diff = n3 * fg_ref[...] - t_ref[...]
        sq = colsum(diff * diff)
        part = sq[:, 0:LANES]
        for jb in range(1, D_MODEL // LANES):
            part = part + sq[:, LANES * jb:LANES * (jb + 1)]
        loss_ref[...] += part * (0.5 / D_MODEL)
        dout = diff * (1.0 / D_MODEL)
        dfin_ref[...] += colsum(dout * n3)
        dh2 = rms_bwd(dout * fg_ref[...], n3, r3)
        dgl = dh2 * pp * gate * (1.0 - gate)
        dgl_b = dgl.astype(BF16)
        dn2 = _mm_nt(dgl_b, wg_ref[...])
        dple_ref[...] += colsum(dn2 * n2h)
        dh1 = dh2 + rms_bwd(dn2 * pg_ref[...], n2h, r2)
        dh1_b = dh1.astype(BF16)
        dycat = _mm_nt(dh1_b, wo_ref[...])
        dh1_ref[...] = dh1
        dh1b_ref[...] = dh1_b
        n2b_ref[...] = n2_b
        dglb_ref[...] = dgl_b
        dppb_ref[...] = (dh2 * gate).astype(BF16)
        pb_ref[...] = p_b
        for g in range(N_GROUPS):
            cols = slice(half * g, half * (g + 1))
            dys_g = dycat[:, cols]
            dssd_ref[:, cols] += colsum(dys_g * yn[g])
            dys = rms_bwd(dys_g * sg_ref[:, cols], yn[g], rg[g])
            dy_ref[:, cols] = dys * (zs[:, cols] * sz[:, cols])
            dzs_ref[:, cols] = (dys * yv[:, cols] * _dsilu(zs[:, cols], sz[:, cols])).astype(BF16)
        for jb in range(N_PAIRS):
            cols = slice(LANES * jb, LANES * (jb + 1))
            dya = dycat[:, SSD_WIDTH + LANES * jb:SSD_WIDTH + LANES * (jb + 1)]
            ag = ag_ref[:, cols]
            dan = dya * silu_za[:, cols]
            dza_ref[:, cols] = (dya * (on[jb] * ag) * _dsilu(za[:, cols], sza[:, cols])).astype(BF16)
            datt_ref[:, cols] += colsum(dan * on[jb])
            don = dan * ag
            q = don * on[jb]
            m0 = jnp.sum(jnp.where(lo, q, 0.0), axis=1, keepdims=True) * (1.0 / HEAD_DIM)
            m1 = jnp.sum(jnp.where(lo, 0.0, q), axis=1, keepdims=True) * (1.0 / HEAD_DIM)
            do2 = ra[jb] * (don - on[jb] * jnp.where(lo, m0, m1))
            prod = do2 * o_ref[:, cols]
            for e in range(2):
                delta = jnp.sum(jnp.where(lo, prod, 0.0) if e == 0 else jnp.where(lo, 0.0, prod),
                                axis=1, keepdims=True)
                base = jnp.where(lo, do2 if e == 0 else pltpu.roll(do2, HEAD_DIM, 1), 0.0)
                dob_ref[2 * jb + e] = (base - _aug(lane, AUG_A, _split3(delta))).astype(BF16)

    def rows(n, dtype=None):
        return pl.BlockSpec((tm, n), lambda i: (i, 0))

    def out(n, dtype):
        return jax.ShapeDtypeStruct((s, n), dtype)

    vec = _const_spec((1, D_MODEL))
    vshape = jax.ShapeDtypeStruct((1, D_MODEL), F32)
    return pl.pallas_call(
        body, name="post_mix",
        out_shape=(out(D_MODEL, F32), out(SSD_WIDTH, F32), out(SSD_WIDTH, BF16),
                   jax.ShapeDtypeStruct((N_HEADS, s, LANES), BF16),
                   out(ATT_WIDTH, BF16), out(D_INNER, BF16), out(D_MODEL, BF16), out(D_MODEL, BF16),
                   out(D_MODEL, BF16), out(D_MODEL, BF16), out(PLE_DIM, BF16),
                   jax.ShapeDtypeStruct((1, LANES), F32), vshape, vshape, vshape, vshape),
        grid=(s // tm,),
        in_specs=[rows(D_MODEL), rows(SSD_WIDTH), rows(SSD_WIDTH), rows(ATT_WIDTH), rows(ATT_WIDTH),
                  rows(PLE_DIM), rows(D_MODEL), vec, vec, vec, vec,
                  _const_spec((D_INNER, D_MODEL)), _const_spec((D_MODEL, D_MODEL)), _const_spec((PLE_DIM, D_MODEL))],
        out_specs=(rows(D_MODEL), rows(SSD_WIDTH), rows(SSD_WIDTH),
                   pl.BlockSpec((N_HEADS, tm, LANES), lambda i: (0, i, 0)), rows(ATT_WIDTH),
                   rows(D_INNER), rows(D_MODEL), rows(D_MODEL), rows(D_MODEL), rows(D_MODEL), rows(PLE_DIM),
                   _const_spec((1, LANES)), vec, vec, vec, vec),
        compiler_params=_params(("arbitrary",)),
    )(x, y, zs, o, za, p, tgt, ssd_g, att_g_lane, ple_g, fin_g, w_out, w_gate, w_proj)


def in_proj_bwd(dsegs, wsegs, x, g, dh1, pres):
    s = x.shape[0]
    tm = _blk(s, 256)
    nseg = len(dsegs)
    nbig = len(pres)
    nsteps = s // tm

    def body(*refs):
        d_refs = refs[:nseg]
        w_refs = refs[nseg:2 * nseg]
        x_ref, g_ref, dh1_ref = refs[2 * nseg:2 * nseg + 3]
        rest = refs[2 * nseg + 3:]
        pre_refs, (dx_ref, dg_ref), part_refs = rest[:nbig], rest[nbig:nbig + 2], rest[nbig + 2:2 * nbig + 2]
        ssem, rsem, lsem = rest[2 * nbig + 2:]

        @pl.when(pl.program_id(0) == 0)
        def _():
            dg_ref[...] = jnp.zeros_like(dg_ref)
            for cp in scatter_copies(pre_refs, part_refs, ssem, rsem, lsem):
                cp.start()

        @pl.when(pl.program_id(0) == nsteps - 1)
        def _():
            for cp in scatter_copies(pre_refs, part_refs, ssem, rsem, lsem):
                cp.wait()

        du = _mm_nt(d_refs[0][...], w_refs[0][...])
        for k in range(1, nseg):
            du = du + _mm_nt(d_refs[k][...], w_refs[k][...])
        xv = x_ref[...]
        r = lax.rsqrt(jnp.mean(xv * xv, axis=-1, keepdims=True) + EPS)
        xh = xv * r
        dg_ref[...] += jnp.sum(du * xh, axis=0, keepdims=True)
        dxh = du * g_ref[...]
        dx_ref[...] = r * (dxh - xh * jnp.mean(dxh * xh, axis=-1, keepdims=True)) + dh1_ref[...]

    rows = lambda n: pl.BlockSpec((tm, n), lambda i: (i, 0))
    return pl.pallas_call(
        body, name="in_proj_bwd",
        out_shape=tuple([jax.ShapeDtypeStruct((s, D_MODEL), F32), jax.ShapeDtypeStruct((1, D_MODEL), F32)]
                        + [jax.ShapeDtypeStruct(a.shape, a.dtype) for a in pres]),
        grid=(nsteps,),
        in_specs=([rows(d.shape[1]) for d in dsegs] + [_const_spec(w.shape) for w in wsegs]
                  + [rows(D_MODEL), _const_spec((1, D_MODEL)), rows(D_MODEL)] + [ANY] * nbig),
        out_specs=tuple([rows(D_MODEL), _const_spec((1, D_MODEL))] + [ANY] * nbig),
        scratch_shapes=_sems(3 * nbig) + [pltpu.SemaphoreType.DMA((nbig,))],
        compiler_params=_params(("arbitrary",)),
    )(*dsegs, *wsegs, x, g, dh1, *pres)


SMALL_NAMES = ("norm_g", "conv_b", "dt_bias", "a_log", "d_skip", "ssd_norm_g", "fg_bias", "att_norm_g",
               "ple_norm_g", "final_norm_g")
SMALL_SIZES = (1024, 1536, 16, 16, 16, 1024, 16, 64, 1024, 1024)
CONV_W_SIZE = CONV_WIDTH * CONV_CH


def _pack_small(vals):
    flat = jnp.concatenate([v.reshape(-1).astype(F32) for v in vals])
    flat = jnp.pad(flat, (0, SMALL_ROWS * LANES - flat.shape[0]))
    return flat.reshape(SMALL_ROWS, LANES)


def _unpack_small(pack, shapes):
    flat = pack.reshape(-1)
    out, off = [], 0
    for n, shp in zip(SMALL_SIZES, shapes):
        out.append(flat[off:off + n].reshape(shp))
        off += n
    return out


def _row128(v16, offset=0):
    return jnp.pad(v16.reshape(1, N_HEADS).astype(F32), ((0, 0), (offset, LANES - N_HEADS - offset)))


def local_step(prereduce, x, p, tgt, w_in, w_out, w_gate, w_proj, conv_w, norm_g, conv_b, dt_bias, a_log, d_skip,
               ssd_norm_g, fg_bias, att_norm_g, ple_norm_g, final_norm_g):
    widths = (SSD_WIDTH, CONV_CH, N_HEADS, ATT_WIDTH, ATT_WIDTH, ATT_WIDTH, ATT_WIDTH)
    c0, c1, c2, c3, c4, c5, c6, c7 = [sum(widths[:i]) for i in range(len(widths) + 1)]
    w_zs, w_xbc, w_dt = w_in[:, c0:c1], w_in[:, c1:c2], w_in[:, c2:c3]
    w_za, w_q, w_k, w_v, w_f = w_in[:, c3:c4], w_in[:, c4:c5], w_in[:, c5:c6], w_in[:, c6:c7], w_in[:, c7:]
    w_small = jnp.concatenate([w_dt, w_f, jnp.zeros((D_MODEL, LANES - 2 * N_HEADS), BF16)], axis=1)

    dtb_row = _row128(dt_bias)
    a_row = _row128(-jnp.exp(a_log.astype(F32)))
    fgb_row = _row128(fg_bias, N_HEADS)
    dskip_lane = jnp.repeat(d_skip.astype(F32), HEAD_DIM).reshape(1, SSD_WIDTH)
    att_g_lane = jnp.tile(att_norm_g.astype(F32), N_HEADS).reshape(1, ATT_WIDTH)
    row = lambda v: v.reshape(1, -1).astype(F32)

    u = rms_prenorm(x, row(norm_g))
    zs = matmul_rows(u, w_zs, F32, "proj_z_ssd")
    xbc = matmul_rows(u, w_xbc, F32, "proj_xbc")
    za = matmul_rows(u, w_za, F32, "proj_z_att")
    small = matmul_rows(u, w_small, F32, "proj_small")
    cum = forget_cumsum(small, fgb_row)
    qa, ka, va, norms = proj_qkv_heads(u, w_q, w_k, w_v, cum)
    n_seq = x.shape[0]
    first, _ = live_blocks(norms, cum, _blk(n_seq, ATT_BLOCK), _blk(n_seq, ATT_BLOCK))
    _, last_q = live_blocks(norms, cum, _blk(n_seq, ATT_BLOCK_BWD_Q), _blk(n_seq, ATT_BLOCK_BWD))
    pre, xc = conv_fwd(xbc, conv_w, row(conv_b))
    y, states = ssd_fwd(xc, small, dtb_row, a_row, dskip_lane)
    o, qb = attention_fwd(first, qa, ka, va)
    (dh1, dy, dzs, dob, dza, ycat, dh1_b, n2_b, dgl_b, dpp_b, p_b,
     loss_l, dfin, dple, dssd_g, datt_lane) = post_mix(
        x, y, zs, o, za, p, tgt, row(ssd_norm_g), att_g_lane, row(ple_norm_g), row(final_norm_g),
        w_out, w_gate, w_proj)
    dq, dk, dv, dc = attention_bwd(last_q, qb, ka, va, dob)
    dxc, ddt_raw, da, ddtb, ddsk_lane = ssd_bwd(xc, small, states, dy, dtb_row, a_row, dskip_lane)
    dsmall, dfgb = forget_bwd(dc, small, ddt_raw, fgb_row)
    dxbc, dconv_w8, dconv_b = conv_bwd(xbc, pre, dxc, conv_w)
    dsegs = [dzs, dxbc, dza, dq, dk, dv, dsmall]
    wsegs = [w_zs, w_xbc, w_za, w_q, w_k, w_v, w_small]
    dws = [matmul_tn(u, d, "dw_in_%d" % i) for i, d in enumerate(dsegs)]
    dw_in = jnp.concatenate([dws[0], dws[1], dws[6][:, :N_HEADS], dws[2], dws[3], dws[4], dws[5],
                             dws[6][:, N_HEADS:2 * N_HEADS]], axis=1)
    dw_out = matmul_tn(ycat, dh1_b, "dw_out")
    dw_gate = matmul_tn(n2_b, dgl_b, "dw_gate")
    dw_proj = matmul_tn(p_b, dpp_b, "dw_proj")
    dx, dnorm_g, *parts = in_proj_bwd(dsegs, wsegs, x, row(norm_g), dh1, prereduce(dw_in, dw_out, dw_gate, dw_proj))
    small_grads = [
        dnorm_g, dconv_b, ddtb[0, :N_HEADS], (da * a_row)[0, :N_HEADS],
        ddsk_lane.reshape(N_HEADS, HEAD_DIM).sum(axis=1), dssd_g, dfgb[0, N_HEADS:2 * N_HEADS],
        datt_lane.reshape(N_HEADS, HEAD_DIM).sum(axis=0), dple, dfin]
    loss = jnp.sum(loss_l)
    return loss, dx, parts, dconv_w8[:CONV_WIDTH], small_grads


def kernel(x, p, norm_g, w_in, conv_w, conv_b, dt_bias, a_log, d_skip, ssd_norm_g, fg_bias, att_norm_g, w_out, ple_norm_g, w_ple_gate, w_ple_proj, final_norm_g, loss_target, m_norm_g, m_w_in, m_conv_w, m_conv_b, m_dt_bias, m_a_log, m_d_skip, m_ssd_norm_g, m_fg_bias, m_att_norm_g, m_w_out, m_ple_norm_g, m_w_ple_gate, m_w_ple_proj, m_final_norm_g, v_norm_g, v_w_in, v_conv_w, v_conv_b, v_dt_bias, v_a_log, v_d_skip, v_ssd_norm_g, v_fg_bias, v_att_norm_g, v_w_out, v_ple_norm_g, v_w_ple_gate, v_w_ple_proj, v_final_norm_g):
    chip = 2 * lax.axis_index("x") + lax.axis_index("y")
    core = lax.axis_index("c")

    big_w = [w_in[0], w_out[0], w_ple_gate[0], w_ple_proj[0]]
    own = [a.astype(BF16) for a in big_w] + [conv_w[0]]
    gathered = gather_weights(own[:4], own[4])

    def joined(k, axis):
        return jnp.concatenate([jnp.where(chip == j, own[k], gathered[k][j]) for j in range(N_CHIPS)], axis=axis)

    w_in_f, w_out_f, w_gate_f, w_proj_f, conv_w_f = joined(0, 1), joined(1, 0), joined(2, 0), joined(3, 1), joined(4, 1)

    core1 = core.reshape(1).astype(jnp.int32)

    def prereduce(dw_in, dw_out, dw_gate, dw_proj):
        n_in, n_proj = w_in.shape[2], w_ple_proj.shape[2]
        gs = [jnp.stack([dw_in[:, n_in * j:n_in * (j + 1)] for j in range(N_CHIPS)]),
              dw_out.reshape(N_CHIPS, w_out.shape[1], D_MODEL), dw_gate.reshape(N_CHIPS, w_ple_gate.shape[1], D_MODEL),
              jnp.stack([dw_proj[:, n_proj * j:n_proj * (j + 1)] for j in range(N_CHIPS)])]
        return add_halves(core1, gs, halves_to_sibling(gs))

    smalls_w = [norm_g, conv_b, dt_bias, a_log, d_skip, ssd_norm_g, fg_bias, att_norm_g, ple_norm_g, final_norm_g]
    loss_l, dx, parts, dconv_w, small_grads = local_step(
        prereduce, x[0], p[0, 0], loss_target[0], w_in_f, w_out_f, w_gate_f, w_proj_f, conv_w_f,
        *[a.reshape(-1) for a in smalls_w])
    loss = lax.psum(loss_l, ("x", "y", "c"))
    smalls = gather_small(_pack_small(list(small_grads) + [dconv_w]))
    mine = sum_parts(parts)

    g_big, d_big, m_big, v_big = adamw_big(
        core1, mine, swap_halves(mine), big_w, [m_w_in[0], m_w_out[0], m_w_ple_gate[0], m_w_ple_proj[0]],
        [v_w_in[0], v_w_out[0], v_w_ple_gate[0], v_w_ple_proj[0]])
    smalls_m = [m_norm_g, m_conv_b, m_dt_bias, m_a_log, m_d_skip, m_ssd_norm_g, m_fg_bias, m_att_norm_g,
                m_ple_norm_g, m_final_norm_g]
    smalls_v = [v_norm_g, v_conv_b, v_dt_bias, v_a_log, v_d_skip, v_ssd_norm_g, v_fg_bias, v_att_norm_g,
                v_ple_norm_g, v_final_norm_g]
    g_sm, d_sm, m_sm, v_sm = adamw_small(smalls, _pack_small(smalls_w), _pack_small(smalls_m), _pack_small(smalls_v))
    n_small = sum(SMALL_SIZES)
    g_conv_full = g_sm.reshape(-1)[n_small:n_small + CONV_W_SIZE].reshape(CONV_WIDTH, CONV_CH)
    n_conv = conv_w.shape[2]
    g_conv = lax.dynamic_slice_in_dim(g_conv_full, chip * n_conv, n_conv, axis=1)
    d_conv, m_conv, v_conv = adamw_whole(g_conv, conv_w[0], m_conv_w[0], v_conv_w[0], "adamw_conv")

    shapes = [a.shape for a in smalls_w]
    outs = []
    for big, conv, sm in ((g_big, g_conv, g_sm), (d_big, d_conv, d_sm), (m_big, m_conv, m_sm), (v_big, v_conv, v_sm)):
        b_in, b_out, b_gate, b_proj = [a[None] for a in big]
        s_norm, s_convb, s_dtb, s_alog, s_dsk, s_ssdg, s_fgb, s_attg, s_pleg, s_fin = _unpack_small(sm, shapes)
        outs.extend([s_norm, b_in, conv[None], s_convb, s_dtb, s_alog, s_dsk, s_ssdg, s_fgb, s_attg, b_out, s_pleg,
                     b_gate, b_proj, s_fin])
    return (loss, dx[None], *outs)
```

```python
import functools

import jax
import jax.numpy as jnp
from jax import lax
from jax.experimental import pallas as pl
from jax.experimental.pallas import tpu as pltpu

F32 = jnp.float32
BF16 = jnp.bfloat16

D_MODEL = 1024
SSD_WIDTH = 1024
ATT_WIDTH = 1024
N_HEADS = 16
HEAD_DIM = 64
N_GROUPS = 2
D_STATE = 128
CONV_CH = 1536
CONV_WIDTH = 4
CHUNK = 128
PLE_DIM = 256
D_INNER = 2048
EPS = 1e-6
IN_COLS = 6688
N_CHIPS = 4
N_DEV = 8
LANES = 128
N_PAIRS = 8

ADAM_LR = 0.001
ADAM_B1 = 0.9
ADAM_B2 = 0.999
ADAM_EPS = 1e-08
ADAM_WD = 0.01
ADAM_STEP = 10

SMALL_ROWS = 96

NEG_BIG = -1e30
VMEM_LIMIT = 56 * 1024 * 1024

MESH = pl.DeviceIdType.MESH
ANY = pl.BlockSpec(memory_space=pl.ANY)


def _mm(a, b):
    return jnp.dot(a, b, preferred_element_type=F32)


def _mm_nt(a, b):
    return lax.dot_general(a, b, (((1,), (1,)), ((), ())), preferred_element_type=F32)


def _mm_tn(a, b):
    return lax.dot_general(a, b, (((0,), (0,)), ((), ())), preferred_element_type=F32)


def _mm_exact(a, b):
    return jnp.dot(a, b, preferred_element_type=F32, precision=lax.Precision.HIGHEST)


def _softplus(x):
    return jnp.maximum(x, 0.0) + jnp.log1p(jnp.exp(-jnp.abs(x)))


def _sigmoid(x):
    return jax.nn.sigmoid(x)


def _iota(shape, dim):
    return lax.broadcasted_iota(jnp.int32, shape, dim)


def _params(sem=None):
    return pltpu.CompilerParams(dimension_semantics=sem, vmem_limit_bytes=VMEM_LIMIT)


def _blk(n, pref):
    return min(n, pref)


def _const_spec(shape):
    nd = len(shape)
    return pl.BlockSpec(shape, lambda *_: (0,) * nd)


def _chip_peers():
    x, y, c = lax.axis_index("x"), lax.axis_index("y"), lax.axis_index("c")
    return x, y, c, [(1 - x, y, c), (x, 1 - y, c), (1 - x, 1 - y, c)]


def _half(rows, c):
    h = rows // 2
    return pl.ds(pl.multiple_of(c * h, 8), h)


def _sems(n):
    return [pltpu.SemaphoreType.DMA((n,)), pltpu.SemaphoreType.DMA((n,))]


def gather_weights(shards, conv_s):
    n = len(shards)

    def body(*refs):
        ins, conv_in = refs[:n], refs[n]
        outs, conv_out = refs[n + 1:2 * n + 1], refs[2 * n + 1]
        ssem1, rsem1, ssem2, rsem2, c_ssem, c_rsem = refs[2 * n + 2:]
        x, y, c, peers = _chip_peers()
        me = 2 * x + y
        sibling = (x, y, 1 - c)
        first, small = [], []
        for k, peer in enumerate(peers):
            for i in range(n):
                h = _half(ins[i].shape[0], c)
                first.append(pltpu.make_async_remote_copy(
                    src_ref=ins[i].at[h], dst_ref=outs[i].at[me, h], send_sem=ssem1.at[n * k + i],
                    recv_sem=rsem1.at[n * k + i], device_id=peer, device_id_type=MESH))
            small.append(pltpu.make_async_remote_copy(
                src_ref=conv_in, dst_ref=conv_out.at[me], send_sem=c_ssem.at[k], recv_sem=c_rsem.at[k],
                device_id=peer, device_id_type=MESH))
        for cp in first + small:
            cp.start()
        passed = []
        for k, peer in enumerate(peers):
            chip = 2 * peer[0] + peer[1]
            for i in range(n):
                h = _half(ins[i].shape[0], c)
                first[n * k + i].wait_recv()
                fwd = pltpu.make_async_remote_copy(
                    src_ref=outs[i].at[chip, h], dst_ref=outs[i].at[chip, h], send_sem=ssem2.at[n * k + i],
                    recv_sem=rsem2.at[n * k + i], device_id=sibling, device_id_type=MESH)
                fwd.start()
                passed.append(fwd)
        for cp in passed:
            cp.wait_recv()
        for cp in first + passed:
            cp.wait_send()
        for cp in small:
            cp.wait()

    return pl.pallas_call(
        body, name="gather_weights",
        out_shape=tuple(jax.ShapeDtypeStruct((N_CHIPS,) + a.shape, a.dtype) for a in list(shards) + [conv_s]),
        in_specs=[ANY] * (n + 1), out_specs=(ANY,) * (n + 1),
        scratch_shapes=_sems(3 * n) + _sems(3 * n) + _sems(3),
    )(*shards, conv_s)


def halves_to_sibling(gs):
    n = len(gs)

    def body(*refs):
        ins, outs = refs[:n], refs[n:2 * n]
        ssem, rsem = refs[2 * n:]
        x, y, c = lax.axis_index("x"), lax.axis_index("y"), lax.axis_index("c")
        copies = []
        for i in range(n):
            for j in range(N_CHIPS):
                copies.append(pltpu.make_async_remote_copy(
                    src_ref=ins[i].at[j, _half(ins[i].shape[1], 1 - c)], dst_ref=outs[i].at[j],
                    send_sem=ssem.at[N_CHIPS * i + j], recv_sem=rsem.at[N_CHIPS * i + j],
                    device_id=(x, y, 1 - c), device_id_type=MESH))
        for cp in copies:
            cp.start()
        for cp in copies:
            cp.wait()

    return pl.pallas_call(
        body, name="halves_to_sibling",
        out_shape=tuple(jax.ShapeDtypeStruct((N_CHIPS, g.shape[1] // 2, g.shape[2]), F32) for g in gs),
        in_specs=[ANY] * n, out_specs=(ANY,) * n, scratch_shapes=_sems(N_CHIPS * n),
    )(*gs)


RED_GRID = 8


def add_halves(core, gs, rbs):
    n = len(gs)

    def body(c_ref, *refs):
        for i in range(n):
            refs[2 * n + i][...] = (refs[i][...] + refs[n + i][...]).astype(BF16)

    def blk(g):
        return (1, g.shape[1] // 2 // RED_GRID, g.shape[2])

    grid_spec = pltpu.PrefetchScalarGridSpec(
        num_scalar_prefetch=1, grid=(N_CHIPS, RED_GRID),
        in_specs=([pl.BlockSpec(blk(g), lambda j, b, c_ref: (j, c_ref[0] * RED_GRID + b, 0)) for g in gs]
                  + [pl.BlockSpec(blk(g), lambda j, b, c_ref: (j, b, 0)) for g in gs]),
        out_specs=[pl.BlockSpec(blk(g), lambda j, b, c_ref: (j, b, 0)) for g in gs])
    return pl.pallas_call(
        body, name="add_halves", grid_spec=grid_spec,
        out_shape=tuple(jax.ShapeDtypeStruct(r.shape, BF16) for r in rbs),
        compiler_params=_params(("parallel", "parallel")),
    )(core, *gs, *rbs)


def scatter_copies(ins, outs, ssem, rsem, lsem):
    n = len(ins)
    x, y, _, peers = _chip_peers()
    me = 2 * x + y
    copies = [pltpu.make_async_copy(ins[i].at[me], outs[i].at[me], lsem.at[i]) for i in range(n)]
    for k, peer in enumerate(peers):
        dst_chip = 2 * peer[0] + peer[1]
        for i in range(n):
            copies.append(pltpu.make_async_remote_copy(
                src_ref=ins[i].at[dst_chip], dst_ref=outs[i].at[me], send_sem=ssem.at[n * k + i],
                recv_sem=rsem.at[n * k + i], device_id=peer, device_id_type=MESH))
    return copies


def gather_small(small):
    def body(s_ref, smalls_ref, ssem, rsem, lsem):
        x, y, c = lax.axis_index("x"), lax.axis_index("y"), lax.axis_index("c")
        dev = 4 * x + 2 * y + c
        copies = [pltpu.make_async_copy(s_ref, smalls_ref.at[dev], lsem)]
        for k in range(1, N_DEV):
            fx, fy, fc = (k >> 2) & 1, (k >> 1) & 1, k & 1
            peer = ((1 - x) if fx else x, (1 - y) if fy else y, (1 - c) if fc else c)
            copies.append(pltpu.make_async_remote_copy(
                src_ref=s_ref, dst_ref=smalls_ref.at[dev], send_sem=ssem.at[k - 1], recv_sem=rsem.at[k - 1],
                device_id=peer, device_id_type=MESH))
        for cp in copies:
            cp.start()
        for cp in copies:
            cp.wait()

    return pl.pallas_call(
        body, name="gather_small",
        out_shape=jax.ShapeDtypeStruct((N_DEV,) + small.shape, F32),
        in_specs=[ANY], out_specs=ANY,
        scratch_shapes=_sems(N_DEV - 1) + [pltpu.SemaphoreType.DMA],
    )(small)


def sum_parts(parts):
    n = len(parts)

    def body(*refs):
        for i in range(n):
            p_ref = refs[i]
            refs[n + i][...] = ((p_ref[0].astype(F32) + p_ref[1].astype(F32)) + p_ref[2].astype(F32)
                                ) + p_ref[3].astype(F32)

    def rows(p):
        return p.shape[1] // RED_GRID

    return pl.pallas_call(
        body, name="sum_parts",
        out_shape=tuple(jax.ShapeDtypeStruct(p.shape[1:], F32) for p in parts),
        grid=(RED_GRID,),
        in_specs=[pl.BlockSpec((N_CHIPS, rows(p), p.shape[2]), lambda b: (0, b, 0)) for p in parts],
        out_specs=tuple(pl.BlockSpec((rows(p), p.shape[2]), lambda b: (b, 0)) for p in parts),
        compiler_params=_params(("parallel",)),
    )(*parts)


def swap_halves(reds):
    n = len(reds)

    def body(*refs):
        ins, outs = refs[:n], refs[n:2 * n]
        ssem, rsem = refs[2 * n:]
        x, y, c = lax.axis_index("x"), lax.axis_index("y"), lax.axis_index("c")
        copies = [pltpu.make_async_remote_copy(
            src_ref=ins[i], dst_ref=outs[i], send_sem=ssem.at[i], recv_sem=rsem.at[i],
            device_id=(x, y, 1 - c), device_id_type=MESH) for i in range(n)]
        for cp in copies:
            cp.start()
        for cp in copies:
            cp.wait()

    return pl.pallas_call(
        body, name="swap_halves",
        out_shape=tuple(jax.ShapeDtypeStruct(r.shape, F32) for r in reds),
        in_specs=[ANY] * n, out_specs=(ANY,) * n, scratch_shapes=_sems(n),
    )(*reds)


def _adamw(w, g, m, v):
    m = ADAM_B1 * m + (1.0 - ADAM_B1) * g
    v = ADAM_B2 * v + (1.0 - ADAM_B2) * (g * g)
    m_hat = m / (1.0 - ADAM_B1 ** ADAM_STEP)
    v_hat = v / (1.0 - ADAM_B2 ** ADAM_STEP)
    delta = -ADAM_LR * (m_hat / (jnp.sqrt(v_hat) + ADAM_EPS) + ADAM_WD * w)
    return delta, m, v


def adamw_big(core, mine, theirs, ws, ms, vs):
    n = len(ws)
    per_half = RED_GRID // 2

    def body(c_ref, *refs):
        own = (pl.program_id(0) // per_half) == c_ref[0]
        for i in range(n):
            g = jnp.where(own, refs[i][...], refs[n + i][...])
            d, mn, vn = _adamw(refs[2 * n + i][...], g, refs[3 * n + i][...], refs[4 * n + i][...])
            refs[5 * n + i][...] = g
            refs[6 * n + i][...] = d
            refs[7 * n + i][...] = mn
            refs[8 * n + i][...] = vn

    def blk(w):
        return (w.shape[0] // RED_GRID, w.shape[1])

    halves = [pl.BlockSpec(blk(w), lambda b, c_ref: (b % per_half, 0)) for w in ws]
    whole = [pl.BlockSpec(blk(w), lambda b, c_ref: (b, 0)) for w in ws]
    shapes = [jax.ShapeDtypeStruct(w.shape, F32) for w in ws]
    grid_spec = pltpu.PrefetchScalarGridSpec(
        num_scalar_prefetch=1, grid=(RED_GRID,), in_specs=halves * 2 + whole * 3, out_specs=whole * 4)
    outs = pl.pallas_call(
        body, name="adamw_big", out_shape=tuple(shapes * 4), grid_spec=grid_spec,
        compiler_params=_params(("parallel",)),
    )(core, *mine, *theirs, *ws, *ms, *vs)
    return outs[:n], outs[n:2 * n], outs[2 * n:3 * n], outs[3 * n:]


def adamw_whole(g, w, m, v, name):
    def body(g_ref, w_ref, m_ref, v_ref, d_out, m_out, v_out):
        d, mn, vn = _adamw(w_ref[...], g_ref[...], m_ref[...], v_ref[...])
        d_out[...] = d
        m_out[...] = mn
        v_out[...] = vn

    shp = jax.ShapeDtypeStruct(g.shape, F32)
    return pl.pallas_call(body, name=name, out_shape=(shp,) * 3)(g, w, m, v)


def adamw_small(smalls, w, m, v):
    def body(s_ref, w_ref, m_ref, v_ref, g_out, d_out, m_out, v_out):
        g = s_ref[0]
        for k in range(1, N_DEV):
            g = g + s_ref[k]
        d, mn, vn = _adamw(w_ref[...], g, m_ref[...], v_ref[...])
        g_out[...] = g
        d_out[...] = d
        m_out[...] = mn
        v_out[...] = vn

    shp = jax.ShapeDtypeStruct((SMALL_ROWS, LANES), F32)
    return pl.pallas_call(body, name="adamw_small", out_shape=(shp,) * 4)(smalls, w, m, v)


def rms_prenorm(x, g):
    s = x.shape[0]
    tm = _blk(s, 512)

    def body(x_ref, g_ref, u_ref):
        xv = x_ref[...]
        r = lax.rsqrt(jnp.mean(xv * xv, axis=-1, keepdims=True) + EPS)
        u_ref[...] = (xv * r * g_ref[...]).astype(BF16)

    return pl.pallas_call(
        body, name="rms_prenorm", out_shape=jax.ShapeDtypeStruct(x.shape, BF16), grid=(s // tm,),
        in_specs=[pl.BlockSpec((tm, D_MODEL), lambda i: (i, 0)), _const_spec((1, D_MODEL))],
        out_specs=pl.BlockSpec((tm, D_MODEL), lambda i: (i, 0)), compiler_params=_params(("parallel",)),
    )(x, g)


def matmul_rows(a, w, out_dtype, name):
    s, k = a.shape
    n = w.shape[1]
    tm = _blk(s, 512)

    def body(a_ref, w_ref, o_ref):
        o_ref[...] = _mm(a_ref[...], w_ref[...]).astype(out_dtype)

    return pl.pallas_call(
        body, name=name, out_shape=jax.ShapeDtypeStruct((s, n), out_dtype), grid=(s // tm,),
        in_specs=[pl.BlockSpec((tm, k), lambda i: (i, 0)), _const_spec((k, n))],
        out_specs=pl.BlockSpec((tm, n), lambda i: (i, 0)), compiler_params=_params(("parallel",)),
    )(a, w)


def matmul_tn(a, b, name):
    s, m = a.shape
    n = b.shape[1]
    tk = _blk(s, 2048)
    tn = _blk(n, 512)

    def body(a_ref, b_ref, o_ref):
        @pl.when(pl.program_id(1) == 0)
        def _():
            o_ref[...] = jnp.zeros_like(o_ref)

        o_ref[...] += _mm_tn(a_ref[...], b_ref[...])

    return pl.pallas_call(
        body, name=name, out_shape=jax.ShapeDtypeStruct((m, n), F32), grid=(n // tn, s // tk),
        in_specs=[pl.BlockSpec((tk, m), lambda j, i: (i, 0)), pl.BlockSpec((tk, tn), lambda j, i: (i, j))],
        out_specs=pl.BlockSpec((m, tn), lambda j, i: (0, j)),
        compiler_params=_params(("parallel", "arbitrary")),
    )(a, b)


def conv_fwd(xbc, w, b):
    s = xbc.shape[0]
    tm = _blk(s, 256)

    def body(x_ref, t_ref, w_ref, b_ref, pre_ref, act_ref):
        i = pl.program_id(0)
        row8 = _iota((8, LANES), 0)
        for c0 in range(0, CONV_CH, LANES):
            cols = slice(c0, c0 + LANES)
            cur = x_ref[:, cols]
            tail = jnp.where(i > 0, t_ref[:, cols], 0.0)
            wv = w_ref[:, cols]
            bias = b_ref[:, cols]
            acc = cur * wv[3:4, :] + bias
            head = cur[0:8, :] * wv[3:4, :] + bias
            for sh in range(1, CONV_WIDTH):
                wk = wv[3 - sh:4 - sh, :]
                acc = acc + pltpu.roll(cur, sh, 0) * wk
                first = jnp.where(row8 < sh, pltpu.roll(tail, sh, 0), pltpu.roll(cur[0:8, :], sh, 0))
                head = head + first * wk
            pre_ref[:, cols] = acc
            act_ref[:, cols] = acc * _sigmoid(acc)
            pre_ref[0:8, cols] = head
            act_ref[0:8, cols] = head * _sigmoid(head)

    shp = jax.ShapeDtypeStruct(xbc.shape, F32)
    rows = pl.BlockSpec((tm, CONV_CH), lambda i: (i, 0))
    return pl.pallas_call(
        body, name="conv_fwd", out_shape=(shp, shp), grid=(s // tm,),
        in_specs=[rows, pl.BlockSpec((8, CONV_CH), lambda i: (jnp.maximum(i * (tm // 8) - 1, 0), 0)),
                  _const_spec((CONV_WIDTH, CONV_CH)), _const_spec((1, CONV_CH))],
        out_specs=(rows, rows), compiler_params=_params(("parallel",)),
    )(xbc, xbc, w, b)


def conv_bwd(xbc, pre, dact, w):
    s = xbc.shape[0]
    tm = _blk(s, 256)
    nb = s // tm

    def dsilu(p):
        sg = _sigmoid(p)
        return sg * (1.0 + p * (1.0 - sg))

    def body(x_ref, xt_ref, p_ref, pn_ref, d_ref, dn_ref, w_ref, dx_ref, dw_ref, db_ref):
        i = pl.program_id(0)

        @pl.when(i == 0)
        def _():
            dw_ref[...] = jnp.zeros_like(dw_ref)
            db_ref[...] = jnp.zeros_like(db_ref)

        row8 = _iota((8, LANES), 0)
        for c0 in range(0, CONV_CH, LANES):
            cols = slice(c0, c0 + LANES)
            wv = w_ref[:, cols]
            dpre = d_ref[:, cols] * dsilu(p_ref[:, cols])
            dnext = jnp.where(i < nb - 1, dn_ref[:, cols] * dsilu(pn_ref[:, cols]), 0.0)
            cur = x_ref[:, cols]
            tail = jnp.where(i > 0, xt_ref[:, cols], 0.0)
            dx = dpre * wv[3:4, :]
            last = dpre[tm - 8:tm, :] * wv[3:4, :]
            db_ref[:, cols] += jnp.sum(dpre, axis=0, keepdims=True)
            dws = [jnp.sum(dpre * cur, axis=0, keepdims=True)]
            for sh in range(1, CONV_WIDTH):
                wk = wv[3 - sh:4 - sh, :]
                dx = dx + pltpu.roll(dpre, tm - sh, 0) * wk
                nxt = jnp.where(row8 >= 8 - sh, pltpu.roll(dnext, 8 - sh, 0),
                                pltpu.roll(dpre[tm - 8:tm, :], 8 - sh, 0))
                last = last + nxt * wk
                xs = pltpu.roll(cur, sh, 0)
                first = jnp.where(row8 < sh, pltpu.roll(tail, sh, 0), xs[0:8, :])
                dws.append(jnp.sum(dpre * xs, axis=0, keepdims=True)
                           + jnp.sum(dpre[0:8, :] * (first - xs[0:8, :]), axis=0, keepdims=True))
            dx_ref[:, cols] = dx.astype(BF16)
            dx_ref[tm - 8:tm, cols] = last.astype(BF16)
            for sh in range(CONV_WIDTH):
                dw_ref[3 - sh:4 - sh, cols] += dws[sh]

    rows = pl.BlockSpec((tm, CONV_CH), lambda i: (i, 0))
    prev8 = pl.BlockSpec((8, CONV_CH), lambda i: (jnp.maximum(i * (tm // 8) - 1, 0), 0))
    next8 = pl.BlockSpec((8, CONV_CH), lambda i: (jnp.minimum((i + 1) * (tm // 8), s // 8 - 1), 0))
    return pl.pallas_call(
        body, name="conv_bwd",
        out_shape=(jax.ShapeDtypeStruct(xbc.shape, BF16), jax.ShapeDtypeStruct((8, CONV_CH), F32),
                   jax.ShapeDtypeStruct((1, CONV_CH), F32)),
        grid=(nb,),
        in_specs=[rows, prev8, rows, next8, rows, next8, _const_spec((CONV_WIDTH, CONV_CH))],
        out_specs=(rows, _const_spec((8, CONV_CH)), _const_spec((1, CONV_CH))),
        compiler_params=_params(("arbitrary",)),
    )(xbc, xbc, pre, pre, dact, dact, w)


def _pair_lanes(mat, j, lane):
    return jnp.where(lane < HEAD_DIM, mat[:, 2 * j:2 * j + 1], mat[:, 2 * j + 1:2 * j + 2])


def _ssd_chunk_prelude(sm, dtb, a_row, lane, sub):
    raw = sm + dtb
    head_lane = lane < N_HEADS
    dt = jnp.where(head_lane, _softplus(raw), 0.0)
    sig = jnp.where(head_lane, _sigmoid(raw), 0.0)
    tri = (lane <= sub).astype(F32)
    acs = _mm_exact(tri, dt * a_row)
    return dt, sig, acs, acs.T


GROUP_WIDTH = SSD_WIDTH // N_GROUPS
HEADS_PER_GROUP = N_HEADS // N_GROUPS


def _expand_group(mat, g, lane):
    return jnp.concatenate([_pair_lanes(mat, j, lane) for j in range(4 * g, 4 * g + 4)], axis=1)


def _head_sums(q, g):
    row = _iota((GROUP_WIDTH, LANES), 0)
    seg = (_iota((GROUP_WIDTH, LANES), 1) == HEADS_PER_GROUP * g + (row >> 6)).astype(BF16)
    hi = q.astype(BF16)
    lo = (q - hi.astype(F32)).astype(BF16)
    return _mm(hi, seg) + _mm(lo, seg)


def _rows_from_lanes(row512):
    return jnp.broadcast_to(row512, (LANES, GROUP_WIDTH)).T


def ssd_fwd(xc, small, dtb_row, a_row, dskip_lane):
    s = xc.shape[0]
    nc = s // CHUNK

    def body(xc_ref, sm_ref, dtb_ref, a_ref, dsk_ref, y_ref, hs_ref, h_scr):
        c = pl.program_id(0)

        @pl.when(c == 0)
        def _():
            h_scr[...] = jnp.zeros_like(h_scr)

        lane = _iota((CHUNK, LANES), 1)
        sub = _iota((CHUNK, LANES), 0)
        causal = lane <= sub
        dt, _, acs, acs_t = _ssd_chunk_prelude(sm_ref[...], dtb_ref[...], a_ref[...], lane, sub)
        for g in range(N_GROUPS):
            cols = slice(GROUP_WIDTH * g, GROUP_WIDTH * (g + 1))
            b_off = SSD_WIDTH + D_STATE * g
            c_off = SSD_WIDTH + N_GROUPS * D_STATE + D_STATE * g
            b_b = xc_ref[:, b_off:b_off + D_STATE].astype(BF16)
            c_b = xc_ref[:, c_off:c_off + D_STATE].astype(BF16)
            cb = _mm_nt(c_b, b_b)
            x_g = xc_ref[:, cols]
            acs_g = _expand_group(acs, g, lane)
            xdt_g = x_g * _expand_group(dt, g, lane)
            xdt_b = xdt_g.astype(BF16)
            heads = range(HEADS_PER_GROUP * g, HEADS_PER_GROUP * (g + 1))
            m_b = [(cb * jnp.exp(jnp.where(causal, acs[:, h:h + 1] - acs_t[h:h + 1, :], NEG_BIG))).astype(BF16)
                   for h in heads]
            yd = [_mm(m_b[k], xdt_b[:, LANES * (k // 2):LANES * (k // 2 + 1)]) for k in range(HEADS_PER_GROUP)]
            yd_g = jnp.concatenate([jnp.where(lane < HEAD_DIM, yd[2 * k], yd[2 * k + 1]) for k in range(4)], axis=1)
            h_g = h_scr[g]
            t_g = _mm_nt(c_b, h_g.astype(BF16))
            y_ref[:, cols] = yd_g + jnp.exp(acs_g) * t_g + dsk_ref[:, cols] * x_g
            hs_ref[0, g] = h_g
            last_g = acs_g[CHUNK - 1:CHUNK, :]
            w_b = (xdt_g * jnp.exp(last_g - acs_g)).astype(BF16)
            h_scr[g] = h_g * jnp.exp(_rows_from_lanes(last_g)) + _mm_tn(w_b, b_b)

    return pl.pallas_call(
        body, name="ssd_fwd",
        out_shape=(jax.ShapeDtypeStruct((s, SSD_WIDTH), F32),
                   jax.ShapeDtypeStruct((nc, N_GROUPS, GROUP_WIDTH, D_STATE), F32)),
        grid=(nc,),
        in_specs=[pl.BlockSpec((CHUNK, CONV_CH), lambda c: (c, 0)), pl.BlockSpec((CHUNK, LANES), lambda c: (c, 0)),
                  _const_spec((1, LANES)), _const_spec((1, LANES)), _const_spec((1, SSD_WIDTH))],
        out_specs=(pl.BlockSpec((CHUNK, SSD_WIDTH), lambda c: (c, 0)),
                   pl.BlockSpec((1, N_GROUPS, GROUP_WIDTH, D_STATE), lambda c: (c, 0, 0, 0))),
        scratch_shapes=[pltpu.VMEM((N_GROUPS, GROUP_WIDTH, D_STATE), F32)],
        compiler_params=_params(("arbitrary",)),
    )(xc, small, dtb_row, a_row, dskip_lane)


def ssd_bwd(xc, small, states, dy, dtb_row, a_row, dskip_lane):
    s = xc.shape[0]
    nc = s // CHUNK
    rev = lambda c: nc - 1 - c

    def body(xc_ref, sm_ref, hs_ref, dy_ref, dtb_ref, a_ref, dsk_ref,
             dxc_ref, ddt_ref, da_ref, ddtb_ref, ddsk_ref, dh_scr):
        c = pl.program_id(0)

        @pl.when(c == 0)
        def _():
            dh_scr[...] = jnp.zeros_like(dh_scr)
            da_ref[...] = jnp.zeros_like(da_ref)
            ddtb_ref[...] = jnp.zeros_like(ddtb_ref)
            ddsk_ref[...] = jnp.zeros_like(ddsk_ref)

        lane = _iota((CHUNK, LANES), 1)
        sub = _iota((CHUNK, LANES), 0)
        causal = lane <= sub
        upper = lane >= sub
        is_last = sub == CHUNK - 1
        a_row_v = a_ref[...]
        dt, sig, acs, acs_t = _ssd_chunk_prelude(sm_ref[...], dtb_ref[...], a_row_v, lane, sub)
        cd = jnp.exp(acs[CHUNK - 1:CHUNK, :])
        dacs_c = jnp.zeros((CHUNK, LANES), F32)
        dacs_r = jnp.zeros((LANES, CHUNK), F32)
        ddtx = jnp.zeros((CHUNK, LANES), F32)
        for g in range(N_GROUPS):
            cols = slice(GROUP_WIDTH * g, GROUP_WIDTH * (g + 1))
            b_off = SSD_WIDTH + D_STATE * g
            c_off = SSD_WIDTH + N_GROUPS * D_STATE + D_STATE * g
            b_b = xc_ref[:, b_off:b_off + D_STATE].astype(BF16)
            c_b = xc_ref[:, c_off:c_off + D_STATE].astype(BF16)
            cb = _mm_nt(c_b, b_b)
            cb_t = _mm_nt(b_b, c_b)
            x_g = xc_ref[:, cols]
            dy_g = dy_ref[:, cols]
            dt_g = _expand_group(dt, g, lane)
            acs_g = _expand_group(acs, g, lane)
            last_g = acs_g[CHUNK - 1:CHUNK, :]
            e_g = jnp.exp(acs_g)
            dte_g = jnp.exp(last_g - acs_g)
            xdt_g = x_g * dt_g
            xdt_b = xdt_g.astype(BF16)
            h_g = hs_ref[0, g]
            dh_g = dh_scr[g]
            h_b = h_g.astype(BF16)
            dh_b = dh_g.astype(BF16)
            heads = list(range(HEADS_PER_GROUP * g, HEADS_PER_GROUP * (g + 1)))
            segs = [acs[:, h:h + 1] - acs_t[h:h + 1, :] for h in heads]
            lms = [jnp.exp(jnp.where(causal, sg, NEG_BIG)) for sg in segs]
            mts = [(cb_t * jnp.exp(jnp.where(upper, -sg, NEG_BIG))).astype(BF16) for sg in segs]
            dyh = []
            for k in range(HEADS_PER_GROUP):
                blk = dy_g[:, LANES * (k // 2):LANES * (k // 2 + 1)]
                in_head = (lane < HEAD_DIM) if k % 2 == 0 else (lane >= HEAD_DIM)
                dyh.append(jnp.where(in_head, blk, 0.0).astype(BF16))
            dms = [_mm_nt(dyh[k], xdt_b[:, LANES * (k // 2):LANES * (k // 2 + 1)]) for k in range(HEADS_PER_GROUP)]
            dxs = [_mm(mts[k], dyh[k]) for k in range(HEADS_PER_GROUP)]
            dcb = jnp.zeros((CHUNK, CHUNK), F32)
            for k, h in enumerate(heads):
                gmat = dms[k] * (cb * lms[k])
                dacs_c = dacs_c + jnp.where(lane == h, jnp.sum(gmat, axis=1, keepdims=True), 0.0)
                dacs_r = dacs_r - jnp.where(sub == h, jnp.sum(gmat, axis=0, keepdims=True), 0.0)
                dcb = dcb + dms[k] * lms[k]
            dxdt_g = jnp.concatenate([dxs[2 * k] + dxs[2 * k + 1] for k in range(4)], axis=1)
            t_g = _mm_nt(c_b, h_b)
            dacs_c = dacs_c + _head_sums(dy_g * e_g * t_g, g)
            dt_b = (dy_g * e_g).astype(BF16)
            dc_acc = _mm(dt_b, h_b)
            dh_prev = _mm_tn(dt_b, c_b)
            dw_g = _mm_nt(b_b, dh_b)
            w_g = xdt_g * dte_g
            dxdt_g = dxdt_g + dw_g * dte_g
            db_acc = _mm(w_g.astype(BF16), dh_b)
            r2 = _head_sums(dw_g * w_g, g)
            dacs_c = dacs_c + jnp.where(is_last, jnp.sum(r2, axis=0, keepdims=True), 0.0) - r2
            q3 = jnp.sum(dh_g * h_g, axis=1, keepdims=True)
            for k, h in enumerate(heads):
                tot = jnp.sum(q3[HEAD_DIM * k:HEAD_DIM * (k + 1), :], keepdims=True) * cd[:, h:h + 1]
                dacs_c = dacs_c + jnp.where(is_last & (lane == h), tot, 0.0)
            dh_scr[g] = dh_prev + dh_g * jnp.exp(_rows_from_lanes(last_g))
            dxc_ref[:, cols] = dxdt_g * dt_g + dsk_ref[:, cols] * dy_g
            ddtx = ddtx + _head_sums(dxdt_g * x_g, g)
            ddsk_ref[:, cols] += jnp.sum(dy_g * x_g, axis=0, keepdims=True)
            dxc_ref[:, b_off:b_off + D_STATE] = db_acc + _mm(dcb.T.astype(BF16), c_b)
            dxc_ref[:, c_off:c_off + D_STATE] = dc_acc + _mm(dcb.astype(BF16), b_b)
        dacs = dacs_c + dacs_r.T
        dadt = _mm_exact((lane >= sub).astype(F32), dacs)
        ddt = dadt * a_row_v + ddtx
        ddt_raw = ddt * sig
        ddt_ref[...] = ddt_raw
        da_ref[...] += jnp.sum(dadt * dt, axis=0, keepdims=True)
        ddtb_ref[...] += jnp.sum(ddt_raw, axis=0, keepdims=True)

    return pl.pallas_call(
        body, name="ssd_bwd",
        out_shape=(jax.ShapeDtypeStruct((s, CONV_CH), F32), jax.ShapeDtypeStruct((s, LANES), F32),
                   jax.ShapeDtypeStruct((1, LANES), F32), jax.ShapeDtypeStruct((1, LANES), F32),
                   jax.ShapeDtypeStruct((1, SSD_WIDTH), F32)),
        grid=(nc,),
        in_specs=[pl.BlockSpec((CHUNK, CONV_CH), lambda c: (rev(c), 0)),
                  pl.BlockSpec((CHUNK, LANES), lambda c: (rev(c), 0)),
                  pl.BlockSpec((1, N_GROUPS, GROUP_WIDTH, D_STATE), lambda c: (rev(c), 0, 0, 0)),
                  pl.BlockSpec((CHUNK, SSD_WIDTH), lambda c: (rev(c), 0)),
                  _const_spec((1, LANES)), _const_spec((1, LANES)), _const_spec((1, SSD_WIDTH))],
        out_specs=(pl.BlockSpec((CHUNK, CONV_CH), lambda c: (rev(c), 0)),
                   pl.BlockSpec((CHUNK, LANES), lambda c: (rev(c), 0)),
                   _const_spec((1, LANES)), _const_spec((1, LANES)), _const_spec((1, SSD_WIDTH))),
        scratch_shapes=[pltpu.VMEM((N_GROUPS, GROUP_WIDTH, D_STATE), F32)],
        compiler_params=_params(("arbitrary",)),
    )(xc, small, states, dy, dtb_row, a_row, dskip_lane)


FORGET_BLOCK = 512


def forget_cumsum(small, fgb_row):
    s = small.shape[0]
    t = _blk(s, FORGET_BLOCK)
    nb = s // t

    def body(sm_ref, b_ref, cc_ref, carry):
        i = pl.program_id(0)

        @pl.when(i == 0)
        def _():
            carry[...] = jnp.zeros_like(carry)

        lane = _iota((t, LANES), 1)
        in_f = (lane >= N_HEADS) & (lane < 2 * N_HEADS)
        logf = jnp.where(in_f, -_softplus(-(sm_ref[...] + b_ref[...])), 0.0)
        tri = (_iota((t, t), 1) <= _iota((t, t), 0)).astype(F32)
        cum = _mm_exact(tri, logf) + carry[0:1, :]
        cc_ref[...] = cum
        carry[...] = jnp.broadcast_to(cum[t - 1:t, :], (8, LANES))

    return pl.pallas_call(
        body, name="forget_cumsum",
        out_shape=jax.ShapeDtypeStruct((s, LANES), F32),
        grid=(nb,),
        in_specs=[pl.BlockSpec((t, LANES), lambda i: (i, 0)), _const_spec((1, LANES))],
        out_specs=pl.BlockSpec((t, LANES), lambda i: (i, 0)),
        scratch_shapes=[pltpu.VMEM((8, LANES), F32)],
        compiler_params=_params(("arbitrary",)),
    )(small, fgb_row)


def forget_bwd(dc, small, ddt_raw, fgb_row):
    s = small.shape[0]
    t = _blk(s, FORGET_BLOCK)
    nb = s // t
    rev = lambda i: nb - 1 - i

    def body(dc_ref, sm_ref, ddt_ref, b_ref, ds_ref, dfb_ref, carry):
        i = pl.program_id(0)

        @pl.when(i == 0)
        def _():
            carry[...] = jnp.zeros_like(carry)
            dfb_ref[...] = jnp.zeros_like(dfb_ref)

        lane = _iota((t, LANES), 1)
        rows = dc_ref[...].T
        tri = (_iota((t, t), 1) <= _iota((t, t), 0)).astype(F32)
        rc = _mm_exact(rows, tri) + carry[:, 0:1]
        carry[...] = jnp.broadcast_to(rc[:, 0:1], (LANES, LANES))
        in_f = (lane >= N_HEADS) & (lane < 2 * N_HEADS)
        df = jnp.where(in_f, rc.T * _sigmoid(-(sm_ref[...] + b_ref[...])), 0.0)
        ds_ref[...] = (df + ddt_ref[...]).astype(BF16)
        dfb_ref[...] += jnp.sum(df, axis=0, keepdims=True)

    blk = pl.BlockSpec((t, LANES), lambda i: (rev(i), 0))
    return pl.pallas_call(
        body, name="forget_bwd",
        out_shape=(jax.ShapeDtypeStruct((s, LANES), BF16), jax.ShapeDtypeStruct((1, LANES), F32)),
        grid=(nb,),
        in_specs=[blk, blk, blk, _const_spec((1, LANES))],
        out_specs=(blk, _const_spec((1, LANES))),
        scratch_shapes=[pltpu.VMEM((LANES, LANES), F32)],
        compiler_params=_params(("arbitrary",)),
    )(dc, small, ddt_raw, fgb_row)


ATT_BLOCK = 1024
ATT_BLOCK_BWD = 512
ATT_BLOCK_BWD_Q = 512
ATT_SCALE = HEAD_DIM ** -0.5
AUG_A = HEAD_DIM
AUG_B = HEAD_DIM + 3


def _split3(c):
    hi = c.astype(BF16).astype(F32)
    r = c - hi
    mid = r.astype(BF16).astype(F32)
    return hi, mid, (r - mid).astype(BF16).astype(F32)


def _aug(lane, first, parts=None, value=1.0):
    if parts is None:
        return jnp.where((lane >= first) & (lane < first + 3), value, 0.0)
    return (jnp.where(lane == first, parts[0], 0.0) + jnp.where(lane == first + 1, parts[1], 0.0)
            + jnp.where(lane == first + 2, parts[2], 0.0))


def _pack_pair(a0, a1, lane):
    return jnp.where(lane < HEAD_DIM, a0, pltpu.roll(a1, HEAD_DIM, 1))


def proj_qkv_heads(u, w_q, w_k, w_v, cum):
    s = u.shape[0]
    tm = _blk(s, 256)

    def body(u_ref, wq_ref, wk_ref, wv_ref, c_ref, qa_ref, ka_ref, va_ref, nrm_ref):
        lane = _iota((tm, LANES), 1)
        lo = lane < HEAD_DIM
        uv = u_ref[...]
        qf = _mm(uv, wq_ref[...]) * ATT_SCALE
        kf = _mm(uv, wk_ref[...])
        vf = _mm(uv, wv_ref[...])
        cc = c_ref[...]
        ones_a = _aug(lane, AUG_A)
        ones_b = _aug(lane, AUG_B)
        sub8 = _iota((8, LANES), 0)
        nrm = jnp.zeros((8, LANES), F32)
        for h in range(N_HEADS):
            j, e = divmod(h, 2)

            def head(full):
                blk = full[:, LANES * j:LANES * (j + 1)]
                if e == 1:
                    blk = pltpu.roll(blk, HEAD_DIM, 1)
                return jnp.where(lo, blk, 0.0)

            parts = _split3(cc[:, N_HEADS + h:N_HEADS + h + 1])
            qh, kh = head(qf), head(kf)
            qa_ref[h] = (qh + _aug(lane, AUG_A, parts) + ones_b).astype(BF16)
            ka_ref[h] = (kh + ones_a - _aug(lane, AUG_B, parts)).astype(BF16)
            va_ref[h] = (head(vf) + ones_a).astype(BF16)
        seg = (_iota((ATT_WIDTH, LANES), 1) == (_iota((ATT_WIDTH, LANES), 0) >> 6)).astype(BF16)
        for r, val in enumerate((qf, kf)):
            sq = val * val
            hi = sq.astype(BF16)
            tot = _mm(hi, seg) + _mm((sq - hi.astype(F32)).astype(BF16), seg)
            nrm = nrm + jnp.where(sub8 == r, jnp.max(tot, axis=0, keepdims=True), 0.0)
        nrm_ref[0] = nrm

    shp = jax.ShapeDtypeStruct((N_HEADS, s, LANES), BF16)
    hspec = pl.BlockSpec((N_HEADS, tm, LANES), lambda i: (0, i, 0))
    wspec = _const_spec((D_MODEL, ATT_WIDTH))
    return pl.pallas_call(
        body, name="proj_qkv_heads",
        out_shape=(shp, shp, shp, jax.ShapeDtypeStruct((s // tm, 8, LANES), F32)), grid=(s // tm,),
        in_specs=[pl.BlockSpec((tm, D_MODEL), lambda i: (i, 0)), wspec, wspec, wspec,
                  pl.BlockSpec((tm, LANES), lambda i: (i, 0))],
        out_specs=(hspec, hspec, hspec, pl.BlockSpec((1, 8, LANES), lambda i: (i, 0, 0))),
        compiler_params=_params(("parallel",)),
    )(u, w_q, w_k, w_v, cum)


SKIP_BELOW = -110.0


def live_blocks(norms, cum, tq, tk):
    qn = jnp.sqrt(jnp.max(norms[:, 0, :N_HEADS], axis=0))
    kn = jnp.sqrt(jnp.max(norms[:, 1, :N_HEADS], axis=0))
    bound = 2.05 * qn * kn + 2.0
    c_first = cum[0::tq, N_HEADS:2 * N_HEADS]
    c_last = cum[tk - 1::tk, N_HEADS:2 * N_HEADS]
    nq, nk = c_first.shape[0], c_last.shape[0]
    top = bound[None, None, :] + c_first[:, None, :] - c_last[None, :, :]
    before = (jnp.arange(nk)[None, :] + 1) * tk <= jnp.arange(nq)[:, None] * tq
    dead = before[:, :, None] & ~(top >= SKIP_BELOW)
    first = jnp.sum(dead, axis=1).astype(jnp.int32).T
    last_q = jnp.sum(first[:, None, :] <= jnp.arange(nk)[None, :, None], axis=2).astype(jnp.int32) - 1
    return first, last_q


def attention_fwd(first, qa, ka, va):
    s = qa.shape[1]
    t = _blk(s, ATT_BLOCK)
    nq = s // t

    def body(first_ref, qa_ref, ka_ref, va_ref, o_ref, qb_ref, m_scr, acc_scr, alpha_scr, p_scr, s_scr):
        qi = pl.program_id(1)
        starts = [first_ref[2 * pl.program_id(0) + e, qi] for e in range(2)]
        k0 = jnp.maximum(starts[0], starts[1])
        m_scr[...] = jnp.full_like(m_scr, NEG_BIG)
        acc_scr[...] = jnp.zeros_like(acc_scr)

        def kv_rows(kb):
            return pl.ds(pl.multiple_of(kb * t, t), t)

        def logits(kb, masked, heads=(0, 1)):
            for e in heads:
                sc = _mm_nt(qa_ref[e], ka_ref[e, kv_rows(kb), :])
                if masked:
                    sc = jnp.where(_iota((t, t), 0) >= _iota((t, t), 1), sc, NEG_BIG)
                s_scr[e] = sc

        def probs(heads=(0, 1)):
            for e in heads:
                cmax = s_scr[e, :, 0:LANES]
                for c in range(1, t // LANES):
                    cmax = jnp.maximum(cmax, s_scr[e, :, LANES * c:LANES * (c + 1)])
                m_old = m_scr[e]
                m_new = jnp.maximum(m_old, jnp.max(cmax, axis=1, keepdims=True))
                alpha_scr[e] = jnp.exp(m_old - m_new)
                m_scr[e] = m_new
                for c in range(t // LANES):
                    cols = slice(LANES * c, LANES * (c + 1))
                    p_scr[e, :, cols] = jnp.exp(s_scr[e, :, cols] - m_new).astype(BF16)

        def accumulate(kb, heads=(0, 1)):
            for e in heads:
                acc_scr[e] = alpha_scr[e] * acc_scr[e] + _mm(p_scr[e], va_ref[e, kv_rows(kb), :])

        for e in range(2):
            def alone(kb, carry, e=e):
                logits(kb, False, (e,))
                probs((e,))
                accumulate(kb, (e,))
                return carry

            lax.fori_loop(starts[e], k0, alone, 0)

        def loop_body(kb, carry):
            logits(kb, False)
            for e in range(2):
                accumulate(kb - 1, (e,))
                probs((e,))
            return carry

        @pl.when(qi > k0)
        def _():
            logits(k0, False)
            probs()

        lax.fori_loop(k0 + 1, qi, loop_body, 0)

        @pl.when(qi > k0)
        def _():
            logits(qi, True)
            accumulate(qi - 1)
            probs()

        @pl.when(qi == k0)
        def _():
            logits(qi, True)
            probs()

        accumulate(qi)

        lane = _iota((t, LANES), 1)
        outs = []
        for e in range(2):
            acc = acc_scr[e]
            l = acc[:, AUG_A:AUG_A + 1]
            outs.append(acc / l)
            lse = m_scr[e][:, 0:1] + jnp.log(l)
            q32 = qa_ref[e].astype(F32)
            c = q32[:, AUG_A:AUG_A + 1] + q32[:, AUG_A + 1:AUG_A + 2] + q32[:, AUG_A + 2:AUG_A + 3]
            qb = jnp.where(lane < HEAD_DIM, q32, 0.0) + _aug(lane, AUG_A, _split3(c - lse)) + _aug(lane, AUG_B)
            qb_ref[e] = qb.astype(BF16)
        o_ref[...] = _pack_pair(outs[0], outs[1], lane)

    grid_spec = pltpu.PrefetchScalarGridSpec(
        num_scalar_prefetch=1, grid=(N_PAIRS, nq),
        in_specs=[pl.BlockSpec((2, t, LANES), lambda j, qi, f: (j, qi, 0)),
                  pl.BlockSpec((2, s, LANES), lambda j, qi, f: (j, 0, 0)),
                  pl.BlockSpec((2, s, LANES), lambda j, qi, f: (j, 0, 0))],
        out_specs=[pl.BlockSpec((t, LANES), lambda j, qi, f: (qi, j)),
                   pl.BlockSpec((2, t, LANES), lambda j, qi, f: (j, qi, 0))],
        scratch_shapes=[pltpu.VMEM((2, t, LANES), F32), pltpu.VMEM((2, t, LANES), F32),
                        pltpu.VMEM((2, t, LANES), F32), pltpu.VMEM((2, t, t), BF16), pltpu.VMEM((2, t, t), F32)])
    return pl.pallas_call(
        body, name="attention_fwd", grid_spec=grid_spec,
        out_shape=(jax.ShapeDtypeStruct((s, ATT_WIDTH), F32), jax.ShapeDtypeStruct((N_HEADS, s, LANES), BF16)),
        compiler_params=_params(("parallel", "parallel")),
    )(first, qa, ka, va)


def attention_bwd(last_q, qb, ka, va, dob):
    s = qb.shape[1]
    t = _blk(s, ATT_BLOCK_BWD)
    tq = _blk(s, ATT_BLOCK_BWD_Q)
    nq = s // tq
    per_q = tq // t

    def body(last_ref, qb_ref, dob_ref, ka_ref, va_ref, dq_ref, dk_ref, dv_ref, dc_ref, dq_scr, dk_scr, dv_scr):
        j, ki = pl.program_id(0), pl.program_id(1)

        @pl.when((j == 0) & (ki == 0))
        def _():
            dc_ref[...] = jnp.zeros_like(dc_ref)

        @pl.when(ki == 0)
        def _():
            dq_scr[...] = jnp.zeros_like(dq_scr)

        dk_scr[...] = jnp.zeros_like(dk_scr)
        dv_scr[...] = jnp.zeros_like(dv_scr)

        def q_step(qblk, masked, heads=(0, 1)):
            rows = pl.ds(pl.multiple_of(qblk * tq, tq), tq)
            scs = [_mm_nt(qb_ref[e, rows, :], ka_ref[e]) for e in heads]
            dps = [_mm_nt(dob_ref[e, rows, :], va_ref[e]) for e in heads]
            for e, sc, dp in zip(heads, scs, dps):
                q = qb_ref[e, rows, :]
                do = dob_ref[e, rows, :]
                if masked:
                    keep = (_iota((tq, t), 0) - _iota((tq, t), 1)) >= ki * t - qblk * tq
                    sc = jnp.where(keep, sc, NEG_BIG)
                p = jnp.exp(sc)
                ds_b = (p * dp).astype(BF16)
                dv_scr[e] += _mm_tn(p.astype(BF16), do)
                dk_scr[e] += _mm_tn(ds_b, q)
                dq_scr[e, rows, :] += _mm(ds_b, ka_ref[e])

        def loop_body(qblk, carry):
            q_step(qblk, False)
            return carry

        ends = [last_ref[2 * j + e, ki] + 1 for e in range(2)]
        both = jnp.minimum(ends[0], ends[1])
        diag = ki // per_q
        q_step(diag, True)
        lax.fori_loop(diag + 1, both, loop_body, 0)
        for e in range(2):
            def alone(qblk, carry, e=e):
                q_step(qblk, False, (e,))
                return carry

            lax.fori_loop(both, ends[e], alone, 0)

        lane = _iota((t, LANES), 1)
        dk_ref[...] = _pack_pair(dk_scr[0], dk_scr[1], lane).astype(BF16)
        dv_ref[...] = _pack_pair(dv_scr[0], dv_scr[1], lane).astype(BF16)
        rows = pl.ds(pl.multiple_of(ki * t, t), t)
        dc_ref[rows, :] -= (jnp.where(lane == N_HEADS + 2 * j, dk_scr[0][:, AUG_B:AUG_B + 1], 0.0)
                            + jnp.where(lane == N_HEADS + 2 * j + 1, dk_scr[1][:, AUG_B:AUG_B + 1], 0.0))

        @pl.when(ki == s // t - 1)
        def _():
            for blk in range(s // t):
                rws = pl.ds(blk * t, t)
                d0 = dq_scr[0, rws, :]
                d1 = dq_scr[1, rws, :]
                dq_ref[rws, :] = (_pack_pair(d0, d1, lane) * ATT_SCALE).astype(BF16)
                dc_ref[rws, :] += (jnp.where(lane == N_HEADS + 2 * j, d0[:, AUG_A:AUG_A + 1], 0.0)
                                   + jnp.where(lane == N_HEADS + 2 * j + 1, d1[:, AUG_A:AUG_A + 1], 0.0))

    full = pl.BlockSpec((2, s, LANES), lambda j, ki, f: (j, 0, 0))
    blk = pl.BlockSpec((2, t, LANES), lambda j, ki, f: (j, ki, 0))
    pair = pl.BlockSpec((t, LANES), lambda j, ki, f: (ki, j))
    wide = jax.ShapeDtypeStruct((s, ATT_WIDTH), BF16)
    grid_spec = pltpu.PrefetchScalarGridSpec(
        num_scalar_prefetch=1, grid=(N_PAIRS, s // t),
        in_specs=[full, full, blk, blk],
        out_specs=[pl.BlockSpec((s, LANES), lambda j, ki, f: (0, j)), pair, pair,
                   pl.BlockSpec((s, LANES), lambda j, ki, f: (0, 0))],
        scratch_shapes=[pltpu.VMEM((2, s, LANES), F32), pltpu.VMEM((2, t, LANES), F32),
                        pltpu.VMEM((2, t, LANES), F32)])
    return pl.pallas_call(
        body, name="attention_bwd", grid_spec=grid_spec,
        out_shape=(wide, wide, wide, jax.ShapeDtypeStruct((s, LANES), F32)),
        compiler_params=_params(("arbitrary", "arbitrary")),
    )(last_q, qb, dob, ka, va)


def _dsilu(z, sg):
    return sg * (1.0 + z * (1.0 - sg))


def post_mix(x, y, zs, o, za, p, tgt, ssd_g, att_g_lane, ple_g, fin_g, w_out, w_gate, w_proj):
    s = x.shape[0]
    tm = _blk(s, 256)
    half = SSD_WIDTH // N_GROUPS

    def rms_bwd(dy, yn, r):
        return r * (dy - yn * jnp.mean(dy * yn, axis=-1, keepdims=True))

    def colsum(a):
        return jnp.sum(a, axis=0, keepdims=True)

    def body(x_ref, y_ref, zs_ref, o_ref, za_ref, p_ref, t_ref, sg_ref, ag_ref, pg_ref, fg_ref,
             wo_ref, wg_ref, wp_ref,
             dh1_ref, dy_ref, dzs_ref, dob_ref, dza_ref, ycat_ref, dh1b_ref, n2b_ref, dglb_ref, dppb_ref, pb_ref,
             loss_ref, dfin_ref, dple_ref, dssd_ref, datt_ref):
        @pl.when(pl.program_id(0) == 0)
        def _():
            for r in (loss_ref, dfin_ref, dple_ref, dssd_ref, datt_ref):
                r[...] = jnp.zeros_like(r)

        lane = _iota((tm, LANES), 1)
        lo = lane < HEAD_DIM
        zs = zs_ref[...]
        sz = _sigmoid(zs)
        yv = y_ref[...]
        ys = yv * (zs * sz)
        yn, rg = [], []
        for g in range(N_GROUPS):
            seg = ys[:, half * g:half * (g + 1)]
            r = lax.rsqrt(jnp.mean(seg * seg, axis=-1, keepdims=True) + EPS)
            yn.append(seg * r)
            rg.append(r)
            ycat_ref[:, half * g:half * (g + 1)] = (yn[g] * sg_ref[:, half * g:half * (g + 1)]).astype(BF16)
        za = za_ref[...]
        sza = _sigmoid(za)
        silu_za = za * sza
        on, ra = [], []
        for jb in range(N_PAIRS):
            blk = o_ref[:, LANES * jb:LANES * (jb + 1)]
            sq = blk * blk
            ms0 = jnp.sum(jnp.where(lo, sq, 0.0), axis=1, keepdims=True) * (1.0 / HEAD_DIM)
            ms1 = jnp.sum(jnp.where(lo, 0.0, sq), axis=1, keepdims=True) * (1.0 / HEAD_DIM)
            r = jnp.where(lo, lax.rsqrt(ms0 + EPS), lax.rsqrt(ms1 + EPS))
            on.append(blk * r)
            ra.append(r)
            an = on[jb] * ag_ref[:, LANES * jb:LANES * (jb + 1)]
            ycat_ref[:, SSD_WIDTH + LANES * jb:SSD_WIDTH + LANES * (jb + 1)] = (
                an * silu_za[:, LANES * jb:LANES * (jb + 1)]).astype(BF16)
        h1 = x_ref[...] + _mm(ycat_ref[...], wo_ref[...])
        r2 = lax.rsqrt(jnp.mean(h1 * h1, axis=-1, keepdims=True) + EPS)
        n2h = h1 * r2
        n2_b = (n2h * pg_ref[...]).astype(BF16)
        gate = _sigmoid(_mm(n2_b, wg_ref[...]))
        p_b = p_ref[...].astype(BF16)
        pp = _mm(p_b, wp_ref[...])
        h2 = h1 + gate * pp
        r3 = lax.rsqrt(jnp.mean(h2 * h2, axis=-1, keepdims=True) + EPS)
        n3 = h2 * r3
        diff = n3 * fg_ref[...] - t_ref[...]
        sq = colsum(diff * diff)
        part = sq[:, 0:LANES]
        for jb in range(1, D_MODEL // LANES):
            part = part + sq[:, LANES * jb:LANES * (jb + 1)]
        loss_ref[...] += part * (0.5 / D_MODEL)
        dout = diff * (1.0 / D_MODEL)
        dfin_ref[...] += colsum(dout * n3)
        dh2 = rms_bwd(dout * fg_ref[...], n3, r3)
        dgl = dh2 * pp * gate * (1.0 - gate)
        dgl_b = dgl.astype(BF16)
        dn2 = _mm_nt(dgl_b, wg_ref[...])
        dple_ref[...] += colsum(dn2 * n2h)
        dh1 = dh2 + rms_bwd(dn2 * pg_ref[...], n2h, r2)
        dh1_b = dh1.astype(BF16)
        dycat = _mm_nt(dh1_b, wo_ref[...])
        dh1_ref[...] = dh1
        dh1b_ref[...] = dh1_b
        n2b_ref[...] = n2_b
        dglb_ref[...] = dgl_b
        dppb_ref[...] = (dh2 * gate).astype(BF16)
        pb_ref[...] = p_b
        for g in range(N_GROUPS):
            cols = slice(half * g, half * (g + 1))
            dys_g = dycat[:, cols]
            dssd_ref[:, cols] += colsum(dys_g * yn[g])
            dys = rms_bwd(dys_g * sg_ref[:, cols], yn[g], rg[g])
            dy_ref[:, cols] = dys * (zs[:, cols] * sz[:, cols])
            dzs_ref[:, cols] = (dys * yv[:, cols] * _dsilu(zs[:, cols], sz[:, cols])).astype(BF16)
        for jb in range(N_PAIRS):
            cols = slice(LANES * jb, LANES * (jb + 1))
            dya = dycat[:, SSD_WIDTH + LANES * jb:SSD_WIDTH + LANES * (jb + 1)]
            ag = ag_ref[:, cols]
            dan = dya * silu_za[:, cols]
            dza_ref[:, cols] = (dya * (on[jb] * ag) * _dsilu(za[:, cols], sza[:, cols])).astype(BF16)
            datt_ref[:, cols] += colsum(dan * on[jb])
            don = dan * ag
            q = don * on[jb]
            m0 = jnp.sum(jnp.where(lo, q, 0.0), axis=1, keepdims=True) * (1.0 / HEAD_DIM)
            m1 = jnp.sum(jnp.where(lo, 0.0, q), axis=1, keepdims=True) * (1.0 / HEAD_DIM)
            do2 = ra[jb] * (don - on[jb] * jnp.where(lo, m0, m1))
            prod = do2 * o_ref[:, cols]
            for e in range(2):
                delta = jnp.sum(jnp.where(lo, prod, 0.0) if e == 0 else jnp.where(lo, 0.0, prod),
                                axis=1, keepdims=True)
                base = jnp.where(lo, do2 if e == 0 else pltpu.roll(do2, HEAD_DIM, 1), 0.0)
                dob_ref[2 * jb + e] = (base - _aug(lane, AUG_A, _split3(delta))).astype(BF16)

    def rows(n, dtype=None):
        return pl.BlockSpec((tm, n), lambda i: (i, 0))

    def out(n, dtype):
        return jax.ShapeDtypeStruct((s, n), dtype)

    vec = _const_spec((1, D_MODEL))
    vshape = jax.ShapeDtypeStruct((1, D_MODEL), F32)
    return pl.pallas_call(
        body, name="post_mix",
        out_shape=(out(D_MODEL, F32), out(SSD_WIDTH, F32), out(SSD_WIDTH, BF16),
                   jax.ShapeDtypeStruct((N_HEADS, s, LANES), BF16),
                   out(ATT_WIDTH, BF16), out(D_INNER, BF16), out(D_MODEL, BF16), out(D_MODEL, BF16),
                   out(D_MODEL, BF16), out(D_MODEL, BF16), out(PLE_DIM, BF16),
                   jax.ShapeDtypeStruct((1, LANES), F32), vshape, vshape, vshape, vshape),
        grid=(s // tm,),
        in_specs=[rows(D_MODEL), rows(SSD_WIDTH), rows(SSD_WIDTH), rows(ATT_WIDTH), rows(ATT_WIDTH),
                  rows(PLE_DIM), rows(D_MODEL), vec, vec, vec, vec,
                  _const_spec((D_INNER, D_MODEL)), _const_spec((D_MODEL, D_MODEL)), _const_spec((PLE_DIM, D_MODEL))],
        out_specs=(rows(D_MODEL), rows(SSD_WIDTH), rows(SSD_WIDTH),
                   pl.BlockSpec((N_HEADS, tm, LANES), lambda i: (0, i, 0)), rows(ATT_WIDTH),
                   rows(D_INNER), rows(D_MODEL), rows(D_MODEL), rows(D_MODEL), rows(D_MODEL), rows(PLE_DIM),
                   _const_spec((1, LANES)), vec, vec, vec, vec),
        compiler_params=_params(("arbitrary",)),
    )(x, y, zs, o, za, p, tgt, ssd_g, att_g_lane, ple_g, fin_g, w_out, w_gate, w_proj)


def in_proj_bwd(dsegs, wsegs, x, g, dh1, pres):
    s = x.shape[0]
    tm = _blk(s, 256)
    nseg = len(dsegs)
    nbig = len(pres)
    nsteps = s // tm

    def body(*refs):
        d_refs = refs[:nseg]
        w_refs = refs[nseg:2 * nseg]
        x_ref, g_ref, dh1_ref = refs[2 * nseg:2 * nseg + 3]
        rest = refs[2 * nseg + 3:]
        pre_refs, (dx_ref, dg_ref), part_refs = rest[:nbig], rest[nbig:nbig + 2], rest[nbig + 2:2 * nbig + 2]
        ssem, rsem, lsem = rest[2 * nbig + 2:]

        @pl.when(pl.program_id(0) == 0)
        def _():
            dg_ref[...] = jnp.zeros_like(dg_ref)
            for cp in scatter_copies(pre_refs, part_refs, ssem, rsem, lsem):
                cp.start()

        @pl.when(pl.program_id(0) == nsteps - 1)
        def _():
            for cp in scatter_copies(pre_refs, part_refs, ssem, rsem, lsem):
                cp.wait()

        du = _mm_nt(d_refs[0][...], w_refs[0][...])
        for k in range(1, nseg):
            du = du + _mm_nt(d_refs[k][...], w_refs[k][...])
        xv = x_ref[...]
        r = lax.rsqrt(jnp.mean(xv * xv, axis=-1, keepdims=True) + EPS)
        xh = xv * r
        dg_ref[...] += jnp.sum(du * xh, axis=0, keepdims=True)
        dxh = du * g_ref[...]
        dx_ref[...] = r * (dxh - xh * jnp.mean(dxh * xh, axis=-1, keepdims=True)) + dh1_ref[...]

    rows = lambda n: pl.BlockSpec((tm, n), lambda i: (i, 0))
    return pl.pallas_call(
        body, name="in_proj_bwd",
        out_shape=tuple([jax.ShapeDtypeStruct((s, D_MODEL), F32), jax.ShapeDtypeStruct((1, D_MODEL), F32)]
                        + [jax.ShapeDtypeStruct(a.shape, a.dtype) for a in pres]),
        grid=(nsteps,),
        in_specs=([rows(d.shape[1]) for d in dsegs] + [_const_spec(w.shape) for w in wsegs]
                  + [rows(D_MODEL), _const_spec((1, D_MODEL)), rows(D_MODEL)] + [ANY] * nbig),
        out_specs=tuple([rows(D_MODEL), _const_spec((1, D_MODEL))] + [ANY] * nbig),
        scratch_shapes=_sems(3 * nbig) + [pltpu.SemaphoreType.DMA((nbig,))],
        compiler_params=_params(("arbitrary",)),
    )(*dsegs, *wsegs, x, g, dh1, *pres)


SMALL_NAMES = ("norm_g", "conv_b", "dt_bias", "a_log", "d_skip", "ssd_norm_g", "fg_bias", "att_norm_g",
               "ple_norm_g", "final_norm_g")
SMALL_SIZES = (1024, 1536, 16, 16, 16, 1024, 16, 64, 1024, 1024)
CONV_W_SIZE = CONV_WIDTH * CONV_CH


def _pack_small(vals):
    flat = jnp.concatenate([v.reshape(-1).astype(F32) for v in vals])
    flat = jnp.pad(flat, (0, SMALL_ROWS * LANES - flat.shape[0]))
    return flat.reshape(SMALL_ROWS, LANES)


def _unpack_small(pack, shapes):
    flat = pack.reshape(-1)
    out, off = [], 0
    for n, shp in zip(SMALL_SIZES, shapes):
        out.append(flat[off:off + n].reshape(shp))
        off += n
    return out


def _row128(v16, offset=0):
    return jnp.pad(v16.reshape(1, N_HEADS).astype(F32), ((0, 0), (offset, LANES - N_HEADS - offset)))


def local_step(prereduce, x, p, tgt, w_in, w_out, w_gate, w_proj, conv_w, norm_g, conv_b, dt_bias, a_log, d_skip,
               ssd_norm_g, fg_bias, att_norm_g, ple_norm_g, final_norm_g):
    widths = (SSD_WIDTH, CONV_CH, N_HEADS, ATT_WIDTH, ATT_WIDTH, ATT_WIDTH, ATT_WIDTH)
    c0, c1, c2, c3, c4, c5, c6, c7 = [sum(widths[:i]) for i in range(len(widths) + 1)]
    w_zs, w_xbc, w_dt = w_in[:, c0:c1], w_in[:, c1:c2], w_in[:, c2:c3]
    w_za, w_q, w_k, w_v, w_f = w_in[:, c3:c4], w_in[:, c4:c5], w_in[:, c5:c6], w_in[:, c6:c7], w_in[:, c7:]
    w_small = jnp.concatenate([w_dt, w_f, jnp.zeros((D_MODEL, LANES - 2 * N_HEADS), BF16)], axis=1)

    dtb_row = _row128(dt_bias)
    a_row = _row128(-jnp.exp(a_log.astype(F32)))
    fgb_row = _row128(fg_bias, N_HEADS)
    dskip_lane = jnp.repeat(d_skip.astype(F32), HEAD_DIM).reshape(1, SSD_WIDTH)
    att_g_lane = jnp.tile(att_norm_g.astype(F32), N_HEADS).reshape(1, ATT_WIDTH)
    row = lambda v: v.reshape(1, -1).astype(F32)

    u = rms_prenorm(x, row(norm_g))
    zs = matmul_rows(u, w_zs, F32, "proj_z_ssd")
    xbc = matmul_rows(u, w_xbc, F32, "proj_xbc")
    za = matmul_rows(u, w_za, F32, "proj_z_att")
    small = matmul_rows(u, w_small, F32, "proj_small")
    cum = forget_cumsum(small, fgb_row)
    qa, ka, va, norms = proj_qkv_heads(u, w_q, w_k, w_v, cum)
    n_seq = x.shape[0]
    first, _ = live_blocks(norms, cum, _blk(n_seq, ATT_BLOCK), _blk(n_seq, ATT_BLOCK))
    _, last_q = live_blocks(norms, cum, _blk(n_seq, ATT_BLOCK_BWD_Q), _blk(n_seq, ATT_BLOCK_BWD))
    pre, xc = conv_fwd(xbc, conv_w, row(conv_b))
    y, states = ssd_fwd(xc, small, dtb_row, a_row, dskip_lane)
    o, qb = attention_fwd(first, qa, ka, va)
    (dh1, dy, dzs, dob, dza, ycat, dh1_b, n2_b, dgl_b, dpp_b, p_b,
     loss_l, dfin, dple, dssd_g, datt_lane) = post_mix(
        x, y, zs, o, za, p, tgt, row(ssd_norm_g), att_g_lane, row(ple_norm_g), row(final_norm_g),
        w_out, w_gate, w_proj)
    dq, dk, dv, dc = attention_bwd(last_q, qb, ka, va, dob)
    dxc, ddt_raw, da, ddtb, ddsk_lane = ssd_bwd(xc, small, states, dy, dtb_row, a_row, dskip_lane)
    dsmall, dfgb = forget_bwd(dc, small, ddt_raw, fgb_row)
    dxbc, dconv_w8, dconv_b = conv_bwd(xbc, pre, dxc, conv_w)
    dsegs = [dzs, dxbc, dza, dq, dk, dv, dsmall]
    wsegs = [w_zs, w_xbc, w_za, w_q, w_k, w_v, w_small]
    dws = [matmul_tn(u, d, "dw_in_%d" % i) for i, d in enumerate(dsegs)]
    dw_in = jnp.concatenate([dws[0], dws[1], dws[6][:, :N_HEADS], dws[2], dws[3], dws[4], dws[5],
                             dws[6][:, N_HEADS:2 * N_HEADS]], axis=1)
    dw_out = matmul_tn(ycat, dh1_b, "dw_out")
    dw_gate = matmul_tn(n2_b, dgl_b, "dw_gate")
    dw_proj = matmul_tn(p_b, dpp_b, "dw_proj")
    dx, dnorm_g, *parts = in_proj_bwd(dsegs, wsegs, x, row(norm_g), dh1, prereduce(dw_in, dw_out, dw_gate, dw_proj))
    small_grads = [
        dnorm_g, dconv_b, ddtb[0, :N_HEADS], (da * a_row)[0, :N_HEADS],
        ddsk_lane.reshape(N_HEADS, HEAD_DIM).sum(axis=1), dssd_g, dfgb[0, N_HEADS:2 * N_HEADS],
        datt_lane.reshape(N_HEADS, HEAD_DIM).sum(axis=0), dple, dfin]
    loss = jnp.sum(loss_l)
    return loss, dx, parts, dconv_w8[:CONV_WIDTH], small_grads


def kernel(x, p, norm_g, w_in, conv_w, conv_b, dt_bias, a_log, d_skip, ssd_norm_g, fg_bias, att_norm_g, w_out, ple_norm_g, w_ple_gate, w_ple_proj, final_norm_g, loss_target, m_norm_g, m_w_in, m_conv_w, m_conv_b, m_dt_bias, m_a_log, m_d_skip, m_ssd_norm_g, m_fg_bias, m_att_norm_g, m_w_out, m_ple_norm_g, m_w_ple_gate, m_w_ple_proj, m_final_norm_g, v_norm_g, v_w_in, v_conv_w, v_conv_b, v_dt_bias, v_a_log, v_d_skip, v_ssd_norm_g, v_fg_bias, v_att_norm_g, v_w_out, v_ple_norm_g, v_w_ple_gate, v_w_ple_proj, v_final_norm_g):
    chip = 2 * lax.axis_index("x") + lax.axis_index("y")
    core = lax.axis_index("c")

    big_w = [w_in[0], w_out[0], w_ple_gate[0], w_ple_proj[0]]
    own = [a.astype(BF16) for a in big_w] + [conv_w[0]]
    gathered = gather_weights(own[:4], own[4])

    def joined(k, axis):
        return jnp.concatenate([jnp.where(chip == j, own[k], gathered[k][j]) for j in range(N_CHIPS)], axis=axis)

    w_in_f, w_out_f, w_gate_f, w_proj_f, conv_w_f = joined(0, 1), joined(1, 0), joined(2, 0), joined(3, 1), joined(4, 1)

    core1 = core.reshape(1).astype(jnp.int32)

    def prereduce(dw_in, dw_out, dw_gate, dw_proj):
        n_in, n_proj = w_in.shape[2], w_ple_proj.shape[2]
        gs = [jnp.stack([dw_in[:, n_in * j:n_in * (j + 1)] for j in range(N_CHIPS)]),
              dw_out.reshape(N_CHIPS, w_out.shape[1], D_MODEL), dw_gate.reshape(N_CHIPS, w_ple_gate.shape[1], D_MODEL),
              jnp.stack([dw_proj[:, n_proj * j:n_proj * (j + 1)] for j in range(N_CHIPS)])]
        return add_halves(core1, gs, halves_to_sibling(gs))

    smalls_w = [norm_g, conv_b, dt_bias, a_log, d_skip, ssd_norm_g, fg_bias, att_norm_g, ple_norm_g, final_norm_g]
    loss_l, dx, parts, dconv_w, small_grads = local_step(
        prereduce, x[0], p[0, 0], loss_target[0], w_in_f, w_out_f, w_gate_f, w_proj_f, conv_w_f,
        *[a.reshape(-1) for a in smalls_w])
    loss = lax.psum(loss_l, ("x", "y", "c"))
    smalls = gather_small(_pack_small(list(small_grads) + [dconv_w]))
    mine = sum_parts(parts)

    g_big, d_big, m_big, v_big = adamw_big(
        core1, mine, swap_halves(mine), big_w, [m_w_in[0], m_w_out[0], m_w_ple_gate[0], m_w_ple_proj[0]],
        [v_w_in[0], v_w_out[0], v_w_ple_gate[0], v_w_ple_proj[0]])
    smalls_m = [m_norm_g, m_conv_b, m_dt_bias, m_a_log, m_d_skip, m_ssd_norm_g, m_fg_bias, m_att_norm_g,
                m_ple_norm_g, m_final_norm_g]
    smalls_v = [v_norm_g, v_conv_b, v_dt_bias, v_a_log, v_d_skip, v_ssd_norm_g, v_fg_bias, v_att_norm_g,
                v_ple_norm_g, v_final_norm_g]
    g_sm, d_sm, m_sm, v_sm = adamw_small(smalls, _pack_small(smalls_w), _pack_small(smalls_m), _pack_small(smalls_v))
    n_small = sum(SMALL_SIZES)
    g_conv_full = g_sm.reshape(-1)[n_small:n_small + CONV_W_SIZE].reshape(CONV_WIDTH, CONV_CH)
    n_conv = conv_w.shape[2]
    g_conv = lax.dynamic_slice_in_dim(g_conv_full, chip * n_conv, n_conv, axis=1)
    d_conv, m_conv, v_conv = adamw_whole(g_conv, conv_w[0], m_conv_w[0], v_conv_w[0], "adamw_conv")

    shapes = [a.shape for a in smalls_w]
    outs = []
    for big, conv, sm in ((g_big, g_conv, g_sm), (d_big, d_conv, d_sm), (m_big, m_conv, m_sm), (v_big, v_conv, v_sm)):
        b_in, b_out, b_gate, b_proj = [a[None] for a in big]
        s_norm, s_convb, s_dtb, s_alog, s_dsk, s_ssdg, s_fgb, s_attg, s_pleg, s_fin = _unpack_small(sm, shapes)
        outs.extend([s_norm, b_in, conv[None], s_convb, s_dtb, s_alog, s_dsk, s_ssdg, s_fgb, s_attg, b_out, s_pleg,
                     b_gate, b_proj, s_fin])
    return (loss, dx[None], *outs)
```

```python
import functools

import jax
import jax.numpy as jnp
from jax import lax
from jax.experimental import pallas as pl
from jax.experimental.pallas import tpu as pltpu

F32 = jnp.float32
BF16 = jnp.bfloat16

D_MODEL = 1024
SSD_WIDTH = 1024
ATT_WIDTH = 1024
N_HEADS = 16
HEAD_DIM = 64
N_GROUPS = 2
D_STATE = 128
CONV_CH = 1536
CONV_WIDTH = 4
CHUNK = 128
PLE_DIM = 256
D_INNER = 2048
EPS = 1e-6
IN_COLS = 6688
N_CHIPS = 4
N_DEV = 8
LANES = 128
N_PAIRS = 8

ADAM_LR = 0.001
ADAM_B1 = 0.9
ADAM_B2 = 0.999
ADAM_EPS = 1e-08
ADAM_WD = 0.01
ADAM_STEP = 10

SMALL_ROWS = 96

NEG_BIG = -1e30
VMEM_LIMIT = 56 * 1024 * 1024

MESH = pl.DeviceIdType.MESH
ANY = pl.BlockSpec(memory_space=pl.ANY)


def _mm(a, b):
    return jnp.dot(a, b, preferred_element_type=F32)


def _mm_nt(a, b):
    return lax.dot_general(a, b, (((1,), (1,)), ((), ())), preferred_element_type=F32)


def _mm_tn(a, b):
    return lax.dot_general(a, b, (((0,), (0,)), ((), ())), preferred_element_type=F32)


def _mm_exact(a, b):
    return jnp.dot(a, b, preferred_element_type=F32, precision=lax.Precision.HIGHEST)


def _softplus(x):
    return jnp.maximum(x, 0.0) + jnp.log1p(jnp.exp(-jnp.abs(x)))


def _sigmoid(x):
    return jax.nn.sigmoid(x)


def _iota(shape, dim):
    return lax.broadcasted_iota(jnp.int32, shape, dim)


def _params(sem=None):
    return pltpu.CompilerParams(dimension_semantics=sem, vmem_limit_bytes=VMEM_LIMIT)


def _blk(n, pref):
    return min(n, pref)


def _const_spec(shape):
    nd = len(shape)
    return pl.BlockSpec(shape, lambda *_: (0,) * nd)


def _chip_peers():
    x, y, c = lax.axis_index("x"), lax.axis_index("y"), lax.axis_index("c")
    return x, y, c, [(1 - x, y, c), (x, 1 - y, c), (1 - x, 1 - y, c)]


def _half(rows, c):
    h = rows // 2
    return pl.ds(pl.multiple_of(c * h, 8), h)


def _sems(n):
    return [pltpu.SemaphoreType.DMA((n,)), pltpu.SemaphoreType.DMA((n,))]


def gather_copies(ins, outs, ssem1, rsem1, ssem2, rsem2):
    n = len(ins)
    x, y, c, peers = _chip_peers()
    me = 2 * x + y
    fetched, passed = [], []
    for k, peer in enumerate(peers):
        chip = 2 * peer[0] + peer[1]
        for i in range(n):
            h = _half(ins[i].shape[0], c)
            fetched.append(pltpu.make_async_remote_copy(
                src_ref=ins[i].at[h], dst_ref=outs[i].at[me, h], send_sem=ssem1.at[n * k + i],
                recv_sem=rsem1.at[n * k + i], device_id=peer, device_id_type=MESH))
            passed.append(pltpu.make_async_remote_copy(
                src_ref=outs[i].at[chip, h], dst_ref=outs[i].at[chip, h], send_sem=ssem2.at[n * k + i],
                recv_sem=rsem2.at[n * k + i], device_id=(x, y, 1 - c), device_id_type=MESH))
    return fetched, passed


def gather_weights(shards, conv_s):
    n = len(shards)

    def body(*refs):
        ins, conv_in = refs[:n], refs[n]
        outs, conv_out = refs[n + 1:2 * n + 1], refs[2 * n + 1]
        ssem1, rsem1, ssem2, rsem2, c_ssem, c_rsem = refs[2 * n + 2:]
        x, y, _, peers = _chip_peers()
        fetched, passed = gather_copies(ins, outs, ssem1, rsem1, ssem2, rsem2)
        small = [pltpu.make_async_remote_copy(
            src_ref=conv_in, dst_ref=conv_out.at[2 * x + y], send_sem=c_ssem.at[k], recv_sem=c_rsem.at[k],
            device_id=peer, device_id_type=MESH) for k, peer in enumerate(peers)]
        for cp in fetched + small:
            cp.start()
        for landed, onward in zip(fetched, passed):
            landed.wait_recv()
            onward.start()
        for cp in passed:
            cp.wait_recv()
        for cp in fetched + passed:
            cp.wait_send()
        for cp in small:
            cp.wait()

    return pl.pallas_call(
        body, name="gather_weights",
        out_shape=tuple(jax.ShapeDtypeStruct((N_CHIPS,) + a.shape, a.dtype) for a in list(shards) + [conv_s]),
        in_specs=[ANY] * (n + 1), out_specs=(ANY,) * (n + 1),
        scratch_shapes=_sems(3 * n) + _sems(3 * n) + _sems(3),
    )(*shards, conv_s)


def halves_to_sibling(gs):
    n = len(gs)

    def body(*refs):
        ins, outs = refs[:n], refs[n:2 * n]
        ssem, rsem = refs[2 * n:]
        x, y, c = lax.axis_index("x"), lax.axis_index("y"), lax.axis_index("c")
        copies = []
        for i in range(n):
            for j in range(N_CHIPS):
                copies.append(pltpu.make_async_remote_copy(
                    src_ref=ins[i].at[j, _half(ins[i].shape[1], 1 - c)], dst_ref=outs[i].at[j],
                    send_sem=ssem.at[N_CHIPS * i + j], recv_sem=rsem.at[N_CHIPS * i + j],
                    device_id=(x, y, 1 - c), device_id_type=MESH))
        for cp in copies:
            cp.start()
        for cp in copies:
            cp.wait()

    return pl.pallas_call(
        body, name="halves_to_sibling",
        out_shape=tuple(jax.ShapeDtypeStruct((N_CHIPS, g.shape[1] // 2, g.shape[2]), F32) for g in gs),
        in_specs=[ANY] * n, out_specs=(ANY,) * n, scratch_shapes=_sems(N_CHIPS * n),
    )(*gs)


RED_GRID = 8


def add_halves(core, gs, rbs):
    n = len(gs)

    def body(c_ref, *refs):
        for i in range(n):
            refs[2 * n + i][...] = (refs[i][...] + refs[n + i][...]).astype(BF16)

    def blk(g):
        return (1, g.shape[1] // 2 // RED_GRID, g.shape[2])

    grid_spec = pltpu.PrefetchScalarGridSpec(
        num_scalar_prefetch=1, grid=(N_CHIPS, RED_GRID),
        in_specs=([pl.BlockSpec(blk(g), lambda j, b, c_ref: (j, c_ref[0] * RED_GRID + b, 0)) for g in gs]
                  + [pl.BlockSpec(blk(g), lambda j, b, c_ref: (j, b, 0)) for g in gs]),
        out_specs=[pl.BlockSpec(blk(g), lambda j, b, c_ref: (j, b, 0)) for g in gs])
    return pl.pallas_call(
        body, name="add_halves", grid_spec=grid_spec,
        out_shape=tuple(jax.ShapeDtypeStruct(r.shape, BF16) for r in rbs),
        compiler_params=_params(("parallel", "parallel")),
    )(core, *gs, *rbs)


def scatter_copies(ins, outs, ssem, rsem, lsem):
    n = len(ins)
    x, y, _, peers = _chip_peers()
    me = 2 * x + y
    copies = [pltpu.make_async_copy(ins[i].at[me], outs[i].at[me], lsem.at[i]) for i in range(n)]
    for k, peer in enumerate(peers):
        dst_chip = 2 * peer[0] + peer[1]
        for i in range(n):
            copies.append(pltpu.make_async_remote_copy(
                src_ref=ins[i].at[dst_chip], dst_ref=outs[i].at[me], send_sem=ssem.at[n * k + i],
                recv_sem=rsem.at[n * k + i], device_id=peer, device_id_type=MESH))
    return copies


def gather_small(small):
    def body(s_ref, smalls_ref, ssem, rsem, lsem):
        x, y, c = lax.axis_index("x"), lax.axis_index("y"), lax.axis_index("c")
        dev = 4 * x + 2 * y + c
        copies = [pltpu.make_async_copy(s_ref, smalls_ref.at[dev], lsem)]
        for k in range(1, N_DEV):
            fx, fy, fc = (k >> 2) & 1, (k >> 1) & 1, k & 1
            peer = ((1 - x) if fx else x, (1 - y) if fy else y, (1 - c) if fc else c)
            copies.append(pltpu.make_async_remote_copy(
                src_ref=s_ref, dst_ref=smalls_ref.at[dev], send_sem=ssem.at[k - 1], recv_sem=rsem.at[k - 1],
                device_id=peer, device_id_type=MESH))
        for cp in copies:
            cp.start()
        for cp in copies:
            cp.wait()

    return pl.pallas_call(
        body, name="gather_small",
        out_shape=jax.ShapeDtypeStruct((N_DEV,) + small.shape, F32),
        in_specs=[ANY], out_specs=ANY,
        scratch_shapes=_sems(N_DEV - 1) + [pltpu.SemaphoreType.DMA],
    )(small)


def sum_parts(parts):
    n = len(parts)

    def body(*refs):
        for i in range(n):
            p_ref = refs[i]
            refs[n + i][...] = ((p_ref[0].astype(F32) + p_ref[1].astype(F32)) + p_ref[2].astype(F32)
                                ) + p_ref[3].astype(F32)

    def rows(p):
        return p.shape[1] // RED_GRID

    return pl.pallas_call(
        body, name="sum_parts",
        out_shape=tuple(jax.ShapeDtypeStruct(p.shape[1:], F32) for p in parts),
        grid=(RED_GRID,),
        in_specs=[pl.BlockSpec((N_CHIPS, rows(p), p.shape[2]), lambda b: (0, b, 0)) for p in parts],
        out_specs=tuple(pl.BlockSpec((rows(p), p.shape[2]), lambda b: (b, 0)) for p in parts),
        compiler_params=_params(("parallel",)),
    )(*parts)


def swap_halves(reds):
    n = len(reds)

    def body(*refs):
        ins, outs = refs[:n], refs[n:2 * n]
        ssem, rsem = refs[2 * n:]
        x, y, c = lax.axis_index("x"), lax.axis_index("y"), lax.axis_index("c")
        copies = [pltpu.make_async_remote_copy(
            src_ref=ins[i], dst_ref=outs[i], send_sem=ssem.at[i], recv_sem=rsem.at[i],
            device_id=(x, y, 1 - c), device_id_type=MESH) for i in range(n)]
        for cp in copies:
            cp.start()
        for cp in copies:
            cp.wait()

    return pl.pallas_call(
        body, name="swap_halves",
        out_shape=tuple(jax.ShapeDtypeStruct(r.shape, F32) for r in reds),
        in_specs=[ANY] * n, out_specs=(ANY,) * n, scratch_shapes=_sems(n),
    )(*reds)


def _adamw(w, g, m, v):
    m = ADAM_B1 * m + (1.0 - ADAM_B1) * g
    v = ADAM_B2 * v + (1.0 - ADAM_B2) * (g * g)
    m_hat = m / (1.0 - ADAM_B1 ** ADAM_STEP)
    v_hat = v / (1.0 - ADAM_B2 ** ADAM_STEP)
    delta = -ADAM_LR * (m_hat / (jnp.sqrt(v_hat) + ADAM_EPS) + ADAM_WD * w)
    return delta, m, v


def adamw_big(core, mine, theirs, ws, ms, vs):
    n = len(ws)
    per_half = RED_GRID // 2

    def body(c_ref, *refs):
        own = (pl.program_id(0) // per_half) == c_ref[0]
        for i in range(n):
            g = jnp.where(own, refs[i][...], refs[n + i][...])
            d, mn, vn = _adamw(refs[2 * n + i][...], g, refs[3 * n + i][...], refs[4 * n + i][...])
            refs[5 * n + i][...] = g
            refs[6 * n + i][...] = d
            refs[7 * n + i][...] = mn
            refs[8 * n + i][...] = vn

    def blk(w):
        return (w.shape[0] // RED_GRID, w.shape[1])

    halves = [pl.BlockSpec(blk(w), lambda b, c_ref: (b % per_half, 0)) for w in ws]
    whole = [pl.BlockSpec(blk(w), lambda b, c_ref: (b, 0)) for w in ws]
    shapes = [jax.ShapeDtypeStruct(w.shape, F32) for w in ws]
    grid_spec = pltpu.PrefetchScalarGridSpec(
        num_scalar_prefetch=1, grid=(RED_GRID,), in_specs=halves * 2 + whole * 3, out_specs=whole * 4)
    outs = pl.pallas_call(
        body, name="adamw_big", out_shape=tuple(shapes * 4), grid_spec=grid_spec,
        compiler_params=_params(("parallel",)),
    )(core, *mine, *theirs, *ws, *ms, *vs)
    return outs[:n], outs[n:2 * n], outs[2 * n:3 * n], outs[3 * n:]


def adamw_whole(g, w, m, v, name):
    def body(g_ref, w_ref, m_ref, v_ref, d_out, m_out, v_out):
        d, mn, vn = _adamw(w_ref[...], g_ref[...], m_ref[...], v_ref[...])
        d_out[...] = d
        m_out[...] = mn
        v_out[...] = vn

    shp = jax.ShapeDtypeStruct(g.shape, F32)
    return pl.pallas_call(body, name=name, out_shape=(shp,) * 3)(g, w, m, v)


def adamw_small(smalls, w, m, v):
    def body(s_ref, w_ref, m_ref, v_ref, g_out, d_out, m_out, v_out):
        g = s_ref[0]
        for k in range(1, N_DEV):
            g = g + s_ref[k]
        d, mn, vn = _adamw(w_ref[...], g, m_ref[...], v_ref[...])
        g_out[...] = g
        d_out[...] = d
        m_out[...] = mn
        v_out[...] = vn

    shp = jax.ShapeDtypeStruct((SMALL_ROWS, LANES), F32)
    return pl.pallas_call(body, name="adamw_small", out_shape=(shp,) * 4)(smalls, w, m, v)


def rms_prenorm(x, g):
    s = x.shape[0]
    tm = _blk(s, 512)

    def body(x_ref, g_ref, u_ref):
        xv = x_ref[...]
        r = lax.rsqrt(jnp.mean(xv * xv, axis=-1, keepdims=True) + EPS)
        u_ref[...] = (xv * r * g_ref[...]).astype(BF16)

    return pl.pallas_call(
        body, name="rms_prenorm", out_shape=jax.ShapeDtypeStruct(x.shape, BF16), grid=(s // tm,),
        in_specs=[pl.BlockSpec((tm, D_MODEL), lambda i: (i, 0)), _const_spec((1, D_MODEL))],
        out_specs=pl.BlockSpec((tm, D_MODEL), lambda i: (i, 0)), compiler_params=_params(("parallel",)),
    )(x, g)


def matmul_rows(a, w, out_dtype, name):
    s, k = a.shape
    n = w.shape[1]
    tm = _blk(s, 512)

    def body(a_ref, w_ref, o_ref):
        o_ref[...] = _mm(a_ref[...], w_ref[...]).astype(out_dtype)

    return pl.pallas_call(
        body, name=name, out_shape=jax.ShapeDtypeStruct((s, n), out_dtype), grid=(s // tm,),
        in_specs=[pl.BlockSpec((tm, k), lambda i: (i, 0)), _const_spec((k, n))],
        out_specs=pl.BlockSpec((tm, n), lambda i: (i, 0)), compiler_params=_params(("parallel",)),
    )(a, w)


def matmul_tn(a, b, name):
    s, m = a.shape
    n = b.shape[1]
    tk = _blk(s, 2048)
    tn = _blk(n, 512)

    def body(a_ref, b_ref, o_ref):
        @pl.when(pl.program_id(1) == 0)
        def _():
            o_ref[...] = jnp.zeros_like(o_ref)

        o_ref[...] += _mm_tn(a_ref[...], b_ref[...])

    return pl.pallas_call(
        body, name=name, out_shape=jax.ShapeDtypeStruct((m, n), F32), grid=(n // tn, s // tk),
        in_specs=[pl.BlockSpec((tk, m), lambda j, i: (i, 0)), pl.BlockSpec((tk, tn), lambda j, i: (i, j))],
        out_specs=pl.BlockSpec((m, tn), lambda j, i: (0, j)),
        compiler_params=_params(("parallel", "arbitrary")),
    )(a, b)


def conv_fwd(xbc, w, b):
    s = xbc.shape[0]
    tm = _blk(s, 256)

    def body(x_ref, t_ref, w_ref, b_ref, pre_ref, act_ref):
        i = pl.program_id(0)
        row8 = _iota((8, LANES), 0)
        for c0 in range(0, CONV_CH, LANES):
            cols = slice(c0, c0 + LANES)
            cur = x_ref[:, cols]
            tail = jnp.where(i > 0, t_ref[:, cols], 0.0)
            wv = w_ref[:, cols]
            bias = b_ref[:, cols]
            acc = cur * wv[3:4, :] + bias
            head = cur[0:8, :] * wv[3:4, :] + bias
            for sh in range(1, CONV_WIDTH):
                wk = wv[3 - sh:4 - sh, :]
                acc = acc + pltpu.roll(cur, sh, 0) * wk
                first = jnp.where(row8 < sh, pltpu.roll(tail, sh, 0), pltpu.roll(cur[0:8, :], sh, 0))
                head = head + first * wk
            pre_ref[:, cols] = acc
            act_ref[:, cols] = acc * _sigmoid(acc)
            pre_ref[0:8, cols] = head
            act_ref[0:8, cols] = head * _sigmoid(head)

    shp = jax.ShapeDtypeStruct(xbc.shape, F32)
    rows = pl.BlockSpec((tm, CONV_CH), lambda i: (i, 0))
    return pl.pallas_call(
        body, name="conv_fwd", out_shape=(shp, shp), grid=(s // tm,),
        in_specs=[rows, pl.BlockSpec((8, CONV_CH), lambda i: (jnp.maximum(i * (tm // 8) - 1, 0), 0)),
                  _const_spec((CONV_WIDTH, CONV_CH)), _const_spec((1, CONV_CH))],
        out_specs=(rows, rows), compiler_params=_params(("parallel",)),
    )(xbc, xbc, w, b)


def conv_bwd(xbc, pre, dact, w):
    s = xbc.shape[0]
    tm = _blk(s, 256)
    nb = s // tm

    def dsilu(p):
        sg = _sigmoid(p)
        return sg * (1.0 + p * (1.0 - sg))

    def body(x_ref, xt_ref, p_ref, pn_ref, d_ref, dn_ref, w_ref, dx_ref, dw_ref, db_ref):
        i = pl.program_id(0)

        @pl.when(i == 0)
        def _():
            dw_ref[...] = jnp.zeros_like(dw_ref)
            db_ref[...] = jnp.zeros_like(db_ref)

        row8 = _iota((8, LANES), 0)
        for c0 in range(0, CONV_CH, LANES):
            cols = slice(c0, c0 + LANES)
            wv = w_ref[:, cols]
            dpre = d_ref[:, cols] * dsilu(p_ref[:, cols])
            dnext = jnp.where(i < nb - 1, dn_ref[:, cols] * dsilu(pn_ref[:, cols]), 0.0)
            cur = x_ref[:, cols]
            tail = jnp.where(i > 0, xt_ref[:, cols], 0.0)
            dx = dpre * wv[3:4, :]
            last = dpre[tm - 8:tm, :] * wv[3:4, :]
            db_ref[:, cols] += jnp.sum(dpre, axis=0, keepdims=True)
            dws = [jnp.sum(dpre * cur, axis=0, keepdims=True)]
            for sh in range(1, CONV_WIDTH):
                wk = wv[3 - sh:4 - sh, :]
                dx = dx + pltpu.roll(dpre, tm - sh, 0) * wk
                nxt = jnp.where(row8 >= 8 - sh, pltpu.roll(dnext, 8 - sh, 0),
                                pltpu.roll(dpre[tm - 8:tm, :], 8 - sh, 0))
                last = last + nxt * wk
                xs = pltpu.roll(cur, sh, 0)
                first = jnp.where(row8 < sh, pltpu.roll(tail, sh, 0), xs[0:8, :])
                dws.append(jnp.sum(dpre * xs, axis=0, keepdims=True)
                           + jnp.sum(dpre[0:8, :] * (first - xs[0:8, :]), axis=0, keepdims=True))
            dx_ref[:, cols] = dx.astype(BF16)
            dx_ref[tm - 8:tm, cols] = last.astype(BF16)
            for sh in range(CONV_WIDTH):
                dw_ref[3 - sh:4 - sh, cols] += dws[sh]

    rows = pl.BlockSpec((tm, CONV_CH), lambda i: (i, 0))
    prev8 = pl.BlockSpec((8, CONV_CH), lambda i: (jnp.maximum(i * (tm // 8) - 1, 0), 0))
    next8 = pl.BlockSpec((8, CONV_CH), lambda i: (jnp.minimum((i + 1) * (tm // 8), s // 8 - 1), 0))
    return pl.pallas_call(
        body, name="conv_bwd",
        out_shape=(jax.ShapeDtypeStruct(xbc.shape, BF16), jax.ShapeDtypeStruct((8, CONV_CH), F32),
                   jax.ShapeDtypeStruct((1, CONV_CH), F32)),
        grid=(nb,),
        in_specs=[rows, prev8, rows, next8, rows, next8, _const_spec((CONV_WIDTH, CONV_CH))],
        out_specs=(rows, _const_spec((8, CONV_CH)), _const_spec((1, CONV_CH))),
        compiler_params=_params(("arbitrary",)),
    )(xbc, xbc, pre, pre, dact, dact, w)


def _pair_lanes(mat, j, lane):
    return jnp.where(lane < HEAD_DIM, mat[:, 2 * j:2 * j + 1], mat[:, 2 * j + 1:2 * j + 2])


def _ssd_chunk_prelude(sm, dtb, a_row, lane, sub):
    raw = sm + dtb
    head_lane = lane < N_HEADS
    dt = jnp.where(head_lane, _softplus(raw), 0.0)
    sig = jnp.where(head_lane, _sigmoid(raw), 0.0)
    tri = (lane <= sub).astype(F32)
    acs = _mm_exact(tri, dt * a_row)
    return dt, sig, acs, acs.T


GROUP_WIDTH = SSD_WIDTH // N_GROUPS
HEADS_PER_GROUP = N_HEADS // N_GROUPS


def _expand_group(mat, g, lane):
    return jnp.concatenate([_pair_lanes(mat, j, lane) for j in range(4 * g, 4 * g + 4)], axis=1)


def _head_sums(q, g):
    row = _iota((GROUP_WIDTH, LANES), 0)
    seg = (_iota((GROUP_WIDTH, LANES), 1) == HEADS_PER_GROUP * g + (row >> 6)).astype(BF16)
    hi = q.astype(BF16)
    lo = (q - hi.astype(F32)).astype(BF16)
    return _mm(hi, seg) + _mm(lo, seg)


def _rows_from_lanes(row512):
    return jnp.broadcast_to(row512, (LANES, GROUP_WIDTH)).T


def ssd_fwd(xc, small, dtb_row, a_row, dskip_lane):
    s = xc.shape[0]
    nc = s // CHUNK

    def body(xc_ref, sm_ref, dtb_ref, a_ref, dsk_ref, y_ref, hs_ref, h_scr):
        c = pl.program_id(0)

        @pl.when(c == 0)
        def _():
            h_scr[...] = jnp.zeros_like(h_scr)

        lane = _iota((CHUNK, LANES), 1)
        sub = _iota((CHUNK, LANES), 0)
        causal = lane <= sub
        dt, _, acs, acs_t = _ssd_chunk_prelude(sm_ref[...], dtb_ref[...], a_ref[...], lane, sub)
        for g in range(N_GROUPS):
            cols = slice(GROUP_WIDTH * g, GROUP_WIDTH * (g + 1))
            b_off = SSD_WIDTH + D_STATE * g
            c_off = SSD_WIDTH + N_GROUPS * D_STATE + D_STATE * g
            b_b = xc_ref[:, b_off:b_off + D_STATE].astype(BF16)
            c_b = xc_ref[:, c_off:c_off + D_STATE].astype(BF16)
            cb = _mm_nt(c_b, b_b)
            x_g = xc_ref[:, cols]
            acs_g = _expand_group(acs, g, lane)
            xdt_g = x_g * _expand_group(dt, g, lane)
            xdt_b = xdt_g.astype(BF16)
            heads = range(HEADS_PER_GROUP * g, HEADS_PER_GROUP * (g + 1))
            m_b = [(cb * jnp.exp(jnp.where(causal, acs[:, h:h + 1] - acs_t[h:h + 1, :], NEG_BIG))).astype(BF16)
                   for h in heads]
            yd = [_mm(m_b[k], xdt_b[:, LANES * (k // 2):LANES * (k // 2 + 1)]) for k in range(HEADS_PER_GROUP)]
            yd_g = jnp.concatenate([jnp.where(lane < HEAD_DIM, yd[2 * k], yd[2 * k + 1]) for k in range(4)], axis=1)
            h_g = h_scr[g]
            t_g = _mm_nt(c_b, h_g.astype(BF16))
            y_ref[:, cols] = yd_g + jnp.exp(acs_g) * t_g + dsk_ref[:, cols] * x_g
            hs_ref[0, g] = h_g
            last_g = acs_g[CHUNK - 1:CHUNK, :]
            w_b = (xdt_g * jnp.exp(last_g - acs_g)).astype(BF16)
            h_scr[g] = h_g * jnp.exp(_rows_from_lanes(last_g)) + _mm_tn(w_b, b_b)

    return pl.pallas_call(
        body, name="ssd_fwd",
        out_shape=(jax.ShapeDtypeStruct((s, SSD_WIDTH), F32),
                   jax.ShapeDtypeStruct((nc, N_GROUPS, GROUP_WIDTH, D_STATE), F32)),
        grid=(nc,),
        in_specs=[pl.BlockSpec((CHUNK, CONV_CH), lambda c: (c, 0)), pl.BlockSpec((CHUNK, LANES), lambda c: (c, 0)),
                  _const_spec((1, LANES)), _const_spec((1, LANES)), _const_spec((1, SSD_WIDTH))],
        out_specs=(pl.BlockSpec((CHUNK, SSD_WIDTH), lambda c: (c, 0)),
                   pl.BlockSpec((1, N_GROUPS, GROUP_WIDTH, D_STATE), lambda c: (c, 0, 0, 0))),
        scratch_shapes=[pltpu.VMEM((N_GROUPS, GROUP_WIDTH, D_STATE), F32)],
        compiler_params=_params(("arbitrary",)),
    )(xc, small, dtb_row, a_row, dskip_lane)


def ssd_bwd(xc, small, states, dy, dtb_row, a_row, dskip_lane):
    s = xc.shape[0]
    nc = s // CHUNK
    rev = lambda c: nc - 1 - c

    def body(xc_ref, sm_ref, hs_ref, dy_ref, dtb_ref, a_ref, dsk_ref,
             dxc_ref, ddt_ref, da_ref, ddtb_ref, ddsk_ref, dh_scr):
        c = pl.program_id(0)

        @pl.when(c == 0)
        def _():
            dh_scr[...] = jnp.zeros_like(dh_scr)
            da_ref[...] = jnp.zeros_like(da_ref)
            ddtb_ref[...] = jnp.zeros_like(ddtb_ref)
            ddsk_ref[...] = jnp.zeros_like(ddsk_ref)

        lane = _iota((CHUNK, LANES), 1)
        sub = _iota((CHUNK, LANES), 0)
        causal = lane <= sub
        upper = lane >= sub
        is_last = sub == CHUNK - 1
        a_row_v = a_ref[...]
        dt, sig, acs, acs_t = _ssd_chunk_prelude(sm_ref[...], dtb_ref[...], a_row_v, lane, sub)
        cd = jnp.exp(acs[CHUNK - 1:CHUNK, :])
        dacs_c = jnp.zeros((CHUNK, LANES), F32)
        dacs_r = jnp.zeros((LANES, CHUNK), F32)
        ddtx = jnp.zeros((CHUNK, LANES), F32)
        for g in range(N_GROUPS):
            cols = slice(GROUP_WIDTH * g, GROUP_WIDTH * (g + 1))
            b_off = SSD_WIDTH + D_STATE * g
            c_off = SSD_WIDTH + N_GROUPS * D_STATE + D_STATE * g
            b_b = xc_ref[:, b_off:b_off + D_STATE].astype(BF16)
            c_b = xc_ref[:, c_off:c_off + D_STATE].astype(BF16)
            cb = _mm_nt(c_b, b_b)
            cb_t = _mm_nt(b_b, c_b)
            x_g = xc_ref[:, cols]
            dy_g = dy_ref[:, cols]
            dt_g = _expand_group(dt, g, lane)
            acs_g = _expand_group(acs, g, lane)
            last_g = acs_g[CHUNK - 1:CHUNK, :]
            e_g = jnp.exp(acs_g)
            dte_g = jnp.exp(last_g - acs_g)
            xdt_g = x_g * dt_g
            xdt_b = xdt_g.astype(BF16)
            h_g = hs_ref[0, g]
            dh_g = dh_scr[g]
            h_b = h_g.astype(BF16)
            dh_b = dh_g.astype(BF16)
            heads = list(range(HEADS_PER_GROUP * g, HEADS_PER_GROUP * (g + 1)))
            segs = [acs[:, h:h + 1] - acs_t[h:h + 1, :] for h in heads]
            lms = [jnp.exp(jnp.where(causal, sg, NEG_BIG)) for sg in segs]
            mts = [(cb_t * jnp.exp(jnp.where(upper, -sg, NEG_BIG))).astype(BF16) for sg in segs]
            dyh = []
            for k in range(HEADS_PER_GROUP):
                blk = dy_g[:, LANES * (k // 2):LANES * (k // 2 + 1)]
                in_head = (lane < HEAD_DIM) if k % 2 == 0 else (lane >= HEAD_DIM)
                dyh.append(jnp.where(in_head, blk, 0.0).astype(BF16))
            dms = [_mm_nt(dyh[k], xdt_b[:, LANES * (k // 2):LANES * (k // 2 + 1)]) for k in range(HEADS_PER_GROUP)]
            dxs = [_mm(mts[k], dyh[k]) for k in range(HEADS_PER_GROUP)]
            dcb = jnp.zeros((CHUNK, CHUNK), F32)
            for k, h in enumerate(heads):
                gmat = dms[k] * (cb * lms[k])
                dacs_c = dacs_c + jnp.where(lane == h, jnp.sum(gmat, axis=1, keepdims=True), 0.0)
                dacs_r = dacs_r - jnp.where(sub == h, jnp.sum(gmat, axis=0, keepdims=True), 0.0)
                dcb = dcb + dms[k] * lms[k]
            dxdt_g = jnp.concatenate([dxs[2 * k] + dxs[2 * k + 1] for k in range(4)], axis=1)
            t_g = _mm_nt(c_b, h_b)
            dacs_c = dacs_c + _head_sums(dy_g * e_g * t_g, g)
            dt_b = (dy_g * e_g).astype(BF16)
            dc_acc = _mm(dt_b, h_b)
            dh_prev = _mm_tn(dt_b, c_b)
            dw_g = _mm_nt(b_b, dh_b)
            w_g = xdt_g * dte_g
            dxdt_g = dxdt_g + dw_g * dte_g
            db_acc = _mm(w_g.astype(BF16), dh_b)
            r2 = _head_sums(dw_g * w_g, g)
            dacs_c = dacs_c + jnp.where(is_last, jnp.sum(r2, axis=0, keepdims=True), 0.0) - r2
            q3 = jnp.sum(dh_g * h_g, axis=1, keepdims=True)
            for k, h in enumerate(heads):
                tot = jnp.sum(q3[HEAD_DIM * k:HEAD_DIM * (k + 1), :], keepdims=True) * cd[:, h:h + 1]
                dacs_c = dacs_c + jnp.where(is_last & (lane == h), tot, 0.0)
            dh_scr[g] = dh_prev + dh_g * jnp.exp(_rows_from_lanes(last_g))
            dxc_ref[:, cols] = dxdt_g * dt_g + dsk_ref[:, cols] * dy_g
            ddtx = ddtx + _head_sums(dxdt_g * x_g, g)
            ddsk_ref[:, cols] += jnp.sum(dy_g * x_g, axis=0, keepdims=True)
            dxc_ref[:, b_off:b_off + D_STATE] = db_acc + _mm(dcb.T.astype(BF16), c_b)
            dxc_ref[:, c_off:c_off + D_STATE] = dc_acc + _mm(dcb.astype(BF16), b_b)
        dacs = dacs_c + dacs_r.T
        dadt = _mm_exact((lane >= sub).astype(F32), dacs)
        ddt = dadt * a_row_v + ddtx
        ddt_raw = ddt * sig
        ddt_ref[...] = ddt_raw
        da_ref[...] += jnp.sum(dadt * dt, axis=0, keepdims=True)
        ddtb_ref[...] += jnp.sum(ddt_raw, axis=0, keepdims=True)

    return pl.pallas_call(
        body, name="ssd_bwd",
        out_shape=(jax.ShapeDtypeStruct((s, CONV_CH), F32), jax.ShapeDtypeStruct((s, LANES), F32),
                   jax.ShapeDtypeStruct((1, LANES), F32), jax.ShapeDtypeStruct((1, LANES), F32),
                   jax.ShapeDtypeStruct((1, SSD_WIDTH), F32)),
        grid=(nc,),
        in_specs=[pl.BlockSpec((CHUNK, CONV_CH), lambda c: (rev(c), 0)),
                  pl.BlockSpec((CHUNK, LANES), lambda c: (rev(c), 0)),
                  pl.BlockSpec((1, N_GROUPS, GROUP_WIDTH, D_STATE), lambda c: (rev(c), 0, 0, 0)),
                  pl.BlockSpec((CHUNK, SSD_WIDTH), lambda c: (rev(c), 0)),
                  _const_spec((1, LANES)), _const_spec((1, LANES)), _const_spec((1, SSD_WIDTH))],
        out_specs=(pl.BlockSpec((CHUNK, CONV_CH), lambda c: (rev(c), 0)),
                   pl.BlockSpec((CHUNK, LANES), lambda c: (rev(c), 0)),
                   _const_spec((1, LANES)), _const_spec((1, LANES)), _const_spec((1, SSD_WIDTH))),
        scratch_shapes=[pltpu.VMEM((N_GROUPS, GROUP_WIDTH, D_STATE), F32)],
        compiler_params=_params(("arbitrary",)),
    )(xc, small, states, dy, dtb_row, a_row, dskip_lane)


FORGET_BLOCK = 512


def forget_cumsum(small, fgb_row):
    s = small.shape[0]
    t = _blk(s, FORGET_BLOCK)
    nb = s // t

    def body(sm_ref, b_ref, cc_ref, carry):
        i = pl.program_id(0)

        @pl.when(i == 0)
        def _():
            carry[...] = jnp.zeros_like(carry)

        lane = _iota((t, LANES), 1)
        in_f = (lane >= N_HEADS) & (lane < 2 * N_HEADS)
        logf = jnp.where(in_f, -_softplus(-(sm_ref[...] + b_ref[...])), 0.0)
        tri = (_iota((t, t), 1) <= _iota((t, t), 0)).astype(F32)
        cum = _mm_exact(tri, logf) + carry[0:1, :]
        cc_ref[...] = cum
        carry[...] = jnp.broadcast_to(cum[t - 1:t, :], (8, LANES))

    return pl.pallas_call(
        body, name="forget_cumsum",
        out_shape=jax.ShapeDtypeStruct((s, LANES), F32),
        grid=(nb,),
        in_specs=[pl.BlockSpec((t, LANES), lambda i: (i, 0)), _const_spec((1, LANES))],
        out_specs=pl.BlockSpec((t, LANES), lambda i: (i, 0)),
        scratch_shapes=[pltpu.VMEM((8, LANES), F32)],
        compiler_params=_params(("arbitrary",)),
    )(small, fgb_row)


def forget_bwd(dc, small, ddt_raw, fgb_row):
    s = small.shape[0]
    t = _blk(s, FORGET_BLOCK)
    nb = s // t
    rev = lambda i: nb - 1 - i

    def body(dc_ref, sm_ref, ddt_ref, b_ref, ds_ref, dfb_ref, carry):
        i = pl.program_id(0)

        @pl.when(i == 0)
        def _():
            carry[...] = jnp.zeros_like(carry)
            dfb_ref[...] = jnp.zeros_like(dfb_ref)

        lane = _iota((t, LANES), 1)
        rows = dc_ref[...].T
        tri = (_iota((t, t), 1) <= _iota((t, t), 0)).astype(F32)
        rc = _mm_exact(rows, tri) + carry[:, 0:1]
        carry[...] = jnp.broadcast_to(rc[:, 0:1], (LANES, LANES))
        in_f = (lane >= N_HEADS) & (lane < 2 * N_HEADS)
        df = jnp.where(in_f, rc.T * _sigmoid(-(sm_ref[...] + b_ref[...])), 0.0)
        ds_ref[...] = (df + ddt_ref[...]).astype(BF16)
        dfb_ref[...] += jnp.sum(df, axis=0, keepdims=True)

    blk = pl.BlockSpec((t, LANES), lambda i: (rev(i), 0))
    return pl.pallas_call(
        body, name="forget_bwd",
        out_shape=(jax.ShapeDtypeStruct((s, LANES), BF16), jax.ShapeDtypeStruct((1, LANES), F32)),
        grid=(nb,),
        in_specs=[blk, blk, blk, _const_spec((1, LANES))],
        out_specs=(blk, _const_spec((1, LANES))),
        scratch_shapes=[pltpu.VMEM((LANES, LANES), F32)],
        compiler_params=_params(("arbitrary",)),
    )(dc, small, ddt_raw, fgb_row)


ATT_BLOCK = 1024
ATT_BLOCK_BWD = 512
ATT_BLOCK_BWD_Q = 512
ATT_SCALE = HEAD_DIM ** -0.5
AUG_A = HEAD_DIM
AUG_B = HEAD_DIM + 3


def _split3(c):
    hi = c.astype(BF16).astype(F32)
    r = c - hi
    mid = r.astype(BF16).astype(F32)
    return hi, mid, (r - mid).astype(BF16).astype(F32)


def _aug(lane, first, parts=None, value=1.0):
    if parts is None:
        return jnp.where((lane >= first) & (lane < first + 3), value, 0.0)
    return (jnp.where(lane == first, parts[0], 0.0) + jnp.where(lane == first + 1, parts[1], 0.0)
            + jnp.where(lane == first + 2, parts[2], 0.0))


def _pack_pair(a0, a1, lane):
    return jnp.where(lane < HEAD_DIM, a0, pltpu.roll(a1, HEAD_DIM, 1))


def proj_qkv_heads(u, w_q, w_k, w_v, cum, later):
    s = u.shape[0]
    tm = _blk(s, 256)
    nsteps = s // tm
    n_later = len(later)

    def body(u_ref, wq_ref, wk_ref, wv_ref, c_ref, *rest):
        later_in = rest[:n_later]
        qa_ref, ka_ref, va_ref, nrm_ref = rest[n_later:n_later + 4]
        later_out = rest[n_later + 4:2 * n_later + 4]
        sems = rest[2 * n_later + 4:]
        step = pl.program_id(0)

        @pl.when(step == 0)
        def _():
            for cp in gather_copies(later_in, later_out, *sems)[0]:
                cp.start()

        @pl.when(step == nsteps // 2)
        def _():
            for landed, onward in zip(*gather_copies(later_in, later_out, *sems)):
                landed.wait_recv()
                onward.start()

        @pl.when(step == nsteps - 1)
        def _():
            fetched, passed = gather_copies(later_in, later_out, *sems)
            for cp in passed:
                cp.wait_recv()
            for cp in fetched + passed:
                cp.wait_send()

        lane = _iota((tm, LANES), 1)
        lo = lane < HEAD_DIM
        uv = u_ref[...]
        qf = _mm(uv, wq_ref[...]) * ATT_SCALE
        kf = _mm(uv, wk_ref[...])
        vf = _mm(uv, wv_ref[...])
        cc = c_ref[...]
        ones_a = _aug(lane, AUG_A)
        ones_b = _aug(lane, AUG_B)
        sub8 = _iota((8, LANES), 0)
        nrm = jnp.zeros((8, LANES), F32)
        for h in range(N_HEADS):
            j, e = divmod(h, 2)

            def head(full):
                blk = full[:, LANES * j:LANES * (j + 1)]
                if e == 1:
                    blk = pltpu.roll(blk, HEAD_DIM, 1)
                return jnp.where(lo, blk, 0.0)

            parts = _split3(cc[:, N_HEADS + h:N_HEADS + h + 1])
            qh, kh = head(qf), head(kf)
            qa_ref[h] = (qh + _aug(lane, AUG_A, parts) + ones_b).astype(BF16)
            ka_ref[h] = (kh + ones_a - _aug(lane, AUG_B, parts)).astype(BF16)
            va_ref[h] = (head(vf) + ones_a).astype(BF16)
        seg = (_iota((ATT_WIDTH, LANES), 1) == (_iota((ATT_WIDTH, LANES), 0) >> 6)).astype(BF16)
        for r, val in enumerate((qf, kf)):
            sq = val * val
            hi = sq.astype(BF16)
            tot = _mm(hi, seg) + _mm((sq - hi.astype(F32)).astype(BF16), seg)
            nrm = nrm + jnp.where(sub8 == r, jnp.max(tot, axis=0, keepdims=True), 0.0)
        nrm_ref[0] = nrm

    shp = jax.ShapeDtypeStruct((N_HEADS, s, LANES), BF16)
    hspec = pl.BlockSpec((N_HEADS, tm, LANES), lambda i: (0, i, 0))
    wspec = _const_spec((D_MODEL, ATT_WIDTH))
    return pl.pallas_call(
        body, name="proj_qkv_heads",
        out_shape=tuple([shp, shp, shp, jax.ShapeDtypeStruct((nsteps, 8, LANES), F32)]
                        + [jax.ShapeDtypeStruct((N_CHIPS,) + a.shape, a.dtype) for a in later]),
        grid=(nsteps,),
        in_specs=[pl.BlockSpec((tm, D_MODEL), lambda i: (i, 0)), wspec, wspec, wspec,
                  pl.BlockSpec((tm, LANES), lambda i: (i, 0))] + [ANY] * n_later,
        out_specs=tuple([hspec, hspec, hspec, pl.BlockSpec((1, 8, LANES), lambda i: (i, 0, 0))]
                        + [ANY] * n_later),
        scratch_shapes=_sems(3 * n_later) + _sems(3 * n_later),
        compiler_params=_params(("arbitrary",)),
    )(u, w_q, w_k, w_v, cum, *later)


SKIP_BELOW = -110.0


def live_blocks(norms, cum, tq, tk):
    qn = jnp.sqrt(jnp.max(norms[:, 0, :N_HEADS], axis=0))
    kn = jnp.sqrt(jnp.max(norms[:, 1, :N_HEADS], axis=0))
    bound = 2.05 * qn * kn + 2.0
    c_first = cum[0::tq, N_HEADS:2 * N_HEADS]
    c_last = cum[tk - 1::tk, N_HEADS:2 * N_HEADS]
    nq, nk = c_first.shape[0], c_last.shape[0]
    top = bound[None, None, :] + c_first[:, None, :] - c_last[None, :, :]
    before = (jnp.arange(nk)[None, :] + 1) * tk <= jnp.arange(nq)[:, None] * tq
    dead = before[:, :, None] & ~(top >= SKIP_BELOW)
    first = jnp.sum(dead, axis=1).astype(jnp.int32).T
    last_q = jnp.sum(first[:, None, :] <= jnp.arange(nk)[None, :, None], axis=2).astype(jnp.int32) - 1
    return first, last_q


def attention_fwd(first, qa, ka, va):
    s = qa.shape[1]
    t = _blk(s, ATT_BLOCK)
    nq = s // t

    def body(first_ref, qa_ref, ka_ref, va_ref, o_ref, qb_ref, m_scr, acc_scr, alpha_scr, p_scr, s_scr):
        qi = pl.program_id(1)
        starts = [first_ref[2 * pl.program_id(0) + e, qi] for e in range(2)]
        k0 = jnp.maximum(starts[0], starts[1])
        m_scr[...] = jnp.full_like(m_scr, NEG_BIG)
        acc_scr[...] = jnp.zeros_like(acc_scr)

        def kv_rows(kb):
            return pl.ds(pl.multiple_of(kb * t, t), t)

        def logits(kb, masked, heads=(0, 1)):
            for e in heads:
                sc = _mm_nt(qa_ref[e], ka_ref[e, kv_rows(kb), :])
                if masked:
                    sc = jnp.where(_iota((t, t), 0) >= _iota((t, t), 1), sc, NEG_BIG)
                s_scr[e] = sc

        def probs(heads=(0, 1)):
            for e in heads:
                cmax = s_scr[e, :, 0:LANES]
                for c in range(1, t // LANES):
                    cmax = jnp.maximum(cmax, s_scr[e, :, LANES * c:LANES * (c + 1)])
                m_old = m_scr[e]
                m_new = jnp.maximum(m_old, jnp.max(cmax, axis=1, keepdims=True))
                alpha_scr[e] = jnp.exp(m_old - m_new)
                m_scr[e] = m_new
                for c in range(t // LANES):
                    cols = slice(LANES * c, LANES * (c + 1))
                    p_scr[e, :, cols] = jnp.exp(s_scr[e, :, cols] - m_new).astype(BF16)

        def accumulate(kb, heads=(0, 1)):
            for e in heads:
                acc_scr[e] = alpha_scr[e] * acc_scr[e] + _mm(p_scr[e], va_ref[e, kv_rows(kb), :])

        for e in range(2):
            def alone(kb, carry, e=e):
                logits(kb, False, (e,))
                probs((e,))
                accumulate(kb, (e,))
                return carry

            lax.fori_loop(starts[e], k0, alone, 0)

        def loop_body(kb, carry):
            logits(kb, False)
            for e in range(2):
                accumulate(kb - 1, (e,))
                probs((e,))
            return carry

        @pl.when(qi > k0)
        def _():
            logits(k0, False)
            probs()

        lax.fori_loop(k0 + 1, qi, loop_body, 0)

        @pl.when(qi > k0)
        def _():
            logits(qi, True)
            accumulate(qi - 1)
            probs()

        @pl.when(qi == k0)
        def _():
            logits(qi, True)
            probs()

        accumulate(qi)

        lane = _iota((t, LANES), 1)
        outs = []
        for e in range(2):
            acc = acc_scr[e]
            l = acc[:, AUG_A:AUG_A + 1]
            outs.append(acc / l)
            lse = m_scr[e][:, 0:1] + jnp.log(l)
            q32 = qa_ref[e].astype(F32)
            c = q32[:, AUG_A:AUG_A + 1] + q32[:, AUG_A + 1:AUG_A + 2] + q32[:, AUG_A + 2:AUG_A + 3]
            qb = jnp.where(lane < HEAD_DIM, q32, 0.0) + _aug(lane, AUG_A, _split3(c - lse)) + _aug(lane, AUG_B)
            qb_ref[e] = qb.astype(BF16)
        o_ref[...] = _pack_pair(outs[0], outs[1], lane)

    grid_spec = pltpu.PrefetchScalarGridSpec(
        num_scalar_prefetch=1, grid=(N_PAIRS, nq),
        in_specs=[pl.BlockSpec((2, t, LANES), lambda j, qi, f: (j, qi, 0)),
                  pl.BlockSpec((2, s, LANES), lambda j, qi, f: (j, 0, 0)),
                  pl.BlockSpec((2, s, LANES), lambda j, qi, f: (j, 0, 0))],
        out_specs=[pl.BlockSpec((t, LANES), lambda j, qi, f: (qi, j)),
                   pl.BlockSpec((2, t, LANES), lambda j, qi, f: (j, qi, 0))],
        scratch_shapes=[pltpu.VMEM((2, t, LANES), F32), pltpu.VMEM((2, t, LANES), F32),
                        pltpu.VMEM((2, t, LANES), F32), pltpu.VMEM((2, t, t), BF16), pltpu.VMEM((2, t, t), F32)])
    return pl.pallas_call(
        body, name="attention_fwd", grid_spec=grid_spec,
        out_shape=(jax.ShapeDtypeStruct((s, ATT_WIDTH), F32), jax.ShapeDtypeStruct((N_HEADS, s, LANES), BF16)),
        compiler_params=_params(("parallel", "parallel")),
    )(first, qa, ka, va)


def attention_bwd(last_q, qb, ka, va, dob):
    s = qb.shape[1]
    t = _blk(s, ATT_BLOCK_BWD)
    tq = _blk(s, ATT_BLOCK_BWD_Q)
    nq = s // tq
    per_q = tq // t

    def body(last_ref, qb_ref, dob_ref, ka_ref, va_ref, dq_ref, dk_ref, dv_ref, dc_ref, dq_scr, dk_scr, dv_scr):
        j, ki = pl.program_id(0), pl.program_id(1)

        @pl.when((j == 0) & (ki == 0))
        def _():
            dc_ref[...] = jnp.zeros_like(dc_ref)

        @pl.when(ki == 0)
        def _():
            dq_scr[...] = jnp.zeros_like(dq_scr)

        dk_scr[...] = jnp.zeros_like(dk_scr)
        dv_scr[...] = jnp.zeros_like(dv_scr)

        def q_step(qblk, masked, heads=(0, 1)):
            rows = pl.ds(pl.multiple_of(qblk * tq, tq), tq)
            scs = [_mm_nt(qb_ref[e, rows, :], ka_ref[e]) for e in heads]
            dps = [_mm_nt(dob_ref[e, rows, :], va_ref[e]) for e in heads]
            for e, sc, dp in zip(heads, scs, dps):
                q = qb_ref[e, rows, :]
                do = dob_ref[e, rows, :]
                if masked:
                    keep = (_iota((tq, t), 0) - _iota((tq, t), 1)) >= ki * t - qblk * tq
                    sc = jnp.where(keep, sc, NEG_BIG)
                p = jnp.exp(sc)
                ds_b = (p * dp).astype(BF16)
                dv_scr[e] += _mm_tn(p.astype(BF16), do)
                dk_scr[e] += _mm_tn(ds_b, q)
                dq_scr[e, rows, :] += _mm(ds_b, ka_ref[e])

        def loop_body(qblk, carry):
            q_step(qblk, False)
            return carry

        ends = [last_ref[2 * j + e, ki] + 1 for e in range(2)]
        both = jnp.minimum(ends[0], ends[1])
        diag = ki // per_q
        q_step(diag, True)
        lax.fori_loop(diag + 1, both, loop_body, 0)
        for e in range(2):
            def alone(qblk, carry, e=e):
                q_step(qblk, False, (e,))
                return carry

            lax.fori_loop(both, ends[e], alone, 0)

        lane = _iota((t, LANES), 1)
        dk_ref[...] = _pack_pair(dk_scr[0], dk_scr[1], lane).astype(BF16)
        dv_ref[...] = _pack_pair(dv_scr[0], dv_scr[1], lane).astype(BF16)
        rows = pl.ds(pl.multiple_of(ki * t, t), t)
        dc_ref[rows, :] -= (jnp.where(lane == N_HEADS + 2 * j, dk_scr[0][:, AUG_B:AUG_B + 1], 0.0)
                            + jnp.where(lane == N_HEADS + 2 * j + 1, dk_scr[1][:, AUG_B:AUG_B + 1], 0.0))

        @pl.when(ki == s // t - 1)
        def _():
            for blk in range(s // t):
                rws = pl.ds(blk * t, t)
                d0 = dq_scr[0, rws, :]
                d1 = dq_scr[1, rws, :]
                dq_ref[rws, :] = (_pack_pair(d0, d1, lane) * ATT_SCALE).astype(BF16)
                dc_ref[rws, :] += (jnp.where(lane == N_HEADS + 2 * j, d0[:, AUG_A:AUG_A + 1], 0.0)
                                   + jnp.where(lane == N_HEADS + 2 * j + 1, d1[:, AUG_A:AUG_A + 1], 0.0))

    full = pl.BlockSpec((2, s, LANES), lambda j, ki, f: (j, 0, 0))
    blk = pl.BlockSpec((2, t, LANES), lambda j, ki, f: (j, ki, 0))
    pair = pl.BlockSpec((t, LANES), lambda j, ki, f: (ki, j))
    wide = jax.ShapeDtypeStruct((s, ATT_WIDTH), BF16)
    grid_spec = pltpu.PrefetchScalarGridSpec(
        num_scalar_prefetch=1, grid=(N_PAIRS, s // t),
        in_specs=[full, full, blk, blk],
        out_specs=[pl.BlockSpec((s, LANES), lambda j, ki, f: (0, j)), pair, pair,
                   pl.BlockSpec((s, LANES), lambda j, ki, f: (0, 0))],
        scratch_shapes=[pltpu.VMEM((2, s, LANES), F32), pltpu.VMEM((2, t, LANES), F32),
                        pltpu.VMEM((2, t, LANES), F32)])
    return pl.pallas_call(
        body, name="attention_bwd", grid_spec=grid_spec,
        out_shape=(wide, wide, wide, jax.ShapeDtypeStruct((s, LANES), F32)),
        compiler_params=_params(("arbitrary", "arbitrary")),
    )(last_q, qb, dob, ka, va)


def _dsilu(z, sg):
    return sg * (1.0 + z * (1.0 - sg))


def post_mix(x, y, zs, o, za, p, tgt, ssd_g, att_g_lane, ple_g, fin_g, w_out, w_gate, w_proj):
    s = x.shape[0]
    tm = _blk(s, 256)
    half = SSD_WIDTH // N_GROUPS

    def rms_bwd(dy, yn, r):
        return r * (dy - yn * jnp.mean(dy * yn, axis=-1, keepdims=True))

    def colsum(a):
        return jnp.sum(a, axis=0, keepdims=True)

    def body(x_ref, y_ref, zs_ref, o_ref, za_ref, p_ref, t_ref, sg_ref, ag_ref, pg_ref, fg_ref,
             wo_ref, wg_ref, wp_ref,
             dh1_ref, dy_ref, dzs_ref, dob_ref, dza_ref, ycat_ref, dh1b_ref, n2b_ref, dglb_ref, dppb_ref, pb_ref,
             loss_ref, dfin_ref, dple_ref, dssd_ref, datt_ref):
        @pl.when(pl.program_id(0) == 0)
        def _():
            for r in (loss_ref, dfin_ref, dple_ref, dssd_ref, datt_ref):
                r[...] = jnp.zeros_like(r)

        lane = _iota((tm, LANES), 1)
        lo = lane < HEAD_DIM
        zs = zs_ref[...]
        sz = _sigmoid(zs)
        yv = y_ref[...]
        ys = yv * (zs * sz)
        yn, rg = [], []
        for g in range(N_GROUPS):
            seg = ys[:, half * g:half * (g + 1)]
            r = lax.rsqrt(jnp.mean(seg * seg, axis=-1, keepdims=True) + EPS)
            yn.append(seg * r)
            rg.append(r)
            ycat_ref[:, half * g:half * (g + 1)] = (yn[g] * sg_ref[:, half * g:half * (g + 1)]).astype(BF16)
        za = za_ref[...]
        sza = _sigmoid(za)
        silu_za = za * sza
        on, ra = [], []
        for jb in range(N_PAIRS):
            blk = o_ref[:, LANES * jb:LANES * (jb + 1)]
            sq = blk * blk
            ms0 = jnp.sum(jnp.where(lo, sq, 0.0), axis=1, keepdims=True) * (1.0 / HEAD_DIM)
            ms1 = jnp.sum(jnp.where(lo, 0.0, sq), axis=1, keepdims=True) * (1.0 / HEAD_DIM)
            r = jnp.where(lo, lax.rsqrt(ms0 + EPS), lax.rsqrt(ms1 + EPS))
            on.append(blk * r)
            ra.append(r)
            an = on[jb] * ag_ref[:, LANES * jb:LANES * (jb + 1)]
            ycat_ref[:, SSD_WIDTH + LANES * jb:SSD_WIDTH + LANES * (jb + 1)] = (
                an * silu_za[:, LANES * jb:LANES * (jb + 1)]).astype(BF16)
        h1 = x_ref[...] + _mm(ycat_ref[...], wo_ref[...])
        r2 = lax.rsqrt(jnp.mean(h1 * h1, axis=-1, keepdims=True) + EPS)
        n2h = h1 * r2
        n2_b = (n2h * pg_ref[...]).astype(BF16)
        gate = _sigmoid(_mm(n2_b, wg_ref[...]))
        p_b = p_ref[...].astype(BF16)
        pp = _mm(p_b, wp_ref[...])
        h2 = h1 + gate * pp
        r3 = lax.rsqrt(jnp.mean(h2 * h2, axis=-1, keepdims=True) + EPS)
        n3 = h2 * r3
        diff = n3 * fg_ref[...] - t_ref[...]
        sq = colsum(diff * diff)
        part = sq[:, 0:LANES]
        for jb in range(1, D_MODEL // LANES):
            part = part + sq[:, LANES * jb:LANES * (jb + 1)]
        loss_ref[...] += part * (0.5 / D_MODEL)
        dout = diff * (1.0 / D_MODEL)
        dfin_ref[...] += colsum(dout * n3)
        dh2 = rms_bwd(dout * fg_ref[...], n3, r3)
        dgl = dh2 * pp * gate * (1.0 - gate)
        dgl_b = dgl.astype(BF16)
        dn2 = _mm_nt(dgl_b, wg_ref[...])
        dple_ref[...] += colsum(dn2 * n2h)
        dh1 = dh2 + rms_bwd(dn2 * pg_ref[...], n2h, r2)
        dh1_b = dh1.astype(BF16)
        dycat = _mm_nt(dh1_b, wo_ref[...])
        dh1_ref[...] = dh1
        dh1b_ref[...] = dh1_b
        n2b_ref[...] = n2_b
        dglb_ref[...] = dgl_b
        dppb_ref[...] = (dh2 * gate).astype(BF16)
        pb_ref[...] = p_b
        for g in range(N_GROUPS):
            cols = slice(half * g, half * (g + 1))
            dys_g = dycat[:, cols]
            dssd_ref[:, cols] += colsum(dys_g * yn[g])
            dys = rms_bwd(dys_g * sg_ref[:, cols], yn[g], rg[g])
            dy_ref[:, cols] = dys * (zs[:, cols] * sz[:, cols])
            dzs_ref[:, cols] = (dys * yv[:, cols] * _dsilu(zs[:, cols], sz[:, cols])).astype(BF16)
        for jb in range(N_PAIRS):
            cols = slice(LANES * jb, LANES * (jb + 1))
            dya = dycat[:, SSD_WIDTH + LANES * jb:SSD_WIDTH + LANES * (jb + 1)]
            ag = ag_ref[:, cols]
            dan = dya * silu_za[:, cols]
            dza_ref[:, cols] = (dya * (on[jb] * ag) * _dsilu(za[:, cols], sza[:, cols])).astype(BF16)
            datt_ref[:, cols] += colsum(dan * on[jb])
            don = dan * ag
            q = don * on[jb]
            m0 = jnp.sum(jnp.where(lo, q, 0.0), axis=1, keepdims=True) * (1.0 / HEAD_DIM)
            m1 = jnp.sum(jnp.where(lo, 0.0, q), axis=1, keepdims=True) * (1.0 / HEAD_DIM)
            do2 = ra[jb] * (don - on[jb] * jnp.where(lo, m0, m1))
            prod = do2 * o_ref[:, cols]
            for e in range(2):
                delta = jnp.sum(jnp.where(lo, prod, 0.0) if e == 0 else jnp.where(lo, 0.0, prod),
                                axis=1, keepdims=True)
                base = jnp.where(lo, do2 if e == 0 else pltpu.roll(do2, HEAD_DIM, 1), 0.0)
                dob_ref[2 * jb + e] = (base - _aug(lane, AUG_A, _split3(delta))).astype(BF16)

    def rows(n, dtype=None):
        return pl.BlockSpec((tm, n), lambda i: (i, 0))

    def out(n, dtype):
        return jax.ShapeDtypeStruct((s, n), dtype)

    vec = _const_spec((1, D_MODEL))
    vshape = jax.ShapeDtypeStruct((1, D_MODEL), F32)
    return pl.pallas_call(
        body, name="post_mix",
        out_shape=(out(D_MODEL, F32), out(SSD_WIDTH, F32), out(SSD_WIDTH, BF16),
                   jax.ShapeDtypeStruct((N_HEADS, s, LANES), BF16),
                   out(ATT_WIDTH, BF16), out(D_INNER, BF16), out(D_MODEL, BF16), out(D_MODEL, BF16),
                   out(D_MODEL, BF16), out(D_MODEL, BF16), out(PLE_DIM, BF16),
                   jax.ShapeDtypeStruct((1, LANES), F32), vshape, vshape, vshape, vshape),
        grid=(s // tm,),
        in_specs=[rows(D_MODEL), rows(SSD_WIDTH), rows(SSD_WIDTH), rows(ATT_WIDTH), rows(ATT_WIDTH),
                  rows(PLE_DIM), rows(D_MODEL), vec, vec, vec, vec,
                  _const_spec((D_INNER, D_MODEL)), _const_spec((D_MODEL, D_MODEL)), _const_spec((PLE_DIM, D_MODEL))],
        out_specs=(rows(D_MODEL), rows(SSD_WIDTH), rows(SSD_WIDTH),
                   pl.BlockSpec((N_HEADS, tm, LANES), lambda i: (0, i, 0)), rows(ATT_WIDTH),
                   rows(D_INNER), rows(D_MODEL), rows(D_MODEL), rows(D_MODEL), rows(D_MODEL), rows(PLE_DIM),
                   _const_spec((1, LANES)), vec, vec, vec, vec),
        compiler_params=_params(("arbitrary",)),
    )(x, y, zs, o, za, p, tgt, ssd_g, att_g_lane, ple_g, fin_g, w_out, w_gate, w_proj)


def in_proj_bwd(dsegs, wsegs, x, g, dh1, pres):
    s = x.shape[0]
    tm = _blk(s, 256)
    nseg = len(dsegs)
    nbig = len(pres)
    nsteps = s // tm

    def body(*refs):
        d_refs = refs[:nseg]
        w_refs = refs[nseg:2 * nseg]
        x_ref, g_ref, dh1_ref = refs[2 * nseg:2 * nseg + 3]
        rest = refs[2 * nseg + 3:]
        pre_refs, (dx_ref, dg_ref), part_refs = rest[:nbig], rest[nbig:nbig + 2], rest[nbig + 2:2 * nbig + 2]
        ssem, rsem, lsem = rest[2 * nbig + 2:]

        @pl.when(pl.program_id(0) == 0)
        def _():
            dg_ref[...] = jnp.zeros_like(dg_ref)
            for cp in scatter_copies(pre_refs, part_refs, ssem, rsem, lsem):
                cp.start()

        @pl.when(pl.program_id(0) == nsteps - 1)
        def _():
            for cp in scatter_copies(pre_refs, part_refs, ssem, rsem, lsem):
                cp.wait()

        du = _mm_nt(d_refs[0][...], w_refs[0][...])
        for k in range(1, nseg):
            du = du + _mm_nt(d_refs[k][...], w_refs[k][...])
        xv = x_ref[...]
        r = lax.rsqrt(jnp.mean(xv * xv, axis=-1, keepdims=True) + EPS)
        xh = xv * r
        dg_ref[...] += jnp.sum(du * xh, axis=0, keepdims=True)
        dxh = du * g_ref[...]
        dx_ref[...] = r * (dxh - xh * jnp.mean(dxh * xh, axis=-1, keepdims=True)) + dh1_ref[...]

    rows = lambda n: pl.BlockSpec((tm, n), lambda i: (i, 0))
    return pl.pallas_call(
        body, name="in_proj_bwd",
        out_shape=tuple([jax.ShapeDtypeStruct((s, D_MODEL), F32), jax.ShapeDtypeStruct((1, D_MODEL), F32)]
                        + [jax.ShapeDtypeStruct(a.shape, a.dtype) for a in pres]),
        grid=(nsteps,),
        in_specs=([rows(d.shape[1]) for d in dsegs] + [_const_spec(w.shape) for w in wsegs]
                  + [rows(D_MODEL), _const_spec((1, D_MODEL)), rows(D_MODEL)] + [ANY] * nbig),
        out_specs=tuple([rows(D_MODEL), _const_spec((1, D_MODEL))] + [ANY] * nbig),
        scratch_shapes=_sems(3 * nbig) + [pltpu.SemaphoreType.DMA((nbig,))],
        compiler_params=_params(("arbitrary",)),
    )(*dsegs, *wsegs, x, g, dh1, *pres)


SMALL_NAMES = ("norm_g", "conv_b", "dt_bias", "a_log", "d_skip", "ssd_norm_g", "fg_bias", "att_norm_g",
               "ple_norm_g", "final_norm_g")
SMALL_SIZES = (1024, 1536, 16, 16, 16, 1024, 16, 64, 1024, 1024)
CONV_W_SIZE = CONV_WIDTH * CONV_CH


def _pack_small(vals):
    flat = jnp.concatenate([v.reshape(-1).astype(F32) for v in vals])
    flat = jnp.pad(flat, (0, SMALL_ROWS * LANES - flat.shape[0]))
    return flat.reshape(SMALL_ROWS, LANES)


def _unpack_small(pack, shapes):
    flat = pack.reshape(-1)
    out, off = [], 0
    for n, shp in zip(SMALL_SIZES, shapes):
        out.append(flat[off:off + n].reshape(shp))
        off += n
    return out


def _row128(v16, offset=0):
    return jnp.pad(v16.reshape(1, N_HEADS).astype(F32), ((0, 0), (offset, LANES - N_HEADS - offset)))


def local_step(prereduce, later, join_later, x, p, tgt, w_in, conv_w, norm_g, conv_b, dt_bias, a_log, d_skip,
               ssd_norm_g, fg_bias, att_norm_g, ple_norm_g, final_norm_g):
    widths = (SSD_WIDTH, CONV_CH, N_HEADS, ATT_WIDTH, ATT_WIDTH, ATT_WIDTH, ATT_WIDTH)
    c0, c1, c2, c3, c4, c5, c6, c7 = [sum(widths[:i]) for i in range(len(widths) + 1)]
    w_zs, w_xbc, w_dt = w_in[:, c0:c1], w_in[:, c1:c2], w_in[:, c2:c3]
    w_za, w_q, w_k, w_v, w_f = w_in[:, c3:c4], w_in[:, c4:c5], w_in[:, c5:c6], w_in[:, c6:c7], w_in[:, c7:]
    w_small = jnp.concatenate([w_dt, w_f, jnp.zeros((D_MODEL, LANES - 2 * N_HEADS), BF16)], axis=1)

    dtb_row = _row128(dt_bias)
    a_row = _row128(-jnp.exp(a_log.astype(F32)))
    fgb_row = _row128(fg_bias, N_HEADS)
    dskip_lane = jnp.repeat(d_skip.astype(F32), HEAD_DIM).reshape(1, SSD_WIDTH)
    att_g_lane = jnp.tile(att_norm_g.astype(F32), N_HEADS).reshape(1, ATT_WIDTH)
    row = lambda v: v.reshape(1, -1).astype(F32)

    u = rms_prenorm(x, row(norm_g))
    zs = matmul_rows(u, w_zs, F32, "proj_z_ssd")
    xbc = matmul_rows(u, w_xbc, F32, "proj_xbc")
    za = matmul_rows(u, w_za, F32, "proj_z_att")
    small = matmul_rows(u, w_small, F32, "proj_small")
    cum = forget_cumsum(small, fgb_row)
    qa, ka, va, norms, *gathered = proj_qkv_heads(u, w_q, w_k, w_v, cum, later)
    w_out, w_gate, w_proj = join_later(gathered)
    n_seq = x.shape[0]
    first, _ = live_blocks(norms, cum, _blk(n_seq, ATT_BLOCK), _blk(n_seq, ATT_BLOCK))
    _, last_q = live_blocks(norms, cum, _blk(n_seq, ATT_BLOCK_BWD_Q), _blk(n_seq, ATT_BLOCK_BWD))
    pre, xc = conv_fwd(xbc, conv_w, row(conv_b))
    y, states = ssd_fwd(xc, small, dtb_row, a_row, dskip_lane)
    o, qb = attention_fwd(first, qa, ka, va)
    (dh1, dy, dzs, dob, dza, ycat, dh1_b, n2_b, dgl_b, dpp_b, p_b,
     loss_l, dfin, dple, dssd_g, datt_lane) = post_mix(
        x, y, zs, o, za, p, tgt, row(ssd_norm_g), att_g_lane, row(ple_norm_g), row(final_norm_g),
        w_out, w_gate, w_proj)
    dq, dk, dv, dc = attention_bwd(last_q, qb, ka, va, dob)
    dxc, ddt_raw, da, ddtb, ddsk_lane = ssd_bwd(xc, small, states, dy, dtb_row, a_row, dskip_lane)
    dsmall, dfgb = forget_bwd(dc, small, ddt_raw, fgb_row)
    dxbc, dconv_w8, dconv_b = conv_bwd(xbc, pre, dxc, conv_w)
    dsegs = [dzs, dxbc, dza, dq, dk, dv, dsmall]
    wsegs = [w_zs, w_xbc, w_za, w_q, w_k, w_v, w_small]
    dws = [matmul_tn(u, d, "dw_in_%d" % i) for i, d in enumerate(dsegs)]
    dw_in = jnp.concatenate([dws[0], dws[1], dws[6][:, :N_HEADS], dws[2], dws[3], dws[4], dws[5],
                             dws[6][:, N_HEADS:2 * N_HEADS]], axis=1)
    dw_out = matmul_tn(ycat, dh1_b, "dw_out")
    dw_gate = matmul_tn(n2_b, dgl_b, "dw_gate")
    dw_proj = matmul_tn(p_b, dpp_b, "dw_proj")
    dx, dnorm_g, *parts = in_proj_bwd(dsegs, wsegs, x, row(norm_g), dh1, prereduce(dw_in, dw_out, dw_gate, dw_proj))
    small_grads = [
        dnorm_g, dconv_b, ddtb[0, :N_HEADS], (da * a_row)[0, :N_HEADS],
        ddsk_lane.reshape(N_HEADS, HEAD_DIM).sum(axis=1), dssd_g, dfgb[0, N_HEADS:2 * N_HEADS],
        datt_lane.reshape(N_HEADS, HEAD_DIM).sum(axis=0), dple, dfin]
    loss = jnp.sum(loss_l)
    return loss, dx, parts, dconv_w8[:CONV_WIDTH], small_grads


def kernel(x, p, norm_g, w_in, conv_w, conv_b, dt_bias, a_log, d_skip, ssd_norm_g, fg_bias, att_norm_g, w_out, ple_norm_g, w_ple_gate, w_ple_proj, final_norm_g, loss_target, m_norm_g, m_w_in, m_conv_w, m_conv_b, m_dt_bias, m_a_log, m_d_skip, m_ssd_norm_g, m_fg_bias, m_att_norm_g, m_w_out, m_ple_norm_g, m_w_ple_gate, m_w_ple_proj, m_final_norm_g, v_norm_g, v_w_in, v_conv_w, v_conv_b, v_dt_bias, v_a_log, v_d_skip, v_ssd_norm_g, v_fg_bias, v_att_norm_g, v_w_out, v_ple_norm_g, v_w_ple_gate, v_w_ple_proj, v_final_norm_g):
    chip = 2 * lax.axis_index("x") + lax.axis_index("y")
    core = lax.axis_index("c")

    big_w = [w_in[0], w_out[0], w_ple_gate[0], w_ple_proj[0]]
    own = [a.astype(BF16) for a in big_w] + [conv_w[0]]

    def joined(mine, gathered, axis):
        return jnp.concatenate([jnp.where(chip == j, mine, gathered[j]) for j in range(N_CHIPS)], axis=axis)

    w_in_all, conv_all = gather_weights(own[:1], own[4])
    w_in_f, conv_w_f = joined(own[0], w_in_all, 1), joined(own[4], conv_all, 1)

    def join_later(gathered):
        return [joined(mine, got, axis) for mine, got, axis in zip(own[1:4], gathered, (0, 0, 1))]

    core1 = core.reshape(1).astype(jnp.int32)

    def prereduce(dw_in, dw_out, dw_gate, dw_proj):
        n_in, n_proj = w_in.shape[2], w_ple_proj.shape[2]
        gs = [jnp.stack([dw_in[:, n_in * j:n_in * (j + 1)] for j in range(N_CHIPS)]),
              dw_out.reshape(N_CHIPS, w_out.shape[1], D_MODEL), dw_gate.reshape(N_CHIPS, w_ple_gate.shape[1], D_MODEL),
              jnp.stack([dw_proj[:, n_proj * j:n_proj * (j + 1)] for j in range(N_CHIPS)])]
        return add_halves(core1, gs, halves_to_sibling(gs))

    smalls_w = [norm_g, conv_b, dt_bias, a_log, d_skip, ssd_norm_g, fg_bias, att_norm_g, ple_norm_g, final_norm_g]
    loss_l, dx, parts, dconv_w, small_grads = local_step(
        prereduce, own[1:4], join_later, x[0], p[0, 0], loss_target[0], w_in_f, conv_w_f,
        *[a.reshape(-1) for a in smalls_w])
    loss = lax.psum(loss_l, ("x", "y", "c"))
    smalls = gather_small(_pack_small(list(small_grads) + [dconv_w]))
    mine = sum_parts(parts)

    g_big, d_big, m_big, v_big = adamw_big(
        core1, mine, swap_halves(mine), big_w, [m_w_in[0], m_w_out[0], m_w_ple_gate[0], m_w_ple_proj[0]],
        [v_w_in[0], v_w_out[0], v_w_ple_gate[0], v_w_ple_proj[0]])
    smalls_m = [m_norm_g, m_conv_b, m_dt_bias, m_a_log, m_d_skip, m_ssd_norm_g, m_fg_bias, m_att_norm_g,
                m_ple_norm_g, m_final_norm_g]
    smalls_v = [v_norm_g, v_conv_b, v_dt_bias, v_a_log, v_d_skip, v_ssd_norm_g, v_fg_bias, v_att_norm_g,
                v_ple_norm_g, v_final_norm_g]
    g_sm, d_sm, m_sm, v_sm = adamw_small(smalls, _pack_small(smalls_w), _pack_small(smalls_m), _pack_small(smalls_v))
    n_small = sum(SMALL_SIZES)
    g_conv_full = g_sm.reshape(-1)[n_small:n_small + CONV_W_SIZE].reshape(CONV_WIDTH, CONV_CH)
    n_conv = conv_w.shape[2]
    g_conv = lax.dynamic_slice_in_dim(g_conv_full, chip * n_conv, n_conv, axis=1)
    d_conv, m_conv, v_conv = adamw_whole(g_conv, conv_w[0], m_conv_w[0], v_conv_w[0], "adamw_conv")

    shapes = [a.shape for a in smalls_w]
    outs = []
    for big, conv, sm in ((g_big, g_conv, g_sm), (d_big, d_conv, d_sm), (m_big, m_conv, m_sm), (v_big, v_conv, v_sm)):
        b_in, b_out, b_gate, b_proj = [a[None] for a in big]
        s_norm, s_convb, s_dtb, s_alog, s_dsk, s_ssdg, s_fgb, s_attg, s_pleg, s_fin = _unpack_small(sm, shapes)
        outs.extend([s_norm, b_in, conv[None], s_convb, s_dtb, s_alog, s_dsk, s_ssdg, s_fgb, s_attg, b_out, s_pleg,
                     b_gate, b_proj, s_fin])
    return (loss, dx[None], *outs)
```

```python
import functools

import jax
import jax.numpy as jnp
from jax import lax
from jax.experimental import pallas as pl
from jax.experimental.pallas import tpu as pltpu

F32 = jnp.float32
BF16 = jnp.bfloat16

D_MODEL = 1024
SSD_WIDTH = 1024
ATT_WIDTH = 1024
N_HEADS = 16
HEAD_DIM = 64
N_GROUPS = 2
D_STATE = 128
CONV_CH = 1536
CONV_WIDTH = 4
CHUNK = 128
PLE_DIM = 256
D_INNER = 2048
EPS = 1e-6
IN_COLS = 6688
N_CHIPS = 4
N_DEV = 8
LANES = 128
N_PAIRS = 8

ADAM_LR = 0.001
ADAM_B1 = 0.9
ADAM_B2 = 0.999
ADAM_EPS = 1e-08
ADAM_WD = 0.01
ADAM_STEP = 10

SMALL_ROWS = 96

NEG_BIG = -1e30
VMEM_LIMIT = 56 * 1024 * 1024

MESH = pl.DeviceIdType.MESH
ANY = pl.BlockSpec(memory_space=pl.ANY)


def _mm(a, b):
    return jnp.dot(a, b, preferred_element_type=F32)


def _mm_nt(a, b):
    return lax.dot_general(a, b, (((1,), (1,)), ((), ())), preferred_element_type=F32)


def _mm_tn(a, b):
    return lax.dot_general(a, b, (((0,), (0,)), ((), ())), preferred_element_type=F32)


def _mm_exact(a, b):
    return jnp.dot(a, b, preferred_element_type=F32, precision=lax.Precision.HIGHEST)


def _softplus(x):
    return jnp.maximum(x, 0.0) + jnp.log1p(jnp.exp(-jnp.abs(x)))


def _sigmoid(x):
    return jax.nn.sigmoid(x)


def _iota(shape, dim):
    return lax.broadcasted_iota(jnp.int32, shape, dim)


def _params(sem=None):
    return pltpu.CompilerParams(dimension_semantics=sem, vmem_limit_bytes=VMEM_LIMIT)


def _blk(n, pref):
    return min(n, pref)


def _const_spec(shape):
    nd = len(shape)
    return pl.BlockSpec(shape, lambda *_: (0,) * nd)


def _chip_peers():
    x, y, c = lax.axis_index("x"), lax.axis_index("y"), lax.axis_index("c")
    return x, y, c, [(1 - x, y, c), (x, 1 - y, c), (1 - x, 1 - y, c)]


def _half(rows, c):
    h = rows // 2
    return pl.ds(pl.multiple_of(c * h, 8), h)


def _sems(n):
    return [pltpu.SemaphoreType.DMA((n,)), pltpu.SemaphoreType.DMA((n,))]


def gather_copies(ins, outs, ssem1, rsem1, ssem2, rsem2):
    n = len(ins)
    x, y, c, peers = _chip_peers()
    me = 2 * x + y
    fetched, passed = [], []
    for k, peer in enumerate(peers):
        chip = 2 * peer[0] + peer[1]
        for i in range(n):
            h = _half(ins[i].shape[0], c)
            fetched.append(pltpu.make_async_remote_copy(
                src_ref=ins[i].at[h], dst_ref=outs[i].at[me, h], send_sem=ssem1.at[n * k + i],
                recv_sem=rsem1.at[n * k + i], device_id=peer, device_id_type=MESH))
            passed.append(pltpu.make_async_remote_copy(
                src_ref=outs[i].at[chip, h], dst_ref=outs[i].at[chip, h], send_sem=ssem2.at[n * k + i],
                recv_sem=rsem2.at[n * k + i], device_id=(x, y, 1 - c), device_id_type=MESH))
    return fetched, passed


def gather_weights(shards, conv_s):
    n = len(shards)

    def body(*refs):
        ins, conv_in = refs[:n], refs[n]
        outs, conv_out = refs[n + 1:2 * n + 1], refs[2 * n + 1]
        ssem1, rsem1, ssem2, rsem2, c_ssem, c_rsem = refs[2 * n + 2:]
        x, y, _, peers = _chip_peers()
        fetched, passed = gather_copies(ins, outs, ssem1, rsem1, ssem2, rsem2)
        small = [pltpu.make_async_remote_copy(
            src_ref=conv_in, dst_ref=conv_out.at[2 * x + y], send_sem=c_ssem.at[k], recv_sem=c_rsem.at[k],
            device_id=peer, device_id_type=MESH) for k, peer in enumerate(peers)]
        for cp in fetched + small:
            cp.start()
        for landed, onward in zip(fetched, passed):
            landed.wait_recv()
            onward.start()
        for cp in passed:
            cp.wait_recv()
        for cp in fetched + passed:
            cp.wait_send()
        for cp in small:
            cp.wait()

    return pl.pallas_call(
        body, name="gather_weights",
        out_shape=tuple(jax.ShapeDtypeStruct((N_CHIPS,) + a.shape, a.dtype) for a in list(shards) + [conv_s]),
        in_specs=[ANY] * (n + 1), out_specs=(ANY,) * (n + 1),
        scratch_shapes=_sems(3 * n) + _sems(3 * n) + _sems(3),
    )(*shards, conv_s)


def halves_to_sibling(gs):
    n = len(gs)

    def body(*refs):
        ins, outs = refs[:n], refs[n:2 * n]
        ssem, rsem = refs[2 * n:]
        x, y, c = lax.axis_index("x"), lax.axis_index("y"), lax.axis_index("c")
        copies = []
        for i in range(n):
            for j in range(N_CHIPS):
                copies.append(pltpu.make_async_remote_copy(
                    src_ref=ins[i].at[j, _half(ins[i].shape[1], 1 - c)], dst_ref=outs[i].at[j],
                    send_sem=ssem.at[N_CHIPS * i + j], recv_sem=rsem.at[N_CHIPS * i + j],
                    device_id=(x, y, 1 - c), device_id_type=MESH))
        for cp in copies:
            cp.start()
        for cp in copies:
            cp.wait()

    return pl.pallas_call(
        body, name="halves_to_sibling",
        out_shape=tuple(jax.ShapeDtypeStruct((N_CHIPS, g.shape[1] // 2, g.shape[2]), F32) for g in gs),
        in_specs=[ANY] * n, out_specs=(ANY,) * n, scratch_shapes=_sems(N_CHIPS * n),
    )(*gs)


RED_GRID = 8


def add_halves(core, gs, rbs):
    n = len(gs)

    def body(c_ref, *refs):
        for i in range(n):
            refs[2 * n + i][...] = (refs[i][...] + refs[n + i][...]).astype(BF16)

    def blk(g):
        return (1, g.shape[1] // 2 // RED_GRID, g.shape[2])

    grid_spec = pltpu.PrefetchScalarGridSpec(
        num_scalar_prefetch=1, grid=(N_CHIPS, RED_GRID),
        in_specs=([pl.BlockSpec(blk(g), lambda j, b, c_ref: (j, c_ref[0] * RED_GRID + b, 0)) for g in gs]
                  + [pl.BlockSpec(blk(g), lambda j, b, c_ref: (j, b, 0)) for g in gs]),
        out_specs=[pl.BlockSpec(blk(g), lambda j, b, c_ref: (j, b, 0)) for g in gs])
    return pl.pallas_call(
        body, name="add_halves", grid_spec=grid_spec,
        out_shape=tuple(jax.ShapeDtypeStruct(r.shape, BF16) for r in rbs),
        compiler_params=_params(("parallel", "parallel")),
    )(core, *gs, *rbs)


def scatter_copies(ins, outs, ssem, rsem, lsem):
    n = len(ins)
    x, y, _, peers = _chip_peers()
    me = 2 * x + y
    copies = [pltpu.make_async_copy(ins[i].at[me], outs[i].at[me], lsem.at[i]) for i in range(n)]
    for k, peer in enumerate(peers):
        dst_chip = 2 * peer[0] + peer[1]
        for i in range(n):
            copies.append(pltpu.make_async_remote_copy(
                src_ref=ins[i].at[dst_chip], dst_ref=outs[i].at[me], send_sem=ssem.at[n * k + i],
                recv_sem=rsem.at[n * k + i], device_id=peer, device_id_type=MESH))
    return copies


def gather_small(small):
    def body(s_ref, smalls_ref, ssem, rsem, lsem):
        x, y, c = lax.axis_index("x"), lax.axis_index("y"), lax.axis_index("c")
        dev = 4 * x + 2 * y + c
        copies = [pltpu.make_async_copy(s_ref, smalls_ref.at[dev], lsem)]
        for k in range(1, N_DEV):
            fx, fy, fc = (k >> 2) & 1, (k >> 1) & 1, k & 1
            peer = ((1 - x) if fx else x, (1 - y) if fy else y, (1 - c) if fc else c)
            copies.append(pltpu.make_async_remote_copy(
                src_ref=s_ref, dst_ref=smalls_ref.at[dev], send_sem=ssem.at[k - 1], recv_sem=rsem.at[k - 1],
                device_id=peer, device_id_type=MESH))
        for cp in copies:
            cp.start()
        for cp in copies:
            cp.wait()

    return pl.pallas_call(
        body, name="gather_small",
        out_shape=jax.ShapeDtypeStruct((N_DEV,) + small.shape, F32),
        in_specs=[ANY], out_specs=ANY,
        scratch_shapes=_sems(N_DEV - 1) + [pltpu.SemaphoreType.DMA],
    )(small)


def sum_parts(parts):
    n = len(parts)

    def body(*refs):
        for i in range(n):
            p_ref = refs[i]
            refs[n + i][...] = ((p_ref[0].astype(F32) + p_ref[1].astype(F32)) + p_ref[2].astype(F32)
                                ) + p_ref[3].astype(F32)

    def rows(p):
        return p.shape[1] // RED_GRID

    return pl.pallas_call(
        body, name="sum_parts",
        out_shape=tuple(jax.ShapeDtypeStruct(p.shape[1:], F32) for p in parts),
        grid=(RED_GRID,),
        in_specs=[pl.BlockSpec((N_CHIPS, rows(p), p.shape[2]), lambda b: (0, b, 0)) for p in parts],
        out_specs=tuple(pl.BlockSpec((rows(p), p.shape[2]), lambda b: (b, 0)) for p in parts),
        compiler_params=_params(("parallel",)),
    )(*parts)


def swap_halves(reds):
    n = len(reds)

    def body(*refs):
        ins, outs = refs[:n], refs[n:2 * n]
        ssem, rsem = refs[2 * n:]
        x, y, c = lax.axis_index("x"), lax.axis_index("y"), lax.axis_index("c")
        copies = [pltpu.make_async_remote_copy(
            src_ref=ins[i], dst_ref=outs[i], send_sem=ssem.at[i], recv_sem=rsem.at[i],
            device_id=(x, y, 1 - c), device_id_type=MESH) for i in range(n)]
        for cp in copies:
            cp.start()
        for cp in copies:
            cp.wait()

    return pl.pallas_call(
        body, name="swap_halves",
        out_shape=tuple(jax.ShapeDtypeStruct(r.shape, F32) for r in reds),
        in_specs=[ANY] * n, out_specs=(ANY,) * n, scratch_shapes=_sems(n),
    )(*reds)


def _adamw(w, g, m, v):
    m = ADAM_B1 * m + (1.0 - ADAM_B1) * g
    v = ADAM_B2 * v + (1.0 - ADAM_B2) * (g * g)
    m_hat = m / (1.0 - ADAM_B1 ** ADAM_STEP)
    v_hat = v / (1.0 - ADAM_B2 ** ADAM_STEP)
    delta = -ADAM_LR * (m_hat / (jnp.sqrt(v_hat) + ADAM_EPS) + ADAM_WD * w)
    return delta, m, v


def adamw_big(core, mine, theirs, ws, ms, vs):
    n = len(ws)
    per_half = RED_GRID // 2

    def body(c_ref, *refs):
        own = (pl.program_id(0) // per_half) == c_ref[0]
        for i in range(n):
            g = jnp.where(own, refs[i][...], refs[n + i][...])
            d, mn, vn = _adamw(refs[2 * n + i][...], g, refs[3 * n + i][...], refs[4 * n + i][...])
            refs[5 * n + i][...] = g
            refs[6 * n + i][...] = d
            refs[7 * n + i][...] = mn
            refs[8 * n + i][...] = vn

    def blk(w):
        return (w.shape[0] // RED_GRID, w.shape[1])

    halves = [pl.BlockSpec(blk(w), lambda b, c_ref: (b % per_half, 0)) for w in ws]
    whole = [pl.BlockSpec(blk(w), lambda b, c_ref: (b, 0)) for w in ws]
    shapes = [jax.ShapeDtypeStruct(w.shape, F32) for w in ws]
    grid_spec = pltpu.PrefetchScalarGridSpec(
        num_scalar_prefetch=1, grid=(RED_GRID,), in_specs=halves * 2 + whole * 3, out_specs=whole * 4)
    outs = pl.pallas_call(
        body, name="adamw_big", out_shape=tuple(shapes * 4), grid_spec=grid_spec,
        compiler_params=_params(("parallel",)),
    )(core, *mine, *theirs, *ws, *ms, *vs)
    return outs[:n], outs[n:2 * n], outs[2 * n:3 * n], outs[3 * n:]


def adamw_whole(g, w, m, v, name):
    def body(g_ref, w_ref, m_ref, v_ref, d_out, m_out, v_out):
        d, mn, vn = _adamw(w_ref[...], g_ref[...], m_ref[...], v_ref[...])
        d_out[...] = d
        m_out[...] = mn
        v_out[...] = vn

    shp = jax.ShapeDtypeStruct(g.shape, F32)
    return pl.pallas_call(body, name=name, out_shape=(shp,) * 3)(g, w, m, v)


def adamw_small(smalls, w, m, v):
    def body(s_ref, w_ref, m_ref, v_ref, g_out, d_out, m_out, v_out):
        g = s_ref[0]
        for k in range(1, N_DEV):
            g = g + s_ref[k]
        d, mn, vn = _adamw(w_ref[...], g, m_ref[...], v_ref[...])
        g_out[...] = g
        d_out[...] = d
        m_out[...] = mn
        v_out[...] = vn

    shp = jax.ShapeDtypeStruct((SMALL_ROWS, LANES), F32)
    return pl.pallas_call(body, name="adamw_small", out_shape=(shp,) * 4)(smalls, w, m, v)


def in_proj_fwd(x, g, ws):
    s = x.shape[0]
    tm = _blk(s, 512)
    n = len(ws)

    def body(x_ref, g_ref, *refs):
        xv = x_ref[...]
        r = lax.rsqrt(jnp.mean(xv * xv, axis=-1, keepdims=True) + EPS)
        u = (xv * r * g_ref[...]).astype(BF16)
        refs[n][...] = u
        for i in range(n):
            refs[n + 1 + i][...] = _mm(u, refs[i][...])

    rows = lambda width: pl.BlockSpec((tm, width), lambda i: (i, 0))
    return pl.pallas_call(
        body, name="in_proj_fwd",
        out_shape=tuple([jax.ShapeDtypeStruct((s, D_MODEL), BF16)]
                        + [jax.ShapeDtypeStruct((s, w.shape[1]), F32) for w in ws]),
        grid=(s // tm,),
        in_specs=[rows(D_MODEL), _const_spec((1, D_MODEL))] + [_const_spec(w.shape) for w in ws],
        out_specs=tuple([rows(D_MODEL)] + [rows(w.shape[1]) for w in ws]),
        compiler_params=_params(("parallel",)),
    )(x, g, *ws)


def matmul_tn(a, b, name):
    s, m = a.shape
    n = b.shape[1]
    tk = _blk(s, 2048)
    tn = _blk(n, 512)

    def body(a_ref, b_ref, o_ref):
        @pl.when(pl.program_id(1) == 0)
        def _():
            o_ref[...] = jnp.zeros_like(o_ref)

        o_ref[...] += _mm_tn(a_ref[...], b_ref[...])

    return pl.pallas_call(
        body, name=name, out_shape=jax.ShapeDtypeStruct((m, n), F32), grid=(n // tn, s // tk),
        in_specs=[pl.BlockSpec((tk, m), lambda j, i: (i, 0)), pl.BlockSpec((tk, tn), lambda j, i: (i, j))],
        out_specs=pl.BlockSpec((m, tn), lambda j, i: (0, j)),
        compiler_params=_params(("parallel", "arbitrary")),
    )(a, b)


def conv_fwd(xbc, w, b):
    s = xbc.shape[0]
    tm = _blk(s, 256)

    def body(x_ref, t_ref, w_ref, b_ref, pre_ref, act_ref):
        i = pl.program_id(0)
        row8 = _iota((8, LANES), 0)
        for c0 in range(0, CONV_CH, LANES):
            cols = slice(c0, c0 + LANES)
            cur = x_ref[:, cols]
            tail = jnp.where(i > 0, t_ref[:, cols], 0.0)
            wv = w_ref[:, cols]
            bias = b_ref[:, cols]
            acc = cur * wv[3:4, :] + bias
            head = cur[0:8, :] * wv[3:4, :] + bias
            for sh in range(1, CONV_WIDTH):
                wk = wv[3 - sh:4 - sh, :]
                acc = acc + pltpu.roll(cur, sh, 0) * wk
                first = jnp.where(row8 < sh, pltpu.roll(tail, sh, 0), pltpu.roll(cur[0:8, :], sh, 0))
                head = head + first * wk
            pre_ref[:, cols] = acc
            act_ref[:, cols] = acc * _sigmoid(acc)
            pre_ref[0:8, cols] = head
            act_ref[0:8, cols] = head * _sigmoid(head)

    shp = jax.ShapeDtypeStruct(xbc.shape, F32)
    rows = pl.BlockSpec((tm, CONV_CH), lambda i: (i, 0))
    return pl.pallas_call(
        body, name="conv_fwd", out_shape=(shp, shp), grid=(s // tm,),
        in_specs=[rows, pl.BlockSpec((8, CONV_CH), lambda i: (jnp.maximum(i * (tm // 8) - 1, 0), 0)),
                  _const_spec((CONV_WIDTH, CONV_CH)), _const_spec((1, CONV_CH))],
        out_specs=(rows, rows), compiler_params=_params(("parallel",)),
    )(xbc, xbc, w, b)


def conv_bwd(xbc, pre, dact, w):
    s = xbc.shape[0]
    tm = _blk(s, 256)
    nb = s // tm

    def dsilu(p):
        sg = _sigmoid(p)
        return sg * (1.0 + p * (1.0 - sg))

    def body(x_ref, xt_ref, p_ref, pn_ref, d_ref, dn_ref, w_ref, dx_ref, dw_ref, db_ref):
        i = pl.program_id(0)

        @pl.when(i == 0)
        def _():
            dw_ref[...] = jnp.zeros_like(dw_ref)
            db_ref[...] = jnp.zeros_like(db_ref)

        row8 = _iota((8, LANES), 0)
        for c0 in range(0, CONV_CH, LANES):
            cols = slice(c0, c0 + LANES)
            wv = w_ref[:, cols]
            dpre = d_ref[:, cols] * dsilu(p_ref[:, cols])
            dnext = jnp.where(i < nb - 1, dn_ref[:, cols] * dsilu(pn_ref[:, cols]), 0.0)
            cur = x_ref[:, cols]
            tail = jnp.where(i > 0, xt_ref[:, cols], 0.0)
            dx = dpre * wv[3:4, :]
            last = dpre[tm - 8:tm, :] * wv[3:4, :]
            db_ref[:, cols] += jnp.sum(dpre, axis=0, keepdims=True)
            dws = [jnp.sum(dpre * cur, axis=0, keepdims=True)]
            for sh in range(1, CONV_WIDTH):
                wk = wv[3 - sh:4 - sh, :]
                dx = dx + pltpu.roll(dpre, tm - sh, 0) * wk
                nxt = jnp.where(row8 >= 8 - sh, pltpu.roll(dnext, 8 - sh, 0),
                                pltpu.roll(dpre[tm - 8:tm, :], 8 - sh, 0))
                last = last + nxt * wk
                xs = pltpu.roll(cur, sh, 0)
                first = jnp.where(row8 < sh, pltpu.roll(tail, sh, 0), xs[0:8, :])
                dws.append(jnp.sum(dpre * xs, axis=0, keepdims=True)
                           + jnp.sum(dpre[0:8, :] * (first - xs[0:8, :]), axis=0, keepdims=True))
            dx_ref[:, cols] = dx.astype(BF16)
            dx_ref[tm - 8:tm, cols] = last.astype(BF16)
            for sh in range(CONV_WIDTH):
                dw_ref[3 - sh:4 - sh, cols] += dws[sh]

    rows = pl.BlockSpec((tm, CONV_CH), lambda i: (i, 0))
    prev8 = pl.BlockSpec((8, CONV_CH), lambda i: (jnp.maximum(i * (tm // 8) - 1, 0), 0))
    next8 = pl.BlockSpec((8, CONV_CH), lambda i: (jnp.minimum((i + 1) * (tm // 8), s // 8 - 1), 0))
    return pl.pallas_call(
        body, name="conv_bwd",
        out_shape=(jax.ShapeDtypeStruct(xbc.shape, BF16), jax.ShapeDtypeStruct((8, CONV_CH), F32),
                   jax.ShapeDtypeStruct((1, CONV_CH), F32)),
        grid=(nb,),
        in_specs=[rows, prev8, rows, next8, rows, next8, _const_spec((CONV_WIDTH, CONV_CH))],
        out_specs=(rows, _const_spec((8, CONV_CH)), _const_spec((1, CONV_CH))),
        compiler_params=_params(("arbitrary",)),
    )(xbc, xbc, pre, pre, dact, dact, w)


def _pair_lanes(mat, j, lane):
    return jnp.where(lane < HEAD_DIM, mat[:, 2 * j:2 * j + 1], mat[:, 2 * j + 1:2 * j + 2])


def _ssd_chunk_prelude(sm, dtb, a_row, lane, sub):
    raw = sm + dtb
    head_lane = lane < N_HEADS
    dt = jnp.where(head_lane, _softplus(raw), 0.0)
    sig = jnp.where(head_lane, _sigmoid(raw), 0.0)
    tri = (lane <= sub).astype(F32)
    acs = _mm_exact(tri, dt * a_row)
    return dt, sig, acs, acs.T


GROUP_WIDTH = SSD_WIDTH // N_GROUPS
HEADS_PER_GROUP = N_HEADS // N_GROUPS


def _expand_group(mat, g, lane):
    return jnp.concatenate([_pair_lanes(mat, j, lane) for j in range(4 * g, 4 * g + 4)], axis=1)


def _head_sums(q, g):
    row = _iota((GROUP_WIDTH, LANES), 0)
    seg = (_iota((GROUP_WIDTH, LANES), 1) == HEADS_PER_GROUP * g + (row >> 6)).astype(BF16)
    hi = q.astype(BF16)
    lo = (q - hi.astype(F32)).astype(BF16)
    return _mm(hi, seg) + _mm(lo, seg)


def _rows_from_lanes(row512):
    return jnp.broadcast_to(row512, (LANES, GROUP_WIDTH)).T


def ssd_fwd(xc, small, dtb_row, a_row, dskip_lane):
    s = xc.shape[0]
    nc = s // CHUNK

    def body(xc_ref, sm_ref, dtb_ref, a_ref, dsk_ref, y_ref, hs_ref, h_scr):
        c = pl.program_id(0)

        @pl.when(c == 0)
        def _():
            h_scr[...] = jnp.zeros_like(h_scr)

        lane = _iota((CHUNK, LANES), 1)
        sub = _iota((CHUNK, LANES), 0)
        causal = lane <= sub
        dt, _, acs, acs_t = _ssd_chunk_prelude(sm_ref[...], dtb_ref[...], a_ref[...], lane, sub)
        for g in range(N_GROUPS):
            cols = slice(GROUP_WIDTH * g, GROUP_WIDTH * (g + 1))
            b_off = SSD_WIDTH + D_STATE * g
            c_off = SSD_WIDTH + N_GROUPS * D_STATE + D_STATE * g
            b_b = xc_ref[:, b_off:b_off + D_STATE].astype(BF16)
            c_b = xc_ref[:, c_off:c_off + D_STATE].astype(BF16)
            cb = _mm_nt(c_b, b_b)
            x_g = xc_ref[:, cols]
            acs_g = _expand_group(acs, g, lane)
            xdt_g = x_g * _expand_group(dt, g, lane)
            xdt_b = xdt_g.astype(BF16)
            heads = range(HEADS_PER_GROUP * g, HEADS_PER_GROUP * (g + 1))
            m_b = [(cb * jnp.exp(jnp.where(causal, acs[:, h:h + 1] - acs_t[h:h + 1, :], NEG_BIG))).astype(BF16)
                   for h in heads]
            yd = [_mm(m_b[k], xdt_b[:, LANES * (k // 2):LANES * (k // 2 + 1)]) for k in range(HEADS_PER_GROUP)]
            yd_g = jnp.concatenate([jnp.where(lane < HEAD_DIM, yd[2 * k], yd[2 * k + 1]) for k in range(4)], axis=1)
            h_g = h_scr[g]
            t_g = _mm_nt(c_b, h_g.astype(BF16))
            y_ref[:, cols] = yd_g + jnp.exp(acs_g) * t_g + dsk_ref[:, cols] * x_g
            hs_ref[0, g] = h_g
            last_g = acs_g[CHUNK - 1:CHUNK, :]
            w_b = (xdt_g * jnp.exp(last_g - acs_g)).astype(BF16)
            h_scr[g] = h_g * jnp.exp(_rows_from_lanes(last_g)) + _mm_tn(w_b, b_b)

    return pl.pallas_call(
        body, name="ssd_fwd",
        out_shape=(jax.ShapeDtypeStruct((s, SSD_WIDTH), F32),
                   jax.ShapeDtypeStruct((nc, N_GROUPS, GROUP_WIDTH, D_STATE), F32)),
        grid=(nc,),
        in_specs=[pl.BlockSpec((CHUNK, CONV_CH), lambda c: (c, 0)), pl.BlockSpec((CHUNK, LANES), lambda c: (c, 0)),
                  _const_spec((1, LANES)), _const_spec((1, LANES)), _const_spec((1, SSD_WIDTH))],
        out_specs=(pl.BlockSpec((CHUNK, SSD_WIDTH), lambda c: (c, 0)),
                   pl.BlockSpec((1, N_GROUPS, GROUP_WIDTH, D_STATE), lambda c: (c, 0, 0, 0))),
        scratch_shapes=[pltpu.VMEM((N_GROUPS, GROUP_WIDTH, D_STATE), F32)],
        compiler_params=_params(("arbitrary",)),
    )(xc, small, dtb_row, a_row, dskip_lane)


def ssd_bwd(xc, small, states, dy, dtb_row, a_row, dskip_lane):
    s = xc.shape[0]
    nc = s // CHUNK
    rev = lambda c: nc - 1 - c

    def body(xc_ref, sm_ref, hs_ref, dy_ref, dtb_ref, a_ref, dsk_ref,
             dxc_ref, ddt_ref, da_ref, ddtb_ref, ddsk_ref, dh_scr):
        c = pl.program_id(0)

        @pl.when(c == 0)
        def _():
            dh_scr[...] = jnp.zeros_like(dh_scr)
            da_ref[...] = jnp.zeros_like(da_ref)
            ddtb_ref[...] = jnp.zeros_like(ddtb_ref)
            ddsk_ref[...] = jnp.zeros_like(ddsk_ref)

        lane = _iota((CHUNK, LANES), 1)
        sub = _iota((CHUNK, LANES), 0)
        causal = lane <= sub
        upper = lane >= sub
        is_last = sub == CHUNK - 1
        a_row_v = a_ref[...]
        dt, sig, acs, acs_t = _ssd_chunk_prelude(sm_ref[...], dtb_ref[...], a_row_v, lane, sub)
        cd = jnp.exp(acs[CHUNK - 1:CHUNK, :])
        dacs_c = jnp.zeros((CHUNK, LANES), F32)
        dacs_r = jnp.zeros((LANES, CHUNK), F32)
        ddtx = jnp.zeros((CHUNK, LANES), F32)
        for g in range(N_GROUPS):
            cols = slice(GROUP_WIDTH * g, GROUP_WIDTH * (g + 1))
            b_off = SSD_WIDTH + D_STATE * g
            c_off = SSD_WIDTH + N_GROUPS * D_STATE + D_STATE * g
            b_b = xc_ref[:, b_off:b_off + D_STATE].astype(BF16)
            c_b = xc_ref[:, c_off:c_off + D_STATE].astype(BF16)
            cb = _mm_nt(c_b, b_b)
            cb_t = _mm_nt(b_b, c_b)
            x_g = xc_ref[:, cols]
            dy_g = dy_ref[:, cols]
            dt_g = _expand_group(dt, g, lane)
            acs_g = _expand_group(acs, g, lane)
            last_g = acs_g[CHUNK - 1:CHUNK, :]
            e_g = jnp.exp(acs_g)
            dte_g = jnp.exp(last_g - acs_g)
            xdt_g = x_g * dt_g
            xdt_b = xdt_g.astype(BF16)
            h_g = hs_ref[0, g]
            dh_g = dh_scr[g]
            h_b = h_g.astype(BF16)
            dh_b = dh_g.astype(BF16)
            heads = list(range(HEADS_PER_GROUP * g, HEADS_PER_GROUP * (g + 1)))
            segs = [acs[:, h:h + 1] - acs_t[h:h + 1, :] for h in heads]
            lms = [jnp.exp(jnp.where(causal, sg, NEG_BIG)) for sg in segs]
            mts = [(cb_t * jnp.exp(jnp.where(upper, -sg, NEG_BIG))).astype(BF16) for sg in segs]
            dyh = []
            for k in range(HEADS_PER_GROUP):
                blk = dy_g[:, LANES * (k // 2):LANES * (k // 2 + 1)]
                in_head = (lane < HEAD_DIM) if k % 2 == 0 else (lane >= HEAD_DIM)
                dyh.append(jnp.where(in_head, blk, 0.0).astype(BF16))
            dms = [_mm_nt(dyh[k], xdt_b[:, LANES * (k // 2):LANES * (k // 2 + 1)]) for k in range(HEADS_PER_GROUP)]
            dxs = [_mm(mts[k], dyh[k]) for k in range(HEADS_PER_GROUP)]
            dcb = jnp.zeros((CHUNK, CHUNK), F32)
            for k, h in enumerate(heads):
                gmat = dms[k] * (cb * lms[k])
                dacs_c = dacs_c + jnp.where(lane == h, jnp.sum(gmat, axis=1, keepdims=True), 0.0)
                dacs_r = dacs_r - jnp.where(sub == h, jnp.sum(gmat, axis=0, keepdims=True), 0.0)
                dcb = dcb + dms[k] * lms[k]
            dxdt_g = jnp.concatenate([dxs[2 * k] + dxs[2 * k + 1] for k in range(4)], axis=1)
            t_g = _mm_nt(c_b, h_b)
            dacs_c = dacs_c + _head_sums(dy_g * e_g * t_g, g)
            dt_b = (dy_g * e_g).astype(BF16)
            dc_acc = _mm(dt_b, h_b)
            dh_prev = _mm_tn(dt_b, c_b)
            dw_g = _mm_nt(b_b, dh_b)
            w_g = xdt_g * dte_g
            dxdt_g = dxdt_g + dw_g * dte_g
            db_acc = _mm(w_g.astype(BF16), dh_b)
            r2 = _head_sums(dw_g * w_g, g)
            dacs_c = dacs_c + jnp.where(is_last, jnp.sum(r2, axis=0, keepdims=True), 0.0) - r2
            q3 = jnp.sum(dh_g * h_g, axis=1, keepdims=True)
            for k, h in enumerate(heads):
                tot = jnp.sum(q3[HEAD_DIM * k:HEAD_DIM * (k + 1), :], keepdims=True) * cd[:, h:h + 1]
                dacs_c = dacs_c + jnp.where(is_last & (lane == h), tot, 0.0)
            dh_scr[g] = dh_prev + dh_g * jnp.exp(_rows_from_lanes(last_g))
            dxc_ref[:, cols] = dxdt_g * dt_g + dsk_ref[:, cols] * dy_g
            ddtx = ddtx + _head_sums(dxdt_g * x_g, g)
            ddsk_ref[:, cols] += jnp.sum(dy_g * x_g, axis=0, keepdims=True)
            dxc_ref[:, b_off:b_off + D_STATE] = db_acc + _mm(dcb.T.astype(BF16), c_b)
            dxc_ref[:, c_off:c_off + D_STATE] = dc_acc + _mm(dcb.astype(BF16), b_b)
        dacs = dacs_c + dacs_r.T
        dadt = _mm_exact((lane >= sub).astype(F32), dacs)
        ddt = dadt * a_row_v + ddtx
        ddt_raw = ddt * sig
        ddt_ref[...] = ddt_raw
        da_ref[...] += jnp.sum(dadt * dt, axis=0, keepdims=True)
        ddtb_ref[...] += jnp.sum(ddt_raw, axis=0, keepdims=True)

    return pl.pallas_call(
        body, name="ssd_bwd",
        out_shape=(jax.ShapeDtypeStruct((s, CONV_CH), F32), jax.ShapeDtypeStruct((s, LANES), F32),
                   jax.ShapeDtypeStruct((1, LANES), F32), jax.ShapeDtypeStruct((1, LANES), F32),
                   jax.ShapeDtypeStruct((1, SSD_WIDTH), F32)),
        grid=(nc,),
        in_specs=[pl.BlockSpec((CHUNK, CONV_CH), lambda c: (rev(c), 0)),
                  pl.BlockSpec((CHUNK, LANES), lambda c: (rev(c), 0)),
                  pl.BlockSpec((1, N_GROUPS, GROUP_WIDTH, D_STATE), lambda c: (rev(c), 0, 0, 0)),
                  pl.BlockSpec((CHUNK, SSD_WIDTH), lambda c: (rev(c), 0)),
                  _const_spec((1, LANES)), _const_spec((1, LANES)), _const_spec((1, SSD_WIDTH))],
        out_specs=(pl.BlockSpec((CHUNK, CONV_CH), lambda c: (rev(c), 0)),
                   pl.BlockSpec((CHUNK, LANES), lambda c: (rev(c), 0)),
                   _const_spec((1, LANES)), _const_spec((1, LANES)), _const_spec((1, SSD_WIDTH))),
        scratch_shapes=[pltpu.VMEM((N_GROUPS, GROUP_WIDTH, D_STATE), F32)],
        compiler_params=_params(("arbitrary",)),
    )(xc, small, states, dy, dtb_row, a_row, dskip_lane)


FORGET_BLOCK = 512


def forget_cumsum(small, fgb_row):
    s = small.shape[0]
    t = _blk(s, FORGET_BLOCK)
    nb = s // t

    def body(sm_ref, b_ref, cc_ref, carry):
        i = pl.program_id(0)

        @pl.when(i == 0)
        def _():
            carry[...] = jnp.zeros_like(carry)

        lane = _iota((t, LANES), 1)
        in_f = (lane >= N_HEADS) & (lane < 2 * N_HEADS)
        logf = jnp.where(in_f, -_softplus(-(sm_ref[...] + b_ref[...])), 0.0)
        tri = (_iota((t, t), 1) <= _iota((t, t), 0)).astype(F32)
        cum = _mm_exact(tri, logf) + carry[0:1, :]
        cc_ref[...] = cum
        carry[...] = jnp.broadcast_to(cum[t - 1:t, :], (8, LANES))

    return pl.pallas_call(
        body, name="forget_cumsum",
        out_shape=jax.ShapeDtypeStruct((s, LANES), F32),
        grid=(nb,),
        in_specs=[pl.BlockSpec((t, LANES), lambda i: (i, 0)), _const_spec((1, LANES))],
        out_specs=pl.BlockSpec((t, LANES), lambda i: (i, 0)),
        scratch_shapes=[pltpu.VMEM((8, LANES), F32)],
        compiler_params=_params(("arbitrary",)),
    )(small, fgb_row)


def forget_bwd(dc, small, ddt_raw, fgb_row):
    s = small.shape[0]
    t = _blk(s, FORGET_BLOCK)
    nb = s // t
    rev = lambda i: nb - 1 - i

    def body(dc_ref, sm_ref, ddt_ref, b_ref, ds_ref, dfb_ref, carry):
        i = pl.program_id(0)

        @pl.when(i == 0)
        def _():
            carry[...] = jnp.zeros_like(carry)
            dfb_ref[...] = jnp.zeros_like(dfb_ref)

        lane = _iota((t, LANES), 1)
        rows = dc_ref[...].T
        tri = (_iota((t, t), 1) <= _iota((t, t), 0)).astype(F32)
        rc = _mm_exact(rows, tri) + carry[:, 0:1]
        carry[...] = jnp.broadcast_to(rc[:, 0:1], (LANES, LANES))
        in_f = (lane >= N_HEADS) & (lane < 2 * N_HEADS)
        df = jnp.where(in_f, rc.T * _sigmoid(-(sm_ref[...] + b_ref[...])), 0.0)
        ds_ref[...] = (df + ddt_ref[...]).astype(BF16)
        dfb_ref[...] += jnp.sum(df, axis=0, keepdims=True)

    blk = pl.BlockSpec((t, LANES), lambda i: (rev(i), 0))
    return pl.pallas_call(
        body, name="forget_bwd",
        out_shape=(jax.ShapeDtypeStruct((s, LANES), BF16), jax.ShapeDtypeStruct((1, LANES), F32)),
        grid=(nb,),
        in_specs=[blk, blk, blk, _const_spec((1, LANES))],
        out_specs=(blk, _const_spec((1, LANES))),
        scratch_shapes=[pltpu.VMEM((LANES, LANES), F32)],
        compiler_params=_params(("arbitrary",)),
    )(dc, small, ddt_raw, fgb_row)


ATT_BLOCK = 1024
ATT_BLOCK_BWD = 512
ATT_BLOCK_BWD_Q = 512
ATT_SCALE = HEAD_DIM ** -0.5
AUG_A = HEAD_DIM
AUG_B = HEAD_DIM + 3


def _split3(c):
    hi = c.astype(BF16).astype(F32)
    r = c - hi
    mid = r.astype(BF16).astype(F32)
    return hi, mid, (r - mid).astype(BF16).astype(F32)


def _aug(lane, first, parts=None, value=1.0):
    if parts is None:
        return jnp.where((lane >= first) & (lane < first + 3), value, 0.0)
    return (jnp.where(lane == first, parts[0], 0.0) + jnp.where(lane == first + 1, parts[1], 0.0)
            + jnp.where(lane == first + 2, parts[2], 0.0))


def _pack_pair(a0, a1, lane):
    return jnp.where(lane < HEAD_DIM, a0, pltpu.roll(a1, HEAD_DIM, 1))


def proj_qkv_heads(u, w_q, w_k, w_v, cum, later):
    s = u.shape[0]
    tm = _blk(s, 256)
    nsteps = s // tm
    n_later = len(later)

    def body(u_ref, wq_ref, wk_ref, wv_ref, c_ref, *rest):
        later_in = rest[:n_later]
        qa_ref, ka_ref, va_ref, nrm_ref = rest[n_later:n_later + 4]
        later_out = rest[n_later + 4:2 * n_later + 4]
        sems = rest[2 * n_later + 4:]
        step = pl.program_id(0)

        @pl.when(step == 0)
        def _():
            for cp in gather_copies(later_in, later_out, *sems)[0]:
                cp.start()

        @pl.when(step == nsteps // 2)
        def _():
            for landed, onward in zip(*gather_copies(later_in, later_out, *sems)):
                landed.wait_recv()
                onward.start()

        @pl.when(step == nsteps - 1)
        def _():
            fetched, passed = gather_copies(later_in, later_out, *sems)
            for cp in passed:
                cp.wait_recv()
            for cp in fetched + passed:
                cp.wait_send()

        lane = _iota((tm, LANES), 1)
        lo = lane < HEAD_DIM
        uv = u_ref[...]
        qf = _mm(uv, wq_ref[...]) * ATT_SCALE
        kf = _mm(uv, wk_ref[...])
        vf = _mm(uv, wv_ref[...])
        cc = c_ref[...]
        ones_a = _aug(lane, AUG_A)
        ones_b = _aug(lane, AUG_B)
        sub8 = _iota((8, LANES), 0)
        nrm = jnp.zeros((8, LANES), F32)
        for h in range(N_HEADS):
            j, e = divmod(h, 2)

            def head(full):
                blk = full[:, LANES * j:LANES * (j + 1)]
                if e == 1:
                    blk = pltpu.roll(blk, HEAD_DIM, 1)
                return jnp.where(lo, blk, 0.0)

            parts = _split3(cc[:, N_HEADS + h:N_HEADS + h + 1])
            qh, kh = head(qf), head(kf)
            qa_ref[h] = (qh + _aug(lane, AUG_A, parts) + ones_b).astype(BF16)
            ka_ref[h] = (kh + ones_a - _aug(lane, AUG_B, parts)).astype(BF16)
            va_ref[h] = (head(vf) + ones_a).astype(BF16)
        seg = (_iota((ATT_WIDTH, LANES), 1) == (_iota((ATT_WIDTH, LANES), 0) >> 6)).astype(BF16)
        for r, val in enumerate((qf, kf)):
            sq = val * val
            hi = sq.astype(BF16)
            tot = _mm(hi, seg) + _mm((sq - hi.astype(F32)).astype(BF16), seg)
            nrm = nrm + jnp.where(sub8 == r, jnp.max(tot, axis=0, keepdims=True), 0.0)
        nrm_ref[0] = nrm

    shp = jax.ShapeDtypeStruct((N_HEADS, s, LANES), BF16)
    hspec = pl.BlockSpec((N_HEADS, tm, LANES), lambda i: (0, i, 0))
    wspec = _const_spec((D_MODEL, ATT_WIDTH))
    return pl.pallas_call(
        body, name="proj_qkv_heads",
        out_shape=tuple([shp, shp, shp, jax.ShapeDtypeStruct((nsteps, 8, LANES), F32)]
                        + [jax.ShapeDtypeStruct((N_CHIPS,) + a.shape, a.dtype) for a in later]),
        grid=(nsteps,),
        in_specs=[pl.BlockSpec((tm, D_MODEL), lambda i: (i, 0)), wspec, wspec, wspec,
                  pl.BlockSpec((tm, LANES), lambda i: (i, 0))] + [ANY] * n_later,
        out_specs=tuple([hspec, hspec, hspec, pl.BlockSpec((1, 8, LANES), lambda i: (i, 0, 0))]
                        + [ANY] * n_later),
        scratch_shapes=_sems(3 * n_later) + _sems(3 * n_later),
        compiler_params=_params(("arbitrary",)),
    )(u, w_q, w_k, w_v, cum, *later)


SKIP_BELOW = -110.0


def live_blocks(norms, cum, tq, tk):
    qn = jnp.sqrt(jnp.max(norms[:, 0, :N_HEADS], axis=0))
    kn = jnp.sqrt(jnp.max(norms[:, 1, :N_HEADS], axis=0))
    bound = 2.05 * qn * kn + 2.0
    c_first = cum[0::tq, N_HEADS:2 * N_HEADS]
    c_last = cum[tk - 1::tk, N_HEADS:2 * N_HEADS]
    nq, nk = c_first.shape[0], c_last.shape[0]
    top = bound[None, None, :] + c_first[:, None, :] - c_last[None, :, :]
    before = (jnp.arange(nk)[None, :] + 1) * tk <= jnp.arange(nq)[:, None] * tq
    dead = before[:, :, None] & ~(top >= SKIP_BELOW)
    first = jnp.sum(dead, axis=1).astype(jnp.int32).T
    last_q = jnp.sum(first[:, None, :] <= jnp.arange(nk)[None, :, None], axis=2).astype(jnp.int32) - 1
    return first, last_q


def attention_fwd(first, qa, ka, va):
    s = qa.shape[1]
    t = _blk(s, ATT_BLOCK)
    nq = s // t

    def body(first_ref, qa_ref, ka_ref, va_ref, o_ref, qb_ref, m_scr, acc_scr, alpha_scr, p_scr, s_scr):
        qi = pl.program_id(1)
        starts = [first_ref[2 * pl.program_id(0) + e, qi] for e in range(2)]
        k0 = jnp.maximum(starts[0], starts[1])
        m_scr[...] = jnp.full_like(m_scr, NEG_BIG)
        acc_scr[...] = jnp.zeros_like(acc_scr)

        def kv_rows(kb):
            return pl.ds(pl.multiple_of(kb * t, t), t)

        def logits(kb, masked, heads=(0, 1)):
            for e in heads:
                sc = _mm_nt(qa_ref[e], ka_ref[e, kv_rows(kb), :])
                if masked:
                    sc = jnp.where(_iota((t, t), 0) >= _iota((t, t), 1), sc, NEG_BIG)
                s_scr[e] = sc

        def probs(heads=(0, 1)):
            for e in heads:
                cmax = s_scr[e, :, 0:LANES]
                for c in range(1, t // LANES):
                    cmax = jnp.maximum(cmax, s_scr[e, :, LANES * c:LANES * (c + 1)])
                m_old = m_scr[e]
                m_new = jnp.maximum(m_old, jnp.max(cmax, axis=1, keepdims=True))
                alpha_scr[e] = jnp.exp(m_old - m_new)
                m_scr[e] = m_new
                for c in range(t // LANES):
                    cols = slice(LANES * c, LANES * (c + 1))
                    p_scr[e, :, cols] = jnp.exp(s_scr[e, :, cols] - m_new).astype(BF16)

        def accumulate(kb, heads=(0, 1)):
            for e in heads:
                acc_scr[e] = alpha_scr[e] * acc_scr[e] + _mm(p_scr[e], va_ref[e, kv_rows(kb), :])

        for e in range(2):
            def alone(kb, carry, e=e):
                logits(kb, False, (e,))
                probs((e,))
                accumulate(kb, (e,))
                return carry

            lax.fori_loop(starts[e], k0, alone, 0)

        def loop_body(kb, carry):
            logits(kb, False)
            for e in range(2):
                accumulate(kb - 1, (e,))
                probs((e,))
            return carry

        @pl.when(qi > k0)
        def _():
            logits(k0, False)
            probs()

        lax.fori_loop(k0 + 1, qi, loop_body, 0)

        @pl.when(qi > k0)
        def _():
            logits(qi, True)
            accumulate(qi - 1)
            probs()

        @pl.when(qi == k0)
        def _():
            logits(qi, True)
            probs()

        accumulate(qi)

        lane = _iota((t, LANES), 1)
        outs = []
        for e in range(2):
            acc = acc_scr[e]
            l = acc[:, AUG_A:AUG_A + 1]
            outs.append(acc / l)
            lse = m_scr[e][:, 0:1] + jnp.log(l)
            q32 = qa_ref[e].astype(F32)
            c = q32[:, AUG_A:AUG_A + 1] + q32[:, AUG_A + 1:AUG_A + 2] + q32[:, AUG_A + 2:AUG_A + 3]
            qb = jnp.where(lane < HEAD_DIM, q32, 0.0) + _aug(lane, AUG_A, _split3(c - lse)) + _aug(lane, AUG_B)
            qb_ref[e] = qb.astype(BF16)
        o_ref[...] = _pack_pair(outs[0], outs[1], lane)

    grid_spec = pltpu.PrefetchScalarGridSpec(
        num_scalar_prefetch=1, grid=(N_PAIRS, nq),
        in_specs=[pl.BlockSpec((2, t, LANES), lambda j, qi, f: (j, qi, 0)),
                  pl.BlockSpec((2, s, LANES), lambda j, qi, f: (j, 0, 0)),
                  pl.BlockSpec((2, s, LANES), lambda j, qi, f: (j, 0, 0))],
        out_specs=[pl.BlockSpec((t, LANES), lambda j, qi, f: (qi, j)),
                   pl.BlockSpec((2, t, LANES), lambda j, qi, f: (j, qi, 0))],
        scratch_shapes=[pltpu.VMEM((2, t, LANES), F32), pltpu.VMEM((2, t, LANES), F32),
                        pltpu.VMEM((2, t, LANES), F32), pltpu.VMEM((2, t, t), BF16), pltpu.VMEM((2, t, t), F32)])
    return pl.pallas_call(
        body, name="attention_fwd", grid_spec=grid_spec,
        out_shape=(jax.ShapeDtypeStruct((s, ATT_WIDTH), F32), jax.ShapeDtypeStruct((N_HEADS, s, LANES), BF16)),
        compiler_params=_params(("parallel", "parallel")),
    )(first, qa, ka, va)


def attention_bwd(last_q, qb, ka, va, dob):
    s = qb.shape[1]
    t = _blk(s, ATT_BLOCK_BWD)
    tq = _blk(s, ATT_BLOCK_BWD_Q)
    nq = s // tq
    per_q = tq // t

    def body(last_ref, qb_ref, dob_ref, ka_ref, va_ref, dq_ref, dk_ref, dv_ref, dc_ref, dq_scr, dk_scr, dv_scr):
        j, ki = pl.program_id(0), pl.program_id(1)

        @pl.when((j == 0) & (ki == 0))
        def _():
            dc_ref[...] = jnp.zeros_like(dc_ref)

        @pl.when(ki == 0)
        def _():
            dq_scr[...] = jnp.zeros_like(dq_scr)

        dk_scr[...] = jnp.zeros_like(dk_scr)
        dv_scr[...] = jnp.zeros_like(dv_scr)

        def q_step(qblk, masked, heads=(0, 1)):
            rows = pl.ds(pl.multiple_of(qblk * tq, tq), tq)
            scs = [_mm_nt(qb_ref[e, rows, :], ka_ref[e]) for e in heads]
            dps = [_mm_nt(dob_ref[e, rows, :], va_ref[e]) for e in heads]
            for e, sc, dp in zip(heads, scs, dps):
                q = qb_ref[e, rows, :]
                do = dob_ref[e, rows, :]
                if masked:
                    keep = (_iota((tq, t), 0) - _iota((tq, t), 1)) >= ki * t - qblk * tq
                    sc = jnp.where(keep, sc, NEG_BIG)
                p = jnp.exp(sc)
                ds_b = (p * dp).astype(BF16)
                dv_scr[e] += _mm_tn(p.astype(BF16), do)
                dk_scr[e] += _mm_tn(ds_b, q)
                dq_scr[e, rows, :] += _mm(ds_b, ka_ref[e])

        def loop_body(qblk, carry):
            q_step(qblk, False)
            return carry

        ends = [last_ref[2 * j + e, ki] + 1 for e in range(2)]
        both = jnp.minimum(ends[0], ends[1])
        diag = ki // per_q
        q_step(diag, True)
        lax.fori_loop(diag + 1, both, loop_body, 0)
        for e in range(2):
            def alone(qblk, carry, e=e):
                q_step(qblk, False, (e,))
                return carry

            lax.fori_loop(both, ends[e], alone, 0)

        lane = _iota((t, LANES), 1)
        dk_ref[...] = _pack_pair(dk_scr[0], dk_scr[1], lane).astype(BF16)
        dv_ref[...] = _pack_pair(dv_scr[0], dv_scr[1], lane).astype(BF16)
        rows = pl.ds(pl.multiple_of(ki * t, t), t)
        dc_ref[rows, :] -= (jnp.where(lane == N_HEADS + 2 * j, dk_scr[0][:, AUG_B:AUG_B + 1], 0.0)
                            + jnp.where(lane == N_HEADS + 2 * j + 1, dk_scr[1][:, AUG_B:AUG_B + 1], 0.0))

        @pl.when(ki == s // t - 1)
        def _():
            for blk in range(s // t):
                rws = pl.ds(blk * t, t)
                d0 = dq_scr[0, rws, :]
                d1 = dq_scr[1, rws, :]
                dq_ref[rws, :] = (_pack_pair(d0, d1, lane) * ATT_SCALE).astype(BF16)
                dc_ref[rws, :] += (jnp.where(lane == N_HEADS + 2 * j, d0[:, AUG_A:AUG_A + 1], 0.0)
                                   + jnp.where(lane == N_HEADS + 2 * j + 1, d1[:, AUG_A:AUG_A + 1], 0.0))

    full = pl.BlockSpec((2, s, LANES), lambda j, ki, f: (j, 0, 0))
    blk = pl.BlockSpec((2, t, LANES), lambda j, ki, f: (j, ki, 0))
    pair = pl.BlockSpec((t, LANES), lambda j, ki, f: (ki, j))
    wide = jax.ShapeDtypeStruct((s, ATT_WIDTH), BF16)
    grid_spec = pltpu.PrefetchScalarGridSpec(
        num_scalar_prefetch=1, grid=(N_PAIRS, s // t),
        in_specs=[full, full, blk, blk],
        out_specs=[pl.BlockSpec((s, LANES), lambda j, ki, f: (0, j)), pair, pair,
                   pl.BlockSpec((s, LANES), lambda j, ki, f: (0, 0))],
        scratch_shapes=[pltpu.VMEM((2, s, LANES), F32), pltpu.VMEM((2, t, LANES), F32),
                        pltpu.VMEM((2, t, LANES), F32)])
    return pl.pallas_call(
        body, name="attention_bwd", grid_spec=grid_spec,
        out_shape=(wide, wide, wide, jax.ShapeDtypeStruct((s, LANES), F32)),
        compiler_params=_params(("arbitrary", "arbitrary")),
    )(last_q, qb, dob, ka, va)


def _dsilu(z, sg):
    return sg * (1.0 + z * (1.0 - sg))


def post_mix(x, y, zs, o, za, p, tgt, ssd_g, att_g_lane, ple_g, fin_g, w_out, w_gate, w_proj):
    s = x.shape[0]
    tm = _blk(s, 256)
    half = SSD_WIDTH // N_GROUPS

    def rms_bwd(dy, yn, r):
        return r * (dy - yn * jnp.mean(dy * yn, axis=-1, keepdims=True))

    def colsum(a):
        return jnp.sum(a, axis=0, keepdims=True)

    def body(x_ref, y_ref, zs_ref, o_ref, za_ref, p_ref, t_ref, sg_ref, ag_ref, pg_ref, fg_ref,
             wo_ref, wg_ref, wp_ref,
             dh1_ref, dy_ref, dzs_ref, dob_ref, dza_ref, ycat_ref, dh1b_ref, n2b_ref, dglb_ref, dppb_ref, pb_ref,
             loss_ref, dfin_ref, dple_ref, dssd_ref, datt_ref):
        @pl.when(pl.program_id(0) == 0)
        def _():
            for r in (loss_ref, dfin_ref, dple_ref, dssd_ref, datt_ref):
                r[...] = jnp.zeros_like(r)

        lane = _iota((tm, LANES), 1)
        lo = lane < HEAD_DIM
        zs = zs_ref[...]
        sz = _sigmoid(zs)
        yv = y_ref[...]
        ys = yv * (zs * sz)
        yn, rg = [], []
        for g in range(N_GROUPS):
            seg = ys[:, half * g:half * (g + 1)]
            r = lax.rsqrt(jnp.mean(seg * seg, axis=-1, keepdims=True) + EPS)
            yn.append(seg * r)
            rg.append(r)
            ycat_ref[:, half * g:half * (g + 1)] = (yn[g] * sg_ref[:, half * g:half * (g + 1)]).astype(BF16)
        za = za_ref[...]
        sza = _sigmoid(za)
        silu_za = za * sza
        on, ra = [], []
        for jb in range(N_PAIRS):
            blk = o_ref[:, LANES * jb:LANES * (jb + 1)]
            sq = blk * blk
            ms0 = jnp.sum(jnp.where(lo, sq, 0.0), axis=1, keepdims=True) * (1.0 / HEAD_DIM)
            ms1 = jnp.sum(jnp.where(lo, 0.0, sq), axis=1, keepdims=True) * (1.0 / HEAD_DIM)
            r = jnp.where(lo, lax.rsqrt(ms0 + EPS), lax.rsqrt(ms1 + EPS))
            on.append(blk * r)
            ra.append(r)
            an = on[jb] * ag_ref[:, LANES * jb:LANES * (jb + 1)]
            ycat_ref[:, SSD_WIDTH + LANES * jb:SSD_WIDTH + LANES * (jb + 1)] = (
                an * silu_za[:, LANES * jb:LANES * (jb + 1)]).astype(BF16)
        h1 = x_ref[...] + _mm(ycat_ref[...], wo_ref[...])
        r2 = lax.rsqrt(jnp.mean(h1 * h1, axis=-1, keepdims=True) + EPS)
        n2h = h1 * r2
        n2_b = (n2h * pg_ref[...]).astype(BF16)
        gate = _sigmoid(_mm(n2_b, wg_ref[...]))
        p_b = p_ref[...].astype(BF16)
        pp = _mm(p_b, wp_ref[...])
        h2 = h1 + gate * pp
        r3 = lax.rsqrt(jnp.mean(h2 * h2, axis=-1, keepdims=True) + EPS)
        n3 = h2 * r3
        diff = n3 * fg_ref[...] - t_ref[...]
        sq = colsum(diff * diff)
        part = sq[:, 0:LANES]
        for jb in range(1, D_MODEL // LANES):
            part = part + sq[:, LANES * jb:LANES * (jb + 1)]
        loss_ref[...] += part * (0.5 / D_MODEL)
        dout = diff * (1.0 / D_MODEL)
        dfin_ref[...] += colsum(dout * n3)
        dh2 = rms_bwd(dout * fg_ref[...], n3, r3)
        dgl = dh2 * pp * gate * (1.0 - gate)
        dgl_b = dgl.astype(BF16)
        dn2 = _mm_nt(dgl_b, wg_ref[...])
        dple_ref[...] += colsum(dn2 * n2h)
        dh1 = dh2 + rms_bwd(dn2 * pg_ref[...], n2h, r2)
        dh1_b = dh1.astype(BF16)
        dycat = _mm_nt(dh1_b, wo_ref[...])
        dh1_ref[...] = dh1
        dh1b_ref[...] = dh1_b
        n2b_ref[...] = n2_b
        dglb_ref[...] = dgl_b
        dppb_ref[...] = (dh2 * gate).astype(BF16)
        pb_ref[...] = p_b
        for g in range(N_GROUPS):
            cols = slice(half * g, half * (g + 1))
            dys_g = dycat[:, cols]
            dssd_ref[:, cols] += colsum(dys_g * yn[g])
            dys = rms_bwd(dys_g * sg_ref[:, cols], yn[g], rg[g])
            dy_ref[:, cols] = dys * (zs[:, cols] * sz[:, cols])
            dzs_ref[:, cols] = (dys * yv[:, cols] * _dsilu(zs[:, cols], sz[:, cols])).astype(BF16)
        for jb in range(N_PAIRS):
            cols = slice(LANES * jb, LANES * (jb + 1))
            dya = dycat[:, SSD_WIDTH + LANES * jb:SSD_WIDTH + LANES * (jb + 1)]
            ag = ag_ref[:, cols]
            dan = dya * silu_za[:, cols]
            dza_ref[:, cols] = (dya * (on[jb] * ag) * _dsilu(za[:, cols], sza[:, cols])).astype(BF16)
            datt_ref[:, cols] += colsum(dan * on[jb])
            don = dan * ag
            q = don * on[jb]
            m0 = jnp.sum(jnp.where(lo, q, 0.0), axis=1, keepdims=True) * (1.0 / HEAD_DIM)
            m1 = jnp.sum(jnp.where(lo, 0.0, q), axis=1, keepdims=True) * (1.0 / HEAD_DIM)
            do2 = ra[jb] * (don - on[jb] * jnp.where(lo, m0, m1))
            prod = do2 * o_ref[:, cols]
            for e in range(2):
                delta = jnp.sum(jnp.where(lo, prod, 0.0) if e == 0 else jnp.where(lo, 0.0, prod),
                                axis=1, keepdims=True)
                base = jnp.where(lo, do2 if e == 0 else pltpu.roll(do2, HEAD_DIM, 1), 0.0)
                dob_ref[2 * jb + e] = (base - _aug(lane, AUG_A, _split3(delta))).astype(BF16)

    def rows(n, dtype=None):
        return pl.BlockSpec((tm, n), lambda i: (i, 0))

    def out(n, dtype):
        return jax.ShapeDtypeStruct((s, n), dtype)

    vec = _const_spec((1, D_MODEL))
    vshape = jax.ShapeDtypeStruct((1, D_MODEL), F32)
    return pl.pallas_call(
        body, name="post_mix",
        out_shape=(out(D_MODEL, F32), out(SSD_WIDTH, F32), out(SSD_WIDTH, BF16),
                   jax.ShapeDtypeStruct((N_HEADS, s, LANES), BF16),
                   out(ATT_WIDTH, BF16), out(D_INNER, BF16), out(D_MODEL, BF16), out(D_MODEL, BF16),
                   out(D_MODEL, BF16), out(D_MODEL, BF16), out(PLE_DIM, BF16),
                   jax.ShapeDtypeStruct((1, LANES), F32), vshape, vshape, vshape, vshape),
        grid=(s // tm,),
        in_specs=[rows(D_MODEL), rows(SSD_WIDTH), rows(SSD_WIDTH), rows(ATT_WIDTH), rows(ATT_WIDTH),
                  rows(PLE_DIM), rows(D_MODEL), vec, vec, vec, vec,
                  _const_spec((D_INNER, D_MODEL)), _const_spec((D_MODEL, D_MODEL)), _const_spec((PLE_DIM, D_MODEL))],
        out_specs=(rows(D_MODEL), rows(SSD_WIDTH), rows(SSD_WIDTH),
                   pl.BlockSpec((N_HEADS, tm, LANES), lambda i: (0, i, 0)), rows(ATT_WIDTH),
                   rows(D_INNER), rows(D_MODEL), rows(D_MODEL), rows(D_MODEL), rows(D_MODEL), rows(PLE_DIM),
                   _const_spec((1, LANES)), vec, vec, vec, vec),
        compiler_params=_params(("arbitrary",)),
    )(x, y, zs, o, za, p, tgt, ssd_g, att_g_lane, ple_g, fin_g, w_out, w_gate, w_proj)


def in_proj_bwd(dsegs, wsegs, x, g, dh1, pres):
    s = x.shape[0]
    tm = _blk(s, 256)
    nseg = len(dsegs)
    nbig = len(pres)
    nsteps = s // tm

    def body(*refs):
        d_refs = refs[:nseg]
        w_refs = refs[nseg:2 * nseg]
        x_ref, g_ref, dh1_ref = refs[2 * nseg:2 * nseg + 3]
        rest = refs[2 * nseg + 3:]
        pre_refs, (dx_ref, dg_ref), part_refs = rest[:nbig], rest[nbig:nbig + 2], rest[nbig + 2:2 * nbig + 2]
        ssem, rsem, lsem = rest[2 * nbig + 2:]

        @pl.when(pl.program_id(0) == 0)
        def _():
            dg_ref[...] = jnp.zeros_like(dg_ref)
            for cp in scatter_copies(pre_refs, part_refs, ssem, rsem, lsem):
                cp.start()

        @pl.when(pl.program_id(0) == nsteps - 1)
        def _():
            for cp in scatter_copies(pre_refs, part_refs, ssem, rsem, lsem):
                cp.wait()

        du = _mm_nt(d_refs[0][...], w_refs[0][...])
        for k in range(1, nseg):
            du = du + _mm_nt(d_refs[k][...], w_refs[k][...])
        xv = x_ref[...]
        r = lax.rsqrt(jnp.mean(xv * xv, axis=-1, keepdims=True) + EPS)
        xh = xv * r
        dg_ref[...] += jnp.sum(du * xh, axis=0, keepdims=True)
        dxh = du * g_ref[...]
        dx_ref[...] = r * (dxh - xh * jnp.mean(dxh * xh, axis=-1, keepdims=True)) + dh1_ref[...]

    rows = lambda n: pl.BlockSpec((tm, n), lambda i: (i, 0))
    return pl.pallas_call(
        body, name="in_proj_bwd",
        out_shape=tuple([jax.ShapeDtypeStruct((s, D_MODEL), F32), jax.ShapeDtypeStruct((1, D_MODEL), F32)]
                        + [jax.ShapeDtypeStruct(a.shape, a.dtype) for a in pres]),
        grid=(nsteps,),
        in_specs=([rows(d.shape[1]) for d in dsegs] + [_const_spec(w.shape) for w in wsegs]
                  + [rows(D_MODEL), _const_spec((1, D_MODEL)), rows(D_MODEL)] + [ANY] * nbig),
        out_specs=tuple([rows(D_MODEL), _const_spec((1, D_MODEL))] + [ANY] * nbig),
        scratch_shapes=_sems(3 * nbig) + [pltpu.SemaphoreType.DMA((nbig,))],
        compiler_params=_params(("arbitrary",)),
    )(*dsegs, *wsegs, x, g, dh1, *pres)


SMALL_NAMES = ("norm_g", "conv_b", "dt_bias", "a_log", "d_skip", "ssd_norm_g", "fg_bias", "att_norm_g",
               "ple_norm_g", "final_norm_g")
SMALL_SIZES = (1024, 1536, 16, 16, 16, 1024, 16, 64, 1024, 1024)
CONV_W_SIZE = CONV_WIDTH * CONV_CH


def _pack_small(vals):
    flat = jnp.concatenate([v.reshape(-1).astype(F32) for v in vals])
    flat = jnp.pad(flat, (0, SMALL_ROWS * LANES - flat.shape[0]))
    return flat.reshape(SMALL_ROWS, LANES)


def _unpack_small(pack, shapes):
    flat = pack.reshape(-1)
    out, off = [], 0
    for n, shp in zip(SMALL_SIZES, shapes):
        out.append(flat[off:off + n].reshape(shp))
        off += n
    return out


def _row128(v16, offset=0):
    return jnp.pad(v16.reshape(1, N_HEADS).astype(F32), ((0, 0), (offset, LANES - N_HEADS - offset)))


def local_step(prereduce, later, join_later, x, p, tgt, w_in, conv_w, norm_g, conv_b, dt_bias, a_log, d_skip,
               ssd_norm_g, fg_bias, att_norm_g, ple_norm_g, final_norm_g):
    widths = (SSD_WIDTH, CONV_CH, N_HEADS, ATT_WIDTH, ATT_WIDTH, ATT_WIDTH, ATT_WIDTH)
    c0, c1, c2, c3, c4, c5, c6, c7 = [sum(widths[:i]) for i in range(len(widths) + 1)]
    w_zs, w_xbc, w_dt = w_in[:, c0:c1], w_in[:, c1:c2], w_in[:, c2:c3]
    w_za, w_q, w_k, w_v, w_f = w_in[:, c3:c4], w_in[:, c4:c5], w_in[:, c5:c6], w_in[:, c6:c7], w_in[:, c7:]
    w_small = jnp.concatenate([w_dt, w_f, jnp.zeros((D_MODEL, LANES - 2 * N_HEADS), BF16)], axis=1)

    dtb_row = _row128(dt_bias)
    a_row = _row128(-jnp.exp(a_log.astype(F32)))
    fgb_row = _row128(fg_bias, N_HEADS)
    dskip_lane = jnp.repeat(d_skip.astype(F32), HEAD_DIM).reshape(1, SSD_WIDTH)
    att_g_lane = jnp.tile(att_norm_g.astype(F32), N_HEADS).reshape(1, ATT_WIDTH)
    row = lambda v: v.reshape(1, -1).astype(F32)

    u, zs, xbc, za, small = in_proj_fwd(x, row(norm_g), [w_zs, w_xbc, w_za, w_small])
    cum = forget_cumsum(small, fgb_row)
    qa, ka, va, norms, *gathered = proj_qkv_heads(u, w_q, w_k, w_v, cum, later)
    w_out, w_gate, w_proj = join_later(gathered)
    n_seq = x.shape[0]
    first, _ = live_blocks(norms, cum, _blk(n_seq, ATT_BLOCK), _blk(n_seq, ATT_BLOCK))
    _, last_q = live_blocks(norms, cum, _blk(n_seq, ATT_BLOCK_BWD_Q), _blk(n_seq, ATT_BLOCK_BWD))
    pre, xc = conv_fwd(xbc, conv_w, row(conv_b))
    y, states = ssd_fwd(xc, small, dtb_row, a_row, dskip_lane)
    o, qb = attention_fwd(first, qa, ka, va)
    (dh1, dy, dzs, dob, dza, ycat, dh1_b, n2_b, dgl_b, dpp_b, p_b,
     loss_l, dfin, dple, dssd_g, datt_lane) = post_mix(
        x, y, zs, o, za, p, tgt, row(ssd_norm_g), att_g_lane, row(ple_norm_g), row(final_norm_g),
        w_out, w_gate, w_proj)
    dq, dk, dv, dc = attention_bwd(last_q, qb, ka, va, dob)
    dxc, ddt_raw, da, ddtb, ddsk_lane = ssd_bwd(xc, small, states, dy, dtb_row, a_row, dskip_lane)
    dsmall, dfgb = forget_bwd(dc, small, ddt_raw, fgb_row)
    dxbc, dconv_w8, dconv_b = conv_bwd(xbc, pre, dxc, conv_w)
    dsegs = [dzs, dxbc, dza, dq, dk, dv, dsmall]
    wsegs = [w_zs, w_xbc, w_za, w_q, w_k, w_v, w_small]
    dws = [matmul_tn(u, d, "dw_in_%d" % i) for i, d in enumerate(dsegs)]
    dw_in = jnp.concatenate([dws[0], dws[1], dws[6][:, :N_HEADS], dws[2], dws[3], dws[4], dws[5],
                             dws[6][:, N_HEADS:2 * N_HEADS]], axis=1)
    dw_out = matmul_tn(ycat, dh1_b, "dw_out")
    dw_gate = matmul_tn(n2_b, dgl_b, "dw_gate")
    dw_proj = matmul_tn(p_b, dpp_b, "dw_proj")
    dx, dnorm_g, *parts = in_proj_bwd(dsegs, wsegs, x, row(norm_g), dh1, prereduce(dw_in, dw_out, dw_gate, dw_proj))
    small_grads = [
        dnorm_g, dconv_b, ddtb[0, :N_HEADS], (da * a_row)[0, :N_HEADS],
        ddsk_lane.reshape(N_HEADS, HEAD_DIM).sum(axis=1), dssd_g, dfgb[0, N_HEADS:2 * N_HEADS],
        datt_lane.reshape(N_HEADS, HEAD_DIM).sum(axis=0), dple, dfin]
    loss = jnp.sum(loss_l)
    return loss, dx, parts, dconv_w8[:CONV_WIDTH], small_grads


def kernel(x, p, norm_g, w_in, conv_w, conv_b, dt_bias, a_log, d_skip, ssd_norm_g, fg_bias, att_norm_g, w_out, ple_norm_g, w_ple_gate, w_ple_proj, final_norm_g, loss_target, m_norm_g, m_w_in, m_conv_w, m_conv_b, m_dt_bias, m_a_log, m_d_skip, m_ssd_norm_g, m_fg_bias, m_att_norm_g, m_w_out, m_ple_norm_g, m_w_ple_gate, m_w_ple_proj, m_final_norm_g, v_norm_g, v_w_in, v_conv_w, v_conv_b, v_dt_bias, v_a_log, v_d_skip, v_ssd_norm_g, v_fg_bias, v_att_norm_g, v_w_out, v_ple_norm_g, v_w_ple_gate, v_w_ple_proj, v_final_norm_g):
    chip = 2 * lax.axis_index("x") + lax.axis_index("y")
    core = lax.axis_index("c")

    big_w = [w_in[0], w_out[0], w_ple_gate[0], w_ple_proj[0]]
    own = [a.astype(BF16) for a in big_w] + [conv_w[0]]

    def joined(mine, gathered, axis):
        return jnp.concatenate([jnp.where(chip == j, mine, gathered[j]) for j in range(N_CHIPS)], axis=axis)

    w_in_all, conv_all = gather_weights(own[:1], own[4])
    w_in_f, conv_w_f = joined(own[0], w_in_all, 1), joined(own[4], conv_all, 1)

    def join_later(gathered):
        return [joined(mine, got, axis) for mine, got, axis in zip(own[1:4], gathered, (0, 0, 1))]

    core1 = core.reshape(1).astype(jnp.int32)

    def prereduce(dw_in, dw_out, dw_gate, dw_proj):
        n_in, n_proj = w_in.shape[2], w_ple_proj.shape[2]
        gs = [jnp.stack([dw_in[:, n_in * j:n_in * (j + 1)] for j in range(N_CHIPS)]),
              dw_out.reshape(N_CHIPS, w_out.shape[1], D_MODEL), dw_gate.reshape(N_CHIPS, w_ple_gate.shape[1], D_MODEL),
              jnp.stack([dw_proj[:, n_proj * j:n_proj * (j + 1)] for j in range(N_CHIPS)])]
        return add_halves(core1, gs, halves_to_sibling(gs))

    smalls_w = [norm_g, conv_b, dt_bias, a_log, d_skip, ssd_norm_g, fg_bias, att_norm_g, ple_norm_g, final_norm_g]
    loss_l, dx, parts, dconv_w, small_grads = local_step(
        prereduce, own[1:4], join_later, x[0], p[0, 0], loss_target[0], w_in_f, conv_w_f,
        *[a.reshape(-1) for a in smalls_w])
    loss = lax.psum(loss_l, ("x", "y", "c"))
    smalls = gather_small(_pack_small(list(small_grads) + [dconv_w]))
    mine = sum_parts(parts)

    g_big, d_big, m_big, v_big = adamw_big(
        core1, mine, swap_halves(mine), big_w, [m_w_in[0], m_w_out[0], m_w_ple_gate[0], m_w_ple_proj[0]],
        [v_w_in[0], v_w_out[0], v_w_ple_gate[0], v_w_ple_proj[0]])
    smalls_m = [m_norm_g, m_conv_b, m_dt_bias, m_a_log, m_d_skip, m_ssd_norm_g, m_fg_bias, m_att_norm_g,
                m_ple_norm_g, m_final_norm_g]
    smalls_v = [v_norm_g, v_conv_b, v_dt_bias, v_a_log, v_d_skip, v_ssd_norm_g, v_fg_bias, v_att_norm_g,
                v_ple_norm_g, v_final_norm_g]
    g_sm, d_sm, m_sm, v_sm = adamw_small(smalls, _pack_small(smalls_w), _pack_small(smalls_m), _pack_small(smalls_v))
    n_small = sum(SMALL_SIZES)
    g_conv_full = g_sm.reshape(-1)[n_small:n_small + CONV_W_SIZE].reshape(CONV_WIDTH, CONV_CH)
    n_conv = conv_w.shape[2]
    g_conv = lax.dynamic_slice_in_dim(g_conv_full, chip * n_conv, n_conv, axis=1)
    d_conv, m_conv, v_conv = adamw_whole(g_conv, conv_w[0], m_conv_w[0], v_conv_w[0], "adamw_conv")

    shapes = [a.shape for a in smalls_w]
    outs = []
    for big, conv, sm in ((g_big, g_conv, g_sm), (d_big, d_conv, d_sm), (m_big, m_conv, m_sm), (v_big, v_conv, v_sm)):
        b_in, b_out, b_gate, b_proj = [a[None] for a in big]
        s_norm, s_convb, s_dtb, s_alog, s_dsk, s_ssdg, s_fgb, s_attg, s_pleg, s_fin = _unpack_small(sm, shapes)
        outs.extend([s_norm, b_in, conv[None], s_convb, s_dtb, s_alog, s_dsk, s_ssdg, s_fgb, s_attg, b_out, s_pleg,
                     b_gate, b_proj, s_fin])
    return (loss, dx[None], *outs)
```

```python
import functools

import jax
import jax.numpy as jnp
from jax import lax
from jax.experimental import pallas as pl
from jax.experimental.pallas import tpu as pltpu

F32 = jnp.float32
BF16 = jnp.bfloat16

D_MODEL = 1024
SSD_WIDTH = 1024
ATT_WIDTH = 1024
N_HEADS = 16
HEAD_DIM = 64
N_GROUPS = 2
D_STATE = 128
CONV_CH = 1536
CONV_WIDTH = 4
CHUNK = 128
PLE_DIM = 256
D_INNER = 2048
EPS = 1e-6
IN_COLS = 6688
N_CHIPS = 4
N_DEV = 8
LANES = 128
N_PAIRS = 8

ADAM_LR = 0.001
ADAM_B1 = 0.9
ADAM_B2 = 0.999
ADAM_EPS = 1e-08
ADAM_WD = 0.01
ADAM_STEP = 10

SMALL_ROWS = 96

NEG_BIG = -1e30
VMEM_LIMIT = 56 * 1024 * 1024

MESH = pl.DeviceIdType.MESH
ANY = pl.BlockSpec(memory_space=pl.ANY)


def _mm(a, b):
    return jnp.dot(a, b, preferred_element_type=F32)


def _mm_nt(a, b):
    return lax.dot_general(a, b, (((1,), (1,)), ((), ())), preferred_element_type=F32)


def _mm_tn(a, b):
    return lax.dot_general(a, b, (((0,), (0,)), ((), ())), preferred_element_type=F32)


def _mm_exact(a, b):
    return jnp.dot(a, b, preferred_element_type=F32, precision=lax.Precision.HIGHEST)


def _softplus(x):
    return jnp.maximum(x, 0.0) + jnp.log1p(jnp.exp(-jnp.abs(x)))


def _sigmoid(x):
    return jax.nn.sigmoid(x)


def _iota(shape, dim):
    return lax.broadcasted_iota(jnp.int32, shape, dim)


def _params(sem=None):
    return pltpu.CompilerParams(dimension_semantics=sem, vmem_limit_bytes=VMEM_LIMIT)


def _blk(n, pref):
    return min(n, pref)


def _const_spec(shape):
    nd = len(shape)
    return pl.BlockSpec(shape, lambda *_: (0,) * nd)


def _chip_peers():
    x, y, c = lax.axis_index("x"), lax.axis_index("y"), lax.axis_index("c")
    return x, y, c, [(1 - x, y, c), (x, 1 - y, c), (1 - x, 1 - y, c)]


def _half(rows, c):
    h = rows // 2
    return pl.ds(pl.multiple_of(c * h, 8), h)


def _sems(n):
    return [pltpu.SemaphoreType.DMA((n,)), pltpu.SemaphoreType.DMA((n,))]


def gather_copies(ins, outs, ssem1, rsem1, ssem2, rsem2):
    n = len(ins)
    x, y, c, peers = _chip_peers()
    me = 2 * x + y
    fetched, passed = [], []
    for k, peer in enumerate(peers):
        chip = 2 * peer[0] + peer[1]
        for i in range(n):
            h = _half(ins[i].shape[0], c)
            fetched.append(pltpu.make_async_remote_copy(
                src_ref=ins[i].at[h], dst_ref=outs[i].at[me, h], send_sem=ssem1.at[n * k + i],
                recv_sem=rsem1.at[n * k + i], device_id=peer, device_id_type=MESH))
            passed.append(pltpu.make_async_remote_copy(
                src_ref=outs[i].at[chip, h], dst_ref=outs[i].at[chip, h], send_sem=ssem2.at[n * k + i],
                recv_sem=rsem2.at[n * k + i], device_id=(x, y, 1 - c), device_id_type=MESH))
    return fetched, passed


def gather_weights(shards, conv_s):
    n = len(shards)

    def body(*refs):
        ins, conv_in = refs[:n], refs[n]
        outs, conv_out = refs[n + 1:2 * n + 1], refs[2 * n + 1]
        ssem1, rsem1, ssem2, rsem2, c_ssem, c_rsem = refs[2 * n + 2:]
        x, y, _, peers = _chip_peers()
        fetched, passed = gather_copies(ins, outs, ssem1, rsem1, ssem2, rsem2)
        small = [pltpu.make_async_remote_copy(
            src_ref=conv_in, dst_ref=conv_out.at[2 * x + y], send_sem=c_ssem.at[k], recv_sem=c_rsem.at[k],
            device_id=peer, device_id_type=MESH) for k, peer in enumerate(peers)]
        for cp in fetched + small:
            cp.start()
        for landed, onward in zip(fetched, passed):
            landed.wait_recv()
            onward.start()
        for cp in passed:
            cp.wait_recv()
        for cp in fetched + passed:
            cp.wait_send()
        for cp in small:
            cp.wait()

    return pl.pallas_call(
        body, name="gather_weights",
        out_shape=tuple(jax.ShapeDtypeStruct((N_CHIPS,) + a.shape, a.dtype) for a in list(shards) + [conv_s]),
        in_specs=[ANY] * (n + 1), out_specs=(ANY,) * (n + 1),
        scratch_shapes=_sems(3 * n) + _sems(3 * n) + _sems(3),
    )(*shards, conv_s)


def halves_to_sibling(gs):
    n = len(gs)

    def body(*refs):
        ins, outs = refs[:n], refs[n:2 * n]
        ssem, rsem = refs[2 * n:]
        x, y, c = lax.axis_index("x"), lax.axis_index("y"), lax.axis_index("c")
        copies = []
        for i in range(n):
            for j in range(N_CHIPS):
                copies.append(pltpu.make_async_remote_copy(
                    src_ref=ins[i].at[j, _half(ins[i].shape[1], 1 - c)], dst_ref=outs[i].at[j],
                    send_sem=ssem.at[N_CHIPS * i + j], recv_sem=rsem.at[N_CHIPS * i + j],
                    device_id=(x, y, 1 - c), device_id_type=MESH))
        for cp in copies:
            cp.start()
        for cp in copies:
            cp.wait()

    return pl.pallas_call(
        body, name="halves_to_sibling",
        out_shape=tuple(jax.ShapeDtypeStruct((N_CHIPS, g.shape[1] // 2, g.shape[2]), F32) for g in gs),
        in_specs=[ANY] * n, out_specs=(ANY,) * n, scratch_shapes=_sems(N_CHIPS * n),
    )(*gs)


RED_GRID = 8


def add_halves(core, gs, rbs):
    n = len(gs)

    def body(c_ref, *refs):
        for i in range(n):
            refs[2 * n + i][...] = (refs[i][...] + refs[n + i][...]).astype(BF16)

    def blk(g):
        return (1, g.shape[1] // 2 // RED_GRID, g.shape[2])

    grid_spec = pltpu.PrefetchScalarGridSpec(
        num_scalar_prefetch=1, grid=(N_CHIPS, RED_GRID),
        in_specs=([pl.BlockSpec(blk(g), lambda j, b, c_ref: (j, c_ref[0] * RED_GRID + b, 0)) for g in gs]
                  + [pl.BlockSpec(blk(g), lambda j, b, c_ref: (j, b, 0)) for g in gs]),
        out_specs=[pl.BlockSpec(blk(g), lambda j, b, c_ref: (j, b, 0)) for g in gs])
    return pl.pallas_call(
        body, name="add_halves", grid_spec=grid_spec,
        out_shape=tuple(jax.ShapeDtypeStruct(r.shape, BF16) for r in rbs),
        compiler_params=_params(("parallel", "parallel")),
    )(core, *gs, *rbs)


def scatter_copies(ins, outs, ssem, rsem, lsem):
    n = len(ins)
    x, y, _, peers = _chip_peers()
    me = 2 * x + y
    copies = [pltpu.make_async_copy(ins[i].at[me], outs[i].at[me], lsem.at[i]) for i in range(n)]
    for k, peer in enumerate(peers):
        dst_chip = 2 * peer[0] + peer[1]
        for i in range(n):
            copies.append(pltpu.make_async_remote_copy(
                src_ref=ins[i].at[dst_chip], dst_ref=outs[i].at[me], send_sem=ssem.at[n * k + i],
                recv_sem=rsem.at[n * k + i], device_id=peer, device_id_type=MESH))
    return copies


def gather_small(small):
    def body(s_ref, smalls_ref, ssem, rsem, lsem):
        x, y, c = lax.axis_index("x"), lax.axis_index("y"), lax.axis_index("c")
        dev = 4 * x + 2 * y + c
        copies = [pltpu.make_async_copy(s_ref, smalls_ref.at[dev], lsem)]
        for k in range(1, N_DEV):
            fx, fy, fc = (k >> 2) & 1, (k >> 1) & 1, k & 1
            peer = ((1 - x) if fx else x, (1 - y) if fy else y, (1 - c) if fc else c)
            copies.append(pltpu.make_async_remote_copy(
                src_ref=s_ref, dst_ref=smalls_ref.at[dev], send_sem=ssem.at[k - 1], recv_sem=rsem.at[k - 1],
                device_id=peer, device_id_type=MESH))
        for cp in copies:
            cp.start()
        for cp in copies:
            cp.wait()

    return pl.pallas_call(
        body, name="gather_small",
        out_shape=jax.ShapeDtypeStruct((N_DEV,) + small.shape, F32),
        in_specs=[ANY], out_specs=ANY,
        scratch_shapes=_sems(N_DEV - 1) + [pltpu.SemaphoreType.DMA],
    )(small)


def sum_parts(parts):
    n = len(parts)

    def body(*refs):
        for i in range(n):
            p_ref = refs[i]
            refs[n + i][...] = ((p_ref[0].astype(F32) + p_ref[1].astype(F32)) + p_ref[2].astype(F32)
                                ) + p_ref[3].astype(F32)

    def rows(p):
        return p.shape[1] // RED_GRID

    return pl.pallas_call(
        body, name="sum_parts",
        out_shape=tuple(jax.ShapeDtypeStruct(p.shape[1:], F32) for p in parts),
        grid=(RED_GRID,),
        in_specs=[pl.BlockSpec((N_CHIPS, rows(p), p.shape[2]), lambda b: (0, b, 0)) for p in parts],
        out_specs=tuple(pl.BlockSpec((rows(p), p.shape[2]), lambda b: (b, 0)) for p in parts),
        compiler_params=_params(("parallel",)),
    )(*parts)


def swap_halves(reds):
    n = len(reds)

    def body(*refs):
        ins, outs = refs[:n], refs[n:2 * n]
        ssem, rsem = refs[2 * n:]
        x, y, c = lax.axis_index("x"), lax.axis_index("y"), lax.axis_index("c")
        copies = [pltpu.make_async_remote_copy(
            src_ref=ins[i], dst_ref=outs[i], send_sem=ssem.at[i], recv_sem=rsem.at[i],
            device_id=(x, y, 1 - c), device_id_type=MESH) for i in range(n)]
        for cp in copies:
            cp.start()
        for cp in copies:
            cp.wait()

    return pl.pallas_call(
        body, name="swap_halves",
        out_shape=tuple(jax.ShapeDtypeStruct(r.shape, F32) for r in reds),
        in_specs=[ANY] * n, out_specs=(ANY,) * n, scratch_shapes=_sems(n),
    )(*reds)


def _adamw(w, g, m, v):
    m = ADAM_B1 * m + (1.0 - ADAM_B1) * g
    v = ADAM_B2 * v + (1.0 - ADAM_B2) * (g * g)
    m_hat = m / (1.0 - ADAM_B1 ** ADAM_STEP)
    v_hat = v / (1.0 - ADAM_B2 ** ADAM_STEP)
    delta = -ADAM_LR * (m_hat / (jnp.sqrt(v_hat) + ADAM_EPS) + ADAM_WD * w)
    return delta, m, v


def adamw_big(core, mine, theirs, ws, ms, vs):
    n = len(ws)
    per_half = RED_GRID // 2

    def body(c_ref, *refs):
        own = (pl.program_id(0) // per_half) == c_ref[0]
        for i in range(n):
            g = jnp.where(own, refs[i][...], refs[n + i][...])
            d, mn, vn = _adamw(refs[2 * n + i][...], g, refs[3 * n + i][...], refs[4 * n + i][...])
            refs[5 * n + i][...] = g
            refs[6 * n + i][...] = d
            refs[7 * n + i][...] = mn
            refs[8 * n + i][...] = vn

    def blk(w):
        return (w.shape[0] // RED_GRID, w.shape[1])

    halves = [pl.BlockSpec(blk(w), lambda b, c_ref: (b % per_half, 0)) for w in ws]
    whole = [pl.BlockSpec(blk(w), lambda b, c_ref: (b, 0)) for w in ws]
    shapes = [jax.ShapeDtypeStruct(w.shape, F32) for w in ws]
    grid_spec = pltpu.PrefetchScalarGridSpec(
        num_scalar_prefetch=1, grid=(RED_GRID,), in_specs=halves * 2 + whole * 3, out_specs=whole * 4)
    outs = pl.pallas_call(
        body, name="adamw_big", out_shape=tuple(shapes * 4), grid_spec=grid_spec,
        compiler_params=_params(("parallel",)),
    )(core, *mine, *theirs, *ws, *ms, *vs)
    return outs[:n], outs[n:2 * n], outs[2 * n:3 * n], outs[3 * n:]


def adamw_whole(g, w, m, v, name):
    def body(g_ref, w_ref, m_ref, v_ref, d_out, m_out, v_out):
        d, mn, vn = _adamw(w_ref[...], g_ref[...], m_ref[...], v_ref[...])
        d_out[...] = d
        m_out[...] = mn
        v_out[...] = vn

    shp = jax.ShapeDtypeStruct(g.shape, F32)
    return pl.pallas_call(body, name=name, out_shape=(shp,) * 3)(g, w, m, v)


def adamw_small(smalls, w, m, v):
    def body(s_ref, w_ref, m_ref, v_ref, g_out, d_out, m_out, v_out):
        g = s_ref[0]
        for k in range(1, N_DEV):
            g = g + s_ref[k]
        d, mn, vn = _adamw(w_ref[...], g, m_ref[...], v_ref[...])
        g_out[...] = g
        d_out[...] = d
        m_out[...] = mn
        v_out[...] = vn

    shp = jax.ShapeDtypeStruct((SMALL_ROWS, LANES), F32)
    return pl.pallas_call(body, name="adamw_small", out_shape=(shp,) * 4)(smalls, w, m, v)


def in_proj_fwd(x, g, ws, fgb_row):
    s = x.shape[0]
    tm = _blk(s, FORGET_BLOCK)
    n = len(ws)

    def body(x_ref, g_ref, b_ref, *refs):
        carry = refs[2 * n + 2]

        @pl.when(pl.program_id(0) == 0)
        def _():
            carry[...] = jnp.zeros_like(carry)

        xv = x_ref[...]
        r = lax.rsqrt(jnp.mean(xv * xv, axis=-1, keepdims=True) + EPS)
        u = (xv * r * g_ref[...]).astype(BF16)
        refs[n][...] = u
        for i in range(n - 1):
            refs[n + 1 + i][...] = _mm(u, refs[i][...])
        small = _mm(u, refs[n - 1][...])
        refs[2 * n][...] = small
        lane = _iota((tm, LANES), 1)
        in_f = (lane >= N_HEADS) & (lane < 2 * N_HEADS)
        logf = jnp.where(in_f, -_softplus(-(small + b_ref[...])), 0.0)
        tri = (_iota((tm, tm), 1) <= _iota((tm, tm), 0)).astype(F32)
        cum = _mm_exact(tri, logf) + carry[0:1, :]
        refs[2 * n + 1][...] = cum
        carry[...] = jnp.broadcast_to(cum[tm - 1:tm, :], (8, LANES))

    rows = lambda width: pl.BlockSpec((tm, width), lambda i: (i, 0))
    return pl.pallas_call(
        body, name="in_proj_fwd",
        out_shape=tuple([jax.ShapeDtypeStruct((s, D_MODEL), BF16)]
                        + [jax.ShapeDtypeStruct((s, w.shape[1]), F32) for w in ws]
                        + [jax.ShapeDtypeStruct((s, LANES), F32)]),
        grid=(s // tm,),
        in_specs=([rows(D_MODEL), _const_spec((1, D_MODEL)), _const_spec((1, LANES))]
                  + [_const_spec(w.shape) for w in ws]),
        out_specs=tuple([rows(D_MODEL)] + [rows(w.shape[1]) for w in ws] + [rows(LANES)]),
        scratch_shapes=[pltpu.VMEM((8, LANES), F32)],
        compiler_params=_params(("arbitrary",)),
    )(x, g, fgb_row, *ws)


def matmul_tn(a, b, name):
    s, m = a.shape
    n = b.shape[1]
    tk = _blk(s, 2048)
    tn = _blk(n, 512)

    def body(a_ref, b_ref, o_ref):
        @pl.when(pl.program_id(1) == 0)
        def _():
            o_ref[...] = jnp.zeros_like(o_ref)

        o_ref[...] += _mm_tn(a_ref[...], b_ref[...])

    return pl.pallas_call(
        body, name=name, out_shape=jax.ShapeDtypeStruct((m, n), F32), grid=(n // tn, s // tk),
        in_specs=[pl.BlockSpec((tk, m), lambda j, i: (i, 0)), pl.BlockSpec((tk, tn), lambda j, i: (i, j))],
        out_specs=pl.BlockSpec((m, tn), lambda j, i: (0, j)),
        compiler_params=_params(("parallel", "arbitrary")),
    )(a, b)


def conv_fwd(xbc, w, b):
    s = xbc.shape[0]
    tm = _blk(s, 256)

    def body(x_ref, t_ref, w_ref, b_ref, pre_ref, act_ref):
        i = pl.program_id(0)
        row8 = _iota((8, LANES), 0)
        for c0 in range(0, CONV_CH, LANES):
            cols = slice(c0, c0 + LANES)
            cur = x_ref[:, cols]
            tail = jnp.where(i > 0, t_ref[:, cols], 0.0)
            wv = w_ref[:, cols]
            bias = b_ref[:, cols]
            acc = cur * wv[3:4, :] + bias
            head = cur[0:8, :] * wv[3:4, :] + bias
            for sh in range(1, CONV_WIDTH):
                wk = wv[3 - sh:4 - sh, :]
                acc = acc + pltpu.roll(cur, sh, 0) * wk
                first = jnp.where(row8 < sh, pltpu.roll(tail, sh, 0), pltpu.roll(cur[0:8, :], sh, 0))
                head = head + first * wk
            pre_ref[:, cols] = acc
            act_ref[:, cols] = acc * _sigmoid(acc)
            pre_ref[0:8, cols] = head
            act_ref[0:8, cols] = head * _sigmoid(head)

    shp = jax.ShapeDtypeStruct(xbc.shape, F32)
    rows = pl.BlockSpec((tm, CONV_CH), lambda i: (i, 0))
    return pl.pallas_call(
        body, name="conv_fwd", out_shape=(shp, shp), grid=(s // tm,),
        in_specs=[rows, pl.BlockSpec((8, CONV_CH), lambda i: (jnp.maximum(i * (tm // 8) - 1, 0), 0)),
                  _const_spec((CONV_WIDTH, CONV_CH)), _const_spec((1, CONV_CH))],
        out_specs=(rows, rows), compiler_params=_params(("parallel",)),
    )(xbc, xbc, w, b)


def conv_bwd(xbc, pre, dact, w):
    s = xbc.shape[0]
    tm = _blk(s, 256)
    nb = s // tm

    def dsilu(p):
        sg = _sigmoid(p)
        return sg * (1.0 + p * (1.0 - sg))

    def body(x_ref, xt_ref, p_ref, pn_ref, d_ref, dn_ref, w_ref, dx_ref, dw_ref, db_ref):
        i = pl.program_id(0)

        @pl.when(i == 0)
        def _():
            dw_ref[...] = jnp.zeros_like(dw_ref)
            db_ref[...] = jnp.zeros_like(db_ref)

        row8 = _iota((8, LANES), 0)
        for c0 in range(0, CONV_CH, LANES):
            cols = slice(c0, c0 + LANES)
            wv = w_ref[:, cols]
            dpre = d_ref[:, cols] * dsilu(p_ref[:, cols])
            dnext = jnp.where(i < nb - 1, dn_ref[:, cols] * dsilu(pn_ref[:, cols]), 0.0)
            cur = x_ref[:, cols]
            tail = jnp.where(i > 0, xt_ref[:, cols], 0.0)
            dx = dpre * wv[3:4, :]
            last = dpre[tm - 8:tm, :] * wv[3:4, :]
            db_ref[:, cols] += jnp.sum(dpre, axis=0, keepdims=True)
            dws = [jnp.sum(dpre * cur, axis=0, keepdims=True)]
            for sh in range(1, CONV_WIDTH):
                wk = wv[3 - sh:4 - sh, :]
                dx = dx + pltpu.roll(dpre, tm - sh, 0) * wk
                nxt = jnp.where(row8 >= 8 - sh, pltpu.roll(dnext, 8 - sh, 0),
                                pltpu.roll(dpre[tm - 8:tm, :], 8 - sh, 0))
                last = last + nxt * wk
                xs = pltpu.roll(cur, sh, 0)
                first = jnp.where(row8 < sh, pltpu.roll(tail, sh, 0), xs[0:8, :])
                dws.append(jnp.sum(dpre * xs, axis=0, keepdims=True)
                           + jnp.sum(dpre[0:8, :] * (first - xs[0:8, :]), axis=0, keepdims=True))
            dx_ref[:, cols] = dx.astype(BF16)
            dx_ref[tm - 8:tm, cols] = last.astype(BF16)
            for sh in range(CONV_WIDTH):
                dw_ref[3 - sh:4 - sh, cols] += dws[sh]

    rows = pl.BlockSpec((tm, CONV_CH), lambda i: (i, 0))
    prev8 = pl.BlockSpec((8, CONV_CH), lambda i: (jnp.maximum(i * (tm // 8) - 1, 0), 0))
    next8 = pl.BlockSpec((8, CONV_CH), lambda i: (jnp.minimum((i + 1) * (tm // 8), s // 8 - 1), 0))
    return pl.pallas_call(
        body, name="conv_bwd",
        out_shape=(jax.ShapeDtypeStruct(xbc.shape, BF16), jax.ShapeDtypeStruct((8, CONV_CH), F32),
                   jax.ShapeDtypeStruct((1, CONV_CH), F32)),
        grid=(nb,),
        in_specs=[rows, prev8, rows, next8, rows, next8, _const_spec((CONV_WIDTH, CONV_CH))],
        out_specs=(rows, _const_spec((8, CONV_CH)), _const_spec((1, CONV_CH))),
        compiler_params=_params(("arbitrary",)),
    )(xbc, xbc, pre, pre, dact, dact, w)


def _pair_lanes(mat, j, lane):
    return jnp.where(lane < HEAD_DIM, mat[:, 2 * j:2 * j + 1], mat[:, 2 * j + 1:2 * j + 2])


def _ssd_chunk_prelude(sm, dtb, a_row, lane, sub):
    raw = sm + dtb
    head_lane = lane < N_HEADS
    dt = jnp.where(head_lane, _softplus(raw), 0.0)
    sig = jnp.where(head_lane, _sigmoid(raw), 0.0)
    tri = (lane <= sub).astype(F32)
    acs = _mm_exact(tri, dt * a_row)
    return dt, sig, acs, acs.T


GROUP_WIDTH = SSD_WIDTH // N_GROUPS
HEADS_PER_GROUP = N_HEADS // N_GROUPS


def _expand_group(mat, g, lane):
    return jnp.concatenate([_pair_lanes(mat, j, lane) for j in range(4 * g, 4 * g + 4)], axis=1)


def _head_sums(q, g):
    row = _iota((GROUP_WIDTH, LANES), 0)
    seg = (_iota((GROUP_WIDTH, LANES), 1) == HEADS_PER_GROUP * g + (row >> 6)).astype(BF16)
    hi = q.astype(BF16)
    lo = (q - hi.astype(F32)).astype(BF16)
    return _mm(hi, seg) + _mm(lo, seg)


def _rows_from_lanes(row512):
    return jnp.broadcast_to(row512, (LANES, GROUP_WIDTH)).T


def ssd_fwd(xc, small, dtb_row, a_row, dskip_lane):
    s = xc.shape[0]
    nc = s // CHUNK

    def body(xc_ref, sm_ref, dtb_ref, a_ref, dsk_ref, y_ref, hs_ref, h_scr):
        c = pl.program_id(0)

        @pl.when(c == 0)
        def _():
            h_scr[...] = jnp.zeros_like(h_scr)

        lane = _iota((CHUNK, LANES), 1)
        sub = _iota((CHUNK, LANES), 0)
        causal = lane <= sub
        dt, _, acs, acs_t = _ssd_chunk_prelude(sm_ref[...], dtb_ref[...], a_ref[...], lane, sub)
        for g in range(N_GROUPS):
            cols = slice(GROUP_WIDTH * g, GROUP_WIDTH * (g + 1))
            b_off = SSD_WIDTH + D_STATE * g
            c_off = SSD_WIDTH + N_GROUPS * D_STATE + D_STATE * g
            b_b = xc_ref[:, b_off:b_off + D_STATE].astype(BF16)
            c_b = xc_ref[:, c_off:c_off + D_STATE].astype(BF16)
            cb = _mm_nt(c_b, b_b)
            x_g = xc_ref[:, cols]
            acs_g = _expand_group(acs, g, lane)
            xdt_g = x_g * _expand_group(dt, g, lane)
            xdt_b = xdt_g.astype(BF16)
            heads = range(HEADS_PER_GROUP * g, HEADS_PER_GROUP * (g + 1))
            m_b = [(cb * jnp.exp(jnp.where(causal, acs[:, h:h + 1] - acs_t[h:h + 1, :], NEG_BIG))).astype(BF16)
                   for h in heads]
            yd = [_mm(m_b[k], xdt_b[:, LANES * (k // 2):LANES * (k // 2 + 1)]) for k in range(HEADS_PER_GROUP)]
            yd_g = jnp.concatenate([jnp.where(lane < HEAD_DIM, yd[2 * k], yd[2 * k + 1]) for k in range(4)], axis=1)
            h_g = h_scr[g]
            t_g = _mm_nt(c_b, h_g.astype(BF16))
            y_ref[:, cols] = yd_g + jnp.exp(acs_g) * t_g + dsk_ref[:, cols] * x_g
            hs_ref[0, g] = h_g
            last_g = acs_g[CHUNK - 1:CHUNK, :]
            w_b = (xdt_g * jnp.exp(last_g - acs_g)).astype(BF16)
            h_scr[g] = h_g * jnp.exp(_rows_from_lanes(last_g)) + _mm_tn(w_b, b_b)

    return pl.pallas_call(
        body, name="ssd_fwd",
        out_shape=(jax.ShapeDtypeStruct((s, SSD_WIDTH), F32),
                   jax.ShapeDtypeStruct((nc, N_GROUPS, GROUP_WIDTH, D_STATE), F32)),
        grid=(nc,),
        in_specs=[pl.BlockSpec((CHUNK, CONV_CH), lambda c: (c, 0)), pl.BlockSpec((CHUNK, LANES), lambda c: (c, 0)),
                  _const_spec((1, LANES)), _const_spec((1, LANES)), _const_spec((1, SSD_WIDTH))],
        out_specs=(pl.BlockSpec((CHUNK, SSD_WIDTH), lambda c: (c, 0)),
                   pl.BlockSpec((1, N_GROUPS, GROUP_WIDTH, D_STATE), lambda c: (c, 0, 0, 0))),
        scratch_shapes=[pltpu.VMEM((N_GROUPS, GROUP_WIDTH, D_STATE), F32)],
        compiler_params=_params(("arbitrary",)),
    )(xc, small, dtb_row, a_row, dskip_lane)


def ssd_bwd(xc, small, states, dy, dtb_row, a_row, dskip_lane):
    s = xc.shape[0]
    nc = s // CHUNK
    rev = lambda c: nc - 1 - c

    def body(xc_ref, sm_ref, hs_ref, dy_ref, dtb_ref, a_ref, dsk_ref,
             dxc_ref, ddt_ref, da_ref, ddtb_ref, ddsk_ref, dh_scr):
        c = pl.program_id(0)

        @pl.when(c == 0)
        def _():
            dh_scr[...] = jnp.zeros_like(dh_scr)
            da_ref[...] = jnp.zeros_like(da_ref)
            ddtb_ref[...] = jnp.zeros_like(ddtb_ref)
            ddsk_ref[...] = jnp.zeros_like(ddsk_ref)

        lane = _iota((CHUNK, LANES), 1)
        sub = _iota((CHUNK, LANES), 0)
        causal = lane <= sub
        upper = lane >= sub
        is_last = sub == CHUNK - 1
        a_row_v = a_ref[...]
        dt, sig, acs, acs_t = _ssd_chunk_prelude(sm_ref[...], dtb_ref[...], a_row_v, lane, sub)
        cd = jnp.exp(acs[CHUNK - 1:CHUNK, :])
        dacs_c = jnp.zeros((CHUNK, LANES), F32)
        dacs_r = jnp.zeros((LANES, CHUNK), F32)
        ddtx = jnp.zeros((CHUNK, LANES), F32)
        for g in range(N_GROUPS):
            cols = slice(GROUP_WIDTH * g, GROUP_WIDTH * (g + 1))
            b_off = SSD_WIDTH + D_STATE * g
            c_off = SSD_WIDTH + N_GROUPS * D_STATE + D_STATE * g
            b_b = xc_ref[:, b_off:b_off + D_STATE].astype(BF16)
            c_b = xc_ref[:, c_off:c_off + D_STATE].astype(BF16)
            cb = _mm_nt(c_b, b_b)
            cb_t = _mm_nt(b_b, c_b)
            x_g = xc_ref[:, cols]
            dy_g = dy_ref[:, cols]
            dt_g = _expand_group(dt, g, lane)
            acs_g = _expand_group(acs, g, lane)
            last_g = acs_g[CHUNK - 1:CHUNK, :]
            e_g = jnp.exp(acs_g)
            dte_g = jnp.exp(last_g - acs_g)
            xdt_g = x_g * dt_g
            xdt_b = xdt_g.astype(BF16)
            h_g = hs_ref[0, g]
            dh_g = dh_scr[g]
            h_b = h_g.astype(BF16)
            dh_b = dh_g.astype(BF16)
            heads = list(range(HEADS_PER_GROUP * g, HEADS_PER_GROUP * (g + 1)))
            segs = [acs[:, h:h + 1] - acs_t[h:h + 1, :] for h in heads]
            lms = [jnp.exp(jnp.where(causal, sg, NEG_BIG)) for sg in segs]
            mts = [(cb_t * jnp.exp(jnp.where(upper, -sg, NEG_BIG))).astype(BF16) for sg in segs]
            dyh = []
            for k in range(HEADS_PER_GROUP):
                blk = dy_g[:, LANES * (k // 2):LANES * (k // 2 + 1)]
                in_head = (lane < HEAD_DIM) if k % 2 == 0 else (lane >= HEAD_DIM)
                dyh.append(jnp.where(in_head, blk, 0.0).astype(BF16))
            dms = [_mm_nt(dyh[k], xdt_b[:, LANES * (k // 2):LANES * (k // 2 + 1)]) for k in range(HEADS_PER_GROUP)]
            dxs = [_mm(mts[k], dyh[k]) for k in range(HEADS_PER_GROUP)]
            dcb = jnp.zeros((CHUNK, CHUNK), F32)
            for k, h in enumerate(heads):
                gmat = dms[k] * (cb * lms[k])
                dacs_c = dacs_c + jnp.where(lane == h, jnp.sum(gmat, axis=1, keepdims=True), 0.0)
                dacs_r = dacs_r - jnp.where(sub == h, jnp.sum(gmat, axis=0, keepdims=True), 0.0)
                dcb = dcb + dms[k] * lms[k]
            dxdt_g = jnp.concatenate([dxs[2 * k] + dxs[2 * k + 1] for k in range(4)], axis=1)
            t_g = _mm_nt(c_b, h_b)
            dacs_c = dacs_c + _head_sums(dy_g * e_g * t_g, g)
            dt_b = (dy_g * e_g).astype(BF16)
            dc_acc = _mm(dt_b, h_b)
            dh_prev = _mm_tn(dt_b, c_b)
            dw_g = _mm_nt(b_b, dh_b)
            w_g = xdt_g * dte_g
            dxdt_g = dxdt_g + dw_g * dte_g
            db_acc = _mm(w_g.astype(BF16), dh_b)
            r2 = _head_sums(dw_g * w_g, g)
            dacs_c = dacs_c + jnp.where(is_last, jnp.sum(r2, axis=0, keepdims=True), 0.0) - r2
            q3 = jnp.sum(dh_g * h_g, axis=1, keepdims=True)
            for k, h in enumerate(heads):
                tot = jnp.sum(q3[HEAD_DIM * k:HEAD_DIM * (k + 1), :], keepdims=True) * cd[:, h:h + 1]
                dacs_c = dacs_c + jnp.where(is_last & (lane == h), tot, 0.0)
            dh_scr[g] = dh_prev + dh_g * jnp.exp(_rows_from_lanes(last_g))
            dxc_ref[:, cols] = dxdt_g * dt_g + dsk_ref[:, cols] * dy_g
            ddtx = ddtx + _head_sums(dxdt_g * x_g, g)
            ddsk_ref[:, cols] += jnp.sum(dy_g * x_g, axis=0, keepdims=True)
            dxc_ref[:, b_off:b_off + D_STATE] = db_acc + _mm(dcb.T.astype(BF16), c_b)
            dxc_ref[:, c_off:c_off + D_STATE] = dc_acc + _mm(dcb.astype(BF16), b_b)
        dacs = dacs_c + dacs_r.T
        dadt = _mm_exact((lane >= sub).astype(F32), dacs)
        ddt = dadt * a_row_v + ddtx
        ddt_raw = ddt * sig
        ddt_ref[...] = ddt_raw
        da_ref[...] += jnp.sum(dadt * dt, axis=0, keepdims=True)
        ddtb_ref[...] += jnp.sum(ddt_raw, axis=0, keepdims=True)

    return pl.pallas_call(
        body, name="ssd_bwd",
        out_shape=(jax.ShapeDtypeStruct((s, CONV_CH), F32), jax.ShapeDtypeStruct((s, LANES), F32),
                   jax.ShapeDtypeStruct((1, LANES), F32), jax.ShapeDtypeStruct((1, LANES), F32),
                   jax.ShapeDtypeStruct((1, SSD_WIDTH), F32)),
        grid=(nc,),
        in_specs=[pl.BlockSpec((CHUNK, CONV_CH), lambda c: (rev(c), 0)),
                  pl.BlockSpec((CHUNK, LANES), lambda c: (rev(c), 0)),
                  pl.BlockSpec((1, N_GROUPS, GROUP_WIDTH, D_STATE), lambda c: (rev(c), 0, 0, 0)),
                  pl.BlockSpec((CHUNK, SSD_WIDTH), lambda c: (rev(c), 0)),
                  _const_spec((1, LANES)), _const_spec((1, LANES)), _const_spec((1, SSD_WIDTH))],
        out_specs=(pl.BlockSpec((CHUNK, CONV_CH), lambda c: (rev(c), 0)),
                   pl.BlockSpec((CHUNK, LANES), lambda c: (rev(c), 0)),
                   _const_spec((1, LANES)), _const_spec((1, LANES)), _const_spec((1, SSD_WIDTH))),
        scratch_shapes=[pltpu.VMEM((N_GROUPS, GROUP_WIDTH, D_STATE), F32)],
        compiler_params=_params(("arbitrary",)),
    )(xc, small, states, dy, dtb_row, a_row, dskip_lane)


FORGET_BLOCK = 512


def forget_bwd(dc, small, ddt_raw, fgb_row):
    s = small.shape[0]
    t = _blk(s, FORGET_BLOCK)
    nb = s // t
    rev = lambda i: nb - 1 - i

    def body(dc_ref, sm_ref, ddt_ref, b_ref, ds_ref, dfb_ref, carry):
        i = pl.program_id(0)

        @pl.when(i == 0)
        def _():
            carry[...] = jnp.zeros_like(carry)
            dfb_ref[...] = jnp.zeros_like(dfb_ref)

        lane = _iota((t, LANES), 1)
        rows = dc_ref[...].T
        tri = (_iota((t, t), 1) <= _iota((t, t), 0)).astype(F32)
        rc = _mm_exact(rows, tri) + carry[:, 0:1]
        carry[...] = jnp.broadcast_to(rc[:, 0:1], (LANES, LANES))
        in_f = (lane >= N_HEADS) & (lane < 2 * N_HEADS)
        df = jnp.where(in_f, rc.T * _sigmoid(-(sm_ref[...] + b_ref[...])), 0.0)
        ds_ref[...] = (df + ddt_ref[...]).astype(BF16)
        dfb_ref[...] += jnp.sum(df, axis=0, keepdims=True)

    blk = pl.BlockSpec((t, LANES), lambda i: (rev(i), 0))
    return pl.pallas_call(
        body, name="forget_bwd",
        out_shape=(jax.ShapeDtypeStruct((s, LANES), BF16), jax.ShapeDtypeStruct((1, LANES), F32)),
        grid=(nb,),
        in_specs=[blk, blk, blk, _const_spec((1, LANES))],
        out_specs=(blk, _const_spec((1, LANES))),
        scratch_shapes=[pltpu.VMEM((LANES, LANES), F32)],
        compiler_params=_params(("arbitrary",)),
    )(dc, small, ddt_raw, fgb_row)


ATT_BLOCK = 1024
ATT_BLOCK_BWD = 512
ATT_BLOCK_BWD_Q = 512
ATT_SCALE = HEAD_DIM ** -0.5
AUG_A = HEAD_DIM
AUG_B = HEAD_DIM + 3


def _split3(c):
    hi = c.astype(BF16).astype(F32)
    r = c - hi
    mid = r.astype(BF16).astype(F32)
    return hi, mid, (r - mid).astype(BF16).astype(F32)


def _aug(lane, first, parts=None, value=1.0):
    if parts is None:
        return jnp.where((lane >= first) & (lane < first + 3), value, 0.0)
    return (jnp.where(lane == first, parts[0], 0.0) + jnp.where(lane == first + 1, parts[1], 0.0)
            + jnp.where(lane == first + 2, parts[2], 0.0))


def _pack_pair(a0, a1, lane):
    return jnp.where(lane < HEAD_DIM, a0, pltpu.roll(a1, HEAD_DIM, 1))


def proj_qkv_heads(u, w_q, w_k, w_v, cum, later):
    s = u.shape[0]
    tm = _blk(s, 256)
    nsteps = s // tm
    n_later = len(later)

    def body(u_ref, wq_ref, wk_ref, wv_ref, c_ref, *rest):
        later_in = rest[:n_later]
        qa_ref, ka_ref, va_ref, nrm_ref = rest[n_later:n_later + 4]
        later_out = rest[n_later + 4:2 * n_later + 4]
        sems = rest[2 * n_later + 4:]
        step = pl.program_id(0)

        @pl.when(step == 0)
        def _():
            for cp in gather_copies(later_in, later_out, *sems)[0]:
                cp.start()

        @pl.when(step == nsteps // 2)
        def _():
            for landed, onward in zip(*gather_copies(later_in, later_out, *sems)):
                landed.wait_recv()
                onward.start()

        @pl.when(step == nsteps - 1)
        def _():
            fetched, passed = gather_copies(later_in, later_out, *sems)
            for cp in passed:
                cp.wait_recv()
            for cp in fetched + passed:
                cp.wait_send()

        lane = _iota((tm, LANES), 1)
        lo = lane < HEAD_DIM
        uv = u_ref[...]
        qf = _mm(uv, wq_ref[...]) * ATT_SCALE
        kf = _mm(uv, wk_ref[...])
        vf = _mm(uv, wv_ref[...])
        cc = c_ref[...]
        ones_a = _aug(lane, AUG_A)
        ones_b = _aug(lane, AUG_B)
        sub8 = _iota((8, LANES), 0)
        nrm = jnp.zeros((8, LANES), F32)
        for h in range(N_HEADS):
            j, e = divmod(h, 2)

            def head(full):
                blk = full[:, LANES * j:LANES * (j + 1)]
                if e == 1:
                    blk = pltpu.roll(blk, HEAD_DIM, 1)
                return jnp.where(lo, blk, 0.0)

            parts = _split3(cc[:, N_HEADS + h:N_HEADS + h + 1])
            qh, kh = head(qf), head(kf)
            qa_ref[h] = (qh + _aug(lane, AUG_A, parts) + ones_b).astype(BF16)
            ka_ref[h] = (kh + ones_a - _aug(lane, AUG_B, parts)).astype(BF16)
            va_ref[h] = (head(vf) + ones_a).astype(BF16)
        seg = (_iota((ATT_WIDTH, LANES), 1) == (_iota((ATT_WIDTH, LANES), 0) >> 6)).astype(BF16)
        for r, val in enumerate((qf, kf)):
            sq = val * val
            hi = sq.astype(BF16)
            tot = _mm(hi, seg) + _mm((sq - hi.astype(F32)).astype(BF16), seg)
            nrm = nrm + jnp.where(sub8 == r, jnp.max(tot, axis=0, keepdims=True), 0.0)
        nrm_ref[0] = nrm

    shp = jax.ShapeDtypeStruct((N_HEADS, s, LANES), BF16)
    hspec = pl.BlockSpec((N_HEADS, tm, LANES), lambda i: (0, i, 0))
    wspec = _const_spec((D_MODEL, ATT_WIDTH))
    return pl.pallas_call(
        body, name="proj_qkv_heads",
        out_shape=tuple([shp, shp, shp, jax.ShapeDtypeStruct((nsteps, 8, LANES), F32)]
                        + [jax.ShapeDtypeStruct((N_CHIPS,) + a.shape, a.dtype) for a in later]),
        grid=(nsteps,),
        in_specs=[pl.BlockSpec((tm, D_MODEL), lambda i: (i, 0)), wspec, wspec, wspec,
                  pl.BlockSpec((tm, LANES), lambda i: (i, 0))] + [ANY] * n_later,
        out_specs=tuple([hspec, hspec, hspec, pl.BlockSpec((1, 8, LANES), lambda i: (i, 0, 0))]
                        + [ANY] * n_later),
        scratch_shapes=_sems(3 * n_later) + _sems(3 * n_later),
        compiler_params=_params(("arbitrary",)),
    )(u, w_q, w_k, w_v, cum, *later)


SKIP_BELOW = -110.0


def live_blocks(norms, cum, tq, tk):
    qn = jnp.sqrt(jnp.max(norms[:, 0, :N_HEADS], axis=0))
    kn = jnp.sqrt(jnp.max(norms[:, 1, :N_HEADS], axis=0))
    bound = 2.05 * qn * kn + 2.0
    c_first = cum[0::tq, N_HEADS:2 * N_HEADS]
    c_last = cum[tk - 1::tk, N_HEADS:2 * N_HEADS]
    nq, nk = c_first.shape[0], c_last.shape[0]
    top = bound[None, None, :] + c_first[:, None, :] - c_last[None, :, :]
    before = (jnp.arange(nk)[None, :] + 1) * tk <= jnp.arange(nq)[:, None] * tq
    dead = before[:, :, None] & ~(top >= SKIP_BELOW)
    first = jnp.sum(dead, axis=1).astype(jnp.int32).T
    last_q = jnp.sum(first[:, None, :] <= jnp.arange(nk)[None, :, None], axis=2).astype(jnp.int32) - 1
    return first, last_q


def attention_fwd(first, qa, ka, va):
    s = qa.shape[1]
    t = _blk(s, ATT_BLOCK)
    nq = s // t

    def body(first_ref, qa_ref, ka_ref, va_ref, o_ref, qb_ref, m_scr, acc_scr, alpha_scr, p_scr, s_scr):
        qi = pl.program_id(1)
        starts = [first_ref[2 * pl.program_id(0) + e, qi] for e in range(2)]
        k0 = jnp.maximum(starts[0], starts[1])
        m_scr[...] = jnp.full_like(m_scr, NEG_BIG)
        acc_scr[...] = jnp.zeros_like(acc_scr)

        def kv_rows(kb):
            return pl.ds(pl.multiple_of(kb * t, t), t)

        def logits(kb, masked, heads=(0, 1)):
            for e in heads:
                sc = _mm_nt(qa_ref[e], ka_ref[e, kv_rows(kb), :])
                if masked:
                    sc = jnp.where(_iota((t, t), 0) >= _iota((t, t), 1), sc, NEG_BIG)
                s_scr[e] = sc

        def probs(heads=(0, 1)):
            for e in heads:
                cmax = s_scr[e, :, 0:LANES]
                for c in range(1, t // LANES):
                    cmax = jnp.maximum(cmax, s_scr[e, :, LANES * c:LANES * (c + 1)])
                m_old = m_scr[e]
                m_new = jnp.maximum(m_old, jnp.max(cmax, axis=1, keepdims=True))
                alpha_scr[e] = jnp.exp(m_old - m_new)
                m_scr[e] = m_new
                for c in range(t // LANES):
                    cols = slice(LANES * c, LANES * (c + 1))
                    p_scr[e, :, cols] = jnp.exp(s_scr[e, :, cols] - m_new).astype(BF16)

        def accumulate(kb, heads=(0, 1)):
            for e in heads:
                acc_scr[e] = alpha_scr[e] * acc_scr[e] + _mm(p_scr[e], va_ref[e, kv_rows(kb), :])

        for e in range(2):
            def alone(kb, carry, e=e):
                logits(kb, False, (e,))
                probs((e,))
                accumulate(kb, (e,))
                return carry

            lax.fori_loop(starts[e], k0, alone, 0)

        def loop_body(kb, carry):
            logits(kb, False)
            for e in range(2):
                accumulate(kb - 1, (e,))
                probs((e,))
            return carry

        @pl.when(qi > k0)
        def _():
            logits(k0, False)
            probs()

        lax.fori_loop(k0 + 1, qi, loop_body, 0)

        @pl.when(qi > k0)
        def _():
            logits(qi, True)
            accumulate(qi - 1)
            probs()

        @pl.when(qi == k0)
        def _():
            logits(qi, True)
            probs()

        accumulate(qi)

        lane = _iota((t, LANES), 1)
        outs = []
        for e in range(2):
            acc = acc_scr[e]
            l = acc[:, AUG_A:AUG_A + 1]
            outs.append(acc / l)
            lse = m_scr[e][:, 0:1] + jnp.log(l)
            q32 = qa_ref[e].astype(F32)
            c = q32[:, AUG_A:AUG_A + 1] + q32[:, AUG_A + 1:AUG_A + 2] + q32[:, AUG_A + 2:AUG_A + 3]
            qb = jnp.where(lane < HEAD_DIM, q32, 0.0) + _aug(lane, AUG_A, _split3(c - lse)) + _aug(lane, AUG_B)
            qb_ref[e] = qb.astype(BF16)
        o_ref[...] = _pack_pair(outs[0], outs[1], lane)

    grid_spec = pltpu.PrefetchScalarGridSpec(
        num_scalar_prefetch=1, grid=(N_PAIRS, nq),
        in_specs=[pl.BlockSpec((2, t, LANES), lambda j, qi, f: (j, qi, 0)),
                  pl.BlockSpec((2, s, LANES), lambda j, qi, f: (j, 0, 0)),
                  pl.BlockSpec((2, s, LANES), lambda j, qi, f: (j, 0, 0))],
        out_specs=[pl.BlockSpec((t, LANES), lambda j, qi, f: (qi, j)),
                   pl.BlockSpec((2, t, LANES), lambda j, qi, f: (j, qi, 0))],
        scratch_shapes=[pltpu.VMEM((2, t, LANES), F32), pltpu.VMEM((2, t, LANES), F32),
                        pltpu.VMEM((2, t, LANES), F32), pltpu.VMEM((2, t, t), BF16), pltpu.VMEM((2, t, t), F32)])
    return pl.pallas_call(
        body, name="attention_fwd", grid_spec=grid_spec,
        out_shape=(jax.ShapeDtypeStruct((s, ATT_WIDTH), F32), jax.ShapeDtypeStruct((N_HEADS, s, LANES), BF16)),
        compiler_params=_params(("parallel", "parallel")),
    )(first, qa, ka, va)


def attention_bwd(last_q, qb, ka, va, dob):
    s = qb.shape[1]
    t = _blk(s, ATT_BLOCK_BWD)
    tq = _blk(s, ATT_BLOCK_BWD_Q)
    nq = s // tq
    per_q = tq // t

    def body(last_ref, qb_ref, dob_ref, ka_ref, va_ref, dq_ref, dk_ref, dv_ref, dc_ref, dq_scr, dk_scr, dv_scr):
        j, ki = pl.program_id(0), pl.program_id(1)

        @pl.when((j == 0) & (ki == 0))
        def _():
            dc_ref[...] = jnp.zeros_like(dc_ref)

        @pl.when(ki == 0)
        def _():
            dq_scr[...] = jnp.zeros_like(dq_scr)

        dk_scr[...] = jnp.zeros_like(dk_scr)
        dv_scr[...] = jnp.zeros_like(dv_scr)

        def q_step(qblk, masked, heads=(0, 1)):
            rows = pl.ds(pl.multiple_of(qblk * tq, tq), tq)
            scs = [_mm_nt(qb_ref[e, rows, :], ka_ref[e]) for e in heads]
            dps = [_mm_nt(dob_ref[e, rows, :], va_ref[e]) for e in heads]
            for e, sc, dp in zip(heads, scs, dps):
                q = qb_ref[e, rows, :]
                do = dob_ref[e, rows, :]
                if masked:
                    keep = (_iota((tq, t), 0) - _iota((tq, t), 1)) >= ki * t - qblk * tq
                    sc = jnp.where(keep, sc, NEG_BIG)
                p = jnp.exp(sc)
                ds_b = (p * dp).astype(BF16)
                dv_scr[e] += _mm_tn(p.astype(BF16), do)
                dk_scr[e] += _mm_tn(ds_b, q)
                dq_scr[e, rows, :] += _mm(ds_b, ka_ref[e])

        def loop_body(qblk, carry):
            q_step(qblk, False)
            return carry

        ends = [last_ref[2 * j + e, ki] + 1 for e in range(2)]
        both = jnp.minimum(ends[0], ends[1])
        diag = ki // per_q
        q_step(diag, True)
        lax.fori_loop(diag + 1, both, loop_body, 0)
        for e in range(2):
            def alone(qblk, carry, e=e):
                q_step(qblk, False, (e,))
                return carry

            lax.fori_loop(both, ends[e], alone, 0)

        lane = _iota((t, LANES), 1)
        dk_ref[...] = _pack_pair(dk_scr[0], dk_scr[1], lane).astype(BF16)
        dv_ref[...] = _pack_pair(dv_scr[0], dv_scr[1], lane).astype(BF16)
        rows = pl.ds(pl.multiple_of(ki * t, t), t)
        dc_ref[rows, :] -= (jnp.where(lane == N_HEADS + 2 * j, dk_scr[0][:, AUG_B:AUG_B + 1], 0.0)
                            + jnp.where(lane == N_HEADS + 2 * j + 1, dk_scr[1][:, AUG_B:AUG_B + 1], 0.0))

        @pl.when(ki == s // t - 1)
        def _():
            for blk in range(s // t):
                rws = pl.ds(blk * t, t)
                d0 = dq_scr[0, rws, :]
                d1 = dq_scr[1, rws, :]
                dq_ref[rws, :] = (_pack_pair(d0, d1, lane) * ATT_SCALE).astype(BF16)
                dc_ref[rws, :] += (jnp.where(lane == N_HEADS + 2 * j, d0[:, AUG_A:AUG_A + 1], 0.0)
                                   + jnp.where(lane == N_HEADS + 2 * j + 1, d1[:, AUG_A:AUG_A + 1], 0.0))

    full = pl.BlockSpec((2, s, LANES), lambda j, ki, f: (j, 0, 0))
    blk = pl.BlockSpec((2, t, LANES), lambda j, ki, f: (j, ki, 0))
    pair = pl.BlockSpec((t, LANES), lambda j, ki, f: (ki, j))
    wide = jax.ShapeDtypeStruct((s, ATT_WIDTH), BF16)
    grid_spec = pltpu.PrefetchScalarGridSpec(
        num_scalar_prefetch=1, grid=(N_PAIRS, s // t),
        in_specs=[full, full, blk, blk],
        out_specs=[pl.BlockSpec((s, LANES), lambda j, ki, f: (0, j)), pair, pair,
                   pl.BlockSpec((s, LANES), lambda j, ki, f: (0, 0))],
        scratch_shapes=[pltpu.VMEM((2, s, LANES), F32), pltpu.VMEM((2, t, LANES), F32),
                        pltpu.VMEM((2, t, LANES), F32)])
    return pl.pallas_call(
        body, name="attention_bwd", grid_spec=grid_spec,
        out_shape=(wide, wide, wide, jax.ShapeDtypeStruct((s, LANES), F32)),
        compiler_params=_params(("arbitrary", "arbitrary")),
    )(last_q, qb, dob, ka, va)


def _dsilu(z, sg):
    return sg * (1.0 + z * (1.0 - sg))


def post_mix(x, y, zs, o, za, p, tgt, ssd_g, att_g_lane, ple_g, fin_g, w_out, w_gate, w_proj):
    s = x.shape[0]
    tm = _blk(s, 256)
    half = SSD_WIDTH // N_GROUPS

    def rms_bwd(dy, yn, r):
        return r * (dy - yn * jnp.mean(dy * yn, axis=-1, keepdims=True))

    def colsum(a):
        return jnp.sum(a, axis=0, keepdims=True)

    def body(x_ref, y_ref, zs_ref, o_ref, za_ref, p_ref, t_ref, sg_ref, ag_ref, pg_ref, fg_ref,
             wo_ref, wg_ref, wp_ref,
             dh1_ref, dy_ref, dzs_ref, dob_ref, dza_ref, ycat_ref, dh1b_ref, n2b_ref, dglb_ref, dppb_ref, pb_ref,
             loss_ref, dfin_ref, dple_ref, dssd_ref, datt_ref):
        @pl.when(pl.program_id(0) == 0)
        def _():
            for r in (loss_ref, dfin_ref, dple_ref, dssd_ref, datt_ref):
                r[...] = jnp.zeros_like(r)

        lane = _iota((tm, LANES), 1)
        lo = lane < HEAD_DIM
        zs = zs_ref[...]
        sz = _sigmoid(zs)
        yv = y_ref[...]
        ys = yv * (zs * sz)
        yn, rg = [], []
        for g in range(N_GROUPS):
            seg = ys[:, half * g:half * (g + 1)]
            r = lax.rsqrt(jnp.mean(seg * seg, axis=-1, keepdims=True) + EPS)
            yn.append(seg * r)
            rg.append(r)
            ycat_ref[:, half * g:half * (g + 1)] = (yn[g] * sg_ref[:, half * g:half * (g + 1)]).astype(BF16)
        za = za_ref[...]
        sza = _sigmoid(za)
        silu_za = za * sza
        on, ra = [], []
        for jb in range(N_PAIRS):
            blk = o_ref[:, LANES * jb:LANES * (jb + 1)]
            sq = blk * blk
            ms0 = jnp.sum(jnp.where(lo, sq, 0.0), axis=1, keepdims=True) * (1.0 / HEAD_DIM)
            ms1 = jnp.sum(jnp.where(lo, 0.0, sq), axis=1, keepdims=True) * (1.0 / HEAD_DIM)
            r = jnp.where(lo, lax.rsqrt(ms0 + EPS), lax.rsqrt(ms1 + EPS))
            on.append(blk * r)
            ra.append(r)
            an = on[jb] * ag_ref[:, LANES * jb:LANES * (jb + 1)]
            ycat_ref[:, SSD_WIDTH + LANES * jb:SSD_WIDTH + LANES * (jb + 1)] = (
                an * silu_za[:, LANES * jb:LANES * (jb + 1)]).astype(BF16)
        h1 = x_ref[...] + _mm(ycat_ref[...], wo_ref[...])
        r2 = lax.rsqrt(jnp.mean(h1 * h1, axis=-1, keepdims=True) + EPS)
        n2h = h1 * r2
        n2_b = (n2h * pg_ref[...]).astype(BF16)
        gate = _sigmoid(_mm(n2_b, wg_ref[...]))
        p_b = p_ref[...].astype(BF16)
        pp = _mm(p_b, wp_ref[...])
        h2 = h1 + gate * pp
        r3 = lax.rsqrt(jnp.mean(h2 * h2, axis=-1, keepdims=True) + EPS)
        n3 = h2 * r3
        diff = n3 * fg_ref[...] - t_ref[...]
        sq = colsum(diff * diff)
        part = sq[:, 0:LANES]
        for jb in range(1, D_MODEL // LANES):
            part = part + sq[:, LANES * jb:LANES * (jb + 1)]
        loss_ref[...] += part * (0.5 / D_MODEL)
        dout = diff * (1.0 / D_MODEL)
        dfin_ref[...] += colsum(dout * n3)
        dh2 = rms_bwd(dout * fg_ref[...], n3, r3)
        dgl = dh2 * pp * gate * (1.0 - gate)
        dgl_b = dgl.astype(BF16)
        dn2 = _mm_nt(dgl_b, wg_ref[...])
        dple_ref[...] += colsum(dn2 * n2h)
        dh1 = dh2 + rms_bwd(dn2 * pg_ref[...], n2h, r2)
        dh1_b = dh1.astype(BF16)
        dycat = _mm_nt(dh1_b, wo_ref[...])
        dh1_ref[...] = dh1
        dh1b_ref[...] = dh1_b
        n2b_ref[...] = n2_b
        dglb_ref[...] = dgl_b
        dppb_ref[...] = (dh2 * gate).astype(BF16)
        pb_ref[...] = p_b
        for g in range(N_GROUPS):
            cols = slice(half * g, half * (g + 1))
            dys_g = dycat[:, cols]
            dssd_ref[:, cols] += colsum(dys_g * yn[g])
            dys = rms_bwd(dys_g * sg_ref[:, cols], yn[g], rg[g])
            dy_ref[:, cols] = dys * (zs[:, cols] * sz[:, cols])
            dzs_ref[:, cols] = (dys * yv[:, cols] * _dsilu(zs[:, cols], sz[:, cols])).astype(BF16)
        for jb in range(N_PAIRS):
            cols = slice(LANES * jb, LANES * (jb + 1))
            dya = dycat[:, SSD_WIDTH + LANES * jb:SSD_WIDTH + LANES * (jb + 1)]
            ag = ag_ref[:, cols]
            dan = dya * silu_za[:, cols]
            dza_ref[:, cols] = (dya * (on[jb] * ag) * _dsilu(za[:, cols], sza[:, cols])).astype(BF16)
            datt_ref[:, cols] += colsum(dan * on[jb])
            don = dan * ag
            q = don * on[jb]
            m0 = jnp.sum(jnp.where(lo, q, 0.0), axis=1, keepdims=True) * (1.0 / HEAD_DIM)
            m1 = jnp.sum(jnp.where(lo, 0.0, q), axis=1, keepdims=True) * (1.0 / HEAD_DIM)
            do2 = ra[jb] * (don - on[jb] * jnp.where(lo, m0, m1))
            prod = do2 * o_ref[:, cols]
            for e in range(2):
                delta = jnp.sum(jnp.where(lo, prod, 0.0) if e == 0 else jnp.where(lo, 0.0, prod),
                                axis=1, keepdims=True)
                base = jnp.where(lo, do2 if e == 0 else pltpu.roll(do2, HEAD_DIM, 1), 0.0)
                dob_ref[2 * jb + e] = (base - _aug(lane, AUG_A, _split3(delta))).astype(BF16)

    def rows(n, dtype=None):
        return pl.BlockSpec((tm, n), lambda i: (i, 0))

    def out(n, dtype):
        return jax.ShapeDtypeStruct((s, n), dtype)

    vec = _const_spec((1, D_MODEL))
    vshape = jax.ShapeDtypeStruct((1, D_MODEL), F32)
    return pl.pallas_call(
        body, name="post_mix",
        out_shape=(out(D_MODEL, F32), out(SSD_WIDTH, F32), out(SSD_WIDTH, BF16),
                   jax.ShapeDtypeStruct((N_HEADS, s, LANES), BF16),
                   out(ATT_WIDTH, BF16), out(D_INNER, BF16), out(D_MODEL, BF16), out(D_MODEL, BF16),
                   out(D_MODEL, BF16), out(D_MODEL, BF16), out(PLE_DIM, BF16),
                   jax.ShapeDtypeStruct((1, LANES), F32), vshape, vshape, vshape, vshape),
        grid=(s // tm,),
        in_specs=[rows(D_MODEL), rows(SSD_WIDTH), rows(SSD_WIDTH), rows(ATT_WIDTH), rows(ATT_WIDTH),
                  rows(PLE_DIM), rows(D_MODEL), vec, vec, vec, vec,
                  _const_spec((D_INNER, D_MODEL)), _const_spec((D_MODEL, D_MODEL)), _const_spec((PLE_DIM, D_MODEL))],
        out_specs=(rows(D_MODEL), rows(SSD_WIDTH), rows(SSD_WIDTH),
                   pl.BlockSpec((N_HEADS, tm, LANES), lambda i: (0, i, 0)), rows(ATT_WIDTH),
                   rows(D_INNER), rows(D_MODEL), rows(D_MODEL), rows(D_MODEL), rows(D_MODEL), rows(PLE_DIM),
                   _const_spec((1, LANES)), vec, vec, vec, vec),
        compiler_params=_params(("arbitrary",)),
    )(x, y, zs, o, za, p, tgt, ssd_g, att_g_lane, ple_g, fin_g, w_out, w_gate, w_proj)


def in_proj_bwd(dsegs, wsegs, x, g, dh1, pres):
    s = x.shape[0]
    tm = _blk(s, 256)
    nseg = len(dsegs)
    nbig = len(pres)
    nsteps = s // tm

    def body(*refs):
        d_refs = refs[:nseg]
        w_refs = refs[nseg:2 * nseg]
        x_ref, g_ref, dh1_ref = refs[2 * nseg:2 * nseg + 3]
        rest = refs[2 * nseg + 3:]
        pre_refs, (dx_ref, dg_ref), part_refs = rest[:nbig], rest[nbig:nbig + 2], rest[nbig + 2:2 * nbig + 2]
        ssem, rsem, lsem = rest[2 * nbig + 2:]

        @pl.when(pl.program_id(0) == 0)
        def _():
            dg_ref[...] = jnp.zeros_like(dg_ref)
            for cp in scatter_copies(pre_refs, part_refs, ssem, rsem, lsem):
                cp.start()

        @pl.when(pl.program_id(0) == nsteps - 1)
        def _():
            for cp in scatter_copies(pre_refs, part_refs, ssem, rsem, lsem):
                cp.wait()

        du = _mm_nt(d_refs[0][...], w_refs[0][...])
        for k in range(1, nseg):
            du = du + _mm_nt(d_refs[k][...], w_refs[k][...])
        xv = x_ref[...]
        r = lax.rsqrt(jnp.mean(xv * xv, axis=-1, keepdims=True) + EPS)
        xh = xv * r
        dg_ref[...] += jnp.sum(du * xh, axis=0, keepdims=True)
        dxh = du * g_ref[...]
        dx_ref[...] = r * (dxh - xh * jnp.mean(dxh * xh, axis=-1, keepdims=True)) + dh1_ref[...]

    rows = lambda n: pl.BlockSpec((tm, n), lambda i: (i, 0))
    return pl.pallas_call(
        body, name="in_proj_bwd",
        out_shape=tuple([jax.ShapeDtypeStruct((s, D_MODEL), F32), jax.ShapeDtypeStruct((1, D_MODEL), F32)]
                        + [jax.ShapeDtypeStruct(a.shape, a.dtype) for a in pres]),
        grid=(nsteps,),
        in_specs=([rows(d.shape[1]) for d in dsegs] + [_const_spec(w.shape) for w in wsegs]
                  + [rows(D_MODEL), _const_spec((1, D_MODEL)), rows(D_MODEL)] + [ANY] * nbig),
        out_specs=tuple([rows(D_MODEL), _const_spec((1, D_MODEL))] + [ANY] * nbig),
        scratch_shapes=_sems(3 * nbig) + [pltpu.SemaphoreType.DMA((nbig,))],
        compiler_params=_params(("arbitrary",)),
    )(*dsegs, *wsegs, x, g, dh1, *pres)


SMALL_NAMES = ("norm_g", "conv_b", "dt_bias", "a_log", "d_skip", "ssd_norm_g", "fg_bias", "att_norm_g",
               "ple_norm_g", "final_norm_g")
SMALL_SIZES = (1024, 1536, 16, 16, 16, 1024, 16, 64, 1024, 1024)
CONV_W_SIZE = CONV_WIDTH * CONV_CH


def _pack_small(vals):
    flat = jnp.concatenate([v.reshape(-1).astype(F32) for v in vals])
    flat = jnp.pad(flat, (0, SMALL_ROWS * LANES - flat.shape[0]))
    return flat.reshape(SMALL_ROWS, LANES)


def _unpack_small(pack, shapes):
    flat = pack.reshape(-1)
    out, off = [], 0
    for n, shp in zip(SMALL_SIZES, shapes):
        out.append(flat[off:off + n].reshape(shp))
        off += n
    return out


def _row128(v16, offset=0):
    return jnp.pad(v16.reshape(1, N_HEADS).astype(F32), ((0, 0), (offset, LANES - N_HEADS - offset)))


def local_step(prereduce, later, join_later, x, p, tgt, w_in, conv_w, norm_g, conv_b, dt_bias, a_log, d_skip,
               ssd_norm_g, fg_bias, att_norm_g, ple_norm_g, final_norm_g):
    widths = (SSD_WIDTH, CONV_CH, N_HEADS, ATT_WIDTH, ATT_WIDTH, ATT_WIDTH, ATT_WIDTH)
    c0, c1, c2, c3, c4, c5, c6, c7 = [sum(widths[:i]) for i in range(len(widths) + 1)]
    w_zs, w_xbc, w_dt = w_in[:, c0:c1], w_in[:, c1:c2], w_in[:, c2:c3]
    w_za, w_q, w_k, w_v, w_f = w_in[:, c3:c4], w_in[:, c4:c5], w_in[:, c5:c6], w_in[:, c6:c7], w_in[:, c7:]
    w_small = jnp.concatenate([w_dt, w_f, jnp.zeros((D_MODEL, LANES - 2 * N_HEADS), BF16)], axis=1)

    dtb_row = _row128(dt_bias)
    a_row = _row128(-jnp.exp(a_log.astype(F32)))
    fgb_row = _row128(fg_bias, N_HEADS)
    dskip_lane = jnp.repeat(d_skip.astype(F32), HEAD_DIM).reshape(1, SSD_WIDTH)
    att_g_lane = jnp.tile(att_norm_g.astype(F32), N_HEADS).reshape(1, ATT_WIDTH)
    row = lambda v: v.reshape(1, -1).astype(F32)

    u, zs, xbc, za, small, cum = in_proj_fwd(x, row(norm_g), [w_zs, w_xbc, w_za, w_small], fgb_row)
    qa, ka, va, norms, *gathered = proj_qkv_heads(u, w_q, w_k, w_v, cum, later)
    w_out, w_gate, w_proj = join_later(gathered)
    n_seq = x.shape[0]
    first, _ = live_blocks(norms, cum, _blk(n_seq, ATT_BLOCK), _blk(n_seq, ATT_BLOCK))
    _, last_q = live_blocks(norms, cum, _blk(n_seq, ATT_BLOCK_BWD_Q), _blk(n_seq, ATT_BLOCK_BWD))
    pre, xc = conv_fwd(xbc, conv_w, row(conv_b))
    y, states = ssd_fwd(xc, small, dtb_row, a_row, dskip_lane)
    o, qb = attention_fwd(first, qa, ka, va)
    (dh1, dy, dzs, dob, dza, ycat, dh1_b, n2_b, dgl_b, dpp_b, p_b,
     loss_l, dfin, dple, dssd_g, datt_lane) = post_mix(
        x, y, zs, o, za, p, tgt, row(ssd_norm_g), att_g_lane, row(ple_norm_g), row(final_norm_g),
        w_out, w_gate, w_proj)
    dq, dk, dv, dc = attention_bwd(last_q, qb, ka, va, dob)
    dxc, ddt_raw, da, ddtb, ddsk_lane = ssd_bwd(xc, small, states, dy, dtb_row, a_row, dskip_lane)
    dsmall, dfgb = forget_bwd(dc, small, ddt_raw, fgb_row)
    dxbc, dconv_w8, dconv_b = conv_bwd(xbc, pre, dxc, conv_w)
    dsegs = [dzs, dxbc, dza, dq, dk, dv, dsmall]
    wsegs = [w_zs, w_xbc, w_za, w_q, w_k, w_v, w_small]
    dws = [matmul_tn(u, d, "dw_in_%d" % i) for i, d in enumerate(dsegs)]
    dw_in = jnp.concatenate([dws[0], dws[1], dws[6][:, :N_HEADS], dws[2], dws[3], dws[4], dws[5],
                             dws[6][:, N_HEADS:2 * N_HEADS]], axis=1)
    dw_out = matmul_tn(ycat, dh1_b, "dw_out")
    dw_gate = matmul_tn(n2_b, dgl_b, "dw_gate")
    dw_proj = matmul_tn(p_b, dpp_b, "dw_proj")
    dx, dnorm_g, *parts = in_proj_bwd(dsegs, wsegs, x, row(norm_g), dh1, prereduce(dw_in, dw_out, dw_gate, dw_proj))
    small_grads = [
        dnorm_g, dconv_b, ddtb[0, :N_HEADS], (da * a_row)[0, :N_HEADS],
        ddsk_lane.reshape(N_HEADS, HEAD_DIM).sum(axis=1), dssd_g, dfgb[0, N_HEADS:2 * N_HEADS],
        datt_lane.reshape(N_HEADS, HEAD_DIM).sum(axis=0), dple, dfin]
    loss = jnp.sum(loss_l)
    return loss, dx, parts, dconv_w8[:CONV_WIDTH], small_grads


def kernel(x, p, norm_g, w_in, conv_w, conv_b, dt_bias, a_log, d_skip, ssd_norm_g, fg_bias, att_norm_g, w_out, ple_norm_g, w_ple_gate, w_ple_proj, final_norm_g, loss_target, m_norm_g, m_w_in, m_conv_w, m_conv_b, m_dt_bias, m_a_log, m_d_skip, m_ssd_norm_g, m_fg_bias, m_att_norm_g, m_w_out, m_ple_norm_g, m_w_ple_gate, m_w_ple_proj, m_final_norm_g, v_norm_g, v_w_in, v_conv_w, v_conv_b, v_dt_bias, v_a_log, v_d_skip, v_ssd_norm_g, v_fg_bias, v_att_norm_g, v_w_out, v_ple_norm_g, v_w_ple_gate, v_w_ple_proj, v_final_norm_g):
    chip = 2 * lax.axis_index("x") + lax.axis_index("y")
    core = lax.axis_index("c")

    big_w = [w_in[0], w_out[0], w_ple_gate[0], w_ple_proj[0]]
    own = [a.astype(BF16) for a in big_w] + [conv_w[0]]

    def joined(mine, gathered, axis):
        return jnp.concatenate([jnp.where(chip == j, mine, gathered[j]) for j in range(N_CHIPS)], axis=axis)

    w_in_all, conv_all = gather_weights(own[:1], own[4])
    w_in_f, conv_w_f = joined(own[0], w_in_all, 1), joined(own[4], conv_all, 1)

    def join_later(gathered):
        return [joined(mine, got, axis) for mine, got, axis in zip(own[1:4], gathered, (0, 0, 1))]

    core1 = core.reshape(1).astype(jnp.int32)

    def prereduce(dw_in, dw_out, dw_gate, dw_proj):
        n_in, n_proj = w_in.shape[2], w_ple_proj.shape[2]
        gs = [jnp.stack([dw_in[:, n_in * j:n_in * (j + 1)] for j in range(N_CHIPS)]),
              dw_out.reshape(N_CHIPS, w_out.shape[1], D_MODEL), dw_gate.reshape(N_CHIPS, w_ple_gate.shape[1], D_MODEL),
              jnp.stack([dw_proj[:, n_proj * j:n_proj * (j + 1)] for j in range(N_CHIPS)])]
        return add_halves(core1, gs, halves_to_sibling(gs))

    smalls_w = [norm_g, conv_b, dt_bias, a_log, d_skip, ssd_norm_g, fg_bias, att_norm_g, ple_norm_g, final_norm_g]
    loss_l, dx, parts, dconv_w, small_grads = local_step(
        prereduce, own[1:4], join_later, x[0], p[0, 0], loss_target[0], w_in_f, conv_w_f,
        *[a.reshape(-1) for a in smalls_w])
    loss = lax.psum(loss_l, ("x", "y", "c"))
    smalls = gather_small(_pack_small(list(small_grads) + [dconv_w]))
    mine = sum_parts(parts)

    g_big, d_big, m_big, v_big = adamw_big(
        core1, mine, swap_halves(mine), big_w, [m_w_in[0], m_w_out[0], m_w_ple_gate[0], m_w_ple_proj[0]],
        [v_w_in[0], v_w_out[0], v_w_ple_gate[0], v_w_ple_proj[0]])
    smalls_m = [m_norm_g, m_conv_b, m_dt_bias, m_a_log, m_d_skip, m_ssd_norm_g, m_fg_bias, m_att_norm_g,
                m_ple_norm_g, m_final_norm_g]
    smalls_v = [v_norm_g, v_conv_b, v_dt_bias, v_a_log, v_d_skip, v_ssd_norm_g, v_fg_bias, v_att_norm_g,
                v_ple_norm_g, v_final_norm_g]
    g_sm, d_sm, m_sm, v_sm = adamw_small(smalls, _pack_small(smalls_w), _pack_small(smalls_m), _pack_small(smalls_v))
    n_small = sum(SMALL_SIZES)
    g_conv_full = g_sm.reshape(-1)[n_small:n_small + CONV_W_SIZE].reshape(CONV_WIDTH, CONV_CH)
    n_conv = conv_w.shape[2]
    g_conv = lax.dynamic_slice_in_dim(g_conv_full, chip * n_conv, n_conv, axis=1)
    d_conv, m_conv, v_conv = adamw_whole(g_conv, conv_w[0], m_conv_w[0], v_conv_w[0], "adamw_conv")

    shapes = [a.shape for a in smalls_w]
    outs = []
    for big, conv, sm in ((g_big, g_conv, g_sm), (d_big, d_conv, d_sm), (m_big, m_conv, m_sm), (v_big, v_conv, v_sm)):
        b_in, b_out, b_gate, b_proj = [a[None] for a in big]
        s_norm, s_convb, s_dtb, s_alog, s_dsk, s_ssdg, s_fgb, s_attg, s_pleg, s_fin = _unpack_small(sm, shapes)
        outs.extend([s_norm, b_in, conv[None], s_convb, s_dtb, s_alog, s_dsk, s_ssdg, s_fgb, s_attg, b_out, s_pleg,
                     b_gate, b_proj, s_fin])
    return (loss, dx[None], *outs)
```

```python
import functools

import jax
import jax.numpy as jnp
from jax import lax
from jax.experimental import pallas as pl
from jax.experimental.pallas import tpu as pltpu

F32 = jnp.float32
BF16 = jnp.bfloat16

D_MODEL = 1024
SSD_WIDTH = 1024
ATT_WIDTH = 1024
N_HEADS = 16
HEAD_DIM = 64
N_GROUPS = 2
D_STATE = 128
CONV_CH = 1536
CONV_WIDTH = 4
CHUNK = 128
PLE_DIM = 256
D_INNER = 2048
EPS = 1e-6
IN_COLS = 6688
N_CHIPS = 4
N_DEV = 8
LANES = 128
N_PAIRS = 8

ADAM_LR = 0.001
ADAM_B1 = 0.9
ADAM_B2 = 0.999
ADAM_EPS = 1e-08
ADAM_WD = 0.01
ADAM_STEP = 10

SMALL_ROWS = 96

NEG_BIG = -1e30
VMEM_LIMIT = 56 * 1024 * 1024

MESH = pl.DeviceIdType.MESH
ANY = pl.BlockSpec(memory_space=pl.ANY)


def _mm(a, b):
    return jnp.dot(a, b, preferred_element_type=F32)


def _mm_nt(a, b):
    return lax.dot_general(a, b, (((1,), (1,)), ((), ())), preferred_element_type=F32)


def _mm_tn(a, b):
    return lax.dot_general(a, b, (((0,), (0,)), ((), ())), preferred_element_type=F32)


def _mm_exact(a, b):
    return jnp.dot(a, b, preferred_element_type=F32, precision=lax.Precision.HIGHEST)


def _softplus(x):
    return jnp.maximum(x, 0.0) + jnp.log1p(jnp.exp(-jnp.abs(x)))


def _sigmoid(x):
    return jax.nn.sigmoid(x)


def _iota(shape, dim):
    return lax.broadcasted_iota(jnp.int32, shape, dim)


def _params(sem=None):
    return pltpu.CompilerParams(dimension_semantics=sem, vmem_limit_bytes=VMEM_LIMIT)


def _blk(n, pref):
    return min(n, pref)


def _const_spec(shape):
    nd = len(shape)
    return pl.BlockSpec(shape, lambda *_: (0,) * nd)


def _chip_peers():
    x, y, c = lax.axis_index("x"), lax.axis_index("y"), lax.axis_index("c")
    return x, y, c, [(1 - x, y, c), (x, 1 - y, c), (1 - x, 1 - y, c)]


def _half(rows, c):
    h = rows // 2
    return pl.ds(pl.multiple_of(c * h, 8), h)


def _sems(n):
    return [pltpu.SemaphoreType.DMA((n,)), pltpu.SemaphoreType.DMA((n,))]


def gather_copies(ins, outs, ssem1, rsem1, ssem2, rsem2):
    n = len(ins)
    x, y, c, peers = _chip_peers()
    me = 2 * x + y
    fetched, passed = [], []
    for k, peer in enumerate(peers):
        chip = 2 * peer[0] + peer[1]
        for i in range(n):
            h = _half(ins[i].shape[0], c)
            fetched.append(pltpu.make_async_remote_copy(
                src_ref=ins[i].at[h], dst_ref=outs[i].at[me, h], send_sem=ssem1.at[n * k + i],
                recv_sem=rsem1.at[n * k + i], device_id=peer, device_id_type=MESH))
            passed.append(pltpu.make_async_remote_copy(
                src_ref=outs[i].at[chip, h], dst_ref=outs[i].at[chip, h], send_sem=ssem2.at[n * k + i],
                recv_sem=rsem2.at[n * k + i], device_id=(x, y, 1 - c), device_id_type=MESH))
    return fetched, passed


def gather_weights(shards, conv_s):
    n = len(shards)

    def body(*refs):
        ins, conv_in = refs[:n], refs[n]
        outs, conv_out = refs[n + 1:2 * n + 1], refs[2 * n + 1]
        ssem1, rsem1, ssem2, rsem2, c_ssem, c_rsem = refs[2 * n + 2:]
        x, y, _, peers = _chip_peers()
        fetched, passed = gather_copies(ins, outs, ssem1, rsem1, ssem2, rsem2)
        small = [pltpu.make_async_remote_copy(
            src_ref=conv_in, dst_ref=conv_out.at[2 * x + y], send_sem=c_ssem.at[k], recv_sem=c_rsem.at[k],
            device_id=peer, device_id_type=MESH) for k, peer in enumerate(peers)]
        for cp in fetched + small:
            cp.start()
        for landed, onward in zip(fetched, passed):
            landed.wait_recv()
            onward.start()
        for cp in passed:
            cp.wait_recv()
        for cp in fetched + passed:
            cp.wait_send()
        for cp in small:
            cp.wait()

    return pl.pallas_call(
        body, name="gather_weights",
        out_shape=tuple(jax.ShapeDtypeStruct((N_CHIPS,) + a.shape, a.dtype) for a in list(shards) + [conv_s]),
        in_specs=[ANY] * (n + 1), out_specs=(ANY,) * (n + 1),
        scratch_shapes=_sems(3 * n) + _sems(3 * n) + _sems(3),
    )(*shards, conv_s)


def halves_to_sibling(gs):
    n = len(gs)

    def body(*refs):
        ins, outs = refs[:n], refs[n:2 * n]
        ssem, rsem = refs[2 * n:]
        x, y, c = lax.axis_index("x"), lax.axis_index("y"), lax.axis_index("c")
        copies = []
        for i in range(n):
            for j in range(N_CHIPS):
                copies.append(pltpu.make_async_remote_copy(
                    src_ref=ins[i].at[j, _half(ins[i].shape[1], 1 - c)], dst_ref=outs[i].at[j],
                    send_sem=ssem.at[N_CHIPS * i + j], recv_sem=rsem.at[N_CHIPS * i + j],
                    device_id=(x, y, 1 - c), device_id_type=MESH))
        for cp in copies:
            cp.start()
        for cp in copies:
            cp.wait()

    return pl.pallas_call(
        body, name="halves_to_sibling",
        out_shape=tuple(jax.ShapeDtypeStruct((N_CHIPS, g.shape[1] // 2, g.shape[2]), F32) for g in gs),
        in_specs=[ANY] * n, out_specs=(ANY,) * n, scratch_shapes=_sems(N_CHIPS * n),
    )(*gs)


RED_GRID = 8


def add_halves(core, gs, rbs):
    n = len(gs)

    def body(c_ref, *refs):
        for i in range(n):
            refs[2 * n + i][...] = (refs[i][...] + refs[n + i][...]).astype(BF16)

    def blk(g):
        return (1, g.shape[1] // 2 // RED_GRID, g.shape[2])

    grid_spec = pltpu.PrefetchScalarGridSpec(
        num_scalar_prefetch=1, grid=(N_CHIPS, RED_GRID),
        in_specs=([pl.BlockSpec(blk(g), lambda j, b, c_ref: (j, c_ref[0] * RED_GRID + b, 0)) for g in gs]
                  + [pl.BlockSpec(blk(g), lambda j, b, c_ref: (j, b, 0)) for g in gs]),
        out_specs=[pl.BlockSpec(blk(g), lambda j, b, c_ref: (j, b, 0)) for g in gs])
    return pl.pallas_call(
        body, name="add_halves", grid_spec=grid_spec,
        out_shape=tuple(jax.ShapeDtypeStruct(r.shape, BF16) for r in rbs),
        compiler_params=_params(("parallel", "parallel")),
    )(core, *gs, *rbs)


def scatter_copies(ins, outs, ssem, rsem, lsem):
    n = len(ins)
    x, y, _, peers = _chip_peers()
    me = 2 * x + y
    copies = [pltpu.make_async_copy(ins[i].at[me], outs[i].at[me], lsem.at[i]) for i in range(n)]
    for k, peer in enumerate(peers):
        dst_chip = 2 * peer[0] + peer[1]
        for i in range(n):
            copies.append(pltpu.make_async_remote_copy(
                src_ref=ins[i].at[dst_chip], dst_ref=outs[i].at[me], send_sem=ssem.at[n * k + i],
                recv_sem=rsem.at[n * k + i], device_id=peer, device_id_type=MESH))
    return copies


def gather_small(small):
    def body(s_ref, smalls_ref, ssem, rsem, lsem):
        x, y, c = lax.axis_index("x"), lax.axis_index("y"), lax.axis_index("c")
        dev = 4 * x + 2 * y + c
        copies = [pltpu.make_async_copy(s_ref, smalls_ref.at[dev], lsem)]
        for k in range(1, N_DEV):
            fx, fy, fc = (k >> 2) & 1, (k >> 1) & 1, k & 1
            peer = ((1 - x) if fx else x, (1 - y) if fy else y, (1 - c) if fc else c)
            copies.append(pltpu.make_async_remote_copy(
                src_ref=s_ref, dst_ref=smalls_ref.at[dev], send_sem=ssem.at[k - 1], recv_sem=rsem.at[k - 1],
                device_id=peer, device_id_type=MESH))
        for cp in copies:
            cp.start()
        for cp in copies:
            cp.wait()

    return pl.pallas_call(
        body, name="gather_small",
        out_shape=jax.ShapeDtypeStruct((N_DEV,) + small.shape, F32),
        in_specs=[ANY], out_specs=ANY,
        scratch_shapes=_sems(N_DEV - 1) + [pltpu.SemaphoreType.DMA],
    )(small)


def sum_parts(parts):
    n = len(parts)

    def body(*refs):
        for i in range(n):
            p_ref = refs[i]
            refs[n + i][...] = ((p_ref[0].astype(F32) + p_ref[1].astype(F32)) + p_ref[2].astype(F32)
                                ) + p_ref[3].astype(F32)

    def rows(p):
        return p.shape[1] // RED_GRID

    return pl.pallas_call(
        body, name="sum_parts",
        out_shape=tuple(jax.ShapeDtypeStruct(p.shape[1:], F32) for p in parts),
        grid=(RED_GRID,),
        in_specs=[pl.BlockSpec((N_CHIPS, rows(p), p.shape[2]), lambda b: (0, b, 0)) for p in parts],
        out_specs=tuple(pl.BlockSpec((rows(p), p.shape[2]), lambda b: (b, 0)) for p in parts),
        compiler_params=_params(("parallel",)),
    )(*parts)


def swap_halves(reds):
    n = len(reds)

    def body(*refs):
        ins, outs = refs[:n], refs[n:2 * n]
        ssem, rsem = refs[2 * n:]
        x, y, c = lax.axis_index("x"), lax.axis_index("y"), lax.axis_index("c")
        copies = [pltpu.make_async_remote_copy(
            src_ref=ins[i], dst_ref=outs[i], send_sem=ssem.at[i], recv_sem=rsem.at[i],
            device_id=(x, y, 1 - c), device_id_type=MESH) for i in range(n)]
        for cp in copies:
            cp.start()
        for cp in copies:
            cp.wait()

    return pl.pallas_call(
        body, name="swap_halves",
        out_shape=tuple(jax.ShapeDtypeStruct(r.shape, F32) for r in reds),
        in_specs=[ANY] * n, out_specs=(ANY,) * n, scratch_shapes=_sems(n),
    )(*reds)


def _adamw(w, g, m, v):
    m = ADAM_B1 * m + (1.0 - ADAM_B1) * g
    v = ADAM_B2 * v + (1.0 - ADAM_B2) * (g * g)
    m_hat = m / (1.0 - ADAM_B1 ** ADAM_STEP)
    v_hat = v / (1.0 - ADAM_B2 ** ADAM_STEP)
    delta = -ADAM_LR * (m_hat / (jnp.sqrt(v_hat) + ADAM_EPS) + ADAM_WD * w)
    return delta, m, v


def adamw_big(core, mine, theirs, ws, ms, vs):
    n = len(ws)
    per_half = RED_GRID // 2

    def body(c_ref, *refs):
        own = (pl.program_id(0) // per_half) == c_ref[0]
        for i in range(n):
            g = jnp.where(own, refs[i][...], refs[n + i][...])
            d, mn, vn = _adamw(refs[2 * n + i][...], g, refs[3 * n + i][...], refs[4 * n + i][...])
            refs[5 * n + i][...] = g
            refs[6 * n + i][...] = d
            refs[7 * n + i][...] = mn
            refs[8 * n + i][...] = vn

    def blk(w):
        return (w.shape[0] // RED_GRID, w.shape[1])

    halves = [pl.BlockSpec(blk(w), lambda b, c_ref: (b % per_half, 0)) for w in ws]
    whole = [pl.BlockSpec(blk(w), lambda b, c_ref: (b, 0)) for w in ws]
    shapes = [jax.ShapeDtypeStruct(w.shape, F32) for w in ws]
    grid_spec = pltpu.PrefetchScalarGridSpec(
        num_scalar_prefetch=1, grid=(RED_GRID,), in_specs=halves * 2 + whole * 3, out_specs=whole * 4)
    outs = pl.pallas_call(
        body, name="adamw_big", out_shape=tuple(shapes * 4), grid_spec=grid_spec,
        compiler_params=_params(("parallel",)),
    )(core, *mine, *theirs, *ws, *ms, *vs)
    return outs[:n], outs[n:2 * n], outs[2 * n:3 * n], outs[3 * n:]


def adamw_whole(g, w, m, v, name):
    def body(g_ref, w_ref, m_ref, v_ref, d_out, m_out, v_out):
        d, mn, vn = _adamw(w_ref[...], g_ref[...], m_ref[...], v_ref[...])
        d_out[...] = d
        m_out[...] = mn
        v_out[...] = vn

    shp = jax.ShapeDtypeStruct(g.shape, F32)
    return pl.pallas_call(body, name=name, out_shape=(shp,) * 3)(g, w, m, v)


def adamw_small(smalls, w, m, v):
    def body(s_ref, w_ref, m_ref, v_ref, g_out, d_out, m_out, v_out):
        g = s_ref[0]
        for k in range(1, N_DEV):
            g = g + s_ref[k]
        d, mn, vn = _adamw(w_ref[...], g, m_ref[...], v_ref[...])
        g_out[...] = g
        d_out[...] = d
        m_out[...] = mn
        v_out[...] = vn

    shp = jax.ShapeDtypeStruct((SMALL_ROWS, LANES), F32)
    return pl.pallas_call(body, name="adamw_small", out_shape=(shp,) * 4)(smalls, w, m, v)


def in_proj_fwd(x, g, ws):
    s = x.shape[0]
    tm = _blk(s, 512)
    n = len(ws)

    def body(x_ref, g_ref, *refs):
        xv = x_ref[...]
        r = lax.rsqrt(jnp.mean(xv * xv, axis=-1, keepdims=True) + EPS)
        u = (xv * r * g_ref[...]).astype(BF16)
        refs[n][...] = u
        for i in range(n):
            refs[n + 1 + i][...] = _mm(u, refs[i][...])

    rows = lambda width: pl.BlockSpec((tm, width), lambda i: (i, 0))
    return pl.pallas_call(
        body, name="in_proj_fwd",
        out_shape=tuple([jax.ShapeDtypeStruct((s, D_MODEL), BF16)]
                        + [jax.ShapeDtypeStruct((s, w.shape[1]), F32) for w in ws]),
        grid=(s // tm,),
        in_specs=[rows(D_MODEL), _const_spec((1, D_MODEL))] + [_const_spec(w.shape) for w in ws],
        out_specs=tuple([rows(D_MODEL)] + [rows(w.shape[1]) for w in ws]),
        compiler_params=_params(("parallel",)),
    )(x, g, *ws)


def matmul_tn(a, b, name):
    s, m = a.shape
    n = b.shape[1]
    tk = _blk(s, 2048)
    tn = _blk(n, 512)

    def body(a_ref, b_ref, o_ref):
        @pl.when(pl.program_id(1) == 0)
        def _():
            o_ref[...] = jnp.zeros_like(o_ref)

        o_ref[...] += _mm_tn(a_ref[...], b_ref[...])

    return pl.pallas_call(
        body, name=name, out_shape=jax.ShapeDtypeStruct((m, n), F32), grid=(n // tn, s // tk),
        in_specs=[pl.BlockSpec((tk, m), lambda j, i: (i, 0)), pl.BlockSpec((tk, tn), lambda j, i: (i, j))],
        out_specs=pl.BlockSpec((m, tn), lambda j, i: (0, j)),
        compiler_params=_params(("parallel", "arbitrary")),
    )(a, b)


def conv_fwd(xbc, w, b):
    s = xbc.shape[0]
    tm = _blk(s, 256)

    def body(x_ref, t_ref, w_ref, b_ref, pre_ref, act_ref):
        i = pl.program_id(0)
        row8 = _iota((8, LANES), 0)
        for c0 in range(0, CONV_CH, LANES):
            cols = slice(c0, c0 + LANES)
            cur = x_ref[:, cols]
            tail = jnp.where(i > 0, t_ref[:, cols], 0.0)
            wv = w_ref[:, cols]
            bias = b_ref[:, cols]
            acc = cur * wv[3:4, :] + bias
            head = cur[0:8, :] * wv[3:4, :] + bias
            for sh in range(1, CONV_WIDTH):
                wk = wv[3 - sh:4 - sh, :]
                acc = acc + pltpu.roll(cur, sh, 0) * wk
                first = jnp.where(row8 < sh, pltpu.roll(tail, sh, 0), pltpu.roll(cur[0:8, :], sh, 0))
                head = head + first * wk
            pre_ref[:, cols] = acc
            act_ref[:, cols] = acc * _sigmoid(acc)
            pre_ref[0:8, cols] = head
            act_ref[0:8, cols] = head * _sigmoid(head)

    shp = jax.ShapeDtypeStruct(xbc.shape, F32)
    rows = pl.BlockSpec((tm, CONV_CH), lambda i: (i, 0))
    return pl.pallas_call(
        body, name="conv_fwd", out_shape=(shp, shp), grid=(s // tm,),
        in_specs=[rows, pl.BlockSpec((8, CONV_CH), lambda i: (jnp.maximum(i * (tm // 8) - 1, 0), 0)),
                  _const_spec((CONV_WIDTH, CONV_CH)), _const_spec((1, CONV_CH))],
        out_specs=(rows, rows), compiler_params=_params(("parallel",)),
    )(xbc, xbc, w, b)


def conv_bwd(xbc, pre, dact, w):
    s = xbc.shape[0]
    tm = _blk(s, 256)
    nb = s // tm

    def dsilu(p):
        sg = _sigmoid(p)
        return sg * (1.0 + p * (1.0 - sg))

    def body(x_ref, xt_ref, p_ref, pn_ref, d_ref, dn_ref, w_ref, dx_ref, dw_ref, db_ref):
        i = pl.program_id(0)

        @pl.when(i == 0)
        def _():
            dw_ref[...] = jnp.zeros_like(dw_ref)
            db_ref[...] = jnp.zeros_like(db_ref)

        row8 = _iota((8, LANES), 0)
        for c0 in range(0, CONV_CH, LANES):
            cols = slice(c0, c0 + LANES)
            wv = w_ref[:, cols]
            dpre = d_ref[:, cols] * dsilu(p_ref[:, cols])
            dnext = jnp.where(i < nb - 1, dn_ref[:, cols] * dsilu(pn_ref[:, cols]), 0.0)
            cur = x_ref[:, cols]
            tail = jnp.where(i > 0, xt_ref[:, cols], 0.0)
            dx = dpre * wv[3:4, :]
            last = dpre[tm - 8:tm, :] * wv[3:4, :]
            db_ref[:, cols] += jnp.sum(dpre, axis=0, keepdims=True)
            dws = [jnp.sum(dpre * cur, axis=0, keepdims=True)]
            for sh in range(1, CONV_WIDTH):
                wk = wv[3 - sh:4 - sh, :]
                dx = dx + pltpu.roll(dpre, tm - sh, 0) * wk
                nxt = jnp.where(row8 >= 8 - sh, pltpu.roll(dnext, 8 - sh, 0),
                                pltpu.roll(dpre[tm - 8:tm, :], 8 - sh, 0))
                last = last + nxt * wk
                xs = pltpu.roll(cur, sh, 0)
                first = jnp.where(row8 < sh, pltpu.roll(tail, sh, 0), xs[0:8, :])
                dws.append(jnp.sum(dpre * xs, axis=0, keepdims=True)
                           + jnp.sum(dpre[0:8, :] * (first - xs[0:8, :]), axis=0, keepdims=True))
            dx_ref[:, cols] = dx.astype(BF16)
            dx_ref[tm - 8:tm, cols] = last.astype(BF16)
            for sh in range(CONV_WIDTH):
                dw_ref[3 - sh:4 - sh, cols] += dws[sh]

    rows = pl.BlockSpec((tm, CONV_CH), lambda i: (i, 0))
    prev8 = pl.BlockSpec((8, CONV_CH), lambda i: (jnp.maximum(i * (tm // 8) - 1, 0), 0))
    next8 = pl.BlockSpec((8, CONV_CH), lambda i: (jnp.minimum((i + 1) * (tm // 8), s // 8 - 1), 0))
    return pl.pallas_call(
        body, name="conv_bwd",
        out_shape=(jax.ShapeDtypeStruct(xbc.shape, BF16), jax.ShapeDtypeStruct((8, CONV_CH), F32),
                   jax.ShapeDtypeStruct((1, CONV_CH), F32)),
        grid=(nb,),
        in_specs=[rows, prev8, rows, next8, rows, next8, _const_spec((CONV_WIDTH, CONV_CH))],
        out_specs=(rows, _const_spec((8, CONV_CH)), _const_spec((1, CONV_CH))),
        compiler_params=_params(("arbitrary",)),
    )(xbc, xbc, pre, pre, dact, dact, w)


def _pair_lanes(mat, j, lane):
    return jnp.where(lane < HEAD_DIM, mat[:, 2 * j:2 * j + 1], mat[:, 2 * j + 1:2 * j + 2])


def _ssd_chunk_prelude(sm, dtb, a_row, lane, sub):
    raw = sm + dtb
    head_lane = lane < N_HEADS
    dt = jnp.where(head_lane, _softplus(raw), 0.0)
    sig = jnp.where(head_lane, _sigmoid(raw), 0.0)
    tri = (lane <= sub).astype(F32)
    acs = _mm_exact(tri, dt * a_row)
    return dt, sig, acs, acs.T


GROUP_WIDTH = SSD_WIDTH // N_GROUPS
HEADS_PER_GROUP = N_HEADS // N_GROUPS


def _expand_group(mat, g, lane):
    return jnp.concatenate([_pair_lanes(mat, j, lane) for j in range(4 * g, 4 * g + 4)], axis=1)


def _head_sums(q, g):
    row = _iota((GROUP_WIDTH, LANES), 0)
    seg = (_iota((GROUP_WIDTH, LANES), 1) == HEADS_PER_GROUP * g + (row >> 6)).astype(BF16)
    hi = q.astype(BF16)
    lo = (q - hi.astype(F32)).astype(BF16)
    return _mm(hi, seg) + _mm(lo, seg)


def _rows_from_lanes(row512):
    return jnp.broadcast_to(row512, (LANES, GROUP_WIDTH)).T


def ssd_fwd(xc, small, dtb_row, a_row, dskip_lane):
    s = xc.shape[0]
    nc = s // CHUNK

    def body(xc_ref, sm_ref, dtb_ref, a_ref, dsk_ref, y_ref, hs_ref, h_scr):
        c = pl.program_id(0)

        @pl.when(c == 0)
        def _():
            h_scr[...] = jnp.zeros_like(h_scr)

        lane = _iota((CHUNK, LANES), 1)
        sub = _iota((CHUNK, LANES), 0)
        causal = lane <= sub
        dt, _, acs, acs_t = _ssd_chunk_prelude(sm_ref[...], dtb_ref[...], a_ref[...], lane, sub)
        for g in range(N_GROUPS):
            cols = slice(GROUP_WIDTH * g, GROUP_WIDTH * (g + 1))
            b_off = SSD_WIDTH + D_STATE * g
            c_off = SSD_WIDTH + N_GROUPS * D_STATE + D_STATE * g
            b_b = xc_ref[:, b_off:b_off + D_STATE].astype(BF16)
            c_b = xc_ref[:, c_off:c_off + D_STATE].astype(BF16)
            cb = _mm_nt(c_b, b_b)
            x_g = xc_ref[:, cols]
            acs_g = _expand_group(acs, g, lane)
            xdt_g = x_g * _expand_group(dt, g, lane)
            xdt_b = xdt_g.astype(BF16)
            heads = range(HEADS_PER_GROUP * g, HEADS_PER_GROUP * (g + 1))
            m_b = [(cb * jnp.exp(jnp.where(causal, acs[:, h:h + 1] - acs_t[h:h + 1, :], NEG_BIG))).astype(BF16)
                   for h in heads]
            yd = [_mm(m_b[k], xdt_b[:, LANES * (k // 2):LANES * (k // 2 + 1)]) for k in range(HEADS_PER_GROUP)]
            yd_g = jnp.concatenate([jnp.where(lane < HEAD_DIM, yd[2 * k], yd[2 * k + 1]) for k in range(4)], axis=1)
            h_g = h_scr[g]
            t_g = _mm_nt(c_b, h_g.astype(BF16))
            y_ref[:, cols] = yd_g + jnp.exp(acs_g) * t_g + dsk_ref[:, cols] * x_g
            hs_ref[0, g] = h_g
            last_g = acs_g[CHUNK - 1:CHUNK, :]
            w_b = (xdt_g * jnp.exp(last_g - acs_g)).astype(BF16)
            h_scr[g] = h_g * jnp.exp(_rows_from_lanes(last_g)) + _mm_tn(w_b, b_b)

    return pl.pallas_call(
        body, name="ssd_fwd",
        out_shape=(jax.ShapeDtypeStruct((s, SSD_WIDTH), F32),
                   jax.ShapeDtypeStruct((nc, N_GROUPS, GROUP_WIDTH, D_STATE), F32)),
        grid=(nc,),
        in_specs=[pl.BlockSpec((CHUNK, CONV_CH), lambda c: (c, 0)), pl.BlockSpec((CHUNK, LANES), lambda c: (c, 0)),
                  _const_spec((1, LANES)), _const_spec((1, LANES)), _const_spec((1, SSD_WIDTH))],
        out_specs=(pl.BlockSpec((CHUNK, SSD_WIDTH), lambda c: (c, 0)),
                   pl.BlockSpec((1, N_GROUPS, GROUP_WIDTH, D_STATE), lambda c: (c, 0, 0, 0))),
        scratch_shapes=[pltpu.VMEM((N_GROUPS, GROUP_WIDTH, D_STATE), F32)],
        compiler_params=_params(("arbitrary",)),
    )(xc, small, dtb_row, a_row, dskip_lane)


def ssd_bwd(xc, small, states, dy, dtb_row, a_row, dskip_lane):
    s = xc.shape[0]
    nc = s // CHUNK
    rev = lambda c: nc - 1 - c

    def body(xc_ref, sm_ref, hs_ref, dy_ref, dtb_ref, a_ref, dsk_ref,
             dxc_ref, ddt_ref, da_ref, ddtb_ref, ddsk_ref, dh_scr):
        c = pl.program_id(0)

        @pl.when(c == 0)
        def _():
            dh_scr[...] = jnp.zeros_like(dh_scr)
            da_ref[...] = jnp.zeros_like(da_ref)
            ddtb_ref[...] = jnp.zeros_like(ddtb_ref)
            ddsk_ref[...] = jnp.zeros_like(ddsk_ref)

        lane = _iota((CHUNK, LANES), 1)
        sub = _iota((CHUNK, LANES), 0)
        causal = lane <= sub
        upper = lane >= sub
        is_last = sub == CHUNK - 1
        a_row_v = a_ref[...]
        dt, sig, acs, acs_t = _ssd_chunk_prelude(sm_ref[...], dtb_ref[...], a_row_v, lane, sub)
        cd = jnp.exp(acs[CHUNK - 1:CHUNK, :])
        dacs_c = jnp.zeros((CHUNK, LANES), F32)
        dacs_r = jnp.zeros((LANES, CHUNK), F32)
        ddtx = jnp.zeros((CHUNK, LANES), F32)
        for g in range(N_GROUPS):
            cols = slice(GROUP_WIDTH * g, GROUP_WIDTH * (g + 1))
            b_off = SSD_WIDTH + D_STATE * g
            c_off = SSD_WIDTH + N_GROUPS * D_STATE + D_STATE * g
            b_b = xc_ref[:, b_off:b_off + D_STATE].astype(BF16)
            c_b = xc_ref[:, c_off:c_off + D_STATE].astype(BF16)
            cb = _mm_nt(c_b, b_b)
            cb_t = _mm_nt(b_b, c_b)
            x_g = xc_ref[:, cols]
            dy_g = dy_ref[:, cols]
            dt_g = _expand_group(dt, g, lane)
            acs_g = _expand_group(acs, g, lane)
            last_g = acs_g[CHUNK - 1:CHUNK, :]
            e_g = jnp.exp(acs_g)
            dte_g = jnp.exp(last_g - acs_g)
            xdt_g = x_g * dt_g
            xdt_b = xdt_g.astype(BF16)
            h_g = hs_ref[0, g]
            dh_g = dh_scr[g]
            h_b = h_g.astype(BF16)
            dh_b = dh_g.astype(BF16)
            heads = list(range(HEADS_PER_GROUP * g, HEADS_PER_GROUP * (g + 1)))
            segs = [acs[:, h:h + 1] - acs_t[h:h + 1, :] for h in heads]
            lms = [jnp.exp(jnp.where(causal, sg, NEG_BIG)) for sg in segs]
            mts = [(cb_t * jnp.exp(jnp.where(upper, -sg, NEG_BIG))).astype(BF16) for sg in segs]
            dyh = []
            for k in range(HEADS_PER_GROUP):
                blk = dy_g[:, LANES * (k // 2):LANES * (k // 2 + 1)]
                in_head = (lane < HEAD_DIM) if k % 2 == 0 else (lane >= HEAD_DIM)
                dyh.append(jnp.where(in_head, blk, 0.0).astype(BF16))
            dms = [_mm_nt(dyh[k], xdt_b[:, LANES * (k // 2):LANES * (k // 2 + 1)]) for k in range(HEADS_PER_GROUP)]
            dxs = [_mm(mts[k], dyh[k]) for k in range(HEADS_PER_GROUP)]
            dcb = jnp.zeros((CHUNK, CHUNK), F32)
            for k, h in enumerate(heads):
                gmat = dms[k] * (cb * lms[k])
                dacs_c = dacs_c + jnp.where(lane == h, jnp.sum(gmat, axis=1, keepdims=True), 0.0)
                dacs_r = dacs_r - jnp.where(sub == h, jnp.sum(gmat, axis=0, keepdims=True), 0.0)
                dcb = dcb + dms[k] * lms[k]
            dxdt_g = jnp.concatenate([dxs[2 * k] + dxs[2 * k + 1] for k in range(4)], axis=1)
            t_g = _mm_nt(c_b, h_b)
            dacs_c = dacs_c + _head_sums(dy_g * e_g * t_g, g)
            dt_b = (dy_g * e_g).astype(BF16)
            dc_acc = _mm(dt_b, h_b)
            dh_prev = _mm_tn(dt_b, c_b)
            dw_g = _mm_nt(b_b, dh_b)
            w_g = xdt_g * dte_g
            dxdt_g = dxdt_g + dw_g * dte_g
            db_acc = _mm(w_g.astype(BF16), dh_b)
            r2 = _head_sums(dw_g * w_g, g)
            dacs_c = dacs_c + jnp.where(is_last, jnp.sum(r2, axis=0, keepdims=True), 0.0) - r2
            q3 = jnp.sum(dh_g * h_g, axis=1, keepdims=True)
            for k, h in enumerate(heads):
                tot = jnp.sum(q3[HEAD_DIM * k:HEAD_DIM * (k + 1), :], keepdims=True) * cd[:, h:h + 1]
                dacs_c = dacs_c + jnp.where(is_last & (lane == h), tot, 0.0)
            dh_scr[g] = dh_prev + dh_g * jnp.exp(_rows_from_lanes(last_g))
            dxc_ref[:, cols] = dxdt_g * dt_g + dsk_ref[:, cols] * dy_g
            ddtx = ddtx + _head_sums(dxdt_g * x_g, g)
            ddsk_ref[:, cols] += jnp.sum(dy_g * x_g, axis=0, keepdims=True)
            dxc_ref[:, b_off:b_off + D_STATE] = db_acc + _mm(dcb.T.astype(BF16), c_b)
            dxc_ref[:, c_off:c_off + D_STATE] = dc_acc + _mm(dcb.astype(BF16), b_b)
        dacs = dacs_c + dacs_r.T
        dadt = _mm_exact((lane >= sub).astype(F32), dacs)
        ddt = dadt * a_row_v + ddtx
        ddt_raw = ddt * sig
        ddt_ref[...] = ddt_raw
        da_ref[...] += jnp.sum(dadt * dt, axis=0, keepdims=True)
        ddtb_ref[...] += jnp.sum(ddt_raw, axis=0, keepdims=True)

    return pl.pallas_call(
        body, name="ssd_bwd",
        out_shape=(jax.ShapeDtypeStruct((s, CONV_CH), F32), jax.ShapeDtypeStruct((s, LANES), F32),
                   jax.ShapeDtypeStruct((1, LANES), F32), jax.ShapeDtypeStruct((1, LANES), F32),
                   jax.ShapeDtypeStruct((1, SSD_WIDTH), F32)),
        grid=(nc,),
        in_specs=[pl.BlockSpec((CHUNK, CONV_CH), lambda c: (rev(c), 0)),
                  pl.BlockSpec((CHUNK, LANES), lambda c: (rev(c), 0)),
                  pl.BlockSpec((1, N_GROUPS, GROUP_WIDTH, D_STATE), lambda c: (rev(c), 0, 0, 0)),
                  pl.BlockSpec((CHUNK, SSD_WIDTH), lambda c: (rev(c), 0)),
                  _const_spec((1, LANES)), _const_spec((1, LANES)), _const_spec((1, SSD_WIDTH))],
        out_specs=(pl.BlockSpec((CHUNK, CONV_CH), lambda c: (rev(c), 0)),
                   pl.BlockSpec((CHUNK, LANES), lambda c: (rev(c), 0)),
                   _const_spec((1, LANES)), _const_spec((1, LANES)), _const_spec((1, SSD_WIDTH))),
        scratch_shapes=[pltpu.VMEM((N_GROUPS, GROUP_WIDTH, D_STATE), F32)],
        compiler_params=_params(("arbitrary",)),
    )(xc, small, states, dy, dtb_row, a_row, dskip_lane)


FORGET_BLOCK = 512


def forget_cumsum(small, fgb_row):
    s = small.shape[0]
    t = _blk(s, FORGET_BLOCK)
    nb = s // t

    def body(sm_ref, b_ref, cc_ref, carry):
        i = pl.program_id(0)

        @pl.when(i == 0)
        def _():
            carry[...] = jnp.zeros_like(carry)

        lane = _iota((t, LANES), 1)
        in_f = (lane >= N_HEADS) & (lane < 2 * N_HEADS)
        logf = jnp.where(in_f, -_softplus(-(sm_ref[...] + b_ref[...])), 0.0)
        tri = (_iota((t, t), 1) <= _iota((t, t), 0)).astype(F32)
        cum = _mm_exact(tri, logf) + carry[0:1, :]
        cc_ref[...] = cum
        carry[...] = jnp.broadcast_to(cum[t - 1:t, :], (8, LANES))

    return pl.pallas_call(
        body, name="forget_cumsum",
        out_shape=jax.ShapeDtypeStruct((s, LANES), F32),
        grid=(nb,),
        in_specs=[pl.BlockSpec((t, LANES), lambda i: (i, 0)), _const_spec((1, LANES))],
        out_specs=pl.BlockSpec((t, LANES), lambda i: (i, 0)),
        scratch_shapes=[pltpu.VMEM((8, LANES), F32)],
        compiler_params=_params(("arbitrary",)),
    )(small, fgb_row)


def forget_bwd(dc, small, ddt_raw, fgb_row):
    s = small.shape[0]
    t = _blk(s, FORGET_BLOCK)
    nb = s // t
    rev = lambda i: nb - 1 - i

    def body(dc_ref, sm_ref, ddt_ref, b_ref, ds_ref, dfb_ref, carry):
        i = pl.program_id(0)

        @pl.when(i == 0)
        def _():
            carry[...] = jnp.zeros_like(carry)
            dfb_ref[...] = jnp.zeros_like(dfb_ref)

        lane = _iota((t, LANES), 1)
        rows = dc_ref[...].T
        tri = (_iota((t, t), 1) <= _iota((t, t), 0)).astype(F32)
        rc = _mm_exact(rows, tri) + carry[:, 0:1]
        carry[...] = jnp.broadcast_to(rc[:, 0:1], (LANES, LANES))
        in_f = (lane >= N_HEADS) & (lane < 2 * N_HEADS)
        df = jnp.where(in_f, rc.T * _sigmoid(-(sm_ref[...] + b_ref[...])), 0.0)
        ds_ref[...] = (df + ddt_ref[...]).astype(BF16)
        dfb_ref[...] += jnp.sum(df, axis=0, keepdims=True)

    blk = pl.BlockSpec((t, LANES), lambda i: (rev(i), 0))
    return pl.pallas_call(
        body, name="forget_bwd",
        out_shape=(jax.ShapeDtypeStruct((s, LANES), BF16), jax.ShapeDtypeStruct((1, LANES), F32)),
        grid=(nb,),
        in_specs=[blk, blk, blk, _const_spec((1, LANES))],
        out_specs=(blk, _const_spec((1, LANES))),
        scratch_shapes=[pltpu.VMEM((LANES, LANES), F32)],
        compiler_params=_params(("arbitrary",)),
    )(dc, small, ddt_raw, fgb_row)


ATT_BLOCK = 1024
ATT_BLOCK_BWD = 512
ATT_BLOCK_BWD_Q = 512
ATT_SCALE = HEAD_DIM ** -0.5
AUG_A = HEAD_DIM
AUG_B = HEAD_DIM + 3


def _split3(c):
    hi = c.astype(BF16).astype(F32)
    r = c - hi
    mid = r.astype(BF16).astype(F32)
    return hi, mid, (r - mid).astype(BF16).astype(F32)


def _aug(lane, first, parts=None, value=1.0):
    if parts is None:
        return jnp.where((lane >= first) & (lane < first + 3), value, 0.0)
    return (jnp.where(lane == first, parts[0], 0.0) + jnp.where(lane == first + 1, parts[1], 0.0)
            + jnp.where(lane == first + 2, parts[2], 0.0))


def _pack_pair(a0, a1, lane):
    return jnp.where(lane < HEAD_DIM, a0, pltpu.roll(a1, HEAD_DIM, 1))


def proj_qkv_heads(u, w_q, w_k, w_v, cum, later):
    s = u.shape[0]
    tm = _blk(s, 256)
    nsteps = s // tm
    n_later = len(later)

    def body(u_ref, wq_ref, wk_ref, wv_ref, c_ref, *rest):
        later_in = rest[:n_later]
        qa_ref, ka_ref, va_ref, nrm_ref = rest[n_later:n_later + 4]
        later_out = rest[n_later + 4:2 * n_later + 4]
        sems = rest[2 * n_later + 4:]
        step = pl.program_id(0)

        @pl.when(step == 0)
        def _():
            for cp in gather_copies(later_in, later_out, *sems)[0]:
                cp.start()

        @pl.when(step == nsteps // 2)
        def _():
            for landed, onward in zip(*gather_copies(later_in, later_out, *sems)):
                landed.wait_recv()
                onward.start()

        @pl.when(step == nsteps - 1)
        def _():
            fetched, passed = gather_copies(later_in, later_out, *sems)
            for cp in passed:
                cp.wait_recv()
            for cp in fetched + passed:
                cp.wait_send()

        lane = _iota((tm, LANES), 1)
        lo = lane < HEAD_DIM
        uv = u_ref[...]
        qf = _mm(uv, wq_ref[...]) * ATT_SCALE
        kf = _mm(uv, wk_ref[...])
        vf = _mm(uv, wv_ref[...])
        cc = c_ref[...]
        ones_a = _aug(lane, AUG_A)
        ones_b = _aug(lane, AUG_B)
        sub8 = _iota((8, LANES), 0)
        nrm = jnp.zeros((8, LANES), F32)
        for h in range(N_HEADS):
            j, e = divmod(h, 2)

            def head(full):
                blk = full[:, LANES * j:LANES * (j + 1)]
                if e == 1:
                    blk = pltpu.roll(blk, HEAD_DIM, 1)
                return jnp.where(lo, blk, 0.0)

            parts = _split3(cc[:, N_HEADS + h:N_HEADS + h + 1])
            qh, kh = head(qf), head(kf)
            qa_ref[h] = (qh + _aug(lane, AUG_A, parts) + ones_b).astype(BF16)
            ka_ref[h] = (kh + ones_a - _aug(lane, AUG_B, parts)).astype(BF16)
            va_ref[h] = (head(vf) + ones_a).astype(BF16)
        seg = (_iota((ATT_WIDTH, LANES), 1) == (_iota((ATT_WIDTH, LANES), 0) >> 6)).astype(BF16)
        for r, val in enumerate((qf, kf)):
            sq = val * val
            hi = sq.astype(BF16)
            tot = _mm(hi, seg) + _mm((sq - hi.astype(F32)).astype(BF16), seg)
            nrm = nrm + jnp.where(sub8 == r, jnp.max(tot, axis=0, keepdims=True), 0.0)
        nrm_ref[0] = nrm

    shp = jax.ShapeDtypeStruct((N_HEADS, s, LANES), BF16)
    hspec = pl.BlockSpec((N_HEADS, tm, LANES), lambda i: (0, i, 0))
    wspec = _const_spec((D_MODEL, ATT_WIDTH))
    return pl.pallas_call(
        body, name="proj_qkv_heads",
        out_shape=tuple([shp, shp, shp, jax.ShapeDtypeStruct((nsteps, 8, LANES), F32)]
                        + [jax.ShapeDtypeStruct((N_CHIPS,) + a.shape, a.dtype) for a in later]),
        grid=(nsteps,),
        in_specs=[pl.BlockSpec((tm, D_MODEL), lambda i: (i, 0)), wspec, wspec, wspec,
                  pl.BlockSpec((tm, LANES), lambda i: (i, 0))] + [ANY] * n_later,
        out_specs=tuple([hspec, hspec, hspec, pl.BlockSpec((1, 8, LANES), lambda i: (i, 0, 0))]
                        + [ANY] * n_later),
        scratch_shapes=_sems(3 * n_later) + _sems(3 * n_later),
        compiler_params=_params(("arbitrary",)),
    )(u, w_q, w_k, w_v, cum, *later)


SKIP_BELOW = -110.0


def live_blocks(norms, cum, tq, tk):
    qn = jnp.sqrt(jnp.max(norms[:, 0, :N_HEADS], axis=0))
    kn = jnp.sqrt(jnp.max(norms[:, 1, :N_HEADS], axis=0))
    bound = 2.05 * qn * kn + 2.0
    c_first = cum[0::tq, N_HEADS:2 * N_HEADS]
    c_last = cum[tk - 1::tk, N_HEADS:2 * N_HEADS]
    nq, nk = c_first.shape[0], c_last.shape[0]
    top = bound[None, None, :] + c_first[:, None, :] - c_last[None, :, :]
    before = (jnp.arange(nk)[None, :] + 1) * tk <= jnp.arange(nq)[:, None] * tq
    dead = before[:, :, None] & ~(top >= SKIP_BELOW)
    first = jnp.sum(dead, axis=1).astype(jnp.int32).T
    last_q = jnp.sum(first[:, None, :] <= jnp.arange(nk)[None, :, None], axis=2).astype(jnp.int32) - 1
    return first, last_q


def attention_fwd(first, qa, ka, va):
    s = qa.shape[1]
    t = _blk(s, ATT_BLOCK)
    nq = s // t

    def body(first_ref, qa_ref, ka_ref, va_ref, o_ref, qb_ref, m_scr, acc_scr, alpha_scr, p_scr, s_scr):
        qi = pl.program_id(1)
        starts = [first_ref[2 * pl.program_id(0) + e, qi] for e in range(2)]
        k0 = jnp.maximum(starts[0], starts[1])
        m_scr[...] = jnp.full_like(m_scr, NEG_BIG)
        acc_scr[...] = jnp.zeros_like(acc_scr)

        def kv_rows(kb):
            return pl.ds(pl.multiple_of(kb * t, t), t)

        def logits(kb, masked, heads=(0, 1)):
            for e in heads:
                sc = _mm_nt(qa_ref[e], ka_ref[e, kv_rows(kb), :])
                if masked:
                    sc = jnp.where(_iota((t, t), 0) >= _iota((t, t), 1), sc, NEG_BIG)
                s_scr[e] = sc

        def probs(heads=(0, 1)):
            for e in heads:
                cmax = s_scr[e, :, 0:LANES]
                for c in range(1, t // LANES):
                    cmax = jnp.maximum(cmax, s_scr[e, :, LANES * c:LANES * (c + 1)])
                m_old = m_scr[e]
                m_new = jnp.maximum(m_old, jnp.max(cmax, axis=1, keepdims=True))
                alpha_scr[e] = jnp.exp(m_old - m_new)
                m_scr[e] = m_new
                for c in range(t // LANES):
                    cols = slice(LANES * c, LANES * (c + 1))
                    p_scr[e, :, cols] = jnp.exp(s_scr[e, :, cols] - m_new).astype(BF16)

        def accumulate(kb, heads=(0, 1)):
            for e in heads:
                acc_scr[e] = alpha_scr[e] * acc_scr[e] + _mm(p_scr[e], va_ref[e, kv_rows(kb), :])

        for e in range(2):
            def alone(kb, carry, e=e):
                logits(kb, False, (e,))
                probs((e,))
                accumulate(kb, (e,))
                return carry

            lax.fori_loop(starts[e], k0, alone, 0)

        def loop_body(kb, carry):
            logits(kb, False)
            for e in range(2):
                accumulate(kb - 1, (e,))
                probs((e,))
            return carry

        @pl.when(qi > k0)
        def _():
            logits(k0, False)
            probs()

        lax.fori_loop(k0 + 1, qi, loop_body, 0)

        @pl.when(qi > k0)
        def _():
            logits(qi, True)
            accumulate(qi - 1)
            probs()

        @pl.when(qi == k0)
        def _():
            logits(qi, True)
            probs()

        accumulate(qi)

        lane = _iota((t, LANES), 1)
        outs = []
        for e in range(2):
            acc = acc_scr[e]
            l = acc[:, AUG_A:AUG_A + 1]
            outs.append(acc / l)
            lse = m_scr[e][:, 0:1] + jnp.log(l)
            q32 = qa_ref[e].astype(F32)
            c = q32[:, AUG_A:AUG_A + 1] + q32[:, AUG_A + 1:AUG_A + 2] + q32[:, AUG_A + 2:AUG_A + 3]
            qb = jnp.where(lane < HEAD_DIM, q32, 0.0) + _aug(lane, AUG_A, _split3(c - lse)) + _aug(lane, AUG_B)
            qb_ref[e] = qb.astype(BF16)
        o_ref[...] = _pack_pair(outs[0], outs[1], lane)

    grid_spec = pltpu.PrefetchScalarGridSpec(
        num_scalar_prefetch=1, grid=(N_PAIRS, nq),
        in_specs=[pl.BlockSpec((2, t, LANES), lambda j, qi, f: (j, qi, 0)),
                  pl.BlockSpec((2, s, LANES), lambda j, qi, f: (j, 0, 0)),
                  pl.BlockSpec((2, s, LANES), lambda j, qi, f: (j, 0, 0))],
        out_specs=[pl.BlockSpec((t, LANES), lambda j, qi, f: (qi, j)),
                   pl.BlockSpec((2, t, LANES), lambda j, qi, f: (j, qi, 0))],
        scratch_shapes=[pltpu.VMEM((2, t, LANES), F32), pltpu.VMEM((2, t, LANES), F32),
                        pltpu.VMEM((2, t, LANES), F32), pltpu.VMEM((2, t, t), BF16), pltpu.VMEM((2, t, t), F32)])
    return pl.pallas_call(
        body, name="attention_fwd", grid_spec=grid_spec,
        out_shape=(jax.ShapeDtypeStruct((s, ATT_WIDTH), F32), jax.ShapeDtypeStruct((N_HEADS, s, LANES), BF16)),
        compiler_params=_params(("parallel", "parallel")),
    )(first, qa, ka, va)


def attention_bwd(last_q, qb, ka, va, dob):
    s = qb.shape[1]
    t = _blk(s, ATT_BLOCK_BWD)
    tq = _blk(s, ATT_BLOCK_BWD_Q)
    nq = s // tq
    per_q = tq // t

    def body(last_ref, qb_ref, dob_ref, ka_ref, va_ref, dq_ref, dk_ref, dv_ref, dc_ref, dq_scr, dk_scr, dv_scr):
        j, ki = pl.program_id(0), pl.program_id(1)

        @pl.when((j == 0) & (ki == 0))
        def _():
            dc_ref[...] = jnp.zeros_like(dc_ref)

        @pl.when(ki == 0)
        def _():
            dq_scr[...] = jnp.zeros_like(dq_scr)

        dk_scr[...] = jnp.zeros_like(dk_scr)
        dv_scr[...] = jnp.zeros_like(dv_scr)

        def q_step(qblk, masked, heads=(0, 1)):
            rows = pl.ds(pl.multiple_of(qblk * tq, tq), tq)
            scs = [_mm_nt(qb_ref[e, rows, :], ka_ref[e]) for e in heads]
            dps = [_mm_nt(dob_ref[e, rows, :], va_ref[e]) for e in heads]
            for e, sc, dp in zip(heads, scs, dps):
                q = qb_ref[e, rows, :]
                do = dob_ref[e, rows, :]
                if masked:
                    keep = (_iota((tq, t), 0) - _iota((tq, t), 1)) >= ki * t - qblk * tq
                    sc = jnp.where(keep, sc, NEG_BIG)
                p = jnp.exp(sc)
                ds_b = (p * dp).astype(BF16)
                dv_scr[e] += _mm_tn(p.astype(BF16), do)
                dk_scr[e] += _mm_tn(ds_b, q)
                dq_scr[e, rows, :] += _mm(ds_b, ka_ref[e])

        def loop_body(qblk, carry):
            q_step(qblk, False)
            return carry

        ends = [last_ref[2 * j + e, ki] + 1 for e in range(2)]
        both = jnp.minimum(ends[0], ends[1])
        diag = ki // per_q
        q_step(diag, True)
        lax.fori_loop(diag + 1, both, loop_body, 0)
        for e in range(2):
            def alone(qblk, carry, e=e):
                q_step(qblk, False, (e,))
                return carry

            lax.fori_loop(both, ends[e], alone, 0)

        lane = _iota((t, LANES), 1)
        dk_ref[...] = _pack_pair(dk_scr[0], dk_scr[1], lane).astype(BF16)
        dv_ref[...] = _pack_pair(dv_scr[0], dv_scr[1], lane).astype(BF16)
        rows = pl.ds(pl.multiple_of(ki * t, t), t)
        dc_ref[rows, :] -= (jnp.where(lane == N_HEADS + 2 * j, dk_scr[0][:, AUG_B:AUG_B + 1], 0.0)
                            + jnp.where(lane == N_HEADS + 2 * j + 1, dk_scr[1][:, AUG_B:AUG_B + 1], 0.0))

        @pl.when(ki == s // t - 1)
        def _():
            for blk in range(s // t):
                rws = pl.ds(blk * t, t)
                d0 = dq_scr[0, rws, :]
                d1 = dq_scr[1, rws, :]
                dq_ref[rws, :] = (_pack_pair(d0, d1, lane) * ATT_SCALE).astype(BF16)
                dc_ref[rws, :] += (jnp.where(lane == N_HEADS + 2 * j, d0[:, AUG_A:AUG_A + 1], 0.0)
                                   + jnp.where(lane == N_HEADS + 2 * j + 1, d1[:, AUG_A:AUG_A + 1], 0.0))

    full = pl.BlockSpec((2, s, LANES), lambda j, ki, f: (j, 0, 0))
    blk = pl.BlockSpec((2, t, LANES), lambda j, ki, f: (j, ki, 0))
    pair = pl.BlockSpec((t, LANES), lambda j, ki, f: (ki, j))
    wide = jax.ShapeDtypeStruct((s, ATT_WIDTH), BF16)
    grid_spec = pltpu.PrefetchScalarGridSpec(
        num_scalar_prefetch=1, grid=(N_PAIRS, s // t),
        in_specs=[full, full, blk, blk],
        out_specs=[pl.BlockSpec((s, LANES), lambda j, ki, f: (0, j)), pair, pair,
                   pl.BlockSpec((s, LANES), lambda j, ki, f: (0, 0))],
        scratch_shapes=[pltpu.VMEM((2, s, LANES), F32), pltpu.VMEM((2, t, LANES), F32),
                        pltpu.VMEM((2, t, LANES), F32)])
    return pl.pallas_call(
        body, name="attention_bwd", grid_spec=grid_spec,
        out_shape=(wide, wide, wide, jax.ShapeDtypeStruct((s, LANES), F32)),
        compiler_params=_params(("arbitrary", "arbitrary")),
    )(last_q, qb, dob, ka, va)


def _dsilu(z, sg):
    return sg * (1.0 + z * (1.0 - sg))


def post_mix(x, y, zs, o, za, p, tgt, ssd_g, att_g_lane, ple_g, fin_g, w_out, w_gate, w_proj):
    s = x.shape[0]
    tm = _blk(s, 256)
    half = SSD_WIDTH // N_GROUPS

    def rms_bwd(dy, yn, r):
        return r * (dy - yn * jnp.mean(dy * yn, axis=-1, keepdims=True))

    def colsum(a):
        return jnp.sum(a, axis=0, keepdims=True)

    def body(x_ref, y_ref, zs_ref, o_ref, za_ref, p_ref, t_ref, sg_ref, ag_ref, pg_ref, fg_ref,
             wo_ref, wg_ref, wp_ref,
             dh1_ref, dy_ref, dzs_ref, dob_ref, dza_ref, ycat_ref, dh1b_ref, n2b_ref, dglb_ref, dppb_ref, pb_ref,
             loss_ref, dfin_ref, dple_ref, dssd_ref, datt_ref):
        @pl.when(pl.program_id(0) == 0)
        def _():
            for r in (loss_ref, dfin_ref, dple_ref, dssd_ref, datt_ref):
                r[...] = jnp.zeros_like(r)

        lane = _iota((tm, LANES), 1)
        lo = lane < HEAD_DIM
        zs = zs_ref[...]
        sz = _sigmoid(zs)
        yv = y_ref[...]
        ys = yv * (zs * sz)
        yn, rg = [], []
        for g in range(N_GROUPS):
            seg = ys[:, half * g:half * (g + 1)]
            r = lax.rsqrt(jnp.mean(seg * seg, axis=-1, keepdims=True) + EPS)
            yn.append(seg * r)
            rg.append(r)
            ycat_ref[:, half * g:half * (g + 1)] = (yn[g] * sg_ref[:, half * g:half * (g + 1)]).astype(BF16)
        za = za_ref[...]
        sza = _sigmoid(za)
        silu_za = za * sza
        on, ra = [], []
        for jb in range(N_PAIRS):
            blk = o_ref[:, LANES * jb:LANES * (jb + 1)]
            sq = blk * blk
            ms0 = jnp.sum(jnp.where(lo, sq, 0.0), axis=1, keepdims=True) * (1.0 / HEAD_DIM)
            ms1 = jnp.sum(jnp.where(lo, 0.0, sq), axis=1, keepdims=True) * (1.0 / HEAD_DIM)
            r = jnp.where(lo, lax.rsqrt(ms0 + EPS), lax.rsqrt(ms1 + EPS))
            on.append(blk * r)
            ra.append(r)
            an = on[jb] * ag_ref[:, LANES * jb:LANES * (jb + 1)]
            ycat_ref[:, SSD_WIDTH + LANES * jb:SSD_WIDTH + LANES * (jb + 1)] = (
                an * silu_za[:, LANES * jb:LANES * (jb + 1)]).astype(BF16)
        h1 = x_ref[...] + _mm(ycat_ref[...], wo_ref[...])
        r2 = lax.rsqrt(jnp.mean(h1 * h1, axis=-1, keepdims=True) + EPS)
        n2h = h1 * r2
        n2_b = (n2h * pg_ref[...]).astype(BF16)
        gate = _sigmoid(_mm(n2_b, wg_ref[...]))
        p_b = p_ref[...].astype(BF16)
        pp = _mm(p_b, wp_ref[...])
        h2 = h1 + gate * pp
        r3 = lax.rsqrt(jnp.mean(h2 * h2, axis=-1, keepdims=True) + EPS)
        n3 = h2 * r3
        diff = n3 * fg_ref[...] - t_ref[...]
        sq = colsum(diff * diff)
        part = sq[:, 0:LANES]
        for jb in range(1, D_MODEL // LANES):
            part = part + sq[:, LANES * jb:LANES * (jb + 1)]
        loss_ref[...] += part * (0.5 / D_MODEL)
        dout = diff * (1.0 / D_MODEL)
        dfin_ref[...] += colsum(dout * n3)
        dh2 = rms_bwd(dout * fg_ref[...], n3, r3)
        dgl = dh2 * pp * gate * (1.0 - gate)
        dgl_b = dgl.astype(BF16)
        dn2 = _mm_nt(dgl_b, wg_ref[...])
        dple_ref[...] += colsum(dn2 * n2h)
        dh1 = dh2 + rms_bwd(dn2 * pg_ref[...], n2h, r2)
        dh1_b = dh1.astype(BF16)
        dycat = _mm_nt(dh1_b, wo_ref[...])
        dh1_ref[...] = dh1
        dh1b_ref[...] = dh1_b
        n2b_ref[...] = n2_b
        dglb_ref[...] = dgl_b
        dppb_ref[...] = (dh2 * gate).astype(BF16)
        pb_ref[...] = p_b
        for g in range(N_GROUPS):
            cols = slice(half * g, half * (g + 1))
            dys_g = dycat[:, cols]
            dssd_ref[:, cols] += colsum(dys_g * yn[g])
            dys = rms_bwd(dys_g * sg_ref[:, cols], yn[g], rg[g])
            dy_ref[:, cols] = dys * (zs[:, cols] * sz[:, cols])
            dzs_ref[:, cols] = (dys * yv[:, cols] * _dsilu(zs[:, cols], sz[:, cols])).astype(BF16)
        for jb in range(N_PAIRS):
            cols = slice(LANES * jb, LANES * (jb + 1))
            dya = dycat[:, SSD_WIDTH + LANES * jb:SSD_WIDTH + LANES * (jb + 1)]
            ag = ag_ref[:, cols]
            dan = dya * silu_za[:, cols]
            dza_ref[:, cols] = (dya * (on[jb] * ag) * _dsilu(za[:, cols], sza[:, cols])).astype(BF16)
            datt_ref[:, cols] += colsum(dan * on[jb])
            don = dan * ag
            q = don * on[jb]
            m0 = jnp.sum(jnp.where(lo, q, 0.0), axis=1, keepdims=True) * (1.0 / HEAD_DIM)
            m1 = jnp.sum(jnp.where(lo, 0.0, q), axis=1, keepdims=True) * (1.0 / HEAD_DIM)
            do2 = ra[jb] * (don - on[jb] * jnp.where(lo, m0, m1))
            prod = do2 * o_ref[:, cols]
            for e in range(2):
                delta = jnp.sum(jnp.where(lo, prod, 0.0) if e == 0 else jnp.where(lo, 0.0, prod),
                                axis=1, keepdims=True)
                base = jnp.where(lo, do2 if e == 0 else pltpu.roll(do2, HEAD_DIM, 1), 0.0)
                dob_ref[2 * jb + e] = (base - _aug(lane, AUG_A, _split3(delta))).astype(BF16)

    def rows(n, dtype=None):
        return pl.BlockSpec((tm, n), lambda i: (i, 0))

    def out(n, dtype):
        return jax.ShapeDtypeStruct((s, n), dtype)

    vec = _const_spec((1, D_MODEL))
    vshape = jax.ShapeDtypeStruct((1, D_MODEL), F32)
    return pl.pallas_call(
        body, name="post_mix",
        out_shape=(out(D_MODEL, F32), out(SSD_WIDTH, F32), out(SSD_WIDTH, BF16),
                   jax.ShapeDtypeStruct((N_HEADS, s, LANES), BF16),
                   out(ATT_WIDTH, BF16), out(D_INNER, BF16), out(D_MODEL, BF16), out(D_MODEL, BF16),
                   out(D_MODEL, BF16), out(D_MODEL, BF16), out(PLE_DIM, BF16),
                   jax.ShapeDtypeStruct((1, LANES), F32), vshape, vshape, vshape, vshape),
        grid=(s // tm,),
        in_specs=[rows(D_MODEL), rows(SSD_WIDTH), rows(SSD_WIDTH), rows(ATT_WIDTH), rows(ATT_WIDTH),
                  rows(PLE_DIM), rows(D_MODEL), vec, vec, vec, vec,
                  _const_spec((D_INNER, D_MODEL)), _const_spec((D_MODEL, D_MODEL)), _const_spec((PLE_DIM, D_MODEL))],
        out_specs=(rows(D_MODEL), rows(SSD_WIDTH), rows(SSD_WIDTH),
                   pl.BlockSpec((N_HEADS, tm, LANES), lambda i: (0, i, 0)), rows(ATT_WIDTH),
                   rows(D_INNER), rows(D_MODEL), rows(D_MODEL), rows(D_MODEL), rows(D_MODEL), rows(PLE_DIM),
                   _const_spec((1, LANES)), vec, vec, vec, vec),
        compiler_params=_params(("arbitrary",)),
    )(x, y, zs, o, za, p, tgt, ssd_g, att_g_lane, ple_g, fin_g, w_out, w_gate, w_proj)


def in_proj_bwd(dsegs, wsegs, x, g, dh1, pres):
    s = x.shape[0]
    tm = _blk(s, 512)
    nseg = len(dsegs)
    nbig = len(pres)
    nsteps = s // tm

    def body(*refs):
        d_refs = refs[:nseg]
        w_refs = refs[nseg:2 * nseg]
        x_ref, g_ref, dh1_ref = refs[2 * nseg:2 * nseg + 3]
        rest = refs[2 * nseg + 3:]
        pre_refs, (dx_ref, dg_ref), part_refs = rest[:nbig], rest[nbig:nbig + 2], rest[nbig + 2:2 * nbig + 2]
        ssem, rsem, lsem = rest[2 * nbig + 2:]

        @pl.when(pl.program_id(0) == 0)
        def _():
            dg_ref[...] = jnp.zeros_like(dg_ref)
            for cp in scatter_copies(pre_refs, part_refs, ssem, rsem, lsem):
                cp.start()

        @pl.when(pl.program_id(0) == nsteps - 1)
        def _():
            for cp in scatter_copies(pre_refs, part_refs, ssem, rsem, lsem):
                cp.wait()

        du = _mm_nt(d_refs[0][...], w_refs[0][...])
        for k in range(1, nseg):
            du = du + _mm_nt(d_refs[k][...], w_refs[k][...])
        xv = x_ref[...]
        r = lax.rsqrt(jnp.mean(xv * xv, axis=-1, keepdims=True) + EPS)
        xh = xv * r
        dg_ref[...] += jnp.sum(du * xh, axis=0, keepdims=True)
        dxh = du * g_ref[...]
        dx_ref[...] = r * (dxh - xh * jnp.mean(dxh * xh, axis=-1, keepdims=True)) + dh1_ref[...]

    rows = lambda n: pl.BlockSpec((tm, n), lambda i: (i, 0))
    return pl.pallas_call(
        body, name="in_proj_bwd",
        out_shape=tuple([jax.ShapeDtypeStruct((s, D_MODEL), F32), jax.ShapeDtypeStruct((1, D_MODEL), F32)]
                        + [jax.ShapeDtypeStruct(a.shape, a.dtype) for a in pres]),
        grid=(nsteps,),
        in_specs=([rows(d.shape[1]) for d in dsegs] + [_const_spec(w.shape) for w in wsegs]
                  + [rows(D_MODEL), _const_spec((1, D_MODEL)), rows(D_MODEL)] + [ANY] * nbig),
        out_specs=tuple([rows(D_MODEL), _const_spec((1, D_MODEL))] + [ANY] * nbig),
        scratch_shapes=_sems(3 * nbig) + [pltpu.SemaphoreType.DMA((nbig,))],
        compiler_params=_params(("arbitrary",)),
    )(*dsegs, *wsegs, x, g, dh1, *pres)


SMALL_NAMES = ("norm_g", "conv_b", "dt_bias", "a_log", "d_skip", "ssd_norm_g", "fg_bias", "att_norm_g",
               "ple_norm_g", "final_norm_g")
SMALL_SIZES = (1024, 1536, 16, 16, 16, 1024, 16, 64, 1024, 1024)
CONV_W_SIZE = CONV_WIDTH * CONV_CH


def _pack_small(vals):
    flat = jnp.concatenate([v.reshape(-1).astype(F32) for v in vals])
    flat = jnp.pad(flat, (0, SMALL_ROWS * LANES - flat.shape[0]))
    return flat.reshape(SMALL_ROWS, LANES)


def _unpack_small(pack, shapes):
    flat = pack.reshape(-1)
    out, off = [], 0
    for n, shp in zip(SMALL_SIZES, shapes):
        out.append(flat[off:off + n].reshape(shp))
        off += n
    return out


def _row128(v16, offset=0):
    return jnp.pad(v16.reshape(1, N_HEADS).astype(F32), ((0, 0), (offset, LANES - N_HEADS - offset)))


def local_step(prereduce, later, join_later, x, p, tgt, w_in, conv_w, norm_g, conv_b, dt_bias, a_log, d_skip,
               ssd_norm_g, fg_bias, att_norm_g, ple_norm_g, final_norm_g):
    widths = (SSD_WIDTH, CONV_CH, N_HEADS, ATT_WIDTH, ATT_WIDTH, ATT_WIDTH, ATT_WIDTH)
    c0, c1, c2, c3, c4, c5, c6, c7 = [sum(widths[:i]) for i in range(len(widths) + 1)]
    w_zs, w_xbc, w_dt = w_in[:, c0:c1], w_in[:, c1:c2], w_in[:, c2:c3]
    w_za, w_q, w_k, w_v, w_f = w_in[:, c3:c4], w_in[:, c4:c5], w_in[:, c5:c6], w_in[:, c6:c7], w_in[:, c7:]
    w_small = jnp.concatenate([w_dt, w_f, jnp.zeros((D_MODEL, LANES - 2 * N_HEADS), BF16)], axis=1)

    dtb_row = _row128(dt_bias)
    a_row = _row128(-jnp.exp(a_log.astype(F32)))
    fgb_row = _row128(fg_bias, N_HEADS)
    dskip_lane = jnp.repeat(d_skip.astype(F32), HEAD_DIM).reshape(1, SSD_WIDTH)
    att_g_lane = jnp.tile(att_norm_g.astype(F32), N_HEADS).reshape(1, ATT_WIDTH)
    row = lambda v: v.reshape(1, -1).astype(F32)

    u, zs, xbc, za, small = in_proj_fwd(x, row(norm_g), [w_zs, w_xbc, w_za, w_small])
    cum = forget_cumsum(small, fgb_row)
    qa, ka, va, norms, *gathered = proj_qkv_heads(u, w_q, w_k, w_v, cum, later)
    w_out, w_gate, w_proj = join_later(gathered)
    n_seq = x.shape[0]
    first, _ = live_blocks(norms, cum, _blk(n_seq, ATT_BLOCK), _blk(n_seq, ATT_BLOCK))
    _, last_q = live_blocks(norms, cum, _blk(n_seq, ATT_BLOCK_BWD_Q), _blk(n_seq, ATT_BLOCK_BWD))
    pre, xc = conv_fwd(xbc, conv_w, row(conv_b))
    y, states = ssd_fwd(xc, small, dtb_row, a_row, dskip_lane)
    o, qb = attention_fwd(first, qa, ka, va)
    (dh1, dy, dzs, dob, dza, ycat, dh1_b, n2_b, dgl_b, dpp_b, p_b,
     loss_l, dfin, dple, dssd_g, datt_lane) = post_mix(
        x, y, zs, o, za, p, tgt, row(ssd_norm_g), att_g_lane, row(ple_norm_g), row(final_norm_g),
        w_out, w_gate, w_proj)
    dq, dk, dv, dc = attention_bwd(last_q, qb, ka, va, dob)
    dxc, ddt_raw, da, ddtb, ddsk_lane = ssd_bwd(xc, small, states, dy, dtb_row, a_row, dskip_lane)
    dsmall, dfgb = forget_bwd(dc, small, ddt_raw, fgb_row)
    dxbc, dconv_w8, dconv_b = conv_bwd(xbc, pre, dxc, conv_w)
    dsegs = [dzs, dxbc, dza, dq, dk, dv, dsmall]
    wsegs = [w_zs, w_xbc, w_za, w_q, w_k, w_v, w_small]
    dws = [matmul_tn(u, d, "dw_in_%d" % i) for i, d in enumerate(dsegs)]
    dw_in = jnp.concatenate([dws[0], dws[1], dws[6][:, :N_HEADS], dws[2], dws[3], dws[4], dws[5],
                             dws[6][:, N_HEADS:2 * N_HEADS]], axis=1)
    dw_out = matmul_tn(ycat, dh1_b, "dw_out")
    dw_gate = matmul_tn(n2_b, dgl_b, "dw_gate")
    dw_proj = matmul_tn(p_b, dpp_b, "dw_proj")
    dx, dnorm_g, *parts = in_proj_bwd(dsegs, wsegs, x, row(norm_g), dh1, prereduce(dw_in, dw_out, dw_gate, dw_proj))
    small_grads = [
        dnorm_g, dconv_b, ddtb[0, :N_HEADS], (da * a_row)[0, :N_HEADS],
        ddsk_lane.reshape(N_HEADS, HEAD_DIM).sum(axis=1), dssd_g, dfgb[0, N_HEADS:2 * N_HEADS],
        datt_lane.reshape(N_HEADS, HEAD_DIM).sum(axis=0), dple, dfin]
    loss = jnp.sum(loss_l)
    return loss, dx, parts, dconv_w8[:CONV_WIDTH], small_grads


def kernel(x, p, norm_g, w_in, conv_w, conv_b, dt_bias, a_log, d_skip, ssd_norm_g, fg_bias, att_norm_g, w_out, ple_norm_g, w_ple_gate, w_ple_proj, final_norm_g, loss_target, m_norm_g, m_w_in, m_conv_w, m_conv_b, m_dt_bias, m_a_log, m_d_skip, m_ssd_norm_g, m_fg_bias, m_att_norm_g, m_w_out, m_ple_norm_g, m_w_ple_gate, m_w_ple_proj, m_final_norm_g, v_norm_g, v_w_in, v_conv_w, v_conv_b, v_dt_bias, v_a_log, v_d_skip, v_ssd_norm_g, v_fg_bias, v_att_norm_g, v_w_out, v_ple_norm_g, v_w_ple_gate, v_w_ple_proj, v_final_norm_g):
    chip = 2 * lax.axis_index("x") + lax.axis_index("y")
    core = lax.axis_index("c")

    big_w = [w_in[0], w_out[0], w_ple_gate[0], w_ple_proj[0]]
    own = [a.astype(BF16) for a in big_w] + [conv_w[0]]

    def joined(mine, gathered, axis):
        return jnp.concatenate([jnp.where(chip == j, mine, gathered[j]) for j in range(N_CHIPS)], axis=axis)

    w_in_all, conv_all = gather_weights(own[:1], own[4])
    w_in_f, conv_w_f = joined(own[0], w_in_all, 1), joined(own[4], conv_all, 1)

    def join_later(gathered):
        return [joined(mine, got, axis) for mine, got, axis in zip(own[1:4], gathered, (0, 0, 1))]

    core1 = core.reshape(1).astype(jnp.int32)

    def prereduce(dw_in, dw_out, dw_gate, dw_proj):
        n_in, n_proj = w_in.shape[2], w_ple_proj.shape[2]
        gs = [jnp.stack([dw_in[:, n_in * j:n_in * (j + 1)] for j in range(N_CHIPS)]),
              dw_out.reshape(N_CHIPS, w_out.shape[1], D_MODEL), dw_gate.reshape(N_CHIPS, w_ple_gate.shape[1], D_MODEL),
              jnp.stack([dw_proj[:, n_proj * j:n_proj * (j + 1)] for j in range(N_CHIPS)])]
        return add_halves(core1, gs, halves_to_sibling(gs))

    smalls_w = [norm_g, conv_b, dt_bias, a_log, d_skip, ssd_norm_g, fg_bias, att_norm_g, ple_norm_g, final_norm_g]
    loss_l, dx, parts, dconv_w, small_grads = local_step(
        prereduce, own[1:4], join_later, x[0], p[0, 0], loss_target[0], w_in_f, conv_w_f,
        *[a.reshape(-1) for a in smalls_w])
    loss = lax.psum(loss_l, ("x", "y", "c"))
    smalls = gather_small(_pack_small(list(small_grads) + [dconv_w]))
    mine = sum_parts(parts)

    g_big, d_big, m_big, v_big = adamw_big(
        core1, mine, swap_halves(mine), big_w, [m_w_in[0], m_w_out[0], m_w_ple_gate[0], m_w_ple_proj[0]],
        [v_w_in[0], v_w_out[0], v_w_ple_gate[0], v_w_ple_proj[0]])
    smalls_m = [m_norm_g, m_conv_b, m_dt_bias, m_a_log, m_d_skip, m_ssd_norm_g, m_fg_bias, m_att_norm_g,
                m_ple_norm_g, m_final_norm_g]
    smalls_v = [v_norm_g, v_conv_b, v_dt_bias, v_a_log, v_d_skip, v_ssd_norm_g, v_fg_bias, v_att_norm_g,
                v_ple_norm_g, v_final_norm_g]
    g_sm, d_sm, m_sm, v_sm = adamw_small(smalls, _pack_small(smalls_w), _pack_small(smalls_m), _pack_small(smalls_v))
    n_small = sum(SMALL_SIZES)
    g_conv_full = g_sm.reshape(-1)[n_small:n_small + CONV_W_SIZE].reshape(CONV_WIDTH, CONV_CH)
    n_conv = conv_w.shape[2]
    g_conv = lax.dynamic_slice_in_dim(g_conv_full, chip * n_conv, n_conv, axis=1)
    d_conv, m_conv, v_conv = adamw_whole(g_conv, conv_w[0], m_conv_w[0], v_conv_w[0], "adamw_conv")

    shapes = [a.shape for a in smalls_w]
    outs = []
    for big, conv, sm in ((g_big, g_conv, g_sm), (d_big, d_conv, d_sm), (m_big, m_conv, m_sm), (v_big, v_conv, v_sm)):
        b_in, b_out, b_gate, b_proj = [a[None] for a in big]
        s_norm, s_convb, s_dtb, s_alog, s_dsk, s_ssdg, s_fgb, s_attg, s_pleg, s_fin = _unpack_small(sm, shapes)
        outs.extend([s_norm, b_in, conv[None], s_convb, s_dtb, s_alog, s_dsk, s_ssdg, s_fgb, s_attg, b_out, s_pleg,
                     b_gate, b_proj, s_fin])
    return (loss, dx[None], *outs)
```

```python
import functools

import jax
import jax.numpy as jnp
from jax import lax
from jax.experimental import pallas as pl
from jax.experimental.pallas import tpu as pltpu

F32 = jnp.float32
BF16 = jnp.bfloat16

D_MODEL = 1024
SSD_WIDTH = 1024
ATT_WIDTH = 1024
N_HEADS = 16
HEAD_DIM = 64
N_GROUPS = 2
D_STATE = 128
CONV_CH = 1536
CONV_WIDTH = 4
CHUNK = 128
PLE_DIM = 256
D_INNER = 2048
EPS = 1e-6
IN_COLS = 6688
N_CHIPS = 4
N_DEV = 8
LANES = 128
N_PAIRS = 8

ADAM_LR = 0.001
ADAM_B1 = 0.9
ADAM_B2 = 0.999
ADAM_EPS = 1e-08
ADAM_WD = 0.01
ADAM_STEP = 10

SMALL_ROWS = 96

NEG_BIG = -1e30
VMEM_LIMIT = 56 * 1024 * 1024

MESH = pl.DeviceIdType.MESH
ANY = pl.BlockSpec(memory_space=pl.ANY)


def _mm(a, b):
    return jnp.dot(a, b, preferred_element_type=F32)


def _mm_nt(a, b):
    return lax.dot_general(a, b, (((1,), (1,)), ((), ())), preferred_element_type=F32)


def _mm_tn(a, b):
    return lax.dot_general(a, b, (((0,), (0,)), ((), ())), preferred_element_type=F32)


def _mm_exact(a, b):
    return jnp.dot(a, b, preferred_element_type=F32, precision=lax.Precision.HIGHEST)


def _softplus(x):
    return jnp.maximum(x, 0.0) + jnp.log1p(jnp.exp(-jnp.abs(x)))


def _sigmoid(x):
    return jax.nn.sigmoid(x)


def _iota(shape, dim):
    return lax.broadcasted_iota(jnp.int32, shape, dim)


def _params(sem=None):
    return pltpu.CompilerParams(dimension_semantics=sem, vmem_limit_bytes=VMEM_LIMIT)


def _blk(n, pref):
    return min(n, pref)


def _const_spec(shape):
    nd = len(shape)
    return pl.BlockSpec(shape, lambda *_: (0,) * nd)


def _chip_peers():
    x, y, c = lax.axis_index("x"), lax.axis_index("y"), lax.axis_index("c")
    return x, y, c, [(1 - x, y, c), (x, 1 - y, c), (1 - x, 1 - y, c)]


def _half(rows, c):
    h = rows // 2
    return pl.ds(pl.multiple_of(c * h, 8), h)


def _sems(n):
    return [pltpu.SemaphoreType.DMA((n,)), pltpu.SemaphoreType.DMA((n,))]


def gather_copies(ins, outs, ssem1, rsem1, ssem2, rsem2):
    n = len(ins)
    x, y, c, peers = _chip_peers()
    me = 2 * x + y
    fetched, passed = [], []
    for k, peer in enumerate(peers):
        chip = 2 * peer[0] + peer[1]
        for i in range(n):
            h = _half(ins[i].shape[0], c)
            fetched.append(pltpu.make_async_remote_copy(
                src_ref=ins[i].at[h], dst_ref=outs[i].at[me, h], send_sem=ssem1.at[n * k + i],
                recv_sem=rsem1.at[n * k + i], device_id=peer, device_id_type=MESH))
            passed.append(pltpu.make_async_remote_copy(
                src_ref=outs[i].at[chip, h], dst_ref=outs[i].at[chip, h], send_sem=ssem2.at[n * k + i],
                recv_sem=rsem2.at[n * k + i], device_id=(x, y, 1 - c), device_id_type=MESH))
    return fetched, passed


def gather_weights(shards, conv_s):
    n = len(shards)

    def body(*refs):
        ins, conv_in = refs[:n], refs[n]
        outs, conv_out = refs[n + 1:2 * n + 1], refs[2 * n + 1]
        ssem1, rsem1, ssem2, rsem2, c_ssem, c_rsem = refs[2 * n + 2:]
        x, y, _, peers = _chip_peers()
        fetched, passed = gather_copies(ins, outs, ssem1, rsem1, ssem2, rsem2)
        small = [pltpu.make_async_remote_copy(
            src_ref=conv_in, dst_ref=conv_out.at[2 * x + y], send_sem=c_ssem.at[k], recv_sem=c_rsem.at[k],
            device_id=peer, device_id_type=MESH) for k, peer in enumerate(peers)]
        for cp in fetched + small:
            cp.start()
        for landed, onward in zip(fetched, passed):
            landed.wait_recv()
            onward.start()
        for cp in passed:
            cp.wait_recv()
        for cp in fetched + passed:
            cp.wait_send()
        for cp in small:
            cp.wait()

    return pl.pallas_call(
        body, name="gather_weights",
        out_shape=tuple(jax.ShapeDtypeStruct((N_CHIPS,) + a.shape, a.dtype) for a in list(shards) + [conv_s]),
        in_specs=[ANY] * (n + 1), out_specs=(ANY,) * (n + 1),
        scratch_shapes=_sems(3 * n) + _sems(3 * n) + _sems(3),
    )(*shards, conv_s)


def halves_to_sibling(gs):
    n = len(gs)

    def body(*refs):
        ins, outs = refs[:n], refs[n:2 * n]
        ssem, rsem = refs[2 * n:]
        x, y, c = lax.axis_index("x"), lax.axis_index("y"), lax.axis_index("c")
        copies = []
        for i in range(n):
            for j in range(N_CHIPS):
                copies.append(pltpu.make_async_remote_copy(
                    src_ref=ins[i].at[j, _half(ins[i].shape[1], 1 - c)], dst_ref=outs[i].at[j],
                    send_sem=ssem.at[N_CHIPS * i + j], recv_sem=rsem.at[N_CHIPS * i + j],
                    device_id=(x, y, 1 - c), device_id_type=MESH))
        for cp in copies:
            cp.start()
        for cp in copies:
            cp.wait()

    return pl.pallas_call(
        body, name="halves_to_sibling",
        out_shape=tuple(jax.ShapeDtypeStruct((N_CHIPS, g.shape[1] // 2, g.shape[2]), F32) for g in gs),
        in_specs=[ANY] * n, out_specs=(ANY,) * n, scratch_shapes=_sems(N_CHIPS * n),
    )(*gs)


RED_GRID = 8


def add_halves(core, gs, rbs):
    n = len(gs)

    def body(c_ref, *refs):
        for i in range(n):
            refs[2 * n + i][...] = (refs[i][...] + refs[n + i][...]).astype(BF16)

    def blk(g):
        return (1, g.shape[1] // 2 // RED_GRID, g.shape[2])

    grid_spec = pltpu.PrefetchScalarGridSpec(
        num_scalar_prefetch=1, grid=(N_CHIPS, RED_GRID),
        in_specs=([pl.BlockSpec(blk(g), lambda j, b, c_ref: (j, c_ref[0] * RED_GRID + b, 0)) for g in gs]
                  + [pl.BlockSpec(blk(g), lambda j, b, c_ref: (j, b, 0)) for g in gs]),
        out_specs=[pl.BlockSpec(blk(g), lambda j, b, c_ref: (j, b, 0)) for g in gs])
    return pl.pallas_call(
        body, name="add_halves", grid_spec=grid_spec,
        out_shape=tuple(jax.ShapeDtypeStruct(r.shape, BF16) for r in rbs),
        compiler_params=_params(("parallel", "parallel")),
    )(core, *gs, *rbs)


def scatter_copies(ins, outs, ssem, rsem, lsem):
    n = len(ins)
    x, y, _, peers = _chip_peers()
    me = 2 * x + y
    copies = [pltpu.make_async_copy(ins[i].at[me], outs[i].at[me], lsem.at[i]) for i in range(n)]
    for k, peer in enumerate(peers):
        dst_chip = 2 * peer[0] + peer[1]
        for i in range(n):
            copies.append(pltpu.make_async_remote_copy(
                src_ref=ins[i].at[dst_chip], dst_ref=outs[i].at[me], send_sem=ssem.at[n * k + i],
                recv_sem=rsem.at[n * k + i], device_id=peer, device_id_type=MESH))
    return copies


def gather_small(small):
    def body(s_ref, smalls_ref, ssem, rsem, lsem):
        x, y, c = lax.axis_index("x"), lax.axis_index("y"), lax.axis_index("c")
        dev = 4 * x + 2 * y + c
        copies = [pltpu.make_async_copy(s_ref, smalls_ref.at[dev], lsem)]
        for k in range(1, N_DEV):
            fx, fy, fc = (k >> 2) & 1, (k >> 1) & 1, k & 1
            peer = ((1 - x) if fx else x, (1 - y) if fy else y, (1 - c) if fc else c)
            copies.append(pltpu.make_async_remote_copy(
                src_ref=s_ref, dst_ref=smalls_ref.at[dev], send_sem=ssem.at[k - 1], recv_sem=rsem.at[k - 1],
                device_id=peer, device_id_type=MESH))
        for cp in copies:
            cp.start()
        for cp in copies:
            cp.wait()

    return pl.pallas_call(
        body, name="gather_small",
        out_shape=jax.ShapeDtypeStruct((N_DEV,) + small.shape, F32),
        in_specs=[ANY], out_specs=ANY,
        scratch_shapes=_sems(N_DEV - 1) + [pltpu.SemaphoreType.DMA],
    )(small)


def sum_parts(parts):
    n = len(parts)

    def body(*refs):
        for i in range(n):
            p_ref = refs[i]
            refs[n + i][...] = ((p_ref[0].astype(F32) + p_ref[1].astype(F32)) + p_ref[2].astype(F32)
                                ) + p_ref[3].astype(F32)

    def rows(p):
        return p.shape[1] // RED_GRID

    return pl.pallas_call(
        body, name="sum_parts",
        out_shape=tuple(jax.ShapeDtypeStruct(p.shape[1:], F32) for p in parts),
        grid=(RED_GRID,),
        in_specs=[pl.BlockSpec((N_CHIPS, rows(p), p.shape[2]), lambda b: (0, b, 0)) for p in parts],
        out_specs=tuple(pl.BlockSpec((rows(p), p.shape[2]), lambda b: (b, 0)) for p in parts),
        compiler_params=_params(("parallel",)),
    )(*parts)


def swap_halves(reds):
    n = len(reds)

    def body(*refs):
        ins, outs = refs[:n], refs[n:2 * n]
        ssem, rsem = refs[2 * n:]
        x, y, c = lax.axis_index("x"), lax.axis_index("y"), lax.axis_index("c")
        copies = [pltpu.make_async_remote_copy(
            src_ref=ins[i], dst_ref=outs[i], send_sem=ssem.at[i], recv_sem=rsem.at[i],
            device_id=(x, y, 1 - c), device_id_type=MESH) for i in range(n)]
        for cp in copies:
            cp.start()
        for cp in copies:
            cp.wait()

    return pl.pallas_call(
        body, name="swap_halves",
        out_shape=tuple(jax.ShapeDtypeStruct(r.shape, F32) for r in reds),
        in_specs=[ANY] * n, out_specs=(ANY,) * n, scratch_shapes=_sems(n),
    )(*reds)


def _adamw(w, g, m, v):
    m = ADAM_B1 * m + (1.0 - ADAM_B1) * g
    v = ADAM_B2 * v + (1.0 - ADAM_B2) * (g * g)
    m_hat = m / (1.0 - ADAM_B1 ** ADAM_STEP)
    v_hat = v / (1.0 - ADAM_B2 ** ADAM_STEP)
    delta = -ADAM_LR * (m_hat / (jnp.sqrt(v_hat) + ADAM_EPS) + ADAM_WD * w)
    return delta, m, v


def adamw_big(core, mine, theirs, ws, ms, vs):
    n = len(ws)
    per_half = RED_GRID // 2

    def body(c_ref, *refs):
        own = (pl.program_id(0) // per_half) == c_ref[0]
        for i in range(n):
            g = jnp.where(own, refs[i][...], refs[n + i][...])
            d, mn, vn = _adamw(refs[2 * n + i][...], g, refs[3 * n + i][...], refs[4 * n + i][...])
            refs[5 * n + i][...] = g
            refs[6 * n + i][...] = d
            refs[7 * n + i][...] = mn
            refs[8 * n + i][...] = vn

    def blk(w):
        return (w.shape[0] // RED_GRID, w.shape[1])

    halves = [pl.BlockSpec(blk(w), lambda b, c_ref: (b % per_half, 0)) for w in ws]
    whole = [pl.BlockSpec(blk(w), lambda b, c_ref: (b, 0)) for w in ws]
    shapes = [jax.ShapeDtypeStruct(w.shape, F32) for w in ws]
    grid_spec = pltpu.PrefetchScalarGridSpec(
        num_scalar_prefetch=1, grid=(RED_GRID,), in_specs=halves * 2 + whole * 3, out_specs=whole * 4)
    outs = pl.pallas_call(
        body, name="adamw_big", out_shape=tuple(shapes * 4), grid_spec=grid_spec,
        compiler_params=_params(("parallel",)),
    )(core, *mine, *theirs, *ws, *ms, *vs)
    return outs[:n], outs[n:2 * n], outs[2 * n:3 * n], outs[3 * n:]


def adamw_whole(g, w, m, v, name):
    def body(g_ref, w_ref, m_ref, v_ref, d_out, m_out, v_out):
        d, mn, vn = _adamw(w_ref[...], g_ref[...], m_ref[...], v_ref[...])
        d_out[...] = d
        m_out[...] = mn
        v_out[...] = vn

    shp = jax.ShapeDtypeStruct(g.shape, F32)
    return pl.pallas_call(body, name=name, out_shape=(shp,) * 3)(g, w, m, v)


def adamw_small(smalls, w, m, v):
    def body(s_ref, w_ref, m_ref, v_ref, g_out, d_out, m_out, v_out):
        g = s_ref[0]
        for k in range(1, N_DEV):
            g = g + s_ref[k]
        d, mn, vn = _adamw(w_ref[...], g, m_ref[...], v_ref[...])
        g_out[...] = g
        d_out[...] = d
        m_out[...] = mn
        v_out[...] = vn

    shp = jax.ShapeDtypeStruct((SMALL_ROWS, LANES), F32)
    return pl.pallas_call(body, name="adamw_small", out_shape=(shp,) * 4)(smalls, w, m, v)


def in_proj_fwd(x, g, ws):
    s = x.shape[0]
    tm = _blk(s, 512)
    n = len(ws)

    def body(x_ref, g_ref, *refs):
        xv = x_ref[...]
        r = lax.rsqrt(jnp.mean(xv * xv, axis=-1, keepdims=True) + EPS)
        u = (xv * r * g_ref[...]).astype(BF16)
        refs[n][...] = u
        for i in range(n):
            refs[n + 1 + i][...] = _mm(u, refs[i][...])

    rows = lambda width: pl.BlockSpec((tm, width), lambda i: (i, 0))
    return pl.pallas_call(
        body, name="in_proj_fwd",
        out_shape=tuple([jax.ShapeDtypeStruct((s, D_MODEL), BF16)]
                        + [jax.ShapeDtypeStruct((s, w.shape[1]), F32) for w in ws]),
        grid=(s // tm,),
        in_specs=[rows(D_MODEL), _const_spec((1, D_MODEL))] + [_const_spec(w.shape) for w in ws],
        out_specs=tuple([rows(D_MODEL)] + [rows(w.shape[1]) for w in ws]),
        compiler_params=_params(("parallel",)),
    )(x, g, *ws)


def matmul_tn(a, b, name):
    s, m = a.shape
    n = b.shape[1]
    tk = _blk(s, 2048)
    tn = _blk(n, 512) if m > D_MODEL else (n // 2 if n > D_MODEL else n)

    def body(a_ref, b_ref, o_ref):
        @pl.when(pl.program_id(1) == 0)
        def _():
            o_ref[...] = jnp.zeros_like(o_ref)

        o_ref[...] += _mm_tn(a_ref[...], b_ref[...])

    return pl.pallas_call(
        body, name=name, out_shape=jax.ShapeDtypeStruct((m, n), F32), grid=(n // tn, s // tk),
        in_specs=[pl.BlockSpec((tk, m), lambda j, i: (i, 0)), pl.BlockSpec((tk, tn), lambda j, i: (i, j))],
        out_specs=pl.BlockSpec((m, tn), lambda j, i: (0, j)),
        compiler_params=_params(("parallel", "arbitrary")),
    )(a, b)


def conv_fwd(xbc, w, b):
    s = xbc.shape[0]
    tm = _blk(s, 256)

    def body(x_ref, t_ref, w_ref, b_ref, pre_ref, act_ref):
        i = pl.program_id(0)
        row8 = _iota((8, LANES), 0)
        for c0 in range(0, CONV_CH, LANES):
            cols = slice(c0, c0 + LANES)
            cur = x_ref[:, cols]
            tail = jnp.where(i > 0, t_ref[:, cols], 0.0)
            wv = w_ref[:, cols]
            bias = b_ref[:, cols]
            acc = cur * wv[3:4, :] + bias
            head = cur[0:8, :] * wv[3:4, :] + bias
            for sh in range(1, CONV_WIDTH):
                wk = wv[3 - sh:4 - sh, :]
                acc = acc + pltpu.roll(cur, sh, 0) * wk
                first = jnp.where(row8 < sh, pltpu.roll(tail, sh, 0), pltpu.roll(cur[0:8, :], sh, 0))
                head = head + first * wk
            pre_ref[:, cols] = acc
            act_ref[:, cols] = acc * _sigmoid(acc)
            pre_ref[0:8, cols] = head
            act_ref[0:8, cols] = head * _sigmoid(head)

    shp = jax.ShapeDtypeStruct(xbc.shape, F32)
    rows = pl.BlockSpec((tm, CONV_CH), lambda i: (i, 0))
    return pl.pallas_call(
        body, name="conv_fwd", out_shape=(shp, shp), grid=(s // tm,),
        in_specs=[rows, pl.BlockSpec((8, CONV_CH), lambda i: (jnp.maximum(i * (tm // 8) - 1, 0), 0)),
                  _const_spec((CONV_WIDTH, CONV_CH)), _const_spec((1, CONV_CH))],
        out_specs=(rows, rows), compiler_params=_params(("parallel",)),
    )(xbc, xbc, w, b)


def conv_bwd(xbc, pre, dact, w):
    s = xbc.shape[0]
    tm = _blk(s, 256)
    nb = s // tm

    def dsilu(p):
        sg = _sigmoid(p)
        return sg * (1.0 + p * (1.0 - sg))

    def body(x_ref, xt_ref, p_ref, pn_ref, d_ref, dn_ref, w_ref, dx_ref, dw_ref, db_ref):
        i = pl.program_id(0)

        @pl.when(i == 0)
        def _():
            dw_ref[...] = jnp.zeros_like(dw_ref)
            db_ref[...] = jnp.zeros_like(db_ref)

        row8 = _iota((8, LANES), 0)
        for c0 in range(0, CONV_CH, LANES):
            cols = slice(c0, c0 + LANES)
            wv = w_ref[:, cols]
            dpre = d_ref[:, cols] * dsilu(p_ref[:, cols])
            dnext = jnp.where(i < nb - 1, dn_ref[:, cols] * dsilu(pn_ref[:, cols]), 0.0)
            cur = x_ref[:, cols]
            tail = jnp.where(i > 0, xt_ref[:, cols], 0.0)
            dx = dpre * wv[3:4, :]
            last = dpre[tm - 8:tm, :] * wv[3:4, :]
            db_ref[:, cols] += jnp.sum(dpre, axis=0, keepdims=True)
            dws = [jnp.sum(dpre * cur, axis=0, keepdims=True)]
            for sh in range(1, CONV_WIDTH):
                wk = wv[3 - sh:4 - sh, :]
                dx = dx + pltpu.roll(dpre, tm - sh, 0) * wk
                nxt = jnp.where(row8 >= 8 - sh, pltpu.roll(dnext, 8 - sh, 0),
                                pltpu.roll(dpre[tm - 8:tm, :], 8 - sh, 0))
                last = last + nxt * wk
                xs = pltpu.roll(cur, sh, 0)
                first = jnp.where(row8 < sh, pltpu.roll(tail, sh, 0), xs[0:8, :])
                dws.append(jnp.sum(dpre * xs, axis=0, keepdims=True)
                           + jnp.sum(dpre[0:8, :] * (first - xs[0:8, :]), axis=0, keepdims=True))
            dx_ref[:, cols] = dx.astype(BF16)
            dx_ref[tm - 8:tm, cols] = last.astype(BF16)
            for sh in range(CONV_WIDTH):
                dw_ref[3 - sh:4 - sh, cols] += dws[sh]

    rows = pl.BlockSpec((tm, CONV_CH), lambda i: (i, 0))
    prev8 = pl.BlockSpec((8, CONV_CH), lambda i: (jnp.maximum(i * (tm // 8) - 1, 0), 0))
    next8 = pl.BlockSpec((8, CONV_CH), lambda i: (jnp.minimum((i + 1) * (tm // 8), s // 8 - 1), 0))
    return pl.pallas_call(
        body, name="conv_bwd",
        out_shape=(jax.ShapeDtypeStruct(xbc.shape, BF16), jax.ShapeDtypeStruct((8, CONV_CH), F32),
                   jax.ShapeDtypeStruct((1, CONV_CH), F32)),
        grid=(nb,),
        in_specs=[rows, prev8, rows, next8, rows, next8, _const_spec((CONV_WIDTH, CONV_CH))],
        out_specs=(rows, _const_spec((8, CONV_CH)), _const_spec((1, CONV_CH))),
        compiler_params=_params(("arbitrary",)),
    )(xbc, xbc, pre, pre, dact, dact, w)


def _pair_lanes(mat, j, lane):
    return jnp.where(lane < HEAD_DIM, mat[:, 2 * j:2 * j + 1], mat[:, 2 * j + 1:2 * j + 2])


def _ssd_chunk_prelude(sm, dtb, a_row, lane, sub):
    raw = sm + dtb
    head_lane = lane < N_HEADS
    dt = jnp.where(head_lane, _softplus(raw), 0.0)
    sig = jnp.where(head_lane, _sigmoid(raw), 0.0)
    tri = (lane <= sub).astype(F32)
    acs = _mm_exact(tri, dt * a_row)
    return dt, sig, acs, acs.T


GROUP_WIDTH = SSD_WIDTH // N_GROUPS
HEADS_PER_GROUP = N_HEADS // N_GROUPS


def _expand_group(mat, g, lane):
    return jnp.concatenate([_pair_lanes(mat, j, lane) for j in range(4 * g, 4 * g + 4)], axis=1)


def _head_sums(q, g):
    row = _iota((GROUP_WIDTH, LANES), 0)
    seg = (_iota((GROUP_WIDTH, LANES), 1) == HEADS_PER_GROUP * g + (row >> 6)).astype(BF16)
    hi = q.astype(BF16)
    lo = (q - hi.astype(F32)).astype(BF16)
    return _mm(hi, seg) + _mm(lo, seg)


def _rows_from_lanes(row512):
    return jnp.broadcast_to(row512, (LANES, GROUP_WIDTH)).T


def ssd_fwd(xc, small, dtb_row, a_row, dskip_lane):
    s = xc.shape[0]
    nc = s // CHUNK

    def body(xc_ref, sm_ref, dtb_ref, a_ref, dsk_ref, y_ref, hs_ref, h_scr):
        c = pl.program_id(0)

        @pl.when(c == 0)
        def _():
            h_scr[...] = jnp.zeros_like(h_scr)

        lane = _iota((CHUNK, LANES), 1)
        sub = _iota((CHUNK, LANES), 0)
        causal = lane <= sub
        dt, _, acs, acs_t = _ssd_chunk_prelude(sm_ref[...], dtb_ref[...], a_ref[...], lane, sub)
        for g in range(N_GROUPS):
            cols = slice(GROUP_WIDTH * g, GROUP_WIDTH * (g + 1))
            b_off = SSD_WIDTH + D_STATE * g
            c_off = SSD_WIDTH + N_GROUPS * D_STATE + D_STATE * g
            b_b = xc_ref[:, b_off:b_off + D_STATE].astype(BF16)
            c_b = xc_ref[:, c_off:c_off + D_STATE].astype(BF16)
            cb = _mm_nt(c_b, b_b)
            x_g = xc_ref[:, cols]
            acs_g = _expand_group(acs, g, lane)
            xdt_g = x_g * _expand_group(dt, g, lane)
            xdt_b = xdt_g.astype(BF16)
            heads = range(HEADS_PER_GROUP * g, HEADS_PER_GROUP * (g + 1))
            m_b = [(cb * jnp.exp(jnp.where(causal, acs[:, h:h + 1] - acs_t[h:h + 1, :], NEG_BIG))).astype(BF16)
                   for h in heads]
            yd = [_mm(m_b[k], xdt_b[:, LANES * (k // 2):LANES * (k // 2 + 1)]) for k in range(HEADS_PER_GROUP)]
            yd_g = jnp.concatenate([jnp.where(lane < HEAD_DIM, yd[2 * k], yd[2 * k + 1]) for k in range(4)], axis=1)
            h_g = h_scr[g]
            t_g = _mm_nt(c_b, h_g.astype(BF16))
            y_ref[:, cols] = yd_g + jnp.exp(acs_g) * t_g + dsk_ref[:, cols] * x_g
            hs_ref[0, g] = h_g
            last_g = acs_g[CHUNK - 1:CHUNK, :]
            w_b = (xdt_g * jnp.exp(last_g - acs_g)).astype(BF16)
            h_scr[g] = h_g * jnp.exp(_rows_from_lanes(last_g)) + _mm_tn(w_b, b_b)

    return pl.pallas_call(
        body, name="ssd_fwd",
        out_shape=(jax.ShapeDtypeStruct((s, SSD_WIDTH), F32),
                   jax.ShapeDtypeStruct((nc, N_GROUPS, GROUP_WIDTH, D_STATE), F32)),
        grid=(nc,),
        in_specs=[pl.BlockSpec((CHUNK, CONV_CH), lambda c: (c, 0)), pl.BlockSpec((CHUNK, LANES), lambda c: (c, 0)),
                  _const_spec((1, LANES)), _const_spec((1, LANES)), _const_spec((1, SSD_WIDTH))],
        out_specs=(pl.BlockSpec((CHUNK, SSD_WIDTH), lambda c: (c, 0)),
                   pl.BlockSpec((1, N_GROUPS, GROUP_WIDTH, D_STATE), lambda c: (c, 0, 0, 0))),
        scratch_shapes=[pltpu.VMEM((N_GROUPS, GROUP_WIDTH, D_STATE), F32)],
        compiler_params=_params(("arbitrary",)),
    )(xc, small, dtb_row, a_row, dskip_lane)


def ssd_bwd(xc, small, states, dy, dtb_row, a_row, dskip_lane):
    s = xc.shape[0]
    nc = s // CHUNK
    rev = lambda c: nc - 1 - c

    def body(xc_ref, sm_ref, hs_ref, dy_ref, dtb_ref, a_ref, dsk_ref,
             dxc_ref, ddt_ref, da_ref, ddtb_ref, ddsk_ref, dh_scr):
        c = pl.program_id(0)

        @pl.when(c == 0)
        def _():
            dh_scr[...] = jnp.zeros_like(dh_scr)
            da_ref[...] = jnp.zeros_like(da_ref)
            ddtb_ref[...] = jnp.zeros_like(ddtb_ref)
            ddsk_ref[...] = jnp.zeros_like(ddsk_ref)

        lane = _iota((CHUNK, LANES), 1)
        sub = _iota((CHUNK, LANES), 0)
        causal = lane <= sub
        upper = lane >= sub
        is_last = sub == CHUNK - 1
        a_row_v = a_ref[...]
        dt, sig, acs, acs_t = _ssd_chunk_prelude(sm_ref[...], dtb_ref[...], a_row_v, lane, sub)
        cd = jnp.exp(acs[CHUNK - 1:CHUNK, :])
        dacs_c = jnp.zeros((CHUNK, LANES), F32)
        dacs_r = jnp.zeros((LANES, CHUNK), F32)
        ddtx = jnp.zeros((CHUNK, LANES), F32)
        for g in range(N_GROUPS):
            cols = slice(GROUP_WIDTH * g, GROUP_WIDTH * (g + 1))
            b_off = SSD_WIDTH + D_STATE * g
            c_off = SSD_WIDTH + N_GROUPS * D_STATE + D_STATE * g
            b_b = xc_ref[:, b_off:b_off + D_STATE].astype(BF16)
            c_b = xc_ref[:, c_off:c_off + D_STATE].astype(BF16)
            cb = _mm_nt(c_b, b_b)
            cb_t = _mm_nt(b_b, c_b)
            x_g = xc_ref[:, cols]
            dy_g = dy_ref[:, cols]
            dt_g = _expand_group(dt, g, lane)
            acs_g = _expand_group(acs, g, lane)
            last_g = acs_g[CHUNK - 1:CHUNK, :]
            e_g = jnp.exp(acs_g)
            dte_g = jnp.exp(last_g - acs_g)
            xdt_g = x_g * dt_g
            xdt_b = xdt_g.astype(BF16)
            h_g = hs_ref[0, g]
            dh_g = dh_scr[g]
            h_b = h_g.astype(BF16)
            dh_b = dh_g.astype(BF16)
            heads = list(range(HEADS_PER_GROUP * g, HEADS_PER_GROUP * (g + 1)))
            segs = [acs[:, h:h + 1] - acs_t[h:h + 1, :] for h in heads]
            lms = [jnp.exp(jnp.where(causal, sg, NEG_BIG)) for sg in segs]
            mts = [(cb_t * jnp.exp(jnp.where(upper, -sg, NEG_BIG))).astype(BF16) for sg in segs]
            dyh = []
            for k in range(HEADS_PER_GROUP):
                blk = dy_g[:, LANES * (k // 2):LANES * (k // 2 + 1)]
                in_head = (lane < HEAD_DIM) if k % 2 == 0 else (lane >= HEAD_DIM)
                dyh.append(jnp.where(in_head, blk, 0.0).astype(BF16))
            dms = [_mm_nt(dyh[k], xdt_b[:, LANES * (k // 2):LANES * (k // 2 + 1)]) for k in range(HEADS_PER_GROUP)]
            dxs = [_mm(mts[k], dyh[k]) for k in range(HEADS_PER_GROUP)]
            dcb = jnp.zeros((CHUNK, CHUNK), F32)
            for k, h in enumerate(heads):
                gmat = dms[k] * (cb * lms[k])
                dacs_c = dacs_c + jnp.where(lane == h, jnp.sum(gmat, axis=1, keepdims=True), 0.0)
                dacs_r = dacs_r - jnp.where(sub == h, jnp.sum(gmat, axis=0, keepdims=True), 0.0)
                dcb = dcb + dms[k] * lms[k]
            dxdt_g = jnp.concatenate([dxs[2 * k] + dxs[2 * k + 1] for k in range(4)], axis=1)
            t_g = _mm_nt(c_b, h_b)
            dacs_c = dacs_c + _head_sums(dy_g * e_g * t_g, g)
            dt_b = (dy_g * e_g).astype(BF16)
            dc_acc = _mm(dt_b, h_b)
            dh_prev = _mm_tn(dt_b, c_b)
            dw_g = _mm_nt(b_b, dh_b)
            w_g = xdt_g * dte_g
            dxdt_g = dxdt_g + dw_g * dte_g
            db_acc = _mm(w_g.astype(BF16), dh_b)
            r2 = _head_sums(dw_g * w_g, g)
            dacs_c = dacs_c + jnp.where(is_last, jnp.sum(r2, axis=0, keepdims=True), 0.0) - r2
            q3 = jnp.sum(dh_g * h_g, axis=1, keepdims=True)
            for k, h in enumerate(heads):
                tot = jnp.sum(q3[HEAD_DIM * k:HEAD_DIM * (k + 1), :], keepdims=True) * cd[:, h:h + 1]
                dacs_c = dacs_c + jnp.where(is_last & (lane == h), tot, 0.0)
            dh_scr[g] = dh_prev + dh_g * jnp.exp(_rows_from_lanes(last_g))
            dxc_ref[:, cols] = dxdt_g * dt_g + dsk_ref[:, cols] * dy_g
            ddtx = ddtx + _head_sums(dxdt_g * x_g, g)
            ddsk_ref[:, cols] += jnp.sum(dy_g * x_g, axis=0, keepdims=True)
            dxc_ref[:, b_off:b_off + D_STATE] = db_acc + _mm(dcb.T.astype(BF16), c_b)
            dxc_ref[:, c_off:c_off + D_STATE] = dc_acc + _mm(dcb.astype(BF16), b_b)
        dacs = dacs_c + dacs_r.T
        dadt = _mm_exact((lane >= sub).astype(F32), dacs)
        ddt = dadt * a_row_v + ddtx
        ddt_raw = ddt * sig
        ddt_ref[...] = ddt_raw
        da_ref[...] += jnp.sum(dadt * dt, axis=0, keepdims=True)
        ddtb_ref[...] += jnp.sum(ddt_raw, axis=0, keepdims=True)

    return pl.pallas_call(
        body, name="ssd_bwd",
        out_shape=(jax.ShapeDtypeStruct((s, CONV_CH), F32), jax.ShapeDtypeStruct((s, LANES), F32),
                   jax.ShapeDtypeStruct((1, LANES), F32), jax.ShapeDtypeStruct((1, LANES), F32),
                   jax.ShapeDtypeStruct((1, SSD_WIDTH), F32)),
        grid=(nc,),
        in_specs=[pl.BlockSpec((CHUNK, CONV_CH), lambda c: (rev(c), 0)),
                  pl.BlockSpec((CHUNK, LANES), lambda c: (rev(c), 0)),
                  pl.BlockSpec((1, N_GROUPS, GROUP_WIDTH, D_STATE), lambda c: (rev(c), 0, 0, 0)),
                  pl.BlockSpec((CHUNK, SSD_WIDTH), lambda c: (rev(c), 0)),
                  _const_spec((1, LANES)), _const_spec((1, LANES)), _const_spec((1, SSD_WIDTH))],
        out_specs=(pl.BlockSpec((CHUNK, CONV_CH), lambda c: (rev(c), 0)),
                   pl.BlockSpec((CHUNK, LANES), lambda c: (rev(c), 0)),
                   _const_spec((1, LANES)), _const_spec((1, LANES)), _const_spec((1, SSD_WIDTH))),
        scratch_shapes=[pltpu.VMEM((N_GROUPS, GROUP_WIDTH, D_STATE), F32)],
        compiler_params=_params(("arbitrary",)),
    )(xc, small, states, dy, dtb_row, a_row, dskip_lane)


FORGET_BLOCK = 512


def forget_cumsum(small, fgb_row):
    s = small.shape[0]
    t = _blk(s, FORGET_BLOCK)
    nb = s // t

    def body(sm_ref, b_ref, cc_ref, carry):
        i = pl.program_id(0)

        @pl.when(i == 0)
        def _():
            carry[...] = jnp.zeros_like(carry)

        lane = _iota((t, LANES), 1)
        in_f = (lane >= N_HEADS) & (lane < 2 * N_HEADS)
        logf = jnp.where(in_f, -_softplus(-(sm_ref[...] + b_ref[...])), 0.0)
        tri = (_iota((t, t), 1) <= _iota((t, t), 0)).astype(F32)
        cum = _mm_exact(tri, logf) + carry[0:1, :]
        cc_ref[...] = cum
        carry[...] = jnp.broadcast_to(cum[t - 1:t, :], (8, LANES))

    return pl.pallas_call(
        body, name="forget_cumsum",
        out_shape=jax.ShapeDtypeStruct((s, LANES), F32),
        grid=(nb,),
        in_specs=[pl.BlockSpec((t, LANES), lambda i: (i, 0)), _const_spec((1, LANES))],
        out_specs=pl.BlockSpec((t, LANES), lambda i: (i, 0)),
        scratch_shapes=[pltpu.VMEM((8, LANES), F32)],
        compiler_params=_params(("arbitrary",)),
    )(small, fgb_row)


def forget_bwd(dc, small, ddt_raw, fgb_row):
    s = small.shape[0]
    t = _blk(s, FORGET_BLOCK)
    nb = s // t
    rev = lambda i: nb - 1 - i

    def body(dc_ref, sm_ref, ddt_ref, b_ref, ds_ref, dfb_ref, carry):
        i = pl.program_id(0)

        @pl.when(i == 0)
        def _():
            carry[...] = jnp.zeros_like(carry)
            dfb_ref[...] = jnp.zeros_like(dfb_ref)

        lane = _iota((t, LANES), 1)
        rows = dc_ref[...].T
        tri = (_iota((t, t), 1) <= _iota((t, t), 0)).astype(F32)
        rc = _mm_exact(rows, tri) + carry[:, 0:1]
        carry[...] = jnp.broadcast_to(rc[:, 0:1], (LANES, LANES))
        in_f = (lane >= N_HEADS) & (lane < 2 * N_HEADS)
        df = jnp.where(in_f, rc.T * _sigmoid(-(sm_ref[...] + b_ref[...])), 0.0)
        ds_ref[...] = (df + ddt_ref[...]).astype(BF16)
        dfb_ref[...] += jnp.sum(df, axis=0, keepdims=True)

    blk = pl.BlockSpec((t, LANES), lambda i: (rev(i), 0))
    return pl.pallas_call(
        body, name="forget_bwd",
        out_shape=(jax.ShapeDtypeStruct((s, LANES), BF16), jax.ShapeDtypeStruct((1, LANES), F32)),
        grid=(nb,),
        in_specs=[blk, blk, blk, _const_spec((1, LANES))],
        out_specs=(blk, _const_spec((1, LANES))),
        scratch_shapes=[pltpu.VMEM((LANES, LANES), F32)],
        compiler_params=_params(("arbitrary",)),
    )(dc, small, ddt_raw, fgb_row)


ATT_BLOCK = 1024
ATT_BLOCK_BWD = 512
ATT_BLOCK_BWD_Q = 512
ATT_SCALE = HEAD_DIM ** -0.5
AUG_A = HEAD_DIM
AUG_B = HEAD_DIM + 3


def _split3(c):
    hi = c.astype(BF16).astype(F32)
    r = c - hi
    mid = r.astype(BF16).astype(F32)
    return hi, mid, (r - mid).astype(BF16).astype(F32)


def _aug(lane, first, parts=None, value=1.0):
    if parts is None:
        return jnp.where((lane >= first) & (lane < first + 3), value, 0.0)
    return (jnp.where(lane == first, parts[0], 0.0) + jnp.where(lane == first + 1, parts[1], 0.0)
            + jnp.where(lane == first + 2, parts[2], 0.0))


def _pack_pair(a0, a1, lane):
    return jnp.where(lane < HEAD_DIM, a0, pltpu.roll(a1, HEAD_DIM, 1))


def proj_qkv_heads(u, w_q, w_k, w_v, cum, later):
    s = u.shape[0]
    tm = _blk(s, 256)
    nsteps = s // tm
    n_later = len(later)

    def body(u_ref, wq_ref, wk_ref, wv_ref, c_ref, *rest):
        later_in = rest[:n_later]
        qa_ref, ka_ref, va_ref, nrm_ref = rest[n_later:n_later + 4]
        later_out = rest[n_later + 4:2 * n_later + 4]
        sems = rest[2 * n_later + 4:]
        step = pl.program_id(0)

        @pl.when(step == 0)
        def _():
            for cp in gather_copies(later_in, later_out, *sems)[0]:
                cp.start()

        @pl.when(step == nsteps // 2)
        def _():
            for landed, onward in zip(*gather_copies(later_in, later_out, *sems)):
                landed.wait_recv()
                onward.start()

        @pl.when(step == nsteps - 1)
        def _():
            fetched, passed = gather_copies(later_in, later_out, *sems)
            for cp in passed:
                cp.wait_recv()
            for cp in fetched + passed:
                cp.wait_send()

        lane = _iota((tm, LANES), 1)
        lo = lane < HEAD_DIM
        uv = u_ref[...]
        qf = _mm(uv, wq_ref[...]) * ATT_SCALE
        kf = _mm(uv, wk_ref[...])
        vf = _mm(uv, wv_ref[...])
        cc = c_ref[...]
        ones_a = _aug(lane, AUG_A)
        ones_b = _aug(lane, AUG_B)
        sub8 = _iota((8, LANES), 0)
        nrm = jnp.zeros((8, LANES), F32)
        for h in range(N_HEADS):
            j, e = divmod(h, 2)

            def head(full):
                blk = full[:, LANES * j:LANES * (j + 1)]
                if e == 1:
                    blk = pltpu.roll(blk, HEAD_DIM, 1)
                return jnp.where(lo, blk, 0.0)

            parts = _split3(cc[:, N_HEADS + h:N_HEADS + h + 1])
            qh, kh = head(qf), head(kf)
            qa_ref[h] = (qh + _aug(lane, AUG_A, parts) + ones_b).astype(BF16)
            ka_ref[h] = (kh + ones_a - _aug(lane, AUG_B, parts)).astype(BF16)
            va_ref[h] = (head(vf) + ones_a).astype(BF16)
        seg = (_iota((ATT_WIDTH, LANES), 1) == (_iota((ATT_WIDTH, LANES), 0) >> 6)).astype(BF16)
        for r, val in enumerate((qf, kf)):
            sq = val * val
            hi = sq.astype(BF16)
            tot = _mm(hi, seg) + _mm((sq - hi.astype(F32)).astype(BF16), seg)
            nrm = nrm + jnp.where(sub8 == r, jnp.max(tot, axis=0, keepdims=True), 0.0)
        nrm_ref[0] = nrm

    shp = jax.ShapeDtypeStruct((N_HEADS, s, LANES), BF16)
    hspec = pl.BlockSpec((N_HEADS, tm, LANES), lambda i: (0, i, 0))
    wspec = _const_spec((D_MODEL, ATT_WIDTH))
    return pl.pallas_call(
        body, name="proj_qkv_heads",
        out_shape=tuple([shp, shp, shp, jax.ShapeDtypeStruct((nsteps, 8, LANES), F32)]
                        + [jax.ShapeDtypeStruct((N_CHIPS,) + a.shape, a.dtype) for a in later]),
        grid=(nsteps,),
        in_specs=[pl.BlockSpec((tm, D_MODEL), lambda i: (i, 0)), wspec, wspec, wspec,
                  pl.BlockSpec((tm, LANES), lambda i: (i, 0))] + [ANY] * n_later,
        out_specs=tuple([hspec, hspec, hspec, pl.BlockSpec((1, 8, LANES), lambda i: (i, 0, 0))]
                        + [ANY] * n_later),
        scratch_shapes=_sems(3 * n_later) + _sems(3 * n_later),
        compiler_params=_params(("arbitrary",)),
    )(u, w_q, w_k, w_v, cum, *later)


SKIP_BELOW = -110.0


def live_blocks(norms, cum, tq, tk):
    qn = jnp.sqrt(jnp.max(norms[:, 0, :N_HEADS], axis=0))
    kn = jnp.sqrt(jnp.max(norms[:, 1, :N_HEADS], axis=0))
    bound = 2.05 * qn * kn + 2.0
    c_first = cum[0::tq, N_HEADS:2 * N_HEADS]
    c_last = cum[tk - 1::tk, N_HEADS:2 * N_HEADS]
    nq, nk = c_first.shape[0], c_last.shape[0]
    top = bound[None, None, :] + c_first[:, None, :] - c_last[None, :, :]
    before = (jnp.arange(nk)[None, :] + 1) * tk <= jnp.arange(nq)[:, None] * tq
    dead = before[:, :, None] & ~(top >= SKIP_BELOW)
    first = jnp.sum(dead, axis=1).astype(jnp.int32).T
    last_q = jnp.sum(first[:, None, :] <= jnp.arange(nk)[None, :, None], axis=2).astype(jnp.int32) - 1
    return first, last_q


def attention_fwd(first, qa, ka, va):
    s = qa.shape[1]
    t = _blk(s, ATT_BLOCK)
    nq = s // t

    def body(first_ref, qa_ref, ka_ref, va_ref, o_ref, qb_ref, m_scr, acc_scr, alpha_scr, p_scr, s_scr):
        qi = pl.program_id(1)
        starts = [first_ref[2 * pl.program_id(0) + e, qi] for e in range(2)]
        k0 = jnp.maximum(starts[0], starts[1])
        m_scr[...] = jnp.full_like(m_scr, NEG_BIG)
        acc_scr[...] = jnp.zeros_like(acc_scr)

        def kv_rows(kb):
            return pl.ds(pl.multiple_of(kb * t, t), t)

        def logits(kb, masked, heads=(0, 1)):
            for e in heads:
                sc = _mm_nt(qa_ref[e], ka_ref[e, kv_rows(kb), :])
                if masked:
                    sc = jnp.where(_iota((t, t), 0) >= _iota((t, t), 1), sc, NEG_BIG)
                s_scr[e] = sc

        def probs(heads=(0, 1)):
            for e in heads:
                cmax = s_scr[e, :, 0:LANES]
                for c in range(1, t // LANES):
                    cmax = jnp.maximum(cmax, s_scr[e, :, LANES * c:LANES * (c + 1)])
                m_old = m_scr[e]
                m_new = jnp.maximum(m_old, jnp.max(cmax, axis=1, keepdims=True))
                alpha_scr[e] = jnp.exp(m_old - m_new)
                m_scr[e] = m_new
                for c in range(t // LANES):
                    cols = slice(LANES * c, LANES * (c + 1))
                    p_scr[e, :, cols] = jnp.exp(s_scr[e, :, cols] - m_new).astype(BF16)

        def accumulate(kb, heads=(0, 1)):
            for e in heads:
                acc_scr[e] = alpha_scr[e] * acc_scr[e] + _mm(p_scr[e], va_ref[e, kv_rows(kb), :])

        for e in range(2):
            def alone(kb, carry, e=e):
                logits(kb, False, (e,))
                probs((e,))
                accumulate(kb, (e,))
                return carry

            lax.fori_loop(starts[e], k0, alone, 0)

        def loop_body(kb, carry):
            logits(kb, False)
            for e in range(2):
                accumulate(kb - 1, (e,))
                probs((e,))
            return carry

        @pl.when(qi > k0)
        def _():
            logits(k0, False)
            probs()

        lax.fori_loop(k0 + 1, qi, loop_body, 0)

        @pl.when(qi > k0)
        def _():
            logits(qi, True)
            accumulate(qi - 1)
            probs()

        @pl.when(qi == k0)
        def _():
            logits(qi, True)
            probs()

        accumulate(qi)

        lane = _iota((t, LANES), 1)
        outs = []
        for e in range(2):
            acc = acc_scr[e]
            l = acc[:, AUG_A:AUG_A + 1]
            outs.append(acc / l)
            lse = m_scr[e][:, 0:1] + jnp.log(l)
            q32 = qa_ref[e].astype(F32)
            c = q32[:, AUG_A:AUG_A + 1] + q32[:, AUG_A + 1:AUG_A + 2] + q32[:, AUG_A + 2:AUG_A + 3]
            qb = jnp.where(lane < HEAD_DIM, q32, 0.0) + _aug(lane, AUG_A, _split3(c - lse)) + _aug(lane, AUG_B)
            qb_ref[e] = qb.astype(BF16)
        o_ref[...] = _pack_pair(outs[0], outs[1], lane)

    grid_spec = pltpu.PrefetchScalarGridSpec(
        num_scalar_prefetch=1, grid=(N_PAIRS, nq),
        in_specs=[pl.BlockSpec((2, t, LANES), lambda j, qi, f: (j, qi, 0)),
                  pl.BlockSpec((2, s, LANES), lambda j, qi, f: (j, 0, 0)),
                  pl.BlockSpec((2, s, LANES), lambda j, qi, f: (j, 0, 0))],
        out_specs=[pl.BlockSpec((t, LANES), lambda j, qi, f: (qi, j)),
                   pl.BlockSpec((2, t, LANES), lambda j, qi, f: (j, qi, 0))],
        scratch_shapes=[pltpu.VMEM((2, t, LANES), F32), pltpu.VMEM((2, t, LANES), F32),
                        pltpu.VMEM((2, t, LANES), F32), pltpu.VMEM((2, t, t), BF16), pltpu.VMEM((2, t, t), F32)])
    return pl.pallas_call(
        body, name="attention_fwd", grid_spec=grid_spec,
        out_shape=(jax.ShapeDtypeStruct((s, ATT_WIDTH), F32), jax.ShapeDtypeStruct((N_HEADS, s, LANES), BF16)),
        compiler_params=_params(("parallel", "parallel")),
    )(first, qa, ka, va)


def attention_bwd(last_q, qb, ka, va, dob):
    s = qb.shape[1]
    t = _blk(s, ATT_BLOCK_BWD)
    tq = _blk(s, ATT_BLOCK_BWD_Q)
    nq = s // tq
    per_q = tq // t

    def body(last_ref, qb_ref, dob_ref, ka_ref, va_ref, dq_ref, dk_ref, dv_ref, dc_ref, dq_scr, dk_scr, dv_scr):
        j, ki = pl.program_id(0), pl.program_id(1)

        @pl.when((j == 0) & (ki == 0))
        def _():
            dc_ref[...] = jnp.zeros_like(dc_ref)

        @pl.when(ki == 0)
        def _():
            dq_scr[...] = jnp.zeros_like(dq_scr)

        dk_scr[...] = jnp.zeros_like(dk_scr)
        dv_scr[...] = jnp.zeros_like(dv_scr)

        def q_step(qblk, masked, heads=(0, 1)):
            rows = pl.ds(pl.multiple_of(qblk * tq, tq), tq)
            scs = [_mm_nt(qb_ref[e, rows, :], ka_ref[e]) for e in heads]
            dps = [_mm_nt(dob_ref[e, rows, :], va_ref[e]) for e in heads]
            for e, sc, dp in zip(heads, scs, dps):
                q = qb_ref[e, rows, :]
                do = dob_ref[e, rows, :]
                if masked:
                    keep = (_iota((tq, t), 0) - _iota((tq, t), 1)) >= ki * t - qblk * tq
                    sc = jnp.where(keep, sc, NEG_BIG)
                p = jnp.exp(sc)
                ds_b = (p * dp).astype(BF16)
                dv_scr[e] += _mm_tn(p.astype(BF16), do)
                dk_scr[e] += _mm_tn(ds_b, q)
                dq_scr[e, rows, :] += _mm(ds_b, ka_ref[e])

        def loop_body(qblk, carry):
            q_step(qblk, False)
            return carry

        ends = [last_ref[2 * j + e, ki] + 1 for e in range(2)]
        both = jnp.minimum(ends[0], ends[1])
        diag = ki // per_q
        q_step(diag, True)
        lax.fori_loop(diag + 1, both, loop_body, 0)
        for e in range(2):
            def alone(qblk, carry, e=e):
                q_step(qblk, False, (e,))
                return carry

            lax.fori_loop(both, ends[e], alone, 0)

        lane = _iota((t, LANES), 1)
        dk_ref[...] = _pack_pair(dk_scr[0], dk_scr[1], lane).astype(BF16)
        dv_ref[...] = _pack_pair(dv_scr[0], dv_scr[1], lane).astype(BF16)
        rows = pl.ds(pl.multiple_of(ki * t, t), t)
        dc_ref[rows, :] -= (jnp.where(lane == N_HEADS + 2 * j, dk_scr[0][:, AUG_B:AUG_B + 1], 0.0)
                            + jnp.where(lane == N_HEADS + 2 * j + 1, dk_scr[1][:, AUG_B:AUG_B + 1], 0.0))

        @pl.when(ki == s // t - 1)
        def _():
            for blk in range(s // t):
                rws = pl.ds(blk * t, t)
                d0 = dq_scr[0, rws, :]
                d1 = dq_scr[1, rws, :]
                dq_ref[rws, :] = (_pack_pair(d0, d1, lane) * ATT_SCALE).astype(BF16)
                dc_ref[rws, :] += (jnp.where(lane == N_HEADS + 2 * j, d0[:, AUG_A:AUG_A + 1], 0.0)
                                   + jnp.where(lane == N_HEADS + 2 * j + 1, d1[:, AUG_A:AUG_A + 1], 0.0))

    full = pl.BlockSpec((2, s, LANES), lambda j, ki, f: (j, 0, 0))
    blk = pl.BlockSpec((2, t, LANES), lambda j, ki, f: (j, ki, 0))
    pair = pl.BlockSpec((t, LANES), lambda j, ki, f: (ki, j))
    wide = jax.ShapeDtypeStruct((s, ATT_WIDTH), BF16)
    grid_spec = pltpu.PrefetchScalarGridSpec(
        num_scalar_prefetch=1, grid=(N_PAIRS, s // t),
        in_specs=[full, full, blk, blk],
        out_specs=[pl.BlockSpec((s, LANES), lambda j, ki, f: (0, j)), pair, pair,
                   pl.BlockSpec((s, LANES), lambda j, ki, f: (0, 0))],
        scratch_shapes=[pltpu.VMEM((2, s, LANES), F32), pltpu.VMEM((2, t, LANES), F32),
                        pltpu.VMEM((2, t, LANES), F32)])
    return pl.pallas_call(
        body, name="attention_bwd", grid_spec=grid_spec,
        out_shape=(wide, wide, wide, jax.ShapeDtypeStruct((s, LANES), F32)),
        compiler_params=_params(("arbitrary", "arbitrary")),
    )(last_q, qb, dob, ka, va)


def _dsilu(z, sg):
    return sg * (1.0 + z * (1.0 - sg))


def post_mix(x, y, zs, o, za, p, tgt, ssd_g, att_g_lane, ple_g, fin_g, w_out, w_gate, w_proj):
    s = x.shape[0]
    tm = _blk(s, 256)
    half = SSD_WIDTH // N_GROUPS

    def rms_bwd(dy, yn, r):
        return r * (dy - yn * jnp.mean(dy * yn, axis=-1, keepdims=True))

    def colsum(a):
        return jnp.sum(a, axis=0, keepdims=True)

    def body(x_ref, y_ref, zs_ref, o_ref, za_ref, p_ref, t_ref, sg_ref, ag_ref, pg_ref, fg_ref,
             wo_ref, wg_ref, wp_ref,
             dh1_ref, dy_ref, dzs_ref, dob_ref, dza_ref, ycat_ref, dh1b_ref, n2b_ref, dglb_ref, dppb_ref, pb_ref,
             loss_ref, dfin_ref, dple_ref, dssd_ref, datt_ref):
        @pl.when(pl.program_id(0) == 0)
        def _():
            for r in (loss_ref, dfin_ref, dple_ref, dssd_ref, datt_ref):
                r[...] = jnp.zeros_like(r)

        lane = _iota((tm, LANES), 1)
        lo = lane < HEAD_DIM
        zs = zs_ref[...]
        sz = _sigmoid(zs)
        yv = y_ref[...]
        ys = yv * (zs * sz)
        yn, rg = [], []
        for g in range(N_GROUPS):
            seg = ys[:, half * g:half * (g + 1)]
            r = lax.rsqrt(jnp.mean(seg * seg, axis=-1, keepdims=True) + EPS)
            yn.append(seg * r)
            rg.append(r)
            ycat_ref[:, half * g:half * (g + 1)] = (yn[g] * sg_ref[:, half * g:half * (g + 1)]).astype(BF16)
        za = za_ref[...]
        sza = _sigmoid(za)
        silu_za = za * sza
        on, ra = [], []
        for jb in range(N_PAIRS):
            blk = o_ref[:, LANES * jb:LANES * (jb + 1)]
            sq = blk * blk
            ms0 = jnp.sum(jnp.where(lo, sq, 0.0), axis=1, keepdims=True) * (1.0 / HEAD_DIM)
            ms1 = jnp.sum(jnp.where(lo, 0.0, sq), axis=1, keepdims=True) * (1.0 / HEAD_DIM)
            r = jnp.where(lo, lax.rsqrt(ms0 + EPS), lax.rsqrt(ms1 + EPS))
            on.append(blk * r)
            ra.append(r)
            an = on[jb] * ag_ref[:, LANES * jb:LANES * (jb + 1)]
            ycat_ref[:, SSD_WIDTH + LANES * jb:SSD_WIDTH + LANES * (jb + 1)] = (
                an * silu_za[:, LANES * jb:LANES * (jb + 1)]).astype(BF16)
        h1 = x_ref[...] + _mm(ycat_ref[...], wo_ref[...])
        r2 = lax.rsqrt(jnp.mean(h1 * h1, axis=-1, keepdims=True) + EPS)
        n2h = h1 * r2
        n2_b = (n2h * pg_ref[...]).astype(BF16)
        gate = _sigmoid(_mm(n2_b, wg_ref[...]))
        p_b = p_ref[...].astype(BF16)
        pp = _mm(p_b, wp_ref[...])
        h2 = h1 + gate * pp
        r3 = lax.rsqrt(jnp.mean(h2 * h2, axis=-1, keepdims=True) + EPS)
        n3 = h2 * r3
        diff = n3 * fg_ref[...] - t_ref[...]
        sq = colsum(diff * diff)
        part = sq[:, 0:LANES]
        for jb in range(1, D_MODEL // LANES):
            part = part + sq[:, LANES * jb:LANES * (jb + 1)]
        loss_ref[...] += part * (0.5 / D_MODEL)
        dout = diff * (1.0 / D_MODEL)
        dfin_ref[...] += colsum(dout * n3)
        dh2 = rms_bwd(dout * fg_ref[...], n3, r3)
        dgl = dh2 * pp * gate * (1.0 - gate)
        dgl_b = dgl.astype(BF16)
        dn2 = _mm_nt(dgl_b, wg_ref[...])
        dple_ref[...] += colsum(dn2 * n2h)
        dh1 = dh2 + rms_bwd(dn2 * pg_ref[...], n2h, r2)
        dh1_b = dh1.astype(BF16)
        dycat = _mm_nt(dh1_b, wo_ref[...])
        dh1_ref[...] = dh1
        dh1b_ref[...] = dh1_b
        n2b_ref[...] = n2_b
        dglb_ref[...] = dgl_b
        dppb_ref[...] = (dh2 * gate).astype(BF16)
        pb_ref[...] = p_b
        for g in range(N_GROUPS):
            cols = slice(half * g, half * (g + 1))
            dys_g = dycat[:, cols]
            dssd_ref[:, cols] += colsum(dys_g * yn[g])
            dys = rms_bwd(dys_g * sg_ref[:, cols], yn[g], rg[g])
            dy_ref[:, cols] = dys * (zs[:, cols] * sz[:, cols])
            dzs_ref[:, cols] = (dys * yv[:, cols] * _dsilu(zs[:, cols], sz[:, cols])).astype(BF16)
        for jb in range(N_PAIRS):
            cols = slice(LANES * jb, LANES * (jb + 1))
            dya = dycat[:, SSD_WIDTH + LANES * jb:SSD_WIDTH + LANES * (jb + 1)]
            ag = ag_ref[:, cols]
            dan = dya * silu_za[:, cols]
            dza_ref[:, cols] = (dya * (on[jb] * ag) * _dsilu(za[:, cols], sza[:, cols])).astype(BF16)
            datt_ref[:, cols] += colsum(dan * on[jb])
            don = dan * ag
            q = don * on[jb]
            m0 = jnp.sum(jnp.where(lo, q, 0.0), axis=1, keepdims=True) * (1.0 / HEAD_DIM)
            m1 = jnp.sum(jnp.where(lo, 0.0, q), axis=1, keepdims=True) * (1.0 / HEAD_DIM)
            do2 = ra[jb] * (don - on[jb] * jnp.where(lo, m0, m1))
            prod = do2 * o_ref[:, cols]
            for e in range(2):
                delta = jnp.sum(jnp.where(lo, prod, 0.0) if e == 0 else jnp.where(lo, 0.0, prod),
                                axis=1, keepdims=True)
                base = jnp.where(lo, do2 if e == 0 else pltpu.roll(do2, HEAD_DIM, 1), 0.0)
                dob_ref[2 * jb + e] = (base - _aug(lane, AUG_A, _split3(delta))).astype(BF16)

    def rows(n, dtype=None):
        return pl.BlockSpec((tm, n), lambda i: (i, 0))

    def out(n, dtype):
        return jax.ShapeDtypeStruct((s, n), dtype)

    vec = _const_spec((1, D_MODEL))
    vshape = jax.ShapeDtypeStruct((1, D_MODEL), F32)
    return pl.pallas_call(
        body, name="post_mix",
        out_shape=(out(D_MODEL, F32), out(SSD_WIDTH, F32), out(SSD_WIDTH, BF16),
                   jax.ShapeDtypeStruct((N_HEADS, s, LANES), BF16),
                   out(ATT_WIDTH, BF16), out(D_INNER, BF16), out(D_MODEL, BF16), out(D_MODEL, BF16),
                   out(D_MODEL, BF16), out(D_MODEL, BF16), out(PLE_DIM, BF16),
                   jax.ShapeDtypeStruct((1, LANES), F32), vshape, vshape, vshape, vshape),
        grid=(s // tm,),
        in_specs=[rows(D_MODEL), rows(SSD_WIDTH), rows(SSD_WIDTH), rows(ATT_WIDTH), rows(ATT_WIDTH),
                  rows(PLE_DIM), rows(D_MODEL), vec, vec, vec, vec,
                  _const_spec((D_INNER, D_MODEL)), _const_spec((D_MODEL, D_MODEL)), _const_spec((PLE_DIM, D_MODEL))],
        out_specs=(rows(D_MODEL), rows(SSD_WIDTH), rows(SSD_WIDTH),
                   pl.BlockSpec((N_HEADS, tm, LANES), lambda i: (0, i, 0)), rows(ATT_WIDTH),
                   rows(D_INNER), rows(D_MODEL), rows(D_MODEL), rows(D_MODEL), rows(D_MODEL), rows(PLE_DIM),
                   _const_spec((1, LANES)), vec, vec, vec, vec),
        compiler_params=_params(("arbitrary",)),
    )(x, y, zs, o, za, p, tgt, ssd_g, att_g_lane, ple_g, fin_g, w_out, w_gate, w_proj)


def in_proj_bwd(dsegs, wsegs, x, g, dh1, pres):
    s = x.shape[0]
    tm = _blk(s, 512)
    nseg = len(dsegs)
    nbig = len(pres)
    nsteps = s // tm

    def body(*refs):
        d_refs = refs[:nseg]
        w_refs = refs[nseg:2 * nseg]
        x_ref, g_ref, dh1_ref = refs[2 * nseg:2 * nseg + 3]
        rest = refs[2 * nseg + 3:]
        pre_refs, (dx_ref, dg_ref), part_refs = rest[:nbig], rest[nbig:nbig + 2], rest[nbig + 2:2 * nbig + 2]
        ssem, rsem, lsem = rest[2 * nbig + 2:]

        @pl.when(pl.program_id(0) == 0)
        def _():
            dg_ref[...] = jnp.zeros_like(dg_ref)
            for cp in scatter_copies(pre_refs, part_refs, ssem, rsem, lsem):
                cp.start()

        @pl.when(pl.program_id(0) == nsteps - 1)
        def _():
            for cp in scatter_copies(pre_refs, part_refs, ssem, rsem, lsem):
                cp.wait()

        du = _mm_nt(d_refs[0][...], w_refs[0][...])
        for k in range(1, nseg):
            du = du + _mm_nt(d_refs[k][...], w_refs[k][...])
        xv = x_ref[...]
        r = lax.rsqrt(jnp.mean(xv * xv, axis=-1, keepdims=True) + EPS)
        xh = xv * r
        dg_ref[...] += jnp.sum(du * xh, axis=0, keepdims=True)
        dxh = du * g_ref[...]
        dx_ref[...] = r * (dxh - xh * jnp.mean(dxh * xh, axis=-1, keepdims=True)) + dh1_ref[...]

    rows = lambda n: pl.BlockSpec((tm, n), lambda i: (i, 0))
    return pl.pallas_call(
        body, name="in_proj_bwd",
        out_shape=tuple([jax.ShapeDtypeStruct((s, D_MODEL), F32), jax.ShapeDtypeStruct((1, D_MODEL), F32)]
                        + [jax.ShapeDtypeStruct(a.shape, a.dtype) for a in pres]),
        grid=(nsteps,),
        in_specs=([rows(d.shape[1]) for d in dsegs] + [_const_spec(w.shape) for w in wsegs]
                  + [rows(D_MODEL), _const_spec((1, D_MODEL)), rows(D_MODEL)] + [ANY] * nbig),
        out_specs=tuple([rows(D_MODEL), _const_spec((1, D_MODEL))] + [ANY] * nbig),
        scratch_shapes=_sems(3 * nbig) + [pltpu.SemaphoreType.DMA((nbig,))],
        compiler_params=_params(("arbitrary",)),
    )(*dsegs, *wsegs, x, g, dh1, *pres)


SMALL_NAMES = ("norm_g", "conv_b", "dt_bias", "a_log", "d_skip", "ssd_norm_g", "fg_bias", "att_norm_g",
               "ple_norm_g", "final_norm_g")
SMALL_SIZES = (1024, 1536, 16, 16, 16, 1024, 16, 64, 1024, 1024)
CONV_W_SIZE = CONV_WIDTH * CONV_CH


def _pack_small(vals):
    flat = jnp.concatenate([v.reshape(-1).astype(F32) for v in vals])
    flat = jnp.pad(flat, (0, SMALL_ROWS * LANES - flat.shape[0]))
    return flat.reshape(SMALL_ROWS, LANES)


def _unpack_small(pack, shapes):
    flat = pack.reshape(-1)
    out, off = [], 0
    for n, shp in zip(SMALL_SIZES, shapes):
        out.append(flat[off:off + n].reshape(shp))
        off += n
    return out


def _row128(v16, offset=0):
    return jnp.pad(v16.reshape(1, N_HEADS).astype(F32), ((0, 0), (offset, LANES - N_HEADS - offset)))


def local_step(prereduce, later, join_later, x, p, tgt, w_in, conv_w, norm_g, conv_b, dt_bias, a_log, d_skip,
               ssd_norm_g, fg_bias, att_norm_g, ple_norm_g, final_norm_g):
    widths = (SSD_WIDTH, CONV_CH, N_HEADS, ATT_WIDTH, ATT_WIDTH, ATT_WIDTH, ATT_WIDTH)
    c0, c1, c2, c3, c4, c5, c6, c7 = [sum(widths[:i]) for i in range(len(widths) + 1)]
    w_zs, w_xbc, w_dt = w_in[:, c0:c1], w_in[:, c1:c2], w_in[:, c2:c3]
    w_za, w_q, w_k, w_v, w_f = w_in[:, c3:c4], w_in[:, c4:c5], w_in[:, c5:c6], w_in[:, c6:c7], w_in[:, c7:]
    w_small = jnp.concatenate([w_dt, w_f, jnp.zeros((D_MODEL, LANES - 2 * N_HEADS), BF16)], axis=1)

    dtb_row = _row128(dt_bias)
    a_row = _row128(-jnp.exp(a_log.astype(F32)))
    fgb_row = _row128(fg_bias, N_HEADS)
    dskip_lane = jnp.repeat(d_skip.astype(F32), HEAD_DIM).reshape(1, SSD_WIDTH)
    att_g_lane = jnp.tile(att_norm_g.astype(F32), N_HEADS).reshape(1, ATT_WIDTH)
    row = lambda v: v.reshape(1, -1).astype(F32)

    u, zs, xbc, za, small = in_proj_fwd(x, row(norm_g), [w_zs, w_xbc, w_za, w_small])
    cum = forget_cumsum(small, fgb_row)
    qa, ka, va, norms, *gathered = proj_qkv_heads(u, w_q, w_k, w_v, cum, later)
    w_out, w_gate, w_proj = join_later(gathered)
    n_seq = x.shape[0]
    first, _ = live_blocks(norms, cum, _blk(n_seq, ATT_BLOCK), _blk(n_seq, ATT_BLOCK))
    _, last_q = live_blocks(norms, cum, _blk(n_seq, ATT_BLOCK_BWD_Q), _blk(n_seq, ATT_BLOCK_BWD))
    pre, xc = conv_fwd(xbc, conv_w, row(conv_b))
    y, states = ssd_fwd(xc, small, dtb_row, a_row, dskip_lane)
    o, qb = attention_fwd(first, qa, ka, va)
    (dh1, dy, dzs, dob, dza, ycat, dh1_b, n2_b, dgl_b, dpp_b, p_b,
     loss_l, dfin, dple, dssd_g, datt_lane) = post_mix(
        x, y, zs, o, za, p, tgt, row(ssd_norm_g), att_g_lane, row(ple_norm_g), row(final_norm_g),
        w_out, w_gate, w_proj)
    dq, dk, dv, dc = attention_bwd(last_q, qb, ka, va, dob)
    dxc, ddt_raw, da, ddtb, ddsk_lane = ssd_bwd(xc, small, states, dy, dtb_row, a_row, dskip_lane)
    dsmall, dfgb = forget_bwd(dc, small, ddt_raw, fgb_row)
    dxbc, dconv_w8, dconv_b = conv_bwd(xbc, pre, dxc, conv_w)
    dsegs = [dzs, dxbc, dza, dq, dk, dv, dsmall]
    wsegs = [w_zs, w_xbc, w_za, w_q, w_k, w_v, w_small]
    dws = [matmul_tn(u, d, "dw_in_%d" % i) for i, d in enumerate(dsegs)]
    dw_in = jnp.concatenate([dws[0], dws[1], dws[6][:, :N_HEADS], dws[2], dws[3], dws[4], dws[5],
                             dws[6][:, N_HEADS:2 * N_HEADS]], axis=1)
    dw_out = matmul_tn(ycat, dh1_b, "dw_out")
    dw_gate = matmul_tn(n2_b, dgl_b, "dw_gate")
    dw_proj = matmul_tn(p_b, dpp_b, "dw_proj")
    dx, dnorm_g, *parts = in_proj_bwd(dsegs, wsegs, x, row(norm_g), dh1, prereduce(dw_in, dw_out, dw_gate, dw_proj))
    small_grads = [
        dnorm_g, dconv_b, ddtb[0, :N_HEADS], (da * a_row)[0, :N_HEADS],
        ddsk_lane.reshape(N_HEADS, HEAD_DIM).sum(axis=1), dssd_g, dfgb[0, N_HEADS:2 * N_HEADS],
        datt_lane.reshape(N_HEADS, HEAD_DIM).sum(axis=0), dple, dfin]
    loss = jnp.sum(loss_l)
    return loss, dx, parts, dconv_w8[:CONV_WIDTH], small_grads


def kernel(x, p, norm_g, w_in, conv_w, conv_b, dt_bias, a_log, d_skip, ssd_norm_g, fg_bias, att_norm_g, w_out, ple_norm_g, w_ple_gate, w_ple_proj, final_norm_g, loss_target, m_norm_g, m_w_in, m_conv_w, m_conv_b, m_dt_bias, m_a_log, m_d_skip, m_ssd_norm_g, m_fg_bias, m_att_norm_g, m_w_out, m_ple_norm_g, m_w_ple_gate, m_w_ple_proj, m_final_norm_g, v_norm_g, v_w_in, v_conv_w, v_conv_b, v_dt_bias, v_a_log, v_d_skip, v_ssd_norm_g, v_fg_bias, v_att_norm_g, v_w_out, v_ple_norm_g, v_w_ple_gate, v_w_ple_proj, v_final_norm_g):
    chip = 2 * lax.axis_index("x") + lax.axis_index("y")
    core = lax.axis_index("c")

    big_w = [w_in[0], w_out[0], w_ple_gate[0], w_ple_proj[0]]
    own = [a.astype(BF16) for a in big_w] + [conv_w[0]]

    def joined(mine, gathered, axis):
        return jnp.concatenate([jnp.where(chip == j, mine, gathered[j]) for j in range(N_CHIPS)], axis=axis)

    w_in_all, conv_all = gather_weights(own[:1], own[4])
    w_in_f, conv_w_f = joined(own[0], w_in_all, 1), joined(own[4], conv_all, 1)

    def join_later(gathered):
        return [joined(mine, got, axis) for mine, got, axis in zip(own[1:4], gathered, (0, 0, 1))]

    core1 = core.reshape(1).astype(jnp.int32)

    def prereduce(dw_in, dw_out, dw_gate, dw_proj):
        n_in, n_proj = w_in.shape[2], w_ple_proj.shape[2]
        gs = [jnp.stack([dw_in[:, n_in * j:n_in * (j + 1)] for j in range(N_CHIPS)]),
              dw_out.reshape(N_CHIPS, w_out.shape[1], D_MODEL), dw_gate.reshape(N_CHIPS, w_ple_gate.shape[1], D_MODEL),
              jnp.stack([dw_proj[:, n_proj * j:n_proj * (j + 1)] for j in range(N_CHIPS)])]
        return add_halves(core1, gs, halves_to_sibling(gs))

    smalls_w = [norm_g, conv_b, dt_bias, a_log, d_skip, ssd_norm_g, fg_bias, att_norm_g, ple_norm_g, final_norm_g]
    loss_l, dx, parts, dconv_w, small_grads = local_step(
        prereduce, own[1:4], join_later, x[0], p[0, 0], loss_target[0], w_in_f, conv_w_f,
        *[a.reshape(-1) for a in smalls_w])
    loss = lax.psum(loss_l, ("x", "y", "c"))
    smalls = gather_small(_pack_small(list(small_grads) + [dconv_w]))
    mine = sum_parts(parts)

    g_big, d_big, m_big, v_big = adamw_big(
        core1, mine, swap_halves(mine), big_w, [m_w_in[0], m_w_out[0], m_w_ple_gate[0], m_w_ple_proj[0]],
        [v_w_in[0], v_w_out[0], v_w_ple_gate[0], v_w_ple_proj[0]])
    smalls_m = [m_norm_g, m_conv_b, m_dt_bias, m_a_log, m_d_skip, m_ssd_norm_g, m_fg_bias, m_att_norm_g,
                m_ple_norm_g, m_final_norm_g]
    smalls_v = [v_norm_g, v_conv_b, v_dt_bias, v_a_log, v_d_skip, v_ssd_norm_g, v_fg_bias, v_att_norm_g,
                v_ple_norm_g, v_final_norm_g]
    g_sm, d_sm, m_sm, v_sm = adamw_small(smalls, _pack_small(smalls_w), _pack_small(smalls_m), _pack_small(smalls_v))
    n_small = sum(SMALL_SIZES)
    g_conv_full = g_sm.reshape(-1)[n_small:n_small + CONV_W_SIZE].reshape(CONV_WIDTH, CONV_CH)
    n_conv = conv_w.shape[2]
    g_conv = lax.dynamic_slice_in_dim(g_conv_full, chip * n_conv, n_conv, axis=1)
    d_conv, m_conv, v_conv = adamw_whole(g_conv, conv_w[0], m_conv_w[0], v_conv_w[0], "adamw_conv")

    shapes = [a.shape for a in smalls_w]
    outs = []
    for big, conv, sm in ((g_big, g_conv, g_sm), (d_big, d_conv, d_sm), (m_big, m_conv, m_sm), (v_big, v_conv, v_sm)):
        b_in, b_out, b_gate, b_proj = [a[None] for a in big]
        s_norm, s_convb, s_dtb, s_alog, s_dsk, s_ssdg, s_fgb, s_attg, s_pleg, s_fin = _unpack_small(sm, shapes)
        outs.extend([s_norm, b_in, conv[None], s_convb, s_dtb, s_alog, s_dsk, s_ssdg, s_fgb, s_attg, b_out, s_pleg,
                     b_gate, b_proj, s_fin])
    return (loss, dx[None], *outs)
```

```python
import functools

import jax
import jax.numpy as jnp
from jax import lax
from jax.experimental import pallas as pl
from jax.experimental.pallas import tpu as pltpu

F32 = jnp.float32
BF16 = jnp.bfloat16

D_MODEL = 1024
SSD_WIDTH = 1024
ATT_WIDTH = 1024
N_HEADS = 16
HEAD_DIM = 64
N_GROUPS = 2
D_STATE = 128
CONV_CH = 1536
CONV_WIDTH = 4
CHUNK = 128
PLE_DIM = 256
D_INNER = 2048
EPS = 1e-6
IN_COLS = 6688
N_CHIPS = 4
N_DEV = 8
LANES = 128
N_PAIRS = 8

ADAM_LR = 0.001
ADAM_B1 = 0.9
ADAM_B2 = 0.999
ADAM_EPS = 1e-08
ADAM_WD = 0.01
ADAM_STEP = 10

SMALL_ROWS = 96

NEG_BIG = -1e30
VMEM_LIMIT = 56 * 1024 * 1024

MESH = pl.DeviceIdType.MESH
ANY = pl.BlockSpec(memory_space=pl.ANY)


def _mm(a, b):
    return jnp.dot(a, b, preferred_element_type=F32)


def _mm_nt(a, b):
    return lax.dot_general(a, b, (((1,), (1,)), ((), ())), preferred_element_type=F32)


def _mm_tn(a, b):
    return lax.dot_general(a, b, (((0,), (0,)), ((), ())), preferred_element_type=F32)


def _mm_exact(a, b):
    return jnp.dot(a, b, preferred_element_type=F32, precision=lax.Precision.HIGHEST)


def _softplus(x):
    return jnp.maximum(x, 0.0) + jnp.log1p(jnp.exp(-jnp.abs(x)))


def _sigmoid(x):
    return jax.nn.sigmoid(x)


def _iota(shape, dim):
    return lax.broadcasted_iota(jnp.int32, shape, dim)


def _params(sem=None):
    return pltpu.CompilerParams(dimension_semantics=sem, vmem_limit_bytes=VMEM_LIMIT)


def _blk(n, pref):
    return min(n, pref)


def _const_spec(shape):
    nd = len(shape)
    return pl.BlockSpec(shape, lambda *_: (0,) * nd)


def _chip_peers():
    x, y, c = lax.axis_index("x"), lax.axis_index("y"), lax.axis_index("c")
    return x, y, c, [(1 - x, y, c), (x, 1 - y, c), (1 - x, 1 - y, c)]


def _half(rows, c):
    h = rows // 2
    return pl.ds(pl.multiple_of(c * h, 8), h)


def _sems(n):
    return [pltpu.SemaphoreType.DMA((n,)), pltpu.SemaphoreType.DMA((n,))]


def gather_copies(ins, outs, ssem1, rsem1, ssem2, rsem2):
    n = len(ins)
    x, y, c, peers = _chip_peers()
    me = 2 * x + y
    fetched, passed = [], []
    for k, peer in enumerate(peers):
        chip = 2 * peer[0] + peer[1]
        for i in range(n):
            h = _half(ins[i].shape[0], c)
            fetched.append(pltpu.make_async_remote_copy(
                src_ref=ins[i].at[h], dst_ref=outs[i].at[me, h], send_sem=ssem1.at[n * k + i],
                recv_sem=rsem1.at[n * k + i], device_id=peer, device_id_type=MESH))
            passed.append(pltpu.make_async_remote_copy(
                src_ref=outs[i].at[chip, h], dst_ref=outs[i].at[chip, h], send_sem=ssem2.at[n * k + i],
                recv_sem=rsem2.at[n * k + i], device_id=(x, y, 1 - c), device_id_type=MESH))
    return fetched, passed


def gather_weights(shards, conv_s):
    n = len(shards)

    def body(*refs):
        ins, conv_in = refs[:n], refs[n]
        outs, conv_out = refs[n + 1:2 * n + 1], refs[2 * n + 1]
        ssem1, rsem1, ssem2, rsem2, c_ssem, c_rsem = refs[2 * n + 2:]
        x, y, _, peers = _chip_peers()
        fetched, passed = gather_copies(ins, outs, ssem1, rsem1, ssem2, rsem2)
        small = [pltpu.make_async_remote_copy(
            src_ref=conv_in, dst_ref=conv_out.at[2 * x + y], send_sem=c_ssem.at[k], recv_sem=c_rsem.at[k],
            device_id=peer, device_id_type=MESH) for k, peer in enumerate(peers)]
        for cp in fetched + small:
            cp.start()
        for landed, onward in zip(fetched, passed):
            landed.wait_recv()
            onward.start()
        for cp in passed:
            cp.wait_recv()
        for cp in fetched + passed:
            cp.wait_send()
        for cp in small:
            cp.wait()

    return pl.pallas_call(
        body, name="gather_weights",
        out_shape=tuple(jax.ShapeDtypeStruct((N_CHIPS,) + a.shape, a.dtype) for a in list(shards) + [conv_s]),
        in_specs=[ANY] * (n + 1), out_specs=(ANY,) * (n + 1),
        scratch_shapes=_sems(3 * n) + _sems(3 * n) + _sems(3),
    )(*shards, conv_s)


def halves_to_sibling(gs):
    n = len(gs)

    def body(*refs):
        ins, outs = refs[:n], refs[n:2 * n]
        ssem, rsem = refs[2 * n:]
        x, y, c = lax.axis_index("x"), lax.axis_index("y"), lax.axis_index("c")
        copies = []
        for i in range(n):
            for j in range(N_CHIPS):
                copies.append(pltpu.make_async_remote_copy(
                    src_ref=ins[i].at[j, _half(ins[i].shape[1], 1 - c)], dst_ref=outs[i].at[j],
                    send_sem=ssem.at[N_CHIPS * i + j], recv_sem=rsem.at[N_CHIPS * i + j],
                    device_id=(x, y, 1 - c), device_id_type=MESH))
        for cp in copies:
            cp.start()
        for cp in copies:
            cp.wait()

    return pl.pallas_call(
        body, name="halves_to_sibling",
        out_shape=tuple(jax.ShapeDtypeStruct((N_CHIPS, g.shape[1] // 2, g.shape[2]), F32) for g in gs),
        in_specs=[ANY] * n, out_specs=(ANY,) * n, scratch_shapes=_sems(N_CHIPS * n),
    )(*gs)


RED_GRID = 8


def add_halves(core, gs, rbs):
    n = len(gs)

    def body(c_ref, *refs):
        for i in range(n):
            refs[2 * n + i][...] = (refs[i][...] + refs[n + i][...]).astype(BF16)

    def blk(g):
        return (1, g.shape[1] // 2 // RED_GRID, g.shape[2])

    grid_spec = pltpu.PrefetchScalarGridSpec(
        num_scalar_prefetch=1, grid=(N_CHIPS, RED_GRID),
        in_specs=([pl.BlockSpec(blk(g), lambda j, b, c_ref: (j, c_ref[0] * RED_GRID + b, 0)) for g in gs]
                  + [pl.BlockSpec(blk(g), lambda j, b, c_ref: (j, b, 0)) for g in gs]),
        out_specs=[pl.BlockSpec(blk(g), lambda j, b, c_ref: (j, b, 0)) for g in gs])
    return pl.pallas_call(
        body, name="add_halves", grid_spec=grid_spec,
        out_shape=tuple(jax.ShapeDtypeStruct(r.shape, BF16) for r in rbs),
        compiler_params=_params(("parallel", "parallel")),
    )(core, *gs, *rbs)


def scatter_copies(ins, outs, ssem, rsem, lsem):
    n = len(ins)
    x, y, _, peers = _chip_peers()
    me = 2 * x + y
    copies = [pltpu.make_async_copy(ins[i].at[me], outs[i].at[me], lsem.at[i]) for i in range(n)]
    for k, peer in enumerate(peers):
        dst_chip = 2 * peer[0] + peer[1]
        for i in range(n):
            copies.append(pltpu.make_async_remote_copy(
                src_ref=ins[i].at[dst_chip], dst_ref=outs[i].at[me], send_sem=ssem.at[n * k + i],
                recv_sem=rsem.at[n * k + i], device_id=peer, device_id_type=MESH))
    return copies


def gather_small(small):
    def body(s_ref, smalls_ref, ssem, rsem, lsem):
        x, y, c = lax.axis_index("x"), lax.axis_index("y"), lax.axis_index("c")
        dev = 4 * x + 2 * y + c
        copies = [pltpu.make_async_copy(s_ref, smalls_ref.at[dev], lsem)]
        for k in range(1, N_DEV):
            fx, fy, fc = (k >> 2) & 1, (k >> 1) & 1, k & 1
            peer = ((1 - x) if fx else x, (1 - y) if fy else y, (1 - c) if fc else c)
            copies.append(pltpu.make_async_remote_copy(
                src_ref=s_ref, dst_ref=smalls_ref.at[dev], send_sem=ssem.at[k - 1], recv_sem=rsem.at[k - 1],
                device_id=peer, device_id_type=MESH))
        for cp in copies:
            cp.start()
        for cp in copies:
            cp.wait()

    return pl.pallas_call(
        body, name="gather_small",
        out_shape=jax.ShapeDtypeStruct((N_DEV,) + small.shape, F32),
        in_specs=[ANY], out_specs=ANY,
        scratch_shapes=_sems(N_DEV - 1) + [pltpu.SemaphoreType.DMA],
    )(small)


def sum_parts(parts):
    n = len(parts)

    def body(*refs):
        for i in range(n):
            p_ref = refs[i]
            refs[n + i][...] = ((p_ref[0].astype(F32) + p_ref[1].astype(F32)) + p_ref[2].astype(F32)
                                ) + p_ref[3].astype(F32)

    def rows(p):
        return p.shape[1] // RED_GRID

    return pl.pallas_call(
        body, name="sum_parts",
        out_shape=tuple(jax.ShapeDtypeStruct(p.shape[1:], F32) for p in parts),
        grid=(RED_GRID,),
        in_specs=[pl.BlockSpec((N_CHIPS, rows(p), p.shape[2]), lambda b: (0, b, 0)) for p in parts],
        out_specs=tuple(pl.BlockSpec((rows(p), p.shape[2]), lambda b: (b, 0)) for p in parts),
        compiler_params=_params(("parallel",)),
    )(*parts)


def swap_halves(reds):
    n = len(reds)

    def body(*refs):
        ins, outs = refs[:n], refs[n:2 * n]
        ssem, rsem = refs[2 * n:]
        x, y, c = lax.axis_index("x"), lax.axis_index("y"), lax.axis_index("c")
        copies = [pltpu.make_async_remote_copy(
            src_ref=ins[i], dst_ref=outs[i], send_sem=ssem.at[i], recv_sem=rsem.at[i],
            device_id=(x, y, 1 - c), device_id_type=MESH) for i in range(n)]
        for cp in copies:
            cp.start()
        for cp in copies:
            cp.wait()

    return pl.pallas_call(
        body, name="swap_halves",
        out_shape=tuple(jax.ShapeDtypeStruct(r.shape, F32) for r in reds),
        in_specs=[ANY] * n, out_specs=(ANY,) * n, scratch_shapes=_sems(n),
    )(*reds)


def _adamw(w, g, m, v):
    m = ADAM_B1 * m + (1.0 - ADAM_B1) * g
    v = ADAM_B2 * v + (1.0 - ADAM_B2) * (g * g)
    m_hat = m / (1.0 - ADAM_B1 ** ADAM_STEP)
    v_hat = v / (1.0 - ADAM_B2 ** ADAM_STEP)
    delta = -ADAM_LR * (m_hat / (jnp.sqrt(v_hat) + ADAM_EPS) + ADAM_WD * w)
    return delta, m, v


def adamw_big(core, mine, theirs, ws, ms, vs):
    n = len(ws)
    per_half = RED_GRID // 2

    def body(c_ref, *refs):
        own = (pl.program_id(0) // per_half) == c_ref[0]
        for i in range(n):
            g = jnp.where(own, refs[i][...], refs[n + i][...])
            d, mn, vn = _adamw(refs[2 * n + i][...], g, refs[3 * n + i][...], refs[4 * n + i][...])
            refs[5 * n + i][...] = g
            refs[6 * n + i][...] = d
            refs[7 * n + i][...] = mn
            refs[8 * n + i][...] = vn

    def blk(w):
        return (w.shape[0] // RED_GRID, w.shape[1])

    halves = [pl.BlockSpec(blk(w), lambda b, c_ref: (b % per_half, 0)) for w in ws]
    whole = [pl.BlockSpec(blk(w), lambda b, c_ref: (b, 0)) for w in ws]
    shapes = [jax.ShapeDtypeStruct(w.shape, F32) for w in ws]
    grid_spec = pltpu.PrefetchScalarGridSpec(
        num_scalar_prefetch=1, grid=(RED_GRID,), in_specs=halves * 2 + whole * 3, out_specs=whole * 4)
    outs = pl.pallas_call(
        body, name="adamw_big", out_shape=tuple(shapes * 4), grid_spec=grid_spec,
        compiler_params=_params(("parallel",)),
    )(core, *mine, *theirs, *ws, *ms, *vs)
    return outs[:n], outs[n:2 * n], outs[2 * n:3 * n], outs[3 * n:]


def adamw_whole(g, w, m, v, name):
    def body(g_ref, w_ref, m_ref, v_ref, d_out, m_out, v_out):
        d, mn, vn = _adamw(w_ref[...], g_ref[...], m_ref[...], v_ref[...])
        d_out[...] = d
        m_out[...] = mn
        v_out[...] = vn

    shp = jax.ShapeDtypeStruct(g.shape, F32)
    return pl.pallas_call(body, name=name, out_shape=(shp,) * 3)(g, w, m, v)


def adamw_small(smalls, w, m, v):
    def body(s_ref, w_ref, m_ref, v_ref, g_out, d_out, m_out, v_out):
        g = s_ref[0]
        for k in range(1, N_DEV):
            g = g + s_ref[k]
        d, mn, vn = _adamw(w_ref[...], g, m_ref[...], v_ref[...])
        g_out[...] = g
        d_out[...] = d
        m_out[...] = mn
        v_out[...] = vn

    shp = jax.ShapeDtypeStruct((SMALL_ROWS, LANES), F32)
    return pl.pallas_call(body, name="adamw_small", out_shape=(shp,) * 4)(smalls, w, m, v)


def in_proj_fwd(x, g, ws):
    s = x.shape[0]
    tm = _blk(s, 512)
    n = len(ws)

    def body(x_ref, g_ref, *refs):
        xv = x_ref[...]
        r = lax.rsqrt(jnp.mean(xv * xv, axis=-1, keepdims=True) + EPS)
        u = (xv * r * g_ref[...]).astype(BF16)
        refs[n][...] = u
        for i in range(n):
            refs[n + 1 + i][...] = _mm(u, refs[i][...])

    rows = lambda width: pl.BlockSpec((tm, width), lambda i: (i, 0))
    return pl.pallas_call(
        body, name="in_proj_fwd",
        out_shape=tuple([jax.ShapeDtypeStruct((s, D_MODEL), BF16)]
                        + [jax.ShapeDtypeStruct((s, w.shape[1]), F32) for w in ws]),
        grid=(s // tm,),
        in_specs=[rows(D_MODEL), _const_spec((1, D_MODEL))] + [_const_spec(w.shape) for w in ws],
        out_specs=tuple([rows(D_MODEL)] + [rows(w.shape[1]) for w in ws]),
        compiler_params=_params(("parallel",)),
    )(x, g, *ws)


def matmul_tn(a, b, name):
    s, m = a.shape
    n = b.shape[1]
    tk = _blk(s, 4096 if m <= D_MODEL else 2048)
    tn = _blk(n, 512) if m > D_MODEL else (n // 2 if n > D_MODEL else n)

    def body(a_ref, b_ref, o_ref):
        @pl.when(pl.program_id(1) == 0)
        def _():
            o_ref[...] = jnp.zeros_like(o_ref)

        o_ref[...] += _mm_tn(a_ref[...], b_ref[...])

    return pl.pallas_call(
        body, name=name, out_shape=jax.ShapeDtypeStruct((m, n), F32), grid=(n // tn, s // tk),
        in_specs=[pl.BlockSpec((tk, m), lambda j, i: (i, 0)), pl.BlockSpec((tk, tn), lambda j, i: (i, j))],
        out_specs=pl.BlockSpec((m, tn), lambda j, i: (0, j)),
        compiler_params=_params(("parallel", "arbitrary")),
    )(a, b)


def conv_fwd(xbc, w, b):
    s = xbc.shape[0]
    tm = _blk(s, 256)

    def body(x_ref, t_ref, w_ref, b_ref, pre_ref, act_ref):
        i = pl.program_id(0)
        row8 = _iota((8, LANES), 0)
        for c0 in range(0, CONV_CH, LANES):
            cols = slice(c0, c0 + LANES)
            cur = x_ref[:, cols]
            tail = jnp.where(i > 0, t_ref[:, cols], 0.0)
            wv = w_ref[:, cols]
            bias = b_ref[:, cols]
            acc = cur * wv[3:4, :] + bias
            head = cur[0:8, :] * wv[3:4, :] + bias
            for sh in range(1, CONV_WIDTH):
                wk = wv[3 - sh:4 - sh, :]
                acc = acc + pltpu.roll(cur, sh, 0) * wk
                first = jnp.where(row8 < sh, pltpu.roll(tail, sh, 0), pltpu.roll(cur[0:8, :], sh, 0))
                head = head + first * wk
            pre_ref[:, cols] = acc
            act_ref[:, cols] = acc * _sigmoid(acc)
            pre_ref[0:8, cols] = head
            act_ref[0:8, cols] = head * _sigmoid(head)

    shp = jax.ShapeDtypeStruct(xbc.shape, F32)
    rows = pl.BlockSpec((tm, CONV_CH), lambda i: (i, 0))
    return pl.pallas_call(
        body, name="conv_fwd", out_shape=(shp, shp), grid=(s // tm,),
        in_specs=[rows, pl.BlockSpec((8, CONV_CH), lambda i: (jnp.maximum(i * (tm // 8) - 1, 0), 0)),
                  _const_spec((CONV_WIDTH, CONV_CH)), _const_spec((1, CONV_CH))],
        out_specs=(rows, rows), compiler_params=_params(("parallel",)),
    )(xbc, xbc, w, b)


def conv_bwd(xbc, pre, dact, w):
    s = xbc.shape[0]
    tm = _blk(s, 256)
    nb = s // tm

    def dsilu(p):
        sg = _sigmoid(p)
        return sg * (1.0 + p * (1.0 - sg))

    def body(x_ref, xt_ref, p_ref, pn_ref, d_ref, dn_ref, w_ref, dx_ref, dw_ref, db_ref):
        i = pl.program_id(0)

        @pl.when(i == 0)
        def _():
            dw_ref[...] = jnp.zeros_like(dw_ref)
            db_ref[...] = jnp.zeros_like(db_ref)

        row8 = _iota((8, LANES), 0)
        for c0 in range(0, CONV_CH, LANES):
            cols = slice(c0, c0 + LANES)
            wv = w_ref[:, cols]
            dpre = d_ref[:, cols] * dsilu(p_ref[:, cols])
            dnext = jnp.where(i < nb - 1, dn_ref[:, cols] * dsilu(pn_ref[:, cols]), 0.0)
            cur = x_ref[:, cols]
            tail = jnp.where(i > 0, xt_ref[:, cols], 0.0)
            dx = dpre * wv[3:4, :]
            last = dpre[tm - 8:tm, :] * wv[3:4, :]
            db_ref[:, cols] += jnp.sum(dpre, axis=0, keepdims=True)
            dws = [jnp.sum(dpre * cur, axis=0, keepdims=True)]
            for sh in range(1, CONV_WIDTH):
                wk = wv[3 - sh:4 - sh, :]
                dx = dx + pltpu.roll(dpre, tm - sh, 0) * wk
                nxt = jnp.where(row8 >= 8 - sh, pltpu.roll(dnext, 8 - sh, 0),
                                pltpu.roll(dpre[tm - 8:tm, :], 8 - sh, 0))
                last = last + nxt * wk
                xs = pltpu.roll(cur, sh, 0)
                first = jnp.where(row8 < sh, pltpu.roll(tail, sh, 0), xs[0:8, :])
                dws.append(jnp.sum(dpre * xs, axis=0, keepdims=True)
                           + jnp.sum(dpre[0:8, :] * (first - xs[0:8, :]), axis=0, keepdims=True))
            dx_ref[:, cols] = dx.astype(BF16)
            dx_ref[tm - 8:tm, cols] = last.astype(BF16)
            for sh in range(CONV_WIDTH):
                dw_ref[3 - sh:4 - sh, cols] += dws[sh]

    rows = pl.BlockSpec((tm, CONV_CH), lambda i: (i, 0))
    prev8 = pl.BlockSpec((8, CONV_CH), lambda i: (jnp.maximum(i * (tm // 8) - 1, 0), 0))
    next8 = pl.BlockSpec((8, CONV_CH), lambda i: (jnp.minimum((i + 1) * (tm // 8), s // 8 - 1), 0))
    return pl.pallas_call(
        body, name="conv_bwd",
        out_shape=(jax.ShapeDtypeStruct(xbc.shape, BF16), jax.ShapeDtypeStruct((8, CONV_CH), F32),
                   jax.ShapeDtypeStruct((1, CONV_CH), F32)),
        grid=(nb,),
        in_specs=[rows, prev8, rows, next8, rows, next8, _const_spec((CONV_WIDTH, CONV_CH))],
        out_specs=(rows, _const_spec((8, CONV_CH)), _const_spec((1, CONV_CH))),
        compiler_params=_params(("arbitrary",)),
    )(xbc, xbc, pre, pre, dact, dact, w)


def _pair_lanes(mat, j, lane):
    return jnp.where(lane < HEAD_DIM, mat[:, 2 * j:2 * j + 1], mat[:, 2 * j + 1:2 * j + 2])


def _ssd_chunk_prelude(sm, dtb, a_row, lane, sub):
    raw = sm + dtb
    head_lane = lane < N_HEADS
    dt = jnp.where(head_lane, _softplus(raw), 0.0)
    sig = jnp.where(head_lane, _sigmoid(raw), 0.0)
    tri = (lane <= sub).astype(F32)
    acs = _mm_exact(tri, dt * a_row)
    return dt, sig, acs, acs.T


GROUP_WIDTH = SSD_WIDTH // N_GROUPS
HEADS_PER_GROUP = N_HEADS // N_GROUPS


def _expand_group(mat, g, lane):
    return jnp.concatenate([_pair_lanes(mat, j, lane) for j in range(4 * g, 4 * g + 4)], axis=1)


def _head_sums(q, g):
    row = _iota((GROUP_WIDTH, LANES), 0)
    seg = (_iota((GROUP_WIDTH, LANES), 1) == HEADS_PER_GROUP * g + (row >> 6)).astype(BF16)
    hi = q.astype(BF16)
    lo = (q - hi.astype(F32)).astype(BF16)
    return _mm(hi, seg) + _mm(lo, seg)


def _rows_from_lanes(row512):
    return jnp.broadcast_to(row512, (LANES, GROUP_WIDTH)).T


def ssd_fwd(xc, small, dtb_row, a_row, dskip_lane):
    s = xc.shape[0]
    nc = s // CHUNK

    def body(xc_ref, sm_ref, dtb_ref, a_ref, dsk_ref, y_ref, hs_ref, h_scr):
        c = pl.program_id(0)

        @pl.when(c == 0)
        def _():
            h_scr[...] = jnp.zeros_like(h_scr)

        lane = _iota((CHUNK, LANES), 1)
        sub = _iota((CHUNK, LANES), 0)
        causal = lane <= sub
        dt, _, acs, acs_t = _ssd_chunk_prelude(sm_ref[...], dtb_ref[...], a_ref[...], lane, sub)
        for g in range(N_GROUPS):
            cols = slice(GROUP_WIDTH * g, GROUP_WIDTH * (g + 1))
            b_off = SSD_WIDTH + D_STATE * g
            c_off = SSD_WIDTH + N_GROUPS * D_STATE + D_STATE * g
            b_b = xc_ref[:, b_off:b_off + D_STATE].astype(BF16)
            c_b = xc_ref[:, c_off:c_off + D_STATE].astype(BF16)
            cb = _mm_nt(c_b, b_b)
            x_g = xc_ref[:, cols]
            acs_g = _expand_group(acs, g, lane)
            xdt_g = x_g * _expand_group(dt, g, lane)
            xdt_b = xdt_g.astype(BF16)
            heads = range(HEADS_PER_GROUP * g, HEADS_PER_GROUP * (g + 1))
            m_b = [(cb * jnp.exp(jnp.where(causal, acs[:, h:h + 1] - acs_t[h:h + 1, :], NEG_BIG))).astype(BF16)
                   for h in heads]
            yd = [_mm(m_b[k], xdt_b[:, LANES * (k // 2):LANES * (k // 2 + 1)]) for k in range(HEADS_PER_GROUP)]
            yd_g = jnp.concatenate([jnp.where(lane < HEAD_DIM, yd[2 * k], yd[2 * k + 1]) for k in range(4)], axis=1)
            h_g = h_scr[g]
            t_g = _mm_nt(c_b, h_g.astype(BF16))
            y_ref[:, cols] = yd_g + jnp.exp(acs_g) * t_g + dsk_ref[:, cols] * x_g
            hs_ref[0, g] = h_g
            last_g = acs_g[CHUNK - 1:CHUNK, :]
            w_b = (xdt_g * jnp.exp(last_g - acs_g)).astype(BF16)
            h_scr[g] = h_g * jnp.exp(_rows_from_lanes(last_g)) + _mm_tn(w_b, b_b)

    return pl.pallas_call(
        body, name="ssd_fwd",
        out_shape=(jax.ShapeDtypeStruct((s, SSD_WIDTH), F32),
                   jax.ShapeDtypeStruct((nc, N_GROUPS, GROUP_WIDTH, D_STATE), F32)),
        grid=(nc,),
        in_specs=[pl.BlockSpec((CHUNK, CONV_CH), lambda c: (c, 0)), pl.BlockSpec((CHUNK, LANES), lambda c: (c, 0)),
                  _const_spec((1, LANES)), _const_spec((1, LANES)), _const_spec((1, SSD_WIDTH))],
        out_specs=(pl.BlockSpec((CHUNK, SSD_WIDTH), lambda c: (c, 0)),
                   pl.BlockSpec((1, N_GROUPS, GROUP_WIDTH, D_STATE), lambda c: (c, 0, 0, 0))),
        scratch_shapes=[pltpu.VMEM((N_GROUPS, GROUP_WIDTH, D_STATE), F32)],
        compiler_params=_params(("arbitrary",)),
    )(xc, small, dtb_row, a_row, dskip_lane)


def ssd_bwd(xc, small, states, dy, dtb_row, a_row, dskip_lane):
    s = xc.shape[0]
    nc = s // CHUNK
    rev = lambda c: nc - 1 - c

    def body(xc_ref, sm_ref, hs_ref, dy_ref, dtb_ref, a_ref, dsk_ref,
             dxc_ref, ddt_ref, da_ref, ddtb_ref, ddsk_ref, dh_scr):
        c = pl.program_id(0)

        @pl.when(c == 0)
        def _():
            dh_scr[...] = jnp.zeros_like(dh_scr)
            da_ref[...] = jnp.zeros_like(da_ref)
            ddtb_ref[...] = jnp.zeros_like(ddtb_ref)
            ddsk_ref[...] = jnp.zeros_like(ddsk_ref)

        lane = _iota((CHUNK, LANES), 1)
        sub = _iota((CHUNK, LANES), 0)
        causal = lane <= sub
        upper = lane >= sub
        is_last = sub == CHUNK - 1
        a_row_v = a_ref[...]
        dt, sig, acs, acs_t = _ssd_chunk_prelude(sm_ref[...], dtb_ref[...], a_row_v, lane, sub)
        cd = jnp.exp(acs[CHUNK - 1:CHUNK, :])
        dacs_c = jnp.zeros((CHUNK, LANES), F32)
        dacs_r = jnp.zeros((LANES, CHUNK), F32)
        ddtx = jnp.zeros((CHUNK, LANES), F32)
        for g in range(N_GROUPS):
            cols = slice(GROUP_WIDTH * g, GROUP_WIDTH * (g + 1))
            b_off = SSD_WIDTH + D_STATE * g
            c_off = SSD_WIDTH + N_GROUPS * D_STATE + D_STATE * g
            b_b = xc_ref[:, b_off:b_off + D_STATE].astype(BF16)
            c_b = xc_ref[:, c_off:c_off + D_STATE].astype(BF16)
            cb = _mm_nt(c_b, b_b)
            cb_t = _mm_nt(b_b, c_b)
            x_g = xc_ref[:, cols]
            dy_g = dy_ref[:, cols]
            dt_g = _expand_group(dt, g, lane)
            acs_g = _expand_group(acs, g, lane)
            last_g = acs_g[CHUNK - 1:CHUNK, :]
            e_g = jnp.exp(acs_g)
            dte_g = jnp.exp(last_g - acs_g)
            xdt_g = x_g * dt_g
            xdt_b = xdt_g.astype(BF16)
            h_g = hs_ref[0, g]
            dh_g = dh_scr[g]
            h_b = h_g.astype(BF16)
            dh_b = dh_g.astype(BF16)
            heads = list(range(HEADS_PER_GROUP * g, HEADS_PER_GROUP * (g + 1)))
            segs = [acs[:, h:h + 1] - acs_t[h:h + 1, :] for h in heads]
            lms = [jnp.exp(jnp.where(causal, sg, NEG_BIG)) for sg in segs]
            mts = [(cb_t * jnp.exp(jnp.where(upper, -sg, NEG_BIG))).astype(BF16) for sg in segs]
            dyh = []
            for k in range(HEADS_PER_GROUP):
                blk = dy_g[:, LANES * (k // 2):LANES * (k // 2 + 1)]
                in_head = (lane < HEAD_DIM) if k % 2 == 0 else (lane >= HEAD_DIM)
                dyh.append(jnp.where(in_head, blk, 0.0).astype(BF16))
            dms = [_mm_nt(dyh[k], xdt_b[:, LANES * (k // 2):LANES * (k // 2 + 1)]) for k in range(HEADS_PER_GROUP)]
            dxs = [_mm(mts[k], dyh[k]) for k in range(HEADS_PER_GROUP)]
            dcb = jnp.zeros((CHUNK, CHUNK), F32)
            for k, h in enumerate(heads):
                gmat = dms[k] * (cb * lms[k])
                dacs_c = dacs_c + jnp.where(lane == h, jnp.sum(gmat, axis=1, keepdims=True), 0.0)
                dacs_r = dacs_r - jnp.where(sub == h, jnp.sum(gmat, axis=0, keepdims=True), 0.0)
                dcb = dcb + dms[k] * lms[k]
            dxdt_g = jnp.concatenate([dxs[2 * k] + dxs[2 * k + 1] for k in range(4)], axis=1)
            t_g = _mm_nt(c_b, h_b)
            dacs_c = dacs_c + _head_sums(dy_g * e_g * t_g, g)
            dt_b = (dy_g * e_g).astype(BF16)
            dc_acc = _mm(dt_b, h_b)
            dh_prev = _mm_tn(dt_b, c_b)
            dw_g = _mm_nt(b_b, dh_b)
            w_g = xdt_g * dte_g
            dxdt_g = dxdt_g + dw_g * dte_g
            db_acc = _mm(w_g.astype(BF16), dh_b)
            r2 = _head_sums(dw_g * w_g, g)
            dacs_c = dacs_c + jnp.where(is_last, jnp.sum(r2, axis=0, keepdims=True), 0.0) - r2
            q3 = jnp.sum(dh_g * h_g, axis=1, keepdims=True)
            for k, h in enumerate(heads):
                tot = jnp.sum(q3[HEAD_DIM * k:HEAD_DIM * (k + 1), :], keepdims=True) * cd[:, h:h + 1]
                dacs_c = dacs_c + jnp.where(is_last & (lane == h), tot, 0.0)
            dh_scr[g] = dh_prev + dh_g * jnp.exp(_rows_from_lanes(last_g))
            dxc_ref[:, cols] = dxdt_g * dt_g + dsk_ref[:, cols] * dy_g
            ddtx = ddtx + _head_sums(dxdt_g * x_g, g)
            ddsk_ref[:, cols] += jnp.sum(dy_g * x_g, axis=0, keepdims=True)
            dxc_ref[:, b_off:b_off + D_STATE] = db_acc + _mm(dcb.T.astype(BF16), c_b)
            dxc_ref[:, c_off:c_off + D_STATE] = dc_acc + _mm(dcb.astype(BF16), b_b)
        dacs = dacs_c + dacs_r.T
        dadt = _mm_exact((lane >= sub).astype(F32), dacs)
        ddt = dadt * a_row_v + ddtx
        ddt_raw = ddt * sig
        ddt_ref[...] = ddt_raw
        da_ref[...] += jnp.sum(dadt * dt, axis=0, keepdims=True)
        ddtb_ref[...] += jnp.sum(ddt_raw, axis=0, keepdims=True)

    return pl.pallas_call(
        body, name="ssd_bwd",
        out_shape=(jax.ShapeDtypeStruct((s, CONV_CH), F32), jax.ShapeDtypeStruct((s, LANES), F32),
                   jax.ShapeDtypeStruct((1, LANES), F32), jax.ShapeDtypeStruct((1, LANES), F32),
                   jax.ShapeDtypeStruct((1, SSD_WIDTH), F32)),
        grid=(nc,),
        in_specs=[pl.BlockSpec((CHUNK, CONV_CH), lambda c: (rev(c), 0)),
                  pl.BlockSpec((CHUNK, LANES), lambda c: (rev(c), 0)),
                  pl.BlockSpec((1, N_GROUPS, GROUP_WIDTH, D_STATE), lambda c: (rev(c), 0, 0, 0)),
                  pl.BlockSpec((CHUNK, SSD_WIDTH), lambda c: (rev(c), 0)),
                  _const_spec((1, LANES)), _const_spec((1, LANES)), _const_spec((1, SSD_WIDTH))],
        out_specs=(pl.BlockSpec((CHUNK, CONV_CH), lambda c: (rev(c), 0)),
                   pl.BlockSpec((CHUNK, LANES), lambda c: (rev(c), 0)),
                   _const_spec((1, LANES)), _const_spec((1, LANES)), _const_spec((1, SSD_WIDTH))),
        scratch_shapes=[pltpu.VMEM((N_GROUPS, GROUP_WIDTH, D_STATE), F32)],
        compiler_params=_params(("arbitrary",)),
    )(xc, small, states, dy, dtb_row, a_row, dskip_lane)


FORGET_BLOCK = 512


def forget_cumsum(small, fgb_row):
    s = small.shape[0]
    t = _blk(s, FORGET_BLOCK)
    nb = s // t

    def body(sm_ref, b_ref, cc_ref, carry):
        i = pl.program_id(0)

        @pl.when(i == 0)
        def _():
            carry[...] = jnp.zeros_like(carry)

        lane = _iota((t, LANES), 1)
        in_f = (lane >= N_HEADS) & (lane < 2 * N_HEADS)
        logf = jnp.where(in_f, -_softplus(-(sm_ref[...] + b_ref[...])), 0.0)
        tri = (_iota((t, t), 1) <= _iota((t, t), 0)).astype(F32)
        cum = _mm_exact(tri, logf) + carry[0:1, :]
        cc_ref[...] = cum
        carry[...] = jnp.broadcast_to(cum[t - 1:t, :], (8, LANES))

    return pl.pallas_call(
        body, name="forget_cumsum",
        out_shape=jax.ShapeDtypeStruct((s, LANES), F32),
        grid=(nb,),
        in_specs=[pl.BlockSpec((t, LANES), lambda i: (i, 0)), _const_spec((1, LANES))],
        out_specs=pl.BlockSpec((t, LANES), lambda i: (i, 0)),
        scratch_shapes=[pltpu.VMEM((8, LANES), F32)],
        compiler_params=_params(("arbitrary",)),
    )(small, fgb_row)


def forget_bwd(dc, small, ddt_raw, fgb_row):
    s = small.shape[0]
    t = _blk(s, FORGET_BLOCK)
    nb = s // t
    rev = lambda i: nb - 1 - i

    def body(dc_ref, sm_ref, ddt_ref, b_ref, ds_ref, dfb_ref, carry):
        i = pl.program_id(0)

        @pl.when(i == 0)
        def _():
            carry[...] = jnp.zeros_like(carry)
            dfb_ref[...] = jnp.zeros_like(dfb_ref)

        lane = _iota((t, LANES), 1)
        rows = dc_ref[...].T
        tri = (_iota((t, t), 1) <= _iota((t, t), 0)).astype(F32)
        rc = _mm_exact(rows, tri) + carry[:, 0:1]
        carry[...] = jnp.broadcast_to(rc[:, 0:1], (LANES, LANES))
        in_f = (lane >= N_HEADS) & (lane < 2 * N_HEADS)
        df = jnp.where(in_f, rc.T * _sigmoid(-(sm_ref[...] + b_ref[...])), 0.0)
        ds_ref[...] = (df + ddt_ref[...]).astype(BF16)
        dfb_ref[...] += jnp.sum(df, axis=0, keepdims=True)

    blk = pl.BlockSpec((t, LANES), lambda i: (rev(i), 0))
    return pl.pallas_call(
        body, name="forget_bwd",
        out_shape=(jax.ShapeDtypeStruct((s, LANES), BF16), jax.ShapeDtypeStruct((1, LANES), F32)),
        grid=(nb,),
        in_specs=[blk, blk, blk, _const_spec((1, LANES))],
        out_specs=(blk, _const_spec((1, LANES))),
        scratch_shapes=[pltpu.VMEM((LANES, LANES), F32)],
        compiler_params=_params(("arbitrary",)),
    )(dc, small, ddt_raw, fgb_row)


ATT_BLOCK = 1024
ATT_BLOCK_BWD = 512
ATT_BLOCK_BWD_Q = 512
ATT_SCALE = HEAD_DIM ** -0.5
AUG_A = HEAD_DIM
AUG_B = HEAD_DIM + 3


def _split3(c):
    hi = c.astype(BF16).astype(F32)
    r = c - hi
    mid = r.astype(BF16).astype(F32)
    return hi, mid, (r - mid).astype(BF16).astype(F32)


def _aug(lane, first, parts=None, value=1.0):
    if parts is None:
        return jnp.where((lane >= first) & (lane < first + 3), value, 0.0)
    return (jnp.where(lane == first, parts[0], 0.0) + jnp.where(lane == first + 1, parts[1], 0.0)
            + jnp.where(lane == first + 2, parts[2], 0.0))


def _pack_pair(a0, a1, lane):
    return jnp.where(lane < HEAD_DIM, a0, pltpu.roll(a1, HEAD_DIM, 1))


def proj_qkv_heads(u, w_q, w_k, w_v, cum, later):
    s = u.shape[0]
    tm = _blk(s, 256)
    nsteps = s // tm
    n_later = len(later)

    def body(u_ref, wq_ref, wk_ref, wv_ref, c_ref, *rest):
        later_in = rest[:n_later]
        qa_ref, ka_ref, va_ref, nrm_ref = rest[n_later:n_later + 4]
        later_out = rest[n_later + 4:2 * n_later + 4]
        sems = rest[2 * n_later + 4:]
        step = pl.program_id(0)

        @pl.when(step == 0)
        def _():
            for cp in gather_copies(later_in, later_out, *sems)[0]:
                cp.start()

        @pl.when(step == nsteps // 2)
        def _():
            for landed, onward in zip(*gather_copies(later_in, later_out, *sems)):
                landed.wait_recv()
                onward.start()

        @pl.when(step == nsteps - 1)
        def _():
            fetched, passed = gather_copies(later_in, later_out, *sems)
            for cp in passed:
                cp.wait_recv()
            for cp in fetched + passed:
                cp.wait_send()

        lane = _iota((tm, LANES), 1)
        lo = lane < HEAD_DIM
        uv = u_ref[...]
        qf = _mm(uv, wq_ref[...]) * ATT_SCALE
        kf = _mm(uv, wk_ref[...])
        vf = _mm(uv, wv_ref[...])
        cc = c_ref[...]
        ones_a = _aug(lane, AUG_A)
        ones_b = _aug(lane, AUG_B)
        sub8 = _iota((8, LANES), 0)
        nrm = jnp.zeros((8, LANES), F32)
        for h in range(N_HEADS):
            j, e = divmod(h, 2)

            def head(full):
                blk = full[:, LANES * j:LANES * (j + 1)]
                if e == 1:
                    blk = pltpu.roll(blk, HEAD_DIM, 1)
                return jnp.where(lo, blk, 0.0)

            parts = _split3(cc[:, N_HEADS + h:N_HEADS + h + 1])
            qh, kh = head(qf), head(kf)
            qa_ref[h] = (qh + _aug(lane, AUG_A, parts) + ones_b).astype(BF16)
            ka_ref[h] = (kh + ones_a - _aug(lane, AUG_B, parts)).astype(BF16)
            va_ref[h] = (head(vf) + ones_a).astype(BF16)
        seg = (_iota((ATT_WIDTH, LANES), 1) == (_iota((ATT_WIDTH, LANES), 0) >> 6)).astype(BF16)
        for r, val in enumerate((qf, kf)):
            sq = val * val
            hi = sq.astype(BF16)
            tot = _mm(hi, seg) + _mm((sq - hi.astype(F32)).astype(BF16), seg)
            nrm = nrm + jnp.where(sub8 == r, jnp.max(tot, axis=0, keepdims=True), 0.0)
        nrm_ref[0] = nrm

    shp = jax.ShapeDtypeStruct((N_HEADS, s, LANES), BF16)
    hspec = pl.BlockSpec((N_HEADS, tm, LANES), lambda i: (0, i, 0))
    wspec = _const_spec((D_MODEL, ATT_WIDTH))
    return pl.pallas_call(
        body, name="proj_qkv_heads",
        out_shape=tuple([shp, shp, shp, jax.ShapeDtypeStruct((nsteps, 8, LANES), F32)]
                        + [jax.ShapeDtypeStruct((N_CHIPS,) + a.shape, a.dtype) for a in later]),
        grid=(nsteps,),
        in_specs=[pl.BlockSpec((tm, D_MODEL), lambda i: (i, 0)), wspec, wspec, wspec,
                  pl.BlockSpec((tm, LANES), lambda i: (i, 0))] + [ANY] * n_later,
        out_specs=tuple([hspec, hspec, hspec, pl.BlockSpec((1, 8, LANES), lambda i: (i, 0, 0))]
                        + [ANY] * n_later),
        scratch_shapes=_sems(3 * n_later) + _sems(3 * n_later),
        compiler_params=_params(("arbitrary",)),
    )(u, w_q, w_k, w_v, cum, *later)


SKIP_BELOW = -110.0


def live_blocks(norms, cum, tq, tk):
    qn = jnp.sqrt(jnp.max(norms[:, 0, :N_HEADS], axis=0))
    kn = jnp.sqrt(jnp.max(norms[:, 1, :N_HEADS], axis=0))
    bound = 2.05 * qn * kn + 2.0
    c_first = cum[0::tq, N_HEADS:2 * N_HEADS]
    c_last = cum[tk - 1::tk, N_HEADS:2 * N_HEADS]
    nq, nk = c_first.shape[0], c_last.shape[0]
    top = bound[None, None, :] + c_first[:, None, :] - c_last[None, :, :]
    before = (jnp.arange(nk)[None, :] + 1) * tk <= jnp.arange(nq)[:, None] * tq
    dead = before[:, :, None] & ~(top >= SKIP_BELOW)
    first = jnp.sum(dead, axis=1).astype(jnp.int32).T
    last_q = jnp.sum(first[:, None, :] <= jnp.arange(nk)[None, :, None], axis=2).astype(jnp.int32) - 1
    return first, last_q


def attention_fwd(first, qa, ka, va):
    s = qa.shape[1]
    t = _blk(s, ATT_BLOCK)
    nq = s // t

    def body(first_ref, qa_ref, ka_ref, va_ref, o_ref, qb_ref, m_scr, acc_scr, alpha_scr, p_scr, s_scr):
        qi = pl.program_id(1)
        starts = [first_ref[2 * pl.program_id(0) + e, qi] for e in range(2)]
        k0 = jnp.maximum(starts[0], starts[1])
        m_scr[...] = jnp.full_like(m_scr, NEG_BIG)
        acc_scr[...] = jnp.zeros_like(acc_scr)

        def kv_rows(kb):
            return pl.ds(pl.multiple_of(kb * t, t), t)

        def logits(kb, masked, heads=(0, 1)):
            for e in heads:
                sc = _mm_nt(qa_ref[e], ka_ref[e, kv_rows(kb), :])
                if masked:
                    sc = jnp.where(_iota((t, t), 0) >= _iota((t, t), 1), sc, NEG_BIG)
                s_scr[e] = sc

        def probs(heads=(0, 1)):
            for e in heads:
                cmax = s_scr[e, :, 0:LANES]
                for c in range(1, t // LANES):
                    cmax = jnp.maximum(cmax, s_scr[e, :, LANES * c:LANES * (c + 1)])
                m_old = m_scr[e]
                m_new = jnp.maximum(m_old, jnp.max(cmax, axis=1, keepdims=True))
                alpha_scr[e] = jnp.exp(m_old - m_new)
                m_scr[e] = m_new
                for c in range(t // LANES):
                    cols = slice(LANES * c, LANES * (c + 1))
                    p_scr[e, :, cols] = jnp.exp(s_scr[e, :, cols] - m_new).astype(BF16)

        def accumulate(kb, heads=(0, 1)):
            for e in heads:
                acc_scr[e] = alpha_scr[e] * acc_scr[e] + _mm(p_scr[e], va_ref[e, kv_rows(kb), :])

        for e in range(2):
            def alone(kb, carry, e=e):
                logits(kb, False, (e,))
                probs((e,))
                accumulate(kb, (e,))
                return carry

            lax.fori_loop(starts[e], k0, alone, 0)

        def loop_body(kb, carry):
            logits(kb, False)
            for e in range(2):
                accumulate(kb - 1, (e,))
                probs((e,))
            return carry

        @pl.when(qi > k0)
        def _():
            logits(k0, False)
            probs()

        lax.fori_loop(k0 + 1, qi, loop_body, 0)

        @pl.when(qi > k0)
        def _():
            logits(qi, True)
            accumulate(qi - 1)
            probs()

        @pl.when(qi == k0)
        def _():
            logits(qi, True)
            probs()

        accumulate(qi)

        lane = _iota((t, LANES), 1)
        outs = []
        for e in range(2):
            acc = acc_scr[e]
            l = acc[:, AUG_A:AUG_A + 1]
            outs.append(acc / l)
            lse = m_scr[e][:, 0:1] + jnp.log(l)
            q32 = qa_ref[e].astype(F32)
            c = q32[:, AUG_A:AUG_A + 1] + q32[:, AUG_A + 1:AUG_A + 2] + q32[:, AUG_A + 2:AUG_A + 3]
            qb = jnp.where(lane < HEAD_DIM, q32, 0.0) + _aug(lane, AUG_A, _split3(c - lse)) + _aug(lane, AUG_B)
            qb_ref[e] = qb.astype(BF16)
        o_ref[...] = _pack_pair(outs[0], outs[1], lane)

    grid_spec = pltpu.PrefetchScalarGridSpec(
        num_scalar_prefetch=1, grid=(N_PAIRS, nq),
        in_specs=[pl.BlockSpec((2, t, LANES), lambda j, qi, f: (j, qi, 0)),
                  pl.BlockSpec((2, s, LANES), lambda j, qi, f: (j, 0, 0)),
                  pl.BlockSpec((2, s, LANES), lambda j, qi, f: (j, 0, 0))],
        out_specs=[pl.BlockSpec((t, LANES), lambda j, qi, f: (qi, j)),
                   pl.BlockSpec((2, t, LANES), lambda j, qi, f: (j, qi, 0))],
        scratch_shapes=[pltpu.VMEM((2, t, LANES), F32), pltpu.VMEM((2, t, LANES), F32),
                        pltpu.VMEM((2, t, LANES), F32), pltpu.VMEM((2, t, t), BF16), pltpu.VMEM((2, t, t), F32)])
    return pl.pallas_call(
        body, name="attention_fwd", grid_spec=grid_spec,
        out_shape=(jax.ShapeDtypeStruct((s, ATT_WIDTH), F32), jax.ShapeDtypeStruct((N_HEADS, s, LANES), BF16)),
        compiler_params=_params(("parallel", "parallel")),
    )(first, qa, ka, va)


def attention_bwd(last_q, qb, ka, va, dob):
    s = qb.shape[1]
    t = _blk(s, ATT_BLOCK_BWD)
    tq = _blk(s, ATT_BLOCK_BWD_Q)
    nq = s // tq
    per_q = tq // t

    def body(last_ref, qb_ref, dob_ref, ka_ref, va_ref, dq_ref, dk_ref, dv_ref, dc_ref, dq_scr, dk_scr, dv_scr):
        j, ki = pl.program_id(0), pl.program_id(1)

        @pl.when((j == 0) & (ki == 0))
        def _():
            dc_ref[...] = jnp.zeros_like(dc_ref)

        @pl.when(ki == 0)
        def _():
            dq_scr[...] = jnp.zeros_like(dq_scr)

        dk_scr[...] = jnp.zeros_like(dk_scr)
        dv_scr[...] = jnp.zeros_like(dv_scr)

        def q_step(qblk, masked, heads=(0, 1)):
            rows = pl.ds(pl.multiple_of(qblk * tq, tq), tq)
            scs = [_mm_nt(qb_ref[e, rows, :], ka_ref[e]) for e in heads]
            dps = [_mm_nt(dob_ref[e, rows, :], va_ref[e]) for e in heads]
            for e, sc, dp in zip(heads, scs, dps):
                q = qb_ref[e, rows, :]
                do = dob_ref[e, rows, :]
                if masked:
                    keep = (_iota((tq, t), 0) - _iota((tq, t), 1)) >= ki * t - qblk * tq
                    sc = jnp.where(keep, sc, NEG_BIG)
                p = jnp.exp(sc)
                ds_b = (p * dp).astype(BF16)
                dv_scr[e] += _mm_tn(p.astype(BF16), do)
                dk_scr[e] += _mm_tn(ds_b, q)
                dq_scr[e, rows, :] += _mm(ds_b, ka_ref[e])

        def loop_body(qblk, carry):
            q_step(qblk, False)
            return carry

        ends = [last_ref[2 * j + e, ki] + 1 for e in range(2)]
        both = jnp.minimum(ends[0], ends[1])
        diag = ki // per_q
        q_step(diag, True)
        lax.fori_loop(diag + 1, both, loop_body, 0)
        for e in range(2):
            def alone(qblk, carry, e=e):
                q_step(qblk, False, (e,))
                return carry

            lax.fori_loop(both, ends[e], alone, 0)

        lane = _iota((t, LANES), 1)
        dk_ref[...] = _pack_pair(dk_scr[0], dk_scr[1], lane).astype(BF16)
        dv_ref[...] = _pack_pair(dv_scr[0], dv_scr[1], lane).astype(BF16)
        rows = pl.ds(pl.multiple_of(ki * t, t), t)
        dc_ref[rows, :] -= (jnp.where(lane == N_HEADS + 2 * j, dk_scr[0][:, AUG_B:AUG_B + 1], 0.0)
                            + jnp.where(lane == N_HEADS + 2 * j + 1, dk_scr[1][:, AUG_B:AUG_B + 1], 0.0))

        @pl.when(ki == s // t - 1)
        def _():
            for blk in range(s // t):
                rws = pl.ds(blk * t, t)
                d0 = dq_scr[0, rws, :]
                d1 = dq_scr[1, rws, :]
                dq_ref[rws, :] = (_pack_pair(d0, d1, lane) * ATT_SCALE).astype(BF16)
                dc_ref[rws, :] += (jnp.where(lane == N_HEADS + 2 * j, d0[:, AUG_A:AUG_A + 1], 0.0)
                                   + jnp.where(lane == N_HEADS + 2 * j + 1, d1[:, AUG_A:AUG_A + 1], 0.0))

    full = pl.BlockSpec((2, s, LANES), lambda j, ki, f: (j, 0, 0))
    blk = pl.BlockSpec((2, t, LANES), lambda j, ki, f: (j, ki, 0))
    pair = pl.BlockSpec((t, LANES), lambda j, ki, f: (ki, j))
    wide = jax.ShapeDtypeStruct((s, ATT_WIDTH), BF16)
    grid_spec = pltpu.PrefetchScalarGridSpec(
        num_scalar_prefetch=1, grid=(N_PAIRS, s // t),
        in_specs=[full, full, blk, blk],
        out_specs=[pl.BlockSpec((s, LANES), lambda j, ki, f: (0, j)), pair, pair,
                   pl.BlockSpec((s, LANES), lambda j, ki, f: (0, 0))],
        scratch_shapes=[pltpu.VMEM((2, s, LANES), F32), pltpu.VMEM((2, t, LANES), F32),
                        pltpu.VMEM((2, t, LANES), F32)])
    return pl.pallas_call(
        body, name="attention_bwd", grid_spec=grid_spec,
        out_shape=(wide, wide, wide, jax.ShapeDtypeStruct((s, LANES), F32)),
        compiler_params=_params(("arbitrary", "arbitrary")),
    )(last_q, qb, dob, ka, va)


def _dsilu(z, sg):
    return sg * (1.0 + z * (1.0 - sg))


def post_mix(x, y, zs, o, za, p, tgt, ssd_g, att_g_lane, ple_g, fin_g, w_out, w_gate, w_proj):
    s = x.shape[0]
    tm = _blk(s, 256)
    half = SSD_WIDTH // N_GROUPS

    def rms_bwd(dy, yn, r):
        return r * (dy - yn * jnp.mean(dy * yn, axis=-1, keepdims=True))

    def colsum(a):
        return jnp.sum(a, axis=0, keepdims=True)

    def body(x_ref, y_ref, zs_ref, o_ref, za_ref, p_ref, t_ref, sg_ref, ag_ref, pg_ref, fg_ref,
             wo_ref, wg_ref, wp_ref,
             dh1_ref, dy_ref, dzs_ref, dob_ref, dza_ref, ycat_ref, dh1b_ref, n2b_ref, dglb_ref, dppb_ref, pb_ref,
             loss_ref, dfin_ref, dple_ref, dssd_ref, datt_ref):
        @pl.when(pl.program_id(0) == 0)
        def _():
            for r in (loss_ref, dfin_ref, dple_ref, dssd_ref, datt_ref):
                r[...] = jnp.zeros_like(r)

        lane = _iota((tm, LANES), 1)
        lo = lane < HEAD_DIM
        zs = zs_ref[...]
        sz = _sigmoid(zs)
        yv = y_ref[...]
        ys = yv * (zs * sz)
        yn, rg = [], []
        for g in range(N_GROUPS):
            seg = ys[:, half * g:half * (g + 1)]
            r = lax.rsqrt(jnp.mean(seg * seg, axis=-1, keepdims=True) + EPS)
            yn.append(seg * r)
            rg.append(r)
            ycat_ref[:, half * g:half * (g + 1)] = (yn[g] * sg_ref[:, half * g:half * (g + 1)]).astype(BF16)
        za = za_ref[...]
        sza = _sigmoid(za)
        silu_za = za * sza
        on, ra = [], []
        for jb in range(N_PAIRS):
            blk = o_ref[:, LANES * jb:LANES * (jb + 1)]
            sq = blk * blk
            ms0 = jnp.sum(jnp.where(lo, sq, 0.0), axis=1, keepdims=True) * (1.0 / HEAD_DIM)
            ms1 = jnp.sum(jnp.where(lo, 0.0, sq), axis=1, keepdims=True) * (1.0 / HEAD_DIM)
            r = jnp.where(lo, lax.rsqrt(ms0 + EPS), lax.rsqrt(ms1 + EPS))
            on.append(blk * r)
            ra.append(r)
            an = on[jb] * ag_ref[:, LANES * jb:LANES * (jb + 1)]
            ycat_ref[:, SSD_WIDTH + LANES * jb:SSD_WIDTH + LANES * (jb + 1)] = (
                an * silu_za[:, LANES * jb:LANES * (jb + 1)]).astype(BF16)
        h1 = x_ref[...] + _mm(ycat_ref[...], wo_ref[...])
        r2 = lax.rsqrt(jnp.mean(h1 * h1, axis=-1, keepdims=True) + EPS)
        n2h = h1 * r2
        n2_b = (n2h * pg_ref[...]).astype(BF16)
        gate = _sigmoid(_mm(n2_b, wg_ref[...]))
        p_b = p_ref[...].astype(BF16)
        pp = _mm(p_b, wp_ref[...])
        h2 = h1 + gate * pp
        r3 = lax.rsqrt(jnp.mean(h2 * h2, axis=-1, keepdims=True) + EPS)
        n3 = h2 * r3
        diff = n3 * fg_ref[...] - t_ref[...]
        sq = colsum(diff * diff)
        part = sq[:, 0:LANES]
        for jb in range(1, D_MODEL // LANES):
            part = part + sq[:, LANES * jb:LANES * (jb + 1)]
        loss_ref[...] += part * (0.5 / D_MODEL)
        dout = diff * (1.0 / D_MODEL)
        dfin_ref[...] += colsum(dout * n3)
        dh2 = rms_bwd(dout * fg_ref[...], n3, r3)
        dgl = dh2 * pp * gate * (1.0 - gate)
        dgl_b = dgl.astype(BF16)
        dn2 = _mm_nt(dgl_b, wg_ref[...])
        dple_ref[...] += colsum(dn2 * n2h)
        dh1 = dh2 + rms_bwd(dn2 * pg_ref[...], n2h, r2)
        dh1_b = dh1.astype(BF16)
        dycat = _mm_nt(dh1_b, wo_ref[...])
        dh1_ref[...] = dh1
        dh1b_ref[...] = dh1_b
        n2b_ref[...] = n2_b
        dglb_ref[...] = dgl_b
        dppb_ref[...] = (dh2 * gate).astype(BF16)
        pb_ref[...] = p_b
        for g in range(N_GROUPS):
            cols = slice(half * g, half * (g + 1))
            dys_g = dycat[:, cols]
            dssd_ref[:, cols] += colsum(dys_g * yn[g])
            dys = rms_bwd(dys_g * sg_ref[:, cols], yn[g], rg[g])
            dy_ref[:, cols] = dys * (zs[:, cols] * sz[:, cols])
            dzs_ref[:, cols] = (dys * yv[:, cols] * _dsilu(zs[:, cols], sz[:, cols])).astype(BF16)
        for jb in range(N_PAIRS):
            cols = slice(LANES * jb, LANES * (jb + 1))
            dya = dycat[:, SSD_WIDTH + LANES * jb:SSD_WIDTH + LANES * (jb + 1)]
            ag = ag_ref[:, cols]
            dan = dya * silu_za[:, cols]
            dza_ref[:, cols] = (dya * (on[jb] * ag) * _dsilu(za[:, cols], sza[:, cols])).astype(BF16)
            datt_ref[:, cols] += colsum(dan * on[jb])
            don = dan * ag
            q = don * on[jb]
            m0 = jnp.sum(jnp.where(lo, q, 0.0), axis=1, keepdims=True) * (1.0 / HEAD_DIM)
            m1 = jnp.sum(jnp.where(lo, 0.0, q), axis=1, keepdims=True) * (1.0 / HEAD_DIM)
            do2 = ra[jb] * (don - on[jb] * jnp.where(lo, m0, m1))
            prod = do2 * o_ref[:, cols]
            for e in range(2):
                delta = jnp.sum(jnp.where(lo, prod, 0.0) if e == 0 else jnp.where(lo, 0.0, prod),
                                axis=1, keepdims=True)
                base = jnp.where(lo, do2 if e == 0 else pltpu.roll(do2, HEAD_DIM, 1), 0.0)
                dob_ref[2 * jb + e] = (base - _aug(lane, AUG_A, _split3(delta))).astype(BF16)

    def rows(n, dtype=None):
        return pl.BlockSpec((tm, n), lambda i: (i, 0))

    def out(n, dtype):
        return jax.ShapeDtypeStruct((s, n), dtype)

    vec = _const_spec((1, D_MODEL))
    vshape = jax.ShapeDtypeStruct((1, D_MODEL), F32)
    return pl.pallas_call(
        body, name="post_mix",
        out_shape=(out(D_MODEL, F32), out(SSD_WIDTH, F32), out(SSD_WIDTH, BF16),
                   jax.ShapeDtypeStruct((N_HEADS, s, LANES), BF16),
                   out(ATT_WIDTH, BF16), out(D_INNER, BF16), out(D_MODEL, BF16), out(D_MODEL, BF16),
                   out(D_MODEL, BF16), out(D_MODEL, BF16), out(PLE_DIM, BF16),
                   jax.ShapeDtypeStruct((1, LANES), F32), vshape, vshape, vshape, vshape),
        grid=(s // tm,),
        in_specs=[rows(D_MODEL), rows(SSD_WIDTH), rows(SSD_WIDTH), rows(ATT_WIDTH), rows(ATT_WIDTH),
                  rows(PLE_DIM), rows(D_MODEL), vec, vec, vec, vec,
                  _const_spec((D_INNER, D_MODEL)), _const_spec((D_MODEL, D_MODEL)), _const_spec((PLE_DIM, D_MODEL))],
        out_specs=(rows(D_MODEL), rows(SSD_WIDTH), rows(SSD_WIDTH),
                   pl.BlockSpec((N_HEADS, tm, LANES), lambda i: (0, i, 0)), rows(ATT_WIDTH),
                   rows(D_INNER), rows(D_MODEL), rows(D_MODEL), rows(D_MODEL), rows(D_MODEL), rows(PLE_DIM),
                   _const_spec((1, LANES)), vec, vec, vec, vec),
        compiler_params=_params(("arbitrary",)),
    )(x, y, zs, o, za, p, tgt, ssd_g, att_g_lane, ple_g, fin_g, w_out, w_gate, w_proj)


def in_proj_bwd(dsegs, wsegs, x, g, dh1, pres):
    s = x.shape[0]
    tm = _blk(s, 512)
    nseg = len(dsegs)
    nbig = len(pres)
    nsteps = s // tm

    def body(*refs):
        d_refs = refs[:nseg]
        w_refs = refs[nseg:2 * nseg]
        x_ref, g_ref, dh1_ref = refs[2 * nseg:2 * nseg + 3]
        rest = refs[2 * nseg + 3:]
        pre_refs, (dx_ref, dg_ref), part_refs = rest[:nbig], rest[nbig:nbig + 2], rest[nbig + 2:2 * nbig + 2]
        ssem, rsem, lsem = rest[2 * nbig + 2:]

        @pl.when(pl.program_id(0) == 0)
        def _():
            dg_ref[...] = jnp.zeros_like(dg_ref)
            for cp in scatter_copies(pre_refs, part_refs, ssem, rsem, lsem):
                cp.start()

        @pl.when(pl.program_id(0) == nsteps - 1)
        def _():
            for cp in scatter_copies(pre_refs, part_refs, ssem, rsem, lsem):
                cp.wait()

        du = _mm_nt(d_refs[0][...], w_refs[0][...])
        for k in range(1, nseg):
            du = du + _mm_nt(d_refs[k][...], w_refs[k][...])
        xv = x_ref[...]
        r = lax.rsqrt(jnp.mean(xv * xv, axis=-1, keepdims=True) + EPS)
        xh = xv * r
        dg_ref[...] += jnp.sum(du * xh, axis=0, keepdims=True)
        dxh = du * g_ref[...]
        dx_ref[...] = r * (dxh - xh * jnp.mean(dxh * xh, axis=-1, keepdims=True)) + dh1_ref[...]

    rows = lambda n: pl.BlockSpec((tm, n), lambda i: (i, 0))
    return pl.pallas_call(
        body, name="in_proj_bwd",
        out_shape=tuple([jax.ShapeDtypeStruct((s, D_MODEL), F32), jax.ShapeDtypeStruct((1, D_MODEL), F32)]
                        + [jax.ShapeDtypeStruct(a.shape, a.dtype) for a in pres]),
        grid=(nsteps,),
        in_specs=([rows(d.shape[1]) for d in dsegs] + [_const_spec(w.shape) for w in wsegs]
                  + [rows(D_MODEL), _const_spec((1, D_MODEL)), rows(D_MODEL)] + [ANY] * nbig),
        out_specs=tuple([rows(D_MODEL), _const_spec((1, D_MODEL))] + [ANY] * nbig),
        scratch_shapes=_sems(3 * nbig) + [pltpu.SemaphoreType.DMA((nbig,))],
        compiler_params=_params(("arbitrary",)),
    )(*dsegs, *wsegs, x, g, dh1, *pres)


SMALL_NAMES = ("norm_g", "conv_b", "dt_bias", "a_log", "d_skip", "ssd_norm_g", "fg_bias", "att_norm_g",
               "ple_norm_g", "final_norm_g")
SMALL_SIZES = (1024, 1536, 16, 16, 16, 1024, 16, 64, 1024, 1024)
CONV_W_SIZE = CONV_WIDTH * CONV_CH


def _pack_small(vals):
    flat = jnp.concatenate([v.reshape(-1).astype(F32) for v in vals])
    flat = jnp.pad(flat, (0, SMALL_ROWS * LANES - flat.shape[0]))
    return flat.reshape(SMALL_ROWS, LANES)


def _unpack_small(pack, shapes):
    flat = pack.reshape(-1)
    out, off = [], 0
    for n, shp in zip(SMALL_SIZES, shapes):
        out.append(flat[off:off + n].reshape(shp))
        off += n
    return out


def _row128(v16, offset=0):
    return jnp.pad(v16.reshape(1, N_HEADS).astype(F32), ((0, 0), (offset, LANES - N_HEADS - offset)))


def local_step(prereduce, later, join_later, x, p, tgt, w_in, conv_w, norm_g, conv_b, dt_bias, a_log, d_skip,
               ssd_norm_g, fg_bias, att_norm_g, ple_norm_g, final_norm_g):
    widths = (SSD_WIDTH, CONV_CH, N_HEADS, ATT_WIDTH, ATT_WIDTH, ATT_WIDTH, ATT_WIDTH)
    c0, c1, c2, c3, c4, c5, c6, c7 = [sum(widths[:i]) for i in range(len(widths) + 1)]
    w_zs, w_xbc, w_dt = w_in[:, c0:c1], w_in[:, c1:c2], w_in[:, c2:c3]
    w_za, w_q, w_k, w_v, w_f = w_in[:, c3:c4], w_in[:, c4:c5], w_in[:, c5:c6], w_in[:, c6:c7], w_in[:, c7:]
    w_small = jnp.concatenate([w_dt, w_f, jnp.zeros((D_MODEL, LANES - 2 * N_HEADS), BF16)], axis=1)

    dtb_row = _row128(dt_bias)
    a_row = _row128(-jnp.exp(a_log.astype(F32)))
    fgb_row = _row128(fg_bias, N_HEADS)
    dskip_lane = jnp.repeat(d_skip.astype(F32), HEAD_DIM).reshape(1, SSD_WIDTH)
    att_g_lane = jnp.tile(att_norm_g.astype(F32), N_HEADS).reshape(1, ATT_WIDTH)
    row = lambda v: v.reshape(1, -1).astype(F32)

    u, zs, xbc, za, small = in_proj_fwd(x, row(norm_g), [w_zs, w_xbc, w_za, w_small])
    cum = forget_cumsum(small, fgb_row)
    qa, ka, va, norms, *gathered = proj_qkv_heads(u, w_q, w_k, w_v, cum, later)
    w_out, w_gate, w_proj = join_later(gathered)
    n_seq = x.shape[0]
    first, _ = live_blocks(norms, cum, _blk(n_seq, ATT_BLOCK), _blk(n_seq, ATT_BLOCK))
    _, last_q = live_blocks(norms, cum, _blk(n_seq, ATT_BLOCK_BWD_Q), _blk(n_seq, ATT_BLOCK_BWD))
    pre, xc = conv_fwd(xbc, conv_w, row(conv_b))
    y, states = ssd_fwd(xc, small, dtb_row, a_row, dskip_lane)
    o, qb = attention_fwd(first, qa, ka, va)
    (dh1, dy, dzs, dob, dza, ycat, dh1_b, n2_b, dgl_b, dpp_b, p_b,
     loss_l, dfin, dple, dssd_g, datt_lane) = post_mix(
        x, y, zs, o, za, p, tgt, row(ssd_norm_g), att_g_lane, row(ple_norm_g), row(final_norm_g),
        w_out, w_gate, w_proj)
    dq, dk, dv, dc = attention_bwd(last_q, qb, ka, va, dob)
    dxc, ddt_raw, da, ddtb, ddsk_lane = ssd_bwd(xc, small, states, dy, dtb_row, a_row, dskip_lane)
    dsmall, dfgb = forget_bwd(dc, small, ddt_raw, fgb_row)
    dxbc, dconv_w8, dconv_b = conv_bwd(xbc, pre, dxc, conv_w)
    dsegs = [dzs, dxbc, dza, dq, dk, dv, dsmall]
    wsegs = [w_zs, w_xbc, w_za, w_q, w_k, w_v, w_small]
    dws = [matmul_tn(u, d, "dw_in_%d" % i) for i, d in enumerate(dsegs)]
    dw_in = jnp.concatenate([dws[0], dws[1], dws[6][:, :N_HEADS], dws[2], dws[3], dws[4], dws[5],
                             dws[6][:, N_HEADS:2 * N_HEADS]], axis=1)
    dw_out = matmul_tn(ycat, dh1_b, "dw_out")
    dw_gate = matmul_tn(n2_b, dgl_b, "dw_gate")
    dw_proj = matmul_tn(p_b, dpp_b, "dw_proj")
    dx, dnorm_g, *parts = in_proj_bwd(dsegs, wsegs, x, row(norm_g), dh1, prereduce(dw_in, dw_out, dw_gate, dw_proj))
    small_grads = [
        dnorm_g, dconv_b, ddtb[0, :N_HEADS], (da * a_row)[0, :N_HEADS],
        ddsk_lane.reshape(N_HEADS, HEAD_DIM).sum(axis=1), dssd_g, dfgb[0, N_HEADS:2 * N_HEADS],
        datt_lane.reshape(N_HEADS, HEAD_DIM).sum(axis=0), dple, dfin]
    loss = jnp.sum(loss_l)
    return loss, dx, parts, dconv_w8[:CONV_WIDTH], small_grads


def kernel(x, p, norm_g, w_in, conv_w, conv_b, dt_bias, a_log, d_skip, ssd_norm_g, fg_bias, att_norm_g, w_out, ple_norm_g, w_ple_gate, w_ple_proj, final_norm_g, loss_target, m_norm_g, m_w_in, m_conv_w, m_conv_b, m_dt_bias, m_a_log, m_d_skip, m_ssd_norm_g, m_fg_bias, m_att_norm_g, m_w_out, m_ple_norm_g, m_w_ple_gate, m_w_ple_proj, m_final_norm_g, v_norm_g, v_w_in, v_conv_w, v_conv_b, v_dt_bias, v_a_log, v_d_skip, v_ssd_norm_g, v_fg_bias, v_att_norm_g, v_w_out, v_ple_norm_g, v_w_ple_gate, v_w_ple_proj, v_final_norm_g):
    chip = 2 * lax.axis_index("x") + lax.axis_index("y")
    core = lax.axis_index("c")

    big_w = [w_in[0], w_out[0], w_ple_gate[0], w_ple_proj[0]]
    own = [a.astype(BF16) for a in big_w] + [conv_w[0]]

    def joined(mine, gathered, axis):
        return jnp.concatenate([jnp.where(chip == j, mine, gathered[j]) for j in range(N_CHIPS)], axis=axis)

    w_in_all, conv_all = gather_weights(own[:1], own[4])
    w_in_f, conv_w_f = joined(own[0], w_in_all, 1), joined(own[4], conv_all, 1)

    def join_later(gathered):
        return [joined(mine, got, axis) for mine, got, axis in zip(own[1:4], gathered, (0, 0, 1))]

    core1 = core.reshape(1).astype(jnp.int32)

    def prereduce(dw_in, dw_out, dw_gate, dw_proj):
        n_in, n_proj = w_in.shape[2], w_ple_proj.shape[2]
        gs = [jnp.stack([dw_in[:, n_in * j:n_in * (j + 1)] for j in range(N_CHIPS)]),
              dw_out.reshape(N_CHIPS, w_out.shape[1], D_MODEL), dw_gate.reshape(N_CHIPS, w_ple_gate.shape[1], D_MODEL),
              jnp.stack([dw_proj[:, n_proj * j:n_proj * (j + 1)] for j in range(N_CHIPS)])]
        return add_halves(core1, gs, halves_to_sibling(gs))

    smalls_w = [norm_g, conv_b, dt_bias, a_log, d_skip, ssd_norm_g, fg_bias, att_norm_g, ple_norm_g, final_norm_g]
    loss_l, dx, parts, dconv_w, small_grads = local_step(
        prereduce, own[1:4], join_later, x[0], p[0, 0], loss_target[0], w_in_f, conv_w_f,
        *[a.reshape(-1) for a in smalls_w])
    loss = lax.psum(loss_l, ("x", "y", "c"))
    smalls = gather_small(_pack_small(list(small_grads) + [dconv_w]))
    mine = sum_parts(parts)

    g_big, d_big, m_big, v_big = adamw_big(
        core1, mine, swap_halves(mine), big_w, [m_w_in[0], m_w_out[0], m_w_ple_gate[0], m_w_ple_proj[0]],
        [v_w_in[0], v_w_out[0], v_w_ple_gate[0], v_w_ple_proj[0]])
    smalls_m = [m_norm_g, m_conv_b, m_dt_bias, m_a_log, m_d_skip, m_ssd_norm_g, m_fg_bias, m_att_norm_g,
                m_ple_norm_g, m_final_norm_g]
    smalls_v = [v_norm_g, v_conv_b, v_dt_bias, v_a_log, v_d_skip, v_ssd_norm_g, v_fg_bias, v_att_norm_g,
                v_ple_norm_g, v_final_norm_g]
    g_sm, d_sm, m_sm, v_sm = adamw_small(smalls, _pack_small(smalls_w), _pack_small(smalls_m), _pack_small(smalls_v))
    n_small = sum(SMALL_SIZES)
    g_conv_full = g_sm.reshape(-1)[n_small:n_small + CONV_W_SIZE].reshape(CONV_WIDTH, CONV_CH)
    n_conv = conv_w.shape[2]
    g_conv = lax.dynamic_slice_in_dim(g_conv_full, chip * n_conv, n_conv, axis=1)
    d_conv, m_conv, v_conv = adamw_whole(g_conv, conv_w[0], m_conv_w[0], v_conv_w[0], "adamw_conv")

    shapes = [a.shape for a in smalls_w]
    outs = []
    for big, conv, sm in ((g_big, g_conv, g_sm), (d_big, d_conv, d_sm), (m_big, m_conv, m_sm), (v_big, v_conv, v_sm)):
        b_in, b_out, b_gate, b_proj = [a[None] for a in big]
        s_norm, s_convb, s_dtb, s_alog, s_dsk, s_ssdg, s_fgb, s_attg, s_pleg, s_fin = _unpack_small(sm, shapes)
        outs.extend([s_norm, b_in, conv[None], s_convb, s_dtb, s_alog, s_dsk, s_ssdg, s_fgb, s_attg, b_out, s_pleg,
                     b_gate, b_proj, s_fin])
    return (loss, dx[None], *outs)
```

```python
import functools

import jax
import jax.numpy as jnp
from jax import lax
from jax.experimental import pallas as pl
from jax.experimental.pallas import tpu as pltpu

F32 = jnp.float32
BF16 = jnp.bfloat16

D_MODEL = 1024
SSD_WIDTH = 1024
ATT_WIDTH = 1024
N_HEADS = 16
HEAD_DIM = 64
N_GROUPS = 2
D_STATE = 128
CONV_CH = 1536
CONV_WIDTH = 4
CHUNK = 128
PLE_DIM = 256
D_INNER = 2048
EPS = 1e-6
IN_COLS = 6688
N_CHIPS = 4
N_DEV = 8
LANES = 128
N_PAIRS = 8

ADAM_LR = 0.001
ADAM_B1 = 0.9
ADAM_B2 = 0.999
ADAM_EPS = 1e-08
ADAM_WD = 0.01
ADAM_STEP = 10

SMALL_ROWS = 96

NEG_BIG = -1e30
VMEM_LIMIT = 56 * 1024 * 1024

MESH = pl.DeviceIdType.MESH
ANY = pl.BlockSpec(memory_space=pl.ANY)


def _mm(a, b):
    return jnp.dot(a, b, preferred_element_type=F32)


def _mm_nt(a, b):
    return lax.dot_general(a, b, (((1,), (1,)), ((), ())), preferred_element_type=F32)


def _mm_tn(a, b):
    return lax.dot_general(a, b, (((0,), (0,)), ((), ())), preferred_element_type=F32)


def _mm_exact(a, b):
    return jnp.dot(a, b, preferred_element_type=F32, precision=lax.Precision.HIGHEST)


def _softplus(x):
    return jnp.maximum(x, 0.0) + jnp.log1p(jnp.exp(-jnp.abs(x)))


def _sigmoid(x):
    return jax.nn.sigmoid(x)


def _iota(shape, dim):
    return lax.broadcasted_iota(jnp.int32, shape, dim)


def _params(sem=None):
    return pltpu.CompilerParams(dimension_semantics=sem, vmem_limit_bytes=VMEM_LIMIT)


def _blk(n, pref):
    return min(n, pref)


def _const_spec(shape):
    nd = len(shape)
    return pl.BlockSpec(shape, lambda *_: (0,) * nd)


def _chip_peers():
    x, y, c = lax.axis_index("x"), lax.axis_index("y"), lax.axis_index("c")
    return x, y, c, [(1 - x, y, c), (x, 1 - y, c), (1 - x, 1 - y, c)]


def _half(rows, c):
    h = rows // 2
    return pl.ds(pl.multiple_of(c * h, 8), h)


def _sems(n):
    return [pltpu.SemaphoreType.DMA((n,)), pltpu.SemaphoreType.DMA((n,))]


def gather_copies(ins, outs, ssem1, rsem1, ssem2, rsem2):
    n = len(ins)
    x, y, c, peers = _chip_peers()
    me = 2 * x + y
    fetched, passed = [], []
    for k, peer in enumerate(peers):
        chip = 2 * peer[0] + peer[1]
        for i in range(n):
            h = _half(ins[i].shape[0], c)
            fetched.append(pltpu.make_async_remote_copy(
                src_ref=ins[i].at[h], dst_ref=outs[i].at[me, h], send_sem=ssem1.at[n * k + i],
                recv_sem=rsem1.at[n * k + i], device_id=peer, device_id_type=MESH))
            passed.append(pltpu.make_async_remote_copy(
                src_ref=outs[i].at[chip, h], dst_ref=outs[i].at[chip, h], send_sem=ssem2.at[n * k + i],
                recv_sem=rsem2.at[n * k + i], device_id=(x, y, 1 - c), device_id_type=MESH))
    return fetched, passed


def gather_weights(shards, conv_s):
    n = len(shards)

    def body(*refs):
        ins, conv_in = refs[:n], refs[n]
        outs, conv_out = refs[n + 1:2 * n + 1], refs[2 * n + 1]
        ssem1, rsem1, ssem2, rsem2, c_ssem, c_rsem = refs[2 * n + 2:]
        x, y, _, peers = _chip_peers()
        fetched, passed = gather_copies(ins, outs, ssem1, rsem1, ssem2, rsem2)
        small = [pltpu.make_async_remote_copy(
            src_ref=conv_in, dst_ref=conv_out.at[2 * x + y], send_sem=c_ssem.at[k], recv_sem=c_rsem.at[k],
            device_id=peer, device_id_type=MESH) for k, peer in enumerate(peers)]
        for cp in fetched + small:
            cp.start()
        for landed, onward in zip(fetched, passed):
            landed.wait_recv()
            onward.start()
        for cp in passed:
            cp.wait_recv()
        for cp in fetched + passed:
            cp.wait_send()
        for cp in small:
            cp.wait()

    return pl.pallas_call(
        body, name="gather_weights",
        out_shape=tuple(jax.ShapeDtypeStruct((N_CHIPS,) + a.shape, a.dtype) for a in list(shards) + [conv_s]),
        in_specs=[ANY] * (n + 1), out_specs=(ANY,) * (n + 1),
        scratch_shapes=_sems(3 * n) + _sems(3 * n) + _sems(3),
    )(*shards, conv_s)


def halves_to_sibling(gs):
    n = len(gs)

    def body(*refs):
        ins, outs = refs[:n], refs[n:2 * n]
        ssem, rsem = refs[2 * n:]
        x, y, c = lax.axis_index("x"), lax.axis_index("y"), lax.axis_index("c")
        copies = []
        for i in range(n):
            for j in range(N_CHIPS):
                copies.append(pltpu.make_async_remote_copy(
                    src_ref=ins[i].at[j, _half(ins[i].shape[1], 1 - c)], dst_ref=outs[i].at[j],
                    send_sem=ssem.at[N_CHIPS * i + j], recv_sem=rsem.at[N_CHIPS * i + j],
                    device_id=(x, y, 1 - c), device_id_type=MESH))
        for cp in copies:
            cp.start()
        for cp in copies:
            cp.wait()

    return pl.pallas_call(
        body, name="halves_to_sibling",
        out_shape=tuple(jax.ShapeDtypeStruct((N_CHIPS, g.shape[1] // 2, g.shape[2]), F32) for g in gs),
        in_specs=[ANY] * n, out_specs=(ANY,) * n, scratch_shapes=_sems(N_CHIPS * n),
    )(*gs)


RED_GRID = 8


def add_halves(core, gs, rbs):
    n = len(gs)

    def body(c_ref, *refs):
        for i in range(n):
            refs[2 * n + i][...] = (refs[i][...] + refs[n + i][...]).astype(BF16)

    def blk(g):
        return (1, g.shape[1] // 2 // RED_GRID, g.shape[2])

    grid_spec = pltpu.PrefetchScalarGridSpec(
        num_scalar_prefetch=1, grid=(N_CHIPS, RED_GRID),
        in_specs=([pl.BlockSpec(blk(g), lambda j, b, c_ref: (j, c_ref[0] * RED_GRID + b, 0)) for g in gs]
                  + [pl.BlockSpec(blk(g), lambda j, b, c_ref: (j, b, 0)) for g in gs]),
        out_specs=[pl.BlockSpec(blk(g), lambda j, b, c_ref: (j, b, 0)) for g in gs])
    return pl.pallas_call(
        body, name="add_halves", grid_spec=grid_spec,
        out_shape=tuple(jax.ShapeDtypeStruct(r.shape, BF16) for r in rbs),
        compiler_params=_params(("parallel", "parallel")),
    )(core, *gs, *rbs)


def scatter_copies(ins, outs, ssem, rsem, lsem):
    n = len(ins)
    x, y, _, peers = _chip_peers()
    me = 2 * x + y
    copies = [pltpu.make_async_copy(ins[i].at[me], outs[i].at[me], lsem.at[i]) for i in range(n)]
    for k, peer in enumerate(peers):
        dst_chip = 2 * peer[0] + peer[1]
        for i in range(n):
            copies.append(pltpu.make_async_remote_copy(
                src_ref=ins[i].at[dst_chip], dst_ref=outs[i].at[me], send_sem=ssem.at[n * k + i],
                recv_sem=rsem.at[n * k + i], device_id=peer, device_id_type=MESH))
    return copies


def sum_parts(parts):
    n = len(parts)

    def body(*refs):
        for i in range(n):
            p_ref = refs[i]
            refs[n + i][...] = ((p_ref[0].astype(F32) + p_ref[1].astype(F32)) + p_ref[2].astype(F32)
                                ) + p_ref[3].astype(F32)

    def rows(p):
        return p.shape[1] // RED_GRID

    return pl.pallas_call(
        body, name="sum_parts",
        out_shape=tuple(jax.ShapeDtypeStruct(p.shape[1:], F32) for p in parts),
        grid=(RED_GRID,),
        in_specs=[pl.BlockSpec((N_CHIPS, rows(p), p.shape[2]), lambda b: (0, b, 0)) for p in parts],
        out_specs=tuple(pl.BlockSpec((rows(p), p.shape[2]), lambda b: (b, 0)) for p in parts),
        compiler_params=_params(("parallel",)),
    )(*parts)


def swap_halves(reds, small):
    n = len(reds)

    def body(*refs):
        ins, s_ref = refs[:n], refs[n]
        outs, smalls_ref = refs[n + 1:2 * n + 1], refs[2 * n + 1]
        ssem, rsem, s_ssem, s_rsem, lsem = refs[2 * n + 2:]
        x, y, c = lax.axis_index("x"), lax.axis_index("y"), lax.axis_index("c")
        dev = 4 * x + 2 * y + c
        copies = [pltpu.make_async_remote_copy(
            src_ref=ins[i], dst_ref=outs[i], send_sem=ssem.at[i], recv_sem=rsem.at[i],
            device_id=(x, y, 1 - c), device_id_type=MESH) for i in range(n)]
        copies.append(pltpu.make_async_copy(s_ref, smalls_ref.at[dev], lsem))
        for k in range(1, N_DEV):
            fx, fy, fc = (k >> 2) & 1, (k >> 1) & 1, k & 1
            peer = ((1 - x) if fx else x, (1 - y) if fy else y, (1 - c) if fc else c)
            copies.append(pltpu.make_async_remote_copy(
                src_ref=s_ref, dst_ref=smalls_ref.at[dev], send_sem=s_ssem.at[k - 1], recv_sem=s_rsem.at[k - 1],
                device_id=peer, device_id_type=MESH))
        for cp in copies:
            cp.start()
        for cp in copies:
            cp.wait()

    return pl.pallas_call(
        body, name="swap_halves",
        out_shape=tuple([jax.ShapeDtypeStruct(r.shape, F32) for r in reds]
                        + [jax.ShapeDtypeStruct((N_DEV,) + small.shape, F32)]),
        in_specs=[ANY] * (n + 1), out_specs=(ANY,) * (n + 1),
        scratch_shapes=_sems(n) + _sems(N_DEV - 1) + [pltpu.SemaphoreType.DMA],
    )(*reds, small)


def _adamw(w, g, m, v):
    m = ADAM_B1 * m + (1.0 - ADAM_B1) * g
    v = ADAM_B2 * v + (1.0 - ADAM_B2) * (g * g)
    m_hat = m / (1.0 - ADAM_B1 ** ADAM_STEP)
    v_hat = v / (1.0 - ADAM_B2 ** ADAM_STEP)
    delta = -ADAM_LR * (m_hat / (jnp.sqrt(v_hat) + ADAM_EPS) + ADAM_WD * w)
    return delta, m, v


def adamw_big(core, mine, theirs, ws, ms, vs):
    n = len(ws)
    per_half = RED_GRID // 2

    def body(c_ref, *refs):
        own = (pl.program_id(0) // per_half) == c_ref[0]
        for i in range(n):
            g = jnp.where(own, refs[i][...], refs[n + i][...])
            d, mn, vn = _adamw(refs[2 * n + i][...], g, refs[3 * n + i][...], refs[4 * n + i][...])
            refs[5 * n + i][...] = g
            refs[6 * n + i][...] = d
            refs[7 * n + i][...] = mn
            refs[8 * n + i][...] = vn

    def blk(w):
        return (w.shape[0] // RED_GRID, w.shape[1])

    halves = [pl.BlockSpec(blk(w), lambda b, c_ref: (b % per_half, 0)) for w in ws]
    whole = [pl.BlockSpec(blk(w), lambda b, c_ref: (b, 0)) for w in ws]
    shapes = [jax.ShapeDtypeStruct(w.shape, F32) for w in ws]
    grid_spec = pltpu.PrefetchScalarGridSpec(
        num_scalar_prefetch=1, grid=(RED_GRID,), in_specs=halves * 2 + whole * 3, out_specs=whole * 4)
    outs = pl.pallas_call(
        body, name="adamw_big", out_shape=tuple(shapes * 4), grid_spec=grid_spec,
        compiler_params=_params(("parallel",)),
    )(core, *mine, *theirs, *ws, *ms, *vs)
    return outs[:n], outs[n:2 * n], outs[2 * n:3 * n], outs[3 * n:]


def adamw_whole(g, w, m, v, name):
    def body(g_ref, w_ref, m_ref, v_ref, d_out, m_out, v_out):
        d, mn, vn = _adamw(w_ref[...], g_ref[...], m_ref[...], v_ref[...])
        d_out[...] = d
        m_out[...] = mn
        v_out[...] = vn

    shp = jax.ShapeDtypeStruct(g.shape, F32)
    return pl.pallas_call(body, name=name, out_shape=(shp,) * 3)(g, w, m, v)


def adamw_small(smalls, w, m, v):
    def body(s_ref, w_ref, m_ref, v_ref, g_out, d_out, m_out, v_out):
        g = s_ref[0]
        for k in range(1, N_DEV):
            g = g + s_ref[k]
        d, mn, vn = _adamw(w_ref[...], g, m_ref[...], v_ref[...])
        g_out[...] = g
        d_out[...] = d
        m_out[...] = mn
        v_out[...] = vn

    shp = jax.ShapeDtypeStruct((SMALL_ROWS, LANES), F32)
    return pl.pallas_call(body, name="adamw_small", out_shape=(shp,) * 4)(smalls, w, m, v)


def in_proj_fwd(x, g, ws):
    s = x.shape[0]
    tm = _blk(s, 512)
    n = len(ws)

    def body(x_ref, g_ref, *refs):
        xv = x_ref[...]
        r = lax.rsqrt(jnp.mean(xv * xv, axis=-1, keepdims=True) + EPS)
        u = (xv * r * g_ref[...]).astype(BF16)
        refs[n][...] = u
        for i in range(n):
            refs[n + 1 + i][...] = _mm(u, refs[i][...])

    rows = lambda width: pl.BlockSpec((tm, width), lambda i: (i, 0))
    return pl.pallas_call(
        body, name="in_proj_fwd",
        out_shape=tuple([jax.ShapeDtypeStruct((s, D_MODEL), BF16)]
                        + [jax.ShapeDtypeStruct((s, w.shape[1]), F32) for w in ws]),
        grid=(s // tm,),
        in_specs=[rows(D_MODEL), _const_spec((1, D_MODEL))] + [_const_spec(w.shape) for w in ws],
        out_specs=tuple([rows(D_MODEL)] + [rows(w.shape[1]) for w in ws]),
        compiler_params=_params(("parallel",)),
    )(x, g, *ws)


def matmul_tn(a, b, name):
    s, m = a.shape
    n = b.shape[1]
    tk = _blk(s, 2048)
    tn = _blk(n, 512) if m > D_MODEL else (n // 2 if n > D_MODEL else n)

    def body(a_ref, b_ref, o_ref):
        @pl.when(pl.program_id(1) == 0)
        def _():
            o_ref[...] = jnp.zeros_like(o_ref)

        o_ref[...] += _mm_tn(a_ref[...], b_ref[...])

    return pl.pallas_call(
        body, name=name, out_shape=jax.ShapeDtypeStruct((m, n), F32), grid=(n // tn, s // tk),
        in_specs=[pl.BlockSpec((tk, m), lambda j, i: (i, 0)), pl.BlockSpec((tk, tn), lambda j, i: (i, j))],
        out_specs=pl.BlockSpec((m, tn), lambda j, i: (0, j)),
        compiler_params=_params(("parallel", "arbitrary")),
    )(a, b)


def conv_fwd(xbc, w, b):
    s = xbc.shape[0]
    tm = _blk(s, 256)

    def body(x_ref, t_ref, w_ref, b_ref, pre_ref, act_ref):
        i = pl.program_id(0)
        row8 = _iota((8, LANES), 0)
        for c0 in range(0, CONV_CH, LANES):
            cols = slice(c0, c0 + LANES)
            cur = x_ref[:, cols]
            tail = jnp.where(i > 0, t_ref[:, cols], 0.0)
            wv = w_ref[:, cols]
            bias = b_ref[:, cols]
            acc = cur * wv[3:4, :] + bias
            head = cur[0:8, :] * wv[3:4, :] + bias
            for sh in range(1, CONV_WIDTH):
                wk = wv[3 - sh:4 - sh, :]
                acc = acc + pltpu.roll(cur, sh, 0) * wk
                first = jnp.where(row8 < sh, pltpu.roll(tail, sh, 0), pltpu.roll(cur[0:8, :], sh, 0))
                head = head + first * wk
            pre_ref[:, cols] = acc
            act_ref[:, cols] = acc * _sigmoid(acc)
            pre_ref[0:8, cols] = head
            act_ref[0:8, cols] = head * _sigmoid(head)

    shp = jax.ShapeDtypeStruct(xbc.shape, F32)
    rows = pl.BlockSpec((tm, CONV_CH), lambda i: (i, 0))
    return pl.pallas_call(
        body, name="conv_fwd", out_shape=(shp, shp), grid=(s // tm,),
        in_specs=[rows, pl.BlockSpec((8, CONV_CH), lambda i: (jnp.maximum(i * (tm // 8) - 1, 0), 0)),
                  _const_spec((CONV_WIDTH, CONV_CH)), _const_spec((1, CONV_CH))],
        out_specs=(rows, rows), compiler_params=_params(("parallel",)),
    )(xbc, xbc, w, b)


def conv_bwd(xbc, pre, dact, w):
    s = xbc.shape[0]
    tm = _blk(s, 256)
    nb = s // tm

    def dsilu(p):
        sg = _sigmoid(p)
        return sg * (1.0 + p * (1.0 - sg))

    def body(x_ref, xt_ref, p_ref, pn_ref, d_ref, dn_ref, w_ref, dx_ref, dw_ref, db_ref):
        i = pl.program_id(0)

        @pl.when(i == 0)
        def _():
            dw_ref[...] = jnp.zeros_like(dw_ref)
            db_ref[...] = jnp.zeros_like(db_ref)

        row8 = _iota((8, LANES), 0)
        for c0 in range(0, CONV_CH, LANES):
            cols = slice(c0, c0 + LANES)
            wv = w_ref[:, cols]
            dpre = d_ref[:, cols] * dsilu(p_ref[:, cols])
            dnext = jnp.where(i < nb - 1, dn_ref[:, cols] * dsilu(pn_ref[:, cols]), 0.0)
            cur = x_ref[:, cols]
            tail = jnp.where(i > 0, xt_ref[:, cols], 0.0)
            dx = dpre * wv[3:4, :]
            last = dpre[tm - 8:tm, :] * wv[3:4, :]
            db_ref[:, cols] += jnp.sum(dpre, axis=0, keepdims=True)
            dws = [jnp.sum(dpre * cur, axis=0, keepdims=True)]
            for sh in range(1, CONV_WIDTH):
                wk = wv[3 - sh:4 - sh, :]
                dx = dx + pltpu.roll(dpre, tm - sh, 0) * wk
                nxt = jnp.where(row8 >= 8 - sh, pltpu.roll(dnext, 8 - sh, 0),
                                pltpu.roll(dpre[tm - 8:tm, :], 8 - sh, 0))
                last = last + nxt * wk
                xs = pltpu.roll(cur, sh, 0)
                first = jnp.where(row8 < sh, pltpu.roll(tail, sh, 0), xs[0:8, :])
                dws.append(jnp.sum(dpre * xs, axis=0, keepdims=True)
                           + jnp.sum(dpre[0:8, :] * (first - xs[0:8, :]), axis=0, keepdims=True))
            dx_ref[:, cols] = dx.astype(BF16)
            dx_ref[tm - 8:tm, cols] = last.astype(BF16)
            for sh in range(CONV_WIDTH):
                dw_ref[3 - sh:4 - sh, cols] += dws[sh]

    rows = pl.BlockSpec((tm, CONV_CH), lambda i: (i, 0))
    prev8 = pl.BlockSpec((8, CONV_CH), lambda i: (jnp.maximum(i * (tm // 8) - 1, 0), 0))
    next8 = pl.BlockSpec((8, CONV_CH), lambda i: (jnp.minimum((i + 1) * (tm // 8), s // 8 - 1), 0))
    return pl.pallas_call(
        body, name="conv_bwd",
        out_shape=(jax.ShapeDtypeStruct(xbc.shape, BF16), jax.ShapeDtypeStruct((8, CONV_CH), F32),
                   jax.ShapeDtypeStruct((1, CONV_CH), F32)),
        grid=(nb,),
        in_specs=[rows, prev8, rows, next8, rows, next8, _const_spec((CONV_WIDTH, CONV_CH))],
        out_specs=(rows, _const_spec((8, CONV_CH)), _const_spec((1, CONV_CH))),
        compiler_params=_params(("arbitrary",)),
    )(xbc, xbc, pre, pre, dact, dact, w)


def _pair_lanes(mat, j, lane):
    return jnp.where(lane < HEAD_DIM, mat[:, 2 * j:2 * j + 1], mat[:, 2 * j + 1:2 * j + 2])


def _ssd_chunk_prelude(sm, dtb, a_row, lane, sub):
    raw = sm + dtb
    head_lane = lane < N_HEADS
    dt = jnp.where(head_lane, _softplus(raw), 0.0)
    sig = jnp.where(head_lane, _sigmoid(raw), 0.0)
    tri = (lane <= sub).astype(F32)
    acs = _mm_exact(tri, dt * a_row)
    return dt, sig, acs, acs.T


GROUP_WIDTH = SSD_WIDTH // N_GROUPS
HEADS_PER_GROUP = N_HEADS // N_GROUPS


def _expand_group(mat, g, lane):
    return jnp.concatenate([_pair_lanes(mat, j, lane) for j in range(4 * g, 4 * g + 4)], axis=1)


def _head_sums(q, g):
    row = _iota((GROUP_WIDTH, LANES), 0)
    seg = (_iota((GROUP_WIDTH, LANES), 1) == HEADS_PER_GROUP * g + (row >> 6)).astype(BF16)
    hi = q.astype(BF16)
    lo = (q - hi.astype(F32)).astype(BF16)
    return _mm(hi, seg) + _mm(lo, seg)


def _rows_from_lanes(row512):
    return jnp.broadcast_to(row512, (LANES, GROUP_WIDTH)).T


def ssd_fwd(xc, small, dtb_row, a_row, dskip_lane):
    s = xc.shape[0]
    nc = s // CHUNK

    def body(xc_ref, sm_ref, dtb_ref, a_ref, dsk_ref, y_ref, hs_ref, h_scr):
        c = pl.program_id(0)

        @pl.when(c == 0)
        def _():
            h_scr[...] = jnp.zeros_like(h_scr)

        lane = _iota((CHUNK, LANES), 1)
        sub = _iota((CHUNK, LANES), 0)
        causal = lane <= sub
        dt, _, acs, acs_t = _ssd_chunk_prelude(sm_ref[...], dtb_ref[...], a_ref[...], lane, sub)
        for g in range(N_GROUPS):
            cols = slice(GROUP_WIDTH * g, GROUP_WIDTH * (g + 1))
            b_off = SSD_WIDTH + D_STATE * g
            c_off = SSD_WIDTH + N_GROUPS * D_STATE + D_STATE * g
            b_b = xc_ref[:, b_off:b_off + D_STATE].astype(BF16)
            c_b = xc_ref[:, c_off:c_off + D_STATE].astype(BF16)
            cb = _mm_nt(c_b, b_b)
            x_g = xc_ref[:, cols]
            acs_g = _expand_group(acs, g, lane)
            xdt_g = x_g * _expand_group(dt, g, lane)
            xdt_b = xdt_g.astype(BF16)
            heads = range(HEADS_PER_GROUP * g, HEADS_PER_GROUP * (g + 1))
            m_b = [(cb * jnp.exp(jnp.where(causal, acs[:, h:h + 1] - acs_t[h:h + 1, :], NEG_BIG))).astype(BF16)
                   for h in heads]
            yd = [_mm(m_b[k], xdt_b[:, LANES * (k // 2):LANES * (k // 2 + 1)]) for k in range(HEADS_PER_GROUP)]
            yd_g = jnp.concatenate([jnp.where(lane < HEAD_DIM, yd[2 * k], yd[2 * k + 1]) for k in range(4)], axis=1)
            h_g = h_scr[g]
            t_g = _mm_nt(c_b, h_g.astype(BF16))
            y_ref[:, cols] = yd_g + jnp.exp(acs_g) * t_g + dsk_ref[:, cols] * x_g
            hs_ref[0, g] = h_g
            last_g = acs_g[CHUNK - 1:CHUNK, :]
            w_b = (xdt_g * jnp.exp(last_g - acs_g)).astype(BF16)
            h_scr[g] = h_g * jnp.exp(_rows_from_lanes(last_g)) + _mm_tn(w_b, b_b)

    return pl.pallas_call(
        body, name="ssd_fwd",
        out_shape=(jax.ShapeDtypeStruct((s, SSD_WIDTH), F32),
                   jax.ShapeDtypeStruct((nc, N_GROUPS, GROUP_WIDTH, D_STATE), F32)),
        grid=(nc,),
        in_specs=[pl.BlockSpec((CHUNK, CONV_CH), lambda c: (c, 0)), pl.BlockSpec((CHUNK, LANES), lambda c: (c, 0)),
                  _const_spec((1, LANES)), _const_spec((1, LANES)), _const_spec((1, SSD_WIDTH))],
        out_specs=(pl.BlockSpec((CHUNK, SSD_WIDTH), lambda c: (c, 0)),
                   pl.BlockSpec((1, N_GROUPS, GROUP_WIDTH, D_STATE), lambda c: (c, 0, 0, 0))),
        scratch_shapes=[pltpu.VMEM((N_GROUPS, GROUP_WIDTH, D_STATE), F32)],
        compiler_params=_params(("arbitrary",)),
    )(xc, small, dtb_row, a_row, dskip_lane)


def ssd_bwd(xc, small, states, dy, dtb_row, a_row, dskip_lane):
    s = xc.shape[0]
    nc = s // CHUNK
    rev = lambda c: nc - 1 - c

    def body(xc_ref, sm_ref, hs_ref, dy_ref, dtb_ref, a_ref, dsk_ref,
             dxc_ref, ddt_ref, da_ref, ddtb_ref, ddsk_ref, dh_scr):
        c = pl.program_id(0)

        @pl.when(c == 0)
        def _():
            dh_scr[...] = jnp.zeros_like(dh_scr)
            da_ref[...] = jnp.zeros_like(da_ref)
            ddtb_ref[...] = jnp.zeros_like(ddtb_ref)
            ddsk_ref[...] = jnp.zeros_like(ddsk_ref)

        lane = _iota((CHUNK, LANES), 1)
        sub = _iota((CHUNK, LANES), 0)
        causal = lane <= sub
        upper = lane >= sub
        is_last = sub == CHUNK - 1
        a_row_v = a_ref[...]
        dt, sig, acs, acs_t = _ssd_chunk_prelude(sm_ref[...], dtb_ref[...], a_row_v, lane, sub)
        cd = jnp.exp(acs[CHUNK - 1:CHUNK, :])
        dacs_c = jnp.zeros((CHUNK, LANES), F32)
        dacs_r = jnp.zeros((LANES, CHUNK), F32)
        ddtx = jnp.zeros((CHUNK, LANES), F32)
        for g in range(N_GROUPS):
            cols = slice(GROUP_WIDTH * g, GROUP_WIDTH * (g + 1))
            b_off = SSD_WIDTH + D_STATE * g
            c_off = SSD_WIDTH + N_GROUPS * D_STATE + D_STATE * g
            b_b = xc_ref[:, b_off:b_off + D_STATE].astype(BF16)
            c_b = xc_ref[:, c_off:c_off + D_STATE].astype(BF16)
            cb = _mm_nt(c_b, b_b)
            cb_t = _mm_nt(b_b, c_b)
            x_g = xc_ref[:, cols]
            dy_g = dy_ref[:, cols]
            dt_g = _expand_group(dt, g, lane)
            acs_g = _expand_group(acs, g, lane)
            last_g = acs_g[CHUNK - 1:CHUNK, :]
            e_g = jnp.exp(acs_g)
            dte_g = jnp.exp(last_g - acs_g)
            xdt_g = x_g * dt_g
            xdt_b = xdt_g.astype(BF16)
            h_g = hs_ref[0, g]
            dh_g = dh_scr[g]
            h_b = h_g.astype(BF16)
            dh_b = dh_g.astype(BF16)
            heads = list(range(HEADS_PER_GROUP * g, HEADS_PER_GROUP * (g + 1)))
            segs = [acs[:, h:h + 1] - acs_t[h:h + 1, :] for h in heads]
            lms = [jnp.exp(jnp.where(causal, sg, NEG_BIG)) for sg in segs]
            mts = [(cb_t * jnp.exp(jnp.where(upper, -sg, NEG_BIG))).astype(BF16) for sg in segs]
            dyh = []
            for k in range(HEADS_PER_GROUP):
                blk = dy_g[:, LANES * (k // 2):LANES * (k // 2 + 1)]
                in_head = (lane < HEAD_DIM) if k % 2 == 0 else (lane >= HEAD_DIM)
                dyh.append(jnp.where(in_head, blk, 0.0).astype(BF16))
            dms = [_mm_nt(dyh[k], xdt_b[:, LANES * (k // 2):LANES * (k // 2 + 1)]) for k in range(HEADS_PER_GROUP)]
            dxs = [_mm(mts[k], dyh[k]) for k in range(HEADS_PER_GROUP)]
            dcb = jnp.zeros((CHUNK, CHUNK), F32)
            for k, h in enumerate(heads):
                gmat = dms[k] * (cb * lms[k])
                dacs_c = dacs_c + jnp.where(lane == h, jnp.sum(gmat, axis=1, keepdims=True), 0.0)
                dacs_r = dacs_r - jnp.where(sub == h, jnp.sum(gmat, axis=0, keepdims=True), 0.0)
                dcb = dcb + dms[k] * lms[k]
            dxdt_g = jnp.concatenate([dxs[2 * k] + dxs[2 * k + 1] for k in range(4)], axis=1)
            t_g = _mm_nt(c_b, h_b)
            dacs_c = dacs_c + _head_sums(dy_g * e_g * t_g, g)
            dt_b = (dy_g * e_g).astype(BF16)
            dc_acc = _mm(dt_b, h_b)
            dh_prev = _mm_tn(dt_b, c_b)
            dw_g = _mm_nt(b_b, dh_b)
            w_g = xdt_g * dte_g
            dxdt_g = dxdt_g + dw_g * dte_g
            db_acc = _mm(w_g.astype(BF16), dh_b)
            r2 = _head_sums(dw_g * w_g, g)
            dacs_c = dacs_c + jnp.where(is_last, jnp.sum(r2, axis=0, keepdims=True), 0.0) - r2
            q3 = jnp.sum(dh_g * h_g, axis=1, keepdims=True)
            for k, h in enumerate(heads):
                tot = jnp.sum(q3[HEAD_DIM * k:HEAD_DIM * (k + 1), :], keepdims=True) * cd[:, h:h + 1]
                dacs_c = dacs_c + jnp.where(is_last & (lane == h), tot, 0.0)
            dh_scr[g] = dh_prev + dh_g * jnp.exp(_rows_from_lanes(last_g))
            dxc_ref[:, cols] = dxdt_g * dt_g + dsk_ref[:, cols] * dy_g
            ddtx = ddtx + _head_sums(dxdt_g * x_g, g)
            ddsk_ref[:, cols] += jnp.sum(dy_g * x_g, axis=0, keepdims=True)
            dxc_ref[:, b_off:b_off + D_STATE] = db_acc + _mm(dcb.T.astype(BF16), c_b)
            dxc_ref[:, c_off:c_off + D_STATE] = dc_acc + _mm(dcb.astype(BF16), b_b)
        dacs = dacs_c + dacs_r.T
        dadt = _mm_exact((lane >= sub).astype(F32), dacs)
        ddt = dadt * a_row_v + ddtx
        ddt_raw = ddt * sig
        ddt_ref[...] = ddt_raw
        da_ref[...] += jnp.sum(dadt * dt, axis=0, keepdims=True)
        ddtb_ref[...] += jnp.sum(ddt_raw, axis=0, keepdims=True)

    return pl.pallas_call(
        body, name="ssd_bwd",
        out_shape=(jax.ShapeDtypeStruct((s, CONV_CH), F32), jax.ShapeDtypeStruct((s, LANES), F32),
                   jax.ShapeDtypeStruct((1, LANES), F32), jax.ShapeDtypeStruct((1, LANES), F32),
                   jax.ShapeDtypeStruct((1, SSD_WIDTH), F32)),
        grid=(nc,),
        in_specs=[pl.BlockSpec((CHUNK, CONV_CH), lambda c: (rev(c), 0)),
                  pl.BlockSpec((CHUNK, LANES), lambda c: (rev(c), 0)),
                  pl.BlockSpec((1, N_GROUPS, GROUP_WIDTH, D_STATE), lambda c: (rev(c), 0, 0, 0)),
                  pl.BlockSpec((CHUNK, SSD_WIDTH), lambda c: (rev(c), 0)),
                  _const_spec((1, LANES)), _const_spec((1, LANES)), _const_spec((1, SSD_WIDTH))],
        out_specs=(pl.BlockSpec((CHUNK, CONV_CH), lambda c: (rev(c), 0)),
                   pl.BlockSpec((CHUNK, LANES), lambda c: (rev(c), 0)),
                   _const_spec((1, LANES)), _const_spec((1, LANES)), _const_spec((1, SSD_WIDTH))),
        scratch_shapes=[pltpu.VMEM((N_GROUPS, GROUP_WIDTH, D_STATE), F32)],
        compiler_params=_params(("arbitrary",)),
    )(xc, small, states, dy, dtb_row, a_row, dskip_lane)


FORGET_BLOCK = 512


def forget_cumsum(small, fgb_row):
    s = small.shape[0]
    t = _blk(s, FORGET_BLOCK)
    nb = s // t

    def body(sm_ref, b_ref, cc_ref, carry):
        i = pl.program_id(0)

        @pl.when(i == 0)
        def _():
            carry[...] = jnp.zeros_like(carry)

        lane = _iota((t, LANES), 1)
        in_f = (lane >= N_HEADS) & (lane < 2 * N_HEADS)
        logf = jnp.where(in_f, -_softplus(-(sm_ref[...] + b_ref[...])), 0.0)
        tri = (_iota((t, t), 1) <= _iota((t, t), 0)).astype(F32)
        cum = _mm_exact(tri, logf) + carry[0:1, :]
        cc_ref[...] = cum
        carry[...] = jnp.broadcast_to(cum[t - 1:t, :], (8, LANES))

    return pl.pallas_call(
        body, name="forget_cumsum",
        out_shape=jax.ShapeDtypeStruct((s, LANES), F32),
        grid=(nb,),
        in_specs=[pl.BlockSpec((t, LANES), lambda i: (i, 0)), _const_spec((1, LANES))],
        out_specs=pl.BlockSpec((t, LANES), lambda i: (i, 0)),
        scratch_shapes=[pltpu.VMEM((8, LANES), F32)],
        compiler_params=_params(("arbitrary",)),
    )(small, fgb_row)


def forget_bwd(dc, small, ddt_raw, fgb_row):
    s = small.shape[0]
    t = _blk(s, FORGET_BLOCK)
    nb = s // t
    rev = lambda i: nb - 1 - i

    def body(dc_ref, sm_ref, ddt_ref, b_ref, ds_ref, dfb_ref, carry):
        i = pl.program_id(0)

        @pl.when(i == 0)
        def _():
            carry[...] = jnp.zeros_like(carry)
            dfb_ref[...] = jnp.zeros_like(dfb_ref)

        lane = _iota((t, LANES), 1)
        rows = dc_ref[...].T
        tri = (_iota((t, t), 1) <= _iota((t, t), 0)).astype(F32)
        rc = _mm_exact(rows, tri) + carry[:, 0:1]
        carry[...] = jnp.broadcast_to(rc[:, 0:1], (LANES, LANES))
        in_f = (lane >= N_HEADS) & (lane < 2 * N_HEADS)
        df = jnp.where(in_f, rc.T * _sigmoid(-(sm_ref[...] + b_ref[...])), 0.0)
        ds_ref[...] = (df + ddt_ref[...]).astype(BF16)
        dfb_ref[...] += jnp.sum(df, axis=0, keepdims=True)

    blk = pl.BlockSpec((t, LANES), lambda i: (rev(i), 0))
    return pl.pallas_call(
        body, name="forget_bwd",
        out_shape=(jax.ShapeDtypeStruct((s, LANES), BF16), jax.ShapeDtypeStruct((1, LANES), F32)),
        grid=(nb,),
        in_specs=[blk, blk, blk, _const_spec((1, LANES))],
        out_specs=(blk, _const_spec((1, LANES))),
        scratch_shapes=[pltpu.VMEM((LANES, LANES), F32)],
        compiler_params=_params(("arbitrary",)),
    )(dc, small, ddt_raw, fgb_row)


ATT_BLOCK = 1024
ATT_BLOCK_BWD = 512
ATT_BLOCK_BWD_Q = 512
ATT_SCALE = HEAD_DIM ** -0.5
AUG_A = HEAD_DIM
AUG_B = HEAD_DIM + 3


def _split3(c):
    hi = c.astype(BF16).astype(F32)
    r = c - hi
    mid = r.astype(BF16).astype(F32)
    return hi, mid, (r - mid).astype(BF16).astype(F32)


def _aug(lane, first, parts=None, value=1.0):
    if parts is None:
        return jnp.where((lane >= first) & (lane < first + 3), value, 0.0)
    return (jnp.where(lane == first, parts[0], 0.0) + jnp.where(lane == first + 1, parts[1], 0.0)
            + jnp.where(lane == first + 2, parts[2], 0.0))


def _pack_pair(a0, a1, lane):
    return jnp.where(lane < HEAD_DIM, a0, pltpu.roll(a1, HEAD_DIM, 1))


def proj_qkv_heads(u, w_q, w_k, w_v, cum, later):
    s = u.shape[0]
    tm = _blk(s, 256)
    nsteps = s // tm
    n_later = len(later)

    def body(u_ref, wq_ref, wk_ref, wv_ref, c_ref, *rest):
        later_in = rest[:n_later]
        qa_ref, ka_ref, va_ref, nrm_ref = rest[n_later:n_later + 4]
        later_out = rest[n_later + 4:2 * n_later + 4]
        sems = rest[2 * n_later + 4:]
        step = pl.program_id(0)

        @pl.when(step == 0)
        def _():
            for cp in gather_copies(later_in, later_out, *sems)[0]:
                cp.start()

        @pl.when(step == nsteps // 2)
        def _():
            for landed, onward in zip(*gather_copies(later_in, later_out, *sems)):
                landed.wait_recv()
                onward.start()

        @pl.when(step == nsteps - 1)
        def _():
            fetched, passed = gather_copies(later_in, later_out, *sems)
            for cp in passed:
                cp.wait_recv()
            for cp in fetched + passed:
                cp.wait_send()

        lane = _iota((tm, LANES), 1)
        lo = lane < HEAD_DIM
        uv = u_ref[...]
        qf = _mm(uv, wq_ref[...]) * ATT_SCALE
        kf = _mm(uv, wk_ref[...])
        vf = _mm(uv, wv_ref[...])
        cc = c_ref[...]
        ones_a = _aug(lane, AUG_A)
        ones_b = _aug(lane, AUG_B)
        sub8 = _iota((8, LANES), 0)
        nrm = jnp.zeros((8, LANES), F32)
        for h in range(N_HEADS):
            j, e = divmod(h, 2)

            def head(full):
                blk = full[:, LANES * j:LANES * (j + 1)]
                if e == 1:
                    blk = pltpu.roll(blk, HEAD_DIM, 1)
                return jnp.where(lo, blk, 0.0)

            parts = _split3(cc[:, N_HEADS + h:N_HEADS + h + 1])
            qh, kh = head(qf), head(kf)
            qa_ref[h] = (qh + _aug(lane, AUG_A, parts) + ones_b).astype(BF16)
            ka_ref[h] = (kh + ones_a - _aug(lane, AUG_B, parts)).astype(BF16)
            va_ref[h] = (head(vf) + ones_a).astype(BF16)
        seg = (_iota((ATT_WIDTH, LANES), 1) == (_iota((ATT_WIDTH, LANES), 0) >> 6)).astype(BF16)
        for r, val in enumerate((qf, kf)):
            sq = val * val
            hi = sq.astype(BF16)
            tot = _mm(hi, seg) + _mm((sq - hi.astype(F32)).astype(BF16), seg)
            nrm = nrm + jnp.where(sub8 == r, jnp.max(tot, axis=0, keepdims=True), 0.0)
        nrm_ref[0] = nrm

    shp = jax.ShapeDtypeStruct((N_HEADS, s, LANES), BF16)
    hspec = pl.BlockSpec((N_HEADS, tm, LANES), lambda i: (0, i, 0))
    wspec = _const_spec((D_MODEL, ATT_WIDTH))
    return pl.pallas_call(
        body, name="proj_qkv_heads",
        out_shape=tuple([shp, shp, shp, jax.ShapeDtypeStruct((nsteps, 8, LANES), F32)]
                        + [jax.ShapeDtypeStruct((N_CHIPS,) + a.shape, a.dtype) for a in later]),
        grid=(nsteps,),
        in_specs=[pl.BlockSpec((tm, D_MODEL), lambda i: (i, 0)), wspec, wspec, wspec,
                  pl.BlockSpec((tm, LANES), lambda i: (i, 0))] + [ANY] * n_later,
        out_specs=tuple([hspec, hspec, hspec, pl.BlockSpec((1, 8, LANES), lambda i: (i, 0, 0))]
                        + [ANY] * n_later),
        scratch_shapes=_sems(3 * n_later) + _sems(3 * n_later),
        compiler_params=_params(("arbitrary",)),
    )(u, w_q, w_k, w_v, cum, *later)


SKIP_BELOW = -110.0


def live_blocks(norms, cum, tq, tk):
    qn = jnp.sqrt(jnp.max(norms[:, 0, :N_HEADS], axis=0))
    kn = jnp.sqrt(jnp.max(norms[:, 1, :N_HEADS], axis=0))
    bound = 2.05 * qn * kn + 2.0
    c_first = cum[0::tq, N_HEADS:2 * N_HEADS]
    c_last = cum[tk - 1::tk, N_HEADS:2 * N_HEADS]
    nq, nk = c_first.shape[0], c_last.shape[0]
    top = bound[None, None, :] + c_first[:, None, :] - c_last[None, :, :]
    before = (jnp.arange(nk)[None, :] + 1) * tk <= jnp.arange(nq)[:, None] * tq
    dead = before[:, :, None] & ~(top >= SKIP_BELOW)
    first = jnp.sum(dead, axis=1).astype(jnp.int32).T
    last_q = jnp.sum(first[:, None, :] <= jnp.arange(nk)[None, :, None], axis=2).astype(jnp.int32) - 1
    return first, last_q


def attention_fwd(first, qa, ka, va):
    s = qa.shape[1]
    t = _blk(s, ATT_BLOCK)
    nq = s // t

    def body(first_ref, qa_ref, ka_ref, va_ref, o_ref, qb_ref, m_scr, acc_scr, alpha_scr, p_scr, s_scr):
        qi = pl.program_id(1)
        starts = [first_ref[2 * pl.program_id(0) + e, qi] for e in range(2)]
        k0 = jnp.maximum(starts[0], starts[1])
        m_scr[...] = jnp.full_like(m_scr, NEG_BIG)
        acc_scr[...] = jnp.zeros_like(acc_scr)

        def kv_rows(kb):
            return pl.ds(pl.multiple_of(kb * t, t), t)

        def logits(kb, masked, heads=(0, 1)):
            for e in heads:
                sc = _mm_nt(qa_ref[e], ka_ref[e, kv_rows(kb), :])
                if masked:
                    sc = jnp.where(_iota((t, t), 0) >= _iota((t, t), 1), sc, NEG_BIG)
                s_scr[e] = sc

        def probs(heads=(0, 1)):
            for e in heads:
                cmax = s_scr[e, :, 0:LANES]
                for c in range(1, t // LANES):
                    cmax = jnp.maximum(cmax, s_scr[e, :, LANES * c:LANES * (c + 1)])
                m_old = m_scr[e]
                m_new = jnp.maximum(m_old, jnp.max(cmax, axis=1, keepdims=True))
                alpha_scr[e] = jnp.exp(m_old - m_new)
                m_scr[e] = m_new
                for c in range(t // LANES):
                    cols = slice(LANES * c, LANES * (c + 1))
                    p_scr[e, :, cols] = jnp.exp(s_scr[e, :, cols] - m_new).astype(BF16)

        def accumulate(kb, heads=(0, 1)):
            for e in heads:
                acc_scr[e] = alpha_scr[e] * acc_scr[e] + _mm(p_scr[e], va_ref[e, kv_rows(kb), :])

        for e in range(2):
            def alone(kb, carry, e=e):
                logits(kb, False, (e,))
                probs((e,))
                accumulate(kb, (e,))
                return carry

            lax.fori_loop(starts[e], k0, alone, 0)

        def loop_body(kb, carry):
            logits(kb, False)
            for e in range(2):
                accumulate(kb - 1, (e,))
                probs((e,))
            return carry

        @pl.when(qi > k0)
        def _():
            logits(k0, False)
            probs()

        lax.fori_loop(k0 + 1, qi, loop_body, 0)

        @pl.when(qi > k0)
        def _():
            logits(qi, True)
            accumulate(qi - 1)
            probs()

        @pl.when(qi == k0)
        def _():
            logits(qi, True)
            probs()

        accumulate(qi)

        lane = _iota((t, LANES), 1)
        outs = []
        for e in range(2):
            acc = acc_scr[e]
            l = acc[:, AUG_A:AUG_A + 1]
            outs.append(acc / l)
            lse = m_scr[e][:, 0:1] + jnp.log(l)
            q32 = qa_ref[e].astype(F32)
            c = q32[:, AUG_A:AUG_A + 1] + q32[:, AUG_A + 1:AUG_A + 2] + q32[:, AUG_A + 2:AUG_A + 3]
            qb = jnp.where(lane < HEAD_DIM, q32, 0.0) + _aug(lane, AUG_A, _split3(c - lse)) + _aug(lane, AUG_B)
            qb_ref[e] = qb.astype(BF16)
        o_ref[...] = _pack_pair(outs[0], outs[1], lane)

    grid_spec = pltpu.PrefetchScalarGridSpec(
        num_scalar_prefetch=1, grid=(N_PAIRS, nq),
        in_specs=[pl.BlockSpec((2, t, LANES), lambda j, qi, f: (j, qi, 0)),
                  pl.BlockSpec((2, s, LANES), lambda j, qi, f: (j, 0, 0)),
                  pl.BlockSpec((2, s, LANES), lambda j, qi, f: (j, 0, 0))],
        out_specs=[pl.BlockSpec((t, LANES), lambda j, qi, f: (qi, j)),
                   pl.BlockSpec((2, t, LANES), lambda j, qi, f: (j, qi, 0))],
        scratch_shapes=[pltpu.VMEM((2, t, LANES), F32), pltpu.VMEM((2, t, LANES), F32),
                        pltpu.VMEM((2, t, LANES), F32), pltpu.VMEM((2, t, t), BF16), pltpu.VMEM((2, t, t), F32)])
    return pl.pallas_call(
        body, name="attention_fwd", grid_spec=grid_spec,
        out_shape=(jax.ShapeDtypeStruct((s, ATT_WIDTH), F32), jax.ShapeDtypeStruct((N_HEADS, s, LANES), BF16)),
        compiler_params=_params(("parallel", "parallel")),
    )(first, qa, ka, va)


def attention_bwd(last_q, qb, ka, va, dob):
    s = qb.shape[1]
    t = _blk(s, ATT_BLOCK_BWD)
    tq = _blk(s, ATT_BLOCK_BWD_Q)
    nq = s // tq
    per_q = tq // t

    def body(last_ref, qb_ref, dob_ref, ka_ref, va_ref, dq_ref, dk_ref, dv_ref, dc_ref, dq_scr, dk_scr, dv_scr):
        j, ki = pl.program_id(0), pl.program_id(1)

        @pl.when((j == 0) & (ki == 0))
        def _():
            dc_ref[...] = jnp.zeros_like(dc_ref)

        @pl.when(ki == 0)
        def _():
            dq_scr[...] = jnp.zeros_like(dq_scr)

        dk_scr[...] = jnp.zeros_like(dk_scr)
        dv_scr[...] = jnp.zeros_like(dv_scr)

        def q_step(qblk, masked, heads=(0, 1)):
            rows = pl.ds(pl.multiple_of(qblk * tq, tq), tq)
            scs = [_mm_nt(qb_ref[e, rows, :], ka_ref[e]) for e in heads]
            dps = [_mm_nt(dob_ref[e, rows, :], va_ref[e]) for e in heads]
            for e, sc, dp in zip(heads, scs, dps):
                q = qb_ref[e, rows, :]
                do = dob_ref[e, rows, :]
                if masked:
                    keep = (_iota((tq, t), 0) - _iota((tq, t), 1)) >= ki * t - qblk * tq
                    sc = jnp.where(keep, sc, NEG_BIG)
                p = jnp.exp(sc)
                ds_b = (p * dp).astype(BF16)
                dv_scr[e] += _mm_tn(p.astype(BF16), do)
                dk_scr[e] += _mm_tn(ds_b, q)
                dq_scr[e, rows, :] += _mm(ds_b, ka_ref[e])

        def loop_body(qblk, carry):
            q_step(qblk, False)
            return carry

        ends = [last_ref[2 * j + e, ki] + 1 for e in range(2)]
        both = jnp.minimum(ends[0], ends[1])
        diag = ki // per_q
        q_step(diag, True)
        lax.fori_loop(diag + 1, both, loop_body, 0)
        for e in range(2):
            def alone(qblk, carry, e=e):
                q_step(qblk, False, (e,))
                return carry

            lax.fori_loop(both, ends[e], alone, 0)

        lane = _iota((t, LANES), 1)
        dk_ref[...] = _pack_pair(dk_scr[0], dk_scr[1], lane).astype(BF16)
        dv_ref[...] = _pack_pair(dv_scr[0], dv_scr[1], lane).astype(BF16)
        rows = pl.ds(pl.multiple_of(ki * t, t), t)
        dc_ref[rows, :] -= (jnp.where(lane == N_HEADS + 2 * j, dk_scr[0][:, AUG_B:AUG_B + 1], 0.0)
                            + jnp.where(lane == N_HEADS + 2 * j + 1, dk_scr[1][:, AUG_B:AUG_B + 1], 0.0))

        @pl.when(ki == s // t - 1)
        def _():
            for blk in range(s // t):
                rws = pl.ds(blk * t, t)
                d0 = dq_scr[0, rws, :]
                d1 = dq_scr[1, rws, :]
                dq_ref[rws, :] = (_pack_pair(d0, d1, lane) * ATT_SCALE).astype(BF16)
                dc_ref[rws, :] += (jnp.where(lane == N_HEADS + 2 * j, d0[:, AUG_A:AUG_A + 1], 0.0)
                                   + jnp.where(lane == N_HEADS + 2 * j + 1, d1[:, AUG_A:AUG_A + 1], 0.0))

    full = pl.BlockSpec((2, s, LANES), lambda j, ki, f: (j, 0, 0))
    blk = pl.BlockSpec((2, t, LANES), lambda j, ki, f: (j, ki, 0))
    pair = pl.BlockSpec((t, LANES), lambda j, ki, f: (ki, j))
    wide = jax.ShapeDtypeStruct((s, ATT_WIDTH), BF16)
    grid_spec = pltpu.PrefetchScalarGridSpec(
        num_scalar_prefetch=1, grid=(N_PAIRS, s // t),
        in_specs=[full, full, blk, blk],
        out_specs=[pl.BlockSpec((s, LANES), lambda j, ki, f: (0, j)), pair, pair,
                   pl.BlockSpec((s, LANES), lambda j, ki, f: (0, 0))],
        scratch_shapes=[pltpu.VMEM((2, s, LANES), F32), pltpu.VMEM((2, t, LANES), F32),
                        pltpu.VMEM((2, t, LANES), F32)])
    return pl.pallas_call(
        body, name="attention_bwd", grid_spec=grid_spec,
        out_shape=(wide, wide, wide, jax.ShapeDtypeStruct((s, LANES), F32)),
        compiler_params=_params(("arbitrary", "arbitrary")),
    )(last_q, qb, dob, ka, va)


def _dsilu(z, sg):
    return sg * (1.0 + z * (1.0 - sg))


def post_mix(x, y, zs, o, za, p, tgt, ssd_g, att_g_lane, ple_g, fin_g, w_out, w_gate, w_proj):
    s = x.shape[0]
    tm = _blk(s, 256)
    half = SSD_WIDTH // N_GROUPS

    def rms_bwd(dy, yn, r):
        return r * (dy - yn * jnp.mean(dy * yn, axis=-1, keepdims=True))

    def colsum(a):
        return jnp.sum(a, axis=0, keepdims=True)

    def body(x_ref, y_ref, zs_ref, o_ref, za_ref, p_ref, t_ref, sg_ref, ag_ref, pg_ref, fg_ref,
             wo_ref, wg_ref, wp_ref,
             dh1_ref, dy_ref, dzs_ref, dob_ref, dza_ref, ycat_ref, dh1b_ref, n2b_ref, dglb_ref, dppb_ref, pb_ref,
             loss_ref, dfin_ref, dple_ref, dssd_ref, datt_ref):
        @pl.when(pl.program_id(0) == 0)
        def _():
            for r in (loss_ref, dfin_ref, dple_ref, dssd_ref, datt_ref):
                r[...] = jnp.zeros_like(r)

        lane = _iota((tm, LANES), 1)
        lo = lane < HEAD_DIM
        zs = zs_ref[...]
        sz = _sigmoid(zs)
        yv = y_ref[...]
        ys = yv * (zs * sz)
        yn, rg = [], []
        for g in range(N_GROUPS):
            seg = ys[:, half * g:half * (g + 1)]
            r = lax.rsqrt(jnp.mean(seg * seg, axis=-1, keepdims=True) + EPS)
            yn.append(seg * r)
            rg.append(r)
            ycat_ref[:, half * g:half * (g + 1)] = (yn[g] * sg_ref[:, half * g:half * (g + 1)]).astype(BF16)
        za = za_ref[...]
        sza = _sigmoid(za)
        silu_za = za * sza
        on, ra = [], []
        for jb in range(N_PAIRS):
            blk = o_ref[:, LANES * jb:LANES * (jb + 1)]
            sq = blk * blk
            ms0 = jnp.sum(jnp.where(lo, sq, 0.0), axis=1, keepdims=True) * (1.0 / HEAD_DIM)
            ms1 = jnp.sum(jnp.where(lo, 0.0, sq), axis=1, keepdims=True) * (1.0 / HEAD_DIM)
            r = jnp.where(lo, lax.rsqrt(ms0 + EPS), lax.rsqrt(ms1 + EPS))
            on.append(blk * r)
            ra.append(r)
            an = on[jb] * ag_ref[:, LANES * jb:LANES * (jb + 1)]
            ycat_ref[:, SSD_WIDTH + LANES * jb:SSD_WIDTH + LANES * (jb + 1)] = (
                an * silu_za[:, LANES * jb:LANES * (jb + 1)]).astype(BF16)
        h1 = x_ref[...] + _mm(ycat_ref[...], wo_ref[...])
        r2 = lax.rsqrt(jnp.mean(h1 * h1, axis=-1, keepdims=True) + EPS)
        n2h = h1 * r2
        n2_b = (n2h * pg_ref[...]).astype(BF16)
        gate = _sigmoid(_mm(n2_b, wg_ref[...]))
        p_b = p_ref[...].astype(BF16)
        pp = _mm(p_b, wp_ref[...])
        h2 = h1 + gate * pp
        r3 = lax.rsqrt(jnp.mean(h2 * h2, axis=-1, keepdims=True) + EPS)
        n3 = h2 * r3
        diff = n3 * fg_ref[...] - t_ref[...]
        sq = colsum(diff * diff)
        part = sq[:, 0:LANES]
        for jb in range(1, D_MODEL // LANES):
            part = part + sq[:, LANES * jb:LANES * (jb + 1)]
        loss_ref[...] += part * (0.5 / D_MODEL)
        dout = diff * (1.0 / D_MODEL)
        dfin_ref[...] += colsum(dout * n3)
        dh2 = rms_bwd(dout * fg_ref[...], n3, r3)
        dgl = dh2 * pp * gate * (1.0 - gate)
        dgl_b = dgl.astype(BF16)
        dn2 = _mm_nt(dgl_b, wg_ref[...])
        dple_ref[...] += colsum(dn2 * n2h)
        dh1 = dh2 + rms_bwd(dn2 * pg_ref[...], n2h, r2)
        dh1_b = dh1.astype(BF16)
        dycat = _mm_nt(dh1_b, wo_ref[...])
        dh1_ref[...] = dh1
        dh1b_ref[...] = dh1_b
        n2b_ref[...] = n2_b
        dglb_ref[...] = dgl_b
        dppb_ref[...] = (dh2 * gate).astype(BF16)
        pb_ref[...] = p_b
        for g in range(N_GROUPS):
            cols = slice(half * g, half * (g + 1))
            dys_g = dycat[:, cols]
            dssd_ref[:, cols] += colsum(dys_g * yn[g])
            dys = rms_bwd(dys_g * sg_ref[:, cols], yn[g], rg[g])
            dy_ref[:, cols] = dys * (zs[:, cols] * sz[:, cols])
            dzs_ref[:, cols] = (dys * yv[:, cols] * _dsilu(zs[:, cols], sz[:, cols])).astype(BF16)
        for jb in range(N_PAIRS):
            cols = slice(LANES * jb, LANES * (jb + 1))
            dya = dycat[:, SSD_WIDTH + LANES * jb:SSD_WIDTH + LANES * (jb + 1)]
            ag = ag_ref[:, cols]
            dan = dya * silu_za[:, cols]
            dza_ref[:, cols] = (dya * (on[jb] * ag) * _dsilu(za[:, cols], sza[:, cols])).astype(BF16)
            datt_ref[:, cols] += colsum(dan * on[jb])
            don = dan * ag
            q = don * on[jb]
            m0 = jnp.sum(jnp.where(lo, q, 0.0), axis=1, keepdims=True) * (1.0 / HEAD_DIM)
            m1 = jnp.sum(jnp.where(lo, 0.0, q), axis=1, keepdims=True) * (1.0 / HEAD_DIM)
            do2 = ra[jb] * (don - on[jb] * jnp.where(lo, m0, m1))
            prod = do2 * o_ref[:, cols]
            for e in range(2):
                delta = jnp.sum(jnp.where(lo, prod, 0.0) if e == 0 else jnp.where(lo, 0.0, prod),
                                axis=1, keepdims=True)
                base = jnp.where(lo, do2 if e == 0 else pltpu.roll(do2, HEAD_DIM, 1), 0.0)
                dob_ref[2 * jb + e] = (base - _aug(lane, AUG_A, _split3(delta))).astype(BF16)

    def rows(n, dtype=None):
        return pl.BlockSpec((tm, n), lambda i: (i, 0))

    def out(n, dtype):
        return jax.ShapeDtypeStruct((s, n), dtype)

    vec = _const_spec((1, D_MODEL))
    vshape = jax.ShapeDtypeStruct((1, D_MODEL), F32)
    return pl.pallas_call(
        body, name="post_mix",
        out_shape=(out(D_MODEL, F32), out(SSD_WIDTH, F32), out(SSD_WIDTH, BF16),
                   jax.ShapeDtypeStruct((N_HEADS, s, LANES), BF16),
                   out(ATT_WIDTH, BF16), out(D_INNER, BF16), out(D_MODEL, BF16), out(D_MODEL, BF16),
                   out(D_MODEL, BF16), out(D_MODEL, BF16), out(PLE_DIM, BF16),
                   jax.ShapeDtypeStruct((1, LANES), F32), vshape, vshape, vshape, vshape),
        grid=(s // tm,),
        in_specs=[rows(D_MODEL), rows(SSD_WIDTH), rows(SSD_WIDTH), rows(ATT_WIDTH), rows(ATT_WIDTH),
                  rows(PLE_DIM), rows(D_MODEL), vec, vec, vec, vec,
                  _const_spec((D_INNER, D_MODEL)), _const_spec((D_MODEL, D_MODEL)), _const_spec((PLE_DIM, D_MODEL))],
        out_specs=(rows(D_MODEL), rows(SSD_WIDTH), rows(SSD_WIDTH),
                   pl.BlockSpec((N_HEADS, tm, LANES), lambda i: (0, i, 0)), rows(ATT_WIDTH),
                   rows(D_INNER), rows(D_MODEL), rows(D_MODEL), rows(D_MODEL), rows(D_MODEL), rows(PLE_DIM),
                   _const_spec((1, LANES)), vec, vec, vec, vec),
        compiler_params=_params(("arbitrary",)),
    )(x, y, zs, o, za, p, tgt, ssd_g, att_g_lane, ple_g, fin_g, w_out, w_gate, w_proj)


def in_proj_bwd(dsegs, wsegs, x, g, dh1, pres):
    s = x.shape[0]
    tm = _blk(s, 512)
    nseg = len(dsegs)
    nbig = len(pres)
    nsteps = s // tm

    def body(*refs):
        d_refs = refs[:nseg]
        w_refs = refs[nseg:2 * nseg]
        x_ref, g_ref, dh1_ref = refs[2 * nseg:2 * nseg + 3]
        rest = refs[2 * nseg + 3:]
        pre_refs, (dx_ref, dg_ref), part_refs = rest[:nbig], rest[nbig:nbig + 2], rest[nbig + 2:2 * nbig + 2]
        ssem, rsem, lsem = rest[2 * nbig + 2:]

        @pl.when(pl.program_id(0) == 0)
        def _():
            dg_ref[...] = jnp.zeros_like(dg_ref)
            for cp in scatter_copies(pre_refs, part_refs, ssem, rsem, lsem):
                cp.start()

        @pl.when(pl.program_id(0) == nsteps - 1)
        def _():
            for cp in scatter_copies(pre_refs, part_refs, ssem, rsem, lsem):
                cp.wait()

        du = _mm_nt(d_refs[0][...], w_refs[0][...])
        for k in range(1, nseg):
            du = du + _mm_nt(d_refs[k][...], w_refs[k][...])
        xv = x_ref[...]
        r = lax.rsqrt(jnp.mean(xv * xv, axis=-1, keepdims=True) + EPS)
        xh = xv * r
        dg_ref[...] += jnp.sum(du * xh, axis=0, keepdims=True)
        dxh = du * g_ref[...]
        dx_ref[...] = r * (dxh - xh * jnp.mean(dxh * xh, axis=-1, keepdims=True)) + dh1_ref[...]

    rows = lambda n: pl.BlockSpec((tm, n), lambda i: (i, 0))
    return pl.pallas_call(
        body, name="in_proj_bwd",
        out_shape=tuple([jax.ShapeDtypeStruct((s, D_MODEL), F32), jax.ShapeDtypeStruct((1, D_MODEL), F32)]
                        + [jax.ShapeDtypeStruct(a.shape, a.dtype) for a in pres]),
        grid=(nsteps,),
        in_specs=([rows(d.shape[1]) for d in dsegs] + [_const_spec(w.shape) for w in wsegs]
                  + [rows(D_MODEL), _const_spec((1, D_MODEL)), rows(D_MODEL)] + [ANY] * nbig),
        out_specs=tuple([rows(D_MODEL), _const_spec((1, D_MODEL))] + [ANY] * nbig),
        scratch_shapes=_sems(3 * nbig) + [pltpu.SemaphoreType.DMA((nbig,))],
        compiler_params=_params(("arbitrary",)),
    )(*dsegs, *wsegs, x, g, dh1, *pres)


SMALL_NAMES = ("norm_g", "conv_b", "dt_bias", "a_log", "d_skip", "ssd_norm_g", "fg_bias", "att_norm_g",
               "ple_norm_g", "final_norm_g")
SMALL_SIZES = (1024, 1536, 16, 16, 16, 1024, 16, 64, 1024, 1024)
CONV_W_SIZE = CONV_WIDTH * CONV_CH


def _pack_small(vals):
    flat = jnp.concatenate([v.reshape(-1).astype(F32) for v in vals])
    flat = jnp.pad(flat, (0, SMALL_ROWS * LANES - flat.shape[0]))
    return flat.reshape(SMALL_ROWS, LANES)


def _unpack_small(pack, shapes):
    flat = pack.reshape(-1)
    out, off = [], 0
    for n, shp in zip(SMALL_SIZES, shapes):
        out.append(flat[off:off + n].reshape(shp))
        off += n
    return out


def _row128(v16, offset=0):
    return jnp.pad(v16.reshape(1, N_HEADS).astype(F32), ((0, 0), (offset, LANES - N_HEADS - offset)))


def local_step(prereduce, later, join_later, x, p, tgt, w_in, conv_w, norm_g, conv_b, dt_bias, a_log, d_skip,
               ssd_norm_g, fg_bias, att_norm_g, ple_norm_g, final_norm_g):
    widths = (SSD_WIDTH, CONV_CH, N_HEADS, ATT_WIDTH, ATT_WIDTH, ATT_WIDTH, ATT_WIDTH)
    c0, c1, c2, c3, c4, c5, c6, c7 = [sum(widths[:i]) for i in range(len(widths) + 1)]
    w_zs, w_xbc, w_dt = w_in[:, c0:c1], w_in[:, c1:c2], w_in[:, c2:c3]
    w_za, w_q, w_k, w_v, w_f = w_in[:, c3:c4], w_in[:, c4:c5], w_in[:, c5:c6], w_in[:, c6:c7], w_in[:, c7:]
    w_small = jnp.concatenate([w_dt, w_f, jnp.zeros((D_MODEL, LANES - 2 * N_HEADS), BF16)], axis=1)

    dtb_row = _row128(dt_bias)
    a_row = _row128(-jnp.exp(a_log.astype(F32)))
    fgb_row = _row128(fg_bias, N_HEADS)
    dskip_lane = jnp.repeat(d_skip.astype(F32), HEAD_DIM).reshape(1, SSD_WIDTH)
    att_g_lane = jnp.tile(att_norm_g.astype(F32), N_HEADS).reshape(1, ATT_WIDTH)
    row = lambda v: v.reshape(1, -1).astype(F32)

    u, zs, xbc, za, small = in_proj_fwd(x, row(norm_g), [w_zs, w_xbc, w_za, w_small])
    cum = forget_cumsum(small, fgb_row)
    qa, ka, va, norms, *gathered = proj_qkv_heads(u, w_q, w_k, w_v, cum, later)
    w_out, w_gate, w_proj = join_later(gathered)
    n_seq = x.shape[0]
    first, _ = live_blocks(norms, cum, _blk(n_seq, ATT_BLOCK), _blk(n_seq, ATT_BLOCK))
    _, last_q = live_blocks(norms, cum, _blk(n_seq, ATT_BLOCK_BWD_Q), _blk(n_seq, ATT_BLOCK_BWD))
    pre, xc = conv_fwd(xbc, conv_w, row(conv_b))
    y, states = ssd_fwd(xc, small, dtb_row, a_row, dskip_lane)
    o, qb = attention_fwd(first, qa, ka, va)
    (dh1, dy, dzs, dob, dza, ycat, dh1_b, n2_b, dgl_b, dpp_b, p_b,
     loss_l, dfin, dple, dssd_g, datt_lane) = post_mix(
        x, y, zs, o, za, p, tgt, row(ssd_norm_g), att_g_lane, row(ple_norm_g), row(final_norm_g),
        w_out, w_gate, w_proj)
    dq, dk, dv, dc = attention_bwd(last_q, qb, ka, va, dob)
    dxc, ddt_raw, da, ddtb, ddsk_lane = ssd_bwd(xc, small, states, dy, dtb_row, a_row, dskip_lane)
    dsmall, dfgb = forget_bwd(dc, small, ddt_raw, fgb_row)
    dxbc, dconv_w8, dconv_b = conv_bwd(xbc, pre, dxc, conv_w)
    dsegs = [dzs, dxbc, dza, dq, dk, dv, dsmall]
    wsegs = [w_zs, w_xbc, w_za, w_q, w_k, w_v, w_small]
    dws = [matmul_tn(u, d, "dw_in_%d" % i) for i, d in enumerate(dsegs)]
    dw_in = jnp.concatenate([dws[0], dws[1], dws[6][:, :N_HEADS], dws[2], dws[3], dws[4], dws[5],
                             dws[6][:, N_HEADS:2 * N_HEADS]], axis=1)
    dw_out = matmul_tn(ycat, dh1_b, "dw_out")
    dw_gate = matmul_tn(n2_b, dgl_b, "dw_gate")
    dw_proj = matmul_tn(p_b, dpp_b, "dw_proj")
    dx, dnorm_g, *parts = in_proj_bwd(dsegs, wsegs, x, row(norm_g), dh1, prereduce(dw_in, dw_out, dw_gate, dw_proj))
    small_grads = [
        dnorm_g, dconv_b, ddtb[0, :N_HEADS], (da * a_row)[0, :N_HEADS],
        ddsk_lane.reshape(N_HEADS, HEAD_DIM).sum(axis=1), dssd_g, dfgb[0, N_HEADS:2 * N_HEADS],
        datt_lane.reshape(N_HEADS, HEAD_DIM).sum(axis=0), dple, dfin]
    loss = jnp.sum(loss_l)
    return loss, dx, parts, dconv_w8[:CONV_WIDTH], small_grads


def kernel(x, p, norm_g, w_in, conv_w, conv_b, dt_bias, a_log, d_skip, ssd_norm_g, fg_bias, att_norm_g, w_out, ple_norm_g, w_ple_gate, w_ple_proj, final_norm_g, loss_target, m_norm_g, m_w_in, m_conv_w, m_conv_b, m_dt_bias, m_a_log, m_d_skip, m_ssd_norm_g, m_fg_bias, m_att_norm_g, m_w_out, m_ple_norm_g, m_w_ple_gate, m_w_ple_proj, m_final_norm_g, v_norm_g, v_w_in, v_conv_w, v_conv_b, v_dt_bias, v_a_log, v_d_skip, v_ssd_norm_g, v_fg_bias, v_att_norm_g, v_w_out, v_ple_norm_g, v_w_ple_gate, v_w_ple_proj, v_final_norm_g):
    chip = 2 * lax.axis_index("x") + lax.axis_index("y")
    core = lax.axis_index("c")

    big_w = [w_in[0], w_out[0], w_ple_gate[0], w_ple_proj[0]]
    own = [a.astype(BF16) for a in big_w] + [conv_w[0]]

    def joined(mine, gathered, axis):
        return jnp.concatenate([jnp.where(chip == j, mine, gathered[j]) for j in range(N_CHIPS)], axis=axis)

    w_in_all, conv_all = gather_weights(own[:1], own[4])
    w_in_f, conv_w_f = joined(own[0], w_in_all, 1), joined(own[4], conv_all, 1)

    def join_later(gathered):
        return [joined(mine, got, axis) for mine, got, axis in zip(own[1:4], gathered, (0, 0, 1))]

    core1 = core.reshape(1).astype(jnp.int32)

    def prereduce(dw_in, dw_out, dw_gate, dw_proj):
        n_in, n_proj = w_in.shape[2], w_ple_proj.shape[2]
        gs = [jnp.stack([dw_in[:, n_in * j:n_in * (j + 1)] for j in range(N_CHIPS)]),
              dw_out.reshape(N_CHIPS, w_out.shape[1], D_MODEL), dw_gate.reshape(N_CHIPS, w_ple_gate.shape[1], D_MODEL),
              jnp.stack([dw_proj[:, n_proj * j:n_proj * (j + 1)] for j in range(N_CHIPS)])]
        return add_halves(core1, gs, halves_to_sibling(gs))

    smalls_w = [norm_g, conv_b, dt_bias, a_log, d_skip, ssd_norm_g, fg_bias, att_norm_g, ple_norm_g, final_norm_g]
    loss_l, dx, parts, dconv_w, small_grads = local_step(
        prereduce, own[1:4], join_later, x[0], p[0, 0], loss_target[0], w_in_f, conv_w_f,
        *[a.reshape(-1) for a in smalls_w])
    loss = lax.psum(loss_l, ("x", "y", "c"))
    mine = sum_parts(parts)
    *theirs, smalls = swap_halves(mine, _pack_small(list(small_grads) + [dconv_w]))

    g_big, d_big, m_big, v_big = adamw_big(
        core1, mine, theirs, big_w, [m_w_in[0], m_w_out[0], m_w_ple_gate[0], m_w_ple_proj[0]],
        [v_w_in[0], v_w_out[0], v_w_ple_gate[0], v_w_ple_proj[0]])
    smalls_m = [m_norm_g, m_conv_b, m_dt_bias, m_a_log, m_d_skip, m_ssd_norm_g, m_fg_bias, m_att_norm_g,
                m_ple_norm_g, m_final_norm_g]
    smalls_v = [v_norm_g, v_conv_b, v_dt_bias, v_a_log, v_d_skip, v_ssd_norm_g, v_fg_bias, v_att_norm_g,
                v_ple_norm_g, v_final_norm_g]
    g_sm, d_sm, m_sm, v_sm = adamw_small(smalls, _pack_small(smalls_w), _pack_small(smalls_m), _pack_small(smalls_v))
    n_small = sum(SMALL_SIZES)
    g_conv_full = g_sm.reshape(-1)[n_small:n_small + CONV_W_SIZE].reshape(CONV_WIDTH, CONV_CH)
    n_conv = conv_w.shape[2]
    g_conv = lax.dynamic_slice_in_dim(g_conv_full, chip * n_conv, n_conv, axis=1)
    d_conv, m_conv, v_conv = adamw_whole(g_conv, conv_w[0], m_conv_w[0], v_conv_w[0], "adamw_conv")

    shapes = [a.shape for a in smalls_w]
    outs = []
    for big, conv, sm in ((g_big, g_conv, g_sm), (d_big, d_conv, d_sm), (m_big, m_conv, m_sm), (v_big, v_conv, v_sm)):
        b_in, b_out, b_gate, b_proj = [a[None] for a in big]
        s_norm, s_convb, s_dtb, s_alog, s_dsk, s_ssdg, s_fgb, s_attg, s_pleg, s_fin = _unpack_small(sm, shapes)
        outs.extend([s_norm, b_in, conv[None], s_convb, s_dtb, s_alog, s_dsk, s_ssdg, s_fgb, s_attg, b_out, s_pleg,
                     b_gate, b_proj, s_fin])
    return (loss, dx[None], *outs)
```

```python
import functools

import jax
import jax.numpy as jnp
from jax import lax
from jax.experimental import pallas as pl
from jax.experimental.pallas import tpu as pltpu

F32 = jnp.float32
BF16 = jnp.bfloat16

D_MODEL = 1024
SSD_WIDTH = 1024
ATT_WIDTH = 1024
N_HEADS = 16
HEAD_DIM = 64
N_GROUPS = 2
D_STATE = 128
CONV_CH = 1536
CONV_WIDTH = 4
CHUNK = 128
PLE_DIM = 256
D_INNER = 2048
EPS = 1e-6
IN_COLS = 6688
N_CHIPS = 4
N_DEV = 8
LANES = 128
N_PAIRS = 8

ADAM_LR = 0.001
ADAM_B1 = 0.9
ADAM_B2 = 0.999
ADAM_EPS = 1e-08
ADAM_WD = 0.01
ADAM_STEP = 10

SMALL_ROWS = 96

NEG_BIG = -1e30
VMEM_LIMIT = 56 * 1024 * 1024

MESH = pl.DeviceIdType.MESH
ANY = pl.BlockSpec(memory_space=pl.ANY)


def _mm(a, b):
    return jnp.dot(a, b, preferred_element_type=F32)


def _mm_nt(a, b):
    return lax.dot_general(a, b, (((1,), (1,)), ((), ())), preferred_element_type=F32)


def _mm_tn(a, b):
    return lax.dot_general(a, b, (((0,), (0,)), ((), ())), preferred_element_type=F32)


def _mm_exact(a, b):
    return jnp.dot(a, b, preferred_element_type=F32, precision=lax.Precision.HIGHEST)


def _softplus(x):
    return jnp.maximum(x, 0.0) + jnp.log1p(jnp.exp(-jnp.abs(x)))


def _sigmoid(x):
    return jax.nn.sigmoid(x)


def _iota(shape, dim):
    return lax.broadcasted_iota(jnp.int32, shape, dim)


def _params(sem=None):
    return pltpu.CompilerParams(dimension_semantics=sem, vmem_limit_bytes=VMEM_LIMIT)


def _blk(n, pref):
    return min(n, pref)


def _const_spec(shape):
    nd = len(shape)
    return pl.BlockSpec(shape, lambda *_: (0,) * nd)


def _chip_peers():
    x, y, c = lax.axis_index("x"), lax.axis_index("y"), lax.axis_index("c")
    return x, y, c, [(1 - x, y, c), (x, 1 - y, c), (1 - x, 1 - y, c)]


def _half(rows, c):
    h = rows // 2
    return pl.ds(pl.multiple_of(c * h, 8), h)


def _sems(n):
    return [pltpu.SemaphoreType.DMA((n,)), pltpu.SemaphoreType.DMA((n,))]


def gather_copies(ins, outs, ssem1, rsem1, ssem2, rsem2):
    n = len(ins)
    x, y, c, peers = _chip_peers()
    me = 2 * x + y
    fetched, passed = [], []
    for k, peer in enumerate(peers):
        chip = 2 * peer[0] + peer[1]
        for i in range(n):
            h = _half(ins[i].shape[0], c)
            fetched.append(pltpu.make_async_remote_copy(
                src_ref=ins[i].at[h], dst_ref=outs[i].at[me, h], send_sem=ssem1.at[n * k + i],
                recv_sem=rsem1.at[n * k + i], device_id=peer, device_id_type=MESH))
            passed.append(pltpu.make_async_remote_copy(
                src_ref=outs[i].at[chip, h], dst_ref=outs[i].at[chip, h], send_sem=ssem2.at[n * k + i],
                recv_sem=rsem2.at[n * k + i], device_id=(x, y, 1 - c), device_id_type=MESH))
    return fetched, passed


def gather_weights(shards, conv_s):
    n = len(shards)

    def body(*refs):
        ins, conv_in = refs[:n], refs[n]
        outs, conv_out = refs[n + 1:2 * n + 1], refs[2 * n + 1]
        ssem1, rsem1, ssem2, rsem2, c_ssem, c_rsem = refs[2 * n + 2:]
        x, y, _, peers = _chip_peers()
        fetched, passed = gather_copies(ins, outs, ssem1, rsem1, ssem2, rsem2)
        small = [pltpu.make_async_remote_copy(
            src_ref=conv_in, dst_ref=conv_out.at[2 * x + y], send_sem=c_ssem.at[k], recv_sem=c_rsem.at[k],
            device_id=peer, device_id_type=MESH) for k, peer in enumerate(peers)]
        for cp in fetched + small:
            cp.start()
        for landed, onward in zip(fetched, passed):
            landed.wait_recv()
            onward.start()
        for cp in passed:
            cp.wait_recv()
        for cp in fetched + passed:
            cp.wait_send()
        for cp in small:
            cp.wait()

    return pl.pallas_call(
        body, name="gather_weights",
        out_shape=tuple(jax.ShapeDtypeStruct((N_CHIPS,) + a.shape, a.dtype) for a in list(shards) + [conv_s]),
        in_specs=[ANY] * (n + 1), out_specs=(ANY,) * (n + 1),
        scratch_shapes=_sems(3 * n) + _sems(3 * n) + _sems(3),
    )(*shards, conv_s)


def halves_to_sibling(gs):
    n = len(gs)

    def body(*refs):
        ins, outs = refs[:n], refs[n:2 * n]
        ssem, rsem = refs[2 * n:]
        x, y, c = lax.axis_index("x"), lax.axis_index("y"), lax.axis_index("c")
        copies = []
        for i in range(n):
            for j in range(N_CHIPS):
                copies.append(pltpu.make_async_remote_copy(
                    src_ref=ins[i].at[j, _half(ins[i].shape[1], 1 - c)], dst_ref=outs[i].at[j],
                    send_sem=ssem.at[N_CHIPS * i + j], recv_sem=rsem.at[N_CHIPS * i + j],
                    device_id=(x, y, 1 - c), device_id_type=MESH))
        for cp in copies:
            cp.start()
        for cp in copies:
            cp.wait()

    return pl.pallas_call(
        body, name="halves_to_sibling",
        out_shape=tuple(jax.ShapeDtypeStruct((N_CHIPS, g.shape[1] // 2, g.shape[2]), F32) for g in gs),
        in_specs=[ANY] * n, out_specs=(ANY,) * n, scratch_shapes=_sems(N_CHIPS * n),
    )(*gs)


RED_GRID = 8
ADD_GRID = 2


def add_halves(core, gs, rbs):
    n = len(gs)

    def body(c_ref, *refs):
        for i in range(n):
            refs[2 * n + i][...] = (refs[i][...] + refs[n + i][...]).astype(BF16)

    def blk(g):
        return (1, g.shape[1] // 2 // ADD_GRID, g.shape[2])

    grid_spec = pltpu.PrefetchScalarGridSpec(
        num_scalar_prefetch=1, grid=(N_CHIPS, ADD_GRID),
        in_specs=([pl.BlockSpec(blk(g), lambda j, b, c_ref: (j, c_ref[0] * ADD_GRID + b, 0)) for g in gs]
                  + [pl.BlockSpec(blk(g), lambda j, b, c_ref: (j, b, 0)) for g in gs]),
        out_specs=[pl.BlockSpec(blk(g), lambda j, b, c_ref: (j, b, 0)) for g in gs])
    return pl.pallas_call(
        body, name="add_halves", grid_spec=grid_spec,
        out_shape=tuple(jax.ShapeDtypeStruct(r.shape, BF16) for r in rbs),
        compiler_params=_params(("parallel", "parallel")),
    )(core, *gs, *rbs)


def scatter_copies(ins, outs, ssem, rsem, lsem):
    n = len(ins)
    x, y, _, peers = _chip_peers()
    me = 2 * x + y
    copies = [pltpu.make_async_copy(ins[i].at[me], outs[i].at[me], lsem.at[i]) for i in range(n)]
    for k, peer in enumerate(peers):
        dst_chip = 2 * peer[0] + peer[1]
        for i in range(n):
            copies.append(pltpu.make_async_remote_copy(
                src_ref=ins[i].at[dst_chip], dst_ref=outs[i].at[me], send_sem=ssem.at[n * k + i],
                recv_sem=rsem.at[n * k + i], device_id=peer, device_id_type=MESH))
    return copies


def sum_parts(parts):
    n = len(parts)

    def body(*refs):
        for i in range(n):
            p_ref = refs[i]
            refs[n + i][...] = ((p_ref[0].astype(F32) + p_ref[1].astype(F32)) + p_ref[2].astype(F32)
                                ) + p_ref[3].astype(F32)

    def rows(p):
        return p.shape[1] // RED_GRID

    return pl.pallas_call(
        body, name="sum_parts",
        out_shape=tuple(jax.ShapeDtypeStruct(p.shape[1:], F32) for p in parts),
        grid=(RED_GRID,),
        in_specs=[pl.BlockSpec((N_CHIPS, rows(p), p.shape[2]), lambda b: (0, b, 0)) for p in parts],
        out_specs=tuple(pl.BlockSpec((rows(p), p.shape[2]), lambda b: (b, 0)) for p in parts),
        compiler_params=_params(("parallel",)),
    )(*parts)


def swap_halves(reds, small):
    n = len(reds)

    def body(*refs):
        ins, s_ref = refs[:n], refs[n]
        outs, smalls_ref = refs[n + 1:2 * n + 1], refs[2 * n + 1]
        ssem, rsem, s_ssem, s_rsem, lsem = refs[2 * n + 2:]
        x, y, c = lax.axis_index("x"), lax.axis_index("y"), lax.axis_index("c")
        dev = 4 * x + 2 * y + c
        copies = [pltpu.make_async_remote_copy(
            src_ref=ins[i], dst_ref=outs[i], send_sem=ssem.at[i], recv_sem=rsem.at[i],
            device_id=(x, y, 1 - c), device_id_type=MESH) for i in range(n)]
        copies.append(pltpu.make_async_copy(s_ref, smalls_ref.at[dev], lsem))
        for k in range(1, N_DEV):
            fx, fy, fc = (k >> 2) & 1, (k >> 1) & 1, k & 1
            peer = ((1 - x) if fx else x, (1 - y) if fy else y, (1 - c) if fc else c)
            copies.append(pltpu.make_async_remote_copy(
                src_ref=s_ref, dst_ref=smalls_ref.at[dev], send_sem=s_ssem.at[k - 1], recv_sem=s_rsem.at[k - 1],
                device_id=peer, device_id_type=MESH))
        for cp in copies:
            cp.start()
        for cp in copies:
            cp.wait()

    return pl.pallas_call(
        body, name="swap_halves",
        out_shape=tuple([jax.ShapeDtypeStruct(r.shape, F32) for r in reds]
                        + [jax.ShapeDtypeStruct((N_DEV,) + small.shape, F32)]),
        in_specs=[ANY] * (n + 1), out_specs=(ANY,) * (n + 1),
        scratch_shapes=_sems(n) + _sems(N_DEV - 1) + [pltpu.SemaphoreType.DMA],
    )(*reds, small)


def _adamw(w, g, m, v):
    m = ADAM_B1 * m + (1.0 - ADAM_B1) * g
    v = ADAM_B2 * v + (1.0 - ADAM_B2) * (g * g)
    m_hat = m / (1.0 - ADAM_B1 ** ADAM_STEP)
    v_hat = v / (1.0 - ADAM_B2 ** ADAM_STEP)
    delta = -ADAM_LR * (m_hat / (jnp.sqrt(v_hat) + ADAM_EPS) + ADAM_WD * w)
    return delta, m, v


def adamw_big(core, mine, theirs, ws, ms, vs):
    n = len(ws)
    per_half = RED_GRID // 2

    def body(c_ref, *refs):
        own = (pl.program_id(0) // per_half) == c_ref[0]
        for i in range(n):
            g = jnp.where(own, refs[i][...], refs[n + i][...])
            d, mn, vn = _adamw(refs[2 * n + i][...], g, refs[3 * n + i][...], refs[4 * n + i][...])
            refs[5 * n + i][...] = g
            refs[6 * n + i][...] = d
            refs[7 * n + i][...] = mn
            refs[8 * n + i][...] = vn

    def blk(w):
        return (w.shape[0] // RED_GRID, w.shape[1])

    halves = [pl.BlockSpec(blk(w), lambda b, c_ref: (b % per_half, 0)) for w in ws]
    whole = [pl.BlockSpec(blk(w), lambda b, c_ref: (b, 0)) for w in ws]
    shapes = [jax.ShapeDtypeStruct(w.shape, F32) for w in ws]
    grid_spec = pltpu.PrefetchScalarGridSpec(
        num_scalar_prefetch=1, grid=(RED_GRID,), in_specs=halves * 2 + whole * 3, out_specs=whole * 4)
    outs = pl.pallas_call(
        body, name="adamw_big", out_shape=tuple(shapes * 4), grid_spec=grid_spec,
        compiler_params=_params(("parallel",)),
    )(core, *mine, *theirs, *ws, *ms, *vs)
    return outs[:n], outs[n:2 * n], outs[2 * n:3 * n], outs[3 * n:]


def adamw_whole(g, w, m, v, name):
    def body(g_ref, w_ref, m_ref, v_ref, d_out, m_out, v_out):
        d, mn, vn = _adamw(w_ref[...], g_ref[...], m_ref[...], v_ref[...])
        d_out[...] = d
        m_out[...] = mn
        v_out[...] = vn

    shp = jax.ShapeDtypeStruct(g.shape, F32)
    return pl.pallas_call(body, name=name, out_shape=(shp,) * 3)(g, w, m, v)


def adamw_small(smalls, w, m, v):
    def body(s_ref, w_ref, m_ref, v_ref, g_out, d_out, m_out, v_out):
        g = s_ref[0]
        for k in range(1, N_DEV):
            g = g + s_ref[k]
        d, mn, vn = _adamw(w_ref[...], g, m_ref[...], v_ref[...])
        g_out[...] = g
        d_out[...] = d
        m_out[...] = mn
        v_out[...] = vn

    shp = jax.ShapeDtypeStruct((SMALL_ROWS, LANES), F32)
    return pl.pallas_call(body, name="adamw_small", out_shape=(shp,) * 4)(smalls, w, m, v)


def in_proj_fwd(x, g, ws):
    s = x.shape[0]
    tm = _blk(s, 512)
    n = len(ws)

    def body(x_ref, g_ref, *refs):
        xv = x_ref[...]
        r = lax.rsqrt(jnp.mean(xv * xv, axis=-1, keepdims=True) + EPS)
        u = (xv * r * g_ref[...]).astype(BF16)
        refs[n][...] = u
        for i in range(n):
            refs[n + 1 + i][...] = _mm(u, refs[i][...])

    rows = lambda width: pl.BlockSpec((tm, width), lambda i: (i, 0))
    return pl.pallas_call(
        body, name="in_proj_fwd",
        out_shape=tuple([jax.ShapeDtypeStruct((s, D_MODEL), BF16)]
                        + [jax.ShapeDtypeStruct((s, w.shape[1]), F32) for w in ws]),
        grid=(s // tm,),
        in_specs=[rows(D_MODEL), _const_spec((1, D_MODEL))] + [_const_spec(w.shape) for w in ws],
        out_specs=tuple([rows(D_MODEL)] + [rows(w.shape[1]) for w in ws]),
        compiler_params=_params(("parallel",)),
    )(x, g, *ws)


def matmul_tn(a, b, name):
    s, m = a.shape
    n = b.shape[1]
    tk = _blk(s, 2048)
    tn = _blk(n, 512) if m > D_MODEL else (n // 2 if n > D_MODEL else n)

    def body(a_ref, b_ref, o_ref):
        @pl.when(pl.program_id(1) == 0)
        def _():
            o_ref[...] = jnp.zeros_like(o_ref)

        o_ref[...] += _mm_tn(a_ref[...], b_ref[...])

    return pl.pallas_call(
        body, name=name, out_shape=jax.ShapeDtypeStruct((m, n), F32), grid=(n // tn, s // tk),
        in_specs=[pl.BlockSpec((tk, m), lambda j, i: (i, 0)), pl.BlockSpec((tk, tn), lambda j, i: (i, j))],
        out_specs=pl.BlockSpec((m, tn), lambda j, i: (0, j)),
        compiler_params=_params(("parallel", "arbitrary")),
    )(a, b)


def conv_fwd(xbc, w, b):
    s = xbc.shape[0]
    tm = _blk(s, 256)

    def body(x_ref, t_ref, w_ref, b_ref, pre_ref, act_ref):
        i = pl.program_id(0)
        row8 = _iota((8, LANES), 0)
        for c0 in range(0, CONV_CH, LANES):
            cols = slice(c0, c0 + LANES)
            cur = x_ref[:, cols]
            tail = jnp.where(i > 0, t_ref[:, cols], 0.0)
            wv = w_ref[:, cols]
            bias = b_ref[:, cols]
            acc = cur * wv[3:4, :] + bias
            head = cur[0:8, :] * wv[3:4, :] + bias
            for sh in range(1, CONV_WIDTH):
                wk = wv[3 - sh:4 - sh, :]
                acc = acc + pltpu.roll(cur, sh, 0) * wk
                first = jnp.where(row8 < sh, pltpu.roll(tail, sh, 0), pltpu.roll(cur[0:8, :], sh, 0))
                head = head + first * wk
            pre_ref[:, cols] = acc
            act_ref[:, cols] = acc * _sigmoid(acc)
            pre_ref[0:8, cols] = head
            act_ref[0:8, cols] = head * _sigmoid(head)

    shp = jax.ShapeDtypeStruct(xbc.shape, F32)
    rows = pl.BlockSpec((tm, CONV_CH), lambda i: (i, 0))
    return pl.pallas_call(
        body, name="conv_fwd", out_shape=(shp, shp), grid=(s // tm,),
        in_specs=[rows, pl.BlockSpec((8, CONV_CH), lambda i: (jnp.maximum(i * (tm // 8) - 1, 0), 0)),
                  _const_spec((CONV_WIDTH, CONV_CH)), _const_spec((1, CONV_CH))],
        out_specs=(rows, rows), compiler_params=_params(("parallel",)),
    )(xbc, xbc, w, b)


def conv_bwd(xbc, pre, dact, w):
    s = xbc.shape[0]
    tm = _blk(s, 256)
    nb = s // tm

    def dsilu(p):
        sg = _sigmoid(p)
        return sg * (1.0 + p * (1.0 - sg))

    def body(x_ref, xt_ref, p_ref, pn_ref, d_ref, dn_ref, w_ref, dx_ref, dw_ref, db_ref):
        i = pl.program_id(0)

        @pl.when(i == 0)
        def _():
            dw_ref[...] = jnp.zeros_like(dw_ref)
            db_ref[...] = jnp.zeros_like(db_ref)

        row8 = _iota((8, LANES), 0)
        for c0 in range(0, CONV_CH, LANES):
            cols = slice(c0, c0 + LANES)
            wv = w_ref[:, cols]
            dpre = d_ref[:, cols] * dsilu(p_ref[:, cols])
            dnext = jnp.where(i < nb - 1, dn_ref[:, cols] * dsilu(pn_ref[:, cols]), 0.0)
            cur = x_ref[:, cols]
            tail = jnp.where(i > 0, xt_ref[:, cols], 0.0)
            dx = dpre * wv[3:4, :]
            last = dpre[tm - 8:tm, :] * wv[3:4, :]
            db_ref[:, cols] += jnp.sum(dpre, axis=0, keepdims=True)
            dws = [jnp.sum(dpre * cur, axis=0, keepdims=True)]
            for sh in range(1, CONV_WIDTH):
                wk = wv[3 - sh:4 - sh, :]
                dx = dx + pltpu.roll(dpre, tm - sh, 0) * wk
                nxt = jnp.where(row8 >= 8 - sh, pltpu.roll(dnext, 8 - sh, 0),
                                pltpu.roll(dpre[tm - 8:tm, :], 8 - sh, 0))
                last = last + nxt * wk
                xs = pltpu.roll(cur, sh, 0)
                first = jnp.where(row8 < sh, pltpu.roll(tail, sh, 0), xs[0:8, :])
                dws.append(jnp.sum(dpre * xs, axis=0, keepdims=True)
                           + jnp.sum(dpre[0:8, :] * (first - xs[0:8, :]), axis=0, keepdims=True))
            dx_ref[:, cols] = dx.astype(BF16)
            dx_ref[tm - 8:tm, cols] = last.astype(BF16)
            for sh in range(CONV_WIDTH):
                dw_ref[3 - sh:4 - sh, cols] += dws[sh]

    rows = pl.BlockSpec((tm, CONV_CH), lambda i: (i, 0))
    prev8 = pl.BlockSpec((8, CONV_CH), lambda i: (jnp.maximum(i * (tm // 8) - 1, 0), 0))
    next8 = pl.BlockSpec((8, CONV_CH), lambda i: (jnp.minimum((i + 1) * (tm // 8), s // 8 - 1), 0))
    return pl.pallas_call(
        body, name="conv_bwd",
        out_shape=(jax.ShapeDtypeStruct(xbc.shape, BF16), jax.ShapeDtypeStruct((8, CONV_CH), F32),
                   jax.ShapeDtypeStruct((1, CONV_CH), F32)),
        grid=(nb,),
        in_specs=[rows, prev8, rows, next8, rows, next8, _const_spec((CONV_WIDTH, CONV_CH))],
        out_specs=(rows, _const_spec((8, CONV_CH)), _const_spec((1, CONV_CH))),
        compiler_params=_params(("arbitrary",)),
    )(xbc, xbc, pre, pre, dact, dact, w)


def _pair_lanes(mat, j, lane):
    return jnp.where(lane < HEAD_DIM, mat[:, 2 * j:2 * j + 1], mat[:, 2 * j + 1:2 * j + 2])


def _ssd_chunk_prelude(sm, dtb, a_row, lane, sub):
    raw = sm + dtb
    head_lane = lane < N_HEADS
    dt = jnp.where(head_lane, _softplus(raw), 0.0)
    sig = jnp.where(head_lane, _sigmoid(raw), 0.0)
    tri = (lane <= sub).astype(F32)
    acs = _mm_exact(tri, dt * a_row)
    return dt, sig, acs, acs.T


GROUP_WIDTH = SSD_WIDTH // N_GROUPS
HEADS_PER_GROUP = N_HEADS // N_GROUPS


def _expand_group(mat, g, lane):
    return jnp.concatenate([_pair_lanes(mat, j, lane) for j in range(4 * g, 4 * g + 4)], axis=1)


def _head_sums(q, g):
    row = _iota((GROUP_WIDTH, LANES), 0)
    seg = (_iota((GROUP_WIDTH, LANES), 1) == HEADS_PER_GROUP * g + (row >> 6)).astype(BF16)
    hi = q.astype(BF16)
    lo = (q - hi.astype(F32)).astype(BF16)
    return _mm(hi, seg) + _mm(lo, seg)


def _rows_from_lanes(row512):
    return jnp.broadcast_to(row512, (LANES, GROUP_WIDTH)).T


def ssd_fwd(xc, small, dtb_row, a_row, dskip_lane):
    s = xc.shape[0]
    nc = s // CHUNK

    def body(xc_ref, sm_ref, dtb_ref, a_ref, dsk_ref, y_ref, hs_ref, h_scr):
        c = pl.program_id(0)

        @pl.when(c == 0)
        def _():
            h_scr[...] = jnp.zeros_like(h_scr)

        lane = _iota((CHUNK, LANES), 1)
        sub = _iota((CHUNK, LANES), 0)
        causal = lane <= sub
        dt, _, acs, acs_t = _ssd_chunk_prelude(sm_ref[...], dtb_ref[...], a_ref[...], lane, sub)
        for g in range(N_GROUPS):
            cols = slice(GROUP_WIDTH * g, GROUP_WIDTH * (g + 1))
            b_off = SSD_WIDTH + D_STATE * g
            c_off = SSD_WIDTH + N_GROUPS * D_STATE + D_STATE * g
            b_b = xc_ref[:, b_off:b_off + D_STATE].astype(BF16)
            c_b = xc_ref[:, c_off:c_off + D_STATE].astype(BF16)
            cb = _mm_nt(c_b, b_b)
            x_g = xc_ref[:, cols]
            acs_g = _expand_group(acs, g, lane)
            xdt_g = x_g * _expand_group(dt, g, lane)
            xdt_b = xdt_g.astype(BF16)
            heads = range(HEADS_PER_GROUP * g, HEADS_PER_GROUP * (g + 1))
            m_b = [(cb * jnp.exp(jnp.where(causal, acs[:, h:h + 1] - acs_t[h:h + 1, :], NEG_BIG))).astype(BF16)
                   for h in heads]
            yd = [_mm(m_b[k], xdt_b[:, LANES * (k // 2):LANES * (k // 2 + 1)]) for k in range(HEADS_PER_GROUP)]
            yd_g = jnp.concatenate([jnp.where(lane < HEAD_DIM, yd[2 * k], yd[2 * k + 1]) for k in range(4)], axis=1)
            h_g = h_scr[g]
            t_g = _mm_nt(c_b, h_g.astype(BF16))
            y_ref[:, cols] = yd_g + jnp.exp(acs_g) * t_g + dsk_ref[:, cols] * x_g
            hs_ref[0, g] = h_g
            last_g = acs_g[CHUNK - 1:CHUNK, :]
            w_b = (xdt_g * jnp.exp(last_g - acs_g)).astype(BF16)
            h_scr[g] = h_g * jnp.exp(_rows_from_lanes(last_g)) + _mm_tn(w_b, b_b)

    return pl.pallas_call(
        body, name="ssd_fwd",
        out_shape=(jax.ShapeDtypeStruct((s, SSD_WIDTH), F32),
                   jax.ShapeDtypeStruct((nc, N_GROUPS, GROUP_WIDTH, D_STATE), F32)),
        grid=(nc,),
        in_specs=[pl.BlockSpec((CHUNK, CONV_CH), lambda c: (c, 0)), pl.BlockSpec((CHUNK, LANES), lambda c: (c, 0)),
                  _const_spec((1, LANES)), _const_spec((1, LANES)), _const_spec((1, SSD_WIDTH))],
        out_specs=(pl.BlockSpec((CHUNK, SSD_WIDTH), lambda c: (c, 0)),
                   pl.BlockSpec((1, N_GROUPS, GROUP_WIDTH, D_STATE), lambda c: (c, 0, 0, 0))),
        scratch_shapes=[pltpu.VMEM((N_GROUPS, GROUP_WIDTH, D_STATE), F32)],
        compiler_params=_params(("arbitrary",)),
    )(xc, small, dtb_row, a_row, dskip_lane)


def ssd_bwd(xc, small, states, dy, dtb_row, a_row, dskip_lane):
    s = xc.shape[0]
    nc = s // CHUNK
    rev = lambda c: nc - 1 - c

    def body(xc_ref, sm_ref, hs_ref, dy_ref, dtb_ref, a_ref, dsk_ref,
             dxc_ref, ddt_ref, da_ref, ddtb_ref, ddsk_ref, dh_scr):
        c = pl.program_id(0)

        @pl.when(c == 0)
        def _():
            dh_scr[...] = jnp.zeros_like(dh_scr)
            da_ref[...] = jnp.zeros_like(da_ref)
            ddtb_ref[...] = jnp.zeros_like(ddtb_ref)
            ddsk_ref[...] = jnp.zeros_like(ddsk_ref)

        lane = _iota((CHUNK, LANES), 1)
        sub = _iota((CHUNK, LANES), 0)
        causal = lane <= sub
        upper = lane >= sub
        is_last = sub == CHUNK - 1
        a_row_v = a_ref[...]
        dt, sig, acs, acs_t = _ssd_chunk_prelude(sm_ref[...], dtb_ref[...], a_row_v, lane, sub)
        cd = jnp.exp(acs[CHUNK - 1:CHUNK, :])
        dacs_c = jnp.zeros((CHUNK, LANES), F32)
        dacs_r = jnp.zeros((LANES, CHUNK), F32)
        ddtx = jnp.zeros((CHUNK, LANES), F32)
        for g in range(N_GROUPS):
            cols = slice(GROUP_WIDTH * g, GROUP_WIDTH * (g + 1))
            b_off = SSD_WIDTH + D_STATE * g
            c_off = SSD_WIDTH + N_GROUPS * D_STATE + D_STATE * g
            b_b = xc_ref[:, b_off:b_off + D_STATE].astype(BF16)
            c_b = xc_ref[:, c_off:c_off + D_STATE].astype(BF16)
            cb = _mm_nt(c_b, b_b)
            cb_t = _mm_nt(b_b, c_b)
            x_g = xc_ref[:, cols]
            dy_g = dy_ref[:, cols]
            dt_g = _expand_group(dt, g, lane)
            acs_g = _expand_group(acs, g, lane)
            last_g = acs_g[CHUNK - 1:CHUNK, :]
            e_g = jnp.exp(acs_g)
            dte_g = jnp.exp(last_g - acs_g)
            xdt_g = x_g * dt_g
            xdt_b = xdt_g.astype(BF16)
            h_g = hs_ref[0, g]
            dh_g = dh_scr[g]
            h_b = h_g.astype(BF16)
            dh_b = dh_g.astype(BF16)
            heads = list(range(HEADS_PER_GROUP * g, HEADS_PER_GROUP * (g + 1)))
            segs = [acs[:, h:h + 1] - acs_t[h:h + 1, :] for h in heads]
            lms = [jnp.exp(jnp.where(causal, sg, NEG_BIG)) for sg in segs]
            mts = [(cb_t * jnp.exp(jnp.where(upper, -sg, NEG_BIG))).astype(BF16) for sg in segs]
            dyh = []
            for k in range(HEADS_PER_GROUP):
                blk = dy_g[:, LANES * (k // 2):LANES * (k // 2 + 1)]
                in_head = (lane < HEAD_DIM) if k % 2 == 0 else (lane >= HEAD_DIM)
                dyh.append(jnp.where(in_head, blk, 0.0).astype(BF16))
            dms = [_mm_nt(dyh[k], xdt_b[:, LANES * (k // 2):LANES * (k // 2 + 1)]) for k in range(HEADS_PER_GROUP)]
            dxs = [_mm(mts[k], dyh[k]) for k in range(HEADS_PER_GROUP)]
            dcb = jnp.zeros((CHUNK, CHUNK), F32)
            for k, h in enumerate(heads):
                gmat = dms[k] * (cb * lms[k])
                dacs_c = dacs_c + jnp.where(lane == h, jnp.sum(gmat, axis=1, keepdims=True), 0.0)
                dacs_r = dacs_r - jnp.where(sub == h, jnp.sum(gmat, axis=0, keepdims=True), 0.0)
                dcb = dcb + dms[k] * lms[k]
            dxdt_g = jnp.concatenate([dxs[2 * k] + dxs[2 * k + 1] for k in range(4)], axis=1)
            t_g = _mm_nt(c_b, h_b)
            dacs_c = dacs_c + _head_sums(dy_g * e_g * t_g, g)
            dt_b = (dy_g * e_g).astype(BF16)
            dc_acc = _mm(dt_b, h_b)
            dh_prev = _mm_tn(dt_b, c_b)
            dw_g = _mm_nt(b_b, dh_b)
            w_g = xdt_g * dte_g
            dxdt_g = dxdt_g + dw_g * dte_g
            db_acc = _mm(w_g.astype(BF16), dh_b)
            r2 = _head_sums(dw_g * w_g, g)
            dacs_c = dacs_c + jnp.where(is_last, jnp.sum(r2, axis=0, keepdims=True), 0.0) - r2
            q3 = jnp.sum(dh_g * h_g, axis=1, keepdims=True)
            for k, h in enumerate(heads):
                tot = jnp.sum(q3[HEAD_DIM * k:HEAD_DIM * (k + 1), :], keepdims=True) * cd[:, h:h + 1]
                dacs_c = dacs_c + jnp.where(is_last & (lane == h), tot, 0.0)
            dh_scr[g] = dh_prev + dh_g * jnp.exp(_rows_from_lanes(last_g))
            dxc_ref[:, cols] = dxdt_g * dt_g + dsk_ref[:, cols] * dy_g
            ddtx = ddtx + _head_sums(dxdt_g * x_g, g)
            ddsk_ref[:, cols] += jnp.sum(dy_g * x_g, axis=0, keepdims=True)
            dxc_ref[:, b_off:b_off + D_STATE] = db_acc + _mm(dcb.T.astype(BF16), c_b)
            dxc_ref[:, c_off:c_off + D_STATE] = dc_acc + _mm(dcb.astype(BF16), b_b)
        dacs = dacs_c + dacs_r.T
        dadt = _mm_exact((lane >= sub).astype(F32), dacs)
        ddt = dadt * a_row_v + ddtx
        ddt_raw = ddt * sig
        ddt_ref[...] = ddt_raw
        da_ref[...] += jnp.sum(dadt * dt, axis=0, keepdims=True)
        ddtb_ref[...] += jnp.sum(ddt_raw, axis=0, keepdims=True)

    return pl.pallas_call(
        body, name="ssd_bwd",
        out_shape=(jax.ShapeDtypeStruct((s, CONV_CH), F32), jax.ShapeDtypeStruct((s, LANES), F32),
                   jax.ShapeDtypeStruct((1, LANES), F32), jax.ShapeDtypeStruct((1, LANES), F32),
                   jax.ShapeDtypeStruct((1, SSD_WIDTH), F32)),
        grid=(nc,),
        in_specs=[pl.BlockSpec((CHUNK, CONV_CH), lambda c: (rev(c), 0)),
                  pl.BlockSpec((CHUNK, LANES), lambda c: (rev(c), 0)),
                  pl.BlockSpec((1, N_GROUPS, GROUP_WIDTH, D_STATE), lambda c: (rev(c), 0, 0, 0)),
                  pl.BlockSpec((CHUNK, SSD_WIDTH), lambda c: (rev(c), 0)),
                  _const_spec((1, LANES)), _const_spec((1, LANES)), _const_spec((1, SSD_WIDTH))],
        out_specs=(pl.BlockSpec((CHUNK, CONV_CH), lambda c: (rev(c), 0)),
                   pl.BlockSpec((CHUNK, LANES), lambda c: (rev(c), 0)),
                   _const_spec((1, LANES)), _const_spec((1, LANES)), _const_spec((1, SSD_WIDTH))),
        scratch_shapes=[pltpu.VMEM((N_GROUPS, GROUP_WIDTH, D_STATE), F32)],
        compiler_params=_params(("arbitrary",)),
    )(xc, small, states, dy, dtb_row, a_row, dskip_lane)


FORGET_BLOCK = 512


def forget_cumsum(small, fgb_row):
    s = small.shape[0]
    t = _blk(s, FORGET_BLOCK)
    nb = s // t

    def body(sm_ref, b_ref, cc_ref, carry):
        i = pl.program_id(0)

        @pl.when(i == 0)
        def _():
            carry[...] = jnp.zeros_like(carry)

        lane = _iota((t, LANES), 1)
        in_f = (lane >= N_HEADS) & (lane < 2 * N_HEADS)
        logf = jnp.where(in_f, -_softplus(-(sm_ref[...] + b_ref[...])), 0.0)
        tri = (_iota((t, t), 1) <= _iota((t, t), 0)).astype(F32)
        cum = _mm_exact(tri, logf) + carry[0:1, :]
        cc_ref[...] = cum
        carry[...] = jnp.broadcast_to(cum[t - 1:t, :], (8, LANES))

    return pl.pallas_call(
        body, name="forget_cumsum",
        out_shape=jax.ShapeDtypeStruct((s, LANES), F32),
        grid=(nb,),
        in_specs=[pl.BlockSpec((t, LANES), lambda i: (i, 0)), _const_spec((1, LANES))],
        out_specs=pl.BlockSpec((t, LANES), lambda i: (i, 0)),
        scratch_shapes=[pltpu.VMEM((8, LANES), F32)],
        compiler_params=_params(("arbitrary",)),
    )(small, fgb_row)


def forget_bwd(dc, small, ddt_raw, fgb_row):
    s = small.shape[0]
    t = _blk(s, FORGET_BLOCK)
    nb = s // t
    rev = lambda i: nb - 1 - i

    def body(dc_ref, sm_ref, ddt_ref, b_ref, ds_ref, dfb_ref, carry):
        i = pl.program_id(0)

        @pl.when(i == 0)
        def _():
            carry[...] = jnp.zeros_like(carry)
            dfb_ref[...] = jnp.zeros_like(dfb_ref)

        lane = _iota((t, LANES), 1)
        rows = dc_ref[...].T
        tri = (_iota((t, t), 1) <= _iota((t, t), 0)).astype(F32)
        rc = _mm_exact(rows, tri) + carry[:, 0:1]
        carry[...] = jnp.broadcast_to(rc[:, 0:1], (LANES, LANES))
        in_f = (lane >= N_HEADS) & (lane < 2 * N_HEADS)
        df = jnp.where(in_f, rc.T * _sigmoid(-(sm_ref[...] + b_ref[...])), 0.0)
        ds_ref[...] = (df + ddt_ref[...]).astype(BF16)
        dfb_ref[...] += jnp.sum(df, axis=0, keepdims=True)

    blk = pl.BlockSpec((t, LANES), lambda i: (rev(i), 0))
    return pl.pallas_call(
        body, name="forget_bwd",
        out_shape=(jax.ShapeDtypeStruct((s, LANES), BF16), jax.ShapeDtypeStruct((1, LANES), F32)),
        grid=(nb,),
        in_specs=[blk, blk, blk, _const_spec((1, LANES))],
        out_specs=(blk, _const_spec((1, LANES))),
        scratch_shapes=[pltpu.VMEM((LANES, LANES), F32)],
        compiler_params=_params(("arbitrary",)),
    )(dc, small, ddt_raw, fgb_row)


ATT_BLOCK = 1024
ATT_BLOCK_BWD = 512
ATT_BLOCK_BWD_Q = 512
ATT_SCALE = HEAD_DIM ** -0.5
AUG_A = HEAD_DIM
AUG_B = HEAD_DIM + 3


def _split3(c):
    hi = c.astype(BF16).astype(F32)
    r = c - hi
    mid = r.astype(BF16).astype(F32)
    return hi, mid, (r - mid).astype(BF16).astype(F32)


def _aug(lane, first, parts=None, value=1.0):
    if parts is None:
        return jnp.where((lane >= first) & (lane < first + 3), value, 0.0)
    return (jnp.where(lane == first, parts[0], 0.0) + jnp.where(lane == first + 1, parts[1], 0.0)
            + jnp.where(lane == first + 2, parts[2], 0.0))


def _pack_pair(a0, a1, lane):
    return jnp.where(lane < HEAD_DIM, a0, pltpu.roll(a1, HEAD_DIM, 1))


def proj_qkv_heads(u, w_q, w_k, w_v, cum, later):
    s = u.shape[0]
    tm = _blk(s, 256)
    nsteps = s // tm
    n_later = len(later)

    def body(u_ref, wq_ref, wk_ref, wv_ref, c_ref, *rest):
        later_in = rest[:n_later]
        qa_ref, ka_ref, va_ref, nrm_ref = rest[n_later:n_later + 4]
        later_out = rest[n_later + 4:2 * n_later + 4]
        sems = rest[2 * n_later + 4:]
        step = pl.program_id(0)

        @pl.when(step == 0)
        def _():
            for cp in gather_copies(later_in, later_out, *sems)[0]:
                cp.start()

        @pl.when(step == nsteps // 2)
        def _():
            for landed, onward in zip(*gather_copies(later_in, later_out, *sems)):
                landed.wait_recv()
                onward.start()

        @pl.when(step == nsteps - 1)
        def _():
            fetched, passed = gather_copies(later_in, later_out, *sems)
            for cp in passed:
                cp.wait_recv()
            for cp in fetched + passed:
                cp.wait_send()

        lane = _iota((tm, LANES), 1)
        lo = lane < HEAD_DIM
        uv = u_ref[...]
        qf = _mm(uv, wq_ref[...]) * ATT_SCALE
        kf = _mm(uv, wk_ref[...])
        vf = _mm(uv, wv_ref[...])
        cc = c_ref[...]
        ones_a = _aug(lane, AUG_A)
        ones_b = _aug(lane, AUG_B)
        sub8 = _iota((8, LANES), 0)
        nrm = jnp.zeros((8, LANES), F32)
        for h in range(N_HEADS):
            j, e = divmod(h, 2)

            def head(full):
                blk = full[:, LANES * j:LANES * (j + 1)]
                if e == 1:
                    blk = pltpu.roll(blk, HEAD_DIM, 1)
                return jnp.where(lo, blk, 0.0)

            parts = _split3(cc[:, N_HEADS + h:N_HEADS + h + 1])
            qh, kh = head(qf), head(kf)
            qa_ref[h] = (qh + _aug(lane, AUG_A, parts) + ones_b).astype(BF16)
            ka_ref[h] = (kh + ones_a - _aug(lane, AUG_B, parts)).astype(BF16)
            va_ref[h] = (head(vf) + ones_a).astype(BF16)
        seg = (_iota((ATT_WIDTH, LANES), 1) == (_iota((ATT_WIDTH, LANES), 0) >> 6)).astype(BF16)
        for r, val in enumerate((qf, kf)):
            sq = val * val
            hi = sq.astype(BF16)
            tot = _mm(hi, seg) + _mm((sq - hi.astype(F32)).astype(BF16), seg)
            nrm = nrm + jnp.where(sub8 == r, jnp.max(tot, axis=0, keepdims=True), 0.0)
        nrm_ref[0] = nrm

    shp = jax.ShapeDtypeStruct((N_HEADS, s, LANES), BF16)
    hspec = pl.BlockSpec((N_HEADS, tm, LANES), lambda i: (0, i, 0))
    wspec = _const_spec((D_MODEL, ATT_WIDTH))
    return pl.pallas_call(
        body, name="proj_qkv_heads",
        out_shape=tuple([shp, shp, shp, jax.ShapeDtypeStruct((nsteps, 8, LANES), F32)]
                        + [jax.ShapeDtypeStruct((N_CHIPS,) + a.shape, a.dtype) for a in later]),
        grid=(nsteps,),
        in_specs=[pl.BlockSpec((tm, D_MODEL), lambda i: (i, 0)), wspec, wspec, wspec,
                  pl.BlockSpec((tm, LANES), lambda i: (i, 0))] + [ANY] * n_later,
        out_specs=tuple([hspec, hspec, hspec, pl.BlockSpec((1, 8, LANES), lambda i: (i, 0, 0))]
                        + [ANY] * n_later),
        scratch_shapes=_sems(3 * n_later) + _sems(3 * n_later),
        compiler_params=_params(("arbitrary",)),
    )(u, w_q, w_k, w_v, cum, *later)


SKIP_BELOW = -110.0


def live_blocks(norms, cum, tq, tk):
    qn = jnp.sqrt(jnp.max(norms[:, 0, :N_HEADS], axis=0))
    kn = jnp.sqrt(jnp.max(norms[:, 1, :N_HEADS], axis=0))
    bound = 2.05 * qn * kn + 2.0
    c_first = cum[0::tq, N_HEADS:2 * N_HEADS]
    c_last = cum[tk - 1::tk, N_HEADS:2 * N_HEADS]
    nq, nk = c_first.shape[0], c_last.shape[0]
    top = bound[None, None, :] + c_first[:, None, :] - c_last[None, :, :]
    before = (jnp.arange(nk)[None, :] + 1) * tk <= jnp.arange(nq)[:, None] * tq
    dead = before[:, :, None] & ~(top >= SKIP_BELOW)
    first = jnp.sum(dead, axis=1).astype(jnp.int32).T
    last_q = jnp.sum(first[:, None, :] <= jnp.arange(nk)[None, :, None], axis=2).astype(jnp.int32) - 1
    return first, last_q


def attention_fwd(first, qa, ka, va):
    s = qa.shape[1]
    t = _blk(s, ATT_BLOCK)
    nq = s // t

    def body(first_ref, qa_ref, ka_ref, va_ref, o_ref, qb_ref, m_scr, acc_scr, alpha_scr, p_scr, s_scr):
        qi = pl.program_id(1)
        starts = [first_ref[2 * pl.program_id(0) + e, qi] for e in range(2)]
        k0 = jnp.maximum(starts[0], starts[1])
        m_scr[...] = jnp.full_like(m_scr, NEG_BIG)
        acc_scr[...] = jnp.zeros_like(acc_scr)

        def kv_rows(kb):
            return pl.ds(pl.multiple_of(kb * t, t), t)

        def logits(kb, masked, heads=(0, 1)):
            for e in heads:
                sc = _mm_nt(qa_ref[e], ka_ref[e, kv_rows(kb), :])
                if masked:
                    sc = jnp.where(_iota((t, t), 0) >= _iota((t, t), 1), sc, NEG_BIG)
                s_scr[e] = sc

        def probs(heads=(0, 1)):
            for e in heads:
                cmax = s_scr[e, :, 0:LANES]
                for c in range(1, t // LANES):
                    cmax = jnp.maximum(cmax, s_scr[e, :, LANES * c:LANES * (c + 1)])
                m_old = m_scr[e]
                m_new = jnp.maximum(m_old, jnp.max(cmax, axis=1, keepdims=True))
                alpha_scr[e] = jnp.exp(m_old - m_new)
                m_scr[e] = m_new
                for c in range(t // LANES):
                    cols = slice(LANES * c, LANES * (c + 1))
                    p_scr[e, :, cols] = jnp.exp(s_scr[e, :, cols] - m_new).astype(BF16)

        def accumulate(kb, heads=(0, 1)):
            for e in heads:
                acc_scr[e] = alpha_scr[e] * acc_scr[e] + _mm(p_scr[e], va_ref[e, kv_rows(kb), :])

        for e in range(2):
            def alone(kb, carry, e=e):
                logits(kb, False, (e,))
                probs((e,))
                accumulate(kb, (e,))
                return carry

            lax.fori_loop(starts[e], k0, alone, 0)

        def loop_body(kb, carry):
            logits(kb, False)
            for e in range(2):
                accumulate(kb - 1, (e,))
                probs((e,))
            return carry

        @pl.when(qi > k0)
        def _():
            logits(k0, False)
            probs()

        lax.fori_loop(k0 + 1, qi, loop_body, 0)

        @pl.when(qi > k0)
        def _():
            logits(qi, True)
            accumulate(qi - 1)
            probs()

        @pl.when(qi == k0)
        def _():
            logits(qi, True)
            probs()

        accumulate(qi)

        lane = _iota((t, LANES), 1)
        outs = []
        for e in range(2):
            acc = acc_scr[e]
            l = acc[:, AUG_A:AUG_A + 1]
            outs.append(acc / l)
            lse = m_scr[e][:, 0:1] + jnp.log(l)
            q32 = qa_ref[e].astype(F32)
            c = q32[:, AUG_A:AUG_A + 1] + q32[:, AUG_A + 1:AUG_A + 2] + q32[:, AUG_A + 2:AUG_A + 3]
            qb = jnp.where(lane < HEAD_DIM, q32, 0.0) + _aug(lane, AUG_A, _split3(c - lse)) + _aug(lane, AUG_B)
            qb_ref[e] = qb.astype(BF16)
        o_ref[...] = _pack_pair(outs[0], outs[1], lane)

    grid_spec = pltpu.PrefetchScalarGridSpec(
        num_scalar_prefetch=1, grid=(N_PAIRS, nq),
        in_specs=[pl.BlockSpec((2, t, LANES), lambda j, qi, f: (j, qi, 0)),
                  pl.BlockSpec((2, s, LANES), lambda j, qi, f: (j, 0, 0)),
                  pl.BlockSpec((2, s, LANES), lambda j, qi, f: (j, 0, 0))],
        out_specs=[pl.BlockSpec((t, LANES), lambda j, qi, f: (qi, j)),
                   pl.BlockSpec((2, t, LANES), lambda j, qi, f: (j, qi, 0))],
        scratch_shapes=[pltpu.VMEM((2, t, LANES), F32), pltpu.VMEM((2, t, LANES), F32),
                        pltpu.VMEM((2, t, LANES), F32), pltpu.VMEM((2, t, t), BF16), pltpu.VMEM((2, t, t), F32)])
    return pl.pallas_call(
        body, name="attention_fwd", grid_spec=grid_spec,
        out_shape=(jax.ShapeDtypeStruct((s, ATT_WIDTH), F32), jax.ShapeDtypeStruct((N_HEADS, s, LANES), BF16)),
        compiler_params=_params(("parallel", "parallel")),
    )(first, qa, ka, va)


def attention_bwd(last_q, qb, ka, va, dob):
    s = qb.shape[1]
    t = _blk(s, ATT_BLOCK_BWD)
    tq = _blk(s, ATT_BLOCK_BWD_Q)
    nq = s // tq
    per_q = tq // t

    def body(last_ref, qb_ref, dob_ref, ka_ref, va_ref, dq_ref, dk_ref, dv_ref, dc_ref, dq_scr, dk_scr, dv_scr):
        j, ki = pl.program_id(0), pl.program_id(1)

        @pl.when((j == 0) & (ki == 0))
        def _():
            dc_ref[...] = jnp.zeros_like(dc_ref)

        @pl.when(ki == 0)
        def _():
            dq_scr[...] = jnp.zeros_like(dq_scr)

        dk_scr[...] = jnp.zeros_like(dk_scr)
        dv_scr[...] = jnp.zeros_like(dv_scr)

        def q_step(qblk, masked, heads=(0, 1)):
            rows = pl.ds(pl.multiple_of(qblk * tq, tq), tq)
            scs = [_mm_nt(qb_ref[e, rows, :], ka_ref[e]) for e in heads]
            dps = [_mm_nt(dob_ref[e, rows, :], va_ref[e]) for e in heads]
            for e, sc, dp in zip(heads, scs, dps):
                q = qb_ref[e, rows, :]
                do = dob_ref[e, rows, :]
                if masked:
                    keep = (_iota((tq, t), 0) - _iota((tq, t), 1)) >= ki * t - qblk * tq
                    sc = jnp.where(keep, sc, NEG_BIG)
                p = jnp.exp(sc)
                ds_b = (p * dp).astype(BF16)
                dv_scr[e] += _mm_tn(p.astype(BF16), do)
                dk_scr[e] += _mm_tn(ds_b, q)
                dq_scr[e, rows, :] += _mm(ds_b, ka_ref[e])

        def loop_body(qblk, carry):
            q_step(qblk, False)
            return carry

        ends = [last_ref[2 * j + e, ki] + 1 for e in range(2)]
        both = jnp.minimum(ends[0], ends[1])
        diag = ki // per_q
        q_step(diag, True)
        lax.fori_loop(diag + 1, both, loop_body, 0)
        for e in range(2):
            def alone(qblk, carry, e=e):
                q_step(qblk, False, (e,))
                return carry

            lax.fori_loop(both, ends[e], alone, 0)

        lane = _iota((t, LANES), 1)
        dk_ref[...] = _pack_pair(dk_scr[0], dk_scr[1], lane).astype(BF16)
        dv_ref[...] = _pack_pair(dv_scr[0], dv_scr[1], lane).astype(BF16)
        rows = pl.ds(pl.multiple_of(ki * t, t), t)
        dc_ref[rows, :] -= (jnp.where(lane == N_HEADS + 2 * j, dk_scr[0][:, AUG_B:AUG_B + 1], 0.0)
                            + jnp.where(lane == N_HEADS + 2 * j + 1, dk_scr[1][:, AUG_B:AUG_B + 1], 0.0))

        @pl.when(ki == s // t - 1)
        def _():
            for blk in range(s // t):
                rws = pl.ds(blk * t, t)
                d0 = dq_scr[0, rws, :]
                d1 = dq_scr[1, rws, :]
                dq_ref[rws, :] = (_pack_pair(d0, d1, lane) * ATT_SCALE).astype(BF16)
                dc_ref[rws, :] += (jnp.where(lane == N_HEADS + 2 * j, d0[:, AUG_A:AUG_A + 1], 0.0)
                                   + jnp.where(lane == N_HEADS + 2 * j + 1, d1[:, AUG_A:AUG_A + 1], 0.0))

    full = pl.BlockSpec((2, s, LANES), lambda j, ki, f: (j, 0, 0))
    blk = pl.BlockSpec((2, t, LANES), lambda j, ki, f: (j, ki, 0))
    pair = pl.BlockSpec((t, LANES), lambda j, ki, f: (ki, j))
    wide = jax.ShapeDtypeStruct((s, ATT_WIDTH), BF16)
    grid_spec = pltpu.PrefetchScalarGridSpec(
        num_scalar_prefetch=1, grid=(N_PAIRS, s // t),
        in_specs=[full, full, blk, blk],
        out_specs=[pl.BlockSpec((s, LANES), lambda j, ki, f: (0, j)), pair, pair,
                   pl.BlockSpec((s, LANES), lambda j, ki, f: (0, 0))],
        scratch_shapes=[pltpu.VMEM((2, s, LANES), F32), pltpu.VMEM((2, t, LANES), F32),
                        pltpu.VMEM((2, t, LANES), F32)])
    return pl.pallas_call(
        body, name="attention_bwd", grid_spec=grid_spec,
        out_shape=(wide, wide, wide, jax.ShapeDtypeStruct((s, LANES), F32)),
        compiler_params=_params(("arbitrary", "arbitrary")),
    )(last_q, qb, dob, ka, va)


def _dsilu(z, sg):
    return sg * (1.0 + z * (1.0 - sg))


def post_mix(x, y, zs, o, za, p, tgt, ssd_g, att_g_lane, ple_g, fin_g, w_out, w_gate, w_proj):
    s = x.shape[0]
    tm = _blk(s, 256)
    half = SSD_WIDTH // N_GROUPS

    def rms_bwd(dy, yn, r):
        return r * (dy - yn * jnp.mean(dy * yn, axis=-1, keepdims=True))

    def colsum(a):
        return jnp.sum(a, axis=0, keepdims=True)

    def body(x_ref, y_ref, zs_ref, o_ref, za_ref, p_ref, t_ref, sg_ref, ag_ref, pg_ref, fg_ref,
             wo_ref, wg_ref, wp_ref,
             dh1_ref, dy_ref, dzs_ref, dob_ref, dza_ref, ycat_ref, dh1b_ref, n2b_ref, dglb_ref, dppb_ref, pb_ref,
             loss_ref, dfin_ref, dple_ref, dssd_ref, datt_ref):
        @pl.when(pl.program_id(0) == 0)
        def _():
            for r in (loss_ref, dfin_ref, dple_ref, dssd_ref, datt_ref):
                r[...] = jnp.zeros_like(r)

        lane = _iota((tm, LANES), 1)
        lo = lane < HEAD_DIM
        zs = zs_ref[...]
        sz = _sigmoid(zs)
        yv = y_ref[...]
        ys = yv * (zs * sz)
        yn, rg = [], []
        for g in range(N_GROUPS):
            seg = ys[:, half * g:half * (g + 1)]
            r = lax.rsqrt(jnp.mean(seg * seg, axis=-1, keepdims=True) + EPS)
            yn.append(seg * r)
            rg.append(r)
            ycat_ref[:, half * g:half * (g + 1)] = (yn[g] * sg_ref[:, half * g:half * (g + 1)]).astype(BF16)
        za = za_ref[...]
        sza = _sigmoid(za)
        silu_za = za * sza
        on, ra = [], []
        for jb in range(N_PAIRS):
            blk = o_ref[:, LANES * jb:LANES * (jb + 1)]
            sq = blk * blk
            ms0 = jnp.sum(jnp.where(lo, sq, 0.0), axis=1, keepdims=True) * (1.0 / HEAD_DIM)
            ms1 = jnp.sum(jnp.where(lo, 0.0, sq), axis=1, keepdims=True) * (1.0 / HEAD_DIM)
            r = jnp.where(lo, lax.rsqrt(ms0 + EPS), lax.rsqrt(ms1 + EPS))
            on.append(blk * r)
            ra.append(r)
            an = on[jb] * ag_ref[:, LANES * jb:LANES * (jb + 1)]
            ycat_ref[:, SSD_WIDTH + LANES * jb:SSD_WIDTH + LANES * (jb + 1)] = (
                an * silu_za[:, LANES * jb:LANES * (jb + 1)]).astype(BF16)
        h1 = x_ref[...] + _mm(ycat_ref[...], wo_ref[...])
        r2 = lax.rsqrt(jnp.mean(h1 * h1, axis=-1, keepdims=True) + EPS)
        n2h = h1 * r2
        n2_b = (n2h * pg_ref[...]).astype(BF16)
        gate = _sigmoid(_mm(n2_b, wg_ref[...]))
        p_b = p_ref[...].astype(BF16)
        pp = _mm(p_b, wp_ref[...])
        h2 = h1 + gate * pp
        r3 = lax.rsqrt(jnp.mean(h2 * h2, axis=-1, keepdims=True) + EPS)
        n3 = h2 * r3
        diff = n3 * fg_ref[...] - t_ref[...]
        sq = colsum(diff * diff)
        part = sq[:, 0:LANES]
        for jb in range(1, D_MODEL // LANES):
            part = part + sq[:, LANES * jb:LANES * (jb + 1)]
        loss_ref[...] += part * (0.5 / D_MODEL)
        dout = diff * (1.0 / D_MODEL)
        dfin_ref[...] += colsum(dout * n3)
        dh2 = rms_bwd(dout * fg_ref[...], n3, r3)
        dgl = dh2 * pp * gate * (1.0 - gate)
        dgl_b = dgl.astype(BF16)
        dn2 = _mm_nt(dgl_b, wg_ref[...])
        dple_ref[...] += colsum(dn2 * n2h)
        dh1 = dh2 + rms_bwd(dn2 * pg_ref[...], n2h, r2)
        dh1_b = dh1.astype(BF16)
        dycat = _mm_nt(dh1_b, wo_ref[...])
        dh1_ref[...] = dh1
        dh1b_ref[...] = dh1_b
        n2b_ref[...] = n2_b
        dglb_ref[...] = dgl_b
        dppb_ref[...] = (dh2 * gate).astype(BF16)
        pb_ref[...] = p_b
        for g in range(N_GROUPS):
            cols = slice(half * g, half * (g + 1))
            dys_g = dycat[:, cols]
            dssd_ref[:, cols] += colsum(dys_g * yn[g])
            dys = rms_bwd(dys_g * sg_ref[:, cols], yn[g], rg[g])
            dy_ref[:, cols] = dys * (zs[:, cols] * sz[:, cols])
            dzs_ref[:, cols] = (dys * yv[:, cols] * _dsilu(zs[:, cols], sz[:, cols])).astype(BF16)
        for jb in range(N_PAIRS):
            cols = slice(LANES * jb, LANES * (jb + 1))
            dya = dycat[:, SSD_WIDTH + LANES * jb:SSD_WIDTH + LANES * (jb + 1)]
            ag = ag_ref[:, cols]
            dan = dya * silu_za[:, cols]
            dza_ref[:, cols] = (dya * (on[jb] * ag) * _dsilu(za[:, cols], sza[:, cols])).astype(BF16)
            datt_ref[:, cols] += colsum(dan * on[jb])
            don = dan * ag
            q = don * on[jb]
            m0 = jnp.sum(jnp.where(lo, q, 0.0), axis=1, keepdims=True) * (1.0 / HEAD_DIM)
            m1 = jnp.sum(jnp.where(lo, 0.0, q), axis=1, keepdims=True) * (1.0 / HEAD_DIM)
            do2 = ra[jb] * (don - on[jb] * jnp.where(lo, m0, m1))
            prod = do2 * o_ref[:, cols]
            for e in range(2):
                delta = jnp.sum(jnp.where(lo, prod, 0.0) if e == 0 else jnp.where(lo, 0.0, prod),
                                axis=1, keepdims=True)
                base = jnp.where(lo, do2 if e == 0 else pltpu.roll(do2, HEAD_DIM, 1), 0.0)
                dob_ref[2 * jb + e] = (base - _aug(lane, AUG_A, _split3(delta))).astype(BF16)

    def rows(n, dtype=None):
        return pl.BlockSpec((tm, n), lambda i: (i, 0))

    def out(n, dtype):
        return jax.ShapeDtypeStruct((s, n), dtype)

    vec = _const_spec((1, D_MODEL))
    vshape = jax.ShapeDtypeStruct((1, D_MODEL), F32)
    return pl.pallas_call(
        body, name="post_mix",
        out_shape=(out(D_MODEL, F32), out(SSD_WIDTH, F32), out(SSD_WIDTH, BF16),
                   jax.ShapeDtypeStruct((N_HEADS, s, LANES), BF16),
                   out(ATT_WIDTH, BF16), out(D_INNER, BF16), out(D_MODEL, BF16), out(D_MODEL, BF16),
                   out(D_MODEL, BF16), out(D_MODEL, BF16), out(PLE_DIM, BF16),
                   jax.ShapeDtypeStruct((1, LANES), F32), vshape, vshape, vshape, vshape),
        grid=(s // tm,),
        in_specs=[rows(D_MODEL), rows(SSD_WIDTH), rows(SSD_WIDTH), rows(ATT_WIDTH), rows(ATT_WIDTH),
                  rows(PLE_DIM), rows(D_MODEL), vec, vec, vec, vec,
                  _const_spec((D_INNER, D_MODEL)), _const_spec((D_MODEL, D_MODEL)), _const_spec((PLE_DIM, D_MODEL))],
        out_specs=(rows(D_MODEL), rows(SSD_WIDTH), rows(SSD_WIDTH),
                   pl.BlockSpec((N_HEADS, tm, LANES), lambda i: (0, i, 0)), rows(ATT_WIDTH),
                   rows(D_INNER), rows(D_MODEL), rows(D_MODEL), rows(D_MODEL), rows(D_MODEL), rows(PLE_DIM),
                   _const_spec((1, LANES)), vec, vec, vec, vec),
        compiler_params=_params(("arbitrary",)),
    )(x, y, zs, o, za, p, tgt, ssd_g, att_g_lane, ple_g, fin_g, w_out, w_gate, w_proj)


def in_proj_bwd(dsegs, wsegs, x, g, dh1, pres):
    s = x.shape[0]
    tm = _blk(s, 512)
    nseg = len(dsegs)
    nbig = len(pres)
    nsteps = s // tm

    def body(*refs):
        d_refs = refs[:nseg]
        w_refs = refs[nseg:2 * nseg]
        x_ref, g_ref, dh1_ref = refs[2 * nseg:2 * nseg + 3]
        rest = refs[2 * nseg + 3:]
        pre_refs, (dx_ref, dg_ref), part_refs = rest[:nbig], rest[nbig:nbig + 2], rest[nbig + 2:2 * nbig + 2]
        ssem, rsem, lsem = rest[2 * nbig + 2:]

        @pl.when(pl.program_id(0) == 0)
        def _():
            dg_ref[...] = jnp.zeros_like(dg_ref)
            for cp in scatter_copies(pre_refs, part_refs, ssem, rsem, lsem):
                cp.start()

        @pl.when(pl.program_id(0) == nsteps - 1)
        def _():
            for cp in scatter_copies(pre_refs, part_refs, ssem, rsem, lsem):
                cp.wait()

        du = _mm_nt(d_refs[0][...], w_refs[0][...])
        for k in range(1, nseg):
            du = du + _mm_nt(d_refs[k][...], w_refs[k][...])
        xv = x_ref[...]
        r = lax.rsqrt(jnp.mean(xv * xv, axis=-1, keepdims=True) + EPS)
        xh = xv * r
        dg_ref[...] += jnp.sum(du * xh, axis=0, keepdims=True)
        dxh = du * g_ref[...]
        dx_ref[...] = r * (dxh - xh * jnp.mean(dxh * xh, axis=-1, keepdims=True)) + dh1_ref[...]

    rows = lambda n: pl.BlockSpec((tm, n), lambda i: (i, 0))
    return pl.pallas_call(
        body, name="in_proj_bwd",
        out_shape=tuple([jax.ShapeDtypeStruct((s, D_MODEL), F32), jax.ShapeDtypeStruct((1, D_MODEL), F32)]
                        + [jax.ShapeDtypeStruct(a.shape, a.dtype) for a in pres]),
        grid=(nsteps,),
        in_specs=([rows(d.shape[1]) for d in dsegs] + [_const_spec(w.shape) for w in wsegs]
                  + [rows(D_MODEL), _const_spec((1, D_MODEL)), rows(D_MODEL)] + [ANY] * nbig),
        out_specs=tuple([rows(D_MODEL), _const_spec((1, D_MODEL))] + [ANY] * nbig),
        scratch_shapes=_sems(3 * nbig) + [pltpu.SemaphoreType.DMA((nbig,))],
        compiler_params=_params(("arbitrary",)),
    )(*dsegs, *wsegs, x, g, dh1, *pres)


SMALL_NAMES = ("norm_g", "conv_b", "dt_bias", "a_log", "d_skip", "ssd_norm_g", "fg_bias", "att_norm_g",
               "ple_norm_g", "final_norm_g")
SMALL_SIZES = (1024, 1536, 16, 16, 16, 1024, 16, 64, 1024, 1024)
CONV_W_SIZE = CONV_WIDTH * CONV_CH


def _pack_small(vals):
    flat = jnp.concatenate([v.reshape(-1).astype(F32) for v in vals])
    flat = jnp.pad(flat, (0, SMALL_ROWS * LANES - flat.shape[0]))
    return flat.reshape(SMALL_ROWS, LANES)


def _unpack_small(pack, shapes):
    flat = pack.reshape(-1)
    out, off = [], 0
    for n, shp in zip(SMALL_SIZES, shapes):
        out.append(flat[off:off + n].reshape(shp))
        off += n
    return out


def _row128(v16, offset=0):
    return jnp.pad(v16.reshape(1, N_HEADS).astype(F32), ((0, 0), (offset, LANES - N_HEADS - offset)))


def local_step(prereduce, later, join_later, x, p, tgt, w_in, conv_w, norm_g, conv_b, dt_bias, a_log, d_skip,
               ssd_norm_g, fg_bias, att_norm_g, ple_norm_g, final_norm_g):
    widths = (SSD_WIDTH, CONV_CH, N_HEADS, ATT_WIDTH, ATT_WIDTH, ATT_WIDTH, ATT_WIDTH)
    c0, c1, c2, c3, c4, c5, c6, c7 = [sum(widths[:i]) for i in range(len(widths) + 1)]
    w_zs, w_xbc, w_dt = w_in[:, c0:c1], w_in[:, c1:c2], w_in[:, c2:c3]
    w_za, w_q, w_k, w_v, w_f = w_in[:, c3:c4], w_in[:, c4:c5], w_in[:, c5:c6], w_in[:, c6:c7], w_in[:, c7:]
    w_small = jnp.concatenate([w_dt, w_f, jnp.zeros((D_MODEL, LANES - 2 * N_HEADS), BF16)], axis=1)

    dtb_row = _row128(dt_bias)
    a_row = _row128(-jnp.exp(a_log.astype(F32)))
    fgb_row = _row128(fg_bias, N_HEADS)
    dskip_lane = jnp.repeat(d_skip.astype(F32), HEAD_DIM).reshape(1, SSD_WIDTH)
    att_g_lane = jnp.tile(att_norm_g.astype(F32), N_HEADS).reshape(1, ATT_WIDTH)
    row = lambda v: v.reshape(1, -1).astype(F32)

    u, zs, xbc, za, small = in_proj_fwd(x, row(norm_g), [w_zs, w_xbc, w_za, w_small])
    cum = forget_cumsum(small, fgb_row)
    qa, ka, va, norms, *gathered = proj_qkv_heads(u, w_q, w_k, w_v, cum, later)
    w_out, w_gate, w_proj = join_later(gathered)
    n_seq = x.shape[0]
    first, _ = live_blocks(norms, cum, _blk(n_seq, ATT_BLOCK), _blk(n_seq, ATT_BLOCK))
    _, last_q = live_blocks(norms, cum, _blk(n_seq, ATT_BLOCK_BWD_Q), _blk(n_seq, ATT_BLOCK_BWD))
    pre, xc = conv_fwd(xbc, conv_w, row(conv_b))
    y, states = ssd_fwd(xc, small, dtb_row, a_row, dskip_lane)
    o, qb = attention_fwd(first, qa, ka, va)
    (dh1, dy, dzs, dob, dza, ycat, dh1_b, n2_b, dgl_b, dpp_b, p_b,
     loss_l, dfin, dple, dssd_g, datt_lane) = post_mix(
        x, y, zs, o, za, p, tgt, row(ssd_norm_g), att_g_lane, row(ple_norm_g), row(final_norm_g),
        w_out, w_gate, w_proj)
    dq, dk, dv, dc = attention_bwd(last_q, qb, ka, va, dob)
    dxc, ddt_raw, da, ddtb, ddsk_lane = ssd_bwd(xc, small, states, dy, dtb_row, a_row, dskip_lane)
    dsmall, dfgb = forget_bwd(dc, small, ddt_raw, fgb_row)
    dxbc, dconv_w8, dconv_b = conv_bwd(xbc, pre, dxc, conv_w)
    dsegs = [dzs, dxbc, dza, dq, dk, dv, dsmall]
    wsegs = [w_zs, w_xbc, w_za, w_q, w_k, w_v, w_small]
    dws = [matmul_tn(u, d, "dw_in_%d" % i) for i, d in enumerate(dsegs)]
    dw_in = jnp.concatenate([dws[0], dws[1], dws[6][:, :N_HEADS], dws[2], dws[3], dws[4], dws[5],
                             dws[6][:, N_HEADS:2 * N_HEADS]], axis=1)
    dw_out = matmul_tn(ycat, dh1_b, "dw_out")
    dw_gate = matmul_tn(n2_b, dgl_b, "dw_gate")
    dw_proj = matmul_tn(p_b, dpp_b, "dw_proj")
    dx, dnorm_g, *parts = in_proj_bwd(dsegs, wsegs, x, row(norm_g), dh1, prereduce(dw_in, dw_out, dw_gate, dw_proj))
    small_grads = [
        dnorm_g, dconv_b, ddtb[0, :N_HEADS], (da * a_row)[0, :N_HEADS],
        ddsk_lane.reshape(N_HEADS, HEAD_DIM).sum(axis=1), dssd_g, dfgb[0, N_HEADS:2 * N_HEADS],
        datt_lane.reshape(N_HEADS, HEAD_DIM).sum(axis=0), dple, dfin]
    loss = jnp.sum(loss_l)
    return loss, dx, parts, dconv_w8[:CONV_WIDTH], small_grads


def kernel(x, p, norm_g, w_in, conv_w, conv_b, dt_bias, a_log, d_skip, ssd_norm_g, fg_bias, att_norm_g, w_out, ple_norm_g, w_ple_gate, w_ple_proj, final_norm_g, loss_target, m_norm_g, m_w_in, m_conv_w, m_conv_b, m_dt_bias, m_a_log, m_d_skip, m_ssd_norm_g, m_fg_bias, m_att_norm_g, m_w_out, m_ple_norm_g, m_w_ple_gate, m_w_ple_proj, m_final_norm_g, v_norm_g, v_w_in, v_conv_w, v_conv_b, v_dt_bias, v_a_log, v_d_skip, v_ssd_norm_g, v_fg_bias, v_att_norm_g, v_w_out, v_ple_norm_g, v_w_ple_gate, v_w_ple_proj, v_final_norm_g):
    chip = 2 * lax.axis_index("x") + lax.axis_index("y")
    core = lax.axis_index("c")

    big_w = [w_in[0], w_out[0], w_ple_gate[0], w_ple_proj[0]]
    own = [a.astype(BF16) for a in big_w] + [conv_w[0]]

    def joined(mine, gathered, axis):
        return jnp.concatenate([jnp.where(chip == j, mine, gathered[j]) for j in range(N_CHIPS)], axis=axis)

    w_in_all, conv_all = gather_weights(own[:1], own[4])
    w_in_f, conv_w_f = joined(own[0], w_in_all, 1), joined(own[4], conv_all, 1)

    def join_later(gathered):
        return [joined(mine, got, axis) for mine, got, axis in zip(own[1:4], gathered, (0, 0, 1))]

    core1 = core.reshape(1).astype(jnp.int32)

    def prereduce(dw_in, dw_out, dw_gate, dw_proj):
        n_in, n_proj = w_in.shape[2], w_ple_proj.shape[2]
        gs = [jnp.stack([dw_in[:, n_in * j:n_in * (j + 1)] for j in range(N_CHIPS)]),
              dw_out.reshape(N_CHIPS, w_out.shape[1], D_MODEL), dw_gate.reshape(N_CHIPS, w_ple_gate.shape[1], D_MODEL),
              jnp.stack([dw_proj[:, n_proj * j:n_proj * (j + 1)] for j in range(N_CHIPS)])]
        return add_halves(core1, gs, halves_to_sibling(gs))

    smalls_w = [norm_g, conv_b, dt_bias, a_log, d_skip, ssd_norm_g, fg_bias, att_norm_g, ple_norm_g, final_norm_g]
    loss_l, dx, parts, dconv_w, small_grads = local_step(
        prereduce, own[1:4], join_later, x[0], p[0, 0], loss_target[0], w_in_f, conv_w_f,
        *[a.reshape(-1) for a in smalls_w])
    loss = lax.psum(loss_l, ("x", "y", "c"))
    mine = sum_parts(parts)
    *theirs, smalls = swap_halves(mine, _pack_small(list(small_grads) + [dconv_w]))

    g_big, d_big, m_big, v_big = adamw_big(
        core1, mine, theirs, big_w, [m_w_in[0], m_w_out[0], m_w_ple_gate[0], m_w_ple_proj[0]],
        [v_w_in[0], v_w_out[0], v_w_ple_gate[0], v_w_ple_proj[0]])
    smalls_m = [m_norm_g, m_conv_b, m_dt_bias, m_a_log, m_d_skip, m_ssd_norm_g, m_fg_bias, m_att_norm_g,
                m_ple_norm_g, m_final_norm_g]
    smalls_v = [v_norm_g, v_conv_b, v_dt_bias, v_a_log, v_d_skip, v_ssd_norm_g, v_fg_bias, v_att_norm_g,
                v_ple_norm_g, v_final_norm_g]
    g_sm, d_sm, m_sm, v_sm = adamw_small(smalls, _pack_small(smalls_w), _pack_small(smalls_m), _pack_small(smalls_v))
    n_small = sum(SMALL_SIZES)
    g_conv_full = g_sm.reshape(-1)[n_small:n_small + CONV_W_SIZE].reshape(CONV_WIDTH, CONV_CH)
    n_conv = conv_w.shape[2]
    g_conv = lax.dynamic_slice_in_dim(g_conv_full, chip * n_conv, n_conv, axis=1)
    d_conv, m_conv, v_conv = adamw_whole(g_conv, conv_w[0], m_conv_w[0], v_conv_w[0], "adamw_conv")

    shapes = [a.shape for a in smalls_w]
    outs = []
    for big, conv, sm in ((g_big, g_conv, g_sm), (d_big, d_conv, d_sm), (m_big, m_conv, m_sm), (v_big, v_conv, v_sm)):
        b_in, b_out, b_gate, b_proj = [a[None] for a in big]
        s_norm, s_convb, s_dtb, s_alog, s_dsk, s_ssdg, s_fgb, s_attg, s_pleg, s_fin = _unpack_small(sm, shapes)
        outs.extend([s_norm, b_in, conv[None], s_convb, s_dtb, s_alog, s_dsk, s_ssdg, s_fgb, s_attg, b_out, s_pleg,
                     b_gate, b_proj, s_fin])
    return (loss, dx[None], *outs)
```

```python
import functools

import jax
import jax.numpy as jnp
from jax import lax
from jax.experimental import pallas as pl
from jax.experimental.pallas import tpu as pltpu

F32 = jnp.float32
BF16 = jnp.bfloat16

D_MODEL = 1024
SSD_WIDTH = 1024
ATT_WIDTH = 1024
N_HEADS = 16
HEAD_DIM = 64
N_GROUPS = 2
D_STATE = 128
CONV_CH = 1536
CONV_WIDTH = 4
CHUNK = 128
PLE_DIM = 256
D_INNER = 2048
EPS = 1e-6
IN_COLS = 6688
N_CHIPS = 4
N_DEV = 8
LANES = 128
N_PAIRS = 8

ADAM_LR = 0.001
ADAM_B1 = 0.9
ADAM_B2 = 0.999
ADAM_EPS = 1e-08
ADAM_WD = 0.01
ADAM_STEP = 10

SMALL_ROWS = 96

NEG_BIG = -1e30
VMEM_LIMIT = 56 * 1024 * 1024

MESH = pl.DeviceIdType.MESH
ANY = pl.BlockSpec(memory_space=pl.ANY)


def _mm(a, b):
    return jnp.dot(a, b, preferred_element_type=F32)


def _mm_nt(a, b):
    return lax.dot_general(a, b, (((1,), (1,)), ((), ())), preferred_element_type=F32)


def _mm_tn(a, b):
    return lax.dot_general(a, b, (((0,), (0,)), ((), ())), preferred_element_type=F32)


def _mm_exact(a, b):
    return jnp.dot(a, b, preferred_element_type=F32, precision=lax.Precision.HIGHEST)


def _softplus(x):
    return jnp.maximum(x, 0.0) + jnp.log1p(jnp.exp(-jnp.abs(x)))


def _sigmoid(x):
    return jax.nn.sigmoid(x)


def _iota(shape, dim):
    return lax.broadcasted_iota(jnp.int32, shape, dim)


def _params(sem=None):
    return pltpu.CompilerParams(dimension_semantics=sem, vmem_limit_bytes=VMEM_LIMIT)


def _blk(n, pref):
    return min(n, pref)


def _const_spec(shape):
    nd = len(shape)
    return pl.BlockSpec(shape, lambda *_: (0,) * nd)


def _chip_peers():
    x, y, c = lax.axis_index("x"), lax.axis_index("y"), lax.axis_index("c")
    return x, y, c, [(1 - x, y, c), (x, 1 - y, c), (1 - x, 1 - y, c)]


def _half(rows, c):
    h = rows // 2
    return pl.ds(pl.multiple_of(c * h, 8), h)


def _sems(n):
    return [pltpu.SemaphoreType.DMA((n,)), pltpu.SemaphoreType.DMA((n,))]


def gather_copies(ins, outs, ssem1, rsem1, ssem2, rsem2):
    n = len(ins)
    x, y, c, peers = _chip_peers()
    me = 2 * x + y
    fetched, passed = [], []
    for k, peer in enumerate(peers):
        chip = 2 * peer[0] + peer[1]
        for i in range(n):
            h = _half(ins[i].shape[0], c)
            fetched.append(pltpu.make_async_remote_copy(
                src_ref=ins[i].at[h], dst_ref=outs[i].at[me, h], send_sem=ssem1.at[n * k + i],
                recv_sem=rsem1.at[n * k + i], device_id=peer, device_id_type=MESH))
            passed.append(pltpu.make_async_remote_copy(
                src_ref=outs[i].at[chip, h], dst_ref=outs[i].at[chip, h], send_sem=ssem2.at[n * k + i],
                recv_sem=rsem2.at[n * k + i], device_id=(x, y, 1 - c), device_id_type=MESH))
    return fetched, passed


def gather_weights(shards, conv_s):
    n = len(shards)

    def body(*refs):
        ins, conv_in = refs[:n], refs[n]
        outs, conv_out = refs[n + 1:2 * n + 1], refs[2 * n + 1]
        ssem1, rsem1, ssem2, rsem2, c_ssem, c_rsem = refs[2 * n + 2:]
        x, y, _, peers = _chip_peers()
        fetched, passed = gather_copies(ins, outs, ssem1, rsem1, ssem2, rsem2)
        small = [pltpu.make_async_remote_copy(
            src_ref=conv_in, dst_ref=conv_out.at[2 * x + y], send_sem=c_ssem.at[k], recv_sem=c_rsem.at[k],
            device_id=peer, device_id_type=MESH) for k, peer in enumerate(peers)]
        for cp in fetched + small:
            cp.start()
        for landed, onward in zip(fetched, passed):
            landed.wait_recv()
            onward.start()
        for cp in passed:
            cp.wait_recv()
        for cp in fetched + passed:
            cp.wait_send()
        for cp in small:
            cp.wait()

    return pl.pallas_call(
        body, name="gather_weights",
        out_shape=tuple(jax.ShapeDtypeStruct((N_CHIPS,) + a.shape, a.dtype) for a in list(shards) + [conv_s]),
        in_specs=[ANY] * (n + 1), out_specs=(ANY,) * (n + 1),
        scratch_shapes=_sems(3 * n) + _sems(3 * n) + _sems(3),
    )(*shards, conv_s)


def halves_to_sibling(gs):
    n = len(gs)

    def body(*refs):
        ins, outs = refs[:n], refs[n:2 * n]
        ssem, rsem = refs[2 * n:]
        x, y, c = lax.axis_index("x"), lax.axis_index("y"), lax.axis_index("c")
        copies = []
        for i in range(n):
            for j in range(N_CHIPS):
                copies.append(pltpu.make_async_remote_copy(
                    src_ref=ins[i].at[j, _half(ins[i].shape[1], 1 - c)], dst_ref=outs[i].at[j],
                    send_sem=ssem.at[N_CHIPS * i + j], recv_sem=rsem.at[N_CHIPS * i + j],
                    device_id=(x, y, 1 - c), device_id_type=MESH))
        for cp in copies:
            cp.start()
        for cp in copies:
            cp.wait()

    return pl.pallas_call(
        body, name="halves_to_sibling",
        out_shape=tuple(jax.ShapeDtypeStruct((N_CHIPS, g.shape[1] // 2, g.shape[2]), F32) for g in gs),
        in_specs=[ANY] * n, out_specs=(ANY,) * n, scratch_shapes=_sems(N_CHIPS * n),
    )(*gs)


RED_GRID = 4
ADD_GRID = 2


def add_halves(core, gs, rbs):
    n = len(gs)

    def body(c_ref, *refs):
        for i in range(n):
            refs[2 * n + i][...] = (refs[i][...] + refs[n + i][...]).astype(BF16)

    def blk(g):
        return (1, g.shape[1] // 2 // ADD_GRID, g.shape[2])

    grid_spec = pltpu.PrefetchScalarGridSpec(
        num_scalar_prefetch=1, grid=(N_CHIPS, ADD_GRID),
        in_specs=([pl.BlockSpec(blk(g), lambda j, b, c_ref: (j, c_ref[0] * ADD_GRID + b, 0)) for g in gs]
                  + [pl.BlockSpec(blk(g), lambda j, b, c_ref: (j, b, 0)) for g in gs]),
        out_specs=[pl.BlockSpec(blk(g), lambda j, b, c_ref: (j, b, 0)) for g in gs])
    return pl.pallas_call(
        body, name="add_halves", grid_spec=grid_spec,
        out_shape=tuple(jax.ShapeDtypeStruct(r.shape, BF16) for r in rbs),
        compiler_params=_params(("parallel", "parallel")),
    )(core, *gs, *rbs)


def scatter_copies(ins, outs, ssem, rsem, lsem):
    n = len(ins)
    x, y, _, peers = _chip_peers()
    me = 2 * x + y
    copies = [pltpu.make_async_copy(ins[i].at[me], outs[i].at[me], lsem.at[i]) for i in range(n)]
    for k, peer in enumerate(peers):
        dst_chip = 2 * peer[0] + peer[1]
        for i in range(n):
            copies.append(pltpu.make_async_remote_copy(
                src_ref=ins[i].at[dst_chip], dst_ref=outs[i].at[me], send_sem=ssem.at[n * k + i],
                recv_sem=rsem.at[n * k + i], device_id=peer, device_id_type=MESH))
    return copies


def sum_parts(parts):
    n = len(parts)

    def body(*refs):
        for i in range(n):
            p_ref = refs[i]
            refs[n + i][...] = ((p_ref[0].astype(F32) + p_ref[1].astype(F32)) + p_ref[2].astype(F32)
                                ) + p_ref[3].astype(F32)

    def rows(p):
        return p.shape[1] // RED_GRID

    return pl.pallas_call(
        body, name="sum_parts",
        out_shape=tuple(jax.ShapeDtypeStruct(p.shape[1:], F32) for p in parts),
        grid=(RED_GRID,),
        in_specs=[pl.BlockSpec((N_CHIPS, rows(p), p.shape[2]), lambda b: (0, b, 0)) for p in parts],
        out_specs=tuple(pl.BlockSpec((rows(p), p.shape[2]), lambda b: (b, 0)) for p in parts),
        compiler_params=_params(("parallel",)),
    )(*parts)


def swap_halves(reds, small):
    n = len(reds)

    def body(*refs):
        ins, s_ref = refs[:n], refs[n]
        outs, smalls_ref = refs[n + 1:2 * n + 1], refs[2 * n + 1]
        ssem, rsem, s_ssem, s_rsem, lsem = refs[2 * n + 2:]
        x, y, c = lax.axis_index("x"), lax.axis_index("y"), lax.axis_index("c")
        dev = 4 * x + 2 * y + c
        copies = [pltpu.make_async_remote_copy(
            src_ref=ins[i], dst_ref=outs[i], send_sem=ssem.at[i], recv_sem=rsem.at[i],
            device_id=(x, y, 1 - c), device_id_type=MESH) for i in range(n)]
        copies.append(pltpu.make_async_copy(s_ref, smalls_ref.at[dev], lsem))
        for k in range(1, N_DEV):
            fx, fy, fc = (k >> 2) & 1, (k >> 1) & 1, k & 1
            peer = ((1 - x) if fx else x, (1 - y) if fy else y, (1 - c) if fc else c)
            copies.append(pltpu.make_async_remote_copy(
                src_ref=s_ref, dst_ref=smalls_ref.at[dev], send_sem=s_ssem.at[k - 1], recv_sem=s_rsem.at[k - 1],
                device_id=peer, device_id_type=MESH))
        for cp in copies:
            cp.start()
        for cp in copies:
            cp.wait()

    return pl.pallas_call(
        body, name="swap_halves",
        out_shape=tuple([jax.ShapeDtypeStruct(r.shape, F32) for r in reds]
                        + [jax.ShapeDtypeStruct((N_DEV,) + small.shape, F32)]),
        in_specs=[ANY] * (n + 1), out_specs=(ANY,) * (n + 1),
        scratch_shapes=_sems(n) + _sems(N_DEV - 1) + [pltpu.SemaphoreType.DMA],
    )(*reds, small)


def _adamw(w, g, m, v):
    m = ADAM_B1 * m + (1.0 - ADAM_B1) * g
    v = ADAM_B2 * v + (1.0 - ADAM_B2) * (g * g)
    m_hat = m / (1.0 - ADAM_B1 ** ADAM_STEP)
    v_hat = v / (1.0 - ADAM_B2 ** ADAM_STEP)
    delta = -ADAM_LR * (m_hat / (jnp.sqrt(v_hat) + ADAM_EPS) + ADAM_WD * w)
    return delta, m, v


def adamw_big(core, mine, theirs, ws, ms, vs):
    n = len(ws)
    per_half = RED_GRID // 2

    def body(c_ref, *refs):
        own = (pl.program_id(0) // per_half) == c_ref[0]
        for i in range(n):
            g = jnp.where(own, refs[i][...], refs[n + i][...])
            d, mn, vn = _adamw(refs[2 * n + i][...], g, refs[3 * n + i][...], refs[4 * n + i][...])
            refs[5 * n + i][...] = g
            refs[6 * n + i][...] = d
            refs[7 * n + i][...] = mn
            refs[8 * n + i][...] = vn

    def blk(w):
        return (w.shape[0] // RED_GRID, w.shape[1])

    halves = [pl.BlockSpec(blk(w), lambda b, c_ref: (b % per_half, 0)) for w in ws]
    whole = [pl.BlockSpec(blk(w), lambda b, c_ref: (b, 0)) for w in ws]
    shapes = [jax.ShapeDtypeStruct(w.shape, F32) for w in ws]
    grid_spec = pltpu.PrefetchScalarGridSpec(
        num_scalar_prefetch=1, grid=(RED_GRID,), in_specs=halves * 2 + whole * 3, out_specs=whole * 4)
    outs = pl.pallas_call(
        body, name="adamw_big", out_shape=tuple(shapes * 4), grid_spec=grid_spec,
        compiler_params=_params(("parallel",)),
    )(core, *mine, *theirs, *ws, *ms, *vs)
    return outs[:n], outs[n:2 * n], outs[2 * n:3 * n], outs[3 * n:]


def adamw_whole(g, w, m, v, name):
    def body(g_ref, w_ref, m_ref, v_ref, d_out, m_out, v_out):
        d, mn, vn = _adamw(w_ref[...], g_ref[...], m_ref[...], v_ref[...])
        d_out[...] = d
        m_out[...] = mn
        v_out[...] = vn

    shp = jax.ShapeDtypeStruct(g.shape, F32)
    return pl.pallas_call(body, name=name, out_shape=(shp,) * 3)(g, w, m, v)


def adamw_small(smalls, w, m, v):
    def body(s_ref, w_ref, m_ref, v_ref, g_out, d_out, m_out, v_out):
        g = s_ref[0]
        for k in range(1, N_DEV):
            g = g + s_ref[k]
        d, mn, vn = _adamw(w_ref[...], g, m_ref[...], v_ref[...])
        g_out[...] = g
        d_out[...] = d
        m_out[...] = mn
        v_out[...] = vn

    shp = jax.ShapeDtypeStruct((SMALL_ROWS, LANES), F32)
    return pl.pallas_call(body, name="adamw_small", out_shape=(shp,) * 4)(smalls, w, m, v)


def in_proj_fwd(x, g, ws):
    s = x.shape[0]
    tm = _blk(s, 512)
    n = len(ws)

    def body(x_ref, g_ref, *refs):
        xv = x_ref[...]
        r = lax.rsqrt(jnp.mean(xv * xv, axis=-1, keepdims=True) + EPS)
        u = (xv * r * g_ref[...]).astype(BF16)
        refs[n][...] = u
        for i in range(n):
            refs[n + 1 + i][...] = _mm(u, refs[i][...])

    rows = lambda width: pl.BlockSpec((tm, width), lambda i: (i, 0))
    return pl.pallas_call(
        body, name="in_proj_fwd",
        out_shape=tuple([jax.ShapeDtypeStruct((s, D_MODEL), BF16)]
                        + [jax.ShapeDtypeStruct((s, w.shape[1]), F32) for w in ws]),
        grid=(s // tm,),
        in_specs=[rows(D_MODEL), _const_spec((1, D_MODEL))] + [_const_spec(w.shape) for w in ws],
        out_specs=tuple([rows(D_MODEL)] + [rows(w.shape[1]) for w in ws]),
        compiler_params=_params(("parallel",)),
    )(x, g, *ws)


def matmul_tn(a, b, name):
    s, m = a.shape
    n = b.shape[1]
    tk = _blk(s, 2048)
    tn = _blk(n, 512) if m > D_MODEL else (n // 2 if n > D_MODEL else n)

    def body(a_ref, b_ref, o_ref):
        @pl.when(pl.program_id(1) == 0)
        def _():
            o_ref[...] = jnp.zeros_like(o_ref)

        o_ref[...] += _mm_tn(a_ref[...], b_ref[...])

    return pl.pallas_call(
        body, name=name, out_shape=jax.ShapeDtypeStruct((m, n), F32), grid=(n // tn, s // tk),
        in_specs=[pl.BlockSpec((tk, m), lambda j, i: (i, 0)), pl.BlockSpec((tk, tn), lambda j, i: (i, j))],
        out_specs=pl.BlockSpec((m, tn), lambda j, i: (0, j)),
        compiler_params=_params(("parallel", "arbitrary")),
    )(a, b)


def conv_fwd(xbc, w, b):
    s = xbc.shape[0]
    tm = _blk(s, 256)

    def body(x_ref, t_ref, w_ref, b_ref, pre_ref, act_ref):
        i = pl.program_id(0)
        row8 = _iota((8, LANES), 0)
        for c0 in range(0, CONV_CH, LANES):
            cols = slice(c0, c0 + LANES)
            cur = x_ref[:, cols]
            tail = jnp.where(i > 0, t_ref[:, cols], 0.0)
            wv = w_ref[:, cols]
            bias = b_ref[:, cols]
            acc = cur * wv[3:4, :] + bias
            head = cur[0:8, :] * wv[3:4, :] + bias
            for sh in range(1, CONV_WIDTH):
                wk = wv[3 - sh:4 - sh, :]
                acc = acc + pltpu.roll(cur, sh, 0) * wk
                first = jnp.where(row8 < sh, pltpu.roll(tail, sh, 0), pltpu.roll(cur[0:8, :], sh, 0))
                head = head + first * wk
            pre_ref[:, cols] = acc
            act_ref[:, cols] = acc * _sigmoid(acc)
            pre_ref[0:8, cols] = head
            act_ref[0:8, cols] = head * _sigmoid(head)

    shp = jax.ShapeDtypeStruct(xbc.shape, F32)
    rows = pl.BlockSpec((tm, CONV_CH), lambda i: (i, 0))
    return pl.pallas_call(
        body, name="conv_fwd", out_shape=(shp, shp), grid=(s // tm,),
        in_specs=[rows, pl.BlockSpec((8, CONV_CH), lambda i: (jnp.maximum(i * (tm // 8) - 1, 0), 0)),
                  _const_spec((CONV_WIDTH, CONV_CH)), _const_spec((1, CONV_CH))],
        out_specs=(rows, rows), compiler_params=_params(("parallel",)),
    )(xbc, xbc, w, b)


def conv_bwd(xbc, pre, dact, w):
    s = xbc.shape[0]
    tm = _blk(s, 256)
    nb = s // tm

    def dsilu(p):
        sg = _sigmoid(p)
        return sg * (1.0 + p * (1.0 - sg))

    def body(x_ref, xt_ref, p_ref, pn_ref, d_ref, dn_ref, w_ref, dx_ref, dw_ref, db_ref):
        i = pl.program_id(0)

        @pl.when(i == 0)
        def _():
            dw_ref[...] = jnp.zeros_like(dw_ref)
            db_ref[...] = jnp.zeros_like(db_ref)

        row8 = _iota((8, LANES), 0)
        for c0 in range(0, CONV_CH, LANES):
            cols = slice(c0, c0 + LANES)
            wv = w_ref[:, cols]
            dpre = d_ref[:, cols] * dsilu(p_ref[:, cols])
            dnext = jnp.where(i < nb - 1, dn_ref[:, cols] * dsilu(pn_ref[:, cols]), 0.0)
            cur = x_ref[:, cols]
            tail = jnp.where(i > 0, xt_ref[:, cols], 0.0)
            dx = dpre * wv[3:4, :]
            last = dpre[tm - 8:tm, :] * wv[3:4, :]
            db_ref[:, cols] += jnp.sum(dpre, axis=0, keepdims=True)
            dws = [jnp.sum(dpre * cur, axis=0, keepdims=True)]
            for sh in range(1, CONV_WIDTH):
                wk = wv[3 - sh:4 - sh, :]
                dx = dx + pltpu.roll(dpre, tm - sh, 0) * wk
                nxt = jnp.where(row8 >= 8 - sh, pltpu.roll(dnext, 8 - sh, 0),
                                pltpu.roll(dpre[tm - 8:tm, :], 8 - sh, 0))
                last = last + nxt * wk
                xs = pltpu.roll(cur, sh, 0)
                first = jnp.where(row8 < sh, pltpu.roll(tail, sh, 0), xs[0:8, :])
                dws.append(jnp.sum(dpre * xs, axis=0, keepdims=True)
                           + jnp.sum(dpre[0:8, :] * (first - xs[0:8, :]), axis=0, keepdims=True))
            dx_ref[:, cols] = dx.astype(BF16)
            dx_ref[tm - 8:tm, cols] = last.astype(BF16)
            for sh in range(CONV_WIDTH):
                dw_ref[3 - sh:4 - sh, cols] += dws[sh]

    rows = pl.BlockSpec((tm, CONV_CH), lambda i: (i, 0))
    prev8 = pl.BlockSpec((8, CONV_CH), lambda i: (jnp.maximum(i * (tm // 8) - 1, 0), 0))
    next8 = pl.BlockSpec((8, CONV_CH), lambda i: (jnp.minimum((i + 1) * (tm // 8), s // 8 - 1), 0))
    return pl.pallas_call(
        body, name="conv_bwd",
        out_shape=(jax.ShapeDtypeStruct(xbc.shape, BF16), jax.ShapeDtypeStruct((8, CONV_CH), F32),
                   jax.ShapeDtypeStruct((1, CONV_CH), F32)),
        grid=(nb,),
        in_specs=[rows, prev8, rows, next8, rows, next8, _const_spec((CONV_WIDTH, CONV_CH))],
        out_specs=(rows, _const_spec((8, CONV_CH)), _const_spec((1, CONV_CH))),
        compiler_params=_params(("arbitrary",)),
    )(xbc, xbc, pre, pre, dact, dact, w)


def _pair_lanes(mat, j, lane):
    return jnp.where(lane < HEAD_DIM, mat[:, 2 * j:2 * j + 1], mat[:, 2 * j + 1:2 * j + 2])


def _ssd_chunk_prelude(sm, dtb, a_row, lane, sub):
    raw = sm + dtb
    head_lane = lane < N_HEADS
    dt = jnp.where(head_lane, _softplus(raw), 0.0)
    sig = jnp.where(head_lane, _sigmoid(raw), 0.0)
    tri = (lane <= sub).astype(F32)
    acs = _mm_exact(tri, dt * a_row)
    return dt, sig, acs, acs.T


GROUP_WIDTH = SSD_WIDTH // N_GROUPS
HEADS_PER_GROUP = N_HEADS // N_GROUPS


def _expand_group(mat, g, lane):
    return jnp.concatenate([_pair_lanes(mat, j, lane) for j in range(4 * g, 4 * g + 4)], axis=1)


def _head_sums(q, g):
    row = _iota((GROUP_WIDTH, LANES), 0)
    seg = (_iota((GROUP_WIDTH, LANES), 1) == HEADS_PER_GROUP * g + (row >> 6)).astype(BF16)
    hi = q.astype(BF16)
    lo = (q - hi.astype(F32)).astype(BF16)
    return _mm(hi, seg) + _mm(lo, seg)


def _rows_from_lanes(row512):
    return jnp.broadcast_to(row512, (LANES, GROUP_WIDTH)).T


def ssd_fwd(xc, small, dtb_row, a_row, dskip_lane):
    s = xc.shape[0]
    nc = s // CHUNK

    def body(xc_ref, sm_ref, dtb_ref, a_ref, dsk_ref, y_ref, hs_ref, h_scr):
        c = pl.program_id(0)

        @pl.when(c == 0)
        def _():
            h_scr[...] = jnp.zeros_like(h_scr)

        lane = _iota((CHUNK, LANES), 1)
        sub = _iota((CHUNK, LANES), 0)
        causal = lane <= sub
        dt, _, acs, acs_t = _ssd_chunk_prelude(sm_ref[...], dtb_ref[...], a_ref[...], lane, sub)
        for g in range(N_GROUPS):
            cols = slice(GROUP_WIDTH * g, GROUP_WIDTH * (g + 1))
            b_off = SSD_WIDTH + D_STATE * g
            c_off = SSD_WIDTH + N_GROUPS * D_STATE + D_STATE * g
            b_b = xc_ref[:, b_off:b_off + D_STATE].astype(BF16)
            c_b = xc_ref[:, c_off:c_off + D_STATE].astype(BF16)
            cb = _mm_nt(c_b, b_b)
            x_g = xc_ref[:, cols]
            acs_g = _expand_group(acs, g, lane)
            xdt_g = x_g * _expand_group(dt, g, lane)
            xdt_b = xdt_g.astype(BF16)
            heads = range(HEADS_PER_GROUP * g, HEADS_PER_GROUP * (g + 1))
            m_b = [(cb * jnp.exp(jnp.where(causal, acs[:, h:h + 1] - acs_t[h:h + 1, :], NEG_BIG))).astype(BF16)
                   for h in heads]
            yd = [_mm(m_b[k], xdt_b[:, LANES * (k // 2):LANES * (k // 2 + 1)]) for k in range(HEADS_PER_GROUP)]
            yd_g = jnp.concatenate([jnp.where(lane < HEAD_DIM, yd[2 * k], yd[2 * k + 1]) for k in range(4)], axis=1)
            h_g = h_scr[g]
            t_g = _mm_nt(c_b, h_g.astype(BF16))
            y_ref[:, cols] = yd_g + jnp.exp(acs_g) * t_g + dsk_ref[:, cols] * x_g
            hs_ref[0, g] = h_g
            last_g = acs_g[CHUNK - 1:CHUNK, :]
            w_b = (xdt_g * jnp.exp(last_g - acs_g)).astype(BF16)
            h_scr[g] = h_g * jnp.exp(_rows_from_lanes(last_g)) + _mm_tn(w_b, b_b)

    return pl.pallas_call(
        body, name="ssd_fwd",
        out_shape=(jax.ShapeDtypeStruct((s, SSD_WIDTH), F32),
                   jax.ShapeDtypeStruct((nc, N_GROUPS, GROUP_WIDTH, D_STATE), F32)),
        grid=(nc,),
        in_specs=[pl.BlockSpec((CHUNK, CONV_CH), lambda c: (c, 0)), pl.BlockSpec((CHUNK, LANES), lambda c: (c, 0)),
                  _const_spec((1, LANES)), _const_spec((1, LANES)), _const_spec((1, SSD_WIDTH))],
        out_specs=(pl.BlockSpec((CHUNK, SSD_WIDTH), lambda c: (c, 0)),
                   pl.BlockSpec((1, N_GROUPS, GROUP_WIDTH, D_STATE), lambda c: (c, 0, 0, 0))),
        scratch_shapes=[pltpu.VMEM((N_GROUPS, GROUP_WIDTH, D_STATE), F32)],
        compiler_params=_params(("arbitrary",)),
    )(xc, small, dtb_row, a_row, dskip_lane)


def ssd_bwd(xc, small, states, dy, dtb_row, a_row, dskip_lane):
    s = xc.shape[0]
    nc = s // CHUNK
    rev = lambda c: nc - 1 - c

    def body(xc_ref, sm_ref, hs_ref, dy_ref, dtb_ref, a_ref, dsk_ref,
             dxc_ref, ddt_ref, da_ref, ddtb_ref, ddsk_ref, dh_scr):
        c = pl.program_id(0)

        @pl.when(c == 0)
        def _():
            dh_scr[...] = jnp.zeros_like(dh_scr)
            da_ref[...] = jnp.zeros_like(da_ref)
            ddtb_ref[...] = jnp.zeros_like(ddtb_ref)
            ddsk_ref[...] = jnp.zeros_like(ddsk_ref)

        lane = _iota((CHUNK, LANES), 1)
        sub = _iota((CHUNK, LANES), 0)
        causal = lane <= sub
        upper = lane >= sub
        is_last = sub == CHUNK - 1
        a_row_v = a_ref[...]
        dt, sig, acs, acs_t = _ssd_chunk_prelude(sm_ref[...], dtb_ref[...], a_row_v, lane, sub)
        cd = jnp.exp(acs[CHUNK - 1:CHUNK, :])
        dacs_c = jnp.zeros((CHUNK, LANES), F32)
        dacs_r = jnp.zeros((LANES, CHUNK), F32)
        ddtx = jnp.zeros((CHUNK, LANES), F32)
        for g in range(N_GROUPS):
            cols = slice(GROUP_WIDTH * g, GROUP_WIDTH * (g + 1))
            b_off = SSD_WIDTH + D_STATE * g
            c_off = SSD_WIDTH + N_GROUPS * D_STATE + D_STATE * g
            b_b = xc_ref[:, b_off:b_off + D_STATE].astype(BF16)
            c_b = xc_ref[:, c_off:c_off + D_STATE].astype(BF16)
            cb = _mm_nt(c_b, b_b)
            cb_t = _mm_nt(b_b, c_b)
            x_g = xc_ref[:, cols]
            dy_g = dy_ref[:, cols]
            dt_g = _expand_group(dt, g, lane)
            acs_g = _expand_group(acs, g, lane)
            last_g = acs_g[CHUNK - 1:CHUNK, :]
            e_g = jnp.exp(acs_g)
            dte_g = jnp.exp(last_g - acs_g)
            xdt_g = x_g * dt_g
            xdt_b = xdt_g.astype(BF16)
            h_g = hs_ref[0, g]
            dh_g = dh_scr[g]
            h_b = h_g.astype(BF16)
            dh_b = dh_g.astype(BF16)
            heads = list(range(HEADS_PER_GROUP * g, HEADS_PER_GROUP * (g + 1)))
            segs = [acs[:, h:h + 1] - acs_t[h:h + 1, :] for h in heads]
            lms = [jnp.exp(jnp.where(causal, sg, NEG_BIG)) for sg in segs]
            mts = [(cb_t * jnp.exp(jnp.where(upper, -sg, NEG_BIG))).astype(BF16) for sg in segs]
            dyh = []
            for k in range(HEADS_PER_GROUP):
                blk = dy_g[:, LANES * (k // 2):LANES * (k // 2 + 1)]
                in_head = (lane < HEAD_DIM) if k % 2 == 0 else (lane >= HEAD_DIM)
                dyh.append(jnp.where(in_head, blk, 0.0).astype(BF16))
            dms = [_mm_nt(dyh[k], xdt_b[:, LANES * (k // 2):LANES * (k // 2 + 1)]) for k in range(HEADS_PER_GROUP)]
            dxs = [_mm(mts[k], dyh[k]) for k in range(HEADS_PER_GROUP)]
            dcb = jnp.zeros((CHUNK, CHUNK), F32)
            for k, h in enumerate(heads):
                gmat = dms[k] * (cb * lms[k])
                dacs_c = dacs_c + jnp.where(lane == h, jnp.sum(gmat, axis=1, keepdims=True), 0.0)
                dacs_r = dacs_r - jnp.where(sub == h, jnp.sum(gmat, axis=0, keepdims=True), 0.0)
                dcb = dcb + dms[k] * lms[k]
            dxdt_g = jnp.concatenate([dxs[2 * k] + dxs[2 * k + 1] for k in range(4)], axis=1)
            t_g = _mm_nt(c_b, h_b)
            dacs_c = dacs_c + _head_sums(dy_g * e_g * t_g, g)
            dt_b = (dy_g * e_g).astype(BF16)
            dc_acc = _mm(dt_b, h_b)
            dh_prev = _mm_tn(dt_b, c_b)
            dw_g = _mm_nt(b_b, dh_b)
            w_g = xdt_g * dte_g
            dxdt_g = dxdt_g + dw_g * dte_g
            db_acc = _mm(w_g.astype(BF16), dh_b)
            r2 = _head_sums(dw_g * w_g, g)
            dacs_c = dacs_c + jnp.where(is_last, jnp.sum(r2, axis=0, keepdims=True), 0.0) - r2
            q3 = jnp.sum(dh_g * h_g, axis=1, keepdims=True)
            for k, h in enumerate(heads):
                tot = jnp.sum(q3[HEAD_DIM * k:HEAD_DIM * (k + 1), :], keepdims=True) * cd[:, h:h + 1]
                dacs_c = dacs_c + jnp.where(is_last & (lane == h), tot, 0.0)
            dh_scr[g] = dh_prev + dh_g * jnp.exp(_rows_from_lanes(last_g))
            dxc_ref[:, cols] = dxdt_g * dt_g + dsk_ref[:, cols] * dy_g
            ddtx = ddtx + _head_sums(dxdt_g * x_g, g)
            ddsk_ref[:, cols] += jnp.sum(dy_g * x_g, axis=0, keepdims=True)
            dxc_ref[:, b_off:b_off + D_STATE] = db_acc + _mm(dcb.T.astype(BF16), c_b)
            dxc_ref[:, c_off:c_off + D_STATE] = dc_acc + _mm(dcb.astype(BF16), b_b)
        dacs = dacs_c + dacs_r.T
        dadt = _mm_exact((lane >= sub).astype(F32), dacs)
        ddt = dadt * a_row_v + ddtx
        ddt_raw = ddt * sig
        ddt_ref[...] = ddt_raw
        da_ref[...] += jnp.sum(dadt * dt, axis=0, keepdims=True)
        ddtb_ref[...] += jnp.sum(ddt_raw, axis=0, keepdims=True)

    return pl.pallas_call(
        body, name="ssd_bwd",
        out_shape=(jax.ShapeDtypeStruct((s, CONV_CH), F32), jax.ShapeDtypeStruct((s, LANES), F32),
                   jax.ShapeDtypeStruct((1, LANES), F32), jax.ShapeDtypeStruct((1, LANES), F32),
                   jax.ShapeDtypeStruct((1, SSD_WIDTH), F32)),
        grid=(nc,),
        in_specs=[pl.BlockSpec((CHUNK, CONV_CH), lambda c: (rev(c), 0)),
                  pl.BlockSpec((CHUNK, LANES), lambda c: (rev(c), 0)),
                  pl.BlockSpec((1, N_GROUPS, GROUP_WIDTH, D_STATE), lambda c: (rev(c), 0, 0, 0)),
                  pl.BlockSpec((CHUNK, SSD_WIDTH), lambda c: (rev(c), 0)),
                  _const_spec((1, LANES)), _const_spec((1, LANES)), _const_spec((1, SSD_WIDTH))],
        out_specs=(pl.BlockSpec((CHUNK, CONV_CH), lambda c: (rev(c), 0)),
                   pl.BlockSpec((CHUNK, LANES), lambda c: (rev(c), 0)),
                   _const_spec((1, LANES)), _const_spec((1, LANES)), _const_spec((1, SSD_WIDTH))),
        scratch_shapes=[pltpu.VMEM((N_GROUPS, GROUP_WIDTH, D_STATE), F32)],
        compiler_params=_params(("arbitrary",)),
    )(xc, small, states, dy, dtb_row, a_row, dskip_lane)


FORGET_BLOCK = 512


def forget_cumsum(small, fgb_row):
    s = small.shape[0]
    t = _blk(s, FORGET_BLOCK)
    nb = s // t

    def body(sm_ref, b_ref, cc_ref, carry):
        i = pl.program_id(0)

        @pl.when(i == 0)
        def _():
            carry[...] = jnp.zeros_like(carry)

        lane = _iota((t, LANES), 1)
        in_f = (lane >= N_HEADS) & (lane < 2 * N_HEADS)
        logf = jnp.where(in_f, -_softplus(-(sm_ref[...] + b_ref[...])), 0.0)
        tri = (_iota((t, t), 1) <= _iota((t, t), 0)).astype(F32)
        cum = _mm_exact(tri, logf) + carry[0:1, :]
        cc_ref[...] = cum
        carry[...] = jnp.broadcast_to(cum[t - 1:t, :], (8, LANES))

    return pl.pallas_call(
        body, name="forget_cumsum",
        out_shape=jax.ShapeDtypeStruct((s, LANES), F32),
        grid=(nb,),
        in_specs=[pl.BlockSpec((t, LANES), lambda i: (i, 0)), _const_spec((1, LANES))],
        out_specs=pl.BlockSpec((t, LANES), lambda i: (i, 0)),
        scratch_shapes=[pltpu.VMEM((8, LANES), F32)],
        compiler_params=_params(("arbitrary",)),
    )(small, fgb_row)


def forget_bwd(dc, small, ddt_raw, fgb_row):
    s = small.shape[0]
    t = _blk(s, FORGET_BLOCK)
    nb = s // t
    rev = lambda i: nb - 1 - i

    def body(dc_ref, sm_ref, ddt_ref, b_ref, ds_ref, dfb_ref, carry):
        i = pl.program_id(0)

        @pl.when(i == 0)
        def _():
            carry[...] = jnp.zeros_like(carry)
            dfb_ref[...] = jnp.zeros_like(dfb_ref)

        lane = _iota((t, LANES), 1)
        rows = dc_ref[...].T
        tri = (_iota((t, t), 1) <= _iota((t, t), 0)).astype(F32)
        rc = _mm_exact(rows, tri) + carry[:, 0:1]
        carry[...] = jnp.broadcast_to(rc[:, 0:1], (LANES, LANES))
        in_f = (lane >= N_HEADS) & (lane < 2 * N_HEADS)
        df = jnp.where(in_f, rc.T * _sigmoid(-(sm_ref[...] + b_ref[...])), 0.0)
        ds_ref[...] = (df + ddt_ref[...]).astype(BF16)
        dfb_ref[...] += jnp.sum(df, axis=0, keepdims=True)

    blk = pl.BlockSpec((t, LANES), lambda i: (rev(i), 0))
    return pl.pallas_call(
        body, name="forget_bwd",
        out_shape=(jax.ShapeDtypeStruct((s, LANES), BF16), jax.ShapeDtypeStruct((1, LANES), F32)),
        grid=(nb,),
        in_specs=[blk, blk, blk, _const_spec((1, LANES))],
        out_specs=(blk, _const_spec((1, LANES))),
        scratch_shapes=[pltpu.VMEM((LANES, LANES), F32)],
        compiler_params=_params(("arbitrary",)),
    )(dc, small, ddt_raw, fgb_row)


ATT_BLOCK = 1024
ATT_BLOCK_BWD = 512
ATT_BLOCK_BWD_Q = 512
ATT_SCALE = HEAD_DIM ** -0.5
AUG_A = HEAD_DIM
AUG_B = HEAD_DIM + 3


def _split3(c):
    hi = c.astype(BF16).astype(F32)
    r = c - hi
    mid = r.astype(BF16).astype(F32)
    return hi, mid, (r - mid).astype(BF16).astype(F32)


def _aug(lane, first, parts=None, value=1.0):
    if parts is None:
        return jnp.where((lane >= first) & (lane < first + 3), value, 0.0)
    return (jnp.where(lane == first, parts[0], 0.0) + jnp.where(lane == first + 1, parts[1], 0.0)
            + jnp.where(lane == first + 2, parts[2], 0.0))


def _pack_pair(a0, a1, lane):
    return jnp.where(lane < HEAD_DIM, a0, pltpu.roll(a1, HEAD_DIM, 1))


def proj_qkv_heads(u, w_q, w_k, w_v, cum, later):
    s = u.shape[0]
    tm = _blk(s, 256)
    nsteps = s // tm
    n_later = len(later)

    def body(u_ref, wq_ref, wk_ref, wv_ref, c_ref, *rest):
        later_in = rest[:n_later]
        qa_ref, ka_ref, va_ref, nrm_ref = rest[n_later:n_later + 4]
        later_out = rest[n_later + 4:2 * n_later + 4]
        sems = rest[2 * n_later + 4:]
        step = pl.program_id(0)

        @pl.when(step == 0)
        def _():
            for cp in gather_copies(later_in, later_out, *sems)[0]:
                cp.start()

        @pl.when(step == nsteps // 2)
        def _():
            for landed, onward in zip(*gather_copies(later_in, later_out, *sems)):
                landed.wait_recv()
                onward.start()

        @pl.when(step == nsteps - 1)
        def _():
            fetched, passed = gather_copies(later_in, later_out, *sems)
            for cp in passed:
                cp.wait_recv()
            for cp in fetched + passed:
                cp.wait_send()

        lane = _iota((tm, LANES), 1)
        lo = lane < HEAD_DIM
        uv = u_ref[...]
        qf = _mm(uv, wq_ref[...]) * ATT_SCALE
        kf = _mm(uv, wk_ref[...])
        vf = _mm(uv, wv_ref[...])
        cc = c_ref[...]
        ones_a = _aug(lane, AUG_A)
        ones_b = _aug(lane, AUG_B)
        sub8 = _iota((8, LANES), 0)
        nrm = jnp.zeros((8, LANES), F32)
        for h in range(N_HEADS):
            j, e = divmod(h, 2)

            def head(full):
                blk = full[:, LANES * j:LANES * (j + 1)]
                if e == 1:
                    blk = pltpu.roll(blk, HEAD_DIM, 1)
                return jnp.where(lo, blk, 0.0)

            parts = _split3(cc[:, N_HEADS + h:N_HEADS + h + 1])
            qh, kh = head(qf), head(kf)
            qa_ref[h] = (qh + _aug(lane, AUG_A, parts) + ones_b).astype(BF16)
            ka_ref[h] = (kh + ones_a - _aug(lane, AUG_B, parts)).astype(BF16)
            va_ref[h] = (head(vf) + ones_a).astype(BF16)
        seg = (_iota((ATT_WIDTH, LANES), 1) == (_iota((ATT_WIDTH, LANES), 0) >> 6)).astype(BF16)
        for r, val in enumerate((qf, kf)):
            sq = val * val
            hi = sq.astype(BF16)
            tot = _mm(hi, seg) + _mm((sq - hi.astype(F32)).astype(BF16), seg)
            nrm = nrm + jnp.where(sub8 == r, jnp.max(tot, axis=0, keepdims=True), 0.0)
        nrm_ref[0] = nrm

    shp = jax.ShapeDtypeStruct((N_HEADS, s, LANES), BF16)
    hspec = pl.BlockSpec((N_HEADS, tm, LANES), lambda i: (0, i, 0))
    wspec = _const_spec((D_MODEL, ATT_WIDTH))
    return pl.pallas_call(
        body, name="proj_qkv_heads",
        out_shape=tuple([shp, shp, shp, jax.ShapeDtypeStruct((nsteps, 8, LANES), F32)]
                        + [jax.ShapeDtypeStruct((N_CHIPS,) + a.shape, a.dtype) for a in later]),
        grid=(nsteps,),
        in_specs=[pl.BlockSpec((tm, D_MODEL), lambda i: (i, 0)), wspec, wspec, wspec,
                  pl.BlockSpec((tm, LANES), lambda i: (i, 0))] + [ANY] * n_later,
        out_specs=tuple([hspec, hspec, hspec, pl.BlockSpec((1, 8, LANES), lambda i: (i, 0, 0))]
                        + [ANY] * n_later),
        scratch_shapes=_sems(3 * n_later) + _sems(3 * n_later),
        compiler_params=_params(("arbitrary",)),
    )(u, w_q, w_k, w_v, cum, *later)


SKIP_BELOW = -110.0


def live_blocks(norms, cum, tq, tk):
    qn = jnp.sqrt(jnp.max(norms[:, 0, :N_HEADS], axis=0))
    kn = jnp.sqrt(jnp.max(norms[:, 1, :N_HEADS], axis=0))
    bound = 2.05 * qn * kn + 2.0
    c_first = cum[0::tq, N_HEADS:2 * N_HEADS]
    c_last = cum[tk - 1::tk, N_HEADS:2 * N_HEADS]
    nq, nk = c_first.shape[0], c_last.shape[0]
    top = bound[None, None, :] + c_first[:, None, :] - c_last[None, :, :]
    before = (jnp.arange(nk)[None, :] + 1) * tk <= jnp.arange(nq)[:, None] * tq
    dead = before[:, :, None] & ~(top >= SKIP_BELOW)
    first = jnp.sum(dead, axis=1).astype(jnp.int32).T
    last_q = jnp.sum(first[:, None, :] <= jnp.arange(nk)[None, :, None], axis=2).astype(jnp.int32) - 1
    return first, last_q


def attention_fwd(first, qa, ka, va):
    s = qa.shape[1]
    t = _blk(s, ATT_BLOCK)
    nq = s // t

    def body(first_ref, qa_ref, ka_ref, va_ref, o_ref, qb_ref, m_scr, acc_scr, alpha_scr, p_scr, s_scr):
        qi = pl.program_id(1)
        starts = [first_ref[2 * pl.program_id(0) + e, qi] for e in range(2)]
        k0 = jnp.maximum(starts[0], starts[1])
        m_scr[...] = jnp.full_like(m_scr, NEG_BIG)
        acc_scr[...] = jnp.zeros_like(acc_scr)

        def kv_rows(kb):
            return pl.ds(pl.multiple_of(kb * t, t), t)

        def logits(kb, masked, heads=(0, 1)):
            for e in heads:
                sc = _mm_nt(qa_ref[e], ka_ref[e, kv_rows(kb), :])
                if masked:
                    sc = jnp.where(_iota((t, t), 0) >= _iota((t, t), 1), sc, NEG_BIG)
                s_scr[e] = sc

        def probs(heads=(0, 1)):
            for e in heads:
                cmax = s_scr[e, :, 0:LANES]
                for c in range(1, t // LANES):
                    cmax = jnp.maximum(cmax, s_scr[e, :, LANES * c:LANES * (c + 1)])
                m_old = m_scr[e]
                m_new = jnp.maximum(m_old, jnp.max(cmax, axis=1, keepdims=True))
                alpha_scr[e] = jnp.exp(m_old - m_new)
                m_scr[e] = m_new
                for c in range(t // LANES):
                    cols = slice(LANES * c, LANES * (c + 1))
                    p_scr[e, :, cols] = jnp.exp(s_scr[e, :, cols] - m_new).astype(BF16)

        def accumulate(kb, heads=(0, 1)):
            for e in heads:
                acc_scr[e] = alpha_scr[e] * acc_scr[e] + _mm(p_scr[e], va_ref[e, kv_rows(kb), :])

        for e in range(2):
            def alone(kb, carry, e=e):
                logits(kb, False, (e,))
                probs((e,))
                accumulate(kb, (e,))
                return carry

            lax.fori_loop(starts[e], k0, alone, 0)

        def loop_body(kb, carry):
            logits(kb, False)
            for e in range(2):
                accumulate(kb - 1, (e,))
                probs((e,))
            return carry

        @pl.when(qi > k0)
        def _():
            logits(k0, False)
            probs()

        lax.fori_loop(k0 + 1, qi, loop_body, 0)

        @pl.when(qi > k0)
        def _():
            logits(qi, True)
            accumulate(qi - 1)
            probs()

        @pl.when(qi == k0)
        def _():
            logits(qi, True)
            probs()

        accumulate(qi)

        lane = _iota((t, LANES), 1)
        outs = []
        for e in range(2):
            acc = acc_scr[e]
            l = acc[:, AUG_A:AUG_A + 1]
            outs.append(acc / l)
            lse = m_scr[e][:, 0:1] + jnp.log(l)
            q32 = qa_ref[e].astype(F32)
            c = q32[:, AUG_A:AUG_A + 1] + q32[:, AUG_A + 1:AUG_A + 2] + q32[:, AUG_A + 2:AUG_A + 3]
            qb = jnp.where(lane < HEAD_DIM, q32, 0.0) + _aug(lane, AUG_A, _split3(c - lse)) + _aug(lane, AUG_B)
            qb_ref[e] = qb.astype(BF16)
        o_ref[...] = _pack_pair(outs[0], outs[1], lane)

    grid_spec = pltpu.PrefetchScalarGridSpec(
        num_scalar_prefetch=1, grid=(N_PAIRS, nq),
        in_specs=[pl.BlockSpec((2, t, LANES), lambda j, qi, f: (j, qi, 0)),
                  pl.BlockSpec((2, s, LANES), lambda j, qi, f: (j, 0, 0)),
                  pl.BlockSpec((2, s, LANES), lambda j, qi, f: (j, 0, 0))],
        out_specs=[pl.BlockSpec((t, LANES), lambda j, qi, f: (qi, j)),
                   pl.BlockSpec((2, t, LANES), lambda j, qi, f: (j, qi, 0))],
        scratch_shapes=[pltpu.VMEM((2, t, LANES), F32), pltpu.VMEM((2, t, LANES), F32),
                        pltpu.VMEM((2, t, LANES), F32), pltpu.VMEM((2, t, t), BF16), pltpu.VMEM((2, t, t), F32)])
    return pl.pallas_call(
        body, name="attention_fwd", grid_spec=grid_spec,
        out_shape=(jax.ShapeDtypeStruct((s, ATT_WIDTH), F32), jax.ShapeDtypeStruct((N_HEADS, s, LANES), BF16)),
        compiler_params=_params(("parallel", "parallel")),
    )(first, qa, ka, va)


def attention_bwd(last_q, qb, ka, va, dob):
    s = qb.shape[1]
    t = _blk(s, ATT_BLOCK_BWD)
    tq = _blk(s, ATT_BLOCK_BWD_Q)
    nq = s // tq
    per_q = tq // t

    def body(last_ref, qb_ref, dob_ref, ka_ref, va_ref, dq_ref, dk_ref, dv_ref, dc_ref, dq_scr, dk_scr, dv_scr):
        j, ki = pl.program_id(0), pl.program_id(1)

        @pl.when((j == 0) & (ki == 0))
        def _():
            dc_ref[...] = jnp.zeros_like(dc_ref)

        @pl.when(ki == 0)
        def _():
            dq_scr[...] = jnp.zeros_like(dq_scr)

        dk_scr[...] = jnp.zeros_like(dk_scr)
        dv_scr[...] = jnp.zeros_like(dv_scr)

        def q_step(qblk, masked, heads=(0, 1)):
            rows = pl.ds(pl.multiple_of(qblk * tq, tq), tq)
            scs = [_mm_nt(qb_ref[e, rows, :], ka_ref[e]) for e in heads]
            dps = [_mm_nt(dob_ref[e, rows, :], va_ref[e]) for e in heads]
            for e, sc, dp in zip(heads, scs, dps):
                q = qb_ref[e, rows, :]
                do = dob_ref[e, rows, :]
                if masked:
                    keep = (_iota((tq, t), 0) - _iota((tq, t), 1)) >= ki * t - qblk * tq
                    sc = jnp.where(keep, sc, NEG_BIG)
                p = jnp.exp(sc)
                ds_b = (p * dp).astype(BF16)
                dv_scr[e] += _mm_tn(p.astype(BF16), do)
                dk_scr[e] += _mm_tn(ds_b, q)
                dq_scr[e, rows, :] += _mm(ds_b, ka_ref[e])

        def loop_body(qblk, carry):
            q_step(qblk, False)
            return carry

        ends = [last_ref[2 * j + e, ki] + 1 for e in range(2)]
        both = jnp.minimum(ends[0], ends[1])
        diag = ki // per_q
        q_step(diag, True)
        lax.fori_loop(diag + 1, both, loop_body, 0)
        for e in range(2):
            def alone(qblk, carry, e=e):
                q_step(qblk, False, (e,))
                return carry

            lax.fori_loop(both, ends[e], alone, 0)

        lane = _iota((t, LANES), 1)
        dk_ref[...] = _pack_pair(dk_scr[0], dk_scr[1], lane).astype(BF16)
        dv_ref[...] = _pack_pair(dv_scr[0], dv_scr[1], lane).astype(BF16)
        rows = pl.ds(pl.multiple_of(ki * t, t), t)
        dc_ref[rows, :] -= (jnp.where(lane == N_HEADS + 2 * j, dk_scr[0][:, AUG_B:AUG_B + 1], 0.0)
                            + jnp.where(lane == N_HEADS + 2 * j + 1, dk_scr[1][:, AUG_B:AUG_B + 1], 0.0))

        @pl.when(ki == s // t - 1)
        def _():
            for blk in range(s // t):
                rws = pl.ds(blk * t, t)
                d0 = dq_scr[0, rws, :]
                d1 = dq_scr[1, rws, :]
                dq_ref[rws, :] = (_pack_pair(d0, d1, lane) * ATT_SCALE).astype(BF16)
                dc_ref[rws, :] += (jnp.where(lane == N_HEADS + 2 * j, d0[:, AUG_A:AUG_A + 1], 0.0)
                                   + jnp.where(lane == N_HEADS + 2 * j + 1, d1[:, AUG_A:AUG_A + 1], 0.0))

    full = pl.BlockSpec((2, s, LANES), lambda j, ki, f: (j, 0, 0))
    blk = pl.BlockSpec((2, t, LANES), lambda j, ki, f: (j, ki, 0))
    pair = pl.BlockSpec((t, LANES), lambda j, ki, f: (ki, j))
    wide = jax.ShapeDtypeStruct((s, ATT_WIDTH), BF16)
    grid_spec = pltpu.PrefetchScalarGridSpec(
        num_scalar_prefetch=1, grid=(N_PAIRS, s // t),
        in_specs=[full, full, blk, blk],
        out_specs=[pl.BlockSpec((s, LANES), lambda j, ki, f: (0, j)), pair, pair,
                   pl.BlockSpec((s, LANES), lambda j, ki, f: (0, 0))],
        scratch_shapes=[pltpu.VMEM((2, s, LANES), F32), pltpu.VMEM((2, t, LANES), F32),
                        pltpu.VMEM((2, t, LANES), F32)])
    return pl.pallas_call(
        body, name="attention_bwd", grid_spec=grid_spec,
        out_shape=(wide, wide, wide, jax.ShapeDtypeStruct((s, LANES), F32)),
        compiler_params=_params(("arbitrary", "arbitrary")),
    )(last_q, qb, dob, ka, va)


def _dsilu(z, sg):
    return sg * (1.0 + z * (1.0 - sg))


def post_mix(x, y, zs, o, za, p, tgt, ssd_g, att_g_lane, ple_g, fin_g, w_out, w_gate, w_proj):
    s = x.shape[0]
    tm = _blk(s, 256)
    half = SSD_WIDTH // N_GROUPS

    def rms_bwd(dy, yn, r):
        return r * (dy - yn * jnp.mean(dy * yn, axis=-1, keepdims=True))

    def colsum(a):
        return jnp.sum(a, axis=0, keepdims=True)

    def body(x_ref, y_ref, zs_ref, o_ref, za_ref, p_ref, t_ref, sg_ref, ag_ref, pg_ref, fg_ref,
             wo_ref, wg_ref, wp_ref,
             dh1_ref, dy_ref, dzs_ref, dob_ref, dza_ref, ycat_ref, dh1b_ref, n2b_ref, dglb_ref, dppb_ref, pb_ref,
             loss_ref, dfin_ref, dple_ref, dssd_ref, datt_ref):
        @pl.when(pl.program_id(0) == 0)
        def _():
            for r in (loss_ref, dfin_ref, dple_ref, dssd_ref, datt_ref):
                r[...] = jnp.zeros_like(r)

        lane = _iota((tm, LANES), 1)
        lo = lane < HEAD_DIM
        zs = zs_ref[...]
        sz = _sigmoid(zs)
        yv = y_ref[...]
        ys = yv * (zs * sz)
        yn, rg = [], []
        for g in range(N_GROUPS):
            seg = ys[:, half * g:half * (g + 1)]
            r = lax.rsqrt(jnp.mean(seg * seg, axis=-1, keepdims=True) + EPS)
            yn.append(seg * r)
            rg.append(r)
            ycat_ref[:, half * g:half * (g + 1)] = (yn[g] * sg_ref[:, half * g:half * (g + 1)]).astype(BF16)
        za = za_ref[...]
        sza = _sigmoid(za)
        silu_za = za * sza
        on, ra = [], []
        for jb in range(N_PAIRS):
            blk = o_ref[:, LANES * jb:LANES * (jb + 1)]
            sq = blk * blk
            ms0 = jnp.sum(jnp.where(lo, sq, 0.0), axis=1, keepdims=True) * (1.0 / HEAD_DIM)
            ms1 = jnp.sum(jnp.where(lo, 0.0, sq), axis=1, keepdims=True) * (1.0 / HEAD_DIM)
            r = jnp.where(lo, lax.rsqrt(ms0 + EPS), lax.rsqrt(ms1 + EPS))
            on.append(blk * r)
            ra.append(r)
            an = on[jb] * ag_ref[:, LANES * jb:LANES * (jb + 1)]
            ycat_ref[:, SSD_WIDTH + LANES * jb:SSD_WIDTH + LANES * (jb + 1)] = (
                an * silu_za[:, LANES * jb:LANES * (jb + 1)]).astype(BF16)
        h1 = x_ref[...] + _mm(ycat_ref[...], wo_ref[...])
        r2 = lax.rsqrt(jnp.mean(h1 * h1, axis=-1, keepdims=True) + EPS)
        n2h = h1 * r2
        n2_b = (n2h * pg_ref[...]).astype(BF16)
        gate = _sigmoid(_mm(n2_b, wg_ref[...]))
        p_b = p_ref[...].astype(BF16)
        pp = _mm(p_b, wp_ref[...])
        h2 = h1 + gate * pp
        r3 = lax.rsqrt(jnp.mean(h2 * h2, axis=-1, keepdims=True) + EPS)
        n3 = h2 * r3
        diff = n3 * fg_ref[...] - t_ref[...]
        sq = colsum(diff * diff)
        part = sq[:, 0:LANES]
        for jb in range(1, D_MODEL // LANES):
            part = part + sq[:, LANES * jb:LANES * (jb + 1)]
        loss_ref[...] += part * (0.5 / D_MODEL)
        dout = diff * (1.0 / D_MODEL)
        dfin_ref[...] += colsum(dout * n3)
        dh2 = rms_bwd(dout * fg_ref[...], n3, r3)
        dgl = dh2 * pp * gate * (1.0 - gate)
        dgl_b = dgl.astype(BF16)
        dn2 = _mm_nt(dgl_b, wg_ref[...])
        dple_ref[...] += colsum(dn2 * n2h)
        dh1 = dh2 + rms_bwd(dn2 * pg_ref[...], n2h, r2)
        dh1_b = dh1.astype(BF16)
        dycat = _mm_nt(dh1_b, wo_ref[...])
        dh1_ref[...] = dh1
        dh1b_ref[...] = dh1_b
        n2b_ref[...] = n2_b
        dglb_ref[...] = dgl_b
        dppb_ref[...] = (dh2 * gate).astype(BF16)
        pb_ref[...] = p_b
        for g in range(N_GROUPS):
            cols = slice(half * g, half * (g + 1))
            dys_g = dycat[:, cols]
            dssd_ref[:, cols] += colsum(dys_g * yn[g])
            dys = rms_bwd(dys_g * sg_ref[:, cols], yn[g], rg[g])
            dy_ref[:, cols] = dys * (zs[:, cols] * sz[:, cols])
            dzs_ref[:, cols] = (dys * yv[:, cols] * _dsilu(zs[:, cols], sz[:, cols])).astype(BF16)
        for jb in range(N_PAIRS):
            cols = slice(LANES * jb, LANES * (jb + 1))
            dya = dycat[:, SSD_WIDTH + LANES * jb:SSD_WIDTH + LANES * (jb + 1)]
            ag = ag_ref[:, cols]
            dan = dya * silu_za[:, cols]
            dza_ref[:, cols] = (dya * (on[jb] * ag) * _dsilu(za[:, cols], sza[:, cols])).astype(BF16)
            datt_ref[:, cols] += colsum(dan * on[jb])
            don = dan * ag
            q = don * on[jb]
            m0 = jnp.sum(jnp.where(lo, q, 0.0), axis=1, keepdims=True) * (1.0 / HEAD_DIM)
            m1 = jnp.sum(jnp.where(lo, 0.0, q), axis=1, keepdims=True) * (1.0 / HEAD_DIM)
            do2 = ra[jb] * (don - on[jb] * jnp.where(lo, m0, m1))
            prod = do2 * o_ref[:, cols]
            for e in range(2):
                delta = jnp.sum(jnp.where(lo, prod, 0.0) if e == 0 else jnp.where(lo, 0.0, prod),
                                axis=1, keepdims=True)
                base = jnp.where(lo, do2 if e == 0 else pltpu.roll(do2, HEAD_DIM, 1), 0.0)
                dob_ref[2 * jb + e] = (base - _aug(lane, AUG_A, _split3(delta))).astype(BF16)

    def rows(n, dtype=None):
        return pl.BlockSpec((tm, n), lambda i: (i, 0))

    def out(n, dtype):
        return jax.ShapeDtypeStruct((s, n), dtype)

    vec = _const_spec((1, D_MODEL))
    vshape = jax.ShapeDtypeStruct((1, D_MODEL), F32)
    return pl.pallas_call(
        body, name="post_mix",
        out_shape=(out(D_MODEL, F32), out(SSD_WIDTH, F32), out(SSD_WIDTH, BF16),
                   jax.ShapeDtypeStruct((N_HEADS, s, LANES), BF16),
                   out(ATT_WIDTH, BF16), out(D_INNER, BF16), out(D_MODEL, BF16), out(D_MODEL, BF16),
                   out(D_MODEL, BF16), out(D_MODEL, BF16), out(PLE_DIM, BF16),
                   jax.ShapeDtypeStruct((1, LANES), F32), vshape, vshape, vshape, vshape),
        grid=(s // tm,),
        in_specs=[rows(D_MODEL), rows(SSD_WIDTH), rows(SSD_WIDTH), rows(ATT_WIDTH), rows(ATT_WIDTH),
                  rows(PLE_DIM), rows(D_MODEL), vec, vec, vec, vec,
                  _const_spec((D_INNER, D_MODEL)), _const_spec((D_MODEL, D_MODEL)), _const_spec((PLE_DIM, D_MODEL))],
        out_specs=(rows(D_MODEL), rows(SSD_WIDTH), rows(SSD_WIDTH),
                   pl.BlockSpec((N_HEADS, tm, LANES), lambda i: (0, i, 0)), rows(ATT_WIDTH),
                   rows(D_INNER), rows(D_MODEL), rows(D_MODEL), rows(D_MODEL), rows(D_MODEL), rows(PLE_DIM),
                   _const_spec((1, LANES)), vec, vec, vec, vec),
        compiler_params=_params(("arbitrary",)),
    )(x, y, zs, o, za, p, tgt, ssd_g, att_g_lane, ple_g, fin_g, w_out, w_gate, w_proj)


def in_proj_bwd(dsegs, wsegs, x, g, dh1, pres):
    s = x.shape[0]
    tm = _blk(s, 512)
    nseg = len(dsegs)
    nbig = len(pres)
    nsteps = s // tm

    def body(*refs):
        d_refs = refs[:nseg]
        w_refs = refs[nseg:2 * nseg]
        x_ref, g_ref, dh1_ref = refs[2 * nseg:2 * nseg + 3]
        rest = refs[2 * nseg + 3:]
        pre_refs, (dx_ref, dg_ref), part_refs = rest[:nbig], rest[nbig:nbig + 2], rest[nbig + 2:2 * nbig + 2]
        ssem, rsem, lsem = rest[2 * nbig + 2:]

        @pl.when(pl.program_id(0) == 0)
        def _():
            dg_ref[...] = jnp.zeros_like(dg_ref)
            for cp in scatter_copies(pre_refs, part_refs, ssem, rsem, lsem):
                cp.start()

        @pl.when(pl.program_id(0) == nsteps - 1)
        def _():
            for cp in scatter_copies(pre_refs, part_refs, ssem, rsem, lsem):
                cp.wait()

        du = _mm_nt(d_refs[0][...], w_refs[0][...])
        for k in range(1, nseg):
            du = du + _mm_nt(d_refs[k][...], w_refs[k][...])
        xv = x_ref[...]
        r = lax.rsqrt(jnp.mean(xv * xv, axis=-1, keepdims=True) + EPS)
        xh = xv * r
        dg_ref[...] += jnp.sum(du * xh, axis=0, keepdims=True)
        dxh = du * g_ref[...]
        dx_ref[...] = r * (dxh - xh * jnp.mean(dxh * xh, axis=-1, keepdims=True)) + dh1_ref[...]

    rows = lambda n: pl.BlockSpec((tm, n), lambda i: (i, 0))
    return pl.pallas_call(
        body, name="in_proj_bwd",
        out_shape=tuple([jax.ShapeDtypeStruct((s, D_MODEL), F32), jax.ShapeDtypeStruct((1, D_MODEL), F32)]
                        + [jax.ShapeDtypeStruct(a.shape, a.dtype) for a in pres]),
        grid=(nsteps,),
        in_specs=([rows(d.shape[1]) for d in dsegs] + [_const_spec(w.shape) for w in wsegs]
                  + [rows(D_MODEL), _const_spec((1, D_MODEL)), rows(D_MODEL)] + [ANY] * nbig),
        out_specs=tuple([rows(D_MODEL), _const_spec((1, D_MODEL))] + [ANY] * nbig),
        scratch_shapes=_sems(3 * nbig) + [pltpu.SemaphoreType.DMA((nbig,))],
        compiler_params=_params(("arbitrary",)),
    )(*dsegs, *wsegs, x, g, dh1, *pres)


SMALL_NAMES = ("norm_g", "conv_b", "dt_bias", "a_log", "d_skip", "ssd_norm_g", "fg_bias", "att_norm_g",
               "ple_norm_g", "final_norm_g")
SMALL_SIZES = (1024, 1536, 16, 16, 16, 1024, 16, 64, 1024, 1024)
CONV_W_SIZE = CONV_WIDTH * CONV_CH


def _pack_small(vals):
    flat = jnp.concatenate([v.reshape(-1).astype(F32) for v in vals])
    flat = jnp.pad(flat, (0, SMALL_ROWS * LANES - flat.shape[0]))
    return flat.reshape(SMALL_ROWS, LANES)


def _unpack_small(pack, shapes):
    flat = pack.reshape(-1)
    out, off = [], 0
    for n, shp in zip(SMALL_SIZES, shapes):
        out.append(flat[off:off + n].reshape(shp))
        off += n
    return out


def _row128(v16, offset=0):
    return jnp.pad(v16.reshape(1, N_HEADS).astype(F32), ((0, 0), (offset, LANES - N_HEADS - offset)))


def local_step(prereduce, later, join_later, x, p, tgt, w_in, conv_w, norm_g, conv_b, dt_bias, a_log, d_skip,
               ssd_norm_g, fg_bias, att_norm_g, ple_norm_g, final_norm_g):
    widths = (SSD_WIDTH, CONV_CH, N_HEADS, ATT_WIDTH, ATT_WIDTH, ATT_WIDTH, ATT_WIDTH)
    c0, c1, c2, c3, c4, c5, c6, c7 = [sum(widths[:i]) for i in range(len(widths) + 1)]
    w_zs, w_xbc, w_dt = w_in[:, c0:c1], w_in[:, c1:c2], w_in[:, c2:c3]
    w_za, w_q, w_k, w_v, w_f = w_in[:, c3:c4], w_in[:, c4:c5], w_in[:, c5:c6], w_in[:, c6:c7], w_in[:, c7:]
    w_small = jnp.concatenate([w_dt, w_f, jnp.zeros((D_MODEL, LANES - 2 * N_HEADS), BF16)], axis=1)

    dtb_row = _row128(dt_bias)
    a_row = _row128(-jnp.exp(a_log.astype(F32)))
    fgb_row = _row128(fg_bias, N_HEADS)
    dskip_lane = jnp.repeat(d_skip.astype(F32), HEAD_DIM).reshape(1, SSD_WIDTH)
    att_g_lane = jnp.tile(att_norm_g.astype(F32), N_HEADS).reshape(1, ATT_WIDTH)
    row = lambda v: v.reshape(1, -1).astype(F32)

    u, zs, xbc, za, small = in_proj_fwd(x, row(norm_g), [w_zs, w_xbc, w_za, w_small])
    cum = forget_cumsum(small, fgb_row)
    qa, ka, va, norms, *gathered = proj_qkv_heads(u, w_q, w_k, w_v, cum, later)
    w_out, w_gate, w_proj = join_later(gathered)
    n_seq = x.shape[0]
    first, _ = live_blocks(norms, cum, _blk(n_seq, ATT_BLOCK), _blk(n_seq, ATT_BLOCK))
    _, last_q = live_blocks(norms, cum, _blk(n_seq, ATT_BLOCK_BWD_Q), _blk(n_seq, ATT_BLOCK_BWD))
    pre, xc = conv_fwd(xbc, conv_w, row(conv_b))
    y, states = ssd_fwd(xc, small, dtb_row, a_row, dskip_lane)
    o, qb = attention_fwd(first, qa, ka, va)
    (dh1, dy, dzs, dob, dza, ycat, dh1_b, n2_b, dgl_b, dpp_b, p_b,
     loss_l, dfin, dple, dssd_g, datt_lane) = post_mix(
        x, y, zs, o, za, p, tgt, row(ssd_norm_g), att_g_lane, row(ple_norm_g), row(final_norm_g),
        w_out, w_gate, w_proj)
    dq, dk, dv, dc = attention_bwd(last_q, qb, ka, va, dob)
    dxc, ddt_raw, da, ddtb, ddsk_lane = ssd_bwd(xc, small, states, dy, dtb_row, a_row, dskip_lane)
    dsmall, dfgb = forget_bwd(dc, small, ddt_raw, fgb_row)
    dxbc, dconv_w8, dconv_b = conv_bwd(xbc, pre, dxc, conv_w)
    dsegs = [dzs, dxbc, dza, dq, dk, dv, dsmall]
    wsegs = [w_zs, w_xbc, w_za, w_q, w_k, w_v, w_small]
    dws = [matmul_tn(u, d, "dw_in_%d" % i) for i, d in enumerate(dsegs)]
    dw_in = jnp.concatenate([dws[0], dws[1], dws[6][:, :N_HEADS], dws[2], dws[3], dws[4], dws[5],
                             dws[6][:, N_HEADS:2 * N_HEADS]], axis=1)
    dw_out = matmul_tn(ycat, dh1_b, "dw_out")
    dw_gate = matmul_tn(n2_b, dgl_b, "dw_gate")
    dw_proj = matmul_tn(p_b, dpp_b, "dw_proj")
    dx, dnorm_g, *parts = in_proj_bwd(dsegs, wsegs, x, row(norm_g), dh1, prereduce(dw_in, dw_out, dw_gate, dw_proj))
    small_grads = [
        dnorm_g, dconv_b, ddtb[0, :N_HEADS], (da * a_row)[0, :N_HEADS],
        ddsk_lane.reshape(N_HEADS, HEAD_DIM).sum(axis=1), dssd_g, dfgb[0, N_HEADS:2 * N_HEADS],
        datt_lane.reshape(N_HEADS, HEAD_DIM).sum(axis=0), dple, dfin]
    loss = jnp.sum(loss_l)
    return loss, dx, parts, dconv_w8[:CONV_WIDTH], small_grads


def kernel(x, p, norm_g, w_in, conv_w, conv_b, dt_bias, a_log, d_skip, ssd_norm_g, fg_bias, att_norm_g, w_out, ple_norm_g, w_ple_gate, w_ple_proj, final_norm_g, loss_target, m_norm_g, m_w_in, m_conv_w, m_conv_b, m_dt_bias, m_a_log, m_d_skip, m_ssd_norm_g, m_fg_bias, m_att_norm_g, m_w_out, m_ple_norm_g, m_w_ple_gate, m_w_ple_proj, m_final_norm_g, v_norm_g, v_w_in, v_conv_w, v_conv_b, v_dt_bias, v_a_log, v_d_skip, v_ssd_norm_g, v_fg_bias, v_att_norm_g, v_w_out, v_ple_norm_g, v_w_ple_gate, v_w_ple_proj, v_final_norm_g):
    chip = 2 * lax.axis_index("x") + lax.axis_index("y")
    core = lax.axis_index("c")

    big_w = [w_in[0], w_out[0], w_ple_gate[0], w_ple_proj[0]]
    own = [a.astype(BF16) for a in big_w] + [conv_w[0]]

    def joined(mine, gathered, axis):
        return jnp.concatenate([jnp.where(chip == j, mine, gathered[j]) for j in range(N_CHIPS)], axis=axis)

    w_in_all, conv_all = gather_weights(own[:1], own[4])
    w_in_f, conv_w_f = joined(own[0], w_in_all, 1), joined(own[4], conv_all, 1)

    def join_later(gathered):
        return [joined(mine, got, axis) for mine, got, axis in zip(own[1:4], gathered, (0, 0, 1))]

    core1 = core.reshape(1).astype(jnp.int32)

    def prereduce(dw_in, dw_out, dw_gate, dw_proj):
        n_in, n_proj = w_in.shape[2], w_ple_proj.shape[2]
        gs = [jnp.stack([dw_in[:, n_in * j:n_in * (j + 1)] for j in range(N_CHIPS)]),
              dw_out.reshape(N_CHIPS, w_out.shape[1], D_MODEL), dw_gate.reshape(N_CHIPS, w_ple_gate.shape[1], D_MODEL),
              jnp.stack([dw_proj[:, n_proj * j:n_proj * (j + 1)] for j in range(N_CHIPS)])]
        return add_halves(core1, gs, halves_to_sibling(gs))

    smalls_w = [norm_g, conv_b, dt_bias, a_log, d_skip, ssd_norm_g, fg_bias, att_norm_g, ple_norm_g, final_norm_g]
    loss_l, dx, parts, dconv_w, small_grads = local_step(
        prereduce, own[1:4], join_later, x[0], p[0, 0], loss_target[0], w_in_f, conv_w_f,
        *[a.reshape(-1) for a in smalls_w])
    loss = lax.psum(loss_l, ("x", "y", "c"))
    mine = sum_parts(parts)
    *theirs, smalls = swap_halves(mine, _pack_small(list(small_grads) + [dconv_w]))

    g_big, d_big, m_big, v_big = adamw_big(
        core1, mine, theirs, big_w, [m_w_in[0], m_w_out[0], m_w_ple_gate[0], m_w_ple_proj[0]],
        [v_w_in[0], v_w_out[0], v_w_ple_gate[0], v_w_ple_proj[0]])
    smalls_m = [m_norm_g, m_conv_b, m_dt_bias, m_a_log, m_d_skip, m_ssd_norm_g, m_fg_bias, m_att_norm_g,
                m_ple_norm_g, m_final_norm_g]
    smalls_v = [v_norm_g, v_conv_b, v_dt_bias, v_a_log, v_d_skip, v_ssd_norm_g, v_fg_bias, v_att_norm_g,
                v_ple_norm_g, v_final_norm_g]
    g_sm, d_sm, m_sm, v_sm = adamw_small(smalls, _pack_small(smalls_w), _pack_small(smalls_m), _pack_small(smalls_v))
    n_small = sum(SMALL_SIZES)
    g_conv_full = g_sm.reshape(-1)[n_small:n_small + CONV_W_SIZE].reshape(CONV_WIDTH, CONV_CH)
    n_conv = conv_w.shape[2]
    g_conv = lax.dynamic_slice_in_dim(g_conv_full, chip * n_conv, n_conv, axis=1)
    d_conv, m_conv, v_conv = adamw_whole(g_conv, conv_w[0], m_conv_w[0], v_conv_w[0], "adamw_conv")

    shapes = [a.shape for a in smalls_w]
    outs = []
    for big, conv, sm in ((g_big, g_conv, g_sm), (d_big, d_conv, d_sm), (m_big, m_conv, m_sm), (v_big, v_conv, v_sm)):
        b_in, b_out, b_gate, b_proj = [a[None] for a in big]
        s_norm, s_convb, s_dtb, s_alog, s_dsk, s_ssdg, s_fgb, s_attg, s_pleg, s_fin = _unpack_small(sm, shapes)
        outs.extend([s_norm, b_in, conv[None], s_convb, s_dtb, s_alog, s_dsk, s_ssdg, s_fgb, s_attg, b_out, s_pleg,
                     b_gate, b_proj, s_fin])
    return (loss, dx[None], *outs)
```
